```python
import math
import jax, jax.numpy as jnp
from jax import lax
import numpy as np

D_MODEL = 1024
BATCH = 8
SEQ = 8192
DEPTH = 1

PLE_DIM = 256
N_HEADS = 4
NOPE_DIM = 128
ROPE_DIM = 64
V_DIM = 128
QK_DIM = NOPE_DIM + ROPE_DIM
Q_LORA = 256
KV_LORA = 128
ATTN_WIDTH = N_HEADS * V_DIM
CONV_WIDTH = D_MODEL - ATTN_WIDTH
CONV_K = 3
ROPE_THETA = 10000.0
RMS_EPS = 1e-6
Q_BLOCK = 128
NEG_INF = -1e30
IN_WIDTHS = (Q_LORA, KV_LORA, ROPE_DIM, ATTN_WIDTH,
             CONV_WIDTH, CONV_WIDTH, CONV_WIDTH, CONV_WIDTH)
IN_TOTAL = Q_LORA + KV_LORA + ROPE_DIM + ATTN_WIDTH + 4 * CONV_WIDTH

kernel_name = "hymba_mla_shortconv_ple_block"


def rms_norm(x, g):
    xf = x.astype(jnp.float32)
    y = xf * lax.rsqrt(jnp.mean(xf * xf, axis=-1, keepdims=True) + RMS_EPS)
    return (y * g.astype(jnp.float32)).astype(x.dtype)


def rope_cos_sin(positions):
    inv_freq = 1.0 / (ROPE_THETA ** (jnp.arange(0, ROPE_DIM, 2, dtype=jnp.float32) / ROPE_DIM))
    ang = positions.astype(jnp.float32)[..., None] * inv_freq
    return jnp.cos(ang)[:, :, None, :], jnp.sin(ang)[:, :, None, :]


def apply_rope(t, cos, sin):
    tf = t.astype(jnp.float32)
    t1, t2 = tf[..., : ROPE_DIM // 2], tf[..., ROPE_DIM // 2:]
    out = jnp.concatenate([t1 * cos - t2 * sin, t2 * cos + t1 * sin], axis=-1)
    return out.astype(t.dtype)


def causal_block_attention(q, k, v):
    B, S, H, D = q.shape
    nb = S // Q_BLOCK
    scale = 1.0 / math.sqrt(D)
    q_blocks = q.reshape(B, nb, Q_BLOCK, H, D).transpose(1, 0, 2, 3, 4)
    k_pos = jnp.arange(S)

    def one_block(args):
        qb, bi = args
        s = jnp.einsum('bqhd,bkhd->bhqk', qb, k, preferred_element_type=jnp.float32) * scale
        q_pos = bi * Q_BLOCK + jnp.arange(Q_BLOCK)
        mask = k_pos[None, :] <= q_pos[:, None]
        s = jnp.where(mask[None, None], s, NEG_INF)
        pr = jax.nn.softmax(s, axis=-1).astype(v.dtype)
        return jnp.einsum('bhqk,bkhd->bqhd', pr, v)

    out = lax.map(one_block, (q_blocks, jnp.arange(nb)))
    return out.transpose(1, 0, 2, 3, 4).reshape(B, S, H, v.shape[-1])


def causal_depthwise_conv(u, w):
    C = u.shape[-1]
    return lax.conv_general_dilated(
        u, w[:, None, :].astype(u.dtype), window_strides=(1,),
        padding=[(CONV_K - 1, 0)], dimension_numbers=('NWC', 'WIO', 'NWC'),
        feature_group_count=C)


def hybrid_layer(x, p_i, cos, sin, g_in, w_in, g_cq, w_uq, g_ckv, w_ukv, g_q, g_k,
                 conv_w, g_oa, g_oc, w_o, w_pl, w_plg, g_pl):
    B, S, _ = x.shape
    h = rms_norm(x, g_in)
    proj = h @ w_in
    splits, acc = [], 0
    for wdt in IN_WIDTHS[:-1]:
        acc += wdt
        splits.append(acc)
    c_q, c_kv, k_pe, z_a, cb, cc, cx, z_c = jnp.split(proj, splits, axis=-1)

    q = (rms_norm(c_q, g_cq) @ w_uq).reshape(B, S, N_HEADS, QK_DIM)
    kv = (rms_norm(c_kv, g_ckv) @ w_ukv).reshape(B, S, N_HEADS, NOPE_DIM + V_DIM)
    k_nope, v = kv[..., :NOPE_DIM], kv[..., NOPE_DIM:]
    k = jnp.concatenate(
        [k_nope, jnp.broadcast_to(k_pe[:, :, None, :], (B, S, N_HEADS, ROPE_DIM))], axis=-1)
    q = rms_norm(q, g_q)
    k = rms_norm(k, g_k)
    q = jnp.concatenate([q[..., :NOPE_DIM], apply_rope(q[..., NOPE_DIM:], cos, sin)], axis=-1)
    k = jnp.concatenate([k[..., :NOPE_DIM], apply_rope(k[..., NOPE_DIM:], cos, sin)], axis=-1)
    o_attn = causal_block_attention(q, k, v).reshape(B, S, ATTN_WIDTH)
    y_attn = rms_norm(o_attn * jax.nn.silu(z_a), g_oa)

    u = causal_depthwise_conv(cc * cx, conv_w)
    y_conv = rms_norm(cb * u * jax.nn.silu(z_c), g_oc)

    x = x + jnp.concatenate([y_attn, y_conv], axis=-1) @ w_o

    gate = jax.nn.sigmoid(rms_norm(x, g_pl) @ w_plg)
    return x + gate * (p_i @ w_pl)


def _fwd_setup_inputs(seed: int = 0) -> dict:
    key = jax.random.key(seed)
    ks = jax.random.split(key, 20)
    f32 = jnp.float32

    def nrm(k, shape, fan_in):
        return jax.random.normal(k, shape, f32) * (fan_in ** -0.5)

    def gain(k, n):
        return 1.0 + 0.02 * jax.random.normal(k, (DEPTH, n), f32)

    x = jax.random.normal(ks[0], (BATCH, SEQ, D_MODEL), f32)
    p = jax.random.normal(ks[1], (DEPTH, BATCH, SEQ, PLE_DIM), f32)
    positions = jnp.broadcast_to(jnp.arange(SEQ, dtype=jnp.int32)[None, :], (BATCH, SEQ))
    return {
        "x": x,
        "p": p,
        "positions": positions,
        "g_in": gain(ks[2], D_MODEL),
        "w_in": nrm(ks[3], (DEPTH, D_MODEL, IN_TOTAL), D_MODEL),
        "g_cq": gain(ks[4], Q_LORA),
        "w_uq": nrm(ks[5], (DEPTH, Q_LORA, N_HEADS * QK_DIM), Q_LORA),
        "g_ckv": gain(ks[6], KV_LORA),
        "w_ukv": nrm(ks[7], (DEPTH, KV_LORA, N_HEADS * (NOPE_DIM + V_DIM)), KV_LORA),
        "g_q": gain(ks[8], QK_DIM),
        "g_k": gain(ks[9], QK_DIM),
        "conv_w": nrm(ks[10], (DEPTH, CONV_K, CONV_WIDTH), CONV_K),
        "g_oa": gain(ks[11], ATTN_WIDTH),
        "g_oc": gain(ks[12], CONV_WIDTH),
        "w_o": nrm(ks[13], (DEPTH, D_MODEL, D_MODEL), D_MODEL),
        "w_pl": nrm(ks[14], (DEPTH, PLE_DIM, D_MODEL), PLE_DIM),
        "w_plg": nrm(ks[15], (DEPTH, D_MODEL, D_MODEL), D_MODEL),
        "g_pl": gain(ks[16], D_MODEL),
    }


def _fwd_reference(x, p, positions, g_in, w_in, g_cq, w_uq, g_ckv, w_ukv, g_q, g_k,
              conv_w, g_oa, g_oc, w_o, w_pl, w_plg, g_pl):
    cos, sin = rope_cos_sin(positions)
    h = x
    for i in range(DEPTH):
        h = hybrid_layer(h, p[i], cos, sin, g_in[i], w_in[i], g_cq[i], w_uq[i],
                         g_ckv[i], w_ukv[i], g_q[i], g_k[i], conv_w[i], g_oa[i],
                         g_oc[i], w_o[i], w_pl[i], w_plg[i], g_pl[i])
    return h.astype(x.dtype)


import jax as _jax
import jax.numpy as _jnp

TWIN_FORMAT = 'train_step'
FWD_PARAMS = ['x', 'p', 'positions', 'g_in', 'w_in', 'g_cq', 'w_uq', 'g_ckv', 'w_ukv', 'g_q', 'g_k', 'conv_w', 'g_oa', 'g_oc', 'w_o', 'w_pl', 'w_plg', 'g_pl']
TWIN_WEIGHTS = ['g_in', 'w_in', 'g_cq', 'w_uq', 'g_ckv', 'w_ukv', 'g_q', 'g_k', 'conv_w', 'g_oa', 'g_oc', 'w_o', 'w_pl', 'w_plg', 'g_pl']
TWIN_DIFF_INPUT = 'x'
TWIN_INPUTS = ['x', 'p', 'positions', 'g_in', 'w_in', 'g_cq', 'w_uq', 'g_ckv', 'w_ukv', 'g_q', 'g_k', 'conv_w', 'g_oa', 'g_oc', 'w_o', 'w_pl', 'w_plg', 'g_pl', 'loss_target', 'm_g_in', 'm_w_in', 'm_g_cq', 'm_w_uq', 'm_g_ckv', 'm_w_ukv', 'm_g_q', 'm_g_k', 'm_conv_w', 'm_g_oa', 'm_g_oc', 'm_w_o', 'm_w_pl', 'm_w_plg', 'm_g_pl', 'v_g_in', 'v_w_in', 'v_g_cq', 'v_w_uq', 'v_g_ckv', 'v_w_ukv', 'v_g_q', 'v_g_k', 'v_conv_w', 'v_g_oa', 'v_g_oc', 'v_w_o', 'v_w_pl', 'v_w_plg', 'v_g_pl']
TWIN_OUTPUTS = ['loss', 'grad_x', 'grad_g_in', 'grad_w_in', 'grad_g_cq', 'grad_w_uq', 'grad_g_ckv', 'grad_w_ukv', 'grad_g_q', 'grad_g_k', 'grad_conv_w', 'grad_g_oa', 'grad_g_oc', 'grad_w_o', 'grad_w_pl', 'grad_w_plg', 'grad_g_pl', 'delta_g_in', 'delta_w_in', 'delta_g_cq', 'delta_w_uq', 'delta_g_ckv', 'delta_w_ukv', 'delta_g_q', 'delta_g_k', 'delta_conv_w', 'delta_g_oa', 'delta_g_oc', 'delta_w_o', 'delta_w_pl', 'delta_w_plg', 'delta_g_pl', 'new_m_g_in', 'new_m_w_in', 'new_m_g_cq', 'new_m_w_uq', 'new_m_g_ckv', 'new_m_w_ukv', 'new_m_g_q', 'new_m_g_k', 'new_m_conv_w', 'new_m_g_oa', 'new_m_g_oc', 'new_m_w_o', 'new_m_w_pl', 'new_m_w_plg', 'new_m_g_pl', 'new_v_g_in', 'new_v_w_in', 'new_v_g_cq', 'new_v_w_uq', 'new_v_g_ckv', 'new_v_w_ukv', 'new_v_g_q', 'new_v_g_k', 'new_v_conv_w', 'new_v_g_oa', 'new_v_g_oc', 'new_v_w_o', 'new_v_w_pl', 'new_v_w_plg', 'new_v_g_pl']
TWIN_LEAF_KINDS = {'loss': 'loss', 'grad_x': 'grad_x', 'grad_g_in': 'grad_w', 'grad_w_in': 'grad_w', 'grad_g_cq': 'grad_w', 'grad_w_uq': 'grad_w', 'grad_g_ckv': 'grad_w', 'grad_w_ukv': 'grad_w', 'grad_g_q': 'grad_w', 'grad_g_k': 'grad_w', 'grad_conv_w': 'grad_w', 'grad_g_oa': 'grad_w', 'grad_g_oc': 'grad_w', 'grad_w_o': 'grad_w', 'grad_w_pl': 'grad_w', 'grad_w_plg': 'grad_w', 'grad_g_pl': 'grad_w', 'delta_g_in': 'delta_w', 'delta_w_in': 'delta_w', 'delta_g_cq': 'delta_w', 'delta_w_uq': 'delta_w', 'delta_g_ckv': 'delta_w', 'delta_w_ukv': 'delta_w', 'delta_g_q': 'delta_w', 'delta_g_k': 'delta_w', 'delta_conv_w': 'delta_w', 'delta_g_oa': 'delta_w', 'delta_g_oc': 'delta_w', 'delta_w_o': 'delta_w', 'delta_w_pl': 'delta_w', 'delta_w_plg': 'delta_w', 'delta_g_pl': 'delta_w', 'new_m_g_in': 'new_m', 'new_m_w_in': 'new_m', 'new_m_g_cq': 'new_m', 'new_m_w_uq': 'new_m', 'new_m_g_ckv': 'new_m', 'new_m_w_ukv': 'new_m', 'new_m_g_q': 'new_m', 'new_m_g_k': 'new_m', 'new_m_conv_w': 'new_m', 'new_m_g_oa': 'new_m', 'new_m_g_oc': 'new_m', 'new_m_w_o': 'new_m', 'new_m_w_pl': 'new_m', 'new_m_w_plg': 'new_m', 'new_m_g_pl': 'new_m', 'new_v_g_in': 'new_v', 'new_v_w_in': 'new_v', 'new_v_g_cq': 'new_v', 'new_v_w_uq': 'new_v', 'new_v_g_ckv': 'new_v', 'new_v_w_ukv': 'new_v', 'new_v_g_q': 'new_v', 'new_v_g_k': 'new_v', 'new_v_conv_w': 'new_v', 'new_v_g_oa': 'new_v', 'new_v_g_oc': 'new_v', 'new_v_w_o': 'new_v', 'new_v_w_pl': 'new_v', 'new_v_w_plg': 'new_v', 'new_v_g_pl': 'new_v'}


def _forward(args):
    return _fwd_reference(*[args[k] for k in FWD_PARAMS])


def _output_shape():
    def fwd():
        inp = _fwd_setup_inputs(0)
        return _fwd_reference(*[inp[k] for k in FWD_PARAMS])
    out = _jax.eval_shape(fwd)
    return out.shape, out.dtype

N_MICROBATCH = 1
ADAM_LR = 0.001
ADAM_B1 = 0.9
ADAM_B2 = 0.999
ADAM_EPS = 1e-08
ADAM_WD = 0.01
ADAM_STEP = 10
PER_EXAMPLE_BATCH_AXIS = {'x': 0, 'p': 1, 'positions': 0, 'loss_target': 0}
SHARED_INPUTS = []
_WEIGHT_DTYPES = {'g_in': _jnp.float32, 'w_in': _jnp.float32, 'g_cq': _jnp.float32, 'w_uq': _jnp.float32, 'g_ckv': _jnp.float32, 'w_ukv': _jnp.float32, 'g_q': _jnp.float32, 'g_k': _jnp.float32, 'conv_w': _jnp.float32, 'g_oa': _jnp.float32, 'g_oc': _jnp.float32, 'w_o': _jnp.float32, 'w_pl': _jnp.float32, 'w_plg': _jnp.float32, 'g_pl': _jnp.float32}
MOMENT_SCALE = {'g_in': 9.551318e-01, 'w_in': 5.422196e-01, 'g_cq': 8.557548e-01, 'w_uq': 4.740982e-01, 'g_ckv': 2.165499e+00, 'w_ukv': 6.253520e-01, 'g_q': 8.801523e-01, 'g_k': 8.871050e-01, 'conv_w': 1.274201e+00, 'g_oa': 6.475482e+01, 'g_oc': 7.796006e+01, 'w_o': 1.771638e+00, 'w_pl': 8.827274e-01, 'w_plg': 1.364202e-01, 'g_pl': 1.932724e+00}


def _to_microbatches(a, axis):
    t = _jnp.moveaxis(a, axis, 0)
    t = t.reshape((N_MICROBATCH, t.shape[0] // N_MICROBATCH) + t.shape[1:])
    return _jnp.moveaxis(t, 1, axis + 1)


def setup_inputs(seed: int = 0) -> dict:
    inp = _fwd_setup_inputs(seed)
    key = _jax.random.fold_in(_jax.random.key(seed), 7919)
    shape, _ = _output_shape()
    out = dict(inp)
    out["loss_target"] = _jax.random.normal(_jax.random.fold_in(key, 0), shape, _jnp.float32)
    for i, name in enumerate(TWIN_WEIGHTS):
        w = inp[name].astype(_jnp.float32)
        if MOMENT_SCALE is None:
            s = _jnp.sqrt(_jnp.mean(_jnp.square(w)) + 1e-30)
        else:
            s = MOMENT_SCALE[name]
        km, kv = _jax.random.split(_jax.random.fold_in(key, i + 1))
        out[name] = w
        out["m_" + name] = s * _jax.random.normal(km, w.shape, _jnp.float32)
        out["v_" + name] = (s * s) * _jax.random.uniform(kv, w.shape, _jnp.float32, 0.5, 1.5)
    if N_MICROBATCH > 1:
        for name, axis in PER_EXAMPLE_BATCH_AXIS.items():
            out[name] = _to_microbatches(out[name], axis)
    return {'x': out['x'], 'p': out['p'], 'positions': out['positions'], 'g_in': out['g_in'], 'w_in': out['w_in'], 'g_cq': out['g_cq'], 'w_uq': out['w_uq'], 'g_ckv': out['g_ckv'], 'w_ukv': out['w_ukv'], 'g_q': out['g_q'], 'g_k': out['g_k'], 'conv_w': out['conv_w'], 'g_oa': out['g_oa'], 'g_oc': out['g_oc'], 'w_o': out['w_o'], 'w_pl': out['w_pl'], 'w_plg': out['w_plg'], 'g_pl': out['g_pl'], 'loss_target': out['loss_target'], 'm_g_in': out['m_g_in'], 'm_w_in': out['m_w_in'], 'm_g_cq': out['m_g_cq'], 'm_w_uq': out['m_w_uq'], 'm_g_ckv': out['m_g_ckv'], 'm_w_ukv': out['m_w_ukv'], 'm_g_q': out['m_g_q'], 'm_g_k': out['m_g_k'], 'm_conv_w': out['m_conv_w'], 'm_g_oa': out['m_g_oa'], 'm_g_oc': out['m_g_oc'], 'm_w_o': out['m_w_o'], 'm_w_pl': out['m_w_pl'], 'm_w_plg': out['m_w_plg'], 'm_g_pl': out['m_g_pl'], 'v_g_in': out['v_g_in'], 'v_w_in': out['v_w_in'], 'v_g_cq': out['v_g_cq'], 'v_w_uq': out['v_w_uq'], 'v_g_ckv': out['v_g_ckv'], 'v_w_ukv': out['v_w_ukv'], 'v_g_q': out['v_g_q'], 'v_g_k': out['v_g_k'], 'v_conv_w': out['v_conv_w'], 'v_g_oa': out['v_g_oa'], 'v_g_oc': out['v_g_oc'], 'v_w_o': out['v_w_o'], 'v_w_pl': out['v_w_pl'], 'v_w_plg': out['v_w_plg'], 'v_g_pl': out['v_g_pl']}


def _loss(weights, diff, rest, loss_target):
    with _jax.named_scope("forward"):
        args = {**rest, TWIN_DIFF_INPUT: diff, **{k: w.astype(_WEIGHT_DTYPES[k]) for k, w in weights.items()}}
        y = _forward(args)
    with _jax.named_scope("loss_head"):
        err = _jnp.square(y.astype(_jnp.float32) - loss_target)
        return 0.5 * _jnp.sum(_jnp.mean(err, axis=-1)) if err.ndim else 0.5 * err


def _adamw(w, g, m, v):
    m = ADAM_B1 * m + (1.0 - ADAM_B1) * g
    v = ADAM_B2 * v + (1.0 - ADAM_B2) * _jnp.square(g)
    m_hat = m / (1.0 - ADAM_B1 ** ADAM_STEP)
    v_hat = v / (1.0 - ADAM_B2 ** ADAM_STEP)
    delta = -ADAM_LR * (m_hat / (_jnp.sqrt(v_hat) + ADAM_EPS) + ADAM_WD * w)
    return delta, m, v


def reference(x, p, positions, g_in, w_in, g_cq, w_uq, g_ckv, w_ukv, g_q, g_k, conv_w, g_oa, g_oc, w_o, w_pl, w_plg, g_pl, loss_target, m_g_in, m_w_in, m_g_cq, m_w_uq, m_g_ckv, m_w_ukv, m_g_q, m_g_k, m_conv_w, m_g_oa, m_g_oc, m_w_o, m_w_pl, m_w_plg, m_g_pl, v_g_in, v_w_in, v_g_cq, v_w_uq, v_g_ckv, v_w_ukv, v_g_q, v_g_k, v_conv_w, v_g_oa, v_g_oc, v_w_o, v_w_pl, v_w_plg, v_g_pl):
    given = dict(x=x, p=p, positions=positions, g_in=g_in, w_in=w_in, g_cq=g_cq, w_uq=w_uq, g_ckv=g_ckv, w_ukv=w_ukv, g_q=g_q, g_k=g_k, conv_w=conv_w, g_oa=g_oa, g_oc=g_oc, w_o=w_o, w_pl=w_pl, w_plg=w_plg, g_pl=g_pl, loss_target=loss_target, m_g_in=m_g_in, m_w_in=m_w_in, m_g_cq=m_g_cq, m_w_uq=m_w_uq, m_g_ckv=m_g_ckv, m_w_ukv=m_w_ukv, m_g_q=m_g_q, m_g_k=m_g_k, m_conv_w=m_conv_w, m_g_oa=m_g_oa, m_g_oc=m_g_oc, m_w_o=m_w_o, m_w_pl=m_w_pl, m_w_plg=m_w_plg, m_g_pl=m_g_pl, v_g_in=v_g_in, v_w_in=v_w_in, v_g_cq=v_g_cq, v_w_uq=v_w_uq, v_g_ckv=v_g_ckv, v_w_ukv=v_w_ukv, v_g_q=v_g_q, v_g_k=v_g_k, v_conv_w=v_conv_w, v_g_oa=v_g_oa, v_g_oc=v_g_oc, v_w_o=v_w_o, v_w_pl=v_w_pl, v_w_plg=v_w_plg, v_g_pl=v_g_pl)
    weights = {n: given[n] for n in TWIN_WEIGHTS}
    shared = {n: given[n] for n in SHARED_INPUTS}
    per_example = {n: given[n] for n in ['x', 'p', 'positions']}
    grad_fn = _jax.value_and_grad(_loss, argnums=(0, 1))

    def one_microbatch(ex, loss_target):
        ex = dict(ex)
        diff = ex.pop(TWIN_DIFF_INPUT)
        return grad_fn(weights, diff, {**shared, **ex}, loss_target)

    if N_MICROBATCH == 1:
        loss, (grad_w, grad_x) = one_microbatch(per_example, given["loss_target"])
    else:
        def body(carry, xs):
            loss_sum, grad_sum = carry
            l_k, (gw_k, gx_k) = one_microbatch(xs[0], xs[1])
            with _jax.named_scope("update"):
                return (loss_sum + l_k, _jax.tree.map(_jnp.add, grad_sum, gw_k)), gx_k

        init = (_jnp.zeros((), _jnp.float32), _jax.tree.map(_jnp.zeros_like, weights))
        (loss, grad_w), grad_x = _jax.lax.scan(body, init, (per_example, given["loss_target"]))
    with _jax.named_scope("update"):
        delta_w, new_m, new_v = {}, {}, {}
        for n in TWIN_WEIGHTS:
            delta_w[n], new_m[n], new_v[n] = _adamw(weights[n], grad_w[n], given["m_" + n], given["v_" + n])
    return (loss, grad_x, *[grad_w[n] for n in TWIN_WEIGHTS], *[delta_w[n] for n in TWIN_WEIGHTS],
            *[new_m[n] for n in TWIN_WEIGHTS], *[new_v[n] for n in TWIN_WEIGHTS])
```

```python
import functools
import math

import jax
import jax.numpy as jnp
from jax import lax
from jax.experimental import pallas as pl
from jax.experimental.pallas import tpu as pltpu

F32 = jnp.float32
BF16 = jnp.bfloat16

D_MODEL = 1024
N_HEADS = 4
NOPE = 128
ROPE = 64
V_DIM = 128
QK_DIM = NOPE + ROPE
HEAD_PAD = 256
Q_LORA = 256
KV_LORA = 128
ATTN_W = 512
CONV_W = 512
PLE = 256
IN_TOTAL = 3008
PROJ_EXT = 3072
ROPE_THETA = 10000.0
EPS = 1e-6
SCALE = 1.0 / math.sqrt(QK_DIM)
NEG = -1e30

LR, B1, B2, ADAM_EPS, WD, STEP = 0.001, 0.9, 0.999, 1e-08, 0.01, 10

N_CHIPS = 4
LANES = 128
PACK_ROWS = 11520
HALF_ROWS = PACK_ROWS // 2
VMEM_LIMIT = 56 * 1024 * 1024
MESH = pl.DeviceIdType.MESH


def _params(**kw):
    return pltpu.CompilerParams(vmem_limit_bytes=VMEM_LIMIT, **kw)


def _inv_rms(x, n):
    return lax.rsqrt(jnp.sum(x * x, axis=-1, keepdims=True) / n + EPS)


def _sigmoid(z):
    return 1.0 / (1.0 + jnp.exp(-z))


def _swap_rope_halves(b):
    lane = lax.broadcasted_iota(jnp.int32, b.shape, 1)
    swapped = jnp.where(lane < 32, pltpu.roll(b, 96, 1), pltpu.roll(b, 32, 1))
    return jnp.where(lane < ROPE, swapped, 0.0)


def _dot(a, b):
    return jnp.dot(a, b, preferred_element_type=F32)


def _dot_nt(a, b):
    return lax.dot_general(a, b, (((1,), (1,)), ((), ())), preferred_element_type=F32)


def _dot_tn(a, b):
    return lax.dot_general(a, b, (((0,), (0,)), ((), ())), preferred_element_type=F32)


def _colsum(a):
    return jnp.sum(a, axis=0, keepdims=True)


def _full(shape):
    return pl.BlockSpec(shape, lambda *_: (0,) * len(shape))


def _rope_tables(pos_ref, invf_ref, sgn_ref):
    ang = pos_ref[...].astype(F32) * invf_ref[...]
    return jnp.cos(ang), jnp.sin(ang) * sgn_ref[...]


def _fwd_proj(x, pos, g_in, w_in, g_cq, w_uq, g_ckv, w_ukv, gq, gk, invf, sgn, tm):
    T = x.shape[0]

    def body(x_ref, pos_ref, g_in_ref, w_in_ref, g_cq_ref, w_uq_ref, g_ckv_ref, w_ukv_ref, gq_ref, gk_ref,
             invf_ref, sgn_ref, proj_ref, q_ref, k_ref, v_ref):
        xv = x_ref[...]
        h = (xv * _inv_rms(xv, D_MODEL) * g_in_ref[...]).astype(BF16)
        for c0 in range(0, PROJ_EXT, 512):
            proj_ref[:, c0:c0 + 512] = _dot(h, w_in_ref[:, c0:c0 + 512])
        c_q = proj_ref[:, 0:Q_LORA]
        cqn = (c_q * _inv_rms(c_q, Q_LORA) * g_cq_ref[...]).astype(BF16)
        c_kv = proj_ref[:, Q_LORA:Q_LORA + KV_LORA]
        ckvn = (c_kv * _inv_rms(c_kv, KV_LORA) * g_ckv_ref[...]).astype(BF16)
        kpe = proj_ref[:, 384:512]
        kpe_ss = jnp.sum(kpe * kpe, axis=-1, keepdims=True)
        cos_b, sin_b = _rope_tables(pos_ref, invf_ref, sgn_ref)
        gq_a, gq_b = gq_ref[:, 0:NOPE], gq_ref[:, NOPE:HEAD_PAD]
        gk_a, gk_b = gk_ref[:, 0:NOPE], gk_ref[:, NOPE:HEAD_PAD]
        for hd in range(N_HEADS):
            c0 = hd * HEAD_PAD
            qh = _dot(cqn, w_uq_ref[:, c0:c0 + HEAD_PAD])
            a, b = qh[:, 0:NOPE], qh[:, NOPE:HEAD_PAD]
            r = lax.rsqrt((jnp.sum(a * a, axis=-1, keepdims=True) + jnp.sum(b * b, axis=-1, keepdims=True)) / QK_DIM + EPS)
            bn = b * r * gq_b
            q_ref[hd, :, 0:NOPE] = (a * r * gq_a).astype(BF16)
            q_ref[hd, :, NOPE:HEAD_PAD] = (bn * cos_b + _swap_rope_halves(bn) * sin_b).astype(BF16)
            kvh = _dot(ckvn, w_ukv_ref[:, c0:c0 + HEAD_PAD])
            ka = kvh[:, 0:NOPE]
            rk = lax.rsqrt((jnp.sum(ka * ka, axis=-1, keepdims=True) + kpe_ss) / QK_DIM + EPS)
            kbn = kpe * rk * gk_b
            k_ref[hd, :, 0:NOPE] = (ka * rk * gk_a).astype(BF16)
            k_ref[hd, :, NOPE:HEAD_PAD] = (kbn * cos_b + _swap_rope_halves(kbn) * sin_b).astype(BF16)
            v_ref[hd] = kvh[:, NOPE:HEAD_PAD].astype(BF16)

    row = lambda i: (i, 0)
    head_rows = lambda i: (0, i, 0)
    return pl.pallas_call(
        body, name="fwd_proj", grid=(T // tm,),
        in_specs=[pl.BlockSpec((tm, D_MODEL), row), pl.BlockSpec((tm, 1), row), _full((1, D_MODEL)),
                  _full((D_MODEL, PROJ_EXT)), _full((1, Q_LORA)), _full((Q_LORA, N_HEADS * HEAD_PAD)),
                  _full((1, KV_LORA)), _full((KV_LORA, N_HEADS * HEAD_PAD)), _full((1, HEAD_PAD)), _full((1, HEAD_PAD)),
                  _full((1, LANES)), _full((1, LANES))],
        out_specs=[pl.BlockSpec((tm, PROJ_EXT), row), pl.BlockSpec((N_HEADS, tm, HEAD_PAD), head_rows),
                   pl.BlockSpec((N_HEADS, tm, HEAD_PAD), head_rows), pl.BlockSpec((N_HEADS, tm, V_DIM), head_rows)],
        out_shape=[jax.ShapeDtypeStruct((T, PROJ_EXT), F32), jax.ShapeDtypeStruct((N_HEADS, T, HEAD_PAD), BF16),
                   jax.ShapeDtypeStruct((N_HEADS, T, HEAD_PAD), BF16), jax.ShapeDtypeStruct((N_HEADS, T, V_DIM), BF16)],
        compiler_params=_params(dimension_semantics=("arbitrary",)),
    )(x, pos, g_in, w_in, g_cq, w_uq, g_ckv, w_ukv, gq, gk, invf, sgn)


def _causal_mask(tq, tk, q0, k0):
    row = lax.broadcasted_iota(jnp.int32, (tq, tk), 0)
    col = lax.broadcasted_iota(jnp.int32, (tq, tk), 1)
    return col + k0 <= row + q0


def _attn_fwd(q, k, v, tq):
    T = q.shape[1]
    tk = tq

    def body(q_ref, k_ref, v_ref, o_ref, lse_ref):
        qi = pl.program_id(1)
        qb = q_ref[0]

        def step(j, carry):
            m, l, acc = carry
            ks = pl.multiple_of(j * tk, tk)
            kb = k_ref[0, pl.ds(ks, tk), :]
            vb = v_ref[0, pl.ds(ks, tk), :]
            s = _dot_nt(qb, kb) * SCALE
            s = jnp.where(_causal_mask(tq, tk, qi * tq, ks), s, NEG)
            m_new = jnp.maximum(m, jnp.max(s, axis=1, keepdims=True))
            alpha = jnp.exp(m - m_new)
            p = jnp.exp(s - m_new)
            l = alpha * l + jnp.sum(p, axis=1, keepdims=True)
            acc = alpha * acc + _dot(p.astype(BF16), vb)
            return m_new, l, acc

        init = (jnp.full((tq, 1), NEG, F32), jnp.zeros((tq, 1), F32), jnp.zeros((tq, V_DIM), F32))
        m, l, acc = lax.fori_loop(0, qi + 1, step, init)
        o_ref[...] = acc / l
        lse_ref[0] = jnp.broadcast_to(m + jnp.log(l), (tq, LANES))

    return pl.pallas_call(
        body, name="attn_fwd", grid=(N_HEADS, T // tq),
        in_specs=[pl.BlockSpec((1, tq, HEAD_PAD), lambda h, i: (h, i, 0)),
                  pl.BlockSpec((1, T, HEAD_PAD), lambda h, i: (h, 0, 0)),
                  pl.BlockSpec((1, T, V_DIM), lambda h, i: (h, 0, 0))],
        out_specs=[pl.BlockSpec((tq, V_DIM), lambda h, i: (i, h)),
                   pl.BlockSpec((1, tq, LANES), lambda h, i: (h, i, 0))],
        out_shape=[jax.ShapeDtypeStruct((T, ATTN_W), F32), jax.ShapeDtypeStruct((N_HEADS, T, LANES), F32)],
        compiler_params=_params(dimension_semantics=("arbitrary", "arbitrary")),
    )(q, k, v)


def _tail(x, o, proj, p, tgt, g_oa, g_oc, g_pl, conv_w, w_o, w_pl, w_plg, tm):
    T = x.shape[0]
    nt = T // tm

    def body(x_ref, o_ref, za_ref, cb_ref, cc_ref, cx_ref, zc_ref, cch_ref, cxh_ref, p_ref, tgt_ref,
             g_oa_ref, g_oc_ref, g_pl_ref, cw_ref, w_o_ref, w_pl_ref, w_plg_ref,
             dx1_ref, do_ref, delta_ref, dtail_ref, du_ref,
             dw_o_ref, dw_pl_ref, dw_plg_ref, dg_oa_ref, dg_oc_ref, dg_pl_ref, dcw_ref, loss_ref):
        i = pl.program_id(0)

        @pl.when(i == 0)
        def _():
            for r in (dw_o_ref, dw_pl_ref, dw_plg_ref, dg_oa_ref, dg_oc_ref, dg_pl_ref, dcw_ref, loss_ref):
                r[...] = jnp.zeros_like(r)

        xv, ov, za, cb, zc = x_ref[...], o_ref[...], za_ref[...], cb_ref[...], zc_ref[...]
        g_oa, g_oc, g_pl = g_oa_ref[...], g_oc_ref[...], g_pl_ref[...]
        w0, w1, w2 = cw_ref[0:1, :], cw_ref[1:2, :], cw_ref[2:3, :]

        sa = _sigmoid(za)
        silu_a = za * sa
        ga = ov * silu_a
        ra = _inv_rms(ga, ATTN_W)
        xa = ga * ra
        ya = xa * g_oa
        v = cc_ref[...] * cx_ref[...]
        not_first = jnp.where(i > 0, 1.0, 0.0)
        hv6 = cch_ref[6:7, :] * cxh_ref[6:7, :] * not_first
        hv7 = cch_ref[7:8, :] * cxh_ref[7:8, :] * not_first
        row = lax.broadcasted_iota(jnp.int32, v.shape, 0)
        v1 = jnp.where(row == 0, hv7, pltpu.roll(v, 1, 0))
        v2 = jnp.where(row == 0, hv6, jnp.where(row == 1, hv7, pltpu.roll(v, 2, 0)))
        u = w0 * v2 + w1 * v1 + w2 * v
        sc = _sigmoid(zc)
        silu_c = zc * sc
        gc = cb * u * silu_c
        rc = _inv_rms(gc, CONV_W)
        xc = gc * rc
        yc = xc * g_oc
        ycat = jnp.concatenate([ya, yc], axis=-1).astype(BF16)
        x1 = xv + _dot(ycat, w_o_ref[...])
        r1 = _inv_rms(x1, D_MODEL)
        xh1 = x1 * r1
        n1 = (xh1 * g_pl).astype(BF16)
        gate = _sigmoid(_dot(n1, w_plg_ref[...]))
        pb = p_ref[...].astype(BF16)
        pp = _dot(pb, w_pl_ref[...])
        err = x1 + gate * pp - tgt_ref[...]
        loss_ref[...] += 0.5 * jnp.sum(err * err) / D_MODEL
        dy = err / D_MODEL

        dpp = (dy * gate).astype(BF16)
        da = (dy * pp * gate * (1.0 - gate)).astype(BF16)
        dw_pl_ref[...] += _dot_tn(pb, dpp)
        dw_plg_ref[...] += _dot_tn(n1, da)
        dn1 = _dot_nt(da, w_plg_ref[...])
        dg_pl_ref[...] += _colsum(dn1 * xh1)
        dxh = dn1 * g_pl
        dx1 = dy + r1 * (dxh - xh1 * (jnp.sum(dxh * xh1, axis=-1, keepdims=True) / D_MODEL))
        dx1_ref[...] = dx1
        dx1b = dx1.astype(BF16)
        dw_o_ref[...] += _dot_tn(ycat, dx1b)
        dycat = _dot_nt(dx1b, w_o_ref[...])
        dya, dyc = dycat[:, 0:ATTN_W], dycat[:, ATTN_W:D_MODEL]

        dg_oa_ref[...] += _colsum(dya * xa)
        dxa = dya * g_oa
        dga = ra * (dxa - xa * (jnp.sum(dxa * xa, axis=-1, keepdims=True) / ATTN_W))
        do = (dga * silu_a).astype(BF16)
        do_ref[...] = do
        dof = do.astype(F32) * ov
        for hd in range(N_HEADS):
            dl = jnp.sum(dof[:, hd * V_DIM:(hd + 1) * V_DIM], axis=-1, keepdims=True)
            delta_ref[hd] = jnp.broadcast_to(dl, (tm, LANES))
        dtail_ref[:, 0:512] = (dga * ov * (sa * (1.0 + za * (1.0 - sa)))).astype(BF16)

        dg_oc_ref[...] += _colsum(dyc * xc)
        dxc = dyc * g_oc
        dgc = rc * (dxc - xc * (jnp.sum(dxc * xc, axis=-1, keepdims=True) / CONV_W))
        dtail_ref[:, 512:1024] = (dgc * u * silu_c).astype(BF16)
        du = dgc * cb * silu_c
        du_ref[...] = du
        dtail_ref[:, 1024:1536] = (dgc * cb * u * (sc * (1.0 + zc * (1.0 - sc)))).astype(BF16)
        dcw_ref[0:1, :] += _colsum(du * v2)
        dcw_ref[1:2, :] += _colsum(du * v1)
        dcw_ref[2:3, :] += _colsum(du * v)

    row = lambda i: (i, 0)
    col = lambda c: (lambda i: (i, c))
    halo = lambda c: (lambda i: (jnp.maximum(i * (tm // 8) - 1, 0), c))
    in_specs = [pl.BlockSpec((tm, D_MODEL), row), pl.BlockSpec((tm, ATTN_W), row)]
    in_specs += [pl.BlockSpec((tm, 512), col(c)) for c in (1, 2, 3, 4, 5)]
    in_specs += [pl.BlockSpec((8, 512), halo(3)), pl.BlockSpec((8, 512), halo(4))]
    in_specs += [pl.BlockSpec((tm, PLE), row), pl.BlockSpec((tm, D_MODEL), row),
                 _full((1, ATTN_W)), _full((1, CONV_W)), _full((1, D_MODEL)), _full((3, CONV_W)),
                 _full((D_MODEL, D_MODEL)), _full((PLE, D_MODEL)), _full((D_MODEL, D_MODEL))]
    out_specs = [pl.BlockSpec((tm, D_MODEL), row), pl.BlockSpec((tm, ATTN_W), row),
                 pl.BlockSpec((N_HEADS, tm, LANES), lambda i: (0, i, 0)), pl.BlockSpec((tm, 1536), row),
                 pl.BlockSpec((tm, CONV_W), row),
                 _full((D_MODEL, D_MODEL)), _full((PLE, D_MODEL)), _full((D_MODEL, D_MODEL)),
                 _full((1, ATTN_W)), _full((1, CONV_W)), _full((1, D_MODEL)), _full((3, CONV_W)), _full((1, LANES))]
    out_shape = [jax.ShapeDtypeStruct((T, D_MODEL), F32), jax.ShapeDtypeStruct((T, ATTN_W), BF16),
                 jax.ShapeDtypeStruct((N_HEADS, T, LANES), F32), jax.ShapeDtypeStruct((T, 1536), BF16),
                 jax.ShapeDtypeStruct((T, CONV_W), F32),
                 jax.ShapeDtypeStruct((D_MODEL, D_MODEL), F32), jax.ShapeDtypeStruct((PLE, D_MODEL), F32),
                 jax.ShapeDtypeStruct((D_MODEL, D_MODEL), F32),
                 jax.ShapeDtypeStruct((1, ATTN_W), F32), jax.ShapeDtypeStruct((1, CONV_W), F32),
                 jax.ShapeDtypeStruct((1, D_MODEL), F32), jax.ShapeDtypeStruct((3, CONV_W), F32),
                 jax.ShapeDtypeStruct((1, LANES), F32)]
    return pl.pallas_call(
        body, name="tail", grid=(nt,), in_specs=in_specs, out_specs=out_specs, out_shape=out_shape,
        compiler_params=_params(dimension_semantics=("arbitrary",)),
    )(x, o, proj, proj, proj, proj, proj, proj, proj, p, tgt, g_oa, g_oc, g_pl, conv_w, w_o, w_pl, w_plg)


def _attn_dq(q, k, v, do, lse, delta, tq):
    T = q.shape[1]
    tk = tq

    def body(q_ref, k_ref, v_ref, do_ref, lse_ref, dl_ref, dq_ref):
        qi = pl.program_id(1)
        qb, dob = q_ref[0], do_ref[...]
        lse_c, dl_c = lse_ref[0, :, 0:1], dl_ref[0, :, 0:1]

        def step(j, dq):
            ks = pl.multiple_of(j * tk, tk)
            kb = k_ref[0, pl.ds(ks, tk), :]
            vb = v_ref[0, pl.ds(ks, tk), :]
            s = _dot_nt(qb, kb) * SCALE
            s = jnp.where(_causal_mask(tq, tk, qi * tq, ks), s, NEG)
            pr = jnp.exp(s - lse_c)
            ds = pr * (_dot_nt(dob, vb) - dl_c) * SCALE
            return dq + _dot(ds.astype(BF16), kb)

        dq_ref[0] = lax.fori_loop(0, qi + 1, step, jnp.zeros((tq, HEAD_PAD), F32))

    return pl.pallas_call(
        body, name="attn_dq", grid=(N_HEADS, T // tq),
        in_specs=[pl.BlockSpec((1, tq, HEAD_PAD), lambda h, i: (h, i, 0)),
                  pl.BlockSpec((1, T, HEAD_PAD), lambda h, i: (h, 0, 0)),
                  pl.BlockSpec((1, T, V_DIM), lambda h, i: (h, 0, 0)),
                  pl.BlockSpec((tq, V_DIM), lambda h, i: (i, h)),
                  pl.BlockSpec((1, tq, LANES), lambda h, i: (h, i, 0)),
                  pl.BlockSpec((1, tq, LANES), lambda h, i: (h, i, 0))],
        out_specs=pl.BlockSpec((1, tq, HEAD_PAD), lambda h, i: (h, i, 0)),
        out_shape=jax.ShapeDtypeStruct((N_HEADS, T, HEAD_PAD), F32),
        compiler_params=_params(dimension_semantics=("arbitrary", "arbitrary")),
    )(q, k, v, do, lse, delta)


def _attn_dkv(q, k, v, do, lse_row, delta_row, tk):
    T = q.shape[1]
    tq = tk
    nq = T // tq

    def body(q_ref, k_ref, v_ref, do_ref, lse_ref, dl_ref, dk_ref, dv_ref):
        kj = pl.program_id(1)
        kb, vb = k_ref[0], v_ref[0]

        def step(i, carry):
            dk, dv = carry
            qs = pl.multiple_of(i * tq, tq)
            qb = q_ref[0, pl.ds(qs, tq), :]
            dob = do_ref[pl.ds(qs, tq), :]
            lse_r = lse_ref[0, :, pl.ds(qs, tq)]
            dl_r = dl_ref[0, :, pl.ds(qs, tq)]
            st = _dot_nt(kb, qb) * SCALE
            row = lax.broadcasted_iota(jnp.int32, (tk, tq), 0)
            col = lax.broadcasted_iota(jnp.int32, (tk, tq), 1)
            st = jnp.where(row + kj * tk <= col + qs, st, NEG)
            pt = jnp.exp(st - lse_r)
            dv = dv + _dot(pt.astype(BF16), dob)
            dst = pt * (_dot_nt(vb, dob) - dl_r) * SCALE
            dk = dk + _dot(dst.astype(BF16), qb)
            return dk, dv

        init = (jnp.zeros((tk, HEAD_PAD), F32), jnp.zeros((tk, V_DIM), F32))
        dk, dv = lax.fori_loop(kj, nq, step, init)
        dk_ref[0] = dk
        dv_ref[0] = dv

    return pl.pallas_call(
        body, name="attn_dkv", grid=(N_HEADS, T // tk),
        in_specs=[pl.BlockSpec((1, T, HEAD_PAD), lambda h, j: (h, 0, 0)),
                  pl.BlockSpec((1, tk, HEAD_PAD), lambda h, j: (h, j, 0)),
                  pl.BlockSpec((1, tk, V_DIM), lambda h, j: (h, j, 0)),
                  pl.BlockSpec((T, V_DIM), lambda h, j: (0, h)),
                  pl.BlockSpec((1, 1, T), lambda h, j: (h, 0, 0)),
                  pl.BlockSpec((1, 1, T), lambda h, j: (h, 0, 0))],
        out_specs=[pl.BlockSpec((1, tk, HEAD_PAD), lambda h, j: (h, j, 0)),
                   pl.BlockSpec((1, tk, V_DIM), lambda h, j: (h, j, 0))],
        out_shape=[jax.ShapeDtypeStruct((N_HEADS, T, HEAD_PAD), F32), jax.ShapeDtypeStruct((N_HEADS, T, V_DIM), F32)],
        compiler_params=_params(dimension_semantics=("arbitrary", "arbitrary")),
    )(q, k, v, do, lse_row, delta_row)


def _bwd_proj(x, dx1, pos, proj, dq, dk, dv, dtail, du, g_in, w_in, g_cq, w_uq, g_ckv, w_ukv, gq, gk, conv_w,
              invf, sgn, tm):
    T = x.shape[0]
    nt = T // tm

    def body(x_ref, dx1_ref, pos_ref, lat_ref, cc_ref, cx_ref, dq_ref, dk_ref, dv_ref, dtail_ref, du_ref, dun_ref,
             g_in_ref, w_in_ref, g_cq_ref, w_uq_ref, g_ckv_ref, w_ukv_ref, gq_ref, gk_ref, cw_ref, invf_ref, sgn_ref,
             gx_ref, h_ref, dproj_ref, dw_uq_ref, dw_ukv_ref, dg_in_ref, dg_cq_ref, dg_ckv_ref, dgq_ref, dgk_ref):
        i = pl.program_id(0)

        @pl.when(i == 0)
        def _():
            for r in (dw_uq_ref, dw_ukv_ref, dg_in_ref, dg_cq_ref, dg_ckv_ref, dgq_ref, dgk_ref):
                r[...] = jnp.zeros_like(r)

        xv = x_ref[...]
        r0 = _inv_rms(xv, D_MODEL)
        xh0 = xv * r0
        g_in = g_in_ref[...]
        h_ref[...] = (xh0 * g_in).astype(BF16)

        c_q = lat_ref[:, 0:Q_LORA]
        rq = _inv_rms(c_q, Q_LORA)
        xq = c_q * rq
        g_cq = g_cq_ref[...]
        cqn = (xq * g_cq).astype(BF16)
        c_kv = lat_ref[:, Q_LORA:Q_LORA + KV_LORA]
        rkv = _inv_rms(c_kv, KV_LORA)
        xkv = c_kv * rkv
        g_ckv = g_ckv_ref[...]
        ckvn = (xkv * g_ckv).astype(BF16)
        kpe = lat_ref[:, 384:512]
        kpe_ss = jnp.sum(kpe * kpe, axis=-1, keepdims=True)
        cos_b, sin_b = _rope_tables(pos_ref, invf_ref, sgn_ref)
        gq_a, gq_b = gq_ref[:, 0:NOPE], gq_ref[:, NOPE:HEAD_PAD]
        gk_a, gk_b = gk_ref[:, 0:NOPE], gk_ref[:, NOPE:HEAD_PAD]

        dkpe = jnp.zeros((tm, LANES), F32)
        dcqn = jnp.zeros((tm, Q_LORA), F32)
        dckvn = jnp.zeros((tm, KV_LORA), F32)
        for hd in range(N_HEADS):
            c0 = hd * HEAD_PAD
            qh = _dot(cqn, w_uq_ref[:, c0:c0 + HEAD_PAD])
            a, b = qh[:, 0:NOPE], qh[:, NOPE:HEAD_PAD]
            r = lax.rsqrt((jnp.sum(a * a, axis=-1, keepdims=True) + jnp.sum(b * b, axis=-1, keepdims=True)) / QK_DIM + EPS)
            xa, xb = a * r, b * r
            dan = dq_ref[hd, :, 0:NOPE]
            dbr = dq_ref[hd, :, NOPE:HEAD_PAD]
            dbn = dbr * cos_b + _swap_rope_halves(dbr * sin_b)
            dgq_ref[:, 0:NOPE] += _colsum(dan * xa)
            dgq_ref[:, NOPE:HEAD_PAD] += _colsum(dbn * xb)
            dxa, dxb = dan * gq_a, dbn * gq_b
            cq = (jnp.sum(dxa * xa, axis=-1, keepdims=True) + jnp.sum(dxb * xb, axis=-1, keepdims=True)) / QK_DIM
            dqh = jnp.concatenate([r * (dxa - xa * cq), r * (dxb - xb * cq)], axis=-1).astype(BF16)
            dw_uq_ref[:, c0:c0 + HEAD_PAD] += _dot_tn(cqn, dqh)
            dcqn = dcqn + _dot_nt(dqh, w_uq_ref[:, c0:c0 + HEAD_PAD])
            kvh = _dot(ckvn, w_ukv_ref[:, c0:c0 + HEAD_PAD])
            ka = kvh[:, 0:NOPE]
            rk = lax.rsqrt((jnp.sum(ka * ka, axis=-1, keepdims=True) + kpe_ss) / QK_DIM + EPS)
            xka, xkb = ka * rk, kpe * rk
            dkan = dk_ref[hd, :, 0:NOPE]
            dkbr = dk_ref[hd, :, NOPE:HEAD_PAD]
            dkbn = dkbr * cos_b + _swap_rope_halves(dkbr * sin_b)
            dgk_ref[:, 0:NOPE] += _colsum(dkan * xka)
            dgk_ref[:, NOPE:HEAD_PAD] += _colsum(dkbn * xkb)
            dxka, dxkb = dkan * gk_a, dkbn * gk_b
            ck = (jnp.sum(dxka * xka, axis=-1, keepdims=True) + jnp.sum(dxkb * xkb, axis=-1, keepdims=True)) / QK_DIM
            dkpe = dkpe + rk * (dxkb - xkb * ck)
            dkvh = jnp.concatenate([rk * (dxka - xka * ck), dv_ref[hd]], axis=-1).astype(BF16)
            dw_ukv_ref[:, c0:c0 + HEAD_PAD] += _dot_tn(ckvn, dkvh)
            dckvn = dckvn + _dot_nt(dkvh, w_ukv_ref[:, c0:c0 + HEAD_PAD])

        dg_cq_ref[...] += _colsum(dcqn * xq)
        dxq = dcqn * g_cq
        dproj_ref[:, 0:Q_LORA] = (rq * (dxq - xq * (jnp.sum(dxq * xq, axis=-1, keepdims=True) / Q_LORA))).astype(BF16)
        dg_ckv_ref[...] += _colsum(dckvn * xkv)
        dxkv = dckvn * g_ckv
        dproj_ref[:, 256:384] = (rkv * (dxkv - xkv * (jnp.sum(dxkv * xkv, axis=-1, keepdims=True) / KV_LORA))).astype(BF16)
        dproj_ref[:, 384:512] = dkpe.astype(BF16)
        dproj_ref[:, 512:1536] = dtail_ref[:, 0:1024]
        dproj_ref[:, 2560:3072] = dtail_ref[:, 1024:1536]

        du_v = du_ref[...]
        not_last = jnp.where(i < nt - 1, 1.0, 0.0)
        nx0 = dun_ref[0:1, :] * not_last
        nx1 = dun_ref[1:2, :] * not_last
        row = lax.broadcasted_iota(jnp.int32, du_v.shape, 0)
        du1 = jnp.where(row == tm - 1, nx0, pltpu.roll(du_v, tm - 1, 0))
        du2 = jnp.where(row == tm - 2, nx0, jnp.where(row == tm - 1, nx1, pltpu.roll(du_v, tm - 2, 0)))
        dvc = cw_ref[2:3, :] * du_v + cw_ref[1:2, :] * du1 + cw_ref[0:1, :] * du2
        dproj_ref[:, 1536:2048] = (dvc * cx_ref[...]).astype(BF16)
        dproj_ref[:, 2048:2560] = (dvc * cc_ref[...]).astype(BF16)

        dh = jnp.zeros((tm, D_MODEL), F32)
        for c0 in range(0, PROJ_EXT, 512):
            dh = dh + _dot_nt(dproj_ref[:, c0:c0 + 512], w_in_ref[:, c0:c0 + 512])
        dg_in_ref[...] += _colsum(dh * xh0)
        dxh = dh * g_in
        gx_ref[...] = dx1_ref[...] + r0 * (dxh - xh0 * (jnp.sum(dxh * xh0, axis=-1, keepdims=True) / D_MODEL))

    row = lambda i: (i, 0)
    col = lambda c: (lambda i: (i, c))
    head_rows = lambda i: (0, i, 0)
    nxt = lambda i: (jnp.minimum((i + 1) * (tm // 8), T // 8 - 1), 0)
    in_specs = [pl.BlockSpec((tm, D_MODEL), row), pl.BlockSpec((tm, D_MODEL), row), pl.BlockSpec((tm, 1), row),
                pl.BlockSpec((tm, 512), col(0)), pl.BlockSpec((tm, 512), col(3)), pl.BlockSpec((tm, 512), col(4)),
                pl.BlockSpec((N_HEADS, tm, HEAD_PAD), head_rows), pl.BlockSpec((N_HEADS, tm, HEAD_PAD), head_rows),
                pl.BlockSpec((N_HEADS, tm, V_DIM), head_rows), pl.BlockSpec((tm, 1536), row),
                pl.BlockSpec((tm, CONV_W), row), pl.BlockSpec((8, CONV_W), nxt),
                _full((1, D_MODEL)), _full((D_MODEL, PROJ_EXT)), _full((1, Q_LORA)), _full((Q_LORA, N_HEADS * HEAD_PAD)),
                _full((1, KV_LORA)), _full((KV_LORA, N_HEADS * HEAD_PAD)), _full((1, HEAD_PAD)), _full((1, HEAD_PAD)),
                _full((3, CONV_W)), _full((1, LANES)), _full((1, LANES))]
    out_specs = [pl.BlockSpec((tm, D_MODEL), row), pl.BlockSpec((tm, D_MODEL), row), pl.BlockSpec((tm, PROJ_EXT), row),
                 _full((Q_LORA, N_HEADS * HEAD_PAD)), _full((KV_LORA, N_HEADS * HEAD_PAD)),
                 _full((1, D_MODEL)), _full((1, Q_LORA)), _full((1, KV_LORA)), _full((1, HEAD_PAD)), _full((1, HEAD_PAD))]
    out_shape = [jax.ShapeDtypeStruct((T, D_MODEL), F32), jax.ShapeDtypeStruct((T, D_MODEL), BF16),
                 jax.ShapeDtypeStruct((T, PROJ_EXT), BF16),
                 jax.ShapeDtypeStruct((Q_LORA, N_HEADS * HEAD_PAD), F32), jax.ShapeDtypeStruct((KV_LORA, N_HEADS * HEAD_PAD), F32),
                 jax.ShapeDtypeStruct((1, D_MODEL), F32), jax.ShapeDtypeStruct((1, Q_LORA), F32),
                 jax.ShapeDtypeStruct((1, KV_LORA), F32), jax.ShapeDtypeStruct((1, HEAD_PAD), F32),
                 jax.ShapeDtypeStruct((1, HEAD_PAD), F32)]
    return pl.pallas_call(
        body, name="bwd_proj", grid=(nt,), in_specs=in_specs, out_specs=out_specs, out_shape=out_shape,
        compiler_params=_params(dimension_semantics=("arbitrary",)),
    )(x, dx1, pos, proj, proj, proj, dq, dk, dv, dtail, du, du, g_in, w_in, g_cq, w_uq, g_ckv, w_ukv, gq, gk, conv_w,
      invf, sgn)


def _matmul_tn(a, b, tt, tn):
    T, M = a.shape
    N = b.shape[1]

    def body(a_ref, b_ref, o_ref):
        @pl.when(pl.program_id(1) == 0)
        def _():
            o_ref[...] = jnp.zeros_like(o_ref)

        o_ref[...] += _dot_tn(a_ref[...], b_ref[...])

    return pl.pallas_call(
        body, name="dw_in", grid=(N // tn, T // tt),
        in_specs=[pl.BlockSpec((tt, M), lambda j, t: (t, 0)), pl.BlockSpec((tt, tn), lambda j, t: (t, j))],
        out_specs=pl.BlockSpec((M, tn), lambda j, t: (0, j)),
        out_shape=jax.ShapeDtypeStruct((M, N), F32),
        compiler_params=_params(dimension_semantics=("arbitrary", "arbitrary")),
    )(a, b)


def _row_block(rows):
    for rb in (1152, 1024, 960, 768, 752, 512, 256, 128, 64, 32, 16, 8):
        if rows % rb == 0:
            return rb
    return rows


def _add2(a, b, name):
    n, rows, _ = a.shape
    rb = _row_block(rows)

    def body(a_ref, b_ref, o_ref):
        o_ref[...] = a_ref[...] + b_ref[...]

    spec = pl.BlockSpec((1, rb, LANES), lambda j, i: (j, i, 0))
    return pl.pallas_call(body, name=name, grid=(n, rows // rb), in_specs=[spec, spec], out_specs=spec,
                          out_shape=jax.ShapeDtypeStruct(a.shape, F32))(a, b)


def _sum4(a, name):
    _, rows, _ = a.shape
    rb = _row_block(rows)

    def body(a_ref, o_ref):
        o_ref[...] = ((a_ref[0] + a_ref[1]) + a_ref[2]) + a_ref[3]

    return pl.pallas_call(body, name=name, grid=(rows // rb,),
                          in_specs=[pl.BlockSpec((N_CHIPS, rb, LANES), lambda i: (0, i, 0))],
                          out_specs=pl.BlockSpec((rb, LANES), lambda i: (i, 0)),
                          out_shape=jax.ShapeDtypeStruct((rows, LANES), F32))(a)


def _adamw(w, g, m, v, name):
    rows, cols = w.shape
    rb = 256 if rows * cols > 512 * 1024 else rows

    def body(w_ref, g_ref, m_ref, v_ref, d_ref, nm_ref, nv_ref):
        gv = g_ref[...]
        nm = B1 * m_ref[...] + (1.0 - B1) * gv
        nv = B2 * v_ref[...] + (1.0 - B2) * (gv * gv)
        m_hat = nm / (1.0 - B1 ** STEP)
        v_hat = nv / (1.0 - B2 ** STEP)
        d_ref[...] = -LR * (m_hat / (jnp.sqrt(v_hat) + ADAM_EPS) + WD * w_ref[...])
        nm_ref[...] = nm
        nv_ref[...] = nv

    spec = pl.BlockSpec((rb, cols), lambda i: (i, 0))
    shp = jax.ShapeDtypeStruct(w.shape, F32)
    return pl.pallas_call(body, name=name, grid=(rows // rb,), in_specs=[spec] * 4, out_specs=[spec] * 3,
                          out_shape=[shp] * 3)(w, g, m, v)


_ANY = pl.BlockSpec(memory_space=pl.ANY)


def _mesh_pos():
    return lax.axis_index("x"), lax.axis_index("y"), lax.axis_index("c")


def _other_chips(x, y):
    return [(1 - x, y), (x, 1 - y), (1 - x, 1 - y)]


def _gather_weights(wpack):
    rows = wpack.shape[0]
    half = rows // 2

    def body(w_ref, out_ref, send_sems, recv_sems, local_sem):
        x, y, c = _mesh_pos()
        me = 2 * x + y
        chips = _other_chips(x, y)

        def part(slot, hc):
            return out_ref.at[slot, pl.ds(hc * half, half), :]

        def copy(k, src, dst, to):
            return pltpu.make_async_remote_copy(src_ref=src, dst_ref=dst, send_sem=send_sems.at[k],
                                                recv_sem=recv_sems.at[k], device_id=to, device_id_type=MESH)

        mine = pltpu.make_async_copy(w_ref, out_ref.at[me], local_sem)
        mine.start()
        my_half = w_ref.at[pl.ds(c * half, half), :]
        first = [copy(j, my_half, part(me, c), (cx, cy, c)) for j, (cx, cy) in enumerate(chips)]
        for cp in first:
            cp.start()
        passed = []
        for j, (cx, cy) in enumerate(chips):
            got = part(2 * cx + cy, c)
            copy(j, got, got, (cx, cy, c)).wait_recv()
            fwd = copy(3 + j, got, got, (x, y, 1 - c))
            fwd.start()
            passed.append(fwd)
        for j, (cx, cy) in enumerate(chips):
            got = part(2 * cx + cy, 1 - c)
            copy(3 + j, got, got, (x, y, 1 - c)).wait_recv()
        for cp in first + passed:
            cp.wait_send()
        mine.wait()

    return pl.pallas_call(
        body, name="gather_weights", in_specs=[_ANY], out_specs=_ANY,
        out_shape=jax.ShapeDtypeStruct((N_CHIPS, rows, LANES), wpack.dtype),
        scratch_shapes=[pltpu.SemaphoreType.DMA((6,)), pltpu.SemaphoreType.DMA((6,)), pltpu.SemaphoreType.DMA],
    )(wpack)


def _swap_with_sibling(a, name):
    def body(a_ref, out_ref, send_sem, recv_sem):
        x, y, c = _mesh_pos()
        cp = pltpu.make_async_remote_copy(src_ref=a_ref, dst_ref=out_ref, send_sem=send_sem, recv_sem=recv_sem,
                                          device_id=(x, y, 1 - c), device_id_type=MESH)
        cp.start()
        cp.wait()

    return pl.pallas_call(
        body, name=name, in_specs=[_ANY], out_specs=_ANY, out_shape=jax.ShapeDtypeStruct(a.shape, a.dtype),
        scratch_shapes=[pltpu.SemaphoreType.DMA, pltpu.SemaphoreType.DMA],
    )(a)


def _scatter_to_chips(part):
    _, rows, _ = part.shape

    def body(p_ref, out_ref, send_sems, recv_sems, local_sem):
        x, y, c = _mesh_pos()
        me = 2 * x + y
        chips = _other_chips(x, y)
        mine = pltpu.make_async_copy(p_ref.at[me], out_ref.at[me], local_sem)
        mine.start()
        sends = []
        for j, (cx, cy) in enumerate(chips):
            cp = pltpu.make_async_remote_copy(src_ref=p_ref.at[2 * cx + cy], dst_ref=out_ref.at[me],
                                              send_sem=send_sems.at[j], recv_sem=recv_sems.at[j],
                                              device_id=(cx, cy, c), device_id_type=MESH)
            cp.start()
            sends.append(cp)
        for j, (cx, cy) in enumerate(chips):
            pltpu.make_async_remote_copy(src_ref=p_ref.at[me], dst_ref=out_ref.at[2 * cx + cy],
                                         send_sem=send_sems.at[j], recv_sem=recv_sems.at[j],
                                         device_id=(cx, cy, c), device_id_type=MESH).wait_recv()
        for cp in sends:
            cp.wait_send()
        mine.wait()

    return pl.pallas_call(
        body, name="scatter_grads", in_specs=[_ANY], out_specs=_ANY,
        out_shape=jax.ShapeDtypeStruct(part.shape, part.dtype),
        scratch_shapes=[pltpu.SemaphoreType.DMA((3,)), pltpu.SemaphoreType.DMA((3,)), pltpu.SemaphoreType.DMA],
    )(part)


def _pack_rows(parts, rows, dtype):
    flat = jnp.concatenate([a.reshape(-1).astype(dtype) for a in parts])
    flat = jnp.concatenate([flat, jnp.zeros((rows * LANES - flat.shape[0],), dtype)])
    return flat.reshape(rows, LANES)


_SHARD_SHAPES = [(D_MODEL, IN_TOTAL // N_CHIPS), (Q_LORA, N_HEADS * QK_DIM // N_CHIPS), (KV_LORA, N_HEADS * 256 // N_CHIPS),
                 (3, CONV_W // N_CHIPS), (D_MODEL // N_CHIPS, D_MODEL), (PLE, D_MODEL // N_CHIPS),
                 (D_MODEL // N_CHIPS, D_MODEL)]
_GAIN_SIZES = [D_MODEL, Q_LORA, KV_LORA, HEAD_PAD, HEAD_PAD, ATTN_W, CONV_W, D_MODEL, LANES]


def _unpack(flat, shapes):
    out, off = [], 0
    for shp in shapes:
        n = math.prod(shp)
        out.append(flat[off:off + n].reshape(shp))
        off += n
    return out, off


def _shard_cols(w, k, n):
    return w[:, k * n:(k + 1) * n]


def _shard_rows(w, k, n):
    return w[k * n:(k + 1) * n, :]


def _local_step(x, p, pos, tgt, gains, w_in, w_uq, w_ukv, conv_w, w_o, w_pl, w_plg, tm, tq):
    g_in, g_cq, g_ckv, g_q, g_k, g_oa, g_oc, g_pl = gains
    T = x.shape[0]
    zpad = lambda a, n: jnp.concatenate([a, jnp.zeros(a.shape[:-1] + (n,), a.dtype)], axis=-1)
    w_in_e = jnp.concatenate([w_in[:, :448], jnp.zeros((D_MODEL, 64), BF16), w_in[:, 448:]], axis=1)
    w_uq_e = zpad(w_uq.reshape(Q_LORA, N_HEADS, QK_DIM), HEAD_PAD - QK_DIM).reshape(Q_LORA, N_HEADS * HEAD_PAD)
    gq, gk = zpad(g_q, HEAD_PAD - QK_DIM), zpad(g_k, HEAD_PAD - QK_DIM)
    inv_freq = 1.0 / (ROPE_THETA ** (jnp.arange(0, ROPE, 2, dtype=F32) / ROPE))
    invf = jnp.concatenate([inv_freq, inv_freq, jnp.zeros((64,), F32)]).reshape(1, LANES)
    sgn = jnp.concatenate([-jnp.ones((32,), F32), jnp.ones((32,), F32), jnp.zeros((64,), F32)]).reshape(1, LANES)

    proj, q, k, v = _fwd_proj(x, pos, g_in, w_in_e, g_cq, w_uq_e, g_ckv, w_ukv, gq, gk, invf, sgn, tm)
    o, lse = _attn_fwd(q, k, v, tq)
    (dx1, do, delta, dtail, du, dw_o, dw_pl, dw_plg, dg_oa, dg_oc, dg_pl, dconv, loss) = _tail(
        x, o, proj, p, tgt, g_oa, g_oc, g_pl, conv_w, w_o, w_pl, w_plg, tm)
    dq = _attn_dq(q, k, v, do, lse, delta, tq)
    lse_row = lse[:, :, 0].reshape(N_HEADS, 1, T)
    delta_row = delta[:, :, 0].reshape(N_HEADS, 1, T)
    dk, dv = _attn_dkv(q, k, v, do, lse_row, delta_row, tq)
    (gx, h, dproj, dw_uq_e, dw_ukv, dg_in, dg_cq, dg_ckv, dgq, dgk) = _bwd_proj(
        x, dx1, pos, proj, dq, dk, dv, dtail, du, g_in, w_in_e, g_cq, w_uq_e, g_ckv, w_ukv, gq, gk, conv_w, invf, sgn, tm)
    dw_in_e = _matmul_tn(h, dproj, min(512, T), 512)
    dw_in = jnp.concatenate([dw_in_e[:, :448], dw_in_e[:, 512:]], axis=1)
    dw_uq = dw_uq_e.reshape(Q_LORA, N_HEADS, HEAD_PAD)[:, :, :QK_DIM].reshape(Q_LORA, N_HEADS * QK_DIM)
    wgrads = (dw_in, dw_uq, dw_ukv, dconv, dw_o, dw_pl, dw_plg)
    ggrads = (dg_in, dg_cq, dg_ckv, dgq, dgk, dg_oa, dg_oc, dg_pl)
    return loss, gx, wgrads, ggrads


def kernel(x, p, positions, g_in, w_in, g_cq, w_uq, g_ckv, w_ukv, g_q, g_k, conv_w, g_oa, g_oc, w_o, w_pl, w_plg, g_pl, loss_target, m_g_in, m_w_in, m_g_cq, m_w_uq, m_g_ckv, m_w_ukv, m_g_q, m_g_k, m_conv_w, m_g_oa, m_g_oc, m_w_o, m_w_pl, m_w_plg, m_g_pl, v_g_in, v_w_in, v_g_cq, v_w_uq, v_g_ckv, v_w_ukv, v_g_q, v_g_k, v_conv_w, v_g_oa, v_g_oc, v_w_o, v_w_pl, v_w_plg, v_g_pl):
    T = x.shape[1]
    c = lax.axis_index("c")
    shards = [w_in[0], w_uq[0], w_ukv[0], conv_w[0], w_o[0], w_pl[0], w_plg[0]]
    gains = [g_in[0:1], g_cq[0:1], g_ckv[0:1], g_q[0:1], g_k[0:1], g_oa[0:1], g_oc[0:1], g_pl[0:1]]
    gains = [g.reshape(1, -1) for g in gains]

    gathered = _gather_weights(_pack_rows(shards, PACK_ROWS, BF16))
    per_chip = [_unpack(gathered[k].reshape(-1), _SHARD_SHAPES)[0] for k in range(N_CHIPS)]
    cat = lambda i, axis: jnp.concatenate([per_chip[k][i] for k in range(N_CHIPS)], axis=axis)
    w_in_f = cat(0, 1)
    w_uq_f = cat(1, 1)
    w_ukv_f = cat(2, 1)
    conv_f = cat(3, 1).astype(F32)
    w_o_f = cat(4, 0)
    w_pl_f = cat(5, 1)
    w_plg_f = cat(6, 0)

    loss, gx, wgrads, ggrads = _local_step(
        x[0], p[0, 0], positions.reshape(T, 1), loss_target[0], gains,
        w_in_f, w_uq_f, w_ukv_f, conv_f, w_o_f, w_pl_f, w_plg_f, 256, 512)

    dw_in, dw_uq, dw_ukv, dconv, dw_o, dw_pl, dw_plg = wgrads
    small = list(ggrads) + [loss]
    packs = []
    for k in range(N_CHIPS):
        parts = [_shard_cols(dw_in, k, IN_TOTAL // N_CHIPS), _shard_cols(dw_uq, k, N_HEADS * QK_DIM // N_CHIPS),
                 _shard_cols(dw_ukv, k, 256), _shard_cols(dconv, k, CONV_W // N_CHIPS),
                 _shard_rows(dw_o, k, D_MODEL // N_CHIPS), _shard_cols(dw_pl, k, D_MODEL // N_CHIPS),
                 _shard_rows(dw_plg, k, D_MODEL // N_CHIPS)] + small
        packs.append(_pack_rows(parts, PACK_ROWS, F32))
    contrib = jnp.stack(packs).reshape(N_CHIPS, 2, HALF_ROWS, LANES)
    keep = lax.dynamic_index_in_dim(contrib, c, axis=1, keepdims=False)
    give = lax.dynamic_index_in_dim(contrib, 1 - c, axis=1, keepdims=False)
    chip_part = _add2(keep, _swap_with_sibling(give, "pair_grads"), "add_pair")
    my_half = _sum4(_scatter_to_chips(chip_part), "add_chips")
    other_half = _swap_with_sibling(my_half, "share_halves")
    lo = jnp.where(c == 0, my_half, other_half)
    hi = jnp.where(c == 0, other_half, my_half)
    total = jnp.concatenate([lo, hi], axis=0).reshape(-1)

    wg, off = _unpack(total, _SHARD_SHAPES)
    gg, _ = _unpack(total[off:], [(1, n) for n in _GAIN_SIZES])
    loss_out = gg[8][0, 0]
    gg = [gg[0], gg[1], gg[2], gg[3][:, :QK_DIM], gg[4][:, :QK_DIM], gg[5], gg[6], gg[7]]
    g_by_name = dict(g_in=gg[0], w_in=wg[0], g_cq=gg[1], w_uq=wg[1], g_ckv=gg[2], w_ukv=wg[2], g_q=gg[3], g_k=gg[4],
                     conv_w=wg[3], g_oa=gg[5], g_oc=gg[6], w_o=wg[4], w_pl=wg[5], w_plg=wg[6], g_pl=gg[7])
    weights = dict(g_in=g_in, w_in=w_in, g_cq=g_cq, w_uq=w_uq, g_ckv=g_ckv, w_ukv=w_ukv, g_q=g_q, g_k=g_k,
                   conv_w=conv_w, g_oa=g_oa, g_oc=g_oc, w_o=w_o, w_pl=w_pl, w_plg=w_plg, g_pl=g_pl)
    ms = dict(g_in=m_g_in, w_in=m_w_in, g_cq=m_g_cq, w_uq=m_w_uq, g_ckv=m_g_ckv, w_ukv=m_w_ukv, g_q=m_g_q, g_k=m_g_k,
              conv_w=m_conv_w, g_oa=m_g_oa, g_oc=m_g_oc, w_o=m_w_o, w_pl=m_w_pl, w_plg=m_w_plg, g_pl=m_g_pl)
    vs = dict(g_in=v_g_in, w_in=v_w_in, g_cq=v_g_cq, w_uq=v_w_uq, g_ckv=v_g_ckv, w_ukv=v_w_ukv, g_q=v_g_q, g_k=v_g_k,
              conv_w=v_conv_w, g_oa=v_g_oa, g_oc=v_g_oc, w_o=v_w_o, w_pl=v_w_pl, w_plg=v_w_plg, g_pl=v_g_pl)
    names = list(weights)
    grads, deltas, new_m, new_v = [], [], [], []
    for n in names:
        w = weights[n]
        w2 = w.reshape(-1, w.shape[-1])
        g2 = g_by_name[n].reshape(w2.shape)
        d, nm, nv = _adamw(w2, g2, ms[n].reshape(w2.shape), vs[n].reshape(w2.shape), "adamw_" + n)
        grads.append(g2.reshape(w.shape))
        deltas.append(d.reshape(w.shape))
        new_m.append(nm.reshape(w.shape))
        new_v.append(nv.reshape(w.shape))
    return (loss_out, gx.reshape(x.shape), *grads, *deltas, *new_m, *new_v)
```

```python
import functools
import math

import jax
import jax.numpy as jnp
from jax import lax
from jax.experimental import pallas as pl
from jax.experimental.pallas import tpu as pltpu

F32 = jnp.float32
BF16 = jnp.bfloat16

D_MODEL = 1024
N_HEADS = 4
NOPE = 128
ROPE = 64
V_DIM = 128
QK_DIM = NOPE + ROPE
HEAD_PAD = 256
Q_LORA = 256
KV_LORA = 128
ATTN_W = 512
CONV_W = 512
PLE = 256
IN_TOTAL = 3008
PROJ_EXT = 3072
ROPE_THETA = 10000.0
EPS = 1e-6
SCALE = 1.0 / math.sqrt(QK_DIM)
EXP2_SCALE = SCALE * math.log2(math.e)
NEG = -1e30
SOFTMAX_ROWS = 32

LR, B1, B2, ADAM_EPS, WD, STEP = 0.001, 0.9, 0.999, 1e-08, 0.01, 10

N_CHIPS = 4
LANES = 128
PACK_ROWS = 11520
HALF_ROWS = PACK_ROWS // 2
VMEM_LIMIT = 56 * 1024 * 1024
MESH = pl.DeviceIdType.MESH


def _params(**kw):
    return pltpu.CompilerParams(vmem_limit_bytes=VMEM_LIMIT, **kw)


def _inv_rms(x, n):
    return lax.rsqrt(jnp.sum(x * x, axis=-1, keepdims=True) / n + EPS)


def _sigmoid(z):
    return 1.0 / (1.0 + jnp.exp(-z))


def _swap_rope_halves(b):
    lane = lax.broadcasted_iota(jnp.int32, b.shape, 1)
    swapped = jnp.where(lane < 32, pltpu.roll(b, 96, 1), pltpu.roll(b, 32, 1))
    return jnp.where(lane < ROPE, swapped, 0.0)


def _dot(a, b):
    return jnp.dot(a, b, preferred_element_type=F32)


def _dot_nt(a, b):
    return lax.dot_general(a, b, (((1,), (1,)), ((), ())), preferred_element_type=F32)


def _dot_tn(a, b):
    return lax.dot_general(a, b, (((0,), (0,)), ((), ())), preferred_element_type=F32)


def _colsum(a):
    return jnp.sum(a, axis=0, keepdims=True)


def _full(shape):
    return pl.BlockSpec(shape, lambda *_: (0,) * len(shape))


def _rope_tables(pos_ref, invf_ref, sgn_ref):
    ang = pos_ref[...].astype(F32) * invf_ref[...]
    return jnp.cos(ang), jnp.sin(ang) * sgn_ref[...]


def _fwd_proj(x, pos, g_in, w_in, g_cq, w_uq, g_ckv, w_ukv, gq, gk, invf, sgn, tm):
    T = x.shape[0]

    def body(x_ref, pos_ref, g_in_ref, w_in_ref, g_cq_ref, w_uq_ref, g_ckv_ref, w_ukv_ref, gq_ref, gk_ref,
             invf_ref, sgn_ref, proj_ref, q_ref, k_ref, v_ref):
        xv = x_ref[...]
        h = (xv * _inv_rms(xv, D_MODEL) * g_in_ref[...]).astype(BF16)
        for c0 in range(0, PROJ_EXT, 512):
            proj_ref[:, c0:c0 + 512] = _dot(h, w_in_ref[:, c0:c0 + 512])
        c_q = proj_ref[:, 0:Q_LORA]
        cqn = (c_q * _inv_rms(c_q, Q_LORA) * g_cq_ref[...]).astype(BF16)
        c_kv = proj_ref[:, Q_LORA:Q_LORA + KV_LORA]
        ckvn = (c_kv * _inv_rms(c_kv, KV_LORA) * g_ckv_ref[...]).astype(BF16)
        kpe = proj_ref[:, 384:512]
        kpe_ss = jnp.sum(kpe * kpe, axis=-1, keepdims=True)
        cos_b, sin_b = _rope_tables(pos_ref, invf_ref, sgn_ref)
        gq_a, gq_b = gq_ref[:, 0:NOPE], gq_ref[:, NOPE:HEAD_PAD]
        gk_a, gk_b = gk_ref[:, 0:NOPE], gk_ref[:, NOPE:HEAD_PAD]
        for hd in range(N_HEADS):
            c0 = hd * HEAD_PAD
            qh = _dot(cqn, w_uq_ref[:, c0:c0 + HEAD_PAD])
            a, b = qh[:, 0:NOPE], qh[:, NOPE:HEAD_PAD]
            r = lax.rsqrt((jnp.sum(a * a, axis=-1, keepdims=True) + jnp.sum(b * b, axis=-1, keepdims=True)) / QK_DIM + EPS)
            bn = b * r * gq_b
            q_ref[hd, :, 0:NOPE] = (a * r * gq_a).astype(BF16)
            q_ref[hd, :, NOPE:HEAD_PAD] = (bn * cos_b + _swap_rope_halves(bn) * sin_b).astype(BF16)
            kvh = _dot(ckvn, w_ukv_ref[:, c0:c0 + HEAD_PAD])
            ka = kvh[:, 0:NOPE]
            rk = lax.rsqrt((jnp.sum(ka * ka, axis=-1, keepdims=True) + kpe_ss) / QK_DIM + EPS)
            kbn = kpe * rk * gk_b
            k_ref[hd, :, 0:NOPE] = (ka * rk * gk_a).astype(BF16)
            k_ref[hd, :, NOPE:HEAD_PAD] = (kbn * cos_b + _swap_rope_halves(kbn) * sin_b).astype(BF16)
            v_ref[hd, :, 0:V_DIM] = kvh[:, NOPE:HEAD_PAD].astype(BF16)
            v_ref[hd, :, V_DIM:2 * V_DIM] = jnp.ones((tm, V_DIM), BF16)

    row = lambda i: (i, 0)
    head_rows = lambda i: (0, i, 0)
    return pl.pallas_call(
        body, name="fwd_proj", grid=(T // tm,),
        in_specs=[pl.BlockSpec((tm, D_MODEL), row), pl.BlockSpec((tm, 1), row), _full((1, D_MODEL)),
                  _full((D_MODEL, PROJ_EXT)), _full((1, Q_LORA)), _full((Q_LORA, N_HEADS * HEAD_PAD)),
                  _full((1, KV_LORA)), _full((KV_LORA, N_HEADS * HEAD_PAD)), _full((1, HEAD_PAD)), _full((1, HEAD_PAD)),
                  _full((1, LANES)), _full((1, LANES))],
        out_specs=[pl.BlockSpec((tm, PROJ_EXT), row), pl.BlockSpec((N_HEADS, tm, HEAD_PAD), head_rows),
                   pl.BlockSpec((N_HEADS, tm, HEAD_PAD), head_rows), pl.BlockSpec((N_HEADS, tm, 2 * V_DIM), head_rows)],
        out_shape=[jax.ShapeDtypeStruct((T, PROJ_EXT), F32), jax.ShapeDtypeStruct((N_HEADS, T, HEAD_PAD), BF16),
                   jax.ShapeDtypeStruct((N_HEADS, T, HEAD_PAD), BF16), jax.ShapeDtypeStruct((N_HEADS, T, 2 * V_DIM), BF16)],
        compiler_params=_params(dimension_semantics=("arbitrary",)),
    )(x, pos, g_in, w_in, g_cq, w_uq, g_ckv, w_ukv, gq, gk, invf, sgn)


def _causal_mask(tq, tk, q0, k0):
    row = lax.broadcasted_iota(jnp.int32, (tq, tk), 0)
    col = lax.broadcasted_iota(jnp.int32, (tq, tk), 1)
    return col + k0 <= row + q0


def _attn_fwd(q, k, v, tq):
    T = q.shape[1]
    tk = tq

    rc = min(SOFTMAX_ROWS, tq)

    def body(q_ref, k_ref, v_ref, o_ref, lse_ref, s0, s1, p0, p1, a0, a1, m_ref, acc_ref):
        qi = pl.program_id(1)
        s_buf, p_buf, a_buf = (s0, s1), (p0, p1), (a0, a1)

        def scores(t, slot):
            ks = pl.multiple_of(t * tk, tk)
            s_buf[slot][...] = _dot_nt(q_ref[0], k_ref[0, pl.ds(ks, tk), :])

        def values(t, slot):
            ks = pl.multiple_of(t * tk, tk)
            acc_ref[...] = acc_ref[...] * a_buf[slot][...] + _dot(p_buf[slot][...], v_ref[0, pl.ds(ks, tk), :])

        def softmax(slot, masked):
            s_all = s_buf[slot][...]
            if masked:
                row = lax.broadcasted_iota(jnp.int32, (tq, tk), 0)
                col = lax.broadcasted_iota(jnp.int32, (tq, tk), 1)
                s_all = jnp.where(col <= row, s_all, NEG)
                s_buf[slot][...] = s_all
            m_old = m_ref[...]
            m_new = jnp.maximum(m_old, jnp.max(s_all, axis=1, keepdims=True))
            a_buf[slot][...] = jnp.exp2((m_old - m_new) * EXP2_SCALE)
            m_ref[...] = m_new
            for r0 in range(0, tq, rc):
                s = s_buf[slot][r0:r0 + rc, :]
                p_buf[slot][r0:r0 + rc, :] = jnp.exp2((s - m_new[r0:r0 + rc, :]) * EXP2_SCALE).astype(BF16)

        def iteration(t, slot):
            scores(t + 1, 1 - slot)
            values(jnp.maximum(t - 1, 0), 1 - slot)
            softmax(slot, False)

        def finish(slot):
            values(jnp.maximum(qi - 1, 0), 1 - slot)
            softmax(slot, True)
            values(qi, slot)
            l = acc_ref[:, V_DIM:2 * V_DIM]
            o_ref[...] = acc_ref[:, 0:V_DIM] / l
            lse_ref[0] = m_ref[...] * SCALE + jnp.log(l)

        m_ref[...] = jnp.full_like(m_ref, NEG)
        acc_ref[...] = jnp.zeros_like(acc_ref)
        p1[...] = jnp.zeros_like(p1)
        a1[...] = jnp.ones_like(a1)
        scores(0, 0)

        def pair(tt, carry):
            iteration(2 * tt, 0)
            iteration(2 * tt + 1, 1)
            return carry

        lax.fori_loop(0, qi // 2, pair, 0)
        odd = lax.rem(qi, 2) == 1

        @pl.when(odd)
        def _():
            iteration(qi - 1, 0)
            finish(1)

        @pl.when(jnp.logical_not(odd))
        def _():
            finish(0)

    return pl.pallas_call(
        body, name="attn_fwd", grid=(N_HEADS, T // tq),
        in_specs=[pl.BlockSpec((1, tq, HEAD_PAD), lambda h, i: (h, i, 0)),
                  pl.BlockSpec((1, T, HEAD_PAD), lambda h, i: (h, 0, 0)),
                  pl.BlockSpec((1, T, 2 * V_DIM), lambda h, i: (h, 0, 0))],
        out_specs=[pl.BlockSpec((tq, V_DIM), lambda h, i: (i, h)),
                   pl.BlockSpec((1, tq, LANES), lambda h, i: (h, i, 0))],
        out_shape=[jax.ShapeDtypeStruct((T, ATTN_W), F32), jax.ShapeDtypeStruct((N_HEADS, T, LANES), F32)],
        scratch_shapes=[pltpu.VMEM((tq, tk), F32), pltpu.VMEM((tq, tk), F32), pltpu.VMEM((tq, tk), BF16),
                        pltpu.VMEM((tq, tk), BF16), pltpu.VMEM((tq, 1), F32), pltpu.VMEM((tq, 1), F32),
                        pltpu.VMEM((tq, 1), F32), pltpu.VMEM((tq, 2 * V_DIM), F32)],
        compiler_params=_params(dimension_semantics=("arbitrary", "arbitrary")),
    )(q, k, v)


def _tail(x, o, proj, p, tgt, g_oa, g_oc, g_pl, conv_w, w_o, w_pl, w_plg, tm):
    T = x.shape[0]
    nt = T // tm

    def body(x_ref, o_ref, za_ref, cb_ref, cc_ref, cx_ref, zc_ref, cch_ref, cxh_ref, p_ref, tgt_ref,
             g_oa_ref, g_oc_ref, g_pl_ref, cw_ref, w_o_ref, w_pl_ref, w_plg_ref,
             dx1_ref, do_ref, delta_ref, dtail_ref, du_ref,
             dw_o_ref, dw_pl_ref, dw_plg_ref, dg_oa_ref, dg_oc_ref, dg_pl_ref, dcw_ref, loss_ref):
        i = pl.program_id(0)

        @pl.when(i == 0)
        def _():
            for r in (dw_o_ref, dw_pl_ref, dw_plg_ref, dg_oa_ref, dg_oc_ref, dg_pl_ref, dcw_ref, loss_ref):
                r[...] = jnp.zeros_like(r)

        xv, ov, za, cb, zc = x_ref[...], o_ref[...], za_ref[...], cb_ref[...], zc_ref[...]
        g_oa, g_oc, g_pl = g_oa_ref[...], g_oc_ref[...], g_pl_ref[...]
        w0, w1, w2 = cw_ref[0:1, :], cw_ref[1:2, :], cw_ref[2:3, :]

        sa = _sigmoid(za)
        silu_a = za * sa
        ga = ov * silu_a
        ra = _inv_rms(ga, ATTN_W)
        xa = ga * ra
        ya = xa * g_oa
        v = cc_ref[...] * cx_ref[...]
        not_first = jnp.where(i > 0, 1.0, 0.0)
        hv6 = cch_ref[6:7, :] * cxh_ref[6:7, :] * not_first
        hv7 = cch_ref[7:8, :] * cxh_ref[7:8, :] * not_first
        row = lax.broadcasted_iota(jnp.int32, v.shape, 0)
        v1 = jnp.where(row == 0, hv7, pltpu.roll(v, 1, 0))
        v2 = jnp.where(row == 0, hv6, jnp.where(row == 1, hv7, pltpu.roll(v, 2, 0)))
        u = w0 * v2 + w1 * v1 + w2 * v
        sc = _sigmoid(zc)
        silu_c = zc * sc
        gc = cb * u * silu_c
        rc = _inv_rms(gc, CONV_W)
        xc = gc * rc
        yc = xc * g_oc
        ycat = jnp.concatenate([ya, yc], axis=-1).astype(BF16)
        x1 = xv + _dot(ycat, w_o_ref[...])
        r1 = _inv_rms(x1, D_MODEL)
        xh1 = x1 * r1
        n1 = (xh1 * g_pl).astype(BF16)
        gate = _sigmoid(_dot(n1, w_plg_ref[...]))
        pb = p_ref[...].astype(BF16)
        pp = _dot(pb, w_pl_ref[...])
        err = x1 + gate * pp - tgt_ref[...]
        loss_ref[...] += 0.5 * jnp.sum(err * err) / D_MODEL
        dy = err / D_MODEL

        dpp = (dy * gate).astype(BF16)
        da = (dy * pp * gate * (1.0 - gate)).astype(BF16)
        dw_pl_ref[...] += _dot_tn(pb, dpp)
        dw_plg_ref[...] += _dot_tn(n1, da)
        dn1 = _dot_nt(da, w_plg_ref[...])
        dg_pl_ref[...] += _colsum(dn1 * xh1)
        dxh = dn1 * g_pl
        dx1 = dy + r1 * (dxh - xh1 * (jnp.sum(dxh * xh1, axis=-1, keepdims=True) / D_MODEL))
        dx1_ref[...] = dx1
        dx1b = dx1.astype(BF16)
        dw_o_ref[...] += _dot_tn(ycat, dx1b)
        dycat = _dot_nt(dx1b, w_o_ref[...])
        dya, dyc = dycat[:, 0:ATTN_W], dycat[:, ATTN_W:D_MODEL]

        dg_oa_ref[...] += _colsum(dya * xa)
        dxa = dya * g_oa
        dga = ra * (dxa - xa * (jnp.sum(dxa * xa, axis=-1, keepdims=True) / ATTN_W))
        do = (dga * silu_a).astype(BF16)
        do_ref[...] = do
        dof = do.astype(F32) * ov
        for hd in range(N_HEADS):
            dl = jnp.sum(dof[:, hd * V_DIM:(hd + 1) * V_DIM], axis=-1, keepdims=True)
            delta_ref[hd] = jnp.broadcast_to(dl, (tm, LANES))
        dtail_ref[:, 0:512] = (dga * ov * (sa * (1.0 + za * (1.0 - sa)))).astype(BF16)

        dg_oc_ref[...] += _colsum(dyc * xc)
        dxc = dyc * g_oc
        dgc = rc * (dxc - xc * (jnp.sum(dxc * xc, axis=-1, keepdims=True) / CONV_W))
        dtail_ref[:, 512:1024] = (dgc * u * silu_c).astype(BF16)
        du = dgc * cb * silu_c
        du_ref[...] = du
        dtail_ref[:, 1024:1536] = (dgc * cb * u * (sc * (1.0 + zc * (1.0 - sc)))).astype(BF16)
        dcw_ref[0:1, :] += _colsum(du * v2)
        dcw_ref[1:2, :] += _colsum(du * v1)
        dcw_ref[2:3, :] += _colsum(du * v)

    row = lambda i: (i, 0)
    col = lambda c: (lambda i: (i, c))
    halo = lambda c: (lambda i: (jnp.maximum(i * (tm // 8) - 1, 0), c))
    in_specs = [pl.BlockSpec((tm, D_MODEL), row), pl.BlockSpec((tm, ATTN_W), row)]
    in_specs += [pl.BlockSpec((tm, 512), col(c)) for c in (1, 2, 3, 4, 5)]
    in_specs += [pl.BlockSpec((8, 512), halo(3)), pl.BlockSpec((8, 512), halo(4))]
    in_specs += [pl.BlockSpec((tm, PLE), row), pl.BlockSpec((tm, D_MODEL), row),
                 _full((1, ATTN_W)), _full((1, CONV_W)), _full((1, D_MODEL)), _full((3, CONV_W)),
                 _full((D_MODEL, D_MODEL)), _full((PLE, D_MODEL)), _full((D_MODEL, D_MODEL))]
    out_specs = [pl.BlockSpec((tm, D_MODEL), row), pl.BlockSpec((tm, ATTN_W), row),
                 pl.BlockSpec((N_HEADS, tm, LANES), lambda i: (0, i, 0)), pl.BlockSpec((tm, 1536), row),
                 pl.BlockSpec((tm, CONV_W), row),
                 _full((D_MODEL, D_MODEL)), _full((PLE, D_MODEL)), _full((D_MODEL, D_MODEL)),
                 _full((1, ATTN_W)), _full((1, CONV_W)), _full((1, D_MODEL)), _full((3, CONV_W)), _full((1, LANES))]
    out_shape = [jax.ShapeDtypeStruct((T, D_MODEL), F32), jax.ShapeDtypeStruct((T, ATTN_W), BF16),
                 jax.ShapeDtypeStruct((N_HEADS, T, LANES), F32), jax.ShapeDtypeStruct((T, 1536), BF16),
                 jax.ShapeDtypeStruct((T, CONV_W), F32),
                 jax.ShapeDtypeStruct((D_MODEL, D_MODEL), F32), jax.ShapeDtypeStruct((PLE, D_MODEL), F32),
                 jax.ShapeDtypeStruct((D_MODEL, D_MODEL), F32),
                 jax.ShapeDtypeStruct((1, ATTN_W), F32), jax.ShapeDtypeStruct((1, CONV_W), F32),
                 jax.ShapeDtypeStruct((1, D_MODEL), F32), jax.ShapeDtypeStruct((3, CONV_W), F32),
                 jax.ShapeDtypeStruct((1, LANES), F32)]
    return pl.pallas_call(
        body, name="tail", grid=(nt,), in_specs=in_specs, out_specs=out_specs, out_shape=out_shape,
        compiler_params=_params(dimension_semantics=("arbitrary",)),
    )(x, o, proj, proj, proj, proj, proj, proj, proj, p, tgt, g_oa, g_oc, g_pl, conv_w, w_o, w_pl, w_plg)


def _attn_dq(q, k, v, do, lse, delta, tq):
    T = q.shape[1]
    tk = tq

    def body(q_ref, k_ref, v_ref, do_ref, lse_ref, dl_ref, dq_ref):
        qi = pl.program_id(1)
        qb, dob = q_ref[0], do_ref[...]
        lse_c, dl_c = lse_ref[0, :, 0:1], dl_ref[0, :, 0:1]

        def step(j, dq):
            ks = pl.multiple_of(j * tk, tk)
            kb = k_ref[0, pl.ds(ks, tk), :]
            vb = v_ref[0, pl.ds(ks, tk), :]
            s = _dot_nt(qb, kb) * SCALE
            s = jnp.where(_causal_mask(tq, tk, qi * tq, ks), s, NEG)
            pr = jnp.exp(s - lse_c)
            ds = pr * (_dot_nt(dob, vb) - dl_c) * SCALE
            return dq + _dot(ds.astype(BF16), kb)

        dq_ref[0] = lax.fori_loop(0, qi + 1, step, jnp.zeros((tq, HEAD_PAD), F32))

    return pl.pallas_call(
        body, name="attn_dq", grid=(N_HEADS, T // tq),
        in_specs=[pl.BlockSpec((1, tq, HEAD_PAD), lambda h, i: (h, i, 0)),
                  pl.BlockSpec((1, T, HEAD_PAD), lambda h, i: (h, 0, 0)),
                  pl.BlockSpec((1, T, V_DIM), lambda h, i: (h, 0, 0)),
                  pl.BlockSpec((tq, V_DIM), lambda h, i: (i, h)),
                  pl.BlockSpec((1, tq, LANES), lambda h, i: (h, i, 0)),
                  pl.BlockSpec((1, tq, LANES), lambda h, i: (h, i, 0))],
        out_specs=pl.BlockSpec((1, tq, HEAD_PAD), lambda h, i: (h, i, 0)),
        out_shape=jax.ShapeDtypeStruct((N_HEADS, T, HEAD_PAD), F32),
        compiler_params=_params(dimension_semantics=("arbitrary", "arbitrary")),
    )(q, k, v, do, lse, delta)


def _attn_dkv(q, k, v, do, lse_row, delta_row, tk):
    T = q.shape[1]
    tq = tk
    nq = T // tq

    def body(q_ref, k_ref, v_ref, do_ref, lse_ref, dl_ref, dk_ref, dv_ref):
        kj = pl.program_id(1)
        kb, vb = k_ref[0], v_ref[0]

        def step(i, carry):
            dk, dv = carry
            qs = pl.multiple_of(i * tq, tq)
            qb = q_ref[0, pl.ds(qs, tq), :]
            dob = do_ref[pl.ds(qs, tq), :]
            lse_r = lse_ref[0, :, pl.ds(qs, tq)]
            dl_r = dl_ref[0, :, pl.ds(qs, tq)]
            st = _dot_nt(kb, qb) * SCALE
            row = lax.broadcasted_iota(jnp.int32, (tk, tq), 0)
            col = lax.broadcasted_iota(jnp.int32, (tk, tq), 1)
            st = jnp.where(row + kj * tk <= col + qs, st, NEG)
            pt = jnp.exp(st - lse_r)
            dv = dv + _dot(pt.astype(BF16), dob)
            dst = pt * (_dot_nt(vb, dob) - dl_r) * SCALE
            dk = dk + _dot(dst.astype(BF16), qb)
            return dk, dv

        init = (jnp.zeros((tk, HEAD_PAD), F32), jnp.zeros((tk, V_DIM), F32))
        dk, dv = lax.fori_loop(kj, nq, step, init)
        dk_ref[0] = dk
        dv_ref[0] = dv

    return pl.pallas_call(
        body, name="attn_dkv", grid=(N_HEADS, T // tk),
        in_specs=[pl.BlockSpec((1, T, HEAD_PAD), lambda h, j: (h, 0, 0)),
                  pl.BlockSpec((1, tk, HEAD_PAD), lambda h, j: (h, j, 0)),
                  pl.BlockSpec((1, tk, V_DIM), lambda h, j: (h, j, 0)),
                  pl.BlockSpec((T, V_DIM), lambda h, j: (0, h)),
                  pl.BlockSpec((1, 1, T), lambda h, j: (h, 0, 0)),
                  pl.BlockSpec((1, 1, T), lambda h, j: (h, 0, 0))],
        out_specs=[pl.BlockSpec((1, tk, HEAD_PAD), lambda h, j: (h, j, 0)),
                   pl.BlockSpec((1, tk, V_DIM), lambda h, j: (h, j, 0))],
        out_shape=[jax.ShapeDtypeStruct((N_HEADS, T, HEAD_PAD), F32), jax.ShapeDtypeStruct((N_HEADS, T, V_DIM), F32)],
        compiler_params=_params(dimension_semantics=("arbitrary", "arbitrary")),
    )(q, k, v, do, lse_row, delta_row)


def _bwd_proj(x, dx1, pos, proj, dq, dk, dv, dtail, du, g_in, w_in, g_cq, w_uq, g_ckv, w_ukv, gq, gk, conv_w,
              invf, sgn, tm):
    T = x.shape[0]
    nt = T // tm

    def body(x_ref, dx1_ref, pos_ref, lat_ref, cc_ref, cx_ref, dq_ref, dk_ref, dv_ref, dtail_ref, du_ref, dun_ref,
             g_in_ref, w_in_ref, g_cq_ref, w_uq_ref, g_ckv_ref, w_ukv_ref, gq_ref, gk_ref, cw_ref, invf_ref, sgn_ref,
             gx_ref, h_ref, dproj_ref, dw_uq_ref, dw_ukv_ref, dg_in_ref, dg_cq_ref, dg_ckv_ref, dgq_ref, dgk_ref):
        i = pl.program_id(0)

        @pl.when(i == 0)
        def _():
            for r in (dw_uq_ref, dw_ukv_ref, dg_in_ref, dg_cq_ref, dg_ckv_ref, dgq_ref, dgk_ref):
                r[...] = jnp.zeros_like(r)

        xv = x_ref[...]
        r0 = _inv_rms(xv, D_MODEL)
        xh0 = xv * r0
        g_in = g_in_ref[...]
        h_ref[...] = (xh0 * g_in).astype(BF16)

        c_q = lat_ref[:, 0:Q_LORA]
        rq = _inv_rms(c_q, Q_LORA)
        xq = c_q * rq
        g_cq = g_cq_ref[...]
        cqn = (xq * g_cq).astype(BF16)
        c_kv = lat_ref[:, Q_LORA:Q_LORA + KV_LORA]
        rkv = _inv_rms(c_kv, KV_LORA)
        xkv = c_kv * rkv
        g_ckv = g_ckv_ref[...]
        ckvn = (xkv * g_ckv).astype(BF16)
        kpe = lat_ref[:, 384:512]
        kpe_ss = jnp.sum(kpe * kpe, axis=-1, keepdims=True)
        cos_b, sin_b = _rope_tables(pos_ref, invf_ref, sgn_ref)
        gq_a, gq_b = gq_ref[:, 0:NOPE], gq_ref[:, NOPE:HEAD_PAD]
        gk_a, gk_b = gk_ref[:, 0:NOPE], gk_ref[:, NOPE:HEAD_PAD]

        dkpe = jnp.zeros((tm, LANES), F32)
        dcqn = jnp.zeros((tm, Q_LORA), F32)
        dckvn = jnp.zeros((tm, KV_LORA), F32)
        for hd in range(N_HEADS):
            c0 = hd * HEAD_PAD
            qh = _dot(cqn, w_uq_ref[:, c0:c0 + HEAD_PAD])
            a, b = qh[:, 0:NOPE], qh[:, NOPE:HEAD_PAD]
            r = lax.rsqrt((jnp.sum(a * a, axis=-1, keepdims=True) + jnp.sum(b * b, axis=-1, keepdims=True)) / QK_DIM + EPS)
            xa, xb = a * r, b * r
            dan = dq_ref[hd, :, 0:NOPE]
            dbr = dq_ref[hd, :, NOPE:HEAD_PAD]
            dbn = dbr * cos_b + _swap_rope_halves(dbr * sin_b)
            dgq_ref[:, 0:NOPE] += _colsum(dan * xa)
            dgq_ref[:, NOPE:HEAD_PAD] += _colsum(dbn * xb)
            dxa, dxb = dan * gq_a, dbn * gq_b
            cq = (jnp.sum(dxa * xa, axis=-1, keepdims=True) + jnp.sum(dxb * xb, axis=-1, keepdims=True)) / QK_DIM
            dqh = jnp.concatenate([r * (dxa - xa * cq), r * (dxb - xb * cq)], axis=-1).astype(BF16)
            dw_uq_ref[:, c0:c0 + HEAD_PAD] += _dot_tn(cqn, dqh)
            dcqn = dcqn + _dot_nt(dqh, w_uq_ref[:, c0:c0 + HEAD_PAD])
            kvh = _dot(ckvn, w_ukv_ref[:, c0:c0 + HEAD_PAD])
            ka = kvh[:, 0:NOPE]
            rk = lax.rsqrt((jnp.sum(ka * ka, axis=-1, keepdims=True) + kpe_ss) / QK_DIM + EPS)
            xka, xkb = ka * rk, kpe * rk
            dkan = dk_ref[hd, :, 0:NOPE]
            dkbr = dk_ref[hd, :, NOPE:HEAD_PAD]
            dkbn = dkbr * cos_b + _swap_rope_halves(dkbr * sin_b)
            dgk_ref[:, 0:NOPE] += _colsum(dkan * xka)
            dgk_ref[:, NOPE:HEAD_PAD] += _colsum(dkbn * xkb)
            dxka, dxkb = dkan * gk_a, dkbn * gk_b
            ck = (jnp.sum(dxka * xka, axis=-1, keepdims=True) + jnp.sum(dxkb * xkb, axis=-1, keepdims=True)) / QK_DIM
            dkpe = dkpe + rk * (dxkb - xkb * ck)
            dkvh = jnp.concatenate([rk * (dxka - xka * ck), dv_ref[hd]], axis=-1).astype(BF16)
            dw_ukv_ref[:, c0:c0 + HEAD_PAD] += _dot_tn(ckvn, dkvh)
            dckvn = dckvn + _dot_nt(dkvh, w_ukv_ref[:, c0:c0 + HEAD_PAD])

        dg_cq_ref[...] += _colsum(dcqn * xq)
        dxq = dcqn * g_cq
        dproj_ref[:, 0:Q_LORA] = (rq * (dxq - xq * (jnp.sum(dxq * xq, axis=-1, keepdims=True) / Q_LORA))).astype(BF16)
        dg_ckv_ref[...] += _colsum(dckvn * xkv)
        dxkv = dckvn * g_ckv
        dproj_ref[:, 256:384] = (rkv * (dxkv - xkv * (jnp.sum(dxkv * xkv, axis=-1, keepdims=True) / KV_LORA))).astype(BF16)
        dproj_ref[:, 384:512] = dkpe.astype(BF16)
        dproj_ref[:, 512:1536] = dtail_ref[:, 0:1024]
        dproj_ref[:, 2560:3072] = dtail_ref[:, 1024:1536]

        du_v = du_ref[...]
        not_last = jnp.where(i < nt - 1, 1.0, 0.0)
        nx0 = dun_ref[0:1, :] * not_last
        nx1 = dun_ref[1:2, :] * not_last
        row = lax.broadcasted_iota(jnp.int32, du_v.shape, 0)
        du1 = jnp.where(row == tm - 1, nx0, pltpu.roll(du_v, tm - 1, 0))
        du2 = jnp.where(row == tm - 2, nx0, jnp.where(row == tm - 1, nx1, pltpu.roll(du_v, tm - 2, 0)))
        dvc = cw_ref[2:3, :] * du_v + cw_ref[1:2, :] * du1 + cw_ref[0:1, :] * du2
        dproj_ref[:, 1536:2048] = (dvc * cx_ref[...]).astype(BF16)
        dproj_ref[:, 2048:2560] = (dvc * cc_ref[...]).astype(BF16)

        dh = jnp.zeros((tm, D_MODEL), F32)
        for c0 in range(0, PROJ_EXT, 512):
            dh = dh + _dot_nt(dproj_ref[:, c0:c0 + 512], w_in_ref[:, c0:c0 + 512])
        dg_in_ref[...] += _colsum(dh * xh0)
        dxh = dh * g_in
        gx_ref[...] = dx1_ref[...] + r0 * (dxh - xh0 * (jnp.sum(dxh * xh0, axis=-1, keepdims=True) / D_MODEL))

    row = lambda i: (i, 0)
    col = lambda c: (lambda i: (i, c))
    head_rows = lambda i: (0, i, 0)
    nxt = lambda i: (jnp.minimum((i + 1) * (tm // 8), T // 8 - 1), 0)
    in_specs = [pl.BlockSpec((tm, D_MODEL), row), pl.BlockSpec((tm, D_MODEL), row), pl.BlockSpec((tm, 1), row),
                pl.BlockSpec((tm, 512), col(0)), pl.BlockSpec((tm, 512), col(3)), pl.BlockSpec((tm, 512), col(4)),
                pl.BlockSpec((N_HEADS, tm, HEAD_PAD), head_rows), pl.BlockSpec((N_HEADS, tm, HEAD_PAD), head_rows),
                pl.BlockSpec((N_HEADS, tm, V_DIM), head_rows), pl.BlockSpec((tm, 1536), row),
                pl.BlockSpec((tm, CONV_W), row), pl.BlockSpec((8, CONV_W), nxt),
                _full((1, D_MODEL)), _full((D_MODEL, PROJ_EXT)), _full((1, Q_LORA)), _full((Q_LORA, N_HEADS * HEAD_PAD)),
                _full((1, KV_LORA)), _full((KV_LORA, N_HEADS * HEAD_PAD)), _full((1, HEAD_PAD)), _full((1, HEAD_PAD)),
                _full((3, CONV_W)), _full((1, LANES)), _full((1, LANES))]
    out_specs = [pl.BlockSpec((tm, D_MODEL), row), pl.BlockSpec((tm, D_MODEL), row), pl.BlockSpec((tm, PROJ_EXT), row),
                 _full((Q_LORA, N_HEADS * HEAD_PAD)), _full((KV_LORA, N_HEADS * HEAD_PAD)),
                 _full((1, D_MODEL)), _full((1, Q_LORA)), _full((1, KV_LORA)), _full((1, HEAD_PAD)), _full((1, HEAD_PAD))]
    out_shape = [jax.ShapeDtypeStruct((T, D_MODEL), F32), jax.ShapeDtypeStruct((T, D_MODEL), BF16),
                 jax.ShapeDtypeStruct((T, PROJ_EXT), BF16),
                 jax.ShapeDtypeStruct((Q_LORA, N_HEADS * HEAD_PAD), F32), jax.ShapeDtypeStruct((KV_LORA, N_HEADS * HEAD_PAD), F32),
                 jax.ShapeDtypeStruct((1, D_MODEL), F32), jax.ShapeDtypeStruct((1, Q_LORA), F32),
                 jax.ShapeDtypeStruct((1, KV_LORA), F32), jax.ShapeDtypeStruct((1, HEAD_PAD), F32),
                 jax.ShapeDtypeStruct((1, HEAD_PAD), F32)]
    return pl.pallas_call(
        body, name="bwd_proj", grid=(nt,), in_specs=in_specs, out_specs=out_specs, out_shape=out_shape,
        compiler_params=_params(dimension_semantics=("arbitrary",)),
    )(x, dx1, pos, proj, proj, proj, dq, dk, dv, dtail, du, du, g_in, w_in, g_cq, w_uq, g_ckv, w_ukv, gq, gk, conv_w,
      invf, sgn)


def _matmul_tn(a, b, tt, tn):
    T, M = a.shape
    N = b.shape[1]

    def body(a_ref, b_ref, o_ref):
        @pl.when(pl.program_id(1) == 0)
        def _():
            o_ref[...] = jnp.zeros_like(o_ref)

        o_ref[...] += _dot_tn(a_ref[...], b_ref[...])

    return pl.pallas_call(
        body, name="dw_in", grid=(N // tn, T // tt),
        in_specs=[pl.BlockSpec((tt, M), lambda j, t: (t, 0)), pl.BlockSpec((tt, tn), lambda j, t: (t, j))],
        out_specs=pl.BlockSpec((M, tn), lambda j, t: (0, j)),
        out_shape=jax.ShapeDtypeStruct((M, N), F32),
        compiler_params=_params(dimension_semantics=("arbitrary", "arbitrary")),
    )(a, b)


def _row_block(rows):
    for rb in (1152, 1024, 960, 768, 752, 512, 256, 128, 64, 32, 16, 8):
        if rows % rb == 0:
            return rb
    return rows


def _add2(a, b, name):
    n, rows, _ = a.shape
    rb = _row_block(rows)

    def body(a_ref, b_ref, o_ref):
        o_ref[...] = a_ref[...] + b_ref[...]

    spec = pl.BlockSpec((1, rb, LANES), lambda j, i: (j, i, 0))
    return pl.pallas_call(body, name=name, grid=(n, rows // rb), in_specs=[spec, spec], out_specs=spec,
                          out_shape=jax.ShapeDtypeStruct(a.shape, F32))(a, b)


def _sum4(a, name):
    _, rows, _ = a.shape
    rb = _row_block(rows)

    def body(a_ref, o_ref):
        o_ref[...] = ((a_ref[0] + a_ref[1]) + a_ref[2]) + a_ref[3]

    return pl.pallas_call(body, name=name, grid=(rows // rb,),
                          in_specs=[pl.BlockSpec((N_CHIPS, rb, LANES), lambda i: (0, i, 0))],
                          out_specs=pl.BlockSpec((rb, LANES), lambda i: (i, 0)),
                          out_shape=jax.ShapeDtypeStruct((rows, LANES), F32))(a)


def _adamw(w, g, m, v, name):
    rows, cols = w.shape
    rb = 256 if rows * cols > 512 * 1024 else rows

    def body(w_ref, g_ref, m_ref, v_ref, d_ref, nm_ref, nv_ref):
        gv = g_ref[...]
        nm = B1 * m_ref[...] + (1.0 - B1) * gv
        nv = B2 * v_ref[...] + (1.0 - B2) * (gv * gv)
        m_hat = nm / (1.0 - B1 ** STEP)
        v_hat = nv / (1.0 - B2 ** STEP)
        d_ref[...] = -LR * (m_hat / (jnp.sqrt(v_hat) + ADAM_EPS) + WD * w_ref[...])
        nm_ref[...] = nm
        nv_ref[...] = nv

    spec = pl.BlockSpec((rb, cols), lambda i: (i, 0))
    shp = jax.ShapeDtypeStruct(w.shape, F32)
    return pl.pallas_call(body, name=name, grid=(rows // rb,), in_specs=[spec] * 4, out_specs=[spec] * 3,
                          out_shape=[shp] * 3)(w, g, m, v)


_ANY = pl.BlockSpec(memory_space=pl.ANY)


def _mesh_pos():
    return lax.axis_index("x"), lax.axis_index("y"), lax.axis_index("c")


def _other_chips(x, y):
    return [(1 - x, y), (x, 1 - y), (1 - x, 1 - y)]


def _gather_weights(wpack):
    rows = wpack.shape[0]
    half = rows // 2

    def body(w_ref, out_ref, send_sems, recv_sems, local_sem):
        x, y, c = _mesh_pos()
        me = 2 * x + y
        chips = _other_chips(x, y)

        def part(slot, hc):
            return out_ref.at[slot, pl.ds(hc * half, half), :]

        def copy(k, src, dst, to):
            return pltpu.make_async_remote_copy(src_ref=src, dst_ref=dst, send_sem=send_sems.at[k],
                                                recv_sem=recv_sems.at[k], device_id=to, device_id_type=MESH)

        mine = pltpu.make_async_copy(w_ref, out_ref.at[me], local_sem)
        mine.start()
        my_half = w_ref.at[pl.ds(c * half, half), :]
        first = [copy(j, my_half, part(me, c), (cx, cy, c)) for j, (cx, cy) in enumerate(chips)]
        for cp in first:
            cp.start()
        passed = []
        for j, (cx, cy) in enumerate(chips):
            got = part(2 * cx + cy, c)
            copy(j, got, got, (cx, cy, c)).wait_recv()
            fwd = copy(3 + j, got, got, (x, y, 1 - c))
            fwd.start()
            passed.append(fwd)
        for j, (cx, cy) in enumerate(chips):
            got = part(2 * cx + cy, 1 - c)
            copy(3 + j, got, got, (x, y, 1 - c)).wait_recv()
        for cp in first + passed:
            cp.wait_send()
        mine.wait()

    return pl.pallas_call(
        body, name="gather_weights", in_specs=[_ANY], out_specs=_ANY,
        out_shape=jax.ShapeDtypeStruct((N_CHIPS, rows, LANES), wpack.dtype),
        scratch_shapes=[pltpu.SemaphoreType.DMA((6,)), pltpu.SemaphoreType.DMA((6,)), pltpu.SemaphoreType.DMA],
    )(wpack)


def _swap_with_sibling(a, name):
    def body(a_ref, out_ref, send_sem, recv_sem):
        x, y, c = _mesh_pos()
        cp = pltpu.make_async_remote_copy(src_ref=a_ref, dst_ref=out_ref, send_sem=send_sem, recv_sem=recv_sem,
                                          device_id=(x, y, 1 - c), device_id_type=MESH)
        cp.start()
        cp.wait()

    return pl.pallas_call(
        body, name=name, in_specs=[_ANY], out_specs=_ANY, out_shape=jax.ShapeDtypeStruct(a.shape, a.dtype),
        scratch_shapes=[pltpu.SemaphoreType.DMA, pltpu.SemaphoreType.DMA],
    )(a)


def _scatter_to_chips(part):
    _, rows, _ = part.shape

    def body(p_ref, out_ref, send_sems, recv_sems, local_sem):
        x, y, c = _mesh_pos()
        me = 2 * x + y
        chips = _other_chips(x, y)
        mine = pltpu.make_async_copy(p_ref.at[me], out_ref.at[me], local_sem)
        mine.start()
        sends = []
        for j, (cx, cy) in enumerate(chips):
            cp = pltpu.make_async_remote_copy(src_ref=p_ref.at[2 * cx + cy], dst_ref=out_ref.at[me],
                                              send_sem=send_sems.at[j], recv_sem=recv_sems.at[j],
                                              device_id=(cx, cy, c), device_id_type=MESH)
            cp.start()
            sends.append(cp)
        for j, (cx, cy) in enumerate(chips):
            pltpu.make_async_remote_copy(src_ref=p_ref.at[me], dst_ref=out_ref.at[2 * cx + cy],
                                         send_sem=send_sems.at[j], recv_sem=recv_sems.at[j],
                                         device_id=(cx, cy, c), device_id_type=MESH).wait_recv()
        for cp in sends:
            cp.wait_send()
        mine.wait()

    return pl.pallas_call(
        body, name="scatter_grads", in_specs=[_ANY], out_specs=_ANY,
        out_shape=jax.ShapeDtypeStruct(part.shape, part.dtype),
        scratch_shapes=[pltpu.SemaphoreType.DMA((3,)), pltpu.SemaphoreType.DMA((3,)), pltpu.SemaphoreType.DMA],
    )(part)


def _pack_rows(parts, rows, dtype):
    flat = jnp.concatenate([a.reshape(-1).astype(dtype) for a in parts])
    flat = jnp.concatenate([flat, jnp.zeros((rows * LANES - flat.shape[0],), dtype)])
    return flat.reshape(rows, LANES)


_SHARD_SHAPES = [(D_MODEL, IN_TOTAL // N_CHIPS), (Q_LORA, N_HEADS * QK_DIM // N_CHIPS), (KV_LORA, N_HEADS * 256 // N_CHIPS),
                 (3, CONV_W // N_CHIPS), (D_MODEL // N_CHIPS, D_MODEL), (PLE, D_MODEL // N_CHIPS),
                 (D_MODEL // N_CHIPS, D_MODEL)]
_GAIN_SIZES = [D_MODEL, Q_LORA, KV_LORA, HEAD_PAD, HEAD_PAD, ATTN_W, CONV_W, D_MODEL, LANES]


def _unpack(flat, shapes):
    out, off = [], 0
    for shp in shapes:
        n = math.prod(shp)
        out.append(flat[off:off + n].reshape(shp))
        off += n
    return out, off


def _shard_cols(w, k, n):
    return w[:, k * n:(k + 1) * n]


def _shard_rows(w, k, n):
    return w[k * n:(k + 1) * n, :]


def _local_step(x, p, pos, tgt, gains, w_in, w_uq, w_ukv, conv_w, w_o, w_pl, w_plg, tm, tq):
    g_in, g_cq, g_ckv, g_q, g_k, g_oa, g_oc, g_pl = gains
    T = x.shape[0]
    zpad = lambda a, n: jnp.concatenate([a, jnp.zeros(a.shape[:-1] + (n,), a.dtype)], axis=-1)
    w_in_e = jnp.concatenate([w_in[:, :448], jnp.zeros((D_MODEL, 64), BF16), w_in[:, 448:]], axis=1)
    w_uq_e = zpad(w_uq.reshape(Q_LORA, N_HEADS, QK_DIM), HEAD_PAD - QK_DIM).reshape(Q_LORA, N_HEADS * HEAD_PAD)
    gq, gk = zpad(g_q, HEAD_PAD - QK_DIM), zpad(g_k, HEAD_PAD - QK_DIM)
    inv_freq = 1.0 / (ROPE_THETA ** (jnp.arange(0, ROPE, 2, dtype=F32) / ROPE))
    invf = jnp.concatenate([inv_freq, inv_freq, jnp.zeros((64,), F32)]).reshape(1, LANES)
    sgn = jnp.concatenate([-jnp.ones((32,), F32), jnp.ones((32,), F32), jnp.zeros((64,), F32)]).reshape(1, LANES)

    proj, q, k, v = _fwd_proj(x, pos, g_in, w_in_e, g_cq, w_uq_e, g_ckv, w_ukv, gq, gk, invf, sgn, tm)
    o, lse = _attn_fwd(q, k, v, tq)
    (dx1, do, delta, dtail, du, dw_o, dw_pl, dw_plg, dg_oa, dg_oc, dg_pl, dconv, loss) = _tail(
        x, o, proj, p, tgt, g_oa, g_oc, g_pl, conv_w, w_o, w_pl, w_plg, tm)
    dq = _attn_dq(q, k, v, do, lse, delta, tq)
    lse_row = lse[:, :, 0].reshape(N_HEADS, 1, T)
    delta_row = delta[:, :, 0].reshape(N_HEADS, 1, T)
    dk, dv = _attn_dkv(q, k, v, do, lse_row, delta_row, tq)
    (gx, h, dproj, dw_uq_e, dw_ukv, dg_in, dg_cq, dg_ckv, dgq, dgk) = _bwd_proj(
        x, dx1, pos, proj, dq, dk, dv, dtail, du, g_in, w_in_e, g_cq, w_uq_e, g_ckv, w_ukv, gq, gk, conv_w, invf, sgn, tm)
    dw_in_e = _matmul_tn(h, dproj, min(512, T), 512)
    dw_in = jnp.concatenate([dw_in_e[:, :448], dw_in_e[:, 512:]], axis=1)
    dw_uq = dw_uq_e.reshape(Q_LORA, N_HEADS, HEAD_PAD)[:, :, :QK_DIM].reshape(Q_LORA, N_HEADS * QK_DIM)
    wgrads = (dw_in, dw_uq, dw_ukv, dconv, dw_o, dw_pl, dw_plg)
    ggrads = (dg_in, dg_cq, dg_ckv, dgq, dgk, dg_oa, dg_oc, dg_pl)
    return loss, gx, wgrads, ggrads


def kernel(x, p, positions, g_in, w_in, g_cq, w_uq, g_ckv, w_ukv, g_q, g_k, conv_w, g_oa, g_oc, w_o, w_pl, w_plg, g_pl, loss_target, m_g_in, m_w_in, m_g_cq, m_w_uq, m_g_ckv, m_w_ukv, m_g_q, m_g_k, m_conv_w, m_g_oa, m_g_oc, m_w_o, m_w_pl, m_w_plg, m_g_pl, v_g_in, v_w_in, v_g_cq, v_w_uq, v_g_ckv, v_w_ukv, v_g_q, v_g_k, v_conv_w, v_g_oa, v_g_oc, v_w_o, v_w_pl, v_w_plg, v_g_pl):
    T = x.shape[1]
    c = lax.axis_index("c")
    shards = [w_in[0], w_uq[0], w_ukv[0], conv_w[0], w_o[0], w_pl[0], w_plg[0]]
    gains = [g_in[0:1], g_cq[0:1], g_ckv[0:1], g_q[0:1], g_k[0:1], g_oa[0:1], g_oc[0:1], g_pl[0:1]]
    gains = [g.reshape(1, -1) for g in gains]

    gathered = _gather_weights(_pack_rows(shards, PACK_ROWS, BF16))
    per_chip = [_unpack(gathered[k].reshape(-1), _SHARD_SHAPES)[0] for k in range(N_CHIPS)]
    cat = lambda i, axis: jnp.concatenate([per_chip[k][i] for k in range(N_CHIPS)], axis=axis)
    w_in_f = cat(0, 1)
    w_uq_f = cat(1, 1)
    w_ukv_f = cat(2, 1)
    conv_f = cat(3, 1).astype(F32)
    w_o_f = cat(4, 0)
    w_pl_f = cat(5, 1)
    w_plg_f = cat(6, 0)

    loss, gx, wgrads, ggrads = _local_step(
        x[0], p[0, 0], positions.reshape(T, 1), loss_target[0], gains,
        w_in_f, w_uq_f, w_ukv_f, conv_f, w_o_f, w_pl_f, w_plg_f, 256, 512)

    dw_in, dw_uq, dw_ukv, dconv, dw_o, dw_pl, dw_plg = wgrads
    small = list(ggrads) + [loss]
    packs = []
    for k in range(N_CHIPS):
        parts = [_shard_cols(dw_in, k, IN_TOTAL // N_CHIPS), _shard_cols(dw_uq, k, N_HEADS * QK_DIM // N_CHIPS),
                 _shard_cols(dw_ukv, k, 256), _shard_cols(dconv, k, CONV_W // N_CHIPS),
                 _shard_rows(dw_o, k, D_MODEL // N_CHIPS), _shard_cols(dw_pl, k, D_MODEL // N_CHIPS),
                 _shard_rows(dw_plg, k, D_MODEL // N_CHIPS)] + small
        packs.append(_pack_rows(parts, PACK_ROWS, F32))
    contrib = jnp.stack(packs).reshape(N_CHIPS, 2, HALF_ROWS, LANES)
    keep = lax.dynamic_index_in_dim(contrib, c, axis=1, keepdims=False)
    give = lax.dynamic_index_in_dim(contrib, 1 - c, axis=1, keepdims=False)
    chip_part = _add2(keep, _swap_with_sibling(give, "pair_grads"), "add_pair")
    my_half = _sum4(_scatter_to_chips(chip_part), "add_chips")
    other_half = _swap_with_sibling(my_half, "share_halves")
    lo = jnp.where(c == 0, my_half, other_half)
    hi = jnp.where(c == 0, other_half, my_half)
    total = jnp.concatenate([lo, hi], axis=0).reshape(-1)

    wg, off = _unpack(total, _SHARD_SHAPES)
    gg, _ = _unpack(total[off:], [(1, n) for n in _GAIN_SIZES])
    loss_out = gg[8][0, 0]
    gg = [gg[0], gg[1], gg[2], gg[3][:, :QK_DIM], gg[4][:, :QK_DIM], gg[5], gg[6], gg[7]]
    g_by_name = dict(g_in=gg[0], w_in=wg[0], g_cq=gg[1], w_uq=wg[1], g_ckv=gg[2], w_ukv=wg[2], g_q=gg[3], g_k=gg[4],
                     conv_w=wg[3], g_oa=gg[5], g_oc=gg[6], w_o=wg[4], w_pl=wg[5], w_plg=wg[6], g_pl=gg[7])
    weights = dict(g_in=g_in, w_in=w_in, g_cq=g_cq, w_uq=w_uq, g_ckv=g_ckv, w_ukv=w_ukv, g_q=g_q, g_k=g_k,
                   conv_w=conv_w, g_oa=g_oa, g_oc=g_oc, w_o=w_o, w_pl=w_pl, w_plg=w_plg, g_pl=g_pl)
    ms = dict(g_in=m_g_in, w_in=m_w_in, g_cq=m_g_cq, w_uq=m_w_uq, g_ckv=m_g_ckv, w_ukv=m_w_ukv, g_q=m_g_q, g_k=m_g_k,
              conv_w=m_conv_w, g_oa=m_g_oa, g_oc=m_g_oc, w_o=m_w_o, w_pl=m_w_pl, w_plg=m_w_plg, g_pl=m_g_pl)
    vs = dict(g_in=v_g_in, w_in=v_w_in, g_cq=v_g_cq, w_uq=v_w_uq, g_ckv=v_g_ckv, w_ukv=v_w_ukv, g_q=v_g_q, g_k=v_g_k,
              conv_w=v_conv_w, g_oa=v_g_oa, g_oc=v_g_oc, w_o=v_w_o, w_pl=v_w_pl, w_plg=v_w_plg, g_pl=v_g_pl)
    names = list(weights)
    grads, deltas, new_m, new_v = [], [], [], []
    for n in names:
        w = weights[n]
        w2 = w.reshape(-1, w.shape[-1])
        g2 = g_by_name[n].reshape(w2.shape)
        d, nm, nv = _adamw(w2, g2, ms[n].reshape(w2.shape), vs[n].reshape(w2.shape), "adamw_" + n)
        grads.append(g2.reshape(w.shape))
        deltas.append(d.reshape(w.shape))
        new_m.append(nm.reshape(w.shape))
        new_v.append(nv.reshape(w.shape))
    return (loss_out, gx.reshape(x.shape), *grads, *deltas, *new_m, *new_v)
```

```python
import functools
import math

import jax
import jax.numpy as jnp
from jax import lax
from jax.experimental import pallas as pl
from jax.experimental.pallas import tpu as pltpu

F32 = jnp.float32
BF16 = jnp.bfloat16

D_MODEL = 1024
N_HEADS = 4
NOPE = 128
ROPE = 64
V_DIM = 128
QK_DIM = NOPE + ROPE
HEAD_PAD = 256
Q_LORA = 256
KV_LORA = 128
ATTN_W = 512
CONV_W = 512
PLE = 256
IN_TOTAL = 3008
PROJ_EXT = 3072
ROPE_THETA = 10000.0
EPS = 1e-6
SCALE = 1.0 / math.sqrt(QK_DIM)
LOG2E = math.log2(math.e)
EXP2_SCALE = SCALE * LOG2E
NEG = -1e30
SOFTMAX_ROWS = 32

LR, B1, B2, ADAM_EPS, WD, STEP = 0.001, 0.9, 0.999, 1e-08, 0.01, 10

N_CHIPS = 4
LANES = 128
PACK_ROWS = 11520
HALF_ROWS = PACK_ROWS // 2
VMEM_LIMIT = 56 * 1024 * 1024
MESH = pl.DeviceIdType.MESH


def _params(**kw):
    return pltpu.CompilerParams(vmem_limit_bytes=VMEM_LIMIT, **kw)


def _inv_rms(x, n):
    return lax.rsqrt(jnp.sum(x * x, axis=-1, keepdims=True) / n + EPS)


def _sigmoid(z):
    return 1.0 / (1.0 + jnp.exp(-z))


def _swap_rope_halves(b):
    lane = lax.broadcasted_iota(jnp.int32, b.shape, 1)
    swapped = jnp.where(lane < 32, pltpu.roll(b, 96, 1), pltpu.roll(b, 32, 1))
    return jnp.where(lane < ROPE, swapped, 0.0)


def _dot(a, b):
    return jnp.dot(a, b, preferred_element_type=F32)


def _dot_nt(a, b):
    return lax.dot_general(a, b, (((1,), (1,)), ((), ())), preferred_element_type=F32)


def _dot_tn(a, b):
    return lax.dot_general(a, b, (((0,), (0,)), ((), ())), preferred_element_type=F32)


def _colsum(a):
    return jnp.sum(a, axis=0, keepdims=True)


def _full(shape):
    return pl.BlockSpec(shape, lambda *_: (0,) * len(shape))


def _rope_tables(pos_ref, invf_ref, sgn_ref):
    ang = pos_ref[...].astype(F32) * invf_ref[...]
    return jnp.cos(ang), jnp.sin(ang) * sgn_ref[...]


def _fwd_proj(x, pos, g_in, w_in, g_cq, w_uq, g_ckv, w_ukv, gq, gk, invf, sgn, tm):
    T = x.shape[0]

    def body(x_ref, pos_ref, g_in_ref, w_in_ref, g_cq_ref, w_uq_ref, g_ckv_ref, w_ukv_ref, gq_ref, gk_ref,
             invf_ref, sgn_ref, proj_ref, q_ref, k_ref, v_ref):
        xv = x_ref[...]
        h = (xv * _inv_rms(xv, D_MODEL) * g_in_ref[...]).astype(BF16)
        for c0 in range(0, PROJ_EXT, 512):
            proj_ref[:, c0:c0 + 512] = _dot(h, w_in_ref[:, c0:c0 + 512])
        c_q = proj_ref[:, 0:Q_LORA]
        cqn = (c_q * _inv_rms(c_q, Q_LORA) * g_cq_ref[...]).astype(BF16)
        c_kv = proj_ref[:, Q_LORA:Q_LORA + KV_LORA]
        ckvn = (c_kv * _inv_rms(c_kv, KV_LORA) * g_ckv_ref[...]).astype(BF16)
        kpe = proj_ref[:, 384:512]
        kpe_ss = jnp.sum(kpe * kpe, axis=-1, keepdims=True)
        cos_b, sin_b = _rope_tables(pos_ref, invf_ref, sgn_ref)
        gq_a, gq_b = gq_ref[:, 0:NOPE], gq_ref[:, NOPE:HEAD_PAD]
        gk_a, gk_b = gk_ref[:, 0:NOPE], gk_ref[:, NOPE:HEAD_PAD]
        for hd in range(N_HEADS):
            c0 = hd * HEAD_PAD
            qh = _dot(cqn, w_uq_ref[:, c0:c0 + HEAD_PAD])
            a, b = qh[:, 0:NOPE], qh[:, NOPE:HEAD_PAD]
            r = lax.rsqrt((jnp.sum(a * a, axis=-1, keepdims=True) + jnp.sum(b * b, axis=-1, keepdims=True)) / QK_DIM + EPS)
            bn = b * r * gq_b
            q_ref[hd, :, 0:NOPE] = (a * r * gq_a).astype(BF16)
            q_ref[hd, :, NOPE:HEAD_PAD] = (bn * cos_b + _swap_rope_halves(bn) * sin_b).astype(BF16)
            kvh = _dot(ckvn, w_ukv_ref[:, c0:c0 + HEAD_PAD])
            ka = kvh[:, 0:NOPE]
            rk = lax.rsqrt((jnp.sum(ka * ka, axis=-1, keepdims=True) + kpe_ss) / QK_DIM + EPS)
            kbn = kpe * rk * gk_b
            k_ref[hd, :, 0:NOPE] = (ka * rk * gk_a).astype(BF16)
            k_ref[hd, :, NOPE:HEAD_PAD] = (kbn * cos_b + _swap_rope_halves(kbn) * sin_b).astype(BF16)
            v_ref[hd, :, 0:V_DIM] = kvh[:, NOPE:HEAD_PAD].astype(BF16)
            v_ref[hd, :, V_DIM:2 * V_DIM] = jnp.ones((tm, V_DIM), BF16)

    row = lambda i: (i, 0)
    head_rows = lambda i: (0, i, 0)
    return pl.pallas_call(
        body, name="fwd_proj", grid=(T // tm,),
        in_specs=[pl.BlockSpec((tm, D_MODEL), row), pl.BlockSpec((tm, 1), row), _full((1, D_MODEL)),
                  _full((D_MODEL, PROJ_EXT)), _full((1, Q_LORA)), _full((Q_LORA, N_HEADS * HEAD_PAD)),
                  _full((1, KV_LORA)), _full((KV_LORA, N_HEADS * HEAD_PAD)), _full((1, HEAD_PAD)), _full((1, HEAD_PAD)),
                  _full((1, LANES)), _full((1, LANES))],
        out_specs=[pl.BlockSpec((tm, PROJ_EXT), row), pl.BlockSpec((N_HEADS, tm, HEAD_PAD), head_rows),
                   pl.BlockSpec((N_HEADS, tm, HEAD_PAD), head_rows), pl.BlockSpec((N_HEADS, tm, 2 * V_DIM), head_rows)],
        out_shape=[jax.ShapeDtypeStruct((T, PROJ_EXT), F32), jax.ShapeDtypeStruct((N_HEADS, T, HEAD_PAD), BF16),
                   jax.ShapeDtypeStruct((N_HEADS, T, HEAD_PAD), BF16), jax.ShapeDtypeStruct((N_HEADS, T, 2 * V_DIM), BF16)],
        compiler_params=_params(dimension_semantics=("arbitrary",)),
    )(x, pos, g_in, w_in, g_cq, w_uq, g_ckv, w_ukv, gq, gk, invf, sgn)


def _chunk_pipeline(n_loop, matmuls, pointwise, accumulate, last):
    def iteration(t, slot):
        matmuls(t + 1, 1 - slot)
        accumulate(jnp.maximum(t - 1, 0), 1 - slot)
        pointwise(t, slot, False)

    def finish(slot):
        accumulate(jnp.maximum(n_loop - 1, 0), 1 - slot)
        pointwise(n_loop, slot, True)
        accumulate(n_loop, slot)
        last()

    matmuls(0, 0)

    def pair(tt, carry):
        iteration(2 * tt, 0)
        iteration(2 * tt + 1, 1)
        return carry

    lax.fori_loop(0, n_loop // 2, pair, 0)
    odd = lax.rem(n_loop, 2) == 1

    @pl.when(odd)
    def _():
        iteration(n_loop - 1, 0)
        finish(1)

    @pl.when(jnp.logical_not(odd))
    def _():
        finish(0)


def _attn_fwd(q, kt, v, tq):
    T = q.shape[1]
    tk = tq
    rc = min(SOFTMAX_ROWS, tq)

    def body(q_ref, kt_ref, v_ref, o_ref, lse_ref, s0, s1, p0, p1, a0, a1, m_ref, acc_ref):
        qi = pl.program_id(1)
        s_buf, p_buf, a_buf = (s0, s1), (p0, p1), (a0, a1)

        def scores(t, slot):
            ks = pl.multiple_of(t * tk, tk)
            s_buf[slot][...] = _dot(q_ref[0], kt_ref[0, :, pl.ds(ks, tk)])

        def values(t, slot):
            ks = pl.multiple_of(t * tk, tk)
            acc_ref[...] = acc_ref[...] * a_buf[slot][...] + _dot(p_buf[slot][...], v_ref[0, pl.ds(ks, tk), :])

        def softmax(t, slot, masked):
            s_all = s_buf[slot][...]
            if masked:
                row = lax.broadcasted_iota(jnp.int32, (tq, tk), 0)
                col = lax.broadcasted_iota(jnp.int32, (tq, tk), 1)
                s_all = jnp.where(col <= row, s_all, NEG)
                s_buf[slot][...] = s_all
            m_old = m_ref[...]
            m_new = jnp.maximum(m_old, jnp.max(s_all, axis=1, keepdims=True))
            a_buf[slot][...] = jnp.exp2((m_old - m_new) * EXP2_SCALE)
            m_ref[...] = m_new
            for r0 in range(0, tq, rc):
                s = s_buf[slot][r0:r0 + rc, :]
                p_buf[slot][r0:r0 + rc, :] = jnp.exp2((s - m_new[r0:r0 + rc, :]) * EXP2_SCALE).astype(BF16)

        def last():
            l = acc_ref[:, V_DIM:2 * V_DIM]
            o_ref[...] = acc_ref[:, 0:V_DIM] / l
            lse_ref[0] = m_ref[...] * SCALE + jnp.log(l)

        m_ref[...] = jnp.full_like(m_ref, NEG)
        acc_ref[...] = jnp.zeros_like(acc_ref)
        p1[...] = jnp.zeros_like(p1)
        a1[...] = jnp.ones_like(a1)
        _chunk_pipeline(qi, scores, softmax, values, last)

    return pl.pallas_call(
        body, name="attn_fwd", grid=(N_HEADS, T // tq),
        in_specs=[pl.BlockSpec((1, tq, HEAD_PAD), lambda h, i: (h, i, 0)),
                  pl.BlockSpec((1, HEAD_PAD, T), lambda h, i: (h, 0, 0)),
                  pl.BlockSpec((1, T, 2 * V_DIM), lambda h, i: (h, 0, 0))],
        out_specs=[pl.BlockSpec((tq, V_DIM), lambda h, i: (i, h)),
                   pl.BlockSpec((1, tq, LANES), lambda h, i: (h, i, 0))],
        out_shape=[jax.ShapeDtypeStruct((T, ATTN_W), F32), jax.ShapeDtypeStruct((N_HEADS, T, LANES), F32)],
        scratch_shapes=[pltpu.VMEM((tq, tk), F32), pltpu.VMEM((tq, tk), F32), pltpu.VMEM((tq, tk), BF16),
                        pltpu.VMEM((tq, tk), BF16), pltpu.VMEM((tq, 1), F32), pltpu.VMEM((tq, 1), F32),
                        pltpu.VMEM((tq, 1), F32), pltpu.VMEM((tq, 2 * V_DIM), F32)],
        compiler_params=_params(dimension_semantics=("arbitrary", "arbitrary")),
    )(q, kt, v)


def _tail(x, o, proj, p, tgt, g_oa, g_oc, g_pl, conv_w, w_o, w_pl, w_plg, tm):
    T = x.shape[0]
    nt = T // tm

    def body(x_ref, o_ref, za_ref, cb_ref, cc_ref, cx_ref, zc_ref, cch_ref, cxh_ref, p_ref, tgt_ref,
             g_oa_ref, g_oc_ref, g_pl_ref, cw_ref, w_o_ref, w_pl_ref, w_plg_ref,
             dx1_ref, do_ref, delta_ref, dtail_ref, du_ref,
             dw_o_ref, dw_pl_ref, dw_plg_ref, dg_oa_ref, dg_oc_ref, dg_pl_ref, dcw_ref, loss_ref):
        i = pl.program_id(0)

        @pl.when(i == 0)
        def _():
            for r in (dw_o_ref, dw_pl_ref, dw_plg_ref, dg_oa_ref, dg_oc_ref, dg_pl_ref, dcw_ref, loss_ref):
                r[...] = jnp.zeros_like(r)

        xv, ov, za, cb, zc = x_ref[...], o_ref[...], za_ref[...], cb_ref[...], zc_ref[...]
        g_oa, g_oc, g_pl = g_oa_ref[...], g_oc_ref[...], g_pl_ref[...]
        w0, w1, w2 = cw_ref[0:1, :], cw_ref[1:2, :], cw_ref[2:3, :]

        sa = _sigmoid(za)
        silu_a = za * sa
        ga = ov * silu_a
        ra = _inv_rms(ga, ATTN_W)
        xa = ga * ra
        ya = xa * g_oa
        v = cc_ref[...] * cx_ref[...]
        not_first = jnp.where(i > 0, 1.0, 0.0)
        hv6 = cch_ref[6:7, :] * cxh_ref[6:7, :] * not_first
        hv7 = cch_ref[7:8, :] * cxh_ref[7:8, :] * not_first
        row = lax.broadcasted_iota(jnp.int32, v.shape, 0)
        v1 = jnp.where(row == 0, hv7, pltpu.roll(v, 1, 0))
        v2 = jnp.where(row == 0, hv6, jnp.where(row == 1, hv7, pltpu.roll(v, 2, 0)))
        u = w0 * v2 + w1 * v1 + w2 * v
        sc = _sigmoid(zc)
        silu_c = zc * sc
        gc = cb * u * silu_c
        rc = _inv_rms(gc, CONV_W)
        xc = gc * rc
        yc = xc * g_oc
        ycat = jnp.concatenate([ya, yc], axis=-1).astype(BF16)
        x1 = xv + _dot(ycat, w_o_ref[...])
        r1 = _inv_rms(x1, D_MODEL)
        xh1 = x1 * r1
        n1 = (xh1 * g_pl).astype(BF16)
        gate = _sigmoid(_dot(n1, w_plg_ref[...]))
        pb = p_ref[...].astype(BF16)
        pp = _dot(pb, w_pl_ref[...])
        err = x1 + gate * pp - tgt_ref[...]
        loss_ref[...] += 0.5 * jnp.sum(err * err) / D_MODEL
        dy = err / D_MODEL

        dpp = (dy * gate).astype(BF16)
        da = (dy * pp * gate * (1.0 - gate)).astype(BF16)
        dw_pl_ref[...] += _dot_tn(pb, dpp)
        dw_plg_ref[...] += _dot_tn(n1, da)
        dn1 = _dot_nt(da, w_plg_ref[...])
        dg_pl_ref[...] += _colsum(dn1 * xh1)
        dxh = dn1 * g_pl
        dx1 = dy + r1 * (dxh - xh1 * (jnp.sum(dxh * xh1, axis=-1, keepdims=True) / D_MODEL))
        dx1_ref[...] = dx1
        dx1b = dx1.astype(BF16)
        dw_o_ref[...] += _dot_tn(ycat, dx1b)
        dycat = _dot_nt(dx1b, w_o_ref[...])
        dya, dyc = dycat[:, 0:ATTN_W], dycat[:, ATTN_W:D_MODEL]

        dg_oa_ref[...] += _colsum(dya * xa)
        dxa = dya * g_oa
        dga = ra * (dxa - xa * (jnp.sum(dxa * xa, axis=-1, keepdims=True) / ATTN_W))
        do = (dga * silu_a).astype(BF16)
        do_ref[...] = do
        dof = do.astype(F32) * ov
        for hd in range(N_HEADS):
            dl = jnp.sum(dof[:, hd * V_DIM:(hd + 1) * V_DIM], axis=-1, keepdims=True)
            delta_ref[hd] = jnp.broadcast_to(dl, (tm, LANES))
        dtail_ref[:, 0:512] = (dga * ov * (sa * (1.0 + za * (1.0 - sa)))).astype(BF16)

        dg_oc_ref[...] += _colsum(dyc * xc)
        dxc = dyc * g_oc
        dgc = rc * (dxc - xc * (jnp.sum(dxc * xc, axis=-1, keepdims=True) / CONV_W))
        dtail_ref[:, 512:1024] = (dgc * u * silu_c).astype(BF16)
        du = dgc * cb * silu_c
        du_ref[...] = du
        dtail_ref[:, 1024:1536] = (dgc * cb * u * (sc * (1.0 + zc * (1.0 - sc)))).astype(BF16)
        dcw_ref[0:1, :] += _colsum(du * v2)
        dcw_ref[1:2, :] += _colsum(du * v1)
        dcw_ref[2:3, :] += _colsum(du * v)

    row = lambda i: (i, 0)
    col = lambda c: (lambda i: (i, c))
    halo = lambda c: (lambda i: (jnp.maximum(i * (tm // 8) - 1, 0), c))
    in_specs = [pl.BlockSpec((tm, D_MODEL), row), pl.BlockSpec((tm, ATTN_W), row)]
    in_specs += [pl.BlockSpec((tm, 512), col(c)) for c in (1, 2, 3, 4, 5)]
    in_specs += [pl.BlockSpec((8, 512), halo(3)), pl.BlockSpec((8, 512), halo(4))]
    in_specs += [pl.BlockSpec((tm, PLE), row), pl.BlockSpec((tm, D_MODEL), row),
                 _full((1, ATTN_W)), _full((1, CONV_W)), _full((1, D_MODEL)), _full((3, CONV_W)),
                 _full((D_MODEL, D_MODEL)), _full((PLE, D_MODEL)), _full((D_MODEL, D_MODEL))]
    out_specs = [pl.BlockSpec((tm, D_MODEL), row), pl.BlockSpec((tm, ATTN_W), row),
                 pl.BlockSpec((N_HEADS, tm, LANES), lambda i: (0, i, 0)), pl.BlockSpec((tm, 1536), row),
                 pl.BlockSpec((tm, CONV_W), row),
                 _full((D_MODEL, D_MODEL)), _full((PLE, D_MODEL)), _full((D_MODEL, D_MODEL)),
                 _full((1, ATTN_W)), _full((1, CONV_W)), _full((1, D_MODEL)), _full((3, CONV_W)), _full((1, LANES))]
    out_shape = [jax.ShapeDtypeStruct((T, D_MODEL), F32), jax.ShapeDtypeStruct((T, ATTN_W), BF16),
                 jax.ShapeDtypeStruct((N_HEADS, T, LANES), F32), jax.ShapeDtypeStruct((T, 1536), BF16),
                 jax.ShapeDtypeStruct((T, CONV_W), F32),
                 jax.ShapeDtypeStruct((D_MODEL, D_MODEL), F32), jax.ShapeDtypeStruct((PLE, D_MODEL), F32),
                 jax.ShapeDtypeStruct((D_MODEL, D_MODEL), F32),
                 jax.ShapeDtypeStruct((1, ATTN_W), F32), jax.ShapeDtypeStruct((1, CONV_W), F32),
                 jax.ShapeDtypeStruct((1, D_MODEL), F32), jax.ShapeDtypeStruct((3, CONV_W), F32),
                 jax.ShapeDtypeStruct((1, LANES), F32)]
    return pl.pallas_call(
        body, name="tail", grid=(nt,), in_specs=in_specs, out_specs=out_specs, out_shape=out_shape,
        compiler_params=_params(dimension_semantics=("arbitrary",)),
    )(x, o, proj, proj, proj, proj, proj, proj, proj, p, tgt, g_oa, g_oc, g_pl, conv_w, w_o, w_pl, w_plg)


def _attn_dq(q, k, kt, vt, do, lse, delta, tq):
    T = q.shape[1]
    tk = tq
    rc = min(SOFTMAX_ROWS, tq)

    def body(q_ref, k_ref, kt_ref, vt_ref, do_ref, lse_ref, dl_ref, dq_ref, s0, s1, d0, d1, g0, g1, acc_ref):
        qi = pl.program_id(1)
        s_buf, dp_buf, g_buf = (s0, s1), (d0, d1), (g0, g1)
        lse2 = lse_ref[0, :, 0:1] * LOG2E
        dl = dl_ref[0, :, 0:1]

        def matmuls(t, slot):
            ks = pl.multiple_of(t * tk, tk)
            s_buf[slot][...] = _dot(q_ref[0], kt_ref[0, :, pl.ds(ks, tk)])
            dp_buf[slot][...] = _dot(do_ref[...], vt_ref[0, :, pl.ds(ks, tk)])

        def pointwise(t, slot, masked):
            for r0 in range(0, tq, rc):
                s = s_buf[slot][r0:r0 + rc, :]
                if masked:
                    row = lax.broadcasted_iota(jnp.int32, (rc, tk), 0)
                    col = lax.broadcasted_iota(jnp.int32, (rc, tk), 1)
                    s = jnp.where(col <= row + r0, s, NEG)
                pr = jnp.exp2(s * EXP2_SCALE - lse2[r0:r0 + rc, :])
                g_buf[slot][r0:r0 + rc, :] = (pr * (dp_buf[slot][r0:r0 + rc, :] - dl[r0:r0 + rc, :]) * SCALE).astype(BF16)

        def accumulate(t, slot):
            ks = pl.multiple_of(t * tk, tk)
            acc_ref[...] += _dot(g_buf[slot][...], k_ref[0, pl.ds(ks, tk), :])

        def last():
            dq_ref[0] = acc_ref[...]

        acc_ref[...] = jnp.zeros_like(acc_ref)
        g1[...] = jnp.zeros_like(g1)
        _chunk_pipeline(qi, matmuls, pointwise, accumulate, last)

    return pl.pallas_call(
        body, name="attn_dq", grid=(N_HEADS, T // tq),
        in_specs=[pl.BlockSpec((1, tq, HEAD_PAD), lambda h, i: (h, i, 0)),
                  pl.BlockSpec((1, T, HEAD_PAD), lambda h, i: (h, 0, 0)),
                  pl.BlockSpec((1, HEAD_PAD, T), lambda h, i: (h, 0, 0)),
                  pl.BlockSpec((1, V_DIM, T), lambda h, i: (h, 0, 0)),
                  pl.BlockSpec((tq, V_DIM), lambda h, i: (i, h)),
                  pl.BlockSpec((1, tq, LANES), lambda h, i: (h, i, 0)),
                  pl.BlockSpec((1, tq, LANES), lambda h, i: (h, i, 0))],
        out_specs=pl.BlockSpec((1, tq, HEAD_PAD), lambda h, i: (h, i, 0)),
        out_shape=jax.ShapeDtypeStruct((N_HEADS, T, HEAD_PAD), F32),
        scratch_shapes=[pltpu.VMEM((tq, tk), F32)] * 4 + [pltpu.VMEM((tq, tk), BF16)] * 2
                       + [pltpu.VMEM((tq, HEAD_PAD), F32)],
        compiler_params=_params(dimension_semantics=("arbitrary", "arbitrary")),
    )(q, k, kt, vt, do, lse, delta)


def _attn_dkv(q, qt, k, v, do, dot, lse_row, delta_row, tk):
    T = q.shape[1]
    tq = tk
    nq = T // tq
    rc = min(SOFTMAX_ROWS, tk)

    def body(q_ref, qt_ref, k_ref, v_ref, do_ref, dot_ref, lse_ref, dl_ref, dk_ref, dv_ref,
             s0, s1, d0, d1, p0, p1, g0, g1, dk_acc, dv_acc):
        kj = pl.program_id(1)
        s_buf, dp_buf, p_buf, g_buf = (s0, s1), (d0, d1), (p0, p1), (g0, g1)

        def q_start(t):
            return pl.multiple_of((nq - 1 - t) * tq, tq)

        def matmuls(t, slot):
            qs = q_start(t)
            s_buf[slot][...] = _dot(k_ref[0], qt_ref[0, :, pl.ds(qs, tq)])
            dp_buf[slot][...] = _dot(v_ref[0], dot_ref[0, :, pl.ds(qs, tq)])

        def pointwise(t, slot, masked):
            qs = q_start(t)
            lse2 = lse_ref[0, :, pl.ds(qs, tq)] * LOG2E
            dl = dl_ref[0, :, pl.ds(qs, tq)]
            for r0 in range(0, tk, rc):
                st = s_buf[slot][r0:r0 + rc, :]
                if masked:
                    row = lax.broadcasted_iota(jnp.int32, (rc, tq), 0)
                    col = lax.broadcasted_iota(jnp.int32, (rc, tq), 1)
                    st = jnp.where(row + r0 <= col, st, NEG)
                pt = jnp.exp2(st * EXP2_SCALE - lse2)
                p_buf[slot][r0:r0 + rc, :] = pt.astype(BF16)
                g_buf[slot][r0:r0 + rc, :] = (pt * (dp_buf[slot][r0:r0 + rc, :] - dl) * SCALE).astype(BF16)

        def accumulate(t, slot):
            qs = q_start(t)
            dv_acc[...] += _dot(p_buf[slot][...], do_ref[pl.ds(qs, tq), :])
            dk_acc[...] += _dot(g_buf[slot][...], q_ref[0, pl.ds(qs, tq), :])

        def last():
            dk_ref[0] = dk_acc[...]
            dv_ref[0] = dv_acc[...]

        dk_acc[...] = jnp.zeros_like(dk_acc)
        dv_acc[...] = jnp.zeros_like(dv_acc)
        p1[...] = jnp.zeros_like(p1)
        g1[...] = jnp.zeros_like(g1)
        _chunk_pipeline(nq - 1 - kj, matmuls, pointwise, accumulate, last)

    return pl.pallas_call(
        body, name="attn_dkv", grid=(N_HEADS, T // tk),
        in_specs=[pl.BlockSpec((1, T, HEAD_PAD), lambda h, j: (h, 0, 0)),
                  pl.BlockSpec((1, HEAD_PAD, T), lambda h, j: (h, 0, 0)),
                  pl.BlockSpec((1, tk, HEAD_PAD), lambda h, j: (h, j, 0)),
                  pl.BlockSpec((1, tk, V_DIM), lambda h, j: (h, j, 0)),
                  pl.BlockSpec((T, V_DIM), lambda h, j: (0, h)),
                  pl.BlockSpec((1, V_DIM, T), lambda h, j: (h, 0, 0)),
                  pl.BlockSpec((1, 1, T), lambda h, j: (h, 0, 0)),
                  pl.BlockSpec((1, 1, T), lambda h, j: (h, 0, 0))],
        out_specs=[pl.BlockSpec((1, tk, HEAD_PAD), lambda h, j: (h, j, 0)),
                   pl.BlockSpec((1, tk, V_DIM), lambda h, j: (h, j, 0))],
        out_shape=[jax.ShapeDtypeStruct((N_HEADS, T, HEAD_PAD), F32), jax.ShapeDtypeStruct((N_HEADS, T, V_DIM), F32)],
        scratch_shapes=[pltpu.VMEM((tk, tq), F32)] * 4 + [pltpu.VMEM((tk, tq), BF16)] * 4
                       + [pltpu.VMEM((tk, HEAD_PAD), F32), pltpu.VMEM((tk, V_DIM), F32)],
        compiler_params=_params(dimension_semantics=("arbitrary", "arbitrary")),
    )(q, qt, k, v, do, dot, lse_row, delta_row)


def _bwd_proj(x, dx1, pos, proj, dq, dk, dv, dtail, du, g_in, w_in, g_cq, w_uq, g_ckv, w_ukv, gq, gk, conv_w,
              invf, sgn, tm):
    T = x.shape[0]
    nt = T // tm

    def body(x_ref, dx1_ref, pos_ref, lat_ref, cc_ref, cx_ref, dq_ref, dk_ref, dv_ref, dtail_ref, du_ref, dun_ref,
             g_in_ref, w_in_ref, g_cq_ref, w_uq_ref, g_ckv_ref, w_ukv_ref, gq_ref, gk_ref, cw_ref, invf_ref, sgn_ref,
             gx_ref, h_ref, dproj_ref, dw_uq_ref, dw_ukv_ref, dg_in_ref, dg_cq_ref, dg_ckv_ref, dgq_ref, dgk_ref):
        i = pl.program_id(0)

        @pl.when(i == 0)
        def _():
            for r in (dw_uq_ref, dw_ukv_ref, dg_in_ref, dg_cq_ref, dg_ckv_ref, dgq_ref, dgk_ref):
                r[...] = jnp.zeros_like(r)

        xv = x_ref[...]
        r0 = _inv_rms(xv, D_MODEL)
        xh0 = xv * r0
        g_in = g_in_ref[...]
        h_ref[...] = (xh0 * g_in).astype(BF16)

        c_q = lat_ref[:, 0:Q_LORA]
        rq = _inv_rms(c_q, Q_LORA)
        xq = c_q * rq
        g_cq = g_cq_ref[...]
        cqn = (xq * g_cq).astype(BF16)
        c_kv = lat_ref[:, Q_LORA:Q_LORA + KV_LORA]
        rkv = _inv_rms(c_kv, KV_LORA)
        xkv = c_kv * rkv
        g_ckv = g_ckv_ref[...]
        ckvn = (xkv * g_ckv).astype(BF16)
        kpe = lat_ref[:, 384:512]
        kpe_ss = jnp.sum(kpe * kpe, axis=-1, keepdims=True)
        cos_b, sin_b = _rope_tables(pos_ref, invf_ref, sgn_ref)
        gq_a, gq_b = gq_ref[:, 0:NOPE], gq_ref[:, NOPE:HEAD_PAD]
        gk_a, gk_b = gk_ref[:, 0:NOPE], gk_ref[:, NOPE:HEAD_PAD]

        dkpe = jnp.zeros((tm, LANES), F32)
        dcqn = jnp.zeros((tm, Q_LORA), F32)
        dckvn = jnp.zeros((tm, KV_LORA), F32)
        for hd in range(N_HEADS):
            c0 = hd * HEAD_PAD
            qh = _dot(cqn, w_uq_ref[:, c0:c0 + HEAD_PAD])
            a, b = qh[:, 0:NOPE], qh[:, NOPE:HEAD_PAD]
            r = lax.rsqrt((jnp.sum(a * a, axis=-1, keepdims=True) + jnp.sum(b * b, axis=-1, keepdims=True)) / QK_DIM + EPS)
            xa, xb = a * r, b * r
            dan = dq_ref[hd, :, 0:NOPE]
            dbr = dq_ref[hd, :, NOPE:HEAD_PAD]
            dbn = dbr * cos_b + _swap_rope_halves(dbr * sin_b)
            dgq_ref[:, 0:NOPE] += _colsum(dan * xa)
            dgq_ref[:, NOPE:HEAD_PAD] += _colsum(dbn * xb)
            dxa, dxb = dan * gq_a, dbn * gq_b
            cq = (jnp.sum(dxa * xa, axis=-1, keepdims=True) + jnp.sum(dxb * xb, axis=-1, keepdims=True)) / QK_DIM
            dqh = jnp.concatenate([r * (dxa - xa * cq), r * (dxb - xb * cq)], axis=-1).astype(BF16)
            dw_uq_ref[:, c0:c0 + HEAD_PAD] += _dot_tn(cqn, dqh)
            dcqn = dcqn + _dot_nt(dqh, w_uq_ref[:, c0:c0 + HEAD_PAD])
            kvh = _dot(ckvn, w_ukv_ref[:, c0:c0 + HEAD_PAD])
            ka = kvh[:, 0:NOPE]
            rk = lax.rsqrt((jnp.sum(ka * ka, axis=-1, keepdims=True) + kpe_ss) / QK_DIM + EPS)
            xka, xkb = ka * rk, kpe * rk
            dkan = dk_ref[hd, :, 0:NOPE]
            dkbr = dk_ref[hd, :, NOPE:HEAD_PAD]
            dkbn = dkbr * cos_b + _swap_rope_halves(dkbr * sin_b)
            dgk_ref[:, 0:NOPE] += _colsum(dkan * xka)
            dgk_ref[:, NOPE:HEAD_PAD] += _colsum(dkbn * xkb)
            dxka, dxkb = dkan * gk_a, dkbn * gk_b
            ck = (jnp.sum(dxka * xka, axis=-1, keepdims=True) + jnp.sum(dxkb * xkb, axis=-1, keepdims=True)) / QK_DIM
            dkpe = dkpe + rk * (dxkb - xkb * ck)
            dkvh = jnp.concatenate([rk * (dxka - xka * ck), dv_ref[hd]], axis=-1).astype(BF16)
            dw_ukv_ref[:, c0:c0 + HEAD_PAD] += _dot_tn(ckvn, dkvh)
            dckvn = dckvn + _dot_nt(dkvh, w_ukv_ref[:, c0:c0 + HEAD_PAD])

        dg_cq_ref[...] += _colsum(dcqn * xq)
        dxq = dcqn * g_cq
        dproj_ref[:, 0:Q_LORA] = (rq * (dxq - xq * (jnp.sum(dxq * xq, axis=-1, keepdims=True) / Q_LORA))).astype(BF16)
        dg_ckv_ref[...] += _colsum(dckvn * xkv)
        dxkv = dckvn * g_ckv
        dproj_ref[:, 256:384] = (rkv * (dxkv - xkv * (jnp.sum(dxkv * xkv, axis=-1, keepdims=True) / KV_LORA))).astype(BF16)
        dproj_ref[:, 384:512] = dkpe.astype(BF16)
        dproj_ref[:, 512:1536] = dtail_ref[:, 0:1024]
        dproj_ref[:, 2560:3072] = dtail_ref[:, 1024:1536]

        du_v = du_ref[...]
        not_last = jnp.where(i < nt - 1, 1.0, 0.0)
        nx0 = dun_ref[0:1, :] * not_last
        nx1 = dun_ref[1:2, :] * not_last
        row = lax.broadcasted_iota(jnp.int32, du_v.shape, 0)
        du1 = jnp.where(row == tm - 1, nx0, pltpu.roll(du_v, tm - 1, 0))
        du2 = jnp.where(row == tm - 2, nx0, jnp.where(row == tm - 1, nx1, pltpu.roll(du_v, tm - 2, 0)))
        dvc = cw_ref[2:3, :] * du_v + cw_ref[1:2, :] * du1 + cw_ref[0:1, :] * du2
        dproj_ref[:, 1536:2048] = (dvc * cx_ref[...]).astype(BF16)
        dproj_ref[:, 2048:2560] = (dvc * cc_ref[...]).astype(BF16)

        dh = jnp.zeros((tm, D_MODEL), F32)
        for c0 in range(0, PROJ_EXT, 512):
            dh = dh + _dot_nt(dproj_ref[:, c0:c0 + 512], w_in_ref[:, c0:c0 + 512])
        dg_in_ref[...] += _colsum(dh * xh0)
        dxh = dh * g_in
        gx_ref[...] = dx1_ref[...] + r0 * (dxh - xh0 * (jnp.sum(dxh * xh0, axis=-1, keepdims=True) / D_MODEL))

    row = lambda i: (i, 0)
    col = lambda c: (lambda i: (i, c))
    head_rows = lambda i: (0, i, 0)
    nxt = lambda i: (jnp.minimum((i + 1) * (tm // 8), T // 8 - 1), 0)
    in_specs = [pl.BlockSpec((tm, D_MODEL), row), pl.BlockSpec((tm, D_MODEL), row), pl.BlockSpec((tm, 1), row),
                pl.BlockSpec((tm, 512), col(0)), pl.BlockSpec((tm, 512), col(3)), pl.BlockSpec((tm, 512), col(4)),
                pl.BlockSpec((N_HEADS, tm, HEAD_PAD), head_rows), pl.BlockSpec((N_HEADS, tm, HEAD_PAD), head_rows),
                pl.BlockSpec((N_HEADS, tm, V_DIM), head_rows), pl.BlockSpec((tm, 1536), row),
                pl.BlockSpec((tm, CONV_W), row), pl.BlockSpec((8, CONV_W), nxt),
                _full((1, D_MODEL)), _full((D_MODEL, PROJ_EXT)), _full((1, Q_LORA)), _full((Q_LORA, N_HEADS * HEAD_PAD)),
                _full((1, KV_LORA)), _full((KV_LORA, N_HEADS * HEAD_PAD)), _full((1, HEAD_PAD)), _full((1, HEAD_PAD)),
                _full((3, CONV_W)), _full((1, LANES)), _full((1, LANES))]
    out_specs = [pl.BlockSpec((tm, D_MODEL), row), pl.BlockSpec((tm, D_MODEL), row), pl.BlockSpec((tm, PROJ_EXT), row),
                 _full((Q_LORA, N_HEADS * HEAD_PAD)), _full((KV_LORA, N_HEADS * HEAD_PAD)),
                 _full((1, D_MODEL)), _full((1, Q_LORA)), _full((1, KV_LORA)), _full((1, HEAD_PAD)), _full((1, HEAD_PAD))]
    out_shape = [jax.ShapeDtypeStruct((T, D_MODEL), F32), jax.ShapeDtypeStruct((T, D_MODEL), BF16),
                 jax.ShapeDtypeStruct((T, PROJ_EXT), BF16),
                 jax.ShapeDtypeStruct((Q_LORA, N_HEADS * HEAD_PAD), F32), jax.ShapeDtypeStruct((KV_LORA, N_HEADS * HEAD_PAD), F32),
                 jax.ShapeDtypeStruct((1, D_MODEL), F32), jax.ShapeDtypeStruct((1, Q_LORA), F32),
                 jax.ShapeDtypeStruct((1, KV_LORA), F32), jax.ShapeDtypeStruct((1, HEAD_PAD), F32),
                 jax.ShapeDtypeStruct((1, HEAD_PAD), F32)]
    return pl.pallas_call(
        body, name="bwd_proj", grid=(nt,), in_specs=in_specs, out_specs=out_specs, out_shape=out_shape,
        compiler_params=_params(dimension_semantics=("arbitrary",)),
    )(x, dx1, pos, proj, proj, proj, dq, dk, dv, dtail, du, du, g_in, w_in, g_cq, w_uq, g_ckv, w_ukv, gq, gk, conv_w,
      invf, sgn)


def _matmul_tn(a, b, tt, tn):
    T, M = a.shape
    N = b.shape[1]

    def body(a_ref, b_ref, o_ref):
        @pl.when(pl.program_id(1) == 0)
        def _():
            o_ref[...] = jnp.zeros_like(o_ref)

        o_ref[...] += _dot_tn(a_ref[...], b_ref[...])

    return pl.pallas_call(
        body, name="dw_in", grid=(N // tn, T // tt),
        in_specs=[pl.BlockSpec((tt, M), lambda j, t: (t, 0)), pl.BlockSpec((tt, tn), lambda j, t: (t, j))],
        out_specs=pl.BlockSpec((M, tn), lambda j, t: (0, j)),
        out_shape=jax.ShapeDtypeStruct((M, N), F32),
        compiler_params=_params(dimension_semantics=("arbitrary", "arbitrary")),
    )(a, b)


def _row_block(rows):
    for rb in (1152, 1024, 960, 768, 752, 512, 256, 128, 64, 32, 16, 8):
        if rows % rb == 0:
            return rb
    return rows


def _add2(a, b, name):
    n, rows, _ = a.shape
    rb = _row_block(rows)

    def body(a_ref, b_ref, o_ref):
        o_ref[...] = a_ref[...] + b_ref[...]

    spec = pl.BlockSpec((1, rb, LANES), lambda j, i: (j, i, 0))
    return pl.pallas_call(body, name=name, grid=(n, rows // rb), in_specs=[spec, spec], out_specs=spec,
                          out_shape=jax.ShapeDtypeStruct(a.shape, F32))(a, b)


def _sum4(a, name):
    _, rows, _ = a.shape
    rb = _row_block(rows)

    def body(a_ref, o_ref):
        o_ref[...] = ((a_ref[0] + a_ref[1]) + a_ref[2]) + a_ref[3]

    return pl.pallas_call(body, name=name, grid=(rows // rb,),
                          in_specs=[pl.BlockSpec((N_CHIPS, rb, LANES), lambda i: (0, i, 0))],
                          out_specs=pl.BlockSpec((rb, LANES), lambda i: (i, 0)),
                          out_shape=jax.ShapeDtypeStruct((rows, LANES), F32))(a)


def _adamw(w, g, m, v, name):
    rows, cols = w.shape
    rb = 256 if rows * cols > 512 * 1024 else rows

    def body(w_ref, g_ref, m_ref, v_ref, d_ref, nm_ref, nv_ref):
        gv = g_ref[...]
        nm = B1 * m_ref[...] + (1.0 - B1) * gv
        nv = B2 * v_ref[...] + (1.0 - B2) * (gv * gv)
        m_hat = nm / (1.0 - B1 ** STEP)
        v_hat = nv / (1.0 - B2 ** STEP)
        d_ref[...] = -LR * (m_hat / (jnp.sqrt(v_hat) + ADAM_EPS) + WD * w_ref[...])
        nm_ref[...] = nm
        nv_ref[...] = nv

    spec = pl.BlockSpec((rb, cols), lambda i: (i, 0))
    shp = jax.ShapeDtypeStruct(w.shape, F32)
    return pl.pallas_call(body, name=name, grid=(rows // rb,), in_specs=[spec] * 4, out_specs=[spec] * 3,
                          out_shape=[shp] * 3)(w, g, m, v)


_ANY = pl.BlockSpec(memory_space=pl.ANY)


def _mesh_pos():
    return lax.axis_index("x"), lax.axis_index("y"), lax.axis_index("c")


def _other_chips(x, y):
    return [(1 - x, y), (x, 1 - y), (1 - x, 1 - y)]


def _gather_weights(wpack):
    rows = wpack.shape[0]
    half = rows // 2

    def body(w_ref, out_ref, send_sems, recv_sems, local_sem):
        x, y, c = _mesh_pos()
        me = 2 * x + y
        chips = _other_chips(x, y)

        def part(slot, hc):
            return out_ref.at[slot, pl.ds(hc * half, half), :]

        def copy(k, src, dst, to):
            return pltpu.make_async_remote_copy(src_ref=src, dst_ref=dst, send_sem=send_sems.at[k],
                                                recv_sem=recv_sems.at[k], device_id=to, device_id_type=MESH)

        mine = pltpu.make_async_copy(w_ref, out_ref.at[me], local_sem)
        mine.start()
        my_half = w_ref.at[pl.ds(c * half, half), :]
        first = [copy(j, my_half, part(me, c), (cx, cy, c)) for j, (cx, cy) in enumerate(chips)]
        for cp in first:
            cp.start()
        passed = []
        for j, (cx, cy) in enumerate(chips):
            got = part(2 * cx + cy, c)
            copy(j, got, got, (cx, cy, c)).wait_recv()
            fwd = copy(3 + j, got, got, (x, y, 1 - c))
            fwd.start()
            passed.append(fwd)
        for j, (cx, cy) in enumerate(chips):
            got = part(2 * cx + cy, 1 - c)
            copy(3 + j, got, got, (x, y, 1 - c)).wait_recv()
        for cp in first + passed:
            cp.wait_send()
        mine.wait()

    return pl.pallas_call(
        body, name="gather_weights", in_specs=[_ANY], out_specs=_ANY,
        out_shape=jax.ShapeDtypeStruct((N_CHIPS, rows, LANES), wpack.dtype),
        scratch_shapes=[pltpu.SemaphoreType.DMA((6,)), pltpu.SemaphoreType.DMA((6,)), pltpu.SemaphoreType.DMA],
    )(wpack)


def _swap_with_sibling(a, name):
    def body(a_ref, out_ref, send_sem, recv_sem):
        x, y, c = _mesh_pos()
        cp = pltpu.make_async_remote_copy(src_ref=a_ref, dst_ref=out_ref, send_sem=send_sem, recv_sem=recv_sem,
                                          device_id=(x, y, 1 - c), device_id_type=MESH)
        cp.start()
        cp.wait()

    return pl.pallas_call(
        body, name=name, in_specs=[_ANY], out_specs=_ANY, out_shape=jax.ShapeDtypeStruct(a.shape, a.dtype),
        scratch_shapes=[pltpu.SemaphoreType.DMA, pltpu.SemaphoreType.DMA],
    )(a)


def _scatter_to_chips(part):
    _, rows, _ = part.shape

    def body(p_ref, out_ref, send_sems, recv_sems, local_sem):
        x, y, c = _mesh_pos()
        me = 2 * x + y
        chips = _other_chips(x, y)
        mine = pltpu.make_async_copy(p_ref.at[me], out_ref.at[me], local_sem)
        mine.start()
        sends = []
        for j, (cx, cy) in enumerate(chips):
            cp = pltpu.make_async_remote_copy(src_ref=p_ref.at[2 * cx + cy], dst_ref=out_ref.at[me],
                                              send_sem=send_sems.at[j], recv_sem=recv_sems.at[j],
                                              device_id=(cx, cy, c), device_id_type=MESH)
            cp.start()
            sends.append(cp)
        for j, (cx, cy) in enumerate(chips):
            pltpu.make_async_remote_copy(src_ref=p_ref.at[me], dst_ref=out_ref.at[2 * cx + cy],
                                         send_sem=send_sems.at[j], recv_sem=recv_sems.at[j],
                                         device_id=(cx, cy, c), device_id_type=MESH).wait_recv()
        for cp in sends:
            cp.wait_send()
        mine.wait()

    return pl.pallas_call(
        body, name="scatter_grads", in_specs=[_ANY], out_specs=_ANY,
        out_shape=jax.ShapeDtypeStruct(part.shape, part.dtype),
        scratch_shapes=[pltpu.SemaphoreType.DMA((3,)), pltpu.SemaphoreType.DMA((3,)), pltpu.SemaphoreType.DMA],
    )(part)


def _pack_rows(parts, rows, dtype):
    flat = jnp.concatenate([a.reshape(-1).astype(dtype) for a in parts])
    flat = jnp.concatenate([flat, jnp.zeros((rows * LANES - flat.shape[0],), dtype)])
    return flat.reshape(rows, LANES)


_SHARD_SHAPES = [(D_MODEL, IN_TOTAL // N_CHIPS), (Q_LORA, N_HEADS * QK_DIM // N_CHIPS), (KV_LORA, N_HEADS * 256 // N_CHIPS),
                 (3, CONV_W // N_CHIPS), (D_MODEL // N_CHIPS, D_MODEL), (PLE, D_MODEL // N_CHIPS),
                 (D_MODEL // N_CHIPS, D_MODEL)]
_GAIN_SIZES = [D_MODEL, Q_LORA, KV_LORA, HEAD_PAD, HEAD_PAD, ATTN_W, CONV_W, D_MODEL, LANES]


def _unpack(flat, shapes):
    out, off = [], 0
    for shp in shapes:
        n = math.prod(shp)
        out.append(flat[off:off + n].reshape(shp))
        off += n
    return out, off


def _shard_cols(w, k, n):
    return w[:, k * n:(k + 1) * n]


def _shard_rows(w, k, n):
    return w[k * n:(k + 1) * n, :]


def _local_step(x, p, pos, tgt, gains, w_in, w_uq, w_ukv, conv_w, w_o, w_pl, w_plg, tm, tq):
    g_in, g_cq, g_ckv, g_q, g_k, g_oa, g_oc, g_pl = gains
    T = x.shape[0]
    zpad = lambda a, n: jnp.concatenate([a, jnp.zeros(a.shape[:-1] + (n,), a.dtype)], axis=-1)
    w_in_e = jnp.concatenate([w_in[:, :448], jnp.zeros((D_MODEL, 64), BF16), w_in[:, 448:]], axis=1)
    w_uq_e = zpad(w_uq.reshape(Q_LORA, N_HEADS, QK_DIM), HEAD_PAD - QK_DIM).reshape(Q_LORA, N_HEADS * HEAD_PAD)
    gq, gk = zpad(g_q, HEAD_PAD - QK_DIM), zpad(g_k, HEAD_PAD - QK_DIM)
    inv_freq = 1.0 / (ROPE_THETA ** (jnp.arange(0, ROPE, 2, dtype=F32) / ROPE))
    invf = jnp.concatenate([inv_freq, inv_freq, jnp.zeros((64,), F32)]).reshape(1, LANES)
    sgn = jnp.concatenate([-jnp.ones((32,), F32), jnp.ones((32,), F32), jnp.zeros((64,), F32)]).reshape(1, LANES)

    proj, q, k, v = _fwd_proj(x, pos, g_in, w_in_e, g_cq, w_uq_e, g_ckv, w_ukv, gq, gk, invf, sgn, tm)
    qt, kt = jnp.swapaxes(q, 1, 2), jnp.swapaxes(k, 1, 2)
    vt = jnp.swapaxes(v[:, :, :V_DIM], 1, 2)
    o, lse = _attn_fwd(q, kt, v, tq)
    (dx1, do, delta, dtail, du, dw_o, dw_pl, dw_plg, dg_oa, dg_oc, dg_pl, dconv, loss) = _tail(
        x, o, proj, p, tgt, g_oa, g_oc, g_pl, conv_w, w_o, w_pl, w_plg, tm)
    dq = _attn_dq(q, k, kt, vt, do, lse, delta, tq)
    lse_row = lse[:, :, 0].reshape(N_HEADS, 1, T)
    delta_row = delta[:, :, 0].reshape(N_HEADS, 1, T)
    do_t = do.reshape(T, N_HEADS, V_DIM).transpose(1, 2, 0)
    dk, dv = _attn_dkv(q, qt, k, v, do, do_t, lse_row, delta_row, tq)
    (gx, h, dproj, dw_uq_e, dw_ukv, dg_in, dg_cq, dg_ckv, dgq, dgk) = _bwd_proj(
        x, dx1, pos, proj, dq, dk, dv, dtail, du, g_in, w_in_e, g_cq, w_uq_e, g_ckv, w_ukv, gq, gk, conv_w, invf, sgn, tm)
    dw_in_e = _matmul_tn(h, dproj, min(512, T), 512)
    dw_in = jnp.concatenate([dw_in_e[:, :448], dw_in_e[:, 512:]], axis=1)
    dw_uq = dw_uq_e.reshape(Q_LORA, N_HEADS, HEAD_PAD)[:, :, :QK_DIM].reshape(Q_LORA, N_HEADS * QK_DIM)
    wgrads = (dw_in, dw_uq, dw_ukv, dconv, dw_o, dw_pl, dw_plg)
    ggrads = (dg_in, dg_cq, dg_ckv, dgq, dgk, dg_oa, dg_oc, dg_pl)
    return loss, gx, wgrads, ggrads


def kernel(x, p, positions, g_in, w_in, g_cq, w_uq, g_ckv, w_ukv, g_q, g_k, conv_w, g_oa, g_oc, w_o, w_pl, w_plg, g_pl, loss_target, m_g_in, m_w_in, m_g_cq, m_w_uq, m_g_ckv, m_w_ukv, m_g_q, m_g_k, m_conv_w, m_g_oa, m_g_oc, m_w_o, m_w_pl, m_w_plg, m_g_pl, v_g_in, v_w_in, v_g_cq, v_w_uq, v_g_ckv, v_w_ukv, v_g_q, v_g_k, v_conv_w, v_g_oa, v_g_oc, v_w_o, v_w_pl, v_w_plg, v_g_pl):
    T = x.shape[1]
    c = lax.axis_index("c")
    shards = [w_in[0], w_uq[0], w_ukv[0], conv_w[0], w_o[0], w_pl[0], w_plg[0]]
    gains = [g_in[0:1], g_cq[0:1], g_ckv[0:1], g_q[0:1], g_k[0:1], g_oa[0:1], g_oc[0:1], g_pl[0:1]]
    gains = [g.reshape(1, -1) for g in gains]

    gathered = _gather_weights(_pack_rows(shards, PACK_ROWS, BF16))
    per_chip = [_unpack(gathered[k].reshape(-1), _SHARD_SHAPES)[0] for k in range(N_CHIPS)]
    cat = lambda i, axis: jnp.concatenate([per_chip[k][i] for k in range(N_CHIPS)], axis=axis)
    w_in_f = cat(0, 1)
    w_uq_f = cat(1, 1)
    w_ukv_f = cat(2, 1)
    conv_f = cat(3, 1).astype(F32)
    w_o_f = cat(4, 0)
    w_pl_f = cat(5, 1)
    w_plg_f = cat(6, 0)

    loss, gx, wgrads, ggrads = _local_step(
        x[0], p[0, 0], positions.reshape(T, 1), loss_target[0], gains,
        w_in_f, w_uq_f, w_ukv_f, conv_f, w_o_f, w_pl_f, w_plg_f, 256, 512)

    dw_in, dw_uq, dw_ukv, dconv, dw_o, dw_pl, dw_plg = wgrads
    small = list(ggrads) + [loss]
    packs = []
    for k in range(N_CHIPS):
        parts = [_shard_cols(dw_in, k, IN_TOTAL // N_CHIPS), _shard_cols(dw_uq, k, N_HEADS * QK_DIM // N_CHIPS),
                 _shard_cols(dw_ukv, k, 256), _shard_cols(dconv, k, CONV_W // N_CHIPS),
                 _shard_rows(dw_o, k, D_MODEL // N_CHIPS), _shard_cols(dw_pl, k, D_MODEL // N_CHIPS),
                 _shard_rows(dw_plg, k, D_MODEL // N_CHIPS)] + small
        packs.append(_pack_rows(parts, PACK_ROWS, F32))
    contrib = jnp.stack(packs).reshape(N_CHIPS, 2, HALF_ROWS, LANES)
    keep = lax.dynamic_index_in_dim(contrib, c, axis=1, keepdims=False)
    give = lax.dynamic_index_in_dim(contrib, 1 - c, axis=1, keepdims=False)
    chip_part = _add2(keep, _swap_with_sibling(give, "pair_grads"), "add_pair")
    my_half = _sum4(_scatter_to_chips(chip_part), "add_chips")
    other_half = _swap_with_sibling(my_half, "share_halves")
    lo = jnp.where(c == 0, my_half, other_half)
    hi = jnp.where(c == 0, other_half, my_half)
    total = jnp.concatenate([lo, hi], axis=0).reshape(-1)

    wg, off = _unpack(total, _SHARD_SHAPES)
    gg, _ = _unpack(total[off:], [(1, n) for n in _GAIN_SIZES])
    loss_out = gg[8][0, 0]
    gg = [gg[0], gg[1], gg[2], gg[3][:, :QK_DIM], gg[4][:, :QK_DIM], gg[5], gg[6], gg[7]]
    g_by_name = dict(g_in=gg[0], w_in=wg[0], g_cq=gg[1], w_uq=wg[1], g_ckv=gg[2], w_ukv=wg[2], g_q=gg[3], g_k=gg[4],
                     conv_w=wg[3], g_oa=gg[5], g_oc=gg[6], w_o=wg[4], w_pl=wg[5], w_plg=wg[6], g_pl=gg[7])
    weights = dict(g_in=g_in, w_in=w_in, g_cq=g_cq, w_uq=w_uq, g_ckv=g_ckv, w_ukv=w_ukv, g_q=g_q, g_k=g_k,
                   conv_w=conv_w, g_oa=g_oa, g_oc=g_oc, w_o=w_o, w_pl=w_pl, w_plg=w_plg, g_pl=g_pl)
    ms = dict(g_in=m_g_in, w_in=m_w_in, g_cq=m_g_cq, w_uq=m_w_uq, g_ckv=m_g_ckv, w_ukv=m_w_ukv, g_q=m_g_q, g_k=m_g_k,
              conv_w=m_conv_w, g_oa=m_g_oa, g_oc=m_g_oc, w_o=m_w_o, w_pl=m_w_pl, w_plg=m_w_plg, g_pl=m_g_pl)
    vs = dict(g_in=v_g_in, w_in=v_w_in, g_cq=v_g_cq, w_uq=v_w_uq, g_ckv=v_g_ckv, w_ukv=v_w_ukv, g_q=v_g_q, g_k=v_g_k,
              conv_w=v_conv_w, g_oa=v_g_oa, g_oc=v_g_oc, w_o=v_w_o, w_pl=v_w_pl, w_plg=v_w_plg, g_pl=v_g_pl)
    names = list(weights)
    grads, deltas, new_m, new_v = [], [], [], []
    for n in names:
        w = weights[n]
        w2 = w.reshape(-1, w.shape[-1])
        g2 = g_by_name[n].reshape(w2.shape)
        d, nm, nv = _adamw(w2, g2, ms[n].reshape(w2.shape), vs[n].reshape(w2.shape), "adamw_" + n)
        grads.append(g2.reshape(w.shape))
        deltas.append(d.reshape(w.shape))
        new_m.append(nm.reshape(w.shape))
        new_v.append(nv.reshape(w.shape))
    return (loss_out, gx.reshape(x.shape), *grads, *deltas, *new_m, *new_v)
```

```python
import functools
import math

import jax
import jax.numpy as jnp
from jax import lax
from jax.experimental import pallas as pl
from jax.experimental.pallas import tpu as pltpu

F32 = jnp.float32
BF16 = jnp.bfloat16

D_MODEL = 1024
N_HEADS = 4
NOPE = 128
ROPE = 64
V_DIM = 128
QK_DIM = NOPE + ROPE
HEAD_PAD = 256
Q_LORA = 256
KV_LORA = 128
ATTN_W = 512
CONV_W = 512
PLE = 256
IN_TOTAL = 3008
PROJ_EXT = 3072
ROPE_THETA = 10000.0
EPS = 1e-6
SCALE = 1.0 / math.sqrt(QK_DIM)
LOG2E = math.log2(math.e)
EXP2_SCALE = SCALE * LOG2E
NEG = -1e30
SOFTMAX_ROWS = 32

LR, B1, B2, ADAM_EPS, WD, STEP = 0.001, 0.9, 0.999, 1e-08, 0.01, 10

N_CHIPS = 4
LANES = 128
SMALL_ROWS = 48
VMEM_LIMIT = 56 * 1024 * 1024
MESH = pl.DeviceIdType.MESH


def _params(**kw):
    return pltpu.CompilerParams(vmem_limit_bytes=VMEM_LIMIT, **kw)


def _inv_rms(x, n):
    return lax.rsqrt(jnp.sum(x * x, axis=-1, keepdims=True) / n + EPS)


def _sigmoid(z):
    return 1.0 / (1.0 + jnp.exp(-z))


def _swap_rope_halves(b):
    lane = lax.broadcasted_iota(jnp.int32, b.shape, 1)
    swapped = jnp.where(lane < 32, pltpu.roll(b, 96, 1), pltpu.roll(b, 32, 1))
    return jnp.where(lane < ROPE, swapped, 0.0)


def _dot(a, b):
    return jnp.dot(a, b, preferred_element_type=F32)


def _dot_nt(a, b):
    return lax.dot_general(a, b, (((1,), (1,)), ((), ())), preferred_element_type=F32)


def _dot_tn(a, b):
    return lax.dot_general(a, b, (((0,), (0,)), ((), ())), preferred_element_type=F32)


def _colsum(a):
    return jnp.sum(a, axis=0, keepdims=True)


def _full(shape):
    return pl.BlockSpec(shape, lambda *_: (0,) * len(shape))


def _rope_tables(pos_ref, invf_ref, sgn_ref):
    ang = pos_ref[...].astype(F32) * invf_ref[...]
    return jnp.cos(ang), jnp.sin(ang) * sgn_ref[...]


def _fwd_proj(x, pos, g_in, w_in, g_cq, w_uq, g_ckv, w_ukv, gq, gk, invf, sgn, tm):
    T = x.shape[0]

    def body(x_ref, pos_ref, g_in_ref, w_in_ref, g_cq_ref, w_uq_ref, g_ckv_ref, w_ukv_ref, gq_ref, gk_ref,
             invf_ref, sgn_ref, proj_ref, q_ref, k_ref, v_ref):
        xv = x_ref[...]
        h = (xv * _inv_rms(xv, D_MODEL) * g_in_ref[...]).astype(BF16)
        for c0 in range(0, PROJ_EXT, 512):
            proj_ref[:, c0:c0 + 512] = _dot(h, w_in_ref[:, c0:c0 + 512])
        c_q = proj_ref[:, 0:Q_LORA]
        cqn = (c_q * _inv_rms(c_q, Q_LORA) * g_cq_ref[...]).astype(BF16)
        c_kv = proj_ref[:, Q_LORA:Q_LORA + KV_LORA]
        ckvn = (c_kv * _inv_rms(c_kv, KV_LORA) * g_ckv_ref[...]).astype(BF16)
        kpe = proj_ref[:, 384:512]
        kpe_ss = jnp.sum(kpe * kpe, axis=-1, keepdims=True)
        cos_b, sin_b = _rope_tables(pos_ref, invf_ref, sgn_ref)
        gq_a, gq_b = gq_ref[:, 0:NOPE], gq_ref[:, NOPE:HEAD_PAD]
        gk_a, gk_b = gk_ref[:, 0:NOPE], gk_ref[:, NOPE:HEAD_PAD]
        for hd in range(N_HEADS):
            c0 = hd * HEAD_PAD
            qh = _dot(cqn, w_uq_ref[:, c0:c0 + HEAD_PAD])
            a, b = qh[:, 0:NOPE], qh[:, NOPE:HEAD_PAD]
            r = lax.rsqrt((jnp.sum(a * a, axis=-1, keepdims=True) + jnp.sum(b * b, axis=-1, keepdims=True)) / QK_DIM + EPS)
            bn = b * r * gq_b
            q_ref[hd, :, 0:NOPE] = (a * r * gq_a).astype(BF16)
            q_ref[hd, :, NOPE:HEAD_PAD] = (bn * cos_b + _swap_rope_halves(bn) * sin_b).astype(BF16)
            kvh = _dot(ckvn, w_ukv_ref[:, c0:c0 + HEAD_PAD])
            ka = kvh[:, 0:NOPE]
            rk = lax.rsqrt((jnp.sum(ka * ka, axis=-1, keepdims=True) + kpe_ss) / QK_DIM + EPS)
            kbn = kpe * rk * gk_b
            k_ref[hd, :, 0:NOPE] = (ka * rk * gk_a).astype(BF16)
            k_ref[hd, :, NOPE:HEAD_PAD] = (kbn * cos_b + _swap_rope_halves(kbn) * sin_b).astype(BF16)
            v_ref[hd, :, 0:V_DIM] = kvh[:, NOPE:HEAD_PAD].astype(BF16)
            v_ref[hd, :, V_DIM:2 * V_DIM] = jnp.ones((tm, V_DIM), BF16)

    row = lambda i: (i, 0)
    head_rows = lambda i: (0, i, 0)
    return pl.pallas_call(
        body, name="fwd_proj", grid=(T // tm,),
        in_specs=[pl.BlockSpec((tm, D_MODEL), row), pl.BlockSpec((tm, 1), row), _full((1, D_MODEL)),
                  _full((D_MODEL, PROJ_EXT)), _full((1, Q_LORA)), _full((Q_LORA, N_HEADS * HEAD_PAD)),
                  _full((1, KV_LORA)), _full((KV_LORA, N_HEADS * HEAD_PAD)), _full((1, HEAD_PAD)), _full((1, HEAD_PAD)),
                  _full((1, LANES)), _full((1, LANES))],
        out_specs=[pl.BlockSpec((tm, PROJ_EXT), row), pl.BlockSpec((N_HEADS, tm, HEAD_PAD), head_rows),
                   pl.BlockSpec((N_HEADS, tm, HEAD_PAD), head_rows), pl.BlockSpec((N_HEADS, tm, 2 * V_DIM), head_rows)],
        out_shape=[jax.ShapeDtypeStruct((T, PROJ_EXT), F32), jax.ShapeDtypeStruct((N_HEADS, T, HEAD_PAD), BF16),
                   jax.ShapeDtypeStruct((N_HEADS, T, HEAD_PAD), BF16), jax.ShapeDtypeStruct((N_HEADS, T, 2 * V_DIM), BF16)],
        compiler_params=_params(dimension_semantics=("arbitrary",)),
    )(x, pos, g_in, w_in, g_cq, w_uq, g_ckv, w_ukv, gq, gk, invf, sgn)


def _chunk_pipeline(n_loop, matmuls, pointwise, accumulate, last):
    def iteration(t, slot):
        matmuls(t + 1, 1 - slot)
        accumulate(jnp.maximum(t - 1, 0), 1 - slot)
        pointwise(t, slot, False)

    def finish(slot):
        accumulate(jnp.maximum(n_loop - 1, 0), 1 - slot)
        pointwise(n_loop, slot, True)
        accumulate(n_loop, slot)
        last()

    matmuls(0, 0)

    def pair(tt, carry):
        iteration(2 * tt, 0)
        iteration(2 * tt + 1, 1)
        return carry

    lax.fori_loop(0, n_loop // 2, pair, 0)
    odd = lax.rem(n_loop, 2) == 1

    @pl.when(odd)
    def _():
        iteration(n_loop - 1, 0)
        finish(1)

    @pl.when(jnp.logical_not(odd))
    def _():
        finish(0)


def _attn_fwd(q, k, v, tq):
    T = q.shape[1]
    tk = tq
    rc = min(SOFTMAX_ROWS, tq)

    def body(q_ref, k_ref, v_ref, o_ref, lse_ref, s0, s1, p0, p1, a0, a1, m_ref, acc_ref):
        qi = pl.program_id(1)
        s_buf, p_buf, a_buf = (s0, s1), (p0, p1), (a0, a1)

        def scores(t, slot):
            ks = pl.multiple_of(t * tk, tk)
            s_buf[slot][...] = _dot_nt(q_ref[0], k_ref[0, pl.ds(ks, tk), :])

        def values(t, slot):
            ks = pl.multiple_of(t * tk, tk)
            acc_ref[...] = acc_ref[...] * a_buf[slot][...] + _dot(p_buf[slot][...], v_ref[0, pl.ds(ks, tk), :])

        def softmax(t, slot, masked):
            s_all = s_buf[slot][...]
            if masked:
                row = lax.broadcasted_iota(jnp.int32, (tq, tk), 0)
                col = lax.broadcasted_iota(jnp.int32, (tq, tk), 1)
                s_all = jnp.where(col <= row, s_all, NEG)
                s_buf[slot][...] = s_all
            m_old = m_ref[...]
            m_new = jnp.maximum(m_old, jnp.max(s_all, axis=1, keepdims=True))
            a_buf[slot][...] = jnp.exp2((m_old - m_new) * EXP2_SCALE)
            m_ref[...] = m_new
            for r0 in range(0, tq, rc):
                s = s_buf[slot][r0:r0 + rc, :]
                p_buf[slot][r0:r0 + rc, :] = jnp.exp2((s - m_new[r0:r0 + rc, :]) * EXP2_SCALE).astype(BF16)

        def last():
            l = acc_ref[:, V_DIM:2 * V_DIM]
            o_ref[...] = acc_ref[:, 0:V_DIM] / l
            lse_ref[0] = m_ref[...] * SCALE + jnp.log(l)

        m_ref[...] = jnp.full_like(m_ref, NEG)
        acc_ref[...] = jnp.zeros_like(acc_ref)
        p1[...] = jnp.zeros_like(p1)
        a1[...] = jnp.ones_like(a1)
        _chunk_pipeline(qi, scores, softmax, values, last)

    return pl.pallas_call(
        body, name="attn_fwd", grid=(N_HEADS, T // tq),
        in_specs=[pl.BlockSpec((1, tq, HEAD_PAD), lambda h, i: (h, i, 0)),
                  pl.BlockSpec((1, T, HEAD_PAD), lambda h, i: (h, 0, 0)),
                  pl.BlockSpec((1, T, 2 * V_DIM), lambda h, i: (h, 0, 0))],
        out_specs=[pl.BlockSpec((tq, V_DIM), lambda h, i: (i, h)),
                   pl.BlockSpec((1, tq, LANES), lambda h, i: (h, i, 0))],
        out_shape=[jax.ShapeDtypeStruct((T, ATTN_W), F32), jax.ShapeDtypeStruct((N_HEADS, T, LANES), F32)],
        scratch_shapes=[pltpu.VMEM((tq, tk), F32), pltpu.VMEM((tq, tk), F32), pltpu.VMEM((tq, tk), BF16),
                        pltpu.VMEM((tq, tk), BF16), pltpu.VMEM((tq, 1), F32), pltpu.VMEM((tq, 1), F32),
                        pltpu.VMEM((tq, 1), F32), pltpu.VMEM((tq, 2 * V_DIM), F32)],
        compiler_params=_params(dimension_semantics=("arbitrary", "arbitrary")),
    )(q, k, v)


def _tail(x, o, proj, p, tgt, g_oa, g_oc, g_pl, conv_w, w_o, w_pl, w_plg, tm):
    T = x.shape[0]
    nt = T // tm

    def body(x_ref, o_ref, za_ref, cb_ref, cc_ref, cx_ref, zc_ref, cch_ref, cxh_ref, p_ref, tgt_ref,
             g_oa_ref, g_oc_ref, g_pl_ref, cw_ref, w_o_ref, w_pl_ref, w_plg_ref,
             dx1_ref, do_ref, delta_ref, dtail_ref, du_ref,
             dw_o_ref, dw_pl_ref, dw_plg_ref, dg_oa_ref, dg_oc_ref, dg_pl_ref, dcw_ref, loss_ref):
        i = pl.program_id(0)

        @pl.when(i == 0)
        def _():
            for r in (dw_o_ref, dw_pl_ref, dw_plg_ref, dg_oa_ref, dg_oc_ref, dg_pl_ref, dcw_ref, loss_ref):
                r[...] = jnp.zeros_like(r)

        xv, ov, za, cb, zc = x_ref[...], o_ref[...], za_ref[...], cb_ref[...], zc_ref[...]
        g_oa, g_oc, g_pl = g_oa_ref[...], g_oc_ref[...], g_pl_ref[...]
        w0, w1, w2 = cw_ref[0:1, :], cw_ref[1:2, :], cw_ref[2:3, :]

        sa = _sigmoid(za)
        silu_a = za * sa
        ga = ov * silu_a
        ra = _inv_rms(ga, ATTN_W)
        xa = ga * ra
        ya = xa * g_oa
        v = cc_ref[...] * cx_ref[...]
        not_first = jnp.where(i > 0, 1.0, 0.0)
        hv6 = cch_ref[6:7, :] * cxh_ref[6:7, :] * not_first
        hv7 = cch_ref[7:8, :] * cxh_ref[7:8, :] * not_first
        row = lax.broadcasted_iota(jnp.int32, v.shape, 0)
        v1 = jnp.where(row == 0, hv7, pltpu.roll(v, 1, 0))
        v2 = jnp.where(row == 0, hv6, jnp.where(row == 1, hv7, pltpu.roll(v, 2, 0)))
        u = w0 * v2 + w1 * v1 + w2 * v
        sc = _sigmoid(zc)
        silu_c = zc * sc
        gc = cb * u * silu_c
        rc = _inv_rms(gc, CONV_W)
        xc = gc * rc
        yc = xc * g_oc
        ycat = jnp.concatenate([ya, yc], axis=-1).astype(BF16)
        x1 = xv + _dot(ycat, w_o_ref[...])
        r1 = _inv_rms(x1, D_MODEL)
        xh1 = x1 * r1
        n1 = (xh1 * g_pl).astype(BF16)
        gate = _sigmoid(_dot(n1, w_plg_ref[...]))
        pb = p_ref[...].astype(BF16)
        pp = _dot(pb, w_pl_ref[...])
        err = x1 + gate * pp - tgt_ref[...]
        loss_ref[...] += 0.5 * jnp.sum(err * err) / D_MODEL
        dy = err / D_MODEL

        dpp = (dy * gate).astype(BF16)
        da = (dy * pp * gate * (1.0 - gate)).astype(BF16)
        dw_pl_ref[...] += _dot_tn(pb, dpp)
        dw_plg_ref[...] += _dot_tn(n1, da)
        dn1 = _dot_nt(da, w_plg_ref[...])
        dg_pl_ref[...] += _colsum(dn1 * xh1)
        dxh = dn1 * g_pl
        dx1 = dy + r1 * (dxh - xh1 * (jnp.sum(dxh * xh1, axis=-1, keepdims=True) / D_MODEL))
        dx1_ref[...] = dx1
        dx1b = dx1.astype(BF16)
        dw_o_ref[...] += _dot_tn(ycat, dx1b)
        dycat = _dot_nt(dx1b, w_o_ref[...])
        dya, dyc = dycat[:, 0:ATTN_W], dycat[:, ATTN_W:D_MODEL]

        dg_oa_ref[...] += _colsum(dya * xa)
        dxa = dya * g_oa
        dga = ra * (dxa - xa * (jnp.sum(dxa * xa, axis=-1, keepdims=True) / ATTN_W))
        do = (dga * silu_a).astype(BF16)
        do_ref[...] = do
        dof = do.astype(F32) * ov
        for hd in range(N_HEADS):
            dl = jnp.sum(dof[:, hd * V_DIM:(hd + 1) * V_DIM], axis=-1, keepdims=True)
            delta_ref[hd] = jnp.broadcast_to(dl, (tm, LANES))
        dtail_ref[:, 0:512] = (dga * ov * (sa * (1.0 + za * (1.0 - sa)))).astype(BF16)

        dg_oc_ref[...] += _colsum(dyc * xc)
        dxc = dyc * g_oc
        dgc = rc * (dxc - xc * (jnp.sum(dxc * xc, axis=-1, keepdims=True) / CONV_W))
        dtail_ref[:, 512:1024] = (dgc * u * silu_c).astype(BF16)
        du = dgc * cb * silu_c
        du_ref[...] = du
        dtail_ref[:, 1024:1536] = (dgc * cb * u * (sc * (1.0 + zc * (1.0 - sc)))).astype(BF16)
        dcw_ref[0:1, :] += _colsum(du * v2)
        dcw_ref[1:2, :] += _colsum(du * v1)
        dcw_ref[2:3, :] += _colsum(du * v)

    row = lambda i: (i, 0)
    col = lambda c: (lambda i: (i, c))
    halo = lambda c: (lambda i: (jnp.maximum(i * (tm // 8) - 1, 0), c))
    in_specs = [pl.BlockSpec((tm, D_MODEL), row), pl.BlockSpec((tm, ATTN_W), row)]
    in_specs += [pl.BlockSpec((tm, 512), col(c)) for c in (1, 2, 3, 4, 5)]
    in_specs += [pl.BlockSpec((8, 512), halo(3)), pl.BlockSpec((8, 512), halo(4))]
    in_specs += [pl.BlockSpec((tm, PLE), row), pl.BlockSpec((tm, D_MODEL), row),
                 _full((1, ATTN_W)), _full((1, CONV_W)), _full((1, D_MODEL)), _full((3, CONV_W)),
                 _full((D_MODEL, D_MODEL)), _full((PLE, D_MODEL)), _full((D_MODEL, D_MODEL))]
    out_specs = [pl.BlockSpec((tm, D_MODEL), row), pl.BlockSpec((tm, ATTN_W), row),
                 pl.BlockSpec((N_HEADS, tm, LANES), lambda i: (0, i, 0)), pl.BlockSpec((tm, 1536), row),
                 pl.BlockSpec((tm, CONV_W), row),
                 _full((D_MODEL, D_MODEL)), _full((PLE, D_MODEL)), _full((D_MODEL, D_MODEL)),
                 _full((1, ATTN_W)), _full((1, CONV_W)), _full((1, D_MODEL)), _full((3, CONV_W)), _full((1, LANES))]
    out_shape = [jax.ShapeDtypeStruct((T, D_MODEL), F32), jax.ShapeDtypeStruct((T, ATTN_W), BF16),
                 jax.ShapeDtypeStruct((N_HEADS, T, LANES), F32), jax.ShapeDtypeStruct((T, 1536), BF16),
                 jax.ShapeDtypeStruct((T, CONV_W), F32),
                 jax.ShapeDtypeStruct((D_MODEL, D_MODEL), F32), jax.ShapeDtypeStruct((PLE, D_MODEL), F32),
                 jax.ShapeDtypeStruct((D_MODEL, D_MODEL), F32),
                 jax.ShapeDtypeStruct((1, ATTN_W), F32), jax.ShapeDtypeStruct((1, CONV_W), F32),
                 jax.ShapeDtypeStruct((1, D_MODEL), F32), jax.ShapeDtypeStruct((3, CONV_W), F32),
                 jax.ShapeDtypeStruct((1, LANES), F32)]
    return pl.pallas_call(
        body, name="tail", grid=(nt,), in_specs=in_specs, out_specs=out_specs, out_shape=out_shape,
        compiler_params=_params(dimension_semantics=("arbitrary",)),
    )(x, o, proj, proj, proj, proj, proj, proj, proj, p, tgt, g_oa, g_oc, g_pl, conv_w, w_o, w_pl, w_plg)


def _attn_dq(q, k, v, do, lse, delta, tq):
    T = q.shape[1]
    tk = tq
    rc = min(SOFTMAX_ROWS, tq)

    def body(q_ref, k_ref, v_ref, do_ref, lse_ref, dl_ref, dq_ref, s0, s1, d0, d1, g0, g1, acc_ref):
        qi = pl.program_id(1)
        s_buf, dp_buf, g_buf = (s0, s1), (d0, d1), (g0, g1)
        lse2 = lse_ref[0, :, 0:1] * LOG2E
        dl = dl_ref[0, :, 0:1]

        def matmuls(t, slot):
            ks = pl.multiple_of(t * tk, tk)
            s_buf[slot][...] = _dot_nt(q_ref[0], k_ref[0, pl.ds(ks, tk), :])
            dp_buf[slot][...] = _dot_nt(do_ref[...], v_ref[0, pl.ds(ks, tk), :])

        def pointwise(t, slot, masked):
            for r0 in range(0, tq, rc):
                s = s_buf[slot][r0:r0 + rc, :]
                if masked:
                    row = lax.broadcasted_iota(jnp.int32, (rc, tk), 0)
                    col = lax.broadcasted_iota(jnp.int32, (rc, tk), 1)
                    s = jnp.where(col <= row + r0, s, NEG)
                pr = jnp.exp2(s * EXP2_SCALE - lse2[r0:r0 + rc, :])
                g_buf[slot][r0:r0 + rc, :] = (pr * (dp_buf[slot][r0:r0 + rc, :] - dl[r0:r0 + rc, :]) * SCALE).astype(BF16)

        def accumulate(t, slot):
            ks = pl.multiple_of(t * tk, tk)
            acc_ref[...] += _dot(g_buf[slot][...], k_ref[0, pl.ds(ks, tk), :])

        def last():
            dq_ref[0] = acc_ref[...]

        acc_ref[...] = jnp.zeros_like(acc_ref)
        g1[...] = jnp.zeros_like(g1)
        _chunk_pipeline(qi, matmuls, pointwise, accumulate, last)

    return pl.pallas_call(
        body, name="attn_dq", grid=(N_HEADS, T // tq),
        in_specs=[pl.BlockSpec((1, tq, HEAD_PAD), lambda h, i: (h, i, 0)),
                  pl.BlockSpec((1, T, HEAD_PAD), lambda h, i: (h, 0, 0)),
                  pl.BlockSpec((1, T, V_DIM), lambda h, i: (h, 0, 0)),
                  pl.BlockSpec((tq, V_DIM), lambda h, i: (i, h)),
                  pl.BlockSpec((1, tq, LANES), lambda h, i: (h, i, 0)),
                  pl.BlockSpec((1, tq, LANES), lambda h, i: (h, i, 0))],
        out_specs=pl.BlockSpec((1, tq, HEAD_PAD), lambda h, i: (h, i, 0)),
        out_shape=jax.ShapeDtypeStruct((N_HEADS, T, HEAD_PAD), F32),
        scratch_shapes=[pltpu.VMEM((tq, tk), F32)] * 4 + [pltpu.VMEM((tq, tk), BF16)] * 2
                       + [pltpu.VMEM((tq, HEAD_PAD), F32)],
        compiler_params=_params(dimension_semantics=("arbitrary", "arbitrary")),
    )(q, k, v, do, lse, delta)


def _attn_dkv(q, k, v, do, lse_row, delta_row, tk):
    T = q.shape[1]
    tq = tk
    nq = T // tq
    rc = min(SOFTMAX_ROWS, tk)

    def body(q_ref, k_ref, v_ref, do_ref, lse_ref, dl_ref, dk_ref, dv_ref,
             s0, s1, d0, d1, p0, p1, g0, g1, dk_acc, dv_acc):
        kj = pl.program_id(1)
        s_buf, dp_buf, p_buf, g_buf = (s0, s1), (d0, d1), (p0, p1), (g0, g1)

        def q_start(t):
            return pl.multiple_of((nq - 1 - t) * tq, tq)

        def matmuls(t, slot):
            qs = q_start(t)
            s_buf[slot][...] = _dot_nt(k_ref[0], q_ref[0, pl.ds(qs, tq), :])
            dp_buf[slot][...] = _dot_nt(v_ref[0], do_ref[pl.ds(qs, tq), :])

        def pointwise(t, slot, masked):
            qs = q_start(t)
            lse2 = lse_ref[0, :, pl.ds(qs, tq)] * LOG2E
            dl = dl_ref[0, :, pl.ds(qs, tq)]
            for r0 in range(0, tk, rc):
                st = s_buf[slot][r0:r0 + rc, :]
                if masked:
                    row = lax.broadcasted_iota(jnp.int32, (rc, tq), 0)
                    col = lax.broadcasted_iota(jnp.int32, (rc, tq), 1)
                    st = jnp.where(row + r0 <= col, st, NEG)
                pt = jnp.exp2(st * EXP2_SCALE - lse2)
                p_buf[slot][r0:r0 + rc, :] = pt.astype(BF16)
                g_buf[slot][r0:r0 + rc, :] = (pt * (dp_buf[slot][r0:r0 + rc, :] - dl) * SCALE).astype(BF16)

        def accumulate(t, slot):
            qs = q_start(t)
            dv_acc[...] += _dot(p_buf[slot][...], do_ref[pl.ds(qs, tq), :])
            dk_acc[...] += _dot(g_buf[slot][...], q_ref[0, pl.ds(qs, tq), :])

        def last():
            dk_ref[0] = dk_acc[...]
            dv_ref[0] = dv_acc[...]

        dk_acc[...] = jnp.zeros_like(dk_acc)
        dv_acc[...] = jnp.zeros_like(dv_acc)
        p1[...] = jnp.zeros_like(p1)
        g1[...] = jnp.zeros_like(g1)
        _chunk_pipeline(nq - 1 - kj, matmuls, pointwise, accumulate, last)

    return pl.pallas_call(
        body, name="attn_dkv", grid=(N_HEADS, T // tk),
        in_specs=[pl.BlockSpec((1, T, HEAD_PAD), lambda h, j: (h, 0, 0)),
                  pl.BlockSpec((1, tk, HEAD_PAD), lambda h, j: (h, j, 0)),
                  pl.BlockSpec((1, tk, V_DIM), lambda h, j: (h, j, 0)),
                  pl.BlockSpec((T, V_DIM), lambda h, j: (0, h)),
                  pl.BlockSpec((1, 1, T), lambda h, j: (h, 0, 0)),
                  pl.BlockSpec((1, 1, T), lambda h, j: (h, 0, 0))],
        out_specs=[pl.BlockSpec((1, tk, HEAD_PAD), lambda h, j: (h, j, 0)),
                   pl.BlockSpec((1, tk, V_DIM), lambda h, j: (h, j, 0))],
        out_shape=[jax.ShapeDtypeStruct((N_HEADS, T, HEAD_PAD), F32), jax.ShapeDtypeStruct((N_HEADS, T, V_DIM), F32)],
        scratch_shapes=[pltpu.VMEM((tk, tq), F32)] * 4 + [pltpu.VMEM((tk, tq), BF16)] * 4
                       + [pltpu.VMEM((tk, HEAD_PAD), F32), pltpu.VMEM((tk, V_DIM), F32)],
        compiler_params=_params(dimension_semantics=("arbitrary", "arbitrary")),
    )(q, k, v, do, lse_row, delta_row)


def _bwd_proj(x, dx1, pos, proj, dq, dk, dv, dtail, du, g_in, w_in, g_cq, w_uq, g_ckv, w_ukv, gq, gk, conv_w,
              invf, sgn, tm):
    T = x.shape[0]
    nt = T // tm

    def body(x_ref, dx1_ref, pos_ref, lat_ref, cc_ref, cx_ref, dq_ref, dk_ref, dv_ref, dtail_ref, du_ref, dun_ref,
             g_in_ref, w_in_ref, g_cq_ref, w_uq_ref, g_ckv_ref, w_ukv_ref, gq_ref, gk_ref, cw_ref, invf_ref, sgn_ref,
             gx_ref, h_ref, dproj_ref, dw_uq_ref, dw_ukv_ref, dg_in_ref, dg_cq_ref, dg_ckv_ref, dgq_ref, dgk_ref):
        i = pl.program_id(0)

        @pl.when(i == 0)
        def _():
            for r in (dw_uq_ref, dw_ukv_ref, dg_in_ref, dg_cq_ref, dg_ckv_ref, dgq_ref, dgk_ref):
                r[...] = jnp.zeros_like(r)

        xv = x_ref[...]
        r0 = _inv_rms(xv, D_MODEL)
        xh0 = xv * r0
        g_in = g_in_ref[...]
        h_ref[...] = (xh0 * g_in).astype(BF16)

        c_q = lat_ref[:, 0:Q_LORA]
        rq = _inv_rms(c_q, Q_LORA)
        xq = c_q * rq
        g_cq = g_cq_ref[...]
        cqn = (xq * g_cq).astype(BF16)
        c_kv = lat_ref[:, Q_LORA:Q_LORA + KV_LORA]
        rkv = _inv_rms(c_kv, KV_LORA)
        xkv = c_kv * rkv
        g_ckv = g_ckv_ref[...]
        ckvn = (xkv * g_ckv).astype(BF16)
        kpe = lat_ref[:, 384:512]
        kpe_ss = jnp.sum(kpe * kpe, axis=-1, keepdims=True)
        cos_b, sin_b = _rope_tables(pos_ref, invf_ref, sgn_ref)
        gq_a, gq_b = gq_ref[:, 0:NOPE], gq_ref[:, NOPE:HEAD_PAD]
        gk_a, gk_b = gk_ref[:, 0:NOPE], gk_ref[:, NOPE:HEAD_PAD]

        dkpe = jnp.zeros((tm, LANES), F32)
        dcqn = jnp.zeros((tm, Q_LORA), F32)
        dckvn = jnp.zeros((tm, KV_LORA), F32)
        for hd in range(N_HEADS):
            c0 = hd * HEAD_PAD
            qh = _dot(cqn, w_uq_ref[:, c0:c0 + HEAD_PAD])
            a, b = qh[:, 0:NOPE], qh[:, NOPE:HEAD_PAD]
            r = lax.rsqrt((jnp.sum(a * a, axis=-1, keepdims=True) + jnp.sum(b * b, axis=-1, keepdims=True)) / QK_DIM + EPS)
            xa, xb = a * r, b * r
            dan = dq_ref[hd, :, 0:NOPE]
            dbr = dq_ref[hd, :, NOPE:HEAD_PAD]
            dbn = dbr * cos_b + _swap_rope_halves(dbr * sin_b)
            dgq_ref[:, 0:NOPE] += _colsum(dan * xa)
            dgq_ref[:, NOPE:HEAD_PAD] += _colsum(dbn * xb)
            dxa, dxb = dan * gq_a, dbn * gq_b
            cq = (jnp.sum(dxa * xa, axis=-1, keepdims=True) + jnp.sum(dxb * xb, axis=-1, keepdims=True)) / QK_DIM
            dqh = jnp.concatenate([r * (dxa - xa * cq), r * (dxb - xb * cq)], axis=-1).astype(BF16)
            dw_uq_ref[:, c0:c0 + HEAD_PAD] += _dot_tn(cqn, dqh)
            dcqn = dcqn + _dot_nt(dqh, w_uq_ref[:, c0:c0 + HEAD_PAD])
            kvh = _dot(ckvn, w_ukv_ref[:, c0:c0 + HEAD_PAD])
            ka = kvh[:, 0:NOPE]
            rk = lax.rsqrt((jnp.sum(ka * ka, axis=-1, keepdims=True) + kpe_ss) / QK_DIM + EPS)
            xka, xkb = ka * rk, kpe * rk
            dkan = dk_ref[hd, :, 0:NOPE]
            dkbr = dk_ref[hd, :, NOPE:HEAD_PAD]
            dkbn = dkbr * cos_b + _swap_rope_halves(dkbr * sin_b)
            dgk_ref[:, 0:NOPE] += _colsum(dkan * xka)
            dgk_ref[:, NOPE:HEAD_PAD] += _colsum(dkbn * xkb)
            dxka, dxkb = dkan * gk_a, dkbn * gk_b
            ck = (jnp.sum(dxka * xka, axis=-1, keepdims=True) + jnp.sum(dxkb * xkb, axis=-1, keepdims=True)) / QK_DIM
            dkpe = dkpe + rk * (dxkb - xkb * ck)
            dkvh = jnp.concatenate([rk * (dxka - xka * ck), dv_ref[hd]], axis=-1).astype(BF16)
            dw_ukv_ref[:, c0:c0 + HEAD_PAD] += _dot_tn(ckvn, dkvh)
            dckvn = dckvn + _dot_nt(dkvh, w_ukv_ref[:, c0:c0 + HEAD_PAD])

        dg_cq_ref[...] += _colsum(dcqn * xq)
        dxq = dcqn * g_cq
        dproj_ref[:, 0:Q_LORA] = (rq * (dxq - xq * (jnp.sum(dxq * xq, axis=-1, keepdims=True) / Q_LORA))).astype(BF16)
        dg_ckv_ref[...] += _colsum(dckvn * xkv)
        dxkv = dckvn * g_ckv
        dproj_ref[:, 256:384] = (rkv * (dxkv - xkv * (jnp.sum(dxkv * xkv, axis=-1, keepdims=True) / KV_LORA))).astype(BF16)
        dproj_ref[:, 384:512] = dkpe.astype(BF16)
        dproj_ref[:, 512:1536] = dtail_ref[:, 0:1024]
        dproj_ref[:, 2560:3072] = dtail_ref[:, 1024:1536]

        du_v = du_ref[...]
        not_last = jnp.where(i < nt - 1, 1.0, 0.0)
        nx0 = dun_ref[0:1, :] * not_last
        nx1 = dun_ref[1:2, :] * not_last
        row = lax.broadcasted_iota(jnp.int32, du_v.shape, 0)
        du1 = jnp.where(row == tm - 1, nx0, pltpu.roll(du_v, tm - 1, 0))
        du2 = jnp.where(row == tm - 2, nx0, jnp.where(row == tm - 1, nx1, pltpu.roll(du_v, tm - 2, 0)))
        dvc = cw_ref[2:3, :] * du_v + cw_ref[1:2, :] * du1 + cw_ref[0:1, :] * du2
        dproj_ref[:, 1536:2048] = (dvc * cx_ref[...]).astype(BF16)
        dproj_ref[:, 2048:2560] = (dvc * cc_ref[...]).astype(BF16)

        dh = jnp.zeros((tm, D_MODEL), F32)
        for c0 in range(0, PROJ_EXT, 512):
            dh = dh + _dot_nt(dproj_ref[:, c0:c0 + 512], w_in_ref[:, c0:c0 + 512])
        dg_in_ref[...] += _colsum(dh * xh0)
        dxh = dh * g_in
        gx_ref[...] = dx1_ref[...] + r0 * (dxh - xh0 * (jnp.sum(dxh * xh0, axis=-1, keepdims=True) / D_MODEL))

    row = lambda i: (i, 0)
    col = lambda c: (lambda i: (i, c))
    head_rows = lambda i: (0, i, 0)
    nxt = lambda i: (jnp.minimum((i + 1) * (tm // 8), T // 8 - 1), 0)
    in_specs = [pl.BlockSpec((tm, D_MODEL), row), pl.BlockSpec((tm, D_MODEL), row), pl.BlockSpec((tm, 1), row),
                pl.BlockSpec((tm, 512), col(0)), pl.BlockSpec((tm, 512), col(3)), pl.BlockSpec((tm, 512), col(4)),
                pl.BlockSpec((N_HEADS, tm, HEAD_PAD), head_rows), pl.BlockSpec((N_HEADS, tm, HEAD_PAD), head_rows),
                pl.BlockSpec((N_HEADS, tm, V_DIM), head_rows), pl.BlockSpec((tm, 1536), row),
                pl.BlockSpec((tm, CONV_W), row), pl.BlockSpec((8, CONV_W), nxt),
                _full((1, D_MODEL)), _full((D_MODEL, PROJ_EXT)), _full((1, Q_LORA)), _full((Q_LORA, N_HEADS * HEAD_PAD)),
                _full((1, KV_LORA)), _full((KV_LORA, N_HEADS * HEAD_PAD)), _full((1, HEAD_PAD)), _full((1, HEAD_PAD)),
                _full((3, CONV_W)), _full((1, LANES)), _full((1, LANES))]
    out_specs = [pl.BlockSpec((tm, D_MODEL), row), pl.BlockSpec((tm, D_MODEL), row), pl.BlockSpec((tm, PROJ_EXT), row),
                 _full((Q_LORA, N_HEADS * HEAD_PAD)), _full((KV_LORA, N_HEADS * HEAD_PAD)),
                 _full((1, D_MODEL)), _full((1, Q_LORA)), _full((1, KV_LORA)), _full((1, HEAD_PAD)), _full((1, HEAD_PAD))]
    out_shape = [jax.ShapeDtypeStruct((T, D_MODEL), F32), jax.ShapeDtypeStruct((T, D_MODEL), BF16),
                 jax.ShapeDtypeStruct((T, PROJ_EXT), BF16),
                 jax.ShapeDtypeStruct((Q_LORA, N_HEADS * HEAD_PAD), F32), jax.ShapeDtypeStruct((KV_LORA, N_HEADS * HEAD_PAD), F32),
                 jax.ShapeDtypeStruct((1, D_MODEL), F32), jax.ShapeDtypeStruct((1, Q_LORA), F32),
                 jax.ShapeDtypeStruct((1, KV_LORA), F32), jax.ShapeDtypeStruct((1, HEAD_PAD), F32),
                 jax.ShapeDtypeStruct((1, HEAD_PAD), F32)]
    return pl.pallas_call(
        body, name="bwd_proj", grid=(nt,), in_specs=in_specs, out_specs=out_specs, out_shape=out_shape,
        compiler_params=_params(dimension_semantics=("arbitrary",)),
    )(x, dx1, pos, proj, proj, proj, dq, dk, dv, dtail, du, du, g_in, w_in, g_cq, w_uq, g_ckv, w_ukv, gq, gk, conv_w,
      invf, sgn)


def _matmul_tn(a, b, tt, tn):
    T, M = a.shape
    N = b.shape[1]

    def body(a_ref, b_ref, o_ref):
        @pl.when(pl.program_id(1) == 0)
        def _():
            o_ref[...] = jnp.zeros_like(o_ref)

        o_ref[...] += _dot_tn(a_ref[...], b_ref[...])

    return pl.pallas_call(
        body, name="dw_in", grid=(N // tn, T // tt),
        in_specs=[pl.BlockSpec((tt, M), lambda j, t: (t, 0)), pl.BlockSpec((tt, tn), lambda j, t: (t, j))],
        out_specs=pl.BlockSpec((M, tn), lambda j, t: (0, j)),
        out_shape=jax.ShapeDtypeStruct((M, N), F32),
        compiler_params=_params(dimension_semantics=("arbitrary", "arbitrary")),
    )(a, b)


def _add_pair(grads, from_sibling, small, small_sibling, c):
    n = len(grads)

    def body(c_ref, *refs):
        ins, outs = refs[:2 * n + 2], refs[2 * n + 2:]
        for i in range(n + 1):
            outs[i][...] = ins[2 * i][...] + ins[2 * i + 1][...]

    in_specs, out_specs, out_shape, args = [], [], [], []
    for g, r in zip(grads, from_sibling):
        _, hr, cols = r.shape
        in_specs += [pl.BlockSpec((1, hr, cols), lambda k, c_ref: (k, c_ref[0], 0)),
                     pl.BlockSpec((1, hr, cols), lambda k, c_ref: (k, 0, 0))]
        out_specs.append(pl.BlockSpec((1, hr, cols), lambda k, c_ref: (k, 0, 0)))
        out_shape.append(jax.ShapeDtypeStruct(r.shape, F32))
        args += [g, r]
    whole = pl.BlockSpec(small.shape, lambda k, c_ref: (0, 0))
    in_specs += [whole, whole]
    out_specs.append(whole)
    out_shape.append(jax.ShapeDtypeStruct(small.shape, F32))
    outs = pl.pallas_call(
        body, name="add_pair", out_shape=out_shape,
        grid_spec=pltpu.PrefetchScalarGridSpec(num_scalar_prefetch=1, grid=(N_CHIPS,), in_specs=in_specs,
                                               out_specs=out_specs),
        compiler_params=_params(dimension_semantics=("arbitrary",)),
    )(c.reshape(1), *args, small, small_sibling)
    return outs[:n], outs[n]


def _add_chips(parts, small_parts):
    arrays = list(parts) + [small_parts]

    def body(*refs):
        ins, outs = refs[:len(arrays)], refs[len(arrays):]
        for a_ref, o_ref in zip(ins, outs):
            o_ref[...] = ((a_ref[0] + a_ref[1]) + a_ref[2]) + a_ref[3]

    in_specs, out_specs, out_shape = [], [], []
    for a in arrays:
        _, rows, cols = a.shape
        in_specs.append(pl.BlockSpec((N_CHIPS, rows // 2, cols), lambda i: (0, i, 0)))
        out_specs.append(pl.BlockSpec((rows // 2, cols), lambda i: (i, 0)))
        out_shape.append(jax.ShapeDtypeStruct((rows, cols), F32))
    outs = pl.pallas_call(body, name="add_chips", grid=(2,), in_specs=in_specs, out_specs=out_specs,
                          out_shape=out_shape, compiler_params=_params(dimension_semantics=("arbitrary",)))(*arrays)
    return outs[:-1], outs[-1]


def _adamw(w, g, m, v, name):
    rows, cols = w.shape
    rb = 256 if rows * cols > 512 * 1024 else rows

    def body(w_ref, g_ref, m_ref, v_ref, d_ref, nm_ref, nv_ref):
        gv = g_ref[...]
        nm = B1 * m_ref[...] + (1.0 - B1) * gv
        nv = B2 * v_ref[...] + (1.0 - B2) * (gv * gv)
        m_hat = nm / (1.0 - B1 ** STEP)
        v_hat = nv / (1.0 - B2 ** STEP)
        d_ref[...] = -LR * (m_hat / (jnp.sqrt(v_hat) + ADAM_EPS) + WD * w_ref[...])
        nm_ref[...] = nm
        nv_ref[...] = nv

    spec = pl.BlockSpec((rb, cols), lambda i: (i, 0))
    shp = jax.ShapeDtypeStruct(w.shape, F32)
    return pl.pallas_call(body, name=name, grid=(rows // rb,), in_specs=[spec] * 4, out_specs=[spec] * 3,
                          out_shape=[shp] * 3)(w, g, m, v)


_ANY = pl.BlockSpec(memory_space=pl.ANY)


def _mesh_pos():
    return lax.axis_index("x"), lax.axis_index("y"), lax.axis_index("c")


def _other_chips(x, y):
    return [(1 - x, y), (x, 1 - y), (1 - x, 1 - y)]


def _remote(src, dst, send_sems, recv_sems, k, to):
    return pltpu.make_async_remote_copy(src_ref=src, dst_ref=dst, send_sem=send_sems.at[k], recv_sem=recv_sems.at[k],
                                        device_id=to, device_id_type=MESH)


def _gather_weights(shards):
    n = len(shards)
    halved = [s.shape[0] % 32 == 0 for s in shards]

    def body(*refs):
        ins, outs, stage = refs[:n], refs[n:2 * n], refs[2 * n:3 * n]
        send_sems, recv_sems, local_sems = refs[3 * n:]
        x, y, c = _mesh_pos()
        me = 2 * x + y
        chips = _other_chips(x, y)

        def part(i, ref, hc):
            if not halved[i]:
                return ref
            hr = shards[i].shape[0] // 2
            return ref.at[pl.ds(hc * hr, hr), :]

        locals_, started = [], []
        for i in range(n):
            stage[i][...] = ins[i][...].astype(BF16)
            mine = pltpu.make_async_copy(stage[i], outs[i].at[me], local_sems.at[i])
            mine.start()
            locals_.append(mine)
            for j, (cx, cy) in enumerate(chips):
                cp = _remote(part(i, stage[i], c), part(i, outs[i].at[me], c), send_sems, recv_sems, 6 * i + j, (cx, cy, c))
                cp.start()
                started.append(cp)
        for i in range(n):
            for j, (cx, cy) in enumerate(chips):
                got = part(i, outs[i].at[2 * cx + cy], c)
                _remote(got, got, send_sems, recv_sems, 6 * i + j, (cx, cy, c)).wait_recv()
                if halved[i]:
                    fwd = _remote(got, got, send_sems, recv_sems, 6 * i + 3 + j, (x, y, 1 - c))
                    fwd.start()
                    started.append(fwd)
        for i in range(n):
            if halved[i]:
                for j, (cx, cy) in enumerate(chips):
                    got = part(i, outs[i].at[2 * cx + cy], 1 - c)
                    _remote(got, got, send_sems, recv_sems, 6 * i + 3 + j, (x, y, 1 - c)).wait_recv()
        for cp in started:
            cp.wait_send()
        for cp in locals_:
            cp.wait()

    vmem = pl.BlockSpec(memory_space=pltpu.VMEM)
    return pl.pallas_call(
        body, name="gather_weights", in_specs=[vmem] * n, out_specs=[_ANY] * n,
        out_shape=[jax.ShapeDtypeStruct((N_CHIPS,) + s.shape, BF16) for s in shards],
        scratch_shapes=[pltpu.VMEM(s.shape, BF16) for s in shards]
                       + [pltpu.SemaphoreType.DMA((6 * n,)), pltpu.SemaphoreType.DMA((6 * n,)), pltpu.SemaphoreType.DMA((n,))],
        compiler_params=_params(),
    )(*shards)


def _swap_halves(grads, small):
    n = len(grads)
    arrays = list(grads) + [small]

    def body(*refs):
        ins, outs, send_sems, recv_sems = refs[:n + 1], refs[n + 1:2 * n + 2], refs[2 * n + 2], refs[2 * n + 3]
        x, y, c = _mesh_pos()
        cps = []
        for i in range(n + 1):
            src = ins[i]
            if i < n:
                hr = grads[i].shape[1] // 2
                src = src.at[:, pl.ds((1 - c) * hr, hr), :]
            cp = _remote(src, outs[i], send_sems, recv_sems, i, (x, y, 1 - c))
            cp.start()
            cps.append(cp)
        for cp in cps:
            cp.wait()

    out_shape = [jax.ShapeDtypeStruct((g.shape[0], g.shape[1] // 2, g.shape[2]), F32) for g in grads]
    out_shape.append(jax.ShapeDtypeStruct(small.shape, F32))
    outs = pl.pallas_call(
        body, name="pair_grads", in_specs=[_ANY] * (n + 1), out_specs=[_ANY] * (n + 1), out_shape=out_shape,
        scratch_shapes=[pltpu.SemaphoreType.DMA((n + 1,)), pltpu.SemaphoreType.DMA((n + 1,))],
    )(*arrays)
    return outs[:n], outs[n]


def _scatter_to_chips(parts, small):
    n = len(parts)
    arrays = list(parts) + [small]

    def body(*refs):
        ins, outs = refs[:n + 1], refs[n + 1:2 * n + 2]
        send_sems, recv_sems, local_sems = refs[2 * n + 2:]
        x, y, c = _mesh_pos()
        me = 2 * x + y
        chips = _other_chips(x, y)
        locals_, sends = [], []
        for i in range(n + 1):
            mine = pltpu.make_async_copy(ins[i].at[me] if i < n else ins[i], outs[i].at[me], local_sems.at[i])
            mine.start()
            locals_.append(mine)
            for j, (cx, cy) in enumerate(chips):
                src = ins[i].at[2 * cx + cy] if i < n else ins[i]
                cp = _remote(src, outs[i].at[me], send_sems, recv_sems, 3 * i + j, (cx, cy, c))
                cp.start()
                sends.append(cp)
        for i in range(n + 1):
            for j, (cx, cy) in enumerate(chips):
                got = outs[i].at[2 * cx + cy]
                _remote(got, got, send_sems, recv_sems, 3 * i + j, (cx, cy, c)).wait_recv()
        for cp in sends:
            cp.wait_send()
        for cp in locals_:
            cp.wait()

    out_shape = [jax.ShapeDtypeStruct(p.shape, F32) for p in parts]
    out_shape.append(jax.ShapeDtypeStruct((N_CHIPS,) + small.shape, F32))
    outs = pl.pallas_call(
        body, name="scatter_grads", in_specs=[_ANY] * (n + 1), out_specs=[_ANY] * (n + 1), out_shape=out_shape,
        scratch_shapes=[pltpu.SemaphoreType.DMA((3 * n + 3,)), pltpu.SemaphoreType.DMA((3 * n + 3,)),
                        pltpu.SemaphoreType.DMA((n + 1,))],
    )(*arrays)
    return outs[:n], outs[n]


def _share_halves(halves):
    n = len(halves)

    def body(*refs):
        ins, outs, send_sems, recv_sems, local_sems = refs[:n], refs[n:2 * n], refs[2 * n], refs[2 * n + 1], refs[2 * n + 2]
        x, y, c = _mesh_pos()
        cps, locals_ = [], []
        for i in range(n):
            hr = halves[i].shape[0]
            dst = outs[i].at[pl.ds(c * hr, hr), :]
            mine = pltpu.make_async_copy(ins[i], dst, local_sems.at[i])
            mine.start()
            locals_.append(mine)
            cp = _remote(ins[i], dst, send_sems, recv_sems, i, (x, y, 1 - c))
            cp.start()
            cps.append(cp)
        for i in range(n):
            hr = halves[i].shape[0]
            got = outs[i].at[pl.ds((1 - c) * hr, hr), :]
            _remote(got, got, send_sems, recv_sems, i, (x, y, 1 - c)).wait_recv()
        for cp in cps:
            cp.wait_send()
        for cp in locals_:
            cp.wait()

    return pl.pallas_call(
        body, name="share_halves", in_specs=[_ANY] * n, out_specs=[_ANY] * n,
        out_shape=[jax.ShapeDtypeStruct((2 * h.shape[0], h.shape[1]), F32) for h in halves],
        scratch_shapes=[pltpu.SemaphoreType.DMA((n,)), pltpu.SemaphoreType.DMA((n,)), pltpu.SemaphoreType.DMA((n,))],
    )(*halves)


SHARD_COLS_IN = IN_TOTAL // N_CHIPS
KPE_END = Q_LORA + KV_LORA + ROPE


def _assemble_weights(c_in, c_uq, c_ukv, c_o, c_pl, c_plg, c_conv):
    by_cols = lambda a: a.transpose(1, 0, 2).reshape(a.shape[1], N_CHIPS * a.shape[2])
    w_in_e = jnp.concatenate([c_in[0][:, :KPE_END], jnp.zeros((D_MODEL, 64), BF16), c_in[0][:, KPE_END:],
                              c_in[1], c_in[2], c_in[3]], axis=1)
    w_uq_e = by_cols(jnp.pad(c_uq, ((0, 0), (0, 0), (0, HEAD_PAD - QK_DIM))))
    return (w_in_e, w_uq_e, by_cols(c_ukv), by_cols(c_conv).astype(F32), c_o.reshape(D_MODEL, D_MODEL),
            by_cols(c_pl), c_plg.reshape(D_MODEL, D_MODEL))


def _split_grads(dw_in_e, dw_uq_e, dw_ukv, dw_o, dw_pl, dw_plg):
    chip_major = lambda a: a.reshape(a.shape[0], N_CHIPS, a.shape[1] // N_CHIPS).transpose(1, 0, 2)
    first = jnp.concatenate([dw_in_e[:, :KPE_END], dw_in_e[:, KPE_END + 64:SHARD_COLS_IN + 64]], axis=1)
    rest = [dw_in_e[:, SHARD_COLS_IN * k + 64:SHARD_COLS_IN * (k + 1) + 64] for k in range(1, N_CHIPS)]
    return [jnp.stack([first] + rest), chip_major(dw_uq_e)[:, :, :QK_DIM], chip_major(dw_ukv),
            dw_o.reshape(N_CHIPS, D_MODEL // N_CHIPS, D_MODEL), chip_major(dw_pl),
            dw_plg.reshape(N_CHIPS, D_MODEL // N_CHIPS, D_MODEL)]


def _local_step(x, p, pos, tgt, gains, w_in_e, w_uq_e, w_ukv, conv_w, w_o, w_pl, w_plg, tm, tq):
    g_in, g_cq, g_ckv, g_q, g_k, g_oa, g_oc, g_pl = gains
    T = x.shape[0]
    zpad = lambda a, n: jnp.concatenate([a, jnp.zeros(a.shape[:-1] + (n,), a.dtype)], axis=-1)
    gq, gk = zpad(g_q, HEAD_PAD - QK_DIM), zpad(g_k, HEAD_PAD - QK_DIM)
    inv_freq = 1.0 / (ROPE_THETA ** (jnp.arange(0, ROPE, 2, dtype=F32) / ROPE))
    invf = jnp.concatenate([inv_freq, inv_freq, jnp.zeros((64,), F32)]).reshape(1, LANES)
    sgn = jnp.concatenate([-jnp.ones((32,), F32), jnp.ones((32,), F32), jnp.zeros((64,), F32)]).reshape(1, LANES)

    proj, q, k, v = _fwd_proj(x, pos, g_in, w_in_e, g_cq, w_uq_e, g_ckv, w_ukv, gq, gk, invf, sgn, tm)
    o, lse = _attn_fwd(q, k, v, tq)
    (dx1, do, delta, dtail, du, dw_o, dw_pl, dw_plg, dg_oa, dg_oc, dg_pl, dconv, loss) = _tail(
        x, o, proj, p, tgt, g_oa, g_oc, g_pl, conv_w, w_o, w_pl, w_plg, tm)
    dq = _attn_dq(q, k, v, do, lse, delta, tq)
    lse_row = lse[:, :, 0].reshape(N_HEADS, 1, T)
    delta_row = delta[:, :, 0].reshape(N_HEADS, 1, T)
    dk, dv = _attn_dkv(q, k, v, do, lse_row, delta_row, tq)
    (gx, h, dproj, dw_uq_e, dw_ukv, dg_in, dg_cq, dg_ckv, dgq, dgk) = _bwd_proj(
        x, dx1, pos, proj, dq, dk, dv, dtail, du, g_in, w_in_e, g_cq, w_uq_e, g_ckv, w_ukv, gq, gk, conv_w, invf, sgn, tm)
    dw_in_e = _matmul_tn(h, dproj, min(512, T), 512)
    wgrads = (dw_in_e, dw_uq_e, dw_ukv, dw_o, dw_pl, dw_plg)
    ggrads = (dg_in, dg_cq, dg_ckv, dgq, dgk, dg_oa, dg_oc, dg_pl)
    return loss, gx, wgrads, ggrads, dconv


def kernel(x, p, positions, g_in, w_in, g_cq, w_uq, g_ckv, w_ukv, g_q, g_k, conv_w, g_oa, g_oc, w_o, w_pl, w_plg, g_pl, loss_target, m_g_in, m_w_in, m_g_cq, m_w_uq, m_g_ckv, m_w_ukv, m_g_q, m_g_k, m_conv_w, m_g_oa, m_g_oc, m_w_o, m_w_pl, m_w_plg, m_g_pl, v_g_in, v_w_in, v_g_cq, v_w_uq, v_g_ckv, v_w_ukv, v_g_q, v_g_k, v_conv_w, v_g_oa, v_g_oc, v_w_o, v_w_pl, v_w_plg, v_g_pl):
    T = x.shape[1]
    c = lax.axis_index("c")
    chip = 2 * lax.axis_index("x") + lax.axis_index("y")
    gains = [g.reshape(1, -1) for g in (g_in, g_cq, g_ckv, g_q, g_k, g_oa, g_oc, g_pl)]

    gathered = _gather_weights([w_in[0], w_uq[0], w_ukv[0], w_o[0], w_pl[0], w_plg[0], conv_w[0]])
    full = _assemble_weights(*gathered)

    loss, gx, wgrads, ggrads, dconv = _local_step(
        x[0], p[0, 0], positions.reshape(T, 1), loss_target[0], gains, *full, 256, 512)

    grads_cm = _split_grads(*wgrads)
    small_parts = [a.reshape(-1, LANES) for a in (*ggrads, loss, dconv)]
    small_rows = [a.shape[0] for a in small_parts]
    small = jnp.concatenate(small_parts + [jnp.zeros((SMALL_ROWS - sum(small_rows), LANES), F32)])
    from_sibling, small_sibling = _swap_halves(grads_cm, small)
    chip_parts, chip_small = _add_pair(grads_cm, from_sibling, small, small_sibling, c)
    by_chip, small_by_chip = _scatter_to_chips(chip_parts, chip_small)
    halves, small_total = _add_chips(by_chip, small_by_chip)
    wg = _share_halves(halves)

    gg, off = [], 0
    for rows in small_rows:
        gg.append(small_total[off:off + rows].reshape(1, -1))
        off += rows
    loss_out = gg[8][0, 0]
    conv_total = gg[9].reshape(3, CONV_W)
    conv_g = lax.dynamic_slice(conv_total, (0, chip * (CONV_W // N_CHIPS)), (3, CONV_W // N_CHIPS))
    g_by_name = dict(g_in=gg[0], w_in=wg[0], g_cq=gg[1], w_uq=wg[1], g_ckv=gg[2], w_ukv=wg[2], g_q=gg[3][:, :QK_DIM],
                     g_k=gg[4][:, :QK_DIM], conv_w=conv_g, g_oa=gg[5], g_oc=gg[6], w_o=wg[3], w_pl=wg[4], w_plg=wg[5],
                     g_pl=gg[7])
    weights = dict(g_in=g_in, w_in=w_in, g_cq=g_cq, w_uq=w_uq, g_ckv=g_ckv, w_ukv=w_ukv, g_q=g_q, g_k=g_k,
                   conv_w=conv_w, g_oa=g_oa, g_oc=g_oc, w_o=w_o, w_pl=w_pl, w_plg=w_plg, g_pl=g_pl)
    ms = dict(g_in=m_g_in, w_in=m_w_in, g_cq=m_g_cq, w_uq=m_w_uq, g_ckv=m_g_ckv, w_ukv=m_w_ukv, g_q=m_g_q, g_k=m_g_k,
              conv_w=m_conv_w, g_oa=m_g_oa, g_oc=m_g_oc, w_o=m_w_o, w_pl=m_w_pl, w_plg=m_w_plg, g_pl=m_g_pl)
    vs = dict(g_in=v_g_in, w_in=v_w_in, g_cq=v_g_cq, w_uq=v_w_uq, g_ckv=v_g_ckv, w_ukv=v_w_ukv, g_q=v_g_q, g_k=v_g_k,
              conv_w=v_conv_w, g_oa=v_g_oa, g_oc=v_g_oc, w_o=v_w_o, w_pl=v_w_pl, w_plg=v_w_plg, g_pl=v_g_pl)
    names = list(weights)
    grads, deltas, new_m, new_v = [], [], [], []
    for n in names:
        w = weights[n]
        w2 = w.reshape(-1, w.shape[-1])
        g2 = g_by_name[n].reshape(w2.shape)
        d, nm, nv = _adamw(w2, g2, ms[n].reshape(w2.shape), vs[n].reshape(w2.shape), "adamw_" + n)
        grads.append(g2.reshape(w.shape))
        deltas.append(d.reshape(w.shape))
        new_m.append(nm.reshape(w.shape))
        new_v.append(nv.reshape(w.shape))
    return (loss_out, gx.reshape(x.shape), *grads, *deltas, *new_m, *new_v)
```

```python
import functools
import math

import jax
import jax.numpy as jnp
from jax import lax
from jax.experimental import pallas as pl
from jax.experimental.pallas import tpu as pltpu

F32 = jnp.float32
BF16 = jnp.bfloat16

D_MODEL = 1024
N_HEADS = 4
NOPE = 128
ROPE = 64
V_DIM = 128
QK_DIM = NOPE + ROPE
HEAD_PAD = 256
Q_LORA = 256
KV_LORA = 128
ATTN_W = 512
CONV_W = 512
PLE = 256
IN_TOTAL = 3008
PROJ_EXT = 3072
ROPE_THETA = 10000.0
EPS = 1e-6
SCALE = 1.0 / math.sqrt(QK_DIM)
LOG2E = math.log2(math.e)
EXP2_SCALE = SCALE * LOG2E
NEG = -1e30
SOFTMAX_ROWS = 32

LR, B1, B2, ADAM_EPS, WD, STEP = 0.001, 0.9, 0.999, 1e-08, 0.01, 10

N_CHIPS = 4
LANES = 128
VMEM_LIMIT = 56 * 1024 * 1024
MESH = pl.DeviceIdType.MESH


def _params(**kw):
    return pltpu.CompilerParams(vmem_limit_bytes=VMEM_LIMIT, **kw)


def _inv_rms(x, n):
    return lax.rsqrt(jnp.sum(x * x, axis=-1, keepdims=True) / n + EPS)


def _sigmoid(z):
    return 1.0 / (1.0 + jnp.exp(-z))


def _swap_rope_halves(b):
    lane = lax.broadcasted_iota(jnp.int32, b.shape, 1)
    swapped = jnp.where(lane < 32, pltpu.roll(b, 96, 1), pltpu.roll(b, 32, 1))
    return jnp.where(lane < ROPE, swapped, 0.0)


def _dot(a, b):
    return jnp.dot(a, b, preferred_element_type=F32)


def _dot_nt(a, b):
    return lax.dot_general(a, b, (((1,), (1,)), ((), ())), preferred_element_type=F32)


def _dot_tn(a, b):
    return lax.dot_general(a, b, (((0,), (0,)), ((), ())), preferred_element_type=F32)


def _colsum(a):
    return jnp.sum(a, axis=0, keepdims=True)


def _full(shape):
    return pl.BlockSpec(shape, lambda *_: (0,) * len(shape))


def _rope_tables(pos_ref, invf_ref, sgn_ref):
    ang = pos_ref[...].astype(F32) * invf_ref[...]
    return jnp.cos(ang), jnp.sin(ang) * sgn_ref[...]


def _fwd_proj(x, pos, g_in, w_in, g_cq, w_uq, g_ckv, w_ukv, gq, gk, invf, sgn, tm):
    T = x.shape[0]

    def body(x_ref, pos_ref, g_in_ref, w_in_ref, g_cq_ref, w_uq_ref, g_ckv_ref, w_ukv_ref, gq_ref, gk_ref,
             invf_ref, sgn_ref, proj_ref, q_ref, k_ref, v_ref):
        xv = x_ref[...]
        h = (xv * _inv_rms(xv, D_MODEL) * g_in_ref[...]).astype(BF16)
        for c0 in range(0, PROJ_EXT, 512):
            proj_ref[:, c0:c0 + 512] = _dot(h, w_in_ref[:, c0:c0 + 512])
        c_q = proj_ref[:, 0:Q_LORA]
        cqn = (c_q * _inv_rms(c_q, Q_LORA) * g_cq_ref[...]).astype(BF16)
        c_kv = proj_ref[:, Q_LORA:Q_LORA + KV_LORA]
        ckvn = (c_kv * _inv_rms(c_kv, KV_LORA) * g_ckv_ref[...]).astype(BF16)
        kpe = proj_ref[:, 384:512]
        kpe_ss = jnp.sum(kpe * kpe, axis=-1, keepdims=True)
        cos_b, sin_b = _rope_tables(pos_ref, invf_ref, sgn_ref)
        gq_a, gq_b = gq_ref[:, 0:NOPE], gq_ref[:, NOPE:HEAD_PAD]
        gk_a, gk_b = gk_ref[:, 0:NOPE], gk_ref[:, NOPE:HEAD_PAD]
        for hd in range(N_HEADS):
            c0 = hd * HEAD_PAD
            qh = _dot(cqn, w_uq_ref[:, c0:c0 + HEAD_PAD])
            a, b = qh[:, 0:NOPE], qh[:, NOPE:HEAD_PAD]
            r = lax.rsqrt((jnp.sum(a * a, axis=-1, keepdims=True) + jnp.sum(b * b, axis=-1, keepdims=True)) / QK_DIM + EPS)
            bn = b * r * gq_b
            q_ref[hd, :, 0:NOPE] = (a * r * gq_a).astype(BF16)
            q_ref[hd, :, NOPE:HEAD_PAD] = (bn * cos_b + _swap_rope_halves(bn) * sin_b).astype(BF16)
            kvh = _dot(ckvn, w_ukv_ref[:, c0:c0 + HEAD_PAD])
            ka = kvh[:, 0:NOPE]
            rk = lax.rsqrt((jnp.sum(ka * ka, axis=-1, keepdims=True) + kpe_ss) / QK_DIM + EPS)
            kbn = kpe * rk * gk_b
            k_ref[hd, :, 0:NOPE] = (ka * rk * gk_a).astype(BF16)
            k_ref[hd, :, NOPE:HEAD_PAD] = (kbn * cos_b + _swap_rope_halves(kbn) * sin_b).astype(BF16)
            v_ref[hd, :, 0:V_DIM] = kvh[:, NOPE:HEAD_PAD].astype(BF16)
            v_ref[hd, :, V_DIM:2 * V_DIM] = jnp.ones((tm, V_DIM), BF16)

    row = lambda i: (i, 0)
    head_rows = lambda i: (0, i, 0)
    return pl.pallas_call(
        body, name="fwd_proj", grid=(T // tm,),
        in_specs=[pl.BlockSpec((tm, D_MODEL), row), pl.BlockSpec((tm, 1), row), _full((1, D_MODEL)),
                  _full((D_MODEL, PROJ_EXT)), _full((1, Q_LORA)), _full((Q_LORA, N_HEADS * HEAD_PAD)),
                  _full((1, KV_LORA)), _full((KV_LORA, N_HEADS * HEAD_PAD)), _full((1, HEAD_PAD)), _full((1, HEAD_PAD)),
                  _full((1, LANES)), _full((1, LANES))],
        out_specs=[pl.BlockSpec((tm, PROJ_EXT), row), pl.BlockSpec((N_HEADS, tm, HEAD_PAD), head_rows),
                   pl.BlockSpec((N_HEADS, tm, HEAD_PAD), head_rows), pl.BlockSpec((N_HEADS, tm, 2 * V_DIM), head_rows)],
        out_shape=[jax.ShapeDtypeStruct((T, PROJ_EXT), F32), jax.ShapeDtypeStruct((N_HEADS, T, HEAD_PAD), BF16),
                   jax.ShapeDtypeStruct((N_HEADS, T, HEAD_PAD), BF16), jax.ShapeDtypeStruct((N_HEADS, T, 2 * V_DIM), BF16)],
        compiler_params=_params(dimension_semantics=("arbitrary",)),
    )(x, pos, g_in, w_in, g_cq, w_uq, g_ckv, w_ukv, gq, gk, invf, sgn)


def _chunk_pipeline(n_loop, matmuls, pointwise, accumulate, last):
    def iteration(t, slot):
        matmuls(t + 1, 1 - slot)
        accumulate(jnp.maximum(t - 1, 0), 1 - slot)
        pointwise(t, slot, False)

    def finish(slot):
        accumulate(jnp.maximum(n_loop - 1, 0), 1 - slot)
        pointwise(n_loop, slot, True)
        accumulate(n_loop, slot)
        last()

    matmuls(0, 0)

    def pair(tt, carry):
        iteration(2 * tt, 0)
        iteration(2 * tt + 1, 1)
        return carry

    lax.fori_loop(0, n_loop // 2, pair, 0)
    odd = lax.rem(n_loop, 2) == 1

    @pl.when(odd)
    def _():
        iteration(n_loop - 1, 0)
        finish(1)

    @pl.when(jnp.logical_not(odd))
    def _():
        finish(0)


def _attn_fwd(q, k, v, tq):
    T = q.shape[1]
    tk = tq
    rc = min(SOFTMAX_ROWS, tq)

    def body(q_ref, k_ref, v_ref, o_ref, lse_ref, s0, s1, p0, p1, a0, a1, m_ref, acc_ref):
        qi = pl.program_id(1)
        s_buf, p_buf, a_buf = (s0, s1), (p0, p1), (a0, a1)

        def scores(t, slot):
            ks = pl.multiple_of(t * tk, tk)
            s_buf[slot][...] = _dot_nt(q_ref[0], k_ref[0, pl.ds(ks, tk), :])

        def values(t, slot):
            ks = pl.multiple_of(t * tk, tk)
            acc_ref[...] = acc_ref[...] * a_buf[slot][...] + _dot(p_buf[slot][...], v_ref[0, pl.ds(ks, tk), :])

        def softmax(t, slot, masked):
            s_all = s_buf[slot][...]
            if masked:
                row = lax.broadcasted_iota(jnp.int32, (tq, tk), 0)
                col = lax.broadcasted_iota(jnp.int32, (tq, tk), 1)
                s_all = jnp.where(col <= row, s_all, NEG)
                s_buf[slot][...] = s_all
            m_old = m_ref[...]
            m_new = jnp.maximum(m_old, jnp.max(s_all, axis=1, keepdims=True))
            a_buf[slot][...] = jnp.exp2((m_old - m_new) * EXP2_SCALE)
            m_ref[...] = m_new
            for r0 in range(0, tq, rc):
                s = s_buf[slot][r0:r0 + rc, :]
                p_buf[slot][r0:r0 + rc, :] = jnp.exp2((s - m_new[r0:r0 + rc, :]) * EXP2_SCALE).astype(BF16)

        def last():
            l = acc_ref[:, V_DIM:2 * V_DIM]
            o_ref[...] = acc_ref[:, 0:V_DIM] / l
            lse_ref[0] = m_ref[...] * SCALE + jnp.log(l)

        m_ref[...] = jnp.full_like(m_ref, NEG)
        acc_ref[...] = jnp.zeros_like(acc_ref)
        p1[...] = jnp.zeros_like(p1)
        a1[...] = jnp.ones_like(a1)
        _chunk_pipeline(qi, scores, softmax, values, last)

    return pl.pallas_call(
        body, name="attn_fwd", grid=(N_HEADS, T // tq),
        in_specs=[pl.BlockSpec((1, tq, HEAD_PAD), lambda h, i: (h, i, 0)),
                  pl.BlockSpec((1, T, HEAD_PAD), lambda h, i: (h, 0, 0)),
                  pl.BlockSpec((1, T, 2 * V_DIM), lambda h, i: (h, 0, 0))],
        out_specs=[pl.BlockSpec((tq, V_DIM), lambda h, i: (i, h)),
                   pl.BlockSpec((1, tq, LANES), lambda h, i: (h, i, 0))],
        out_shape=[jax.ShapeDtypeStruct((T, ATTN_W), F32), jax.ShapeDtypeStruct((N_HEADS, T, LANES), F32)],
        scratch_shapes=[pltpu.VMEM((tq, tk), F32), pltpu.VMEM((tq, tk), F32), pltpu.VMEM((tq, tk), BF16),
                        pltpu.VMEM((tq, tk), BF16), pltpu.VMEM((tq, 1), F32), pltpu.VMEM((tq, 1), F32),
                        pltpu.VMEM((tq, 1), F32), pltpu.VMEM((tq, 2 * V_DIM), F32)],
        compiler_params=_params(dimension_semantics=("arbitrary", "arbitrary")),
    )(q, k, v)


def _tail(x, o, proj, p, tgt, g_oa, g_oc, g_pl, conv_w, w_o, w_pl, w_plg, tm):
    T = x.shape[0]
    nt = T // tm

    def body(x_ref, o_ref, za_ref, cb_ref, cc_ref, cx_ref, zc_ref, cch_ref, cxh_ref, p_ref, tgt_ref,
             g_oa_ref, g_oc_ref, g_pl_ref, cw_ref, w_o_ref, w_pl_ref, w_plg_ref,
             dx1_ref, do_ref, delta_ref, dtail_ref, du_ref,
             dw_o_ref, dw_pl_ref, dw_plg_ref, dg_oa_ref, dg_oc_ref, dg_pl_ref, dcw_ref, loss_ref):
        i = pl.program_id(0)

        @pl.when(i == 0)
        def _():
            for r in (dw_o_ref, dw_pl_ref, dw_plg_ref, dg_oa_ref, dg_oc_ref, dg_pl_ref, dcw_ref, loss_ref):
                r[...] = jnp.zeros_like(r)

        xv, ov, za, cb, zc = x_ref[...], o_ref[...], za_ref[...], cb_ref[...], zc_ref[...]
        g_oa, g_oc, g_pl = g_oa_ref[...], g_oc_ref[...], g_pl_ref[...]
        w0, w1, w2 = cw_ref[0:1, :], cw_ref[1:2, :], cw_ref[2:3, :]

        sa = _sigmoid(za)
        silu_a = za * sa
        ga = ov * silu_a
        ra = _inv_rms(ga, ATTN_W)
        xa = ga * ra
        ya = xa * g_oa
        v = cc_ref[...] * cx_ref[...]
        not_first = jnp.where(i > 0, 1.0, 0.0)
        hv6 = cch_ref[6:7, :] * cxh_ref[6:7, :] * not_first
        hv7 = cch_ref[7:8, :] * cxh_ref[7:8, :] * not_first
        row = lax.broadcasted_iota(jnp.int32, v.shape, 0)
        v1 = jnp.where(row == 0, hv7, pltpu.roll(v, 1, 0))
        v2 = jnp.where(row == 0, hv6, jnp.where(row == 1, hv7, pltpu.roll(v, 2, 0)))
        u = w0 * v2 + w1 * v1 + w2 * v
        sc = _sigmoid(zc)
        silu_c = zc * sc
        gc = cb * u * silu_c
        rc = _inv_rms(gc, CONV_W)
        xc = gc * rc
        yc = xc * g_oc
        ycat = jnp.concatenate([ya, yc], axis=-1).astype(BF16)
        x1 = xv + _dot(ycat, w_o_ref[...])
        r1 = _inv_rms(x1, D_MODEL)
        xh1 = x1 * r1
        n1 = (xh1 * g_pl).astype(BF16)
        gate = _sigmoid(_dot(n1, w_plg_ref[...]))
        pb = p_ref[...].astype(BF16)
        pp = _dot(pb, w_pl_ref[...])
        err = x1 + gate * pp - tgt_ref[...]
        loss_ref[...] += 0.5 * jnp.sum(err * err) / D_MODEL
        dy = err / D_MODEL

        dpp = (dy * gate).astype(BF16)
        da = (dy * pp * gate * (1.0 - gate)).astype(BF16)
        dw_pl_ref[...] += _dot_tn(pb, dpp)
        dw_plg_ref[...] += _dot_tn(n1, da)
        dn1 = _dot_nt(da, w_plg_ref[...])
        dg_pl_ref[...] += _colsum(dn1 * xh1)
        dxh = dn1 * g_pl
        dx1 = dy + r1 * (dxh - xh1 * (jnp.sum(dxh * xh1, axis=-1, keepdims=True) / D_MODEL))
        dx1_ref[...] = dx1
        dx1b = dx1.astype(BF16)
        dw_o_ref[...] += _dot_tn(ycat, dx1b)
        dycat = _dot_nt(dx1b, w_o_ref[...])
        dya, dyc = dycat[:, 0:ATTN_W], dycat[:, ATTN_W:D_MODEL]

        dg_oa_ref[...] += _colsum(dya * xa)
        dxa = dya * g_oa
        dga = ra * (dxa - xa * (jnp.sum(dxa * xa, axis=-1, keepdims=True) / ATTN_W))
        do = (dga * silu_a).astype(BF16)
        do_ref[...] = do
        dof = do.astype(F32) * ov
        for hd in range(N_HEADS):
            dl = jnp.sum(dof[:, hd * V_DIM:(hd + 1) * V_DIM], axis=-1, keepdims=True)
            delta_ref[hd] = jnp.broadcast_to(dl, (tm, LANES))
        dtail_ref[:, 0:512] = (dga * ov * (sa * (1.0 + za * (1.0 - sa)))).astype(BF16)

        dg_oc_ref[...] += _colsum(dyc * xc)
        dxc = dyc * g_oc
        dgc = rc * (dxc - xc * (jnp.sum(dxc * xc, axis=-1, keepdims=True) / CONV_W))
        dtail_ref[:, 512:1024] = (dgc * u * silu_c).astype(BF16)
        du = dgc * cb * silu_c
        du_ref[...] = du
        dtail_ref[:, 1024:1536] = (dgc * cb * u * (sc * (1.0 + zc * (1.0 - sc)))).astype(BF16)
        dcw_ref[0:1, :] += _colsum(du * v2)
        dcw_ref[1:2, :] += _colsum(du * v1)
        dcw_ref[2:3, :] += _colsum(du * v)

    row = lambda i: (i, 0)
    col = lambda c: (lambda i: (i, c))
    halo = lambda c: (lambda i: (jnp.maximum(i * (tm // 8) - 1, 0), c))
    in_specs = [pl.BlockSpec((tm, D_MODEL), row), pl.BlockSpec((tm, ATTN_W), row)]
    in_specs += [pl.BlockSpec((tm, 512), col(c)) for c in (1, 2, 3, 4, 5)]
    in_specs += [pl.BlockSpec((8, 512), halo(3)), pl.BlockSpec((8, 512), halo(4))]
    in_specs += [pl.BlockSpec((tm, PLE), row), pl.BlockSpec((tm, D_MODEL), row),
                 _full((1, ATTN_W)), _full((1, CONV_W)), _full((1, D_MODEL)), _full((3, CONV_W)),
                 _full((D_MODEL, D_MODEL)), _full((PLE, D_MODEL)), _full((D_MODEL, D_MODEL))]
    out_specs = [pl.BlockSpec((tm, D_MODEL), row), pl.BlockSpec((tm, ATTN_W), row),
                 pl.BlockSpec((N_HEADS, tm, LANES), lambda i: (0, i, 0)), pl.BlockSpec((tm, 1536), row),
                 pl.BlockSpec((tm, CONV_W), row),
                 _full((D_MODEL, D_MODEL)), _full((PLE, D_MODEL)), _full((D_MODEL, D_MODEL)),
                 _full((1, ATTN_W)), _full((1, CONV_W)), _full((1, D_MODEL)), _full((3, CONV_W)), _full((1, LANES))]
    out_shape = [jax.ShapeDtypeStruct((T, D_MODEL), F32), jax.ShapeDtypeStruct((T, ATTN_W), BF16),
                 jax.ShapeDtypeStruct((N_HEADS, T, LANES), F32), jax.ShapeDtypeStruct((T, 1536), BF16),
                 jax.ShapeDtypeStruct((T, CONV_W), F32),
                 jax.ShapeDtypeStruct((D_MODEL, D_MODEL), F32), jax.ShapeDtypeStruct((PLE, D_MODEL), F32),
                 jax.ShapeDtypeStruct((D_MODEL, D_MODEL), F32),
                 jax.ShapeDtypeStruct((1, ATTN_W), F32), jax.ShapeDtypeStruct((1, CONV_W), F32),
                 jax.ShapeDtypeStruct((1, D_MODEL), F32), jax.ShapeDtypeStruct((3, CONV_W), F32),
                 jax.ShapeDtypeStruct((1, LANES), F32)]
    return pl.pallas_call(
        body, name="tail", grid=(nt,), in_specs=in_specs, out_specs=out_specs, out_shape=out_shape,
        compiler_params=_params(dimension_semantics=("arbitrary",)),
    )(x, o, proj, proj, proj, proj, proj, proj, proj, p, tgt, g_oa, g_oc, g_pl, conv_w, w_o, w_pl, w_plg)


def _attn_dq(q, k, v, do, lse, delta, tq):
    T = q.shape[1]
    tk = tq
    rc = min(SOFTMAX_ROWS, tq)

    def body(q_ref, k_ref, v_ref, do_ref, lse_ref, dl_ref, dq_ref, s0, s1, d0, d1, g0, g1, acc_ref):
        qi = pl.program_id(1)
        s_buf, dp_buf, g_buf = (s0, s1), (d0, d1), (g0, g1)
        lse2 = lse_ref[0, :, 0:1] * LOG2E
        dl = dl_ref[0, :, 0:1]

        def matmuls(t, slot):
            ks = pl.multiple_of(t * tk, tk)
            s_buf[slot][...] = _dot_nt(q_ref[0], k_ref[0, pl.ds(ks, tk), :])
            dp_buf[slot][...] = _dot_nt(do_ref[...], v_ref[0, pl.ds(ks, tk), :])

        def pointwise(t, slot, masked):
            for r0 in range(0, tq, rc):
                s = s_buf[slot][r0:r0 + rc, :]
                if masked:
                    row = lax.broadcasted_iota(jnp.int32, (rc, tk), 0)
                    col = lax.broadcasted_iota(jnp.int32, (rc, tk), 1)
                    s = jnp.where(col <= row + r0, s, NEG)
                pr = jnp.exp2(s * EXP2_SCALE - lse2[r0:r0 + rc, :])
                g_buf[slot][r0:r0 + rc, :] = (pr * (dp_buf[slot][r0:r0 + rc, :] - dl[r0:r0 + rc, :]) * SCALE).astype(BF16)

        def accumulate(t, slot):
            ks = pl.multiple_of(t * tk, tk)
            acc_ref[...] += _dot(g_buf[slot][...], k_ref[0, pl.ds(ks, tk), :])

        def last():
            dq_ref[0] = acc_ref[...]

        acc_ref[...] = jnp.zeros_like(acc_ref)
        g1[...] = jnp.zeros_like(g1)
        _chunk_pipeline(qi, matmuls, pointwise, accumulate, last)

    return pl.pallas_call(
        body, name="attn_dq", grid=(N_HEADS, T // tq),
        in_specs=[pl.BlockSpec((1, tq, HEAD_PAD), lambda h, i: (h, i, 0)),
                  pl.BlockSpec((1, T, HEAD_PAD), lambda h, i: (h, 0, 0)),
                  pl.BlockSpec((1, T, V_DIM), lambda h, i: (h, 0, 0)),
                  pl.BlockSpec((tq, V_DIM), lambda h, i: (i, h)),
                  pl.BlockSpec((1, tq, LANES), lambda h, i: (h, i, 0)),
                  pl.BlockSpec((1, tq, LANES), lambda h, i: (h, i, 0))],
        out_specs=pl.BlockSpec((1, tq, HEAD_PAD), lambda h, i: (h, i, 0)),
        out_shape=jax.ShapeDtypeStruct((N_HEADS, T, HEAD_PAD), F32),
        scratch_shapes=[pltpu.VMEM((tq, tk), F32)] * 4 + [pltpu.VMEM((tq, tk), BF16)] * 2
                       + [pltpu.VMEM((tq, HEAD_PAD), F32)],
        compiler_params=_params(dimension_semantics=("arbitrary", "arbitrary")),
    )(q, k, v, do, lse, delta)


def _attn_dkv(q, k, v, do, lse_row, delta_row, tk):
    T = q.shape[1]
    tq = tk
    nq = T // tq
    rc = min(SOFTMAX_ROWS, tk)

    def body(q_ref, k_ref, v_ref, do_ref, lse_ref, dl_ref, dk_ref, dv_ref,
             s0, s1, d0, d1, p0, p1, g0, g1, dk_acc, dv_acc):
        kj = pl.program_id(1)
        s_buf, dp_buf, p_buf, g_buf = (s0, s1), (d0, d1), (p0, p1), (g0, g1)

        def q_start(t):
            return pl.multiple_of((nq - 1 - t) * tq, tq)

        def matmuls(t, slot):
            qs = q_start(t)
            s_buf[slot][...] = _dot_nt(k_ref[0], q_ref[0, pl.ds(qs, tq), :])
            dp_buf[slot][...] = _dot_nt(v_ref[0], do_ref[pl.ds(qs, tq), :])

        def pointwise(t, slot, masked):
            qs = q_start(t)
            lse2 = lse_ref[0, :, pl.ds(qs, tq)] * LOG2E
            dl = dl_ref[0, :, pl.ds(qs, tq)]
            for r0 in range(0, tk, rc):
                st = s_buf[slot][r0:r0 + rc, :]
                if masked:
                    row = lax.broadcasted_iota(jnp.int32, (rc, tq), 0)
                    col = lax.broadcasted_iota(jnp.int32, (rc, tq), 1)
                    st = jnp.where(row + r0 <= col, st, NEG)
                pt = jnp.exp2(st * EXP2_SCALE - lse2)
                p_buf[slot][r0:r0 + rc, :] = pt.astype(BF16)
                g_buf[slot][r0:r0 + rc, :] = (pt * (dp_buf[slot][r0:r0 + rc, :] - dl) * SCALE).astype(BF16)

        def accumulate(t, slot):
            qs = q_start(t)
            dv_acc[...] += _dot(p_buf[slot][...], do_ref[pl.ds(qs, tq), :])
            dk_acc[...] += _dot(g_buf[slot][...], q_ref[0, pl.ds(qs, tq), :])

        def last():
            dk_ref[0] = dk_acc[...]
            dv_ref[0] = dv_acc[...]

        dk_acc[...] = jnp.zeros_like(dk_acc)
        dv_acc[...] = jnp.zeros_like(dv_acc)
        p1[...] = jnp.zeros_like(p1)
        g1[...] = jnp.zeros_like(g1)
        _chunk_pipeline(nq - 1 - kj, matmuls, pointwise, accumulate, last)

    return pl.pallas_call(
        body, name="attn_dkv", grid=(N_HEADS, T // tk),
        in_specs=[pl.BlockSpec((1, T, HEAD_PAD), lambda h, j: (h, 0, 0)),
                  pl.BlockSpec((1, tk, HEAD_PAD), lambda h, j: (h, j, 0)),
                  pl.BlockSpec((1, tk, V_DIM), lambda h, j: (h, j, 0)),
                  pl.BlockSpec((T, V_DIM), lambda h, j: (0, h)),
                  pl.BlockSpec((1, 1, T), lambda h, j: (h, 0, 0)),
                  pl.BlockSpec((1, 1, T), lambda h, j: (h, 0, 0))],
        out_specs=[pl.BlockSpec((1, tk, HEAD_PAD), lambda h, j: (h, j, 0)),
                   pl.BlockSpec((1, tk, V_DIM), lambda h, j: (h, j, 0))],
        out_shape=[jax.ShapeDtypeStruct((N_HEADS, T, HEAD_PAD), F32), jax.ShapeDtypeStruct((N_HEADS, T, V_DIM), F32)],
        scratch_shapes=[pltpu.VMEM((tk, tq), F32)] * 4 + [pltpu.VMEM((tk, tq), BF16)] * 4
                       + [pltpu.VMEM((tk, HEAD_PAD), F32), pltpu.VMEM((tk, V_DIM), F32)],
        compiler_params=_params(dimension_semantics=("arbitrary", "arbitrary")),
    )(q, k, v, do, lse_row, delta_row)


def _bwd_proj(x, dx1, pos, proj, dq, dk, dv, dtail, du, g_in, w_in, g_cq, w_uq, g_ckv, w_ukv, gq, gk, conv_w,
              invf, sgn, tm):
    T = x.shape[0]
    nt = T // tm

    def body(x_ref, dx1_ref, pos_ref, lat_ref, cc_ref, cx_ref, dq_ref, dk_ref, dv_ref, dtail_ref, du_ref, dun_ref,
             g_in_ref, w_in_ref, g_cq_ref, w_uq_ref, g_ckv_ref, w_ukv_ref, gq_ref, gk_ref, cw_ref, invf_ref, sgn_ref,
             gx_ref, h_ref, dproj_ref, dw_uq_ref, dw_ukv_ref, dg_in_ref, dg_cq_ref, dg_ckv_ref, dgq_ref, dgk_ref):
        i = pl.program_id(0)

        @pl.when(i == 0)
        def _():
            for r in (dw_uq_ref, dw_ukv_ref, dg_in_ref, dg_cq_ref, dg_ckv_ref, dgq_ref, dgk_ref):
                r[...] = jnp.zeros_like(r)

        xv = x_ref[...]
        r0 = _inv_rms(xv, D_MODEL)
        xh0 = xv * r0
        g_in = g_in_ref[...]
        h_ref[...] = (xh0 * g_in).astype(BF16)

        c_q = lat_ref[:, 0:Q_LORA]
        rq = _inv_rms(c_q, Q_LORA)
        xq = c_q * rq
        g_cq = g_cq_ref[...]
        cqn = (xq * g_cq).astype(BF16)
        c_kv = lat_ref[:, Q_LORA:Q_LORA + KV_LORA]
        rkv = _inv_rms(c_kv, KV_LORA)
        xkv = c_kv * rkv
        g_ckv = g_ckv_ref[...]
        ckvn = (xkv * g_ckv).astype(BF16)
        kpe = lat_ref[:, 384:512]
        kpe_ss = jnp.sum(kpe * kpe, axis=-1, keepdims=True)
        cos_b, sin_b = _rope_tables(pos_ref, invf_ref, sgn_ref)
        gq_a, gq_b = gq_ref[:, 0:NOPE], gq_ref[:, NOPE:HEAD_PAD]
        gk_a, gk_b = gk_ref[:, 0:NOPE], gk_ref[:, NOPE:HEAD_PAD]

        dkpe = jnp.zeros((tm, LANES), F32)
        dcqn = jnp.zeros((tm, Q_LORA), F32)
        dckvn = jnp.zeros((tm, KV_LORA), F32)
        for hd in range(N_HEADS):
            c0 = hd * HEAD_PAD
            qh = _dot(cqn, w_uq_ref[:, c0:c0 + HEAD_PAD])
            a, b = qh[:, 0:NOPE], qh[:, NOPE:HEAD_PAD]
            r = lax.rsqrt((jnp.sum(a * a, axis=-1, keepdims=True) + jnp.sum(b * b, axis=-1, keepdims=True)) / QK_DIM + EPS)
            xa, xb = a * r, b * r
            dan = dq_ref[hd, :, 0:NOPE]
            dbr = dq_ref[hd, :, NOPE:HEAD_PAD]
            dbn = dbr * cos_b + _swap_rope_halves(dbr * sin_b)
            dgq_ref[:, 0:NOPE] += _colsum(dan * xa)
            dgq_ref[:, NOPE:HEAD_PAD] += _colsum(dbn * xb)
            dxa, dxb = dan * gq_a, dbn * gq_b
            cq = (jnp.sum(dxa * xa, axis=-1, keepdims=True) + jnp.sum(dxb * xb, axis=-1, keepdims=True)) / QK_DIM
            dqh = jnp.concatenate([r * (dxa - xa * cq), r * (dxb - xb * cq)], axis=-1).astype(BF16)
            dw_uq_ref[:, c0:c0 + HEAD_PAD] += _dot_tn(cqn, dqh)
            dcqn = dcqn + _dot_nt(dqh, w_uq_ref[:, c0:c0 + HEAD_PAD])
            kvh = _dot(ckvn, w_ukv_ref[:, c0:c0 + HEAD_PAD])
            ka = kvh[:, 0:NOPE]
            rk = lax.rsqrt((jnp.sum(ka * ka, axis=-1, keepdims=True) + kpe_ss) / QK_DIM + EPS)
            xka, xkb = ka * rk, kpe * rk
            dkan = dk_ref[hd, :, 0:NOPE]
            dkbr = dk_ref[hd, :, NOPE:HEAD_PAD]
            dkbn = dkbr * cos_b + _swap_rope_halves(dkbr * sin_b)
            dgk_ref[:, 0:NOPE] += _colsum(dkan * xka)
            dgk_ref[:, NOPE:HEAD_PAD] += _colsum(dkbn * xkb)
            dxka, dxkb = dkan * gk_a, dkbn * gk_b
            ck = (jnp.sum(dxka * xka, axis=-1, keepdims=True) + jnp.sum(dxkb * xkb, axis=-1, keepdims=True)) / QK_DIM
            dkpe = dkpe + rk * (dxkb - xkb * ck)
            dkvh = jnp.concatenate([rk * (dxka - xka * ck), dv_ref[hd]], axis=-1).astype(BF16)
            dw_ukv_ref[:, c0:c0 + HEAD_PAD] += _dot_tn(ckvn, dkvh)
            dckvn = dckvn + _dot_nt(dkvh, w_ukv_ref[:, c0:c0 + HEAD_PAD])

        dg_cq_ref[...] += _colsum(dcqn * xq)
        dxq = dcqn * g_cq
        dproj_ref[:, 0:Q_LORA] = (rq * (dxq - xq * (jnp.sum(dxq * xq, axis=-1, keepdims=True) / Q_LORA))).astype(BF16)
        dg_ckv_ref[...] += _colsum(dckvn * xkv)
        dxkv = dckvn * g_ckv
        dproj_ref[:, 256:384] = (rkv * (dxkv - xkv * (jnp.sum(dxkv * xkv, axis=-1, keepdims=True) / KV_LORA))).astype(BF16)
        dproj_ref[:, 384:512] = dkpe.astype(BF16)
        dproj_ref[:, 512:1536] = dtail_ref[:, 0:1024]
        dproj_ref[:, 2560:3072] = dtail_ref[:, 1024:1536]

        du_v = du_ref[...]
        not_last = jnp.where(i < nt - 1, 1.0, 0.0)
        nx0 = dun_ref[0:1, :] * not_last
        nx1 = dun_ref[1:2, :] * not_last
        row = lax.broadcasted_iota(jnp.int32, du_v.shape, 0)
        du1 = jnp.where(row == tm - 1, nx0, pltpu.roll(du_v, tm - 1, 0))
        du2 = jnp.where(row == tm - 2, nx0, jnp.where(row == tm - 1, nx1, pltpu.roll(du_v, tm - 2, 0)))
        dvc = cw_ref[2:3, :] * du_v + cw_ref[1:2, :] * du1 + cw_ref[0:1, :] * du2
        dproj_ref[:, 1536:2048] = (dvc * cx_ref[...]).astype(BF16)
        dproj_ref[:, 2048:2560] = (dvc * cc_ref[...]).astype(BF16)

        dh = jnp.zeros((tm, D_MODEL), F32)
        for c0 in range(0, PROJ_EXT, 512):
            dh = dh + _dot_nt(dproj_ref[:, c0:c0 + 512], w_in_ref[:, c0:c0 + 512])
        dg_in_ref[...] += _colsum(dh * xh0)
        dxh = dh * g_in
        gx_ref[...] = dx1_ref[...] + r0 * (dxh - xh0 * (jnp.sum(dxh * xh0, axis=-1, keepdims=True) / D_MODEL))

    row = lambda i: (i, 0)
    col = lambda c: (lambda i: (i, c))
    head_rows = lambda i: (0, i, 0)
    nxt = lambda i: (jnp.minimum((i + 1) * (tm // 8), T // 8 - 1), 0)
    in_specs = [pl.BlockSpec((tm, D_MODEL), row), pl.BlockSpec((tm, D_MODEL), row), pl.BlockSpec((tm, 1), row),
                pl.BlockSpec((tm, 512), col(0)), pl.BlockSpec((tm, 512), col(3)), pl.BlockSpec((tm, 512), col(4)),
                pl.BlockSpec((N_HEADS, tm, HEAD_PAD), head_rows), pl.BlockSpec((N_HEADS, tm, HEAD_PAD), head_rows),
                pl.BlockSpec((N_HEADS, tm, V_DIM), head_rows), pl.BlockSpec((tm, 1536), row),
                pl.BlockSpec((tm, CONV_W), row), pl.BlockSpec((8, CONV_W), nxt),
                _full((1, D_MODEL)), _full((D_MODEL, PROJ_EXT)), _full((1, Q_LORA)), _full((Q_LORA, N_HEADS * HEAD_PAD)),
                _full((1, KV_LORA)), _full((KV_LORA, N_HEADS * HEAD_PAD)), _full((1, HEAD_PAD)), _full((1, HEAD_PAD)),
                _full((3, CONV_W)), _full((1, LANES)), _full((1, LANES))]
    out_specs = [pl.BlockSpec((tm, D_MODEL), row), pl.BlockSpec((tm, D_MODEL), row), pl.BlockSpec((tm, PROJ_EXT), row),
                 _full((Q_LORA, N_HEADS * HEAD_PAD)), _full((KV_LORA, N_HEADS * HEAD_PAD)),
                 _full((1, D_MODEL)), _full((1, Q_LORA)), _full((1, KV_LORA)), _full((1, HEAD_PAD)), _full((1, HEAD_PAD))]
    out_shape = [jax.ShapeDtypeStruct((T, D_MODEL), F32), jax.ShapeDtypeStruct((T, D_MODEL), BF16),
                 jax.ShapeDtypeStruct((T, PROJ_EXT), BF16),
                 jax.ShapeDtypeStruct((Q_LORA, N_HEADS * HEAD_PAD), F32), jax.ShapeDtypeStruct((KV_LORA, N_HEADS * HEAD_PAD), F32),
                 jax.ShapeDtypeStruct((1, D_MODEL), F32), jax.ShapeDtypeStruct((1, Q_LORA), F32),
                 jax.ShapeDtypeStruct((1, KV_LORA), F32), jax.ShapeDtypeStruct((1, HEAD_PAD), F32),
                 jax.ShapeDtypeStruct((1, HEAD_PAD), F32)]
    return pl.pallas_call(
        body, name="bwd_proj", grid=(nt,), in_specs=in_specs, out_specs=out_specs, out_shape=out_shape,
        compiler_params=_params(dimension_semantics=("arbitrary",)),
    )(x, dx1, pos, proj, proj, proj, dq, dk, dv, dtail, du, du, g_in, w_in, g_cq, w_uq, g_ckv, w_ukv, gq, gk, conv_w,
      invf, sgn)


def _matmul_tn(a, b, tt, tn):
    T, M = a.shape
    N = b.shape[1]

    def body(a_ref, b_ref, o_ref):
        @pl.when(pl.program_id(1) == 0)
        def _():
            o_ref[...] = jnp.zeros_like(o_ref)

        o_ref[...] += _dot_tn(a_ref[...], b_ref[...])

    return pl.pallas_call(
        body, name="dw_in", grid=(N // tn, T // tt),
        in_specs=[pl.BlockSpec((tt, M), lambda j, t: (t, 0)), pl.BlockSpec((tt, tn), lambda j, t: (t, j))],
        out_specs=pl.BlockSpec((M, tn), lambda j, t: (0, j)),
        out_shape=jax.ShapeDtypeStruct((M, N), F32),
        compiler_params=_params(dimension_semantics=("arbitrary", "arbitrary")),
    )(a, b)


def _add_pair(grads, from_sibling, small, small_sibling, c):
    n = len(grads)

    def body(c_ref, *refs):
        ins, outs = refs[:2 * n + 2], refs[2 * n + 2:]
        for i in range(n + 1):
            outs[i][...] = (ins[2 * i][...] + ins[2 * i + 1][...]).astype(outs[i].dtype)

    in_specs, out_specs, out_shape, args = [], [], [], []
    for g, r in zip(grads, from_sibling):
        _, hr, cols = r.shape
        in_specs += [pl.BlockSpec((1, hr, cols), lambda k, c_ref: (k, c_ref[0], 0)),
                     pl.BlockSpec((1, hr, cols), lambda k, c_ref: (k, 0, 0))]
        out_specs.append(pl.BlockSpec((1, hr, cols), lambda k, c_ref: (k, 0, 0)))
        out_shape.append(jax.ShapeDtypeStruct(r.shape, BF16))
        args += [g, r]
    whole = pl.BlockSpec(small.shape, lambda k, c_ref: (0, 0))
    in_specs += [whole, whole]
    out_specs.append(whole)
    out_shape.append(jax.ShapeDtypeStruct(small.shape, F32))
    outs = pl.pallas_call(
        body, name="add_pair", out_shape=out_shape,
        grid_spec=pltpu.PrefetchScalarGridSpec(num_scalar_prefetch=1, grid=(N_CHIPS,), in_specs=in_specs,
                                               out_specs=out_specs),
        compiler_params=_params(dimension_semantics=("arbitrary",)),
    )(c.reshape(1), *args, small, small_sibling)
    return outs[:n], outs[n]


def _add_chips(parts, small_parts):
    arrays = list(parts) + [small_parts]

    def body(*refs):
        ins, outs = refs[:len(arrays)], refs[len(arrays):]
        for a_ref, o_ref in zip(ins, outs):
            part = lambda k: a_ref[k].astype(F32)
            o_ref[...] = ((part(0) + part(1)) + part(2)) + part(3)

    in_specs, out_specs, out_shape = [], [], []
    for a in arrays:
        _, rows, cols = a.shape
        in_specs.append(pl.BlockSpec((N_CHIPS, rows // 2, cols), lambda i: (0, i, 0)))
        out_specs.append(pl.BlockSpec((rows // 2, cols), lambda i: (i, 0)))
        out_shape.append(jax.ShapeDtypeStruct((rows, cols), F32))
    outs = pl.pallas_call(body, name="add_chips", grid=(2,), in_specs=in_specs, out_specs=out_specs,
                          out_shape=out_shape, compiler_params=_params(dimension_semantics=("arbitrary",)))(*arrays)
    return outs[:-1], outs[-1]


def _adamw(w, g, m, v, name):
    rows, cols = w.shape
    rb = 256 if rows * cols > 512 * 1024 else rows

    def body(w_ref, g_ref, m_ref, v_ref, d_ref, nm_ref, nv_ref):
        _adamw_math(g_ref[...], w_ref, m_ref, v_ref, d_ref, nm_ref, nv_ref)

    spec = pl.BlockSpec((rb, cols), lambda i: (i, 0))
    shp = jax.ShapeDtypeStruct(w.shape, F32)
    return pl.pallas_call(body, name=name, grid=(rows // rb,), in_specs=[spec] * 4, out_specs=[spec] * 3,
                          out_shape=[shp] * 3)(w, g, m, v)


def _adamw_math(gv, w_ref, m_ref, v_ref, d_ref, nm_ref, nv_ref):
    nm = B1 * m_ref[...] + (1.0 - B1) * gv
    nv = B2 * v_ref[...] + (1.0 - B2) * (gv * gv)
    m_hat = nm / (1.0 - B1 ** STEP)
    v_hat = nv / (1.0 - B2 ** STEP)
    d_ref[...] = -LR * (m_hat / (jnp.sqrt(v_hat) + ADAM_EPS) + WD * w_ref[...])
    nm_ref[...] = nm
    nv_ref[...] = nv


def _adamw_halves(w, mine, other, m, v, c, name):
    hr, cols = mine.shape

    def body(c_ref, w_ref, mine_ref, other_ref, m_ref, v_ref, g_ref, d_ref, nm_ref, nv_ref):
        gv = jnp.where(pl.program_id(0) == c_ref[0], mine_ref[...], other_ref[...])
        g_ref[...] = gv
        _adamw_math(gv, w_ref, m_ref, v_ref, d_ref, nm_ref, nv_ref)

    half = pl.BlockSpec((hr, cols), lambda i, c_ref: (i, 0))
    whole = pl.BlockSpec((hr, cols), lambda i, c_ref: (0, 0))
    shp = jax.ShapeDtypeStruct(w.shape, F32)
    return pl.pallas_call(
        body, name=name, out_shape=[shp] * 4,
        grid_spec=pltpu.PrefetchScalarGridSpec(num_scalar_prefetch=1, grid=(2,), in_specs=[half, whole, whole, half, half],
                                               out_specs=[half] * 4),
        compiler_params=_params(dimension_semantics=("arbitrary",)),
    )(c.reshape(1), w, mine, other, m, v)


_ANY = pl.BlockSpec(memory_space=pl.ANY)


def _mesh_pos():
    return lax.axis_index("x"), lax.axis_index("y"), lax.axis_index("c")


def _other_chips(x, y):
    return [(1 - x, y), (x, 1 - y), (1 - x, 1 - y)]


def _remote(src, dst, send_sems, recv_sems, k, to):
    return pltpu.make_async_remote_copy(src_ref=src, dst_ref=dst, send_sem=send_sems.at[k], recv_sem=recv_sems.at[k],
                                        device_id=to, device_id_type=MESH)


def _gather_weights(shards):
    n = len(shards)
    halved = [s.shape[0] % 32 == 0 for s in shards]

    def body(*refs):
        ins, outs, stage = refs[:n], refs[n:2 * n], refs[2 * n:3 * n]
        send_sems, recv_sems, local_sems = refs[3 * n:]
        x, y, c = _mesh_pos()
        me = 2 * x + y
        chips = _other_chips(x, y)

        def part(i, ref, hc):
            if not halved[i]:
                return ref
            hr = shards[i].shape[0] // 2
            return ref.at[pl.ds(hc * hr, hr), :]

        locals_, started = [], []
        for i in range(n):
            stage[i][...] = ins[i][...].astype(BF16)
            mine = pltpu.make_async_copy(stage[i], outs[i].at[me], local_sems.at[i])
            mine.start()
            locals_.append(mine)
            for j, (cx, cy) in enumerate(chips):
                cp = _remote(part(i, stage[i], c), part(i, outs[i].at[me], c), send_sems, recv_sems, 6 * i + j, (cx, cy, c))
                cp.start()
                started.append(cp)
        for i in range(n):
            for j, (cx, cy) in enumerate(chips):
                got = part(i, outs[i].at[2 * cx + cy], c)
                _remote(got, got, send_sems, recv_sems, 6 * i + j, (cx, cy, c)).wait_recv()
                if halved[i]:
                    fwd = _remote(got, got, send_sems, recv_sems, 6 * i + 3 + j, (x, y, 1 - c))
                    fwd.start()
                    started.append(fwd)
        for i in range(n):
            if halved[i]:
                for j, (cx, cy) in enumerate(chips):
                    got = part(i, outs[i].at[2 * cx + cy], 1 - c)
                    _remote(got, got, send_sems, recv_sems, 6 * i + 3 + j, (x, y, 1 - c)).wait_recv()
        for cp in started:
            cp.wait_send()
        for cp in locals_:
            cp.wait()

    vmem = pl.BlockSpec(memory_space=pltpu.VMEM)
    return pl.pallas_call(
        body, name="gather_weights", in_specs=[vmem] * n, out_specs=[_ANY] * n,
        out_shape=[jax.ShapeDtypeStruct((N_CHIPS,) + s.shape, BF16) for s in shards],
        scratch_shapes=[pltpu.VMEM(s.shape, BF16) for s in shards]
                       + [pltpu.SemaphoreType.DMA((6 * n,)), pltpu.SemaphoreType.DMA((6 * n,)), pltpu.SemaphoreType.DMA((n,))],
        compiler_params=_params(),
    )(*shards)


def _swap_halves(grads, small):
    n = len(grads)
    arrays = list(grads) + [small]

    def body(*refs):
        ins, outs, send_sems, recv_sems = refs[:n + 1], refs[n + 1:2 * n + 2], refs[2 * n + 2], refs[2 * n + 3]
        x, y, c = _mesh_pos()
        cps = []
        for i in range(n + 1):
            src = ins[i]
            if i < n:
                hr = grads[i].shape[1] // 2
                src = src.at[:, pl.ds((1 - c) * hr, hr), :]
            cp = _remote(src, outs[i], send_sems, recv_sems, i, (x, y, 1 - c))
            cp.start()
            cps.append(cp)
        for cp in cps:
            cp.wait()

    out_shape = [jax.ShapeDtypeStruct((g.shape[0], g.shape[1] // 2, g.shape[2]), F32) for g in grads]
    out_shape.append(jax.ShapeDtypeStruct(small.shape, F32))
    outs = pl.pallas_call(
        body, name="pair_grads", in_specs=[_ANY] * (n + 1), out_specs=[_ANY] * (n + 1), out_shape=out_shape,
        scratch_shapes=[pltpu.SemaphoreType.DMA((n + 1,)), pltpu.SemaphoreType.DMA((n + 1,))],
    )(*arrays)
    return outs[:n], outs[n]


def _scatter_to_chips(parts, small):
    n = len(parts)
    arrays = list(parts) + [small]

    def body(*refs):
        ins, outs = refs[:n + 1], refs[n + 1:2 * n + 2]
        send_sems, recv_sems, local_sems = refs[2 * n + 2:]
        x, y, c = _mesh_pos()
        me = 2 * x + y
        chips = _other_chips(x, y)
        locals_, sends = [], []
        for i in range(n + 1):
            mine = pltpu.make_async_copy(ins[i].at[me] if i < n else ins[i], outs[i].at[me], local_sems.at[i])
            mine.start()
            locals_.append(mine)
            for j, (cx, cy) in enumerate(chips):
                src = ins[i].at[2 * cx + cy] if i < n else ins[i]
                cp = _remote(src, outs[i].at[me], send_sems, recv_sems, 3 * i + j, (cx, cy, c))
                cp.start()
                sends.append(cp)
        for i in range(n + 1):
            for j, (cx, cy) in enumerate(chips):
                got = outs[i].at[2 * cx + cy]
                _remote(got, got, send_sems, recv_sems, 3 * i + j, (cx, cy, c)).wait_recv()
        for cp in sends:
            cp.wait_send()
        for cp in locals_:
            cp.wait()

    out_shape = [jax.ShapeDtypeStruct(p.shape, p.dtype) for p in parts]
    out_shape.append(jax.ShapeDtypeStruct((N_CHIPS,) + small.shape, small.dtype))
    outs = pl.pallas_call(
        body, name="scatter_grads", in_specs=[_ANY] * (n + 1), out_specs=[_ANY] * (n + 1), out_shape=out_shape,
        scratch_shapes=[pltpu.SemaphoreType.DMA((3 * n + 3,)), pltpu.SemaphoreType.DMA((3 * n + 3,)),
                        pltpu.SemaphoreType.DMA((n + 1,))],
    )(*arrays)
    return outs[:n], outs[n]


def _share_halves(halves):
    n = len(halves)

    def body(*refs):
        ins, outs, send_sems, recv_sems = refs[:n], refs[n:2 * n], refs[2 * n], refs[2 * n + 1]
        x, y, c = _mesh_pos()
        cps = [_remote(ins[i], outs[i], send_sems, recv_sems, i, (x, y, 1 - c)) for i in range(n)]
        for cp in cps:
            cp.start()
        for cp in cps:
            cp.wait()

    return pl.pallas_call(
        body, name="share_halves", in_specs=[_ANY] * n, out_specs=[_ANY] * n,
        out_shape=[jax.ShapeDtypeStruct(h.shape, h.dtype) for h in halves],
        scratch_shapes=[pltpu.SemaphoreType.DMA((n,)), pltpu.SemaphoreType.DMA((n,))],
    )(*halves)


SHARD_COLS_IN = IN_TOTAL // N_CHIPS
KPE_END = Q_LORA + KV_LORA + ROPE


def _assemble_weights(c_in, c_uq, c_ukv, c_o, c_pl, c_plg, c_conv):
    by_cols = lambda a: a.transpose(1, 0, 2).reshape(a.shape[1], N_CHIPS * a.shape[2])
    w_in_e = jnp.concatenate([c_in[0][:, :KPE_END], jnp.zeros((D_MODEL, 64), BF16), c_in[0][:, KPE_END:],
                              c_in[1], c_in[2], c_in[3]], axis=1)
    w_uq_e = by_cols(jnp.pad(c_uq, ((0, 0), (0, 0), (0, HEAD_PAD - QK_DIM))))
    return (w_in_e, w_uq_e, by_cols(c_ukv), by_cols(c_conv).astype(F32), c_o.reshape(D_MODEL, D_MODEL),
            by_cols(c_pl), c_plg.reshape(D_MODEL, D_MODEL))


def _split_grads(dw_in_e, dw_uq_e, dw_ukv, dw_o, dw_pl, dw_plg):
    chip_major = lambda a: a.reshape(a.shape[0], N_CHIPS, a.shape[1] // N_CHIPS).transpose(1, 0, 2)
    first = jnp.concatenate([dw_in_e[:, :KPE_END], dw_in_e[:, KPE_END + 64:SHARD_COLS_IN + 64]], axis=1)
    rest = [dw_in_e[:, SHARD_COLS_IN * k + 64:SHARD_COLS_IN * (k + 1) + 64] for k in range(1, N_CHIPS)]
    return [jnp.stack([first] + rest), chip_major(dw_uq_e)[:, :, :QK_DIM], chip_major(dw_ukv),
            dw_o.reshape(N_CHIPS, D_MODEL // N_CHIPS, D_MODEL), chip_major(dw_pl),
            dw_plg.reshape(N_CHIPS, D_MODEL // N_CHIPS, D_MODEL)]


def _local_step(x, p, pos, tgt, gains, w_in_e, w_uq_e, w_ukv, conv_w, w_o, w_pl, w_plg, tm, tq):
    g_in, g_cq, g_ckv, g_q, g_k, g_oa, g_oc, g_pl = gains
    T = x.shape[0]
    zpad = lambda a, n: jnp.concatenate([a, jnp.zeros(a.shape[:-1] + (n,), a.dtype)], axis=-1)
    gq, gk = zpad(g_q, HEAD_PAD - QK_DIM), zpad(g_k, HEAD_PAD - QK_DIM)
    inv_freq = 1.0 / (ROPE_THETA ** (jnp.arange(0, ROPE, 2, dtype=F32) / ROPE))
    invf = jnp.concatenate([inv_freq, inv_freq, jnp.zeros((64,), F32)]).reshape(1, LANES)
    sgn = jnp.concatenate([-jnp.ones((32,), F32), jnp.ones((32,), F32), jnp.zeros((64,), F32)]).reshape(1, LANES)

    proj, q, k, v = _fwd_proj(x, pos, g_in, w_in_e, g_cq, w_uq_e, g_ckv, w_ukv, gq, gk, invf, sgn, tm)
    o, lse = _attn_fwd(q, k, v, tq)
    (dx1, do, delta, dtail, du, dw_o, dw_pl, dw_plg, dg_oa, dg_oc, dg_pl, dconv, loss) = _tail(
        x, o, proj, p, tgt, g_oa, g_oc, g_pl, conv_w, w_o, w_pl, w_plg, tm)
    dq = _attn_dq(q, k, v, do, lse, delta, tq)
    lse_row = lse[:, :, 0].reshape(N_HEADS, 1, T)
    delta_row = delta[:, :, 0].reshape(N_HEADS, 1, T)
    dk, dv = _attn_dkv(q, k, v, do, lse_row, delta_row, tq)
    (gx, h, dproj, dw_uq_e, dw_ukv, dg_in, dg_cq, dg_ckv, dgq, dgk) = _bwd_proj(
        x, dx1, pos, proj, dq, dk, dv, dtail, du, g_in, w_in_e, g_cq, w_uq_e, g_ckv, w_ukv, gq, gk, conv_w, invf, sgn, tm)
    dw_in_e = _matmul_tn(h, dproj, min(512, T), 512)
    wgrads = (dw_in_e, dw_uq_e, dw_ukv, dw_o, dw_pl, dw_plg)
    ggrads = (dg_in, dg_cq, dg_ckv, dgq, dgk, dg_oa, dg_oc, dg_pl)
    return loss, gx, wgrads, ggrads, dconv


def kernel(x, p, positions, g_in, w_in, g_cq, w_uq, g_ckv, w_ukv, g_q, g_k, conv_w, g_oa, g_oc, w_o, w_pl, w_plg, g_pl, loss_target, m_g_in, m_w_in, m_g_cq, m_w_uq, m_g_ckv, m_w_ukv, m_g_q, m_g_k, m_conv_w, m_g_oa, m_g_oc, m_w_o, m_w_pl, m_w_plg, m_g_pl, v_g_in, v_w_in, v_g_cq, v_w_uq, v_g_ckv, v_w_ukv, v_g_q, v_g_k, v_conv_w, v_g_oa, v_g_oc, v_w_o, v_w_pl, v_w_plg, v_g_pl):
    T = x.shape[1]
    c = lax.axis_index("c")
    chip = 2 * lax.axis_index("x") + lax.axis_index("y")
    gains = [g.reshape(1, -1) for g in (g_in, g_cq, g_ckv, g_q, g_k, g_oa, g_oc, g_pl)]

    gathered = _gather_weights([w_in[0], w_uq[0], w_ukv[0], w_o[0], w_pl[0], w_plg[0], conv_w[0]])
    full = _assemble_weights(*gathered)

    loss, gx, wgrads, ggrads, dconv = _local_step(
        x[0], p[0, 0], positions.reshape(T, 1), loss_target[0], gains, *full, 256, 512)

    grads_cm = _split_grads(*wgrads)
    small_parts = [a.reshape(-1, LANES) for a in (*ggrads, loss, dconv)]
    small_rows = [a.shape[0] for a in small_parts]
    tile_rows = [-(-r // 8) * 8 for r in small_rows]
    tile_rows[-1] += -sum(tile_rows) % 16
    small = jnp.concatenate([jnp.pad(a, ((0, t - r), (0, 0))) for a, r, t in zip(small_parts, small_rows, tile_rows)])
    from_sibling, small_sibling = _swap_halves(grads_cm, small)
    chip_parts, chip_small = _add_pair(grads_cm, from_sibling, small, small_sibling, c)
    by_chip, small_by_chip = _scatter_to_chips(chip_parts, chip_small)
    halves, small_total = _add_chips(by_chip, small_by_chip)
    other_halves = _share_halves(halves)

    gg, off = [], 0
    for rows, tiled in zip(small_rows, tile_rows):
        gg.append(small_total[off:off + rows].reshape(1, -1))
        off += tiled
    loss_out = gg[8][0, 0]
    conv_total = gg[9].reshape(3, CONV_W)
    conv_g = lax.dynamic_slice(conv_total, (0, chip * (CONV_W // N_CHIPS)), (3, CONV_W // N_CHIPS))
    g_by_name = dict(g_in=gg[0], g_cq=gg[1], g_ckv=gg[2], g_q=gg[3][:, :QK_DIM], g_k=gg[4][:, :QK_DIM], conv_w=conv_g,
                     g_oa=gg[5], g_oc=gg[6], g_pl=gg[7])
    half_by_name = dict(zip(("w_in", "w_uq", "w_ukv", "w_o", "w_pl", "w_plg"), zip(halves, other_halves)))
    weights = dict(g_in=g_in, w_in=w_in, g_cq=g_cq, w_uq=w_uq, g_ckv=g_ckv, w_ukv=w_ukv, g_q=g_q, g_k=g_k,
                   conv_w=conv_w, g_oa=g_oa, g_oc=g_oc, w_o=w_o, w_pl=w_pl, w_plg=w_plg, g_pl=g_pl)
    ms = dict(g_in=m_g_in, w_in=m_w_in, g_cq=m_g_cq, w_uq=m_w_uq, g_ckv=m_g_ckv, w_ukv=m_w_ukv, g_q=m_g_q, g_k=m_g_k,
              conv_w=m_conv_w, g_oa=m_g_oa, g_oc=m_g_oc, w_o=m_w_o, w_pl=m_w_pl, w_plg=m_w_plg, g_pl=m_g_pl)
    vs = dict(g_in=v_g_in, w_in=v_w_in, g_cq=v_g_cq, w_uq=v_w_uq, g_ckv=v_g_ckv, w_ukv=v_w_ukv, g_q=v_g_q, g_k=v_g_k,
              conv_w=v_conv_w, g_oa=v_g_oa, g_oc=v_g_oc, w_o=v_w_o, w_pl=v_w_pl, w_plg=v_w_plg, g_pl=v_g_pl)
    names = list(weights)
    grads, deltas, new_m, new_v = [], [], [], []
    for n in names:
        w = weights[n]
        w2 = w.reshape(-1, w.shape[-1])
        if n in half_by_name:
            g2, d, nm, nv = _adamw_halves(w2, *half_by_name[n], ms[n].reshape(w2.shape), vs[n].reshape(w2.shape), c,
                                          "adamw_" + n)
        else:
            g2 = g_by_name[n].reshape(w2.shape)
            d, nm, nv = _adamw(w2, g2, ms[n].reshape(w2.shape), vs[n].reshape(w2.shape), "adamw_" + n)
        grads.append(g2.reshape(w.shape))
        deltas.append(d.reshape(w.shape))
        new_m.append(nm.reshape(w.shape))
        new_v.append(nv.reshape(w.shape))
    return (loss_out, gx.reshape(x.shape), *grads, *deltas, *new_m, *new_v)
```

```python
import functools
import math

import jax
import jax.numpy as jnp
from jax import lax
from jax.experimental import pallas as pl
from jax.experimental.pallas import tpu as pltpu

F32 = jnp.float32
BF16 = jnp.bfloat16

D_MODEL = 1024
N_HEADS = 4
NOPE = 128
ROPE = 64
V_DIM = 128
QK_DIM = NOPE + ROPE
HEAD_PAD = 256
Q_LORA = 256
KV_LORA = 128
ATTN_W = 512
CONV_W = 512
PLE = 256
IN_TOTAL = 3008
PROJ_EXT = 3072
ROPE_THETA = 10000.0
EPS = 1e-6
SCALE = 1.0 / math.sqrt(QK_DIM)
LOG2E = math.log2(math.e)
EXP2_SCALE = SCALE * LOG2E
NEG = -1e30
SOFTMAX_ROWS = 32

LR, B1, B2, ADAM_EPS, WD, STEP = 0.001, 0.9, 0.999, 1e-08, 0.01, 10

N_CHIPS = 4
LANES = 128
VMEM_LIMIT = 56 * 1024 * 1024
MESH = pl.DeviceIdType.MESH


def _params(**kw):
    return pltpu.CompilerParams(vmem_limit_bytes=VMEM_LIMIT, **kw)


def _inv_rms(x, n):
    return lax.rsqrt(jnp.sum(x * x, axis=-1, keepdims=True) / n + EPS)


def _sigmoid(z):
    return 1.0 / (1.0 + jnp.exp(-z))


def _swap_rope_halves(b):
    lane = lax.broadcasted_iota(jnp.int32, b.shape, 1)
    swapped = jnp.where(lane < 32, pltpu.roll(b, 96, 1), pltpu.roll(b, 32, 1))
    return jnp.where(lane < ROPE, swapped, 0.0)


def _dot(a, b):
    return jnp.dot(a, b, preferred_element_type=F32)


def _dot_nt(a, b):
    return lax.dot_general(a, b, (((1,), (1,)), ((), ())), preferred_element_type=F32)


def _dot_tn(a, b):
    return lax.dot_general(a, b, (((0,), (0,)), ((), ())), preferred_element_type=F32)


def _colsum(a):
    return jnp.sum(a, axis=0, keepdims=True)


def _full(shape):
    return pl.BlockSpec(shape, lambda *_: (0,) * len(shape))


def _rope_tables(pos_ref, invf_ref, sgn_ref):
    ang = pos_ref[...].astype(F32) * invf_ref[...]
    return jnp.cos(ang), jnp.sin(ang) * sgn_ref[...]


def _fwd_proj(x, pos, g_in, w_in, g_cq, w_uq, g_ckv, w_ukv, gq, gk, invf, sgn, tm):
    T = x.shape[0]

    def body(x_ref, pos_ref, g_in_ref, w_in_ref, g_cq_ref, w_uq_ref, g_ckv_ref, w_ukv_ref, gq_ref, gk_ref,
             invf_ref, sgn_ref, proj_ref, q_ref, k_ref, v_ref):
        xv = x_ref[...]
        h = (xv * _inv_rms(xv, D_MODEL) * g_in_ref[...]).astype(BF16)
        for c0 in range(0, PROJ_EXT, 512):
            proj_ref[:, c0:c0 + 512] = _dot(h, w_in_ref[:, c0:c0 + 512])
        c_q = proj_ref[:, 0:Q_LORA]
        cqn = (c_q * _inv_rms(c_q, Q_LORA) * g_cq_ref[...]).astype(BF16)
        c_kv = proj_ref[:, Q_LORA:Q_LORA + KV_LORA]
        ckvn = (c_kv * _inv_rms(c_kv, KV_LORA) * g_ckv_ref[...]).astype(BF16)
        kpe = proj_ref[:, 384:512]
        kpe_ss = jnp.sum(kpe * kpe, axis=-1, keepdims=True)
        cos_b, sin_b = _rope_tables(pos_ref, invf_ref, sgn_ref)
        gq_a, gq_b = gq_ref[:, 0:NOPE], gq_ref[:, NOPE:HEAD_PAD]
        gk_a, gk_b = gk_ref[:, 0:NOPE], gk_ref[:, NOPE:HEAD_PAD]
        for hd in range(N_HEADS):
            c0 = hd * HEAD_PAD
            qh = _dot(cqn, w_uq_ref[:, c0:c0 + HEAD_PAD])
            a, b = qh[:, 0:NOPE], qh[:, NOPE:HEAD_PAD]
            r = lax.rsqrt((jnp.sum(a * a, axis=-1, keepdims=True) + jnp.sum(b * b, axis=-1, keepdims=True)) / QK_DIM + EPS)
            bn = b * r * gq_b
            q_ref[hd, :, 0:NOPE] = (a * r * gq_a).astype(BF16)
            q_ref[hd, :, NOPE:HEAD_PAD] = (bn * cos_b + _swap_rope_halves(bn) * sin_b).astype(BF16)
            kvh = _dot(ckvn, w_ukv_ref[:, c0:c0 + HEAD_PAD])
            ka = kvh[:, 0:NOPE]
            rk = lax.rsqrt((jnp.sum(ka * ka, axis=-1, keepdims=True) + kpe_ss) / QK_DIM + EPS)
            kbn = kpe * rk * gk_b
            k_ref[hd, :, 0:NOPE] = (ka * rk * gk_a).astype(BF16)
            k_ref[hd, :, NOPE:HEAD_PAD] = (kbn * cos_b + _swap_rope_halves(kbn) * sin_b).astype(BF16)
            v_ref[hd, :, 0:V_DIM] = kvh[:, NOPE:HEAD_PAD].astype(BF16)
            v_ref[hd, :, V_DIM:2 * V_DIM] = jnp.ones((tm, V_DIM), BF16)

    row = lambda i: (i, 0)
    head_rows = lambda i: (0, i, 0)
    return pl.pallas_call(
        body, name="fwd_proj", grid=(T // tm,),
        in_specs=[pl.BlockSpec((tm, D_MODEL), row), pl.BlockSpec((tm, 1), row), _full((1, D_MODEL)),
                  _full((D_MODEL, PROJ_EXT)), _full((1, Q_LORA)), _full((Q_LORA, N_HEADS * HEAD_PAD)),
                  _full((1, KV_LORA)), _full((KV_LORA, N_HEADS * HEAD_PAD)), _full((1, HEAD_PAD)), _full((1, HEAD_PAD)),
                  _full((1, LANES)), _full((1, LANES))],
        out_specs=[pl.BlockSpec((tm, PROJ_EXT), row), pl.BlockSpec((N_HEADS, tm, HEAD_PAD), head_rows),
                   pl.BlockSpec((N_HEADS, tm, HEAD_PAD), head_rows), pl.BlockSpec((N_HEADS, tm, 2 * V_DIM), head_rows)],
        out_shape=[jax.ShapeDtypeStruct((T, PROJ_EXT), F32), jax.ShapeDtypeStruct((N_HEADS, T, HEAD_PAD), BF16),
                   jax.ShapeDtypeStruct((N_HEADS, T, HEAD_PAD), BF16), jax.ShapeDtypeStruct((N_HEADS, T, 2 * V_DIM), BF16)],
        compiler_params=_params(dimension_semantics=("arbitrary",)),
    )(x, pos, g_in, w_in, g_cq, w_uq, g_ckv, w_ukv, gq, gk, invf, sgn)


def _chunk_pipeline(n_loop, matmuls, pointwise, accumulate, last):
    def iteration(t, slot):
        matmuls(t + 1, 1 - slot)
        accumulate(jnp.maximum(t - 1, 0), 1 - slot)
        pointwise(t, slot, False)

    def finish(slot):
        accumulate(jnp.maximum(n_loop - 1, 0), 1 - slot)
        pointwise(n_loop, slot, True)
        accumulate(n_loop, slot)
        last()

    matmuls(0, 0)

    def pair(tt, carry):
        iteration(2 * tt, 0)
        iteration(2 * tt + 1, 1)
        return carry

    lax.fori_loop(0, n_loop // 2, pair, 0)
    odd = lax.rem(n_loop, 2) == 1

    @pl.when(odd)
    def _():
        iteration(n_loop - 1, 0)
        finish(1)

    @pl.when(jnp.logical_not(odd))
    def _():
        finish(0)


def _attn_fwd(q, k, v, tq):
    T = q.shape[1]
    tk = tq
    rc = min(SOFTMAX_ROWS, tq)

    def body(q_ref, k_ref, v_ref, o_ref, lse_ref, s0, s1, p0, p1, a0, a1, m_ref, acc_ref):
        qi = pl.program_id(1)
        s_buf, p_buf, a_buf = (s0, s1), (p0, p1), (a0, a1)

        def scores(t, slot):
            ks = pl.multiple_of(t * tk, tk)
            s_buf[slot][...] = _dot_nt(q_ref[0], k_ref[0, pl.ds(ks, tk), :])

        def values(t, slot):
            ks = pl.multiple_of(t * tk, tk)
            acc_ref[...] = acc_ref[...] * a_buf[slot][...] + _dot(p_buf[slot][...], v_ref[0, pl.ds(ks, tk), :])

        def softmax(t, slot, masked):
            s_all = s_buf[slot][...]
            if masked:
                row = lax.broadcasted_iota(jnp.int32, (tq, tk), 0)
                col = lax.broadcasted_iota(jnp.int32, (tq, tk), 1)
                s_all = jnp.where(col <= row, s_all, NEG)
                s_buf[slot][...] = s_all
            m_old = m_ref[...]
            m_new = jnp.maximum(m_old, jnp.max(s_all, axis=1, keepdims=True))
            a_buf[slot][...] = jnp.exp2((m_old - m_new) * EXP2_SCALE)
            m_ref[...] = m_new
            for r0 in range(0, tq, rc):
                s = s_buf[slot][r0:r0 + rc, :]
                p_buf[slot][r0:r0 + rc, :] = jnp.exp2((s - m_new[r0:r0 + rc, :]) * EXP2_SCALE).astype(BF16)

        def last():
            l = acc_ref[:, V_DIM:2 * V_DIM]
            o_ref[...] = acc_ref[:, 0:V_DIM] / l
            lse_ref[0] = m_ref[...] * SCALE + jnp.log(l)

        m_ref[...] = jnp.full_like(m_ref, NEG)
        acc_ref[...] = jnp.zeros_like(acc_ref)
        p1[...] = jnp.zeros_like(p1)
        a1[...] = jnp.ones_like(a1)
        _chunk_pipeline(qi, scores, softmax, values, last)

    return pl.pallas_call(
        body, name="attn_fwd", grid=(N_HEADS, T // tq),
        in_specs=[pl.BlockSpec((1, tq, HEAD_PAD), lambda h, i: (h, i, 0)),
                  pl.BlockSpec((1, T, HEAD_PAD), lambda h, i: (h, 0, 0)),
                  pl.BlockSpec((1, T, 2 * V_DIM), lambda h, i: (h, 0, 0))],
        out_specs=[pl.BlockSpec((tq, V_DIM), lambda h, i: (i, h)),
                   pl.BlockSpec((1, tq, LANES), lambda h, i: (h, i, 0))],
        out_shape=[jax.ShapeDtypeStruct((T, ATTN_W), F32), jax.ShapeDtypeStruct((N_HEADS, T, LANES), F32)],
        scratch_shapes=[pltpu.VMEM((tq, tk), F32), pltpu.VMEM((tq, tk), F32), pltpu.VMEM((tq, tk), BF16),
                        pltpu.VMEM((tq, tk), BF16), pltpu.VMEM((tq, 1), F32), pltpu.VMEM((tq, 1), F32),
                        pltpu.VMEM((tq, 1), F32), pltpu.VMEM((tq, 2 * V_DIM), F32)],
        compiler_params=_params(dimension_semantics=("arbitrary", "arbitrary")),
    )(q, k, v)


def _tail(x, o, proj, p, tgt, g_oa, g_oc, g_pl, conv_w, w_o, w_pl, w_plg, tm):
    T = x.shape[0]
    nt = T // tm

    def body(x_ref, o_ref, za_ref, cb_ref, cc_ref, cx_ref, zc_ref, cch_ref, cxh_ref, p_ref, tgt_ref,
             g_oa_ref, g_oc_ref, g_pl_ref, cw_ref, w_o_ref, w_pl_ref, w_plg_ref,
             dx1_ref, do_ref, delta_ref, dtail_ref, du_ref,
             dw_o_ref, dw_pl_ref, dw_plg_ref, dg_oa_ref, dg_oc_ref, dg_pl_ref, dcw_ref, loss_ref):
        i = pl.program_id(0)

        @pl.when(i == 0)
        def _():
            for r in (dw_o_ref, dw_pl_ref, dw_plg_ref, dg_oa_ref, dg_oc_ref, dg_pl_ref, dcw_ref, loss_ref):
                r[...] = jnp.zeros_like(r)

        xv, ov, za, cb, zc = x_ref[...], o_ref[...], za_ref[...], cb_ref[...], zc_ref[...]
        g_oa, g_oc, g_pl = g_oa_ref[...], g_oc_ref[...], g_pl_ref[...]
        w0, w1, w2 = cw_ref[0:1, :], cw_ref[1:2, :], cw_ref[2:3, :]

        sa = _sigmoid(za)
        silu_a = za * sa
        ga = ov * silu_a
        ra = _inv_rms(ga, ATTN_W)
        xa = ga * ra
        ya = xa * g_oa
        v = cc_ref[...] * cx_ref[...]
        not_first = jnp.where(i > 0, 1.0, 0.0)
        hv6 = cch_ref[6:7, :] * cxh_ref[6:7, :] * not_first
        hv7 = cch_ref[7:8, :] * cxh_ref[7:8, :] * not_first
        row = lax.broadcasted_iota(jnp.int32, v.shape, 0)
        v1 = jnp.where(row == 0, hv7, pltpu.roll(v, 1, 0))
        v2 = jnp.where(row == 0, hv6, jnp.where(row == 1, hv7, pltpu.roll(v, 2, 0)))
        u = w0 * v2 + w1 * v1 + w2 * v
        sc = _sigmoid(zc)
        silu_c = zc * sc
        gc = cb * u * silu_c
        rc = _inv_rms(gc, CONV_W)
        xc = gc * rc
        yc = xc * g_oc
        ycat = jnp.concatenate([ya, yc], axis=-1).astype(BF16)
        x1 = xv + _dot(ycat, w_o_ref[...])
        r1 = _inv_rms(x1, D_MODEL)
        xh1 = x1 * r1
        n1 = (xh1 * g_pl).astype(BF16)
        gate = _sigmoid(_dot(n1, w_plg_ref[...]))
        pb = p_ref[...].astype(BF16)
        pp = _dot(pb, w_pl_ref[...])
        err = x1 + gate * pp - tgt_ref[...]
        loss_ref[...] += 0.5 * jnp.sum(err * err) / D_MODEL
        dy = err / D_MODEL

        dpp = (dy * gate).astype(BF16)
        da = (dy * pp * gate * (1.0 - gate)).astype(BF16)
        dw_pl_ref[...] += _dot_tn(pb, dpp)
        dw_plg_ref[...] += _dot_tn(n1, da)
        dn1 = _dot_nt(da, w_plg_ref[...])
        dg_pl_ref[...] += _colsum(dn1 * xh1)
        dxh = dn1 * g_pl
        dx1 = dy + r1 * (dxh - xh1 * (jnp.sum(dxh * xh1, axis=-1, keepdims=True) / D_MODEL))
        dx1_ref[...] = dx1
        dx1b = dx1.astype(BF16)
        dw_o_ref[...] += _dot_tn(ycat, dx1b)
        dycat = _dot_nt(dx1b, w_o_ref[...])
        dya, dyc = dycat[:, 0:ATTN_W], dycat[:, ATTN_W:D_MODEL]

        dg_oa_ref[...] += _colsum(dya * xa)
        dxa = dya * g_oa
        dga = ra * (dxa - xa * (jnp.sum(dxa * xa, axis=-1, keepdims=True) / ATTN_W))
        do = (dga * silu_a).astype(BF16)
        do_ref[...] = do
        dof = do.astype(F32) * ov
        for hd in range(N_HEADS):
            dl = jnp.sum(dof[:, hd * V_DIM:(hd + 1) * V_DIM], axis=-1, keepdims=True)
            delta_ref[hd] = jnp.broadcast_to(dl, (tm, LANES))
        dtail_ref[:, 0:512] = (dga * ov * (sa * (1.0 + za * (1.0 - sa)))).astype(BF16)

        dg_oc_ref[...] += _colsum(dyc * xc)
        dxc = dyc * g_oc
        dgc = rc * (dxc - xc * (jnp.sum(dxc * xc, axis=-1, keepdims=True) / CONV_W))
        dtail_ref[:, 512:1024] = (dgc * u * silu_c).astype(BF16)
        du = dgc * cb * silu_c
        du_ref[...] = du
        dtail_ref[:, 1024:1536] = (dgc * cb * u * (sc * (1.0 + zc * (1.0 - sc)))).astype(BF16)
        dcw_ref[0:1, :] += _colsum(du * v2)
        dcw_ref[1:2, :] += _colsum(du * v1)
        dcw_ref[2:3, :] += _colsum(du * v)

    row = lambda i: (i, 0)
    col = lambda c: (lambda i: (i, c))
    halo = lambda c: (lambda i: (jnp.maximum(i * (tm // 8) - 1, 0), c))
    in_specs = [pl.BlockSpec((tm, D_MODEL), row), pl.BlockSpec((tm, ATTN_W), row)]
    in_specs += [pl.BlockSpec((tm, 512), col(c)) for c in (1, 2, 3, 4, 5)]
    in_specs += [pl.BlockSpec((8, 512), halo(3)), pl.BlockSpec((8, 512), halo(4))]
    in_specs += [pl.BlockSpec((tm, PLE), row), pl.BlockSpec((tm, D_MODEL), row),
                 _full((1, ATTN_W)), _full((1, CONV_W)), _full((1, D_MODEL)), _full((3, CONV_W)),
                 _full((D_MODEL, D_MODEL)), _full((PLE, D_MODEL)), _full((D_MODEL, D_MODEL))]
    out_specs = [pl.BlockSpec((tm, D_MODEL), row), pl.BlockSpec((tm, ATTN_W), row),
                 pl.BlockSpec((N_HEADS, tm, LANES), lambda i: (0, i, 0)), pl.BlockSpec((tm, 1536), row),
                 pl.BlockSpec((tm, CONV_W), row),
                 _full((D_MODEL, D_MODEL)), _full((PLE, D_MODEL)), _full((D_MODEL, D_MODEL)),
                 _full((1, ATTN_W)), _full((1, CONV_W)), _full((1, D_MODEL)), _full((3, CONV_W)), _full((1, LANES))]
    out_shape = [jax.ShapeDtypeStruct((T, D_MODEL), F32), jax.ShapeDtypeStruct((T, ATTN_W), BF16),
                 jax.ShapeDtypeStruct((N_HEADS, T, LANES), F32), jax.ShapeDtypeStruct((T, 1536), BF16),
                 jax.ShapeDtypeStruct((T, CONV_W), F32),
                 jax.ShapeDtypeStruct((D_MODEL, D_MODEL), F32), jax.ShapeDtypeStruct((PLE, D_MODEL), F32),
                 jax.ShapeDtypeStruct((D_MODEL, D_MODEL), F32),
                 jax.ShapeDtypeStruct((1, ATTN_W), F32), jax.ShapeDtypeStruct((1, CONV_W), F32),
                 jax.ShapeDtypeStruct((1, D_MODEL), F32), jax.ShapeDtypeStruct((3, CONV_W), F32),
                 jax.ShapeDtypeStruct((1, LANES), F32)]
    return pl.pallas_call(
        body, name="tail", grid=(nt,), in_specs=in_specs, out_specs=out_specs, out_shape=out_shape,
        compiler_params=_params(dimension_semantics=("arbitrary",)),
    )(x, o, proj, proj, proj, proj, proj, proj, proj, p, tgt, g_oa, g_oc, g_pl, conv_w, w_o, w_pl, w_plg)


def _attn_bwd(q, k, v, do, lse_row, delta_row, tk):
    T = q.shape[1]
    tq = tk
    nq = T // tq
    rc = min(SOFTMAX_ROWS, tk)

    def body(q_ref, k_ref, v_ref, do_ref, lse_ref, dl_ref, dq_ref, dk_ref, dv_ref,
             s0, s1, d0, d1, p0, p1, g0, g1, dk_acc, dv_acc):
        kj = pl.program_id(1)
        s_buf, dp_buf, p_buf, g_buf = (s0, s1), (d0, d1), (p0, p1), (g0, g1)

        @pl.when(kj == 0)
        def _():
            dq_ref[...] = jnp.zeros_like(dq_ref)

        def q_start(t):
            return pl.multiple_of((nq - 1 - t) * tq, tq)

        def matmuls(t, slot):
            qs = q_start(t)
            s_buf[slot][...] = _dot_nt(k_ref[0], q_ref[0, pl.ds(qs, tq), :])
            dp_buf[slot][...] = _dot_nt(v_ref[0], do_ref[pl.ds(qs, tq), :])

        def pointwise(t, slot, masked):
            qs = q_start(t)
            lse2 = lse_ref[0, :, pl.ds(qs, tq)] * LOG2E
            dl = dl_ref[0, :, pl.ds(qs, tq)]
            for r0 in range(0, tk, rc):
                st = s_buf[slot][r0:r0 + rc, :]
                if masked:
                    row = lax.broadcasted_iota(jnp.int32, (rc, tq), 0)
                    col = lax.broadcasted_iota(jnp.int32, (rc, tq), 1)
                    st = jnp.where(row + r0 <= col, st, NEG)
                pt = jnp.exp2(st * EXP2_SCALE - lse2)
                p_buf[slot][r0:r0 + rc, :] = pt.astype(BF16)
                g_buf[slot][r0:r0 + rc, :] = (pt * (dp_buf[slot][r0:r0 + rc, :] - dl) * SCALE).astype(BF16)

        def accumulate(t, slot):
            qs = q_start(t)
            dv_acc[...] += _dot(p_buf[slot][...], do_ref[pl.ds(qs, tq), :])
            dk_acc[...] += _dot(g_buf[slot][...], q_ref[0, pl.ds(qs, tq), :])
            dq_ref[0, pl.ds(qs, tq), :] += _dot_tn(g_buf[slot][...], k_ref[0])

        def last():
            dk_ref[0] = dk_acc[...]
            dv_ref[0] = dv_acc[...]

        dk_acc[...] = jnp.zeros_like(dk_acc)
        dv_acc[...] = jnp.zeros_like(dv_acc)
        p1[...] = jnp.zeros_like(p1)
        g1[...] = jnp.zeros_like(g1)
        _chunk_pipeline(nq - 1 - kj, matmuls, pointwise, accumulate, last)

    return pl.pallas_call(
        body, name="attn_bwd", grid=(N_HEADS, T // tk),
        in_specs=[pl.BlockSpec((1, T, HEAD_PAD), lambda h, j: (h, 0, 0)),
                  pl.BlockSpec((1, tk, HEAD_PAD), lambda h, j: (h, j, 0)),
                  pl.BlockSpec((1, tk, V_DIM), lambda h, j: (h, j, 0)),
                  pl.BlockSpec((T, V_DIM), lambda h, j: (0, h)),
                  pl.BlockSpec((1, 1, T), lambda h, j: (h, 0, 0)),
                  pl.BlockSpec((1, 1, T), lambda h, j: (h, 0, 0))],
        out_specs=[pl.BlockSpec((1, T, HEAD_PAD), lambda h, j: (h, 0, 0)),
                   pl.BlockSpec((1, tk, HEAD_PAD), lambda h, j: (h, j, 0)),
                   pl.BlockSpec((1, tk, V_DIM), lambda h, j: (h, j, 0))],
        out_shape=[jax.ShapeDtypeStruct((N_HEADS, T, HEAD_PAD), F32), jax.ShapeDtypeStruct((N_HEADS, T, HEAD_PAD), F32),
                   jax.ShapeDtypeStruct((N_HEADS, T, V_DIM), F32)],
        scratch_shapes=[pltpu.VMEM((tk, tq), F32)] * 4 + [pltpu.VMEM((tk, tq), BF16)] * 4
                       + [pltpu.VMEM((tk, HEAD_PAD), F32), pltpu.VMEM((tk, V_DIM), F32)],
        compiler_params=_params(dimension_semantics=("arbitrary", "arbitrary")),
    )(q, k, v, do, lse_row, delta_row)


def _bwd_proj(x, dx1, pos, proj, dq, dk, dv, dtail, du, g_in, w_in, g_cq, w_uq, g_ckv, w_ukv, gq, gk, conv_w,
              invf, sgn, tm):
    T = x.shape[0]
    nt = T // tm

    def body(x_ref, dx1_ref, pos_ref, lat_ref, cc_ref, cx_ref, dq_ref, dk_ref, dv_ref, dtail_ref, du_ref, dun_ref,
             g_in_ref, w_in_ref, g_cq_ref, w_uq_ref, g_ckv_ref, w_ukv_ref, gq_ref, gk_ref, cw_ref, invf_ref, sgn_ref,
             gx_ref, h_ref, dproj_ref, dw_uq_ref, dw_ukv_ref, dg_in_ref, dg_cq_ref, dg_ckv_ref, dgq_ref, dgk_ref):
        i = pl.program_id(0)

        @pl.when(i == 0)
        def _():
            for r in (dw_uq_ref, dw_ukv_ref, dg_in_ref, dg_cq_ref, dg_ckv_ref, dgq_ref, dgk_ref):
                r[...] = jnp.zeros_like(r)

        xv = x_ref[...]
        r0 = _inv_rms(xv, D_MODEL)
        xh0 = xv * r0
        g_in = g_in_ref[...]
        h_ref[...] = (xh0 * g_in).astype(BF16)

        c_q = lat_ref[:, 0:Q_LORA]
        rq = _inv_rms(c_q, Q_LORA)
        xq = c_q * rq
        g_cq = g_cq_ref[...]
        cqn = (xq * g_cq).astype(BF16)
        c_kv = lat_ref[:, Q_LORA:Q_LORA + KV_LORA]
        rkv = _inv_rms(c_kv, KV_LORA)
        xkv = c_kv * rkv
        g_ckv = g_ckv_ref[...]
        ckvn = (xkv * g_ckv).astype(BF16)
        kpe = lat_ref[:, 384:512]
        kpe_ss = jnp.sum(kpe * kpe, axis=-1, keepdims=True)
        cos_b, sin_b = _rope_tables(pos_ref, invf_ref, sgn_ref)
        gq_a, gq_b = gq_ref[:, 0:NOPE], gq_ref[:, NOPE:HEAD_PAD]
        gk_a, gk_b = gk_ref[:, 0:NOPE], gk_ref[:, NOPE:HEAD_PAD]

        dkpe = jnp.zeros((tm, LANES), F32)
        dcqn = jnp.zeros((tm, Q_LORA), F32)
        dckvn = jnp.zeros((tm, KV_LORA), F32)
        for hd in range(N_HEADS):
            c0 = hd * HEAD_PAD
            qh = _dot(cqn, w_uq_ref[:, c0:c0 + HEAD_PAD])
            a, b = qh[:, 0:NOPE], qh[:, NOPE:HEAD_PAD]
            r = lax.rsqrt((jnp.sum(a * a, axis=-1, keepdims=True) + jnp.sum(b * b, axis=-1, keepdims=True)) / QK_DIM + EPS)
            xa, xb = a * r, b * r
            dan = dq_ref[hd, :, 0:NOPE]
            dbr = dq_ref[hd, :, NOPE:HEAD_PAD]
            dbn = dbr * cos_b + _swap_rope_halves(dbr * sin_b)
            dgq_ref[:, 0:NOPE] += _colsum(dan * xa)
            dgq_ref[:, NOPE:HEAD_PAD] += _colsum(dbn * xb)
            dxa, dxb = dan * gq_a, dbn * gq_b
            cq = (jnp.sum(dxa * xa, axis=-1, keepdims=True) + jnp.sum(dxb * xb, axis=-1, keepdims=True)) / QK_DIM
            dqh = jnp.concatenate([r * (dxa - xa * cq), r * (dxb - xb * cq)], axis=-1).astype(BF16)
            dw_uq_ref[:, c0:c0 + HEAD_PAD] += _dot_tn(cqn, dqh)
            dcqn = dcqn + _dot_nt(dqh, w_uq_ref[:, c0:c0 + HEAD_PAD])
            kvh = _dot(ckvn, w_ukv_ref[:, c0:c0 + HEAD_PAD])
            ka = kvh[:, 0:NOPE]
            rk = lax.rsqrt((jnp.sum(ka * ka, axis=-1, keepdims=True) + kpe_ss) / QK_DIM + EPS)
            xka, xkb = ka * rk, kpe * rk
            dkan = dk_ref[hd, :, 0:NOPE]
            dkbr = dk_ref[hd, :, NOPE:HEAD_PAD]
            dkbn = dkbr * cos_b + _swap_rope_halves(dkbr * sin_b)
            dgk_ref[:, 0:NOPE] += _colsum(dkan * xka)
            dgk_ref[:, NOPE:HEAD_PAD] += _colsum(dkbn * xkb)
            dxka, dxkb = dkan * gk_a, dkbn * gk_b
            ck = (jnp.sum(dxka * xka, axis=-1, keepdims=True) + jnp.sum(dxkb * xkb, axis=-1, keepdims=True)) / QK_DIM
            dkpe = dkpe + rk * (dxkb - xkb * ck)
            dkvh = jnp.concatenate([rk * (dxka - xka * ck), dv_ref[hd]], axis=-1).astype(BF16)
            dw_ukv_ref[:, c0:c0 + HEAD_PAD] += _dot_tn(ckvn, dkvh)
            dckvn = dckvn + _dot_nt(dkvh, w_ukv_ref[:, c0:c0 + HEAD_PAD])

        dg_cq_ref[...] += _colsum(dcqn * xq)
        dxq = dcqn * g_cq
        dproj_ref[:, 0:Q_LORA] = (rq * (dxq - xq * (jnp.sum(dxq * xq, axis=-1, keepdims=True) / Q_LORA))).astype(BF16)
        dg_ckv_ref[...] += _colsum(dckvn * xkv)
        dxkv = dckvn * g_ckv
        dproj_ref[:, 256:384] = (rkv * (dxkv - xkv * (jnp.sum(dxkv * xkv, axis=-1, keepdims=True) / KV_LORA))).astype(BF16)
        dproj_ref[:, 384:512] = dkpe.astype(BF16)
        dproj_ref[:, 512:1536] = dtail_ref[:, 0:1024]
        dproj_ref[:, 2560:3072] = dtail_ref[:, 1024:1536]

        du_v = du_ref[...]
        not_last = jnp.where(i < nt - 1, 1.0, 0.0)
        nx0 = dun_ref[0:1, :] * not_last
        nx1 = dun_ref[1:2, :] * not_last
        row = lax.broadcasted_iota(jnp.int32, du_v.shape, 0)
        du1 = jnp.where(row == tm - 1, nx0, pltpu.roll(du_v, tm - 1, 0))
        du2 = jnp.where(row == tm - 2, nx0, jnp.where(row == tm - 1, nx1, pltpu.roll(du_v, tm - 2, 0)))
        dvc = cw_ref[2:3, :] * du_v + cw_ref[1:2, :] * du1 + cw_ref[0:1, :] * du2
        dproj_ref[:, 1536:2048] = (dvc * cx_ref[...]).astype(BF16)
        dproj_ref[:, 2048:2560] = (dvc * cc_ref[...]).astype(BF16)

        dh = jnp.zeros((tm, D_MODEL), F32)
        for c0 in range(0, PROJ_EXT, 512):
            dh = dh + _dot_nt(dproj_ref[:, c0:c0 + 512], w_in_ref[:, c0:c0 + 512])
        dg_in_ref[...] += _colsum(dh * xh0)
        dxh = dh * g_in
        gx_ref[...] = dx1_ref[...] + r0 * (dxh - xh0 * (jnp.sum(dxh * xh0, axis=-1, keepdims=True) / D_MODEL))

    row = lambda i: (i, 0)
    col = lambda c: (lambda i: (i, c))
    head_rows = lambda i: (0, i, 0)
    nxt = lambda i: (jnp.minimum((i + 1) * (tm // 8), T // 8 - 1), 0)
    in_specs = [pl.BlockSpec((tm, D_MODEL), row), pl.BlockSpec((tm, D_MODEL), row), pl.BlockSpec((tm, 1), row),
                pl.BlockSpec((tm, 512), col(0)), pl.BlockSpec((tm, 512), col(3)), pl.BlockSpec((tm, 512), col(4)),
                pl.BlockSpec((N_HEADS, tm, HEAD_PAD), head_rows), pl.BlockSpec((N_HEADS, tm, HEAD_PAD), head_rows),
                pl.BlockSpec((N_HEADS, tm, V_DIM), head_rows), pl.BlockSpec((tm, 1536), row),
                pl.BlockSpec((tm, CONV_W), row), pl.BlockSpec((8, CONV_W), nxt),
                _full((1, D_MODEL)), _full((D_MODEL, PROJ_EXT)), _full((1, Q_LORA)), _full((Q_LORA, N_HEADS * HEAD_PAD)),
                _full((1, KV_LORA)), _full((KV_LORA, N_HEADS * HEAD_PAD)), _full((1, HEAD_PAD)), _full((1, HEAD_PAD)),
                _full((3, CONV_W)), _full((1, LANES)), _full((1, LANES))]
    out_specs = [pl.BlockSpec((tm, D_MODEL), row), pl.BlockSpec((tm, D_MODEL), row), pl.BlockSpec((tm, PROJ_EXT), row),
                 _full((Q_LORA, N_HEADS * HEAD_PAD)), _full((KV_LORA, N_HEADS * HEAD_PAD)),
                 _full((1, D_MODEL)), _full((1, Q_LORA)), _full((1, KV_LORA)), _full((1, HEAD_PAD)), _full((1, HEAD_PAD))]
    out_shape = [jax.ShapeDtypeStruct((T, D_MODEL), F32), jax.ShapeDtypeStruct((T, D_MODEL), BF16),
                 jax.ShapeDtypeStruct((T, PROJ_EXT), BF16),
                 jax.ShapeDtypeStruct((Q_LORA, N_HEADS * HEAD_PAD), F32), jax.ShapeDtypeStruct((KV_LORA, N_HEADS * HEAD_PAD), F32),
                 jax.ShapeDtypeStruct((1, D_MODEL), F32), jax.ShapeDtypeStruct((1, Q_LORA), F32),
                 jax.ShapeDtypeStruct((1, KV_LORA), F32), jax.ShapeDtypeStruct((1, HEAD_PAD), F32),
                 jax.ShapeDtypeStruct((1, HEAD_PAD), F32)]
    return pl.pallas_call(
        body, name="bwd_proj", grid=(nt,), in_specs=in_specs, out_specs=out_specs, out_shape=out_shape,
        compiler_params=_params(dimension_semantics=("arbitrary",)),
    )(x, dx1, pos, proj, proj, proj, dq, dk, dv, dtail, du, du, g_in, w_in, g_cq, w_uq, g_ckv, w_ukv, gq, gk, conv_w,
      invf, sgn)


def _matmul_tn(a, b, tt, tn):
    T, M = a.shape
    N = b.shape[1]

    def body(a_ref, b_ref, o_ref):
        @pl.when(pl.program_id(1) == 0)
        def _():
            o_ref[...] = jnp.zeros_like(o_ref)

        o_ref[...] += _dot_tn(a_ref[...], b_ref[...])

    return pl.pallas_call(
        body, name="dw_in", grid=(N // tn, T // tt),
        in_specs=[pl.BlockSpec((tt, M), lambda j, t: (t, 0)), pl.BlockSpec((tt, tn), lambda j, t: (t, j))],
        out_specs=pl.BlockSpec((M, tn), lambda j, t: (0, j)),
        out_shape=jax.ShapeDtypeStruct((M, N), F32),
        compiler_params=_params(dimension_semantics=("arbitrary", "arbitrary")),
    )(a, b)


def _add_pair(grads, from_sibling, small, small_sibling, c):
    n = len(grads)

    def body(c_ref, *refs):
        ins, outs = refs[:2 * n + 2], refs[2 * n + 2:]
        for i in range(n + 1):
            outs[i][...] = (ins[2 * i][...] + ins[2 * i + 1][...]).astype(outs[i].dtype)

    in_specs, out_specs, out_shape, args = [], [], [], []
    for g, r in zip(grads, from_sibling):
        _, hr, cols = r.shape
        in_specs += [pl.BlockSpec((1, hr, cols), lambda k, c_ref: (k, c_ref[0], 0)),
                     pl.BlockSpec((1, hr, cols), lambda k, c_ref: (k, 0, 0))]
        out_specs.append(pl.BlockSpec((1, hr, cols), lambda k, c_ref: (k, 0, 0)))
        out_shape.append(jax.ShapeDtypeStruct(r.shape, BF16))
        args += [g, r]
    whole = pl.BlockSpec(small.shape, lambda k, c_ref: (0, 0))
    in_specs += [whole, whole]
    out_specs.append(whole)
    out_shape.append(jax.ShapeDtypeStruct(small.shape, F32))
    outs = pl.pallas_call(
        body, name="add_pair", out_shape=out_shape,
        grid_spec=pltpu.PrefetchScalarGridSpec(num_scalar_prefetch=1, grid=(N_CHIPS,), in_specs=in_specs,
                                               out_specs=out_specs),
        compiler_params=_params(dimension_semantics=("arbitrary",)),
    )(c.reshape(1), *args, small, small_sibling)
    return outs[:n], outs[n]


def _add_chips(parts, small_parts):
    arrays = list(parts) + [small_parts]

    def body(*refs):
        ins, outs = refs[:len(arrays)], refs[len(arrays):]
        for a_ref, o_ref in zip(ins, outs):
            part = lambda k: a_ref[k].astype(F32)
            o_ref[...] = ((part(0) + part(1)) + part(2)) + part(3)

    in_specs, out_specs, out_shape = [], [], []
    for a in arrays:
        _, rows, cols = a.shape
        in_specs.append(pl.BlockSpec((N_CHIPS, rows // 2, cols), lambda i: (0, i, 0)))
        out_specs.append(pl.BlockSpec((rows // 2, cols), lambda i: (i, 0)))
        out_shape.append(jax.ShapeDtypeStruct((rows, cols), F32))
    outs = pl.pallas_call(body, name="add_chips", grid=(2,), in_specs=in_specs, out_specs=out_specs,
                          out_shape=out_shape, compiler_params=_params(dimension_semantics=("arbitrary",)))(*arrays)
    return outs[:-1], outs[-1]


def _adamw(w, g, m, v, name):
    rows, cols = w.shape
    rb = 256 if rows * cols > 512 * 1024 else rows

    def body(w_ref, g_ref, m_ref, v_ref, d_ref, nm_ref, nv_ref):
        _adamw_math(g_ref[...], w_ref, m_ref, v_ref, d_ref, nm_ref, nv_ref)

    spec = pl.BlockSpec((rb, cols), lambda i: (i, 0))
    shp = jax.ShapeDtypeStruct(w.shape, F32)
    return pl.pallas_call(body, name=name, grid=(rows // rb,), in_specs=[spec] * 4, out_specs=[spec] * 3,
                          out_shape=[shp] * 3)(w, g, m, v)


def _adamw_math(gv, w_ref, m_ref, v_ref, d_ref, nm_ref, nv_ref):
    nm = B1 * m_ref[...] + (1.0 - B1) * gv
    nv = B2 * v_ref[...] + (1.0 - B2) * (gv * gv)
    m_hat = nm / (1.0 - B1 ** STEP)
    v_hat = nv / (1.0 - B2 ** STEP)
    d_ref[...] = -LR * (m_hat / (jnp.sqrt(v_hat) + ADAM_EPS) + WD * w_ref[...])
    nm_ref[...] = nm
    nv_ref[...] = nv


def _adamw_halves(w, mine, other, m, v, c, name):
    hr, cols = mine.shape

    def body(c_ref, w_ref, mine_ref, other_ref, m_ref, v_ref, g_ref, d_ref, nm_ref, nv_ref):
        gv = jnp.where(pl.program_id(0) == c_ref[0], mine_ref[...], other_ref[...])
        g_ref[...] = gv
        _adamw_math(gv, w_ref, m_ref, v_ref, d_ref, nm_ref, nv_ref)

    half = pl.BlockSpec((hr, cols), lambda i, c_ref: (i, 0))
    whole = pl.BlockSpec((hr, cols), lambda i, c_ref: (0, 0))
    shp = jax.ShapeDtypeStruct(w.shape, F32)
    return pl.pallas_call(
        body, name=name, out_shape=[shp] * 4,
        grid_spec=pltpu.PrefetchScalarGridSpec(num_scalar_prefetch=1, grid=(2,), in_specs=[half, whole, whole, half, half],
                                               out_specs=[half] * 4),
        compiler_params=_params(dimension_semantics=("arbitrary",)),
    )(c.reshape(1), w, mine, other, m, v)


_ANY = pl.BlockSpec(memory_space=pl.ANY)


def _mesh_pos():
    return lax.axis_index("x"), lax.axis_index("y"), lax.axis_index("c")


def _other_chips(x, y):
    return [(1 - x, y), (x, 1 - y), (1 - x, 1 - y)]


def _remote(src, dst, send_sems, recv_sems, k, to):
    return pltpu.make_async_remote_copy(src_ref=src, dst_ref=dst, send_sem=send_sems.at[k], recv_sem=recv_sems.at[k],
                                        device_id=to, device_id_type=MESH)


def _gather_weights(shards):
    n = len(shards)
    halved = [s.shape[0] % 32 == 0 for s in shards]

    def body(*refs):
        ins, outs, stage = refs[:n], refs[n:2 * n], refs[2 * n:3 * n]
        send_sems, recv_sems, local_sems = refs[3 * n:]
        x, y, c = _mesh_pos()
        me = 2 * x + y
        chips = _other_chips(x, y)

        def part(i, ref, hc):
            if not halved[i]:
                return ref
            hr = shards[i].shape[0] // 2
            return ref.at[pl.ds(hc * hr, hr), :]

        locals_, started = [], []
        for i in range(n):
            stage[i][...] = ins[i][...].astype(BF16)
            mine = pltpu.make_async_copy(stage[i], outs[i].at[me], local_sems.at[i])
            mine.start()
            locals_.append(mine)
            for j, (cx, cy) in enumerate(chips):
                cp = _remote(part(i, stage[i], c), part(i, outs[i].at[me], c), send_sems, recv_sems, 6 * i + j, (cx, cy, c))
                cp.start()
                started.append(cp)
        for i in range(n):
            for j, (cx, cy) in enumerate(chips):
                got = part(i, outs[i].at[2 * cx + cy], c)
                _remote(got, got, send_sems, recv_sems, 6 * i + j, (cx, cy, c)).wait_recv()
                if halved[i]:
                    fwd = _remote(got, got, send_sems, recv_sems, 6 * i + 3 + j, (x, y, 1 - c))
                    fwd.start()
                    started.append(fwd)
        for i in range(n):
            if halved[i]:
                for j, (cx, cy) in enumerate(chips):
                    got = part(i, outs[i].at[2 * cx + cy], 1 - c)
                    _remote(got, got, send_sems, recv_sems, 6 * i + 3 + j, (x, y, 1 - c)).wait_recv()
        for cp in started:
            cp.wait_send()
        for cp in locals_:
            cp.wait()

    vmem = pl.BlockSpec(memory_space=pltpu.VMEM)
    return pl.pallas_call(
        body, name="gather_weights", in_specs=[vmem] * n, out_specs=[_ANY] * n,
        out_shape=[jax.ShapeDtypeStruct((N_CHIPS,) + s.shape, BF16) for s in shards],
        scratch_shapes=[pltpu.VMEM(s.shape, BF16) for s in shards]
                       + [pltpu.SemaphoreType.DMA((6 * n,)), pltpu.SemaphoreType.DMA((6 * n,)), pltpu.SemaphoreType.DMA((n,))],
        compiler_params=_params(),
    )(*shards)


def _swap_halves(grads, small):
    n = len(grads)
    arrays = list(grads) + [small]

    def body(*refs):
        ins, outs, send_sems, recv_sems = refs[:n + 1], refs[n + 1:2 * n + 2], refs[2 * n + 2], refs[2 * n + 3]
        x, y, c = _mesh_pos()
        cps = []
        for i in range(n + 1):
            src = ins[i]
            if i < n:
                hr = grads[i].shape[1] // 2
                src = src.at[:, pl.ds((1 - c) * hr, hr), :]
            cp = _remote(src, outs[i], send_sems, recv_sems, i, (x, y, 1 - c))
            cp.start()
            cps.append(cp)
        for cp in cps:
            cp.wait()

    out_shape = [jax.ShapeDtypeStruct((g.shape[0], g.shape[1] // 2, g.shape[2]), F32) for g in grads]
    out_shape.append(jax.ShapeDtypeStruct(small.shape, F32))
    outs = pl.pallas_call(
        body, name="pair_grads", in_specs=[_ANY] * (n + 1), out_specs=[_ANY] * (n + 1), out_shape=out_shape,
        scratch_shapes=[pltpu.SemaphoreType.DMA((n + 1,)), pltpu.SemaphoreType.DMA((n + 1,))],
    )(*arrays)
    return outs[:n], outs[n]


def _scatter_to_chips(parts, small):
    n = len(parts)
    arrays = list(parts) + [small]

    def body(*refs):
        ins, outs = refs[:n + 1], refs[n + 1:2 * n + 2]
        send_sems, recv_sems, local_sems = refs[2 * n + 2:]
        x, y, c = _mesh_pos()
        me = 2 * x + y
        chips = _other_chips(x, y)
        locals_, sends = [], []
        for i in range(n + 1):
            mine = pltpu.make_async_copy(ins[i].at[me] if i < n else ins[i], outs[i].at[me], local_sems.at[i])
            mine.start()
            locals_.append(mine)
            for j, (cx, cy) in enumerate(chips):
                src = ins[i].at[2 * cx + cy] if i < n else ins[i]
                cp = _remote(src, outs[i].at[me], send_sems, recv_sems, 3 * i + j, (cx, cy, c))
                cp.start()
                sends.append(cp)
        for i in range(n + 1):
            for j, (cx, cy) in enumerate(chips):
                got = outs[i].at[2 * cx + cy]
                _remote(got, got, send_sems, recv_sems, 3 * i + j, (cx, cy, c)).wait_recv()
        for cp in sends:
            cp.wait_send()
        for cp in locals_:
            cp.wait()

    out_shape = [jax.ShapeDtypeStruct(p.shape, p.dtype) for p in parts]
    out_shape.append(jax.ShapeDtypeStruct((N_CHIPS,) + small.shape, small.dtype))
    outs = pl.pallas_call(
        body, name="scatter_grads", in_specs=[_ANY] * (n + 1), out_specs=[_ANY] * (n + 1), out_shape=out_shape,
        scratch_shapes=[pltpu.SemaphoreType.DMA((3 * n + 3,)), pltpu.SemaphoreType.DMA((3 * n + 3,)),
                        pltpu.SemaphoreType.DMA((n + 1,))],
    )(*arrays)
    return outs[:n], outs[n]


def _share_halves(halves):
    n = len(halves)

    def body(*refs):
        ins, outs, send_sems, recv_sems = refs[:n], refs[n:2 * n], refs[2 * n], refs[2 * n + 1]
        x, y, c = _mesh_pos()
        cps = [_remote(ins[i], outs[i], send_sems, recv_sems, i, (x, y, 1 - c)) for i in range(n)]
        for cp in cps:
            cp.start()
        for cp in cps:
            cp.wait()

    return pl.pallas_call(
        body, name="share_halves", in_specs=[_ANY] * n, out_specs=[_ANY] * n,
        out_shape=[jax.ShapeDtypeStruct(h.shape, h.dtype) for h in halves],
        scratch_shapes=[pltpu.SemaphoreType.DMA((n,)), pltpu.SemaphoreType.DMA((n,))],
    )(*halves)


SHARD_COLS_IN = IN_TOTAL // N_CHIPS
KPE_END = Q_LORA + KV_LORA + ROPE


def _assemble_weights(c_in, c_uq, c_ukv, c_o, c_pl, c_plg, c_conv):
    by_cols = lambda a: a.transpose(1, 0, 2).reshape(a.shape[1], N_CHIPS * a.shape[2])
    w_in_e = jnp.concatenate([c_in[0][:, :KPE_END], jnp.zeros((D_MODEL, 64), BF16), c_in[0][:, KPE_END:],
                              c_in[1], c_in[2], c_in[3]], axis=1)
    w_uq_e = by_cols(jnp.pad(c_uq, ((0, 0), (0, 0), (0, HEAD_PAD - QK_DIM))))
    return (w_in_e, w_uq_e, by_cols(c_ukv), by_cols(c_conv).astype(F32), c_o.reshape(D_MODEL, D_MODEL),
            by_cols(c_pl), c_plg.reshape(D_MODEL, D_MODEL))


def _split_grads(dw_in_e, dw_uq_e, dw_ukv, dw_o, dw_pl, dw_plg):
    chip_major = lambda a: a.reshape(a.shape[0], N_CHIPS, a.shape[1] // N_CHIPS).transpose(1, 0, 2)
    first = jnp.concatenate([dw_in_e[:, :KPE_END], dw_in_e[:, KPE_END + 64:SHARD_COLS_IN + 64]], axis=1)
    rest = [dw_in_e[:, SHARD_COLS_IN * k + 64:SHARD_COLS_IN * (k + 1) + 64] for k in range(1, N_CHIPS)]
    return [jnp.stack([first] + rest), chip_major(dw_uq_e)[:, :, :QK_DIM], chip_major(dw_ukv),
            dw_o.reshape(N_CHIPS, D_MODEL // N_CHIPS, D_MODEL), chip_major(dw_pl),
            dw_plg.reshape(N_CHIPS, D_MODEL // N_CHIPS, D_MODEL)]


def _local_step(x, p, pos, tgt, gains, w_in_e, w_uq_e, w_ukv, conv_w, w_o, w_pl, w_plg, tm, tq):
    g_in, g_cq, g_ckv, g_q, g_k, g_oa, g_oc, g_pl = gains
    T = x.shape[0]
    zpad = lambda a, n: jnp.concatenate([a, jnp.zeros(a.shape[:-1] + (n,), a.dtype)], axis=-1)
    gq, gk = zpad(g_q, HEAD_PAD - QK_DIM), zpad(g_k, HEAD_PAD - QK_DIM)
    inv_freq = 1.0 / (ROPE_THETA ** (jnp.arange(0, ROPE, 2, dtype=F32) / ROPE))
    invf = jnp.concatenate([inv_freq, inv_freq, jnp.zeros((64,), F32)]).reshape(1, LANES)
    sgn = jnp.concatenate([-jnp.ones((32,), F32), jnp.ones((32,), F32), jnp.zeros((64,), F32)]).reshape(1, LANES)

    proj, q, k, v = _fwd_proj(x, pos, g_in, w_in_e, g_cq, w_uq_e, g_ckv, w_ukv, gq, gk, invf, sgn, tm)
    o, lse = _attn_fwd(q, k, v, tq)
    (dx1, do, delta, dtail, du, dw_o, dw_pl, dw_plg, dg_oa, dg_oc, dg_pl, dconv, loss) = _tail(
        x, o, proj, p, tgt, g_oa, g_oc, g_pl, conv_w, w_o, w_pl, w_plg, tm)
    lse_row = lse[:, :, 0].reshape(N_HEADS, 1, T)
    delta_row = delta[:, :, 0].reshape(N_HEADS, 1, T)
    dq, dk, dv = _attn_bwd(q, k, v, do, lse_row, delta_row, tq)
    (gx, h, dproj, dw_uq_e, dw_ukv, dg_in, dg_cq, dg_ckv, dgq, dgk) = _bwd_proj(
        x, dx1, pos, proj, dq, dk, dv, dtail, du, g_in, w_in_e, g_cq, w_uq_e, g_ckv, w_ukv, gq, gk, conv_w, invf, sgn, tm)
    dw_in_e = _matmul_tn(h, dproj, min(512, T), 512)
    wgrads = (dw_in_e, dw_uq_e, dw_ukv, dw_o, dw_pl, dw_plg)
    ggrads = (dg_in, dg_cq, dg_ckv, dgq, dgk, dg_oa, dg_oc, dg_pl)
    return loss, gx, wgrads, ggrads, dconv


def kernel(x, p, positions, g_in, w_in, g_cq, w_uq, g_ckv, w_ukv, g_q, g_k, conv_w, g_oa, g_oc, w_o, w_pl, w_plg, g_pl, loss_target, m_g_in, m_w_in, m_g_cq, m_w_uq, m_g_ckv, m_w_ukv, m_g_q, m_g_k, m_conv_w, m_g_oa, m_g_oc, m_w_o, m_w_pl, m_w_plg, m_g_pl, v_g_in, v_w_in, v_g_cq, v_w_uq, v_g_ckv, v_w_ukv, v_g_q, v_g_k, v_conv_w, v_g_oa, v_g_oc, v_w_o, v_w_pl, v_w_plg, v_g_pl):
    T = x.shape[1]
    c = lax.axis_index("c")
    chip = 2 * lax.axis_index("x") + lax.axis_index("y")
    gains = [g.reshape(1, -1) for g in (g_in, g_cq, g_ckv, g_q, g_k, g_oa, g_oc, g_pl)]

    gathered = _gather_weights([w_in[0], w_uq[0], w_ukv[0], w_o[0], w_pl[0], w_plg[0], conv_w[0]])
    full = _assemble_weights(*gathered)

    loss, gx, wgrads, ggrads, dconv = _local_step(
        x[0], p[0, 0], positions.reshape(T, 1), loss_target[0], gains, *full, 256, 512)

    grads_cm = _split_grads(*wgrads)
    small_parts = [a.reshape(-1, LANES) for a in (*ggrads, loss, dconv)]
    small_rows = [a.shape[0] for a in small_parts]
    tile_rows = [-(-r // 8) * 8 for r in small_rows]
    tile_rows[-1] += -sum(tile_rows) % 16
    small = jnp.concatenate([jnp.pad(a, ((0, t - r), (0, 0))) for a, r, t in zip(small_parts, small_rows, tile_rows)])
    from_sibling, small_sibling = _swap_halves(grads_cm, small)
    chip_parts, chip_small = _add_pair(grads_cm, from_sibling, small, small_sibling, c)
    by_chip, small_by_chip = _scatter_to_chips(chip_parts, chip_small)
    halves, small_total = _add_chips(by_chip, small_by_chip)
    other_halves = _share_halves(halves)

    gg, off = [], 0
    for rows, tiled in zip(small_rows, tile_rows):
        gg.append(small_total[off:off + rows].reshape(1, -1))
        off += tiled
    loss_out = gg[8][0, 0]
    conv_total = gg[9].reshape(3, CONV_W)
    conv_g = lax.dynamic_slice(conv_total, (0, chip * (CONV_W // N_CHIPS)), (3, CONV_W // N_CHIPS))
    g_by_name = dict(g_in=gg[0], g_cq=gg[1], g_ckv=gg[2], g_q=gg[3][:, :QK_DIM], g_k=gg[4][:, :QK_DIM], conv_w=conv_g,
                     g_oa=gg[5], g_oc=gg[6], g_pl=gg[7])
    half_by_name = dict(zip(("w_in", "w_uq", "w_ukv", "w_o", "w_pl", "w_plg"), zip(halves, other_halves)))
    weights = dict(g_in=g_in, w_in=w_in, g_cq=g_cq, w_uq=w_uq, g_ckv=g_ckv, w_ukv=w_ukv, g_q=g_q, g_k=g_k,
                   conv_w=conv_w, g_oa=g_oa, g_oc=g_oc, w_o=w_o, w_pl=w_pl, w_plg=w_plg, g_pl=g_pl)
    ms = dict(g_in=m_g_in, w_in=m_w_in, g_cq=m_g_cq, w_uq=m_w_uq, g_ckv=m_g_ckv, w_ukv=m_w_ukv, g_q=m_g_q, g_k=m_g_k,
              conv_w=m_conv_w, g_oa=m_g_oa, g_oc=m_g_oc, w_o=m_w_o, w_pl=m_w_pl, w_plg=m_w_plg, g_pl=m_g_pl)
    vs = dict(g_in=v_g_in, w_in=v_w_in, g_cq=v_g_cq, w_uq=v_w_uq, g_ckv=v_g_ckv, w_ukv=v_w_ukv, g_q=v_g_q, g_k=v_g_k,
              conv_w=v_conv_w, g_oa=v_g_oa, g_oc=v_g_oc, w_o=v_w_o, w_pl=v_w_pl, w_plg=v_w_plg, g_pl=v_g_pl)
    names = list(weights)
    grads, deltas, new_m, new_v = [], [], [], []
    for n in names:
        w = weights[n]
        w2 = w.reshape(-1, w.shape[-1])
        if n in half_by_name:
            g2, d, nm, nv = _adamw_halves(w2, *half_by_name[n], ms[n].reshape(w2.shape), vs[n].reshape(w2.shape), c,
                                          "adamw_" + n)
        else:
            g2 = g_by_name[n].reshape(w2.shape)
            d, nm, nv = _adamw(w2, g2, ms[n].reshape(w2.shape), vs[n].reshape(w2.shape), "adamw_" + n)
        grads.append(g2.reshape(w.shape))
        deltas.append(d.reshape(w.shape))
        new_m.append(nm.reshape(w.shape))
        new_v.append(nv.reshape(w.shape))
    return (loss_out, gx.reshape(x.shape), *grads, *deltas, *new_m, *new_v)
```

```python
import functools
import math

import jax
import jax.numpy as jnp
from jax import lax
from jax.experimental import pallas as pl
from jax.experimental.pallas import tpu as pltpu

F32 = jnp.float32
BF16 = jnp.bfloat16

D_MODEL = 1024
N_HEADS = 4
NOPE = 128
ROPE = 64
V_DIM = 128
QK_DIM = NOPE + ROPE
HEAD_PAD = 256
Q_LORA = 256
KV_LORA = 128
ATTN_W = 512
CONV_W = 512
PLE = 256
IN_TOTAL = 3008
PROJ_EXT = 3072
ROPE_THETA = 10000.0
EPS = 1e-6
SCALE = 1.0 / math.sqrt(QK_DIM)
LOG2E = math.log2(math.e)
EXP2_SCALE = SCALE * LOG2E
NEG = -1e30
SOFTMAX_ROWS = 32
SUB_TILE = 256

LR, B1, B2, ADAM_EPS, WD, STEP = 0.001, 0.9, 0.999, 1e-08, 0.01, 10

N_CHIPS = 4
LANES = 128
VMEM_LIMIT = 56 * 1024 * 1024
MESH = pl.DeviceIdType.MESH


def _params(**kw):
    return pltpu.CompilerParams(vmem_limit_bytes=VMEM_LIMIT, **kw)


def _inv_rms(x, n):
    return lax.rsqrt(jnp.sum(x * x, axis=-1, keepdims=True) / n + EPS)


def _lane_sum(a):
    folded = a[:, 0:LANES]
    for c0 in range(LANES, a.shape[1], LANES):
        folded = folded + a[:, c0:c0 + LANES]
    head = folded.astype(BF16)
    tail = (folded - head.astype(F32)).astype(BF16)
    ones = jnp.ones((LANES, LANES), BF16)
    return _dot(head, ones) + _dot(tail, ones)


def _inv_rms_mxu(x):
    return lax.rsqrt(_lane_sum(x * x) / x.shape[1] + EPS)


def _rep(r, width):
    return r if width == LANES else jnp.tile(r, (1, width // LANES))


def _sigmoid(z):
    return 1.0 / (1.0 + jnp.exp(-z))


def _swap_rope_halves(b):
    lane = lax.broadcasted_iota(jnp.int32, b.shape, 1)
    swapped = jnp.where(lane < 32, pltpu.roll(b, 96, 1), pltpu.roll(b, 32, 1))
    return jnp.where(lane < ROPE, swapped, 0.0)


def _dot(a, b):
    return jnp.dot(a, b, preferred_element_type=F32)


def _dot_nt(a, b):
    return lax.dot_general(a, b, (((1,), (1,)), ((), ())), preferred_element_type=F32)


def _dot_tn(a, b):
    return lax.dot_general(a, b, (((0,), (0,)), ((), ())), preferred_element_type=F32)


def _colsum(a):
    return jnp.sum(a, axis=0, keepdims=True)


def _full(shape):
    return pl.BlockSpec(shape, lambda *_: (0,) * len(shape))


def _rope_tables(pos_ref, invf_ref, sgn_ref):
    ang = pos_ref[...].astype(F32) * invf_ref[...]
    return jnp.cos(ang), jnp.sin(ang) * sgn_ref[...]


def _fwd_proj(x, pos, g_in, w_in, g_cq, w_uq, g_ckv, w_ukv, gq, gk, invf, sgn, tm):
    T = x.shape[0]

    ts = min(SUB_TILE, tm)

    def body(x_ref, pos_ref, g_in_ref, w_in_ref, g_cq_ref, w_uq_ref, g_ckv_ref, w_ukv_ref, gq_ref, gk_ref,
             invf_ref, sgn_ref, proj_ref, q_ref, k_ref, v_ref):
        for r0 in range(0, tm, ts):
            rows = slice(r0, r0 + ts)
            xv = x_ref[rows, :]
            h = (xv * _rep(_inv_rms_mxu(xv), D_MODEL) * g_in_ref[...]).astype(BF16)
            for c0 in range(0, PROJ_EXT, 512):
                proj_ref[rows, c0:c0 + 512] = _dot(h, w_in_ref[:, c0:c0 + 512])
            c_q = proj_ref[rows, 0:Q_LORA]
            cqn = (c_q * _rep(_inv_rms_mxu(c_q), Q_LORA) * g_cq_ref[...]).astype(BF16)
            c_kv = proj_ref[rows, Q_LORA:Q_LORA + KV_LORA]
            ckvn = (c_kv * _inv_rms_mxu(c_kv) * g_ckv_ref[...]).astype(BF16)
            kpe = proj_ref[rows, 384:512]
            kpe_sq = kpe * kpe
            cos_b, sin_b = _rope_tables(pos_ref.at[rows, :], invf_ref, sgn_ref)
            gq_a, gq_b = gq_ref[:, 0:NOPE], gq_ref[:, NOPE:HEAD_PAD]
            gk_a, gk_b = gk_ref[:, 0:NOPE], gk_ref[:, NOPE:HEAD_PAD]
            for hd in range(N_HEADS):
                c0 = hd * HEAD_PAD
                qh = _dot(cqn, w_uq_ref[:, c0:c0 + HEAD_PAD])
                a, b = qh[:, 0:NOPE], qh[:, NOPE:HEAD_PAD]
                r = lax.rsqrt(_lane_sum(a * a + b * b) / QK_DIM + EPS)
                bn = b * r * gq_b
                q_ref[hd, rows, 0:NOPE] = (a * r * gq_a).astype(BF16)
                q_ref[hd, rows, NOPE:HEAD_PAD] = (bn * cos_b + _swap_rope_halves(bn) * sin_b).astype(BF16)
                kvh = _dot(ckvn, w_ukv_ref[:, c0:c0 + HEAD_PAD])
                ka = kvh[:, 0:NOPE]
                rk = lax.rsqrt(_lane_sum(ka * ka + kpe_sq) / QK_DIM + EPS)
                kbn = kpe * rk * gk_b
                k_ref[hd, rows, 0:NOPE] = (ka * rk * gk_a).astype(BF16)
                k_ref[hd, rows, NOPE:HEAD_PAD] = (kbn * cos_b + _swap_rope_halves(kbn) * sin_b).astype(BF16)
                v_ref[hd, rows, 0:V_DIM] = kvh[:, NOPE:HEAD_PAD].astype(BF16)
                v_ref[hd, rows, V_DIM:2 * V_DIM] = jnp.ones((ts, V_DIM), BF16)

    row = lambda i: (i, 0)
    head_rows = lambda i: (0, i, 0)
    return pl.pallas_call(
        body, name="fwd_proj", grid=(T // tm,),
        in_specs=[pl.BlockSpec((tm, D_MODEL), row), pl.BlockSpec((tm, 1), row), _full((1, D_MODEL)),
                  _full((D_MODEL, PROJ_EXT)), _full((1, Q_LORA)), _full((Q_LORA, N_HEADS * HEAD_PAD)),
                  _full((1, KV_LORA)), _full((KV_LORA, N_HEADS * HEAD_PAD)), _full((1, HEAD_PAD)), _full((1, HEAD_PAD)),
                  _full((1, LANES)), _full((1, LANES))],
        out_specs=[pl.BlockSpec((tm, PROJ_EXT), row), pl.BlockSpec((N_HEADS, tm, HEAD_PAD), head_rows),
                   pl.BlockSpec((N_HEADS, tm, HEAD_PAD), head_rows), pl.BlockSpec((N_HEADS, tm, 2 * V_DIM), head_rows)],
        out_shape=[jax.ShapeDtypeStruct((T, PROJ_EXT), F32), jax.ShapeDtypeStruct((N_HEADS, T, HEAD_PAD), BF16),
                   jax.ShapeDtypeStruct((N_HEADS, T, HEAD_PAD), BF16), jax.ShapeDtypeStruct((N_HEADS, T, 2 * V_DIM), BF16)],
        compiler_params=_params(dimension_semantics=("arbitrary",)),
    )(x, pos, g_in, w_in, g_cq, w_uq, g_ckv, w_ukv, gq, gk, invf, sgn)


def _chunk_pipeline(n_loop, matmuls, pointwise, accumulate, last):
    def iteration(t, slot):
        matmuls(t + 1, 1 - slot)
        accumulate(jnp.maximum(t - 1, 0), 1 - slot)
        pointwise(t, slot, False)

    def finish(slot):
        accumulate(jnp.maximum(n_loop - 1, 0), 1 - slot)
        pointwise(n_loop, slot, True)
        accumulate(n_loop, slot)
        last()

    matmuls(0, 0)

    def pair(tt, carry):
        iteration(2 * tt, 0)
        iteration(2 * tt + 1, 1)
        return carry

    lax.fori_loop(0, n_loop // 2, pair, 0)
    odd = lax.rem(n_loop, 2) == 1

    @pl.when(odd)
    def _():
        iteration(n_loop - 1, 0)
        finish(1)

    @pl.when(jnp.logical_not(odd))
    def _():
        finish(0)


def _attn_fwd(q, k, v, tq):
    T = q.shape[1]
    tk = tq
    rc = min(SOFTMAX_ROWS, tq)

    def body(q_ref, k_ref, v_ref, o_ref, lse_ref, s0, s1, p0, p1, a0, a1, m_ref, acc_ref):
        qi = pl.program_id(1)
        s_buf, p_buf, a_buf = (s0, s1), (p0, p1), (a0, a1)

        def scores(t, slot):
            ks = pl.multiple_of(t * tk, tk)
            s_buf[slot][...] = _dot_nt(q_ref[0], k_ref[0, pl.ds(ks, tk), :])

        def values(t, slot):
            ks = pl.multiple_of(t * tk, tk)
            acc_ref[...] = acc_ref[...] * a_buf[slot][...] + _dot(p_buf[slot][...], v_ref[0, pl.ds(ks, tk), :])

        def softmax(t, slot, masked):
            s_all = s_buf[slot][...]
            if masked:
                row = lax.broadcasted_iota(jnp.int32, (tq, tk), 0)
                col = lax.broadcasted_iota(jnp.int32, (tq, tk), 1)
                s_all = jnp.where(col <= row, s_all, NEG)
                s_buf[slot][...] = s_all
            m_old = m_ref[...]
            m_new = jnp.maximum(m_old, jnp.max(s_all, axis=1, keepdims=True))
            a_buf[slot][...] = jnp.exp2((m_old - m_new) * EXP2_SCALE)
            m_ref[...] = m_new
            for r0 in range(0, tq, rc):
                s = s_buf[slot][r0:r0 + rc, :]
                p_buf[slot][r0:r0 + rc, :] = jnp.exp2((s - m_new[r0:r0 + rc, :]) * EXP2_SCALE).astype(BF16)

        def last():
            l = acc_ref[:, V_DIM:2 * V_DIM]
            o_ref[...] = acc_ref[:, 0:V_DIM] / l
            lse_ref[0] = m_ref[...] * SCALE + jnp.log(l)

        m_ref[...] = jnp.full_like(m_ref, NEG)
        acc_ref[...] = jnp.zeros_like(acc_ref)
        p1[...] = jnp.zeros_like(p1)
        a1[...] = jnp.ones_like(a1)
        _chunk_pipeline(qi, scores, softmax, values, last)

    return pl.pallas_call(
        body, name="attn_fwd", grid=(N_HEADS, T // tq),
        in_specs=[pl.BlockSpec((1, tq, HEAD_PAD), lambda h, i: (h, i, 0)),
                  pl.BlockSpec((1, T, HEAD_PAD), lambda h, i: (h, 0, 0)),
                  pl.BlockSpec((1, T, 2 * V_DIM), lambda h, i: (h, 0, 0))],
        out_specs=[pl.BlockSpec((tq, V_DIM), lambda h, i: (i, h)),
                   pl.BlockSpec((1, tq, LANES), lambda h, i: (h, i, 0))],
        out_shape=[jax.ShapeDtypeStruct((T, ATTN_W), F32), jax.ShapeDtypeStruct((N_HEADS, T, LANES), F32)],
        scratch_shapes=[pltpu.VMEM((tq, tk), F32), pltpu.VMEM((tq, tk), F32), pltpu.VMEM((tq, tk), BF16),
                        pltpu.VMEM((tq, tk), BF16), pltpu.VMEM((tq, 1), F32), pltpu.VMEM((tq, 1), F32),
                        pltpu.VMEM((tq, 1), F32), pltpu.VMEM((tq, 2 * V_DIM), F32)],
        compiler_params=_params(dimension_semantics=("arbitrary", "arbitrary")),
    )(q, k, v)


def _tail(x, o, proj, p, tgt, g_oa, g_oc, g_pl, conv_w, w_o, w_pl, w_plg, tm):
    T = x.shape[0]
    nt = T // tm

    def body(x_ref, o_ref, za_ref, cb_ref, cc_ref, cx_ref, zc_ref, cch_ref, cxh_ref, p_ref, tgt_ref,
             g_oa_ref, g_oc_ref, g_pl_ref, cw_ref, w_o_ref, w_pl_ref, w_plg_ref,
             dx1_ref, do_ref, delta_ref, dtail_ref, du_ref,
             dw_o_ref, dw_pl_ref, dw_plg_ref, dg_oa_ref, dg_oc_ref, dg_pl_ref, dcw_ref, loss_ref):
        i = pl.program_id(0)

        @pl.when(i == 0)
        def _():
            for r in (dw_o_ref, dw_pl_ref, dw_plg_ref, dg_oa_ref, dg_oc_ref, dg_pl_ref, dcw_ref, loss_ref):
                r[...] = jnp.zeros_like(r)

        xv, ov, za, cb, zc = x_ref[...], o_ref[...], za_ref[...], cb_ref[...], zc_ref[...]
        g_oa, g_oc, g_pl = g_oa_ref[...], g_oc_ref[...], g_pl_ref[...]
        w0, w1, w2 = cw_ref[0:1, :], cw_ref[1:2, :], cw_ref[2:3, :]

        sa = _sigmoid(za)
        silu_a = za * sa
        ga = ov * silu_a
        ra = _inv_rms(ga, ATTN_W)
        xa = ga * ra
        ya = xa * g_oa
        v = cc_ref[...] * cx_ref[...]
        not_first = jnp.where(i > 0, 1.0, 0.0)
        hv6 = cch_ref[6:7, :] * cxh_ref[6:7, :] * not_first
        hv7 = cch_ref[7:8, :] * cxh_ref[7:8, :] * not_first
        row = lax.broadcasted_iota(jnp.int32, v.shape, 0)
        v1 = jnp.where(row == 0, hv7, pltpu.roll(v, 1, 0))
        v2 = jnp.where(row == 0, hv6, jnp.where(row == 1, hv7, pltpu.roll(v, 2, 0)))
        u = w0 * v2 + w1 * v1 + w2 * v
        sc = _sigmoid(zc)
        silu_c = zc * sc
        gc = cb * u * silu_c
        rc = _inv_rms(gc, CONV_W)
        xc = gc * rc
        yc = xc * g_oc
        ycat = jnp.concatenate([ya, yc], axis=-1).astype(BF16)
        x1 = xv + _dot(ycat, w_o_ref[...])
        r1 = _inv_rms(x1, D_MODEL)
        xh1 = x1 * r1
        n1 = (xh1 * g_pl).astype(BF16)
        gate = _sigmoid(_dot(n1, w_plg_ref[...]))
        pb = p_ref[...].astype(BF16)
        pp = _dot(pb, w_pl_ref[...])
        err = x1 + gate * pp - tgt_ref[...]
        loss_ref[...] += 0.5 * jnp.sum(err * err) / D_MODEL
        dy = err / D_MODEL

        dpp = (dy * gate).astype(BF16)
        da = (dy * pp * gate * (1.0 - gate)).astype(BF16)
        dw_pl_ref[...] += _dot_tn(pb, dpp)
        dw_plg_ref[...] += _dot_tn(n1, da)
        dn1 = _dot_nt(da, w_plg_ref[...])
        dg_pl_ref[...] += _colsum(dn1 * xh1)
        dxh = dn1 * g_pl
        dx1 = dy + r1 * (dxh - xh1 * (jnp.sum(dxh * xh1, axis=-1, keepdims=True) / D_MODEL))
        dx1_ref[...] = dx1
        dx1b = dx1.astype(BF16)
        dw_o_ref[...] += _dot_tn(ycat, dx1b)
        dycat = _dot_nt(dx1b, w_o_ref[...])
        dya, dyc = dycat[:, 0:ATTN_W], dycat[:, ATTN_W:D_MODEL]

        dg_oa_ref[...] += _colsum(dya * xa)
        dxa = dya * g_oa
        dga = ra * (dxa - xa * (jnp.sum(dxa * xa, axis=-1, keepdims=True) / ATTN_W))
        do = (dga * silu_a).astype(BF16)
        do_ref[...] = do
        dof = do.astype(F32) * ov
        for hd in range(N_HEADS):
            delta_ref[hd] = _lane_sum(dof[:, hd * V_DIM:(hd + 1) * V_DIM])
        dtail_ref[:, 0:512] = (dga * ov * (sa * (1.0 + za * (1.0 - sa)))).astype(BF16)

        dg_oc_ref[...] += _colsum(dyc * xc)
        dxc = dyc * g_oc
        dgc = rc * (dxc - xc * (jnp.sum(dxc * xc, axis=-1, keepdims=True) / CONV_W))
        dtail_ref[:, 512:1024] = (dgc * u * silu_c).astype(BF16)
        du = dgc * cb * silu_c
        du_ref[...] = du
        dtail_ref[:, 1024:1536] = (dgc * cb * u * (sc * (1.0 + zc * (1.0 - sc)))).astype(BF16)
        dcw_ref[0:1, :] += _colsum(du * v2)
        dcw_ref[1:2, :] += _colsum(du * v1)
        dcw_ref[2:3, :] += _colsum(du * v)

    row = lambda i: (i, 0)
    col = lambda c: (lambda i: (i, c))
    halo = lambda c: (lambda i: (jnp.maximum(i * (tm // 8) - 1, 0), c))
    in_specs = [pl.BlockSpec((tm, D_MODEL), row), pl.BlockSpec((tm, ATTN_W), row)]
    in_specs += [pl.BlockSpec((tm, 512), col(c)) for c in (1, 2, 3, 4, 5)]
    in_specs += [pl.BlockSpec((8, 512), halo(3)), pl.BlockSpec((8, 512), halo(4))]
    in_specs += [pl.BlockSpec((tm, PLE), row), pl.BlockSpec((tm, D_MODEL), row),
                 _full((1, ATTN_W)), _full((1, CONV_W)), _full((1, D_MODEL)), _full((3, CONV_W)),
                 _full((D_MODEL, D_MODEL)), _full((PLE, D_MODEL)), _full((D_MODEL, D_MODEL))]
    out_specs = [pl.BlockSpec((tm, D_MODEL), row), pl.BlockSpec((tm, ATTN_W), row),
                 pl.BlockSpec((N_HEADS, tm, LANES), lambda i: (0, i, 0)), pl.BlockSpec((tm, 1536), row),
                 pl.BlockSpec((tm, CONV_W), row),
                 _full((D_MODEL, D_MODEL)), _full((PLE, D_MODEL)), _full((D_MODEL, D_MODEL)),
                 _full((1, ATTN_W)), _full((1, CONV_W)), _full((1, D_MODEL)), _full((3, CONV_W)), _full((1, LANES))]
    out_shape = [jax.ShapeDtypeStruct((T, D_MODEL), F32), jax.ShapeDtypeStruct((T, ATTN_W), BF16),
                 jax.ShapeDtypeStruct((N_HEADS, T, LANES), F32), jax.ShapeDtypeStruct((T, 1536), BF16),
                 jax.ShapeDtypeStruct((T, CONV_W), F32),
                 jax.ShapeDtypeStruct((D_MODEL, D_MODEL), F32), jax.ShapeDtypeStruct((PLE, D_MODEL), F32),
                 jax.ShapeDtypeStruct((D_MODEL, D_MODEL), F32),
                 jax.ShapeDtypeStruct((1, ATTN_W), F32), jax.ShapeDtypeStruct((1, CONV_W), F32),
                 jax.ShapeDtypeStruct((1, D_MODEL), F32), jax.ShapeDtypeStruct((3, CONV_W), F32),
                 jax.ShapeDtypeStruct((1, LANES), F32)]
    return pl.pallas_call(
        body, name="tail", grid=(nt,), in_specs=in_specs, out_specs=out_specs, out_shape=out_shape,
        compiler_params=_params(dimension_semantics=("arbitrary",)),
    )(x, o, proj, proj, proj, proj, proj, proj, proj, p, tgt, g_oa, g_oc, g_pl, conv_w, w_o, w_pl, w_plg)


def _attn_bwd(q, k, v, do, lse_row, delta_row, tk):
    T = q.shape[1]
    tq = tk
    nq = T // tq
    rc = min(SOFTMAX_ROWS, tk)

    def body(q_ref, k_ref, v_ref, do_ref, lse_ref, dl_ref, dq_ref, dk_ref, dv_ref,
             s0, s1, d0, d1, p0, p1, g0, g1, dk_acc, dv_acc):
        kj = pl.program_id(1)
        s_buf, dp_buf, p_buf, g_buf = (s0, s1), (d0, d1), (p0, p1), (g0, g1)

        @pl.when(kj == 0)
        def _():
            dq_ref[...] = jnp.zeros_like(dq_ref)

        def q_start(t):
            return pl.multiple_of((nq - 1 - t) * tq, tq)

        def matmuls(t, slot):
            qs = q_start(t)
            s_buf[slot][...] = _dot_nt(k_ref[0], q_ref[0, pl.ds(qs, tq), :])
            dp_buf[slot][...] = _dot_nt(v_ref[0], do_ref[pl.ds(qs, tq), :])

        def pointwise(t, slot, masked):
            qs = q_start(t)
            lse2 = lse_ref[0, :, pl.ds(qs, tq)] * LOG2E
            dl = dl_ref[0, :, pl.ds(qs, tq)]
            for r0 in range(0, tk, rc):
                st = s_buf[slot][r0:r0 + rc, :]
                if masked:
                    row = lax.broadcasted_iota(jnp.int32, (rc, tq), 0)
                    col = lax.broadcasted_iota(jnp.int32, (rc, tq), 1)
                    st = jnp.where(row + r0 <= col, st, NEG)
                pt = jnp.exp2(st * EXP2_SCALE - lse2)
                p_buf[slot][r0:r0 + rc, :] = pt.astype(BF16)
                g_buf[slot][r0:r0 + rc, :] = (pt * (dp_buf[slot][r0:r0 + rc, :] - dl) * SCALE).astype(BF16)

        def accumulate(t, slot):
            qs = q_start(t)
            dv_acc[...] += _dot(p_buf[slot][...], do_ref[pl.ds(qs, tq), :])
            dk_acc[...] += _dot(g_buf[slot][...], q_ref[0, pl.ds(qs, tq), :])
            dq_ref[0, pl.ds(qs, tq), :] += _dot_tn(g_buf[slot][...], k_ref[0])

        def last():
            dk_ref[0] = dk_acc[...]
            dv_ref[0] = dv_acc[...]

        dk_acc[...] = jnp.zeros_like(dk_acc)
        dv_acc[...] = jnp.zeros_like(dv_acc)
        p1[...] = jnp.zeros_like(p1)
        g1[...] = jnp.zeros_like(g1)
        _chunk_pipeline(nq - 1 - kj, matmuls, pointwise, accumulate, last)

    return pl.pallas_call(
        body, name="attn_bwd", grid=(N_HEADS, T // tk),
        in_specs=[pl.BlockSpec((1, T, HEAD_PAD), lambda h, j: (h, 0, 0)),
                  pl.BlockSpec((1, tk, HEAD_PAD), lambda h, j: (h, j, 0)),
                  pl.BlockSpec((1, tk, V_DIM), lambda h, j: (h, j, 0)),
                  pl.BlockSpec((T, V_DIM), lambda h, j: (0, h)),
                  pl.BlockSpec((1, 1, T), lambda h, j: (h, 0, 0)),
                  pl.BlockSpec((1, 1, T), lambda h, j: (h, 0, 0))],
        out_specs=[pl.BlockSpec((1, T, HEAD_PAD), lambda h, j: (h, 0, 0)),
                   pl.BlockSpec((1, tk, HEAD_PAD), lambda h, j: (h, j, 0)),
                   pl.BlockSpec((1, tk, V_DIM), lambda h, j: (h, j, 0))],
        out_shape=[jax.ShapeDtypeStruct((N_HEADS, T, HEAD_PAD), F32), jax.ShapeDtypeStruct((N_HEADS, T, HEAD_PAD), F32),
                   jax.ShapeDtypeStruct((N_HEADS, T, V_DIM), F32)],
        scratch_shapes=[pltpu.VMEM((tk, tq), F32)] * 4 + [pltpu.VMEM((tk, tq), BF16)] * 4
                       + [pltpu.VMEM((tk, HEAD_PAD), F32), pltpu.VMEM((tk, V_DIM), F32)],
        compiler_params=_params(dimension_semantics=("arbitrary", "arbitrary")),
    )(q, k, v, do, lse_row, delta_row)


def _bwd_proj(x, dx1, pos, proj, dq, dk, dv, dtail, du, g_in, w_in, g_cq, w_uq, g_ckv, w_ukv, gq, gk, conv_w,
              invf, sgn, tm):
    T = x.shape[0]
    nt = T // tm

    ts = min(SUB_TILE, tm)

    def body(x_ref, dx1_ref, pos_ref, lat_ref, cc_ref, cx_ref, dq_ref, dk_ref, dv_ref, dtail_ref, du_ref, dun_ref, *rest):
        consts, (gx_ref, h_ref, dproj_ref), sums = rest[:11], rest[11:14], rest[14:]
        cw_ref = consts[8]
        i = pl.program_id(0)

        @pl.when(i == 0)
        def _():
            for r in sums:
                r[...] = jnp.zeros_like(r)

        du_v = du_ref[...]
        not_last = jnp.where(i < nt - 1, 1.0, 0.0)
        nx0 = dun_ref[0:1, :] * not_last
        nx1 = dun_ref[1:2, :] * not_last
        row = lax.broadcasted_iota(jnp.int32, du_v.shape, 0)
        du1 = jnp.where(row == tm - 1, nx0, pltpu.roll(du_v, tm - 1, 0))
        du2 = jnp.where(row == tm - 2, nx0, jnp.where(row == tm - 1, nx1, pltpu.roll(du_v, tm - 2, 0)))
        dvc = cw_ref[2:3, :] * du_v + cw_ref[1:2, :] * du1 + cw_ref[0:1, :] * du2
        dproj_ref[:, 1536:2048] = (dvc * cx_ref[...]).astype(BF16)
        dproj_ref[:, 2048:2560] = (dvc * cc_ref[...]).astype(BF16)

        for r0 in range(0, tm, ts):
            rows = slice(r0, r0 + ts)
            work(x_ref.at[rows, :], dx1_ref.at[rows, :], pos_ref.at[rows, :], lat_ref.at[rows, :],
                 dq_ref.at[:, rows, :], dk_ref.at[:, rows, :], dv_ref.at[:, rows, :], dtail_ref.at[rows, :], *consts,
                 gx_ref.at[rows, :], h_ref.at[rows, :], dproj_ref.at[rows, :], *sums)

    def work(x_ref, dx1_ref, pos_ref, lat_ref, dq_ref, dk_ref, dv_ref, dtail_ref,
             g_in_ref, w_in_ref, g_cq_ref, w_uq_ref, g_ckv_ref, w_ukv_ref, gq_ref, gk_ref, cw_ref, invf_ref, sgn_ref,
             gx_ref, h_ref, dproj_ref, dw_uq_ref, dw_ukv_ref, dg_in_ref, dg_cq_ref, dg_ckv_ref, dgq_ref, dgk_ref):
        xv = x_ref[...]
        r0 = _rep(_inv_rms_mxu(xv), D_MODEL)
        xh0 = xv * r0
        g_in = g_in_ref[...]
        h_ref[...] = (xh0 * g_in).astype(BF16)

        c_q = lat_ref[:, 0:Q_LORA]
        rq = _rep(_inv_rms_mxu(c_q), Q_LORA)
        xq = c_q * rq
        g_cq = g_cq_ref[...]
        cqn = (xq * g_cq).astype(BF16)
        c_kv = lat_ref[:, Q_LORA:Q_LORA + KV_LORA]
        rkv = _inv_rms_mxu(c_kv)
        xkv = c_kv * rkv
        g_ckv = g_ckv_ref[...]
        ckvn = (xkv * g_ckv).astype(BF16)
        kpe = lat_ref[:, 384:512]
        kpe_sq = kpe * kpe
        cos_b, sin_b = _rope_tables(pos_ref, invf_ref, sgn_ref)
        gq_a, gq_b = gq_ref[:, 0:NOPE], gq_ref[:, NOPE:HEAD_PAD]
        gk_a, gk_b = gk_ref[:, 0:NOPE], gk_ref[:, NOPE:HEAD_PAD]

        dkpe = jnp.zeros((ts, LANES), F32)
        dcqn = jnp.zeros((ts, Q_LORA), F32)
        dckvn = jnp.zeros((ts, KV_LORA), F32)
        for hd in range(N_HEADS):
            c0 = hd * HEAD_PAD
            qh = _dot(cqn, w_uq_ref[:, c0:c0 + HEAD_PAD])
            a, b = qh[:, 0:NOPE], qh[:, NOPE:HEAD_PAD]
            r = lax.rsqrt(_lane_sum(a * a + b * b) / QK_DIM + EPS)
            xa, xb = a * r, b * r
            dan = dq_ref[hd, :, 0:NOPE]
            dbr = dq_ref[hd, :, NOPE:HEAD_PAD]
            dbn = dbr * cos_b + _swap_rope_halves(dbr * sin_b)
            dgq_ref[:, 0:NOPE] += _colsum(dan * xa)
            dgq_ref[:, NOPE:HEAD_PAD] += _colsum(dbn * xb)
            dxa, dxb = dan * gq_a, dbn * gq_b
            cq = _lane_sum(dxa * xa + dxb * xb) / QK_DIM
            dqh = jnp.concatenate([r * (dxa - xa * cq), r * (dxb - xb * cq)], axis=-1).astype(BF16)
            dw_uq_ref[:, c0:c0 + HEAD_PAD] += _dot_tn(cqn, dqh)
            dcqn = dcqn + _dot_nt(dqh, w_uq_ref[:, c0:c0 + HEAD_PAD])
            kvh = _dot(ckvn, w_ukv_ref[:, c0:c0 + HEAD_PAD])
            ka = kvh[:, 0:NOPE]
            rk = lax.rsqrt(_lane_sum(ka * ka + kpe_sq) / QK_DIM + EPS)
            xka, xkb = ka * rk, kpe * rk
            dkan = dk_ref[hd, :, 0:NOPE]
            dkbr = dk_ref[hd, :, NOPE:HEAD_PAD]
            dkbn = dkbr * cos_b + _swap_rope_halves(dkbr * sin_b)
            dgk_ref[:, 0:NOPE] += _colsum(dkan * xka)
            dgk_ref[:, NOPE:HEAD_PAD] += _colsum(dkbn * xkb)
            dxka, dxkb = dkan * gk_a, dkbn * gk_b
            ck = _lane_sum(dxka * xka + dxkb * xkb) / QK_DIM
            dkpe = dkpe + rk * (dxkb - xkb * ck)
            dkvh = jnp.concatenate([rk * (dxka - xka * ck), dv_ref[hd]], axis=-1).astype(BF16)
            dw_ukv_ref[:, c0:c0 + HEAD_PAD] += _dot_tn(ckvn, dkvh)
            dckvn = dckvn + _dot_nt(dkvh, w_ukv_ref[:, c0:c0 + HEAD_PAD])

        dg_cq_ref[...] += _colsum(dcqn * xq)
        dxq = dcqn * g_cq
        dproj_ref[:, 0:Q_LORA] = (rq * (dxq - xq * _rep(_lane_sum(dxq * xq) / Q_LORA, Q_LORA))).astype(BF16)
        dg_ckv_ref[...] += _colsum(dckvn * xkv)
        dxkv = dckvn * g_ckv
        dproj_ref[:, 256:384] = (rkv * (dxkv - xkv * (_lane_sum(dxkv * xkv) / KV_LORA))).astype(BF16)
        dproj_ref[:, 384:512] = dkpe.astype(BF16)
        dproj_ref[:, 512:1536] = dtail_ref[:, 0:1024]
        dproj_ref[:, 2560:3072] = dtail_ref[:, 1024:1536]

        dh = jnp.zeros((ts, D_MODEL), F32)
        for c0 in range(0, PROJ_EXT, 512):
            dh = dh + _dot_nt(dproj_ref[:, c0:c0 + 512], w_in_ref[:, c0:c0 + 512])
        dg_in_ref[...] += _colsum(dh * xh0)
        dxh = dh * g_in
        gx_ref[...] = dx1_ref[...] + r0 * (dxh - xh0 * _rep(_lane_sum(dxh * xh0) / D_MODEL, D_MODEL))

    row = lambda i: (i, 0)
    col = lambda c: (lambda i: (i, c))
    head_rows = lambda i: (0, i, 0)
    nxt = lambda i: (jnp.minimum((i + 1) * (tm // 8), T // 8 - 1), 0)
    in_specs = [pl.BlockSpec((tm, D_MODEL), row), pl.BlockSpec((tm, D_MODEL), row), pl.BlockSpec((tm, 1), row),
                pl.BlockSpec((tm, 512), col(0)), pl.BlockSpec((tm, 512), col(3)), pl.BlockSpec((tm, 512), col(4)),
                pl.BlockSpec((N_HEADS, tm, HEAD_PAD), head_rows), pl.BlockSpec((N_HEADS, tm, HEAD_PAD), head_rows),
                pl.BlockSpec((N_HEADS, tm, V_DIM), head_rows), pl.BlockSpec((tm, 1536), row),
                pl.BlockSpec((tm, CONV_W), row), pl.BlockSpec((8, CONV_W), nxt),
                _full((1, D_MODEL)), _full((D_MODEL, PROJ_EXT)), _full((1, Q_LORA)), _full((Q_LORA, N_HEADS * HEAD_PAD)),
                _full((1, KV_LORA)), _full((KV_LORA, N_HEADS * HEAD_PAD)), _full((1, HEAD_PAD)), _full((1, HEAD_PAD)),
                _full((3, CONV_W)), _full((1, LANES)), _full((1, LANES))]
    out_specs = [pl.BlockSpec((tm, D_MODEL), row), pl.BlockSpec((tm, D_MODEL), row), pl.BlockSpec((tm, PROJ_EXT), row),
                 _full((Q_LORA, N_HEADS * HEAD_PAD)), _full((KV_LORA, N_HEADS * HEAD_PAD)),
                 _full((1, D_MODEL)), _full((1, Q_LORA)), _full((1, KV_LORA)), _full((1, HEAD_PAD)), _full((1, HEAD_PAD))]
    out_shape = [jax.ShapeDtypeStruct((T, D_MODEL), F32), jax.ShapeDtypeStruct((T, D_MODEL), BF16),
                 jax.ShapeDtypeStruct((T, PROJ_EXT), BF16),
                 jax.ShapeDtypeStruct((Q_LORA, N_HEADS * HEAD_PAD), F32), jax.ShapeDtypeStruct((KV_LORA, N_HEADS * HEAD_PAD), F32),
                 jax.ShapeDtypeStruct((1, D_MODEL), F32), jax.ShapeDtypeStruct((1, Q_LORA), F32),
                 jax.ShapeDtypeStruct((1, KV_LORA), F32), jax.ShapeDtypeStruct((1, HEAD_PAD), F32),
                 jax.ShapeDtypeStruct((1, HEAD_PAD), F32)]
    return pl.pallas_call(
        body, name="bwd_proj", grid=(nt,), in_specs=in_specs, out_specs=out_specs, out_shape=out_shape,
        compiler_params=_params(dimension_semantics=("arbitrary",)),
    )(x, dx1, pos, proj, proj, proj, dq, dk, dv, dtail, du, du, g_in, w_in, g_cq, w_uq, g_ckv, w_ukv, gq, gk, conv_w,
      invf, sgn)


def _matmul_tn(a, b, tt, tn):
    T, M = a.shape
    N = b.shape[1]

    def body(a_ref, b_ref, o_ref):
        @pl.when(pl.program_id(1) == 0)
        def _():
            o_ref[...] = jnp.zeros_like(o_ref)

        o_ref[...] += _dot_tn(a_ref[...], b_ref[...])

    return pl.pallas_call(
        body, name="dw_in", grid=(N // tn, T // tt),
        in_specs=[pl.BlockSpec((tt, M), lambda j, t: (t, 0)), pl.BlockSpec((tt, tn), lambda j, t: (t, j))],
        out_specs=pl.BlockSpec((M, tn), lambda j, t: (0, j)),
        out_shape=jax.ShapeDtypeStruct((M, N), F32),
        compiler_params=_params(dimension_semantics=("arbitrary", "arbitrary")),
    )(a, b)


def _add_pair(grads, from_sibling, small, small_sibling, c):
    n = len(grads)

    def body(c_ref, *refs):
        ins, outs = refs[:2 * n + 2], refs[2 * n + 2:]
        for i in range(n + 1):
            outs[i][...] = (ins[2 * i][...] + ins[2 * i + 1][...]).astype(outs[i].dtype)

    in_specs, out_specs, out_shape, args = [], [], [], []
    for g, r in zip(grads, from_sibling):
        _, hr, cols = r.shape
        in_specs += [pl.BlockSpec((1, hr, cols), lambda k, c_ref: (k, c_ref[0], 0)),
                     pl.BlockSpec((1, hr, cols), lambda k, c_ref: (k, 0, 0))]
        out_specs.append(pl.BlockSpec((1, hr, cols), lambda k, c_ref: (k, 0, 0)))
        out_shape.append(jax.ShapeDtypeStruct(r.shape, BF16))
        args += [g, r]
    whole = pl.BlockSpec(small.shape, lambda k, c_ref: (0, 0))
    in_specs += [whole, whole]
    out_specs.append(whole)
    out_shape.append(jax.ShapeDtypeStruct(small.shape, F32))
    outs = pl.pallas_call(
        body, name="add_pair", out_shape=out_shape,
        grid_spec=pltpu.PrefetchScalarGridSpec(num_scalar_prefetch=1, grid=(N_CHIPS,), in_specs=in_specs,
                                               out_specs=out_specs),
        compiler_params=_params(dimension_semantics=("arbitrary",)),
    )(c.reshape(1), *args, small, small_sibling)
    return outs[:n], outs[n]


def _add_chips(parts, small_parts):
    arrays = list(parts) + [small_parts]

    def body(*refs):
        ins, outs = refs[:len(arrays)], refs[len(arrays):]
        for a_ref, o_ref in zip(ins, outs):
            part = lambda k: a_ref[k].astype(F32)
            o_ref[...] = ((part(0) + part(1)) + part(2)) + part(3)

    in_specs, out_specs, out_shape = [], [], []
    for a in arrays:
        _, rows, cols = a.shape
        in_specs.append(pl.BlockSpec((N_CHIPS, rows // 2, cols), lambda i: (0, i, 0)))
        out_specs.append(pl.BlockSpec((rows // 2, cols), lambda i: (i, 0)))
        out_shape.append(jax.ShapeDtypeStruct((rows, cols), F32))
    outs = pl.pallas_call(body, name="add_chips", grid=(2,), in_specs=in_specs, out_specs=out_specs,
                          out_shape=out_shape, compiler_params=_params(dimension_semantics=("arbitrary",)))(*arrays)
    return outs[:-1], outs[-1]


def _adamw(w, g, m, v, name):
    rows, cols = w.shape
    rb = 256 if rows * cols > 512 * 1024 else rows

    def body(w_ref, g_ref, m_ref, v_ref, d_ref, nm_ref, nv_ref):
        _adamw_math(g_ref[...], w_ref, m_ref, v_ref, d_ref, nm_ref, nv_ref)

    spec = pl.BlockSpec((rb, cols), lambda i: (i, 0))
    shp = jax.ShapeDtypeStruct(w.shape, F32)
    return pl.pallas_call(body, name=name, grid=(rows // rb,), in_specs=[spec] * 4, out_specs=[spec] * 3,
                          out_shape=[shp] * 3)(w, g, m, v)


def _adamw_math(gv, w_ref, m_ref, v_ref, d_ref, nm_ref, nv_ref):
    nm = B1 * m_ref[...] + (1.0 - B1) * gv
    nv = B2 * v_ref[...] + (1.0 - B2) * (gv * gv)
    m_hat = nm / (1.0 - B1 ** STEP)
    v_hat = nv / (1.0 - B2 ** STEP)
    d_ref[...] = -LR * (m_hat / (jnp.sqrt(v_hat) + ADAM_EPS) + WD * w_ref[...])
    nm_ref[...] = nm
    nv_ref[...] = nv


def _adamw_halves(w, mine, other, m, v, c, name):
    hr, cols = mine.shape

    def body(c_ref, w_ref, mine_ref, other_ref, m_ref, v_ref, g_ref, d_ref, nm_ref, nv_ref):
        gv = jnp.where(pl.program_id(0) == c_ref[0], mine_ref[...], other_ref[...])
        g_ref[...] = gv
        _adamw_math(gv, w_ref, m_ref, v_ref, d_ref, nm_ref, nv_ref)

    half = pl.BlockSpec((hr, cols), lambda i, c_ref: (i, 0))
    whole = pl.BlockSpec((hr, cols), lambda i, c_ref: (0, 0))
    shp = jax.ShapeDtypeStruct(w.shape, F32)
    return pl.pallas_call(
        body, name=name, out_shape=[shp] * 4,
        grid_spec=pltpu.PrefetchScalarGridSpec(num_scalar_prefetch=1, grid=(2,), in_specs=[half, whole, whole, half, half],
                                               out_specs=[half] * 4),
        compiler_params=_params(dimension_semantics=("arbitrary",)),
    )(c.reshape(1), w, mine, other, m, v)


_ANY = pl.BlockSpec(memory_space=pl.ANY)


def _mesh_pos():
    return lax.axis_index("x"), lax.axis_index("y"), lax.axis_index("c")


def _other_chips(x, y):
    return [(1 - x, y), (x, 1 - y), (1 - x, 1 - y)]


def _remote(src, dst, send_sems, recv_sems, k, to):
    return pltpu.make_async_remote_copy(src_ref=src, dst_ref=dst, send_sem=send_sems.at[k], recv_sem=recv_sems.at[k],
                                        device_id=to, device_id_type=MESH)


def _gather_weights(shards):
    n = len(shards)
    halved = [s.shape[0] % 32 == 0 for s in shards]

    def body(*refs):
        ins, outs, stage = refs[:n], refs[n:2 * n], refs[2 * n:3 * n]
        send_sems, recv_sems, local_sems = refs[3 * n:]
        x, y, c = _mesh_pos()
        me = 2 * x + y
        chips = _other_chips(x, y)

        def part(i, ref, hc):
            if not halved[i]:
                return ref
            hr = shards[i].shape[0] // 2
            return ref.at[pl.ds(hc * hr, hr), :]

        locals_, started = [], []
        for i in range(n):
            stage[i][...] = ins[i][...].astype(BF16)
            mine = pltpu.make_async_copy(stage[i], outs[i].at[me], local_sems.at[i])
            mine.start()
            locals_.append(mine)
            for j, (cx, cy) in enumerate(chips):
                cp = _remote(part(i, stage[i], c), part(i, outs[i].at[me], c), send_sems, recv_sems, 6 * i + j, (cx, cy, c))
                cp.start()
                started.append(cp)
        for i in range(n):
            for j, (cx, cy) in enumerate(chips):
                got = part(i, outs[i].at[2 * cx + cy], c)
                _remote(got, got, send_sems, recv_sems, 6 * i + j, (cx, cy, c)).wait_recv()
                if halved[i]:
                    fwd = _remote(got, got, send_sems, recv_sems, 6 * i + 3 + j, (x, y, 1 - c))
                    fwd.start()
                    started.append(fwd)
        for i in range(n):
            if halved[i]:
                for j, (cx, cy) in enumerate(chips):
                    got = part(i, outs[i].at[2 * cx + cy], 1 - c)
                    _remote(got, got, send_sems, recv_sems, 6 * i + 3 + j, (x, y, 1 - c)).wait_recv()
        for cp in started:
            cp.wait_send()
        for cp in locals_:
            cp.wait()

    vmem = pl.BlockSpec(memory_space=pltpu.VMEM)
    return pl.pallas_call(
        body, name="gather_weights", in_specs=[vmem] * n, out_specs=[_ANY] * n,
        out_shape=[jax.ShapeDtypeStruct((N_CHIPS,) + s.shape, BF16) for s in shards],
        scratch_shapes=[pltpu.VMEM(s.shape, BF16) for s in shards]
                       + [pltpu.SemaphoreType.DMA((6 * n,)), pltpu.SemaphoreType.DMA((6 * n,)), pltpu.SemaphoreType.DMA((n,))],
        compiler_params=_params(),
    )(*shards)


def _swap_halves(grads, small):
    n = len(grads)
    arrays = list(grads) + [small]

    def body(*refs):
        ins, outs, send_sems, recv_sems = refs[:n + 1], refs[n + 1:2 * n + 2], refs[2 * n + 2], refs[2 * n + 3]
        x, y, c = _mesh_pos()
        cps = []
        for i in range(n + 1):
            src = ins[i]
            if i < n:
                hr = grads[i].shape[1] // 2
                src = src.at[:, pl.ds((1 - c) * hr, hr), :]
            cp = _remote(src, outs[i], send_sems, recv_sems, i, (x, y, 1 - c))
            cp.start()
            cps.append(cp)
        for cp in cps:
            cp.wait()

    out_shape = [jax.ShapeDtypeStruct((g.shape[0], g.shape[1] // 2, g.shape[2]), F32) for g in grads]
    out_shape.append(jax.ShapeDtypeStruct(small.shape, F32))
    outs = pl.pallas_call(
        body, name="pair_grads", in_specs=[_ANY] * (n + 1), out_specs=[_ANY] * (n + 1), out_shape=out_shape,
        scratch_shapes=[pltpu.SemaphoreType.DMA((n + 1,)), pltpu.SemaphoreType.DMA((n + 1,))],
    )(*arrays)
    return outs[:n], outs[n]


def _scatter_to_chips(parts, small):
    n = len(parts)
    arrays = list(parts) + [small]

    def body(*refs):
        ins, outs = refs[:n + 1], refs[n + 1:2 * n + 2]
        send_sems, recv_sems, local_sems = refs[2 * n + 2:]
        x, y, c = _mesh_pos()
        me = 2 * x + y
        chips = _other_chips(x, y)
        locals_, sends = [], []
        for i in range(n + 1):
            mine = pltpu.make_async_copy(ins[i].at[me] if i < n else ins[i], outs[i].at[me], local_sems.at[i])
            mine.start()
            locals_.append(mine)
            for j, (cx, cy) in enumerate(chips):
                src = ins[i].at[2 * cx + cy] if i < n else ins[i]
                cp = _remote(src, outs[i].at[me], send_sems, recv_sems, 3 * i + j, (cx, cy, c))
                cp.start()
                sends.append(cp)
        for i in range(n + 1):
            for j, (cx, cy) in enumerate(chips):
                got = outs[i].at[2 * cx + cy]
                _remote(got, got, send_sems, recv_sems, 3 * i + j, (cx, cy, c)).wait_recv()
        for cp in sends:
            cp.wait_send()
        for cp in locals_:
            cp.wait()

    out_shape = [jax.ShapeDtypeStruct(p.shape, p.dtype) for p in parts]
    out_shape.append(jax.ShapeDtypeStruct((N_CHIPS,) + small.shape, small.dtype))
    outs = pl.pallas_call(
        body, name="scatter_grads", in_specs=[_ANY] * (n + 1), out_specs=[_ANY] * (n + 1), out_shape=out_shape,
        scratch_shapes=[pltpu.SemaphoreType.DMA((3 * n + 3,)), pltpu.SemaphoreType.DMA((3 * n + 3,)),
                        pltpu.SemaphoreType.DMA((n + 1,))],
    )(*arrays)
    return outs[:n], outs[n]


def _share_halves(halves):
    n = len(halves)

    def body(*refs):
        ins, outs, send_sems, recv_sems = refs[:n], refs[n:2 * n], refs[2 * n], refs[2 * n + 1]
        x, y, c = _mesh_pos()
        cps = [_remote(ins[i], outs[i], send_sems, recv_sems, i, (x, y, 1 - c)) for i in range(n)]
        for cp in cps:
            cp.start()
        for cp in cps:
            cp.wait()

    return pl.pallas_call(
        body, name="share_halves", in_specs=[_ANY] * n, out_specs=[_ANY] * n,
        out_shape=[jax.ShapeDtypeStruct(h.shape, h.dtype) for h in halves],
        scratch_shapes=[pltpu.SemaphoreType.DMA((n,)), pltpu.SemaphoreType.DMA((n,))],
    )(*halves)


SHARD_COLS_IN = IN_TOTAL // N_CHIPS
KPE_END = Q_LORA + KV_LORA + ROPE


def _assemble_weights(c_in, c_uq, c_ukv, c_o, c_pl, c_plg, c_conv):
    by_cols = lambda a: a.transpose(1, 0, 2).reshape(a.shape[1], N_CHIPS * a.shape[2])
    w_in_e = jnp.concatenate([c_in[0][:, :KPE_END], jnp.zeros((D_MODEL, 64), BF16), c_in[0][:, KPE_END:],
                              c_in[1], c_in[2], c_in[3]], axis=1)
    w_uq_e = by_cols(jnp.pad(c_uq, ((0, 0), (0, 0), (0, HEAD_PAD - QK_DIM))))
    return (w_in_e, w_uq_e, by_cols(c_ukv), by_cols(c_conv).astype(F32), c_o.reshape(D_MODEL, D_MODEL),
            by_cols(c_pl), c_plg.reshape(D_MODEL, D_MODEL))


def _split_grads(dw_in_e, dw_uq_e, dw_ukv, dw_o, dw_pl, dw_plg):
    chip_major = lambda a: a.reshape(a.shape[0], N_CHIPS, a.shape[1] // N_CHIPS).transpose(1, 0, 2)
    first = jnp.concatenate([dw_in_e[:, :KPE_END], dw_in_e[:, KPE_END + 64:SHARD_COLS_IN + 64]], axis=1)
    rest = [dw_in_e[:, SHARD_COLS_IN * k + 64:SHARD_COLS_IN * (k + 1) + 64] for k in range(1, N_CHIPS)]
    return [jnp.stack([first] + rest), chip_major(dw_uq_e)[:, :, :QK_DIM], chip_major(dw_ukv),
            dw_o.reshape(N_CHIPS, D_MODEL // N_CHIPS, D_MODEL), chip_major(dw_pl),
            dw_plg.reshape(N_CHIPS, D_MODEL // N_CHIPS, D_MODEL)]


def _local_step(x, p, pos, tgt, gains, w_in_e, w_uq_e, w_ukv, conv_w, w_o, w_pl, w_plg, tm, tq):
    g_in, g_cq, g_ckv, g_q, g_k, g_oa, g_oc, g_pl = gains
    T = x.shape[0]
    zpad = lambda a, n: jnp.concatenate([a, jnp.zeros(a.shape[:-1] + (n,), a.dtype)], axis=-1)
    gq, gk = zpad(g_q, HEAD_PAD - QK_DIM), zpad(g_k, HEAD_PAD - QK_DIM)
    inv_freq = 1.0 / (ROPE_THETA ** (jnp.arange(0, ROPE, 2, dtype=F32) / ROPE))
    invf = jnp.concatenate([inv_freq, inv_freq, jnp.zeros((64,), F32)]).reshape(1, LANES)
    sgn = jnp.concatenate([-jnp.ones((32,), F32), jnp.ones((32,), F32), jnp.zeros((64,), F32)]).reshape(1, LANES)

    proj, q, k, v = _fwd_proj(x, pos, g_in, w_in_e, g_cq, w_uq_e, g_ckv, w_ukv, gq, gk, invf, sgn, min(2 * tm, T))
    o, lse = _attn_fwd(q, k, v, tq)
    (dx1, do, delta, dtail, du, dw_o, dw_pl, dw_plg, dg_oa, dg_oc, dg_pl, dconv, loss) = _tail(
        x, o, proj, p, tgt, g_oa, g_oc, g_pl, conv_w, w_o, w_pl, w_plg, tm)
    lse_row = lse[:, :, 0].reshape(N_HEADS, 1, T)
    delta_row = delta[:, :, 0].reshape(N_HEADS, 1, T)
    dq, dk, dv = _attn_bwd(q, k, v, do, lse_row, delta_row, tq)
    (gx, h, dproj, dw_uq_e, dw_ukv, dg_in, dg_cq, dg_ckv, dgq, dgk) = _bwd_proj(
        x, dx1, pos, proj, dq, dk, dv, dtail, du, g_in, w_in_e, g_cq, w_uq_e, g_ckv, w_ukv, gq, gk, conv_w, invf, sgn,
        min(2 * tm, T))
    dw_in_e = _matmul_tn(h, dproj, min(512, T), 512)
    wgrads = (dw_in_e, dw_uq_e, dw_ukv, dw_o, dw_pl, dw_plg)
    ggrads = (dg_in, dg_cq, dg_ckv, dgq, dgk, dg_oa, dg_oc, dg_pl)
    return loss, gx, wgrads, ggrads, dconv


def kernel(x, p, positions, g_in, w_in, g_cq, w_uq, g_ckv, w_ukv, g_q, g_k, conv_w, g_oa, g_oc, w_o, w_pl, w_plg, g_pl, loss_target, m_g_in, m_w_in, m_g_cq, m_w_uq, m_g_ckv, m_w_ukv, m_g_q, m_g_k, m_conv_w, m_g_oa, m_g_oc, m_w_o, m_w_pl, m_w_plg, m_g_pl, v_g_in, v_w_in, v_g_cq, v_w_uq, v_g_ckv, v_w_ukv, v_g_q, v_g_k, v_conv_w, v_g_oa, v_g_oc, v_w_o, v_w_pl, v_w_plg, v_g_pl):
    T = x.shape[1]
    c = lax.axis_index("c")
    chip = 2 * lax.axis_index("x") + lax.axis_index("y")
    gains = [g.reshape(1, -1) for g in (g_in, g_cq, g_ckv, g_q, g_k, g_oa, g_oc, g_pl)]

    gathered = _gather_weights([w_in[0], w_uq[0], w_ukv[0], w_o[0], w_pl[0], w_plg[0], conv_w[0]])
    full = _assemble_weights(*gathered)

    loss, gx, wgrads, ggrads, dconv = _local_step(
        x[0], p[0, 0], positions.reshape(T, 1), loss_target[0], gains, *full, 256, 512)

    grads_cm = _split_grads(*wgrads)
    small_parts = [a.reshape(-1, LANES) for a in (*ggrads, loss, dconv)]
    small_rows = [a.shape[0] for a in small_parts]
    tile_rows = [-(-r // 8) * 8 for r in small_rows]
    tile_rows[-1] += -sum(tile_rows) % 16
    small = jnp.concatenate([jnp.pad(a, ((0, t - r), (0, 0))) for a, r, t in zip(small_parts, small_rows, tile_rows)])
    from_sibling, small_sibling = _swap_halves(grads_cm, small)
    chip_parts, chip_small = _add_pair(grads_cm, from_sibling, small, small_sibling, c)
    by_chip, small_by_chip = _scatter_to_chips(chip_parts, chip_small)
    halves, small_total = _add_chips(by_chip, small_by_chip)
    other_halves = _share_halves(halves)

    gg, off = [], 0
    for rows, tiled in zip(small_rows, tile_rows):
        gg.append(small_total[off:off + rows].reshape(1, -1))
        off += tiled
    loss_out = gg[8][0, 0]
    conv_total = gg[9].reshape(3, CONV_W)
    conv_g = lax.dynamic_slice(conv_total, (0, chip * (CONV_W // N_CHIPS)), (3, CONV_W // N_CHIPS))
    g_by_name = dict(g_in=gg[0], g_cq=gg[1], g_ckv=gg[2], g_q=gg[3][:, :QK_DIM], g_k=gg[4][:, :QK_DIM], conv_w=conv_g,
                     g_oa=gg[5], g_oc=gg[6], g_pl=gg[7])
    half_by_name = dict(zip(("w_in", "w_uq", "w_ukv", "w_o", "w_pl", "w_plg"), zip(halves, other_halves)))
    weights = dict(g_in=g_in, w_in=w_in, g_cq=g_cq, w_uq=w_uq, g_ckv=g_ckv, w_ukv=w_ukv, g_q=g_q, g_k=g_k,
                   conv_w=conv_w, g_oa=g_oa, g_oc=g_oc, w_o=w_o, w_pl=w_pl, w_plg=w_plg, g_pl=g_pl)
    ms = dict(g_in=m_g_in, w_in=m_w_in, g_cq=m_g_cq, w_uq=m_w_uq, g_ckv=m_g_ckv, w_ukv=m_w_ukv, g_q=m_g_q, g_k=m_g_k,
              conv_w=m_conv_w, g_oa=m_g_oa, g_oc=m_g_oc, w_o=m_w_o, w_pl=m_w_pl, w_plg=m_w_plg, g_pl=m_g_pl)
    vs = dict(g_in=v_g_in, w_in=v_w_in, g_cq=v_g_cq, w_uq=v_w_uq, g_ckv=v_g_ckv, w_ukv=v_w_ukv, g_q=v_g_q, g_k=v_g_k,
              conv_w=v_conv_w, g_oa=v_g_oa, g_oc=v_g_oc, w_o=v_w_o, w_pl=v_w_pl, w_plg=v_w_plg, g_pl=v_g_pl)
    names = list(weights)
    grads, deltas, new_m, new_v = [], [], [], []
    for n in names:
        w = weights[n]
        w2 = w.reshape(-1, w.shape[-1])
        if n in half_by_name:
            g2, d, nm, nv = _adamw_halves(w2, *half_by_name[n], ms[n].reshape(w2.shape), vs[n].reshape(w2.shape), c,
                                          "adamw_" + n)
        else:
            g2 = g_by_name[n].reshape(w2.shape)
            d, nm, nv = _adamw(w2, g2, ms[n].reshape(w2.shape), vs[n].reshape(w2.shape), "adamw_" + n)
        grads.append(g2.reshape(w.shape))
        deltas.append(d.reshape(w.shape))
        new_m.append(nm.reshape(w.shape))
        new_v.append(nv.reshape(w.shape))
    return (loss_out, gx.reshape(x.shape), *grads, *deltas, *new_m, *new_v)
```

```python
import functools
import math

import jax
import jax.numpy as jnp
from jax import lax
from jax.experimental import pallas as pl
from jax.experimental.pallas import tpu as pltpu

F32 = jnp.float32
BF16 = jnp.bfloat16

D_MODEL = 1024
N_HEADS = 4
NOPE = 128
ROPE = 64
V_DIM = 128
QK_DIM = NOPE + ROPE
HEAD_PAD = 256
Q_LORA = 256
KV_LORA = 128
ATTN_W = 512
CONV_W = 512
PLE = 256
IN_TOTAL = 3008
PROJ_EXT = 3072
ROPE_THETA = 10000.0
EPS = 1e-6
SCALE = 1.0 / math.sqrt(QK_DIM)
LOG2E = math.log2(math.e)
EXP2_SCALE = SCALE * LOG2E
NEG = -1e30
SOFTMAX_ROWS = 32
SUB_TILE = 256

LR, B1, B2, ADAM_EPS, WD, STEP = 0.001, 0.9, 0.999, 1e-08, 0.01, 10

N_CHIPS = 4
LANES = 128
VMEM_LIMIT = 56 * 1024 * 1024
MESH = pl.DeviceIdType.MESH


def _params(**kw):
    return pltpu.CompilerParams(vmem_limit_bytes=VMEM_LIMIT, **kw)


def _inv_rms(x, n):
    return lax.rsqrt(jnp.sum(x * x, axis=-1, keepdims=True) / n + EPS)


def _lane_sum(a):
    folded = a[:, 0:LANES]
    for c0 in range(LANES, a.shape[1], LANES):
        folded = folded + a[:, c0:c0 + LANES]
    head = folded.astype(BF16)
    tail = (folded - head.astype(F32)).astype(BF16)
    return _dot(jnp.concatenate([head, tail], axis=1), jnp.ones((2 * LANES, LANES), BF16))


def _inv_rms_mxu(x):
    return lax.rsqrt(_lane_sum(x * x) / x.shape[1] + EPS)


def _rep(r, width):
    return r if width == LANES else jnp.tile(r, (1, width // LANES))


def _sigmoid(z):
    return 1.0 / (1.0 + jnp.exp(-z))


def _swap_rope_halves(b):
    lane = lax.broadcasted_iota(jnp.int32, b.shape, 1)
    swapped = jnp.where(lane < 32, pltpu.roll(b, 96, 1), pltpu.roll(b, 32, 1))
    return jnp.where(lane < ROPE, swapped, 0.0)


def _dot(a, b):
    return jnp.dot(a, b, preferred_element_type=F32)


def _dot_nt(a, b):
    return lax.dot_general(a, b, (((1,), (1,)), ((), ())), preferred_element_type=F32)


def _dot_tn(a, b):
    return lax.dot_general(a, b, (((0,), (0,)), ((), ())), preferred_element_type=F32)


def _colsum(a):
    return jnp.sum(a, axis=0, keepdims=True)


def _full(shape):
    return pl.BlockSpec(shape, lambda *_: (0,) * len(shape))


def _rope_tables(pos_ref, invf_ref, sgn_ref):
    ang = pos_ref[...].astype(F32) * invf_ref[...]
    return jnp.cos(ang), jnp.sin(ang) * sgn_ref[...]


def _fwd_proj(x, pos, g_in, w_in, g_cq, w_uq, g_ckv, w_ukv, gq, gk, invf, sgn, tm):
    T = x.shape[0]

    ts = min(SUB_TILE, tm)

    def body(x_ref, pos_ref, g_in_ref, w_in_ref, g_cq_ref, w_uq_ref, g_ckv_ref, w_ukv_ref, gq_ref, gk_ref,
             invf_ref, sgn_ref, proj_ref, q_ref, k_ref, v_ref):
        for r0 in range(0, tm, ts):
            rows = slice(r0, r0 + ts)
            xv = x_ref[rows, :]
            h = (xv * _rep(_inv_rms_mxu(xv), D_MODEL) * g_in_ref[...]).astype(BF16)
            def project(c0):
                proj_ref[rows, c0:c0 + 512] = _dot(h, w_in_ref[:, c0:c0 + 512])

            lat = _dot(h, w_in_ref[:, 0:512])
            proj_ref[rows, 0:512] = lat
            c_q = lat[:, 0:Q_LORA]
            cqn = (c_q * _rep(_inv_rms_mxu(c_q), Q_LORA) * g_cq_ref[...]).astype(BF16)
            c_kv = lat[:, Q_LORA:Q_LORA + KV_LORA]
            ckvn = (c_kv * _inv_rms_mxu(c_kv) * g_ckv_ref[...]).astype(BF16)
            kpe = lat[:, 384:512]
            kpe_sq = kpe * kpe
            cos_b, sin_b = _rope_tables(pos_ref.at[rows, :], invf_ref, sgn_ref)
            gq_a, gq_b = gq_ref[:, 0:NOPE], gq_ref[:, NOPE:HEAD_PAD]
            gk_a, gk_b = gk_ref[:, 0:NOPE], gk_ref[:, NOPE:HEAD_PAD]
            for hd in range(N_HEADS):
                project(512 * (hd + 1))
                c0 = hd * HEAD_PAD
                qh = _dot(cqn, w_uq_ref[:, c0:c0 + HEAD_PAD])
                a, b = qh[:, 0:NOPE], qh[:, NOPE:HEAD_PAD]
                r = lax.rsqrt(_lane_sum(a * a + b * b) / QK_DIM + EPS)
                bn = b * r * gq_b
                q_ref[hd, rows, 0:NOPE] = (a * r * gq_a).astype(BF16)
                q_ref[hd, rows, NOPE:HEAD_PAD] = (bn * cos_b + _swap_rope_halves(bn) * sin_b).astype(BF16)
                kvh = _dot(ckvn, w_ukv_ref[:, c0:c0 + HEAD_PAD])
                ka = kvh[:, 0:NOPE]
                rk = lax.rsqrt(_lane_sum(ka * ka + kpe_sq) / QK_DIM + EPS)
                kbn = kpe * rk * gk_b
                k_ref[hd, rows, 0:NOPE] = (ka * rk * gk_a).astype(BF16)
                k_ref[hd, rows, NOPE:HEAD_PAD] = (kbn * cos_b + _swap_rope_halves(kbn) * sin_b).astype(BF16)
                v_ref[hd, rows, 0:V_DIM] = kvh[:, NOPE:HEAD_PAD].astype(BF16)
                v_ref[hd, rows, V_DIM:2 * V_DIM] = jnp.ones((ts, V_DIM), BF16)
            project(512 * (N_HEADS + 1))

    row = lambda i: (i, 0)
    head_rows = lambda i: (0, i, 0)
    return pl.pallas_call(
        body, name="fwd_proj", grid=(T // tm,),
        in_specs=[pl.BlockSpec((tm, D_MODEL), row), pl.BlockSpec((tm, 1), row), _full((1, D_MODEL)),
                  _full((D_MODEL, PROJ_EXT)), _full((1, Q_LORA)), _full((Q_LORA, N_HEADS * HEAD_PAD)),
                  _full((1, KV_LORA)), _full((KV_LORA, N_HEADS * HEAD_PAD)), _full((1, HEAD_PAD)), _full((1, HEAD_PAD)),
                  _full((1, LANES)), _full((1, LANES))],
        out_specs=[pl.BlockSpec((tm, PROJ_EXT), row), pl.BlockSpec((N_HEADS, tm, HEAD_PAD), head_rows),
                   pl.BlockSpec((N_HEADS, tm, HEAD_PAD), head_rows), pl.BlockSpec((N_HEADS, tm, 2 * V_DIM), head_rows)],
        out_shape=[jax.ShapeDtypeStruct((T, PROJ_EXT), F32), jax.ShapeDtypeStruct((N_HEADS, T, HEAD_PAD), BF16),
                   jax.ShapeDtypeStruct((N_HEADS, T, HEAD_PAD), BF16), jax.ShapeDtypeStruct((N_HEADS, T, 2 * V_DIM), BF16)],
        compiler_params=_params(dimension_semantics=("arbitrary",)),
    )(x, pos, g_in, w_in, g_cq, w_uq, g_ckv, w_ukv, gq, gk, invf, sgn)


def _chunk_pipeline(n_loop, lag, matmuls, pointwise, accumulate, last):
    slots = lag + 1

    def iteration(t, slot):
        matmuls(jnp.minimum(t + lag, n_loop), (slot + lag) % slots)
        accumulate(jnp.maximum(t - lag, 0), (slot + 1) % slots)
        pointwise(t, slot, False)

    def finish(slot):
        for back in range(lag, 0, -1):
            accumulate(jnp.maximum(n_loop - back, 0), (slot - back) % slots)
        pointwise(n_loop, slot, True)
        accumulate(n_loop, slot)
        last()

    for u in range(lag):
        matmuls(jnp.minimum(u, n_loop), u)

    def unrolled(tt, carry):
        for slot in range(slots):
            iteration(slots * tt + slot, slot)
        return carry

    lax.fori_loop(0, n_loop // slots, unrolled, 0)
    rest = lax.rem(n_loop, slots)
    t0 = n_loop - rest

    for r in range(slots):
        @pl.when(rest == r)
        def _():
            for slot in range(r):
                iteration(t0 + slot, slot)
            finish(r)


def _attn_fwd(q, k, v, tq):
    T = q.shape[1]
    tk = tq
    rc = min(SOFTMAX_ROWS, tq)

    def body(q_ref, k_ref, v_ref, o_ref, lse_ref, s0, s1, s2, p0, p1, p2, a0, a1, a2, m_ref, acc_ref):
        qi = pl.program_id(1)
        s_buf, p_buf, a_buf = (s0, s1, s2), (p0, p1, p2), (a0, a1, a2)

        def scores(t, slot):
            ks = pl.multiple_of(t * tk, tk)
            s_buf[slot][...] = _dot_nt(q_ref[0], k_ref[0, pl.ds(ks, tk), :])

        def values(t, slot):
            ks = pl.multiple_of(t * tk, tk)
            acc_ref[...] = acc_ref[...] * a_buf[slot][...] + _dot(p_buf[slot][...], v_ref[0, pl.ds(ks, tk), :])

        def softmax(t, slot, masked):
            s_all = s_buf[slot][...]
            if masked:
                row = lax.broadcasted_iota(jnp.int32, (tq, tk), 0)
                col = lax.broadcasted_iota(jnp.int32, (tq, tk), 1)
                s_all = jnp.where(col <= row, s_all, NEG)
                s_buf[slot][...] = s_all
            m_old = m_ref[...]
            m_new = jnp.maximum(m_old, jnp.max(s_all, axis=1, keepdims=True))
            a_buf[slot][...] = jnp.exp2((m_old - m_new) * EXP2_SCALE)
            m_ref[...] = m_new
            for r0 in range(0, tq, rc):
                s = s_buf[slot][r0:r0 + rc, :]
                p_buf[slot][r0:r0 + rc, :] = jnp.exp2((s - m_new[r0:r0 + rc, :]) * EXP2_SCALE).astype(BF16)

        def last():
            l = acc_ref[:, V_DIM:2 * V_DIM]
            o_ref[...] = acc_ref[:, 0:V_DIM] / l
            lse_ref[0] = m_ref[...] * SCALE + jnp.log(l)

        m_ref[...] = jnp.full_like(m_ref, NEG)
        acc_ref[...] = jnp.zeros_like(acc_ref)
        for p_late, a_late in ((p1, a1), (p2, a2)):
            p_late[...] = jnp.zeros_like(p_late)
            a_late[...] = jnp.ones_like(a_late)
        _chunk_pipeline(qi, 2, scores, softmax, values, last)

    return pl.pallas_call(
        body, name="attn_fwd", grid=(N_HEADS, T // tq),
        in_specs=[pl.BlockSpec((1, tq, HEAD_PAD), lambda h, i: (h, i, 0)),
                  pl.BlockSpec((1, T, HEAD_PAD), lambda h, i: (h, 0, 0)),
                  pl.BlockSpec((1, T, 2 * V_DIM), lambda h, i: (h, 0, 0))],
        out_specs=[pl.BlockSpec((tq, V_DIM), lambda h, i: (i, h)),
                   pl.BlockSpec((1, tq, LANES), lambda h, i: (h, i, 0))],
        out_shape=[jax.ShapeDtypeStruct((T, ATTN_W), F32), jax.ShapeDtypeStruct((N_HEADS, T, LANES), F32)],
        scratch_shapes=[pltpu.VMEM((tq, tk), F32)] * 3 + [pltpu.VMEM((tq, tk), BF16)] * 3
                       + [pltpu.VMEM((tq, 1), F32)] * 4 + [pltpu.VMEM((tq, 2 * V_DIM), F32)],
        compiler_params=_params(dimension_semantics=("arbitrary", "arbitrary")),
    )(q, k, v)


def _tail(x, o, proj, p, tgt, g_oa, g_oc, g_pl, conv_w, w_o, w_pl, w_plg, tm):
    T = x.shape[0]
    nt = T // tm

    def body(x_ref, o_ref, za_ref, cb_ref, cc_ref, cx_ref, zc_ref, cch_ref, cxh_ref, p_ref, tgt_ref,
             g_oa_ref, g_oc_ref, g_pl_ref, cw_ref, w_o_ref, w_pl_ref, w_plg_ref,
             dx1_ref, do_ref, delta_ref, dtail_ref, du_ref,
             dw_o_ref, dw_pl_ref, dw_plg_ref, dg_oa_ref, dg_oc_ref, dg_pl_ref, dcw_ref, loss_ref):
        i = pl.program_id(0)

        @pl.when(i == 0)
        def _():
            for r in (dw_o_ref, dw_pl_ref, dw_plg_ref, dg_oa_ref, dg_oc_ref, dg_pl_ref, dcw_ref, loss_ref):
                r[...] = jnp.zeros_like(r)

        xv, ov, za, cb, zc = x_ref[...], o_ref[...], za_ref[...], cb_ref[...], zc_ref[...]
        g_oa, g_oc, g_pl = g_oa_ref[...], g_oc_ref[...], g_pl_ref[...]
        w0, w1, w2 = cw_ref[0:1, :], cw_ref[1:2, :], cw_ref[2:3, :]

        sa = _sigmoid(za)
        silu_a = za * sa
        ga = ov * silu_a
        ra = _inv_rms(ga, ATTN_W)
        xa = ga * ra
        ya = xa * g_oa
        v = cc_ref[...] * cx_ref[...]
        not_first = jnp.where(i > 0, 1.0, 0.0)
        hv6 = cch_ref[6:7, :] * cxh_ref[6:7, :] * not_first
        hv7 = cch_ref[7:8, :] * cxh_ref[7:8, :] * not_first
        row = lax.broadcasted_iota(jnp.int32, v.shape, 0)
        v1 = jnp.where(row == 0, hv7, pltpu.roll(v, 1, 0))
        v2 = jnp.where(row == 0, hv6, jnp.where(row == 1, hv7, pltpu.roll(v, 2, 0)))
        u = w0 * v2 + w1 * v1 + w2 * v
        sc = _sigmoid(zc)
        silu_c = zc * sc
        gc = cb * u * silu_c
        rc = _inv_rms(gc, CONV_W)
        xc = gc * rc
        yc = xc * g_oc
        ycat = jnp.concatenate([ya, yc], axis=-1).astype(BF16)
        x1 = xv + _dot(ycat, w_o_ref[...])
        r1 = _inv_rms(x1, D_MODEL)
        xh1 = x1 * r1
        n1 = (xh1 * g_pl).astype(BF16)
        gate = _sigmoid(_dot(n1, w_plg_ref[...]))
        pb = p_ref[...].astype(BF16)
        pp = _dot(pb, w_pl_ref[...])
        err = x1 + gate * pp - tgt_ref[...]
        loss_ref[...] += 0.5 * jnp.sum(err * err) / D_MODEL
        dy = err / D_MODEL

        dpp = (dy * gate).astype(BF16)
        da = (dy * pp * gate * (1.0 - gate)).astype(BF16)
        dw_pl_ref[...] += _dot_tn(pb, dpp)
        dw_plg_ref[...] += _dot_tn(n1, da)
        dn1 = _dot_nt(da, w_plg_ref[...])
        dg_pl_ref[...] += _colsum(dn1 * xh1)
        dxh = dn1 * g_pl
        dx1 = dy + r1 * (dxh - xh1 * (jnp.sum(dxh * xh1, axis=-1, keepdims=True) / D_MODEL))
        dx1_ref[...] = dx1
        dx1b = dx1.astype(BF16)
        dw_o_ref[...] += _dot_tn(ycat, dx1b)
        dycat = _dot_nt(dx1b, w_o_ref[...])
        dya, dyc = dycat[:, 0:ATTN_W], dycat[:, ATTN_W:D_MODEL]

        dg_oa_ref[...] += _colsum(dya * xa)
        dxa = dya * g_oa
        dga = ra * (dxa - xa * (jnp.sum(dxa * xa, axis=-1, keepdims=True) / ATTN_W))
        do = (dga * silu_a).astype(BF16)
        do_ref[...] = do
        dof = do.astype(F32) * ov
        for hd in range(N_HEADS):
            delta_ref[hd] = _lane_sum(dof[:, hd * V_DIM:(hd + 1) * V_DIM])
        dtail_ref[:, 0:512] = (dga * ov * (sa * (1.0 + za * (1.0 - sa)))).astype(BF16)

        dg_oc_ref[...] += _colsum(dyc * xc)
        dxc = dyc * g_oc
        dgc = rc * (dxc - xc * (jnp.sum(dxc * xc, axis=-1, keepdims=True) / CONV_W))
        dtail_ref[:, 512:1024] = (dgc * u * silu_c).astype(BF16)
        du = dgc * cb * silu_c
        du_ref[...] = du
        dtail_ref[:, 1024:1536] = (dgc * cb * u * (sc * (1.0 + zc * (1.0 - sc)))).astype(BF16)
        dcw_ref[0:1, :] += _colsum(du * v2)
        dcw_ref[1:2, :] += _colsum(du * v1)
        dcw_ref[2:3, :] += _colsum(du * v)

    row = lambda i: (i, 0)
    col = lambda c: (lambda i: (i, c))
    halo = lambda c: (lambda i: (jnp.maximum(i * (tm // 8) - 1, 0), c))
    in_specs = [pl.BlockSpec((tm, D_MODEL), row), pl.BlockSpec((tm, ATTN_W), row)]
    in_specs += [pl.BlockSpec((tm, 512), col(c)) for c in (1, 2, 3, 4, 5)]
    in_specs += [pl.BlockSpec((8, 512), halo(3)), pl.BlockSpec((8, 512), halo(4))]
    in_specs += [pl.BlockSpec((tm, PLE), row), pl.BlockSpec((tm, D_MODEL), row),
                 _full((1, ATTN_W)), _full((1, CONV_W)), _full((1, D_MODEL)), _full((3, CONV_W)),
                 _full((D_MODEL, D_MODEL)), _full((PLE, D_MODEL)), _full((D_MODEL, D_MODEL))]
    out_specs = [pl.BlockSpec((tm, D_MODEL), row), pl.BlockSpec((tm, ATTN_W), row),
                 pl.BlockSpec((N_HEADS, tm, LANES), lambda i: (0, i, 0)), pl.BlockSpec((tm, 1536), row),
                 pl.BlockSpec((tm, CONV_W), row),
                 _full((D_MODEL, D_MODEL)), _full((PLE, D_MODEL)), _full((D_MODEL, D_MODEL)),
                 _full((1, ATTN_W)), _full((1, CONV_W)), _full((1, D_MODEL)), _full((3, CONV_W)), _full((1, LANES))]
    out_shape = [jax.ShapeDtypeStruct((T, D_MODEL), F32), jax.ShapeDtypeStruct((T, ATTN_W), BF16),
                 jax.ShapeDtypeStruct((N_HEADS, T, LANES), F32), jax.ShapeDtypeStruct((T, 1536), BF16),
                 jax.ShapeDtypeStruct((T, CONV_W), F32),
                 jax.ShapeDtypeStruct((D_MODEL, D_MODEL), F32), jax.ShapeDtypeStruct((PLE, D_MODEL), F32),
                 jax.ShapeDtypeStruct((D_MODEL, D_MODEL), F32),
                 jax.ShapeDtypeStruct((1, ATTN_W), F32), jax.ShapeDtypeStruct((1, CONV_W), F32),
                 jax.ShapeDtypeStruct((1, D_MODEL), F32), jax.ShapeDtypeStruct((3, CONV_W), F32),
                 jax.ShapeDtypeStruct((1, LANES), F32)]
    return pl.pallas_call(
        body, name="tail", grid=(nt,), in_specs=in_specs, out_specs=out_specs, out_shape=out_shape,
        compiler_params=_params(dimension_semantics=("arbitrary",)),
    )(x, o, proj, proj, proj, proj, proj, proj, proj, p, tgt, g_oa, g_oc, g_pl, conv_w, w_o, w_pl, w_plg)


def _attn_bwd(q, k, v, do, lse_row, delta_row, tk):
    T = q.shape[1]
    tq = tk
    nq = T // tq
    rc = min(SOFTMAX_ROWS, tk)

    def body(q_ref, k_ref, v_ref, do_ref, lse_ref, dl_ref, dq_ref, dk_ref, dv_ref,
             s0, s1, d0, d1, p0, p1, g0, g1, dk_acc, dv_acc):
        kj = pl.program_id(1)
        s_buf, dp_buf, p_buf, g_buf = (s0, s1), (d0, d1), (p0, p1), (g0, g1)

        @pl.when(kj == 0)
        def _():
            dq_ref[...] = jnp.zeros_like(dq_ref)

        def q_start(t):
            return pl.multiple_of((nq - 1 - t) * tq, tq)

        def matmuls(t, slot):
            qs = q_start(t)
            s_buf[slot][...] = _dot_nt(k_ref[0], q_ref[0, pl.ds(qs, tq), :])
            dp_buf[slot][...] = _dot_nt(v_ref[0], do_ref[pl.ds(qs, tq), :])

        def pointwise(t, slot, masked):
            qs = q_start(t)
            lse2 = lse_ref[0, :, pl.ds(qs, tq)] * LOG2E
            dl = dl_ref[0, :, pl.ds(qs, tq)]
            for r0 in range(0, tk, rc):
                st = s_buf[slot][r0:r0 + rc, :]
                if masked:
                    row = lax.broadcasted_iota(jnp.int32, (rc, tq), 0)
                    col = lax.broadcasted_iota(jnp.int32, (rc, tq), 1)
                    st = jnp.where(row + r0 <= col, st, NEG)
                pt = jnp.exp2(st * EXP2_SCALE - lse2)
                p_buf[slot][r0:r0 + rc, :] = pt.astype(BF16)
                g_buf[slot][r0:r0 + rc, :] = (pt * (dp_buf[slot][r0:r0 + rc, :] - dl) * SCALE).astype(BF16)

        def accumulate(t, slot):
            qs = q_start(t)
            dv_acc[...] += _dot(p_buf[slot][...], do_ref[pl.ds(qs, tq), :])
            dk_acc[...] += _dot(g_buf[slot][...], q_ref[0, pl.ds(qs, tq), :])
            dq_ref[0, pl.ds(qs, tq), :] += _dot_tn(g_buf[slot][...], k_ref[0])

        def last():
            dk_ref[0] = dk_acc[...]
            dv_ref[0] = dv_acc[...]

        dk_acc[...] = jnp.zeros_like(dk_acc)
        dv_acc[...] = jnp.zeros_like(dv_acc)
        for late in (p1, g1):
            late[...] = jnp.zeros_like(late)
        _chunk_pipeline(nq - 1 - kj, 1, matmuls, pointwise, accumulate, last)

    return pl.pallas_call(
        body, name="attn_bwd", grid=(N_HEADS, T // tk),
        in_specs=[pl.BlockSpec((1, T, HEAD_PAD), lambda h, j: (h, 0, 0)),
                  pl.BlockSpec((1, tk, HEAD_PAD), lambda h, j: (h, j, 0)),
                  pl.BlockSpec((1, tk, V_DIM), lambda h, j: (h, j, 0)),
                  pl.BlockSpec((T, V_DIM), lambda h, j: (0, h)),
                  pl.BlockSpec((1, 1, T), lambda h, j: (h, 0, 0)),
                  pl.BlockSpec((1, 1, T), lambda h, j: (h, 0, 0))],
        out_specs=[pl.BlockSpec((1, T, HEAD_PAD), lambda h, j: (h, 0, 0)),
                   pl.BlockSpec((1, tk, HEAD_PAD), lambda h, j: (h, j, 0)),
                   pl.BlockSpec((1, tk, V_DIM), lambda h, j: (h, j, 0))],
        out_shape=[jax.ShapeDtypeStruct((N_HEADS, T, HEAD_PAD), F32), jax.ShapeDtypeStruct((N_HEADS, T, HEAD_PAD), F32),
                   jax.ShapeDtypeStruct((N_HEADS, T, V_DIM), F32)],
        scratch_shapes=[pltpu.VMEM((tk, tq), F32)] * 4 + [pltpu.VMEM((tk, tq), BF16)] * 4
                       + [pltpu.VMEM((tk, HEAD_PAD), F32), pltpu.VMEM((tk, V_DIM), F32)],
        compiler_params=_params(dimension_semantics=("arbitrary", "arbitrary")),
    )(q, k, v, do, lse_row, delta_row)


def _bwd_proj(x, dx1, pos, proj, dq, dk, dv, dtail, du, g_in, w_in, g_cq, w_uq, g_ckv, w_ukv, gq, gk, conv_w,
              invf, sgn, tm):
    T = x.shape[0]
    nt = T // tm

    ts = min(SUB_TILE, tm)

    def body(x_ref, dx1_ref, pos_ref, lat_ref, cc_ref, cx_ref, dq_ref, dk_ref, dv_ref, dtail_ref, du_ref, dun_ref, *rest):
        consts, (gx_ref, h_ref, dproj_ref), sums = rest[:11], rest[11:14], rest[14:]
        cw_ref = consts[8]
        i = pl.program_id(0)

        @pl.when(i == 0)
        def _():
            for r in sums:
                r[...] = jnp.zeros_like(r)

        du_v = du_ref[...]
        not_last = jnp.where(i < nt - 1, 1.0, 0.0)
        nx0 = dun_ref[0:1, :] * not_last
        nx1 = dun_ref[1:2, :] * not_last
        row = lax.broadcasted_iota(jnp.int32, du_v.shape, 0)
        du1 = jnp.where(row == tm - 1, nx0, pltpu.roll(du_v, tm - 1, 0))
        du2 = jnp.where(row == tm - 2, nx0, jnp.where(row == tm - 1, nx1, pltpu.roll(du_v, tm - 2, 0)))
        dvc = cw_ref[2:3, :] * du_v + cw_ref[1:2, :] * du1 + cw_ref[0:1, :] * du2
        dproj_ref[:, 1536:2048] = (dvc * cx_ref[...]).astype(BF16)
        dproj_ref[:, 2048:2560] = (dvc * cc_ref[...]).astype(BF16)

        for r0 in range(0, tm, ts):
            rows = slice(r0, r0 + ts)
            work(x_ref.at[rows, :], dx1_ref.at[rows, :], pos_ref.at[rows, :], lat_ref.at[rows, :],
                 dq_ref.at[:, rows, :], dk_ref.at[:, rows, :], dv_ref.at[:, rows, :], dtail_ref.at[rows, :], *consts,
                 gx_ref.at[rows, :], h_ref.at[rows, :], dproj_ref.at[rows, :], *sums)

    def work(x_ref, dx1_ref, pos_ref, lat_ref, dq_ref, dk_ref, dv_ref, dtail_ref,
             g_in_ref, w_in_ref, g_cq_ref, w_uq_ref, g_ckv_ref, w_ukv_ref, gq_ref, gk_ref, cw_ref, invf_ref, sgn_ref,
             gx_ref, h_ref, dproj_ref, dw_uq_ref, dw_ukv_ref, dg_in_ref, dg_cq_ref, dg_ckv_ref, dgq_ref, dgk_ref):
        xv = x_ref[...]
        r0 = _rep(_inv_rms_mxu(xv), D_MODEL)
        xh0 = xv * r0
        g_in = g_in_ref[...]
        h_ref[...] = (xh0 * g_in).astype(BF16)

        c_q = lat_ref[:, 0:Q_LORA]
        rq = _rep(_inv_rms_mxu(c_q), Q_LORA)
        xq = c_q * rq
        g_cq = g_cq_ref[...]
        cqn = (xq * g_cq).astype(BF16)
        c_kv = lat_ref[:, Q_LORA:Q_LORA + KV_LORA]
        rkv = _inv_rms_mxu(c_kv)
        xkv = c_kv * rkv
        g_ckv = g_ckv_ref[...]
        ckvn = (xkv * g_ckv).astype(BF16)
        kpe = lat_ref[:, 384:512]
        kpe_sq = kpe * kpe
        cos_b, sin_b = _rope_tables(pos_ref, invf_ref, sgn_ref)
        gq_a, gq_b = gq_ref[:, 0:NOPE], gq_ref[:, NOPE:HEAD_PAD]
        gk_a, gk_b = gk_ref[:, 0:NOPE], gk_ref[:, NOPE:HEAD_PAD]

        dproj_ref[:, 512:1536] = dtail_ref[:, 0:1024]
        dproj_ref[:, 2560:3072] = dtail_ref[:, 1024:1536]

        def dh_part(c0):
            return _dot_nt(dproj_ref[:, c0:c0 + 512], w_in_ref[:, c0:c0 + 512])

        later_chunks = ((512,), (1024,), (1536, 2048), (2560,))
        dh = jnp.zeros((ts, D_MODEL), F32)
        dkpe = jnp.zeros((ts, LANES), F32)
        dcqn = jnp.zeros((ts, Q_LORA), F32)
        dckvn = jnp.zeros((ts, KV_LORA), F32)
        for hd in range(N_HEADS):
            for chunk in later_chunks[hd]:
                dh = dh + dh_part(chunk)
            c0 = hd * HEAD_PAD
            qh = _dot(cqn, w_uq_ref[:, c0:c0 + HEAD_PAD])
            a, b = qh[:, 0:NOPE], qh[:, NOPE:HEAD_PAD]
            r = lax.rsqrt(_lane_sum(a * a + b * b) / QK_DIM + EPS)
            xa, xb = a * r, b * r
            dan = dq_ref[hd, :, 0:NOPE]
            dbr = dq_ref[hd, :, NOPE:HEAD_PAD]
            dbn = dbr * cos_b + _swap_rope_halves(dbr * sin_b)
            dgq_ref[:, 0:NOPE] += _colsum(dan * xa)
            dgq_ref[:, NOPE:HEAD_PAD] += _colsum(dbn * xb)
            dxa, dxb = dan * gq_a, dbn * gq_b
            cq = _lane_sum(dxa * xa + dxb * xb) / QK_DIM
            dqh = jnp.concatenate([r * (dxa - xa * cq), r * (dxb - xb * cq)], axis=-1).astype(BF16)
            dw_uq_ref[:, c0:c0 + HEAD_PAD] += _dot_tn(cqn, dqh)
            dcqn = dcqn + _dot_nt(dqh, w_uq_ref[:, c0:c0 + HEAD_PAD])
            kvh = _dot(ckvn, w_ukv_ref[:, c0:c0 + HEAD_PAD])
            ka = kvh[:, 0:NOPE]
            rk = lax.rsqrt(_lane_sum(ka * ka + kpe_sq) / QK_DIM + EPS)
            xka, xkb = ka * rk, kpe * rk
            dkan = dk_ref[hd, :, 0:NOPE]
            dkbr = dk_ref[hd, :, NOPE:HEAD_PAD]
            dkbn = dkbr * cos_b + _swap_rope_halves(dkbr * sin_b)
            dgk_ref[:, 0:NOPE] += _colsum(dkan * xka)
            dgk_ref[:, NOPE:HEAD_PAD] += _colsum(dkbn * xkb)
            dxka, dxkb = dkan * gk_a, dkbn * gk_b
            ck = _lane_sum(dxka * xka + dxkb * xkb) / QK_DIM
            dkpe = dkpe + rk * (dxkb - xkb * ck)
            dkvh = jnp.concatenate([rk * (dxka - xka * ck), dv_ref[hd]], axis=-1).astype(BF16)
            dw_ukv_ref[:, c0:c0 + HEAD_PAD] += _dot_tn(ckvn, dkvh)
            dckvn = dckvn + _dot_nt(dkvh, w_ukv_ref[:, c0:c0 + HEAD_PAD])

        dg_cq_ref[...] += _colsum(dcqn * xq)
        dxq = dcqn * g_cq
        dproj_ref[:, 0:Q_LORA] = (rq * (dxq - xq * _rep(_lane_sum(dxq * xq) / Q_LORA, Q_LORA))).astype(BF16)
        dg_ckv_ref[...] += _colsum(dckvn * xkv)
        dxkv = dckvn * g_ckv
        dproj_ref[:, 256:384] = (rkv * (dxkv - xkv * (_lane_sum(dxkv * xkv) / KV_LORA))).astype(BF16)
        dproj_ref[:, 384:512] = dkpe.astype(BF16)
        dh = dh + dh_part(0)
        dg_in_ref[...] += _colsum(dh * xh0)
        dxh = dh * g_in
        gx_ref[...] = dx1_ref[...] + r0 * (dxh - xh0 * _rep(_lane_sum(dxh * xh0) / D_MODEL, D_MODEL))

    row = lambda i: (i, 0)
    col = lambda c: (lambda i: (i, c))
    head_rows = lambda i: (0, i, 0)
    nxt = lambda i: (jnp.minimum((i + 1) * (tm // 8), T // 8 - 1), 0)
    in_specs = [pl.BlockSpec((tm, D_MODEL), row), pl.BlockSpec((tm, D_MODEL), row), pl.BlockSpec((tm, 1), row),
                pl.BlockSpec((tm, 512), col(0)), pl.BlockSpec((tm, 512), col(3)), pl.BlockSpec((tm, 512), col(4)),
                pl.BlockSpec((N_HEADS, tm, HEAD_PAD), head_rows), pl.BlockSpec((N_HEADS, tm, HEAD_PAD), head_rows),
                pl.BlockSpec((N_HEADS, tm, V_DIM), head_rows), pl.BlockSpec((tm, 1536), row),
                pl.BlockSpec((tm, CONV_W), row), pl.BlockSpec((8, CONV_W), nxt),
                _full((1, D_MODEL)), _full((D_MODEL, PROJ_EXT)), _full((1, Q_LORA)), _full((Q_LORA, N_HEADS * HEAD_PAD)),
                _full((1, KV_LORA)), _full((KV_LORA, N_HEADS * HEAD_PAD)), _full((1, HEAD_PAD)), _full((1, HEAD_PAD)),
                _full((3, CONV_W)), _full((1, LANES)), _full((1, LANES))]
    out_specs = [pl.BlockSpec((tm, D_MODEL), row), pl.BlockSpec((tm, D_MODEL), row), pl.BlockSpec((tm, PROJ_EXT), row),
                 _full((Q_LORA, N_HEADS * HEAD_PAD)), _full((KV_LORA, N_HEADS * HEAD_PAD)),
                 _full((1, D_MODEL)), _full((1, Q_LORA)), _full((1, KV_LORA)), _full((1, HEAD_PAD)), _full((1, HEAD_PAD))]
    out_shape = [jax.ShapeDtypeStruct((T, D_MODEL), F32), jax.ShapeDtypeStruct((T, D_MODEL), BF16),
                 jax.ShapeDtypeStruct((T, PROJ_EXT), BF16),
                 jax.ShapeDtypeStruct((Q_LORA, N_HEADS * HEAD_PAD), F32), jax.ShapeDtypeStruct((KV_LORA, N_HEADS * HEAD_PAD), F32),
                 jax.ShapeDtypeStruct((1, D_MODEL), F32), jax.ShapeDtypeStruct((1, Q_LORA), F32),
                 jax.ShapeDtypeStruct((1, KV_LORA), F32), jax.ShapeDtypeStruct((1, HEAD_PAD), F32),
                 jax.ShapeDtypeStruct((1, HEAD_PAD), F32)]
    return pl.pallas_call(
        body, name="bwd_proj", grid=(nt,), in_specs=in_specs, out_specs=out_specs, out_shape=out_shape,
        compiler_params=_params(dimension_semantics=("arbitrary",)),
    )(x, dx1, pos, proj, proj, proj, dq, dk, dv, dtail, du, du, g_in, w_in, g_cq, w_uq, g_ckv, w_ukv, gq, gk, conv_w,
      invf, sgn)


def _matmul_tn(a, b, tt, tn):
    T, M = a.shape
    N = b.shape[1]

    def body(a_ref, b_ref, o_ref):
        @pl.when(pl.program_id(1) == 0)
        def _():
            o_ref[...] = jnp.zeros_like(o_ref)

        o_ref[...] += _dot_tn(a_ref[...], b_ref[...])

    return pl.pallas_call(
        body, name="dw_in", grid=(N // tn, T // tt),
        in_specs=[pl.BlockSpec((tt, M), lambda j, t: (t, 0)), pl.BlockSpec((tt, tn), lambda j, t: (t, j))],
        out_specs=pl.BlockSpec((M, tn), lambda j, t: (0, j)),
        out_shape=jax.ShapeDtypeStruct((M, N), F32),
        compiler_params=_params(dimension_semantics=("arbitrary", "arbitrary")),
    )(a, b)


def _add_pair(grads, from_sibling, small, small_sibling, c):
    n = len(grads)

    def body(c_ref, *refs):
        ins, outs = refs[:2 * n + 2], refs[2 * n + 2:]
        for i in range(n + 1):
            outs[i][...] = (ins[2 * i][...] + ins[2 * i + 1][...]).astype(outs[i].dtype)

    in_specs, out_specs, out_shape, args = [], [], [], []
    for g, r in zip(grads, from_sibling):
        _, hr, cols = r.shape
        in_specs += [pl.BlockSpec((1, hr, cols), lambda k, c_ref: (k, c_ref[0], 0)),
                     pl.BlockSpec((1, hr, cols), lambda k, c_ref: (k, 0, 0))]
        out_specs.append(pl.BlockSpec((1, hr, cols), lambda k, c_ref: (k, 0, 0)))
        out_shape.append(jax.ShapeDtypeStruct(r.shape, BF16))
        args += [g, r]
    whole = pl.BlockSpec(small.shape, lambda k, c_ref: (0, 0))
    in_specs += [whole, whole]
    out_specs.append(whole)
    out_shape.append(jax.ShapeDtypeStruct(small.shape, F32))
    outs = pl.pallas_call(
        body, name="add_pair", out_shape=out_shape,
        grid_spec=pltpu.PrefetchScalarGridSpec(num_scalar_prefetch=1, grid=(N_CHIPS,), in_specs=in_specs,
                                               out_specs=out_specs),
        compiler_params=_params(dimension_semantics=("arbitrary",)),
    )(c.reshape(1), *args, small, small_sibling)
    return outs[:n], outs[n]


def _add_chips(parts, small_parts):
    arrays = list(parts) + [small_parts]

    def body(*refs):
        ins, outs = refs[:len(arrays)], refs[len(arrays):]
        for a_ref, o_ref in zip(ins, outs):
            part = lambda k: a_ref[k].astype(F32)
            o_ref[...] = ((part(0) + part(1)) + part(2)) + part(3)

    in_specs, out_specs, out_shape = [], [], []
    for a in arrays:
        _, rows, cols = a.shape
        in_specs.append(pl.BlockSpec((N_CHIPS, rows // 2, cols), lambda i: (0, i, 0)))
        out_specs.append(pl.BlockSpec((rows // 2, cols), lambda i: (i, 0)))
        out_shape.append(jax.ShapeDtypeStruct((rows, cols), F32))
    outs = pl.pallas_call(body, name="add_chips", grid=(2,), in_specs=in_specs, out_specs=out_specs,
                          out_shape=out_shape, compiler_params=_params(dimension_semantics=("arbitrary",)))(*arrays)
    return outs[:-1], outs[-1]


def _adamw(w, g, m, v, name):
    rows, cols = w.shape
    rb = 256 if rows * cols > 512 * 1024 else rows

    def body(w_ref, g_ref, m_ref, v_ref, d_ref, nm_ref, nv_ref):
        _adamw_math(g_ref[...], w_ref, m_ref, v_ref, d_ref, nm_ref, nv_ref)

    spec = pl.BlockSpec((rb, cols), lambda i: (i, 0))
    shp = jax.ShapeDtypeStruct(w.shape, F32)
    return pl.pallas_call(body, name=name, grid=(rows // rb,), in_specs=[spec] * 4, out_specs=[spec] * 3,
                          out_shape=[shp] * 3)(w, g, m, v)


def _adamw_math(gv, w_ref, m_ref, v_ref, d_ref, nm_ref, nv_ref):
    nm = B1 * m_ref[...] + (1.0 - B1) * gv
    nv = B2 * v_ref[...] + (1.0 - B2) * (gv * gv)
    m_hat = nm / (1.0 - B1 ** STEP)
    v_hat = nv / (1.0 - B2 ** STEP)
    d_ref[...] = -LR * (m_hat / (jnp.sqrt(v_hat) + ADAM_EPS) + WD * w_ref[...])
    nm_ref[...] = nm
    nv_ref[...] = nv


def _adamw_halves(w, mine, other, m, v, c, name):
    hr, cols = mine.shape

    def body(c_ref, w_ref, mine_ref, other_ref, m_ref, v_ref, g_ref, d_ref, nm_ref, nv_ref):
        gv = jnp.where(pl.program_id(0) == c_ref[0], mine_ref[...], other_ref[...])
        g_ref[...] = gv
        _adamw_math(gv, w_ref, m_ref, v_ref, d_ref, nm_ref, nv_ref)

    half = pl.BlockSpec((hr, cols), lambda i, c_ref: (i, 0))
    whole = pl.BlockSpec((hr, cols), lambda i, c_ref: (0, 0))
    shp = jax.ShapeDtypeStruct(w.shape, F32)
    return pl.pallas_call(
        body, name=name, out_shape=[shp] * 4,
        grid_spec=pltpu.PrefetchScalarGridSpec(num_scalar_prefetch=1, grid=(2,), in_specs=[half, whole, whole, half, half],
                                               out_specs=[half] * 4),
        compiler_params=_params(dimension_semantics=("arbitrary",)),
    )(c.reshape(1), w, mine, other, m, v)


_ANY = pl.BlockSpec(memory_space=pl.ANY)


def _mesh_pos():
    return lax.axis_index("x"), lax.axis_index("y"), lax.axis_index("c")


def _other_chips(x, y):
    return [(1 - x, y), (x, 1 - y), (1 - x, 1 - y)]


def _remote(src, dst, send_sems, recv_sems, k, to):
    return pltpu.make_async_remote_copy(src_ref=src, dst_ref=dst, send_sem=send_sems.at[k], recv_sem=recv_sems.at[k],
                                        device_id=to, device_id_type=MESH)


def _gather_weights(shards):
    n = len(shards)
    halved = [s.shape[0] % 32 == 0 for s in shards]

    def body(*refs):
        ins, outs, stage = refs[:n], refs[n:2 * n], refs[2 * n:3 * n]
        send_sems, recv_sems, local_sems = refs[3 * n:]
        x, y, c = _mesh_pos()
        me = 2 * x + y
        chips = _other_chips(x, y)

        def part(i, ref, hc):
            if not halved[i]:
                return ref
            hr = shards[i].shape[0] // 2
            return ref.at[pl.ds(hc * hr, hr), :]

        locals_, started = [], []
        for i in range(n):
            stage[i][...] = ins[i][...].astype(BF16)
            mine = pltpu.make_async_copy(stage[i], outs[i].at[me], local_sems.at[i])
            mine.start()
            locals_.append(mine)
            for j, (cx, cy) in enumerate(chips):
                cp = _remote(part(i, stage[i], c), part(i, outs[i].at[me], c), send_sems, recv_sems, 6 * i + j, (cx, cy, c))
                cp.start()
                started.append(cp)
        for i in range(n):
            for j, (cx, cy) in enumerate(chips):
                got = part(i, outs[i].at[2 * cx + cy], c)
                _remote(got, got, send_sems, recv_sems, 6 * i + j, (cx, cy, c)).wait_recv()
                if halved[i]:
                    fwd = _remote(got, got, send_sems, recv_sems, 6 * i + 3 + j, (x, y, 1 - c))
                    fwd.start()
                    started.append(fwd)
        for i in range(n):
            if halved[i]:
                for j, (cx, cy) in enumerate(chips):
                    got = part(i, outs[i].at[2 * cx + cy], 1 - c)
                    _remote(got, got, send_sems, recv_sems, 6 * i + 3 + j, (x, y, 1 - c)).wait_recv()
        for cp in started:
            cp.wait_send()
        for cp in locals_:
            cp.wait()

    vmem = pl.BlockSpec(memory_space=pltpu.VMEM)
    return pl.pallas_call(
        body, name="gather_weights", in_specs=[vmem] * n, out_specs=[_ANY] * n,
        out_shape=[jax.ShapeDtypeStruct((N_CHIPS,) + s.shape, BF16) for s in shards],
        scratch_shapes=[pltpu.VMEM(s.shape, BF16) for s in shards]
                       + [pltpu.SemaphoreType.DMA((6 * n,)), pltpu.SemaphoreType.DMA((6 * n,)), pltpu.SemaphoreType.DMA((n,))],
        compiler_params=_params(),
    )(*shards)


def _swap_halves(grads, small):
    n = len(grads)
    arrays = list(grads) + [small]

    def body(*refs):
        ins, outs, send_sems, recv_sems = refs[:n + 1], refs[n + 1:2 * n + 2], refs[2 * n + 2], refs[2 * n + 3]
        x, y, c = _mesh_pos()
        cps = []
        for i in range(n + 1):
            src = ins[i]
            if i < n:
                hr = grads[i].shape[1] // 2
                src = src.at[:, pl.ds((1 - c) * hr, hr), :]
            cp = _remote(src, outs[i], send_sems, recv_sems, i, (x, y, 1 - c))
            cp.start()
            cps.append(cp)
        for cp in cps:
            cp.wait()

    out_shape = [jax.ShapeDtypeStruct((g.shape[0], g.shape[1] // 2, g.shape[2]), F32) for g in grads]
    out_shape.append(jax.ShapeDtypeStruct(small.shape, F32))
    outs = pl.pallas_call(
        body, name="pair_grads", in_specs=[_ANY] * (n + 1), out_specs=[_ANY] * (n + 1), out_shape=out_shape,
        scratch_shapes=[pltpu.SemaphoreType.DMA((n + 1,)), pltpu.SemaphoreType.DMA((n + 1,))],
    )(*arrays)
    return outs[:n], outs[n]


def _scatter_to_chips(parts, small):
    n = len(parts)
    arrays = list(parts) + [small]

    def body(*refs):
        ins, outs = refs[:n + 1], refs[n + 1:2 * n + 2]
        send_sems, recv_sems, local_sems = refs[2 * n + 2:]
        x, y, c = _mesh_pos()
        me = 2 * x + y
        chips = _other_chips(x, y)
        locals_, sends = [], []
        for i in range(n + 1):
            mine = pltpu.make_async_copy(ins[i].at[me] if i < n else ins[i], outs[i].at[me], local_sems.at[i])
            mine.start()
            locals_.append(mine)
            for j, (cx, cy) in enumerate(chips):
                src = ins[i].at[2 * cx + cy] if i < n else ins[i]
                cp = _remote(src, outs[i].at[me], send_sems, recv_sems, 3 * i + j, (cx, cy, c))
                cp.start()
                sends.append(cp)
        for i in range(n + 1):
            for j, (cx, cy) in enumerate(chips):
                got = outs[i].at[2 * cx + cy]
                _remote(got, got, send_sems, recv_sems, 3 * i + j, (cx, cy, c)).wait_recv()
        for cp in sends:
            cp.wait_send()
        for cp in locals_:
            cp.wait()

    out_shape = [jax.ShapeDtypeStruct(p.shape, p.dtype) for p in parts]
    out_shape.append(jax.ShapeDtypeStruct((N_CHIPS,) + small.shape, small.dtype))
    outs = pl.pallas_call(
        body, name="scatter_grads", in_specs=[_ANY] * (n + 1), out_specs=[_ANY] * (n + 1), out_shape=out_shape,
        scratch_shapes=[pltpu.SemaphoreType.DMA((3 * n + 3,)), pltpu.SemaphoreType.DMA((3 * n + 3,)),
                        pltpu.SemaphoreType.DMA((n + 1,))],
    )(*arrays)
    return outs[:n], outs[n]


def _share_halves(halves):
    n = len(halves)

    def body(*refs):
        ins, outs, send_sems, recv_sems = refs[:n], refs[n:2 * n], refs[2 * n], refs[2 * n + 1]
        x, y, c = _mesh_pos()
        cps = [_remote(ins[i], outs[i], send_sems, recv_sems, i, (x, y, 1 - c)) for i in range(n)]
        for cp in cps:
            cp.start()
        for cp in cps:
            cp.wait()

    return pl.pallas_call(
        body, name="share_halves", in_specs=[_ANY] * n, out_specs=[_ANY] * n,
        out_shape=[jax.ShapeDtypeStruct(h.shape, h.dtype) for h in halves],
        scratch_shapes=[pltpu.SemaphoreType.DMA((n,)), pltpu.SemaphoreType.DMA((n,))],
    )(*halves)


SHARD_COLS_IN = IN_TOTAL // N_CHIPS
KPE_END = Q_LORA + KV_LORA + ROPE


def _assemble_weights(c_in, c_uq, c_ukv, c_o, c_pl, c_plg, c_conv):
    by_cols = lambda a: a.transpose(1, 0, 2).reshape(a.shape[1], N_CHIPS * a.shape[2])
    w_in_e = jnp.concatenate([c_in[0][:, :KPE_END], jnp.zeros((D_MODEL, 64), BF16), c_in[0][:, KPE_END:],
                              c_in[1], c_in[2], c_in[3]], axis=1)
    w_uq_e = by_cols(jnp.pad(c_uq, ((0, 0), (0, 0), (0, HEAD_PAD - QK_DIM))))
    return (w_in_e, w_uq_e, by_cols(c_ukv), by_cols(c_conv).astype(F32), c_o.reshape(D_MODEL, D_MODEL),
            by_cols(c_pl), c_plg.reshape(D_MODEL, D_MODEL))


def _split_grads(dw_in_e, dw_uq_e, dw_ukv, dw_o, dw_pl, dw_plg):
    chip_major = lambda a: a.reshape(a.shape[0], N_CHIPS, a.shape[1] // N_CHIPS).transpose(1, 0, 2)
    first = jnp.concatenate([dw_in_e[:, :KPE_END], dw_in_e[:, KPE_END + 64:SHARD_COLS_IN + 64]], axis=1)
    rest = [dw_in_e[:, SHARD_COLS_IN * k + 64:SHARD_COLS_IN * (k + 1) + 64] for k in range(1, N_CHIPS)]
    return [jnp.stack([first] + rest), chip_major(dw_uq_e)[:, :, :QK_DIM], chip_major(dw_ukv),
            dw_o.reshape(N_CHIPS, D_MODEL // N_CHIPS, D_MODEL), chip_major(dw_pl),
            dw_plg.reshape(N_CHIPS, D_MODEL // N_CHIPS, D_MODEL)]


def _local_step(x, p, pos, tgt, gains, w_in_e, w_uq_e, w_ukv, conv_w, w_o, w_pl, w_plg, tm, tq):
    g_in, g_cq, g_ckv, g_q, g_k, g_oa, g_oc, g_pl = gains
    T = x.shape[0]
    zpad = lambda a, n: jnp.concatenate([a, jnp.zeros(a.shape[:-1] + (n,), a.dtype)], axis=-1)
    gq, gk = zpad(g_q, HEAD_PAD - QK_DIM), zpad(g_k, HEAD_PAD - QK_DIM)
    inv_freq = 1.0 / (ROPE_THETA ** (jnp.arange(0, ROPE, 2, dtype=F32) / ROPE))
    invf = jnp.concatenate([inv_freq, inv_freq, jnp.zeros((64,), F32)]).reshape(1, LANES)
    sgn = jnp.concatenate([-jnp.ones((32,), F32), jnp.ones((32,), F32), jnp.zeros((64,), F32)]).reshape(1, LANES)

    proj, q, k, v = _fwd_proj(x, pos, g_in, w_in_e, g_cq, w_uq_e, g_ckv, w_ukv, gq, gk, invf, sgn, min(2 * tm, T))
    o, lse = _attn_fwd(q, k, v, tq)
    (dx1, do, delta, dtail, du, dw_o, dw_pl, dw_plg, dg_oa, dg_oc, dg_pl, dconv, loss) = _tail(
        x, o, proj, p, tgt, g_oa, g_oc, g_pl, conv_w, w_o, w_pl, w_plg, tm)
    lse_row = lse[:, :, 0].reshape(N_HEADS, 1, T)
    delta_row = delta[:, :, 0].reshape(N_HEADS, 1, T)
    dq, dk, dv = _attn_bwd(q, k, v, do, lse_row, delta_row, tq)
    (gx, h, dproj, dw_uq_e, dw_ukv, dg_in, dg_cq, dg_ckv, dgq, dgk) = _bwd_proj(
        x, dx1, pos, proj, dq, dk, dv, dtail, du, g_in, w_in_e, g_cq, w_uq_e, g_ckv, w_ukv, gq, gk, conv_w, invf, sgn,
        min(2 * tm, T))
    dw_in_e = _matmul_tn(h, dproj, min(512, T), 512)
    wgrads = (dw_in_e, dw_uq_e, dw_ukv, dw_o, dw_pl, dw_plg)
    ggrads = (dg_in, dg_cq, dg_ckv, dgq, dgk, dg_oa, dg_oc, dg_pl)
    return loss, gx, wgrads, ggrads, dconv


def kernel(x, p, positions, g_in, w_in, g_cq, w_uq, g_ckv, w_ukv, g_q, g_k, conv_w, g_oa, g_oc, w_o, w_pl, w_plg, g_pl, loss_target, m_g_in, m_w_in, m_g_cq, m_w_uq, m_g_ckv, m_w_ukv, m_g_q, m_g_k, m_conv_w, m_g_oa, m_g_oc, m_w_o, m_w_pl, m_w_plg, m_g_pl, v_g_in, v_w_in, v_g_cq, v_w_uq, v_g_ckv, v_w_ukv, v_g_q, v_g_k, v_conv_w, v_g_oa, v_g_oc, v_w_o, v_w_pl, v_w_plg, v_g_pl):
    T = x.shape[1]
    c = lax.axis_index("c")
    chip = 2 * lax.axis_index("x") + lax.axis_index("y")
    gains = [g.reshape(1, -1) for g in (g_in, g_cq, g_ckv, g_q, g_k, g_oa, g_oc, g_pl)]

    gathered = _gather_weights([w_in[0], w_uq[0], w_ukv[0], w_o[0], w_pl[0], w_plg[0], conv_w[0]])
    full = _assemble_weights(*gathered)

    loss, gx, wgrads, ggrads, dconv = _local_step(
        x[0], p[0, 0], positions.reshape(T, 1), loss_target[0], gains, *full, 256, 512)

    grads_cm = _split_grads(*wgrads)
    small_parts = [a.reshape(-1, LANES) for a in (*ggrads, loss, dconv)]
    small_rows = [a.shape[0] for a in small_parts]
    tile_rows = [-(-r // 8) * 8 for r in small_rows]
    tile_rows[-1] += -sum(tile_rows) % 16
    small = jnp.concatenate([jnp.pad(a, ((0, t - r), (0, 0))) for a, r, t in zip(small_parts, small_rows, tile_rows)])
    from_sibling, small_sibling = _swap_halves(grads_cm, small)
    chip_parts, chip_small = _add_pair(grads_cm, from_sibling, small, small_sibling, c)
    by_chip, small_by_chip = _scatter_to_chips(chip_parts, chip_small)
    halves, small_total = _add_chips(by_chip, small_by_chip)
    other_halves = _share_halves(halves)

    gg, off = [], 0
    for rows, tiled in zip(small_rows, tile_rows):
        gg.append(small_total[off:off + rows].reshape(1, -1))
        off += tiled
    loss_out = gg[8][0, 0]
    conv_total = gg[9].reshape(3, CONV_W)
    conv_g = lax.dynamic_slice(conv_total, (0, chip * (CONV_W // N_CHIPS)), (3, CONV_W // N_CHIPS))
    g_by_name = dict(g_in=gg[0], g_cq=gg[1], g_ckv=gg[2], g_q=gg[3][:, :QK_DIM], g_k=gg[4][:, :QK_DIM], conv_w=conv_g,
                     g_oa=gg[5], g_oc=gg[6], g_pl=gg[7])
    half_by_name = dict(zip(("w_in", "w_uq", "w_ukv", "w_o", "w_pl", "w_plg"), zip(halves, other_halves)))
    weights = dict(g_in=g_in, w_in=w_in, g_cq=g_cq, w_uq=w_uq, g_ckv=g_ckv, w_ukv=w_ukv, g_q=g_q, g_k=g_k,
                   conv_w=conv_w, g_oa=g_oa, g_oc=g_oc, w_o=w_o, w_pl=w_pl, w_plg=w_plg, g_pl=g_pl)
    ms = dict(g_in=m_g_in, w_in=m_w_in, g_cq=m_g_cq, w_uq=m_w_uq, g_ckv=m_g_ckv, w_ukv=m_w_ukv, g_q=m_g_q, g_k=m_g_k,
              conv_w=m_conv_w, g_oa=m_g_oa, g_oc=m_g_oc, w_o=m_w_o, w_pl=m_w_pl, w_plg=m_w_plg, g_pl=m_g_pl)
    vs = dict(g_in=v_g_in, w_in=v_w_in, g_cq=v_g_cq, w_uq=v_w_uq, g_ckv=v_g_ckv, w_ukv=v_w_ukv, g_q=v_g_q, g_k=v_g_k,
              conv_w=v_conv_w, g_oa=v_g_oa, g_oc=v_g_oc, w_o=v_w_o, w_pl=v_w_pl, w_plg=v_w_plg, g_pl=v_g_pl)
    names = list(weights)
    grads, deltas, new_m, new_v = [], [], [], []
    for n in names:
        w = weights[n]
        w2 = w.reshape(-1, w.shape[-1])
        if n in half_by_name:
            g2, d, nm, nv = _adamw_halves(w2, *half_by_name[n], ms[n].reshape(w2.shape), vs[n].reshape(w2.shape), c,
                                          "adamw_" + n)
        else:
            g2 = g_by_name[n].reshape(w2.shape)
            d, nm, nv = _adamw(w2, g2, ms[n].reshape(w2.shape), vs[n].reshape(w2.shape), "adamw_" + n)
        grads.append(g2.reshape(w.shape))
        deltas.append(d.reshape(w.shape))
        new_m.append(nm.reshape(w.shape))
        new_v.append(nv.reshape(w.shape))
    return (loss_out, gx.reshape(x.shape), *grads, *deltas, *new_m, *new_v)
```

```python
import functools
import math

import jax
import jax.numpy as jnp
from jax import lax
from jax.experimental import pallas as pl
from jax.experimental.pallas import tpu as pltpu

F32 = jnp.float32
BF16 = jnp.bfloat16

D_MODEL = 1024
N_HEADS = 4
NOPE = 128
ROPE = 64
V_DIM = 128
QK_DIM = NOPE + ROPE
HEAD_PAD = 256
Q_LORA = 256
KV_LORA = 128
ATTN_W = 512
CONV_W = 512
PLE = 256
IN_TOTAL = 3008
PROJ_EXT = 3072
ROPE_THETA = 10000.0
EPS = 1e-6
SCALE = 1.0 / math.sqrt(QK_DIM)
LOG2E = math.log2(math.e)
EXP2_SCALE = SCALE * LOG2E
NEG = -1e30
SOFTMAX_ROWS = 32
SUB_TILE = 256

LR, B1, B2, ADAM_EPS, WD, STEP = 0.001, 0.9, 0.999, 1e-08, 0.01, 10

N_CHIPS = 4
LANES = 128
VMEM_LIMIT = 56 * 1024 * 1024
MESH = pl.DeviceIdType.MESH


def _params(**kw):
    return pltpu.CompilerParams(vmem_limit_bytes=VMEM_LIMIT, **kw)


def _inv_rms(x, n):
    return lax.rsqrt(jnp.sum(x * x, axis=-1, keepdims=True) / n + EPS)


def _lane_sum(a):
    folded = a[:, 0:LANES]
    for c0 in range(LANES, a.shape[1], LANES):
        folded = folded + a[:, c0:c0 + LANES]
    head = folded.astype(BF16)
    tail = (folded - head.astype(F32)).astype(BF16)
    return _dot(jnp.concatenate([head, tail], axis=1), jnp.ones((2 * LANES, LANES), BF16))


def _inv_rms_mxu(x):
    return lax.rsqrt(_lane_sum(x * x) / x.shape[1] + EPS)


def _rep(r, width):
    return r if width == LANES else jnp.tile(r, (1, width // LANES))


def _sigmoid(z):
    return 1.0 / (1.0 + jnp.exp(-z))


def _swap_rope_halves(b):
    lane = lax.broadcasted_iota(jnp.int32, b.shape, 1)
    swapped = jnp.where(lane < 32, pltpu.roll(b, 96, 1), pltpu.roll(b, 32, 1))
    return jnp.where(lane < ROPE, swapped, 0.0)


def _dot(a, b):
    return jnp.dot(a, b, preferred_element_type=F32)


def _dot_nt(a, b):
    return lax.dot_general(a, b, (((1,), (1,)), ((), ())), preferred_element_type=F32)


def _dot_tn(a, b):
    return lax.dot_general(a, b, (((0,), (0,)), ((), ())), preferred_element_type=F32)


def _colsum(a):
    return jnp.sum(a, axis=0, keepdims=True)


def _full(shape):
    return pl.BlockSpec(shape, lambda *_: (0,) * len(shape))


def _rope_tables(pos_ref, invf_ref, sgn_ref):
    ang = pos_ref[...].astype(F32) * invf_ref[...]
    return jnp.cos(ang), jnp.sin(ang) * sgn_ref[...]


def _fwd_proj(x, pos, g_in, w_in, g_cq, w_uq, g_ckv, w_ukv, gq, gk, invf, sgn, tm):
    T = x.shape[0]

    ts = min(SUB_TILE, tm)

    def body(x_ref, pos_ref, g_in_ref, w_in_ref, g_cq_ref, w_uq_ref, g_ckv_ref, w_ukv_ref, gq_ref, gk_ref,
             invf_ref, sgn_ref, proj_ref, q_ref, k_ref, v_ref):
        for r0 in range(0, tm, ts):
            rows = slice(r0, r0 + ts)
            xv = x_ref[rows, :]
            h = (xv * _rep(_inv_rms_mxu(xv), D_MODEL) * g_in_ref[...]).astype(BF16)
            def project(c0):
                proj_ref[rows, c0:c0 + 512] = _dot(h, w_in_ref[:, c0:c0 + 512])

            lat = _dot(h, w_in_ref[:, 0:512])
            proj_ref[rows, 0:512] = lat
            c_q = lat[:, 0:Q_LORA]
            cqn = (c_q * _rep(_inv_rms_mxu(c_q), Q_LORA) * g_cq_ref[...]).astype(BF16)
            c_kv = lat[:, Q_LORA:Q_LORA + KV_LORA]
            ckvn = (c_kv * _inv_rms_mxu(c_kv) * g_ckv_ref[...]).astype(BF16)
            kpe = lat[:, 384:512]
            kpe_sq = kpe * kpe
            cos_b, sin_b = _rope_tables(pos_ref.at[rows, :], invf_ref, sgn_ref)
            gq_a, gq_b = gq_ref[:, 0:NOPE], gq_ref[:, NOPE:HEAD_PAD]
            gk_a, gk_b = gk_ref[:, 0:NOPE], gk_ref[:, NOPE:HEAD_PAD]
            for hd in range(N_HEADS):
                project(512 * (hd + 1))
                c0 = hd * HEAD_PAD
                qh = _dot(cqn, w_uq_ref[:, c0:c0 + HEAD_PAD])
                a, b = qh[:, 0:NOPE], qh[:, NOPE:HEAD_PAD]
                r = lax.rsqrt(_lane_sum(a * a + b * b) / QK_DIM + EPS)
                bn = b * r * gq_b
                q_ref[hd, rows, 0:NOPE] = (a * r * gq_a).astype(BF16)
                q_ref[hd, rows, NOPE:HEAD_PAD] = (bn * cos_b + _swap_rope_halves(bn) * sin_b).astype(BF16)
                kvh = _dot(ckvn, w_ukv_ref[:, c0:c0 + HEAD_PAD])
                ka = kvh[:, 0:NOPE]
                rk = lax.rsqrt(_lane_sum(ka * ka + kpe_sq) / QK_DIM + EPS)
                kbn = kpe * rk * gk_b
                k_ref[hd, rows, 0:NOPE] = (ka * rk * gk_a).astype(BF16)
                k_ref[hd, rows, NOPE:HEAD_PAD] = (kbn * cos_b + _swap_rope_halves(kbn) * sin_b).astype(BF16)
                v_ref[hd, rows, 0:V_DIM] = kvh[:, NOPE:HEAD_PAD].astype(BF16)
                v_ref[hd, rows, V_DIM:2 * V_DIM] = jnp.ones((ts, V_DIM), BF16)
            project(512 * (N_HEADS + 1))

    row = lambda i: (i, 0)
    head_rows = lambda i: (0, i, 0)
    return pl.pallas_call(
        body, name="fwd_proj", grid=(T // tm,),
        in_specs=[pl.BlockSpec((tm, D_MODEL), row), pl.BlockSpec((tm, 1), row), _full((1, D_MODEL)),
                  _full((D_MODEL, PROJ_EXT)), _full((1, Q_LORA)), _full((Q_LORA, N_HEADS * HEAD_PAD)),
                  _full((1, KV_LORA)), _full((KV_LORA, N_HEADS * HEAD_PAD)), _full((1, HEAD_PAD)), _full((1, HEAD_PAD)),
                  _full((1, LANES)), _full((1, LANES))],
        out_specs=[pl.BlockSpec((tm, PROJ_EXT), row), pl.BlockSpec((N_HEADS, tm, HEAD_PAD), head_rows),
                   pl.BlockSpec((N_HEADS, tm, HEAD_PAD), head_rows), pl.BlockSpec((N_HEADS, tm, 2 * V_DIM), head_rows)],
        out_shape=[jax.ShapeDtypeStruct((T, PROJ_EXT), F32), jax.ShapeDtypeStruct((N_HEADS, T, HEAD_PAD), BF16),
                   jax.ShapeDtypeStruct((N_HEADS, T, HEAD_PAD), BF16), jax.ShapeDtypeStruct((N_HEADS, T, 2 * V_DIM), BF16)],
        compiler_params=_params(dimension_semantics=("arbitrary",)),
    )(x, pos, g_in, w_in, g_cq, w_uq, g_ckv, w_ukv, gq, gk, invf, sgn)


def _chunk_pipeline(n_loop, lag, matmuls, pointwise, accumulate, last):
    slots = lag + 1

    def iteration(t, slot):
        matmuls(jnp.minimum(t + lag, n_loop), (slot + lag) % slots)
        accumulate(jnp.maximum(t - lag, 0), (slot + 1) % slots)
        pointwise(t, slot, False)

    def finish(slot):
        for back in range(lag, 0, -1):
            accumulate(jnp.maximum(n_loop - back, 0), (slot - back) % slots)
        pointwise(n_loop, slot, True)
        accumulate(n_loop, slot)
        last()

    for u in range(lag):
        matmuls(jnp.minimum(u, n_loop), u)

    def unrolled(tt, carry):
        for slot in range(slots):
            iteration(slots * tt + slot, slot)
        return carry

    lax.fori_loop(0, n_loop // slots, unrolled, 0)
    rest = lax.rem(n_loop, slots)
    t0 = n_loop - rest

    for r in range(slots):
        @pl.when(rest == r)
        def _():
            for slot in range(r):
                iteration(t0 + slot, slot)
            finish(r)


def _attn_fwd(q, k, v, tq):
    T = q.shape[1]
    tk = tq
    rc = min(SOFTMAX_ROWS, tq)

    def body(q_ref, k_ref, v_ref, o_ref, lse_ref, s0, s1, s2, p0, p1, p2, a0, a1, a2, m_ref, acc_ref):
        qi = pl.program_id(1)
        s_buf, p_buf, a_buf = (s0, s1, s2), (p0, p1, p2), (a0, a1, a2)

        def scores(t, slot):
            ks = pl.multiple_of(t * tk, tk)
            s_buf[slot][...] = _dot_nt(q_ref[0], k_ref[0, pl.ds(ks, tk), :])

        def values(t, slot):
            ks = pl.multiple_of(t * tk, tk)
            acc_ref[...] = acc_ref[...] * a_buf[slot][...] + _dot(p_buf[slot][...], v_ref[0, pl.ds(ks, tk), :])

        def softmax(t, slot, masked):
            s_all = s_buf[slot][...]
            if masked:
                row = lax.broadcasted_iota(jnp.int32, (tq, tk), 0)
                col = lax.broadcasted_iota(jnp.int32, (tq, tk), 1)
                s_all = jnp.where(col <= row, s_all, NEG)
                s_buf[slot][...] = s_all
            m_old = m_ref[...]
            m_new = jnp.maximum(m_old, jnp.max(s_all, axis=1, keepdims=True))
            a_buf[slot][...] = jnp.exp2((m_old - m_new) * EXP2_SCALE)
            m_ref[...] = m_new
            for r0 in range(0, tq, rc):
                s = s_buf[slot][r0:r0 + rc, :]
                p_buf[slot][r0:r0 + rc, :] = jnp.exp2((s - m_new[r0:r0 + rc, :]) * EXP2_SCALE).astype(BF16)

        def last():
            l = acc_ref[:, V_DIM:2 * V_DIM]
            o_ref[...] = acc_ref[:, 0:V_DIM] / l
            lse_ref[0] = m_ref[...] * SCALE + jnp.log(l)

        m_ref[...] = jnp.full_like(m_ref, NEG)
        acc_ref[...] = jnp.zeros_like(acc_ref)
        for p_late, a_late in ((p1, a1), (p2, a2)):
            p_late[...] = jnp.zeros_like(p_late)
            a_late[...] = jnp.ones_like(a_late)
        _chunk_pipeline(qi, 2, scores, softmax, values, last)

    return pl.pallas_call(
        body, name="attn_fwd", grid=(N_HEADS, T // tq),
        in_specs=[pl.BlockSpec((1, tq, HEAD_PAD), lambda h, i: (h, i, 0)),
                  pl.BlockSpec((1, T, HEAD_PAD), lambda h, i: (h, 0, 0)),
                  pl.BlockSpec((1, T, 2 * V_DIM), lambda h, i: (h, 0, 0))],
        out_specs=[pl.BlockSpec((tq, V_DIM), lambda h, i: (i, h)),
                   pl.BlockSpec((1, tq, LANES), lambda h, i: (h, i, 0))],
        out_shape=[jax.ShapeDtypeStruct((T, ATTN_W), F32), jax.ShapeDtypeStruct((N_HEADS, T, LANES), F32)],
        scratch_shapes=[pltpu.VMEM((tq, tk), F32)] * 3 + [pltpu.VMEM((tq, tk), BF16)] * 3
                       + [pltpu.VMEM((tq, 1), F32)] * 4 + [pltpu.VMEM((tq, 2 * V_DIM), F32)],
        compiler_params=_params(dimension_semantics=("arbitrary", "arbitrary")),
    )(q, k, v)


def _tail(x, o, proj, p, tgt, g_oa, g_oc, g_pl, conv_w, w_o, w_pl, w_plg, tm):
    T = x.shape[0]
    nt = T // tm

    def body(x_ref, o_ref, za_ref, cb_ref, cc_ref, cx_ref, zc_ref, cch_ref, cxh_ref, p_ref, tgt_ref,
             g_oa_ref, g_oc_ref, g_pl_ref, cw_ref, w_o_ref, w_pl_ref, w_plg_ref,
             dx1_ref, do_ref, delta_ref, dtail_ref, du_ref,
             dw_o_ref, dw_pl_ref, dw_plg_ref, dg_oa_ref, dg_oc_ref, dg_pl_ref, dcw_ref, loss_ref):
        i = pl.program_id(0)

        @pl.when(i == 0)
        def _():
            for r in (dw_o_ref, dw_pl_ref, dw_plg_ref, dg_oa_ref, dg_oc_ref, dg_pl_ref, dcw_ref, loss_ref):
                r[...] = jnp.zeros_like(r)

        xv, ov, za, cb, zc = x_ref[...], o_ref[...], za_ref[...], cb_ref[...], zc_ref[...]
        g_oa, g_oc, g_pl = g_oa_ref[...], g_oc_ref[...], g_pl_ref[...]
        w0, w1, w2 = cw_ref[0:1, :], cw_ref[1:2, :], cw_ref[2:3, :]

        pb = p_ref[...].astype(BF16)
        pp = _dot(pb, w_pl_ref[...])

        sa = _sigmoid(za)
        silu_a = za * sa
        ga = ov * silu_a
        ra = _inv_rms(ga, ATTN_W)
        xa = ga * ra
        ya = xa * g_oa
        v = cc_ref[...] * cx_ref[...]
        not_first = jnp.where(i > 0, 1.0, 0.0)
        hv6 = cch_ref[6:7, :] * cxh_ref[6:7, :] * not_first
        hv7 = cch_ref[7:8, :] * cxh_ref[7:8, :] * not_first
        row = lax.broadcasted_iota(jnp.int32, v.shape, 0)
        v1 = jnp.where(row == 0, hv7, pltpu.roll(v, 1, 0))
        v2 = jnp.where(row == 0, hv6, jnp.where(row == 1, hv7, pltpu.roll(v, 2, 0)))
        u = w0 * v2 + w1 * v1 + w2 * v
        sc = _sigmoid(zc)
        silu_c = zc * sc
        gc = cb * u * silu_c
        rc = _inv_rms(gc, CONV_W)
        xc = gc * rc
        yc = xc * g_oc
        ycat = jnp.concatenate([ya, yc], axis=-1).astype(BF16)
        x1 = xv + _dot(ycat, w_o_ref[...])
        r1 = _inv_rms(x1, D_MODEL)
        xh1 = x1 * r1
        n1 = (xh1 * g_pl).astype(BF16)
        gate = _sigmoid(_dot(n1, w_plg_ref[...]))
        err = x1 + gate * pp - tgt_ref[...]
        loss_ref[...] += 0.5 * jnp.sum(err * err) / D_MODEL
        dy = err / D_MODEL

        dpp = (dy * gate).astype(BF16)
        da = (dy * pp * gate * (1.0 - gate)).astype(BF16)
        dn1 = _dot_nt(da, w_plg_ref[...])
        dw_pl_ref[...] += _dot_tn(pb, dpp)
        dw_plg_ref[...] += _dot_tn(n1, da)
        dg_pl_ref[...] += _colsum(dn1 * xh1)
        dxh = dn1 * g_pl
        dx1 = dy + r1 * (dxh - xh1 * (jnp.sum(dxh * xh1, axis=-1, keepdims=True) / D_MODEL))
        dx1_ref[...] = dx1
        dx1b = dx1.astype(BF16)
        dycat = _dot_nt(dx1b, w_o_ref[...])
        dya, dyc = dycat[:, 0:ATTN_W], dycat[:, ATTN_W:D_MODEL]

        dw_o_ref[0:ATTN_W, :] += _dot_tn(ycat[:, 0:ATTN_W], dx1b)
        dg_oa_ref[...] += _colsum(dya * xa)
        dxa = dya * g_oa
        dga = ra * (dxa - xa * (jnp.sum(dxa * xa, axis=-1, keepdims=True) / ATTN_W))
        do = (dga * silu_a).astype(BF16)
        do_ref[...] = do
        dof = do.astype(F32) * ov
        for hd in range(N_HEADS):
            delta_ref[hd] = _lane_sum(dof[:, hd * V_DIM:(hd + 1) * V_DIM])
        dtail_ref[:, 0:512] = (dga * ov * (sa * (1.0 + za * (1.0 - sa)))).astype(BF16)

        dw_o_ref[ATTN_W:D_MODEL, :] += _dot_tn(ycat[:, ATTN_W:D_MODEL], dx1b)
        dg_oc_ref[...] += _colsum(dyc * xc)
        dxc = dyc * g_oc
        dgc = rc * (dxc - xc * (jnp.sum(dxc * xc, axis=-1, keepdims=True) / CONV_W))
        dtail_ref[:, 512:1024] = (dgc * u * silu_c).astype(BF16)
        du = dgc * cb * silu_c
        du_ref[...] = du
        dtail_ref[:, 1024:1536] = (dgc * cb * u * (sc * (1.0 + zc * (1.0 - sc)))).astype(BF16)
        dcw_ref[0:1, :] += _colsum(du * v2)
        dcw_ref[1:2, :] += _colsum(du * v1)
        dcw_ref[2:3, :] += _colsum(du * v)

    row = lambda i: (i, 0)
    col = lambda c: (lambda i: (i, c))
    halo = lambda c: (lambda i: (jnp.maximum(i * (tm // 8) - 1, 0), c))
    in_specs = [pl.BlockSpec((tm, D_MODEL), row), pl.BlockSpec((tm, ATTN_W), row)]
    in_specs += [pl.BlockSpec((tm, 512), col(c)) for c in (1, 2, 3, 4, 5)]
    in_specs += [pl.BlockSpec((8, 512), halo(3)), pl.BlockSpec((8, 512), halo(4))]
    in_specs += [pl.BlockSpec((tm, PLE), row), pl.BlockSpec((tm, D_MODEL), row),
                 _full((1, ATTN_W)), _full((1, CONV_W)), _full((1, D_MODEL)), _full((3, CONV_W)),
                 _full((D_MODEL, D_MODEL)), _full((PLE, D_MODEL)), _full((D_MODEL, D_MODEL))]
    out_specs = [pl.BlockSpec((tm, D_MODEL), row), pl.BlockSpec((tm, ATTN_W), row),
                 pl.BlockSpec((N_HEADS, tm, LANES), lambda i: (0, i, 0)), pl.BlockSpec((tm, 1536), row),
                 pl.BlockSpec((tm, CONV_W), row),
                 _full((D_MODEL, D_MODEL)), _full((PLE, D_MODEL)), _full((D_MODEL, D_MODEL)),
                 _full((1, ATTN_W)), _full((1, CONV_W)), _full((1, D_MODEL)), _full((3, CONV_W)), _full((1, LANES))]
    out_shape = [jax.ShapeDtypeStruct((T, D_MODEL), F32), jax.ShapeDtypeStruct((T, ATTN_W), BF16),
                 jax.ShapeDtypeStruct((N_HEADS, T, LANES), F32), jax.ShapeDtypeStruct((T, 1536), BF16),
                 jax.ShapeDtypeStruct((T, CONV_W), F32),
                 jax.ShapeDtypeStruct((D_MODEL, D_MODEL), F32), jax.ShapeDtypeStruct((PLE, D_MODEL), F32),
                 jax.ShapeDtypeStruct((D_MODEL, D_MODEL), F32),
                 jax.ShapeDtypeStruct((1, ATTN_W), F32), jax.ShapeDtypeStruct((1, CONV_W), F32),
                 jax.ShapeDtypeStruct((1, D_MODEL), F32), jax.ShapeDtypeStruct((3, CONV_W), F32),
                 jax.ShapeDtypeStruct((1, LANES), F32)]
    return pl.pallas_call(
        body, name="tail", grid=(nt,), in_specs=in_specs, out_specs=out_specs, out_shape=out_shape,
        compiler_params=_params(dimension_semantics=("arbitrary",)),
    )(x, o, proj, proj, proj, proj, proj, proj, proj, p, tgt, g_oa, g_oc, g_pl, conv_w, w_o, w_pl, w_plg)


def _attn_bwd(q, k, v, do, lse_row, delta_row, tk):
    T = q.shape[1]
    tq = tk
    nq = T // tq
    rc = min(SOFTMAX_ROWS, tk)

    def body(q_ref, k_ref, v_ref, do_ref, lse_ref, dl_ref, dq_ref, dk_ref, dv_ref,
             s0, s1, d0, d1, p0, p1, g0, g1, dk_acc, dv_acc):
        kj = pl.program_id(1)
        s_buf, dp_buf, p_buf, g_buf = (s0, s1), (d0, d1), (p0, p1), (g0, g1)

        @pl.when(kj == 0)
        def _():
            dq_ref[...] = jnp.zeros_like(dq_ref)

        def q_start(t):
            return pl.multiple_of((nq - 1 - t) * tq, tq)

        def matmuls(t, slot):
            qs = q_start(t)
            s_buf[slot][...] = _dot_nt(k_ref[0], q_ref[0, pl.ds(qs, tq), :])
            dp_buf[slot][...] = _dot_nt(v_ref[0], do_ref[pl.ds(qs, tq), :])

        def pointwise(t, slot, masked):
            qs = q_start(t)
            lse2 = lse_ref[0, :, pl.ds(qs, tq)] * LOG2E
            dl = dl_ref[0, :, pl.ds(qs, tq)]
            for r0 in range(0, tk, rc):
                st = s_buf[slot][r0:r0 + rc, :]
                if masked:
                    row = lax.broadcasted_iota(jnp.int32, (rc, tq), 0)
                    col = lax.broadcasted_iota(jnp.int32, (rc, tq), 1)
                    st = jnp.where(row + r0 <= col, st, NEG)
                pt = jnp.exp2(st * EXP2_SCALE - lse2)
                p_buf[slot][r0:r0 + rc, :] = pt.astype(BF16)
                g_buf[slot][r0:r0 + rc, :] = (pt * (dp_buf[slot][r0:r0 + rc, :] - dl) * SCALE).astype(BF16)

        def accumulate(t, slot):
            qs = q_start(t)
            dv_acc[...] += _dot(p_buf[slot][...], do_ref[pl.ds(qs, tq), :])
            dk_acc[...] += _dot(g_buf[slot][...], q_ref[0, pl.ds(qs, tq), :])
            dq_ref[0, pl.ds(qs, tq), :] += _dot_tn(g_buf[slot][...], k_ref[0])

        def last():
            dk_ref[0] = dk_acc[...]
            dv_ref[0] = dv_acc[...]

        dk_acc[...] = jnp.zeros_like(dk_acc)
        dv_acc[...] = jnp.zeros_like(dv_acc)
        for late in (p1, g1):
            late[...] = jnp.zeros_like(late)
        _chunk_pipeline(nq - 1 - kj, 1, matmuls, pointwise, accumulate, last)

    return pl.pallas_call(
        body, name="attn_bwd", grid=(N_HEADS, T // tk),
        in_specs=[pl.BlockSpec((1, T, HEAD_PAD), lambda h, j: (h, 0, 0)),
                  pl.BlockSpec((1, tk, HEAD_PAD), lambda h, j: (h, j, 0)),
                  pl.BlockSpec((1, tk, V_DIM), lambda h, j: (h, j, 0)),
                  pl.BlockSpec((T, V_DIM), lambda h, j: (0, h)),
                  pl.BlockSpec((1, 1, T), lambda h, j: (h, 0, 0)),
                  pl.BlockSpec((1, 1, T), lambda h, j: (h, 0, 0))],
        out_specs=[pl.BlockSpec((1, T, HEAD_PAD), lambda h, j: (h, 0, 0)),
                   pl.BlockSpec((1, tk, HEAD_PAD), lambda h, j: (h, j, 0)),
                   pl.BlockSpec((1, tk, V_DIM), lambda h, j: (h, j, 0))],
        out_shape=[jax.ShapeDtypeStruct((N_HEADS, T, HEAD_PAD), F32), jax.ShapeDtypeStruct((N_HEADS, T, HEAD_PAD), F32),
                   jax.ShapeDtypeStruct((N_HEADS, T, V_DIM), F32)],
        scratch_shapes=[pltpu.VMEM((tk, tq), F32)] * 4 + [pltpu.VMEM((tk, tq), BF16)] * 4
                       + [pltpu.VMEM((tk, HEAD_PAD), F32), pltpu.VMEM((tk, V_DIM), F32)],
        compiler_params=_params(dimension_semantics=("arbitrary", "arbitrary")),
    )(q, k, v, do, lse_row, delta_row)


def _bwd_proj(x, dx1, pos, proj, dq, dk, dv, dtail, du, g_in, w_in, g_cq, w_uq, g_ckv, w_ukv, gq, gk, conv_w,
              invf, sgn, tm):
    T = x.shape[0]
    nt = T // tm

    ts = min(SUB_TILE, tm)

    def body(x_ref, dx1_ref, pos_ref, lat_ref, cc_ref, cx_ref, dq_ref, dk_ref, dv_ref, dtail_ref, du_ref, dun_ref, *rest):
        consts, (gx_ref, h_ref, dproj_ref), sums = rest[:11], rest[11:14], rest[14:]
        cw_ref = consts[8]
        i = pl.program_id(0)

        @pl.when(i == 0)
        def _():
            for r in sums:
                r[...] = jnp.zeros_like(r)

        du_v = du_ref[...]
        not_last = jnp.where(i < nt - 1, 1.0, 0.0)
        nx0 = dun_ref[0:1, :] * not_last
        nx1 = dun_ref[1:2, :] * not_last
        row = lax.broadcasted_iota(jnp.int32, du_v.shape, 0)
        du1 = jnp.where(row == tm - 1, nx0, pltpu.roll(du_v, tm - 1, 0))
        du2 = jnp.where(row == tm - 2, nx0, jnp.where(row == tm - 1, nx1, pltpu.roll(du_v, tm - 2, 0)))
        dvc = cw_ref[2:3, :] * du_v + cw_ref[1:2, :] * du1 + cw_ref[0:1, :] * du2
        dproj_ref[:, 1536:2048] = (dvc * cx_ref[...]).astype(BF16)
        dproj_ref[:, 2048:2560] = (dvc * cc_ref[...]).astype(BF16)

        for r0 in range(0, tm, ts):
            rows = slice(r0, r0 + ts)
            work(x_ref.at[rows, :], dx1_ref.at[rows, :], pos_ref.at[rows, :], lat_ref.at[rows, :],
                 dq_ref.at[:, rows, :], dk_ref.at[:, rows, :], dv_ref.at[:, rows, :], dtail_ref.at[rows, :], *consts,
                 gx_ref.at[rows, :], h_ref.at[:, rows], dproj_ref.at[rows, :], *sums)

    def work(x_ref, dx1_ref, pos_ref, lat_ref, dq_ref, dk_ref, dv_ref, dtail_ref,
             g_in_ref, w_in_ref, g_cq_ref, w_uq_ref, g_ckv_ref, w_ukv_ref, gq_ref, gk_ref, cw_ref, invf_ref, sgn_ref,
             gx_ref, h_ref, dproj_ref, dw_uq_ref, dw_ukv_ref, dg_in_ref, dg_cq_ref, dg_ckv_ref, dgq_ref, dgk_ref):
        xv = x_ref[...]
        r0 = _rep(_inv_rms_mxu(xv), D_MODEL)
        xh0 = xv * r0
        g_in = g_in_ref[...]
        h_ref[...] = (xh0 * g_in).astype(BF16).T

        c_q = lat_ref[:, 0:Q_LORA]
        rq = _rep(_inv_rms_mxu(c_q), Q_LORA)
        xq = c_q * rq
        g_cq = g_cq_ref[...]
        cqn = (xq * g_cq).astype(BF16)
        c_kv = lat_ref[:, Q_LORA:Q_LORA + KV_LORA]
        rkv = _inv_rms_mxu(c_kv)
        xkv = c_kv * rkv
        g_ckv = g_ckv_ref[...]
        ckvn = (xkv * g_ckv).astype(BF16)
        kpe = lat_ref[:, 384:512]
        kpe_sq = kpe * kpe
        cos_b, sin_b = _rope_tables(pos_ref, invf_ref, sgn_ref)
        gq_a, gq_b = gq_ref[:, 0:NOPE], gq_ref[:, NOPE:HEAD_PAD]
        gk_a, gk_b = gk_ref[:, 0:NOPE], gk_ref[:, NOPE:HEAD_PAD]

        dproj_ref[:, 512:1536] = dtail_ref[:, 0:1024]
        dproj_ref[:, 2560:3072] = dtail_ref[:, 1024:1536]

        def dh_part(c0):
            return _dot_nt(dproj_ref[:, c0:c0 + 512], w_in_ref[:, c0:c0 + 512])

        later_chunks = ((512,), (1024,), (1536, 2048), (2560,))
        dh = jnp.zeros((ts, D_MODEL), F32)
        dkpe = jnp.zeros((ts, LANES), F32)
        dcqn = jnp.zeros((ts, Q_LORA), F32)
        dckvn = jnp.zeros((ts, KV_LORA), F32)
        for hd in range(N_HEADS):
            for chunk in later_chunks[hd]:
                dh = dh + dh_part(chunk)
            c0 = hd * HEAD_PAD
            qh = _dot(cqn, w_uq_ref[:, c0:c0 + HEAD_PAD])
            a, b = qh[:, 0:NOPE], qh[:, NOPE:HEAD_PAD]
            r = lax.rsqrt(_lane_sum(a * a + b * b) / QK_DIM + EPS)
            xa, xb = a * r, b * r
            dan = dq_ref[hd, :, 0:NOPE]
            dbr = dq_ref[hd, :, NOPE:HEAD_PAD]
            dbn = dbr * cos_b + _swap_rope_halves(dbr * sin_b)
            dgq_ref[:, 0:NOPE] += _colsum(dan * xa)
            dgq_ref[:, NOPE:HEAD_PAD] += _colsum(dbn * xb)
            dxa, dxb = dan * gq_a, dbn * gq_b
            cq = _lane_sum(dxa * xa + dxb * xb) / QK_DIM
            dqh = jnp.concatenate([r * (dxa - xa * cq), r * (dxb - xb * cq)], axis=-1).astype(BF16)
            dw_uq_ref[:, c0:c0 + HEAD_PAD] += _dot_tn(cqn, dqh)
            dcqn = dcqn + _dot_nt(dqh, w_uq_ref[:, c0:c0 + HEAD_PAD])
            kvh = _dot(ckvn, w_ukv_ref[:, c0:c0 + HEAD_PAD])
            ka = kvh[:, 0:NOPE]
            rk = lax.rsqrt(_lane_sum(ka * ka + kpe_sq) / QK_DIM + EPS)
            xka, xkb = ka * rk, kpe * rk
            dkan = dk_ref[hd, :, 0:NOPE]
            dkbr = dk_ref[hd, :, NOPE:HEAD_PAD]
            dkbn = dkbr * cos_b + _swap_rope_halves(dkbr * sin_b)
            dgk_ref[:, 0:NOPE] += _colsum(dkan * xka)
            dgk_ref[:, NOPE:HEAD_PAD] += _colsum(dkbn * xkb)
            dxka, dxkb = dkan * gk_a, dkbn * gk_b
            ck = _lane_sum(dxka * xka + dxkb * xkb) / QK_DIM
            dkpe = dkpe + rk * (dxkb - xkb * ck)
            dkvh = jnp.concatenate([rk * (dxka - xka * ck), dv_ref[hd]], axis=-1).astype(BF16)
            dw_ukv_ref[:, c0:c0 + HEAD_PAD] += _dot_tn(ckvn, dkvh)
            dckvn = dckvn + _dot_nt(dkvh, w_ukv_ref[:, c0:c0 + HEAD_PAD])

        dg_cq_ref[...] += _colsum(dcqn * xq)
        dxq = dcqn * g_cq
        dproj_ref[:, 0:Q_LORA] = (rq * (dxq - xq * _rep(_lane_sum(dxq * xq) / Q_LORA, Q_LORA))).astype(BF16)
        dg_ckv_ref[...] += _colsum(dckvn * xkv)
        dxkv = dckvn * g_ckv
        dproj_ref[:, 256:384] = (rkv * (dxkv - xkv * (_lane_sum(dxkv * xkv) / KV_LORA))).astype(BF16)
        dproj_ref[:, 384:512] = dkpe.astype(BF16)
        dh = dh + dh_part(0)
        dg_in_ref[...] += _colsum(dh * xh0)
        dxh = dh * g_in
        gx_ref[...] = dx1_ref[...] + r0 * (dxh - xh0 * _rep(_lane_sum(dxh * xh0) / D_MODEL, D_MODEL))

    row = lambda i: (i, 0)
    col = lambda c: (lambda i: (i, c))
    head_rows = lambda i: (0, i, 0)
    nxt = lambda i: (jnp.minimum((i + 1) * (tm // 8), T // 8 - 1), 0)
    in_specs = [pl.BlockSpec((tm, D_MODEL), row), pl.BlockSpec((tm, D_MODEL), row), pl.BlockSpec((tm, 1), row),
                pl.BlockSpec((tm, 512), col(0)), pl.BlockSpec((tm, 512), col(3)), pl.BlockSpec((tm, 512), col(4)),
                pl.BlockSpec((N_HEADS, tm, HEAD_PAD), head_rows), pl.BlockSpec((N_HEADS, tm, HEAD_PAD), head_rows),
                pl.BlockSpec((N_HEADS, tm, V_DIM), head_rows), pl.BlockSpec((tm, 1536), row),
                pl.BlockSpec((tm, CONV_W), row), pl.BlockSpec((8, CONV_W), nxt),
                _full((1, D_MODEL)), _full((D_MODEL, PROJ_EXT)), _full((1, Q_LORA)), _full((Q_LORA, N_HEADS * HEAD_PAD)),
                _full((1, KV_LORA)), _full((KV_LORA, N_HEADS * HEAD_PAD)), _full((1, HEAD_PAD)), _full((1, HEAD_PAD)),
                _full((3, CONV_W)), _full((1, LANES)), _full((1, LANES))]
    out_specs = [pl.BlockSpec((tm, D_MODEL), row), pl.BlockSpec((D_MODEL, tm), lambda i: (0, i)),
                 pl.BlockSpec((tm, PROJ_EXT), row),
                 _full((Q_LORA, N_HEADS * HEAD_PAD)), _full((KV_LORA, N_HEADS * HEAD_PAD)),
                 _full((1, D_MODEL)), _full((1, Q_LORA)), _full((1, KV_LORA)), _full((1, HEAD_PAD)), _full((1, HEAD_PAD))]
    out_shape = [jax.ShapeDtypeStruct((T, D_MODEL), F32), jax.ShapeDtypeStruct((D_MODEL, T), BF16),
                 jax.ShapeDtypeStruct((T, PROJ_EXT), BF16),
                 jax.ShapeDtypeStruct((Q_LORA, N_HEADS * HEAD_PAD), F32), jax.ShapeDtypeStruct((KV_LORA, N_HEADS * HEAD_PAD), F32),
                 jax.ShapeDtypeStruct((1, D_MODEL), F32), jax.ShapeDtypeStruct((1, Q_LORA), F32),
                 jax.ShapeDtypeStruct((1, KV_LORA), F32), jax.ShapeDtypeStruct((1, HEAD_PAD), F32),
                 jax.ShapeDtypeStruct((1, HEAD_PAD), F32)]
    return pl.pallas_call(
        body, name="bwd_proj", grid=(nt,), in_specs=in_specs, out_specs=out_specs, out_shape=out_shape,
        compiler_params=_params(dimension_semantics=("arbitrary",)),
    )(x, dx1, pos, proj, proj, proj, dq, dk, dv, dtail, du, du, g_in, w_in, g_cq, w_uq, g_ckv, w_ukv, gq, gk, conv_w,
      invf, sgn)


def _matmul_acc(a, b, tt, tn):
    M, T = a.shape
    N = b.shape[1]

    def body(a_ref, b_ref, o_ref):
        @pl.when(pl.program_id(1) == 0)
        def _():
            o_ref[...] = jnp.zeros_like(o_ref)

        o_ref[...] += _dot(a_ref[...], b_ref[...])

    return pl.pallas_call(
        body, name="dw_in", grid=(N // tn, T // tt),
        in_specs=[pl.BlockSpec((M, tt), lambda j, t: (0, t)), pl.BlockSpec((tt, tn), lambda j, t: (t, j))],
        out_specs=pl.BlockSpec((M, tn), lambda j, t: (0, j)),
        out_shape=jax.ShapeDtypeStruct((M, N), F32),
        compiler_params=_params(dimension_semantics=("arbitrary", "arbitrary")),
    )(a, b)


def _add_pair(grads, from_sibling, small, small_sibling, c):
    n = len(grads)

    def body(c_ref, *refs):
        ins, outs = refs[:2 * n + 2], refs[2 * n + 2:]
        for i in range(n + 1):
            outs[i][...] = (ins[2 * i][...] + ins[2 * i + 1][...]).astype(outs[i].dtype)

    in_specs, out_specs, out_shape, args = [], [], [], []
    for g, r in zip(grads, from_sibling):
        _, hr, cols = r.shape
        in_specs += [pl.BlockSpec((1, hr, cols), lambda k, c_ref: (k, c_ref[0], 0)),
                     pl.BlockSpec((1, hr, cols), lambda k, c_ref: (k, 0, 0))]
        out_specs.append(pl.BlockSpec((1, hr, cols), lambda k, c_ref: (k, 0, 0)))
        out_shape.append(jax.ShapeDtypeStruct(r.shape, BF16))
        args += [g, r]
    whole = pl.BlockSpec(small.shape, lambda k, c_ref: (0, 0))
    in_specs += [whole, whole]
    out_specs.append(whole)
    out_shape.append(jax.ShapeDtypeStruct(small.shape, F32))
    outs = pl.pallas_call(
        body, name="add_pair", out_shape=out_shape,
        grid_spec=pltpu.PrefetchScalarGridSpec(num_scalar_prefetch=1, grid=(N_CHIPS,), in_specs=in_specs,
                                               out_specs=out_specs),
        compiler_params=_params(dimension_semantics=("arbitrary",)),
    )(c.reshape(1), *args, small, small_sibling)
    return outs[:n], outs[n]


def _add_chips(parts, small_parts):
    arrays = list(parts) + [small_parts]

    def body(*refs):
        ins, outs = refs[:len(arrays)], refs[len(arrays):]
        for a_ref, o_ref in zip(ins, outs):
            part = lambda k: a_ref[k].astype(F32)
            o_ref[...] = ((part(0) + part(1)) + part(2)) + part(3)

    in_specs, out_specs, out_shape = [], [], []
    for a in arrays:
        _, rows, cols = a.shape
        in_specs.append(pl.BlockSpec((N_CHIPS, rows // 2, cols), lambda i: (0, i, 0)))
        out_specs.append(pl.BlockSpec((rows // 2, cols), lambda i: (i, 0)))
        out_shape.append(jax.ShapeDtypeStruct((rows, cols), F32))
    outs = pl.pallas_call(body, name="add_chips", grid=(2,), in_specs=in_specs, out_specs=out_specs,
                          out_shape=out_shape, compiler_params=_params(dimension_semantics=("arbitrary",)))(*arrays)
    return outs[:-1], outs[-1]


def _adamw(w, g, m, v, name):
    rows, cols = w.shape
    rb = 256 if rows * cols > 512 * 1024 else rows

    def body(w_ref, g_ref, m_ref, v_ref, d_ref, nm_ref, nv_ref):
        _adamw_math(g_ref[...], w_ref, m_ref, v_ref, d_ref, nm_ref, nv_ref)

    spec = pl.BlockSpec((rb, cols), lambda i: (i, 0))
    shp = jax.ShapeDtypeStruct(w.shape, F32)
    return pl.pallas_call(body, name=name, grid=(rows // rb,), in_specs=[spec] * 4, out_specs=[spec] * 3,
                          out_shape=[shp] * 3)(w, g, m, v)


def _adamw_math(gv, w_ref, m_ref, v_ref, d_ref, nm_ref, nv_ref):
    nm = B1 * m_ref[...] + (1.0 - B1) * gv
    nv = B2 * v_ref[...] + (1.0 - B2) * (gv * gv)
    m_hat = nm / (1.0 - B1 ** STEP)
    v_hat = nv / (1.0 - B2 ** STEP)
    d_ref[...] = -LR * (m_hat / (jnp.sqrt(v_hat) + ADAM_EPS) + WD * w_ref[...])
    nm_ref[...] = nm
    nv_ref[...] = nv


def _adamw_halves(w, mine, other, m, v, c, name):
    hr, cols = mine.shape

    def body(c_ref, w_ref, mine_ref, other_ref, m_ref, v_ref, g_ref, d_ref, nm_ref, nv_ref):
        gv = jnp.where(pl.program_id(0) == c_ref[0], mine_ref[...], other_ref[...])
        g_ref[...] = gv
        _adamw_math(gv, w_ref, m_ref, v_ref, d_ref, nm_ref, nv_ref)

    half = pl.BlockSpec((hr, cols), lambda i, c_ref: (i, 0))
    whole = pl.BlockSpec((hr, cols), lambda i, c_ref: (0, 0))
    shp = jax.ShapeDtypeStruct(w.shape, F32)
    return pl.pallas_call(
        body, name=name, out_shape=[shp] * 4,
        grid_spec=pltpu.PrefetchScalarGridSpec(num_scalar_prefetch=1, grid=(2,), in_specs=[half, whole, whole, half, half],
                                               out_specs=[half] * 4),
        compiler_params=_params(dimension_semantics=("arbitrary",)),
    )(c.reshape(1), w, mine, other, m, v)


_ANY = pl.BlockSpec(memory_space=pl.ANY)


def _mesh_pos():
    return lax.axis_index("x"), lax.axis_index("y"), lax.axis_index("c")


def _other_chips(x, y):
    return [(1 - x, y), (x, 1 - y), (1 - x, 1 - y)]


def _remote(src, dst, send_sems, recv_sems, k, to):
    return pltpu.make_async_remote_copy(src_ref=src, dst_ref=dst, send_sem=send_sems.at[k], recv_sem=recv_sems.at[k],
                                        device_id=to, device_id_type=MESH)


def _gather_weights(shards):
    n = len(shards)
    halved = [s.shape[0] % 32 == 0 for s in shards]

    def body(*refs):
        ins, outs, stage = refs[:n], refs[n:2 * n], refs[2 * n:3 * n]
        send_sems, recv_sems, local_sems = refs[3 * n:]
        x, y, c = _mesh_pos()
        me = 2 * x + y
        chips = _other_chips(x, y)

        def part(i, ref, hc):
            if not halved[i]:
                return ref
            hr = shards[i].shape[0] // 2
            return ref.at[pl.ds(hc * hr, hr), :]

        locals_, started = [], []
        for i in range(n):
            stage[i][...] = ins[i][...].astype(BF16)
            mine = pltpu.make_async_copy(stage[i], outs[i].at[me], local_sems.at[i])
            mine.start()
            locals_.append(mine)
            for j, (cx, cy) in enumerate(chips):
                cp = _remote(part(i, stage[i], c), part(i, outs[i].at[me], c), send_sems, recv_sems, 6 * i + j, (cx, cy, c))
                cp.start()
                started.append(cp)
        for i in range(n):
            for j, (cx, cy) in enumerate(chips):
                got = part(i, outs[i].at[2 * cx + cy], c)
                _remote(got, got, send_sems, recv_sems, 6 * i + j, (cx, cy, c)).wait_recv()
                if halved[i]:
                    fwd = _remote(got, got, send_sems, recv_sems, 6 * i + 3 + j, (x, y, 1 - c))
                    fwd.start()
                    started.append(fwd)
        for i in range(n):
            if halved[i]:
                for j, (cx, cy) in enumerate(chips):
                    got = part(i, outs[i].at[2 * cx + cy], 1 - c)
                    _remote(got, got, send_sems, recv_sems, 6 * i + 3 + j, (x, y, 1 - c)).wait_recv()
        for cp in started:
            cp.wait_send()
        for cp in locals_:
            cp.wait()

    vmem = pl.BlockSpec(memory_space=pltpu.VMEM)
    return pl.pallas_call(
        body, name="gather_weights", in_specs=[vmem] * n, out_specs=[_ANY] * n,
        out_shape=[jax.ShapeDtypeStruct((N_CHIPS,) + s.shape, BF16) for s in shards],
        scratch_shapes=[pltpu.VMEM(s.shape, BF16) for s in shards]
                       + [pltpu.SemaphoreType.DMA((6 * n,)), pltpu.SemaphoreType.DMA((6 * n,)), pltpu.SemaphoreType.DMA((n,))],
        compiler_params=_params(),
    )(*shards)


def _swap_halves(grads, small):
    n = len(grads)
    arrays = list(grads) + [small]

    def body(*refs):
        ins, outs, send_sems, recv_sems = refs[:n + 1], refs[n + 1:2 * n + 2], refs[2 * n + 2], refs[2 * n + 3]
        x, y, c = _mesh_pos()
        cps = []
        for i in range(n + 1):
            src = ins[i]
            if i < n:
                hr = grads[i].shape[1] // 2
                src = src.at[:, pl.ds((1 - c) * hr, hr), :]
            cp = _remote(src, outs[i], send_sems, recv_sems, i, (x, y, 1 - c))
            cp.start()
            cps.append(cp)
        for cp in cps:
            cp.wait()

    out_shape = [jax.ShapeDtypeStruct((g.shape[0], g.shape[1] // 2, g.shape[2]), F32) for g in grads]
    out_shape.append(jax.ShapeDtypeStruct(small.shape, F32))
    outs = pl.pallas_call(
        body, name="pair_grads", in_specs=[_ANY] * (n + 1), out_specs=[_ANY] * (n + 1), out_shape=out_shape,
        scratch_shapes=[pltpu.SemaphoreType.DMA((n + 1,)), pltpu.SemaphoreType.DMA((n + 1,))],
    )(*arrays)
    return outs[:n], outs[n]


def _scatter_to_chips(parts, small):
    n = len(parts)
    arrays = list(parts) + [small]

    def body(*refs):
        ins, outs = refs[:n + 1], refs[n + 1:2 * n + 2]
        send_sems, recv_sems, local_sems = refs[2 * n + 2:]
        x, y, c = _mesh_pos()
        me = 2 * x + y
        chips = _other_chips(x, y)
        locals_, sends = [], []
        for i in range(n + 1):
            mine = pltpu.make_async_copy(ins[i].at[me] if i < n else ins[i], outs[i].at[me], local_sems.at[i])
            mine.start()
            locals_.append(mine)
            for j, (cx, cy) in enumerate(chips):
                src = ins[i].at[2 * cx + cy] if i < n else ins[i]
                cp = _remote(src, outs[i].at[me], send_sems, recv_sems, 3 * i + j, (cx, cy, c))
                cp.start()
                sends.append(cp)
        for i in range(n + 1):
            for j, (cx, cy) in enumerate(chips):
                got = outs[i].at[2 * cx + cy]
                _remote(got, got, send_sems, recv_sems, 3 * i + j, (cx, cy, c)).wait_recv()
        for cp in sends:
            cp.wait_send()
        for cp in locals_:
            cp.wait()

    out_shape = [jax.ShapeDtypeStruct(p.shape, p.dtype) for p in parts]
    out_shape.append(jax.ShapeDtypeStruct((N_CHIPS,) + small.shape, small.dtype))
    outs = pl.pallas_call(
        body, name="scatter_grads", in_specs=[_ANY] * (n + 1), out_specs=[_ANY] * (n + 1), out_shape=out_shape,
        scratch_shapes=[pltpu.SemaphoreType.DMA((3 * n + 3,)), pltpu.SemaphoreType.DMA((3 * n + 3,)),
                        pltpu.SemaphoreType.DMA((n + 1,))],
    )(*arrays)
    return outs[:n], outs[n]


def _share_halves(halves):
    n = len(halves)

    def body(*refs):
        ins, outs, send_sems, recv_sems = refs[:n], refs[n:2 * n], refs[2 * n], refs[2 * n + 1]
        x, y, c = _mesh_pos()
        cps = [_remote(ins[i], outs[i], send_sems, recv_sems, i, (x, y, 1 - c)) for i in range(n)]
        for cp in cps:
            cp.start()
        for cp in cps:
            cp.wait()

    return pl.pallas_call(
        body, name="share_halves", in_specs=[_ANY] * n, out_specs=[_ANY] * n,
        out_shape=[jax.ShapeDtypeStruct(h.shape, h.dtype) for h in halves],
        scratch_shapes=[pltpu.SemaphoreType.DMA((n,)), pltpu.SemaphoreType.DMA((n,))],
    )(*halves)


SHARD_COLS_IN = IN_TOTAL // N_CHIPS
KPE_END = Q_LORA + KV_LORA + ROPE


def _assemble_weights(c_in, c_uq, c_ukv, c_o, c_pl, c_plg, c_conv):
    by_cols = lambda a: a.transpose(1, 0, 2).reshape(a.shape[1], N_CHIPS * a.shape[2])
    w_in_e = jnp.concatenate([c_in[0][:, :KPE_END], jnp.zeros((D_MODEL, 64), BF16), c_in[0][:, KPE_END:],
                              c_in[1], c_in[2], c_in[3]], axis=1)
    w_uq_e = by_cols(jnp.pad(c_uq, ((0, 0), (0, 0), (0, HEAD_PAD - QK_DIM))))
    return (w_in_e, w_uq_e, by_cols(c_ukv), by_cols(c_conv).astype(F32), c_o.reshape(D_MODEL, D_MODEL),
            by_cols(c_pl), c_plg.reshape(D_MODEL, D_MODEL))


def _split_grads(dw_in_e, dw_uq_e, dw_ukv, dw_o, dw_pl, dw_plg):
    chip_major = lambda a: a.reshape(a.shape[0], N_CHIPS, a.shape[1] // N_CHIPS).transpose(1, 0, 2)
    first = jnp.concatenate([dw_in_e[:, :KPE_END], dw_in_e[:, KPE_END + 64:SHARD_COLS_IN + 64]], axis=1)
    rest = [dw_in_e[:, SHARD_COLS_IN * k + 64:SHARD_COLS_IN * (k + 1) + 64] for k in range(1, N_CHIPS)]
    return [jnp.stack([first] + rest), chip_major(dw_uq_e)[:, :, :QK_DIM], chip_major(dw_ukv),
            dw_o.reshape(N_CHIPS, D_MODEL // N_CHIPS, D_MODEL), chip_major(dw_pl),
            dw_plg.reshape(N_CHIPS, D_MODEL // N_CHIPS, D_MODEL)]


def _local_step(x, p, pos, tgt, gains, w_in_e, w_uq_e, w_ukv, conv_w, w_o, w_pl, w_plg, tm, tq):
    g_in, g_cq, g_ckv, g_q, g_k, g_oa, g_oc, g_pl = gains
    T = x.shape[0]
    zpad = lambda a, n: jnp.concatenate([a, jnp.zeros(a.shape[:-1] + (n,), a.dtype)], axis=-1)
    gq, gk = zpad(g_q, HEAD_PAD - QK_DIM), zpad(g_k, HEAD_PAD - QK_DIM)
    inv_freq = 1.0 / (ROPE_THETA ** (jnp.arange(0, ROPE, 2, dtype=F32) / ROPE))
    invf = jnp.concatenate([inv_freq, inv_freq, jnp.zeros((64,), F32)]).reshape(1, LANES)
    sgn = jnp.concatenate([-jnp.ones((32,), F32), jnp.ones((32,), F32), jnp.zeros((64,), F32)]).reshape(1, LANES)

    proj, q, k, v = _fwd_proj(x, pos, g_in, w_in_e, g_cq, w_uq_e, g_ckv, w_ukv, gq, gk, invf, sgn, min(2 * tm, T))
    o, lse = _attn_fwd(q, k, v, tq)
    (dx1, do, delta, dtail, du, dw_o, dw_pl, dw_plg, dg_oa, dg_oc, dg_pl, dconv, loss) = _tail(
        x, o, proj, p, tgt, g_oa, g_oc, g_pl, conv_w, w_o, w_pl, w_plg, tm)
    lse_row = lse[:, :, 0].reshape(N_HEADS, 1, T)
    delta_row = delta[:, :, 0].reshape(N_HEADS, 1, T)
    dq, dk, dv = _attn_bwd(q, k, v, do, lse_row, delta_row, tq)
    (gx, h, dproj, dw_uq_e, dw_ukv, dg_in, dg_cq, dg_ckv, dgq, dgk) = _bwd_proj(
        x, dx1, pos, proj, dq, dk, dv, dtail, du, g_in, w_in_e, g_cq, w_uq_e, g_ckv, w_ukv, gq, gk, conv_w, invf, sgn,
        min(2 * tm, T))
    dw_in_e = _matmul_acc(h, dproj, min(4096, T), 512)
    wgrads = (dw_in_e, dw_uq_e, dw_ukv, dw_o, dw_pl, dw_plg)
    ggrads = (dg_in, dg_cq, dg_ckv, dgq, dgk, dg_oa, dg_oc, dg_pl)
    return loss, gx, wgrads, ggrads, dconv


def kernel(x, p, positions, g_in, w_in, g_cq, w_uq, g_ckv, w_ukv, g_q, g_k, conv_w, g_oa, g_oc, w_o, w_pl, w_plg, g_pl, loss_target, m_g_in, m_w_in, m_g_cq, m_w_uq, m_g_ckv, m_w_ukv, m_g_q, m_g_k, m_conv_w, m_g_oa, m_g_oc, m_w_o, m_w_pl, m_w_plg, m_g_pl, v_g_in, v_w_in, v_g_cq, v_w_uq, v_g_ckv, v_w_ukv, v_g_q, v_g_k, v_conv_w, v_g_oa, v_g_oc, v_w_o, v_w_pl, v_w_plg, v_g_pl):
    T = x.shape[1]
    c = lax.axis_index("c")
    chip = 2 * lax.axis_index("x") + lax.axis_index("y")
    gains = [g.reshape(1, -1) for g in (g_in, g_cq, g_ckv, g_q, g_k, g_oa, g_oc, g_pl)]

    gathered = _gather_weights([w_in[0], w_uq[0], w_ukv[0], w_o[0], w_pl[0], w_plg[0], conv_w[0]])
    full = _assemble_weights(*gathered)

    loss, gx, wgrads, ggrads, dconv = _local_step(
        x[0], p[0, 0], positions.reshape(T, 1), loss_target[0], gains, *full, 256, 512)

    grads_cm = _split_grads(*wgrads)
    small_parts = [a.reshape(-1, LANES) for a in (*ggrads, loss, dconv)]
    small_rows = [a.shape[0] for a in small_parts]
    tile_rows = [-(-r // 8) * 8 for r in small_rows]
    tile_rows[-1] += -sum(tile_rows) % 16
    small = jnp.concatenate([jnp.pad(a, ((0, t - r), (0, 0))) for a, r, t in zip(small_parts, small_rows, tile_rows)])
    from_sibling, small_sibling = _swap_halves(grads_cm, small)
    chip_parts, chip_small = _add_pair(grads_cm, from_sibling, small, small_sibling, c)
    by_chip, small_by_chip = _scatter_to_chips(chip_parts, chip_small)
    halves, small_total = _add_chips(by_chip, small_by_chip)
    other_halves = _share_halves(halves)

    gg, off = [], 0
    for rows, tiled in zip(small_rows, tile_rows):
        gg.append(small_total[off:off + rows].reshape(1, -1))
        off += tiled
    loss_out = gg[8][0, 0]
    conv_total = gg[9].reshape(3, CONV_W)
    conv_g = lax.dynamic_slice(conv_total, (0, chip * (CONV_W // N_CHIPS)), (3, CONV_W // N_CHIPS))
    g_by_name = dict(g_in=gg[0], g_cq=gg[1], g_ckv=gg[2], g_q=gg[3][:, :QK_DIM], g_k=gg[4][:, :QK_DIM], conv_w=conv_g,
                     g_oa=gg[5], g_oc=gg[6], g_pl=gg[7])
    half_by_name = dict(zip(("w_in", "w_uq", "w_ukv", "w_o", "w_pl", "w_plg"), zip(halves, other_halves)))
    weights = dict(g_in=g_in, w_in=w_in, g_cq=g_cq, w_uq=w_uq, g_ckv=g_ckv, w_ukv=w_ukv, g_q=g_q, g_k=g_k,
                   conv_w=conv_w, g_oa=g_oa, g_oc=g_oc, w_o=w_o, w_pl=w_pl, w_plg=w_plg, g_pl=g_pl)
    ms = dict(g_in=m_g_in, w_in=m_w_in, g_cq=m_g_cq, w_uq=m_w_uq, g_ckv=m_g_ckv, w_ukv=m_w_ukv, g_q=m_g_q, g_k=m_g_k,
              conv_w=m_conv_w, g_oa=m_g_oa, g_oc=m_g_oc, w_o=m_w_o, w_pl=m_w_pl, w_plg=m_w_plg, g_pl=m_g_pl)
    vs = dict(g_in=v_g_in, w_in=v_w_in, g_cq=v_g_cq, w_uq=v_w_uq, g_ckv=v_g_ckv, w_ukv=v_w_ukv, g_q=v_g_q, g_k=v_g_k,
              conv_w=v_conv_w, g_oa=v_g_oa, g_oc=v_g_oc, w_o=v_w_o, w_pl=v_w_pl, w_plg=v_w_plg, g_pl=v_g_pl)
    names = list(weights)
    grads, deltas, new_m, new_v = [], [], [], []
    for n in names:
        w = weights[n]
        w2 = w.reshape(-1, w.shape[-1])
        if n in half_by_name:
            g2, d, nm, nv = _adamw_halves(w2, *half_by_name[n], ms[n].reshape(w2.shape), vs[n].reshape(w2.shape), c,
                                          "adamw_" + n)
        else:
            g2 = g_by_name[n].reshape(w2.shape)
            d, nm, nv = _adamw(w2, g2, ms[n].reshape(w2.shape), vs[n].reshape(w2.shape), "adamw_" + n)
        grads.append(g2.reshape(w.shape))
        deltas.append(d.reshape(w.shape))
        new_m.append(nm.reshape(w.shape))
        new_v.append(nv.reshape(w.shape))
    return (loss_out, gx.reshape(x.shape), *grads, *deltas, *new_m, *new_v)
```

```python
import functools
import math

import jax
import jax.numpy as jnp
from jax import lax
from jax.experimental import pallas as pl
from jax.experimental.pallas import tpu as pltpu

F32 = jnp.float32
BF16 = jnp.bfloat16

D_MODEL = 1024
N_HEADS = 4
NOPE = 128
ROPE = 64
V_DIM = 128
QK_DIM = NOPE + ROPE
HEAD_PAD = 256
Q_LORA = 256
KV_LORA = 128
ATTN_W = 512
CONV_W = 512
PLE = 256
IN_TOTAL = 3008
PROJ_EXT = 3072
ROPE_THETA = 10000.0
EPS = 1e-6
SCALE = 1.0 / math.sqrt(QK_DIM)
LOG2E = math.log2(math.e)
EXP2_SCALE = SCALE * LOG2E
NEG = -1e30
SOFTMAX_ROWS = 32
SUB_TILE = 256

LR, B1, B2, ADAM_EPS, WD, STEP = 0.001, 0.9, 0.999, 1e-08, 0.01, 10

N_CHIPS = 4
LANES = 128
VMEM_LIMIT = 56 * 1024 * 1024
MESH = pl.DeviceIdType.MESH


def _params(**kw):
    return pltpu.CompilerParams(vmem_limit_bytes=VMEM_LIMIT, **kw)


def _inv_rms(x, n):
    return lax.rsqrt(jnp.sum(x * x, axis=-1, keepdims=True) / n + EPS)


def _lane_sum(a):
    folded = a[:, 0:LANES]
    for c0 in range(LANES, a.shape[1], LANES):
        folded = folded + a[:, c0:c0 + LANES]
    head = folded.astype(BF16)
    tail = (folded - head.astype(F32)).astype(BF16)
    return _dot(jnp.concatenate([head, tail], axis=1), jnp.ones((2 * LANES, LANES), BF16))


def _inv_rms_mxu(x):
    return lax.rsqrt(_lane_sum(x * x) / x.shape[1] + EPS)


def _rep(r, width):
    return r if width == LANES else jnp.tile(r, (1, width // LANES))


def _sigmoid(z):
    return 1.0 / (1.0 + jnp.exp(-z))


def _swap_rope_halves(b):
    lane = lax.broadcasted_iota(jnp.int32, b.shape, 1)
    swapped = jnp.where(lane < 32, pltpu.roll(b, 96, 1), pltpu.roll(b, 32, 1))
    return jnp.where(lane < ROPE, swapped, 0.0)


def _dot(a, b):
    return jnp.dot(a, b, preferred_element_type=F32)


def _dot_nt(a, b):
    return lax.dot_general(a, b, (((1,), (1,)), ((), ())), preferred_element_type=F32)


def _dot_tn(a, b):
    return lax.dot_general(a, b, (((0,), (0,)), ((), ())), preferred_element_type=F32)


def _colsum(a):
    return jnp.sum(a, axis=0, keepdims=True)


def _full(shape):
    return pl.BlockSpec(shape, lambda *_: (0,) * len(shape))


def _rope_tables(pos_ref, invf_ref, sgn_ref):
    ang = pos_ref[...].astype(F32) * invf_ref[...]
    return jnp.cos(ang), jnp.sin(ang) * sgn_ref[...]


def _fwd_proj(x, pos, g_in, w_in, g_cq, w_uq, g_ckv, w_ukv, gq, gk, invf, sgn, tm):
    T = x.shape[0]

    ts = min(SUB_TILE, tm)

    def body(x_ref, pos_ref, g_in_ref, w_in_ref, g_cq_ref, w_uq_ref, g_ckv_ref, w_ukv_ref, gq_ref, gk_ref,
             invf_ref, sgn_ref, proj_ref, q_ref, k_ref, v_ref):
        for r0 in range(0, tm, ts):
            rows = slice(r0, r0 + ts)
            xv = x_ref[rows, :]
            h = (xv * _rep(_inv_rms_mxu(xv), D_MODEL) * g_in_ref[...]).astype(BF16)
            def project(c0):
                proj_ref[rows, c0:c0 + 512] = _dot(h, w_in_ref[:, c0:c0 + 512])

            lat = _dot(h, w_in_ref[:, 0:512])
            proj_ref[rows, 0:512] = lat
            c_q = lat[:, 0:Q_LORA]
            cqn = (c_q * _rep(_inv_rms_mxu(c_q), Q_LORA) * g_cq_ref[...]).astype(BF16)
            c_kv = lat[:, Q_LORA:Q_LORA + KV_LORA]
            ckvn = (c_kv * _inv_rms_mxu(c_kv) * g_ckv_ref[...]).astype(BF16)
            kpe = lat[:, 384:512]
            kpe_sq = kpe * kpe
            cos_b, sin_b = _rope_tables(pos_ref.at[rows, :], invf_ref, sgn_ref)
            gq_a, gq_b = gq_ref[:, 0:NOPE], gq_ref[:, NOPE:HEAD_PAD]
            gk_a, gk_b = gk_ref[:, 0:NOPE], gk_ref[:, NOPE:HEAD_PAD]
            for hd in range(N_HEADS):
                project(512 * (hd + 1))
                c0 = hd * HEAD_PAD
                qh = _dot(cqn, w_uq_ref[:, c0:c0 + HEAD_PAD])
                a, b = qh[:, 0:NOPE], qh[:, NOPE:HEAD_PAD]
                r = lax.rsqrt(_lane_sum(a * a + b * b) / QK_DIM + EPS)
                bn = b * r * gq_b
                q_ref[hd, rows, 0:NOPE] = (a * r * gq_a).astype(BF16)
                q_ref[hd, rows, NOPE:HEAD_PAD] = (bn * cos_b + _swap_rope_halves(bn) * sin_b).astype(BF16)
                kvh = _dot(ckvn, w_ukv_ref[:, c0:c0 + HEAD_PAD])
                ka = kvh[:, 0:NOPE]
                rk = lax.rsqrt(_lane_sum(ka * ka + kpe_sq) / QK_DIM + EPS)
                kbn = kpe * rk * gk_b
                k_ref[hd, rows, 0:NOPE] = (ka * rk * gk_a).astype(BF16)
                k_ref[hd, rows, NOPE:HEAD_PAD] = (kbn * cos_b + _swap_rope_halves(kbn) * sin_b).astype(BF16)
                v_ref[hd, rows, 0:V_DIM] = kvh[:, NOPE:HEAD_PAD].astype(BF16)
                v_ref[hd, rows, V_DIM:2 * V_DIM] = jnp.ones((ts, V_DIM), BF16)
            project(512 * (N_HEADS + 1))

    row = lambda i: (i, 0)
    head_rows = lambda i: (0, i, 0)
    return pl.pallas_call(
        body, name="fwd_proj", grid=(T // tm,),
        in_specs=[pl.BlockSpec((tm, D_MODEL), row), pl.BlockSpec((tm, 1), row), _full((1, D_MODEL)),
                  _full((D_MODEL, PROJ_EXT)), _full((1, Q_LORA)), _full((Q_LORA, N_HEADS * HEAD_PAD)),
                  _full((1, KV_LORA)), _full((KV_LORA, N_HEADS * HEAD_PAD)), _full((1, HEAD_PAD)), _full((1, HEAD_PAD)),
                  _full((1, LANES)), _full((1, LANES))],
        out_specs=[pl.BlockSpec((tm, PROJ_EXT), row), pl.BlockSpec((N_HEADS, tm, HEAD_PAD), head_rows),
                   pl.BlockSpec((N_HEADS, tm, HEAD_PAD), head_rows), pl.BlockSpec((N_HEADS, tm, 2 * V_DIM), head_rows)],
        out_shape=[jax.ShapeDtypeStruct((T, PROJ_EXT), F32), jax.ShapeDtypeStruct((N_HEADS, T, HEAD_PAD), BF16),
                   jax.ShapeDtypeStruct((N_HEADS, T, HEAD_PAD), BF16), jax.ShapeDtypeStruct((N_HEADS, T, 2 * V_DIM), BF16)],
        compiler_params=_params(dimension_semantics=("arbitrary",)),
    )(x, pos, g_in, w_in, g_cq, w_uq, g_ckv, w_ukv, gq, gk, invf, sgn)


def _chunk_pipeline(n_loop, lag, matmuls, pointwise, accumulate, last):
    slots = lag + 1

    def iteration(t, slot):
        matmuls(jnp.minimum(t + lag, n_loop), (slot + lag) % slots)
        accumulate(jnp.maximum(t - lag, 0), (slot + 1) % slots)
        pointwise(t, slot, False)

    def finish(slot):
        for back in range(lag, 0, -1):
            accumulate(jnp.maximum(n_loop - back, 0), (slot - back) % slots)
        pointwise(n_loop, slot, True)
        accumulate(n_loop, slot)
        last()

    for u in range(lag):
        matmuls(jnp.minimum(u, n_loop), u)

    def unrolled(tt, carry):
        for slot in range(slots):
            iteration(slots * tt + slot, slot)
        return carry

    lax.fori_loop(0, n_loop // slots, unrolled, 0)
    rest = lax.rem(n_loop, slots)
    t0 = n_loop - rest

    for r in range(slots):
        @pl.when(rest == r)
        def _():
            for slot in range(r):
                iteration(t0 + slot, slot)
            finish(r)


def _attn_fwd(q, k, v, tq):
    T = q.shape[1]
    tk = tq
    rc = min(SOFTMAX_ROWS, tq)

    def body(q_ref, k_ref, v_ref, o_ref, lse_ref, s0, s1, s2, p0, p1, p2, a0, a1, a2, m_ref, acc_ref):
        qi = pl.program_id(1)
        s_buf, p_buf, a_buf = (s0, s1, s2), (p0, p1, p2), (a0, a1, a2)

        def scores(t, slot):
            ks = pl.multiple_of(t * tk, tk)
            s_buf[slot][...] = _dot_nt(q_ref[0], k_ref[0, pl.ds(ks, tk), :])

        def values(t, slot):
            ks = pl.multiple_of(t * tk, tk)
            acc_ref[...] = acc_ref[...] * a_buf[slot][...] + _dot(p_buf[slot][...], v_ref[0, pl.ds(ks, tk), :])

        def softmax(t, slot, masked):
            s_all = s_buf[slot][...]
            if masked:
                row = lax.broadcasted_iota(jnp.int32, (tq, tk), 0)
                col = lax.broadcasted_iota(jnp.int32, (tq, tk), 1)
                s_all = jnp.where(col <= row, s_all, NEG)
                s_buf[slot][...] = s_all
            m_old = m_ref[...]
            m_new = jnp.maximum(m_old, jnp.max(s_all, axis=1, keepdims=True))
            a_buf[slot][...] = jnp.exp2((m_old - m_new) * EXP2_SCALE)
            m_ref[...] = m_new
            for r0 in range(0, tq, rc):
                s = s_buf[slot][r0:r0 + rc, :]
                p_buf[slot][r0:r0 + rc, :] = jnp.exp2((s - m_new[r0:r0 + rc, :]) * EXP2_SCALE).astype(BF16)

        def last():
            l = acc_ref[:, V_DIM:2 * V_DIM]
            o_ref[...] = acc_ref[:, 0:V_DIM] / l
            lse_ref[0] = (m_ref[...] * SCALE + jnp.log(l)).T[0:1, :]

        m_ref[...] = jnp.full_like(m_ref, NEG)
        acc_ref[...] = jnp.zeros_like(acc_ref)
        for p_late, a_late in ((p1, a1), (p2, a2)):
            p_late[...] = jnp.zeros_like(p_late)
            a_late[...] = jnp.ones_like(a_late)
        _chunk_pipeline(qi, 2, scores, softmax, values, last)

    return pl.pallas_call(
        body, name="attn_fwd", grid=(N_HEADS, T // tq),
        in_specs=[pl.BlockSpec((1, tq, HEAD_PAD), lambda h, i: (h, i, 0)),
                  pl.BlockSpec((1, T, HEAD_PAD), lambda h, i: (h, 0, 0)),
                  pl.BlockSpec((1, T, 2 * V_DIM), lambda h, i: (h, 0, 0))],
        out_specs=[pl.BlockSpec((tq, V_DIM), lambda h, i: (i, h)),
                   pl.BlockSpec((1, 1, tq), lambda h, i: (h, 0, i))],
        out_shape=[jax.ShapeDtypeStruct((T, ATTN_W), F32), jax.ShapeDtypeStruct((N_HEADS, 1, T), F32)],
        scratch_shapes=[pltpu.VMEM((tq, tk), F32)] * 3 + [pltpu.VMEM((tq, tk), BF16)] * 3
                       + [pltpu.VMEM((tq, 1), F32)] * 4 + [pltpu.VMEM((tq, 2 * V_DIM), F32)],
        compiler_params=_params(dimension_semantics=("arbitrary", "arbitrary")),
    )(q, k, v)


def _tail(x, o, proj, p, tgt, g_oa, g_oc, g_pl, conv_w, w_o, w_pl, w_plg, tm):
    T = x.shape[0]
    nt = T // tm

    def body(x_ref, o_ref, za_ref, cb_ref, cc_ref, cx_ref, zc_ref, cch_ref, cxh_ref, p_ref, tgt_ref,
             g_oa_ref, g_oc_ref, g_pl_ref, cw_ref, w_o_ref, w_pl_ref, w_plg_ref,
             dx1_ref, do_ref, delta_ref, dtail_ref, du_ref,
             dw_o_ref, dw_pl_ref, dw_plg_ref, dg_oa_ref, dg_oc_ref, dg_pl_ref, dcw_ref, loss_ref):
        i = pl.program_id(0)

        @pl.when(i == 0)
        def _():
            for r in (dw_o_ref, dw_pl_ref, dw_plg_ref, dg_oa_ref, dg_oc_ref, dg_pl_ref, dcw_ref, loss_ref):
                r[...] = jnp.zeros_like(r)

        xv, ov, za, cb, zc = x_ref[...], o_ref[...], za_ref[...], cb_ref[...], zc_ref[...]
        g_oa, g_oc, g_pl = g_oa_ref[...], g_oc_ref[...], g_pl_ref[...]
        w0, w1, w2 = cw_ref[0:1, :], cw_ref[1:2, :], cw_ref[2:3, :]

        pb = p_ref[...].astype(BF16)
        pp = _dot(pb, w_pl_ref[...])

        sa = _sigmoid(za)
        silu_a = za * sa
        ga = ov * silu_a
        ra = _inv_rms(ga, ATTN_W)
        xa = ga * ra
        ya = xa * g_oa
        v = cc_ref[...] * cx_ref[...]
        not_first = jnp.where(i > 0, 1.0, 0.0)
        hv6 = cch_ref[6:7, :] * cxh_ref[6:7, :] * not_first
        hv7 = cch_ref[7:8, :] * cxh_ref[7:8, :] * not_first
        row = lax.broadcasted_iota(jnp.int32, v.shape, 0)
        v1 = jnp.where(row == 0, hv7, pltpu.roll(v, 1, 0))
        v2 = jnp.where(row == 0, hv6, jnp.where(row == 1, hv7, pltpu.roll(v, 2, 0)))
        u = w0 * v2 + w1 * v1 + w2 * v
        sc = _sigmoid(zc)
        silu_c = zc * sc
        gc = cb * u * silu_c
        rc = _inv_rms(gc, CONV_W)
        xc = gc * rc
        yc = xc * g_oc
        ycat = jnp.concatenate([ya, yc], axis=-1).astype(BF16)
        x1 = xv + _dot(ycat, w_o_ref[...])
        r1 = _inv_rms(x1, D_MODEL)
        xh1 = x1 * r1
        n1 = (xh1 * g_pl).astype(BF16)
        gate = _sigmoid(_dot(n1, w_plg_ref[...]))
        err = x1 + gate * pp - tgt_ref[...]
        loss_ref[...] += 0.5 * jnp.sum(err * err) / D_MODEL
        dy = err / D_MODEL

        dpp = (dy * gate).astype(BF16)
        da = (dy * pp * gate * (1.0 - gate)).astype(BF16)
        dn1 = _dot_nt(da, w_plg_ref[...])
        dw_pl_ref[...] += _dot_tn(pb, dpp)
        dw_plg_ref[...] += _dot_tn(n1, da)
        dg_pl_ref[...] += _colsum(dn1 * xh1)
        dxh = dn1 * g_pl
        dx1 = dy + r1 * (dxh - xh1 * (jnp.sum(dxh * xh1, axis=-1, keepdims=True) / D_MODEL))
        dx1_ref[...] = dx1
        dx1b = dx1.astype(BF16)
        dycat = _dot_nt(dx1b, w_o_ref[...])
        dya, dyc = dycat[:, 0:ATTN_W], dycat[:, ATTN_W:D_MODEL]

        dw_o_ref[0:ATTN_W, :] += _dot_tn(ycat[:, 0:ATTN_W], dx1b)
        dg_oa_ref[...] += _colsum(dya * xa)
        dxa = dya * g_oa
        dga = ra * (dxa - xa * (jnp.sum(dxa * xa, axis=-1, keepdims=True) / ATTN_W))
        do = (dga * silu_a).astype(BF16)
        do_ref[...] = do
        dof = do.astype(F32) * ov
        for hd in range(N_HEADS):
            delta_ref[hd] = _lane_sum(dof[:, hd * V_DIM:(hd + 1) * V_DIM]).T[0:1, :]
        dtail_ref[:, 0:512] = (dga * ov * (sa * (1.0 + za * (1.0 - sa)))).astype(BF16)

        dw_o_ref[ATTN_W:D_MODEL, :] += _dot_tn(ycat[:, ATTN_W:D_MODEL], dx1b)
        dg_oc_ref[...] += _colsum(dyc * xc)
        dxc = dyc * g_oc
        dgc = rc * (dxc - xc * (jnp.sum(dxc * xc, axis=-1, keepdims=True) / CONV_W))
        dtail_ref[:, 512:1024] = (dgc * u * silu_c).astype(BF16)
        du = dgc * cb * silu_c
        du_ref[...] = du
        dtail_ref[:, 1024:1536] = (dgc * cb * u * (sc * (1.0 + zc * (1.0 - sc)))).astype(BF16)
        dcw_ref[0:1, :] += _colsum(du * v2)
        dcw_ref[1:2, :] += _colsum(du * v1)
        dcw_ref[2:3, :] += _colsum(du * v)

    row = lambda i: (i, 0)
    col = lambda c: (lambda i: (i, c))
    halo = lambda c: (lambda i: (jnp.maximum(i * (tm // 8) - 1, 0), c))
    in_specs = [pl.BlockSpec((tm, D_MODEL), row), pl.BlockSpec((tm, ATTN_W), row)]
    in_specs += [pl.BlockSpec((tm, 512), col(c)) for c in (1, 2, 3, 4, 5)]
    in_specs += [pl.BlockSpec((8, 512), halo(3)), pl.BlockSpec((8, 512), halo(4))]
    in_specs += [pl.BlockSpec((tm, PLE), row), pl.BlockSpec((tm, D_MODEL), row),
                 _full((1, ATTN_W)), _full((1, CONV_W)), _full((1, D_MODEL)), _full((3, CONV_W)),
                 _full((D_MODEL, D_MODEL)), _full((PLE, D_MODEL)), _full((D_MODEL, D_MODEL))]
    out_specs = [pl.BlockSpec((tm, D_MODEL), row), pl.BlockSpec((tm, ATTN_W), row),
                 pl.BlockSpec((N_HEADS, 1, tm), lambda i: (0, 0, i)), pl.BlockSpec((tm, 1536), row),
                 pl.BlockSpec((tm, CONV_W), row),
                 _full((D_MODEL, D_MODEL)), _full((PLE, D_MODEL)), _full((D_MODEL, D_MODEL)),
                 _full((1, ATTN_W)), _full((1, CONV_W)), _full((1, D_MODEL)), _full((3, CONV_W)), _full((1, LANES))]
    out_shape = [jax.ShapeDtypeStruct((T, D_MODEL), F32), jax.ShapeDtypeStruct((T, ATTN_W), BF16),
                 jax.ShapeDtypeStruct((N_HEADS, 1, T), F32), jax.ShapeDtypeStruct((T, 1536), BF16),
                 jax.ShapeDtypeStruct((T, CONV_W), F32),
                 jax.ShapeDtypeStruct((D_MODEL, D_MODEL), F32), jax.ShapeDtypeStruct((PLE, D_MODEL), F32),
                 jax.ShapeDtypeStruct((D_MODEL, D_MODEL), F32),
                 jax.ShapeDtypeStruct((1, ATTN_W), F32), jax.ShapeDtypeStruct((1, CONV_W), F32),
                 jax.ShapeDtypeStruct((1, D_MODEL), F32), jax.ShapeDtypeStruct((3, CONV_W), F32),
                 jax.ShapeDtypeStruct((1, LANES), F32)]
    return pl.pallas_call(
        body, name="tail", grid=(nt,), in_specs=in_specs, out_specs=out_specs, out_shape=out_shape,
        compiler_params=_params(dimension_semantics=("arbitrary",)),
    )(x, o, proj, proj, proj, proj, proj, proj, proj, p, tgt, g_oa, g_oc, g_pl, conv_w, w_o, w_pl, w_plg)


def _attn_bwd(q, k, v, do, lse_row, delta_row, tk):
    T = q.shape[1]
    tq = tk
    nq = T // tq
    rc = min(SOFTMAX_ROWS, tk)

    def body(q_ref, k_ref, v_ref, do_ref, lse_ref, dl_ref, dq_ref, dk_ref, dv_ref,
             s0, s1, d0, d1, p0, p1, g0, g1, dk_acc, dv_acc):
        kj = pl.program_id(1)
        s_buf, dp_buf, p_buf, g_buf = (s0, s1), (d0, d1), (p0, p1), (g0, g1)

        @pl.when(kj == 0)
        def _():
            dq_ref[...] = jnp.zeros_like(dq_ref)

        def q_start(t):
            return pl.multiple_of((nq - 1 - t) * tq, tq)

        def matmuls(t, slot):
            qs = q_start(t)
            s_buf[slot][...] = _dot_nt(k_ref[0], q_ref[0, pl.ds(qs, tq), :])
            dp_buf[slot][...] = _dot_nt(v_ref[0], do_ref[pl.ds(qs, tq), :])

        def pointwise(t, slot, masked):
            qs = q_start(t)
            lse2 = lse_ref[0, :, pl.ds(qs, tq)] * LOG2E
            dl = dl_ref[0, :, pl.ds(qs, tq)]
            for r0 in range(0, tk, rc):
                st = s_buf[slot][r0:r0 + rc, :]
                if masked:
                    row = lax.broadcasted_iota(jnp.int32, (rc, tq), 0)
                    col = lax.broadcasted_iota(jnp.int32, (rc, tq), 1)
                    st = jnp.where(row + r0 <= col, st, NEG)
                pt = jnp.exp2(st * EXP2_SCALE - lse2)
                p_buf[slot][r0:r0 + rc, :] = pt.astype(BF16)
                g_buf[slot][r0:r0 + rc, :] = (pt * (dp_buf[slot][r0:r0 + rc, :] - dl) * SCALE).astype(BF16)

        def accumulate(t, slot):
            qs = q_start(t)
            dv_acc[...] += _dot(p_buf[slot][...], do_ref[pl.ds(qs, tq), :])
            dk_acc[...] += _dot(g_buf[slot][...], q_ref[0, pl.ds(qs, tq), :])
            dq_ref[0, pl.ds(qs, tq), :] += _dot_tn(g_buf[slot][...], k_ref[0])

        def last():
            dk_ref[0] = dk_acc[...]
            dv_ref[0] = dv_acc[...]

        dk_acc[...] = jnp.zeros_like(dk_acc)
        dv_acc[...] = jnp.zeros_like(dv_acc)
        for late in (p1, g1):
            late[...] = jnp.zeros_like(late)
        _chunk_pipeline(nq - 1 - kj, 1, matmuls, pointwise, accumulate, last)

    return pl.pallas_call(
        body, name="attn_bwd", grid=(N_HEADS, T // tk),
        in_specs=[pl.BlockSpec((1, T, HEAD_PAD), lambda h, j: (h, 0, 0)),
                  pl.BlockSpec((1, tk, HEAD_PAD), lambda h, j: (h, j, 0)),
                  pl.BlockSpec((1, tk, V_DIM), lambda h, j: (h, j, 0)),
                  pl.BlockSpec((T, V_DIM), lambda h, j: (0, h)),
                  pl.BlockSpec((1, 1, T), lambda h, j: (h, 0, 0)),
                  pl.BlockSpec((1, 1, T), lambda h, j: (h, 0, 0))],
        out_specs=[pl.BlockSpec((1, T, HEAD_PAD), lambda h, j: (h, 0, 0)),
                   pl.BlockSpec((1, tk, HEAD_PAD), lambda h, j: (h, j, 0)),
                   pl.BlockSpec((1, tk, V_DIM), lambda h, j: (h, j, 0))],
        out_shape=[jax.ShapeDtypeStruct((N_HEADS, T, HEAD_PAD), F32), jax.ShapeDtypeStruct((N_HEADS, T, HEAD_PAD), F32),
                   jax.ShapeDtypeStruct((N_HEADS, T, V_DIM), F32)],
        scratch_shapes=[pltpu.VMEM((tk, tq), F32)] * 4 + [pltpu.VMEM((tk, tq), BF16)] * 4
                       + [pltpu.VMEM((tk, HEAD_PAD), F32), pltpu.VMEM((tk, V_DIM), F32)],
        compiler_params=_params(dimension_semantics=("arbitrary", "arbitrary")),
    )(q, k, v, do, lse_row, delta_row)


def _bwd_proj(x, dx1, pos, proj, dq, dk, dv, dtail, du, g_in, w_in, g_cq, w_uq, g_ckv, w_ukv, gq, gk, conv_w,
              invf, sgn, tm):
    T = x.shape[0]
    nt = T // tm

    ts = min(SUB_TILE, tm)

    def body(x_ref, dx1_ref, pos_ref, lat_ref, cc_ref, cx_ref, dq_ref, dk_ref, dv_ref, dtail_ref, du_ref, dun_ref, *rest):
        consts, (gx_ref, h_ref, dproj_ref), sums = rest[:11], rest[11:14], rest[14:]
        cw_ref = consts[8]
        i = pl.program_id(0)

        @pl.when(i == 0)
        def _():
            for r in sums:
                r[...] = jnp.zeros_like(r)

        du_v = du_ref[...]
        not_last = jnp.where(i < nt - 1, 1.0, 0.0)
        nx0 = dun_ref[0:1, :] * not_last
        nx1 = dun_ref[1:2, :] * not_last
        row = lax.broadcasted_iota(jnp.int32, du_v.shape, 0)
        du1 = jnp.where(row == tm - 1, nx0, pltpu.roll(du_v, tm - 1, 0))
        du2 = jnp.where(row == tm - 2, nx0, jnp.where(row == tm - 1, nx1, pltpu.roll(du_v, tm - 2, 0)))
        dvc = cw_ref[2:3, :] * du_v + cw_ref[1:2, :] * du1 + cw_ref[0:1, :] * du2
        dproj_ref[:, 1536:2048] = (dvc * cx_ref[...]).astype(BF16)
        dproj_ref[:, 2048:2560] = (dvc * cc_ref[...]).astype(BF16)

        for r0 in range(0, tm, ts):
            rows = slice(r0, r0 + ts)
            work(x_ref.at[rows, :], dx1_ref.at[rows, :], pos_ref.at[rows, :], lat_ref.at[rows, :],
                 dq_ref.at[:, rows, :], dk_ref.at[:, rows, :], dv_ref.at[:, rows, :], dtail_ref.at[rows, :], *consts,
                 gx_ref.at[rows, :], h_ref.at[:, rows], dproj_ref.at[rows, :], *sums)

    def work(x_ref, dx1_ref, pos_ref, lat_ref, dq_ref, dk_ref, dv_ref, dtail_ref,
             g_in_ref, w_in_ref, g_cq_ref, w_uq_ref, g_ckv_ref, w_ukv_ref, gq_ref, gk_ref, cw_ref, invf_ref, sgn_ref,
             gx_ref, h_ref, dproj_ref, dw_uq_ref, dw_ukv_ref, dg_in_ref, dg_cq_ref, dg_ckv_ref, dgq_ref, dgk_ref):
        xv = x_ref[...]
        r0 = _rep(_inv_rms_mxu(xv), D_MODEL)
        xh0 = xv * r0
        g_in = g_in_ref[...]
        h_ref[...] = (xh0 * g_in).astype(BF16).T

        c_q = lat_ref[:, 0:Q_LORA]
        rq = _rep(_inv_rms_mxu(c_q), Q_LORA)
        xq = c_q * rq
        g_cq = g_cq_ref[...]
        cqn = (xq * g_cq).astype(BF16)
        c_kv = lat_ref[:, Q_LORA:Q_LORA + KV_LORA]
        rkv = _inv_rms_mxu(c_kv)
        xkv = c_kv * rkv
        g_ckv = g_ckv_ref[...]
        ckvn = (xkv * g_ckv).astype(BF16)
        kpe = lat_ref[:, 384:512]
        kpe_sq = kpe * kpe
        cos_b, sin_b = _rope_tables(pos_ref, invf_ref, sgn_ref)
        gq_a, gq_b = gq_ref[:, 0:NOPE], gq_ref[:, NOPE:HEAD_PAD]
        gk_a, gk_b = gk_ref[:, 0:NOPE], gk_ref[:, NOPE:HEAD_PAD]

        dproj_ref[:, 512:1536] = dtail_ref[:, 0:1024]
        dproj_ref[:, 2560:3072] = dtail_ref[:, 1024:1536]

        def dh_part(c0):
            return _dot_nt(dproj_ref[:, c0:c0 + 512], w_in_ref[:, c0:c0 + 512])

        later_chunks = ((512,), (1024,), (1536, 2048), (2560,))
        dh = jnp.zeros((ts, D_MODEL), F32)
        dkpe = jnp.zeros((ts, LANES), F32)
        dcqn = jnp.zeros((ts, Q_LORA), F32)
        dckvn = jnp.zeros((ts, KV_LORA), F32)
        for hd in range(N_HEADS):
            for chunk in later_chunks[hd]:
                dh = dh + dh_part(chunk)
            c0 = hd * HEAD_PAD
            qh = _dot(cqn, w_uq_ref[:, c0:c0 + HEAD_PAD])
            a, b = qh[:, 0:NOPE], qh[:, NOPE:HEAD_PAD]
            r = lax.rsqrt(_lane_sum(a * a + b * b) / QK_DIM + EPS)
            xa, xb = a * r, b * r
            dan = dq_ref[hd, :, 0:NOPE]
            dbr = dq_ref[hd, :, NOPE:HEAD_PAD]
            dbn = dbr * cos_b + _swap_rope_halves(dbr * sin_b)
            dgq_ref[:, 0:NOPE] += _colsum(dan * xa)
            dgq_ref[:, NOPE:HEAD_PAD] += _colsum(dbn * xb)
            dxa, dxb = dan * gq_a, dbn * gq_b
            cq = _lane_sum(dxa * xa + dxb * xb) / QK_DIM
            dqh = jnp.concatenate([r * (dxa - xa * cq), r * (dxb - xb * cq)], axis=-1).astype(BF16)
            dw_uq_ref[:, c0:c0 + HEAD_PAD] += _dot_tn(cqn, dqh)
            dcqn = dcqn + _dot_nt(dqh, w_uq_ref[:, c0:c0 + HEAD_PAD])
            kvh = _dot(ckvn, w_ukv_ref[:, c0:c0 + HEAD_PAD])
            ka = kvh[:, 0:NOPE]
            rk = lax.rsqrt(_lane_sum(ka * ka + kpe_sq) / QK_DIM + EPS)
            xka, xkb = ka * rk, kpe * rk
            dkan = dk_ref[hd, :, 0:NOPE]
            dkbr = dk_ref[hd, :, NOPE:HEAD_PAD]
            dkbn = dkbr * cos_b + _swap_rope_halves(dkbr * sin_b)
            dgk_ref[:, 0:NOPE] += _colsum(dkan * xka)
            dgk_ref[:, NOPE:HEAD_PAD] += _colsum(dkbn * xkb)
            dxka, dxkb = dkan * gk_a, dkbn * gk_b
            ck = _lane_sum(dxka * xka + dxkb * xkb) / QK_DIM
            dkpe = dkpe + rk * (dxkb - xkb * ck)
            dkvh = jnp.concatenate([rk * (dxka - xka * ck), dv_ref[hd]], axis=-1).astype(BF16)
            dw_ukv_ref[:, c0:c0 + HEAD_PAD] += _dot_tn(ckvn, dkvh)
            dckvn = dckvn + _dot_nt(dkvh, w_ukv_ref[:, c0:c0 + HEAD_PAD])

        dg_cq_ref[...] += _colsum(dcqn * xq)
        dxq = dcqn * g_cq
        dproj_ref[:, 0:Q_LORA] = (rq * (dxq - xq * _rep(_lane_sum(dxq * xq) / Q_LORA, Q_LORA))).astype(BF16)
        dg_ckv_ref[...] += _colsum(dckvn * xkv)
        dxkv = dckvn * g_ckv
        dproj_ref[:, 256:384] = (rkv * (dxkv - xkv * (_lane_sum(dxkv * xkv) / KV_LORA))).astype(BF16)
        dproj_ref[:, 384:512] = dkpe.astype(BF16)
        dh = dh + dh_part(0)
        dg_in_ref[...] += _colsum(dh * xh0)
        dxh = dh * g_in
        gx_ref[...] = dx1_ref[...] + r0 * (dxh - xh0 * _rep(_lane_sum(dxh * xh0) / D_MODEL, D_MODEL))

    row = lambda i: (i, 0)
    col = lambda c: (lambda i: (i, c))
    head_rows = lambda i: (0, i, 0)
    nxt = lambda i: (jnp.minimum((i + 1) * (tm // 8), T // 8 - 1), 0)
    in_specs = [pl.BlockSpec((tm, D_MODEL), row), pl.BlockSpec((tm, D_MODEL), row), pl.BlockSpec((tm, 1), row),
                pl.BlockSpec((tm, 512), col(0)), pl.BlockSpec((tm, 512), col(3)), pl.BlockSpec((tm, 512), col(4)),
                pl.BlockSpec((N_HEADS, tm, HEAD_PAD), head_rows), pl.BlockSpec((N_HEADS, tm, HEAD_PAD), head_rows),
                pl.BlockSpec((N_HEADS, tm, V_DIM), head_rows), pl.BlockSpec((tm, 1536), row),
                pl.BlockSpec((tm, CONV_W), row), pl.BlockSpec((8, CONV_W), nxt),
                _full((1, D_MODEL)), _full((D_MODEL, PROJ_EXT)), _full((1, Q_LORA)), _full((Q_LORA, N_HEADS * HEAD_PAD)),
                _full((1, KV_LORA)), _full((KV_LORA, N_HEADS * HEAD_PAD)), _full((1, HEAD_PAD)), _full((1, HEAD_PAD)),
                _full((3, CONV_W)), _full((1, LANES)), _full((1, LANES))]
    out_specs = [pl.BlockSpec((tm, D_MODEL), row), pl.BlockSpec((D_MODEL, tm), lambda i: (0, i)),
                 pl.BlockSpec((tm, PROJ_EXT), row),
                 _full((Q_LORA, N_HEADS * HEAD_PAD)), _full((KV_LORA, N_HEADS * HEAD_PAD)),
                 _full((1, D_MODEL)), _full((1, Q_LORA)), _full((1, KV_LORA)), _full((1, HEAD_PAD)), _full((1, HEAD_PAD))]
    out_shape = [jax.ShapeDtypeStruct((T, D_MODEL), F32), jax.ShapeDtypeStruct((D_MODEL, T), BF16),
                 jax.ShapeDtypeStruct((T, PROJ_EXT), BF16),
                 jax.ShapeDtypeStruct((Q_LORA, N_HEADS * HEAD_PAD), F32), jax.ShapeDtypeStruct((KV_LORA, N_HEADS * HEAD_PAD), F32),
                 jax.ShapeDtypeStruct((1, D_MODEL), F32), jax.ShapeDtypeStruct((1, Q_LORA), F32),
                 jax.ShapeDtypeStruct((1, KV_LORA), F32), jax.ShapeDtypeStruct((1, HEAD_PAD), F32),
                 jax.ShapeDtypeStruct((1, HEAD_PAD), F32)]
    return pl.pallas_call(
        body, name="bwd_proj", grid=(nt,), in_specs=in_specs, out_specs=out_specs, out_shape=out_shape,
        compiler_params=_params(dimension_semantics=("arbitrary",)),
    )(x, dx1, pos, proj, proj, proj, dq, dk, dv, dtail, du, du, g_in, w_in, g_cq, w_uq, g_ckv, w_ukv, gq, gk, conv_w,
      invf, sgn)


def _matmul_acc(a, b, tt, tn):
    M, T = a.shape
    N = b.shape[1]

    def body(a_ref, b_ref, o_ref):
        @pl.when(pl.program_id(1) == 0)
        def _():
            o_ref[...] = jnp.zeros_like(o_ref)

        o_ref[...] += _dot(a_ref[...], b_ref[...])

    return pl.pallas_call(
        body, name="dw_in", grid=(N // tn, T // tt),
        in_specs=[pl.BlockSpec((M, tt), lambda j, t: (0, t)), pl.BlockSpec((tt, tn), lambda j, t: (t, j))],
        out_specs=pl.BlockSpec((M, tn), lambda j, t: (0, j)),
        out_shape=jax.ShapeDtypeStruct((M, N), F32),
        compiler_params=_params(dimension_semantics=("arbitrary", "arbitrary")),
    )(a, b)


def _add_pair(grads, from_sibling, small, small_sibling, c):
    n = len(grads)

    def body(c_ref, *refs):
        ins, outs = refs[:2 * n + 2], refs[2 * n + 2:]
        for i in range(n + 1):
            outs[i][...] = (ins[2 * i][...] + ins[2 * i + 1][...]).astype(outs[i].dtype)

    in_specs, out_specs, out_shape, args = [], [], [], []
    for g, r in zip(grads, from_sibling):
        _, hr, cols = r.shape
        in_specs += [pl.BlockSpec((1, hr, cols), lambda k, c_ref: (k, c_ref[0], 0)),
                     pl.BlockSpec((1, hr, cols), lambda k, c_ref: (k, 0, 0))]
        out_specs.append(pl.BlockSpec((1, hr, cols), lambda k, c_ref: (k, 0, 0)))
        out_shape.append(jax.ShapeDtypeStruct(r.shape, BF16))
        args += [g, r]
    whole = pl.BlockSpec(small.shape, lambda k, c_ref: (0, 0))
    in_specs += [whole, whole]
    out_specs.append(whole)
    out_shape.append(jax.ShapeDtypeStruct(small.shape, F32))
    outs = pl.pallas_call(
        body, name="add_pair", out_shape=out_shape,
        grid_spec=pltpu.PrefetchScalarGridSpec(num_scalar_prefetch=1, grid=(N_CHIPS,), in_specs=in_specs,
                                               out_specs=out_specs),
        compiler_params=_params(dimension_semantics=("arbitrary",)),
    )(c.reshape(1), *args, small, small_sibling)
    return outs[:n], outs[n]


def _add_chips(parts, small_parts):
    arrays = list(parts) + [small_parts]

    def body(*refs):
        ins, outs = refs[:len(arrays)], refs[len(arrays):]
        for a_ref, o_ref in zip(ins, outs):
            part = lambda k: a_ref[k].astype(F32)
            o_ref[...] = ((part(0) + part(1)) + part(2)) + part(3)

    in_specs, out_specs, out_shape = [], [], []
    for a in arrays:
        _, rows, cols = a.shape
        in_specs.append(pl.BlockSpec((N_CHIPS, rows // 2, cols), lambda i: (0, i, 0)))
        out_specs.append(pl.BlockSpec((rows // 2, cols), lambda i: (i, 0)))
        out_shape.append(jax.ShapeDtypeStruct((rows, cols), F32))
    outs = pl.pallas_call(body, name="add_chips", grid=(2,), in_specs=in_specs, out_specs=out_specs,
                          out_shape=out_shape, compiler_params=_params(dimension_semantics=("arbitrary",)))(*arrays)
    return outs[:-1], outs[-1]


def _adamw(w, g, m, v, name):
    rows, cols = w.shape
    rb = 256 if rows * cols > 512 * 1024 else rows

    def body(w_ref, g_ref, m_ref, v_ref, d_ref, nm_ref, nv_ref):
        _adamw_math(g_ref[...], w_ref, m_ref, v_ref, d_ref, nm_ref, nv_ref)

    spec = pl.BlockSpec((rb, cols), lambda i: (i, 0))
    shp = jax.ShapeDtypeStruct(w.shape, F32)
    return pl.pallas_call(body, name=name, grid=(rows // rb,), in_specs=[spec] * 4, out_specs=[spec] * 3,
                          out_shape=[shp] * 3)(w, g, m, v)


def _adamw_math(gv, w_ref, m_ref, v_ref, d_ref, nm_ref, nv_ref):
    nm = B1 * m_ref[...] + (1.0 - B1) * gv
    nv = B2 * v_ref[...] + (1.0 - B2) * (gv * gv)
    m_hat = nm / (1.0 - B1 ** STEP)
    v_hat = nv / (1.0 - B2 ** STEP)
    d_ref[...] = -LR * (m_hat / (jnp.sqrt(v_hat) + ADAM_EPS) + WD * w_ref[...])
    nm_ref[...] = nm
    nv_ref[...] = nv


def _adamw_halves(w, mine, other, m, v, c, name):
    hr, cols = mine.shape

    def body(c_ref, w_ref, mine_ref, other_ref, m_ref, v_ref, g_ref, d_ref, nm_ref, nv_ref):
        gv = jnp.where(pl.program_id(0) == c_ref[0], mine_ref[...], other_ref[...])
        g_ref[...] = gv
        _adamw_math(gv, w_ref, m_ref, v_ref, d_ref, nm_ref, nv_ref)

    half = pl.BlockSpec((hr, cols), lambda i, c_ref: (i, 0))
    whole = pl.BlockSpec((hr, cols), lambda i, c_ref: (0, 0))
    shp = jax.ShapeDtypeStruct(w.shape, F32)
    return pl.pallas_call(
        body, name=name, out_shape=[shp] * 4,
        grid_spec=pltpu.PrefetchScalarGridSpec(num_scalar_prefetch=1, grid=(2,), in_specs=[half, whole, whole, half, half],
                                               out_specs=[half] * 4),
        compiler_params=_params(dimension_semantics=("arbitrary",)),
    )(c.reshape(1), w, mine, other, m, v)


_ANY = pl.BlockSpec(memory_space=pl.ANY)


def _mesh_pos():
    return lax.axis_index("x"), lax.axis_index("y"), lax.axis_index("c")


def _other_chips(x, y):
    return [(1 - x, y), (x, 1 - y), (1 - x, 1 - y)]


def _remote(src, dst, send_sems, recv_sems, k, to):
    return pltpu.make_async_remote_copy(src_ref=src, dst_ref=dst, send_sem=send_sems.at[k], recv_sem=recv_sems.at[k],
                                        device_id=to, device_id_type=MESH)


def _gather_weights(shards):
    n = len(shards)
    halved = [s.shape[0] % 32 == 0 for s in shards]

    def body(*refs):
        ins, outs, stage = refs[:n], refs[n:2 * n], refs[2 * n:3 * n]
        send_sems, recv_sems, local_sems = refs[3 * n:]
        x, y, c = _mesh_pos()
        me = 2 * x + y
        chips = _other_chips(x, y)

        def part(i, ref, hc):
            if not halved[i]:
                return ref
            hr = shards[i].shape[0] // 2
            return ref.at[pl.ds(hc * hr, hr), :]

        locals_, started = [], []
        for i in range(n):
            stage[i][...] = ins[i][...].astype(BF16)
            mine = pltpu.make_async_copy(stage[i], outs[i].at[me], local_sems.at[i])
            mine.start()
            locals_.append(mine)
            for j, (cx, cy) in enumerate(chips):
                cp = _remote(part(i, stage[i], c), part(i, outs[i].at[me], c), send_sems, recv_sems, 6 * i + j, (cx, cy, c))
                cp.start()
                started.append(cp)
        for i in range(n):
            for j, (cx, cy) in enumerate(chips):
                got = part(i, outs[i].at[2 * cx + cy], c)
                _remote(got, got, send_sems, recv_sems, 6 * i + j, (cx, cy, c)).wait_recv()
                if halved[i]:
                    fwd = _remote(got, got, send_sems, recv_sems, 6 * i + 3 + j, (x, y, 1 - c))
                    fwd.start()
                    started.append(fwd)
        for i in range(n):
            if halved[i]:
                for j, (cx, cy) in enumerate(chips):
                    got = part(i, outs[i].at[2 * cx + cy], 1 - c)
                    _remote(got, got, send_sems, recv_sems, 6 * i + 3 + j, (x, y, 1 - c)).wait_recv()
        for cp in started:
            cp.wait_send()
        for cp in locals_:
            cp.wait()

    vmem = pl.BlockSpec(memory_space=pltpu.VMEM)
    return pl.pallas_call(
        body, name="gather_weights", in_specs=[vmem] * n, out_specs=[_ANY] * n,
        out_shape=[jax.ShapeDtypeStruct((N_CHIPS,) + s.shape, BF16) for s in shards],
        scratch_shapes=[pltpu.VMEM(s.shape, BF16) for s in shards]
                       + [pltpu.SemaphoreType.DMA((6 * n,)), pltpu.SemaphoreType.DMA((6 * n,)), pltpu.SemaphoreType.DMA((n,))],
        compiler_params=_params(),
    )(*shards)


def _swap_halves(grads, small):
    n = len(grads)
    arrays = list(grads) + [small]

    def body(*refs):
        ins, outs, send_sems, recv_sems = refs[:n + 1], refs[n + 1:2 * n + 2], refs[2 * n + 2], refs[2 * n + 3]
        x, y, c = _mesh_pos()
        cps = []
        for i in range(n + 1):
            src = ins[i]
            if i < n:
                hr = grads[i].shape[1] // 2
                src = src.at[:, pl.ds((1 - c) * hr, hr), :]
            cp = _remote(src, outs[i], send_sems, recv_sems, i, (x, y, 1 - c))
            cp.start()
            cps.append(cp)
        for cp in cps:
            cp.wait()

    out_shape = [jax.ShapeDtypeStruct((g.shape[0], g.shape[1] // 2, g.shape[2]), F32) for g in grads]
    out_shape.append(jax.ShapeDtypeStruct(small.shape, F32))
    outs = pl.pallas_call(
        body, name="pair_grads", in_specs=[_ANY] * (n + 1), out_specs=[_ANY] * (n + 1), out_shape=out_shape,
        scratch_shapes=[pltpu.SemaphoreType.DMA((n + 1,)), pltpu.SemaphoreType.DMA((n + 1,))],
    )(*arrays)
    return outs[:n], outs[n]


def _scatter_to_chips(parts, small):
    n = len(parts)
    arrays = list(parts) + [small]

    def body(*refs):
        ins, outs = refs[:n + 1], refs[n + 1:2 * n + 2]
        send_sems, recv_sems, local_sems = refs[2 * n + 2:]
        x, y, c = _mesh_pos()
        me = 2 * x + y
        chips = _other_chips(x, y)
        locals_, sends = [], []
        for i in range(n + 1):
            mine = pltpu.make_async_copy(ins[i].at[me] if i < n else ins[i], outs[i].at[me], local_sems.at[i])
            mine.start()
            locals_.append(mine)
            for j, (cx, cy) in enumerate(chips):
                src = ins[i].at[2 * cx + cy] if i < n else ins[i]
                cp = _remote(src, outs[i].at[me], send_sems, recv_sems, 3 * i + j, (cx, cy, c))
                cp.start()
                sends.append(cp)
        for i in range(n + 1):
            for j, (cx, cy) in enumerate(chips):
                got = outs[i].at[2 * cx + cy]
                _remote(got, got, send_sems, recv_sems, 3 * i + j, (cx, cy, c)).wait_recv()
        for cp in sends:
            cp.wait_send()
        for cp in locals_:
            cp.wait()

    out_shape = [jax.ShapeDtypeStruct(p.shape, p.dtype) for p in parts]
    out_shape.append(jax.ShapeDtypeStruct((N_CHIPS,) + small.shape, small.dtype))
    outs = pl.pallas_call(
        body, name="scatter_grads", in_specs=[_ANY] * (n + 1), out_specs=[_ANY] * (n + 1), out_shape=out_shape,
        scratch_shapes=[pltpu.SemaphoreType.DMA((3 * n + 3,)), pltpu.SemaphoreType.DMA((3 * n + 3,)),
                        pltpu.SemaphoreType.DMA((n + 1,))],
    )(*arrays)
    return outs[:n], outs[n]


def _share_halves(halves):
    n = len(halves)

    def body(*refs):
        ins, outs, send_sems, recv_sems = refs[:n], refs[n:2 * n], refs[2 * n], refs[2 * n + 1]
        x, y, c = _mesh_pos()
        cps = [_remote(ins[i], outs[i], send_sems, recv_sems, i, (x, y, 1 - c)) for i in range(n)]
        for cp in cps:
            cp.start()
        for cp in cps:
            cp.wait()

    return pl.pallas_call(
        body, name="share_halves", in_specs=[_ANY] * n, out_specs=[_ANY] * n,
        out_shape=[jax.ShapeDtypeStruct(h.shape, h.dtype) for h in halves],
        scratch_shapes=[pltpu.SemaphoreType.DMA((n,)), pltpu.SemaphoreType.DMA((n,))],
    )(*halves)


SHARD_COLS_IN = IN_TOTAL // N_CHIPS
KPE_END = Q_LORA + KV_LORA + ROPE


def _assemble_weights(c_in, c_uq, c_ukv, c_o, c_pl, c_plg, c_conv):
    by_cols = lambda a: a.transpose(1, 0, 2).reshape(a.shape[1], N_CHIPS * a.shape[2])
    w_in_e = jnp.concatenate([c_in[0][:, :KPE_END], jnp.zeros((D_MODEL, 64), BF16), c_in[0][:, KPE_END:],
                              c_in[1], c_in[2], c_in[3]], axis=1)
    w_uq_e = by_cols(jnp.pad(c_uq, ((0, 0), (0, 0), (0, HEAD_PAD - QK_DIM))))
    return (w_in_e, w_uq_e, by_cols(c_ukv), by_cols(c_conv).astype(F32), c_o.reshape(D_MODEL, D_MODEL),
            by_cols(c_pl), c_plg.reshape(D_MODEL, D_MODEL))


def _split_grads(dw_in_e, dw_uq_e, dw_ukv, dw_o, dw_pl, dw_plg):
    chip_major = lambda a: a.reshape(a.shape[0], N_CHIPS, a.shape[1] // N_CHIPS).transpose(1, 0, 2)
    first = jnp.concatenate([dw_in_e[:, :KPE_END], dw_in_e[:, KPE_END + 64:SHARD_COLS_IN + 64]], axis=1)
    rest = [dw_in_e[:, SHARD_COLS_IN * k + 64:SHARD_COLS_IN * (k + 1) + 64] for k in range(1, N_CHIPS)]
    return [jnp.stack([first] + rest), chip_major(dw_uq_e)[:, :, :QK_DIM], chip_major(dw_ukv),
            dw_o.reshape(N_CHIPS, D_MODEL // N_CHIPS, D_MODEL), chip_major(dw_pl),
            dw_plg.reshape(N_CHIPS, D_MODEL // N_CHIPS, D_MODEL)]


def _local_step(x, p, pos, tgt, gains, w_in_e, w_uq_e, w_ukv, conv_w, w_o, w_pl, w_plg, tm, tq):
    g_in, g_cq, g_ckv, g_q, g_k, g_oa, g_oc, g_pl = gains
    T = x.shape[0]
    zpad = lambda a, n: jnp.concatenate([a, jnp.zeros(a.shape[:-1] + (n,), a.dtype)], axis=-1)
    gq, gk = zpad(g_q, HEAD_PAD - QK_DIM), zpad(g_k, HEAD_PAD - QK_DIM)
    inv_freq = 1.0 / (ROPE_THETA ** (jnp.arange(0, ROPE, 2, dtype=F32) / ROPE))
    invf = jnp.concatenate([inv_freq, inv_freq, jnp.zeros((64,), F32)]).reshape(1, LANES)
    sgn = jnp.concatenate([-jnp.ones((32,), F32), jnp.ones((32,), F32), jnp.zeros((64,), F32)]).reshape(1, LANES)

    proj, q, k, v = _fwd_proj(x, pos, g_in, w_in_e, g_cq, w_uq_e, g_ckv, w_ukv, gq, gk, invf, sgn, min(2 * tm, T))
    o, lse = _attn_fwd(q, k, v, tq)
    (dx1, do, delta, dtail, du, dw_o, dw_pl, dw_plg, dg_oa, dg_oc, dg_pl, dconv, loss) = _tail(
        x, o, proj, p, tgt, g_oa, g_oc, g_pl, conv_w, w_o, w_pl, w_plg, tm)
    dq, dk, dv = _attn_bwd(q, k, v, do, lse, delta, tq)
    (gx, h, dproj, dw_uq_e, dw_ukv, dg_in, dg_cq, dg_ckv, dgq, dgk) = _bwd_proj(
        x, dx1, pos, proj, dq, dk, dv, dtail, du, g_in, w_in_e, g_cq, w_uq_e, g_ckv, w_ukv, gq, gk, conv_w, invf, sgn,
        min(2 * tm, T))
    dw_in_e = _matmul_acc(h, dproj, min(4096, T), 512)
    wgrads = (dw_in_e, dw_uq_e, dw_ukv, dw_o, dw_pl, dw_plg)
    ggrads = (dg_in, dg_cq, dg_ckv, dgq, dgk, dg_oa, dg_oc, dg_pl)
    return loss, gx, wgrads, ggrads, dconv


def kernel(x, p, positions, g_in, w_in, g_cq, w_uq, g_ckv, w_ukv, g_q, g_k, conv_w, g_oa, g_oc, w_o, w_pl, w_plg, g_pl, loss_target, m_g_in, m_w_in, m_g_cq, m_w_uq, m_g_ckv, m_w_ukv, m_g_q, m_g_k, m_conv_w, m_g_oa, m_g_oc, m_w_o, m_w_pl, m_w_plg, m_g_pl, v_g_in, v_w_in, v_g_cq, v_w_uq, v_g_ckv, v_w_ukv, v_g_q, v_g_k, v_conv_w, v_g_oa, v_g_oc, v_w_o, v_w_pl, v_w_plg, v_g_pl):
    T = x.shape[1]
    c = lax.axis_index("c")
    chip = 2 * lax.axis_index("x") + lax.axis_index("y")
    gains = [g.reshape(1, -1) for g in (g_in, g_cq, g_ckv, g_q, g_k, g_oa, g_oc, g_pl)]

    gathered = _gather_weights([w_in[0], w_uq[0], w_ukv[0], w_o[0], w_pl[0], w_plg[0], conv_w[0]])
    full = _assemble_weights(*gathered)

    loss, gx, wgrads, ggrads, dconv = _local_step(
        x[0], p[0, 0], positions.reshape(T, 1), loss_target[0], gains, *full, 256, 512)

    grads_cm = _split_grads(*wgrads)
    small_parts = [a.reshape(-1, LANES) for a in (*ggrads, loss, dconv)]
    small_rows = [a.shape[0] for a in small_parts]
    tile_rows = [-(-r // 8) * 8 for r in small_rows]
    tile_rows[-1] += -sum(tile_rows) % 16
    small = jnp.concatenate([jnp.pad(a, ((0, t - r), (0, 0))) for a, r, t in zip(small_parts, small_rows, tile_rows)])
    from_sibling, small_sibling = _swap_halves(grads_cm, small)
    chip_parts, chip_small = _add_pair(grads_cm, from_sibling, small, small_sibling, c)
    by_chip, small_by_chip = _scatter_to_chips(chip_parts, chip_small)
    halves, small_total = _add_chips(by_chip, small_by_chip)
    other_halves = _share_halves(halves)

    gg, off = [], 0
    for rows, tiled in zip(small_rows, tile_rows):
        gg.append(small_total[off:off + rows].reshape(1, -1))
        off += tiled
    loss_out = gg[8][0, 0]
    conv_total = gg[9].reshape(3, CONV_W)
    conv_g = lax.dynamic_slice(conv_total, (0, chip * (CONV_W // N_CHIPS)), (3, CONV_W // N_CHIPS))
    g_by_name = dict(g_in=gg[0], g_cq=gg[1], g_ckv=gg[2], g_q=gg[3][:, :QK_DIM], g_k=gg[4][:, :QK_DIM], conv_w=conv_g,
                     g_oa=gg[5], g_oc=gg[6], g_pl=gg[7])
    half_by_name = dict(zip(("w_in", "w_uq", "w_ukv", "w_o", "w_pl", "w_plg"), zip(halves, other_halves)))
    weights = dict(g_in=g_in, w_in=w_in, g_cq=g_cq, w_uq=w_uq, g_ckv=g_ckv, w_ukv=w_ukv, g_q=g_q, g_k=g_k,
                   conv_w=conv_w, g_oa=g_oa, g_oc=g_oc, w_o=w_o, w_pl=w_pl, w_plg=w_plg, g_pl=g_pl)
    ms = dict(g_in=m_g_in, w_in=m_w_in, g_cq=m_g_cq, w_uq=m_w_uq, g_ckv=m_g_ckv, w_ukv=m_w_ukv, g_q=m_g_q, g_k=m_g_k,
              conv_w=m_conv_w, g_oa=m_g_oa, g_oc=m_g_oc, w_o=m_w_o, w_pl=m_w_pl, w_plg=m_w_plg, g_pl=m_g_pl)
    vs = dict(g_in=v_g_in, w_in=v_w_in, g_cq=v_g_cq, w_uq=v_w_uq, g_ckv=v_g_ckv, w_ukv=v_w_ukv, g_q=v_g_q, g_k=v_g_k,
              conv_w=v_conv_w, g_oa=v_g_oa, g_oc=v_g_oc, w_o=v_w_o, w_pl=v_w_pl, w_plg=v_w_plg, g_pl=v_g_pl)
    names = list(weights)
    grads, deltas, new_m, new_v = [], [], [], []
    for n in names:
        w = weights[n]
        w2 = w.reshape(-1, w.shape[-1])
        if n in half_by_name:
            g2, d, nm, nv = _adamw_halves(w2, *half_by_name[n], ms[n].reshape(w2.shape), vs[n].reshape(w2.shape), c,
                                          "adamw_" + n)
        else:
            g2 = g_by_name[n].reshape(w2.shape)
            d, nm, nv = _adamw(w2, g2, ms[n].reshape(w2.shape), vs[n].reshape(w2.shape), "adamw_" + n)
        grads.append(g2.reshape(w.shape))
        deltas.append(d.reshape(w.shape))
        new_m.append(nm.reshape(w.shape))
        new_v.append(nv.reshape(w.shape))
    return (loss_out, gx.reshape(x.shape), *grads, *deltas, *new_m, *new_v)
```

```python
import functools
import math

import jax
import jax.numpy as jnp
from jax import lax
from jax.experimental import pallas as pl
from jax.experimental.pallas import tpu as pltpu

F32 = jnp.float32
BF16 = jnp.bfloat16

D_MODEL = 1024
N_HEADS = 4
NOPE = 128
ROPE = 64
V_DIM = 128
QK_DIM = NOPE + ROPE
HEAD_PAD = 256
Q_LORA = 256
KV_LORA = 128
ATTN_W = 512
CONV_W = 512
PLE = 256
IN_TOTAL = 3008
PROJ_EXT = 3072
ROPE_THETA = 10000.0
EPS = 1e-6
SCALE = 1.0 / math.sqrt(QK_DIM)
LOG2E = math.log2(math.e)
EXP2_SCALE = SCALE * LOG2E
NEG = -1e30
SOFTMAX_ROWS = 32
SUB_TILE = 256

LR, B1, B2, ADAM_EPS, WD, STEP = 0.001, 0.9, 0.999, 1e-08, 0.01, 10

N_CHIPS = 4
LANES = 128
VMEM_LIMIT = 56 * 1024 * 1024
MESH = pl.DeviceIdType.MESH


def _params(**kw):
    return pltpu.CompilerParams(vmem_limit_bytes=VMEM_LIMIT, **kw)


def _inv_rms(x, n):
    return lax.rsqrt(jnp.sum(x * x, axis=-1, keepdims=True) / n + EPS)


def _lane_sum(a):
    folded = a[:, 0:LANES]
    for c0 in range(LANES, a.shape[1], LANES):
        folded = folded + a[:, c0:c0 + LANES]
    head = folded.astype(BF16)
    tail = (folded - head.astype(F32)).astype(BF16)
    return _dot(jnp.concatenate([head, tail], axis=1), jnp.ones((2 * LANES, LANES), BF16))


def _inv_rms_mxu(x):
    return lax.rsqrt(_lane_sum(x * x) / x.shape[1] + EPS)


def _rep(r, width):
    return r if width == LANES else jnp.tile(r, (1, width // LANES))


def _sigmoid(z):
    return 1.0 / (1.0 + jnp.exp(-z))


def _swap_rope_halves(b):
    lane = lax.broadcasted_iota(jnp.int32, b.shape, 1)
    swapped = jnp.where(lane < 32, pltpu.roll(b, 96, 1), pltpu.roll(b, 32, 1))
    return jnp.where(lane < ROPE, swapped, 0.0)


def _dot(a, b):
    return jnp.dot(a, b, preferred_element_type=F32)


def _dot_nt(a, b):
    return lax.dot_general(a, b, (((1,), (1,)), ((), ())), preferred_element_type=F32)


def _dot_tn(a, b):
    return lax.dot_general(a, b, (((0,), (0,)), ((), ())), preferred_element_type=F32)


def _colsum(a):
    return jnp.sum(a, axis=0, keepdims=True)


def _full(shape):
    return pl.BlockSpec(shape, lambda *_: (0,) * len(shape))


def _rope_tables(pos_ref, invf_ref, sgn_ref):
    ang = pos_ref[...].astype(F32) * invf_ref[...]
    return jnp.cos(ang), jnp.sin(ang) * sgn_ref[...]


def _fwd_proj(x, pos, g_in, w_in, g_cq, w_uq, g_ckv, w_ukv, gq, gk, invf, sgn, tm):
    T = x.shape[0]

    ts = min(SUB_TILE, tm)

    def body(x_ref, pos_ref, g_in_ref, w_in_ref, g_cq_ref, w_uq_ref, g_ckv_ref, w_ukv_ref, gq_ref, gk_ref,
             invf_ref, sgn_ref, proj_ref, q_ref, k_ref, v_ref):
        for r0 in range(0, tm, ts):
            rows = slice(r0, r0 + ts)
            xv = x_ref[rows, :]
            h = (xv * _rep(_inv_rms_mxu(xv), D_MODEL) * g_in_ref[...]).astype(BF16)
            def project(c0):
                proj_ref[rows, c0:c0 + 512] = _dot(h, w_in_ref[:, c0:c0 + 512])

            lat = _dot(h, w_in_ref[:, 0:512])
            proj_ref[rows, 0:512] = lat
            c_q = lat[:, 0:Q_LORA]
            cqn = (c_q * _rep(_inv_rms_mxu(c_q), Q_LORA) * g_cq_ref[...]).astype(BF16)
            c_kv = lat[:, Q_LORA:Q_LORA + KV_LORA]
            ckvn = (c_kv * _inv_rms_mxu(c_kv) * g_ckv_ref[...]).astype(BF16)
            kpe = lat[:, 384:512]
            kpe_sq = kpe * kpe
            cos_b, sin_b = _rope_tables(pos_ref.at[rows, :], invf_ref, sgn_ref)
            gq_a, gq_b = gq_ref[:, 0:NOPE], gq_ref[:, NOPE:HEAD_PAD]
            gk_a, gk_b = gk_ref[:, 0:NOPE], gk_ref[:, NOPE:HEAD_PAD]
            for hd in range(N_HEADS):
                project(512 * (hd + 1))
                c0 = hd * HEAD_PAD
                qh = _dot(cqn, w_uq_ref[:, c0:c0 + HEAD_PAD])
                a, b = qh[:, 0:NOPE], qh[:, NOPE:HEAD_PAD]
                r = lax.rsqrt(_lane_sum(a * a + b * b) / QK_DIM + EPS)
                bn = b * r * gq_b
                q_ref[hd, rows, 0:NOPE] = (a * r * gq_a).astype(BF16)
                q_ref[hd, rows, NOPE:HEAD_PAD] = (bn * cos_b + _swap_rope_halves(bn) * sin_b).astype(BF16)
                kvh = _dot(ckvn, w_ukv_ref[:, c0:c0 + HEAD_PAD])
                ka = kvh[:, 0:NOPE]
                rk = lax.rsqrt(_lane_sum(ka * ka + kpe_sq) / QK_DIM + EPS)
                kbn = kpe * rk * gk_b
                k_ref[hd, rows, 0:NOPE] = (ka * rk * gk_a).astype(BF16)
                k_ref[hd, rows, NOPE:HEAD_PAD] = (kbn * cos_b + _swap_rope_halves(kbn) * sin_b).astype(BF16)
                v_ref[hd, rows, 0:V_DIM] = kvh[:, NOPE:HEAD_PAD].astype(BF16)
                v_ref[hd, rows, V_DIM:2 * V_DIM] = jnp.ones((ts, V_DIM), BF16)
            project(512 * (N_HEADS + 1))

    row = lambda i: (i, 0)
    head_rows = lambda i: (0, i, 0)
    return pl.pallas_call(
        body, name="fwd_proj", grid=(T // tm,),
        in_specs=[pl.BlockSpec((tm, D_MODEL), row), pl.BlockSpec((tm, 1), row), _full((1, D_MODEL)),
                  _full((D_MODEL, PROJ_EXT)), _full((1, Q_LORA)), _full((Q_LORA, N_HEADS * HEAD_PAD)),
                  _full((1, KV_LORA)), _full((KV_LORA, N_HEADS * HEAD_PAD)), _full((1, HEAD_PAD)), _full((1, HEAD_PAD)),
                  _full((1, LANES)), _full((1, LANES))],
        out_specs=[pl.BlockSpec((tm, PROJ_EXT), row), pl.BlockSpec((N_HEADS, tm, HEAD_PAD), head_rows),
                   pl.BlockSpec((N_HEADS, tm, HEAD_PAD), head_rows), pl.BlockSpec((N_HEADS, tm, 2 * V_DIM), head_rows)],
        out_shape=[jax.ShapeDtypeStruct((T, PROJ_EXT), F32), jax.ShapeDtypeStruct((N_HEADS, T, HEAD_PAD), BF16),
                   jax.ShapeDtypeStruct((N_HEADS, T, HEAD_PAD), BF16), jax.ShapeDtypeStruct((N_HEADS, T, 2 * V_DIM), BF16)],
        compiler_params=_params(dimension_semantics=("arbitrary",)),
    )(x, pos, g_in, w_in, g_cq, w_uq, g_ckv, w_ukv, gq, gk, invf, sgn)


def _chunk_pipeline(n_loop, lag, matmuls, pointwise, accumulate, last):
    slots = lag + 1

    def iteration(t, slot):
        matmuls(jnp.minimum(t + lag, n_loop), (slot + lag) % slots)
        accumulate(jnp.maximum(t - lag, 0), (slot + 1) % slots)
        pointwise(t, slot, False)

    def finish(slot):
        for back in range(lag, 0, -1):
            accumulate(jnp.maximum(n_loop - back, 0), (slot - back) % slots)
        pointwise(n_loop, slot, True)
        accumulate(n_loop, slot)
        last()

    for u in range(lag):
        matmuls(jnp.minimum(u, n_loop), u)

    def unrolled(tt, carry):
        for slot in range(slots):
            iteration(slots * tt + slot, slot)
        return carry

    lax.fori_loop(0, n_loop // slots, unrolled, 0)
    rest = lax.rem(n_loop, slots)
    t0 = n_loop - rest

    for r in range(slots):
        @pl.when(rest == r)
        def _():
            for slot in range(r):
                iteration(t0 + slot, slot)
            finish(r)


def _attn_fwd(q, k, v, tq):
    T = q.shape[1]
    tk = tq
    rc = min(SOFTMAX_ROWS, tq)

    def body(q_ref, k_ref, v_ref, o_ref, lse_ref, s0, s1, s2, p0, p1, p2, a0, a1, a2, m_ref, acc_ref):
        qi = pl.program_id(1)
        s_buf, p_buf, a_buf = (s0, s1, s2), (p0, p1, p2), (a0, a1, a2)

        def scores(t, slot):
            ks = pl.multiple_of(t * tk, tk)
            s_buf[slot][...] = _dot_nt(q_ref[0], k_ref[0, pl.ds(ks, tk), :])

        def values(t, slot):
            ks = pl.multiple_of(t * tk, tk)
            acc_ref[...] = acc_ref[...] * a_buf[slot][...] + _dot(p_buf[slot][...], v_ref[0, pl.ds(ks, tk), :])

        def softmax(t, slot, masked):
            s_all = s_buf[slot][...]
            if masked:
                row = lax.broadcasted_iota(jnp.int32, (tq, tk), 0)
                col = lax.broadcasted_iota(jnp.int32, (tq, tk), 1)
                s_all = jnp.where(col <= row, s_all, NEG)
                s_buf[slot][...] = s_all
            m_old = m_ref[...]
            m_new = jnp.maximum(m_old, jnp.max(s_all, axis=1, keepdims=True))
            a_buf[slot][...] = jnp.exp2((m_old - m_new) * EXP2_SCALE)
            m_ref[...] = m_new
            for r0 in range(0, tq, rc):
                s = s_buf[slot][r0:r0 + rc, :]
                p_buf[slot][r0:r0 + rc, :] = jnp.exp2((s - m_new[r0:r0 + rc, :]) * EXP2_SCALE).astype(BF16)

        def last():
            l = acc_ref[:, V_DIM:2 * V_DIM]
            o_ref[...] = acc_ref[:, 0:V_DIM] / l
            lse_ref[0] = (m_ref[...] * SCALE + jnp.log(l)).T[0:1, :]

        m_ref[...] = jnp.full_like(m_ref, NEG)
        acc_ref[...] = jnp.zeros_like(acc_ref)
        for p_late, a_late in ((p1, a1), (p2, a2)):
            p_late[...] = jnp.zeros_like(p_late)
            a_late[...] = jnp.ones_like(a_late)
        _chunk_pipeline(qi, 2, scores, softmax, values, last)

    return pl.pallas_call(
        body, name="attn_fwd", grid=(N_HEADS, T // tq),
        in_specs=[pl.BlockSpec((1, tq, HEAD_PAD), lambda h, i: (h, i, 0)),
                  pl.BlockSpec((1, T, HEAD_PAD), lambda h, i: (h, 0, 0)),
                  pl.BlockSpec((1, T, 2 * V_DIM), lambda h, i: (h, 0, 0))],
        out_specs=[pl.BlockSpec((tq, V_DIM), lambda h, i: (i, h)),
                   pl.BlockSpec((1, 1, tq), lambda h, i: (h, 0, i))],
        out_shape=[jax.ShapeDtypeStruct((T, ATTN_W), F32), jax.ShapeDtypeStruct((N_HEADS, 1, T), F32)],
        scratch_shapes=[pltpu.VMEM((tq, tk), F32)] * 3 + [pltpu.VMEM((tq, tk), BF16)] * 3
                       + [pltpu.VMEM((tq, 1), F32)] * 4 + [pltpu.VMEM((tq, 2 * V_DIM), F32)],
        compiler_params=_params(dimension_semantics=("arbitrary", "arbitrary")),
    )(q, k, v)


def _tail(x, o, proj, p, tgt, g_oa, g_oc, g_pl, conv_w, w_o, w_pl, w_plg, tm):
    T = x.shape[0]
    nt = T // tm

    def body(x_ref, o_ref, za_ref, cb_ref, cc_ref, cx_ref, zc_ref, cch_ref, cxh_ref, p_ref, tgt_ref,
             g_oa_ref, g_oc_ref, g_pl_ref, cw_ref, w_o_ref, w_pl_ref, w_plg_ref,
             dx1_ref, do_ref, delta_ref, dtail_ref, du_ref,
             dw_o_ref, dw_pl_ref, dw_plg_ref, dg_oa_ref, dg_oc_ref, dg_pl_ref, dcw_ref, loss_ref):
        i = pl.program_id(0)

        @pl.when(i == 0)
        def _():
            for r in (dw_o_ref, dw_pl_ref, dw_plg_ref, dg_oa_ref, dg_oc_ref, dg_pl_ref, dcw_ref, loss_ref):
                r[...] = jnp.zeros_like(r)

        xv, ov, za, cb, zc = x_ref[...], o_ref[...], za_ref[...], cb_ref[...], zc_ref[...]
        g_oa, g_oc, g_pl = g_oa_ref[...], g_oc_ref[...], g_pl_ref[...]
        w0, w1, w2 = cw_ref[0:1, :], cw_ref[1:2, :], cw_ref[2:3, :]

        pb = p_ref[...].astype(BF16)
        pp = _dot(pb, w_pl_ref[...])

        sa = _sigmoid(za)
        silu_a = za * sa
        ga = ov * silu_a
        ra = _inv_rms(ga, ATTN_W)
        xa = ga * ra
        ya = xa * g_oa
        v = cc_ref[...] * cx_ref[...]
        not_first = jnp.where(i > 0, 1.0, 0.0)
        hv6 = cch_ref[6:7, :] * cxh_ref[6:7, :] * not_first
        hv7 = cch_ref[7:8, :] * cxh_ref[7:8, :] * not_first
        row = lax.broadcasted_iota(jnp.int32, v.shape, 0)
        v1 = jnp.where(row == 0, hv7, pltpu.roll(v, 1, 0))
        v2 = jnp.where(row == 0, hv6, jnp.where(row == 1, hv7, pltpu.roll(v, 2, 0)))
        u = w0 * v2 + w1 * v1 + w2 * v
        sc = _sigmoid(zc)
        silu_c = zc * sc
        gc = cb * u * silu_c
        rc = _inv_rms(gc, CONV_W)
        xc = gc * rc
        yc = xc * g_oc
        ycat = jnp.concatenate([ya, yc], axis=-1).astype(BF16)
        x1 = xv + _dot(ycat, w_o_ref[...])
        r1 = _inv_rms(x1, D_MODEL)
        xh1 = x1 * r1
        n1 = (xh1 * g_pl).astype(BF16)
        gate = _sigmoid(_dot(n1, w_plg_ref[...]))
        err = x1 + gate * pp - tgt_ref[...]
        loss_ref[...] += 0.5 * jnp.sum(err * err) / D_MODEL
        dy = err / D_MODEL

        dpp = (dy * gate).astype(BF16)
        da = (dy * pp * gate * (1.0 - gate)).astype(BF16)
        dn1 = _dot_nt(da, w_plg_ref[...])
        dw_pl_ref[...] += _dot_tn(pb, dpp)
        dw_plg_ref[...] += _dot_tn(n1, da)
        dg_pl_ref[...] += _colsum(dn1 * xh1)
        dxh = dn1 * g_pl
        dx1 = dy + r1 * (dxh - xh1 * (jnp.sum(dxh * xh1, axis=-1, keepdims=True) / D_MODEL))
        dx1_ref[...] = dx1
        dx1b = dx1.astype(BF16)
        dycat = _dot_nt(dx1b, w_o_ref[...])
        dya, dyc = dycat[:, 0:ATTN_W], dycat[:, ATTN_W:D_MODEL]

        dw_o_ref[0:ATTN_W, :] += _dot_tn(ycat[:, 0:ATTN_W], dx1b)
        dg_oa_ref[...] += _colsum(dya * xa)
        dxa = dya * g_oa
        dga = ra * (dxa - xa * (jnp.sum(dxa * xa, axis=-1, keepdims=True) / ATTN_W))
        do = (dga * silu_a).astype(BF16)
        do_ref[...] = do
        dof = do.astype(F32) * ov
        for hd in range(N_HEADS):
            delta_ref[hd] = _lane_sum(dof[:, hd * V_DIM:(hd + 1) * V_DIM]).T[0:1, :]
        dtail_ref[:, 0:512] = (dga * ov * (sa * (1.0 + za * (1.0 - sa)))).astype(BF16)

        dw_o_ref[ATTN_W:D_MODEL, :] += _dot_tn(ycat[:, ATTN_W:D_MODEL], dx1b)
        dg_oc_ref[...] += _colsum(dyc * xc)
        dxc = dyc * g_oc
        dgc = rc * (dxc - xc * (jnp.sum(dxc * xc, axis=-1, keepdims=True) / CONV_W))
        dtail_ref[:, 512:1024] = (dgc * u * silu_c).astype(BF16)
        du = dgc * cb * silu_c
        du_ref[...] = du
        dtail_ref[:, 1024:1536] = (dgc * cb * u * (sc * (1.0 + zc * (1.0 - sc)))).astype(BF16)
        dcw_ref[0:1, :] += _colsum(du * v2)
        dcw_ref[1:2, :] += _colsum(du * v1)
        dcw_ref[2:3, :] += _colsum(du * v)

    row = lambda i: (i, 0)
    col = lambda c: (lambda i: (i, c))
    halo = lambda c: (lambda i: (jnp.maximum(i * (tm // 8) - 1, 0), c))
    in_specs = [pl.BlockSpec((tm, D_MODEL), row), pl.BlockSpec((tm, ATTN_W), row)]
    in_specs += [pl.BlockSpec((tm, 512), col(c)) for c in (1, 2, 3, 4, 5)]
    in_specs += [pl.BlockSpec((8, 512), halo(3)), pl.BlockSpec((8, 512), halo(4))]
    in_specs += [pl.BlockSpec((tm, PLE), row), pl.BlockSpec((tm, D_MODEL), row),
                 _full((1, ATTN_W)), _full((1, CONV_W)), _full((1, D_MODEL)), _full((3, CONV_W)),
                 _full((D_MODEL, D_MODEL)), _full((PLE, D_MODEL)), _full((D_MODEL, D_MODEL))]
    out_specs = [pl.BlockSpec((tm, D_MODEL), row), pl.BlockSpec((tm, ATTN_W), row),
                 pl.BlockSpec((N_HEADS, 1, tm), lambda i: (0, 0, i)), pl.BlockSpec((tm, 1536), row),
                 pl.BlockSpec((tm, CONV_W), row),
                 _full((D_MODEL, D_MODEL)), _full((PLE, D_MODEL)), _full((D_MODEL, D_MODEL)),
                 _full((1, ATTN_W)), _full((1, CONV_W)), _full((1, D_MODEL)), _full((3, CONV_W)), _full((1, LANES))]
    out_shape = [jax.ShapeDtypeStruct((T, D_MODEL), F32), jax.ShapeDtypeStruct((T, ATTN_W), BF16),
                 jax.ShapeDtypeStruct((N_HEADS, 1, T), F32), jax.ShapeDtypeStruct((T, 1536), BF16),
                 jax.ShapeDtypeStruct((T, CONV_W), F32),
                 jax.ShapeDtypeStruct((D_MODEL, D_MODEL), F32), jax.ShapeDtypeStruct((PLE, D_MODEL), F32),
                 jax.ShapeDtypeStruct((D_MODEL, D_MODEL), F32),
                 jax.ShapeDtypeStruct((1, ATTN_W), F32), jax.ShapeDtypeStruct((1, CONV_W), F32),
                 jax.ShapeDtypeStruct((1, D_MODEL), F32), jax.ShapeDtypeStruct((3, CONV_W), F32),
                 jax.ShapeDtypeStruct((1, LANES), F32)]
    return pl.pallas_call(
        body, name="tail", grid=(nt,), in_specs=in_specs, out_specs=out_specs, out_shape=out_shape,
        compiler_params=_params(dimension_semantics=("arbitrary",)),
    )(x, o, proj, proj, proj, proj, proj, proj, proj, p, tgt, g_oa, g_oc, g_pl, conv_w, w_o, w_pl, w_plg)


def _attn_bwd(q, k, v, do, lse_row, delta_row, tk):
    T = q.shape[1]
    tq = tk
    nq = T // tq
    rc = min(SOFTMAX_ROWS, tk)

    def body(q_ref, k_ref, v_ref, do_ref, lse_ref, dl_ref, dq_ref, dk_ref, dv_ref,
             s0, s1, d0, d1, p0, p1, g0, g1, dk_acc, dv_acc):
        kj = pl.program_id(1)
        s_buf, dp_buf, p_buf, g_buf = (s0, s1), (d0, d1), (p0, p1), (g0, g1)

        @pl.when(kj == 0)
        def _():
            dq_ref[...] = jnp.zeros_like(dq_ref)

        def q_start(t):
            return pl.multiple_of((nq - 1 - t) * tq, tq)

        def matmuls(t, slot):
            qs = q_start(t)
            s_buf[slot][...] = _dot_nt(k_ref[0], q_ref[0, pl.ds(qs, tq), :])
            dp_buf[slot][...] = _dot_nt(v_ref[0], do_ref[pl.ds(qs, tq), :])

        def pointwise(t, slot, masked):
            qs = q_start(t)
            lse2 = lse_ref[0, :, pl.ds(qs, tq)] * LOG2E
            dl = dl_ref[0, :, pl.ds(qs, tq)]
            for r0 in range(0, tk, rc):
                st = s_buf[slot][r0:r0 + rc, :]
                if masked:
                    row = lax.broadcasted_iota(jnp.int32, (rc, tq), 0)
                    col = lax.broadcasted_iota(jnp.int32, (rc, tq), 1)
                    st = jnp.where(row + r0 <= col, st, NEG)
                pt = jnp.exp2(st * EXP2_SCALE - lse2)
                p_buf[slot][r0:r0 + rc, :] = pt.astype(BF16)
                g_buf[slot][r0:r0 + rc, :] = (pt * (dp_buf[slot][r0:r0 + rc, :] - dl) * SCALE).astype(BF16)

        def accumulate(t, slot):
            qs = q_start(t)
            dv_acc[...] += _dot(p_buf[slot][...], do_ref[pl.ds(qs, tq), :])
            dk_acc[...] += _dot(g_buf[slot][...], q_ref[0, pl.ds(qs, tq), :])
            dq_ref[0, pl.ds(qs, tq), :] += _dot_tn(g_buf[slot][...], k_ref[0])

        def last():
            dk_ref[0] = dk_acc[...]
            dv_ref[0] = dv_acc[...]

        dk_acc[...] = jnp.zeros_like(dk_acc)
        dv_acc[...] = jnp.zeros_like(dv_acc)
        for late in (p1, g1):
            late[...] = jnp.zeros_like(late)
        _chunk_pipeline(nq - 1 - kj, 1, matmuls, pointwise, accumulate, last)

    return pl.pallas_call(
        body, name="attn_bwd", grid=(N_HEADS, T // tk),
        in_specs=[pl.BlockSpec((1, T, HEAD_PAD), lambda h, j: (h, 0, 0)),
                  pl.BlockSpec((1, tk, HEAD_PAD), lambda h, j: (h, j, 0)),
                  pl.BlockSpec((1, tk, V_DIM), lambda h, j: (h, j, 0)),
                  pl.BlockSpec((T, V_DIM), lambda h, j: (0, h)),
                  pl.BlockSpec((1, 1, T), lambda h, j: (h, 0, 0)),
                  pl.BlockSpec((1, 1, T), lambda h, j: (h, 0, 0))],
        out_specs=[pl.BlockSpec((1, T, HEAD_PAD), lambda h, j: (h, 0, 0)),
                   pl.BlockSpec((1, tk, HEAD_PAD), lambda h, j: (h, j, 0)),
                   pl.BlockSpec((1, tk, V_DIM), lambda h, j: (h, j, 0))],
        out_shape=[jax.ShapeDtypeStruct((N_HEADS, T, HEAD_PAD), F32), jax.ShapeDtypeStruct((N_HEADS, T, HEAD_PAD), F32),
                   jax.ShapeDtypeStruct((N_HEADS, T, V_DIM), F32)],
        scratch_shapes=[pltpu.VMEM((tk, tq), F32)] * 4 + [pltpu.VMEM((tk, tq), BF16)] * 4
                       + [pltpu.VMEM((tk, HEAD_PAD), F32), pltpu.VMEM((tk, V_DIM), F32)],
        compiler_params=_params(dimension_semantics=("arbitrary", "arbitrary")),
    )(q, k, v, do, lse_row, delta_row)


def _bwd_proj(x, dx1, pos, proj, dq, dk, dv, dtail, du, g_in, w_in, g_cq, w_uq, g_ckv, w_ukv, gq, gk, conv_w,
              invf, sgn, tm):
    T = x.shape[0]
    nt = T // tm

    ts = min(SUB_TILE, tm)

    def body(x_ref, dx1_ref, pos_ref, lat_ref, cc_ref, cx_ref, dq_ref, dk_ref, dv_ref, dtail_ref, du_ref, dun_ref, *rest):
        consts, (gx_ref, h_ref, dproj_ref), sums = rest[:11], rest[11:14], rest[14:]
        cw_ref = consts[8]
        i = pl.program_id(0)

        @pl.when(i == 0)
        def _():
            for r in sums:
                r[...] = jnp.zeros_like(r)

        du_v = du_ref[...]
        not_last = jnp.where(i < nt - 1, 1.0, 0.0)
        nx0 = dun_ref[0:1, :] * not_last
        nx1 = dun_ref[1:2, :] * not_last
        row = lax.broadcasted_iota(jnp.int32, du_v.shape, 0)
        du1 = jnp.where(row == tm - 1, nx0, pltpu.roll(du_v, tm - 1, 0))
        du2 = jnp.where(row == tm - 2, nx0, jnp.where(row == tm - 1, nx1, pltpu.roll(du_v, tm - 2, 0)))
        dvc = cw_ref[2:3, :] * du_v + cw_ref[1:2, :] * du1 + cw_ref[0:1, :] * du2
        dproj_ref[:, 1536:2048] = (dvc * cx_ref[...]).astype(BF16)
        dproj_ref[:, 2048:2560] = (dvc * cc_ref[...]).astype(BF16)

        for r0 in range(0, tm, ts):
            rows = slice(r0, r0 + ts)
            work(x_ref.at[rows, :], dx1_ref.at[rows, :], pos_ref.at[rows, :], lat_ref.at[rows, :],
                 dq_ref.at[:, rows, :], dk_ref.at[:, rows, :], dv_ref.at[:, rows, :], dtail_ref.at[rows, :], *consts,
                 gx_ref.at[rows, :], h_ref.at[:, rows], dproj_ref.at[rows, :], *sums)

    def work(x_ref, dx1_ref, pos_ref, lat_ref, dq_ref, dk_ref, dv_ref, dtail_ref,
             g_in_ref, w_in_ref, g_cq_ref, w_uq_ref, g_ckv_ref, w_ukv_ref, gq_ref, gk_ref, cw_ref, invf_ref, sgn_ref,
             gx_ref, h_ref, dproj_ref, dw_uq_ref, dw_ukv_ref, dg_in_ref, dg_cq_ref, dg_ckv_ref, dgq_ref, dgk_ref):
        xv = x_ref[...]
        r0 = _rep(_inv_rms_mxu(xv), D_MODEL)
        xh0 = xv * r0
        g_in = g_in_ref[...]
        h_ref[...] = (xh0 * g_in).astype(BF16).T

        c_q = lat_ref[:, 0:Q_LORA]
        rq = _rep(_inv_rms_mxu(c_q), Q_LORA)
        xq = c_q * rq
        g_cq = g_cq_ref[...]
        cqn = (xq * g_cq).astype(BF16)
        c_kv = lat_ref[:, Q_LORA:Q_LORA + KV_LORA]
        rkv = _inv_rms_mxu(c_kv)
        xkv = c_kv * rkv
        g_ckv = g_ckv_ref[...]
        ckvn = (xkv * g_ckv).astype(BF16)
        kpe = lat_ref[:, 384:512]
        kpe_sq = kpe * kpe
        cos_b, sin_b = _rope_tables(pos_ref, invf_ref, sgn_ref)
        gq_a, gq_b = gq_ref[:, 0:NOPE], gq_ref[:, NOPE:HEAD_PAD]
        gk_a, gk_b = gk_ref[:, 0:NOPE], gk_ref[:, NOPE:HEAD_PAD]

        dproj_ref[:, 512:1536] = dtail_ref[:, 0:1024]
        dproj_ref[:, 2560:3072] = dtail_ref[:, 1024:1536]

        def dh_part(c0):
            return _dot_nt(dproj_ref[:, c0:c0 + 512], w_in_ref[:, c0:c0 + 512])

        later_chunks = ((512,), (1024,), (1536, 2048), (2560,))
        dh = jnp.zeros((ts, D_MODEL), F32)
        dkpe = jnp.zeros((ts, LANES), F32)
        dcqn = jnp.zeros((ts, Q_LORA), F32)
        dckvn = jnp.zeros((ts, KV_LORA), F32)
        for hd in range(N_HEADS):
            for chunk in later_chunks[hd]:
                dh = dh + dh_part(chunk)
            c0 = hd * HEAD_PAD
            qh = _dot(cqn, w_uq_ref[:, c0:c0 + HEAD_PAD])
            a, b = qh[:, 0:NOPE], qh[:, NOPE:HEAD_PAD]
            r = lax.rsqrt(_lane_sum(a * a + b * b) / QK_DIM + EPS)
            xa, xb = a * r, b * r
            dan = dq_ref[hd, :, 0:NOPE]
            dbr = dq_ref[hd, :, NOPE:HEAD_PAD]
            dbn = dbr * cos_b + _swap_rope_halves(dbr * sin_b)
            dgq_ref[:, 0:NOPE] += _colsum(dan * xa)
            dgq_ref[:, NOPE:HEAD_PAD] += _colsum(dbn * xb)
            dxa, dxb = dan * gq_a, dbn * gq_b
            cq = _lane_sum(dxa * xa + dxb * xb) / QK_DIM
            dqh = jnp.concatenate([r * (dxa - xa * cq), r * (dxb - xb * cq)], axis=-1).astype(BF16)
            dw_uq_ref[:, c0:c0 + HEAD_PAD] += _dot_tn(cqn, dqh)
            dcqn = dcqn + _dot_nt(dqh, w_uq_ref[:, c0:c0 + HEAD_PAD])
            kvh = _dot(ckvn, w_ukv_ref[:, c0:c0 + HEAD_PAD])
            ka = kvh[:, 0:NOPE]
            rk = lax.rsqrt(_lane_sum(ka * ka + kpe_sq) / QK_DIM + EPS)
            xka, xkb = ka * rk, kpe * rk
            dkan = dk_ref[hd, :, 0:NOPE]
            dkbr = dk_ref[hd, :, NOPE:HEAD_PAD]
            dkbn = dkbr * cos_b + _swap_rope_halves(dkbr * sin_b)
            dgk_ref[:, 0:NOPE] += _colsum(dkan * xka)
            dgk_ref[:, NOPE:HEAD_PAD] += _colsum(dkbn * xkb)
            dxka, dxkb = dkan * gk_a, dkbn * gk_b
            ck = _lane_sum(dxka * xka + dxkb * xkb) / QK_DIM
            dkpe = dkpe + rk * (dxkb - xkb * ck)
            dkvh = jnp.concatenate([rk * (dxka - xka * ck), dv_ref[hd]], axis=-1).astype(BF16)
            dw_ukv_ref[:, c0:c0 + HEAD_PAD] += _dot_tn(ckvn, dkvh)
            dckvn = dckvn + _dot_nt(dkvh, w_ukv_ref[:, c0:c0 + HEAD_PAD])

        dg_cq_ref[...] += _colsum(dcqn * xq)
        dxq = dcqn * g_cq
        dproj_ref[:, 0:Q_LORA] = (rq * (dxq - xq * _rep(_lane_sum(dxq * xq) / Q_LORA, Q_LORA))).astype(BF16)
        dg_ckv_ref[...] += _colsum(dckvn * xkv)
        dxkv = dckvn * g_ckv
        dproj_ref[:, 256:384] = (rkv * (dxkv - xkv * (_lane_sum(dxkv * xkv) / KV_LORA))).astype(BF16)
        dproj_ref[:, 384:512] = dkpe.astype(BF16)
        dh = dh + dh_part(0)
        dg_in_ref[...] += _colsum(dh * xh0)
        dxh = dh * g_in
        gx_ref[...] = dx1_ref[...] + r0 * (dxh - xh0 * _rep(_lane_sum(dxh * xh0) / D_MODEL, D_MODEL))

    row = lambda i: (i, 0)
    col = lambda c: (lambda i: (i, c))
    head_rows = lambda i: (0, i, 0)
    nxt = lambda i: (jnp.minimum((i + 1) * (tm // 8), T // 8 - 1), 0)
    in_specs = [pl.BlockSpec((tm, D_MODEL), row), pl.BlockSpec((tm, D_MODEL), row), pl.BlockSpec((tm, 1), row),
                pl.BlockSpec((tm, 512), col(0)), pl.BlockSpec((tm, 512), col(3)), pl.BlockSpec((tm, 512), col(4)),
                pl.BlockSpec((N_HEADS, tm, HEAD_PAD), head_rows), pl.BlockSpec((N_HEADS, tm, HEAD_PAD), head_rows),
                pl.BlockSpec((N_HEADS, tm, V_DIM), head_rows), pl.BlockSpec((tm, 1536), row),
                pl.BlockSpec((tm, CONV_W), row), pl.BlockSpec((8, CONV_W), nxt),
                _full((1, D_MODEL)), _full((D_MODEL, PROJ_EXT)), _full((1, Q_LORA)), _full((Q_LORA, N_HEADS * HEAD_PAD)),
                _full((1, KV_LORA)), _full((KV_LORA, N_HEADS * HEAD_PAD)), _full((1, HEAD_PAD)), _full((1, HEAD_PAD)),
                _full((3, CONV_W)), _full((1, LANES)), _full((1, LANES))]
    out_specs = [pl.BlockSpec((tm, D_MODEL), row), pl.BlockSpec((D_MODEL, tm), lambda i: (0, i)),
                 pl.BlockSpec((tm, PROJ_EXT), row),
                 _full((Q_LORA, N_HEADS * HEAD_PAD)), _full((KV_LORA, N_HEADS * HEAD_PAD)),
                 _full((1, D_MODEL)), _full((1, Q_LORA)), _full((1, KV_LORA)), _full((1, HEAD_PAD)), _full((1, HEAD_PAD))]
    out_shape = [jax.ShapeDtypeStruct((T, D_MODEL), F32), jax.ShapeDtypeStruct((D_MODEL, T), BF16),
                 jax.ShapeDtypeStruct((T, PROJ_EXT), BF16),
                 jax.ShapeDtypeStruct((Q_LORA, N_HEADS * HEAD_PAD), F32), jax.ShapeDtypeStruct((KV_LORA, N_HEADS * HEAD_PAD), F32),
                 jax.ShapeDtypeStruct((1, D_MODEL), F32), jax.ShapeDtypeStruct((1, Q_LORA), F32),
                 jax.ShapeDtypeStruct((1, KV_LORA), F32), jax.ShapeDtypeStruct((1, HEAD_PAD), F32),
                 jax.ShapeDtypeStruct((1, HEAD_PAD), F32)]
    return pl.pallas_call(
        body, name="bwd_proj", grid=(nt,), in_specs=in_specs, out_specs=out_specs, out_shape=out_shape,
        compiler_params=_params(dimension_semantics=("arbitrary",)),
    )(x, dx1, pos, proj, proj, proj, dq, dk, dv, dtail, du, du, g_in, w_in, g_cq, w_uq, g_ckv, w_ukv, gq, gk, conv_w,
      invf, sgn)


def _matmul_acc(a, b, tt, tn):
    M, T = a.shape
    N = b.shape[1]

    def body(a_ref, b_ref, o_ref):
        @pl.when(pl.program_id(1) == 0)
        def _():
            o_ref[...] = jnp.zeros_like(o_ref)

        o_ref[...] += _dot(a_ref[...], b_ref[...])

    return pl.pallas_call(
        body, name="dw_in", grid=(N // tn, T // tt),
        in_specs=[pl.BlockSpec((M, tt), lambda j, t: (0, t)), pl.BlockSpec((tt, tn), lambda j, t: (t, j))],
        out_specs=pl.BlockSpec((M, tn), lambda j, t: (0, j)),
        out_shape=jax.ShapeDtypeStruct((M, N), F32),
        compiler_params=_params(dimension_semantics=("arbitrary", "arbitrary")),
    )(a, b)


def _add_chips(parts, small_parts):
    arrays = list(parts) + [small_parts]

    def body(*refs):
        ins, outs = refs[:len(arrays)], refs[len(arrays):]
        for a_ref, o_ref in zip(ins, outs):
            part = lambda k: a_ref[k].astype(F32)
            o_ref[...] = ((part(0) + part(1)) + part(2)) + part(3)

    in_specs, out_specs, out_shape = [], [], []
    for a in arrays:
        _, rows, cols = a.shape
        in_specs.append(pl.BlockSpec((N_CHIPS, rows // 2, cols), lambda i: (0, i, 0)))
        out_specs.append(pl.BlockSpec((rows // 2, cols), lambda i: (i, 0)))
        out_shape.append(jax.ShapeDtypeStruct((rows, cols), F32))
    outs = pl.pallas_call(body, name="add_chips", grid=(2,), in_specs=in_specs, out_specs=out_specs,
                          out_shape=out_shape, compiler_params=_params(dimension_semantics=("arbitrary",)))(*arrays)
    return outs[:-1], outs[-1]


def _adamw(w, g, m, v, name):
    rows, cols = w.shape
    rb = 256 if rows * cols > 512 * 1024 else rows

    def body(w_ref, g_ref, m_ref, v_ref, d_ref, nm_ref, nv_ref):
        _adamw_math(g_ref[...], w_ref, m_ref, v_ref, d_ref, nm_ref, nv_ref)

    spec = pl.BlockSpec((rb, cols), lambda i: (i, 0))
    shp = jax.ShapeDtypeStruct(w.shape, F32)
    return pl.pallas_call(body, name=name, grid=(rows // rb,), in_specs=[spec] * 4, out_specs=[spec] * 3,
                          out_shape=[shp] * 3)(w, g, m, v)


def _adamw_math(gv, w_ref, m_ref, v_ref, d_ref, nm_ref, nv_ref):
    nm = B1 * m_ref[...] + (1.0 - B1) * gv
    nv = B2 * v_ref[...] + (1.0 - B2) * (gv * gv)
    m_hat = nm / (1.0 - B1 ** STEP)
    v_hat = nv / (1.0 - B2 ** STEP)
    d_ref[...] = -LR * (m_hat / (jnp.sqrt(v_hat) + ADAM_EPS) + WD * w_ref[...])
    nm_ref[...] = nm
    nv_ref[...] = nv


def _adamw_halves(w, mine, other, m, v, c, name):
    hr, cols = mine.shape

    def body(c_ref, w_ref, mine_ref, other_ref, m_ref, v_ref, g_ref, d_ref, nm_ref, nv_ref):
        gv = jnp.where(pl.program_id(0) == c_ref[0], mine_ref[...], other_ref[...])
        g_ref[...] = gv
        _adamw_math(gv, w_ref, m_ref, v_ref, d_ref, nm_ref, nv_ref)

    half = pl.BlockSpec((hr, cols), lambda i, c_ref: (i, 0))
    whole = pl.BlockSpec((hr, cols), lambda i, c_ref: (0, 0))
    shp = jax.ShapeDtypeStruct(w.shape, F32)
    return pl.pallas_call(
        body, name=name, out_shape=[shp] * 4,
        grid_spec=pltpu.PrefetchScalarGridSpec(num_scalar_prefetch=1, grid=(2,), in_specs=[half, whole, whole, half, half],
                                               out_specs=[half] * 4),
        compiler_params=_params(dimension_semantics=("arbitrary",)),
    )(c.reshape(1), w, mine, other, m, v)


_ANY = pl.BlockSpec(memory_space=pl.ANY)


def _mesh_pos():
    return lax.axis_index("x"), lax.axis_index("y"), lax.axis_index("c")


def _other_chips(x, y):
    return [(1 - x, y), (x, 1 - y), (1 - x, 1 - y)]


def _remote(src, dst, send_sems, recv_sems, k, to):
    return pltpu.make_async_remote_copy(src_ref=src, dst_ref=dst, send_sem=send_sems.at[k], recv_sem=recv_sems.at[k],
                                        device_id=to, device_id_type=MESH)


def _gather_weights(shards):
    n = len(shards)
    halved = [s.shape[0] % 32 == 0 for s in shards]

    def body(*refs):
        ins, outs, stage = refs[:n], refs[n:2 * n], refs[2 * n:3 * n]
        send_sems, recv_sems, local_sems = refs[3 * n:]
        x, y, c = _mesh_pos()
        me = 2 * x + y
        chips = _other_chips(x, y)

        def part(i, ref, hc):
            if not halved[i]:
                return ref
            hr = shards[i].shape[0] // 2
            return ref.at[pl.ds(hc * hr, hr), :]

        locals_, started = [], []
        for i in range(n):
            stage[i][...] = ins[i][...].astype(BF16)
            mine = pltpu.make_async_copy(stage[i], outs[i].at[me], local_sems.at[i])
            mine.start()
            locals_.append(mine)
            for j, (cx, cy) in enumerate(chips):
                cp = _remote(part(i, stage[i], c), part(i, outs[i].at[me], c), send_sems, recv_sems, 6 * i + j, (cx, cy, c))
                cp.start()
                started.append(cp)
        for i in range(n):
            for j, (cx, cy) in enumerate(chips):
                got = part(i, outs[i].at[2 * cx + cy], c)
                _remote(got, got, send_sems, recv_sems, 6 * i + j, (cx, cy, c)).wait_recv()
                if halved[i]:
                    fwd = _remote(got, got, send_sems, recv_sems, 6 * i + 3 + j, (x, y, 1 - c))
                    fwd.start()
                    started.append(fwd)
        for i in range(n):
            if halved[i]:
                for j, (cx, cy) in enumerate(chips):
                    got = part(i, outs[i].at[2 * cx + cy], 1 - c)
                    _remote(got, got, send_sems, recv_sems, 6 * i + 3 + j, (x, y, 1 - c)).wait_recv()
        for cp in started:
            cp.wait_send()
        for cp in locals_:
            cp.wait()

    vmem = pl.BlockSpec(memory_space=pltpu.VMEM)
    return pl.pallas_call(
        body, name="gather_weights", in_specs=[vmem] * n, out_specs=[_ANY] * n,
        out_shape=[jax.ShapeDtypeStruct((N_CHIPS,) + s.shape, BF16) for s in shards],
        scratch_shapes=[pltpu.VMEM(s.shape, BF16) for s in shards]
                       + [pltpu.SemaphoreType.DMA((6 * n,)), pltpu.SemaphoreType.DMA((6 * n,)), pltpu.SemaphoreType.DMA((n,))],
        compiler_params=_params(),
    )(*shards)


def _swap_halves(grads, small):
    n = len(grads)
    arrays = list(grads) + [small]

    def body(*refs):
        ins, outs, send_sems, recv_sems = refs[:n + 1], refs[n + 1:2 * n + 2], refs[2 * n + 2], refs[2 * n + 3]
        x, y, c = _mesh_pos()
        cps = []
        for i in range(n + 1):
            src = ins[i]
            if i < n:
                hr = grads[i].shape[1] // 2
                src = src.at[:, pl.ds((1 - c) * hr, hr), :]
            cp = _remote(src, outs[i], send_sems, recv_sems, i, (x, y, 1 - c))
            cp.start()
            cps.append(cp)
        for cp in cps:
            cp.wait()

    out_shape = [jax.ShapeDtypeStruct((g.shape[0], g.shape[1] // 2, g.shape[2]), F32) for g in grads]
    out_shape.append(jax.ShapeDtypeStruct(small.shape, F32))
    outs = pl.pallas_call(
        body, name="pair_grads", in_specs=[_ANY] * (n + 1), out_specs=[_ANY] * (n + 1), out_shape=out_shape,
        scratch_shapes=[pltpu.SemaphoreType.DMA((n + 1,)), pltpu.SemaphoreType.DMA((n + 1,))],
    )(*arrays)
    return outs[:n], outs[n]


def _scatter_to_chips(grads, from_sibling, small, small_sibling):
    n = len(grads)

    def body(*refs):
        g_in, r_in, small_ref, small_sib_ref = refs[:n], refs[n:2 * n], refs[2 * n], refs[2 * n + 1]
        outs = refs[2 * n + 2:3 * n + 3]
        scratch = refs[3 * n + 3:]
        g_buf, r_buf, p_buf = scratch[:n], scratch[n:2 * n], scratch[2 * n:3 * n]
        s_buf, ss_buf, sp_buf, load_sems, send_sems, recv_sems, local_sems = scratch[3 * n:]
        x, y, c = _mesh_pos()
        me = 2 * x + y
        chips = _other_chips(x, y)

        loads = []
        for i in range(n):
            hr = from_sibling[i].shape[1]
            pair = (pltpu.make_async_copy(g_in[i].at[:, pl.ds(c * hr, hr), :], g_buf[i], load_sems.at[2 * i]),
                    pltpu.make_async_copy(r_in[i], r_buf[i], load_sems.at[2 * i + 1]))
            loads.append(pair)
        loads.append((pltpu.make_async_copy(small_ref, s_buf, load_sems.at[2 * n]),
                      pltpu.make_async_copy(small_sib_ref, ss_buf, load_sems.at[2 * n + 1])))
        for pair in loads:
            for cp in pair:
                cp.start()

        locals_, sends = [], []
        for i in range(n + 1):
            for cp in loads[i]:
                cp.wait()
            if i < n:
                p_buf[i][...] = (g_buf[i][...] + r_buf[i][...]).astype(BF16)
                part = lambda k, i=i: p_buf[i].at[k]
            else:
                sp_buf[...] = s_buf[...] + ss_buf[...]
                part = lambda k: sp_buf
            mine = pltpu.make_async_copy(part(me), outs[i].at[me], local_sems.at[i])
            mine.start()
            locals_.append(mine)
            for j, (cx, cy) in enumerate(chips):
                cp = _remote(part(2 * cx + cy), outs[i].at[me], send_sems, recv_sems, 3 * i + j, (cx, cy, c))
                cp.start()
                sends.append(cp)
        for i in range(n + 1):
            for j, (cx, cy) in enumerate(chips):
                got = outs[i].at[2 * cx + cy]
                _remote(got, got, send_sems, recv_sems, 3 * i + j, (cx, cy, c)).wait_recv()
        for cp in sends:
            cp.wait_send()
        for cp in locals_:
            cp.wait()

    out_shape = [jax.ShapeDtypeStruct(r.shape, BF16) for r in from_sibling]
    out_shape.append(jax.ShapeDtypeStruct((N_CHIPS,) + small.shape, F32))
    scratch = [pltpu.VMEM(r.shape, F32) for r in from_sibling] * 2 + [pltpu.VMEM(r.shape, BF16) for r in from_sibling]
    scratch += [pltpu.VMEM(small.shape, F32)] * 3
    scratch += [pltpu.SemaphoreType.DMA((2 * n + 2,)), pltpu.SemaphoreType.DMA((3 * n + 3,)),
                pltpu.SemaphoreType.DMA((3 * n + 3,)), pltpu.SemaphoreType.DMA((n + 1,))]
    outs = pl.pallas_call(
        body, name="scatter_grads", in_specs=[_ANY] * (2 * n + 2), out_specs=[_ANY] * (n + 1), out_shape=out_shape,
        scratch_shapes=scratch, compiler_params=_params(),
    )(*grads, *from_sibling, small, small_sibling)
    return outs[:n], outs[n]


def _share_halves(halves):
    n = len(halves)

    def body(*refs):
        ins, outs, send_sems, recv_sems = refs[:n], refs[n:2 * n], refs[2 * n], refs[2 * n + 1]
        x, y, c = _mesh_pos()
        cps = [_remote(ins[i], outs[i], send_sems, recv_sems, i, (x, y, 1 - c)) for i in range(n)]
        for cp in cps:
            cp.start()
        for cp in cps:
            cp.wait()

    return pl.pallas_call(
        body, name="share_halves", in_specs=[_ANY] * n, out_specs=[_ANY] * n,
        out_shape=[jax.ShapeDtypeStruct(h.shape, h.dtype) for h in halves],
        scratch_shapes=[pltpu.SemaphoreType.DMA((n,)), pltpu.SemaphoreType.DMA((n,))],
    )(*halves)


SHARD_COLS_IN = IN_TOTAL // N_CHIPS
KPE_END = Q_LORA + KV_LORA + ROPE


def _assemble_weights(c_in, c_uq, c_ukv, c_o, c_pl, c_plg, c_conv):
    by_cols = lambda a: a.transpose(1, 0, 2).reshape(a.shape[1], N_CHIPS * a.shape[2])
    w_in_e = jnp.concatenate([c_in[0][:, :KPE_END], jnp.zeros((D_MODEL, 64), BF16), c_in[0][:, KPE_END:],
                              c_in[1], c_in[2], c_in[3]], axis=1)
    w_uq_e = by_cols(jnp.pad(c_uq, ((0, 0), (0, 0), (0, HEAD_PAD - QK_DIM))))
    return (w_in_e, w_uq_e, by_cols(c_ukv), by_cols(c_conv).astype(F32), c_o.reshape(D_MODEL, D_MODEL),
            by_cols(c_pl), c_plg.reshape(D_MODEL, D_MODEL))


def _split_grads(dw_in_e, dw_uq_e, dw_ukv, dw_o, dw_pl, dw_plg):
    chip_major = lambda a: a.reshape(a.shape[0], N_CHIPS, a.shape[1] // N_CHIPS).transpose(1, 0, 2)
    first = jnp.concatenate([dw_in_e[:, :KPE_END], dw_in_e[:, KPE_END + 64:SHARD_COLS_IN + 64]], axis=1)
    rest = [dw_in_e[:, SHARD_COLS_IN * k + 64:SHARD_COLS_IN * (k + 1) + 64] for k in range(1, N_CHIPS)]
    return [jnp.stack([first] + rest), chip_major(dw_uq_e)[:, :, :QK_DIM], chip_major(dw_ukv),
            dw_o.reshape(N_CHIPS, D_MODEL // N_CHIPS, D_MODEL), chip_major(dw_pl),
            dw_plg.reshape(N_CHIPS, D_MODEL // N_CHIPS, D_MODEL)]


def _local_step(x, p, pos, tgt, gains, w_in_e, w_uq_e, w_ukv, conv_w, w_o, w_pl, w_plg, tm, tq):
    g_in, g_cq, g_ckv, g_q, g_k, g_oa, g_oc, g_pl = gains
    T = x.shape[0]
    zpad = lambda a, n: jnp.concatenate([a, jnp.zeros(a.shape[:-1] + (n,), a.dtype)], axis=-1)
    gq, gk = zpad(g_q, HEAD_PAD - QK_DIM), zpad(g_k, HEAD_PAD - QK_DIM)
    inv_freq = 1.0 / (ROPE_THETA ** (jnp.arange(0, ROPE, 2, dtype=F32) / ROPE))
    invf = jnp.concatenate([inv_freq, inv_freq, jnp.zeros((64,), F32)]).reshape(1, LANES)
    sgn = jnp.concatenate([-jnp.ones((32,), F32), jnp.ones((32,), F32), jnp.zeros((64,), F32)]).reshape(1, LANES)

    proj, q, k, v = _fwd_proj(x, pos, g_in, w_in_e, g_cq, w_uq_e, g_ckv, w_ukv, gq, gk, invf, sgn, min(2 * tm, T))
    o, lse = _attn_fwd(q, k, v, tq)
    (dx1, do, delta, dtail, du, dw_o, dw_pl, dw_plg, dg_oa, dg_oc, dg_pl, dconv, loss) = _tail(
        x, o, proj, p, tgt, g_oa, g_oc, g_pl, conv_w, w_o, w_pl, w_plg, tm)
    dq, dk, dv = _attn_bwd(q, k, v, do, lse, delta, tq)
    (gx, h, dproj, dw_uq_e, dw_ukv, dg_in, dg_cq, dg_ckv, dgq, dgk) = _bwd_proj(
        x, dx1, pos, proj, dq, dk, dv, dtail, du, g_in, w_in_e, g_cq, w_uq_e, g_ckv, w_ukv, gq, gk, conv_w, invf, sgn,
        min(2 * tm, T))
    dw_in_e = _matmul_acc(h, dproj, min(4096, T), 512)
    wgrads = (dw_in_e, dw_uq_e, dw_ukv, dw_o, dw_pl, dw_plg)
    ggrads = (dg_in, dg_cq, dg_ckv, dgq, dgk, dg_oa, dg_oc, dg_pl)
    return loss, gx, wgrads, ggrads, dconv


def kernel(x, p, positions, g_in, w_in, g_cq, w_uq, g_ckv, w_ukv, g_q, g_k, conv_w, g_oa, g_oc, w_o, w_pl, w_plg, g_pl, loss_target, m_g_in, m_w_in, m_g_cq, m_w_uq, m_g_ckv, m_w_ukv, m_g_q, m_g_k, m_conv_w, m_g_oa, m_g_oc, m_w_o, m_w_pl, m_w_plg, m_g_pl, v_g_in, v_w_in, v_g_cq, v_w_uq, v_g_ckv, v_w_ukv, v_g_q, v_g_k, v_conv_w, v_g_oa, v_g_oc, v_w_o, v_w_pl, v_w_plg, v_g_pl):
    T = x.shape[1]
    c = lax.axis_index("c")
    chip = 2 * lax.axis_index("x") + lax.axis_index("y")
    gains = [g.reshape(1, -1) for g in (g_in, g_cq, g_ckv, g_q, g_k, g_oa, g_oc, g_pl)]

    gathered = _gather_weights([w_in[0], w_uq[0], w_ukv[0], w_o[0], w_pl[0], w_plg[0], conv_w[0]])
    full = _assemble_weights(*gathered)

    loss, gx, wgrads, ggrads, dconv = _local_step(
        x[0], p[0, 0], positions.reshape(T, 1), loss_target[0], gains, *full, 256, 512)

    grads_cm = _split_grads(*wgrads)
    small_parts = [a.reshape(-1, LANES) for a in (*ggrads, loss, dconv)]
    small_rows = [a.shape[0] for a in small_parts]
    tile_rows = [-(-r // 8) * 8 for r in small_rows]
    tile_rows[-1] += -sum(tile_rows) % 16
    small = jnp.concatenate([jnp.pad(a, ((0, t - r), (0, 0))) for a, r, t in zip(small_parts, small_rows, tile_rows)])
    from_sibling, small_sibling = _swap_halves(grads_cm, small)
    by_chip, small_by_chip = _scatter_to_chips(grads_cm, from_sibling, small, small_sibling)
    halves, small_total = _add_chips(by_chip, small_by_chip)
    other_halves = _share_halves(halves)

    gg, off = [], 0
    for rows, tiled in zip(small_rows, tile_rows):
        gg.append(small_total[off:off + rows].reshape(1, -1))
        off += tiled
    loss_out = gg[8][0, 0]
    conv_total = gg[9].reshape(3, CONV_W)
    conv_g = lax.dynamic_slice(conv_total, (0, chip * (CONV_W // N_CHIPS)), (3, CONV_W // N_CHIPS))
    g_by_name = dict(g_in=gg[0], g_cq=gg[1], g_ckv=gg[2], g_q=gg[3][:, :QK_DIM], g_k=gg[4][:, :QK_DIM], conv_w=conv_g,
                     g_oa=gg[5], g_oc=gg[6], g_pl=gg[7])
    half_by_name = dict(zip(("w_in", "w_uq", "w_ukv", "w_o", "w_pl", "w_plg"), zip(halves, other_halves)))
    weights = dict(g_in=g_in, w_in=w_in, g_cq=g_cq, w_uq=w_uq, g_ckv=g_ckv, w_ukv=w_ukv, g_q=g_q, g_k=g_k,
                   conv_w=conv_w, g_oa=g_oa, g_oc=g_oc, w_o=w_o, w_pl=w_pl, w_plg=w_plg, g_pl=g_pl)
    ms = dict(g_in=m_g_in, w_in=m_w_in, g_cq=m_g_cq, w_uq=m_w_uq, g_ckv=m_g_ckv, w_ukv=m_w_ukv, g_q=m_g_q, g_k=m_g_k,
              conv_w=m_conv_w, g_oa=m_g_oa, g_oc=m_g_oc, w_o=m_w_o, w_pl=m_w_pl, w_plg=m_w_plg, g_pl=m_g_pl)
    vs = dict(g_in=v_g_in, w_in=v_w_in, g_cq=v_g_cq, w_uq=v_w_uq, g_ckv=v_g_ckv, w_ukv=v_w_ukv, g_q=v_g_q, g_k=v_g_k,
              conv_w=v_conv_w, g_oa=v_g_oa, g_oc=v_g_oc, w_o=v_w_o, w_pl=v_w_pl, w_plg=v_w_plg, g_pl=v_g_pl)
    names = list(weights)
    grads, deltas, new_m, new_v = [], [], [], []
    for n in names:
        w = weights[n]
        w2 = w.reshape(-1, w.shape[-1])
        if n in half_by_name:
            g2, d, nm, nv = _adamw_halves(w2, *half_by_name[n], ms[n].reshape(w2.shape), vs[n].reshape(w2.shape), c,
                                          "adamw_" + n)
        else:
            g2 = g_by_name[n].reshape(w2.shape)
            d, nm, nv = _adamw(w2, g2, ms[n].reshape(w2.shape), vs[n].reshape(w2.shape), "adamw_" + n)
        grads.append(g2.reshape(w.shape))
        deltas.append(d.reshape(w.shape))
        new_m.append(nm.reshape(w.shape))
        new_v.append(nv.reshape(w.shape))
    return (loss_out, gx.reshape(x.shape), *grads, *deltas, *new_m, *new_v)
```

```python
import functools
import math

import jax
import jax.numpy as jnp
from jax import lax
from jax.experimental import pallas as pl
from jax.experimental.pallas import tpu as pltpu

F32 = jnp.float32
BF16 = jnp.bfloat16

D_MODEL = 1024
N_HEADS = 4
NOPE = 128
ROPE = 64
V_DIM = 128
QK_DIM = NOPE + ROPE
HEAD_PAD = 256
Q_LORA = 256
KV_LORA = 128
ATTN_W = 512
CONV_W = 512
PLE = 256
IN_TOTAL = 3008
PROJ_EXT = 3072
ROPE_THETA = 10000.0
EPS = 1e-6
SCALE = 1.0 / math.sqrt(QK_DIM)
LOG2E = math.log2(math.e)
EXP2_SCALE = SCALE * LOG2E
NEG = -1e30
SOFTMAX_ROWS = 32
SUB_TILE = 256

LR, B1, B2, ADAM_EPS, WD, STEP = 0.001, 0.9, 0.999, 1e-08, 0.01, 10

N_CHIPS = 4
LANES = 128
VMEM_LIMIT = 56 * 1024 * 1024
MESH = pl.DeviceIdType.MESH


def _params(**kw):
    return pltpu.CompilerParams(vmem_limit_bytes=VMEM_LIMIT, **kw)


def _inv_rms(x, n):
    return lax.rsqrt(jnp.sum(x * x, axis=-1, keepdims=True) / n + EPS)


def _lane_sum(a):
    folded = a[:, 0:LANES]
    for c0 in range(LANES, a.shape[1], LANES):
        folded = folded + a[:, c0:c0 + LANES]
    head = folded.astype(BF16)
    tail = (folded - head.astype(F32)).astype(BF16)
    return _dot(jnp.concatenate([head, tail], axis=1), jnp.ones((2 * LANES, LANES), BF16))


def _inv_rms_mxu(x):
    return lax.rsqrt(_lane_sum(x * x) / x.shape[1] + EPS)


def _rep(r, width):
    return r if width == LANES else jnp.tile(r, (1, width // LANES))


def _sigmoid(z):
    return 1.0 / (1.0 + jnp.exp(-z))


def _swap_rope_halves(b):
    lane = lax.broadcasted_iota(jnp.int32, b.shape, 1)
    swapped = jnp.where(lane < 32, pltpu.roll(b, 96, 1), pltpu.roll(b, 32, 1))
    return jnp.where(lane < ROPE, swapped, 0.0)


def _dot(a, b):
    return jnp.dot(a, b, preferred_element_type=F32)


def _dot_nt(a, b):
    return lax.dot_general(a, b, (((1,), (1,)), ((), ())), preferred_element_type=F32)


def _dot_tn(a, b):
    return lax.dot_general(a, b, (((0,), (0,)), ((), ())), preferred_element_type=F32)


def _colsum(a):
    return jnp.sum(a, axis=0, keepdims=True)


def _full(shape):
    return pl.BlockSpec(shape, lambda *_: (0,) * len(shape))


def _rope_tables(pos_ref, invf_ref, sgn_ref):
    ang = pos_ref[...].astype(F32) * invf_ref[...]
    return jnp.cos(ang), jnp.sin(ang) * sgn_ref[...]


def _fwd_proj(x, pos, g_in, w_in, g_cq, w_uq, g_ckv, w_ukv, gq, gk, invf, sgn, late_shards, tm):
    T = x.shape[0]
    nt = T // tm
    n_late = len(late_shards)
    ts = min(SUB_TILE, tm)

    def body(x_ref, pos_ref, g_in_ref, w_in_ref, g_cq_ref, w_uq_ref, g_ckv_ref, w_ukv_ref, gq_ref, gk_ref,
             invf_ref, sgn_ref, *rest):
        late_in, (proj_ref, q_ref, k_ref, v_ref) = rest[:n_late], rest[n_late:n_late + 4]
        late_out, late_scratch = rest[n_late + 4:2 * n_late + 4], rest[2 * n_late + 4:]
        i = pl.program_id(0)
        if n_late:
            start, forward, drain = _gather_steps([s.shape for s in late_shards], late_in, late_out,
                                                  late_scratch[:n_late], *late_scratch[n_late:])
            pl.when(i == 0)(start)
            pl.when(i == nt // 2)(forward)

        for r0 in range(0, tm, ts):
            rows = slice(r0, r0 + ts)
            xv = x_ref[rows, :]
            h = (xv * _rep(_inv_rms_mxu(xv), D_MODEL) * g_in_ref[...]).astype(BF16)
            def project(c0):
                proj_ref[rows, c0:c0 + 512] = _dot(h, w_in_ref[:, c0:c0 + 512])

            lat = _dot(h, w_in_ref[:, 0:512])
            proj_ref[rows, 0:512] = lat
            c_q = lat[:, 0:Q_LORA]
            cqn = (c_q * _rep(_inv_rms_mxu(c_q), Q_LORA) * g_cq_ref[...]).astype(BF16)
            c_kv = lat[:, Q_LORA:Q_LORA + KV_LORA]
            ckvn = (c_kv * _inv_rms_mxu(c_kv) * g_ckv_ref[...]).astype(BF16)
            kpe = lat[:, 384:512]
            kpe_sq = kpe * kpe
            cos_b, sin_b = _rope_tables(pos_ref.at[rows, :], invf_ref, sgn_ref)
            gq_a, gq_b = gq_ref[:, 0:NOPE], gq_ref[:, NOPE:HEAD_PAD]
            gk_a, gk_b = gk_ref[:, 0:NOPE], gk_ref[:, NOPE:HEAD_PAD]
            for hd in range(N_HEADS):
                project(512 * (hd + 1))
                c0 = hd * HEAD_PAD
                qh = _dot(cqn, w_uq_ref[:, c0:c0 + HEAD_PAD])
                a, b = qh[:, 0:NOPE], qh[:, NOPE:HEAD_PAD]
                r = lax.rsqrt(_lane_sum(a * a + b * b) / QK_DIM + EPS)
                bn = b * r * gq_b
                q_ref[hd, rows, 0:NOPE] = (a * r * gq_a).astype(BF16)
                q_ref[hd, rows, NOPE:HEAD_PAD] = (bn * cos_b + _swap_rope_halves(bn) * sin_b).astype(BF16)
                kvh = _dot(ckvn, w_ukv_ref[:, c0:c0 + HEAD_PAD])
                ka = kvh[:, 0:NOPE]
                rk = lax.rsqrt(_lane_sum(ka * ka + kpe_sq) / QK_DIM + EPS)
                kbn = kpe * rk * gk_b
                k_ref[hd, rows, 0:NOPE] = (ka * rk * gk_a).astype(BF16)
                k_ref[hd, rows, NOPE:HEAD_PAD] = (kbn * cos_b + _swap_rope_halves(kbn) * sin_b).astype(BF16)
                v_ref[hd, rows, 0:V_DIM] = kvh[:, NOPE:HEAD_PAD].astype(BF16)
                v_ref[hd, rows, V_DIM:2 * V_DIM] = jnp.ones((ts, V_DIM), BF16)
            project(512 * (N_HEADS + 1))

        if n_late:
            pl.when(i == nt - 1)(drain)

    row = lambda i: (i, 0)
    head_rows = lambda i: (0, i, 0)
    outs = pl.pallas_call(
        body, name="fwd_proj", grid=(nt,),
        in_specs=[pl.BlockSpec((tm, D_MODEL), row), pl.BlockSpec((tm, 1), row), _full((1, D_MODEL)),
                  _full((D_MODEL, PROJ_EXT)), _full((1, Q_LORA)), _full((Q_LORA, N_HEADS * HEAD_PAD)),
                  _full((1, KV_LORA)), _full((KV_LORA, N_HEADS * HEAD_PAD)), _full((1, HEAD_PAD)), _full((1, HEAD_PAD)),
                  _full((1, LANES)), _full((1, LANES))] + [_full(s.shape) for s in late_shards],
        out_specs=[pl.BlockSpec((tm, PROJ_EXT), row), pl.BlockSpec((N_HEADS, tm, HEAD_PAD), head_rows),
                   pl.BlockSpec((N_HEADS, tm, HEAD_PAD), head_rows), pl.BlockSpec((N_HEADS, tm, 2 * V_DIM), head_rows)]
                  + [_ANY] * n_late,
        out_shape=[jax.ShapeDtypeStruct((T, PROJ_EXT), F32), jax.ShapeDtypeStruct((N_HEADS, T, HEAD_PAD), BF16),
                   jax.ShapeDtypeStruct((N_HEADS, T, HEAD_PAD), BF16), jax.ShapeDtypeStruct((N_HEADS, T, 2 * V_DIM), BF16)]
                  + _gathered_shapes(late_shards),
        scratch_shapes=_gather_scratch(late_shards) if n_late else [],
        compiler_params=_params(dimension_semantics=("arbitrary",)),
    )(x, pos, g_in, w_in, g_cq, w_uq, g_ckv, w_ukv, gq, gk, invf, sgn, *late_shards)
    return outs[:4], outs[4:]


def _chunk_pipeline(n_loop, lag, matmuls, pointwise, accumulate, last):
    slots = lag + 1

    def iteration(t, slot):
        matmuls(jnp.minimum(t + lag, n_loop), (slot + lag) % slots)
        accumulate(jnp.maximum(t - lag, 0), (slot + 1) % slots)
        pointwise(t, slot, False)

    def finish(slot):
        for back in range(lag, 0, -1):
            accumulate(jnp.maximum(n_loop - back, 0), (slot - back) % slots)
        pointwise(n_loop, slot, True)
        accumulate(n_loop, slot)
        last()

    for u in range(lag):
        matmuls(jnp.minimum(u, n_loop), u)

    def unrolled(tt, carry):
        for slot in range(slots):
            iteration(slots * tt + slot, slot)
        return carry

    lax.fori_loop(0, n_loop // slots, unrolled, 0)
    rest = lax.rem(n_loop, slots)
    t0 = n_loop - rest

    for r in range(slots):
        @pl.when(rest == r)
        def _():
            for slot in range(r):
                iteration(t0 + slot, slot)
            finish(r)


def _attn_fwd(q, k, v, tq):
    T = q.shape[1]
    tk = tq
    rc = min(SOFTMAX_ROWS, tq)

    def body(q_ref, k_ref, v_ref, o_ref, lse_ref, s0, s1, s2, p0, p1, p2, a0, a1, a2, m_ref, acc_ref):
        qi = pl.program_id(1)
        s_buf, p_buf, a_buf = (s0, s1, s2), (p0, p1, p2), (a0, a1, a2)

        def scores(t, slot):
            ks = pl.multiple_of(t * tk, tk)
            s_buf[slot][...] = _dot_nt(q_ref[0], k_ref[0, pl.ds(ks, tk), :])

        def values(t, slot):
            ks = pl.multiple_of(t * tk, tk)
            acc_ref[...] = acc_ref[...] * a_buf[slot][...] + _dot(p_buf[slot][...], v_ref[0, pl.ds(ks, tk), :])

        def softmax(t, slot, masked):
            s_all = s_buf[slot][...]
            if masked:
                row = lax.broadcasted_iota(jnp.int32, (tq, tk), 0)
                col = lax.broadcasted_iota(jnp.int32, (tq, tk), 1)
                s_all = jnp.where(col <= row, s_all, NEG)
                s_buf[slot][...] = s_all
            m_old = m_ref[...]
            m_new = jnp.maximum(m_old, jnp.max(s_all, axis=1, keepdims=True))
            a_buf[slot][...] = jnp.exp2((m_old - m_new) * EXP2_SCALE)
            m_ref[...] = m_new
            for r0 in range(0, tq, rc):
                s = s_buf[slot][r0:r0 + rc, :]
                p_buf[slot][r0:r0 + rc, :] = jnp.exp2((s - m_new[r0:r0 + rc, :]) * EXP2_SCALE).astype(BF16)

        def last():
            l = acc_ref[:, V_DIM:2 * V_DIM]
            o_ref[...] = acc_ref[:, 0:V_DIM] / l
            lse_ref[0] = (m_ref[...] * SCALE + jnp.log(l)).T[0:1, :]

        m_ref[...] = jnp.full_like(m_ref, NEG)
        acc_ref[...] = jnp.zeros_like(acc_ref)
        for p_late, a_late in ((p1, a1), (p2, a2)):
            p_late[...] = jnp.zeros_like(p_late)
            a_late[...] = jnp.ones_like(a_late)
        _chunk_pipeline(qi, 2, scores, softmax, values, last)

    return pl.pallas_call(
        body, name="attn_fwd", grid=(N_HEADS, T // tq),
        in_specs=[pl.BlockSpec((1, tq, HEAD_PAD), lambda h, i: (h, i, 0)),
                  pl.BlockSpec((1, T, HEAD_PAD), lambda h, i: (h, 0, 0)),
                  pl.BlockSpec((1, T, 2 * V_DIM), lambda h, i: (h, 0, 0))],
        out_specs=[pl.BlockSpec((tq, V_DIM), lambda h, i: (i, h)),
                   pl.BlockSpec((1, 1, tq), lambda h, i: (h, 0, i))],
        out_shape=[jax.ShapeDtypeStruct((T, ATTN_W), F32), jax.ShapeDtypeStruct((N_HEADS, 1, T), F32)],
        scratch_shapes=[pltpu.VMEM((tq, tk), F32)] * 3 + [pltpu.VMEM((tq, tk), BF16)] * 3
                       + [pltpu.VMEM((tq, 1), F32)] * 4 + [pltpu.VMEM((tq, 2 * V_DIM), F32)],
        compiler_params=_params(dimension_semantics=("arbitrary", "arbitrary")),
    )(q, k, v)


def _tail(x, o, proj, p, tgt, g_oa, g_oc, g_pl, conv_w, w_o, w_pl, w_plg, tm):
    T = x.shape[0]
    nt = T // tm

    def body(x_ref, o_ref, za_ref, cb_ref, cc_ref, cx_ref, zc_ref, cch_ref, cxh_ref, p_ref, tgt_ref,
             g_oa_ref, g_oc_ref, g_pl_ref, cw_ref, w_o_ref, w_pl_ref, w_plg_ref,
             dx1_ref, do_ref, delta_ref, dtail_ref, du_ref,
             dw_o_ref, dw_pl_ref, dw_plg_ref, dg_oa_ref, dg_oc_ref, dg_pl_ref, dcw_ref, loss_ref):
        i = pl.program_id(0)

        @pl.when(i == 0)
        def _():
            for r in (dw_o_ref, dw_pl_ref, dw_plg_ref, dg_oa_ref, dg_oc_ref, dg_pl_ref, dcw_ref, loss_ref):
                r[...] = jnp.zeros_like(r)

        xv, ov, za, cb, zc = x_ref[...], o_ref[...], za_ref[...], cb_ref[...], zc_ref[...]
        g_oa, g_oc, g_pl = g_oa_ref[...], g_oc_ref[...], g_pl_ref[...]
        w0, w1, w2 = cw_ref[0:1, :], cw_ref[1:2, :], cw_ref[2:3, :]

        pb = p_ref[...].astype(BF16)
        pp = _dot(pb, w_pl_ref[...])

        sa = _sigmoid(za)
        silu_a = za * sa
        ga = ov * silu_a
        ra = _inv_rms(ga, ATTN_W)
        xa = ga * ra
        ya = xa * g_oa
        v = cc_ref[...] * cx_ref[...]
        not_first = jnp.where(i > 0, 1.0, 0.0)
        hv6 = cch_ref[6:7, :] * cxh_ref[6:7, :] * not_first
        hv7 = cch_ref[7:8, :] * cxh_ref[7:8, :] * not_first
        row = lax.broadcasted_iota(jnp.int32, v.shape, 0)
        v1 = jnp.where(row == 0, hv7, pltpu.roll(v, 1, 0))
        v2 = jnp.where(row == 0, hv6, jnp.where(row == 1, hv7, pltpu.roll(v, 2, 0)))
        u = w0 * v2 + w1 * v1 + w2 * v
        sc = _sigmoid(zc)
        silu_c = zc * sc
        gc = cb * u * silu_c
        rc = _inv_rms(gc, CONV_W)
        xc = gc * rc
        yc = xc * g_oc
        ycat = jnp.concatenate([ya, yc], axis=-1).astype(BF16)
        x1 = xv + _dot(ycat, w_o_ref[...])
        r1 = _inv_rms(x1, D_MODEL)
        xh1 = x1 * r1
        n1 = (xh1 * g_pl).astype(BF16)
        gate = _sigmoid(_dot(n1, w_plg_ref[...]))
        err = x1 + gate * pp - tgt_ref[...]
        loss_ref[...] += 0.5 * jnp.sum(err * err) / D_MODEL
        dy = err / D_MODEL

        dpp = (dy * gate).astype(BF16)
        da = (dy * pp * gate * (1.0 - gate)).astype(BF16)
        dn1 = _dot_nt(da, w_plg_ref[...])
        dw_pl_ref[...] += _dot_tn(pb, dpp)
        dw_plg_ref[...] += _dot_tn(n1, da)
        dg_pl_ref[...] += _colsum(dn1 * xh1)
        dxh = dn1 * g_pl
        dx1 = dy + r1 * (dxh - xh1 * (jnp.sum(dxh * xh1, axis=-1, keepdims=True) / D_MODEL))
        dx1_ref[...] = dx1
        dx1b = dx1.astype(BF16)
        dycat = _dot_nt(dx1b, w_o_ref[...])
        dya, dyc = dycat[:, 0:ATTN_W], dycat[:, ATTN_W:D_MODEL]

        dw_o_ref[0:ATTN_W, :] += _dot_tn(ycat[:, 0:ATTN_W], dx1b)
        dg_oa_ref[...] += _colsum(dya * xa)
        dxa = dya * g_oa
        dga = ra * (dxa - xa * (jnp.sum(dxa * xa, axis=-1, keepdims=True) / ATTN_W))
        do = (dga * silu_a).astype(BF16)
        do_ref[...] = do
        dof = do.astype(F32) * ov
        for hd in range(N_HEADS):
            delta_ref[hd] = _lane_sum(dof[:, hd * V_DIM:(hd + 1) * V_DIM]).T[0:1, :]
        dtail_ref[:, 0:512] = (dga * ov * (sa * (1.0 + za * (1.0 - sa)))).astype(BF16)

        dw_o_ref[ATTN_W:D_MODEL, :] += _dot_tn(ycat[:, ATTN_W:D_MODEL], dx1b)
        dg_oc_ref[...] += _colsum(dyc * xc)
        dxc = dyc * g_oc
        dgc = rc * (dxc - xc * (jnp.sum(dxc * xc, axis=-1, keepdims=True) / CONV_W))
        dtail_ref[:, 512:1024] = (dgc * u * silu_c).astype(BF16)
        du = dgc * cb * silu_c
        du_ref[...] = du
        dtail_ref[:, 1024:1536] = (dgc * cb * u * (sc * (1.0 + zc * (1.0 - sc)))).astype(BF16)
        dcw_ref[0:1, :] += _colsum(du * v2)
        dcw_ref[1:2, :] += _colsum(du * v1)
        dcw_ref[2:3, :] += _colsum(du * v)

    row = lambda i: (i, 0)
    col = lambda c: (lambda i: (i, c))
    halo = lambda c: (lambda i: (jnp.maximum(i * (tm // 8) - 1, 0), c))
    in_specs = [pl.BlockSpec((tm, D_MODEL), row), pl.BlockSpec((tm, ATTN_W), row)]
    in_specs += [pl.BlockSpec((tm, 512), col(c)) for c in (1, 2, 3, 4, 5)]
    in_specs += [pl.BlockSpec((8, 512), halo(3)), pl.BlockSpec((8, 512), halo(4))]
    in_specs += [pl.BlockSpec((tm, PLE), row), pl.BlockSpec((tm, D_MODEL), row),
                 _full((1, ATTN_W)), _full((1, CONV_W)), _full((1, D_MODEL)), _full((3, CONV_W)),
                 _full((D_MODEL, D_MODEL)), _full((PLE, D_MODEL)), _full((D_MODEL, D_MODEL))]
    out_specs = [pl.BlockSpec((tm, D_MODEL), row), pl.BlockSpec((tm, ATTN_W), row),
                 pl.BlockSpec((N_HEADS, 1, tm), lambda i: (0, 0, i)), pl.BlockSpec((tm, 1536), row),
                 pl.BlockSpec((tm, CONV_W), row),
                 _full((D_MODEL, D_MODEL)), _full((PLE, D_MODEL)), _full((D_MODEL, D_MODEL)),
                 _full((1, ATTN_W)), _full((1, CONV_W)), _full((1, D_MODEL)), _full((3, CONV_W)), _full((1, LANES))]
    out_shape = [jax.ShapeDtypeStruct((T, D_MODEL), F32), jax.ShapeDtypeStruct((T, ATTN_W), BF16),
                 jax.ShapeDtypeStruct((N_HEADS, 1, T), F32), jax.ShapeDtypeStruct((T, 1536), BF16),
                 jax.ShapeDtypeStruct((T, CONV_W), F32),
                 jax.ShapeDtypeStruct((D_MODEL, D_MODEL), F32), jax.ShapeDtypeStruct((PLE, D_MODEL), F32),
                 jax.ShapeDtypeStruct((D_MODEL, D_MODEL), F32),
                 jax.ShapeDtypeStruct((1, ATTN_W), F32), jax.ShapeDtypeStruct((1, CONV_W), F32),
                 jax.ShapeDtypeStruct((1, D_MODEL), F32), jax.ShapeDtypeStruct((3, CONV_W), F32),
                 jax.ShapeDtypeStruct((1, LANES), F32)]
    return pl.pallas_call(
        body, name="tail", grid=(nt,), in_specs=in_specs, out_specs=out_specs, out_shape=out_shape,
        compiler_params=_params(dimension_semantics=("arbitrary",)),
    )(x, o, proj, proj, proj, proj, proj, proj, proj, p, tgt, g_oa, g_oc, g_pl, conv_w, w_o, w_pl, w_plg)


def _attn_bwd(q, k, v, do, lse_row, delta_row, tk):
    T = q.shape[1]
    tq = tk
    nq = T // tq
    rc = min(SOFTMAX_ROWS, tk)

    def body(q_ref, k_ref, v_ref, do_ref, lse_ref, dl_ref, dq_ref, dk_ref, dv_ref,
             s0, s1, d0, d1, p0, p1, g0, g1, dk_acc, dv_acc):
        kj = pl.program_id(1)
        s_buf, dp_buf, p_buf, g_buf = (s0, s1), (d0, d1), (p0, p1), (g0, g1)

        @pl.when(kj == 0)
        def _():
            dq_ref[...] = jnp.zeros_like(dq_ref)

        def q_start(t):
            return pl.multiple_of((nq - 1 - t) * tq, tq)

        def matmuls(t, slot):
            qs = q_start(t)
            s_buf[slot][...] = _dot_nt(k_ref[0], q_ref[0, pl.ds(qs, tq), :])
            dp_buf[slot][...] = _dot_nt(v_ref[0], do_ref[pl.ds(qs, tq), :])

        def pointwise(t, slot, masked):
            qs = q_start(t)
            lse2 = lse_ref[0, :, pl.ds(qs, tq)] * LOG2E
            dl = dl_ref[0, :, pl.ds(qs, tq)]
            for r0 in range(0, tk, rc):
                st = s_buf[slot][r0:r0 + rc, :]
                if masked:
                    row = lax.broadcasted_iota(jnp.int32, (rc, tq), 0)
                    col = lax.broadcasted_iota(jnp.int32, (rc, tq), 1)
                    st = jnp.where(row + r0 <= col, st, NEG)
                pt = jnp.exp2(st * EXP2_SCALE - lse2)
                p_buf[slot][r0:r0 + rc, :] = pt.astype(BF16)
                g_buf[slot][r0:r0 + rc, :] = (pt * (dp_buf[slot][r0:r0 + rc, :] - dl) * SCALE).astype(BF16)

        def accumulate(t, slot):
            qs = q_start(t)
            dv_acc[...] += _dot(p_buf[slot][...], do_ref[pl.ds(qs, tq), :])
            dk_acc[...] += _dot(g_buf[slot][...], q_ref[0, pl.ds(qs, tq), :])
            dq_ref[0, pl.ds(qs, tq), :] += _dot_tn(g_buf[slot][...], k_ref[0])

        def last():
            dk_ref[0] = dk_acc[...]
            dv_ref[0] = dv_acc[...]

        dk_acc[...] = jnp.zeros_like(dk_acc)
        dv_acc[...] = jnp.zeros_like(dv_acc)
        for late in (p1, g1):
            late[...] = jnp.zeros_like(late)
        _chunk_pipeline(nq - 1 - kj, 1, matmuls, pointwise, accumulate, last)

    return pl.pallas_call(
        body, name="attn_bwd", grid=(N_HEADS, T // tk),
        in_specs=[pl.BlockSpec((1, T, HEAD_PAD), lambda h, j: (h, 0, 0)),
                  pl.BlockSpec((1, tk, HEAD_PAD), lambda h, j: (h, j, 0)),
                  pl.BlockSpec((1, tk, V_DIM), lambda h, j: (h, j, 0)),
                  pl.BlockSpec((T, V_DIM), lambda h, j: (0, h)),
                  pl.BlockSpec((1, 1, T), lambda h, j: (h, 0, 0)),
                  pl.BlockSpec((1, 1, T), lambda h, j: (h, 0, 0))],
        out_specs=[pl.BlockSpec((1, T, HEAD_PAD), lambda h, j: (h, 0, 0)),
                   pl.BlockSpec((1, tk, HEAD_PAD), lambda h, j: (h, j, 0)),
                   pl.BlockSpec((1, tk, V_DIM), lambda h, j: (h, j, 0))],
        out_shape=[jax.ShapeDtypeStruct((N_HEADS, T, HEAD_PAD), F32), jax.ShapeDtypeStruct((N_HEADS, T, HEAD_PAD), F32),
                   jax.ShapeDtypeStruct((N_HEADS, T, V_DIM), F32)],
        scratch_shapes=[pltpu.VMEM((tk, tq), F32)] * 4 + [pltpu.VMEM((tk, tq), BF16)] * 4
                       + [pltpu.VMEM((tk, HEAD_PAD), F32), pltpu.VMEM((tk, V_DIM), F32)],
        compiler_params=_params(dimension_semantics=("arbitrary", "arbitrary")),
    )(q, k, v, do, lse_row, delta_row)


def _bwd_proj(x, dx1, pos, proj, dq, dk, dv, dtail, du, g_in, w_in, g_cq, w_uq, g_ckv, w_ukv, gq, gk, conv_w,
              invf, sgn, tm):
    T = x.shape[0]
    nt = T // tm

    ts = min(SUB_TILE, tm)

    def body(x_ref, dx1_ref, pos_ref, lat_ref, cc_ref, cx_ref, dq_ref, dk_ref, dv_ref, dtail_ref, du_ref, dun_ref, *rest):
        consts, (gx_ref, h_ref, dproj_ref), sums = rest[:11], rest[11:14], rest[14:]
        cw_ref = consts[8]
        i = pl.program_id(0)

        @pl.when(i == 0)
        def _():
            for r in sums:
                r[...] = jnp.zeros_like(r)

        du_v = du_ref[...]
        not_last = jnp.where(i < nt - 1, 1.0, 0.0)
        nx0 = dun_ref[0:1, :] * not_last
        nx1 = dun_ref[1:2, :] * not_last
        row = lax.broadcasted_iota(jnp.int32, du_v.shape, 0)
        du1 = jnp.where(row == tm - 1, nx0, pltpu.roll(du_v, tm - 1, 0))
        du2 = jnp.where(row == tm - 2, nx0, jnp.where(row == tm - 1, nx1, pltpu.roll(du_v, tm - 2, 0)))
        dvc = cw_ref[2:3, :] * du_v + cw_ref[1:2, :] * du1 + cw_ref[0:1, :] * du2
        dproj_ref[:, 1536:2048] = (dvc * cx_ref[...]).astype(BF16)
        dproj_ref[:, 2048:2560] = (dvc * cc_ref[...]).astype(BF16)

        for r0 in range(0, tm, ts):
            rows = slice(r0, r0 + ts)
            work(x_ref.at[rows, :], dx1_ref.at[rows, :], pos_ref.at[rows, :], lat_ref.at[rows, :],
                 dq_ref.at[:, rows, :], dk_ref.at[:, rows, :], dv_ref.at[:, rows, :], dtail_ref.at[rows, :], *consts,
                 gx_ref.at[rows, :], h_ref.at[:, rows], dproj_ref.at[rows, :], *sums)

    def work(x_ref, dx1_ref, pos_ref, lat_ref, dq_ref, dk_ref, dv_ref, dtail_ref,
             g_in_ref, w_in_ref, g_cq_ref, w_uq_ref, g_ckv_ref, w_ukv_ref, gq_ref, gk_ref, cw_ref, invf_ref, sgn_ref,
             gx_ref, h_ref, dproj_ref, dw_uq_ref, dw_ukv_ref, dg_in_ref, dg_cq_ref, dg_ckv_ref, dgq_ref, dgk_ref):
        xv = x_ref[...]
        r0 = _rep(_inv_rms_mxu(xv), D_MODEL)
        xh0 = xv * r0
        g_in = g_in_ref[...]
        h_ref[...] = (xh0 * g_in).astype(BF16).T

        c_q = lat_ref[:, 0:Q_LORA]
        rq = _rep(_inv_rms_mxu(c_q), Q_LORA)
        xq = c_q * rq
        g_cq = g_cq_ref[...]
        cqn = (xq * g_cq).astype(BF16)
        c_kv = lat_ref[:, Q_LORA:Q_LORA + KV_LORA]
        rkv = _inv_rms_mxu(c_kv)
        xkv = c_kv * rkv
        g_ckv = g_ckv_ref[...]
        ckvn = (xkv * g_ckv).astype(BF16)
        kpe = lat_ref[:, 384:512]
        kpe_sq = kpe * kpe
        cos_b, sin_b = _rope_tables(pos_ref, invf_ref, sgn_ref)
        gq_a, gq_b = gq_ref[:, 0:NOPE], gq_ref[:, NOPE:HEAD_PAD]
        gk_a, gk_b = gk_ref[:, 0:NOPE], gk_ref[:, NOPE:HEAD_PAD]

        dproj_ref[:, 512:1536] = dtail_ref[:, 0:1024]
        dproj_ref[:, 2560:3072] = dtail_ref[:, 1024:1536]

        def dh_part(c0):
            return _dot_nt(dproj_ref[:, c0:c0 + 512], w_in_ref[:, c0:c0 + 512])

        later_chunks = ((512,), (1024,), (1536, 2048), (2560,))
        dh = jnp.zeros((ts, D_MODEL), F32)
        dkpe = jnp.zeros((ts, LANES), F32)
        dcqn = jnp.zeros((ts, Q_LORA), F32)
        dckvn = jnp.zeros((ts, KV_LORA), F32)
        for hd in range(N_HEADS):
            for chunk in later_chunks[hd]:
                dh = dh + dh_part(chunk)
            c0 = hd * HEAD_PAD
            qh = _dot(cqn, w_uq_ref[:, c0:c0 + HEAD_PAD])
            a, b = qh[:, 0:NOPE], qh[:, NOPE:HEAD_PAD]
            r = lax.rsqrt(_lane_sum(a * a + b * b) / QK_DIM + EPS)
            xa, xb = a * r, b * r
            dan = dq_ref[hd, :, 0:NOPE]
            dbr = dq_ref[hd, :, NOPE:HEAD_PAD]
            dbn = dbr * cos_b + _swap_rope_halves(dbr * sin_b)
            dgq_ref[:, 0:NOPE] += _colsum(dan * xa)
            dgq_ref[:, NOPE:HEAD_PAD] += _colsum(dbn * xb)
            dxa, dxb = dan * gq_a, dbn * gq_b
            cq = _lane_sum(dxa * xa + dxb * xb) / QK_DIM
            dqh = jnp.concatenate([r * (dxa - xa * cq), r * (dxb - xb * cq)], axis=-1).astype(BF16)
            dw_uq_ref[:, c0:c0 + HEAD_PAD] += _dot_tn(cqn, dqh)
            dcqn = dcqn + _dot_nt(dqh, w_uq_ref[:, c0:c0 + HEAD_PAD])
            kvh = _dot(ckvn, w_ukv_ref[:, c0:c0 + HEAD_PAD])
            ka = kvh[:, 0:NOPE]
            rk = lax.rsqrt(_lane_sum(ka * ka + kpe_sq) / QK_DIM + EPS)
            xka, xkb = ka * rk, kpe * rk
            dkan = dk_ref[hd, :, 0:NOPE]
            dkbr = dk_ref[hd, :, NOPE:HEAD_PAD]
            dkbn = dkbr * cos_b + _swap_rope_halves(dkbr * sin_b)
            dgk_ref[:, 0:NOPE] += _colsum(dkan * xka)
            dgk_ref[:, NOPE:HEAD_PAD] += _colsum(dkbn * xkb)
            dxka, dxkb = dkan * gk_a, dkbn * gk_b
            ck = _lane_sum(dxka * xka + dxkb * xkb) / QK_DIM
            dkpe = dkpe + rk * (dxkb - xkb * ck)
            dkvh = jnp.concatenate([rk * (dxka - xka * ck), dv_ref[hd]], axis=-1).astype(BF16)
            dw_ukv_ref[:, c0:c0 + HEAD_PAD] += _dot_tn(ckvn, dkvh)
            dckvn = dckvn + _dot_nt(dkvh, w_ukv_ref[:, c0:c0 + HEAD_PAD])

        dg_cq_ref[...] += _colsum(dcqn * xq)
        dxq = dcqn * g_cq
        dproj_ref[:, 0:Q_LORA] = (rq * (dxq - xq * _rep(_lane_sum(dxq * xq) / Q_LORA, Q_LORA))).astype(BF16)
        dg_ckv_ref[...] += _colsum(dckvn * xkv)
        dxkv = dckvn * g_ckv
        dproj_ref[:, 256:384] = (rkv * (dxkv - xkv * (_lane_sum(dxkv * xkv) / KV_LORA))).astype(BF16)
        dproj_ref[:, 384:512] = dkpe.astype(BF16)
        dh = dh + dh_part(0)
        dg_in_ref[...] += _colsum(dh * xh0)
        dxh = dh * g_in
        gx_ref[...] = dx1_ref[...] + r0 * (dxh - xh0 * _rep(_lane_sum(dxh * xh0) / D_MODEL, D_MODEL))

    row = lambda i: (i, 0)
    col = lambda c: (lambda i: (i, c))
    head_rows = lambda i: (0, i, 0)
    nxt = lambda i: (jnp.minimum((i + 1) * (tm // 8), T // 8 - 1), 0)
    in_specs = [pl.BlockSpec((tm, D_MODEL), row), pl.BlockSpec((tm, D_MODEL), row), pl.BlockSpec((tm, 1), row),
                pl.BlockSpec((tm, 512), col(0)), pl.BlockSpec((tm, 512), col(3)), pl.BlockSpec((tm, 512), col(4)),
                pl.BlockSpec((N_HEADS, tm, HEAD_PAD), head_rows), pl.BlockSpec((N_HEADS, tm, HEAD_PAD), head_rows),
                pl.BlockSpec((N_HEADS, tm, V_DIM), head_rows), pl.BlockSpec((tm, 1536), row),
                pl.BlockSpec((tm, CONV_W), row), pl.BlockSpec((8, CONV_W), nxt),
                _full((1, D_MODEL)), _full((D_MODEL, PROJ_EXT)), _full((1, Q_LORA)), _full((Q_LORA, N_HEADS * HEAD_PAD)),
                _full((1, KV_LORA)), _full((KV_LORA, N_HEADS * HEAD_PAD)), _full((1, HEAD_PAD)), _full((1, HEAD_PAD)),
                _full((3, CONV_W)), _full((1, LANES)), _full((1, LANES))]
    out_specs = [pl.BlockSpec((tm, D_MODEL), row), pl.BlockSpec((D_MODEL, tm), lambda i: (0, i)),
                 pl.BlockSpec((tm, PROJ_EXT), row),
                 _full((Q_LORA, N_HEADS * HEAD_PAD)), _full((KV_LORA, N_HEADS * HEAD_PAD)),
                 _full((1, D_MODEL)), _full((1, Q_LORA)), _full((1, KV_LORA)), _full((1, HEAD_PAD)), _full((1, HEAD_PAD))]
    out_shape = [jax.ShapeDtypeStruct((T, D_MODEL), F32), jax.ShapeDtypeStruct((D_MODEL, T), BF16),
                 jax.ShapeDtypeStruct((T, PROJ_EXT), BF16),
                 jax.ShapeDtypeStruct((Q_LORA, N_HEADS * HEAD_PAD), F32), jax.ShapeDtypeStruct((KV_LORA, N_HEADS * HEAD_PAD), F32),
                 jax.ShapeDtypeStruct((1, D_MODEL), F32), jax.ShapeDtypeStruct((1, Q_LORA), F32),
                 jax.ShapeDtypeStruct((1, KV_LORA), F32), jax.ShapeDtypeStruct((1, HEAD_PAD), F32),
                 jax.ShapeDtypeStruct((1, HEAD_PAD), F32)]
    return pl.pallas_call(
        body, name="bwd_proj", grid=(nt,), in_specs=in_specs, out_specs=out_specs, out_shape=out_shape,
        compiler_params=_params(dimension_semantics=("arbitrary",)),
    )(x, dx1, pos, proj, proj, proj, dq, dk, dv, dtail, du, du, g_in, w_in, g_cq, w_uq, g_ckv, w_ukv, gq, gk, conv_w,
      invf, sgn)


def _matmul_acc(a, b, tt, tn):
    M, T = a.shape
    N = b.shape[1]

    def body(a_ref, b_ref, o_ref):
        @pl.when(pl.program_id(1) == 0)
        def _():
            o_ref[...] = jnp.zeros_like(o_ref)

        o_ref[...] += _dot(a_ref[...], b_ref[...])

    return pl.pallas_call(
        body, name="dw_in", grid=(N // tn, T // tt),
        in_specs=[pl.BlockSpec((M, tt), lambda j, t: (0, t)), pl.BlockSpec((tt, tn), lambda j, t: (t, j))],
        out_specs=pl.BlockSpec((M, tn), lambda j, t: (0, j)),
        out_shape=jax.ShapeDtypeStruct((M, N), F32),
        compiler_params=_params(dimension_semantics=("arbitrary", "arbitrary")),
    )(a, b)


def _add_chips(parts, small_parts):
    arrays = list(parts) + [small_parts]

    def body(*refs):
        ins, outs = refs[:len(arrays)], refs[len(arrays):]
        for a_ref, o_ref in zip(ins, outs):
            part = lambda k: a_ref[k].astype(F32)
            o_ref[...] = ((part(0) + part(1)) + part(2)) + part(3)

    in_specs, out_specs, out_shape = [], [], []
    for a in arrays:
        _, rows, cols = a.shape
        in_specs.append(pl.BlockSpec((N_CHIPS, rows // 2, cols), lambda i: (0, i, 0)))
        out_specs.append(pl.BlockSpec((rows // 2, cols), lambda i: (i, 0)))
        out_shape.append(jax.ShapeDtypeStruct((rows, cols), F32))
    outs = pl.pallas_call(body, name="add_chips", grid=(2,), in_specs=in_specs, out_specs=out_specs,
                          out_shape=out_shape, compiler_params=_params(dimension_semantics=("arbitrary",)))(*arrays)
    return outs[:-1], outs[-1]


def _adamw(w, g, m, v, name):
    rows, cols = w.shape
    rb = 256 if rows * cols > 512 * 1024 else rows

    def body(w_ref, g_ref, m_ref, v_ref, d_ref, nm_ref, nv_ref):
        _adamw_math(g_ref[...], w_ref, m_ref, v_ref, d_ref, nm_ref, nv_ref)

    spec = pl.BlockSpec((rb, cols), lambda i: (i, 0))
    shp = jax.ShapeDtypeStruct(w.shape, F32)
    return pl.pallas_call(body, name=name, grid=(rows // rb,), in_specs=[spec] * 4, out_specs=[spec] * 3,
                          out_shape=[shp] * 3)(w, g, m, v)


def _adamw_math(gv, w_ref, m_ref, v_ref, d_ref, nm_ref, nv_ref):
    nm = B1 * m_ref[...] + (1.0 - B1) * gv
    nv = B2 * v_ref[...] + (1.0 - B2) * (gv * gv)
    m_hat = nm / (1.0 - B1 ** STEP)
    v_hat = nv / (1.0 - B2 ** STEP)
    d_ref[...] = -LR * (m_hat / (jnp.sqrt(v_hat) + ADAM_EPS) + WD * w_ref[...])
    nm_ref[...] = nm
    nv_ref[...] = nv


def _adamw_halves(w, mine, other, m, v, c, name):
    hr, cols = mine.shape

    def body(c_ref, w_ref, mine_ref, other_ref, m_ref, v_ref, g_ref, d_ref, nm_ref, nv_ref):
        gv = jnp.where(pl.program_id(0) == c_ref[0], mine_ref[...], other_ref[...])
        g_ref[...] = gv
        _adamw_math(gv, w_ref, m_ref, v_ref, d_ref, nm_ref, nv_ref)

    half = pl.BlockSpec((hr, cols), lambda i, c_ref: (i, 0))
    whole = pl.BlockSpec((hr, cols), lambda i, c_ref: (0, 0))
    shp = jax.ShapeDtypeStruct(w.shape, F32)
    return pl.pallas_call(
        body, name=name, out_shape=[shp] * 4,
        grid_spec=pltpu.PrefetchScalarGridSpec(num_scalar_prefetch=1, grid=(2,), in_specs=[half, whole, whole, half, half],
                                               out_specs=[half] * 4),
        compiler_params=_params(dimension_semantics=("arbitrary",)),
    )(c.reshape(1), w, mine, other, m, v)


_ANY = pl.BlockSpec(memory_space=pl.ANY)


def _mesh_pos():
    return lax.axis_index("x"), lax.axis_index("y"), lax.axis_index("c")


def _other_chips(x, y):
    return [(1 - x, y), (x, 1 - y), (1 - x, 1 - y)]


def _remote(src, dst, send_sems, recv_sems, k, to):
    return pltpu.make_async_remote_copy(src_ref=src, dst_ref=dst, send_sem=send_sems.at[k], recv_sem=recv_sems.at[k],
                                        device_id=to, device_id_type=MESH)


def _gather_weights(shards):
    n = len(shards)

    def body(*refs):
        start, forward, drain = _gather_steps([s.shape for s in shards], refs[:n], refs[n:2 * n], refs[2 * n:3 * n],
                                              *refs[3 * n:])
        start()
        forward()
        drain()

    vmem = pl.BlockSpec(memory_space=pltpu.VMEM)
    return pl.pallas_call(
        body, name="gather_weights", in_specs=[vmem] * n, out_specs=[_ANY] * n,
        out_shape=_gathered_shapes(shards), scratch_shapes=_gather_scratch(shards), compiler_params=_params(),
    )(*shards)


def _gathered_shapes(shards):
    return [jax.ShapeDtypeStruct((N_CHIPS,) + s.shape, BF16) for s in shards]


def _gather_scratch(shards):
    n = len(shards)
    return ([pltpu.VMEM(s.shape, BF16) for s in shards]
            + [pltpu.SemaphoreType.DMA((6 * n,)), pltpu.SemaphoreType.DMA((6 * n,)), pltpu.SemaphoreType.DMA((n,))])


def _gather_steps(shapes, ins, outs, stage, send_sems, recv_sems, local_sems):
    n = len(shapes)
    halved = [s[0] % 32 == 0 for s in shapes]

    def part(i, ref, hc):
        if not halved[i]:
            return ref
        hr = shapes[i][0] // 2
        return ref.at[pl.ds(hc * hr, hr), :]

    def to_chip(i, j, x, y, c):
        cx, cy = _other_chips(x, y)[j]
        return _remote(part(i, stage[i], c), part(i, outs[i].at[2 * x + y], c), send_sems, recv_sems, 6 * i + j, (cx, cy, c))

    def to_sibling(i, j, x, y, c):
        cx, cy = _other_chips(x, y)[j]
        got = part(i, outs[i].at[2 * cx + cy], c)
        return _remote(got, got, send_sems, recv_sems, 6 * i + 3 + j, (x, y, 1 - c))

    def local(i, x, y):
        return pltpu.make_async_copy(stage[i], outs[i].at[2 * x + y], local_sems.at[i])

    def start():
        x, y, c = _mesh_pos()
        for i in range(n):
            stage[i][...] = ins[i][...].astype(BF16)
            local(i, x, y).start()
            for j in range(3):
                to_chip(i, j, x, y, c).start()

    def forward():
        x, y, c = _mesh_pos()
        for i in range(n):
            for j, (cx, cy) in enumerate(_other_chips(x, y)):
                got = part(i, outs[i].at[2 * cx + cy], c)
                _remote(got, got, send_sems, recv_sems, 6 * i + j, (cx, cy, c)).wait_recv()
                if halved[i]:
                    to_sibling(i, j, x, y, c).start()

    def drain():
        x, y, c = _mesh_pos()
        for i in range(n):
            for j, (cx, cy) in enumerate(_other_chips(x, y)):
                if halved[i]:
                    got = part(i, outs[i].at[2 * cx + cy], 1 - c)
                    _remote(got, got, send_sems, recv_sems, 6 * i + 3 + j, (x, y, 1 - c)).wait_recv()
                    to_sibling(i, j, x, y, c).wait_send()
                to_chip(i, j, x, y, c).wait_send()
            local(i, x, y).wait()

    return start, forward, drain


def _swap_halves(grads, small):
    n = len(grads)
    arrays = list(grads) + [small]

    def body(*refs):
        ins, outs, send_sems, recv_sems = refs[:n + 1], refs[n + 1:2 * n + 2], refs[2 * n + 2], refs[2 * n + 3]
        x, y, c = _mesh_pos()
        cps = []
        for i in range(n + 1):
            src = ins[i]
            if i < n:
                hr = grads[i].shape[1] // 2
                src = src.at[:, pl.ds((1 - c) * hr, hr), :]
            cp = _remote(src, outs[i], send_sems, recv_sems, i, (x, y, 1 - c))
            cp.start()
            cps.append(cp)
        for cp in cps:
            cp.wait()

    out_shape = [jax.ShapeDtypeStruct((g.shape[0], g.shape[1] // 2, g.shape[2]), F32) for g in grads]
    out_shape.append(jax.ShapeDtypeStruct(small.shape, F32))
    outs = pl.pallas_call(
        body, name="pair_grads", in_specs=[_ANY] * (n + 1), out_specs=[_ANY] * (n + 1), out_shape=out_shape,
        scratch_shapes=[pltpu.SemaphoreType.DMA((n + 1,)), pltpu.SemaphoreType.DMA((n + 1,))],
    )(*arrays)
    return outs[:n], outs[n]


def _scatter_to_chips(grads, from_sibling, small, small_sibling):
    n = len(grads)

    def body(*refs):
        g_in, r_in, small_ref, small_sib_ref = refs[:n], refs[n:2 * n], refs[2 * n], refs[2 * n + 1]
        outs = refs[2 * n + 2:3 * n + 3]
        scratch = refs[3 * n + 3:]
        g_buf, r_buf, p_buf = scratch[:n], scratch[n:2 * n], scratch[2 * n:3 * n]
        s_buf, ss_buf, sp_buf, load_sems, send_sems, recv_sems, local_sems = scratch[3 * n:]
        x, y, c = _mesh_pos()
        me = 2 * x + y
        chips = _other_chips(x, y)

        loads = []
        for i in range(n):
            hr = from_sibling[i].shape[1]
            pair = (pltpu.make_async_copy(g_in[i].at[:, pl.ds(c * hr, hr), :], g_buf[i], load_sems.at[2 * i]),
                    pltpu.make_async_copy(r_in[i], r_buf[i], load_sems.at[2 * i + 1]))
            loads.append(pair)
        loads.append((pltpu.make_async_copy(small_ref, s_buf, load_sems.at[2 * n]),
                      pltpu.make_async_copy(small_sib_ref, ss_buf, load_sems.at[2 * n + 1])))
        for pair in loads:
            for cp in pair:
                cp.start()

        locals_, sends = [], []
        for i in range(n + 1):
            for cp in loads[i]:
                cp.wait()
            if i < n:
                p_buf[i][...] = (g_buf[i][...] + r_buf[i][...]).astype(BF16)
                part = lambda k, i=i: p_buf[i].at[k]
            else:
                sp_buf[...] = s_buf[...] + ss_buf[...]
                part = lambda k: sp_buf
            mine = pltpu.make_async_copy(part(me), outs[i].at[me], local_sems.at[i])
            mine.start()
            locals_.append(mine)
            for j, (cx, cy) in enumerate(chips):
                cp = _remote(part(2 * cx + cy), outs[i].at[me], send_sems, recv_sems, 3 * i + j, (cx, cy, c))
                cp.start()
                sends.append(cp)
        for i in range(n + 1):
            for j, (cx, cy) in enumerate(chips):
                got = outs[i].at[2 * cx + cy]
                _remote(got, got, send_sems, recv_sems, 3 * i + j, (cx, cy, c)).wait_recv()
        for cp in sends:
            cp.wait_send()
        for cp in locals_:
            cp.wait()

    out_shape = [jax.ShapeDtypeStruct(r.shape, BF16) for r in from_sibling]
    out_shape.append(jax.ShapeDtypeStruct((N_CHIPS,) + small.shape, F32))
    scratch = [pltpu.VMEM(r.shape, F32) for r in from_sibling] * 2 + [pltpu.VMEM(r.shape, BF16) for r in from_sibling]
    scratch += [pltpu.VMEM(small.shape, F32)] * 3
    scratch += [pltpu.SemaphoreType.DMA((2 * n + 2,)), pltpu.SemaphoreType.DMA((3 * n + 3,)),
                pltpu.SemaphoreType.DMA((3 * n + 3,)), pltpu.SemaphoreType.DMA((n + 1,))]
    outs = pl.pallas_call(
        body, name="scatter_grads", in_specs=[_ANY] * (2 * n + 2), out_specs=[_ANY] * (n + 1), out_shape=out_shape,
        scratch_shapes=scratch, compiler_params=_params(),
    )(*grads, *from_sibling, small, small_sibling)
    return outs[:n], outs[n]


def _share_halves(halves):
    n = len(halves)

    def body(*refs):
        ins, outs, send_sems, recv_sems = refs[:n], refs[n:2 * n], refs[2 * n], refs[2 * n + 1]
        x, y, c = _mesh_pos()
        cps = [_remote(ins[i], outs[i], send_sems, recv_sems, i, (x, y, 1 - c)) for i in range(n)]
        for cp in cps:
            cp.start()
        for cp in cps:
            cp.wait()

    return pl.pallas_call(
        body, name="share_halves", in_specs=[_ANY] * n, out_specs=[_ANY] * n,
        out_shape=[jax.ShapeDtypeStruct(h.shape, h.dtype) for h in halves],
        scratch_shapes=[pltpu.SemaphoreType.DMA((n,)), pltpu.SemaphoreType.DMA((n,))],
    )(*halves)


SHARD_COLS_IN = IN_TOTAL // N_CHIPS
KPE_END = Q_LORA + KV_LORA + ROPE


def _by_cols(a):
    return a.transpose(1, 0, 2).reshape(a.shape[1], N_CHIPS * a.shape[2])


def _assemble_early(c_in, c_uq, c_ukv, c_conv):
    w_in_e = jnp.concatenate([c_in[0][:, :KPE_END], jnp.zeros((D_MODEL, 64), BF16), c_in[0][:, KPE_END:],
                              c_in[1], c_in[2], c_in[3]], axis=1)
    w_uq_e = _by_cols(jnp.pad(c_uq, ((0, 0), (0, 0), (0, HEAD_PAD - QK_DIM))))
    return w_in_e, w_uq_e, _by_cols(c_ukv), _by_cols(c_conv).astype(F32)


def _assemble_late(c_o, c_pl, c_plg):
    return c_o.reshape(D_MODEL, D_MODEL), _by_cols(c_pl), c_plg.reshape(D_MODEL, D_MODEL)


def _split_grads(dw_in_e, dw_uq_e, dw_ukv, dw_o, dw_pl, dw_plg):
    chip_major = lambda a: a.reshape(a.shape[0], N_CHIPS, a.shape[1] // N_CHIPS).transpose(1, 0, 2)
    first = jnp.concatenate([dw_in_e[:, :KPE_END], dw_in_e[:, KPE_END + 64:SHARD_COLS_IN + 64]], axis=1)
    rest = [dw_in_e[:, SHARD_COLS_IN * k + 64:SHARD_COLS_IN * (k + 1) + 64] for k in range(1, N_CHIPS)]
    return [jnp.stack([first] + rest), chip_major(dw_uq_e)[:, :, :QK_DIM], chip_major(dw_ukv),
            dw_o.reshape(N_CHIPS, D_MODEL // N_CHIPS, D_MODEL), chip_major(dw_pl),
            dw_plg.reshape(N_CHIPS, D_MODEL // N_CHIPS, D_MODEL)]


def _local_step(x, p, pos, tgt, gains, early, late_shards, late_gathered, tm, tq):
    w_in_e, w_uq_e, w_ukv, conv_w = early
    g_in, g_cq, g_ckv, g_q, g_k, g_oa, g_oc, g_pl = gains
    T = x.shape[0]
    zpad = lambda a, n: jnp.concatenate([a, jnp.zeros(a.shape[:-1] + (n,), a.dtype)], axis=-1)
    gq, gk = zpad(g_q, HEAD_PAD - QK_DIM), zpad(g_k, HEAD_PAD - QK_DIM)
    inv_freq = 1.0 / (ROPE_THETA ** (jnp.arange(0, ROPE, 2, dtype=F32) / ROPE))
    invf = jnp.concatenate([inv_freq, inv_freq, jnp.zeros((64,), F32)]).reshape(1, LANES)
    sgn = jnp.concatenate([-jnp.ones((32,), F32), jnp.ones((32,), F32), jnp.zeros((64,), F32)]).reshape(1, LANES)

    (proj, q, k, v), gathered = _fwd_proj(x, pos, g_in, w_in_e, g_cq, w_uq_e, g_ckv, w_ukv, gq, gk, invf, sgn,
                                          late_shards, min(2 * tm, T))
    w_o, w_pl, w_plg = _assemble_late(*(gathered if late_shards else late_gathered))
    o, lse = _attn_fwd(q, k, v, tq)
    (dx1, do, delta, dtail, du, dw_o, dw_pl, dw_plg, dg_oa, dg_oc, dg_pl, dconv, loss) = _tail(
        x, o, proj, p, tgt, g_oa, g_oc, g_pl, conv_w, w_o, w_pl, w_plg, tm)
    dq, dk, dv = _attn_bwd(q, k, v, do, lse, delta, tq)
    (gx, h, dproj, dw_uq_e, dw_ukv, dg_in, dg_cq, dg_ckv, dgq, dgk) = _bwd_proj(
        x, dx1, pos, proj, dq, dk, dv, dtail, du, g_in, w_in_e, g_cq, w_uq_e, g_ckv, w_ukv, gq, gk, conv_w, invf, sgn,
        min(2 * tm, T))
    dw_in_e = _matmul_acc(h, dproj, min(4096, T), 512)
    wgrads = (dw_in_e, dw_uq_e, dw_ukv, dw_o, dw_pl, dw_plg)
    ggrads = (dg_in, dg_cq, dg_ckv, dgq, dgk, dg_oa, dg_oc, dg_pl)
    return loss, gx, wgrads, ggrads, dconv


def kernel(x, p, positions, g_in, w_in, g_cq, w_uq, g_ckv, w_ukv, g_q, g_k, conv_w, g_oa, g_oc, w_o, w_pl, w_plg, g_pl, loss_target, m_g_in, m_w_in, m_g_cq, m_w_uq, m_g_ckv, m_w_ukv, m_g_q, m_g_k, m_conv_w, m_g_oa, m_g_oc, m_w_o, m_w_pl, m_w_plg, m_g_pl, v_g_in, v_w_in, v_g_cq, v_w_uq, v_g_ckv, v_w_ukv, v_g_q, v_g_k, v_conv_w, v_g_oa, v_g_oc, v_w_o, v_w_pl, v_w_plg, v_g_pl):
    T = x.shape[1]
    c = lax.axis_index("c")
    chip = 2 * lax.axis_index("x") + lax.axis_index("y")
    gains = [g.reshape(1, -1) for g in (g_in, g_cq, g_ckv, g_q, g_k, g_oa, g_oc, g_pl)]

    early = _assemble_early(*_gather_weights([w_in[0], w_uq[0], w_ukv[0], conv_w[0]]))

    loss, gx, wgrads, ggrads, dconv = _local_step(
        x[0], p[0, 0], positions.reshape(T, 1), loss_target[0], gains, early, [w_o[0], w_pl[0], w_plg[0]], None, 256, 512)

    grads_cm = _split_grads(*wgrads)
    small_parts = [a.reshape(-1, LANES) for a in (*ggrads, loss, dconv)]
    small_rows = [a.shape[0] for a in small_parts]
    tile_rows = [-(-r // 8) * 8 for r in small_rows]
    tile_rows[-1] += -sum(tile_rows) % 16
    small = jnp.concatenate([jnp.pad(a, ((0, t - r), (0, 0))) for a, r, t in zip(small_parts, small_rows, tile_rows)])
    from_sibling, small_sibling = _swap_halves(grads_cm, small)
    by_chip, small_by_chip = _scatter_to_chips(grads_cm, from_sibling, small, small_sibling)
    halves, small_total = _add_chips(by_chip, small_by_chip)
    other_halves = _share_halves(halves)

    gg, off = [], 0
    for rows, tiled in zip(small_rows, tile_rows):
        gg.append(small_total[off:off + rows].reshape(1, -1))
        off += tiled
    loss_out = gg[8][0, 0]
    conv_total = gg[9].reshape(3, CONV_W)
    conv_g = lax.dynamic_slice(conv_total, (0, chip * (CONV_W // N_CHIPS)), (3, CONV_W // N_CHIPS))
    g_by_name = dict(g_in=gg[0], g_cq=gg[1], g_ckv=gg[2], g_q=gg[3][:, :QK_DIM], g_k=gg[4][:, :QK_DIM], conv_w=conv_g,
                     g_oa=gg[5], g_oc=gg[6], g_pl=gg[7])
    half_by_name = dict(zip(("w_in", "w_uq", "w_ukv", "w_o", "w_pl", "w_plg"), zip(halves, other_halves)))
    weights = dict(g_in=g_in, w_in=w_in, g_cq=g_cq, w_uq=w_uq, g_ckv=g_ckv, w_ukv=w_ukv, g_q=g_q, g_k=g_k,
                   conv_w=conv_w, g_oa=g_oa, g_oc=g_oc, w_o=w_o, w_pl=w_pl, w_plg=w_plg, g_pl=g_pl)
    ms = dict(g_in=m_g_in, w_in=m_w_in, g_cq=m_g_cq, w_uq=m_w_uq, g_ckv=m_g_ckv, w_ukv=m_w_ukv, g_q=m_g_q, g_k=m_g_k,
              conv_w=m_conv_w, g_oa=m_g_oa, g_oc=m_g_oc, w_o=m_w_o, w_pl=m_w_pl, w_plg=m_w_plg, g_pl=m_g_pl)
    vs = dict(g_in=v_g_in, w_in=v_w_in, g_cq=v_g_cq, w_uq=v_w_uq, g_ckv=v_g_ckv, w_ukv=v_w_ukv, g_q=v_g_q, g_k=v_g_k,
              conv_w=v_conv_w, g_oa=v_g_oa, g_oc=v_g_oc, w_o=v_w_o, w_pl=v_w_pl, w_plg=v_w_plg, g_pl=v_g_pl)
    names = list(weights)
    grads, deltas, new_m, new_v = [], [], [], []
    for n in names:
        w = weights[n]
        w2 = w.reshape(-1, w.shape[-1])
        if n in half_by_name:
            g2, d, nm, nv = _adamw_halves(w2, *half_by_name[n], ms[n].reshape(w2.shape), vs[n].reshape(w2.shape), c,
                                          "adamw_" + n)
        else:
            g2 = g_by_name[n].reshape(w2.shape)
            d, nm, nv = _adamw(w2, g2, ms[n].reshape(w2.shape), vs[n].reshape(w2.shape), "adamw_" + n)
        grads.append(g2.reshape(w.shape))
        deltas.append(d.reshape(w.shape))
        new_m.append(nm.reshape(w.shape))
        new_v.append(nv.reshape(w.shape))
    return (loss_out, gx.reshape(x.shape), *grads, *deltas, *new_m, *new_v)
```

```python
import functools
import math

import jax
import jax.numpy as jnp
from jax import lax
from jax.experimental import pallas as pl
from jax.experimental.pallas import tpu as pltpu

F32 = jnp.float32
BF16 = jnp.bfloat16

D_MODEL = 1024
N_HEADS = 4
NOPE = 128
ROPE = 64
V_DIM = 128
QK_DIM = NOPE + ROPE
HEAD_PAD = 256
Q_LORA = 256
KV_LORA = 128
ATTN_W = 512
CONV_W = 512
PLE = 256
IN_TOTAL = 3008
PROJ_EXT = 3072
ROPE_THETA = 10000.0
EPS = 1e-6
SCALE = 1.0 / math.sqrt(QK_DIM)
LOG2E = math.log2(math.e)
EXP2_SCALE = SCALE * LOG2E
NEG = -1e30
SOFTMAX_ROWS = 32
SUB_TILE = 256

LR, B1, B2, ADAM_EPS, WD, STEP = 0.001, 0.9, 0.999, 1e-08, 0.01, 10

N_CHIPS = 4
LANES = 128
VMEM_LIMIT = 56 * 1024 * 1024
MESH = pl.DeviceIdType.MESH


def _params(**kw):
    return pltpu.CompilerParams(vmem_limit_bytes=VMEM_LIMIT, **kw)


def _inv_rms(x, n):
    return lax.rsqrt(jnp.sum(x * x, axis=-1, keepdims=True) / n + EPS)


def _lane_sum(a):
    folded = a[:, 0:LANES]
    for c0 in range(LANES, a.shape[1], LANES):
        folded = folded + a[:, c0:c0 + LANES]
    head = folded.astype(BF16)
    tail = (folded - head.astype(F32)).astype(BF16)
    return _dot(jnp.concatenate([head, tail], axis=1), jnp.ones((2 * LANES, LANES), BF16))


def _inv_rms_mxu(x):
    return lax.rsqrt(_lane_sum(x * x) / x.shape[1] + EPS)


def _rep(r, width):
    return r if width == LANES else jnp.tile(r, (1, width // LANES))


def _sigmoid(z):
    return 1.0 / (1.0 + jnp.exp(-z))


def _swap_rope_halves(b):
    lane = lax.broadcasted_iota(jnp.int32, b.shape, 1)
    swapped = jnp.where(lane < 32, pltpu.roll(b, 96, 1), pltpu.roll(b, 32, 1))
    return jnp.where(lane < ROPE, swapped, 0.0)


def _dot(a, b):
    return jnp.dot(a, b, preferred_element_type=F32)


def _dot_nt(a, b):
    return lax.dot_general(a, b, (((1,), (1,)), ((), ())), preferred_element_type=F32)


def _dot_tn(a, b):
    return lax.dot_general(a, b, (((0,), (0,)), ((), ())), preferred_element_type=F32)


def _colsum(a):
    return jnp.sum(a, axis=0, keepdims=True)


def _full(shape):
    return pl.BlockSpec(shape, lambda *_: (0,) * len(shape))


def _round_robin(chains, width):
    waiting, active = list(chains), []
    while waiting or active:
        while waiting and len(active) < width:
            active.append(waiting.pop(0))
        for chain in list(active):
            if next(chain, _DONE) is _DONE:
                active.remove(chain)


_DONE = object()


def _rope_tables(pos_ref, invf_ref, sgn_ref):
    ang = pos_ref[...].astype(F32) * invf_ref[...]
    return jnp.cos(ang), jnp.sin(ang) * sgn_ref[...]


def _fwd_proj(x, pos, g_in, w_in, g_cq, w_uq, g_ckv, w_ukv, gq, gk, invf, sgn, late_shards, tm):
    T = x.shape[0]
    nt = T // tm
    n_late = len(late_shards)
    ts = min(SUB_TILE, tm)

    def body(x_ref, pos_ref, g_in_ref, w_in_ref, g_cq_ref, w_uq_ref, g_ckv_ref, w_ukv_ref, gq_ref, gk_ref,
             invf_ref, sgn_ref, *rest):
        late_in, (proj_ref, q_ref, k_ref, v_ref) = rest[:n_late], rest[n_late:n_late + 4]
        late_out, late_scratch = rest[n_late + 4:2 * n_late + 4], rest[2 * n_late + 4:]
        i = pl.program_id(0)
        if n_late:
            start, forward, drain = _gather_steps([s.shape for s in late_shards], late_in, late_out,
                                                  late_scratch[:n_late], *late_scratch[n_late:])
            pl.when(i == 0)(start)
            pl.when(i == nt // 2)(forward)

        def sub_tile(rows):
            xv = x_ref[rows, :]
            h = (xv * _rep(_inv_rms_mxu(xv), D_MODEL) * g_in_ref[...]).astype(BF16)
            yield

            def project(c0):
                proj_ref[rows, c0:c0 + 512] = _dot(h, w_in_ref[:, c0:c0 + 512])

            lat = _dot(h, w_in_ref[:, 0:512])
            proj_ref[rows, 0:512] = lat
            yield
            c_q = lat[:, 0:Q_LORA]
            cqn = (c_q * _rep(_inv_rms_mxu(c_q), Q_LORA) * g_cq_ref[...]).astype(BF16)
            c_kv = lat[:, Q_LORA:Q_LORA + KV_LORA]
            ckvn = (c_kv * _inv_rms_mxu(c_kv) * g_ckv_ref[...]).astype(BF16)
            kpe = lat[:, 384:512]
            kpe_sq = kpe * kpe
            cos_b, sin_b = _rope_tables(pos_ref.at[rows, :], invf_ref, sgn_ref)
            gq_a, gq_b = gq_ref[:, 0:NOPE], gq_ref[:, NOPE:HEAD_PAD]
            gk_a, gk_b = gk_ref[:, 0:NOPE], gk_ref[:, NOPE:HEAD_PAD]
            yield
            for hd in range(N_HEADS):
                project(512 * (hd + 1))
                yield
                c0 = hd * HEAD_PAD
                qh = _dot(cqn, w_uq_ref[:, c0:c0 + HEAD_PAD])
                yield
                a, b = qh[:, 0:NOPE], qh[:, NOPE:HEAD_PAD]
                r = lax.rsqrt(_lane_sum(a * a + b * b) / QK_DIM + EPS)
                yield
                bn = b * r * gq_b
                q_ref[hd, rows, 0:NOPE] = (a * r * gq_a).astype(BF16)
                q_ref[hd, rows, NOPE:HEAD_PAD] = (bn * cos_b + _swap_rope_halves(bn) * sin_b).astype(BF16)
                yield
                kvh = _dot(ckvn, w_ukv_ref[:, c0:c0 + HEAD_PAD])
                yield
                ka = kvh[:, 0:NOPE]
                rk = lax.rsqrt(_lane_sum(ka * ka + kpe_sq) / QK_DIM + EPS)
                yield
                kbn = kpe * rk * gk_b
                k_ref[hd, rows, 0:NOPE] = (ka * rk * gk_a).astype(BF16)
                k_ref[hd, rows, NOPE:HEAD_PAD] = (kbn * cos_b + _swap_rope_halves(kbn) * sin_b).astype(BF16)
                v_ref[hd, rows, 0:V_DIM] = kvh[:, NOPE:HEAD_PAD].astype(BF16)
                v_ref[hd, rows, V_DIM:2 * V_DIM] = jnp.ones((ts, V_DIM), BF16)
                yield
            project(512 * (N_HEADS + 1))

        _round_robin([sub_tile(slice(r0, r0 + ts)) for r0 in range(0, tm, ts)], tm // ts)

        if n_late:
            pl.when(i == nt - 1)(drain)

    row = lambda i: (i, 0)
    head_rows = lambda i: (0, i, 0)
    outs = pl.pallas_call(
        body, name="fwd_proj", grid=(nt,),
        in_specs=[pl.BlockSpec((tm, D_MODEL), row), pl.BlockSpec((tm, 1), row), _full((1, D_MODEL)),
                  _full((D_MODEL, PROJ_EXT)), _full((1, Q_LORA)), _full((Q_LORA, N_HEADS * HEAD_PAD)),
                  _full((1, KV_LORA)), _full((KV_LORA, N_HEADS * HEAD_PAD)), _full((1, HEAD_PAD)), _full((1, HEAD_PAD)),
                  _full((1, LANES)), _full((1, LANES))] + [_full(s.shape) for s in late_shards],
        out_specs=[pl.BlockSpec((tm, PROJ_EXT), row), pl.BlockSpec((N_HEADS, tm, HEAD_PAD), head_rows),
                   pl.BlockSpec((N_HEADS, tm, HEAD_PAD), head_rows), pl.BlockSpec((N_HEADS, tm, 2 * V_DIM), head_rows)]
                  + [_ANY] * n_late,
        out_shape=[jax.ShapeDtypeStruct((T, PROJ_EXT), F32), jax.ShapeDtypeStruct((N_HEADS, T, HEAD_PAD), BF16),
                   jax.ShapeDtypeStruct((N_HEADS, T, HEAD_PAD), BF16), jax.ShapeDtypeStruct((N_HEADS, T, 2 * V_DIM), BF16)]
                  + _gathered_shapes(late_shards),
        scratch_shapes=_gather_scratch(late_shards) if n_late else [],
        compiler_params=_params(dimension_semantics=("arbitrary",)),
    )(x, pos, g_in, w_in, g_cq, w_uq, g_ckv, w_ukv, gq, gk, invf, sgn, *late_shards)
    return outs[:4], outs[4:]


def _chunk_pipeline(n_loop, lag, matmuls, pointwise, accumulate, last):
    slots = lag + 1

    def iteration(t, slot):
        matmuls(jnp.minimum(t + lag, n_loop), (slot + lag) % slots)
        accumulate(jnp.maximum(t - lag, 0), (slot + 1) % slots)
        pointwise(t, slot, False)

    def finish(slot):
        for back in range(lag, 0, -1):
            accumulate(jnp.maximum(n_loop - back, 0), (slot - back) % slots)
        pointwise(n_loop, slot, True)
        accumulate(n_loop, slot)
        last()

    for u in range(lag):
        matmuls(jnp.minimum(u, n_loop), u)

    def unrolled(tt, carry):
        for slot in range(slots):
            iteration(slots * tt + slot, slot)
        return carry

    lax.fori_loop(0, n_loop // slots, unrolled, 0)
    rest = lax.rem(n_loop, slots)
    t0 = n_loop - rest

    for r in range(slots):
        @pl.when(rest == r)
        def _():
            for slot in range(r):
                iteration(t0 + slot, slot)
            finish(r)


def _attn_fwd(q, k, v, tq):
    T = q.shape[1]
    tk = tq
    rc = min(SOFTMAX_ROWS, tq)

    def body(q_ref, k_ref, v_ref, o_ref, lse_ref, s0, s1, s2, p0, p1, p2, a0, a1, a2, m_ref, acc_ref):
        qi = pl.program_id(1)
        s_buf, p_buf, a_buf = (s0, s1, s2), (p0, p1, p2), (a0, a1, a2)

        def scores(t, slot):
            ks = pl.multiple_of(t * tk, tk)
            s_buf[slot][...] = _dot_nt(q_ref[0], k_ref[0, pl.ds(ks, tk), :])

        def values(t, slot):
            ks = pl.multiple_of(t * tk, tk)
            acc_ref[...] = acc_ref[...] * a_buf[slot][...] + _dot(p_buf[slot][...], v_ref[0, pl.ds(ks, tk), :])

        def softmax(t, slot, masked):
            s_all = s_buf[slot][...]
            if masked:
                row = lax.broadcasted_iota(jnp.int32, (tq, tk), 0)
                col = lax.broadcasted_iota(jnp.int32, (tq, tk), 1)
                s_all = jnp.where(col <= row, s_all, NEG)
                s_buf[slot][...] = s_all
            m_old = m_ref[...]
            m_new = jnp.maximum(m_old, jnp.max(s_all, axis=1, keepdims=True))
            a_buf[slot][...] = jnp.exp2((m_old - m_new) * EXP2_SCALE)
            m_ref[...] = m_new
            for r0 in range(0, tq, rc):
                s = s_buf[slot][r0:r0 + rc, :]
                p_buf[slot][r0:r0 + rc, :] = jnp.exp2((s - m_new[r0:r0 + rc, :]) * EXP2_SCALE).astype(BF16)

        def last():
            l = acc_ref[:, V_DIM:2 * V_DIM]
            o_ref[...] = acc_ref[:, 0:V_DIM] / l
            lse_ref[0] = (m_ref[...] * SCALE + jnp.log(l)).T[0:1, :]

        m_ref[...] = jnp.full_like(m_ref, NEG)
        acc_ref[...] = jnp.zeros_like(acc_ref)
        for p_late, a_late in ((p1, a1), (p2, a2)):
            p_late[...] = jnp.zeros_like(p_late)
            a_late[...] = jnp.ones_like(a_late)
        _chunk_pipeline(qi, 2, scores, softmax, values, last)

    return pl.pallas_call(
        body, name="attn_fwd", grid=(N_HEADS, T // tq),
        in_specs=[pl.BlockSpec((1, tq, HEAD_PAD), lambda h, i: (h, i, 0)),
                  pl.BlockSpec((1, T, HEAD_PAD), lambda h, i: (h, 0, 0)),
                  pl.BlockSpec((1, T, 2 * V_DIM), lambda h, i: (h, 0, 0))],
        out_specs=[pl.BlockSpec((tq, V_DIM), lambda h, i: (i, h)),
                   pl.BlockSpec((1, 1, tq), lambda h, i: (h, 0, i))],
        out_shape=[jax.ShapeDtypeStruct((T, ATTN_W), F32), jax.ShapeDtypeStruct((N_HEADS, 1, T), F32)],
        scratch_shapes=[pltpu.VMEM((tq, tk), F32)] * 3 + [pltpu.VMEM((tq, tk), BF16)] * 3
                       + [pltpu.VMEM((tq, 1), F32)] * 4 + [pltpu.VMEM((tq, 2 * V_DIM), F32)],
        compiler_params=_params(dimension_semantics=("arbitrary", "arbitrary")),
    )(q, k, v)


def _tail(x, o, proj, p, tgt, g_oa, g_oc, g_pl, conv_w, w_o, w_pl, w_plg, tm):
    T = x.shape[0]
    nt = T // tm

    def body(x_ref, o_ref, za_ref, cb_ref, cc_ref, cx_ref, zc_ref, cch_ref, cxh_ref, p_ref, tgt_ref,
             g_oa_ref, g_oc_ref, g_pl_ref, cw_ref, w_o_ref, w_pl_ref, w_plg_ref,
             dx1_ref, do_ref, delta_ref, dtail_ref, du_ref,
             dw_o_ref, dw_pl_ref, dw_plg_ref, dg_oa_ref, dg_oc_ref, dg_pl_ref, dcw_ref, loss_ref):
        i = pl.program_id(0)

        @pl.when(i == 0)
        def _():
            for r in (dw_o_ref, dw_pl_ref, dw_plg_ref, dg_oa_ref, dg_oc_ref, dg_pl_ref, dcw_ref, loss_ref):
                r[...] = jnp.zeros_like(r)

        xv, ov, za, cb, zc = x_ref[...], o_ref[...], za_ref[...], cb_ref[...], zc_ref[...]
        g_oa, g_oc, g_pl = g_oa_ref[...], g_oc_ref[...], g_pl_ref[...]
        w0, w1, w2 = cw_ref[0:1, :], cw_ref[1:2, :], cw_ref[2:3, :]

        pb = p_ref[...].astype(BF16)
        pp = _dot(pb, w_pl_ref[...])

        sa = _sigmoid(za)
        silu_a = za * sa
        ga = ov * silu_a
        ra = _inv_rms(ga, ATTN_W)
        xa = ga * ra
        ya = xa * g_oa
        v = cc_ref[...] * cx_ref[...]
        not_first = jnp.where(i > 0, 1.0, 0.0)
        hv6 = cch_ref[6:7, :] * cxh_ref[6:7, :] * not_first
        hv7 = cch_ref[7:8, :] * cxh_ref[7:8, :] * not_first
        row = lax.broadcasted_iota(jnp.int32, v.shape, 0)
        v1 = jnp.where(row == 0, hv7, pltpu.roll(v, 1, 0))
        v2 = jnp.where(row == 0, hv6, jnp.where(row == 1, hv7, pltpu.roll(v, 2, 0)))
        u = w0 * v2 + w1 * v1 + w2 * v
        sc = _sigmoid(zc)
        silu_c = zc * sc
        gc = cb * u * silu_c
        rc = _inv_rms(gc, CONV_W)
        xc = gc * rc
        yc = xc * g_oc
        ycat = jnp.concatenate([ya, yc], axis=-1).astype(BF16)
        x1 = xv + _dot(ycat, w_o_ref[...])
        r1 = _inv_rms(x1, D_MODEL)
        xh1 = x1 * r1
        n1 = (xh1 * g_pl).astype(BF16)
        gate = _sigmoid(_dot(n1, w_plg_ref[...]))
        err = x1 + gate * pp - tgt_ref[...]
        loss_ref[...] += 0.5 * jnp.sum(err * err) / D_MODEL
        dy = err / D_MODEL

        dpp = (dy * gate).astype(BF16)
        da = (dy * pp * gate * (1.0 - gate)).astype(BF16)
        dn1 = _dot_nt(da, w_plg_ref[...])
        dw_pl_ref[...] += _dot_tn(pb, dpp)
        dw_plg_ref[...] += _dot_tn(n1, da)
        dg_pl_ref[...] += _colsum(dn1 * xh1)
        dxh = dn1 * g_pl
        dx1 = dy + r1 * (dxh - xh1 * (jnp.sum(dxh * xh1, axis=-1, keepdims=True) / D_MODEL))
        dx1_ref[...] = dx1
        dx1b = dx1.astype(BF16)
        dycat = _dot_nt(dx1b, w_o_ref[...])
        dya, dyc = dycat[:, 0:ATTN_W], dycat[:, ATTN_W:D_MODEL]

        dw_o_ref[0:ATTN_W, :] += _dot_tn(ycat[:, 0:ATTN_W], dx1b)
        dg_oa_ref[...] += _colsum(dya * xa)
        dxa = dya * g_oa
        dga = ra * (dxa - xa * (jnp.sum(dxa * xa, axis=-1, keepdims=True) / ATTN_W))
        do = (dga * silu_a).astype(BF16)
        do_ref[...] = do
        dof = do.astype(F32) * ov
        for hd in range(N_HEADS):
            delta_ref[hd] = _lane_sum(dof[:, hd * V_DIM:(hd + 1) * V_DIM]).T[0:1, :]
        dtail_ref[:, 0:512] = (dga * ov * (sa * (1.0 + za * (1.0 - sa)))).astype(BF16)

        dw_o_ref[ATTN_W:D_MODEL, :] += _dot_tn(ycat[:, ATTN_W:D_MODEL], dx1b)
        dg_oc_ref[...] += _colsum(dyc * xc)
        dxc = dyc * g_oc
        dgc = rc * (dxc - xc * (jnp.sum(dxc * xc, axis=-1, keepdims=True) / CONV_W))
        dtail_ref[:, 512:1024] = (dgc * u * silu_c).astype(BF16)
        du = dgc * cb * silu_c
        du_ref[...] = du
        dtail_ref[:, 1024:1536] = (dgc * cb * u * (sc * (1.0 + zc * (1.0 - sc)))).astype(BF16)
        dcw_ref[0:1, :] += _colsum(du * v2)
        dcw_ref[1:2, :] += _colsum(du * v1)
        dcw_ref[2:3, :] += _colsum(du * v)

    row = lambda i: (i, 0)
    col = lambda c: (lambda i: (i, c))
    halo = lambda c: (lambda i: (jnp.maximum(i * (tm // 8) - 1, 0), c))
    in_specs = [pl.BlockSpec((tm, D_MODEL), row), pl.BlockSpec((tm, ATTN_W), row)]
    in_specs += [pl.BlockSpec((tm, 512), col(c)) for c in (1, 2, 3, 4, 5)]
    in_specs += [pl.BlockSpec((8, 512), halo(3)), pl.BlockSpec((8, 512), halo(4))]
    in_specs += [pl.BlockSpec((tm, PLE), row), pl.BlockSpec((tm, D_MODEL), row),
                 _full((1, ATTN_W)), _full((1, CONV_W)), _full((1, D_MODEL)), _full((3, CONV_W)),
                 _full((D_MODEL, D_MODEL)), _full((PLE, D_MODEL)), _full((D_MODEL, D_MODEL))]
    out_specs = [pl.BlockSpec((tm, D_MODEL), row), pl.BlockSpec((tm, ATTN_W), row),
                 pl.BlockSpec((N_HEADS, 1, tm), lambda i: (0, 0, i)), pl.BlockSpec((tm, 1536), row),
                 pl.BlockSpec((tm, CONV_W), row),
                 _full((D_MODEL, D_MODEL)), _full((PLE, D_MODEL)), _full((D_MODEL, D_MODEL)),
                 _full((1, ATTN_W)), _full((1, CONV_W)), _full((1, D_MODEL)), _full((3, CONV_W)), _full((1, LANES))]
    out_shape = [jax.ShapeDtypeStruct((T, D_MODEL), F32), jax.ShapeDtypeStruct((T, ATTN_W), BF16),
                 jax.ShapeDtypeStruct((N_HEADS, 1, T), F32), jax.ShapeDtypeStruct((T, 1536), BF16),
                 jax.ShapeDtypeStruct((T, CONV_W), F32),
                 jax.ShapeDtypeStruct((D_MODEL, D_MODEL), F32), jax.ShapeDtypeStruct((PLE, D_MODEL), F32),
                 jax.ShapeDtypeStruct((D_MODEL, D_MODEL), F32),
                 jax.ShapeDtypeStruct((1, ATTN_W), F32), jax.ShapeDtypeStruct((1, CONV_W), F32),
                 jax.ShapeDtypeStruct((1, D_MODEL), F32), jax.ShapeDtypeStruct((3, CONV_W), F32),
                 jax.ShapeDtypeStruct((1, LANES), F32)]
    return pl.pallas_call(
        body, name="tail", grid=(nt,), in_specs=in_specs, out_specs=out_specs, out_shape=out_shape,
        compiler_params=_params(dimension_semantics=("arbitrary",)),
    )(x, o, proj, proj, proj, proj, proj, proj, proj, p, tgt, g_oa, g_oc, g_pl, conv_w, w_o, w_pl, w_plg)


def _attn_bwd(q, k, v, do, lse_row, delta_row, tk):
    T = q.shape[1]
    tq = tk
    nq = T // tq
    rc = min(SOFTMAX_ROWS, tk)

    def body(q_ref, k_ref, v_ref, do_ref, lse_ref, dl_ref, dq_ref, dk_ref, dv_ref,
             s0, s1, d0, d1, p0, p1, g0, g1, dk_acc, dv_acc):
        kj = pl.program_id(1)
        s_buf, dp_buf, p_buf, g_buf = (s0, s1), (d0, d1), (p0, p1), (g0, g1)

        @pl.when(kj == 0)
        def _():
            dq_ref[...] = jnp.zeros_like(dq_ref)

        def q_start(t):
            return pl.multiple_of((nq - 1 - t) * tq, tq)

        def matmuls(t, slot):
            qs = q_start(t)
            s_buf[slot][...] = _dot_nt(k_ref[0], q_ref[0, pl.ds(qs, tq), :])
            dp_buf[slot][...] = _dot_nt(v_ref[0], do_ref[pl.ds(qs, tq), :])

        def pointwise(t, slot, masked):
            qs = q_start(t)
            lse2 = lse_ref[0, :, pl.ds(qs, tq)] * LOG2E
            dl = dl_ref[0, :, pl.ds(qs, tq)]
            for r0 in range(0, tk, rc):
                st = s_buf[slot][r0:r0 + rc, :]
                if masked:
                    row = lax.broadcasted_iota(jnp.int32, (rc, tq), 0)
                    col = lax.broadcasted_iota(jnp.int32, (rc, tq), 1)
                    st = jnp.where(row + r0 <= col, st, NEG)
                pt = jnp.exp2(st * EXP2_SCALE - lse2)
                p_buf[slot][r0:r0 + rc, :] = pt.astype(BF16)
                g_buf[slot][r0:r0 + rc, :] = (pt * (dp_buf[slot][r0:r0 + rc, :] - dl) * SCALE).astype(BF16)

        def accumulate(t, slot):
            qs = q_start(t)
            dv_acc[...] += _dot(p_buf[slot][...], do_ref[pl.ds(qs, tq), :])
            dk_acc[...] += _dot(g_buf[slot][...], q_ref[0, pl.ds(qs, tq), :])
            dq_ref[0, pl.ds(qs, tq), :] += _dot_tn(g_buf[slot][...], k_ref[0])

        def last():
            dk_ref[0] = dk_acc[...]
            dv_ref[0] = dv_acc[...]

        dk_acc[...] = jnp.zeros_like(dk_acc)
        dv_acc[...] = jnp.zeros_like(dv_acc)
        for late in (p1, g1):
            late[...] = jnp.zeros_like(late)
        _chunk_pipeline(nq - 1 - kj, 1, matmuls, pointwise, accumulate, last)

    return pl.pallas_call(
        body, name="attn_bwd", grid=(N_HEADS, T // tk),
        in_specs=[pl.BlockSpec((1, T, HEAD_PAD), lambda h, j: (h, 0, 0)),
                  pl.BlockSpec((1, tk, HEAD_PAD), lambda h, j: (h, j, 0)),
                  pl.BlockSpec((1, tk, V_DIM), lambda h, j: (h, j, 0)),
                  pl.BlockSpec((T, V_DIM), lambda h, j: (0, h)),
                  pl.BlockSpec((1, 1, T), lambda h, j: (h, 0, 0)),
                  pl.BlockSpec((1, 1, T), lambda h, j: (h, 0, 0))],
        out_specs=[pl.BlockSpec((1, T, HEAD_PAD), lambda h, j: (h, 0, 0)),
                   pl.BlockSpec((1, tk, HEAD_PAD), lambda h, j: (h, j, 0)),
                   pl.BlockSpec((1, tk, V_DIM), lambda h, j: (h, j, 0))],
        out_shape=[jax.ShapeDtypeStruct((N_HEADS, T, HEAD_PAD), F32), jax.ShapeDtypeStruct((N_HEADS, T, HEAD_PAD), F32),
                   jax.ShapeDtypeStruct((N_HEADS, T, V_DIM), F32)],
        scratch_shapes=[pltpu.VMEM((tk, tq), F32)] * 4 + [pltpu.VMEM((tk, tq), BF16)] * 4
                       + [pltpu.VMEM((tk, HEAD_PAD), F32), pltpu.VMEM((tk, V_DIM), F32)],
        compiler_params=_params(dimension_semantics=("arbitrary", "arbitrary")),
    )(q, k, v, do, lse_row, delta_row)


def _bwd_proj(x, dx1, pos, proj, dq, dk, dv, dtail, du, g_in, w_in, g_cq, w_uq, g_ckv, w_ukv, gq, gk, conv_w,
              invf, sgn, tm):
    T = x.shape[0]
    nt = T // tm

    ts = min(SUB_TILE, tm)

    def body(x_ref, dx1_ref, pos_ref, lat_ref, cc_ref, cx_ref, dq_ref, dk_ref, dv_ref, dtail_ref, du_ref, dun_ref, *rest):
        consts, (gx_ref, h_ref, dproj_ref), sums = rest[:11], rest[11:14], rest[14:]
        cw_ref = consts[8]
        i = pl.program_id(0)

        @pl.when(i == 0)
        def _():
            for r in sums:
                r[...] = jnp.zeros_like(r)

        du_v = du_ref[...]
        not_last = jnp.where(i < nt - 1, 1.0, 0.0)
        nx0 = dun_ref[0:1, :] * not_last
        nx1 = dun_ref[1:2, :] * not_last
        row = lax.broadcasted_iota(jnp.int32, du_v.shape, 0)
        du1 = jnp.where(row == tm - 1, nx0, pltpu.roll(du_v, tm - 1, 0))
        du2 = jnp.where(row == tm - 2, nx0, jnp.where(row == tm - 1, nx1, pltpu.roll(du_v, tm - 2, 0)))
        dvc = cw_ref[2:3, :] * du_v + cw_ref[1:2, :] * du1 + cw_ref[0:1, :] * du2
        dproj_ref[:, 1536:2048] = (dvc * cx_ref[...]).astype(BF16)
        dproj_ref[:, 2048:2560] = (dvc * cc_ref[...]).astype(BF16)

        for r0 in range(0, tm, ts):
            rows = slice(r0, r0 + ts)
            work(x_ref.at[rows, :], dx1_ref.at[rows, :], pos_ref.at[rows, :], lat_ref.at[rows, :],
                 dq_ref.at[:, rows, :], dk_ref.at[:, rows, :], dv_ref.at[:, rows, :], dtail_ref.at[rows, :], *consts,
                 gx_ref.at[rows, :], h_ref.at[:, rows], dproj_ref.at[rows, :], *sums)

    def work(x_ref, dx1_ref, pos_ref, lat_ref, dq_ref, dk_ref, dv_ref, dtail_ref,
             g_in_ref, w_in_ref, g_cq_ref, w_uq_ref, g_ckv_ref, w_ukv_ref, gq_ref, gk_ref, cw_ref, invf_ref, sgn_ref,
             gx_ref, h_ref, dproj_ref, dw_uq_ref, dw_ukv_ref, dg_in_ref, dg_cq_ref, dg_ckv_ref, dgq_ref, dgk_ref):
        xv = x_ref[...]
        r0 = _rep(_inv_rms_mxu(xv), D_MODEL)
        xh0 = xv * r0
        g_in = g_in_ref[...]
        h_ref[...] = (xh0 * g_in).astype(BF16).T

        c_q = lat_ref[:, 0:Q_LORA]
        rq = _rep(_inv_rms_mxu(c_q), Q_LORA)
        xq = c_q * rq
        g_cq = g_cq_ref[...]
        cqn = (xq * g_cq).astype(BF16)
        c_kv = lat_ref[:, Q_LORA:Q_LORA + KV_LORA]
        rkv = _inv_rms_mxu(c_kv)
        xkv = c_kv * rkv
        g_ckv = g_ckv_ref[...]
        ckvn = (xkv * g_ckv).astype(BF16)
        kpe = lat_ref[:, 384:512]
        kpe_sq = kpe * kpe
        cos_b, sin_b = _rope_tables(pos_ref, invf_ref, sgn_ref)
        gq_a, gq_b = gq_ref[:, 0:NOPE], gq_ref[:, NOPE:HEAD_PAD]
        gk_a, gk_b = gk_ref[:, 0:NOPE], gk_ref[:, NOPE:HEAD_PAD]

        dproj_ref[:, 512:1536] = dtail_ref[:, 0:1024]
        dproj_ref[:, 2560:3072] = dtail_ref[:, 1024:1536]

        def dh_part(c0):
            return _dot_nt(dproj_ref[:, c0:c0 + 512], w_in_ref[:, c0:c0 + 512])

        later_chunks = ((512,), (1024,), (1536, 2048), (2560,))
        dh = jnp.zeros((ts, D_MODEL), F32)
        acc = dict(dh=dh, dkpe=jnp.zeros((ts, LANES), F32), dcqn=jnp.zeros((ts, Q_LORA), F32),
                   dckvn=jnp.zeros((ts, KV_LORA), F32))

        def dh_chunks():
            for chunks in later_chunks:
                for chunk in chunks:
                    acc["dh"] = acc["dh"] + dh_part(chunk)
                    yield

        def queries(hd):
            c0 = hd * HEAD_PAD
            qh = _dot(cqn, w_uq_ref[:, c0:c0 + HEAD_PAD])
            yield
            a, b = qh[:, 0:NOPE], qh[:, NOPE:HEAD_PAD]
            r = lax.rsqrt(_lane_sum(a * a + b * b) / QK_DIM + EPS)
            yield
            xa, xb = a * r, b * r
            dan = dq_ref[hd, :, 0:NOPE]
            dbr = dq_ref[hd, :, NOPE:HEAD_PAD]
            dbn = dbr * cos_b + _swap_rope_halves(dbr * sin_b)
            yield
            dgq_ref[:, 0:NOPE] += _colsum(dan * xa)
            dgq_ref[:, NOPE:HEAD_PAD] += _colsum(dbn * xb)
            dxa, dxb = dan * gq_a, dbn * gq_b
            cq = _lane_sum(dxa * xa + dxb * xb) / QK_DIM
            yield
            dqh = jnp.concatenate([r * (dxa - xa * cq), r * (dxb - xb * cq)], axis=-1).astype(BF16)
            yield
            dw_uq_ref[:, c0:c0 + HEAD_PAD] += _dot_tn(cqn, dqh)
            yield
            acc["dcqn"] = acc["dcqn"] + _dot_nt(dqh, w_uq_ref[:, c0:c0 + HEAD_PAD])
            yield

        def keys(hd):
            c0 = hd * HEAD_PAD
            kvh = _dot(ckvn, w_ukv_ref[:, c0:c0 + HEAD_PAD])
            yield
            ka = kvh[:, 0:NOPE]
            rk = lax.rsqrt(_lane_sum(ka * ka + kpe_sq) / QK_DIM + EPS)
            yield
            xka, xkb = ka * rk, kpe * rk
            dkan = dk_ref[hd, :, 0:NOPE]
            dkbr = dk_ref[hd, :, NOPE:HEAD_PAD]
            dkbn = dkbr * cos_b + _swap_rope_halves(dkbr * sin_b)
            yield
            dgk_ref[:, 0:NOPE] += _colsum(dkan * xka)
            dgk_ref[:, NOPE:HEAD_PAD] += _colsum(dkbn * xkb)
            dxka, dxkb = dkan * gk_a, dkbn * gk_b
            ck = _lane_sum(dxka * xka + dxkb * xkb) / QK_DIM
            yield
            acc["dkpe"] = acc["dkpe"] + rk * (dxkb - xkb * ck)
            dkvh = jnp.concatenate([rk * (dxka - xka * ck), dv_ref[hd]], axis=-1).astype(BF16)
            yield
            dw_ukv_ref[:, c0:c0 + HEAD_PAD] += _dot_tn(ckvn, dkvh)
            yield
            acc["dckvn"] = acc["dckvn"] + _dot_nt(dkvh, w_ukv_ref[:, c0:c0 + HEAD_PAD])
            yield

        chains = [dh_chunks()]
        for hd in range(N_HEADS):
            chains += [queries(hd), keys(hd)]
        _round_robin(chains, 5)
        dh, dkpe, dcqn, dckvn = acc["dh"], acc["dkpe"], acc["dcqn"], acc["dckvn"]

        dg_cq_ref[...] += _colsum(dcqn * xq)
        dxq = dcqn * g_cq
        dproj_ref[:, 0:Q_LORA] = (rq * (dxq - xq * _rep(_lane_sum(dxq * xq) / Q_LORA, Q_LORA))).astype(BF16)
        dg_ckv_ref[...] += _colsum(dckvn * xkv)
        dxkv = dckvn * g_ckv
        dproj_ref[:, 256:384] = (rkv * (dxkv - xkv * (_lane_sum(dxkv * xkv) / KV_LORA))).astype(BF16)
        dproj_ref[:, 384:512] = dkpe.astype(BF16)
        dh = dh + dh_part(0)
        dg_in_ref[...] += _colsum(dh * xh0)
        dxh = dh * g_in
        gx_ref[...] = dx1_ref[...] + r0 * (dxh - xh0 * _rep(_lane_sum(dxh * xh0) / D_MODEL, D_MODEL))

    row = lambda i: (i, 0)
    col = lambda c: (lambda i: (i, c))
    head_rows = lambda i: (0, i, 0)
    nxt = lambda i: (jnp.minimum((i + 1) * (tm // 8), T // 8 - 1), 0)
    in_specs = [pl.BlockSpec((tm, D_MODEL), row), pl.BlockSpec((tm, D_MODEL), row), pl.BlockSpec((tm, 1), row),
                pl.BlockSpec((tm, 512), col(0)), pl.BlockSpec((tm, 512), col(3)), pl.BlockSpec((tm, 512), col(4)),
                pl.BlockSpec((N_HEADS, tm, HEAD_PAD), head_rows), pl.BlockSpec((N_HEADS, tm, HEAD_PAD), head_rows),
                pl.BlockSpec((N_HEADS, tm, V_DIM), head_rows), pl.BlockSpec((tm, 1536), row),
                pl.BlockSpec((tm, CONV_W), row), pl.BlockSpec((8, CONV_W), nxt),
                _full((1, D_MODEL)), _full((D_MODEL, PROJ_EXT)), _full((1, Q_LORA)), _full((Q_LORA, N_HEADS * HEAD_PAD)),
                _full((1, KV_LORA)), _full((KV_LORA, N_HEADS * HEAD_PAD)), _full((1, HEAD_PAD)), _full((1, HEAD_PAD)),
                _full((3, CONV_W)), _full((1, LANES)), _full((1, LANES))]
    out_specs = [pl.BlockSpec((tm, D_MODEL), row), pl.BlockSpec((D_MODEL, tm), lambda i: (0, i)),
                 pl.BlockSpec((tm, PROJ_EXT), row),
                 _full((Q_LORA, N_HEADS * HEAD_PAD)), _full((KV_LORA, N_HEADS * HEAD_PAD)),
                 _full((1, D_MODEL)), _full((1, Q_LORA)), _full((1, KV_LORA)), _full((1, HEAD_PAD)), _full((1, HEAD_PAD))]
    out_shape = [jax.ShapeDtypeStruct((T, D_MODEL), F32), jax.ShapeDtypeStruct((D_MODEL, T), BF16),
                 jax.ShapeDtypeStruct((T, PROJ_EXT), BF16),
                 jax.ShapeDtypeStruct((Q_LORA, N_HEADS * HEAD_PAD), F32), jax.ShapeDtypeStruct((KV_LORA, N_HEADS * HEAD_PAD), F32),
                 jax.ShapeDtypeStruct((1, D_MODEL), F32), jax.ShapeDtypeStruct((1, Q_LORA), F32),
                 jax.ShapeDtypeStruct((1, KV_LORA), F32), jax.ShapeDtypeStruct((1, HEAD_PAD), F32),
                 jax.ShapeDtypeStruct((1, HEAD_PAD), F32)]
    return pl.pallas_call(
        body, name="bwd_proj", grid=(nt,), in_specs=in_specs, out_specs=out_specs, out_shape=out_shape,
        compiler_params=_params(dimension_semantics=("arbitrary",)),
    )(x, dx1, pos, proj, proj, proj, dq, dk, dv, dtail, du, du, g_in, w_in, g_cq, w_uq, g_ckv, w_ukv, gq, gk, conv_w,
      invf, sgn)


def _matmul_acc(a, b, tt, tn):
    M, T = a.shape
    N = b.shape[1]

    def body(a_ref, b_ref, o_ref):
        @pl.when(pl.program_id(1) == 0)
        def _():
            o_ref[...] = jnp.zeros_like(o_ref)

        o_ref[...] += _dot(a_ref[...], b_ref[...])

    return pl.pallas_call(
        body, name="dw_in", grid=(N // tn, T // tt),
        in_specs=[pl.BlockSpec((M, tt), lambda j, t: (0, t)), pl.BlockSpec((tt, tn), lambda j, t: (t, j))],
        out_specs=pl.BlockSpec((M, tn), lambda j, t: (0, j)),
        out_shape=jax.ShapeDtypeStruct((M, N), F32),
        compiler_params=_params(dimension_semantics=("arbitrary", "arbitrary")),
    )(a, b)


def _add_chips(parts, small_parts):
    arrays = list(parts) + [small_parts]

    def body(*refs):
        ins, outs = refs[:len(arrays)], refs[len(arrays):]
        for a_ref, o_ref in zip(ins, outs):
            part = lambda k: a_ref[k].astype(F32)
            o_ref[...] = ((part(0) + part(1)) + part(2)) + part(3)

    in_specs, out_specs, out_shape = [], [], []
    for a in arrays:
        _, rows, cols = a.shape
        in_specs.append(pl.BlockSpec((N_CHIPS, rows // 2, cols), lambda i: (0, i, 0)))
        out_specs.append(pl.BlockSpec((rows // 2, cols), lambda i: (i, 0)))
        out_shape.append(jax.ShapeDtypeStruct((rows, cols), F32))
    outs = pl.pallas_call(body, name="add_chips", grid=(2,), in_specs=in_specs, out_specs=out_specs,
                          out_shape=out_shape, compiler_params=_params(dimension_semantics=("arbitrary",)))(*arrays)
    return outs[:-1], outs[-1]


def _adamw(w, g, m, v, name):
    rows, cols = w.shape
    rb = 256 if rows * cols > 512 * 1024 else rows

    def body(w_ref, g_ref, m_ref, v_ref, d_ref, nm_ref, nv_ref):
        _adamw_math(g_ref[...], w_ref, m_ref, v_ref, d_ref, nm_ref, nv_ref)

    spec = pl.BlockSpec((rb, cols), lambda i: (i, 0))
    shp = jax.ShapeDtypeStruct(w.shape, F32)
    return pl.pallas_call(body, name=name, grid=(rows // rb,), in_specs=[spec] * 4, out_specs=[spec] * 3,
                          out_shape=[shp] * 3)(w, g, m, v)


def _adamw_math(gv, w_ref, m_ref, v_ref, d_ref, nm_ref, nv_ref):
    nm = B1 * m_ref[...] + (1.0 - B1) * gv
    nv = B2 * v_ref[...] + (1.0 - B2) * (gv * gv)
    m_hat = nm / (1.0 - B1 ** STEP)
    v_hat = nv / (1.0 - B2 ** STEP)
    d_ref[...] = -LR * (m_hat / (jnp.sqrt(v_hat) + ADAM_EPS) + WD * w_ref[...])
    nm_ref[...] = nm
    nv_ref[...] = nv


def _adamw_halves(w, mine, other, m, v, c, name):
    hr, cols = mine.shape

    def body(c_ref, w_ref, mine_ref, other_ref, m_ref, v_ref, g_ref, d_ref, nm_ref, nv_ref):
        gv = jnp.where(pl.program_id(0) == c_ref[0], mine_ref[...], other_ref[...])
        g_ref[...] = gv
        _adamw_math(gv, w_ref, m_ref, v_ref, d_ref, nm_ref, nv_ref)

    half = pl.BlockSpec((hr, cols), lambda i, c_ref: (i, 0))
    whole = pl.BlockSpec((hr, cols), lambda i, c_ref: (0, 0))
    shp = jax.ShapeDtypeStruct(w.shape, F32)
    return pl.pallas_call(
        body, name=name, out_shape=[shp] * 4,
        grid_spec=pltpu.PrefetchScalarGridSpec(num_scalar_prefetch=1, grid=(2,), in_specs=[half, whole, whole, half, half],
                                               out_specs=[half] * 4),
        compiler_params=_params(dimension_semantics=("arbitrary",)),
    )(c.reshape(1), w, mine, other, m, v)


_ANY = pl.BlockSpec(memory_space=pl.ANY)


def _mesh_pos():
    return lax.axis_index("x"), lax.axis_index("y"), lax.axis_index("c")


def _other_chips(x, y):
    return [(1 - x, y), (x, 1 - y), (1 - x, 1 - y)]


def _remote(src, dst, send_sems, recv_sems, k, to):
    return pltpu.make_async_remote_copy(src_ref=src, dst_ref=dst, send_sem=send_sems.at[k], recv_sem=recv_sems.at[k],
                                        device_id=to, device_id_type=MESH)


def _gather_weights(shards):
    n = len(shards)

    def body(*refs):
        start, forward, drain = _gather_steps([s.shape for s in shards], refs[:n], refs[n:2 * n], refs[2 * n:3 * n],
                                              *refs[3 * n:])
        start()
        forward()
        drain()

    vmem = pl.BlockSpec(memory_space=pltpu.VMEM)
    return pl.pallas_call(
        body, name="gather_weights", in_specs=[vmem] * n, out_specs=[_ANY] * n,
        out_shape=_gathered_shapes(shards), scratch_shapes=_gather_scratch(shards), compiler_params=_params(),
    )(*shards)


def _gathered_shapes(shards):
    return [jax.ShapeDtypeStruct((N_CHIPS,) + s.shape, BF16) for s in shards]


def _gather_scratch(shards):
    n = len(shards)
    return ([pltpu.VMEM(s.shape, BF16) for s in shards]
            + [pltpu.SemaphoreType.DMA((6 * n,)), pltpu.SemaphoreType.DMA((6 * n,)), pltpu.SemaphoreType.DMA((n,))])


def _gather_steps(shapes, ins, outs, stage, send_sems, recv_sems, local_sems):
    n = len(shapes)
    halved = [s[0] % 32 == 0 for s in shapes]

    def part(i, ref, hc):
        if not halved[i]:
            return ref
        hr = shapes[i][0] // 2
        return ref.at[pl.ds(hc * hr, hr), :]

    def to_chip(i, j, x, y, c):
        cx, cy = _other_chips(x, y)[j]
        return _remote(part(i, stage[i], c), part(i, outs[i].at[2 * x + y], c), send_sems, recv_sems, 6 * i + j, (cx, cy, c))

    def to_sibling(i, j, x, y, c):
        cx, cy = _other_chips(x, y)[j]
        got = part(i, outs[i].at[2 * cx + cy], c)
        return _remote(got, got, send_sems, recv_sems, 6 * i + 3 + j, (x, y, 1 - c))

    def local(i, x, y):
        return pltpu.make_async_copy(stage[i], outs[i].at[2 * x + y], local_sems.at[i])

    def start():
        x, y, c = _mesh_pos()
        for i in range(n):
            stage[i][...] = ins[i][...].astype(BF16)
            local(i, x, y).start()
            for j in range(3):
                to_chip(i, j, x, y, c).start()

    def forward():
        x, y, c = _mesh_pos()
        for i in range(n):
            for j, (cx, cy) in enumerate(_other_chips(x, y)):
                got = part(i, outs[i].at[2 * cx + cy], c)
                _remote(got, got, send_sems, recv_sems, 6 * i + j, (cx, cy, c)).wait_recv()
                if halved[i]:
                    to_sibling(i, j, x, y, c).start()

    def drain():
        x, y, c = _mesh_pos()
        for i in range(n):
            for j, (cx, cy) in enumerate(_other_chips(x, y)):
                if halved[i]:
                    got = part(i, outs[i].at[2 * cx + cy], 1 - c)
                    _remote(got, got, send_sems, recv_sems, 6 * i + 3 + j, (x, y, 1 - c)).wait_recv()
                    to_sibling(i, j, x, y, c).wait_send()
                to_chip(i, j, x, y, c).wait_send()
            local(i, x, y).wait()

    return start, forward, drain


def _swap_halves(grads, small):
    n = len(grads)
    arrays = list(grads) + [small]

    def body(*refs):
        ins, outs, send_sems, recv_sems = refs[:n + 1], refs[n + 1:2 * n + 2], refs[2 * n + 2], refs[2 * n + 3]
        x, y, c = _mesh_pos()
        cps = []
        for i in range(n + 1):
            src = ins[i]
            if i < n:
                hr = grads[i].shape[1] // 2
                src = src.at[:, pl.ds((1 - c) * hr, hr), :]
            cp = _remote(src, outs[i], send_sems, recv_sems, i, (x, y, 1 - c))
            cp.start()
            cps.append(cp)
        for cp in cps:
            cp.wait()

    out_shape = [jax.ShapeDtypeStruct((g.shape[0], g.shape[1] // 2, g.shape[2]), F32) for g in grads]
    out_shape.append(jax.ShapeDtypeStruct(small.shape, F32))
    outs = pl.pallas_call(
        body, name="pair_grads", in_specs=[_ANY] * (n + 1), out_specs=[_ANY] * (n + 1), out_shape=out_shape,
        scratch_shapes=[pltpu.SemaphoreType.DMA((n + 1,)), pltpu.SemaphoreType.DMA((n + 1,))],
    )(*arrays)
    return outs[:n], outs[n]


def _scatter_to_chips(grads, from_sibling, small, small_sibling):
    n = len(grads)

    def body(*refs):
        g_in, r_in, small_ref, small_sib_ref = refs[:n], refs[n:2 * n], refs[2 * n], refs[2 * n + 1]
        outs = refs[2 * n + 2:3 * n + 3]
        scratch = refs[3 * n + 3:]
        g_buf, r_buf, p_buf = scratch[:n], scratch[n:2 * n], scratch[2 * n:3 * n]
        s_buf, ss_buf, sp_buf, load_sems, send_sems, recv_sems, local_sems = scratch[3 * n:]
        x, y, c = _mesh_pos()
        me = 2 * x + y
        chips = _other_chips(x, y)

        loads = []
        for i in range(n):
            hr = from_sibling[i].shape[1]
            pair = (pltpu.make_async_copy(g_in[i].at[:, pl.ds(c * hr, hr), :], g_buf[i], load_sems.at[2 * i]),
                    pltpu.make_async_copy(r_in[i], r_buf[i], load_sems.at[2 * i + 1]))
            loads.append(pair)
        loads.append((pltpu.make_async_copy(small_ref, s_buf, load_sems.at[2 * n]),
                      pltpu.make_async_copy(small_sib_ref, ss_buf, load_sems.at[2 * n + 1])))
        for pair in loads:
            for cp in pair:
                cp.start()

        locals_, sends = [], []
        for i in range(n + 1):
            for cp in loads[i]:
                cp.wait()
            if i < n:
                p_buf[i][...] = (g_buf[i][...] + r_buf[i][...]).astype(BF16)
                part = lambda k, i=i: p_buf[i].at[k]
            else:
                sp_buf[...] = s_buf[...] + ss_buf[...]
                part = lambda k: sp_buf
            mine = pltpu.make_async_copy(part(me), outs[i].at[me], local_sems.at[i])
            mine.start()
            locals_.append(mine)
            for j, (cx, cy) in enumerate(chips):
                cp = _remote(part(2 * cx + cy), outs[i].at[me], send_sems, recv_sems, 3 * i + j, (cx, cy, c))
                cp.start()
                sends.append(cp)
        for i in range(n + 1):
            for j, (cx, cy) in enumerate(chips):
                got = outs[i].at[2 * cx + cy]
                _remote(got, got, send_sems, recv_sems, 3 * i + j, (cx, cy, c)).wait_recv()
        for cp in sends:
            cp.wait_send()
        for cp in locals_:
            cp.wait()

    out_shape = [jax.ShapeDtypeStruct(r.shape, BF16) for r in from_sibling]
    out_shape.append(jax.ShapeDtypeStruct((N_CHIPS,) + small.shape, F32))
    scratch = [pltpu.VMEM(r.shape, F32) for r in from_sibling] * 2 + [pltpu.VMEM(r.shape, BF16) for r in from_sibling]
    scratch += [pltpu.VMEM(small.shape, F32)] * 3
    scratch += [pltpu.SemaphoreType.DMA((2 * n + 2,)), pltpu.SemaphoreType.DMA((3 * n + 3,)),
                pltpu.SemaphoreType.DMA((3 * n + 3,)), pltpu.SemaphoreType.DMA((n + 1,))]
    outs = pl.pallas_call(
        body, name="scatter_grads", in_specs=[_ANY] * (2 * n + 2), out_specs=[_ANY] * (n + 1), out_shape=out_shape,
        scratch_shapes=scratch, compiler_params=_params(),
    )(*grads, *from_sibling, small, small_sibling)
    return outs[:n], outs[n]


def _share_halves(halves):
    n = len(halves)

    def body(*refs):
        ins, outs, send_sems, recv_sems = refs[:n], refs[n:2 * n], refs[2 * n], refs[2 * n + 1]
        x, y, c = _mesh_pos()
        cps = [_remote(ins[i], outs[i], send_sems, recv_sems, i, (x, y, 1 - c)) for i in range(n)]
        for cp in cps:
            cp.start()
        for cp in cps:
            cp.wait()

    return pl.pallas_call(
        body, name="share_halves", in_specs=[_ANY] * n, out_specs=[_ANY] * n,
        out_shape=[jax.ShapeDtypeStruct(h.shape, h.dtype) for h in halves],
        scratch_shapes=[pltpu.SemaphoreType.DMA((n,)), pltpu.SemaphoreType.DMA((n,))],
    )(*halves)


SHARD_COLS_IN = IN_TOTAL // N_CHIPS
KPE_END = Q_LORA + KV_LORA + ROPE


def _by_cols(a):
    return a.transpose(1, 0, 2).reshape(a.shape[1], N_CHIPS * a.shape[2])


def _assemble_early(c_in, c_uq, c_ukv, c_conv):
    w_in_e = jnp.concatenate([c_in[0][:, :KPE_END], jnp.zeros((D_MODEL, 64), BF16), c_in[0][:, KPE_END:],
                              c_in[1], c_in[2], c_in[3]], axis=1)
    w_uq_e = _by_cols(jnp.pad(c_uq, ((0, 0), (0, 0), (0, HEAD_PAD - QK_DIM))))
    return w_in_e, w_uq_e, _by_cols(c_ukv), _by_cols(c_conv).astype(F32)


def _assemble_late(c_o, c_pl, c_plg):
    return c_o.reshape(D_MODEL, D_MODEL), _by_cols(c_pl), c_plg.reshape(D_MODEL, D_MODEL)


def _split_grads(dw_in_e, dw_uq_e, dw_ukv, dw_o, dw_pl, dw_plg):
    chip_major = lambda a: a.reshape(a.shape[0], N_CHIPS, a.shape[1] // N_CHIPS).transpose(1, 0, 2)
    first = jnp.concatenate([dw_in_e[:, :KPE_END], dw_in_e[:, KPE_END + 64:SHARD_COLS_IN + 64]], axis=1)
    rest = [dw_in_e[:, SHARD_COLS_IN * k + 64:SHARD_COLS_IN * (k + 1) + 64] for k in range(1, N_CHIPS)]
    return [jnp.stack([first] + rest), chip_major(dw_uq_e)[:, :, :QK_DIM], chip_major(dw_ukv),
            dw_o.reshape(N_CHIPS, D_MODEL // N_CHIPS, D_MODEL), chip_major(dw_pl),
            dw_plg.reshape(N_CHIPS, D_MODEL // N_CHIPS, D_MODEL)]


def _local_step(x, p, pos, tgt, gains, early, late_shards, late_gathered, tm, tq):
    w_in_e, w_uq_e, w_ukv, conv_w = early
    g_in, g_cq, g_ckv, g_q, g_k, g_oa, g_oc, g_pl = gains
    T = x.shape[0]
    zpad = lambda a, n: jnp.concatenate([a, jnp.zeros(a.shape[:-1] + (n,), a.dtype)], axis=-1)
    gq, gk = zpad(g_q, HEAD_PAD - QK_DIM), zpad(g_k, HEAD_PAD - QK_DIM)
    inv_freq = 1.0 / (ROPE_THETA ** (jnp.arange(0, ROPE, 2, dtype=F32) / ROPE))
    invf = jnp.concatenate([inv_freq, inv_freq, jnp.zeros((64,), F32)]).reshape(1, LANES)
    sgn = jnp.concatenate([-jnp.ones((32,), F32), jnp.ones((32,), F32), jnp.zeros((64,), F32)]).reshape(1, LANES)

    (proj, q, k, v), gathered = _fwd_proj(x, pos, g_in, w_in_e, g_cq, w_uq_e, g_ckv, w_ukv, gq, gk, invf, sgn,
                                          late_shards, min(2 * tm, T))
    w_o, w_pl, w_plg = _assemble_late(*(gathered if late_shards else late_gathered))
    o, lse = _attn_fwd(q, k, v, tq)
    (dx1, do, delta, dtail, du, dw_o, dw_pl, dw_plg, dg_oa, dg_oc, dg_pl, dconv, loss) = _tail(
        x, o, proj, p, tgt, g_oa, g_oc, g_pl, conv_w, w_o, w_pl, w_plg, tm)
    dq, dk, dv = _attn_bwd(q, k, v, do, lse, delta, tq)
    (gx, h, dproj, dw_uq_e, dw_ukv, dg_in, dg_cq, dg_ckv, dgq, dgk) = _bwd_proj(
        x, dx1, pos, proj, dq, dk, dv, dtail, du, g_in, w_in_e, g_cq, w_uq_e, g_ckv, w_ukv, gq, gk, conv_w, invf, sgn, tm)
    dw_in_e = _matmul_acc(h, dproj, min(4096, T), 512)
    wgrads = (dw_in_e, dw_uq_e, dw_ukv, dw_o, dw_pl, dw_plg)
    ggrads = (dg_in, dg_cq, dg_ckv, dgq, dgk, dg_oa, dg_oc, dg_pl)
    return loss, gx, wgrads, ggrads, dconv


def kernel(x, p, positions, g_in, w_in, g_cq, w_uq, g_ckv, w_ukv, g_q, g_k, conv_w, g_oa, g_oc, w_o, w_pl, w_plg, g_pl, loss_target, m_g_in, m_w_in, m_g_cq, m_w_uq, m_g_ckv, m_w_ukv, m_g_q, m_g_k, m_conv_w, m_g_oa, m_g_oc, m_w_o, m_w_pl, m_w_plg, m_g_pl, v_g_in, v_w_in, v_g_cq, v_w_uq, v_g_ckv, v_w_ukv, v_g_q, v_g_k, v_conv_w, v_g_oa, v_g_oc, v_w_o, v_w_pl, v_w_plg, v_g_pl):
    T = x.shape[1]
    c = lax.axis_index("c")
    chip = 2 * lax.axis_index("x") + lax.axis_index("y")
    gains = [g.reshape(1, -1) for g in (g_in, g_cq, g_ckv, g_q, g_k, g_oa, g_oc, g_pl)]

    early = _assemble_early(*_gather_weights([w_in[0], w_uq[0], w_ukv[0], conv_w[0]]))

    loss, gx, wgrads, ggrads, dconv = _local_step(
        x[0], p[0, 0], positions.reshape(T, 1), loss_target[0], gains, early, [w_o[0], w_pl[0], w_plg[0]], None, 256, 512)

    grads_cm = _split_grads(*wgrads)
    small_parts = [a.reshape(-1, LANES) for a in (*ggrads, loss, dconv)]
    small_rows = [a.shape[0] for a in small_parts]
    tile_rows = [-(-r // 8) * 8 for r in small_rows]
    tile_rows[-1] += -sum(tile_rows) % 16
    small = jnp.concatenate([jnp.pad(a, ((0, t - r), (0, 0))) for a, r, t in zip(small_parts, small_rows, tile_rows)])
    from_sibling, small_sibling = _swap_halves(grads_cm, small)
    by_chip, small_by_chip = _scatter_to_chips(grads_cm, from_sibling, small, small_sibling)
    halves, small_total = _add_chips(by_chip, small_by_chip)
    other_halves = _share_halves(halves)

    gg, off = [], 0
    for rows, tiled in zip(small_rows, tile_rows):
        gg.append(small_total[off:off + rows].reshape(1, -1))
        off += tiled
    loss_out = gg[8][0, 0]
    conv_total = gg[9].reshape(3, CONV_W)
    conv_g = lax.dynamic_slice(conv_total, (0, chip * (CONV_W // N_CHIPS)), (3, CONV_W // N_CHIPS))
    g_by_name = dict(g_in=gg[0], g_cq=gg[1], g_ckv=gg[2], g_q=gg[3][:, :QK_DIM], g_k=gg[4][:, :QK_DIM], conv_w=conv_g,
                     g_oa=gg[5], g_oc=gg[6], g_pl=gg[7])
    half_by_name = dict(zip(("w_in", "w_uq", "w_ukv", "w_o", "w_pl", "w_plg"), zip(halves, other_halves)))
    weights = dict(g_in=g_in, w_in=w_in, g_cq=g_cq, w_uq=w_uq, g_ckv=g_ckv, w_ukv=w_ukv, g_q=g_q, g_k=g_k,
                   conv_w=conv_w, g_oa=g_oa, g_oc=g_oc, w_o=w_o, w_pl=w_pl, w_plg=w_plg, g_pl=g_pl)
    ms = dict(g_in=m_g_in, w_in=m_w_in, g_cq=m_g_cq, w_uq=m_w_uq, g_ckv=m_g_ckv, w_ukv=m_w_ukv, g_q=m_g_q, g_k=m_g_k,
              conv_w=m_conv_w, g_oa=m_g_oa, g_oc=m_g_oc, w_o=m_w_o, w_pl=m_w_pl, w_plg=m_w_plg, g_pl=m_g_pl)
    vs = dict(g_in=v_g_in, w_in=v_w_in, g_cq=v_g_cq, w_uq=v_w_uq, g_ckv=v_g_ckv, w_ukv=v_w_ukv, g_q=v_g_q, g_k=v_g_k,
              conv_w=v_conv_w, g_oa=v_g_oa, g_oc=v_g_oc, w_o=v_w_o, w_pl=v_w_pl, w_plg=v_w_plg, g_pl=v_g_pl)
    names = list(weights)
    grads, deltas, new_m, new_v = [], [], [], []
    for n in names:
        w = weights[n]
        w2 = w.reshape(-1, w.shape[-1])
        if n in half_by_name:
            g2, d, nm, nv = _adamw_halves(w2, *half_by_name[n], ms[n].reshape(w2.shape), vs[n].reshape(w2.shape), c,
                                          "adamw_" + n)
        else:
            g2 = g_by_name[n].reshape(w2.shape)
            d, nm, nv = _adamw(w2, g2, ms[n].reshape(w2.shape), vs[n].reshape(w2.shape), "adamw_" + n)
        grads.append(g2.reshape(w.shape))
        deltas.append(d.reshape(w.shape))
        new_m.append(nm.reshape(w.shape))
        new_v.append(nv.reshape(w.shape))
    return (loss_out, gx.reshape(x.shape), *grads, *deltas, *new_m, *new_v)
```

```python
import functools
import math

import jax
import jax.numpy as jnp
from jax import lax
from jax.experimental import pallas as pl
from jax.experimental.pallas import tpu as pltpu

F32 = jnp.float32
BF16 = jnp.bfloat16

D_MODEL = 1024
N_HEADS = 4
NOPE = 128
ROPE = 64
V_DIM = 128
QK_DIM = NOPE + ROPE
HEAD_PAD = 256
Q_LORA = 256
KV_LORA = 128
ATTN_W = 512
CONV_W = 512
PLE = 256
IN_TOTAL = 3008
PROJ_EXT = 3072
ROPE_THETA = 10000.0
EPS = 1e-6
SCALE = 1.0 / math.sqrt(QK_DIM)
LOG2E = math.log2(math.e)
EXP2_SCALE = SCALE * LOG2E
NEG = -1e30
SOFTMAX_ROWS = 32
SUB_TILE = 256

LR, B1, B2, ADAM_EPS, WD, STEP = 0.001, 0.9, 0.999, 1e-08, 0.01, 10

N_CHIPS = 4
LANES = 128
VMEM_LIMIT = 56 * 1024 * 1024
MESH = pl.DeviceIdType.MESH


def _params(**kw):
    return pltpu.CompilerParams(vmem_limit_bytes=VMEM_LIMIT, **kw)


def _inv_rms(x, n):
    return lax.rsqrt(jnp.sum(x * x, axis=-1, keepdims=True) / n + EPS)


def _lane_sum(a):
    folded = a[:, 0:LANES]
    for c0 in range(LANES, a.shape[1], LANES):
        folded = folded + a[:, c0:c0 + LANES]
    head = folded.astype(BF16)
    tail = (folded - head.astype(F32)).astype(BF16)
    return _dot(jnp.concatenate([head, tail], axis=1), jnp.ones((2 * LANES, LANES), BF16))


def _inv_rms_mxu(x):
    return lax.rsqrt(_lane_sum(x * x) / x.shape[1] + EPS)


def _rep(r, width):
    return r if width == LANES else jnp.tile(r, (1, width // LANES))


def _sigmoid(z):
    return 1.0 / (1.0 + jnp.exp(-z))


def _swap_rope_halves(b):
    lane = lax.broadcasted_iota(jnp.int32, b.shape, 1)
    swapped = jnp.where(lane < 32, pltpu.roll(b, 96, 1), pltpu.roll(b, 32, 1))
    return jnp.where(lane < ROPE, swapped, 0.0)


def _dot(a, b):
    return jnp.dot(a, b, preferred_element_type=F32)


def _dot_nt(a, b):
    return lax.dot_general(a, b, (((1,), (1,)), ((), ())), preferred_element_type=F32)


def _dot_tn(a, b):
    return lax.dot_general(a, b, (((0,), (0,)), ((), ())), preferred_element_type=F32)


def _colsum(a):
    return jnp.sum(a, axis=0, keepdims=True)


def _full(shape):
    return pl.BlockSpec(shape, lambda *_: (0,) * len(shape))


def _round_robin(chains, width):
    waiting, active = list(chains), []
    while waiting or active:
        while waiting and len(active) < width:
            active.append(waiting.pop(0))
        for chain in list(active):
            if next(chain, _DONE) is _DONE:
                active.remove(chain)


_DONE = object()


def _rope_tables(pos_ref, invf_ref, sgn_ref):
    ang = pos_ref[...].astype(F32) * invf_ref[...]
    return jnp.cos(ang), jnp.sin(ang) * sgn_ref[...]


def _fwd_proj(x, pos, g_in, w_in, g_cq, w_uq, g_ckv, w_ukv, gq, gk, invf, sgn, late_shards, tm):
    T = x.shape[0]
    nt = T // tm
    n_late = len(late_shards)
    ts = min(SUB_TILE, tm)

    def body(x_ref, pos_ref, g_in_ref, w_in_ref, g_cq_ref, w_uq_ref, g_ckv_ref, w_ukv_ref, gq_ref, gk_ref,
             invf_ref, sgn_ref, *rest):
        late_in, (proj_ref, q_ref, k_ref, v_ref) = rest[:n_late], rest[n_late:n_late + 4]
        late_out, late_scratch = rest[n_late + 4:2 * n_late + 4], rest[2 * n_late + 4:]
        i = pl.program_id(0)
        if n_late:
            start, forward, drain = _gather_steps([s.shape for s in late_shards], late_in, late_out,
                                                  late_scratch[:n_late], *late_scratch[n_late:])
            pl.when(i == 0)(start)
            pl.when(i == nt // 2)(forward)

        for r0 in range(0, tm, ts):
            rows = slice(r0, r0 + ts)
            xv = x_ref[rows, :]
            h = (xv * _rep(_inv_rms_mxu(xv), D_MODEL) * g_in_ref[...]).astype(BF16)
            lat = _dot(h, w_in_ref[:, 0:512])
            proj_ref[rows, 0:512] = lat
            c_q = lat[:, 0:Q_LORA]
            cqn = (c_q * _rep(_inv_rms_mxu(c_q), Q_LORA) * g_cq_ref[...]).astype(BF16)
            c_kv = lat[:, Q_LORA:Q_LORA + KV_LORA]
            ckvn = (c_kv * _inv_rms_mxu(c_kv) * g_ckv_ref[...]).astype(BF16)
            kpe = lat[:, 384:512]
            kpe_sq = kpe * kpe
            cos_b, sin_b = _rope_tables(pos_ref.at[rows, :], invf_ref, sgn_ref)
            gq_a, gq_b = gq_ref[:, 0:NOPE], gq_ref[:, NOPE:HEAD_PAD]
            gk_a, gk_b = gk_ref[:, 0:NOPE], gk_ref[:, NOPE:HEAD_PAD]

            def projections(rows=rows, h=h):
                for c0 in range(512, PROJ_EXT, 512):
                    proj_ref[rows, c0:c0 + 512] = _dot(h, w_in_ref[:, c0:c0 + 512])
                    yield

            def queries(hd, rows=rows, cqn=cqn, cos_b=cos_b, sin_b=sin_b, gq_a=gq_a, gq_b=gq_b):
                qh = _dot(cqn, w_uq_ref[:, hd * HEAD_PAD:(hd + 1) * HEAD_PAD])
                yield
                a, b = qh[:, 0:NOPE], qh[:, NOPE:HEAD_PAD]
                r = lax.rsqrt(_lane_sum(a * a + b * b) / QK_DIM + EPS)
                yield
                bn = b * r * gq_b
                q_ref[hd, rows, 0:NOPE] = (a * r * gq_a).astype(BF16)
                q_ref[hd, rows, NOPE:HEAD_PAD] = (bn * cos_b + _swap_rope_halves(bn) * sin_b).astype(BF16)
                yield

            def keys(hd, rows=rows, ckvn=ckvn, kpe=kpe, kpe_sq=kpe_sq, cos_b=cos_b, sin_b=sin_b, gk_a=gk_a, gk_b=gk_b):
                kvh = _dot(ckvn, w_ukv_ref[:, hd * HEAD_PAD:(hd + 1) * HEAD_PAD])
                yield
                ka = kvh[:, 0:NOPE]
                rk = lax.rsqrt(_lane_sum(ka * ka + kpe_sq) / QK_DIM + EPS)
                yield
                kbn = kpe * rk * gk_b
                k_ref[hd, rows, 0:NOPE] = (ka * rk * gk_a).astype(BF16)
                k_ref[hd, rows, NOPE:HEAD_PAD] = (kbn * cos_b + _swap_rope_halves(kbn) * sin_b).astype(BF16)
                v_ref[hd, rows, 0:V_DIM] = kvh[:, NOPE:HEAD_PAD].astype(BF16)
                v_ref[hd, rows, V_DIM:2 * V_DIM] = jnp.ones((ts, V_DIM), BF16)
                yield

            chains = [projections()]
            for hd in range(N_HEADS):
                chains += [queries(hd), keys(hd)]
            _round_robin(chains, 4)

        if n_late:
            pl.when(i == nt - 1)(drain)

    row = lambda i: (i, 0)
    head_rows = lambda i: (0, i, 0)
    outs = pl.pallas_call(
        body, name="fwd_proj", grid=(nt,),
        in_specs=[pl.BlockSpec((tm, D_MODEL), row), pl.BlockSpec((tm, 1), row), _full((1, D_MODEL)),
                  _full((D_MODEL, PROJ_EXT)), _full((1, Q_LORA)), _full((Q_LORA, N_HEADS * HEAD_PAD)),
                  _full((1, KV_LORA)), _full((KV_LORA, N_HEADS * HEAD_PAD)), _full((1, HEAD_PAD)), _full((1, HEAD_PAD)),
                  _full((1, LANES)), _full((1, LANES))] + [_full(s.shape) for s in late_shards],
        out_specs=[pl.BlockSpec((tm, PROJ_EXT), row), pl.BlockSpec((N_HEADS, tm, HEAD_PAD), head_rows),
                   pl.BlockSpec((N_HEADS, tm, HEAD_PAD), head_rows), pl.BlockSpec((N_HEADS, tm, 2 * V_DIM), head_rows)]
                  + [_ANY] * n_late,
        out_shape=[jax.ShapeDtypeStruct((T, PROJ_EXT), F32), jax.ShapeDtypeStruct((N_HEADS, T, HEAD_PAD), BF16),
                   jax.ShapeDtypeStruct((N_HEADS, T, HEAD_PAD), BF16), jax.ShapeDtypeStruct((N_HEADS, T, 2 * V_DIM), BF16)]
                  + _gathered_shapes(late_shards),
        scratch_shapes=_gather_scratch(late_shards) if n_late else [],
        compiler_params=_params(dimension_semantics=("arbitrary",)),
    )(x, pos, g_in, w_in, g_cq, w_uq, g_ckv, w_ukv, gq, gk, invf, sgn, *late_shards)
    return outs[:4], outs[4:]


def _chunk_pipeline(n_loop, lag, matmuls, pointwise, accumulate, last):
    slots = lag + 1

    def iteration(t, slot):
        matmuls(jnp.minimum(t + lag, n_loop), (slot + lag) % slots)
        accumulate(jnp.maximum(t - lag, 0), (slot + 1) % slots)
        pointwise(t, slot, False)

    def finish(slot):
        for back in range(lag, 0, -1):
            accumulate(jnp.maximum(n_loop - back, 0), (slot - back) % slots)
        pointwise(n_loop, slot, True)
        accumulate(n_loop, slot)
        last()

    for u in range(lag):
        matmuls(jnp.minimum(u, n_loop), u)

    def unrolled(tt, carry):
        for slot in range(slots):
            iteration(slots * tt + slot, slot)
        return carry

    lax.fori_loop(0, n_loop // slots, unrolled, 0)
    rest = lax.rem(n_loop, slots)
    t0 = n_loop - rest

    for r in range(slots):
        @pl.when(rest == r)
        def _():
            for slot in range(r):
                iteration(t0 + slot, slot)
            finish(r)


def _attn_fwd(q, k, v, tq):
    T = q.shape[1]
    tk = tq
    rc = min(SOFTMAX_ROWS, tq)

    def body(q_ref, k_ref, v_ref, o_ref, lse_ref, s0, s1, s2, p0, p1, p2, a0, a1, a2, m_ref, acc_ref):
        qi = pl.program_id(1)
        s_buf, p_buf, a_buf = (s0, s1, s2), (p0, p1, p2), (a0, a1, a2)

        def scores(t, slot):
            ks = pl.multiple_of(t * tk, tk)
            s_buf[slot][...] = _dot_nt(q_ref[0], k_ref[0, pl.ds(ks, tk), :])

        def values(t, slot):
            ks = pl.multiple_of(t * tk, tk)
            acc_ref[...] = acc_ref[...] * a_buf[slot][...] + _dot(p_buf[slot][...], v_ref[0, pl.ds(ks, tk), :])

        def softmax(t, slot, masked):
            s_all = s_buf[slot][...]
            if masked:
                row = lax.broadcasted_iota(jnp.int32, (tq, tk), 0)
                col = lax.broadcasted_iota(jnp.int32, (tq, tk), 1)
                s_all = jnp.where(col <= row, s_all, NEG)
                s_buf[slot][...] = s_all
            m_old = m_ref[...]
            m_new = jnp.maximum(m_old, jnp.max(s_all, axis=1, keepdims=True))
            a_buf[slot][...] = jnp.exp2((m_old - m_new) * EXP2_SCALE)
            m_ref[...] = m_new
            for r0 in range(0, tq, rc):
                s = s_buf[slot][r0:r0 + rc, :]
                p_buf[slot][r0:r0 + rc, :] = jnp.exp2((s - m_new[r0:r0 + rc, :]) * EXP2_SCALE).astype(BF16)

        def last():
            l = acc_ref[:, V_DIM:2 * V_DIM]
            o_ref[...] = acc_ref[:, 0:V_DIM] / l
            lse_ref[0] = (m_ref[...] * SCALE + jnp.log(l)).T[0:1, :]

        m_ref[...] = jnp.full_like(m_ref, NEG)
        acc_ref[...] = jnp.zeros_like(acc_ref)
        for p_late, a_late in ((p1, a1), (p2, a2)):
            p_late[...] = jnp.zeros_like(p_late)
            a_late[...] = jnp.ones_like(a_late)
        _chunk_pipeline(qi, 2, scores, softmax, values, last)

    return pl.pallas_call(
        body, name="attn_fwd", grid=(N_HEADS, T // tq),
        in_specs=[pl.BlockSpec((1, tq, HEAD_PAD), lambda h, i: (h, i, 0)),
                  pl.BlockSpec((1, T, HEAD_PAD), lambda h, i: (h, 0, 0)),
                  pl.BlockSpec((1, T, 2 * V_DIM), lambda h, i: (h, 0, 0))],
        out_specs=[pl.BlockSpec((tq, V_DIM), lambda h, i: (i, h)),
                   pl.BlockSpec((1, 1, tq), lambda h, i: (h, 0, i))],
        out_shape=[jax.ShapeDtypeStruct((T, ATTN_W), F32), jax.ShapeDtypeStruct((N_HEADS, 1, T), F32)],
        scratch_shapes=[pltpu.VMEM((tq, tk), F32)] * 3 + [pltpu.VMEM((tq, tk), BF16)] * 3
                       + [pltpu.VMEM((tq, 1), F32)] * 4 + [pltpu.VMEM((tq, 2 * V_DIM), F32)],
        compiler_params=_params(dimension_semantics=("arbitrary", "arbitrary")),
    )(q, k, v)


def _tail(x, o, proj, p, tgt, g_oa, g_oc, g_pl, conv_w, w_o, w_pl, w_plg, tm):
    T = x.shape[0]
    nt = T // tm
    ts = min(SUB_TILE, tm)

    def body(x_ref, o_ref, za_ref, cb_ref, cc_ref, cx_ref, zc_ref, cch_ref, cxh_ref, p_ref, tgt_ref,
             g_oa_ref, g_oc_ref, g_pl_ref, cw_ref, w_o_ref, w_pl_ref, w_plg_ref,
             dx1_ref, do_ref, delta_ref, dtail_ref, du_ref,
             dw_o_ref, dw_pl_ref, dw_plg_ref, dg_oa_ref, dg_oc_ref, dg_pl_ref, dcw_ref, loss_ref,
             v_buf, v1_buf, v2_buf):
        i = pl.program_id(0)

        @pl.when(i == 0)
        def _():
            for r in (dw_o_ref, dw_pl_ref, dw_plg_ref, dg_oa_ref, dg_oc_ref, dg_pl_ref, dcw_ref, loss_ref):
                r[...] = jnp.zeros_like(r)

        g_oa, g_oc, g_pl = g_oa_ref[...], g_oc_ref[...], g_pl_ref[...]
        w0, w1, w2 = cw_ref[0:1, :], cw_ref[1:2, :], cw_ref[2:3, :]

        v = cc_ref[...] * cx_ref[...]
        not_first = jnp.where(i > 0, 1.0, 0.0)
        hv6 = cch_ref[6:7, :] * cxh_ref[6:7, :] * not_first
        hv7 = cch_ref[7:8, :] * cxh_ref[7:8, :] * not_first
        row = lax.broadcasted_iota(jnp.int32, v.shape, 0)
        v_buf[...] = v
        v1_buf[...] = jnp.where(row == 0, hv7, pltpu.roll(v, 1, 0))
        v2_buf[...] = jnp.where(row == 0, hv6, jnp.where(row == 1, hv7, pltpu.roll(v, 2, 0)))

        def sub_tile(rows):
            xv, ov, za, cb, zc = x_ref[rows, :], o_ref[rows, :], za_ref[rows, :], cb_ref[rows, :], zc_ref[rows, :]
            pb = p_ref[rows, :].astype(BF16)
            pp = _dot(pb, w_pl_ref[...])
            yield
            sa = _sigmoid(za)
            silu_a = za * sa
            ga = ov * silu_a
            ra = _inv_rms(ga, ATTN_W)
            xa = ga * ra
            ya = xa * g_oa
            yield
            vt, v1, v2 = v_buf[rows, :], v1_buf[rows, :], v2_buf[rows, :]
            u = w0 * v2 + w1 * v1 + w2 * vt
            sc = _sigmoid(zc)
            silu_c = zc * sc
            gc = cb * u * silu_c
            rc = _inv_rms(gc, CONV_W)
            xc = gc * rc
            yc = xc * g_oc
            yield
            ycat = jnp.concatenate([ya, yc], axis=-1).astype(BF16)
            x1 = xv + _dot(ycat, w_o_ref[...])
            yield
            r1 = _inv_rms(x1, D_MODEL)
            xh1 = x1 * r1
            n1 = (xh1 * g_pl).astype(BF16)
            yield
            gate = _sigmoid(_dot(n1, w_plg_ref[...]))
            yield
            err = x1 + gate * pp - tgt_ref[rows, :]
            loss_ref[...] += 0.5 * jnp.sum(err * err) / D_MODEL
            dy = err / D_MODEL
            dpp = (dy * gate).astype(BF16)
            da = (dy * pp * gate * (1.0 - gate)).astype(BF16)
            yield
            dn1 = _dot_nt(da, w_plg_ref[...])
            yield
            dw_pl_ref[...] += _dot_tn(pb, dpp)
            yield
            dw_plg_ref[...] += _dot_tn(n1, da)
            yield
            dg_pl_ref[...] += _colsum(dn1 * xh1)
            dxh = dn1 * g_pl
            dx1 = dy + r1 * (dxh - xh1 * (jnp.sum(dxh * xh1, axis=-1, keepdims=True) / D_MODEL))
            dx1_ref[rows, :] = dx1
            dx1b = dx1.astype(BF16)
            yield
            dycat = _dot_nt(dx1b, w_o_ref[...])
            dya, dyc = dycat[:, 0:ATTN_W], dycat[:, ATTN_W:D_MODEL]
            yield
            dw_o_ref[0:ATTN_W, :] += _dot_tn(ycat[:, 0:ATTN_W], dx1b)
            yield
            dg_oa_ref[...] += _colsum(dya * xa)
            dxa = dya * g_oa
            dga = ra * (dxa - xa * (jnp.sum(dxa * xa, axis=-1, keepdims=True) / ATTN_W))
            do = (dga * silu_a).astype(BF16)
            do_ref[rows, :] = do
            yield
            dof = do.astype(F32) * ov
            for hd in range(N_HEADS):
                delta_ref[hd, :, rows] = _lane_sum(dof[:, hd * V_DIM:(hd + 1) * V_DIM]).T[0:1, :]
            dtail_ref[rows, 0:512] = (dga * ov * (sa * (1.0 + za * (1.0 - sa)))).astype(BF16)
            yield
            dw_o_ref[ATTN_W:D_MODEL, :] += _dot_tn(ycat[:, ATTN_W:D_MODEL], dx1b)
            yield
            dg_oc_ref[...] += _colsum(dyc * xc)
            dxc = dyc * g_oc
            dgc = rc * (dxc - xc * (jnp.sum(dxc * xc, axis=-1, keepdims=True) / CONV_W))
            dtail_ref[rows, 512:1024] = (dgc * u * silu_c).astype(BF16)
            yield
            du = dgc * cb * silu_c
            du_ref[rows, :] = du
            dtail_ref[rows, 1024:1536] = (dgc * cb * u * (sc * (1.0 + zc * (1.0 - sc)))).astype(BF16)
            yield
            dcw_ref[0:1, :] += _colsum(du * v2)
            dcw_ref[1:2, :] += _colsum(du * v1)
            dcw_ref[2:3, :] += _colsum(du * vt)

        _round_robin([sub_tile(slice(r0, r0 + ts)) for r0 in range(0, tm, ts)], tm // ts)

    row = lambda i: (i, 0)
    col = lambda c: (lambda i: (i, c))
    halo = lambda c: (lambda i: (jnp.maximum(i * (tm // 8) - 1, 0), c))
    in_specs = [pl.BlockSpec((tm, D_MODEL), row), pl.BlockSpec((tm, ATTN_W), row)]
    in_specs += [pl.BlockSpec((tm, 512), col(c)) for c in (1, 2, 3, 4, 5)]
    in_specs += [pl.BlockSpec((8, 512), halo(3)), pl.BlockSpec((8, 512), halo(4))]
    in_specs += [pl.BlockSpec((tm, PLE), row), pl.BlockSpec((tm, D_MODEL), row),
                 _full((1, ATTN_W)), _full((1, CONV_W)), _full((1, D_MODEL)), _full((3, CONV_W)),
                 _full((D_MODEL, D_MODEL)), _full((PLE, D_MODEL)), _full((D_MODEL, D_MODEL))]
    out_specs = [pl.BlockSpec((tm, D_MODEL), row), pl.BlockSpec((tm, ATTN_W), row),
                 pl.BlockSpec((N_HEADS, 1, tm), lambda i: (0, 0, i)), pl.BlockSpec((tm, 1536), row),
                 pl.BlockSpec((tm, CONV_W), row),
                 _full((D_MODEL, D_MODEL)), _full((PLE, D_MODEL)), _full((D_MODEL, D_MODEL)),
                 _full((1, ATTN_W)), _full((1, CONV_W)), _full((1, D_MODEL)), _full((3, CONV_W)), _full((1, LANES))]
    out_shape = [jax.ShapeDtypeStruct((T, D_MODEL), F32), jax.ShapeDtypeStruct((T, ATTN_W), BF16),
                 jax.ShapeDtypeStruct((N_HEADS, 1, T), F32), jax.ShapeDtypeStruct((T, 1536), BF16),
                 jax.ShapeDtypeStruct((T, CONV_W), F32),
                 jax.ShapeDtypeStruct((D_MODEL, D_MODEL), F32), jax.ShapeDtypeStruct((PLE, D_MODEL), F32),
                 jax.ShapeDtypeStruct((D_MODEL, D_MODEL), F32),
                 jax.ShapeDtypeStruct((1, ATTN_W), F32), jax.ShapeDtypeStruct((1, CONV_W), F32),
                 jax.ShapeDtypeStruct((1, D_MODEL), F32), jax.ShapeDtypeStruct((3, CONV_W), F32),
                 jax.ShapeDtypeStruct((1, LANES), F32)]
    return pl.pallas_call(
        body, name="tail", grid=(nt,), in_specs=in_specs, out_specs=out_specs, out_shape=out_shape,
        scratch_shapes=[pltpu.VMEM((tm, CONV_W), F32)] * 3,
        compiler_params=_params(dimension_semantics=("arbitrary",)),
    )(x, o, proj, proj, proj, proj, proj, proj, proj, p, tgt, g_oa, g_oc, g_pl, conv_w, w_o, w_pl, w_plg)


def _attn_bwd(q, k, v, do, lse_row, delta_row, tk):
    T = q.shape[1]
    tq = tk
    nq = T // tq
    rc = min(SOFTMAX_ROWS, tk)

    def body(q_ref, k_ref, v_ref, do_ref, lse_ref, dl_ref, dq_ref, dk_ref, dv_ref,
             s0, s1, d0, d1, p0, p1, g0, g1, dk_acc, dv_acc):
        kj = pl.program_id(1)
        s_buf, dp_buf, p_buf, g_buf = (s0, s1), (d0, d1), (p0, p1), (g0, g1)

        @pl.when(kj == 0)
        def _():
            dq_ref[...] = jnp.zeros_like(dq_ref)

        def q_start(t):
            return pl.multiple_of((nq - 1 - t) * tq, tq)

        def matmuls(t, slot):
            qs = q_start(t)
            s_buf[slot][...] = _dot_nt(k_ref[0], q_ref[0, pl.ds(qs, tq), :])
            dp_buf[slot][...] = _dot_nt(v_ref[0], do_ref[pl.ds(qs, tq), :])

        def pointwise(t, slot, masked):
            qs = q_start(t)
            lse2 = lse_ref[0, :, pl.ds(qs, tq)] * LOG2E
            dl = dl_ref[0, :, pl.ds(qs, tq)]
            for r0 in range(0, tk, rc):
                st = s_buf[slot][r0:r0 + rc, :]
                if masked:
                    row = lax.broadcasted_iota(jnp.int32, (rc, tq), 0)
                    col = lax.broadcasted_iota(jnp.int32, (rc, tq), 1)
                    st = jnp.where(row + r0 <= col, st, NEG)
                pt = jnp.exp2(st * EXP2_SCALE - lse2)
                p_buf[slot][r0:r0 + rc, :] = pt.astype(BF16)
                g_buf[slot][r0:r0 + rc, :] = (pt * (dp_buf[slot][r0:r0 + rc, :] - dl) * SCALE).astype(BF16)

        def accumulate(t, slot):
            qs = q_start(t)
            dv_acc[...] += _dot(p_buf[slot][...], do_ref[pl.ds(qs, tq), :])
            dk_acc[...] += _dot(g_buf[slot][...], q_ref[0, pl.ds(qs, tq), :])
            dq_ref[0, pl.ds(qs, tq), :] += _dot_tn(g_buf[slot][...], k_ref[0])

        def last():
            dk_ref[0] = dk_acc[...]
            dv_ref[0] = dv_acc[...]

        dk_acc[...] = jnp.zeros_like(dk_acc)
        dv_acc[...] = jnp.zeros_like(dv_acc)
        for late in (p1, g1):
            late[...] = jnp.zeros_like(late)
        _chunk_pipeline(nq - 1 - kj, 1, matmuls, pointwise, accumulate, last)

    return pl.pallas_call(
        body, name="attn_bwd", grid=(N_HEADS, T // tk),
        in_specs=[pl.BlockSpec((1, T, HEAD_PAD), lambda h, j: (h, 0, 0)),
                  pl.BlockSpec((1, tk, HEAD_PAD), lambda h, j: (h, j, 0)),
                  pl.BlockSpec((1, tk, V_DIM), lambda h, j: (h, j, 0)),
                  pl.BlockSpec((T, V_DIM), lambda h, j: (0, h)),
                  pl.BlockSpec((1, 1, T), lambda h, j: (h, 0, 0)),
                  pl.BlockSpec((1, 1, T), lambda h, j: (h, 0, 0))],
        out_specs=[pl.BlockSpec((1, T, HEAD_PAD), lambda h, j: (h, 0, 0)),
                   pl.BlockSpec((1, tk, HEAD_PAD), lambda h, j: (h, j, 0)),
                   pl.BlockSpec((1, tk, V_DIM), lambda h, j: (h, j, 0))],
        out_shape=[jax.ShapeDtypeStruct((N_HEADS, T, HEAD_PAD), F32), jax.ShapeDtypeStruct((N_HEADS, T, HEAD_PAD), F32),
                   jax.ShapeDtypeStruct((N_HEADS, T, V_DIM), F32)],
        scratch_shapes=[pltpu.VMEM((tk, tq), F32)] * 4 + [pltpu.VMEM((tk, tq), BF16)] * 4
                       + [pltpu.VMEM((tk, HEAD_PAD), F32), pltpu.VMEM((tk, V_DIM), F32)],
        compiler_params=_params(dimension_semantics=("arbitrary", "arbitrary")),
    )(q, k, v, do, lse_row, delta_row)


def _bwd_proj(x, dx1, pos, proj, dq, dk, dv, dtail, du, g_in, w_in, g_cq, w_uq, g_ckv, w_ukv, gq, gk, conv_w,
              invf, sgn, tm):
    T = x.shape[0]
    nt = T // tm

    ts = min(SUB_TILE, tm)

    def body(x_ref, dx1_ref, pos_ref, lat_ref, cc_ref, cx_ref, dq_ref, dk_ref, dv_ref, dtail_ref, du_ref, dun_ref, *rest):
        consts, (gx_ref, h_ref, dproj_ref), sums = rest[:11], rest[11:14], rest[14:]
        cw_ref = consts[8]
        i = pl.program_id(0)

        @pl.when(i == 0)
        def _():
            for r in sums:
                r[...] = jnp.zeros_like(r)

        du_v = du_ref[...]
        not_last = jnp.where(i < nt - 1, 1.0, 0.0)
        nx0 = dun_ref[0:1, :] * not_last
        nx1 = dun_ref[1:2, :] * not_last
        row = lax.broadcasted_iota(jnp.int32, du_v.shape, 0)
        du1 = jnp.where(row == tm - 1, nx0, pltpu.roll(du_v, tm - 1, 0))
        du2 = jnp.where(row == tm - 2, nx0, jnp.where(row == tm - 1, nx1, pltpu.roll(du_v, tm - 2, 0)))
        dvc = cw_ref[2:3, :] * du_v + cw_ref[1:2, :] * du1 + cw_ref[0:1, :] * du2
        dproj_ref[:, 1536:2048] = (dvc * cx_ref[...]).astype(BF16)
        dproj_ref[:, 2048:2560] = (dvc * cc_ref[...]).astype(BF16)

        for r0 in range(0, tm, ts):
            rows = slice(r0, r0 + ts)
            work(x_ref.at[rows, :], dx1_ref.at[rows, :], pos_ref.at[rows, :], lat_ref.at[rows, :],
                 dq_ref.at[:, rows, :], dk_ref.at[:, rows, :], dv_ref.at[:, rows, :], dtail_ref.at[rows, :], *consts,
                 gx_ref.at[rows, :], h_ref.at[:, rows], dproj_ref.at[rows, :], *sums)

    def work(x_ref, dx1_ref, pos_ref, lat_ref, dq_ref, dk_ref, dv_ref, dtail_ref,
             g_in_ref, w_in_ref, g_cq_ref, w_uq_ref, g_ckv_ref, w_ukv_ref, gq_ref, gk_ref, cw_ref, invf_ref, sgn_ref,
             gx_ref, h_ref, dproj_ref, dw_uq_ref, dw_ukv_ref, dg_in_ref, dg_cq_ref, dg_ckv_ref, dgq_ref, dgk_ref):
        xv = x_ref[...]
        r0 = _rep(_inv_rms_mxu(xv), D_MODEL)
        xh0 = xv * r0
        g_in = g_in_ref[...]
        h_ref[...] = (xh0 * g_in).astype(BF16).T

        c_q = lat_ref[:, 0:Q_LORA]
        rq = _rep(_inv_rms_mxu(c_q), Q_LORA)
        xq = c_q * rq
        g_cq = g_cq_ref[...]
        cqn = (xq * g_cq).astype(BF16)
        c_kv = lat_ref[:, Q_LORA:Q_LORA + KV_LORA]
        rkv = _inv_rms_mxu(c_kv)
        xkv = c_kv * rkv
        g_ckv = g_ckv_ref[...]
        ckvn = (xkv * g_ckv).astype(BF16)
        kpe = lat_ref[:, 384:512]
        kpe_sq = kpe * kpe
        cos_b, sin_b = _rope_tables(pos_ref, invf_ref, sgn_ref)
        gq_a, gq_b = gq_ref[:, 0:NOPE], gq_ref[:, NOPE:HEAD_PAD]
        gk_a, gk_b = gk_ref[:, 0:NOPE], gk_ref[:, NOPE:HEAD_PAD]

        dproj_ref[:, 512:1536] = dtail_ref[:, 0:1024]
        dproj_ref[:, 2560:3072] = dtail_ref[:, 1024:1536]

        def dh_part(c0):
            return _dot_nt(dproj_ref[:, c0:c0 + 512], w_in_ref[:, c0:c0 + 512])

        later_chunks = ((512,), (1024,), (1536, 2048), (2560,))
        dh = jnp.zeros((ts, D_MODEL), F32)
        acc = dict(dh=dh, dkpe=jnp.zeros((ts, LANES), F32), dcqn=jnp.zeros((ts, Q_LORA), F32),
                   dckvn=jnp.zeros((ts, KV_LORA), F32))

        def dh_chunks():
            for chunks in later_chunks:
                for chunk in chunks:
                    acc["dh"] = acc["dh"] + dh_part(chunk)
                    yield

        def queries(hd):
            c0 = hd * HEAD_PAD
            qh = _dot(cqn, w_uq_ref[:, c0:c0 + HEAD_PAD])
            yield
            a, b = qh[:, 0:NOPE], qh[:, NOPE:HEAD_PAD]
            r = lax.rsqrt(_lane_sum(a * a + b * b) / QK_DIM + EPS)
            yield
            xa, xb = a * r, b * r
            dan = dq_ref[hd, :, 0:NOPE]
            dbr = dq_ref[hd, :, NOPE:HEAD_PAD]
            dbn = dbr * cos_b + _swap_rope_halves(dbr * sin_b)
            yield
            dgq_ref[:, 0:NOPE] += _colsum(dan * xa)
            dgq_ref[:, NOPE:HEAD_PAD] += _colsum(dbn * xb)
            dxa, dxb = dan * gq_a, dbn * gq_b
            cq = _lane_sum(dxa * xa + dxb * xb) / QK_DIM
            yield
            dqh = jnp.concatenate([r * (dxa - xa * cq), r * (dxb - xb * cq)], axis=-1).astype(BF16)
            yield
            dw_uq_ref[:, c0:c0 + HEAD_PAD] += _dot_tn(cqn, dqh)
            yield
            acc["dcqn"] = acc["dcqn"] + _dot_nt(dqh, w_uq_ref[:, c0:c0 + HEAD_PAD])
            yield

        def keys(hd):
            c0 = hd * HEAD_PAD
            kvh = _dot(ckvn, w_ukv_ref[:, c0:c0 + HEAD_PAD])
            yield
            ka = kvh[:, 0:NOPE]
            rk = lax.rsqrt(_lane_sum(ka * ka + kpe_sq) / QK_DIM + EPS)
            yield
            xka, xkb = ka * rk, kpe * rk
            dkan = dk_ref[hd, :, 0:NOPE]
            dkbr = dk_ref[hd, :, NOPE:HEAD_PAD]
            dkbn = dkbr * cos_b + _swap_rope_halves(dkbr * sin_b)
            yield
            dgk_ref[:, 0:NOPE] += _colsum(dkan * xka)
            dgk_ref[:, NOPE:HEAD_PAD] += _colsum(dkbn * xkb)
            dxka, dxkb = dkan * gk_a, dkbn * gk_b
            ck = _lane_sum(dxka * xka + dxkb * xkb) / QK_DIM
            yield
            acc["dkpe"] = acc["dkpe"] + rk * (dxkb - xkb * ck)
            dkvh = jnp.concatenate([rk * (dxka - xka * ck), dv_ref[hd]], axis=-1).astype(BF16)
            yield
            dw_ukv_ref[:, c0:c0 + HEAD_PAD] += _dot_tn(ckvn, dkvh)
            yield
            acc["dckvn"] = acc["dckvn"] + _dot_nt(dkvh, w_ukv_ref[:, c0:c0 + HEAD_PAD])
            yield

        chains = [dh_chunks()]
        for hd in range(N_HEADS):
            chains += [queries(hd), keys(hd)]
        _round_robin(chains, 5)
        dh, dkpe, dcqn, dckvn = acc["dh"], acc["dkpe"], acc["dcqn"], acc["dckvn"]

        dg_cq_ref[...] += _colsum(dcqn * xq)
        dxq = dcqn * g_cq
        dproj_ref[:, 0:Q_LORA] = (rq * (dxq - xq * _rep(_lane_sum(dxq * xq) / Q_LORA, Q_LORA))).astype(BF16)
        dg_ckv_ref[...] += _colsum(dckvn * xkv)
        dxkv = dckvn * g_ckv
        dproj_ref[:, 256:384] = (rkv * (dxkv - xkv * (_lane_sum(dxkv * xkv) / KV_LORA))).astype(BF16)
        dproj_ref[:, 384:512] = dkpe.astype(BF16)
        dh = dh + dh_part(0)
        dg_in_ref[...] += _colsum(dh * xh0)
        dxh = dh * g_in
        gx_ref[...] = dx1_ref[...] + r0 * (dxh - xh0 * _rep(_lane_sum(dxh * xh0) / D_MODEL, D_MODEL))

    row = lambda i: (i, 0)
    col = lambda c: (lambda i: (i, c))
    head_rows = lambda i: (0, i, 0)
    nxt = lambda i: (jnp.minimum((i + 1) * (tm // 8), T // 8 - 1), 0)
    in_specs = [pl.BlockSpec((tm, D_MODEL), row), pl.BlockSpec((tm, D_MODEL), row), pl.BlockSpec((tm, 1), row),
                pl.BlockSpec((tm, 512), col(0)), pl.BlockSpec((tm, 512), col(3)), pl.BlockSpec((tm, 512), col(4)),
                pl.BlockSpec((N_HEADS, tm, HEAD_PAD), head_rows), pl.BlockSpec((N_HEADS, tm, HEAD_PAD), head_rows),
                pl.BlockSpec((N_HEADS, tm, V_DIM), head_rows), pl.BlockSpec((tm, 1536), row),
                pl.BlockSpec((tm, CONV_W), row), pl.BlockSpec((8, CONV_W), nxt),
                _full((1, D_MODEL)), _full((D_MODEL, PROJ_EXT)), _full((1, Q_LORA)), _full((Q_LORA, N_HEADS * HEAD_PAD)),
                _full((1, KV_LORA)), _full((KV_LORA, N_HEADS * HEAD_PAD)), _full((1, HEAD_PAD)), _full((1, HEAD_PAD)),
                _full((3, CONV_W)), _full((1, LANES)), _full((1, LANES))]
    out_specs = [pl.BlockSpec((tm, D_MODEL), row), pl.BlockSpec((D_MODEL, tm), lambda i: (0, i)),
                 pl.BlockSpec((tm, PROJ_EXT), row),
                 _full((Q_LORA, N_HEADS * HEAD_PAD)), _full((KV_LORA, N_HEADS * HEAD_PAD)),
                 _full((1, D_MODEL)), _full((1, Q_LORA)), _full((1, KV_LORA)), _full((1, HEAD_PAD)), _full((1, HEAD_PAD))]
    out_shape = [jax.ShapeDtypeStruct((T, D_MODEL), F32), jax.ShapeDtypeStruct((D_MODEL, T), BF16),
                 jax.ShapeDtypeStruct((T, PROJ_EXT), BF16),
                 jax.ShapeDtypeStruct((Q_LORA, N_HEADS * HEAD_PAD), F32), jax.ShapeDtypeStruct((KV_LORA, N_HEADS * HEAD_PAD), F32),
                 jax.ShapeDtypeStruct((1, D_MODEL), F32), jax.ShapeDtypeStruct((1, Q_LORA), F32),
                 jax.ShapeDtypeStruct((1, KV_LORA), F32), jax.ShapeDtypeStruct((1, HEAD_PAD), F32),
                 jax.ShapeDtypeStruct((1, HEAD_PAD), F32)]
    return pl.pallas_call(
        body, name="bwd_proj", grid=(nt,), in_specs=in_specs, out_specs=out_specs, out_shape=out_shape,
        compiler_params=_params(dimension_semantics=("arbitrary",)),
    )(x, dx1, pos, proj, proj, proj, dq, dk, dv, dtail, du, du, g_in, w_in, g_cq, w_uq, g_ckv, w_ukv, gq, gk, conv_w,
      invf, sgn)


def _matmul_acc(a, b, tt, tn, parts):
    M, T = a.shape
    N = b.shape[1]
    n = len(parts)
    grid = (N // tn, T // tt)

    def body(a_ref, b_ref, *rest):
        part_refs, o_ref, out_refs, sems = rest[:n], rest[n], rest[n + 1:2 * n + 1], rest[2 * n + 1:]
        j, t = pl.program_id(0), pl.program_id(1)
        if n:
            start, drain = _scatter_steps(part_refs, out_refs, *sems)
            pl.when(jnp.logical_and(j == 0, t == 0))(start)

        @pl.when(t == 0)
        def _():
            o_ref[...] = jnp.zeros_like(o_ref)

        o_ref[...] += _dot(a_ref[...], b_ref[...])
        if n:
            pl.when(jnp.logical_and(j == grid[0] - 1, t == grid[1] - 1))(drain)

    sems = [pltpu.SemaphoreType.DMA((3 * n,)), pltpu.SemaphoreType.DMA((3 * n,)), pltpu.SemaphoreType.DMA((n,))]
    outs = pl.pallas_call(
        body, name="dw_in", grid=grid,
        in_specs=[pl.BlockSpec((M, tt), lambda j, t: (0, t)), pl.BlockSpec((tt, tn), lambda j, t: (t, j))] + [_ANY] * n,
        out_specs=[pl.BlockSpec((M, tn), lambda j, t: (0, j))] + [_ANY] * n,
        out_shape=[jax.ShapeDtypeStruct((M, N), F32)] + _scattered_shapes(parts),
        scratch_shapes=sems if n else [],
        compiler_params=_params(dimension_semantics=("arbitrary", "arbitrary")),
    )(a, b, *parts)
    return outs[0], outs[1:]


def _add_chips(parts, small_parts):
    arrays = list(parts) + [small_parts]

    def body(*refs):
        ins, outs = refs[:len(arrays)], refs[len(arrays):]
        for a_ref, o_ref in zip(ins, outs):
            part = lambda k: a_ref[k].astype(F32)
            o_ref[...] = ((part(0) + part(1)) + part(2)) + part(3)

    in_specs, out_specs, out_shape = [], [], []
    for a in arrays:
        _, rows, cols = a.shape
        in_specs.append(pl.BlockSpec((N_CHIPS, rows // 2, cols), lambda i: (0, i, 0)))
        out_specs.append(pl.BlockSpec((rows // 2, cols), lambda i: (i, 0)))
        out_shape.append(jax.ShapeDtypeStruct((rows, cols), F32))
    outs = pl.pallas_call(body, name="add_chips", grid=(2,), in_specs=in_specs, out_specs=out_specs,
                          out_shape=out_shape, compiler_params=_params(dimension_semantics=("arbitrary",)))(*arrays)
    return outs[:-1], outs[-1]


def _adamw(w, g, m, v, name):
    rows, cols = w.shape
    rb = 256 if rows * cols > 512 * 1024 else rows

    def body(w_ref, g_ref, m_ref, v_ref, d_ref, nm_ref, nv_ref):
        _adamw_math(g_ref[...], w_ref, m_ref, v_ref, d_ref, nm_ref, nv_ref)

    spec = pl.BlockSpec((rb, cols), lambda i: (i, 0))
    shp = jax.ShapeDtypeStruct(w.shape, F32)
    return pl.pallas_call(body, name=name, grid=(rows // rb,), in_specs=[spec] * 4, out_specs=[spec] * 3,
                          out_shape=[shp] * 3)(w, g, m, v)


def _adamw_math(gv, w_ref, m_ref, v_ref, d_ref, nm_ref, nv_ref):
    nm = B1 * m_ref[...] + (1.0 - B1) * gv
    nv = B2 * v_ref[...] + (1.0 - B2) * (gv * gv)
    m_hat = nm / (1.0 - B1 ** STEP)
    v_hat = nv / (1.0 - B2 ** STEP)
    d_ref[...] = -LR * (m_hat / (jnp.sqrt(v_hat) + ADAM_EPS) + WD * w_ref[...])
    nm_ref[...] = nm
    nv_ref[...] = nv


def _adamw_halves(w, mine, other, m, v, c, name):
    hr, cols = mine.shape

    def body(c_ref, w_ref, mine_ref, other_ref, m_ref, v_ref, g_ref, d_ref, nm_ref, nv_ref):
        gv = jnp.where(pl.program_id(0) == c_ref[0], mine_ref[...], other_ref[...])
        g_ref[...] = gv
        _adamw_math(gv, w_ref, m_ref, v_ref, d_ref, nm_ref, nv_ref)

    half = pl.BlockSpec((hr, cols), lambda i, c_ref: (i, 0))
    whole = pl.BlockSpec((hr, cols), lambda i, c_ref: (0, 0))
    shp = jax.ShapeDtypeStruct(w.shape, F32)
    return pl.pallas_call(
        body, name=name, out_shape=[shp] * 4,
        grid_spec=pltpu.PrefetchScalarGridSpec(num_scalar_prefetch=1, grid=(2,), in_specs=[half, whole, whole, half, half],
                                               out_specs=[half] * 4),
        compiler_params=_params(dimension_semantics=("arbitrary",)),
    )(c.reshape(1), w, mine, other, m, v)


_ANY = pl.BlockSpec(memory_space=pl.ANY)


def _mesh_pos():
    return lax.axis_index("x"), lax.axis_index("y"), lax.axis_index("c")


def _other_chips(x, y):
    return [(1 - x, y), (x, 1 - y), (1 - x, 1 - y)]


def _remote(src, dst, send_sems, recv_sems, k, to):
    return pltpu.make_async_remote_copy(src_ref=src, dst_ref=dst, send_sem=send_sems.at[k], recv_sem=recv_sems.at[k],
                                        device_id=to, device_id_type=MESH)


def _gather_weights(shards):
    n = len(shards)

    def body(*refs):
        start, forward, drain = _gather_steps([s.shape for s in shards], refs[:n], refs[n:2 * n], refs[2 * n:3 * n],
                                              *refs[3 * n:])
        start()
        forward()
        drain()

    vmem = pl.BlockSpec(memory_space=pltpu.VMEM)
    return pl.pallas_call(
        body, name="gather_weights", in_specs=[vmem] * n, out_specs=[_ANY] * n,
        out_shape=_gathered_shapes(shards), scratch_shapes=_gather_scratch(shards), compiler_params=_params(),
    )(*shards)


def _gathered_shapes(shards):
    return [jax.ShapeDtypeStruct((N_CHIPS,) + s.shape, BF16) for s in shards]


def _gather_scratch(shards):
    n = len(shards)
    return ([pltpu.VMEM(s.shape, BF16) for s in shards]
            + [pltpu.SemaphoreType.DMA((6 * n,)), pltpu.SemaphoreType.DMA((6 * n,)), pltpu.SemaphoreType.DMA((n,))])


def _gather_steps(shapes, ins, outs, stage, send_sems, recv_sems, local_sems):
    n = len(shapes)
    halved = [s[0] % 32 == 0 for s in shapes]

    def part(i, ref, hc):
        if not halved[i]:
            return ref
        hr = shapes[i][0] // 2
        return ref.at[pl.ds(hc * hr, hr), :]

    def to_chip(i, j, x, y, c):
        cx, cy = _other_chips(x, y)[j]
        return _remote(part(i, stage[i], c), part(i, outs[i].at[2 * x + y], c), send_sems, recv_sems, 6 * i + j, (cx, cy, c))

    def to_sibling(i, j, x, y, c):
        cx, cy = _other_chips(x, y)[j]
        got = part(i, outs[i].at[2 * cx + cy], c)
        return _remote(got, got, send_sems, recv_sems, 6 * i + 3 + j, (x, y, 1 - c))

    def local(i, x, y):
        return pltpu.make_async_copy(stage[i], outs[i].at[2 * x + y], local_sems.at[i])

    def start():
        x, y, c = _mesh_pos()
        for i in range(n):
            stage[i][...] = ins[i][...].astype(BF16)
            local(i, x, y).start()
            for j in range(3):
                to_chip(i, j, x, y, c).start()

    def forward():
        x, y, c = _mesh_pos()
        for i in range(n):
            for j, (cx, cy) in enumerate(_other_chips(x, y)):
                got = part(i, outs[i].at[2 * cx + cy], c)
                _remote(got, got, send_sems, recv_sems, 6 * i + j, (cx, cy, c)).wait_recv()
                if halved[i]:
                    to_sibling(i, j, x, y, c).start()

    def drain():
        x, y, c = _mesh_pos()
        for i in range(n):
            for j, (cx, cy) in enumerate(_other_chips(x, y)):
                if halved[i]:
                    got = part(i, outs[i].at[2 * cx + cy], 1 - c)
                    _remote(got, got, send_sems, recv_sems, 6 * i + 3 + j, (x, y, 1 - c)).wait_recv()
                    to_sibling(i, j, x, y, c).wait_send()
                to_chip(i, j, x, y, c).wait_send()
            local(i, x, y).wait()

    return start, forward, drain


def _swap_halves(grads, whole, name):
    n, m = len(grads), len(grads) + len(whole)

    def body(*refs):
        ins, outs, send_sems, recv_sems = refs[:m], refs[m:2 * m], refs[2 * m], refs[2 * m + 1]
        x, y, c = _mesh_pos()
        cps = []
        for i in range(m):
            src = ins[i]
            if i < n:
                hr = grads[i].shape[1] // 2
                src = src.at[:, pl.ds((1 - c) * hr, hr), :]
            cp = _remote(src, outs[i], send_sems, recv_sems, i, (x, y, 1 - c))
            cp.start()
            cps.append(cp)
        for cp in cps:
            cp.wait()

    out_shape = [jax.ShapeDtypeStruct((g.shape[0], g.shape[1] // 2, g.shape[2]), F32) for g in grads]
    out_shape += [jax.ShapeDtypeStruct(w.shape, F32) for w in whole]
    outs = pl.pallas_call(
        body, name=name, in_specs=[_ANY] * m, out_specs=[_ANY] * m, out_shape=out_shape,
        scratch_shapes=[pltpu.SemaphoreType.DMA((m,)), pltpu.SemaphoreType.DMA((m,))],
    )(*grads, *whole)
    return outs[:n], outs[n:]


def _scattered_shapes(parts):
    return [jax.ShapeDtypeStruct(p.shape if p.ndim == 3 else (N_CHIPS,) + p.shape, p.dtype) for p in parts]


def _scatter_steps(ins, outs, send_sems, recv_sems, local_sems):
    n = len(ins)

    def src(i, k):
        return ins[i].at[k] if len(ins[i].shape) == 3 else ins[i]

    def sends(x, y, c):
        return [_remote(src(i, 2 * cx + cy), outs[i].at[2 * x + y], send_sems, recv_sems, 3 * i + j, (cx, cy, c))
                for i in range(n) for j, (cx, cy) in enumerate(_other_chips(x, y))]

    def local(i, x, y):
        return pltpu.make_async_copy(src(i, 2 * x + y), outs[i].at[2 * x + y], local_sems.at[i])

    def start():
        x, y, c = _mesh_pos()
        for i in range(n):
            local(i, x, y).start()
        for cp in sends(x, y, c):
            cp.start()

    def drain():
        x, y, c = _mesh_pos()
        for i in range(n):
            for j, (cx, cy) in enumerate(_other_chips(x, y)):
                got = outs[i].at[2 * cx + cy]
                _remote(got, got, send_sems, recv_sems, 3 * i + j, (cx, cy, c)).wait_recv()
        for cp in sends(x, y, c):
            cp.wait_send()
        for i in range(n):
            local(i, x, y).wait()

    return start, drain


def _add_pair(grads, from_sibling, small, small_sibling, c):
    n = len(grads)

    def body(c_ref, *refs):
        ins, outs = refs[:2 * n + 2], refs[2 * n + 2:]
        for i in range(n + 1):
            outs[i][...] = (ins[2 * i][...] + ins[2 * i + 1][...]).astype(outs[i].dtype)

    in_specs, out_specs, out_shape, args = [], [], [], []
    for g, r in zip(grads, from_sibling):
        _, hr, cols = r.shape
        in_specs += [pl.BlockSpec((1, hr, cols), lambda k, c_ref: (k, c_ref[0], 0)),
                     pl.BlockSpec((1, hr, cols), lambda k, c_ref: (k, 0, 0))]
        out_specs.append(pl.BlockSpec((1, hr, cols), lambda k, c_ref: (k, 0, 0)))
        out_shape.append(jax.ShapeDtypeStruct(r.shape, BF16))
        args += [g, r]
    whole = pl.BlockSpec(small.shape, lambda k, c_ref: (0, 0))
    in_specs += [whole, whole]
    out_specs.append(whole)
    out_shape.append(jax.ShapeDtypeStruct(small.shape, F32))
    outs = pl.pallas_call(
        body, name="add_pair", out_shape=out_shape,
        grid_spec=pltpu.PrefetchScalarGridSpec(num_scalar_prefetch=1, grid=(N_CHIPS,), in_specs=in_specs,
                                               out_specs=out_specs),
        compiler_params=_params(dimension_semantics=("arbitrary",)),
    )(c.reshape(1), *args, small, small_sibling)
    return outs[:n], outs[n]


def _scatter_to_chips(grad, from_sibling):
    hr = from_sibling.shape[1]

    def body(g_in, r_in, out, g_buf, r_buf, p_buf, load_sems, send_sems, recv_sems, local_sems):
        c = lax.axis_index("c")
        loads = (pltpu.make_async_copy(g_in.at[:, pl.ds(c * hr, hr), :], g_buf, load_sems.at[0]),
                 pltpu.make_async_copy(r_in, r_buf, load_sems.at[1]))
        for cp in loads:
            cp.start()
        for cp in loads:
            cp.wait()
        p_buf[...] = (g_buf[...] + r_buf[...]).astype(BF16)
        start, drain = _scatter_steps([p_buf], [out], send_sems, recv_sems, local_sems)
        start()
        drain()

    return pl.pallas_call(
        body, name="scatter_grads", in_specs=[_ANY] * 2, out_specs=_ANY,
        out_shape=jax.ShapeDtypeStruct(from_sibling.shape, BF16),
        scratch_shapes=[pltpu.VMEM(from_sibling.shape, F32)] * 2 + [pltpu.VMEM(from_sibling.shape, BF16)]
                       + [pltpu.SemaphoreType.DMA((2,)), pltpu.SemaphoreType.DMA((3,)), pltpu.SemaphoreType.DMA((3,)),
                          pltpu.SemaphoreType.DMA((1,))],
        compiler_params=_params(),
    )(grad, from_sibling)


def _share_halves(halves):
    n = len(halves)

    def body(*refs):
        ins, outs, send_sems, recv_sems = refs[:n], refs[n:2 * n], refs[2 * n], refs[2 * n + 1]
        x, y, c = _mesh_pos()
        cps = [_remote(ins[i], outs[i], send_sems, recv_sems, i, (x, y, 1 - c)) for i in range(n)]
        for cp in cps:
            cp.start()
        for cp in cps:
            cp.wait()

    return pl.pallas_call(
        body, name="share_halves", in_specs=[_ANY] * n, out_specs=[_ANY] * n,
        out_shape=[jax.ShapeDtypeStruct(h.shape, h.dtype) for h in halves],
        scratch_shapes=[pltpu.SemaphoreType.DMA((n,)), pltpu.SemaphoreType.DMA((n,))],
    )(*halves)


SHARD_COLS_IN = IN_TOTAL // N_CHIPS
KPE_END = Q_LORA + KV_LORA + ROPE


def _by_cols(a):
    return a.transpose(1, 0, 2).reshape(a.shape[1], N_CHIPS * a.shape[2])


def _assemble_early(c_in, c_uq, c_ukv, c_conv):
    w_in_e = jnp.concatenate([c_in[0][:, :KPE_END], jnp.zeros((D_MODEL, 64), BF16), c_in[0][:, KPE_END:],
                              c_in[1], c_in[2], c_in[3]], axis=1)
    w_uq_e = _by_cols(jnp.pad(c_uq, ((0, 0), (0, 0), (0, HEAD_PAD - QK_DIM))))
    return w_in_e, w_uq_e, _by_cols(c_ukv), _by_cols(c_conv).astype(F32)


def _assemble_late(c_o, c_pl, c_plg):
    return c_o.reshape(D_MODEL, D_MODEL), _by_cols(c_pl), c_plg.reshape(D_MODEL, D_MODEL)


def _split_w_in(dw_in_e):
    first = jnp.concatenate([dw_in_e[:, :KPE_END], dw_in_e[:, KPE_END + 64:SHARD_COLS_IN + 64]], axis=1)
    rest = [dw_in_e[:, SHARD_COLS_IN * k + 64:SHARD_COLS_IN * (k + 1) + 64] for k in range(1, N_CHIPS)]
    return jnp.stack([first] + rest)


def _split_others(dw_uq_e, dw_ukv, dw_o, dw_pl, dw_plg):
    chip_major = lambda a: a.reshape(a.shape[0], N_CHIPS, a.shape[1] // N_CHIPS).transpose(1, 0, 2)
    return [chip_major(dw_uq_e)[:, :, :QK_DIM], chip_major(dw_ukv), dw_o.reshape(N_CHIPS, D_MODEL // N_CHIPS, D_MODEL),
            chip_major(dw_pl), dw_plg.reshape(N_CHIPS, D_MODEL // N_CHIPS, D_MODEL)]


def _local_step(x, p, pos, tgt, gains, early, late_shards, late_gathered, tm, tq):
    w_in_e, w_uq_e, w_ukv, conv_w = early
    g_in, g_cq, g_ckv, g_q, g_k, g_oa, g_oc, g_pl = gains
    T = x.shape[0]
    zpad = lambda a, n: jnp.concatenate([a, jnp.zeros(a.shape[:-1] + (n,), a.dtype)], axis=-1)
    gq, gk = zpad(g_q, HEAD_PAD - QK_DIM), zpad(g_k, HEAD_PAD - QK_DIM)
    inv_freq = 1.0 / (ROPE_THETA ** (jnp.arange(0, ROPE, 2, dtype=F32) / ROPE))
    invf = jnp.concatenate([inv_freq, inv_freq, jnp.zeros((64,), F32)]).reshape(1, LANES)
    sgn = jnp.concatenate([-jnp.ones((32,), F32), jnp.ones((32,), F32), jnp.zeros((64,), F32)]).reshape(1, LANES)

    (proj, q, k, v), gathered = _fwd_proj(x, pos, g_in, w_in_e, g_cq, w_uq_e, g_ckv, w_ukv, gq, gk, invf, sgn,
                                          late_shards, min(2 * tm, T))
    w_o, w_pl, w_plg = _assemble_late(*(gathered if late_shards else late_gathered))
    o, lse = _attn_fwd(q, k, v, tq)
    (dx1, do, delta, dtail, du, dw_o, dw_pl, dw_plg, dg_oa, dg_oc, dg_pl, dconv, loss) = _tail(
        x, o, proj, p, tgt, g_oa, g_oc, g_pl, conv_w, w_o, w_pl, w_plg, tm)
    dq, dk, dv = _attn_bwd(q, k, v, do, lse, delta, tq)
    (gx, h, dproj, dw_uq_e, dw_ukv, dg_in, dg_cq, dg_ckv, dgq, dgk) = _bwd_proj(
        x, dx1, pos, proj, dq, dk, dv, dtail, du, g_in, w_in_e, g_cq, w_uq_e, g_ckv, w_ukv, gq, gk, conv_w, invf, sgn, tm)
    wgrads = (dw_uq_e, dw_ukv, dw_o, dw_pl, dw_plg)
    ggrads = (dg_in, dg_cq, dg_ckv, dgq, dgk, dg_oa, dg_oc, dg_pl)
    return loss, gx, (h, dproj), wgrads, ggrads, dconv


def kernel(x, p, positions, g_in, w_in, g_cq, w_uq, g_ckv, w_ukv, g_q, g_k, conv_w, g_oa, g_oc, w_o, w_pl, w_plg, g_pl, loss_target, m_g_in, m_w_in, m_g_cq, m_w_uq, m_g_ckv, m_w_ukv, m_g_q, m_g_k, m_conv_w, m_g_oa, m_g_oc, m_w_o, m_w_pl, m_w_plg, m_g_pl, v_g_in, v_w_in, v_g_cq, v_w_uq, v_g_ckv, v_w_ukv, v_g_q, v_g_k, v_conv_w, v_g_oa, v_g_oc, v_w_o, v_w_pl, v_w_plg, v_g_pl):
    T = x.shape[1]
    c = lax.axis_index("c")
    chip = 2 * lax.axis_index("x") + lax.axis_index("y")
    gains = [g.reshape(1, -1) for g in (g_in, g_cq, g_ckv, g_q, g_k, g_oa, g_oc, g_pl)]

    early = _assemble_early(*_gather_weights([w_in[0], w_uq[0], w_ukv[0], conv_w[0]]))

    loss, gx, (h_t, dproj), wgrads, ggrads, dconv = _local_step(
        x[0], p[0, 0], positions.reshape(T, 1), loss_target[0], gains, early, [w_o[0], w_pl[0], w_plg[0]], None, 256, 512)

    others_cm = _split_others(*wgrads)
    small_parts = [a.reshape(-1, LANES) for a in (*ggrads, loss, dconv)]
    small_rows = [a.shape[0] for a in small_parts]
    tile_rows = [-(-r // 8) * 8 for r in small_rows]
    tile_rows[-1] += -sum(tile_rows) % 16
    small = jnp.concatenate([jnp.pad(a, ((0, t - r), (0, 0))) for a, r, t in zip(small_parts, small_rows, tile_rows)])
    from_sibling, (small_sibling,) = _swap_halves(others_cm, [small], "pair_grads")
    chip_parts, chip_small = _add_pair(others_cm, from_sibling, small, small_sibling, c)
    dw_in_e, exchanged = _matmul_acc(h_t, dproj, min(4096, T), 512, [*chip_parts, chip_small])
    w_in_cm = _split_w_in(dw_in_e)
    (w_in_sibling,), _ = _swap_halves([w_in_cm], [], "pair_w_in")
    by_chip = [_scatter_to_chips(w_in_cm, w_in_sibling), *exchanged[:-1]]
    halves, small_total = _add_chips(by_chip, exchanged[-1])
    other_halves = _share_halves(halves)

    gg, off = [], 0
    for rows, tiled in zip(small_rows, tile_rows):
        gg.append(small_total[off:off + rows].reshape(1, -1))
        off += tiled
    loss_out = gg[8][0, 0]
    conv_total = gg[9].reshape(3, CONV_W)
    conv_g = lax.dynamic_slice(conv_total, (0, chip * (CONV_W // N_CHIPS)), (3, CONV_W // N_CHIPS))
    g_by_name = dict(g_in=gg[0], g_cq=gg[1], g_ckv=gg[2], g_q=gg[3][:, :QK_DIM], g_k=gg[4][:, :QK_DIM], conv_w=conv_g,
                     g_oa=gg[5], g_oc=gg[6], g_pl=gg[7])
    half_by_name = dict(zip(("w_in", "w_uq", "w_ukv", "w_o", "w_pl", "w_plg"), zip(halves, other_halves)))
    weights = dict(g_in=g_in, w_in=w_in, g_cq=g_cq, w_uq=w_uq, g_ckv=g_ckv, w_ukv=w_ukv, g_q=g_q, g_k=g_k,
                   conv_w=conv_w, g_oa=g_oa, g_oc=g_oc, w_o=w_o, w_pl=w_pl, w_plg=w_plg, g_pl=g_pl)
    ms = dict(g_in=m_g_in, w_in=m_w_in, g_cq=m_g_cq, w_uq=m_w_uq, g_ckv=m_g_ckv, w_ukv=m_w_ukv, g_q=m_g_q, g_k=m_g_k,
              conv_w=m_conv_w, g_oa=m_g_oa, g_oc=m_g_oc, w_o=m_w_o, w_pl=m_w_pl, w_plg=m_w_plg, g_pl=m_g_pl)
    vs = dict(g_in=v_g_in, w_in=v_w_in, g_cq=v_g_cq, w_uq=v_w_uq, g_ckv=v_g_ckv, w_ukv=v_w_ukv, g_q=v_g_q, g_k=v_g_k,
              conv_w=v_conv_w, g_oa=v_g_oa, g_oc=v_g_oc, w_o=v_w_o, w_pl=v_w_pl, w_plg=v_w_plg, g_pl=v_g_pl)
    names = list(weights)
    grads, deltas, new_m, new_v = [], [], [], []
    for n in names:
        w = weights[n]
        w2 = w.reshape(-1, w.shape[-1])
        if n in half_by_name:
            g2, d, nm, nv = _adamw_halves(w2, *half_by_name[n], ms[n].reshape(w2.shape), vs[n].reshape(w2.shape), c,
                                          "adamw_" + n)
        else:
            g2 = g_by_name[n].reshape(w2.shape)
            d, nm, nv = _adamw(w2, g2, ms[n].reshape(w2.shape), vs[n].reshape(w2.shape), "adamw_" + n)
        grads.append(g2.reshape(w.shape))
        deltas.append(d.reshape(w.shape))
        new_m.append(nm.reshape(w.shape))
        new_v.append(nv.reshape(w.shape))
    return (loss_out, gx.reshape(x.shape), *grads, *deltas, *new_m, *new_v)
```

```python
import functools
import math

import jax
import jax.numpy as jnp
from jax import lax
from jax.experimental import pallas as pl
from jax.experimental.pallas import tpu as pltpu

F32 = jnp.float32
BF16 = jnp.bfloat16

D_MODEL = 1024
N_HEADS = 4
NOPE = 128
ROPE = 64
V_DIM = 128
QK_DIM = NOPE + ROPE
HEAD_PAD = 256
Q_LORA = 256
KV_LORA = 128
ATTN_W = 512
CONV_W = 512
PLE = 256
IN_TOTAL = 3008
PROJ_EXT = 3072
ROPE_THETA = 10000.0
EPS = 1e-6
SCALE = 1.0 / math.sqrt(QK_DIM)
LOG2E = math.log2(math.e)
EXP2_SCALE = SCALE * LOG2E
NEG = -1e30
SOFTMAX_ROWS = 32
SUB_TILE = 256

LR, B1, B2, ADAM_EPS, WD, STEP = 0.001, 0.9, 0.999, 1e-08, 0.01, 10

N_CHIPS = 4
LANES = 128
VMEM_LIMIT = 56 * 1024 * 1024
MESH = pl.DeviceIdType.MESH


def _params(**kw):
    return pltpu.CompilerParams(vmem_limit_bytes=VMEM_LIMIT, **kw)


def _inv_rms(x, n):
    return lax.rsqrt(jnp.sum(x * x, axis=-1, keepdims=True) / n + EPS)


def _lane_sum(a):
    folded = a[:, 0:LANES]
    for c0 in range(LANES, a.shape[1], LANES):
        folded = folded + a[:, c0:c0 + LANES]
    head = folded.astype(BF16)
    tail = (folded - head.astype(F32)).astype(BF16)
    return _dot(jnp.concatenate([head, tail], axis=1), jnp.ones((2 * LANES, LANES), BF16))


def _inv_rms_mxu(x):
    return lax.rsqrt(_lane_sum(x * x) / x.shape[1] + EPS)


def _rep(r, width):
    return r if width == LANES else jnp.tile(r, (1, width // LANES))


def _sigmoid(z):
    return 1.0 / (1.0 + jnp.exp(-z))


def _swap_rope_halves(b):
    lane = lax.broadcasted_iota(jnp.int32, b.shape, 1)
    swapped = jnp.where(lane < 32, pltpu.roll(b, 96, 1), pltpu.roll(b, 32, 1))
    return jnp.where(lane < ROPE, swapped, 0.0)


def _dot(a, b):
    return jnp.dot(a, b, preferred_element_type=F32)


def _dot_nt(a, b):
    return lax.dot_general(a, b, (((1,), (1,)), ((), ())), preferred_element_type=F32)


def _dot_tn(a, b):
    return lax.dot_general(a, b, (((0,), (0,)), ((), ())), preferred_element_type=F32)


def _colsum(a):
    return jnp.sum(a, axis=0, keepdims=True)


def _full(shape):
    return pl.BlockSpec(shape, lambda *_: (0,) * len(shape))


def _round_robin(chains, width):
    waiting, active = list(chains), []
    while waiting or active:
        while waiting and len(active) < width:
            active.append(waiting.pop(0))
        for chain in list(active):
            if next(chain, _DONE) is _DONE:
                active.remove(chain)


_DONE = object()


def _rope_tables(pos_ref, invf_ref, sgn_ref):
    ang = pos_ref[...].astype(F32) * invf_ref[...]
    return jnp.cos(ang), jnp.sin(ang) * sgn_ref[...]


def _fwd_proj(x, pos, g_in, w_in, g_cq, w_uq, g_ckv, w_ukv, gq, gk, invf, sgn, late_shards, tm):
    T = x.shape[0]
    nt = T // tm
    n_late = len(late_shards)
    ts = min(SUB_TILE, tm)

    def body(x_ref, pos_ref, g_in_ref, w_in_ref, g_cq_ref, w_uq_ref, g_ckv_ref, w_ukv_ref, gq_ref, gk_ref,
             invf_ref, sgn_ref, *rest):
        late_in, (proj_ref, q_ref, k_ref, v_ref) = rest[:n_late], rest[n_late:n_late + 4]
        late_out, late_scratch = rest[n_late + 4:2 * n_late + 4], rest[2 * n_late + 4:]
        i = pl.program_id(0)
        if n_late:
            start, forward, drain = _gather_steps([s.shape for s in late_shards], late_in, late_out,
                                                  late_scratch[:n_late], *late_scratch[n_late:])
            pl.when(i == 0)(start)
            pl.when(i == nt // 2)(forward)

        for r0 in range(0, tm, ts):
            rows = slice(r0, r0 + ts)
            xv = x_ref[rows, :]
            h = (xv * _rep(_inv_rms_mxu(xv), D_MODEL) * g_in_ref[...]).astype(BF16)
            lat = _dot(h, w_in_ref[:, 0:512])
            proj_ref[rows, 0:512] = lat
            c_q = lat[:, 0:Q_LORA]
            cqn = (c_q * _rep(_inv_rms_mxu(c_q), Q_LORA) * g_cq_ref[...]).astype(BF16)
            c_kv = lat[:, Q_LORA:Q_LORA + KV_LORA]
            ckvn = (c_kv * _inv_rms_mxu(c_kv) * g_ckv_ref[...]).astype(BF16)
            kpe = lat[:, 384:512]
            kpe_sq = kpe * kpe
            cos_b, sin_b = _rope_tables(pos_ref.at[rows, :], invf_ref, sgn_ref)
            gq_a, gq_b = gq_ref[:, 0:NOPE], gq_ref[:, NOPE:HEAD_PAD]
            gk_a, gk_b = gk_ref[:, 0:NOPE], gk_ref[:, NOPE:HEAD_PAD]

            def projections(rows=rows, h=h):
                for c0 in range(512, PROJ_EXT, 512):
                    proj_ref[rows, c0:c0 + 512] = _dot(h, w_in_ref[:, c0:c0 + 512])
                    yield

            def queries(hd, rows=rows, cqn=cqn, cos_b=cos_b, sin_b=sin_b, gq_a=gq_a, gq_b=gq_b):
                qh = _dot(cqn, w_uq_ref[:, hd * HEAD_PAD:(hd + 1) * HEAD_PAD])
                yield
                a, b = qh[:, 0:NOPE], qh[:, NOPE:HEAD_PAD]
                r = lax.rsqrt(_lane_sum(a * a + b * b) / QK_DIM + EPS)
                yield
                bn = b * r * gq_b
                q_ref[hd, rows, 0:NOPE] = (a * r * gq_a).astype(BF16)
                q_ref[hd, rows, NOPE:HEAD_PAD] = (bn * cos_b + _swap_rope_halves(bn) * sin_b).astype(BF16)
                yield

            def keys(hd, rows=rows, ckvn=ckvn, kpe=kpe, kpe_sq=kpe_sq, cos_b=cos_b, sin_b=sin_b, gk_a=gk_a, gk_b=gk_b):
                kvh = _dot(ckvn, w_ukv_ref[:, hd * HEAD_PAD:(hd + 1) * HEAD_PAD])
                yield
                ka = kvh[:, 0:NOPE]
                rk = lax.rsqrt(_lane_sum(ka * ka + kpe_sq) / QK_DIM + EPS)
                yield
                kbn = kpe * rk * gk_b
                k_ref[hd, rows, 0:NOPE] = (ka * rk * gk_a).astype(BF16)
                k_ref[hd, rows, NOPE:HEAD_PAD] = (kbn * cos_b + _swap_rope_halves(kbn) * sin_b).astype(BF16)
                v_ref[hd, rows, 0:V_DIM] = kvh[:, NOPE:HEAD_PAD].astype(BF16)
                v_ref[hd, rows, V_DIM:2 * V_DIM] = jnp.ones((ts, V_DIM), BF16)
                yield

            chains = [projections()]
            for hd in range(N_HEADS):
                chains += [queries(hd), keys(hd)]
            _round_robin(chains, 4)

        if n_late:
            pl.when(i == nt - 1)(drain)

    row = lambda i: (i, 0)
    head_rows = lambda i: (0, i, 0)
    outs = pl.pallas_call(
        body, name="fwd_proj", grid=(nt,),
        in_specs=[pl.BlockSpec((tm, D_MODEL), row), pl.BlockSpec((tm, 1), row), _full((1, D_MODEL)),
                  _full((D_MODEL, PROJ_EXT)), _full((1, Q_LORA)), _full((Q_LORA, N_HEADS * HEAD_PAD)),
                  _full((1, KV_LORA)), _full((KV_LORA, N_HEADS * HEAD_PAD)), _full((1, HEAD_PAD)), _full((1, HEAD_PAD)),
                  _full((1, LANES)), _full((1, LANES))] + [_full(s.shape) for s in late_shards],
        out_specs=[pl.BlockSpec((tm, PROJ_EXT), row), pl.BlockSpec((N_HEADS, tm, HEAD_PAD), head_rows),
                   pl.BlockSpec((N_HEADS, tm, HEAD_PAD), head_rows), pl.BlockSpec((N_HEADS, tm, 2 * V_DIM), head_rows)]
                  + [_ANY] * n_late,
        out_shape=[jax.ShapeDtypeStruct((T, PROJ_EXT), F32), jax.ShapeDtypeStruct((N_HEADS, T, HEAD_PAD), BF16),
                   jax.ShapeDtypeStruct((N_HEADS, T, HEAD_PAD), BF16), jax.ShapeDtypeStruct((N_HEADS, T, 2 * V_DIM), BF16)]
                  + _gathered_shapes(late_shards),
        scratch_shapes=_gather_scratch(late_shards) if n_late else [],
        compiler_params=_params(dimension_semantics=("arbitrary",)),
    )(x, pos, g_in, w_in, g_cq, w_uq, g_ckv, w_ukv, gq, gk, invf, sgn, *late_shards)
    return outs[:4], outs[4:]


def _chunk_pipeline(n_loop, lag, matmuls, pointwise, accumulate, last):
    slots = lag + 1

    def iteration(t, slot):
        matmuls(jnp.minimum(t + lag, n_loop), (slot + lag) % slots)
        accumulate(jnp.maximum(t - lag, 0), (slot + 1) % slots)
        pointwise(t, slot, False)

    def finish(slot):
        for back in range(lag, 0, -1):
            accumulate(jnp.maximum(n_loop - back, 0), (slot - back) % slots)
        pointwise(n_loop, slot, True)
        accumulate(n_loop, slot)
        last()

    for u in range(lag):
        matmuls(jnp.minimum(u, n_loop), u)

    def unrolled(tt, carry):
        for slot in range(slots):
            iteration(slots * tt + slot, slot)
        return carry

    lax.fori_loop(0, n_loop // slots, unrolled, 0)
    rest = lax.rem(n_loop, slots)
    t0 = n_loop - rest

    for r in range(slots):
        @pl.when(rest == r)
        def _():
            for slot in range(r):
                iteration(t0 + slot, slot)
            finish(r)


def _attn_fwd(q, k, v, tq):
    T = q.shape[1]
    tk = tq
    rc = min(SOFTMAX_ROWS, tq)

    def body(q_ref, k_ref, v_ref, o_ref, lse_ref, s0, s1, s2, p0, p1, p2, a0, a1, a2, m_ref, acc_ref):
        qi = pl.program_id(1)
        s_buf, p_buf, a_buf = (s0, s1, s2), (p0, p1, p2), (a0, a1, a2)

        def scores(t, slot):
            ks = pl.multiple_of(t * tk, tk)
            s_buf[slot][...] = _dot_nt(q_ref[0], k_ref[0, pl.ds(ks, tk), :])

        def values(t, slot):
            ks = pl.multiple_of(t * tk, tk)
            acc_ref[...] = acc_ref[...] * a_buf[slot][...] + _dot(p_buf[slot][...], v_ref[0, pl.ds(ks, tk), :])

        def softmax(t, slot, masked):
            s_all = s_buf[slot][...]
            if masked:
                row = lax.broadcasted_iota(jnp.int32, (tq, tk), 0)
                col = lax.broadcasted_iota(jnp.int32, (tq, tk), 1)
                s_all = jnp.where(col <= row, s_all, NEG)
                s_buf[slot][...] = s_all
            m_old = m_ref[...]
            m_new = jnp.maximum(m_old, jnp.max(s_all, axis=1, keepdims=True))
            a_buf[slot][...] = jnp.exp2((m_old - m_new) * EXP2_SCALE)
            m_ref[...] = m_new
            for r0 in range(0, tq, rc):
                s = s_buf[slot][r0:r0 + rc, :]
                p_buf[slot][r0:r0 + rc, :] = jnp.exp2((s - m_new[r0:r0 + rc, :]) * EXP2_SCALE).astype(BF16)

        def last():
            l = acc_ref[:, V_DIM:2 * V_DIM]
            o_ref[...] = acc_ref[:, 0:V_DIM] / l
            lse_ref[0] = (m_ref[...] * SCALE + jnp.log(l)).T[0:1, :]

        m_ref[...] = jnp.full_like(m_ref, NEG)
        acc_ref[...] = jnp.zeros_like(acc_ref)
        for p_late, a_late in ((p1, a1), (p2, a2)):
            p_late[...] = jnp.zeros_like(p_late)
            a_late[...] = jnp.ones_like(a_late)
        _chunk_pipeline(qi, 2, scores, softmax, values, last)

    return pl.pallas_call(
        body, name="attn_fwd", grid=(N_HEADS, T // tq),
        in_specs=[pl.BlockSpec((1, tq, HEAD_PAD), lambda h, i: (h, i, 0)),
                  pl.BlockSpec((1, T, HEAD_PAD), lambda h, i: (h, 0, 0)),
                  pl.BlockSpec((1, T, 2 * V_DIM), lambda h, i: (h, 0, 0))],
        out_specs=[pl.BlockSpec((tq, V_DIM), lambda h, i: (i, h)),
                   pl.BlockSpec((1, 1, tq), lambda h, i: (h, 0, i))],
        out_shape=[jax.ShapeDtypeStruct((T, ATTN_W), F32), jax.ShapeDtypeStruct((N_HEADS, 1, T), F32)],
        scratch_shapes=[pltpu.VMEM((tq, tk), F32)] * 3 + [pltpu.VMEM((tq, tk), BF16)] * 3
                       + [pltpu.VMEM((tq, 1), F32)] * 4 + [pltpu.VMEM((tq, 2 * V_DIM), F32)],
        compiler_params=_params(dimension_semantics=("arbitrary", "arbitrary")),
    )(q, k, v)


def _tail(x, o, proj, p, tgt, g_oa, g_oc, g_pl, conv_w, w_o, w_pl, w_plg, tm):
    T = x.shape[0]
    nt = T // tm
    ts = min(SUB_TILE, tm)

    def body(x_ref, o_ref, za_ref, cb_ref, cc_ref, cx_ref, zc_ref, cch_ref, cxh_ref, p_ref, tgt_ref,
             g_oa_ref, g_oc_ref, g_pl_ref, cw_ref, w_o_ref, w_pl_ref, w_plg_ref,
             dx1_ref, do_ref, delta_ref, dtail_ref, du_ref,
             dw_o_ref, dw_pl_ref, dw_plg_ref, dg_oa_ref, dg_oc_ref, dg_pl_ref, dcw_ref, loss_ref,
             v_buf, v1_buf, v2_buf):
        i = pl.program_id(0)

        @pl.when(i == 0)
        def _():
            for r in (dw_o_ref, dw_pl_ref, dw_plg_ref, dg_oa_ref, dg_oc_ref, dg_pl_ref, dcw_ref, loss_ref):
                r[...] = jnp.zeros_like(r)

        g_oa, g_oc, g_pl = g_oa_ref[...], g_oc_ref[...], g_pl_ref[...]
        w0, w1, w2 = cw_ref[0:1, :], cw_ref[1:2, :], cw_ref[2:3, :]

        v = cc_ref[...] * cx_ref[...]
        not_first = jnp.where(i > 0, 1.0, 0.0)
        hv6 = cch_ref[6:7, :] * cxh_ref[6:7, :] * not_first
        hv7 = cch_ref[7:8, :] * cxh_ref[7:8, :] * not_first
        row = lax.broadcasted_iota(jnp.int32, v.shape, 0)
        v_buf[...] = v
        v1_buf[...] = jnp.where(row == 0, hv7, pltpu.roll(v, 1, 0))
        v2_buf[...] = jnp.where(row == 0, hv6, jnp.where(row == 1, hv7, pltpu.roll(v, 2, 0)))

        def sub_tile(rows):
            xv, ov, za, cb, zc = x_ref[rows, :], o_ref[rows, :], za_ref[rows, :], cb_ref[rows, :], zc_ref[rows, :]
            pb = p_ref[rows, :].astype(BF16)
            pp = _dot(pb, w_pl_ref[...])
            yield
            sa = _sigmoid(za)
            silu_a = za * sa
            ga = ov * silu_a
            ra = _inv_rms(ga, ATTN_W)
            xa = ga * ra
            ya = xa * g_oa
            yield
            vt, v1, v2 = v_buf[rows, :], v1_buf[rows, :], v2_buf[rows, :]
            u = w0 * v2 + w1 * v1 + w2 * vt
            sc = _sigmoid(zc)
            silu_c = zc * sc
            gc = cb * u * silu_c
            rc = _inv_rms(gc, CONV_W)
            xc = gc * rc
            yc = xc * g_oc
            yield
            ycat = jnp.concatenate([ya, yc], axis=-1).astype(BF16)
            x1 = xv + _dot(ycat, w_o_ref[...])
            yield
            r1 = _inv_rms(x1, D_MODEL)
            xh1 = x1 * r1
            n1 = (xh1 * g_pl).astype(BF16)
            yield
            gate = _sigmoid(_dot(n1, w_plg_ref[...]))
            yield
            err = x1 + gate * pp - tgt_ref[rows, :]
            loss_ref[...] += 0.5 * jnp.sum(err * err) / D_MODEL
            dy = err / D_MODEL
            dpp = (dy * gate).astype(BF16)
            da = (dy * pp * gate * (1.0 - gate)).astype(BF16)
            yield
            dn1 = _dot_nt(da, w_plg_ref[...])
            yield
            dw_pl_ref[...] += _dot_tn(pb, dpp)
            yield
            dw_plg_ref[...] += _dot_tn(n1, da)
            yield
            dg_pl_ref[...] += _colsum(dn1 * xh1)
            dxh = dn1 * g_pl
            dx1 = dy + r1 * (dxh - xh1 * (jnp.sum(dxh * xh1, axis=-1, keepdims=True) / D_MODEL))
            dx1_ref[rows, :] = dx1
            dx1b = dx1.astype(BF16)
            yield
            dycat = _dot_nt(dx1b, w_o_ref[...])
            dya, dyc = dycat[:, 0:ATTN_W], dycat[:, ATTN_W:D_MODEL]
            yield
            dw_o_ref[0:ATTN_W, :] += _dot_tn(ycat[:, 0:ATTN_W], dx1b)
            yield
            dg_oa_ref[...] += _colsum(dya * xa)
            dxa = dya * g_oa
            dga = ra * (dxa - xa * (jnp.sum(dxa * xa, axis=-1, keepdims=True) / ATTN_W))
            do = (dga * silu_a).astype(BF16)
            do_ref[rows, :] = do
            yield
            dof = do.astype(F32) * ov
            for hd in range(N_HEADS):
                delta_ref[hd, :, rows] = _lane_sum(dof[:, hd * V_DIM:(hd + 1) * V_DIM]).T[0:1, :]
            dtail_ref[rows, 0:512] = (dga * ov * (sa * (1.0 + za * (1.0 - sa)))).astype(BF16)
            yield
            dw_o_ref[ATTN_W:D_MODEL, :] += _dot_tn(ycat[:, ATTN_W:D_MODEL], dx1b)
            yield
            dg_oc_ref[...] += _colsum(dyc * xc)
            dxc = dyc * g_oc
            dgc = rc * (dxc - xc * (jnp.sum(dxc * xc, axis=-1, keepdims=True) / CONV_W))
            dtail_ref[rows, 512:1024] = (dgc * u * silu_c).astype(BF16)
            yield
            du = dgc * cb * silu_c
            du_ref[rows, :] = du
            dtail_ref[rows, 1024:1536] = (dgc * cb * u * (sc * (1.0 + zc * (1.0 - sc)))).astype(BF16)
            yield
            dcw_ref[0:1, :] += _colsum(du * v2)
            dcw_ref[1:2, :] += _colsum(du * v1)
            dcw_ref[2:3, :] += _colsum(du * vt)

        _round_robin([sub_tile(slice(r0, r0 + ts)) for r0 in range(0, tm, ts)], tm // ts)

    row = lambda i: (i, 0)
    col = lambda c: (lambda i: (i, c))
    halo = lambda c: (lambda i: (jnp.maximum(i * (tm // 8) - 1, 0), c))
    in_specs = [pl.BlockSpec((tm, D_MODEL), row), pl.BlockSpec((tm, ATTN_W), row)]
    in_specs += [pl.BlockSpec((tm, 512), col(c)) for c in (1, 2, 3, 4, 5)]
    in_specs += [pl.BlockSpec((8, 512), halo(3)), pl.BlockSpec((8, 512), halo(4))]
    in_specs += [pl.BlockSpec((tm, PLE), row), pl.BlockSpec((tm, D_MODEL), row),
                 _full((1, ATTN_W)), _full((1, CONV_W)), _full((1, D_MODEL)), _full((3, CONV_W)),
                 _full((D_MODEL, D_MODEL)), _full((PLE, D_MODEL)), _full((D_MODEL, D_MODEL))]
    out_specs = [pl.BlockSpec((tm, D_MODEL), row), pl.BlockSpec((tm, ATTN_W), row),
                 pl.BlockSpec((N_HEADS, 1, tm), lambda i: (0, 0, i)), pl.BlockSpec((tm, 1536), row),
                 pl.BlockSpec((tm, CONV_W), row),
                 _full((D_MODEL, D_MODEL)), _full((PLE, D_MODEL)), _full((D_MODEL, D_MODEL)),
                 _full((1, ATTN_W)), _full((1, CONV_W)), _full((1, D_MODEL)), _full((3, CONV_W)), _full((1, LANES))]
    out_shape = [jax.ShapeDtypeStruct((T, D_MODEL), F32), jax.ShapeDtypeStruct((T, ATTN_W), BF16),
                 jax.ShapeDtypeStruct((N_HEADS, 1, T), F32), jax.ShapeDtypeStruct((T, 1536), BF16),
                 jax.ShapeDtypeStruct((T, CONV_W), F32),
                 jax.ShapeDtypeStruct((D_MODEL, D_MODEL), F32), jax.ShapeDtypeStruct((PLE, D_MODEL), F32),
                 jax.ShapeDtypeStruct((D_MODEL, D_MODEL), F32),
                 jax.ShapeDtypeStruct((1, ATTN_W), F32), jax.ShapeDtypeStruct((1, CONV_W), F32),
                 jax.ShapeDtypeStruct((1, D_MODEL), F32), jax.ShapeDtypeStruct((3, CONV_W), F32),
                 jax.ShapeDtypeStruct((1, LANES), F32)]
    return pl.pallas_call(
        body, name="tail", grid=(nt,), in_specs=in_specs, out_specs=out_specs, out_shape=out_shape,
        scratch_shapes=[pltpu.VMEM((tm, CONV_W), F32)] * 3,
        compiler_params=_params(dimension_semantics=("arbitrary",)),
    )(x, o, proj, proj, proj, proj, proj, proj, proj, p, tgt, g_oa, g_oc, g_pl, conv_w, w_o, w_pl, w_plg)


def _attn_bwd(q, k, v, do, lse_row, delta_row, tk):
    T = q.shape[1]
    tq = tk
    nq = T // tq
    rc = min(SOFTMAX_ROWS, tk)

    def body(q_ref, k_ref, v_ref, do_ref, lse_ref, dl_ref, dq_ref, dk_ref, dv_ref,
             s0, s1, d0, d1, p0, p1, g0, g1, dk_acc, dv_acc):
        kj = pl.program_id(1)
        s_buf, dp_buf, p_buf, g_buf = (s0, s1), (d0, d1), (p0, p1), (g0, g1)

        @pl.when(kj == 0)
        def _():
            dq_ref[...] = jnp.zeros_like(dq_ref)

        def q_start(t):
            return pl.multiple_of((nq - 1 - t) * tq, tq)

        def matmuls(t, slot):
            qs = q_start(t)
            s_buf[slot][...] = _dot_nt(k_ref[0], q_ref[0, pl.ds(qs, tq), :])
            dp_buf[slot][...] = _dot_nt(v_ref[0], do_ref[pl.ds(qs, tq), :])

        def pointwise(t, slot, masked):
            qs = q_start(t)
            lse2 = lse_ref[0, :, pl.ds(qs, tq)] * LOG2E
            dl = dl_ref[0, :, pl.ds(qs, tq)]
            for r0 in range(0, tk, rc):
                st = s_buf[slot][r0:r0 + rc, :]
                if masked:
                    row = lax.broadcasted_iota(jnp.int32, (rc, tq), 0)
                    col = lax.broadcasted_iota(jnp.int32, (rc, tq), 1)
                    st = jnp.where(row + r0 <= col, st, NEG)
                pt = jnp.exp2(st * EXP2_SCALE - lse2)
                p_buf[slot][r0:r0 + rc, :] = pt.astype(BF16)
                g_buf[slot][r0:r0 + rc, :] = (pt * (dp_buf[slot][r0:r0 + rc, :] - dl) * SCALE).astype(BF16)

        def accumulate(t, slot):
            qs = q_start(t)
            dv_acc[...] += _dot(p_buf[slot][...], do_ref[pl.ds(qs, tq), :])
            dk_acc[...] += _dot(g_buf[slot][...], q_ref[0, pl.ds(qs, tq), :])
            dq_ref[0, pl.ds(qs, tq), :] += _dot_tn(g_buf[slot][...], k_ref[0])

        def last():
            dk_ref[0] = dk_acc[...]
            dv_ref[0] = dv_acc[...]

        dk_acc[...] = jnp.zeros_like(dk_acc)
        dv_acc[...] = jnp.zeros_like(dv_acc)
        for late in (p1, g1):
            late[...] = jnp.zeros_like(late)
        _chunk_pipeline(nq - 1 - kj, 1, matmuls, pointwise, accumulate, last)

    return pl.pallas_call(
        body, name="attn_bwd", grid=(N_HEADS, T // tk),
        in_specs=[pl.BlockSpec((1, T, HEAD_PAD), lambda h, j: (h, 0, 0)),
                  pl.BlockSpec((1, tk, HEAD_PAD), lambda h, j: (h, j, 0)),
                  pl.BlockSpec((1, tk, V_DIM), lambda h, j: (h, j, 0)),
                  pl.BlockSpec((T, V_DIM), lambda h, j: (0, h)),
                  pl.BlockSpec((1, 1, T), lambda h, j: (h, 0, 0)),
                  pl.BlockSpec((1, 1, T), lambda h, j: (h, 0, 0))],
        out_specs=[pl.BlockSpec((1, T, HEAD_PAD), lambda h, j: (h, 0, 0)),
                   pl.BlockSpec((1, tk, HEAD_PAD), lambda h, j: (h, j, 0)),
                   pl.BlockSpec((1, tk, V_DIM), lambda h, j: (h, j, 0))],
        out_shape=[jax.ShapeDtypeStruct((N_HEADS, T, HEAD_PAD), F32), jax.ShapeDtypeStruct((N_HEADS, T, HEAD_PAD), F32),
                   jax.ShapeDtypeStruct((N_HEADS, T, V_DIM), F32)],
        scratch_shapes=[pltpu.VMEM((tk, tq), F32)] * 4 + [pltpu.VMEM((tk, tq), BF16)] * 4
                       + [pltpu.VMEM((tk, HEAD_PAD), F32), pltpu.VMEM((tk, V_DIM), F32)],
        compiler_params=_params(dimension_semantics=("arbitrary", "arbitrary")),
    )(q, k, v, do, lse_row, delta_row)


def _bwd_proj(x, dx1, pos, proj, dq, dk, dv, dtail, du, g_in, w_in, g_cq, w_uq, g_ckv, w_ukv, gq, gk, conv_w,
              invf, sgn, tm):
    T = x.shape[0]
    nt = T // tm

    ts = min(SUB_TILE, tm)

    def body(x_ref, dx1_ref, pos_ref, lat_ref, cc_ref, cx_ref, dq_ref, dk_ref, dv_ref, dtail_ref, du_ref, dun_ref, *rest):
        consts, (gx_ref, h_ref, dproj_ref), sums = rest[:11], rest[11:14], rest[14:]
        cw_ref = consts[8]
        i = pl.program_id(0)

        @pl.when(i == 0)
        def _():
            for r in sums:
                r[...] = jnp.zeros_like(r)

        du_v = du_ref[...]
        not_last = jnp.where(i < nt - 1, 1.0, 0.0)
        nx0 = dun_ref[0:1, :] * not_last
        nx1 = dun_ref[1:2, :] * not_last
        row = lax.broadcasted_iota(jnp.int32, du_v.shape, 0)
        du1 = jnp.where(row == tm - 1, nx0, pltpu.roll(du_v, tm - 1, 0))
        du2 = jnp.where(row == tm - 2, nx0, jnp.where(row == tm - 1, nx1, pltpu.roll(du_v, tm - 2, 0)))
        dvc = cw_ref[2:3, :] * du_v + cw_ref[1:2, :] * du1 + cw_ref[0:1, :] * du2
        dproj_ref[:, 1536:2048] = (dvc * cx_ref[...]).astype(BF16)
        dproj_ref[:, 2048:2560] = (dvc * cc_ref[...]).astype(BF16)

        for r0 in range(0, tm, ts):
            rows = slice(r0, r0 + ts)
            work(x_ref.at[rows, :], dx1_ref.at[rows, :], pos_ref.at[rows, :], lat_ref.at[rows, :],
                 dq_ref.at[:, rows, :], dk_ref.at[:, rows, :], dv_ref.at[:, rows, :], dtail_ref.at[rows, :], *consts,
                 gx_ref.at[rows, :], h_ref.at[:, rows], dproj_ref.at[rows, :], *sums)

    def work(x_ref, dx1_ref, pos_ref, lat_ref, dq_ref, dk_ref, dv_ref, dtail_ref,
             g_in_ref, w_in_ref, g_cq_ref, w_uq_ref, g_ckv_ref, w_ukv_ref, gq_ref, gk_ref, cw_ref, invf_ref, sgn_ref,
             gx_ref, h_ref, dproj_ref, dw_uq_ref, dw_ukv_ref, dg_in_ref, dg_cq_ref, dg_ckv_ref, dgq_ref, dgk_ref):
        xv = x_ref[...]
        r0 = _rep(_inv_rms_mxu(xv), D_MODEL)
        xh0 = xv * r0
        g_in = g_in_ref[...]
        h_ref[...] = (xh0 * g_in).astype(BF16).T

        c_q = lat_ref[:, 0:Q_LORA]
        rq = _rep(_inv_rms_mxu(c_q), Q_LORA)
        xq = c_q * rq
        g_cq = g_cq_ref[...]
        cqn = (xq * g_cq).astype(BF16)
        c_kv = lat_ref[:, Q_LORA:Q_LORA + KV_LORA]
        rkv = _inv_rms_mxu(c_kv)
        xkv = c_kv * rkv
        g_ckv = g_ckv_ref[...]
        ckvn = (xkv * g_ckv).astype(BF16)
        kpe = lat_ref[:, 384:512]
        kpe_sq = kpe * kpe
        cos_b, sin_b = _rope_tables(pos_ref, invf_ref, sgn_ref)
        gq_a, gq_b = gq_ref[:, 0:NOPE], gq_ref[:, NOPE:HEAD_PAD]
        gk_a, gk_b = gk_ref[:, 0:NOPE], gk_ref[:, NOPE:HEAD_PAD]

        dproj_ref[:, 512:1536] = dtail_ref[:, 0:1024]
        dproj_ref[:, 2560:3072] = dtail_ref[:, 1024:1536]

        def dh_part(c0):
            return _dot_nt(dproj_ref[:, c0:c0 + 512], w_in_ref[:, c0:c0 + 512])

        later_chunks = ((512,), (1024,), (1536, 2048), (2560,))
        dh = jnp.zeros((ts, D_MODEL), F32)
        acc = dict(dh=dh, dkpe=jnp.zeros((ts, LANES), F32), dcqn=jnp.zeros((ts, Q_LORA), F32),
                   dckvn=jnp.zeros((ts, KV_LORA), F32))

        def dh_chunks():
            for chunks in later_chunks:
                for chunk in chunks:
                    acc["dh"] = acc["dh"] + dh_part(chunk)
                    yield

        def queries(hd):
            c0 = hd * HEAD_PAD
            qh = _dot(cqn, w_uq_ref[:, c0:c0 + HEAD_PAD])
            yield
            a, b = qh[:, 0:NOPE], qh[:, NOPE:HEAD_PAD]
            r = lax.rsqrt(_lane_sum(a * a + b * b) / QK_DIM + EPS)
            yield
            xa, xb = a * r, b * r
            dan = dq_ref[hd, :, 0:NOPE]
            dbr = dq_ref[hd, :, NOPE:HEAD_PAD]
            dbn = dbr * cos_b + _swap_rope_halves(dbr * sin_b)
            yield
            dgq_ref[:, 0:NOPE] += _colsum(dan * xa)
            dgq_ref[:, NOPE:HEAD_PAD] += _colsum(dbn * xb)
            dxa, dxb = dan * gq_a, dbn * gq_b
            cq = _lane_sum(dxa * xa + dxb * xb) / QK_DIM
            yield
            dqh = jnp.concatenate([r * (dxa - xa * cq), r * (dxb - xb * cq)], axis=-1).astype(BF16)
            yield
            dw_uq_ref[:, c0:c0 + HEAD_PAD] += _dot_tn(cqn, dqh)
            yield
            acc["dcqn"] = acc["dcqn"] + _dot_nt(dqh, w_uq_ref[:, c0:c0 + HEAD_PAD])
            yield

        def keys(hd):
            c0 = hd * HEAD_PAD
            kvh = _dot(ckvn, w_ukv_ref[:, c0:c0 + HEAD_PAD])
            yield
            ka = kvh[:, 0:NOPE]
            rk = lax.rsqrt(_lane_sum(ka * ka + kpe_sq) / QK_DIM + EPS)
            yield
            xka, xkb = ka * rk, kpe * rk
            dkan = dk_ref[hd, :, 0:NOPE]
            dkbr = dk_ref[hd, :, NOPE:HEAD_PAD]
            dkbn = dkbr * cos_b + _swap_rope_halves(dkbr * sin_b)
            yield
            dgk_ref[:, 0:NOPE] += _colsum(dkan * xka)
            dgk_ref[:, NOPE:HEAD_PAD] += _colsum(dkbn * xkb)
            dxka, dxkb = dkan * gk_a, dkbn * gk_b
            ck = _lane_sum(dxka * xka + dxkb * xkb) / QK_DIM
            yield
            acc["dkpe"] = acc["dkpe"] + rk * (dxkb - xkb * ck)
            dkvh = jnp.concatenate([rk * (dxka - xka * ck), dv_ref[hd]], axis=-1).astype(BF16)
            yield
            dw_ukv_ref[:, c0:c0 + HEAD_PAD] += _dot_tn(ckvn, dkvh)
            yield
            acc["dckvn"] = acc["dckvn"] + _dot_nt(dkvh, w_ukv_ref[:, c0:c0 + HEAD_PAD])
            yield

        chains = [dh_chunks()]
        for hd in range(N_HEADS):
            chains += [queries(hd), keys(hd)]
        _round_robin(chains, 5)
        dh, dkpe, dcqn, dckvn = acc["dh"], acc["dkpe"], acc["dcqn"], acc["dckvn"]

        dg_cq_ref[...] += _colsum(dcqn * xq)
        dxq = dcqn * g_cq
        dproj_ref[:, 0:Q_LORA] = (rq * (dxq - xq * _rep(_lane_sum(dxq * xq) / Q_LORA, Q_LORA))).astype(BF16)
        dg_ckv_ref[...] += _colsum(dckvn * xkv)
        dxkv = dckvn * g_ckv
        dproj_ref[:, 256:384] = (rkv * (dxkv - xkv * (_lane_sum(dxkv * xkv) / KV_LORA))).astype(BF16)
        dproj_ref[:, 384:512] = dkpe.astype(BF16)
        dh = dh + dh_part(0)
        dg_in_ref[...] += _colsum(dh * xh0)
        dxh = dh * g_in
        gx_ref[...] = dx1_ref[...] + r0 * (dxh - xh0 * _rep(_lane_sum(dxh * xh0) / D_MODEL, D_MODEL))

    row = lambda i: (i, 0)
    col = lambda c: (lambda i: (i, c))
    head_rows = lambda i: (0, i, 0)
    nxt = lambda i: (jnp.minimum((i + 1) * (tm // 8), T // 8 - 1), 0)
    in_specs = [pl.BlockSpec((tm, D_MODEL), row), pl.BlockSpec((tm, D_MODEL), row), pl.BlockSpec((tm, 1), row),
                pl.BlockSpec((tm, 512), col(0)), pl.BlockSpec((tm, 512), col(3)), pl.BlockSpec((tm, 512), col(4)),
                pl.BlockSpec((N_HEADS, tm, HEAD_PAD), head_rows), pl.BlockSpec((N_HEADS, tm, HEAD_PAD), head_rows),
                pl.BlockSpec((N_HEADS, tm, V_DIM), head_rows), pl.BlockSpec((tm, 1536), row),
                pl.BlockSpec((tm, CONV_W), row), pl.BlockSpec((8, CONV_W), nxt),
                _full((1, D_MODEL)), _full((D_MODEL, PROJ_EXT)), _full((1, Q_LORA)), _full((Q_LORA, N_HEADS * HEAD_PAD)),
                _full((1, KV_LORA)), _full((KV_LORA, N_HEADS * HEAD_PAD)), _full((1, HEAD_PAD)), _full((1, HEAD_PAD)),
                _full((3, CONV_W)), _full((1, LANES)), _full((1, LANES))]
    out_specs = [pl.BlockSpec((tm, D_MODEL), row), pl.BlockSpec((D_MODEL, tm), lambda i: (0, i)),
                 pl.BlockSpec((tm, PROJ_EXT), row),
                 _full((Q_LORA, N_HEADS * HEAD_PAD)), _full((KV_LORA, N_HEADS * HEAD_PAD)),
                 _full((1, D_MODEL)), _full((1, Q_LORA)), _full((1, KV_LORA)), _full((1, HEAD_PAD)), _full((1, HEAD_PAD))]
    out_shape = [jax.ShapeDtypeStruct((T, D_MODEL), F32), jax.ShapeDtypeStruct((D_MODEL, T), BF16),
                 jax.ShapeDtypeStruct((T, PROJ_EXT), BF16),
                 jax.ShapeDtypeStruct((Q_LORA, N_HEADS * HEAD_PAD), F32), jax.ShapeDtypeStruct((KV_LORA, N_HEADS * HEAD_PAD), F32),
                 jax.ShapeDtypeStruct((1, D_MODEL), F32), jax.ShapeDtypeStruct((1, Q_LORA), F32),
                 jax.ShapeDtypeStruct((1, KV_LORA), F32), jax.ShapeDtypeStruct((1, HEAD_PAD), F32),
                 jax.ShapeDtypeStruct((1, HEAD_PAD), F32)]
    return pl.pallas_call(
        body, name="bwd_proj", grid=(nt,), in_specs=in_specs, out_specs=out_specs, out_shape=out_shape,
        compiler_params=_params(dimension_semantics=("arbitrary",)),
    )(x, dx1, pos, proj, proj, proj, dq, dk, dv, dtail, du, du, g_in, w_in, g_cq, w_uq, g_ckv, w_ukv, gq, gk, conv_w,
      invf, sgn)


def _matmul_acc(a, b, tt, tn, parts):
    M, T = a.shape
    N = b.shape[1]
    n = len(parts)
    grid = (N // tn, T // tt)

    def body(a_ref, b_ref, *rest):
        part_refs, o_ref, out_refs, sems = rest[:n], rest[n], rest[n + 1:2 * n + 1], rest[2 * n + 1:]
        j, t = pl.program_id(0), pl.program_id(1)
        if n:
            start, drain = _scatter_steps(part_refs, out_refs, *sems)
            pl.when(jnp.logical_and(j == 0, t == 0))(start)

        @pl.when(t == 0)
        def _():
            o_ref[...] = jnp.zeros_like(o_ref)

        o_ref[...] += _dot(a_ref[...], b_ref[...])
        if n:
            pl.when(jnp.logical_and(j == grid[0] - 1, t == grid[1] - 1))(drain)

    sems = [pltpu.SemaphoreType.DMA((3 * n,)), pltpu.SemaphoreType.DMA((3 * n,)), pltpu.SemaphoreType.DMA((n,))]
    outs = pl.pallas_call(
        body, name="dw_in", grid=grid,
        in_specs=[pl.BlockSpec((M, tt), lambda j, t: (0, t)), pl.BlockSpec((tt, tn), lambda j, t: (t, j))] + [_ANY] * n,
        out_specs=[pl.BlockSpec((M, tn), lambda j, t: (0, j))] + [_ANY] * n,
        out_shape=[jax.ShapeDtypeStruct((M, N), F32)] + _scattered_shapes(parts),
        scratch_shapes=sems if n else [],
        compiler_params=_params(dimension_semantics=("arbitrary", "arbitrary")),
    )(a, b, *parts)
    return outs[0], outs[1:]


def _add_chips(parts, small_parts):
    arrays = list(parts) + [small_parts]

    def body(*refs):
        ins, outs = refs[:len(arrays)], refs[len(arrays):]
        for a_ref, o_ref in zip(ins, outs):
            part = lambda k: a_ref[k].astype(F32)
            o_ref[...] = ((part(0) + part(1)) + part(2)) + part(3)

    in_specs, out_specs, out_shape = [], [], []
    for a in arrays:
        _, rows, cols = a.shape
        in_specs.append(pl.BlockSpec((N_CHIPS, rows // 2, cols), lambda i: (0, i, 0)))
        out_specs.append(pl.BlockSpec((rows // 2, cols), lambda i: (i, 0)))
        out_shape.append(jax.ShapeDtypeStruct((rows, cols), F32))
    outs = pl.pallas_call(body, name="add_chips", grid=(2,), in_specs=in_specs, out_specs=out_specs,
                          out_shape=out_shape, compiler_params=_params(dimension_semantics=("arbitrary",)))(*arrays)
    return outs[:-1], outs[-1]


def _adamw_small(ws, gs, ms, vs):
    n = len(ws)

    def body(*refs):
        for i in range(n):
            w_ref, g_ref, m_ref, v_ref = (refs[k * n + i] for k in range(4))
            d_ref, nm_ref, nv_ref = (refs[(4 + k) * n + i] for k in range(3))
            _adamw_math(g_ref[...], w_ref, m_ref, v_ref, d_ref, nm_ref, nv_ref)

    shapes = [jax.ShapeDtypeStruct(w.shape, F32) for w in ws]
    outs = pl.pallas_call(body, name="adamw_small", out_shape=shapes * 3)(*ws, *gs, *ms, *vs)
    return outs[:n], outs[n:2 * n], outs[2 * n:]


def _adamw_math(gv, w_ref, m_ref, v_ref, d_ref, nm_ref, nv_ref):
    nm = B1 * m_ref[...] + (1.0 - B1) * gv
    nv = B2 * v_ref[...] + (1.0 - B2) * (gv * gv)
    m_hat = nm / (1.0 - B1 ** STEP)
    v_hat = nv / (1.0 - B2 ** STEP)
    d_ref[...] = -LR * (m_hat / (jnp.sqrt(v_hat) + ADAM_EPS) + WD * w_ref[...])
    nm_ref[...] = nm
    nv_ref[...] = nv


def _adamw_halves(w, mine, other, m, v, c, name):
    hr, cols = mine.shape

    def body(c_ref, w_ref, mine_ref, other_ref, m_ref, v_ref, g_ref, d_ref, nm_ref, nv_ref):
        gv = jnp.where(pl.program_id(0) == c_ref[0], mine_ref[...], other_ref[...])
        g_ref[...] = gv
        _adamw_math(gv, w_ref, m_ref, v_ref, d_ref, nm_ref, nv_ref)

    half = pl.BlockSpec((hr, cols), lambda i, c_ref: (i, 0))
    whole = pl.BlockSpec((hr, cols), lambda i, c_ref: (0, 0))
    shp = jax.ShapeDtypeStruct(w.shape, F32)
    return pl.pallas_call(
        body, name=name, out_shape=[shp] * 4,
        grid_spec=pltpu.PrefetchScalarGridSpec(num_scalar_prefetch=1, grid=(2,), in_specs=[half, whole, whole, half, half],
                                               out_specs=[half] * 4),
        compiler_params=_params(dimension_semantics=("arbitrary",)),
    )(c.reshape(1), w, mine, other, m, v)


_ANY = pl.BlockSpec(memory_space=pl.ANY)


def _mesh_pos():
    return lax.axis_index("x"), lax.axis_index("y"), lax.axis_index("c")


def _other_chips(x, y):
    return [(1 - x, y), (x, 1 - y), (1 - x, 1 - y)]


def _remote(src, dst, send_sems, recv_sems, k, to):
    return pltpu.make_async_remote_copy(src_ref=src, dst_ref=dst, send_sem=send_sems.at[k], recv_sem=recv_sems.at[k],
                                        device_id=to, device_id_type=MESH)


def _gather_weights(shards):
    n = len(shards)

    def body(*refs):
        start, forward, drain = _gather_steps([s.shape for s in shards], refs[:n], refs[n:2 * n], refs[2 * n:3 * n],
                                              *refs[3 * n:])
        start()
        forward()
        drain()

    vmem = pl.BlockSpec(memory_space=pltpu.VMEM)
    return pl.pallas_call(
        body, name="gather_weights", in_specs=[vmem] * n, out_specs=[_ANY] * n,
        out_shape=_gathered_shapes(shards), scratch_shapes=_gather_scratch(shards), compiler_params=_params(),
    )(*shards)


def _gathered_shapes(shards):
    return [jax.ShapeDtypeStruct((N_CHIPS,) + s.shape, BF16) for s in shards]


def _gather_scratch(shards):
    n = len(shards)
    return ([pltpu.VMEM(s.shape, BF16) for s in shards]
            + [pltpu.SemaphoreType.DMA((6 * n,)), pltpu.SemaphoreType.DMA((6 * n,)), pltpu.SemaphoreType.DMA((n,))])


def _gather_steps(shapes, ins, outs, stage, send_sems, recv_sems, local_sems):
    n = len(shapes)
    halved = [s[0] % 32 == 0 for s in shapes]

    def part(i, ref, hc):
        if not halved[i]:
            return ref
        hr = shapes[i][0] // 2
        return ref.at[pl.ds(hc * hr, hr), :]

    def to_chip(i, j, x, y, c):
        cx, cy = _other_chips(x, y)[j]
        return _remote(part(i, stage[i], c), part(i, outs[i].at[2 * x + y], c), send_sems, recv_sems, 6 * i + j, (cx, cy, c))

    def to_sibling(i, j, x, y, c):
        cx, cy = _other_chips(x, y)[j]
        got = part(i, outs[i].at[2 * cx + cy], c)
        return _remote(got, got, send_sems, recv_sems, 6 * i + 3 + j, (x, y, 1 - c))

    def local(i, x, y):
        return pltpu.make_async_copy(stage[i], outs[i].at[2 * x + y], local_sems.at[i])

    def start():
        x, y, c = _mesh_pos()
        for i in range(n):
            stage[i][...] = ins[i][...].astype(BF16)
            local(i, x, y).start()
            for j in range(3):
                to_chip(i, j, x, y, c).start()

    def forward():
        x, y, c = _mesh_pos()
        for i in range(n):
            for j, (cx, cy) in enumerate(_other_chips(x, y)):
                got = part(i, outs[i].at[2 * cx + cy], c)
                _remote(got, got, send_sems, recv_sems, 6 * i + j, (cx, cy, c)).wait_recv()
                if halved[i]:
                    to_sibling(i, j, x, y, c).start()

    def drain():
        x, y, c = _mesh_pos()
        for i in range(n):
            for j, (cx, cy) in enumerate(_other_chips(x, y)):
                if halved[i]:
                    got = part(i, outs[i].at[2 * cx + cy], 1 - c)
                    _remote(got, got, send_sems, recv_sems, 6 * i + 3 + j, (x, y, 1 - c)).wait_recv()
                    to_sibling(i, j, x, y, c).wait_send()
                to_chip(i, j, x, y, c).wait_send()
            local(i, x, y).wait()

    return start, forward, drain


def _swap_halves(grads, whole, name):
    n, m = len(grads), len(grads) + len(whole)

    def body(*refs):
        ins, outs, send_sems, recv_sems = refs[:m], refs[m:2 * m], refs[2 * m], refs[2 * m + 1]
        x, y, c = _mesh_pos()
        cps = []
        for i in range(m):
            src = ins[i]
            if i < n:
                hr = grads[i].shape[1] // 2
                src = src.at[:, pl.ds((1 - c) * hr, hr), :]
            cp = _remote(src, outs[i], send_sems, recv_sems, i, (x, y, 1 - c))
            cp.start()
            cps.append(cp)
        for cp in cps:
            cp.wait()

    out_shape = [jax.ShapeDtypeStruct((g.shape[0], g.shape[1] // 2, g.shape[2]), F32) for g in grads]
    out_shape += [jax.ShapeDtypeStruct(w.shape, F32) for w in whole]
    outs = pl.pallas_call(
        body, name=name, in_specs=[_ANY] * m, out_specs=[_ANY] * m, out_shape=out_shape,
        scratch_shapes=[pltpu.SemaphoreType.DMA((m,)), pltpu.SemaphoreType.DMA((m,))],
    )(*grads, *whole)
    return outs[:n], outs[n:]


def _scattered_shapes(parts):
    return [jax.ShapeDtypeStruct(p.shape if p.ndim == 3 else (N_CHIPS,) + p.shape, p.dtype) for p in parts]


def _scatter_steps(ins, outs, send_sems, recv_sems, local_sems):
    n = len(ins)

    def src(i, k):
        return ins[i].at[k] if len(ins[i].shape) == 3 else ins[i]

    def sends(x, y, c):
        return [_remote(src(i, 2 * cx + cy), outs[i].at[2 * x + y], send_sems, recv_sems, 3 * i + j, (cx, cy, c))
                for i in range(n) for j, (cx, cy) in enumerate(_other_chips(x, y))]

    def local(i, x, y):
        return pltpu.make_async_copy(src(i, 2 * x + y), outs[i].at[2 * x + y], local_sems.at[i])

    def start():
        x, y, c = _mesh_pos()
        for i in range(n):
            local(i, x, y).start()
        for cp in sends(x, y, c):
            cp.start()

    def drain():
        x, y, c = _mesh_pos()
        for i in range(n):
            for j, (cx, cy) in enumerate(_other_chips(x, y)):
                got = outs[i].at[2 * cx + cy]
                _remote(got, got, send_sems, recv_sems, 3 * i + j, (cx, cy, c)).wait_recv()
        for cp in sends(x, y, c):
            cp.wait_send()
        for i in range(n):
            local(i, x, y).wait()

    return start, drain


def _add_pair(grads, from_sibling, small, small_sibling, c):
    n = len(grads)

    def body(c_ref, *refs):
        ins, outs = refs[:2 * n + 2], refs[2 * n + 2:]
        for i in range(n + 1):
            outs[i][...] = (ins[2 * i][...] + ins[2 * i + 1][...]).astype(outs[i].dtype)

    in_specs, out_specs, out_shape, args = [], [], [], []
    for g, r in zip(grads, from_sibling):
        _, hr, cols = r.shape
        in_specs += [pl.BlockSpec((1, hr, cols), lambda k, c_ref: (k, c_ref[0], 0)),
                     pl.BlockSpec((1, hr, cols), lambda k, c_ref: (k, 0, 0))]
        out_specs.append(pl.BlockSpec((1, hr, cols), lambda k, c_ref: (k, 0, 0)))
        out_shape.append(jax.ShapeDtypeStruct(r.shape, BF16))
        args += [g, r]
    whole = pl.BlockSpec(small.shape, lambda k, c_ref: (0, 0))
    in_specs += [whole, whole]
    out_specs.append(whole)
    out_shape.append(jax.ShapeDtypeStruct(small.shape, F32))
    outs = pl.pallas_call(
        body, name="add_pair", out_shape=out_shape,
        grid_spec=pltpu.PrefetchScalarGridSpec(num_scalar_prefetch=1, grid=(N_CHIPS,), in_specs=in_specs,
                                               out_specs=out_specs),
        compiler_params=_params(dimension_semantics=("arbitrary",)),
    )(c.reshape(1), *args, small, small_sibling)
    return outs[:n], outs[n]


def _scatter_to_chips(grad, from_sibling):
    hr = from_sibling.shape[1]

    def body(g_in, r_in, out, g_buf, r_buf, p_buf, load_sems, send_sems, recv_sems, local_sems):
        c = lax.axis_index("c")
        loads = (pltpu.make_async_copy(g_in.at[:, pl.ds(c * hr, hr), :], g_buf, load_sems.at[0]),
                 pltpu.make_async_copy(r_in, r_buf, load_sems.at[1]))
        for cp in loads:
            cp.start()
        for cp in loads:
            cp.wait()
        p_buf[...] = (g_buf[...] + r_buf[...]).astype(BF16)
        start, drain = _scatter_steps([p_buf], [out], send_sems, recv_sems, local_sems)
        start()
        drain()

    return pl.pallas_call(
        body, name="scatter_grads", in_specs=[_ANY] * 2, out_specs=_ANY,
        out_shape=jax.ShapeDtypeStruct(from_sibling.shape, BF16),
        scratch_shapes=[pltpu.VMEM(from_sibling.shape, F32)] * 2 + [pltpu.VMEM(from_sibling.shape, BF16)]
                       + [pltpu.SemaphoreType.DMA((2,)), pltpu.SemaphoreType.DMA((3,)), pltpu.SemaphoreType.DMA((3,)),
                          pltpu.SemaphoreType.DMA((1,))],
        compiler_params=_params(),
    )(grad, from_sibling)


def _share_halves(halves):
    n = len(halves)

    def body(*refs):
        ins, outs, send_sems, recv_sems = refs[:n], refs[n:2 * n], refs[2 * n], refs[2 * n + 1]
        x, y, c = _mesh_pos()
        cps = [_remote(ins[i], outs[i], send_sems, recv_sems, i, (x, y, 1 - c)) for i in range(n)]
        for cp in cps:
            cp.start()
        for cp in cps:
            cp.wait()

    return pl.pallas_call(
        body, name="share_halves", in_specs=[_ANY] * n, out_specs=[_ANY] * n,
        out_shape=[jax.ShapeDtypeStruct(h.shape, h.dtype) for h in halves],
        scratch_shapes=[pltpu.SemaphoreType.DMA((n,)), pltpu.SemaphoreType.DMA((n,))],
    )(*halves)


SHARD_COLS_IN = IN_TOTAL // N_CHIPS
KPE_END = Q_LORA + KV_LORA + ROPE


def _by_cols(a):
    return a.transpose(1, 0, 2).reshape(a.shape[1], N_CHIPS * a.shape[2])


def _assemble_early(c_in, c_uq, c_ukv, c_conv):
    w_in_e = jnp.concatenate([c_in[0][:, :KPE_END], jnp.zeros((D_MODEL, 64), BF16), c_in[0][:, KPE_END:],
                              c_in[1], c_in[2], c_in[3]], axis=1)
    w_uq_e = _by_cols(jnp.pad(c_uq, ((0, 0), (0, 0), (0, HEAD_PAD - QK_DIM))))
    return w_in_e, w_uq_e, _by_cols(c_ukv), _by_cols(c_conv).astype(F32)


def _assemble_late(c_o, c_pl, c_plg):
    return c_o.reshape(D_MODEL, D_MODEL), _by_cols(c_pl), c_plg.reshape(D_MODEL, D_MODEL)


def _split_w_in(dw_in_e):
    first = jnp.concatenate([dw_in_e[:, :KPE_END], dw_in_e[:, KPE_END + 64:SHARD_COLS_IN + 64]], axis=1)
    rest = [dw_in_e[:, SHARD_COLS_IN * k + 64:SHARD_COLS_IN * (k + 1) + 64] for k in range(1, N_CHIPS)]
    return jnp.stack([first] + rest)


def _split_others(dw_uq_e, dw_ukv, dw_o, dw_pl, dw_plg):
    chip_major = lambda a: a.reshape(a.shape[0], N_CHIPS, a.shape[1] // N_CHIPS).transpose(1, 0, 2)
    return [chip_major(dw_uq_e)[:, :, :QK_DIM], chip_major(dw_ukv), dw_o.reshape(N_CHIPS, D_MODEL // N_CHIPS, D_MODEL),
            chip_major(dw_pl), dw_plg.reshape(N_CHIPS, D_MODEL // N_CHIPS, D_MODEL)]


def _local_step(x, p, pos, tgt, gains, early, late_shards, late_gathered, tm, tq):
    w_in_e, w_uq_e, w_ukv, conv_w = early
    g_in, g_cq, g_ckv, g_q, g_k, g_oa, g_oc, g_pl = gains
    T = x.shape[0]
    zpad = lambda a, n: jnp.concatenate([a, jnp.zeros(a.shape[:-1] + (n,), a.dtype)], axis=-1)
    gq, gk = zpad(g_q, HEAD_PAD - QK_DIM), zpad(g_k, HEAD_PAD - QK_DIM)
    inv_freq = 1.0 / (ROPE_THETA ** (jnp.arange(0, ROPE, 2, dtype=F32) / ROPE))
    invf = jnp.concatenate([inv_freq, inv_freq, jnp.zeros((64,), F32)]).reshape(1, LANES)
    sgn = jnp.concatenate([-jnp.ones((32,), F32), jnp.ones((32,), F32), jnp.zeros((64,), F32)]).reshape(1, LANES)

    (proj, q, k, v), gathered = _fwd_proj(x, pos, g_in, w_in_e, g_cq, w_uq_e, g_ckv, w_ukv, gq, gk, invf, sgn,
                                          late_shards, min(2 * tm, T))
    w_o, w_pl, w_plg = _assemble_late(*(gathered if late_shards else late_gathered))
    o, lse = _attn_fwd(q, k, v, tq)
    (dx1, do, delta, dtail, du, dw_o, dw_pl, dw_plg, dg_oa, dg_oc, dg_pl, dconv, loss) = _tail(
        x, o, proj, p, tgt, g_oa, g_oc, g_pl, conv_w, w_o, w_pl, w_plg, tm)
    dq, dk, dv = _attn_bwd(q, k, v, do, lse, delta, tq)
    (gx, h, dproj, dw_uq_e, dw_ukv, dg_in, dg_cq, dg_ckv, dgq, dgk) = _bwd_proj(
        x, dx1, pos, proj, dq, dk, dv, dtail, du, g_in, w_in_e, g_cq, w_uq_e, g_ckv, w_ukv, gq, gk, conv_w, invf, sgn, tm)
    wgrads = (dw_uq_e, dw_ukv, dw_o, dw_pl, dw_plg)
    ggrads = (dg_in, dg_cq, dg_ckv, dgq, dgk, dg_oa, dg_oc, dg_pl)
    return loss, gx, (h, dproj), wgrads, ggrads, dconv


def kernel(x, p, positions, g_in, w_in, g_cq, w_uq, g_ckv, w_ukv, g_q, g_k, conv_w, g_oa, g_oc, w_o, w_pl, w_plg, g_pl, loss_target, m_g_in, m_w_in, m_g_cq, m_w_uq, m_g_ckv, m_w_ukv, m_g_q, m_g_k, m_conv_w, m_g_oa, m_g_oc, m_w_o, m_w_pl, m_w_plg, m_g_pl, v_g_in, v_w_in, v_g_cq, v_w_uq, v_g_ckv, v_w_ukv, v_g_q, v_g_k, v_conv_w, v_g_oa, v_g_oc, v_w_o, v_w_pl, v_w_plg, v_g_pl):
    T = x.shape[1]
    c = lax.axis_index("c")
    chip = 2 * lax.axis_index("x") + lax.axis_index("y")
    gains = [g.reshape(1, -1) for g in (g_in, g_cq, g_ckv, g_q, g_k, g_oa, g_oc, g_pl)]

    early = _assemble_early(*_gather_weights([w_in[0], w_uq[0], w_ukv[0], conv_w[0]]))

    loss, gx, (h_t, dproj), wgrads, ggrads, dconv = _local_step(
        x[0], p[0, 0], positions.reshape(T, 1), loss_target[0], gains, early, [w_o[0], w_pl[0], w_plg[0]], None, 256, 512)

    others_cm = _split_others(*wgrads)
    small_parts = [a.reshape(-1, LANES) for a in (*ggrads, loss, dconv)]
    small_rows = [a.shape[0] for a in small_parts]
    tile_rows = [-(-r // 8) * 8 for r in small_rows]
    tile_rows[-1] += -sum(tile_rows) % 16
    small = jnp.concatenate([jnp.pad(a, ((0, t - r), (0, 0))) for a, r, t in zip(small_parts, small_rows, tile_rows)])
    from_sibling, (small_sibling,) = _swap_halves(others_cm, [small], "pair_grads")
    chip_parts, chip_small = _add_pair(others_cm, from_sibling, small, small_sibling, c)
    dw_in_e, exchanged = _matmul_acc(h_t, dproj, min(4096, T), 512, [*chip_parts, chip_small])
    w_in_cm = _split_w_in(dw_in_e)
    (w_in_sibling,), _ = _swap_halves([w_in_cm], [], "pair_w_in")
    by_chip = [_scatter_to_chips(w_in_cm, w_in_sibling), *exchanged[:-1]]
    halves, small_total = _add_chips(by_chip, exchanged[-1])
    other_halves = _share_halves(halves)

    gg, off = [], 0
    for rows, tiled in zip(small_rows, tile_rows):
        gg.append(small_total[off:off + rows].reshape(1, -1))
        off += tiled
    loss_out = gg[8][0, 0]
    conv_total = gg[9].reshape(3, CONV_W)
    conv_g = lax.dynamic_slice(conv_total, (0, chip * (CONV_W // N_CHIPS)), (3, CONV_W // N_CHIPS))
    g_by_name = dict(g_in=gg[0], g_cq=gg[1], g_ckv=gg[2], g_q=gg[3][:, :QK_DIM], g_k=gg[4][:, :QK_DIM], conv_w=conv_g,
                     g_oa=gg[5], g_oc=gg[6], g_pl=gg[7])
    half_by_name = dict(zip(("w_in", "w_uq", "w_ukv", "w_o", "w_pl", "w_plg"), zip(halves, other_halves)))
    weights = dict(g_in=g_in, w_in=w_in, g_cq=g_cq, w_uq=w_uq, g_ckv=g_ckv, w_ukv=w_ukv, g_q=g_q, g_k=g_k,
                   conv_w=conv_w, g_oa=g_oa, g_oc=g_oc, w_o=w_o, w_pl=w_pl, w_plg=w_plg, g_pl=g_pl)
    ms = dict(g_in=m_g_in, w_in=m_w_in, g_cq=m_g_cq, w_uq=m_w_uq, g_ckv=m_g_ckv, w_ukv=m_w_ukv, g_q=m_g_q, g_k=m_g_k,
              conv_w=m_conv_w, g_oa=m_g_oa, g_oc=m_g_oc, w_o=m_w_o, w_pl=m_w_pl, w_plg=m_w_plg, g_pl=m_g_pl)
    vs = dict(g_in=v_g_in, w_in=v_w_in, g_cq=v_g_cq, w_uq=v_w_uq, g_ckv=v_g_ckv, w_ukv=v_w_ukv, g_q=v_g_q, g_k=v_g_k,
              conv_w=v_conv_w, g_oa=v_g_oa, g_oc=v_g_oc, w_o=v_w_o, w_pl=v_w_pl, w_plg=v_w_plg, g_pl=v_g_pl)
    names = list(weights)
    flat = lambda a: a.reshape(-1, a.shape[-1])
    small_names = list(g_by_name)
    small_out = _adamw_small([flat(weights[n]) for n in small_names], [flat(g_by_name[n]) for n in small_names],
                             [flat(ms[n]) for n in small_names], [flat(vs[n]) for n in small_names])
    results = {n: (flat(g_by_name[n]), *(out[i] for out in small_out)) for i, n in enumerate(small_names)}
    for n in half_by_name:
        results[n] = _adamw_halves(flat(weights[n]), *half_by_name[n], flat(ms[n]), flat(vs[n]), c, "adamw_" + n)
    per_kind = [[results[n][kind].reshape(weights[n].shape) for n in names] for kind in range(4)]
    return (loss_out, gx.reshape(x.shape), *per_kind[0], *per_kind[1], *per_kind[2], *per_kind[3])
```

```python
import math

import jax
import jax.numpy as jnp
from jax import lax
from jax.experimental import pallas as pl
from jax.experimental.pallas import tpu as pltpu

F32 = jnp.float32
BF16 = jnp.bfloat16

D_MODEL = 1024
N_HEADS = 4
NOPE = 128
ROPE = 64
V_DIM = 128
QK_DIM = NOPE + ROPE
HEAD_PAD = 256
Q_LORA = 256
KV_LORA = 128
ATTN_W = 512
CONV_W = 512
PLE = 256
IN_TOTAL = 3008
PROJ_EXT = 3072
ROPE_THETA = 10000.0
EPS = 1e-6
SCALE = 1.0 / math.sqrt(QK_DIM)
LOG2E = math.log2(math.e)
EXP2_SCALE = SCALE * LOG2E
NEG = -1e30
SOFTMAX_ROWS = 32
SUB_TILE = 256

LR, B1, B2, ADAM_EPS, WD, STEP = 0.001, 0.9, 0.999, 1e-08, 0.01, 10

N_CHIPS = 4
LANES = 128
VMEM_LIMIT = 56 * 1024 * 1024
MESH = pl.DeviceIdType.MESH


def _params(**kw):
    return pltpu.CompilerParams(vmem_limit_bytes=VMEM_LIMIT, **kw)


def _inv_rms(x, n):
    return lax.rsqrt(jnp.sum(x * x, axis=-1, keepdims=True) / n + EPS)


def _lane_sum(a):
    folded = a[:, 0:LANES]
    for c0 in range(LANES, a.shape[1], LANES):
        folded = folded + a[:, c0:c0 + LANES]
    head = folded.astype(BF16)
    tail = (folded - head.astype(F32)).astype(BF16)
    return _dot(jnp.concatenate([head, tail], axis=1), jnp.ones((2 * LANES, LANES), BF16))


def _inv_rms_mxu(x):
    return lax.rsqrt(_lane_sum(x * x) / x.shape[1] + EPS)


def _rep(r, width):
    return r if width == LANES else jnp.tile(r, (1, width // LANES))


def _sigmoid(z):
    return 1.0 / (1.0 + jnp.exp(-z))


def _swap_rope_halves(b):
    lane = lax.broadcasted_iota(jnp.int32, b.shape, 1)
    swapped = jnp.where(lane < 32, pltpu.roll(b, 96, 1), pltpu.roll(b, 32, 1))
    return jnp.where(lane < ROPE, swapped, 0.0)


def _dot(a, b):
    return jnp.dot(a, b, preferred_element_type=F32)


def _dot_nt(a, b):
    return lax.dot_general(a, b, (((1,), (1,)), ((), ())), preferred_element_type=F32)


def _dot_tn(a, b):
    return lax.dot_general(a, b, (((0,), (0,)), ((), ())), preferred_element_type=F32)


def _colsum(a):
    return jnp.sum(a, axis=0, keepdims=True)


def _full(shape):
    return pl.BlockSpec(shape, lambda *_: (0,) * len(shape))


def _round_robin(chains, width):
    waiting, active = list(chains), []
    while waiting or active:
        while waiting and len(active) < width:
            active.append(waiting.pop(0))
        for chain in list(active):
            if next(chain, _DONE) is _DONE:
                active.remove(chain)


_DONE = object()


def _rope_tables(pos_ref, invf_ref, sgn_ref):
    ang = pos_ref[...].astype(F32) * invf_ref[...]
    return jnp.cos(ang), jnp.sin(ang) * sgn_ref[...]


def _fwd_proj(x, pos, g_in, w_in, g_cq, w_uq, g_ckv, w_ukv, gq, gk, invf, sgn, late_shards, tm):
    T = x.shape[0]
    nt = T // tm
    n_late = len(late_shards)
    ts = min(SUB_TILE, tm)

    def body(x_ref, pos_ref, g_in_ref, w_in_ref, g_cq_ref, w_uq_ref, g_ckv_ref, w_ukv_ref, gq_ref, gk_ref,
             invf_ref, sgn_ref, *rest):
        late_in, (proj_ref, q_ref, k_ref, v_ref) = rest[:n_late], rest[n_late:n_late + 4]
        late_out, late_scratch = rest[n_late + 4:2 * n_late + 4], rest[2 * n_late + 4:]
        i = pl.program_id(0)
        if n_late:
            start, forward, drain = _gather_steps([s.shape for s in late_shards], late_in, late_out,
                                                  late_scratch[:n_late], *late_scratch[n_late:])
            pl.when(i == 0)(start)
            pl.when(i == nt // 2)(forward)

        for r0 in range(0, tm, ts):
            rows = slice(r0, r0 + ts)
            xv = x_ref[rows, :]
            h = (xv * _rep(_inv_rms_mxu(xv), D_MODEL) * g_in_ref[...]).astype(BF16)
            lat = _dot(h, w_in_ref[:, 0:512])
            proj_ref[rows, 0:512] = lat
            c_q = lat[:, 0:Q_LORA]
            cqn = (c_q * _rep(_inv_rms_mxu(c_q), Q_LORA) * g_cq_ref[...]).astype(BF16)
            c_kv = lat[:, Q_LORA:Q_LORA + KV_LORA]
            ckvn = (c_kv * _inv_rms_mxu(c_kv) * g_ckv_ref[...]).astype(BF16)
            kpe = lat[:, 384:512]
            kpe_sq = kpe * kpe
            cos_b, sin_b = _rope_tables(pos_ref.at[rows, :], invf_ref, sgn_ref)
            gq_a, gq_b = gq_ref[:, 0:NOPE], gq_ref[:, NOPE:HEAD_PAD]
            gk_a, gk_b = gk_ref[:, 0:NOPE], gk_ref[:, NOPE:HEAD_PAD]

            def projections(rows=rows, h=h):
                for c0 in range(512, PROJ_EXT, 512):
                    proj_ref[rows, c0:c0 + 512] = _dot(h, w_in_ref[:, c0:c0 + 512])
                    yield

            def queries(hd, rows=rows, cqn=cqn, cos_b=cos_b, sin_b=sin_b, gq_a=gq_a, gq_b=gq_b):
                qh = _dot(cqn, w_uq_ref[:, hd * HEAD_PAD:(hd + 1) * HEAD_PAD])
                yield
                a, b = qh[:, 0:NOPE], qh[:, NOPE:HEAD_PAD]
                r = lax.rsqrt(_lane_sum(a * a + b * b) / QK_DIM + EPS)
                yield
                bn = b * r * gq_b
                q_ref[hd, rows, 0:NOPE] = (a * r * gq_a).astype(BF16)
                q_ref[hd, rows, NOPE:HEAD_PAD] = (bn * cos_b + _swap_rope_halves(bn) * sin_b).astype(BF16)
                yield

            def keys(hd, rows=rows, ckvn=ckvn, kpe=kpe, kpe_sq=kpe_sq, cos_b=cos_b, sin_b=sin_b, gk_a=gk_a, gk_b=gk_b):
                kvh = _dot(ckvn, w_ukv_ref[:, hd * HEAD_PAD:(hd + 1) * HEAD_PAD])
                yield
                ka = kvh[:, 0:NOPE]
                rk = lax.rsqrt(_lane_sum(ka * ka + kpe_sq) / QK_DIM + EPS)
                yield
                kbn = kpe * rk * gk_b
                k_ref[hd, rows, 0:NOPE] = (ka * rk * gk_a).astype(BF16)
                k_ref[hd, rows, NOPE:HEAD_PAD] = (kbn * cos_b + _swap_rope_halves(kbn) * sin_b).astype(BF16)
                v_ref[hd, rows, 0:V_DIM] = kvh[:, NOPE:HEAD_PAD].astype(BF16)
                v_ref[hd, rows, V_DIM:2 * V_DIM] = jnp.ones((ts, V_DIM), BF16)
                yield

            chains = [projections()]
            for hd in range(N_HEADS):
                chains += [queries(hd), keys(hd)]
            _round_robin(chains, 4)

        if n_late:
            pl.when(i == nt - 1)(drain)

    row = lambda i: (i, 0)
    head_rows = lambda i: (0, i, 0)
    outs = pl.pallas_call(
        body, name="fwd_proj", grid=(nt,),
        in_specs=[pl.BlockSpec((tm, D_MODEL), row), pl.BlockSpec((tm, 1), row), _full((1, D_MODEL)),
                  _full((D_MODEL, PROJ_EXT)), _full((1, Q_LORA)), _full((Q_LORA, N_HEADS * HEAD_PAD)),
                  _full((1, KV_LORA)), _full((KV_LORA, N_HEADS * HEAD_PAD)), _full((1, HEAD_PAD)), _full((1, HEAD_PAD)),
                  _full((1, LANES)), _full((1, LANES))] + [_full(s.shape) for s in late_shards],
        out_specs=[pl.BlockSpec((tm, PROJ_EXT), row), pl.BlockSpec((N_HEADS, tm, HEAD_PAD), head_rows),
                   pl.BlockSpec((N_HEADS, tm, HEAD_PAD), head_rows), pl.BlockSpec((N_HEADS, tm, 2 * V_DIM), head_rows)]
                  + [_ANY] * n_late,
        out_shape=[jax.ShapeDtypeStruct((T, PROJ_EXT), F32), jax.ShapeDtypeStruct((N_HEADS, T, HEAD_PAD), BF16),
                   jax.ShapeDtypeStruct((N_HEADS, T, HEAD_PAD), BF16), jax.ShapeDtypeStruct((N_HEADS, T, 2 * V_DIM), BF16)]
                  + _gathered_shapes(late_shards),
        scratch_shapes=_gather_scratch(late_shards) if n_late else [],
        compiler_params=_params(dimension_semantics=("arbitrary",)),
    )(x, pos, g_in, w_in, g_cq, w_uq, g_ckv, w_ukv, gq, gk, invf, sgn, *late_shards)
    return outs[:4], outs[4:]


def _chunk_pipeline(n_loop, lag, matmuls, pointwise, accumulate, last):
    slots = lag + 1

    def iteration(t, slot):
        matmuls(jnp.minimum(t + lag, n_loop), (slot + lag) % slots)
        accumulate(jnp.maximum(t - lag, 0), (slot + 1) % slots)
        pointwise(t, slot, False)

    def finish(slot):
        for back in range(lag, 0, -1):
            accumulate(jnp.maximum(n_loop - back, 0), (slot - back) % slots)
        pointwise(n_loop, slot, True)
        accumulate(n_loop, slot)
        last()

    for u in range(lag):
        matmuls(jnp.minimum(u, n_loop), u)

    def unrolled(tt, carry):
        for slot in range(slots):
            iteration(slots * tt + slot, slot)
        return carry

    lax.fori_loop(0, n_loop // slots, unrolled, 0)
    rest = lax.rem(n_loop, slots)
    t0 = n_loop - rest

    for r in range(slots):
        @pl.when(rest == r)
        def _():
            for slot in range(r):
                iteration(t0 + slot, slot)
            finish(r)


def _attn_fwd(q, k, v, tq):
    T = q.shape[1]
    tk = tq
    rc = min(SOFTMAX_ROWS, tq)

    def body(q_ref, k_ref, v_ref, o_ref, lse_ref, s0, s1, s2, p0, p1, p2, a0, a1, a2, m_ref, acc_ref):
        qi = pl.program_id(1)
        s_buf, p_buf, a_buf = (s0, s1, s2), (p0, p1, p2), (a0, a1, a2)

        def scores(t, slot):
            ks = pl.multiple_of(t * tk, tk)
            s_buf[slot][...] = _dot_nt(q_ref[0], k_ref[0, pl.ds(ks, tk), :])

        def values(t, slot):
            ks = pl.multiple_of(t * tk, tk)
            acc_ref[...] = acc_ref[...] * a_buf[slot][...] + _dot(p_buf[slot][...], v_ref[0, pl.ds(ks, tk), :])

        def softmax(t, slot, masked):
            s_all = s_buf[slot][...]
            if masked:
                row = lax.broadcasted_iota(jnp.int32, (tq, tk), 0)
                col = lax.broadcasted_iota(jnp.int32, (tq, tk), 1)
                s_all = jnp.where(col <= row, s_all, NEG)
                s_buf[slot][...] = s_all
            m_old = m_ref[...]
            m_new = jnp.maximum(m_old, jnp.max(s_all, axis=1, keepdims=True))
            a_buf[slot][...] = jnp.exp2((m_old - m_new) * EXP2_SCALE)
            m_ref[...] = m_new
            for r0 in range(0, tq, rc):
                s = s_buf[slot][r0:r0 + rc, :]
                p_buf[slot][r0:r0 + rc, :] = jnp.exp2((s - m_new[r0:r0 + rc, :]) * EXP2_SCALE).astype(BF16)

        def last():
            l = acc_ref[:, V_DIM:2 * V_DIM]
            o_ref[...] = acc_ref[:, 0:V_DIM] / l
            lse_ref[0] = (m_ref[...] * SCALE + jnp.log(l)).T[0:1, :]

        m_ref[...] = jnp.full_like(m_ref, NEG)
        acc_ref[...] = jnp.zeros_like(acc_ref)
        for p_late, a_late in ((p1, a1), (p2, a2)):
            p_late[...] = jnp.zeros_like(p_late)
            a_late[...] = jnp.ones_like(a_late)
        _chunk_pipeline(qi, 2, scores, softmax, values, last)

    return pl.pallas_call(
        body, name="attn_fwd", grid=(N_HEADS, T // tq),
        in_specs=[pl.BlockSpec((1, tq, HEAD_PAD), lambda h, i: (h, i, 0)),
                  pl.BlockSpec((1, T, HEAD_PAD), lambda h, i: (h, 0, 0)),
                  pl.BlockSpec((1, T, 2 * V_DIM), lambda h, i: (h, 0, 0))],
        out_specs=[pl.BlockSpec((tq, V_DIM), lambda h, i: (i, h)),
                   pl.BlockSpec((1, 1, tq), lambda h, i: (h, 0, i))],
        out_shape=[jax.ShapeDtypeStruct((T, ATTN_W), F32), jax.ShapeDtypeStruct((N_HEADS, 1, T), F32)],
        scratch_shapes=[pltpu.VMEM((tq, tk), F32)] * 3 + [pltpu.VMEM((tq, tk), BF16)] * 3
                       + [pltpu.VMEM((tq, 1), F32)] * 4 + [pltpu.VMEM((tq, 2 * V_DIM), F32)],
        compiler_params=_params(dimension_semantics=("arbitrary", "arbitrary")),
    )(q, k, v)


def _tail(x, o, proj, p, tgt, g_oa, g_oc, g_pl, conv_w, w_o, w_pl, w_plg, tm):
    T = x.shape[0]
    nt = T // tm

    def body(x_ref, o_ref, za_ref, cb_ref, cc_ref, cx_ref, zc_ref, cch_ref, cxh_ref, p_ref, tgt_ref,
             g_oa_ref, g_oc_ref, g_pl_ref, cw_ref, w_o_ref, w_pl_ref, w_plg_ref,
             dx1_ref, do_ref, delta_ref, dtail_ref, du_ref,
             dw_o_ref, dw_pl_ref, dw_plg_ref, dg_oa_ref, dg_oc_ref, dg_pl_ref, dcw_ref, loss_ref):
        i = pl.program_id(0)

        @pl.when(i == 0)
        def _():
            for r in (dw_o_ref, dw_pl_ref, dw_plg_ref, dg_oa_ref, dg_oc_ref, dg_pl_ref, dcw_ref, loss_ref):
                r[...] = jnp.zeros_like(r)

        g_oa, g_oc, g_pl = g_oa_ref[...], g_oc_ref[...], g_pl_ref[...]
        w0, w1, w2 = cw_ref[0:1, :], cw_ref[1:2, :], cw_ref[2:3, :]

        xv, ov, za, cb, zc = x_ref[...], o_ref[...], za_ref[...], cb_ref[...], zc_ref[...]
        pb = p_ref[...].astype(BF16)
        pp = _dot(pb, w_pl_ref[...])

        sa = _sigmoid(za)
        silu_a = za * sa
        ga = ov * silu_a
        ra = _inv_rms(ga, ATTN_W)
        xa = ga * ra
        ya = xa * g_oa
        v = cc_ref[...] * cx_ref[...]
        not_first = jnp.where(i > 0, 1.0, 0.0)
        hv6 = cch_ref[6:7, :] * cxh_ref[6:7, :] * not_first
        hv7 = cch_ref[7:8, :] * cxh_ref[7:8, :] * not_first
        row = lax.broadcasted_iota(jnp.int32, v.shape, 0)
        v1 = jnp.where(row == 0, hv7, pltpu.roll(v, 1, 0))
        v2 = jnp.where(row == 0, hv6, jnp.where(row == 1, hv7, pltpu.roll(v, 2, 0)))
        u = w0 * v2 + w1 * v1 + w2 * v
        sc = _sigmoid(zc)
        silu_c = zc * sc
        gc = cb * u * silu_c
        rc = _inv_rms(gc, CONV_W)
        xc = gc * rc
        yc = xc * g_oc
        ycat = jnp.concatenate([ya, yc], axis=-1).astype(BF16)
        x1 = xv + _dot(ycat, w_o_ref[...])
        r1 = _inv_rms(x1, D_MODEL)
        xh1 = x1 * r1
        n1 = (xh1 * g_pl).astype(BF16)
        gate = _sigmoid(_dot(n1, w_plg_ref[...]))
        err = x1 + gate * pp - tgt_ref[...]
        loss_ref[...] += 0.5 * jnp.sum(err * err) / D_MODEL
        dy = err / D_MODEL

        dpp = (dy * gate).astype(BF16)
        da = (dy * pp * gate * (1.0 - gate)).astype(BF16)
        dn1 = _dot_nt(da, w_plg_ref[...])
        dw_pl_ref[...] += _dot_tn(pb, dpp)
        dw_plg_ref[...] += _dot_tn(n1, da)
        dg_pl_ref[...] += _colsum(dn1 * xh1)
        dxh = dn1 * g_pl
        dx1 = dy + r1 * (dxh - xh1 * (jnp.sum(dxh * xh1, axis=-1, keepdims=True) / D_MODEL))
        dx1_ref[...] = dx1
        dx1b = dx1.astype(BF16)
        dycat = _dot_nt(dx1b, w_o_ref[...])
        dya, dyc = dycat[:, 0:ATTN_W], dycat[:, ATTN_W:D_MODEL]

        dw_o_ref[0:ATTN_W, :] += _dot_tn(ycat[:, 0:ATTN_W], dx1b)
        dg_oa_ref[...] += _colsum(dya * xa)
        dxa = dya * g_oa
        dga = ra * (dxa - xa * (jnp.sum(dxa * xa, axis=-1, keepdims=True) / ATTN_W))
        do = (dga * silu_a).astype(BF16)
        do_ref[...] = do
        dof = do.astype(F32) * ov
        for hd in range(N_HEADS):
            delta_ref[hd] = _lane_sum(dof[:, hd * V_DIM:(hd + 1) * V_DIM]).T[0:1, :]
        dtail_ref[:, 0:512] = (dga * ov * (sa * (1.0 + za * (1.0 - sa)))).astype(BF16)

        dw_o_ref[ATTN_W:D_MODEL, :] += _dot_tn(ycat[:, ATTN_W:D_MODEL], dx1b)
        dg_oc_ref[...] += _colsum(dyc * xc)
        dxc = dyc * g_oc
        dgc = rc * (dxc - xc * (jnp.sum(dxc * xc, axis=-1, keepdims=True) / CONV_W))
        dtail_ref[:, 512:1024] = (dgc * u * silu_c).astype(BF16)
        du = dgc * cb * silu_c
        du_ref[...] = du
        dtail_ref[:, 1024:1536] = (dgc * cb * u * (sc * (1.0 + zc * (1.0 - sc)))).astype(BF16)
        dcw_ref[0:1, :] += _colsum(du * v2)
        dcw_ref[1:2, :] += _colsum(du * v1)
        dcw_ref[2:3, :] += _colsum(du * v)

    row = lambda i: (i, 0)
    col = lambda c: (lambda i: (i, c))
    halo = lambda c: (lambda i: (jnp.maximum(i * (tm // 8) - 1, 0), c))
    in_specs = [pl.BlockSpec((tm, D_MODEL), row), pl.BlockSpec((tm, ATTN_W), row)]
    in_specs += [pl.BlockSpec((tm, 512), col(c)) for c in (1, 2, 3, 4, 5)]
    in_specs += [pl.BlockSpec((8, 512), halo(3)), pl.BlockSpec((8, 512), halo(4))]
    in_specs += [pl.BlockSpec((tm, PLE), row), pl.BlockSpec((tm, D_MODEL), row),
                 _full((1, ATTN_W)), _full((1, CONV_W)), _full((1, D_MODEL)), _full((3, CONV_W)),
                 _full((D_MODEL, D_MODEL)), _full((PLE, D_MODEL)), _full((D_MODEL, D_MODEL))]
    out_specs = [pl.BlockSpec((tm, D_MODEL), row), pl.BlockSpec((tm, ATTN_W), row),
                 pl.BlockSpec((N_HEADS, 1, tm), lambda i: (0, 0, i)), pl.BlockSpec((tm, 1536), row),
                 pl.BlockSpec((tm, CONV_W), row),
                 _full((D_MODEL, D_MODEL)), _full((PLE, D_MODEL)), _full((D_MODEL, D_MODEL)),
                 _full((1, ATTN_W)), _full((1, CONV_W)), _full((1, D_MODEL)), _full((3, CONV_W)), _full((1, LANES))]
    out_shape = [jax.ShapeDtypeStruct((T, D_MODEL), F32), jax.ShapeDtypeStruct((T, ATTN_W), BF16),
                 jax.ShapeDtypeStruct((N_HEADS, 1, T), F32), jax.ShapeDtypeStruct((T, 1536), BF16),
                 jax.ShapeDtypeStruct((T, CONV_W), F32),
                 jax.ShapeDtypeStruct((D_MODEL, D_MODEL), F32), jax.ShapeDtypeStruct((PLE, D_MODEL), F32),
                 jax.ShapeDtypeStruct((D_MODEL, D_MODEL), F32),
                 jax.ShapeDtypeStruct((1, ATTN_W), F32), jax.ShapeDtypeStruct((1, CONV_W), F32),
                 jax.ShapeDtypeStruct((1, D_MODEL), F32), jax.ShapeDtypeStruct((3, CONV_W), F32),
                 jax.ShapeDtypeStruct((1, LANES), F32)]
    return pl.pallas_call(
        body, name="tail", grid=(nt,), in_specs=in_specs, out_specs=out_specs, out_shape=out_shape,
        compiler_params=_params(dimension_semantics=("arbitrary",)),
    )(x, o, proj, proj, proj, proj, proj, proj, proj, p, tgt, g_oa, g_oc, g_pl, conv_w, w_o, w_pl, w_plg)


def _attn_bwd(q, k, v, do, lse_row, delta_row, tk):
    T = q.shape[1]
    tq = tk
    nq = T // tq
    rc = min(SOFTMAX_ROWS, tk)

    def body(q_ref, k_ref, v_ref, do_ref, lse_ref, dl_ref, dq_ref, dk_ref, dv_ref,
             s0, s1, d0, d1, p0, p1, g0, g1, dk_acc, dv_acc):
        kj = pl.program_id(1)
        s_buf, dp_buf, p_buf, g_buf = (s0, s1), (d0, d1), (p0, p1), (g0, g1)

        @pl.when(kj == 0)
        def _():
            dq_ref[...] = jnp.zeros_like(dq_ref)

        def q_start(t):
            return pl.multiple_of((nq - 1 - t) * tq, tq)

        def matmuls(t, slot):
            qs = q_start(t)
            s_buf[slot][...] = _dot_nt(k_ref[0], q_ref[0, pl.ds(qs, tq), :])
            dp_buf[slot][...] = _dot_nt(v_ref[0], do_ref[pl.ds(qs, tq), :])

        def pointwise(t, slot, masked):
            qs = q_start(t)
            lse2 = lse_ref[0, :, pl.ds(qs, tq)] * LOG2E
            dl = dl_ref[0, :, pl.ds(qs, tq)]
            for r0 in range(0, tk, rc):
                st = s_buf[slot][r0:r0 + rc, :]
                if masked:
                    row = lax.broadcasted_iota(jnp.int32, (rc, tq), 0)
                    col = lax.broadcasted_iota(jnp.int32, (rc, tq), 1)
                    st = jnp.where(row + r0 <= col, st, NEG)
                pt = jnp.exp2(st * EXP2_SCALE - lse2)
                p_buf[slot][r0:r0 + rc, :] = pt.astype(BF16)
                g_buf[slot][r0:r0 + rc, :] = (pt * (dp_buf[slot][r0:r0 + rc, :] - dl) * SCALE).astype(BF16)

        def accumulate(t, slot):
            qs = q_start(t)
            dv_acc[...] += _dot(p_buf[slot][...], do_ref[pl.ds(qs, tq), :])
            dk_acc[...] += _dot(g_buf[slot][...], q_ref[0, pl.ds(qs, tq), :])
            dq_ref[0, pl.ds(qs, tq), :] += _dot_tn(g_buf[slot][...], k_ref[0])

        def last():
            dk_ref[0] = dk_acc[...]
            dv_ref[0] = dv_acc[...]

        dk_acc[...] = jnp.zeros_like(dk_acc)
        dv_acc[...] = jnp.zeros_like(dv_acc)
        for late in (p1, g1):
            late[...] = jnp.zeros_like(late)
        _chunk_pipeline(nq - 1 - kj, 1, matmuls, pointwise, accumulate, last)

    return pl.pallas_call(
        body, name="attn_bwd", grid=(N_HEADS, T // tk),
        in_specs=[pl.BlockSpec((1, T, HEAD_PAD), lambda h, j: (h, 0, 0)),
                  pl.BlockSpec((1, tk, HEAD_PAD), lambda h, j: (h, j, 0)),
                  pl.BlockSpec((1, tk, V_DIM), lambda h, j: (h, j, 0)),
                  pl.BlockSpec((T, V_DIM), lambda h, j: (0, h)),
                  pl.BlockSpec((1, 1, T), lambda h, j: (h, 0, 0)),
                  pl.BlockSpec((1, 1, T), lambda h, j: (h, 0, 0))],
        out_specs=[pl.BlockSpec((1, T, HEAD_PAD), lambda h, j: (h, 0, 0)),
                   pl.BlockSpec((1, tk, HEAD_PAD), lambda h, j: (h, j, 0)),
                   pl.BlockSpec((1, tk, V_DIM), lambda h, j: (h, j, 0))],
        out_shape=[jax.ShapeDtypeStruct((N_HEADS, T, HEAD_PAD), F32), jax.ShapeDtypeStruct((N_HEADS, T, HEAD_PAD), F32),
                   jax.ShapeDtypeStruct((N_HEADS, T, V_DIM), F32)],
        scratch_shapes=[pltpu.VMEM((tk, tq), F32)] * 4 + [pltpu.VMEM((tk, tq), BF16)] * 4
                       + [pltpu.VMEM((tk, HEAD_PAD), F32), pltpu.VMEM((tk, V_DIM), F32)],
        compiler_params=_params(dimension_semantics=("arbitrary", "arbitrary")),
    )(q, k, v, do, lse_row, delta_row)


def _bwd_proj(x, dx1, pos, proj, dq, dk, dv, dtail, du, g_in, w_in, g_cq, w_uq, g_ckv, w_ukv, gq, gk, conv_w,
              invf, sgn, tm):
    T = x.shape[0]
    nt = T // tm

    ts = min(SUB_TILE, tm)

    def body(x_ref, dx1_ref, pos_ref, lat_ref, cc_ref, cx_ref, dq_ref, dk_ref, dv_ref, dtail_ref, du_ref, dun_ref, *rest):
        consts, (gx_ref, h_ref, dproj_ref), sums = rest[:11], rest[11:14], rest[14:]
        cw_ref = consts[8]
        i = pl.program_id(0)

        @pl.when(i == 0)
        def _():
            for r in sums:
                r[...] = jnp.zeros_like(r)

        du_v = du_ref[...]
        not_last = jnp.where(i < nt - 1, 1.0, 0.0)
        nx0 = dun_ref[0:1, :] * not_last
        nx1 = dun_ref[1:2, :] * not_last
        row = lax.broadcasted_iota(jnp.int32, du_v.shape, 0)
        du1 = jnp.where(row == tm - 1, nx0, pltpu.roll(du_v, tm - 1, 0))
        du2 = jnp.where(row == tm - 2, nx0, jnp.where(row == tm - 1, nx1, pltpu.roll(du_v, tm - 2, 0)))
        dvc = cw_ref[2:3, :] * du_v + cw_ref[1:2, :] * du1 + cw_ref[0:1, :] * du2
        dproj_ref[:, 1536:2048] = (dvc * cx_ref[...]).astype(BF16)
        dproj_ref[:, 2048:2560] = (dvc * cc_ref[...]).astype(BF16)

        for r0 in range(0, tm, ts):
            rows = slice(r0, r0 + ts)
            work(x_ref.at[rows, :], dx1_ref.at[rows, :], pos_ref.at[rows, :], lat_ref.at[rows, :],
                 dq_ref.at[:, rows, :], dk_ref.at[:, rows, :], dv_ref.at[:, rows, :], dtail_ref.at[rows, :], *consts,
                 gx_ref.at[rows, :], h_ref.at[:, rows], dproj_ref.at[rows, :], *sums)

    def work(x_ref, dx1_ref, pos_ref, lat_ref, dq_ref, dk_ref, dv_ref, dtail_ref,
             g_in_ref, w_in_ref, g_cq_ref, w_uq_ref, g_ckv_ref, w_ukv_ref, gq_ref, gk_ref, cw_ref, invf_ref, sgn_ref,
             gx_ref, h_ref, dproj_ref, dw_uq_ref, dw_ukv_ref, dg_in_ref, dg_cq_ref, dg_ckv_ref, dgq_ref, dgk_ref):
        xv = x_ref[...]
        r0 = _rep(_inv_rms_mxu(xv), D_MODEL)
        xh0 = xv * r0
        g_in = g_in_ref[...]
        h_ref[...] = (xh0 * g_in).astype(BF16).T

        c_q = lat_ref[:, 0:Q_LORA]
        rq = _rep(_inv_rms_mxu(c_q), Q_LORA)
        xq = c_q * rq
        g_cq = g_cq_ref[...]
        cqn = (xq * g_cq).astype(BF16)
        c_kv = lat_ref[:, Q_LORA:Q_LORA + KV_LORA]
        rkv = _inv_rms_mxu(c_kv)
        xkv = c_kv * rkv
        g_ckv = g_ckv_ref[...]
        ckvn = (xkv * g_ckv).astype(BF16)
        kpe = lat_ref[:, 384:512]
        kpe_sq = kpe * kpe
        cos_b, sin_b = _rope_tables(pos_ref, invf_ref, sgn_ref)
        gq_a, gq_b = gq_ref[:, 0:NOPE], gq_ref[:, NOPE:HEAD_PAD]
        gk_a, gk_b = gk_ref[:, 0:NOPE], gk_ref[:, NOPE:HEAD_PAD]

        dproj_ref[:, 512:1536] = dtail_ref[:, 0:1024]
        dproj_ref[:, 2560:3072] = dtail_ref[:, 1024:1536]

        def dh_part(c0):
            return _dot_nt(dproj_ref[:, c0:c0 + 512], w_in_ref[:, c0:c0 + 512])

        later_chunks = ((512,), (1024,), (1536, 2048), (2560,))
        dh = jnp.zeros((ts, D_MODEL), F32)
        acc = dict(dh=dh, dkpe=jnp.zeros((ts, LANES), F32), dcqn=jnp.zeros((ts, Q_LORA), F32),
                   dckvn=jnp.zeros((ts, KV_LORA), F32))

        def dh_chunks():
            for chunks in later_chunks:
                for chunk in chunks:
                    acc["dh"] = acc["dh"] + dh_part(chunk)
                    yield

        def queries(hd):
            c0 = hd * HEAD_PAD
            qh = _dot(cqn, w_uq_ref[:, c0:c0 + HEAD_PAD])
            yield
            a, b = qh[:, 0:NOPE], qh[:, NOPE:HEAD_PAD]
            r = lax.rsqrt(_lane_sum(a * a + b * b) / QK_DIM + EPS)
            yield
            xa, xb = a * r, b * r
            dan = dq_ref[hd, :, 0:NOPE]
            dbr = dq_ref[hd, :, NOPE:HEAD_PAD]
            dbn = dbr * cos_b + _swap_rope_halves(dbr * sin_b)
            yield
            dgq_ref[:, 0:NOPE] += _colsum(dan * xa)
            dgq_ref[:, NOPE:HEAD_PAD] += _colsum(dbn * xb)
            dxa, dxb = dan * gq_a, dbn * gq_b
            cq = _lane_sum(dxa * xa + dxb * xb) / QK_DIM
            yield
            dqh = jnp.concatenate([r * (dxa - xa * cq), r * (dxb - xb * cq)], axis=-1).astype(BF16)
            yield
            dw_uq_ref[:, c0:c0 + HEAD_PAD] += _dot_tn(cqn, dqh)
            yield
            acc["dcqn"] = acc["dcqn"] + _dot_nt(dqh, w_uq_ref[:, c0:c0 + HEAD_PAD])
            yield

        def keys(hd):
            c0 = hd * HEAD_PAD
            kvh = _dot(ckvn, w_ukv_ref[:, c0:c0 + HEAD_PAD])
            yield
            ka = kvh[:, 0:NOPE]
            rk = lax.rsqrt(_lane_sum(ka * ka + kpe_sq) / QK_DIM + EPS)
            yield
            xka, xkb = ka * rk, kpe * rk
            dkan = dk_ref[hd, :, 0:NOPE]
            dkbr = dk_ref[hd, :, NOPE:HEAD_PAD]
            dkbn = dkbr * cos_b + _swap_rope_halves(dkbr * sin_b)
            yield
            dgk_ref[:, 0:NOPE] += _colsum(dkan * xka)
            dgk_ref[:, NOPE:HEAD_PAD] += _colsum(dkbn * xkb)
            dxka, dxkb = dkan * gk_a, dkbn * gk_b
            ck = _lane_sum(dxka * xka + dxkb * xkb) / QK_DIM
            yield
            acc["dkpe"] = acc["dkpe"] + rk * (dxkb - xkb * ck)
            dkvh = jnp.concatenate([rk * (dxka - xka * ck), dv_ref[hd]], axis=-1).astype(BF16)
            yield
            dw_ukv_ref[:, c0:c0 + HEAD_PAD] += _dot_tn(ckvn, dkvh)
            yield
            acc["dckvn"] = acc["dckvn"] + _dot_nt(dkvh, w_ukv_ref[:, c0:c0 + HEAD_PAD])
            yield

        chains = [dh_chunks()]
        for hd in range(N_HEADS):
            chains += [queries(hd), keys(hd)]
        _round_robin(chains, 5)
        dh, dkpe, dcqn, dckvn = acc["dh"], acc["dkpe"], acc["dcqn"], acc["dckvn"]

        dg_cq_ref[...] += _colsum(dcqn * xq)
        dxq = dcqn * g_cq
        dproj_ref[:, 0:Q_LORA] = (rq * (dxq - xq * _rep(_lane_sum(dxq * xq) / Q_LORA, Q_LORA))).astype(BF16)
        dg_ckv_ref[...] += _colsum(dckvn * xkv)
        dxkv = dckvn * g_ckv
        dproj_ref[:, 256:384] = (rkv * (dxkv - xkv * (_lane_sum(dxkv * xkv) / KV_LORA))).astype(BF16)
        dproj_ref[:, 384:512] = dkpe.astype(BF16)
        dh = dh + dh_part(0)
        dg_in_ref[...] += _colsum(dh * xh0)
        dxh = dh * g_in
        gx_ref[...] = dx1_ref[...] + r0 * (dxh - xh0 * _rep(_lane_sum(dxh * xh0) / D_MODEL, D_MODEL))

    row = lambda i: (i, 0)
    col = lambda c: (lambda i: (i, c))
    head_rows = lambda i: (0, i, 0)
    nxt = lambda i: (jnp.minimum((i + 1) * (tm // 8), T // 8 - 1), 0)
    in_specs = [pl.BlockSpec((tm, D_MODEL), row), pl.BlockSpec((tm, D_MODEL), row), pl.BlockSpec((tm, 1), row),
                pl.BlockSpec((tm, 512), col(0)), pl.BlockSpec((tm, 512), col(3)), pl.BlockSpec((tm, 512), col(4)),
                pl.BlockSpec((N_HEADS, tm, HEAD_PAD), head_rows), pl.BlockSpec((N_HEADS, tm, HEAD_PAD), head_rows),
                pl.BlockSpec((N_HEADS, tm, V_DIM), head_rows), pl.BlockSpec((tm, 1536), row),
                pl.BlockSpec((tm, CONV_W), row), pl.BlockSpec((8, CONV_W), nxt),
                _full((1, D_MODEL)), _full((D_MODEL, PROJ_EXT)), _full((1, Q_LORA)), _full((Q_LORA, N_HEADS * HEAD_PAD)),
                _full((1, KV_LORA)), _full((KV_LORA, N_HEADS * HEAD_PAD)), _full((1, HEAD_PAD)), _full((1, HEAD_PAD)),
                _full((3, CONV_W)), _full((1, LANES)), _full((1, LANES))]
    out_specs = [pl.BlockSpec((tm, D_MODEL), row), pl.BlockSpec((D_MODEL, tm), lambda i: (0, i)),
                 pl.BlockSpec((tm, PROJ_EXT), row),
                 _full((Q_LORA, N_HEADS * HEAD_PAD)), _full((KV_LORA, N_HEADS * HEAD_PAD)),
                 _full((1, D_MODEL)), _full((1, Q_LORA)), _full((1, KV_LORA)), _full((1, HEAD_PAD)), _full((1, HEAD_PAD))]
    out_shape = [jax.ShapeDtypeStruct((T, D_MODEL), F32), jax.ShapeDtypeStruct((D_MODEL, T), BF16),
                 jax.ShapeDtypeStruct((T, PROJ_EXT), BF16),
                 jax.ShapeDtypeStruct((Q_LORA, N_HEADS * HEAD_PAD), F32), jax.ShapeDtypeStruct((KV_LORA, N_HEADS * HEAD_PAD), F32),
                 jax.ShapeDtypeStruct((1, D_MODEL), F32), jax.ShapeDtypeStruct((1, Q_LORA), F32),
                 jax.ShapeDtypeStruct((1, KV_LORA), F32), jax.ShapeDtypeStruct((1, HEAD_PAD), F32),
                 jax.ShapeDtypeStruct((1, HEAD_PAD), F32)]
    return pl.pallas_call(
        body, name="bwd_proj", grid=(nt,), in_specs=in_specs, out_specs=out_specs, out_shape=out_shape,
        compiler_params=_params(dimension_semantics=("arbitrary",)),
    )(x, dx1, pos, proj, proj, proj, dq, dk, dv, dtail, du, du, g_in, w_in, g_cq, w_uq, g_ckv, w_ukv, gq, gk, conv_w,
      invf, sgn)


def _matmul_acc(a, b, tt, tn, parts):
    M, T = a.shape
    N = b.shape[1]
    n = len(parts)
    grid = (N // tn, T // tt)

    def body(a_ref, b_ref, *rest):
        part_refs, o_ref, out_refs, sems = rest[:n], rest[n], rest[n + 1:2 * n + 1], rest[2 * n + 1:]
        j, t = pl.program_id(0), pl.program_id(1)
        if n:
            start, drain = _scatter_steps(part_refs, out_refs, *sems)
            pl.when(jnp.logical_and(j == 0, t == 0))(start)

        @pl.when(t == 0)
        def _():
            o_ref[...] = jnp.zeros_like(o_ref)

        o_ref[...] += _dot(a_ref[...], b_ref[...])
        if n:
            pl.when(jnp.logical_and(j == grid[0] - 1, t == grid[1] - 1))(drain)

    sems = [pltpu.SemaphoreType.DMA((3 * n,)), pltpu.SemaphoreType.DMA((3 * n,)), pltpu.SemaphoreType.DMA((n,))]
    outs = pl.pallas_call(
        body, name="dw_in", grid=grid,
        in_specs=[pl.BlockSpec((M, tt), lambda j, t: (0, t)), pl.BlockSpec((tt, tn), lambda j, t: (t, j))] + [_ANY] * n,
        out_specs=[pl.BlockSpec((M, tn), lambda j, t: (0, j))] + [_ANY] * n,
        out_shape=[jax.ShapeDtypeStruct((M, N), F32)] + _scattered_shapes(parts),
        scratch_shapes=sems if n else [],
        compiler_params=_params(dimension_semantics=("arbitrary", "arbitrary")),
    )(a, b, *parts)
    return outs[0], outs[1:]


def _add_chips(parts, small_parts):
    arrays = list(parts) + [small_parts]

    def body(*refs):
        ins, outs = refs[:len(arrays)], refs[len(arrays):]
        for a_ref, o_ref in zip(ins, outs):
            part = lambda k: a_ref[k].astype(F32)
            o_ref[...] = ((part(0) + part(1)) + part(2)) + part(3)

    in_specs, out_specs, out_shape = [], [], []
    for a in arrays:
        _, rows, cols = a.shape
        in_specs.append(pl.BlockSpec((N_CHIPS, rows // 2, cols), lambda i: (0, i, 0)))
        out_specs.append(pl.BlockSpec((rows // 2, cols), lambda i: (i, 0)))
        out_shape.append(jax.ShapeDtypeStruct((rows, cols), F32))
    outs = pl.pallas_call(body, name="add_chips", grid=(2,), in_specs=in_specs, out_specs=out_specs,
                          out_shape=out_shape, compiler_params=_params(dimension_semantics=("arbitrary",)))(*arrays)
    return outs[:-1], outs[-1]


def _adamw_small(ws, gs, ms, vs):
    n = len(ws)

    def body(*refs):
        for i in range(n):
            w_ref, g_ref, m_ref, v_ref = (refs[k * n + i] for k in range(4))
            d_ref, nm_ref, nv_ref = (refs[(4 + k) * n + i] for k in range(3))
            _adamw_math(g_ref[...], w_ref, m_ref, v_ref, d_ref, nm_ref, nv_ref)

    shapes = [jax.ShapeDtypeStruct(w.shape, F32) for w in ws]
    outs = pl.pallas_call(body, name="adamw_small", out_shape=shapes * 3)(*ws, *gs, *ms, *vs)
    return outs[:n], outs[n:2 * n], outs[2 * n:]


def _adamw_math(gv, w_ref, m_ref, v_ref, d_ref, nm_ref, nv_ref):
    nm = B1 * m_ref[...] + (1.0 - B1) * gv
    nv = B2 * v_ref[...] + (1.0 - B2) * (gv * gv)
    m_hat = nm / (1.0 - B1 ** STEP)
    v_hat = nv / (1.0 - B2 ** STEP)
    d_ref[...] = -LR * (m_hat / (jnp.sqrt(v_hat) + ADAM_EPS) + WD * w_ref[...])
    nm_ref[...] = nm
    nv_ref[...] = nv


def _adamw_halves(w, mine, other, m, v, c, name):
    hr, cols = mine.shape

    def body(c_ref, w_ref, mine_ref, other_ref, m_ref, v_ref, g_ref, d_ref, nm_ref, nv_ref):
        gv = jnp.where(pl.program_id(0) == c_ref[0], mine_ref[...], other_ref[...])
        g_ref[...] = gv
        _adamw_math(gv, w_ref, m_ref, v_ref, d_ref, nm_ref, nv_ref)

    half = pl.BlockSpec((hr, cols), lambda i, c_ref: (i, 0))
    whole = pl.BlockSpec((hr, cols), lambda i, c_ref: (0, 0))
    shp = jax.ShapeDtypeStruct(w.shape, F32)
    return pl.pallas_call(
        body, name=name, out_shape=[shp] * 4,
        grid_spec=pltpu.PrefetchScalarGridSpec(num_scalar_prefetch=1, grid=(2,), in_specs=[half, whole, whole, half, half],
                                               out_specs=[half] * 4),
        compiler_params=_params(dimension_semantics=("arbitrary",)),
    )(c.reshape(1), w, mine, other, m, v)


_ANY = pl.BlockSpec(memory_space=pl.ANY)


def _mesh_pos():
    return lax.axis_index("x"), lax.axis_index("y"), lax.axis_index("c")


def _other_chips(x, y):
    return [(1 - x, y), (x, 1 - y), (1 - x, 1 - y)]


def _remote(src, dst, send_sems, recv_sems, k, to):
    return pltpu.make_async_remote_copy(src_ref=src, dst_ref=dst, send_sem=send_sems.at[k], recv_sem=recv_sems.at[k],
                                        device_id=to, device_id_type=MESH)


def _gather_weights(shards):
    n = len(shards)

    def body(*refs):
        start, forward, drain = _gather_steps([s.shape for s in shards], refs[:n], refs[n:2 * n], refs[2 * n:3 * n],
                                              *refs[3 * n:])
        start()
        forward()
        drain()

    vmem = pl.BlockSpec(memory_space=pltpu.VMEM)
    return pl.pallas_call(
        body, name="gather_weights", in_specs=[vmem] * n, out_specs=[_ANY] * n,
        out_shape=_gathered_shapes(shards), scratch_shapes=_gather_scratch(shards), compiler_params=_params(),
    )(*shards)


def _gathered_shapes(shards):
    return [jax.ShapeDtypeStruct((N_CHIPS,) + s.shape, BF16) for s in shards]


def _gather_scratch(shards):
    n = len(shards)
    return ([pltpu.VMEM(s.shape, BF16) for s in shards]
            + [pltpu.SemaphoreType.DMA((6 * n,)), pltpu.SemaphoreType.DMA((6 * n,)), pltpu.SemaphoreType.DMA((n,))])


def _gather_steps(shapes, ins, outs, stage, send_sems, recv_sems, local_sems):
    n = len(shapes)
    halved = [s[0] % 32 == 0 for s in shapes]

    def part(i, ref, hc):
        if not halved[i]:
            return ref
        hr = shapes[i][0] // 2
        return ref.at[pl.ds(hc * hr, hr), :]

    def to_chip(i, j, x, y, c):
        cx, cy = _other_chips(x, y)[j]
        return _remote(part(i, stage[i], c), part(i, outs[i].at[2 * x + y], c), send_sems, recv_sems, 6 * i + j, (cx, cy, c))

    def to_sibling(i, j, x, y, c):
        cx, cy = _other_chips(x, y)[j]
        got = part(i, outs[i].at[2 * cx + cy], c)
        return _remote(got, got, send_sems, recv_sems, 6 * i + 3 + j, (x, y, 1 - c))

    def local(i, x, y):
        return pltpu.make_async_copy(stage[i], outs[i].at[2 * x + y], local_sems.at[i])

    def start():
        x, y, c = _mesh_pos()
        for i in range(n):
            stage[i][...] = ins[i][...].astype(BF16)
            local(i, x, y).start()
            for j in range(3):
                to_chip(i, j, x, y, c).start()

    def forward():
        x, y, c = _mesh_pos()
        for i in range(n):
            for j, (cx, cy) in enumerate(_other_chips(x, y)):
                got = part(i, outs[i].at[2 * cx + cy], c)
                _remote(got, got, send_sems, recv_sems, 6 * i + j, (cx, cy, c)).wait_recv()
                if halved[i]:
                    to_sibling(i, j, x, y, c).start()

    def drain():
        x, y, c = _mesh_pos()
        for i in range(n):
            for j, (cx, cy) in enumerate(_other_chips(x, y)):
                if halved[i]:
                    got = part(i, outs[i].at[2 * cx + cy], 1 - c)
                    _remote(got, got, send_sems, recv_sems, 6 * i + 3 + j, (x, y, 1 - c)).wait_recv()
                    to_sibling(i, j, x, y, c).wait_send()
                to_chip(i, j, x, y, c).wait_send()
            local(i, x, y).wait()

    return start, forward, drain


def _swap_halves(grads, whole, name):
    n, m = len(grads), len(grads) + len(whole)

    def body(*refs):
        ins, outs, send_sems, recv_sems = refs[:m], refs[m:2 * m], refs[2 * m], refs[2 * m + 1]
        x, y, c = _mesh_pos()
        cps = []
        for i in range(m):
            src = ins[i]
            if i < n:
                hr = grads[i].shape[1] // 2
                src = src.at[:, pl.ds((1 - c) * hr, hr), :]
            cp = _remote(src, outs[i], send_sems, recv_sems, i, (x, y, 1 - c))
            cp.start()
            cps.append(cp)
        for cp in cps:
            cp.wait()

    out_shape = [jax.ShapeDtypeStruct((g.shape[0], g.shape[1] // 2, g.shape[2]), F32) for g in grads]
    out_shape += [jax.ShapeDtypeStruct(w.shape, F32) for w in whole]
    outs = pl.pallas_call(
        body, name=name, in_specs=[_ANY] * m, out_specs=[_ANY] * m, out_shape=out_shape,
        scratch_shapes=[pltpu.SemaphoreType.DMA((m,)), pltpu.SemaphoreType.DMA((m,))],
    )(*grads, *whole)
    return outs[:n], outs[n:]


def _scattered_shapes(parts):
    return [jax.ShapeDtypeStruct(p.shape if p.ndim == 3 else (N_CHIPS,) + p.shape, p.dtype) for p in parts]


def _scatter_steps(ins, outs, send_sems, recv_sems, local_sems):
    n = len(ins)

    def src(i, k):
        return ins[i].at[k] if len(ins[i].shape) == 3 else ins[i]

    def sends(x, y, c):
        return [_remote(src(i, 2 * cx + cy), outs[i].at[2 * x + y], send_sems, recv_sems, 3 * i + j, (cx, cy, c))
                for i in range(n) for j, (cx, cy) in enumerate(_other_chips(x, y))]

    def local(i, x, y):
        return pltpu.make_async_copy(src(i, 2 * x + y), outs[i].at[2 * x + y], local_sems.at[i])

    def start():
        x, y, c = _mesh_pos()
        for i in range(n):
            local(i, x, y).start()
        for cp in sends(x, y, c):
            cp.start()

    def drain():
        x, y, c = _mesh_pos()
        for i in range(n):
            for j, (cx, cy) in enumerate(_other_chips(x, y)):
                got = outs[i].at[2 * cx + cy]
                _remote(got, got, send_sems, recv_sems, 3 * i + j, (cx, cy, c)).wait_recv()
        for cp in sends(x, y, c):
            cp.wait_send()
        for i in range(n):
            local(i, x, y).wait()

    return start, drain


def _add_pair(grads, from_sibling, small, small_sibling, c):
    n = len(grads)

    def body(c_ref, *refs):
        ins, outs = refs[:2 * n + 2], refs[2 * n + 2:]
        for i in range(n + 1):
            outs[i][...] = (ins[2 * i][...] + ins[2 * i + 1][...]).astype(outs[i].dtype)

    in_specs, out_specs, out_shape, args = [], [], [], []
    for g, r in zip(grads, from_sibling):
        _, hr, cols = r.shape
        in_specs += [pl.BlockSpec((1, hr, cols), lambda k, c_ref: (k, c_ref[0], 0)),
                     pl.BlockSpec((1, hr, cols), lambda k, c_ref: (k, 0, 0))]
        out_specs.append(pl.BlockSpec((1, hr, cols), lambda k, c_ref: (k, 0, 0)))
        out_shape.append(jax.ShapeDtypeStruct(r.shape, BF16))
        args += [g, r]
    whole = pl.BlockSpec(small.shape, lambda k, c_ref: (0, 0))
    in_specs += [whole, whole]
    out_specs.append(whole)
    out_shape.append(jax.ShapeDtypeStruct(small.shape, F32))
    outs = pl.pallas_call(
        body, name="add_pair", out_shape=out_shape,
        grid_spec=pltpu.PrefetchScalarGridSpec(num_scalar_prefetch=1, grid=(N_CHIPS,), in_specs=in_specs,
                                               out_specs=out_specs),
        compiler_params=_params(dimension_semantics=("arbitrary",)),
    )(c.reshape(1), *args, small, small_sibling)
    return outs[:n], outs[n]


def _scatter_to_chips(grad, from_sibling):
    hr = from_sibling.shape[1]

    def body(g_in, r_in, out, g_buf, r_buf, p_buf, load_sems, send_sems, recv_sems, local_sems):
        c = lax.axis_index("c")
        loads = (pltpu.make_async_copy(g_in.at[:, pl.ds(c * hr, hr), :], g_buf, load_sems.at[0]),
                 pltpu.make_async_copy(r_in, r_buf, load_sems.at[1]))
        for cp in loads:
            cp.start()
        for cp in loads:
            cp.wait()
        p_buf[...] = (g_buf[...] + r_buf[...]).astype(BF16)
        start, drain = _scatter_steps([p_buf], [out], send_sems, recv_sems, local_sems)
        start()
        drain()

    return pl.pallas_call(
        body, name="scatter_grads", in_specs=[_ANY] * 2, out_specs=_ANY,
        out_shape=jax.ShapeDtypeStruct(from_sibling.shape, BF16),
        scratch_shapes=[pltpu.VMEM(from_sibling.shape, F32)] * 2 + [pltpu.VMEM(from_sibling.shape, BF16)]
                       + [pltpu.SemaphoreType.DMA((2,)), pltpu.SemaphoreType.DMA((3,)), pltpu.SemaphoreType.DMA((3,)),
                          pltpu.SemaphoreType.DMA((1,))],
        compiler_params=_params(),
    )(grad, from_sibling)


def _share_halves(halves):
    n = len(halves)

    def body(*refs):
        ins, outs, send_sems, recv_sems = refs[:n], refs[n:2 * n], refs[2 * n], refs[2 * n + 1]
        x, y, c = _mesh_pos()
        cps = [_remote(ins[i], outs[i], send_sems, recv_sems, i, (x, y, 1 - c)) for i in range(n)]
        for cp in cps:
            cp.start()
        for cp in cps:
            cp.wait()

    return pl.pallas_call(
        body, name="share_halves", in_specs=[_ANY] * n, out_specs=[_ANY] * n,
        out_shape=[jax.ShapeDtypeStruct(h.shape, h.dtype) for h in halves],
        scratch_shapes=[pltpu.SemaphoreType.DMA((n,)), pltpu.SemaphoreType.DMA((n,))],
    )(*halves)


SHARD_COLS_IN = IN_TOTAL // N_CHIPS
KPE_END = Q_LORA + KV_LORA + ROPE


def _by_cols(a):
    return a.transpose(1, 0, 2).reshape(a.shape[1], N_CHIPS * a.shape[2])


def _assemble_early(c_in, c_uq, c_ukv, c_conv):
    w_in_e = jnp.concatenate([c_in[0][:, :KPE_END], jnp.zeros((D_MODEL, 64), BF16), c_in[0][:, KPE_END:],
                              c_in[1], c_in[2], c_in[3]], axis=1)
    w_uq_e = _by_cols(jnp.pad(c_uq, ((0, 0), (0, 0), (0, HEAD_PAD - QK_DIM))))
    return w_in_e, w_uq_e, _by_cols(c_ukv), _by_cols(c_conv).astype(F32)


def _assemble_late(c_o, c_pl, c_plg):
    return c_o.reshape(D_MODEL, D_MODEL), _by_cols(c_pl), c_plg.reshape(D_MODEL, D_MODEL)


def _split_w_in(dw_in_e):
    first = jnp.concatenate([dw_in_e[:, :KPE_END], dw_in_e[:, KPE_END + 64:SHARD_COLS_IN + 64]], axis=1)
    rest = [dw_in_e[:, SHARD_COLS_IN * k + 64:SHARD_COLS_IN * (k + 1) + 64] for k in range(1, N_CHIPS)]
    return jnp.stack([first] + rest)


def _split_others(dw_uq_e, dw_ukv, dw_o, dw_pl, dw_plg):
    chip_major = lambda a: a.reshape(a.shape[0], N_CHIPS, a.shape[1] // N_CHIPS).transpose(1, 0, 2)
    return [chip_major(dw_uq_e)[:, :, :QK_DIM], chip_major(dw_ukv), dw_o.reshape(N_CHIPS, D_MODEL // N_CHIPS, D_MODEL),
            chip_major(dw_pl), dw_plg.reshape(N_CHIPS, D_MODEL // N_CHIPS, D_MODEL)]


def _local_step(x, p, pos, tgt, gains, early, late_shards, late_gathered, tm, tq):
    w_in_e, w_uq_e, w_ukv, conv_w = early
    g_in, g_cq, g_ckv, g_q, g_k, g_oa, g_oc, g_pl = gains
    T = x.shape[0]
    zpad = lambda a, n: jnp.concatenate([a, jnp.zeros(a.shape[:-1] + (n,), a.dtype)], axis=-1)
    gq, gk = zpad(g_q, HEAD_PAD - QK_DIM), zpad(g_k, HEAD_PAD - QK_DIM)
    inv_freq = 1.0 / (ROPE_THETA ** (jnp.arange(0, ROPE, 2, dtype=F32) / ROPE))
    invf = jnp.concatenate([inv_freq, inv_freq, jnp.zeros((64,), F32)]).reshape(1, LANES)
    sgn = jnp.concatenate([-jnp.ones((32,), F32), jnp.ones((32,), F32), jnp.zeros((64,), F32)]).reshape(1, LANES)

    (proj, q, k, v), gathered = _fwd_proj(x, pos, g_in, w_in_e, g_cq, w_uq_e, g_ckv, w_ukv, gq, gk, invf, sgn,
                                          late_shards, min(2 * tm, T))
    w_o, w_pl, w_plg = _assemble_late(*(gathered if late_shards else late_gathered))
    o, lse = _attn_fwd(q, k, v, tq)
    (dx1, do, delta, dtail, du, dw_o, dw_pl, dw_plg, dg_oa, dg_oc, dg_pl, dconv, loss) = _tail(
        x, o, proj, p, tgt, g_oa, g_oc, g_pl, conv_w, w_o, w_pl, w_plg, tm)
    dq, dk, dv = _attn_bwd(q, k, v, do, lse, delta, tq)
    (gx, h, dproj, dw_uq_e, dw_ukv, dg_in, dg_cq, dg_ckv, dgq, dgk) = _bwd_proj(
        x, dx1, pos, proj, dq, dk, dv, dtail, du, g_in, w_in_e, g_cq, w_uq_e, g_ckv, w_ukv, gq, gk, conv_w, invf, sgn, tm)
    wgrads = (dw_uq_e, dw_ukv, dw_o, dw_pl, dw_plg)
    ggrads = (dg_in, dg_cq, dg_ckv, dgq, dgk, dg_oa, dg_oc, dg_pl)
    return loss, gx, (h, dproj), wgrads, ggrads, dconv


def kernel(x, p, positions, g_in, w_in, g_cq, w_uq, g_ckv, w_ukv, g_q, g_k, conv_w, g_oa, g_oc, w_o, w_pl, w_plg, g_pl, loss_target, m_g_in, m_w_in, m_g_cq, m_w_uq, m_g_ckv, m_w_ukv, m_g_q, m_g_k, m_conv_w, m_g_oa, m_g_oc, m_w_o, m_w_pl, m_w_plg, m_g_pl, v_g_in, v_w_in, v_g_cq, v_w_uq, v_g_ckv, v_w_ukv, v_g_q, v_g_k, v_conv_w, v_g_oa, v_g_oc, v_w_o, v_w_pl, v_w_plg, v_g_pl):
    T = x.shape[1]
    c = lax.axis_index("c")
    chip = 2 * lax.axis_index("x") + lax.axis_index("y")
    gains = [g.reshape(1, -1) for g in (g_in, g_cq, g_ckv, g_q, g_k, g_oa, g_oc, g_pl)]

    early = _assemble_early(*_gather_weights([w_in[0], w_uq[0], w_ukv[0], conv_w[0]]))

    loss, gx, (h_t, dproj), wgrads, ggrads, dconv = _local_step(
        x[0], p[0, 0], positions.reshape(T, 1), loss_target[0], gains, early, [w_o[0], w_pl[0], w_plg[0]], None, 256, 512)

    others_cm = _split_others(*wgrads)
    small_parts = [a.reshape(-1, LANES) for a in (*ggrads, loss, dconv)]
    small_rows = [a.shape[0] for a in small_parts]
    tile_rows = [-(-r // 8) * 8 for r in small_rows]
    tile_rows[-1] += -sum(tile_rows) % 16
    small = jnp.concatenate([jnp.pad(a, ((0, t - r), (0, 0))) for a, r, t in zip(small_parts, small_rows, tile_rows)])
    from_sibling, (small_sibling,) = _swap_halves(others_cm, [small], "pair_grads")
    chip_parts, chip_small = _add_pair(others_cm, from_sibling, small, small_sibling, c)
    dw_in_e, exchanged = _matmul_acc(h_t, dproj, min(4096, T), 512, [*chip_parts, chip_small])
    w_in_cm = _split_w_in(dw_in_e)
    (w_in_sibling,), _ = _swap_halves([w_in_cm], [], "pair_w_in")
    by_chip = [_scatter_to_chips(w_in_cm, w_in_sibling), *exchanged[:-1]]
    halves, small_total = _add_chips(by_chip, exchanged[-1])
    other_halves = _share_halves(halves)

    gg, off = [], 0
    for rows, tiled in zip(small_rows, tile_rows):
        gg.append(small_total[off:off + rows].reshape(1, -1))
        off += tiled
    loss_out = gg[8][0, 0]
    conv_total = gg[9].reshape(3, CONV_W)
    conv_g = lax.dynamic_slice(conv_total, (0, chip * (CONV_W // N_CHIPS)), (3, CONV_W // N_CHIPS))
    g_by_name = dict(g_in=gg[0], g_cq=gg[1], g_ckv=gg[2], g_q=gg[3][:, :QK_DIM], g_k=gg[4][:, :QK_DIM], conv_w=conv_g,
                     g_oa=gg[5], g_oc=gg[6], g_pl=gg[7])
    half_by_name = dict(zip(("w_in", "w_uq", "w_ukv", "w_o", "w_pl", "w_plg"), zip(halves, other_halves)))
    weights = dict(g_in=g_in, w_in=w_in, g_cq=g_cq, w_uq=w_uq, g_ckv=g_ckv, w_ukv=w_ukv, g_q=g_q, g_k=g_k,
                   conv_w=conv_w, g_oa=g_oa, g_oc=g_oc, w_o=w_o, w_pl=w_pl, w_plg=w_plg, g_pl=g_pl)
    ms = dict(g_in=m_g_in, w_in=m_w_in, g_cq=m_g_cq, w_uq=m_w_uq, g_ckv=m_g_ckv, w_ukv=m_w_ukv, g_q=m_g_q, g_k=m_g_k,
              conv_w=m_conv_w, g_oa=m_g_oa, g_oc=m_g_oc, w_o=m_w_o, w_pl=m_w_pl, w_plg=m_w_plg, g_pl=m_g_pl)
    vs = dict(g_in=v_g_in, w_in=v_w_in, g_cq=v_g_cq, w_uq=v_w_uq, g_ckv=v_g_ckv, w_ukv=v_w_ukv, g_q=v_g_q, g_k=v_g_k,
              conv_w=v_conv_w, g_oa=v_g_oa, g_oc=v_g_oc, w_o=v_w_o, w_pl=v_w_pl, w_plg=v_w_plg, g_pl=v_g_pl)
    names = list(weights)
    flat = lambda a: a.reshape(-1, a.shape[-1])
    small_names = list(g_by_name)
    small_out = _adamw_small([flat(weights[n]) for n in small_names], [flat(g_by_name[n]) for n in small_names],
                             [flat(ms[n]) for n in small_names], [flat(vs[n]) for n in small_names])
    results = {n: (flat(g_by_name[n]), *(out[i] for out in small_out)) for i, n in enumerate(small_names)}
    for n in half_by_name:
        results[n] = _adamw_halves(flat(weights[n]), *half_by_name[n], flat(ms[n]), flat(vs[n]), c, "adamw_" + n)
    per_kind = [[results[n][kind].reshape(weights[n].shape) for n in names] for kind in range(4)]
    return (loss_out, gx.reshape(x.shape), *per_kind[0], *per_kind[1], *per_kind[2], *per_kind[3])
```

```python
import math

import jax
import jax.numpy as jnp
from jax import lax
from jax.experimental import pallas as pl
from jax.experimental.pallas import tpu as pltpu

F32 = jnp.float32
BF16 = jnp.bfloat16

D_MODEL = 1024
N_HEADS = 4
NOPE = 128
ROPE = 64
V_DIM = 128
QK_DIM = NOPE + ROPE
HEAD_PAD = 256
Q_LORA = 256
KV_LORA = 128
ATTN_W = 512
CONV_W = 512
PLE = 256
IN_TOTAL = 3008
PROJ_EXT = 3072
ROPE_THETA = 10000.0
EPS = 1e-6
SCALE = 1.0 / math.sqrt(QK_DIM)
LOG2E = math.log2(math.e)
EXP2_SCALE = SCALE * LOG2E
NEG = -1e30
SOFTMAX_ROWS = 32
SUB_TILE = 256

LR, B1, B2, ADAM_EPS, WD, STEP = 0.001, 0.9, 0.999, 1e-08, 0.01, 10

N_CHIPS = 4
LANES = 128
VMEM_LIMIT = 56 * 1024 * 1024
MESH = pl.DeviceIdType.MESH


def _params(**kw):
    return pltpu.CompilerParams(vmem_limit_bytes=VMEM_LIMIT, **kw)


def _inv_rms(x, n):
    return lax.rsqrt(jnp.sum(x * x, axis=-1, keepdims=True) / n + EPS)


def _lane_sum(a):
    folded = a[:, 0:LANES]
    for c0 in range(LANES, a.shape[1], LANES):
        folded = folded + a[:, c0:c0 + LANES]
    head = folded.astype(BF16)
    tail = (folded - head.astype(F32)).astype(BF16)
    return _dot(jnp.concatenate([head, tail], axis=1), jnp.ones((2 * LANES, LANES), BF16))


def _inv_rms_mxu(x):
    return lax.rsqrt(_lane_sum(x * x) / x.shape[1] + EPS)


def _rep(r, width):
    return r if width == LANES else jnp.tile(r, (1, width // LANES))


def _sigmoid(z):
    return jax.nn.sigmoid(z)


def _swap_rope_halves(b):
    lane = lax.broadcasted_iota(jnp.int32, b.shape, 1)
    swapped = jnp.where(lane < 32, pltpu.roll(b, 96, 1), pltpu.roll(b, 32, 1))
    return jnp.where(lane < ROPE, swapped, 0.0)


def _dot(a, b):
    return jnp.dot(a, b, preferred_element_type=F32)


def _dot_nt(a, b):
    return lax.dot_general(a, b, (((1,), (1,)), ((), ())), preferred_element_type=F32)


def _dot_tn(a, b):
    return lax.dot_general(a, b, (((0,), (0,)), ((), ())), preferred_element_type=F32)


def _colsum(a):
    return jnp.sum(a, axis=0, keepdims=True)


def _full(shape):
    return pl.BlockSpec(shape, lambda *_: (0,) * len(shape))


def _round_robin(chains, width):
    waiting, active = list(chains), []
    while waiting or active:
        while waiting and len(active) < width:
            active.append(waiting.pop(0))
        for chain in list(active):
            if next(chain, _DONE) is _DONE:
                active.remove(chain)


_DONE = object()


def _rope_tables(pos_ref, invf_ref, sgn_ref):
    ang = pos_ref[...].astype(F32) * invf_ref[...]
    return jnp.cos(ang), jnp.sin(ang) * sgn_ref[...]


def _fwd_proj(x, pos, g_in, w_in, g_cq, w_uq, g_ckv, w_ukv, gq, gk, invf, sgn, late_shards, tm):
    T = x.shape[0]
    nt = T // tm
    n_late = len(late_shards)
    ts = min(SUB_TILE, tm)

    def body(x_ref, pos_ref, g_in_ref, w_in_ref, g_cq_ref, w_uq_ref, g_ckv_ref, w_ukv_ref, gq_ref, gk_ref,
             invf_ref, sgn_ref, *rest):
        late_in, (proj_ref, q_ref, k_ref, v_ref) = rest[:n_late], rest[n_late:n_late + 4]
        late_out, late_scratch = rest[n_late + 4:2 * n_late + 4], rest[2 * n_late + 4:]
        i = pl.program_id(0)
        if n_late:
            start, forward, drain = _gather_steps([s.shape for s in late_shards], late_in, late_out,
                                                  late_scratch[:n_late], *late_scratch[n_late:])
            pl.when(i == 0)(start)
            pl.when(i == nt // 2)(forward)

        for r0 in range(0, tm, ts):
            rows = slice(r0, r0 + ts)
            xv = x_ref[rows, :]
            h = (xv * _rep(_inv_rms_mxu(xv), D_MODEL) * g_in_ref[...]).astype(BF16)
            lat = _dot(h, w_in_ref[:, 0:512])
            proj_ref[rows, 0:512] = lat
            c_q = lat[:, 0:Q_LORA]
            cqn = (c_q * _rep(_inv_rms_mxu(c_q), Q_LORA) * g_cq_ref[...]).astype(BF16)
            c_kv = lat[:, Q_LORA:Q_LORA + KV_LORA]
            ckvn = (c_kv * _inv_rms_mxu(c_kv) * g_ckv_ref[...]).astype(BF16)
            kpe = lat[:, 384:512]
            kpe_sq = kpe * kpe
            cos_b, sin_b = _rope_tables(pos_ref.at[rows, :], invf_ref, sgn_ref)
            gq_a, gq_b = gq_ref[:, 0:NOPE], gq_ref[:, NOPE:HEAD_PAD]
            gk_a, gk_b = gk_ref[:, 0:NOPE], gk_ref[:, NOPE:HEAD_PAD]

            def projections(rows=rows, h=h):
                for c0 in range(512, PROJ_EXT, 512):
                    proj_ref[rows, c0:c0 + 512] = _dot(h, w_in_ref[:, c0:c0 + 512])
                    yield

            def queries(hd, rows=rows, cqn=cqn, cos_b=cos_b, sin_b=sin_b, gq_a=gq_a, gq_b=gq_b):
                qh = _dot(cqn, w_uq_ref[:, hd * HEAD_PAD:(hd + 1) * HEAD_PAD])
                yield
                a, b = qh[:, 0:NOPE], qh[:, NOPE:HEAD_PAD]
                r = lax.rsqrt(_lane_sum(a * a + b * b) / QK_DIM + EPS)
                yield
                bn = b * r * gq_b
                q_ref[hd, rows, 0:NOPE] = (a * r * gq_a).astype(BF16)
                q_ref[hd, rows, NOPE:HEAD_PAD] = (bn * cos_b + _swap_rope_halves(bn) * sin_b).astype(BF16)
                yield

            def keys(hd, rows=rows, ckvn=ckvn, kpe=kpe, kpe_sq=kpe_sq, cos_b=cos_b, sin_b=sin_b, gk_a=gk_a, gk_b=gk_b):
                kvh = _dot(ckvn, w_ukv_ref[:, hd * HEAD_PAD:(hd + 1) * HEAD_PAD])
                yield
                ka = kvh[:, 0:NOPE]
                rk = lax.rsqrt(_lane_sum(ka * ka + kpe_sq) / QK_DIM + EPS)
                yield
                kbn = kpe * rk * gk_b
                k_ref[hd, rows, 0:NOPE] = (ka * rk * gk_a).astype(BF16)
                k_ref[hd, rows, NOPE:HEAD_PAD] = (kbn * cos_b + _swap_rope_halves(kbn) * sin_b).astype(BF16)
                v_ref[hd, rows, 0:V_DIM] = kvh[:, NOPE:HEAD_PAD].astype(BF16)
                v_ref[hd, rows, V_DIM:2 * V_DIM] = jnp.ones((ts, V_DIM), BF16)
                yield

            chains = [projections()]
            for hd in range(N_HEADS):
                chains += [queries(hd), keys(hd)]
            _round_robin(chains, 4)

        if n_late:
            pl.when(i == nt - 1)(drain)

    row = lambda i: (i, 0)
    head_rows = lambda i: (0, i, 0)
    outs = pl.pallas_call(
        body, name="fwd_proj", grid=(nt,),
        in_specs=[pl.BlockSpec((tm, D_MODEL), row), pl.BlockSpec((tm, 1), row), _full((1, D_MODEL)),
                  _full((D_MODEL, PROJ_EXT)), _full((1, Q_LORA)), _full((Q_LORA, N_HEADS * HEAD_PAD)),
                  _full((1, KV_LORA)), _full((KV_LORA, N_HEADS * HEAD_PAD)), _full((1, HEAD_PAD)), _full((1, HEAD_PAD)),
                  _full((1, LANES)), _full((1, LANES))] + [_full(s.shape) for s in late_shards],
        out_specs=[pl.BlockSpec((tm, PROJ_EXT), row), pl.BlockSpec((N_HEADS, tm, HEAD_PAD), head_rows),
                   pl.BlockSpec((N_HEADS, tm, HEAD_PAD), head_rows), pl.BlockSpec((N_HEADS, tm, 2 * V_DIM), head_rows)]
                  + [_ANY] * n_late,
        out_shape=[jax.ShapeDtypeStruct((T, PROJ_EXT), F32), jax.ShapeDtypeStruct((N_HEADS, T, HEAD_PAD), BF16),
                   jax.ShapeDtypeStruct((N_HEADS, T, HEAD_PAD), BF16), jax.ShapeDtypeStruct((N_HEADS, T, 2 * V_DIM), BF16)]
                  + _gathered_shapes(late_shards),
        scratch_shapes=_gather_scratch(late_shards) if n_late else [],
        compiler_params=_params(dimension_semantics=("arbitrary",)),
    )(x, pos, g_in, w_in, g_cq, w_uq, g_ckv, w_ukv, gq, gk, invf, sgn, *late_shards)
    return outs[:4], outs[4:]


def _chunk_pipeline(n_loop, lag, matmuls, pointwise, accumulate, last):
    slots = lag + 1

    def iteration(t, slot):
        matmuls(jnp.minimum(t + lag, n_loop), (slot + lag) % slots)
        accumulate(jnp.maximum(t - lag, 0), (slot + 1) % slots)
        pointwise(t, slot, False)

    def finish(slot):
        for back in range(lag, 0, -1):
            accumulate(jnp.maximum(n_loop - back, 0), (slot - back) % slots)
        pointwise(n_loop, slot, True)
        accumulate(n_loop, slot)
        last()

    for u in range(lag):
        matmuls(jnp.minimum(u, n_loop), u)

    def unrolled(tt, carry):
        for slot in range(slots):
            iteration(slots * tt + slot, slot)
        return carry

    lax.fori_loop(0, n_loop // slots, unrolled, 0)
    rest = lax.rem(n_loop, slots)
    t0 = n_loop - rest

    for r in range(slots):
        @pl.when(rest == r)
        def _():
            for slot in range(r):
                iteration(t0 + slot, slot)
            finish(r)


def _attn_fwd(q, k, v, tq):
    T = q.shape[1]
    tk = tq
    rc = min(SOFTMAX_ROWS, tq)

    def body(q_ref, k_ref, v_ref, o_ref, lse_ref, s0, s1, s2, p0, p1, p2, a0, a1, a2, m_ref, acc_ref):
        qi = pl.program_id(1)
        s_buf, p_buf, a_buf = (s0, s1, s2), (p0, p1, p2), (a0, a1, a2)

        def scores(t, slot):
            ks = pl.multiple_of(t * tk, tk)
            s_buf[slot][...] = _dot_nt(q_ref[0], k_ref[0, pl.ds(ks, tk), :])

        def values(t, slot):
            ks = pl.multiple_of(t * tk, tk)
            acc_ref[...] = acc_ref[...] * a_buf[slot][...] + _dot(p_buf[slot][...], v_ref[0, pl.ds(ks, tk), :])

        def softmax(t, slot, masked):
            s_all = s_buf[slot][...]
            if masked:
                row = lax.broadcasted_iota(jnp.int32, (tq, tk), 0)
                col = lax.broadcasted_iota(jnp.int32, (tq, tk), 1)
                s_all = jnp.where(col <= row, s_all, NEG)
                s_buf[slot][...] = s_all
            m_old = m_ref[...]
            m_new = jnp.maximum(m_old, jnp.max(s_all, axis=1, keepdims=True))
            a_buf[slot][...] = jnp.exp2((m_old - m_new) * EXP2_SCALE)
            m_ref[...] = m_new
            for r0 in range(0, tq, rc):
                s = s_buf[slot][r0:r0 + rc, :]
                p_buf[slot][r0:r0 + rc, :] = jnp.exp2((s - m_new[r0:r0 + rc, :]) * EXP2_SCALE).astype(BF16)

        def last():
            l = acc_ref[:, V_DIM:2 * V_DIM]
            o_ref[...] = acc_ref[:, 0:V_DIM] / l
            lse_ref[0] = (m_ref[...] * SCALE + jnp.log(l)).T[0:1, :]

        m_ref[...] = jnp.full_like(m_ref, NEG)
        acc_ref[...] = jnp.zeros_like(acc_ref)
        for p_late, a_late in ((p1, a1), (p2, a2)):
            p_late[...] = jnp.zeros_like(p_late)
            a_late[...] = jnp.ones_like(a_late)
        _chunk_pipeline(qi, 2, scores, softmax, values, last)

    return pl.pallas_call(
        body, name="attn_fwd", grid=(N_HEADS, T // tq),
        in_specs=[pl.BlockSpec((1, tq, HEAD_PAD), lambda h, i: (h, i, 0)),
                  pl.BlockSpec((1, T, HEAD_PAD), lambda h, i: (h, 0, 0)),
                  pl.BlockSpec((1, T, 2 * V_DIM), lambda h, i: (h, 0, 0))],
        out_specs=[pl.BlockSpec((tq, V_DIM), lambda h, i: (i, h)),
                   pl.BlockSpec((1, 1, tq), lambda h, i: (h, 0, i))],
        out_shape=[jax.ShapeDtypeStruct((T, ATTN_W), F32), jax.ShapeDtypeStruct((N_HEADS, 1, T), F32)],
        scratch_shapes=[pltpu.VMEM((tq, tk), F32)] * 3 + [pltpu.VMEM((tq, tk), BF16)] * 3
                       + [pltpu.VMEM((tq, 1), F32)] * 4 + [pltpu.VMEM((tq, 2 * V_DIM), F32)],
        compiler_params=_params(dimension_semantics=("arbitrary", "arbitrary")),
    )(q, k, v)


def _tail(x, o, proj, p, tgt, g_oa, g_oc, g_pl, conv_w, w_o, w_pl, w_plg, tm):
    T = x.shape[0]
    nt = T // tm

    def body(x_ref, o_ref, za_ref, cb_ref, cc_ref, cx_ref, zc_ref, cch_ref, cxh_ref, p_ref, tgt_ref,
             g_oa_ref, g_oc_ref, g_pl_ref, cw_ref, w_o_ref, w_pl_ref, w_plg_ref,
             dx1_ref, do_ref, delta_ref, dtail_ref, du_ref,
             dw_o_ref, dw_pl_ref, dw_plg_ref, dg_oa_ref, dg_oc_ref, dg_pl_ref, dcw_ref, loss_ref):
        i = pl.program_id(0)

        @pl.when(i == 0)
        def _():
            for r in (dw_o_ref, dw_pl_ref, dw_plg_ref, dg_oa_ref, dg_oc_ref, dg_pl_ref, dcw_ref, loss_ref):
                r[...] = jnp.zeros_like(r)

        g_oa, g_oc, g_pl = g_oa_ref[...], g_oc_ref[...], g_pl_ref[...]
        w0, w1, w2 = cw_ref[0:1, :], cw_ref[1:2, :], cw_ref[2:3, :]

        xv, ov, za, cb, zc = x_ref[...], o_ref[...], za_ref[...], cb_ref[...], zc_ref[...]
        pb = p_ref[...].astype(BF16)
        pp = _dot(pb, w_pl_ref[...])

        sa = _sigmoid(za)
        silu_a = za * sa
        ga = ov * silu_a
        ra = _inv_rms(ga, ATTN_W)
        xa = ga * ra
        ya = (xa * g_oa).astype(BF16)
        x1_a = _dot(ya, w_o_ref[0:ATTN_W, :])
        v = cc_ref[...] * cx_ref[...]
        not_first = jnp.where(i > 0, 1.0, 0.0)
        hv6 = cch_ref[6:7, :] * cxh_ref[6:7, :] * not_first
        hv7 = cch_ref[7:8, :] * cxh_ref[7:8, :] * not_first
        row = lax.broadcasted_iota(jnp.int32, v.shape, 0)
        v1 = jnp.where(row == 0, hv7, pltpu.roll(v, 1, 0))
        v2 = jnp.where(row == 0, hv6, jnp.where(row == 1, hv7, pltpu.roll(v, 2, 0)))
        u = w0 * v2 + w1 * v1 + w2 * v
        sc = _sigmoid(zc)
        silu_c = zc * sc
        gc = cb * u * silu_c
        rc = _inv_rms(gc, CONV_W)
        xc = gc * rc
        yc = (xc * g_oc).astype(BF16)
        x1 = xv + (x1_a + _dot(yc, w_o_ref[ATTN_W:D_MODEL, :]))
        r1 = _inv_rms(x1, D_MODEL)
        xh1 = x1 * r1
        n1 = (xh1 * g_pl).astype(BF16)
        gate = _sigmoid(_dot(n1, w_plg_ref[...]))
        err = x1 + gate * pp - tgt_ref[...]
        loss_ref[...] += 0.5 * jnp.sum(err * err) / D_MODEL
        dy = err / D_MODEL

        dpp = (dy * gate).astype(BF16)
        da = (dy * pp * gate * (1.0 - gate)).astype(BF16)
        dn1 = _dot_nt(da, w_plg_ref[...])
        dw_pl_ref[...] += _dot_tn(pb, dpp)
        dw_plg_ref[...] += _dot_tn(n1, da)
        dg_pl_ref[...] += _colsum(dn1 * xh1)
        dxh = dn1 * g_pl
        dx1 = dy + r1 * (dxh - xh1 * (jnp.sum(dxh * xh1, axis=-1, keepdims=True) / D_MODEL))
        dx1_ref[...] = dx1
        dx1b = dx1.astype(BF16)
        dya = _dot_nt(dx1b, w_o_ref[0:ATTN_W, :])
        dyc = _dot_nt(dx1b, w_o_ref[ATTN_W:D_MODEL, :])

        dw_o_ref[0:ATTN_W, :] += _dot_tn(ya, dx1b)
        dg_oa_ref[...] += _colsum(dya * xa)
        dxa = dya * g_oa
        dga = ra * (dxa - xa * (jnp.sum(dxa * xa, axis=-1, keepdims=True) / ATTN_W))
        do = (dga * silu_a).astype(BF16)
        do_ref[...] = do
        dof = do.astype(F32) * ov
        for hd in range(N_HEADS):
            delta_ref[hd] = _lane_sum(dof[:, hd * V_DIM:(hd + 1) * V_DIM]).T[0:1, :]
        dtail_ref[:, 0:512] = (dga * ov * (sa * (1.0 + za * (1.0 - sa)))).astype(BF16)

        dw_o_ref[ATTN_W:D_MODEL, :] += _dot_tn(yc, dx1b)
        dg_oc_ref[...] += _colsum(dyc * xc)
        dxc = dyc * g_oc
        dgc = rc * (dxc - xc * (jnp.sum(dxc * xc, axis=-1, keepdims=True) / CONV_W))
        dtail_ref[:, 512:1024] = (dgc * u * silu_c).astype(BF16)
        du = dgc * cb * silu_c
        du_ref[...] = du
        dtail_ref[:, 1024:1536] = (dgc * cb * u * (sc * (1.0 + zc * (1.0 - sc)))).astype(BF16)
        dcw_ref[0:1, :] += _colsum(du * v2)
        dcw_ref[1:2, :] += _colsum(du * v1)
        dcw_ref[2:3, :] += _colsum(du * v)

    row = lambda i: (i, 0)
    col = lambda c: (lambda i: (i, c))
    halo = lambda c: (lambda i: (jnp.maximum(i * (tm // 8) - 1, 0), c))
    in_specs = [pl.BlockSpec((tm, D_MODEL), row), pl.BlockSpec((tm, ATTN_W), row)]
    in_specs += [pl.BlockSpec((tm, 512), col(c)) for c in (1, 2, 3, 4, 5)]
    in_specs += [pl.BlockSpec((8, 512), halo(3)), pl.BlockSpec((8, 512), halo(4))]
    in_specs += [pl.BlockSpec((tm, PLE), row), pl.BlockSpec((tm, D_MODEL), row),
                 _full((1, ATTN_W)), _full((1, CONV_W)), _full((1, D_MODEL)), _full((3, CONV_W)),
                 _full((D_MODEL, D_MODEL)), _full((PLE, D_MODEL)), _full((D_MODEL, D_MODEL))]
    out_specs = [pl.BlockSpec((tm, D_MODEL), row), pl.BlockSpec((tm, ATTN_W), row),
                 pl.BlockSpec((N_HEADS, 1, tm), lambda i: (0, 0, i)), pl.BlockSpec((tm, 1536), row),
                 pl.BlockSpec((tm, CONV_W), row),
                 _full((D_MODEL, D_MODEL)), _full((PLE, D_MODEL)), _full((D_MODEL, D_MODEL)),
                 _full((1, ATTN_W)), _full((1, CONV_W)), _full((1, D_MODEL)), _full((3, CONV_W)), _full((1, LANES))]
    out_shape = [jax.ShapeDtypeStruct((T, D_MODEL), F32), jax.ShapeDtypeStruct((T, ATTN_W), BF16),
                 jax.ShapeDtypeStruct((N_HEADS, 1, T), F32), jax.ShapeDtypeStruct((T, 1536), BF16),
                 jax.ShapeDtypeStruct((T, CONV_W), F32),
                 jax.ShapeDtypeStruct((D_MODEL, D_MODEL), F32), jax.ShapeDtypeStruct((PLE, D_MODEL), F32),
                 jax.ShapeDtypeStruct((D_MODEL, D_MODEL), F32),
                 jax.ShapeDtypeStruct((1, ATTN_W), F32), jax.ShapeDtypeStruct((1, CONV_W), F32),
                 jax.ShapeDtypeStruct((1, D_MODEL), F32), jax.ShapeDtypeStruct((3, CONV_W), F32),
                 jax.ShapeDtypeStruct((1, LANES), F32)]
    return pl.pallas_call(
        body, name="tail", grid=(nt,), in_specs=in_specs, out_specs=out_specs, out_shape=out_shape,
        compiler_params=_params(dimension_semantics=("arbitrary",)),
    )(x, o, proj, proj, proj, proj, proj, proj, proj, p, tgt, g_oa, g_oc, g_pl, conv_w, w_o, w_pl, w_plg)


def _attn_bwd(q, k, v, do, lse_row, delta_row, tk):
    T = q.shape[1]
    tq = tk
    nq = T // tq
    rc = min(SOFTMAX_ROWS, tk)

    def body(q_ref, k_ref, v_ref, do_ref, lse_ref, dl_ref, dq_ref, dk_ref, dv_ref,
             s0, s1, d0, d1, p0, p1, g0, g1, dk_acc, dv_acc):
        kj = pl.program_id(1)
        s_buf, dp_buf, p_buf, g_buf = (s0, s1), (d0, d1), (p0, p1), (g0, g1)

        @pl.when(kj == 0)
        def _():
            dq_ref[...] = jnp.zeros_like(dq_ref)

        def q_start(t):
            return pl.multiple_of((nq - 1 - t) * tq, tq)

        def matmuls(t, slot):
            qs = q_start(t)
            s_buf[slot][...] = _dot_nt(k_ref[0], q_ref[0, pl.ds(qs, tq), :])
            dp_buf[slot][...] = _dot_nt(v_ref[0], do_ref[pl.ds(qs, tq), :])

        def pointwise(t, slot, masked):
            qs = q_start(t)
            lse2 = lse_ref[0, :, pl.ds(qs, tq)] * LOG2E
            dl = dl_ref[0, :, pl.ds(qs, tq)]
            for r0 in range(0, tk, rc):
                st = s_buf[slot][r0:r0 + rc, :]
                if masked:
                    row = lax.broadcasted_iota(jnp.int32, (rc, tq), 0)
                    col = lax.broadcasted_iota(jnp.int32, (rc, tq), 1)
                    st = jnp.where(row + r0 <= col, st, NEG)
                pt = jnp.exp2(st * EXP2_SCALE - lse2)
                p_buf[slot][r0:r0 + rc, :] = pt.astype(BF16)
                g_buf[slot][r0:r0 + rc, :] = (pt * (dp_buf[slot][r0:r0 + rc, :] - dl) * SCALE).astype(BF16)

        def accumulate(t, slot):
            qs = q_start(t)
            dv_acc[...] += _dot(p_buf[slot][...], do_ref[pl.ds(qs, tq), :])
            dk_acc[...] += _dot(g_buf[slot][...], q_ref[0, pl.ds(qs, tq), :])
            dq_ref[0, pl.ds(qs, tq), :] += _dot_tn(g_buf[slot][...], k_ref[0])

        def last():
            dk_ref[0] = dk_acc[...]
            dv_ref[0] = dv_acc[...]

        dk_acc[...] = jnp.zeros_like(dk_acc)
        dv_acc[...] = jnp.zeros_like(dv_acc)
        for late in (p1, g1):
            late[...] = jnp.zeros_like(late)
        _chunk_pipeline(nq - 1 - kj, 1, matmuls, pointwise, accumulate, last)

    return pl.pallas_call(
        body, name="attn_bwd", grid=(N_HEADS, T // tk),
        in_specs=[pl.BlockSpec((1, T, HEAD_PAD), lambda h, j: (h, 0, 0)),
                  pl.BlockSpec((1, tk, HEAD_PAD), lambda h, j: (h, j, 0)),
                  pl.BlockSpec((1, tk, V_DIM), lambda h, j: (h, j, 0)),
                  pl.BlockSpec((T, V_DIM), lambda h, j: (0, h)),
                  pl.BlockSpec((1, 1, T), lambda h, j: (h, 0, 0)),
                  pl.BlockSpec((1, 1, T), lambda h, j: (h, 0, 0))],
        out_specs=[pl.BlockSpec((1, T, HEAD_PAD), lambda h, j: (h, 0, 0)),
                   pl.BlockSpec((1, tk, HEAD_PAD), lambda h, j: (h, j, 0)),
                   pl.BlockSpec((1, tk, V_DIM), lambda h, j: (h, j, 0))],
        out_shape=[jax.ShapeDtypeStruct((N_HEADS, T, HEAD_PAD), F32), jax.ShapeDtypeStruct((N_HEADS, T, HEAD_PAD), F32),
                   jax.ShapeDtypeStruct((N_HEADS, T, V_DIM), F32)],
        scratch_shapes=[pltpu.VMEM((tk, tq), F32)] * 4 + [pltpu.VMEM((tk, tq), BF16)] * 4
                       + [pltpu.VMEM((tk, HEAD_PAD), F32), pltpu.VMEM((tk, V_DIM), F32)],
        compiler_params=_params(dimension_semantics=("arbitrary", "arbitrary")),
    )(q, k, v, do, lse_row, delta_row)


def _bwd_proj(x, dx1, pos, proj, dq, dk, dv, dtail, du, g_in, w_in, g_cq, w_uq, g_ckv, w_ukv, gq, gk, conv_w,
              invf, sgn, tm):
    T = x.shape[0]
    nt = T // tm

    ts = min(SUB_TILE, tm)

    def body(x_ref, dx1_ref, pos_ref, lat_ref, cc_ref, cx_ref, dq_ref, dk_ref, dv_ref, dtail_ref, du_ref, dun_ref, *rest):
        consts, (gx_ref, h_ref, dproj_ref), sums = rest[:11], rest[11:14], rest[14:]
        cw_ref = consts[8]
        i = pl.program_id(0)

        @pl.when(i == 0)
        def _():
            for r in sums:
                r[...] = jnp.zeros_like(r)

        du_v = du_ref[...]
        not_last = jnp.where(i < nt - 1, 1.0, 0.0)
        nx0 = dun_ref[0:1, :] * not_last
        nx1 = dun_ref[1:2, :] * not_last
        row = lax.broadcasted_iota(jnp.int32, du_v.shape, 0)
        du1 = jnp.where(row == tm - 1, nx0, pltpu.roll(du_v, tm - 1, 0))
        du2 = jnp.where(row == tm - 2, nx0, jnp.where(row == tm - 1, nx1, pltpu.roll(du_v, tm - 2, 0)))
        dvc = cw_ref[2:3, :] * du_v + cw_ref[1:2, :] * du1 + cw_ref[0:1, :] * du2
        dproj_ref[:, 1536:2048] = (dvc * cx_ref[...]).astype(BF16)
        dproj_ref[:, 2048:2560] = (dvc * cc_ref[...]).astype(BF16)

        for r0 in range(0, tm, ts):
            rows = slice(r0, r0 + ts)
            work(x_ref.at[rows, :], dx1_ref.at[rows, :], pos_ref.at[rows, :], lat_ref.at[rows, :],
                 dq_ref.at[:, rows, :], dk_ref.at[:, rows, :], dv_ref.at[:, rows, :], dtail_ref.at[rows, :], *consts,
                 gx_ref.at[rows, :], h_ref.at[:, rows], dproj_ref.at[rows, :], *sums)

    def work(x_ref, dx1_ref, pos_ref, lat_ref, dq_ref, dk_ref, dv_ref, dtail_ref,
             g_in_ref, w_in_ref, g_cq_ref, w_uq_ref, g_ckv_ref, w_ukv_ref, gq_ref, gk_ref, cw_ref, invf_ref, sgn_ref,
             gx_ref, h_ref, dproj_ref, dw_uq_ref, dw_ukv_ref, dg_in_ref, dg_cq_ref, dg_ckv_ref, dgq_ref, dgk_ref):
        xv = x_ref[...]
        r0 = _rep(_inv_rms_mxu(xv), D_MODEL)
        xh0 = xv * r0
        g_in = g_in_ref[...]
        h_ref[...] = (xh0 * g_in).astype(BF16).T

        c_q = lat_ref[:, 0:Q_LORA]
        rq = _rep(_inv_rms_mxu(c_q), Q_LORA)
        xq = c_q * rq
        g_cq = g_cq_ref[...]
        cqn = (xq * g_cq).astype(BF16)
        c_kv = lat_ref[:, Q_LORA:Q_LORA + KV_LORA]
        rkv = _inv_rms_mxu(c_kv)
        xkv = c_kv * rkv
        g_ckv = g_ckv_ref[...]
        ckvn = (xkv * g_ckv).astype(BF16)
        kpe = lat_ref[:, 384:512]
        kpe_sq = kpe * kpe
        cos_b, sin_b = _rope_tables(pos_ref, invf_ref, sgn_ref)
        gq_a, gq_b = gq_ref[:, 0:NOPE], gq_ref[:, NOPE:HEAD_PAD]
        gk_a, gk_b = gk_ref[:, 0:NOPE], gk_ref[:, NOPE:HEAD_PAD]

        dproj_ref[:, 512:1536] = dtail_ref[:, 0:1024]
        dproj_ref[:, 2560:3072] = dtail_ref[:, 1024:1536]

        def dh_part(c0):
            return _dot_nt(dproj_ref[:, c0:c0 + 512], w_in_ref[:, c0:c0 + 512])

        later_chunks = ((512,), (1024,), (1536, 2048), (2560,))
        dh = jnp.zeros((ts, D_MODEL), F32)
        acc = dict(dh=dh, dkpe=jnp.zeros((ts, LANES), F32), dcqn=jnp.zeros((ts, Q_LORA), F32),
                   dckvn=jnp.zeros((ts, KV_LORA), F32))

        def dh_chunks():
            for chunks in later_chunks:
                for chunk in chunks:
                    acc["dh"] = acc["dh"] + dh_part(chunk)
                    yield

        def queries(hd):
            c0 = hd * HEAD_PAD
            qh = _dot(cqn, w_uq_ref[:, c0:c0 + HEAD_PAD])
            yield
            a, b = qh[:, 0:NOPE], qh[:, NOPE:HEAD_PAD]
            r = lax.rsqrt(_lane_sum(a * a + b * b) / QK_DIM + EPS)
            yield
            xa, xb = a * r, b * r
            dan = dq_ref[hd, :, 0:NOPE]
            dbr = dq_ref[hd, :, NOPE:HEAD_PAD]
            dbn = dbr * cos_b + _swap_rope_halves(dbr * sin_b)
            yield
            dgq_ref[:, 0:NOPE] += _colsum(dan * xa)
            dgq_ref[:, NOPE:HEAD_PAD] += _colsum(dbn * xb)
            dxa, dxb = dan * gq_a, dbn * gq_b
            cq = _lane_sum(dxa * xa + dxb * xb) / QK_DIM
            yield
            dqh = jnp.concatenate([r * (dxa - xa * cq), r * (dxb - xb * cq)], axis=-1).astype(BF16)
            yield
            dw_uq_ref[:, c0:c0 + HEAD_PAD] += _dot_tn(cqn, dqh)
            yield
            acc["dcqn"] = acc["dcqn"] + _dot_nt(dqh, w_uq_ref[:, c0:c0 + HEAD_PAD])
            yield

        def keys(hd):
            c0 = hd * HEAD_PAD
            kvh = _dot(ckvn, w_ukv_ref[:, c0:c0 + HEAD_PAD])
            yield
            ka = kvh[:, 0:NOPE]
            rk = lax.rsqrt(_lane_sum(ka * ka + kpe_sq) / QK_DIM + EPS)
            yield
            xka, xkb = ka * rk, kpe * rk
            dkan = dk_ref[hd, :, 0:NOPE]
            dkbr = dk_ref[hd, :, NOPE:HEAD_PAD]
            dkbn = dkbr * cos_b + _swap_rope_halves(dkbr * sin_b)
            yield
            dgk_ref[:, 0:NOPE] += _colsum(dkan * xka)
            dgk_ref[:, NOPE:HEAD_PAD] += _colsum(dkbn * xkb)
            dxka, dxkb = dkan * gk_a, dkbn * gk_b
            ck = _lane_sum(dxka * xka + dxkb * xkb) / QK_DIM
            yield
            acc["dkpe"] = acc["dkpe"] + rk * (dxkb - xkb * ck)
            dkvh = jnp.concatenate([rk * (dxka - xka * ck), dv_ref[hd]], axis=-1).astype(BF16)
            yield
            dw_ukv_ref[:, c0:c0 + HEAD_PAD] += _dot_tn(ckvn, dkvh)
            yield
            acc["dckvn"] = acc["dckvn"] + _dot_nt(dkvh, w_ukv_ref[:, c0:c0 + HEAD_PAD])
            yield

        chains = [dh_chunks()]
        for hd in range(N_HEADS):
            chains += [queries(hd), keys(hd)]
        _round_robin(chains, 5)
        dh, dkpe, dcqn, dckvn = acc["dh"], acc["dkpe"], acc["dcqn"], acc["dckvn"]

        dg_cq_ref[...] += _colsum(dcqn * xq)
        dxq = dcqn * g_cq
        dproj_ref[:, 0:Q_LORA] = (rq * (dxq - xq * _rep(_lane_sum(dxq * xq) / Q_LORA, Q_LORA))).astype(BF16)
        dg_ckv_ref[...] += _colsum(dckvn * xkv)
        dxkv = dckvn * g_ckv
        dproj_ref[:, 256:384] = (rkv * (dxkv - xkv * (_lane_sum(dxkv * xkv) / KV_LORA))).astype(BF16)
        dproj_ref[:, 384:512] = dkpe.astype(BF16)
        dh = dh + dh_part(0)
        dg_in_ref[...] += _colsum(dh * xh0)
        dxh = dh * g_in
        gx_ref[...] = dx1_ref[...] + r0 * (dxh - xh0 * _rep(_lane_sum(dxh * xh0) / D_MODEL, D_MODEL))

    row = lambda i: (i, 0)
    col = lambda c: (lambda i: (i, c))
    head_rows = lambda i: (0, i, 0)
    nxt = lambda i: (jnp.minimum((i + 1) * (tm // 8), T // 8 - 1), 0)
    in_specs = [pl.BlockSpec((tm, D_MODEL), row), pl.BlockSpec((tm, D_MODEL), row), pl.BlockSpec((tm, 1), row),
                pl.BlockSpec((tm, 512), col(0)), pl.BlockSpec((tm, 512), col(3)), pl.BlockSpec((tm, 512), col(4)),
                pl.BlockSpec((N_HEADS, tm, HEAD_PAD), head_rows), pl.BlockSpec((N_HEADS, tm, HEAD_PAD), head_rows),
                pl.BlockSpec((N_HEADS, tm, V_DIM), head_rows), pl.BlockSpec((tm, 1536), row),
                pl.BlockSpec((tm, CONV_W), row), pl.BlockSpec((8, CONV_W), nxt),
                _full((1, D_MODEL)), _full((D_MODEL, PROJ_EXT)), _full((1, Q_LORA)), _full((Q_LORA, N_HEADS * HEAD_PAD)),
                _full((1, KV_LORA)), _full((KV_LORA, N_HEADS * HEAD_PAD)), _full((1, HEAD_PAD)), _full((1, HEAD_PAD)),
                _full((3, CONV_W)), _full((1, LANES)), _full((1, LANES))]
    out_specs = [pl.BlockSpec((tm, D_MODEL), row), pl.BlockSpec((D_MODEL, tm), lambda i: (0, i)),
                 pl.BlockSpec((tm, PROJ_EXT), row),
                 _full((Q_LORA, N_HEADS * HEAD_PAD)), _full((KV_LORA, N_HEADS * HEAD_PAD)),
                 _full((1, D_MODEL)), _full((1, Q_LORA)), _full((1, KV_LORA)), _full((1, HEAD_PAD)), _full((1, HEAD_PAD))]
    out_shape = [jax.ShapeDtypeStruct((T, D_MODEL), F32), jax.ShapeDtypeStruct((D_MODEL, T), BF16),
                 jax.ShapeDtypeStruct((T, PROJ_EXT), BF16),
                 jax.ShapeDtypeStruct((Q_LORA, N_HEADS * HEAD_PAD), F32), jax.ShapeDtypeStruct((KV_LORA, N_HEADS * HEAD_PAD), F32),
                 jax.ShapeDtypeStruct((1, D_MODEL), F32), jax.ShapeDtypeStruct((1, Q_LORA), F32),
                 jax.ShapeDtypeStruct((1, KV_LORA), F32), jax.ShapeDtypeStruct((1, HEAD_PAD), F32),
                 jax.ShapeDtypeStruct((1, HEAD_PAD), F32)]
    return pl.pallas_call(
        body, name="bwd_proj", grid=(nt,), in_specs=in_specs, out_specs=out_specs, out_shape=out_shape,
        compiler_params=_params(dimension_semantics=("arbitrary",)),
    )(x, dx1, pos, proj, proj, proj, dq, dk, dv, dtail, du, du, g_in, w_in, g_cq, w_uq, g_ckv, w_ukv, gq, gk, conv_w,
      invf, sgn)


def _matmul_acc(a, b, tt, tn, parts):
    M, T = a.shape
    N = b.shape[1]
    n = len(parts)
    grid = (N // tn, T // tt)

    def body(a_ref, b_ref, *rest):
        part_refs, o_ref, out_refs, sems = rest[:n], rest[n], rest[n + 1:2 * n + 1], rest[2 * n + 1:]
        j, t = pl.program_id(0), pl.program_id(1)
        if n:
            start, drain = _scatter_steps(part_refs, out_refs, *sems)
            pl.when(jnp.logical_and(j == 0, t == 0))(start)

        @pl.when(t == 0)
        def _():
            o_ref[...] = jnp.zeros_like(o_ref)

        o_ref[...] += _dot(a_ref[...], b_ref[...])
        if n:
            pl.when(jnp.logical_and(j == grid[0] - 1, t == grid[1] - 1))(drain)

    sems = [pltpu.SemaphoreType.DMA((3 * n,)), pltpu.SemaphoreType.DMA((3 * n,)), pltpu.SemaphoreType.DMA((n,))]
    outs = pl.pallas_call(
        body, name="dw_in", grid=grid,
        in_specs=[pl.BlockSpec((M, tt), lambda j, t: (0, t)), pl.BlockSpec((tt, tn), lambda j, t: (t, j))] + [_ANY] * n,
        out_specs=[pl.BlockSpec((M, tn), lambda j, t: (0, j))] + [_ANY] * n,
        out_shape=[jax.ShapeDtypeStruct((M, N), F32)] + _scattered_shapes(parts),
        scratch_shapes=sems if n else [],
        compiler_params=_params(dimension_semantics=("arbitrary", "arbitrary")),
    )(a, b, *parts)
    return outs[0], outs[1:]


def _add_chips(parts, small_parts):
    arrays = list(parts) + [small_parts]

    def body(*refs):
        ins, outs = refs[:len(arrays)], refs[len(arrays):]
        for a_ref, o_ref in zip(ins, outs):
            part = lambda k: a_ref[k].astype(F32)
            o_ref[...] = ((part(0) + part(1)) + part(2)) + part(3)

    in_specs, out_specs, out_shape = [], [], []
    for a in arrays:
        _, rows, cols = a.shape
        in_specs.append(pl.BlockSpec((N_CHIPS, rows // 2, cols), lambda i: (0, i, 0)))
        out_specs.append(pl.BlockSpec((rows // 2, cols), lambda i: (i, 0)))
        out_shape.append(jax.ShapeDtypeStruct((rows, cols), F32))
    outs = pl.pallas_call(body, name="add_chips", grid=(2,), in_specs=in_specs, out_specs=out_specs,
                          out_shape=out_shape, compiler_params=_params(dimension_semantics=("arbitrary",)))(*arrays)
    return outs[:-1], outs[-1]


def _adamw_small(ws, gs, ms, vs):
    n = len(ws)

    def body(*refs):
        for i in range(n):
            w_ref, g_ref, m_ref, v_ref = (refs[k * n + i] for k in range(4))
            d_ref, nm_ref, nv_ref = (refs[(4 + k) * n + i] for k in range(3))
            _adamw_math(g_ref[...], w_ref, m_ref, v_ref, d_ref, nm_ref, nv_ref)

    shapes = [jax.ShapeDtypeStruct(w.shape, F32) for w in ws]
    outs = pl.pallas_call(body, name="adamw_small", out_shape=shapes * 3)(*ws, *gs, *ms, *vs)
    return outs[:n], outs[n:2 * n], outs[2 * n:]


def _adamw_math(gv, w_ref, m_ref, v_ref, d_ref, nm_ref, nv_ref):
    nm = B1 * m_ref[...] + (1.0 - B1) * gv
    nv = B2 * v_ref[...] + (1.0 - B2) * (gv * gv)
    m_hat = nm / (1.0 - B1 ** STEP)
    v_hat = nv / (1.0 - B2 ** STEP)
    d_ref[...] = -LR * (m_hat / (jnp.sqrt(v_hat) + ADAM_EPS) + WD * w_ref[...])
    nm_ref[...] = nm
    nv_ref[...] = nv


def _adamw_halves(w, mine, other, m, v, c, name):
    hr, cols = mine.shape

    def body(c_ref, w_ref, mine_ref, other_ref, m_ref, v_ref, g_ref, d_ref, nm_ref, nv_ref):
        gv = jnp.where(pl.program_id(0) == c_ref[0], mine_ref[...], other_ref[...])
        g_ref[...] = gv
        _adamw_math(gv, w_ref, m_ref, v_ref, d_ref, nm_ref, nv_ref)

    half = pl.BlockSpec((hr, cols), lambda i, c_ref: (i, 0))
    whole = pl.BlockSpec((hr, cols), lambda i, c_ref: (0, 0))
    shp = jax.ShapeDtypeStruct(w.shape, F32)
    return pl.pallas_call(
        body, name=name, out_shape=[shp] * 4,
        grid_spec=pltpu.PrefetchScalarGridSpec(num_scalar_prefetch=1, grid=(2,), in_specs=[half, whole, whole, half, half],
                                               out_specs=[half] * 4),
        compiler_params=_params(dimension_semantics=("arbitrary",)),
    )(c.reshape(1), w, mine, other, m, v)


_ANY = pl.BlockSpec(memory_space=pl.ANY)


def _mesh_pos():
    return lax.axis_index("x"), lax.axis_index("y"), lax.axis_index("c")


def _other_chips(x, y):
    return [(1 - x, y), (x, 1 - y), (1 - x, 1 - y)]


def _remote(src, dst, send_sems, recv_sems, k, to):
    return pltpu.make_async_remote_copy(src_ref=src, dst_ref=dst, send_sem=send_sems.at[k], recv_sem=recv_sems.at[k],
                                        device_id=to, device_id_type=MESH)


def _gather_weights(shards):
    n = len(shards)

    def body(*refs):
        start, forward, drain = _gather_steps([s.shape for s in shards], refs[:n], refs[n:2 * n], refs[2 * n:3 * n],
                                              *refs[3 * n:])
        start()
        forward()
        drain()

    vmem = pl.BlockSpec(memory_space=pltpu.VMEM)
    return pl.pallas_call(
        body, name="gather_weights", in_specs=[vmem] * n, out_specs=[_ANY] * n,
        out_shape=_gathered_shapes(shards), scratch_shapes=_gather_scratch(shards), compiler_params=_params(),
    )(*shards)


def _gathered_shapes(shards):
    return [jax.ShapeDtypeStruct((N_CHIPS,) + s.shape, BF16) for s in shards]


def _gather_scratch(shards):
    n = len(shards)
    return ([pltpu.VMEM(s.shape, BF16) for s in shards]
            + [pltpu.SemaphoreType.DMA((6 * n,)), pltpu.SemaphoreType.DMA((6 * n,)), pltpu.SemaphoreType.DMA((n,))])


def _gather_steps(shapes, ins, outs, stage, send_sems, recv_sems, local_sems):
    n = len(shapes)
    halved = [s[0] % 32 == 0 for s in shapes]

    def part(i, ref, hc):
        if not halved[i]:
            return ref
        hr = shapes[i][0] // 2
        return ref.at[pl.ds(hc * hr, hr), :]

    def to_chip(i, j, x, y, c):
        cx, cy = _other_chips(x, y)[j]
        return _remote(part(i, stage[i], c), part(i, outs[i].at[2 * x + y], c), send_sems, recv_sems, 6 * i + j, (cx, cy, c))

    def to_sibling(i, j, x, y, c):
        cx, cy = _other_chips(x, y)[j]
        got = part(i, outs[i].at[2 * cx + cy], c)
        return _remote(got, got, send_sems, recv_sems, 6 * i + 3 + j, (x, y, 1 - c))

    def local(i, x, y):
        return pltpu.make_async_copy(stage[i], outs[i].at[2 * x + y], local_sems.at[i])

    def start():
        x, y, c = _mesh_pos()
        for i in range(n):
            stage[i][...] = ins[i][...].astype(BF16)
            local(i, x, y).start()
            for j in range(3):
                to_chip(i, j, x, y, c).start()

    def forward():
        x, y, c = _mesh_pos()
        for i in range(n):
            for j, (cx, cy) in enumerate(_other_chips(x, y)):
                got = part(i, outs[i].at[2 * cx + cy], c)
                _remote(got, got, send_sems, recv_sems, 6 * i + j, (cx, cy, c)).wait_recv()
                if halved[i]:
                    to_sibling(i, j, x, y, c).start()

    def drain():
        x, y, c = _mesh_pos()
        for i in range(n):
            for j, (cx, cy) in enumerate(_other_chips(x, y)):
                if halved[i]:
                    got = part(i, outs[i].at[2 * cx + cy], 1 - c)
                    _remote(got, got, send_sems, recv_sems, 6 * i + 3 + j, (x, y, 1 - c)).wait_recv()
                    to_sibling(i, j, x, y, c).wait_send()
                to_chip(i, j, x, y, c).wait_send()
            local(i, x, y).wait()

    return start, forward, drain


def _swap_halves(grads, whole, name):
    n, m = len(grads), len(grads) + len(whole)

    def body(*refs):
        ins, outs, send_sems, recv_sems = refs[:m], refs[m:2 * m], refs[2 * m], refs[2 * m + 1]
        x, y, c = _mesh_pos()
        cps = []
        for i in range(m):
            src = ins[i]
            if i < n:
                hr = grads[i].shape[1] // 2
                src = src.at[:, pl.ds((1 - c) * hr, hr), :]
            cp = _remote(src, outs[i], send_sems, recv_sems, i, (x, y, 1 - c))
            cp.start()
            cps.append(cp)
        for cp in cps:
            cp.wait()

    out_shape = [jax.ShapeDtypeStruct((g.shape[0], g.shape[1] // 2, g.shape[2]), F32) for g in grads]
    out_shape += [jax.ShapeDtypeStruct(w.shape, F32) for w in whole]
    outs = pl.pallas_call(
        body, name=name, in_specs=[_ANY] * m, out_specs=[_ANY] * m, out_shape=out_shape,
        scratch_shapes=[pltpu.SemaphoreType.DMA((m,)), pltpu.SemaphoreType.DMA((m,))],
    )(*grads, *whole)
    return outs[:n], outs[n:]


def _scattered_shapes(parts):
    return [jax.ShapeDtypeStruct(p.shape if p.ndim == 3 else (N_CHIPS,) + p.shape, p.dtype) for p in parts]


def _scatter_steps(ins, outs, send_sems, recv_sems, local_sems):
    n = len(ins)

    def src(i, k):
        return ins[i].at[k] if len(ins[i].shape) == 3 else ins[i]

    def sends(x, y, c):
        return [_remote(src(i, 2 * cx + cy), outs[i].at[2 * x + y], send_sems, recv_sems, 3 * i + j, (cx, cy, c))
                for i in range(n) for j, (cx, cy) in enumerate(_other_chips(x, y))]

    def local(i, x, y):
        return pltpu.make_async_copy(src(i, 2 * x + y), outs[i].at[2 * x + y], local_sems.at[i])

    def start():
        x, y, c = _mesh_pos()
        for i in range(n):
            local(i, x, y).start()
        for cp in sends(x, y, c):
            cp.start()

    def drain():
        x, y, c = _mesh_pos()
        for i in range(n):
            for j, (cx, cy) in enumerate(_other_chips(x, y)):
                got = outs[i].at[2 * cx + cy]
                _remote(got, got, send_sems, recv_sems, 3 * i + j, (cx, cy, c)).wait_recv()
        for cp in sends(x, y, c):
            cp.wait_send()
        for i in range(n):
            local(i, x, y).wait()

    return start, drain


def _add_pair(grads, from_sibling, small, small_sibling, c):
    n = len(grads)

    def body(c_ref, *refs):
        ins, outs = refs[:2 * n + 2], refs[2 * n + 2:]
        for i in range(n + 1):
            outs[i][...] = (ins[2 * i][...] + ins[2 * i + 1][...]).astype(outs[i].dtype)

    in_specs, out_specs, out_shape, args = [], [], [], []
    for g, r in zip(grads, from_sibling):
        _, hr, cols = r.shape
        in_specs += [pl.BlockSpec((1, hr, cols), lambda k, c_ref: (k, c_ref[0], 0)),
                     pl.BlockSpec((1, hr, cols), lambda k, c_ref: (k, 0, 0))]
        out_specs.append(pl.BlockSpec((1, hr, cols), lambda k, c_ref: (k, 0, 0)))
        out_shape.append(jax.ShapeDtypeStruct(r.shape, BF16))
        args += [g, r]
    whole = pl.BlockSpec(small.shape, lambda k, c_ref: (0, 0))
    in_specs += [whole, whole]
    out_specs.append(whole)
    out_shape.append(jax.ShapeDtypeStruct(small.shape, F32))
    outs = pl.pallas_call(
        body, name="add_pair", out_shape=out_shape,
        grid_spec=pltpu.PrefetchScalarGridSpec(num_scalar_prefetch=1, grid=(N_CHIPS,), in_specs=in_specs,
                                               out_specs=out_specs),
        compiler_params=_params(dimension_semantics=("arbitrary",)),
    )(c.reshape(1), *args, small, small_sibling)
    return outs[:n], outs[n]


def _scatter_to_chips(grad, from_sibling):
    hr = from_sibling.shape[1]

    def body(g_in, r_in, out, g_buf, r_buf, p_buf, load_sems, send_sems, recv_sems, local_sems):
        c = lax.axis_index("c")
        loads = (pltpu.make_async_copy(g_in.at[:, pl.ds(c * hr, hr), :], g_buf, load_sems.at[0]),
                 pltpu.make_async_copy(r_in, r_buf, load_sems.at[1]))
        for cp in loads:
            cp.start()
        for cp in loads:
            cp.wait()
        p_buf[...] = (g_buf[...] + r_buf[...]).astype(BF16)
        start, drain = _scatter_steps([p_buf], [out], send_sems, recv_sems, local_sems)
        start()
        drain()

    return pl.pallas_call(
        body, name="scatter_grads", in_specs=[_ANY] * 2, out_specs=_ANY,
        out_shape=jax.ShapeDtypeStruct(from_sibling.shape, BF16),
        scratch_shapes=[pltpu.VMEM(from_sibling.shape, F32)] * 2 + [pltpu.VMEM(from_sibling.shape, BF16)]
                       + [pltpu.SemaphoreType.DMA((2,)), pltpu.SemaphoreType.DMA((3,)), pltpu.SemaphoreType.DMA((3,)),
                          pltpu.SemaphoreType.DMA((1,))],
        compiler_params=_params(),
    )(grad, from_sibling)


def _share_halves(halves):
    n = len(halves)

    def body(*refs):
        ins, outs, send_sems, recv_sems = refs[:n], refs[n:2 * n], refs[2 * n], refs[2 * n + 1]
        x, y, c = _mesh_pos()
        cps = [_remote(ins[i], outs[i], send_sems, recv_sems, i, (x, y, 1 - c)) for i in range(n)]
        for cp in cps:
            cp.start()
        for cp in cps:
            cp.wait()

    return pl.pallas_call(
        body, name="share_halves", in_specs=[_ANY] * n, out_specs=[_ANY] * n,
        out_shape=[jax.ShapeDtypeStruct(h.shape, h.dtype) for h in halves],
        scratch_shapes=[pltpu.SemaphoreType.DMA((n,)), pltpu.SemaphoreType.DMA((n,))],
    )(*halves)


SHARD_COLS_IN = IN_TOTAL // N_CHIPS
KPE_END = Q_LORA + KV_LORA + ROPE


def _by_cols(a):
    return a.transpose(1, 0, 2).reshape(a.shape[1], N_CHIPS * a.shape[2])


def _assemble_early(c_in, c_uq, c_ukv, c_conv):
    w_in_e = jnp.concatenate([c_in[0][:, :KPE_END], jnp.zeros((D_MODEL, 64), BF16), c_in[0][:, KPE_END:],
                              c_in[1], c_in[2], c_in[3]], axis=1)
    w_uq_e = _by_cols(jnp.pad(c_uq, ((0, 0), (0, 0), (0, HEAD_PAD - QK_DIM))))
    return w_in_e, w_uq_e, _by_cols(c_ukv), _by_cols(c_conv).astype(F32)


def _assemble_late(c_o, c_pl, c_plg):
    return c_o.reshape(D_MODEL, D_MODEL), _by_cols(c_pl), c_plg.reshape(D_MODEL, D_MODEL)


def _split_w_in(dw_in_e):
    first = jnp.concatenate([dw_in_e[:, :KPE_END], dw_in_e[:, KPE_END + 64:SHARD_COLS_IN + 64]], axis=1)
    rest = [dw_in_e[:, SHARD_COLS_IN * k + 64:SHARD_COLS_IN * (k + 1) + 64] for k in range(1, N_CHIPS)]
    return jnp.stack([first] + rest)


def _split_others(dw_uq_e, dw_ukv, dw_o, dw_pl, dw_plg):
    chip_major = lambda a: a.reshape(a.shape[0], N_CHIPS, a.shape[1] // N_CHIPS).transpose(1, 0, 2)
    return [chip_major(dw_uq_e)[:, :, :QK_DIM], chip_major(dw_ukv), dw_o.reshape(N_CHIPS, D_MODEL // N_CHIPS, D_MODEL),
            chip_major(dw_pl), dw_plg.reshape(N_CHIPS, D_MODEL // N_CHIPS, D_MODEL)]


def _local_step(x, p, pos, tgt, gains, early, late_shards, late_gathered, tm, tq):
    w_in_e, w_uq_e, w_ukv, conv_w = early
    g_in, g_cq, g_ckv, g_q, g_k, g_oa, g_oc, g_pl = gains
    T = x.shape[0]
    zpad = lambda a, n: jnp.concatenate([a, jnp.zeros(a.shape[:-1] + (n,), a.dtype)], axis=-1)
    gq, gk = zpad(g_q, HEAD_PAD - QK_DIM), zpad(g_k, HEAD_PAD - QK_DIM)
    inv_freq = 1.0 / (ROPE_THETA ** (jnp.arange(0, ROPE, 2, dtype=F32) / ROPE))
    invf = jnp.concatenate([inv_freq, inv_freq, jnp.zeros((64,), F32)]).reshape(1, LANES)
    sgn = jnp.concatenate([-jnp.ones((32,), F32), jnp.ones((32,), F32), jnp.zeros((64,), F32)]).reshape(1, LANES)

    (proj, q, k, v), gathered = _fwd_proj(x, pos, g_in, w_in_e, g_cq, w_uq_e, g_ckv, w_ukv, gq, gk, invf, sgn,
                                          late_shards, min(2 * tm, T))
    w_o, w_pl, w_plg = _assemble_late(*(gathered if late_shards else late_gathered))
    o, lse = _attn_fwd(q, k, v, tq)
    (dx1, do, delta, dtail, du, dw_o, dw_pl, dw_plg, dg_oa, dg_oc, dg_pl, dconv, loss) = _tail(
        x, o, proj, p, tgt, g_oa, g_oc, g_pl, conv_w, w_o, w_pl, w_plg, tm)
    dq, dk, dv = _attn_bwd(q, k, v, do, lse, delta, tq)
    (gx, h, dproj, dw_uq_e, dw_ukv, dg_in, dg_cq, dg_ckv, dgq, dgk) = _bwd_proj(
        x, dx1, pos, proj, dq, dk, dv, dtail, du, g_in, w_in_e, g_cq, w_uq_e, g_ckv, w_ukv, gq, gk, conv_w, invf, sgn, tm)
    wgrads = (dw_uq_e, dw_ukv, dw_o, dw_pl, dw_plg)
    ggrads = (dg_in, dg_cq, dg_ckv, dgq, dgk, dg_oa, dg_oc, dg_pl)
    return loss, gx, (h, dproj), wgrads, ggrads, dconv


def kernel(x, p, positions, g_in, w_in, g_cq, w_uq, g_ckv, w_ukv, g_q, g_k, conv_w, g_oa, g_oc, w_o, w_pl, w_plg, g_pl, loss_target, m_g_in, m_w_in, m_g_cq, m_w_uq, m_g_ckv, m_w_ukv, m_g_q, m_g_k, m_conv_w, m_g_oa, m_g_oc, m_w_o, m_w_pl, m_w_plg, m_g_pl, v_g_in, v_w_in, v_g_cq, v_w_uq, v_g_ckv, v_w_ukv, v_g_q, v_g_k, v_conv_w, v_g_oa, v_g_oc, v_w_o, v_w_pl, v_w_plg, v_g_pl):
    T = x.shape[1]
    c = lax.axis_index("c")
    chip = 2 * lax.axis_index("x") + lax.axis_index("y")
    gains = [g.reshape(1, -1) for g in (g_in, g_cq, g_ckv, g_q, g_k, g_oa, g_oc, g_pl)]

    early = _assemble_early(*_gather_weights([w_in[0], w_uq[0], w_ukv[0], conv_w[0]]))

    loss, gx, (h_t, dproj), wgrads, ggrads, dconv = _local_step(
        x[0], p[0, 0], positions.reshape(T, 1), loss_target[0], gains, early, [w_o[0], w_pl[0], w_plg[0]], None, 256, 512)

    others_cm = _split_others(*wgrads)
    small_parts = [a.reshape(-1, LANES) for a in (*ggrads, loss, dconv)]
    small_rows = [a.shape[0] for a in small_parts]
    tile_rows = [-(-r // 8) * 8 for r in small_rows]
    tile_rows[-1] += -sum(tile_rows) % 16
    small = jnp.concatenate([jnp.pad(a, ((0, t - r), (0, 0))) for a, r, t in zip(small_parts, small_rows, tile_rows)])
    from_sibling, (small_sibling,) = _swap_halves(others_cm, [small], "pair_grads")
    chip_parts, chip_small = _add_pair(others_cm, from_sibling, small, small_sibling, c)
    dw_in_e, exchanged = _matmul_acc(h_t, dproj, min(4096, T), 512, [*chip_parts, chip_small])
    w_in_cm = _split_w_in(dw_in_e)
    (w_in_sibling,), _ = _swap_halves([w_in_cm], [], "pair_w_in")
    by_chip = [_scatter_to_chips(w_in_cm, w_in_sibling), *exchanged[:-1]]
    halves, small_total = _add_chips(by_chip, exchanged[-1])
    other_halves = _share_halves(halves)

    gg, off = [], 0
    for rows, tiled in zip(small_rows, tile_rows):
        gg.append(small_total[off:off + rows].reshape(1, -1))
        off += tiled
    loss_out = gg[8][0, 0]
    conv_total = gg[9].reshape(3, CONV_W)
    conv_g = lax.dynamic_slice(conv_total, (0, chip * (CONV_W // N_CHIPS)), (3, CONV_W // N_CHIPS))
    g_by_name = dict(g_in=gg[0], g_cq=gg[1], g_ckv=gg[2], g_q=gg[3][:, :QK_DIM], g_k=gg[4][:, :QK_DIM], conv_w=conv_g,
                     g_oa=gg[5], g_oc=gg[6], g_pl=gg[7])
    half_by_name = dict(zip(("w_in", "w_uq", "w_ukv", "w_o", "w_pl", "w_plg"), zip(halves, other_halves)))
    weights = dict(g_in=g_in, w_in=w_in, g_cq=g_cq, w_uq=w_uq, g_ckv=g_ckv, w_ukv=w_ukv, g_q=g_q, g_k=g_k,
                   conv_w=conv_w, g_oa=g_oa, g_oc=g_oc, w_o=w_o, w_pl=w_pl, w_plg=w_plg, g_pl=g_pl)
    ms = dict(g_in=m_g_in, w_in=m_w_in, g_cq=m_g_cq, w_uq=m_w_uq, g_ckv=m_g_ckv, w_ukv=m_w_ukv, g_q=m_g_q, g_k=m_g_k,
              conv_w=m_conv_w, g_oa=m_g_oa, g_oc=m_g_oc, w_o=m_w_o, w_pl=m_w_pl, w_plg=m_w_plg, g_pl=m_g_pl)
    vs = dict(g_in=v_g_in, w_in=v_w_in, g_cq=v_g_cq, w_uq=v_w_uq, g_ckv=v_g_ckv, w_ukv=v_w_ukv, g_q=v_g_q, g_k=v_g_k,
              conv_w=v_conv_w, g_oa=v_g_oa, g_oc=v_g_oc, w_o=v_w_o, w_pl=v_w_pl, w_plg=v_w_plg, g_pl=v_g_pl)
    names = list(weights)
    flat = lambda a: a.reshape(-1, a.shape[-1])
    small_names = list(g_by_name)
    small_out = _adamw_small([flat(weights[n]) for n in small_names], [flat(g_by_name[n]) for n in small_names],
                             [flat(ms[n]) for n in small_names], [flat(vs[n]) for n in small_names])
    results = {n: (flat(g_by_name[n]), *(out[i] for out in small_out)) for i, n in enumerate(small_names)}
    for n in half_by_name:
        results[n] = _adamw_halves(flat(weights[n]), *half_by_name[n], flat(ms[n]), flat(vs[n]), c, "adamw_" + n)
    per_kind = [[results[n][kind].reshape(weights[n].shape) for n in names] for kind in range(4)]
    return (loss_out, gx.reshape(x.shape), *per_kind[0], *per_kind[1], *per_kind[2], *per_kind[3])
```

```python
import math

import jax
import jax.numpy as jnp
from jax import lax
from jax.experimental import pallas as pl
from jax.experimental.pallas import tpu as pltpu

F32 = jnp.float32
BF16 = jnp.bfloat16

D_MODEL = 1024
N_HEADS = 4
NOPE = 128
ROPE = 64
V_DIM = 128
QK_DIM = NOPE + ROPE
HEAD_PAD = 256
Q_LORA = 256
KV_LORA = 128
ATTN_W = 512
CONV_W = 512
PLE = 256
IN_TOTAL = 3008
PROJ_EXT = 3072
ROPE_THETA = 10000.0
EPS = 1e-6
SCALE = 1.0 / math.sqrt(QK_DIM)
LOG2E = math.log2(math.e)
EXP2_SCALE = SCALE * LOG2E
NEG = -1e30
SOFTMAX_ROWS = 32
SUB_TILE = 256

LR, B1, B2, ADAM_EPS, WD, STEP = 0.001, 0.9, 0.999, 1e-08, 0.01, 10

N_CHIPS = 4
LANES = 128
VMEM_LIMIT = 56 * 1024 * 1024
MESH = pl.DeviceIdType.MESH


def _params(**kw):
    return pltpu.CompilerParams(vmem_limit_bytes=VMEM_LIMIT, **kw)


def _inv_rms(x, n):
    return lax.rsqrt(jnp.sum(x * x, axis=-1, keepdims=True) / n + EPS)


def _lane_sum(a):
    folded = a[:, 0:LANES]
    for c0 in range(LANES, a.shape[1], LANES):
        folded = folded + a[:, c0:c0 + LANES]
    head = folded.astype(BF16)
    tail = (folded - head.astype(F32)).astype(BF16)
    return _dot(jnp.concatenate([head, tail], axis=1), jnp.ones((2 * LANES, LANES), BF16))


def _inv_rms_mxu(x):
    return lax.rsqrt(_lane_sum(x * x) / x.shape[1] + EPS)


def _rep(r, width):
    return r if width == LANES else jnp.tile(r, (1, width // LANES))


def _sigmoid(z):
    return jax.nn.sigmoid(z)


def _swap_rope_halves(b):
    lane = lax.broadcasted_iota(jnp.int32, b.shape, 1)
    swapped = jnp.where(lane < 32, pltpu.roll(b, 96, 1), pltpu.roll(b, 32, 1))
    return jnp.where(lane < ROPE, swapped, 0.0)


def _dot(a, b):
    return jnp.dot(a, b, preferred_element_type=F32)


def _dot_nt(a, b):
    return lax.dot_general(a, b, (((1,), (1,)), ((), ())), preferred_element_type=F32)


def _dot_tn(a, b):
    return lax.dot_general(a, b, (((0,), (0,)), ((), ())), preferred_element_type=F32)


def _colsum(a):
    return jnp.sum(a, axis=0, keepdims=True)


def _full(shape):
    return pl.BlockSpec(shape, lambda *_: (0,) * len(shape))


def _round_robin(chains, width):
    waiting, active = list(chains), []
    while waiting or active:
        while waiting and len(active) < width:
            active.append(waiting.pop(0))
        for chain in list(active):
            if next(chain, _DONE) is _DONE:
                active.remove(chain)


_DONE = object()


def _rope_tables(pos_ref, invf_ref, sgn_ref):
    ang = pos_ref[...].astype(F32) * invf_ref[...]
    return jnp.cos(ang), jnp.sin(ang) * sgn_ref[...]


def _fwd_proj(x, pos, g_in, w_in, g_cq, w_uq, g_ckv, w_ukv, gq, gk, invf, sgn, late_shards, tm):
    T = x.shape[0]
    nt = T // tm
    n_late = len(late_shards)
    ts = min(SUB_TILE, tm)

    def body(x_ref, pos_ref, g_in_ref, w_in_ref, g_cq_ref, w_uq_ref, g_ckv_ref, w_ukv_ref, gq_ref, gk_ref,
             invf_ref, sgn_ref, *rest):
        late_in, (proj_ref, q_ref, k_ref, v_ref) = rest[:n_late], rest[n_late:n_late + 4]
        late_out, late_scratch = rest[n_late + 4:2 * n_late + 4], rest[2 * n_late + 4:]
        i = pl.program_id(0)
        if n_late:
            start, forward, drain = _gather_steps([s.shape for s in late_shards], late_in, late_out,
                                                  late_scratch[:n_late], *late_scratch[n_late:])
            pl.when(i == 0)(start)
            pl.when(i == nt // 2)(forward)

        for r0 in range(0, tm, ts):
            rows = slice(r0, r0 + ts)
            xv = x_ref[rows, :]
            h = (xv * _rep(_inv_rms_mxu(xv), D_MODEL) * g_in_ref[...]).astype(BF16)
            lat = _dot(h, w_in_ref[:, 0:512])
            proj_ref[rows, 0:512] = lat
            c_q = lat[:, 0:Q_LORA]
            cqn = (c_q * _rep(_inv_rms_mxu(c_q), Q_LORA) * g_cq_ref[...]).astype(BF16)
            c_kv = lat[:, Q_LORA:Q_LORA + KV_LORA]
            ckvn = (c_kv * _inv_rms_mxu(c_kv) * g_ckv_ref[...]).astype(BF16)
            kpe = lat[:, 384:512]
            kpe_sq = kpe * kpe
            cos_b, sin_b = _rope_tables(pos_ref.at[rows, :], invf_ref, sgn_ref)
            gq_a, gq_b = gq_ref[:, 0:NOPE], gq_ref[:, NOPE:HEAD_PAD]
            gk_a, gk_b = gk_ref[:, 0:NOPE], gk_ref[:, NOPE:HEAD_PAD]

            def projections(rows=rows, h=h):
                for c0 in range(512, PROJ_EXT, 512):
                    proj_ref[rows, c0:c0 + 512] = _dot(h, w_in_ref[:, c0:c0 + 512])
                    yield

            def queries(hd, rows=rows, cqn=cqn, cos_b=cos_b, sin_b=sin_b, gq_a=gq_a, gq_b=gq_b):
                qh = _dot(cqn, w_uq_ref[hd])
                yield
                a, b = qh[:, 0:NOPE], qh[:, NOPE:HEAD_PAD]
                r = lax.rsqrt(_lane_sum(a * a + b * b) / QK_DIM + EPS)
                yield
                bn = b * r * gq_b
                q_ref[hd, rows, 0:NOPE] = (a * r * gq_a).astype(BF16)
                q_ref[hd, rows, NOPE:HEAD_PAD] = (bn * cos_b + _swap_rope_halves(bn) * sin_b).astype(BF16)
                yield

            def keys(hd, rows=rows, ckvn=ckvn, kpe=kpe, kpe_sq=kpe_sq, cos_b=cos_b, sin_b=sin_b, gk_a=gk_a, gk_b=gk_b):
                kvh = _dot(ckvn, w_ukv_ref[hd])
                yield
                ka = kvh[:, 0:NOPE]
                rk = lax.rsqrt(_lane_sum(ka * ka + kpe_sq) / QK_DIM + EPS)
                yield
                kbn = kpe * rk * gk_b
                k_ref[hd, rows, 0:NOPE] = (ka * rk * gk_a).astype(BF16)
                k_ref[hd, rows, NOPE:HEAD_PAD] = (kbn * cos_b + _swap_rope_halves(kbn) * sin_b).astype(BF16)
                v_ref[hd, rows, 0:V_DIM] = kvh[:, NOPE:HEAD_PAD].astype(BF16)
                v_ref[hd, rows, V_DIM:2 * V_DIM] = jnp.ones((ts, V_DIM), BF16)
                yield

            chains = [projections()]
            for hd in range(N_HEADS):
                chains += [queries(hd), keys(hd)]
            _round_robin(chains, 4)

        if n_late:
            pl.when(i == nt - 1)(drain)

    row = lambda i: (i, 0)
    head_rows = lambda i: (0, i, 0)
    outs = pl.pallas_call(
        body, name="fwd_proj", grid=(nt,),
        in_specs=[pl.BlockSpec((tm, D_MODEL), row), pl.BlockSpec((tm, 1), row), _full((1, D_MODEL)),
                  _full((D_MODEL, PROJ_EXT)), _full((1, Q_LORA)), _full((N_HEADS, Q_LORA, HEAD_PAD)),
                  _full((1, KV_LORA)), _full((N_HEADS, KV_LORA, HEAD_PAD)), _full((1, HEAD_PAD)), _full((1, HEAD_PAD)),
                  _full((1, LANES)), _full((1, LANES))] + [_full(s.shape) for s in late_shards],
        out_specs=[pl.BlockSpec((tm, PROJ_EXT), row), pl.BlockSpec((N_HEADS, tm, HEAD_PAD), head_rows),
                   pl.BlockSpec((N_HEADS, tm, HEAD_PAD), head_rows), pl.BlockSpec((N_HEADS, tm, 2 * V_DIM), head_rows)]
                  + [_ANY] * n_late,
        out_shape=[jax.ShapeDtypeStruct((T, PROJ_EXT), F32), jax.ShapeDtypeStruct((N_HEADS, T, HEAD_PAD), BF16),
                   jax.ShapeDtypeStruct((N_HEADS, T, HEAD_PAD), BF16), jax.ShapeDtypeStruct((N_HEADS, T, 2 * V_DIM), BF16)]
                  + _gathered_shapes(late_shards),
        scratch_shapes=_gather_scratch(late_shards) if n_late else [],
        compiler_params=_params(dimension_semantics=("arbitrary",)),
    )(x, pos, g_in, w_in, g_cq, w_uq, g_ckv, w_ukv, gq, gk, invf, sgn, *late_shards)
    return outs[:4], outs[4:]


def _chunk_pipeline(n_loop, lag, matmuls, pointwise, accumulate, last):
    slots = lag + 1

    def iteration(t, slot):
        matmuls(jnp.minimum(t + lag, n_loop), (slot + lag) % slots)
        accumulate(jnp.maximum(t - lag, 0), (slot + 1) % slots)
        pointwise(t, slot, False)

    def finish(slot):
        for back in range(lag, 0, -1):
            accumulate(jnp.maximum(n_loop - back, 0), (slot - back) % slots)
        pointwise(n_loop, slot, True)
        accumulate(n_loop, slot)
        last()

    for u in range(lag):
        matmuls(jnp.minimum(u, n_loop), u)

    def unrolled(tt, carry):
        for slot in range(slots):
            iteration(slots * tt + slot, slot)
        return carry

    lax.fori_loop(0, n_loop // slots, unrolled, 0)
    rest = lax.rem(n_loop, slots)
    t0 = n_loop - rest

    for r in range(slots):
        @pl.when(rest == r)
        def _():
            for slot in range(r):
                iteration(t0 + slot, slot)
            finish(r)


def _attn_fwd(q, k, v, tq):
    T = q.shape[1]
    tk = tq
    rc = min(SOFTMAX_ROWS, tq)

    def body(q_ref, k_ref, v_ref, o_ref, lse_ref, s0, s1, s2, p0, p1, p2, a0, a1, a2, m_ref, acc_ref):
        qi = pl.program_id(1)
        s_buf, p_buf, a_buf = (s0, s1, s2), (p0, p1, p2), (a0, a1, a2)

        def scores(t, slot):
            ks = pl.multiple_of(t * tk, tk)
            s_buf[slot][...] = _dot_nt(q_ref[0], k_ref[0, pl.ds(ks, tk), :])

        def values(t, slot):
            ks = pl.multiple_of(t * tk, tk)
            acc_ref[...] = acc_ref[...] * a_buf[slot][...] + _dot(p_buf[slot][...], v_ref[0, pl.ds(ks, tk), :])

        def softmax(t, slot, masked):
            s_all = s_buf[slot][...]
            if masked:
                row = lax.broadcasted_iota(jnp.int32, (tq, tk), 0)
                col = lax.broadcasted_iota(jnp.int32, (tq, tk), 1)
                s_all = jnp.where(col <= row, s_all, NEG)
                s_buf[slot][...] = s_all
            m_old = m_ref[...]
            m_new = jnp.maximum(m_old, jnp.max(s_all, axis=1, keepdims=True))
            a_buf[slot][...] = jnp.exp2((m_old - m_new) * EXP2_SCALE)
            m_ref[...] = m_new
            for r0 in range(0, tq, rc):
                s = s_buf[slot][r0:r0 + rc, :]
                p_buf[slot][r0:r0 + rc, :] = jnp.exp2((s - m_new[r0:r0 + rc, :]) * EXP2_SCALE).astype(BF16)

        def last():
            l = acc_ref[:, V_DIM:2 * V_DIM]
            o_ref[...] = acc_ref[:, 0:V_DIM] / l
            lse_ref[0] = (m_ref[...] * SCALE + jnp.log(l)).T[0:1, :]

        m_ref[...] = jnp.full_like(m_ref, NEG)
        acc_ref[...] = jnp.zeros_like(acc_ref)
        for p_late, a_late in ((p1, a1), (p2, a2)):
            p_late[...] = jnp.zeros_like(p_late)
            a_late[...] = jnp.ones_like(a_late)
        _chunk_pipeline(qi, 2, scores, softmax, values, last)

    return pl.pallas_call(
        body, name="attn_fwd", grid=(N_HEADS, T // tq),
        in_specs=[pl.BlockSpec((1, tq, HEAD_PAD), lambda h, i: (h, i, 0)),
                  pl.BlockSpec((1, T, HEAD_PAD), lambda h, i: (h, 0, 0)),
                  pl.BlockSpec((1, T, 2 * V_DIM), lambda h, i: (h, 0, 0))],
        out_specs=[pl.BlockSpec((tq, V_DIM), lambda h, i: (i, h)),
                   pl.BlockSpec((1, 1, tq), lambda h, i: (h, 0, i))],
        out_shape=[jax.ShapeDtypeStruct((T, ATTN_W), F32), jax.ShapeDtypeStruct((N_HEADS, 1, T), F32)],
        scratch_shapes=[pltpu.VMEM((tq, tk), F32)] * 3 + [pltpu.VMEM((tq, tk), BF16)] * 3
                       + [pltpu.VMEM((tq, 1), F32)] * 4 + [pltpu.VMEM((tq, 2 * V_DIM), F32)],
        compiler_params=_params(dimension_semantics=("arbitrary", "arbitrary")),
    )(q, k, v)


def _tail(x, o, proj, p, tgt, g_oa, g_oc, g_pl, conv_w, w_o, w_pl, w_plg, tm):
    T = x.shape[0]
    nt = T // tm

    def body(x_ref, o_ref, za_ref, cb_ref, cc_ref, cx_ref, zc_ref, cch_ref, cxh_ref, p_ref, tgt_ref,
             g_oa_ref, g_oc_ref, g_pl_ref, cw_ref, w_o_ref, w_pl_ref, w_plg_ref,
             dx1_ref, do_ref, delta_ref, dtail_ref, du_ref,
             dw_o_ref, dw_pl_ref, dw_plg_ref, dg_oa_ref, dg_oc_ref, dg_pl_ref, dcw_ref, loss_ref):
        i = pl.program_id(0)

        @pl.when(i == 0)
        def _():
            for r in (dw_o_ref, dw_pl_ref, dw_plg_ref, dg_oa_ref, dg_oc_ref, dg_pl_ref, dcw_ref, loss_ref):
                r[...] = jnp.zeros_like(r)

        g_oa, g_oc, g_pl = g_oa_ref[...], g_oc_ref[...], g_pl_ref[...]
        w0, w1, w2 = cw_ref[0:1, :], cw_ref[1:2, :], cw_ref[2:3, :]

        xv, ov, za, cb, zc = x_ref[...], o_ref[...], za_ref[...], cb_ref[...], zc_ref[...]
        pb = p_ref[...].astype(BF16)
        pp = _dot(pb, w_pl_ref[...])

        sa = _sigmoid(za)
        silu_a = za * sa
        ga = ov * silu_a
        ra = _inv_rms(ga, ATTN_W)
        xa = ga * ra
        ya = (xa * g_oa).astype(BF16)
        x1_a = _dot(ya, w_o_ref[0:ATTN_W, :])
        v = cc_ref[...] * cx_ref[...]
        not_first = jnp.where(i > 0, 1.0, 0.0)
        hv6 = cch_ref[6:7, :] * cxh_ref[6:7, :] * not_first
        hv7 = cch_ref[7:8, :] * cxh_ref[7:8, :] * not_first
        row = lax.broadcasted_iota(jnp.int32, v.shape, 0)
        v1 = jnp.where(row == 0, hv7, pltpu.roll(v, 1, 0))
        v2 = jnp.where(row == 0, hv6, jnp.where(row == 1, hv7, pltpu.roll(v, 2, 0)))
        u = w0 * v2 + w1 * v1 + w2 * v
        sc = _sigmoid(zc)
        silu_c = zc * sc
        gc = cb * u * silu_c
        rc = _inv_rms(gc, CONV_W)
        xc = gc * rc
        yc = (xc * g_oc).astype(BF16)
        x1 = xv + (x1_a + _dot(yc, w_o_ref[ATTN_W:D_MODEL, :]))
        r1 = _inv_rms(x1, D_MODEL)
        xh1 = x1 * r1
        n1 = (xh1 * g_pl).astype(BF16)
        gate = _sigmoid(_dot(n1, w_plg_ref[...]))
        err = x1 + gate * pp - tgt_ref[...]
        loss_ref[...] += 0.5 * jnp.sum(err * err) / D_MODEL
        dy = err / D_MODEL

        dpp = (dy * gate).astype(BF16)
        da = (dy * pp * gate * (1.0 - gate)).astype(BF16)
        dn1 = _dot_nt(da, w_plg_ref[...])
        dw_pl_ref[...] += _dot_tn(pb, dpp)
        dw_plg_ref[...] += _dot_tn(n1, da)
        dg_pl_ref[...] += _colsum(dn1 * xh1)
        dxh = dn1 * g_pl
        dx1 = dy + r1 * (dxh - xh1 * (jnp.sum(dxh * xh1, axis=-1, keepdims=True) / D_MODEL))
        dx1_ref[...] = dx1
        dx1b = dx1.astype(BF16)
        dya = _dot_nt(dx1b, w_o_ref[0:ATTN_W, :])
        dyc = _dot_nt(dx1b, w_o_ref[ATTN_W:D_MODEL, :])

        dw_o_ref[0:ATTN_W, :] += _dot_tn(ya, dx1b)
        dg_oa_ref[...] += _colsum(dya * xa)
        dxa = dya * g_oa
        dga = ra * (dxa - xa * (jnp.sum(dxa * xa, axis=-1, keepdims=True) / ATTN_W))
        do = (dga * silu_a).astype(BF16)
        do_ref[...] = do
        dof = do.astype(F32) * ov
        for hd in range(N_HEADS):
            delta_ref[hd] = _lane_sum(dof[:, hd * V_DIM:(hd + 1) * V_DIM]).T[0:1, :]
        dtail_ref[:, 0:512] = (dga * ov * (sa * (1.0 + za * (1.0 - sa)))).astype(BF16)

        dw_o_ref[ATTN_W:D_MODEL, :] += _dot_tn(yc, dx1b)
        dg_oc_ref[...] += _colsum(dyc * xc)
        dxc = dyc * g_oc
        dgc = rc * (dxc - xc * (jnp.sum(dxc * xc, axis=-1, keepdims=True) / CONV_W))
        dtail_ref[:, 512:1024] = (dgc * u * silu_c).astype(BF16)
        du = dgc * cb * silu_c
        du_ref[...] = du
        dtail_ref[:, 1024:1536] = (dgc * cb * u * (sc * (1.0 + zc * (1.0 - sc)))).astype(BF16)
        dcw_ref[0:1, :] += _colsum(du * v2)
        dcw_ref[1:2, :] += _colsum(du * v1)
        dcw_ref[2:3, :] += _colsum(du * v)

    row = lambda i: (i, 0)
    col = lambda c: (lambda i: (i, c))
    halo = lambda c: (lambda i: (jnp.maximum(i * (tm // 8) - 1, 0), c))
    in_specs = [pl.BlockSpec((tm, D_MODEL), row), pl.BlockSpec((tm, ATTN_W), row)]
    in_specs += [pl.BlockSpec((tm, 512), col(c)) for c in (1, 2, 3, 4, 5)]
    in_specs += [pl.BlockSpec((8, 512), halo(3)), pl.BlockSpec((8, 512), halo(4))]
    in_specs += [pl.BlockSpec((tm, PLE), row), pl.BlockSpec((tm, D_MODEL), row),
                 _full((1, ATTN_W)), _full((1, CONV_W)), _full((1, D_MODEL)), _full((3, CONV_W)),
                 _full((D_MODEL, D_MODEL)), _full((PLE, D_MODEL)), _full((D_MODEL, D_MODEL))]
    out_specs = [pl.BlockSpec((tm, D_MODEL), row), pl.BlockSpec((tm, ATTN_W), row),
                 pl.BlockSpec((N_HEADS, 1, tm), lambda i: (0, 0, i)), pl.BlockSpec((tm, 1536), row),
                 pl.BlockSpec((tm, CONV_W), row),
                 _full((D_MODEL, D_MODEL)), _full((PLE, D_MODEL)), _full((D_MODEL, D_MODEL)),
                 _full((1, ATTN_W)), _full((1, CONV_W)), _full((1, D_MODEL)), _full((3, CONV_W)), _full((1, LANES))]
    out_shape = [jax.ShapeDtypeStruct((T, D_MODEL), F32), jax.ShapeDtypeStruct((T, ATTN_W), BF16),
                 jax.ShapeDtypeStruct((N_HEADS, 1, T), F32), jax.ShapeDtypeStruct((T, 1536), BF16),
                 jax.ShapeDtypeStruct((T, CONV_W), F32),
                 jax.ShapeDtypeStruct((D_MODEL, D_MODEL), F32), jax.ShapeDtypeStruct((PLE, D_MODEL), F32),
                 jax.ShapeDtypeStruct((D_MODEL, D_MODEL), F32),
                 jax.ShapeDtypeStruct((1, ATTN_W), F32), jax.ShapeDtypeStruct((1, CONV_W), F32),
                 jax.ShapeDtypeStruct((1, D_MODEL), F32), jax.ShapeDtypeStruct((3, CONV_W), F32),
                 jax.ShapeDtypeStruct((1, LANES), F32)]
    return pl.pallas_call(
        body, name="tail", grid=(nt,), in_specs=in_specs, out_specs=out_specs, out_shape=out_shape,
        compiler_params=_params(dimension_semantics=("arbitrary",)),
    )(x, o, proj, proj, proj, proj, proj, proj, proj, p, tgt, g_oa, g_oc, g_pl, conv_w, w_o, w_pl, w_plg)


def _attn_bwd(q, k, v, do, lse_row, delta_row, tk):
    T = q.shape[1]
    tq = tk
    nq = T // tq
    rc = min(SOFTMAX_ROWS, tk)

    def body(q_ref, k_ref, v_ref, do_ref, lse_ref, dl_ref, dq_ref, dk_ref, dv_ref,
             s0, s1, d0, d1, p0, p1, g0, g1, dk_acc, dv_acc):
        kj = pl.program_id(1)
        s_buf, dp_buf, p_buf, g_buf = (s0, s1), (d0, d1), (p0, p1), (g0, g1)

        @pl.when(kj == 0)
        def _():
            dq_ref[...] = jnp.zeros_like(dq_ref)

        def q_start(t):
            return pl.multiple_of((nq - 1 - t) * tq, tq)

        def matmuls(t, slot):
            qs = q_start(t)
            s_buf[slot][...] = _dot_nt(k_ref[0], q_ref[0, pl.ds(qs, tq), :])
            dp_buf[slot][...] = _dot_nt(v_ref[0], do_ref[pl.ds(qs, tq), :])

        def pointwise(t, slot, masked):
            qs = q_start(t)
            lse2 = lse_ref[0, :, pl.ds(qs, tq)] * LOG2E
            dl = dl_ref[0, :, pl.ds(qs, tq)]
            for r0 in range(0, tk, rc):
                st = s_buf[slot][r0:r0 + rc, :]
                if masked:
                    row = lax.broadcasted_iota(jnp.int32, (rc, tq), 0)
                    col = lax.broadcasted_iota(jnp.int32, (rc, tq), 1)
                    st = jnp.where(row + r0 <= col, st, NEG)
                pt = jnp.exp2(st * EXP2_SCALE - lse2)
                p_buf[slot][r0:r0 + rc, :] = pt.astype(BF16)
                g_buf[slot][r0:r0 + rc, :] = (pt * (dp_buf[slot][r0:r0 + rc, :] - dl) * SCALE).astype(BF16)

        def accumulate(t, slot):
            qs = q_start(t)
            dv_acc[...] += _dot(p_buf[slot][...], do_ref[pl.ds(qs, tq), :])
            dk_acc[...] += _dot(g_buf[slot][...], q_ref[0, pl.ds(qs, tq), :])
            dq_ref[0, pl.ds(qs, tq), :] += _dot_tn(g_buf[slot][...], k_ref[0])

        def last():
            dk_ref[0] = dk_acc[...]
            dv_ref[0] = dv_acc[...]

        dk_acc[...] = jnp.zeros_like(dk_acc)
        dv_acc[...] = jnp.zeros_like(dv_acc)
        for late in (p1, g1):
            late[...] = jnp.zeros_like(late)
        _chunk_pipeline(nq - 1 - kj, 1, matmuls, pointwise, accumulate, last)

    return pl.pallas_call(
        body, name="attn_bwd", grid=(N_HEADS, T // tk),
        in_specs=[pl.BlockSpec((1, T, HEAD_PAD), lambda h, j: (h, 0, 0)),
                  pl.BlockSpec((1, tk, HEAD_PAD), lambda h, j: (h, j, 0)),
                  pl.BlockSpec((1, tk, V_DIM), lambda h, j: (h, j, 0)),
                  pl.BlockSpec((T, V_DIM), lambda h, j: (0, h)),
                  pl.BlockSpec((1, 1, T), lambda h, j: (h, 0, 0)),
                  pl.BlockSpec((1, 1, T), lambda h, j: (h, 0, 0))],
        out_specs=[pl.BlockSpec((1, T, HEAD_PAD), lambda h, j: (h, 0, 0)),
                   pl.BlockSpec((1, tk, HEAD_PAD), lambda h, j: (h, j, 0)),
                   pl.BlockSpec((1, tk, V_DIM), lambda h, j: (h, j, 0))],
        out_shape=[jax.ShapeDtypeStruct((N_HEADS, T, HEAD_PAD), F32), jax.ShapeDtypeStruct((N_HEADS, T, HEAD_PAD), F32),
                   jax.ShapeDtypeStruct((N_HEADS, T, V_DIM), F32)],
        scratch_shapes=[pltpu.VMEM((tk, tq), F32)] * 4 + [pltpu.VMEM((tk, tq), BF16)] * 4
                       + [pltpu.VMEM((tk, HEAD_PAD), F32), pltpu.VMEM((tk, V_DIM), F32)],
        compiler_params=_params(dimension_semantics=("arbitrary", "arbitrary")),
    )(q, k, v, do, lse_row, delta_row)


def _bwd_proj(x, dx1, pos, proj, dq, dk, dv, dtail, du, g_in, w_in, g_cq, w_uq, g_ckv, w_ukv, gq, gk, conv_w,
              invf, sgn, tm):
    T = x.shape[0]
    nt = T // tm

    ts = min(SUB_TILE, tm)

    def body(x_ref, dx1_ref, pos_ref, lat_ref, cc_ref, cx_ref, dq_ref, dk_ref, dv_ref, dtail_ref, du_ref, dun_ref, *rest):
        consts, (gx_ref, h_ref, dproj_ref), sums = rest[:11], rest[11:14], rest[14:]
        cw_ref = consts[8]
        i = pl.program_id(0)

        @pl.when(i == 0)
        def _():
            for r in sums:
                r[...] = jnp.zeros_like(r)

        du_v = du_ref[...]
        not_last = jnp.where(i < nt - 1, 1.0, 0.0)
        nx0 = dun_ref[0:1, :] * not_last
        nx1 = dun_ref[1:2, :] * not_last
        row = lax.broadcasted_iota(jnp.int32, du_v.shape, 0)
        du1 = jnp.where(row == tm - 1, nx0, pltpu.roll(du_v, tm - 1, 0))
        du2 = jnp.where(row == tm - 2, nx0, jnp.where(row == tm - 1, nx1, pltpu.roll(du_v, tm - 2, 0)))
        dvc = cw_ref[2:3, :] * du_v + cw_ref[1:2, :] * du1 + cw_ref[0:1, :] * du2
        dproj_ref[:, 1536:2048] = (dvc * cx_ref[...]).astype(BF16)
        dproj_ref[:, 2048:2560] = (dvc * cc_ref[...]).astype(BF16)

        for r0 in range(0, tm, ts):
            rows = slice(r0, r0 + ts)
            work(x_ref.at[rows, :], dx1_ref.at[rows, :], pos_ref.at[rows, :], lat_ref.at[rows, :],
                 dq_ref.at[:, rows, :], dk_ref.at[:, rows, :], dv_ref.at[:, rows, :], dtail_ref.at[rows, :], *consts,
                 gx_ref.at[rows, :], h_ref.at[:, rows], dproj_ref.at[rows, :], *sums)

    def work(x_ref, dx1_ref, pos_ref, lat_ref, dq_ref, dk_ref, dv_ref, dtail_ref,
             g_in_ref, w_in_ref, g_cq_ref, w_uq_ref, g_ckv_ref, w_ukv_ref, gq_ref, gk_ref, cw_ref, invf_ref, sgn_ref,
             gx_ref, h_ref, dproj_ref, dw_uq_ref, dw_ukv_ref, dg_in_ref, dg_cq_ref, dg_ckv_ref, dgq_ref, dgk_ref):
        xv = x_ref[...]
        r0 = _rep(_inv_rms_mxu(xv), D_MODEL)
        xh0 = xv * r0
        g_in = g_in_ref[...]
        h_ref[...] = (xh0 * g_in).astype(BF16).T

        c_q = lat_ref[:, 0:Q_LORA]
        rq = _rep(_inv_rms_mxu(c_q), Q_LORA)
        xq = c_q * rq
        g_cq = g_cq_ref[...]
        cqn = (xq * g_cq).astype(BF16)
        c_kv = lat_ref[:, Q_LORA:Q_LORA + KV_LORA]
        rkv = _inv_rms_mxu(c_kv)
        xkv = c_kv * rkv
        g_ckv = g_ckv_ref[...]
        ckvn = (xkv * g_ckv).astype(BF16)
        kpe = lat_ref[:, 384:512]
        kpe_sq = kpe * kpe
        cos_b, sin_b = _rope_tables(pos_ref, invf_ref, sgn_ref)
        gq_a, gq_b = gq_ref[:, 0:NOPE], gq_ref[:, NOPE:HEAD_PAD]
        gk_a, gk_b = gk_ref[:, 0:NOPE], gk_ref[:, NOPE:HEAD_PAD]

        dproj_ref[:, 512:1536] = dtail_ref[:, 0:1024]
        dproj_ref[:, 2560:3072] = dtail_ref[:, 1024:1536]

        def dh_part(c0):
            return _dot_nt(dproj_ref[:, c0:c0 + 512], w_in_ref[:, c0:c0 + 512])

        later_chunks = ((512,), (1024,), (1536, 2048), (2560,))
        dh = jnp.zeros((ts, D_MODEL), F32)
        acc = dict(dh=dh, dkpe=jnp.zeros((ts, LANES), F32), dcqn=jnp.zeros((ts, Q_LORA), F32),
                   dckvn=jnp.zeros((ts, KV_LORA), F32))

        def dh_chunks():
            for chunks in later_chunks:
                for chunk in chunks:
                    acc["dh"] = acc["dh"] + dh_part(chunk)
                    yield

        def queries(hd):
            qh = _dot(cqn, w_uq_ref[hd])
            yield
            a, b = qh[:, 0:NOPE], qh[:, NOPE:HEAD_PAD]
            r = lax.rsqrt(_lane_sum(a * a + b * b) / QK_DIM + EPS)
            yield
            xa, xb = a * r, b * r
            dan = dq_ref[hd, :, 0:NOPE]
            dbr = dq_ref[hd, :, NOPE:HEAD_PAD]
            dbn = dbr * cos_b + _swap_rope_halves(dbr * sin_b)
            yield
            dgq_ref[:, 0:NOPE] += _colsum(dan * xa)
            dgq_ref[:, NOPE:HEAD_PAD] += _colsum(dbn * xb)
            dxa, dxb = dan * gq_a, dbn * gq_b
            cq = _lane_sum(dxa * xa + dxb * xb) / QK_DIM
            yield
            dqh = jnp.concatenate([r * (dxa - xa * cq), r * (dxb - xb * cq)], axis=-1).astype(BF16)
            yield
            dw_uq_ref[hd] += _dot_tn(cqn, dqh)
            yield
            acc["dcqn"] = acc["dcqn"] + _dot_nt(dqh, w_uq_ref[hd])
            yield

        def keys(hd):
            kvh = _dot(ckvn, w_ukv_ref[hd])
            yield
            ka = kvh[:, 0:NOPE]
            rk = lax.rsqrt(_lane_sum(ka * ka + kpe_sq) / QK_DIM + EPS)
            yield
            xka, xkb = ka * rk, kpe * rk
            dkan = dk_ref[hd, :, 0:NOPE]
            dkbr = dk_ref[hd, :, NOPE:HEAD_PAD]
            dkbn = dkbr * cos_b + _swap_rope_halves(dkbr * sin_b)
            yield
            dgk_ref[:, 0:NOPE] += _colsum(dkan * xka)
            dgk_ref[:, NOPE:HEAD_PAD] += _colsum(dkbn * xkb)
            dxka, dxkb = dkan * gk_a, dkbn * gk_b
            ck = _lane_sum(dxka * xka + dxkb * xkb) / QK_DIM
            yield
            acc["dkpe"] = acc["dkpe"] + rk * (dxkb - xkb * ck)
            dkvh = jnp.concatenate([rk * (dxka - xka * ck), dv_ref[hd]], axis=-1).astype(BF16)
            yield
            dw_ukv_ref[hd] += _dot_tn(ckvn, dkvh)
            yield
            acc["dckvn"] = acc["dckvn"] + _dot_nt(dkvh, w_ukv_ref[hd])
            yield

        chains = [dh_chunks()]
        for hd in range(N_HEADS):
            chains += [queries(hd), keys(hd)]
        _round_robin(chains, 5)
        dh, dkpe, dcqn, dckvn = acc["dh"], acc["dkpe"], acc["dcqn"], acc["dckvn"]

        dg_cq_ref[...] += _colsum(dcqn * xq)
        dxq = dcqn * g_cq
        dproj_ref[:, 0:Q_LORA] = (rq * (dxq - xq * _rep(_lane_sum(dxq * xq) / Q_LORA, Q_LORA))).astype(BF16)
        dg_ckv_ref[...] += _colsum(dckvn * xkv)
        dxkv = dckvn * g_ckv
        dproj_ref[:, 256:384] = (rkv * (dxkv - xkv * (_lane_sum(dxkv * xkv) / KV_LORA))).astype(BF16)
        dproj_ref[:, 384:512] = dkpe.astype(BF16)
        dh = dh + dh_part(0)
        dg_in_ref[...] += _colsum(dh * xh0)
        dxh = dh * g_in
        gx_ref[...] = dx1_ref[...] + r0 * (dxh - xh0 * _rep(_lane_sum(dxh * xh0) / D_MODEL, D_MODEL))

    row = lambda i: (i, 0)
    col = lambda c: (lambda i: (i, c))
    head_rows = lambda i: (0, i, 0)
    nxt = lambda i: (jnp.minimum((i + 1) * (tm // 8), T // 8 - 1), 0)
    in_specs = [pl.BlockSpec((tm, D_MODEL), row), pl.BlockSpec((tm, D_MODEL), row), pl.BlockSpec((tm, 1), row),
                pl.BlockSpec((tm, 512), col(0)), pl.BlockSpec((tm, 512), col(3)), pl.BlockSpec((tm, 512), col(4)),
                pl.BlockSpec((N_HEADS, tm, HEAD_PAD), head_rows), pl.BlockSpec((N_HEADS, tm, HEAD_PAD), head_rows),
                pl.BlockSpec((N_HEADS, tm, V_DIM), head_rows), pl.BlockSpec((tm, 1536), row),
                pl.BlockSpec((tm, CONV_W), row), pl.BlockSpec((8, CONV_W), nxt),
                _full((1, D_MODEL)), _full((D_MODEL, PROJ_EXT)), _full((1, Q_LORA)), _full((N_HEADS, Q_LORA, HEAD_PAD)),
                _full((1, KV_LORA)), _full((N_HEADS, KV_LORA, HEAD_PAD)), _full((1, HEAD_PAD)), _full((1, HEAD_PAD)),
                _full((3, CONV_W)), _full((1, LANES)), _full((1, LANES))]
    out_specs = [pl.BlockSpec((tm, D_MODEL), row), pl.BlockSpec((D_MODEL, tm), lambda i: (0, i)),
                 pl.BlockSpec((tm, PROJ_EXT), row),
                 _full((N_HEADS, Q_LORA, HEAD_PAD)), _full((N_HEADS, KV_LORA, HEAD_PAD)),
                 _full((1, D_MODEL)), _full((1, Q_LORA)), _full((1, KV_LORA)), _full((1, HEAD_PAD)), _full((1, HEAD_PAD))]
    out_shape = [jax.ShapeDtypeStruct((T, D_MODEL), F32), jax.ShapeDtypeStruct((D_MODEL, T), BF16),
                 jax.ShapeDtypeStruct((T, PROJ_EXT), BF16),
                 jax.ShapeDtypeStruct((N_HEADS, Q_LORA, HEAD_PAD), F32), jax.ShapeDtypeStruct((N_HEADS, KV_LORA, HEAD_PAD), F32),
                 jax.ShapeDtypeStruct((1, D_MODEL), F32), jax.ShapeDtypeStruct((1, Q_LORA), F32),
                 jax.ShapeDtypeStruct((1, KV_LORA), F32), jax.ShapeDtypeStruct((1, HEAD_PAD), F32),
                 jax.ShapeDtypeStruct((1, HEAD_PAD), F32)]
    return pl.pallas_call(
        body, name="bwd_proj", grid=(nt,), in_specs=in_specs, out_specs=out_specs, out_shape=out_shape,
        compiler_params=_params(dimension_semantics=("arbitrary",)),
    )(x, dx1, pos, proj, proj, proj, dq, dk, dv, dtail, du, du, g_in, w_in, g_cq, w_uq, g_ckv, w_ukv, gq, gk, conv_w,
      invf, sgn)


def _matmul_acc(a, b, tt, tn, parts):
    M, T = a.shape
    N = b.shape[1]
    n = len(parts)
    grid = (N // tn, T // tt)

    def body(a_ref, b_ref, *rest):
        part_refs, o_ref, out_refs, sems = rest[:n], rest[n], rest[n + 1:2 * n + 1], rest[2 * n + 1:]
        j, t = pl.program_id(0), pl.program_id(1)
        if n:
            start, drain = _scatter_steps(part_refs, out_refs, *sems)
            pl.when(jnp.logical_and(j == 0, t == 0))(start)

        @pl.when(t == 0)
        def _():
            o_ref[...] = jnp.zeros_like(o_ref)

        o_ref[...] += _dot(a_ref[...], b_ref[...])
        if n:
            pl.when(jnp.logical_and(j == grid[0] - 1, t == grid[1] - 1))(drain)

    sems = [pltpu.SemaphoreType.DMA((3 * n,)), pltpu.SemaphoreType.DMA((3 * n,)), pltpu.SemaphoreType.DMA((n,))]
    outs = pl.pallas_call(
        body, name="dw_in", grid=grid,
        in_specs=[pl.BlockSpec((M, tt), lambda j, t: (0, t)), pl.BlockSpec((tt, tn), lambda j, t: (t, j))] + [_ANY] * n,
        out_specs=[pl.BlockSpec((M, tn), lambda j, t: (0, j))] + [_ANY] * n,
        out_shape=[jax.ShapeDtypeStruct((M, N), F32)] + _scattered_shapes(parts),
        scratch_shapes=sems if n else [],
        compiler_params=_params(dimension_semantics=("arbitrary", "arbitrary")),
    )(a, b, *parts)
    return outs[0], outs[1:]


def _add_chips(parts, small_parts):
    arrays = list(parts) + [small_parts]

    def body(*refs):
        ins, outs = refs[:len(arrays)], refs[len(arrays):]
        for a_ref, o_ref in zip(ins, outs):
            part = lambda k: a_ref[k].astype(F32)
            o_ref[...] = ((part(0) + part(1)) + part(2)) + part(3)

    in_specs, out_specs, out_shape = [], [], []
    for a in arrays:
        _, rows, cols = a.shape
        in_specs.append(pl.BlockSpec((N_CHIPS, rows // 2, cols), lambda i: (0, i, 0)))
        out_specs.append(pl.BlockSpec((rows // 2, cols), lambda i: (i, 0)))
        out_shape.append(jax.ShapeDtypeStruct((rows, cols), F32))
    outs = pl.pallas_call(body, name="add_chips", grid=(2,), in_specs=in_specs, out_specs=out_specs,
                          out_shape=out_shape, compiler_params=_params(dimension_semantics=("arbitrary",)))(*arrays)
    return outs[:-1], outs[-1]


def _adamw_small(ws, gs, ms, vs):
    n = len(ws)

    def body(*refs):
        for i in range(n):
            w_ref, g_ref, m_ref, v_ref = (refs[k * n + i] for k in range(4))
            d_ref, nm_ref, nv_ref = (refs[(4 + k) * n + i] for k in range(3))
            _adamw_math(g_ref[...], w_ref, m_ref, v_ref, d_ref, nm_ref, nv_ref)

    shapes = [jax.ShapeDtypeStruct(w.shape, F32) for w in ws]
    outs = pl.pallas_call(body, name="adamw_small", out_shape=shapes * 3)(*ws, *gs, *ms, *vs)
    return outs[:n], outs[n:2 * n], outs[2 * n:]


def _adamw_math(gv, w_ref, m_ref, v_ref, d_ref, nm_ref, nv_ref):
    nm = B1 * m_ref[...] + (1.0 - B1) * gv
    nv = B2 * v_ref[...] + (1.0 - B2) * (gv * gv)
    m_hat = nm / (1.0 - B1 ** STEP)
    v_hat = nv / (1.0 - B2 ** STEP)
    d_ref[...] = -LR * (m_hat / (jnp.sqrt(v_hat) + ADAM_EPS) + WD * w_ref[...])
    nm_ref[...] = nm
    nv_ref[...] = nv


def _adamw_halves(w, mine, other, m, v, c, name):
    hr, cols = mine.shape

    def body(c_ref, w_ref, mine_ref, other_ref, m_ref, v_ref, g_ref, d_ref, nm_ref, nv_ref):
        gv = jnp.where(pl.program_id(0) == c_ref[0], mine_ref[...], other_ref[...])
        g_ref[...] = gv
        _adamw_math(gv, w_ref, m_ref, v_ref, d_ref, nm_ref, nv_ref)

    half = pl.BlockSpec((hr, cols), lambda i, c_ref: (i, 0))
    whole = pl.BlockSpec((hr, cols), lambda i, c_ref: (0, 0))
    shp = jax.ShapeDtypeStruct(w.shape, F32)
    return pl.pallas_call(
        body, name=name, out_shape=[shp] * 4,
        grid_spec=pltpu.PrefetchScalarGridSpec(num_scalar_prefetch=1, grid=(2,), in_specs=[half, whole, whole, half, half],
                                               out_specs=[half] * 4),
        compiler_params=_params(dimension_semantics=("arbitrary",)),
    )(c.reshape(1), w, mine, other, m, v)


_ANY = pl.BlockSpec(memory_space=pl.ANY)


def _mesh_pos():
    return lax.axis_index("x"), lax.axis_index("y"), lax.axis_index("c")


def _other_chips(x, y):
    return [(1 - x, y), (x, 1 - y), (1 - x, 1 - y)]


def _remote(src, dst, send_sems, recv_sems, k, to):
    return pltpu.make_async_remote_copy(src_ref=src, dst_ref=dst, send_sem=send_sems.at[k], recv_sem=recv_sems.at[k],
                                        device_id=to, device_id_type=MESH)


def _gather_weights(shards):
    n = len(shards)

    def body(*refs):
        start, forward, drain = _gather_steps([s.shape for s in shards], refs[:n], refs[n:2 * n], refs[2 * n:3 * n],
                                              *refs[3 * n:])
        start()
        forward()
        drain()

    vmem = pl.BlockSpec(memory_space=pltpu.VMEM)
    return pl.pallas_call(
        body, name="gather_weights", in_specs=[vmem] * n, out_specs=[_ANY] * n,
        out_shape=_gathered_shapes(shards), scratch_shapes=_gather_scratch(shards), compiler_params=_params(),
    )(*shards)


def _travel_shape(shard):
    rows, cols = shard.shape
    return (rows, HEAD_PAD if cols == QK_DIM else cols)


def _gathered_shapes(shards):
    return [jax.ShapeDtypeStruct((N_CHIPS,) + _travel_shape(s), BF16) for s in shards]


def _gather_scratch(shards):
    n = len(shards)
    return ([pltpu.VMEM(_travel_shape(s), BF16) for s in shards]
            + [pltpu.SemaphoreType.DMA((6 * n,)), pltpu.SemaphoreType.DMA((6 * n,)), pltpu.SemaphoreType.DMA((n,))])


def _gather_steps(shapes, ins, outs, stage, send_sems, recv_sems, local_sems):
    n = len(shapes)
    halved = [s[0] % 32 == 0 for s in shapes]

    def part(i, ref, hc):
        if not halved[i]:
            return ref
        hr = shapes[i][0] // 2
        return ref.at[pl.ds(hc * hr, hr), :]

    def to_chip(i, j, x, y, c):
        cx, cy = _other_chips(x, y)[j]
        return _remote(part(i, stage[i], c), part(i, outs[i].at[2 * x + y], c), send_sems, recv_sems, 6 * i + j, (cx, cy, c))

    def to_sibling(i, j, x, y, c):
        cx, cy = _other_chips(x, y)[j]
        got = part(i, outs[i].at[2 * cx + cy], c)
        return _remote(got, got, send_sems, recv_sems, 6 * i + 3 + j, (x, y, 1 - c))

    def local(i, x, y):
        return pltpu.make_async_copy(stage[i], outs[i].at[2 * x + y], local_sems.at[i])

    def start():
        x, y, c = _mesh_pos()
        for i in range(n):
            cols = ins[i].shape[1]
            if stage[i].shape[1] != cols:
                stage[i][...] = jnp.zeros_like(stage[i])
            stage[i][:, 0:cols] = ins[i][...].astype(BF16)
            local(i, x, y).start()
            for j in range(3):
                to_chip(i, j, x, y, c).start()

    def forward():
        x, y, c = _mesh_pos()
        for i in range(n):
            for j, (cx, cy) in enumerate(_other_chips(x, y)):
                got = part(i, outs[i].at[2 * cx + cy], c)
                _remote(got, got, send_sems, recv_sems, 6 * i + j, (cx, cy, c)).wait_recv()
                if halved[i]:
                    to_sibling(i, j, x, y, c).start()

    def drain():
        x, y, c = _mesh_pos()
        for i in range(n):
            for j, (cx, cy) in enumerate(_other_chips(x, y)):
                if halved[i]:
                    got = part(i, outs[i].at[2 * cx + cy], 1 - c)
                    _remote(got, got, send_sems, recv_sems, 6 * i + 3 + j, (x, y, 1 - c)).wait_recv()
                    to_sibling(i, j, x, y, c).wait_send()
                to_chip(i, j, x, y, c).wait_send()
            local(i, x, y).wait()

    return start, forward, drain


def _swap_halves(grads, whole, name):
    n, m = len(grads), len(grads) + len(whole)

    def body(*refs):
        ins, outs, send_sems, recv_sems = refs[:m], refs[m:2 * m], refs[2 * m], refs[2 * m + 1]
        x, y, c = _mesh_pos()
        cps = []
        for i in range(m):
            src = ins[i]
            if i < n:
                hr = grads[i].shape[1] // 2
                src = src.at[:, pl.ds((1 - c) * hr, hr), :]
            cp = _remote(src, outs[i], send_sems, recv_sems, i, (x, y, 1 - c))
            cp.start()
            cps.append(cp)
        for cp in cps:
            cp.wait()

    out_shape = [jax.ShapeDtypeStruct((g.shape[0], g.shape[1] // 2, g.shape[2]), F32) for g in grads]
    out_shape += [jax.ShapeDtypeStruct(w.shape, F32) for w in whole]
    outs = pl.pallas_call(
        body, name=name, in_specs=[_ANY] * m, out_specs=[_ANY] * m, out_shape=out_shape,
        scratch_shapes=[pltpu.SemaphoreType.DMA((m,)), pltpu.SemaphoreType.DMA((m,))],
    )(*grads, *whole)
    return outs[:n], outs[n:]


def _scattered_shapes(parts):
    return [jax.ShapeDtypeStruct(p.shape if p.ndim == 3 else (N_CHIPS,) + p.shape, p.dtype) for p in parts]


def _scatter_steps(ins, outs, send_sems, recv_sems, local_sems):
    n = len(ins)

    def src(i, k):
        return ins[i].at[k] if len(ins[i].shape) == 3 else ins[i]

    def sends(x, y, c):
        return [_remote(src(i, 2 * cx + cy), outs[i].at[2 * x + y], send_sems, recv_sems, 3 * i + j, (cx, cy, c))
                for i in range(n) for j, (cx, cy) in enumerate(_other_chips(x, y))]

    def local(i, x, y):
        return pltpu.make_async_copy(src(i, 2 * x + y), outs[i].at[2 * x + y], local_sems.at[i])

    def start():
        x, y, c = _mesh_pos()
        for i in range(n):
            local(i, x, y).start()
        for cp in sends(x, y, c):
            cp.start()

    def drain():
        x, y, c = _mesh_pos()
        for i in range(n):
            for j, (cx, cy) in enumerate(_other_chips(x, y)):
                got = outs[i].at[2 * cx + cy]
                _remote(got, got, send_sems, recv_sems, 3 * i + j, (cx, cy, c)).wait_recv()
        for cp in sends(x, y, c):
            cp.wait_send()
        for i in range(n):
            local(i, x, y).wait()

    return start, drain


def _add_pair(grads, from_sibling, small, small_sibling, c):
    n = len(grads)

    def body(c_ref, *refs):
        ins, outs = refs[:2 * n + 2], refs[2 * n + 2:]
        for i in range(n + 1):
            outs[i][...] = (ins[2 * i][...] + ins[2 * i + 1][...]).astype(outs[i].dtype)

    in_specs, out_specs, out_shape, args = [], [], [], []
    for g, r in zip(grads, from_sibling):
        _, hr, cols = r.shape
        in_specs += [pl.BlockSpec((1, hr, cols), lambda k, c_ref: (k, c_ref[0], 0)),
                     pl.BlockSpec((1, hr, cols), lambda k, c_ref: (k, 0, 0))]
        out_specs.append(pl.BlockSpec((1, hr, cols), lambda k, c_ref: (k, 0, 0)))
        out_shape.append(jax.ShapeDtypeStruct(r.shape, BF16))
        args += [g, r]
    whole = pl.BlockSpec(small.shape, lambda k, c_ref: (0, 0))
    in_specs += [whole, whole]
    out_specs.append(whole)
    out_shape.append(jax.ShapeDtypeStruct(small.shape, F32))
    outs = pl.pallas_call(
        body, name="add_pair", out_shape=out_shape,
        grid_spec=pltpu.PrefetchScalarGridSpec(num_scalar_prefetch=1, grid=(N_CHIPS,), in_specs=in_specs,
                                               out_specs=out_specs),
        compiler_params=_params(dimension_semantics=("arbitrary",)),
    )(c.reshape(1), *args, small, small_sibling)
    return outs[:n], outs[n]


def _scatter_to_chips(grad, from_sibling):
    hr = from_sibling.shape[1]

    def body(g_in, r_in, out, g_buf, r_buf, p_buf, load_sems, send_sems, recv_sems, local_sems):
        c = lax.axis_index("c")
        loads = (pltpu.make_async_copy(g_in.at[:, pl.ds(c * hr, hr), :], g_buf, load_sems.at[0]),
                 pltpu.make_async_copy(r_in, r_buf, load_sems.at[1]))
        for cp in loads:
            cp.start()
        for cp in loads:
            cp.wait()
        p_buf[...] = (g_buf[...] + r_buf[...]).astype(BF16)
        start, drain = _scatter_steps([p_buf], [out], send_sems, recv_sems, local_sems)
        start()
        drain()

    return pl.pallas_call(
        body, name="scatter_grads", in_specs=[_ANY] * 2, out_specs=_ANY,
        out_shape=jax.ShapeDtypeStruct(from_sibling.shape, BF16),
        scratch_shapes=[pltpu.VMEM(from_sibling.shape, F32)] * 2 + [pltpu.VMEM(from_sibling.shape, BF16)]
                       + [pltpu.SemaphoreType.DMA((2,)), pltpu.SemaphoreType.DMA((3,)), pltpu.SemaphoreType.DMA((3,)),
                          pltpu.SemaphoreType.DMA((1,))],
        compiler_params=_params(),
    )(grad, from_sibling)


def _share_halves(halves):
    n = len(halves)

    def body(*refs):
        ins, outs, send_sems, recv_sems = refs[:n], refs[n:2 * n], refs[2 * n], refs[2 * n + 1]
        x, y, c = _mesh_pos()
        cps = [_remote(ins[i], outs[i], send_sems, recv_sems, i, (x, y, 1 - c)) for i in range(n)]
        for cp in cps:
            cp.start()
        for cp in cps:
            cp.wait()

    return pl.pallas_call(
        body, name="share_halves", in_specs=[_ANY] * n, out_specs=[_ANY] * n,
        out_shape=[jax.ShapeDtypeStruct(h.shape, h.dtype) for h in halves],
        scratch_shapes=[pltpu.SemaphoreType.DMA((n,)), pltpu.SemaphoreType.DMA((n,))],
    )(*halves)


SHARD_COLS_IN = IN_TOTAL // N_CHIPS
KPE_END = Q_LORA + KV_LORA + ROPE


def _by_cols(a):
    return a.transpose(1, 0, 2).reshape(a.shape[1], N_CHIPS * a.shape[2])


def _assemble_early(c_in, c_uq, c_ukv, c_conv):
    w_in_e = jnp.concatenate([c_in[0][:, :KPE_END], jnp.zeros((D_MODEL, 64), BF16), c_in[0][:, KPE_END:],
                              c_in[1], c_in[2], c_in[3]], axis=1)
    return w_in_e, c_uq, c_ukv, _by_cols(c_conv).astype(F32)


def _assemble_late(c_o, c_pl, c_plg):
    return c_o.reshape(D_MODEL, D_MODEL), _by_cols(c_pl), c_plg.reshape(D_MODEL, D_MODEL)


def _split_w_in(dw_in_e):
    first = jnp.concatenate([dw_in_e[:, :KPE_END], dw_in_e[:, KPE_END + 64:SHARD_COLS_IN + 64]], axis=1)
    rest = [dw_in_e[:, SHARD_COLS_IN * k + 64:SHARD_COLS_IN * (k + 1) + 64] for k in range(1, N_CHIPS)]
    return jnp.stack([first] + rest)


def _split_others(dw_uq, dw_ukv, dw_o, dw_pl, dw_plg):
    chip_major = lambda a: a.reshape(a.shape[0], N_CHIPS, a.shape[1] // N_CHIPS).transpose(1, 0, 2)
    return [dw_uq[:, :, :QK_DIM], dw_ukv, dw_o.reshape(N_CHIPS, D_MODEL // N_CHIPS, D_MODEL),
            chip_major(dw_pl), dw_plg.reshape(N_CHIPS, D_MODEL // N_CHIPS, D_MODEL)]


def _local_step(x, p, pos, tgt, gains, early, late_shards, late_gathered, tm, tq):
    w_in_e, w_uq_e, w_ukv, conv_w = early
    g_in, g_cq, g_ckv, g_q, g_k, g_oa, g_oc, g_pl = gains
    T = x.shape[0]
    zpad = lambda a, n: jnp.concatenate([a, jnp.zeros(a.shape[:-1] + (n,), a.dtype)], axis=-1)
    gq, gk = zpad(g_q, HEAD_PAD - QK_DIM), zpad(g_k, HEAD_PAD - QK_DIM)
    inv_freq = 1.0 / (ROPE_THETA ** (jnp.arange(0, ROPE, 2, dtype=F32) / ROPE))
    invf = jnp.concatenate([inv_freq, inv_freq, jnp.zeros((64,), F32)]).reshape(1, LANES)
    sgn = jnp.concatenate([-jnp.ones((32,), F32), jnp.ones((32,), F32), jnp.zeros((64,), F32)]).reshape(1, LANES)

    (proj, q, k, v), gathered = _fwd_proj(x, pos, g_in, w_in_e, g_cq, w_uq_e, g_ckv, w_ukv, gq, gk, invf, sgn,
                                          late_shards, min(2 * tm, T))
    w_o, w_pl, w_plg = _assemble_late(*(gathered if late_shards else late_gathered))
    o, lse = _attn_fwd(q, k, v, tq)
    (dx1, do, delta, dtail, du, dw_o, dw_pl, dw_plg, dg_oa, dg_oc, dg_pl, dconv, loss) = _tail(
        x, o, proj, p, tgt, g_oa, g_oc, g_pl, conv_w, w_o, w_pl, w_plg, tm)
    dq, dk, dv = _attn_bwd(q, k, v, do, lse, delta, tq)
    (gx, h, dproj, dw_uq_e, dw_ukv, dg_in, dg_cq, dg_ckv, dgq, dgk) = _bwd_proj(
        x, dx1, pos, proj, dq, dk, dv, dtail, du, g_in, w_in_e, g_cq, w_uq_e, g_ckv, w_ukv, gq, gk, conv_w, invf, sgn, tm)
    wgrads = (dw_uq_e, dw_ukv, dw_o, dw_pl, dw_plg)
    ggrads = (dg_in, dg_cq, dg_ckv, dgq, dgk, dg_oa, dg_oc, dg_pl)
    return loss, gx, (h, dproj), wgrads, ggrads, dconv


def kernel(x, p, positions, g_in, w_in, g_cq, w_uq, g_ckv, w_ukv, g_q, g_k, conv_w, g_oa, g_oc, w_o, w_pl, w_plg, g_pl, loss_target, m_g_in, m_w_in, m_g_cq, m_w_uq, m_g_ckv, m_w_ukv, m_g_q, m_g_k, m_conv_w, m_g_oa, m_g_oc, m_w_o, m_w_pl, m_w_plg, m_g_pl, v_g_in, v_w_in, v_g_cq, v_w_uq, v_g_ckv, v_w_ukv, v_g_q, v_g_k, v_conv_w, v_g_oa, v_g_oc, v_w_o, v_w_pl, v_w_plg, v_g_pl):
    T = x.shape[1]
    c = lax.axis_index("c")
    chip = 2 * lax.axis_index("x") + lax.axis_index("y")
    gains = [g.reshape(1, -1) for g in (g_in, g_cq, g_ckv, g_q, g_k, g_oa, g_oc, g_pl)]

    early = _assemble_early(*_gather_weights([w_in[0], w_uq[0], w_ukv[0], conv_w[0]]))

    loss, gx, (h_t, dproj), wgrads, ggrads, dconv = _local_step(
        x[0], p[0, 0], positions.reshape(T, 1), loss_target[0], gains, early, [w_o[0], w_pl[0], w_plg[0]], None, 256, 512)

    others_cm = _split_others(*wgrads)
    small_parts = [a.reshape(-1, LANES) for a in (*ggrads, loss, dconv)]
    small_rows = [a.shape[0] for a in small_parts]
    tile_rows = [-(-r // 8) * 8 for r in small_rows]
    tile_rows[-1] += -sum(tile_rows) % 16
    small = jnp.concatenate([jnp.pad(a, ((0, t - r), (0, 0))) for a, r, t in zip(small_parts, small_rows, tile_rows)])
    from_sibling, (small_sibling,) = _swap_halves(others_cm, [small], "pair_grads")
    chip_parts, chip_small = _add_pair(others_cm, from_sibling, small, small_sibling, c)
    dw_in_e, exchanged = _matmul_acc(h_t, dproj, min(4096, T), 512, [*chip_parts, chip_small])
    w_in_cm = _split_w_in(dw_in_e)
    (w_in_sibling,), _ = _swap_halves([w_in_cm], [], "pair_w_in")
    by_chip = [_scatter_to_chips(w_in_cm, w_in_sibling), *exchanged[:-1]]
    halves, small_total = _add_chips(by_chip, exchanged[-1])
    other_halves = _share_halves(halves)

    gg, off = [], 0
    for rows, tiled in zip(small_rows, tile_rows):
        gg.append(small_total[off:off + rows].reshape(1, -1))
        off += tiled
    loss_out = gg[8][0, 0]
    conv_total = gg[9].reshape(3, CONV_W)
    conv_g = lax.dynamic_slice(conv_total, (0, chip * (CONV_W // N_CHIPS)), (3, CONV_W // N_CHIPS))
    g_by_name = dict(g_in=gg[0], g_cq=gg[1], g_ckv=gg[2], g_q=gg[3][:, :QK_DIM], g_k=gg[4][:, :QK_DIM], conv_w=conv_g,
                     g_oa=gg[5], g_oc=gg[6], g_pl=gg[7])
    half_by_name = dict(zip(("w_in", "w_uq", "w_ukv", "w_o", "w_pl", "w_plg"), zip(halves, other_halves)))
    weights = dict(g_in=g_in, w_in=w_in, g_cq=g_cq, w_uq=w_uq, g_ckv=g_ckv, w_ukv=w_ukv, g_q=g_q, g_k=g_k,
                   conv_w=conv_w, g_oa=g_oa, g_oc=g_oc, w_o=w_o, w_pl=w_pl, w_plg=w_plg, g_pl=g_pl)
    ms = dict(g_in=m_g_in, w_in=m_w_in, g_cq=m_g_cq, w_uq=m_w_uq, g_ckv=m_g_ckv, w_ukv=m_w_ukv, g_q=m_g_q, g_k=m_g_k,
              conv_w=m_conv_w, g_oa=m_g_oa, g_oc=m_g_oc, w_o=m_w_o, w_pl=m_w_pl, w_plg=m_w_plg, g_pl=m_g_pl)
    vs = dict(g_in=v_g_in, w_in=v_w_in, g_cq=v_g_cq, w_uq=v_w_uq, g_ckv=v_g_ckv, w_ukv=v_w_ukv, g_q=v_g_q, g_k=v_g_k,
              conv_w=v_conv_w, g_oa=v_g_oa, g_oc=v_g_oc, w_o=v_w_o, w_pl=v_w_pl, w_plg=v_w_plg, g_pl=v_g_pl)
    names = list(weights)
    flat = lambda a: a.reshape(-1, a.shape[-1])
    small_names = list(g_by_name)
    small_out = _adamw_small([flat(weights[n]) for n in small_names], [flat(g_by_name[n]) for n in small_names],
                             [flat(ms[n]) for n in small_names], [flat(vs[n]) for n in small_names])
    results = {n: (flat(g_by_name[n]), *(out[i] for out in small_out)) for i, n in enumerate(small_names)}
    for n in half_by_name:
        results[n] = _adamw_halves(flat(weights[n]), *half_by_name[n], flat(ms[n]), flat(vs[n]), c, "adamw_" + n)
    per_kind = [[results[n][kind].reshape(weights[n].shape) for n in names] for kind in range(4)]
    return (loss_out, gx.reshape(x.shape), *per_kind[0], *per_kind[1], *per_kind[2], *per_kind[3])
```

```python
import math

import jax
import jax.numpy as jnp
from jax import lax
from jax.experimental import pallas as pl
from jax.experimental.pallas import tpu as pltpu

F32 = jnp.float32
BF16 = jnp.bfloat16

D_MODEL = 1024
N_HEADS = 4
NOPE = 128
ROPE = 64
V_DIM = 128
QK_DIM = NOPE + ROPE
HEAD_PAD = 256
Q_LORA = 256
KV_LORA = 128
ATTN_W = 512
CONV_W = 512
PLE = 256
IN_TOTAL = 3008
PROJ_EXT = 3072
ROPE_THETA = 10000.0
EPS = 1e-6
SCALE = 1.0 / math.sqrt(QK_DIM)
LOG2E = math.log2(math.e)
EXP2_SCALE = SCALE * LOG2E
NEG = -1e30
SOFTMAX_ROWS = 32
SUB_TILE = 256

LR, B1, B2, ADAM_EPS, WD, STEP = 0.001, 0.9, 0.999, 1e-08, 0.01, 10

N_CHIPS = 4
LANES = 128
VMEM_LIMIT = 56 * 1024 * 1024
MESH = pl.DeviceIdType.MESH


def _params(**kw):
    return pltpu.CompilerParams(vmem_limit_bytes=VMEM_LIMIT, **kw)


def _inv_rms(x, n):
    return lax.rsqrt(jnp.sum(x * x, axis=-1, keepdims=True) / n + EPS)


def _lane_sum(a):
    folded = a[:, 0:LANES]
    for c0 in range(LANES, a.shape[1], LANES):
        folded = folded + a[:, c0:c0 + LANES]
    head = folded.astype(BF16)
    tail = (folded - head.astype(F32)).astype(BF16)
    return _dot(jnp.concatenate([head, tail], axis=1), jnp.ones((2 * LANES, LANES), BF16))


def _inv_rms_mxu(x):
    return lax.rsqrt(_lane_sum(x * x) / x.shape[1] + EPS)


def _rep(r, width):
    return r if width == LANES else jnp.tile(r, (1, width // LANES))


def _sigmoid(z):
    return jax.nn.sigmoid(z)


def _swap_rope_halves(b):
    lane = lax.broadcasted_iota(jnp.int32, b.shape, 1)
    swapped = jnp.where(lane < 32, pltpu.roll(b, 96, 1), pltpu.roll(b, 32, 1))
    return jnp.where(lane < ROPE, swapped, 0.0)


def _dot(a, b):
    return jnp.dot(a, b, preferred_element_type=F32)


def _dot_nt(a, b):
    return lax.dot_general(a, b, (((1,), (1,)), ((), ())), preferred_element_type=F32)


def _dot_tn(a, b):
    return lax.dot_general(a, b, (((0,), (0,)), ((), ())), preferred_element_type=F32)


def _colsum(a):
    return jnp.sum(a, axis=0, keepdims=True)


def _full(shape):
    return pl.BlockSpec(shape, lambda *_: (0,) * len(shape))


def _round_robin(chains, width):
    waiting, active = list(chains), []
    while waiting or active:
        while waiting and len(active) < width:
            active.append(waiting.pop(0))
        for chain in list(active):
            if next(chain, _DONE) is _DONE:
                active.remove(chain)


_DONE = object()


def _rope_tables(pos_ref, invf_ref, sgn_ref):
    ang = pos_ref[...].astype(F32) * invf_ref[...]
    return jnp.cos(ang), jnp.sin(ang) * sgn_ref[...]


def _fwd_proj(x, pos, g_in, w_in, g_cq, w_uq, g_ckv, w_ukv, gq, gk, invf, sgn, late_shards, tm):
    T = x.shape[0]
    nt = T // tm
    n_late = len(late_shards)
    ts = min(SUB_TILE, tm)

    def body(x_ref, pos_ref, g_in_ref, w_in_ref, g_cq_ref, w_uq_ref, g_ckv_ref, w_ukv_ref, gq_ref, gk_ref,
             invf_ref, sgn_ref, *rest):
        late_in, (proj_ref, q_ref, k_ref, v_ref) = rest[:n_late], rest[n_late:n_late + 4]
        late_out, late_scratch = rest[n_late + 4:2 * n_late + 4], rest[2 * n_late + 4:]
        i = pl.program_id(0)
        if n_late:
            start, forward, drain = _gather_steps([s.shape for s in late_shards], late_in, late_out,
                                                  late_scratch[:n_late], *late_scratch[n_late:])
            pl.when(i == 0)(start)
            pl.when(i == nt // 2)(forward)

        for r0 in range(0, tm, ts):
            rows = slice(r0, r0 + ts)
            xv = x_ref[rows, :]
            h = (xv * _rep(_inv_rms_mxu(xv), D_MODEL) * g_in_ref[...]).astype(BF16)
            lat = _dot(h, w_in_ref[:, 0:512])
            proj_ref[rows, 0:512] = lat
            c_q = lat[:, 0:Q_LORA]
            cqn = (c_q * _rep(_inv_rms_mxu(c_q), Q_LORA) * g_cq_ref[...]).astype(BF16)
            c_kv = lat[:, Q_LORA:Q_LORA + KV_LORA]
            ckvn = (c_kv * _inv_rms_mxu(c_kv) * g_ckv_ref[...]).astype(BF16)
            kpe = lat[:, 384:512]
            kpe_sq = kpe * kpe
            cos_b, sin_b = _rope_tables(pos_ref.at[rows, :], invf_ref, sgn_ref)
            gq_a, gq_b = gq_ref[:, 0:NOPE], gq_ref[:, NOPE:HEAD_PAD]
            gk_a, gk_b = gk_ref[:, 0:NOPE], gk_ref[:, NOPE:HEAD_PAD]

            def projections(rows=rows, h=h):
                for c0 in range(512, PROJ_EXT, 512):
                    proj_ref[rows, c0:c0 + 512] = _dot(h, w_in_ref[:, c0:c0 + 512])
                    yield

            def queries(hd, rows=rows, cqn=cqn, cos_b=cos_b, sin_b=sin_b, gq_a=gq_a, gq_b=gq_b):
                qh = _dot(cqn, w_uq_ref[hd])
                yield
                a, b = qh[:, 0:NOPE], qh[:, NOPE:HEAD_PAD]
                r = lax.rsqrt(_lane_sum(a * a + b * b) / QK_DIM + EPS)
                yield
                bn = b * r * gq_b
                q_ref[hd, rows, 0:NOPE] = (a * r * gq_a).astype(BF16)
                q_ref[hd, rows, NOPE:HEAD_PAD] = (bn * cos_b + _swap_rope_halves(bn) * sin_b).astype(BF16)
                yield

            def keys(hd, rows=rows, ckvn=ckvn, kpe=kpe, kpe_sq=kpe_sq, cos_b=cos_b, sin_b=sin_b, gk_a=gk_a, gk_b=gk_b):
                kvh = _dot(ckvn, w_ukv_ref[hd])
                yield
                ka = kvh[:, 0:NOPE]
                rk = lax.rsqrt(_lane_sum(ka * ka + kpe_sq) / QK_DIM + EPS)
                yield
                kbn = kpe * rk * gk_b
                k_ref[hd, rows, 0:NOPE] = (ka * rk * gk_a).astype(BF16)
                k_ref[hd, rows, NOPE:HEAD_PAD] = (kbn * cos_b + _swap_rope_halves(kbn) * sin_b).astype(BF16)
                v_ref[hd, rows, 0:V_DIM] = kvh[:, NOPE:HEAD_PAD].astype(BF16)
                v_ref[hd, rows, V_DIM:2 * V_DIM] = jnp.ones((ts, V_DIM), BF16)
                yield

            chains = [projections()]
            for hd in range(N_HEADS):
                chains += [queries(hd), keys(hd)]
            _round_robin(chains, 4)

        if n_late:
            pl.when(i == nt - 1)(drain)

    row = lambda i: (i, 0)
    head_rows = lambda i: (0, i, 0)
    outs = pl.pallas_call(
        body, name="fwd_proj", grid=(nt,),
        in_specs=[pl.BlockSpec((tm, D_MODEL), row), pl.BlockSpec((tm, 1), row), _full((1, D_MODEL)),
                  _full((D_MODEL, PROJ_EXT)), _full((1, Q_LORA)), _full((N_HEADS, Q_LORA, HEAD_PAD)),
                  _full((1, KV_LORA)), _full((N_HEADS, KV_LORA, HEAD_PAD)), _full((1, HEAD_PAD)), _full((1, HEAD_PAD)),
                  _full((1, LANES)), _full((1, LANES))] + [_full(s.shape) for s in late_shards],
        out_specs=[pl.BlockSpec((tm, PROJ_EXT), row), pl.BlockSpec((N_HEADS, tm, HEAD_PAD), head_rows),
                   pl.BlockSpec((N_HEADS, tm, HEAD_PAD), head_rows), pl.BlockSpec((N_HEADS, tm, 2 * V_DIM), head_rows)]
                  + [_ANY] * n_late,
        out_shape=[jax.ShapeDtypeStruct((T, PROJ_EXT), F32), jax.ShapeDtypeStruct((N_HEADS, T, HEAD_PAD), BF16),
                   jax.ShapeDtypeStruct((N_HEADS, T, HEAD_PAD), BF16), jax.ShapeDtypeStruct((N_HEADS, T, 2 * V_DIM), BF16)]
                  + _gathered_shapes(late_shards),
        scratch_shapes=_gather_scratch(late_shards) if n_late else [],
        compiler_params=_params(dimension_semantics=("arbitrary",)),
    )(x, pos, g_in, w_in, g_cq, w_uq, g_ckv, w_ukv, gq, gk, invf, sgn, *late_shards)
    return outs[:4], outs[4:]


def _chunk_pipeline(n_loop, lag, matmuls, pointwise, accumulate, last):
    slots = lag + 1

    def iteration(t, slot, pending=True):
        matmuls(jnp.minimum(t + lag, n_loop), (slot + lag) % slots)
        if pending:
            accumulate(t - lag, (slot + 1) % slots)
        pointwise(t, slot, False)

    def finish(slot, pending):
        for back in range(pending, 0, -1):
            accumulate(n_loop - back, (slot - back) % slots)
        pointwise(n_loop, slot, True)
        accumulate(n_loop, slot)
        last()

    for u in range(lag):
        matmuls(jnp.minimum(u, n_loop), u)
    for u in range(lag):
        pl.when(u < n_loop)(lambda u=u: iteration(u, u, pending=False))

    n_main = jnp.maximum(n_loop - lag, 0)

    def unrolled(tt, carry):
        for j in range(slots):
            iteration(lag + slots * tt + j, (lag + j) % slots)
        return carry

    lax.fori_loop(0, n_main // slots, unrolled, 0)
    rest = lax.rem(n_main, slots)
    t0 = n_loop - rest

    for r in range(slots):
        @pl.when(jnp.logical_and(n_loop >= lag, rest == r))
        def _():
            for j in range(r):
                iteration(t0 + j, (lag + j) % slots)
            finish((lag + r) % slots, lag)

    for short in range(lag):
        pl.when(n_loop == short)(lambda short=short: finish(short, short))


def _attn_fwd(q, k, v, tq):
    T = q.shape[1]
    tk = tq
    rc = min(SOFTMAX_ROWS, tq)

    def body(q_ref, k_ref, v_ref, o_ref, lse_ref, s0, s1, s2, p0, p1, p2, a0, a1, a2, m_ref, acc_ref):
        qi = pl.program_id(1)
        s_buf, p_buf, a_buf = (s0, s1, s2), (p0, p1, p2), (a0, a1, a2)

        def scores(t, slot):
            ks = pl.multiple_of(t * tk, tk)
            s_buf[slot][...] = _dot_nt(q_ref[0], k_ref[0, pl.ds(ks, tk), :])

        def values(t, slot):
            ks = pl.multiple_of(t * tk, tk)
            acc_ref[...] = acc_ref[...] * a_buf[slot][...] + _dot(p_buf[slot][...], v_ref[0, pl.ds(ks, tk), :])

        def softmax(t, slot, masked):
            s_all = s_buf[slot][...]
            if masked:
                row = lax.broadcasted_iota(jnp.int32, (tq, tk), 0)
                col = lax.broadcasted_iota(jnp.int32, (tq, tk), 1)
                s_all = jnp.where(col <= row, s_all, NEG)
                s_buf[slot][...] = s_all
            m_old = m_ref[...]
            m_new = jnp.maximum(m_old, jnp.max(s_all, axis=1, keepdims=True))
            a_buf[slot][...] = jnp.exp2((m_old - m_new) * EXP2_SCALE)
            m_ref[...] = m_new
            for r0 in range(0, tq, rc):
                s = s_buf[slot][r0:r0 + rc, :]
                p_buf[slot][r0:r0 + rc, :] = jnp.exp2((s - m_new[r0:r0 + rc, :]) * EXP2_SCALE).astype(BF16)

        def last():
            l = acc_ref[:, V_DIM:2 * V_DIM]
            o_ref[...] = acc_ref[:, 0:V_DIM] / l
            lse_ref[0] = (m_ref[...] * SCALE + jnp.log(l)).T[0:1, :]

        m_ref[...] = jnp.full_like(m_ref, NEG)
        acc_ref[...] = jnp.zeros_like(acc_ref)
        _chunk_pipeline(qi, 2, scores, softmax, values, last)

    return pl.pallas_call(
        body, name="attn_fwd", grid=(N_HEADS, T // tq),
        in_specs=[pl.BlockSpec((1, tq, HEAD_PAD), lambda h, i: (h, i, 0)),
                  pl.BlockSpec((1, T, HEAD_PAD), lambda h, i: (h, 0, 0)),
                  pl.BlockSpec((1, T, 2 * V_DIM), lambda h, i: (h, 0, 0))],
        out_specs=[pl.BlockSpec((tq, V_DIM), lambda h, i: (i, h)),
                   pl.BlockSpec((1, 1, tq), lambda h, i: (h, 0, i))],
        out_shape=[jax.ShapeDtypeStruct((T, ATTN_W), F32), jax.ShapeDtypeStruct((N_HEADS, 1, T), F32)],
        scratch_shapes=[pltpu.VMEM((tq, tk), F32)] * 3 + [pltpu.VMEM((tq, tk), BF16)] * 3
                       + [pltpu.VMEM((tq, 1), F32)] * 4 + [pltpu.VMEM((tq, 2 * V_DIM), F32)],
        compiler_params=_params(dimension_semantics=("arbitrary", "arbitrary")),
    )(q, k, v)


def _tail(x, o, proj, p, tgt, g_oa, g_oc, g_pl, conv_w, w_o, w_pl, w_plg, tm):
    T = x.shape[0]
    nt = T // tm

    def body(x_ref, o_ref, za_ref, cb_ref, cc_ref, cx_ref, zc_ref, cch_ref, cxh_ref, p_ref, tgt_ref,
             g_oa_ref, g_oc_ref, g_pl_ref, cw_ref, w_o_ref, w_pl_ref, w_plg_ref,
             dx1_ref, do_ref, delta_ref, dtail_ref, du_ref,
             dw_o_ref, dw_pl_ref, dw_plg_ref, dg_oa_ref, dg_oc_ref, dg_pl_ref, dcw_ref, loss_ref):
        i = pl.program_id(0)

        @pl.when(i == 0)
        def _():
            for r in (dw_o_ref, dw_pl_ref, dw_plg_ref, dg_oa_ref, dg_oc_ref, dg_pl_ref, dcw_ref, loss_ref):
                r[...] = jnp.zeros_like(r)

        g_oa, g_oc, g_pl = g_oa_ref[...], g_oc_ref[...], g_pl_ref[...]
        w0, w1, w2 = cw_ref[0:1, :], cw_ref[1:2, :], cw_ref[2:3, :]

        xv, ov, za, cb, zc = x_ref[...], o_ref[...], za_ref[...], cb_ref[...], zc_ref[...]
        pb = p_ref[...].astype(BF16)
        pp = _dot(pb, w_pl_ref[...])

        sa = _sigmoid(za)
        silu_a = za * sa
        ga = ov * silu_a
        ra = _inv_rms(ga, ATTN_W)
        xa = ga * ra
        ya = (xa * g_oa).astype(BF16)
        x1_a = _dot(ya, w_o_ref[0:ATTN_W, :])
        v = cc_ref[...] * cx_ref[...]
        not_first = jnp.where(i > 0, 1.0, 0.0)
        hv6 = cch_ref[6:7, :] * cxh_ref[6:7, :] * not_first
        hv7 = cch_ref[7:8, :] * cxh_ref[7:8, :] * not_first
        row = lax.broadcasted_iota(jnp.int32, v.shape, 0)
        v1 = jnp.where(row == 0, hv7, pltpu.roll(v, 1, 0))
        v2 = jnp.where(row == 0, hv6, jnp.where(row == 1, hv7, pltpu.roll(v, 2, 0)))
        u = w0 * v2 + w1 * v1 + w2 * v
        sc = _sigmoid(zc)
        silu_c = zc * sc
        gc = cb * u * silu_c
        rc = _inv_rms(gc, CONV_W)
        xc = gc * rc
        yc = (xc * g_oc).astype(BF16)
        x1 = xv + (x1_a + _dot(yc, w_o_ref[ATTN_W:D_MODEL, :]))
        r1 = _inv_rms(x1, D_MODEL)
        xh1 = x1 * r1
        n1 = (xh1 * g_pl).astype(BF16)
        gate = _sigmoid(_dot(n1, w_plg_ref[...]))
        err = x1 + gate * pp - tgt_ref[...]
        loss_ref[...] += 0.5 * jnp.sum(err * err) / D_MODEL
        dy = err / D_MODEL

        dpp = (dy * gate).astype(BF16)
        da = (dy * pp * gate * (1.0 - gate)).astype(BF16)
        dn1 = _dot_nt(da, w_plg_ref[...])
        dw_pl_ref[...] += _dot_tn(pb, dpp)
        dw_plg_ref[...] += _dot_tn(n1, da)
        dg_pl_ref[...] += _colsum(dn1 * xh1)
        dxh = dn1 * g_pl
        dx1 = dy + r1 * (dxh - xh1 * (jnp.sum(dxh * xh1, axis=-1, keepdims=True) / D_MODEL))
        dx1_ref[...] = dx1
        dx1b = dx1.astype(BF16)
        dya = _dot_nt(dx1b, w_o_ref[0:ATTN_W, :])
        dyc = _dot_nt(dx1b, w_o_ref[ATTN_W:D_MODEL, :])

        dw_o_ref[0:ATTN_W, :] += _dot_tn(ya, dx1b)
        dg_oa_ref[...] += _colsum(dya * xa)
        dxa = dya * g_oa
        dga = ra * (dxa - xa * (jnp.sum(dxa * xa, axis=-1, keepdims=True) / ATTN_W))
        do = (dga * silu_a).astype(BF16)
        do_ref[...] = do
        dof = do.astype(F32) * ov
        for hd in range(N_HEADS):
            delta_ref[hd] = _lane_sum(dof[:, hd * V_DIM:(hd + 1) * V_DIM]).T[0:1, :]
        dtail_ref[:, 0:512] = (dga * ov * (sa * (1.0 + za * (1.0 - sa)))).astype(BF16)

        dw_o_ref[ATTN_W:D_MODEL, :] += _dot_tn(yc, dx1b)
        dg_oc_ref[...] += _colsum(dyc * xc)
        dxc = dyc * g_oc
        dgc = rc * (dxc - xc * (jnp.sum(dxc * xc, axis=-1, keepdims=True) / CONV_W))
        dtail_ref[:, 512:1024] = (dgc * u * silu_c).astype(BF16)
        du = dgc * cb * silu_c
        du_ref[...] = du
        dtail_ref[:, 1024:1536] = (dgc * cb * u * (sc * (1.0 + zc * (1.0 - sc)))).astype(BF16)
        dcw_ref[0:1, :] += _colsum(du * v2)
        dcw_ref[1:2, :] += _colsum(du * v1)
        dcw_ref[2:3, :] += _colsum(du * v)

    row = lambda i: (i, 0)
    col = lambda c: (lambda i: (i, c))
    halo = lambda c: (lambda i: (jnp.maximum(i * (tm // 8) - 1, 0), c))
    in_specs = [pl.BlockSpec((tm, D_MODEL), row), pl.BlockSpec((tm, ATTN_W), row)]
    in_specs += [pl.BlockSpec((tm, 512), col(c)) for c in (1, 2, 3, 4, 5)]
    in_specs += [pl.BlockSpec((8, 512), halo(3)), pl.BlockSpec((8, 512), halo(4))]
    in_specs += [pl.BlockSpec((tm, PLE), row), pl.BlockSpec((tm, D_MODEL), row),
                 _full((1, ATTN_W)), _full((1, CONV_W)), _full((1, D_MODEL)), _full((3, CONV_W)),
                 _full((D_MODEL, D_MODEL)), _full((PLE, D_MODEL)), _full((D_MODEL, D_MODEL))]
    out_specs = [pl.BlockSpec((tm, D_MODEL), row), pl.BlockSpec((tm, ATTN_W), row),
                 pl.BlockSpec((N_HEADS, 1, tm), lambda i: (0, 0, i)), pl.BlockSpec((tm, 1536), row),
                 pl.BlockSpec((tm, CONV_W), row),
                 _full((D_MODEL, D_MODEL)), _full((PLE, D_MODEL)), _full((D_MODEL, D_MODEL)),
                 _full((1, ATTN_W)), _full((1, CONV_W)), _full((1, D_MODEL)), _full((3, CONV_W)), _full((1, LANES))]
    out_shape = [jax.ShapeDtypeStruct((T, D_MODEL), F32), jax.ShapeDtypeStruct((T, ATTN_W), BF16),
                 jax.ShapeDtypeStruct((N_HEADS, 1, T), F32), jax.ShapeDtypeStruct((T, 1536), BF16),
                 jax.ShapeDtypeStruct((T, CONV_W), F32),
                 jax.ShapeDtypeStruct((D_MODEL, D_MODEL), F32), jax.ShapeDtypeStruct((PLE, D_MODEL), F32),
                 jax.ShapeDtypeStruct((D_MODEL, D_MODEL), F32),
                 jax.ShapeDtypeStruct((1, ATTN_W), F32), jax.ShapeDtypeStruct((1, CONV_W), F32),
                 jax.ShapeDtypeStruct((1, D_MODEL), F32), jax.ShapeDtypeStruct((3, CONV_W), F32),
                 jax.ShapeDtypeStruct((1, LANES), F32)]
    return pl.pallas_call(
        body, name="tail", grid=(nt,), in_specs=in_specs, out_specs=out_specs, out_shape=out_shape,
        compiler_params=_params(dimension_semantics=("arbitrary",)),
    )(x, o, proj, proj, proj, proj, proj, proj, proj, p, tgt, g_oa, g_oc, g_pl, conv_w, w_o, w_pl, w_plg)


def _attn_bwd(q, k, v, do, lse_row, delta_row, tk):
    T = q.shape[1]
    tq = tk
    nq = T // tq
    rc = min(SOFTMAX_ROWS, tk)

    def body(q_ref, k_ref, v_ref, do_ref, lse_ref, dl_ref, dq_ref, dk_ref, dv_ref,
             s0, s1, d0, d1, p0, p1, g0, g1, dk_acc, dv_acc):
        kj = pl.program_id(1)
        s_buf, dp_buf, p_buf, g_buf = (s0, s1), (d0, d1), (p0, p1), (g0, g1)

        @pl.when(kj == 0)
        def _():
            dq_ref[...] = jnp.zeros_like(dq_ref)

        def q_start(t):
            return pl.multiple_of((nq - 1 - t) * tq, tq)

        def matmuls(t, slot):
            qs = q_start(t)
            s_buf[slot][...] = _dot_nt(k_ref[0], q_ref[0, pl.ds(qs, tq), :])
            dp_buf[slot][...] = _dot_nt(v_ref[0], do_ref[pl.ds(qs, tq), :])

        def pointwise(t, slot, masked):
            qs = q_start(t)
            lse2 = lse_ref[0, :, pl.ds(qs, tq)] * LOG2E
            dl = dl_ref[0, :, pl.ds(qs, tq)]
            for r0 in range(0, tk, rc):
                st = s_buf[slot][r0:r0 + rc, :]
                if masked:
                    row = lax.broadcasted_iota(jnp.int32, (rc, tq), 0)
                    col = lax.broadcasted_iota(jnp.int32, (rc, tq), 1)
                    st = jnp.where(row + r0 <= col, st, NEG)
                pt = jnp.exp2(st * EXP2_SCALE - lse2)
                p_buf[slot][r0:r0 + rc, :] = pt.astype(BF16)
                g_buf[slot][r0:r0 + rc, :] = (pt * (dp_buf[slot][r0:r0 + rc, :] - dl) * SCALE).astype(BF16)

        def accumulate(t, slot):
            qs = q_start(t)
            dv_acc[...] += _dot(p_buf[slot][...], do_ref[pl.ds(qs, tq), :])
            dk_acc[...] += _dot(g_buf[slot][...], q_ref[0, pl.ds(qs, tq), :])
            dq_ref[0, pl.ds(qs, tq), :] += _dot_tn(g_buf[slot][...], k_ref[0])

        def last():
            dk_ref[0] = dk_acc[...]
            dv_ref[0] = dv_acc[...]

        dk_acc[...] = jnp.zeros_like(dk_acc)
        dv_acc[...] = jnp.zeros_like(dv_acc)
        _chunk_pipeline(nq - 1 - kj, 1, matmuls, pointwise, accumulate, last)

    return pl.pallas_call(
        body, name="attn_bwd", grid=(N_HEADS, T // tk),
        in_specs=[pl.BlockSpec((1, T, HEAD_PAD), lambda h, j: (h, 0, 0)),
                  pl.BlockSpec((1, tk, HEAD_PAD), lambda h, j: (h, j, 0)),
                  pl.BlockSpec((1, tk, V_DIM), lambda h, j: (h, j, 0)),
                  pl.BlockSpec((T, V_DIM), lambda h, j: (0, h)),
                  pl.BlockSpec((1, 1, T), lambda h, j: (h, 0, 0)),
                  pl.BlockSpec((1, 1, T), lambda h, j: (h, 0, 0))],
        out_specs=[pl.BlockSpec((1, T, HEAD_PAD), lambda h, j: (h, 0, 0)),
                   pl.BlockSpec((1, tk, HEAD_PAD), lambda h, j: (h, j, 0)),
                   pl.BlockSpec((1, tk, V_DIM), lambda h, j: (h, j, 0))],
        out_shape=[jax.ShapeDtypeStruct((N_HEADS, T, HEAD_PAD), F32), jax.ShapeDtypeStruct((N_HEADS, T, HEAD_PAD), F32),
                   jax.ShapeDtypeStruct((N_HEADS, T, V_DIM), F32)],
        scratch_shapes=[pltpu.VMEM((tk, tq), F32)] * 4 + [pltpu.VMEM((tk, tq), BF16)] * 4
                       + [pltpu.VMEM((tk, HEAD_PAD), F32), pltpu.VMEM((tk, V_DIM), F32)],
        compiler_params=_params(dimension_semantics=("arbitrary", "arbitrary")),
    )(q, k, v, do, lse_row, delta_row)


def _bwd_proj(x, dx1, pos, proj, dq, dk, dv, dtail, du, g_in, w_in, g_cq, w_uq, g_ckv, w_ukv, gq, gk, conv_w,
              invf, sgn, tm):
    T = x.shape[0]
    nt = T // tm

    ts = min(SUB_TILE, tm)

    def body(x_ref, dx1_ref, pos_ref, lat_ref, cc_ref, cx_ref, dq_ref, dk_ref, dv_ref, dtail_ref, du_ref, dun_ref, *rest):
        consts, (gx_ref, h_ref, dproj_ref), sums = rest[:11], rest[11:14], rest[14:]
        cw_ref = consts[8]
        i = pl.program_id(0)

        @pl.when(i == 0)
        def _():
            for r in sums:
                r[...] = jnp.zeros_like(r)

        du_v = du_ref[...]
        not_last = jnp.where(i < nt - 1, 1.0, 0.0)
        nx0 = dun_ref[0:1, :] * not_last
        nx1 = dun_ref[1:2, :] * not_last
        row = lax.broadcasted_iota(jnp.int32, du_v.shape, 0)
        du1 = jnp.where(row == tm - 1, nx0, pltpu.roll(du_v, tm - 1, 0))
        du2 = jnp.where(row == tm - 2, nx0, jnp.where(row == tm - 1, nx1, pltpu.roll(du_v, tm - 2, 0)))
        dvc = cw_ref[2:3, :] * du_v + cw_ref[1:2, :] * du1 + cw_ref[0:1, :] * du2
        dproj_ref[:, 1536:2048] = (dvc * cx_ref[...]).astype(BF16)
        dproj_ref[:, 2048:2560] = (dvc * cc_ref[...]).astype(BF16)

        for r0 in range(0, tm, ts):
            rows = slice(r0, r0 + ts)
            work(x_ref.at[rows, :], dx1_ref.at[rows, :], pos_ref.at[rows, :], lat_ref.at[rows, :],
                 dq_ref.at[:, rows, :], dk_ref.at[:, rows, :], dv_ref.at[:, rows, :], dtail_ref.at[rows, :], *consts,
                 gx_ref.at[rows, :], h_ref.at[:, rows], dproj_ref.at[rows, :], *sums)

    def work(x_ref, dx1_ref, pos_ref, lat_ref, dq_ref, dk_ref, dv_ref, dtail_ref,
             g_in_ref, w_in_ref, g_cq_ref, w_uq_ref, g_ckv_ref, w_ukv_ref, gq_ref, gk_ref, cw_ref, invf_ref, sgn_ref,
             gx_ref, h_ref, dproj_ref, dw_uq_ref, dw_ukv_ref, dg_in_ref, dg_cq_ref, dg_ckv_ref, dgq_ref, dgk_ref):
        xv = x_ref[...]
        r0 = _rep(_inv_rms_mxu(xv), D_MODEL)
        xh0 = xv * r0
        g_in = g_in_ref[...]
        h_ref[...] = (xh0 * g_in).astype(BF16).T

        c_q = lat_ref[:, 0:Q_LORA]
        rq = _rep(_inv_rms_mxu(c_q), Q_LORA)
        xq = c_q * rq
        g_cq = g_cq_ref[...]
        cqn = (xq * g_cq).astype(BF16)
        c_kv = lat_ref[:, Q_LORA:Q_LORA + KV_LORA]
        rkv = _inv_rms_mxu(c_kv)
        xkv = c_kv * rkv
        g_ckv = g_ckv_ref[...]
        ckvn = (xkv * g_ckv).astype(BF16)
        kpe = lat_ref[:, 384:512]
        kpe_sq = kpe * kpe
        cos_b, sin_b = _rope_tables(pos_ref, invf_ref, sgn_ref)
        gq_a, gq_b = gq_ref[:, 0:NOPE], gq_ref[:, NOPE:HEAD_PAD]
        gk_a, gk_b = gk_ref[:, 0:NOPE], gk_ref[:, NOPE:HEAD_PAD]

        dproj_ref[:, 512:1536] = dtail_ref[:, 0:1024]
        dproj_ref[:, 2560:3072] = dtail_ref[:, 1024:1536]

        def dh_part(c0):
            return _dot_nt(dproj_ref[:, c0:c0 + 512], w_in_ref[:, c0:c0 + 512])

        later_chunks = ((512,), (1024,), (1536, 2048), (2560,))
        dh = jnp.zeros((ts, D_MODEL), F32)
        acc = dict(dh=dh, dkpe=jnp.zeros((ts, LANES), F32), dcqn=jnp.zeros((ts, Q_LORA), F32),
                   dckvn=jnp.zeros((ts, KV_LORA), F32))

        def dh_chunks():
            for chunks in later_chunks:
                for chunk in chunks:
                    acc["dh"] = acc["dh"] + dh_part(chunk)
                    yield

        def queries(hd):
            qh = _dot(cqn, w_uq_ref[hd])
            yield
            a, b = qh[:, 0:NOPE], qh[:, NOPE:HEAD_PAD]
            r = lax.rsqrt(_lane_sum(a * a + b * b) / QK_DIM + EPS)
            yield
            xa, xb = a * r, b * r
            dan = dq_ref[hd, :, 0:NOPE]
            dbr = dq_ref[hd, :, NOPE:HEAD_PAD]
            dbn = dbr * cos_b + _swap_rope_halves(dbr * sin_b)
            yield
            dgq_ref[:, 0:NOPE] += _colsum(dan * xa)
            dgq_ref[:, NOPE:HEAD_PAD] += _colsum(dbn * xb)
            dxa, dxb = dan * gq_a, dbn * gq_b
            cq = _lane_sum(dxa * xa + dxb * xb) / QK_DIM
            yield
            dqh = jnp.concatenate([r * (dxa - xa * cq), r * (dxb - xb * cq)], axis=-1).astype(BF16)
            yield
            dw_uq_ref[hd] += _dot_tn(cqn, dqh)
            yield
            acc["dcqn"] = acc["dcqn"] + _dot_nt(dqh, w_uq_ref[hd])
            yield

        def keys(hd):
            kvh = _dot(ckvn, w_ukv_ref[hd])
            yield
            ka = kvh[:, 0:NOPE]
            rk = lax.rsqrt(_lane_sum(ka * ka + kpe_sq) / QK_DIM + EPS)
            yield
            xka, xkb = ka * rk, kpe * rk
            dkan = dk_ref[hd, :, 0:NOPE]
            dkbr = dk_ref[hd, :, NOPE:HEAD_PAD]
            dkbn = dkbr * cos_b + _swap_rope_halves(dkbr * sin_b)
            yield
            dgk_ref[:, 0:NOPE] += _colsum(dkan * xka)
            dgk_ref[:, NOPE:HEAD_PAD] += _colsum(dkbn * xkb)
            dxka, dxkb = dkan * gk_a, dkbn * gk_b
            ck = _lane_sum(dxka * xka + dxkb * xkb) / QK_DIM
            yield
            acc["dkpe"] = acc["dkpe"] + rk * (dxkb - xkb * ck)
            dkvh = jnp.concatenate([rk * (dxka - xka * ck), dv_ref[hd]], axis=-1).astype(BF16)
            yield
            dw_ukv_ref[hd] += _dot_tn(ckvn, dkvh)
            yield
            acc["dckvn"] = acc["dckvn"] + _dot_nt(dkvh, w_ukv_ref[hd])
            yield

        chains = [dh_chunks()]
        for hd in range(N_HEADS):
            chains += [queries(hd), keys(hd)]
        _round_robin(chains, 5)
        dh, dkpe, dcqn, dckvn = acc["dh"], acc["dkpe"], acc["dcqn"], acc["dckvn"]

        dg_cq_ref[...] += _colsum(dcqn * xq)
        dxq = dcqn * g_cq
        dproj_ref[:, 0:Q_LORA] = (rq * (dxq - xq * _rep(_lane_sum(dxq * xq) / Q_LORA, Q_LORA))).astype(BF16)
        dg_ckv_ref[...] += _colsum(dckvn * xkv)
        dxkv = dckvn * g_ckv
        dproj_ref[:, 256:384] = (rkv * (dxkv - xkv * (_lane_sum(dxkv * xkv) / KV_LORA))).astype(BF16)
        dproj_ref[:, 384:512] = dkpe.astype(BF16)
        dh = dh + dh_part(0)
        dg_in_ref[...] += _colsum(dh * xh0)
        dxh = dh * g_in
        gx_ref[...] = dx1_ref[...] + r0 * (dxh - xh0 * _rep(_lane_sum(dxh * xh0) / D_MODEL, D_MODEL))

    row = lambda i: (i, 0)
    col = lambda c: (lambda i: (i, c))
    head_rows = lambda i: (0, i, 0)
    nxt = lambda i: (jnp.minimum((i + 1) * (tm // 8), T // 8 - 1), 0)
    in_specs = [pl.BlockSpec((tm, D_MODEL), row), pl.BlockSpec((tm, D_MODEL), row), pl.BlockSpec((tm, 1), row),
                pl.BlockSpec((tm, 512), col(0)), pl.BlockSpec((tm, 512), col(3)), pl.BlockSpec((tm, 512), col(4)),
                pl.BlockSpec((N_HEADS, tm, HEAD_PAD), head_rows), pl.BlockSpec((N_HEADS, tm, HEAD_PAD), head_rows),
                pl.BlockSpec((N_HEADS, tm, V_DIM), head_rows), pl.BlockSpec((tm, 1536), row),
                pl.BlockSpec((tm, CONV_W), row), pl.BlockSpec((8, CONV_W), nxt),
                _full((1, D_MODEL)), _full((D_MODEL, PROJ_EXT)), _full((1, Q_LORA)), _full((N_HEADS, Q_LORA, HEAD_PAD)),
                _full((1, KV_LORA)), _full((N_HEADS, KV_LORA, HEAD_PAD)), _full((1, HEAD_PAD)), _full((1, HEAD_PAD)),
                _full((3, CONV_W)), _full((1, LANES)), _full((1, LANES))]
    out_specs = [pl.BlockSpec((tm, D_MODEL), row), pl.BlockSpec((D_MODEL, tm), lambda i: (0, i)),
                 pl.BlockSpec((tm, PROJ_EXT), row),
                 _full((N_HEADS, Q_LORA, HEAD_PAD)), _full((N_HEADS, KV_LORA, HEAD_PAD)),
                 _full((1, D_MODEL)), _full((1, Q_LORA)), _full((1, KV_LORA)), _full((1, HEAD_PAD)), _full((1, HEAD_PAD))]
    out_shape = [jax.ShapeDtypeStruct((T, D_MODEL), F32), jax.ShapeDtypeStruct((D_MODEL, T), BF16),
                 jax.ShapeDtypeStruct((T, PROJ_EXT), BF16),
                 jax.ShapeDtypeStruct((N_HEADS, Q_LORA, HEAD_PAD), F32), jax.ShapeDtypeStruct((N_HEADS, KV_LORA, HEAD_PAD), F32),
                 jax.ShapeDtypeStruct((1, D_MODEL), F32), jax.ShapeDtypeStruct((1, Q_LORA), F32),
                 jax.ShapeDtypeStruct((1, KV_LORA), F32), jax.ShapeDtypeStruct((1, HEAD_PAD), F32),
                 jax.ShapeDtypeStruct((1, HEAD_PAD), F32)]
    return pl.pallas_call(
        body, name="bwd_proj", grid=(nt,), in_specs=in_specs, out_specs=out_specs, out_shape=out_shape,
        compiler_params=_params(dimension_semantics=("arbitrary",)),
    )(x, dx1, pos, proj, proj, proj, dq, dk, dv, dtail, du, du, g_in, w_in, g_cq, w_uq, g_ckv, w_ukv, gq, gk, conv_w,
      invf, sgn)


def _matmul_acc(a, b, tt, tn, parts):
    M, T = a.shape
    N = b.shape[1]
    n = len(parts)
    grid = (N // tn, T // tt)

    def body(a_ref, b_ref, *rest):
        part_refs, o_ref, out_refs, sems = rest[:n], rest[n], rest[n + 1:2 * n + 1], rest[2 * n + 1:]
        j, t = pl.program_id(0), pl.program_id(1)
        if n:
            start, drain = _scatter_steps(part_refs, out_refs, *sems)
            pl.when(jnp.logical_and(j == 0, t == 0))(start)

        @pl.when(t == 0)
        def _():
            o_ref[...] = jnp.zeros_like(o_ref)

        o_ref[...] += _dot(a_ref[...], b_ref[...])
        if n:
            pl.when(jnp.logical_and(j == grid[0] - 1, t == grid[1] - 1))(drain)

    sems = [pltpu.SemaphoreType.DMA((3 * n,)), pltpu.SemaphoreType.DMA((3 * n,)), pltpu.SemaphoreType.DMA((n,))]
    outs = pl.pallas_call(
        body, name="dw_in", grid=grid,
        in_specs=[pl.BlockSpec((M, tt), lambda j, t: (0, t)), pl.BlockSpec((tt, tn), lambda j, t: (t, j))] + [_ANY] * n,
        out_specs=[pl.BlockSpec((M, tn), lambda j, t: (0, j))] + [_ANY] * n,
        out_shape=[jax.ShapeDtypeStruct((M, N), F32)] + _scattered_shapes(parts),
        scratch_shapes=sems if n else [],
        compiler_params=_params(dimension_semantics=("arbitrary", "arbitrary")),
    )(a, b, *parts)
    return outs[0], outs[1:]


def _add_chips(parts, small_parts):
    arrays = list(parts) + [small_parts]

    def body(*refs):
        ins, outs = refs[:len(arrays)], refs[len(arrays):]
        for a_ref, o_ref in zip(ins, outs):
            part = lambda k: a_ref[k].astype(F32)
            o_ref[...] = ((part(0) + part(1)) + part(2)) + part(3)

    in_specs, out_specs, out_shape = [], [], []
    for a in arrays:
        _, rows, cols = a.shape
        in_specs.append(pl.BlockSpec((N_CHIPS, rows // 2, cols), lambda i: (0, i, 0)))
        out_specs.append(pl.BlockSpec((rows // 2, cols), lambda i: (i, 0)))
        out_shape.append(jax.ShapeDtypeStruct((rows, cols), F32))
    outs = pl.pallas_call(body, name="add_chips", grid=(2,), in_specs=in_specs, out_specs=out_specs,
                          out_shape=out_shape, compiler_params=_params(dimension_semantics=("arbitrary",)))(*arrays)
    return outs[:-1], outs[-1]


def _adamw_small(ws, gs, ms, vs):
    n = len(ws)

    def body(*refs):
        for i in range(n):
            w_ref, g_ref, m_ref, v_ref = (refs[k * n + i] for k in range(4))
            d_ref, nm_ref, nv_ref = (refs[(4 + k) * n + i] for k in range(3))
            _adamw_math(g_ref[...], w_ref, m_ref, v_ref, d_ref, nm_ref, nv_ref)

    shapes = [jax.ShapeDtypeStruct(w.shape, F32) for w in ws]
    outs = pl.pallas_call(body, name="adamw_small", out_shape=shapes * 3)(*ws, *gs, *ms, *vs)
    return outs[:n], outs[n:2 * n], outs[2 * n:]


def _adamw_math(gv, w_ref, m_ref, v_ref, d_ref, nm_ref, nv_ref):
    nm = B1 * m_ref[...] + (1.0 - B1) * gv
    nv = B2 * v_ref[...] + (1.0 - B2) * (gv * gv)
    m_hat = nm / (1.0 - B1 ** STEP)
    v_hat = nv / (1.0 - B2 ** STEP)
    d_ref[...] = -LR * (m_hat / (jnp.sqrt(v_hat) + ADAM_EPS) + WD * w_ref[...])
    nm_ref[...] = nm
    nv_ref[...] = nv


def _adamw_halves(w, mine, other, m, v, c, name):
    hr, cols = mine.shape

    def body(c_ref, w_ref, mine_ref, other_ref, m_ref, v_ref, g_ref, d_ref, nm_ref, nv_ref):
        gv = jnp.where(pl.program_id(0) == c_ref[0], mine_ref[...], other_ref[...])
        g_ref[...] = gv
        _adamw_math(gv, w_ref, m_ref, v_ref, d_ref, nm_ref, nv_ref)

    half = pl.BlockSpec((hr, cols), lambda i, c_ref: (i, 0))
    whole = pl.BlockSpec((hr, cols), lambda i, c_ref: (0, 0))
    shp = jax.ShapeDtypeStruct(w.shape, F32)
    return pl.pallas_call(
        body, name=name, out_shape=[shp] * 4,
        grid_spec=pltpu.PrefetchScalarGridSpec(num_scalar_prefetch=1, grid=(2,), in_specs=[half, whole, whole, half, half],
                                               out_specs=[half] * 4),
        compiler_params=_params(dimension_semantics=("arbitrary",)),
    )(c.reshape(1), w, mine, other, m, v)


_ANY = pl.BlockSpec(memory_space=pl.ANY)


def _mesh_pos():
    return lax.axis_index("x"), lax.axis_index("y"), lax.axis_index("c")


def _other_chips(x, y):
    return [(1 - x, y), (x, 1 - y), (1 - x, 1 - y)]


def _remote(src, dst, send_sems, recv_sems, k, to):
    return pltpu.make_async_remote_copy(src_ref=src, dst_ref=dst, send_sem=send_sems.at[k], recv_sem=recv_sems.at[k],
                                        device_id=to, device_id_type=MESH)


def _gather_weights(shards):
    n = len(shards)

    def body(*refs):
        start, forward, drain = _gather_steps([s.shape for s in shards], refs[:n], refs[n:2 * n], refs[2 * n:3 * n],
                                              *refs[3 * n:])
        start()
        forward()
        drain()

    vmem = pl.BlockSpec(memory_space=pltpu.VMEM)
    return pl.pallas_call(
        body, name="gather_weights", in_specs=[vmem] * n, out_specs=[_ANY] * n,
        out_shape=_gathered_shapes(shards), scratch_shapes=_gather_scratch(shards), compiler_params=_params(),
    )(*shards)


def _travel_shape(shard):
    rows, cols = shard.shape
    return (rows, HEAD_PAD if cols == QK_DIM else cols)


def _gathered_shapes(shards):
    return [jax.ShapeDtypeStruct((N_CHIPS,) + _travel_shape(s), BF16) for s in shards]


def _gather_scratch(shards):
    n = len(shards)
    return ([pltpu.VMEM(_travel_shape(s), BF16) for s in shards]
            + [pltpu.SemaphoreType.DMA((6 * n,)), pltpu.SemaphoreType.DMA((6 * n,)), pltpu.SemaphoreType.DMA((n,))])


def _gather_steps(shapes, ins, outs, stage, send_sems, recv_sems, local_sems):
    n = len(shapes)
    halved = [s[0] % 32 == 0 for s in shapes]

    def part(i, ref, hc):
        if not halved[i]:
            return ref
        hr = shapes[i][0] // 2
        return ref.at[pl.ds(hc * hr, hr), :]

    def to_chip(i, j, x, y, c):
        cx, cy = _other_chips(x, y)[j]
        return _remote(part(i, stage[i], c), part(i, outs[i].at[2 * x + y], c), send_sems, recv_sems, 6 * i + j, (cx, cy, c))

    def to_sibling(i, j, x, y, c):
        cx, cy = _other_chips(x, y)[j]
        got = part(i, outs[i].at[2 * cx + cy], c)
        return _remote(got, got, send_sems, recv_sems, 6 * i + 3 + j, (x, y, 1 - c))

    def local(i, x, y):
        return pltpu.make_async_copy(stage[i], outs[i].at[2 * x + y], local_sems.at[i])

    def start():
        x, y, c = _mesh_pos()
        for i in range(n):
            cols = ins[i].shape[1]
            if stage[i].shape[1] != cols:
                stage[i][...] = jnp.zeros_like(stage[i])
            stage[i][:, 0:cols] = ins[i][...].astype(BF16)
            local(i, x, y).start()
            for j in range(3):
                to_chip(i, j, x, y, c).start()

    def forward():
        x, y, c = _mesh_pos()
        for i in range(n):
            for j, (cx, cy) in enumerate(_other_chips(x, y)):
                got = part(i, outs[i].at[2 * cx + cy], c)
                _remote(got, got, send_sems, recv_sems, 6 * i + j, (cx, cy, c)).wait_recv()
                if halved[i]:
                    to_sibling(i, j, x, y, c).start()

    def drain():
        x, y, c = _mesh_pos()
        for i in range(n):
            for j, (cx, cy) in enumerate(_other_chips(x, y)):
                if halved[i]:
                    got = part(i, outs[i].at[2 * cx + cy], 1 - c)
                    _remote(got, got, send_sems, recv_sems, 6 * i + 3 + j, (x, y, 1 - c)).wait_recv()
                    to_sibling(i, j, x, y, c).wait_send()
                to_chip(i, j, x, y, c).wait_send()
            local(i, x, y).wait()

    return start, forward, drain


def _swap_halves(grads, whole, name):
    n, m = len(grads), len(grads) + len(whole)

    def body(*refs):
        ins, outs, send_sems, recv_sems = refs[:m], refs[m:2 * m], refs[2 * m], refs[2 * m + 1]
        x, y, c = _mesh_pos()
        cps = []
        for i in range(m):
            src = ins[i]
            if i < n:
                hr = grads[i].shape[1] // 2
                src = src.at[:, pl.ds((1 - c) * hr, hr), :]
            cp = _remote(src, outs[i], send_sems, recv_sems, i, (x, y, 1 - c))
            cp.start()
            cps.append(cp)
        for cp in cps:
            cp.wait()

    out_shape = [jax.ShapeDtypeStruct((g.shape[0], g.shape[1] // 2, g.shape[2]), F32) for g in grads]
    out_shape += [jax.ShapeDtypeStruct(w.shape, F32) for w in whole]
    outs = pl.pallas_call(
        body, name=name, in_specs=[_ANY] * m, out_specs=[_ANY] * m, out_shape=out_shape,
        scratch_shapes=[pltpu.SemaphoreType.DMA((m,)), pltpu.SemaphoreType.DMA((m,))],
    )(*grads, *whole)
    return outs[:n], outs[n:]


def _scattered_shapes(parts):
    return [jax.ShapeDtypeStruct(p.shape if p.ndim == 3 else (N_CHIPS,) + p.shape, p.dtype) for p in parts]


def _scatter_steps(ins, outs, send_sems, recv_sems, local_sems):
    n = len(ins)

    def src(i, k):
        return ins[i].at[k] if len(ins[i].shape) == 3 else ins[i]

    def sends(x, y, c):
        return [_remote(src(i, 2 * cx + cy), outs[i].at[2 * x + y], send_sems, recv_sems, 3 * i + j, (cx, cy, c))
                for i in range(n) for j, (cx, cy) in enumerate(_other_chips(x, y))]

    def local(i, x, y):
        return pltpu.make_async_copy(src(i, 2 * x + y), outs[i].at[2 * x + y], local_sems.at[i])

    def start():
        x, y, c = _mesh_pos()
        for i in range(n):
            local(i, x, y).start()
        for cp in sends(x, y, c):
            cp.start()

    def drain():
        x, y, c = _mesh_pos()
        for i in range(n):
            for j, (cx, cy) in enumerate(_other_chips(x, y)):
                got = outs[i].at[2 * cx + cy]
                _remote(got, got, send_sems, recv_sems, 3 * i + j, (cx, cy, c)).wait_recv()
        for cp in sends(x, y, c):
            cp.wait_send()
        for i in range(n):
            local(i, x, y).wait()

    return start, drain


def _add_pair(grads, from_sibling, small, small_sibling, c):
    n = len(grads)

    def body(c_ref, *refs):
        ins, outs = refs[:2 * n + 2], refs[2 * n + 2:]
        for i in range(n + 1):
            outs[i][...] = (ins[2 * i][...] + ins[2 * i + 1][...]).astype(outs[i].dtype)

    in_specs, out_specs, out_shape, args = [], [], [], []
    for g, r in zip(grads, from_sibling):
        _, hr, cols = r.shape
        in_specs += [pl.BlockSpec((1, hr, cols), lambda k, c_ref: (k, c_ref[0], 0)),
                     pl.BlockSpec((1, hr, cols), lambda k, c_ref: (k, 0, 0))]
        out_specs.append(pl.BlockSpec((1, hr, cols), lambda k, c_ref: (k, 0, 0)))
        out_shape.append(jax.ShapeDtypeStruct(r.shape, BF16))
        args += [g, r]
    whole = pl.BlockSpec(small.shape, lambda k, c_ref: (0, 0))
    in_specs += [whole, whole]
    out_specs.append(whole)
    out_shape.append(jax.ShapeDtypeStruct(small.shape, F32))
    outs = pl.pallas_call(
        body, name="add_pair", out_shape=out_shape,
        grid_spec=pltpu.PrefetchScalarGridSpec(num_scalar_prefetch=1, grid=(N_CHIPS,), in_specs=in_specs,
                                               out_specs=out_specs),
        compiler_params=_params(dimension_semantics=("arbitrary",)),
    )(c.reshape(1), *args, small, small_sibling)
    return outs[:n], outs[n]


def _scatter_to_chips(grad, from_sibling):
    hr = from_sibling.shape[1]

    def body(g_in, r_in, out, g_buf, r_buf, p_buf, load_sems, send_sems, recv_sems, local_sems):
        c = lax.axis_index("c")
        loads = (pltpu.make_async_copy(g_in.at[:, pl.ds(c * hr, hr), :], g_buf, load_sems.at[0]),
                 pltpu.make_async_copy(r_in, r_buf, load_sems.at[1]))
        for cp in loads:
            cp.start()
        for cp in loads:
            cp.wait()
        p_buf[...] = (g_buf[...] + r_buf[...]).astype(BF16)
        start, drain = _scatter_steps([p_buf], [out], send_sems, recv_sems, local_sems)
        start()
        drain()

    return pl.pallas_call(
        body, name="scatter_grads", in_specs=[_ANY] * 2, out_specs=_ANY,
        out_shape=jax.ShapeDtypeStruct(from_sibling.shape, BF16),
        scratch_shapes=[pltpu.VMEM(from_sibling.shape, F32)] * 2 + [pltpu.VMEM(from_sibling.shape, BF16)]
                       + [pltpu.SemaphoreType.DMA((2,)), pltpu.SemaphoreType.DMA((3,)), pltpu.SemaphoreType.DMA((3,)),
                          pltpu.SemaphoreType.DMA((1,))],
        compiler_params=_params(),
    )(grad, from_sibling)


def _share_halves(halves):
    n = len(halves)

    def body(*refs):
        ins, outs, send_sems, recv_sems = refs[:n], refs[n:2 * n], refs[2 * n], refs[2 * n + 1]
        x, y, c = _mesh_pos()
        cps = [_remote(ins[i], outs[i], send_sems, recv_sems, i, (x, y, 1 - c)) for i in range(n)]
        for cp in cps:
            cp.start()
        for cp in cps:
            cp.wait()

    return pl.pallas_call(
        body, name="share_halves", in_specs=[_ANY] * n, out_specs=[_ANY] * n,
        out_shape=[jax.ShapeDtypeStruct(h.shape, h.dtype) for h in halves],
        scratch_shapes=[pltpu.SemaphoreType.DMA((n,)), pltpu.SemaphoreType.DMA((n,))],
    )(*halves)


SHARD_COLS_IN = IN_TOTAL // N_CHIPS
KPE_END = Q_LORA + KV_LORA + ROPE


def _by_cols(a):
    return a.transpose(1, 0, 2).reshape(a.shape[1], N_CHIPS * a.shape[2])


def _assemble_early(c_in, c_uq, c_ukv, c_conv):
    w_in_e = jnp.concatenate([c_in[0][:, :KPE_END], jnp.zeros((D_MODEL, 64), BF16), c_in[0][:, KPE_END:],
                              c_in[1], c_in[2], c_in[3]], axis=1)
    return w_in_e, c_uq, c_ukv, _by_cols(c_conv).astype(F32)


def _assemble_late(c_o, c_pl, c_plg):
    return c_o.reshape(D_MODEL, D_MODEL), _by_cols(c_pl), c_plg.reshape(D_MODEL, D_MODEL)


def _split_w_in(dw_in_e):
    first = jnp.concatenate([dw_in_e[:, :KPE_END], dw_in_e[:, KPE_END + 64:SHARD_COLS_IN + 64]], axis=1)
    rest = [dw_in_e[:, SHARD_COLS_IN * k + 64:SHARD_COLS_IN * (k + 1) + 64] for k in range(1, N_CHIPS)]
    return jnp.stack([first] + rest)


def _split_others(dw_uq, dw_ukv, dw_o, dw_pl, dw_plg):
    chip_major = lambda a: a.reshape(a.shape[0], N_CHIPS, a.shape[1] // N_CHIPS).transpose(1, 0, 2)
    return [dw_uq[:, :, :QK_DIM], dw_ukv, dw_o.reshape(N_CHIPS, D_MODEL // N_CHIPS, D_MODEL),
            chip_major(dw_pl), dw_plg.reshape(N_CHIPS, D_MODEL // N_CHIPS, D_MODEL)]


def _local_step(x, p, pos, tgt, gains, early, late_shards, late_gathered, tm, tq):
    w_in_e, w_uq_e, w_ukv, conv_w = early
    g_in, g_cq, g_ckv, g_q, g_k, g_oa, g_oc, g_pl = gains
    T = x.shape[0]
    zpad = lambda a, n: jnp.concatenate([a, jnp.zeros(a.shape[:-1] + (n,), a.dtype)], axis=-1)
    gq, gk = zpad(g_q, HEAD_PAD - QK_DIM), zpad(g_k, HEAD_PAD - QK_DIM)
    inv_freq = 1.0 / (ROPE_THETA ** (jnp.arange(0, ROPE, 2, dtype=F32) / ROPE))
    invf = jnp.concatenate([inv_freq, inv_freq, jnp.zeros((64,), F32)]).reshape(1, LANES)
    sgn = jnp.concatenate([-jnp.ones((32,), F32), jnp.ones((32,), F32), jnp.zeros((64,), F32)]).reshape(1, LANES)

    (proj, q, k, v), gathered = _fwd_proj(x, pos, g_in, w_in_e, g_cq, w_uq_e, g_ckv, w_ukv, gq, gk, invf, sgn,
                                          late_shards, min(2 * tm, T))
    w_o, w_pl, w_plg = _assemble_late(*(gathered if late_shards else late_gathered))
    o, lse = _attn_fwd(q, k, v, tq)
    (dx1, do, delta, dtail, du, dw_o, dw_pl, dw_plg, dg_oa, dg_oc, dg_pl, dconv, loss) = _tail(
        x, o, proj, p, tgt, g_oa, g_oc, g_pl, conv_w, w_o, w_pl, w_plg, tm)
    dq, dk, dv = _attn_bwd(q, k, v, do, lse, delta, tq)
    (gx, h, dproj, dw_uq_e, dw_ukv, dg_in, dg_cq, dg_ckv, dgq, dgk) = _bwd_proj(
        x, dx1, pos, proj, dq, dk, dv, dtail, du, g_in, w_in_e, g_cq, w_uq_e, g_ckv, w_ukv, gq, gk, conv_w, invf, sgn, tm)
    wgrads = (dw_uq_e, dw_ukv, dw_o, dw_pl, dw_plg)
    ggrads = (dg_in, dg_cq, dg_ckv, dgq, dgk, dg_oa, dg_oc, dg_pl)
    return loss, gx, (h, dproj), wgrads, ggrads, dconv


def kernel(x, p, positions, g_in, w_in, g_cq, w_uq, g_ckv, w_ukv, g_q, g_k, conv_w, g_oa, g_oc, w_o, w_pl, w_plg, g_pl, loss_target, m_g_in, m_w_in, m_g_cq, m_w_uq, m_g_ckv, m_w_ukv, m_g_q, m_g_k, m_conv_w, m_g_oa, m_g_oc, m_w_o, m_w_pl, m_w_plg, m_g_pl, v_g_in, v_w_in, v_g_cq, v_w_uq, v_g_ckv, v_w_ukv, v_g_q, v_g_k, v_conv_w, v_g_oa, v_g_oc, v_w_o, v_w_pl, v_w_plg, v_g_pl):
    T = x.shape[1]
    c = lax.axis_index("c")
    chip = 2 * lax.axis_index("x") + lax.axis_index("y")
    gains = [g.reshape(1, -1) for g in (g_in, g_cq, g_ckv, g_q, g_k, g_oa, g_oc, g_pl)]

    early = _assemble_early(*_gather_weights([w_in[0], w_uq[0], w_ukv[0], conv_w[0]]))

    loss, gx, (h_t, dproj), wgrads, ggrads, dconv = _local_step(
        x[0], p[0, 0], positions.reshape(T, 1), loss_target[0], gains, early, [w_o[0], w_pl[0], w_plg[0]], None, 256, 512)

    others_cm = _split_others(*wgrads)
    small_parts = [a.reshape(-1, LANES) for a in (*ggrads, loss, dconv)]
    small_rows = [a.shape[0] for a in small_parts]
    tile_rows = [-(-r // 8) * 8 for r in small_rows]
    tile_rows[-1] += -sum(tile_rows) % 16
    small = jnp.concatenate([jnp.pad(a, ((0, t - r), (0, 0))) for a, r, t in zip(small_parts, small_rows, tile_rows)])
    from_sibling, (small_sibling,) = _swap_halves(others_cm, [small], "pair_grads")
    chip_parts, chip_small = _add_pair(others_cm, from_sibling, small, small_sibling, c)
    dw_in_e, exchanged = _matmul_acc(h_t, dproj, min(4096, T), 512, [*chip_parts, chip_small])
    w_in_cm = _split_w_in(dw_in_e)
    (w_in_sibling,), _ = _swap_halves([w_in_cm], [], "pair_w_in")
    by_chip = [_scatter_to_chips(w_in_cm, w_in_sibling), *exchanged[:-1]]
    halves, small_total = _add_chips(by_chip, exchanged[-1])
    other_halves = _share_halves(halves)

    gg, off = [], 0
    for rows, tiled in zip(small_rows, tile_rows):
        gg.append(small_total[off:off + rows].reshape(1, -1))
        off += tiled
    loss_out = gg[8][0, 0]
    conv_total = gg[9].reshape(3, CONV_W)
    conv_g = lax.dynamic_slice(conv_total, (0, chip * (CONV_W // N_CHIPS)), (3, CONV_W // N_CHIPS))
    g_by_name = dict(g_in=gg[0], g_cq=gg[1], g_ckv=gg[2], g_q=gg[3][:, :QK_DIM], g_k=gg[4][:, :QK_DIM], conv_w=conv_g,
                     g_oa=gg[5], g_oc=gg[6], g_pl=gg[7])
    half_by_name = dict(zip(("w_in", "w_uq", "w_ukv", "w_o", "w_pl", "w_plg"), zip(halves, other_halves)))
    weights = dict(g_in=g_in, w_in=w_in, g_cq=g_cq, w_uq=w_uq, g_ckv=g_ckv, w_ukv=w_ukv, g_q=g_q, g_k=g_k,
                   conv_w=conv_w, g_oa=g_oa, g_oc=g_oc, w_o=w_o, w_pl=w_pl, w_plg=w_plg, g_pl=g_pl)
    ms = dict(g_in=m_g_in, w_in=m_w_in, g_cq=m_g_cq, w_uq=m_w_uq, g_ckv=m_g_ckv, w_ukv=m_w_ukv, g_q=m_g_q, g_k=m_g_k,
              conv_w=m_conv_w, g_oa=m_g_oa, g_oc=m_g_oc, w_o=m_w_o, w_pl=m_w_pl, w_plg=m_w_plg, g_pl=m_g_pl)
    vs = dict(g_in=v_g_in, w_in=v_w_in, g_cq=v_g_cq, w_uq=v_w_uq, g_ckv=v_g_ckv, w_ukv=v_w_ukv, g_q=v_g_q, g_k=v_g_k,
              conv_w=v_conv_w, g_oa=v_g_oa, g_oc=v_g_oc, w_o=v_w_o, w_pl=v_w_pl, w_plg=v_w_plg, g_pl=v_g_pl)
    names = list(weights)
    flat = lambda a: a.reshape(-1, a.shape[-1])
    small_names = list(g_by_name)
    small_out = _adamw_small([flat(weights[n]) for n in small_names], [flat(g_by_name[n]) for n in small_names],
                             [flat(ms[n]) for n in small_names], [flat(vs[n]) for n in small_names])
    results = {n: (flat(g_by_name[n]), *(out[i] for out in small_out)) for i, n in enumerate(small_names)}
    for n in half_by_name:
        results[n] = _adamw_halves(flat(weights[n]), *half_by_name[n], flat(ms[n]), flat(vs[n]), c, "adamw_" + n)
    per_kind = [[results[n][kind].reshape(weights[n].shape) for n in names] for kind in range(4)]
    return (loss_out, gx.reshape(x.shape), *per_kind[0], *per_kind[1], *per_kind[2], *per_kind[3])
```

```python
import math

import jax
import jax.numpy as jnp
from jax import lax
from jax.experimental import pallas as pl
from jax.experimental.pallas import tpu as pltpu

F32 = jnp.float32
BF16 = jnp.bfloat16

D_MODEL = 1024
N_HEADS = 4
NOPE = 128
ROPE = 64
V_DIM = 128
QK_DIM = NOPE + ROPE
HEAD_PAD = 256
Q_LORA = 256
KV_LORA = 128
ATTN_W = 512
CONV_W = 512
PLE = 256
IN_TOTAL = 3008
PROJ_EXT = 3072
ROPE_THETA = 10000.0
EPS = 1e-6
SCALE = 1.0 / math.sqrt(QK_DIM)
LOG2E = math.log2(math.e)
EXP2_SCALE = SCALE * LOG2E
NEG = -1e30
SOFTMAX_ROWS = 32
SUB_TILE = 256

LR, B1, B2, ADAM_EPS, WD, STEP = 0.001, 0.9, 0.999, 1e-08, 0.01, 10

N_CHIPS = 4
LANES = 128
VMEM_LIMIT = 56 * 1024 * 1024
MESH = pl.DeviceIdType.MESH


def _params(**kw):
    return pltpu.CompilerParams(vmem_limit_bytes=VMEM_LIMIT, **kw)


def _inv_rms(x, n):
    return lax.rsqrt(jnp.sum(x * x, axis=-1, keepdims=True) / n + EPS)


def _lane_sum(a):
    folded = a[:, 0:LANES]
    for c0 in range(LANES, a.shape[1], LANES):
        folded = folded + a[:, c0:c0 + LANES]
    head = folded.astype(BF16)
    tail = (folded - head.astype(F32)).astype(BF16)
    return _dot(jnp.concatenate([head, tail], axis=1), jnp.ones((2 * LANES, LANES), BF16))


def _inv_rms_mxu(x):
    return lax.rsqrt(_lane_sum(x * x) / x.shape[1] + EPS)


def _rep(r, width):
    return r if width == LANES else jnp.tile(r, (1, width // LANES))


def _sigmoid(z):
    return jax.nn.sigmoid(z)


def _swap_rope_halves(b):
    lane = lax.broadcasted_iota(jnp.int32, b.shape, 1)
    swapped = jnp.where(lane < 32, pltpu.roll(b, 96, 1), pltpu.roll(b, 32, 1))
    return jnp.where(lane < ROPE, swapped, 0.0)


def _dot(a, b):
    return jnp.dot(a, b, preferred_element_type=F32)


def _dot_nt(a, b):
    return lax.dot_general(a, b, (((1,), (1,)), ((), ())), preferred_element_type=F32)


def _dot_tn(a, b):
    return lax.dot_general(a, b, (((0,), (0,)), ((), ())), preferred_element_type=F32)


def _colsum(a):
    return jnp.sum(a, axis=0, keepdims=True)


def _full(shape):
    return pl.BlockSpec(shape, lambda *_: (0,) * len(shape))


def _round_robin(chains, width):
    waiting, active = list(chains), []
    while waiting or active:
        while waiting and len(active) < width:
            active.append(waiting.pop(0))
        for chain in list(active):
            if next(chain, _DONE) is _DONE:
                active.remove(chain)


_DONE = object()


def _rope_tables(pos_ref, invf_ref, sgn_ref):
    ang = pos_ref[...].astype(F32) * invf_ref[...]
    return jnp.cos(ang), jnp.sin(ang) * sgn_ref[...]


def _fwd_proj(x, pos, g_in, w_in, g_cq, w_uq, g_ckv, w_ukv, gq, gk, invf, sgn, late_shards, tm):
    T = x.shape[0]
    nt = T // tm
    n_late = len(late_shards)
    ts = min(SUB_TILE, tm)

    def body(x_ref, pos_ref, g_in_ref, w_in_ref, g_cq_ref, w_uq_ref, g_ckv_ref, w_ukv_ref, gq_ref, gk_ref,
             invf_ref, sgn_ref, *rest):
        late_in, (proj_ref, q_ref, k_ref, v_ref) = rest[:n_late], rest[n_late:n_late + 4]
        late_out, late_scratch = rest[n_late + 4:2 * n_late + 4], rest[2 * n_late + 4:]
        i = pl.program_id(0)
        if n_late:
            start, forward, drain = _gather_steps([s.shape for s in late_shards], late_in, late_out,
                                                  late_scratch[:n_late], *late_scratch[n_late:])
            pl.when(i == 0)(start)
            pl.when(i == nt // 2)(forward)

        for r0 in range(0, tm, ts):
            rows = slice(r0, r0 + ts)
            xv = x_ref[rows, :]
            h = (xv * _rep(_inv_rms_mxu(xv), D_MODEL) * g_in_ref[...]).astype(BF16)
            lat = _dot(h, w_in_ref[:, 0:512])
            proj_ref[rows, 0:512] = lat
            c_q = lat[:, 0:Q_LORA]
            cqn = (c_q * _rep(_inv_rms_mxu(c_q), Q_LORA) * g_cq_ref[...]).astype(BF16)
            c_kv = lat[:, Q_LORA:Q_LORA + KV_LORA]
            ckvn = (c_kv * _inv_rms_mxu(c_kv) * g_ckv_ref[...]).astype(BF16)
            kpe = lat[:, 384:512]
            kpe_sq = kpe * kpe
            cos_b, sin_b = _rope_tables(pos_ref.at[rows, :], invf_ref, sgn_ref)
            gq_a, gq_b = gq_ref[:, 0:NOPE], gq_ref[:, NOPE:HEAD_PAD]
            gk_a, gk_b = gk_ref[:, 0:NOPE], gk_ref[:, NOPE:HEAD_PAD]

            def projections(rows=rows, h=h):
                for c0 in range(512, PROJ_EXT, 512):
                    proj_ref[rows, c0:c0 + 512] = _dot(h, w_in_ref[:, c0:c0 + 512])
                    yield

            def queries(hd, rows=rows, cqn=cqn, cos_b=cos_b, sin_b=sin_b, gq_a=gq_a, gq_b=gq_b):
                qh = _dot(cqn, w_uq_ref[hd])
                yield
                a, b = qh[:, 0:NOPE], qh[:, NOPE:HEAD_PAD]
                r = lax.rsqrt(_lane_sum(a * a + b * b) / QK_DIM + EPS)
                yield
                bn = b * r * gq_b
                q_ref[hd, rows, 0:NOPE] = (a * r * gq_a).astype(BF16)
                q_ref[hd, rows, NOPE:HEAD_PAD] = (bn * cos_b + _swap_rope_halves(bn) * sin_b).astype(BF16)
                yield

            def keys(hd, rows=rows, ckvn=ckvn, kpe=kpe, kpe_sq=kpe_sq, cos_b=cos_b, sin_b=sin_b, gk_a=gk_a, gk_b=gk_b):
                kvh = _dot(ckvn, w_ukv_ref[hd])
                yield
                ka = kvh[:, 0:NOPE]
                rk = lax.rsqrt(_lane_sum(ka * ka + kpe_sq) / QK_DIM + EPS)
                yield
                kbn = kpe * rk * gk_b
                k_ref[hd, rows, 0:NOPE] = (ka * rk * gk_a).astype(BF16)
                k_ref[hd, rows, NOPE:HEAD_PAD] = (kbn * cos_b + _swap_rope_halves(kbn) * sin_b).astype(BF16)
                v_ref[hd, rows, 0:V_DIM] = kvh[:, NOPE:HEAD_PAD].astype(BF16)
                v_ref[hd, rows, V_DIM:2 * V_DIM] = jnp.ones((ts, V_DIM), BF16)
                yield

            chains = [projections()]
            for hd in range(N_HEADS):
                chains += [queries(hd), keys(hd)]
            _round_robin(chains, 4)

        if n_late:
            pl.when(i == nt - 1)(drain)

    row = lambda i: (i, 0)
    head_rows = lambda i: (0, i, 0)
    outs = pl.pallas_call(
        body, name="fwd_proj", grid=(nt,),
        in_specs=[pl.BlockSpec((tm, D_MODEL), row), pl.BlockSpec((tm, 1), row), _full((1, D_MODEL)),
                  _full((D_MODEL, PROJ_EXT)), _full((1, Q_LORA)), _full((N_HEADS, Q_LORA, HEAD_PAD)),
                  _full((1, KV_LORA)), _full((N_HEADS, KV_LORA, HEAD_PAD)), _full((1, HEAD_PAD)), _full((1, HEAD_PAD)),
                  _full((1, LANES)), _full((1, LANES))] + [_full(s.shape) for s in late_shards],
        out_specs=[pl.BlockSpec((tm, PROJ_EXT), row), pl.BlockSpec((N_HEADS, tm, HEAD_PAD), head_rows),
                   pl.BlockSpec((N_HEADS, tm, HEAD_PAD), head_rows), pl.BlockSpec((N_HEADS, tm, 2 * V_DIM), head_rows)]
                  + [_ANY] * n_late,
        out_shape=[jax.ShapeDtypeStruct((T, PROJ_EXT), F32), jax.ShapeDtypeStruct((N_HEADS, T, HEAD_PAD), BF16),
                   jax.ShapeDtypeStruct((N_HEADS, T, HEAD_PAD), BF16), jax.ShapeDtypeStruct((N_HEADS, T, 2 * V_DIM), BF16)]
                  + _gathered_shapes(late_shards),
        scratch_shapes=_gather_scratch(late_shards) if n_late else [],
        compiler_params=_params(dimension_semantics=("arbitrary",)),
    )(x, pos, g_in, w_in, g_cq, w_uq, g_ckv, w_ukv, gq, gk, invf, sgn, *late_shards)
    return outs[:4], outs[4:]


def _chunk_pipeline(n_loop, lag, matmuls, pointwise, accumulate, last):
    slots = lag + 1

    def iteration(t, slot, pending=True):
        matmuls(jnp.minimum(t + lag, n_loop), (slot + lag) % slots)
        if pending:
            accumulate(t - lag, (slot + 1) % slots)
        pointwise(t, slot, False)

    def finish(slot, pending):
        for back in range(pending, 0, -1):
            accumulate(n_loop - back, (slot - back) % slots)
        pointwise(n_loop, slot, True)
        accumulate(n_loop, slot)
        last()

    for u in range(lag):
        matmuls(jnp.minimum(u, n_loop), u)
    for u in range(lag):
        pl.when(u < n_loop)(lambda u=u: iteration(u, u, pending=False))

    n_main = jnp.maximum(n_loop - lag, 0)

    def unrolled(tt, carry):
        for j in range(slots):
            iteration(lag + slots * tt + j, (lag + j) % slots)
        return carry

    lax.fori_loop(0, n_main // slots, unrolled, 0)
    rest = lax.rem(n_main, slots)
    t0 = n_loop - rest

    for r in range(slots):
        @pl.when(jnp.logical_and(n_loop >= lag, rest == r))
        def _():
            for j in range(r):
                iteration(t0 + j, (lag + j) % slots)
            finish((lag + r) % slots, lag)

    for short in range(lag):
        pl.when(n_loop == short)(lambda short=short: finish(short, short))


def _attn_fwd(q, k, v, tq):
    T = q.shape[1]
    tk = tq
    rc = min(SOFTMAX_ROWS, tq)

    def body(q_ref, k_ref, v_ref, o_ref, lse_ref, s0, s1, s2, p0, p1, p2, a0, a1, a2, m_ref, acc_ref):
        qi = pl.program_id(1)
        s_buf, p_buf, a_buf = (s0, s1, s2), (p0, p1, p2), (a0, a1, a2)

        def scores(t, slot):
            ks = pl.multiple_of(t * tk, tk)
            s_buf[slot][...] = _dot_nt(q_ref[0], k_ref[0, pl.ds(ks, tk), :])

        def values(t, slot):
            ks = pl.multiple_of(t * tk, tk)
            acc_ref[...] = acc_ref[...] * a_buf[slot][...] + _dot(p_buf[slot][...], v_ref[0, pl.ds(ks, tk), :])

        def softmax(t, slot, masked):
            s_all = s_buf[slot][...]
            if masked:
                row = lax.broadcasted_iota(jnp.int32, (tq, tk), 0)
                col = lax.broadcasted_iota(jnp.int32, (tq, tk), 1)
                s_all = jnp.where(col <= row, s_all, NEG)
                s_buf[slot][...] = s_all
            m_old = m_ref[...]
            m_new = jnp.maximum(m_old, jnp.max(s_all, axis=1, keepdims=True))
            a_buf[slot][...] = jnp.exp2((m_old - m_new) * EXP2_SCALE)
            m_ref[...] = m_new
            for r0 in range(0, tq, rc):
                s = s_buf[slot][r0:r0 + rc, :]
                p_buf[slot][r0:r0 + rc, :] = jnp.exp2((s - m_new[r0:r0 + rc, :]) * EXP2_SCALE).astype(BF16)

        def last():
            l = acc_ref[:, V_DIM:2 * V_DIM]
            o_ref[...] = acc_ref[:, 0:V_DIM] / l
            lse_ref[0] = (m_ref[...] * SCALE + jnp.log(l)).T[0:1, :]

        m_ref[...] = jnp.full_like(m_ref, NEG)
        acc_ref[...] = jnp.zeros_like(acc_ref)
        _chunk_pipeline(qi, 2, scores, softmax, values, last)

    return pl.pallas_call(
        body, name="attn_fwd", grid=(N_HEADS, T // tq),
        in_specs=[pl.BlockSpec((1, tq, HEAD_PAD), lambda h, i: (h, i, 0)),
                  pl.BlockSpec((1, T, HEAD_PAD), lambda h, i: (h, 0, 0)),
                  pl.BlockSpec((1, T, 2 * V_DIM), lambda h, i: (h, 0, 0))],
        out_specs=[pl.BlockSpec((tq, V_DIM), lambda h, i: (i, h)),
                   pl.BlockSpec((1, 1, tq), lambda h, i: (h, 0, i))],
        out_shape=[jax.ShapeDtypeStruct((T, ATTN_W), F32), jax.ShapeDtypeStruct((N_HEADS, 1, T), F32)],
        scratch_shapes=[pltpu.VMEM((tq, tk), F32)] * 3 + [pltpu.VMEM((tq, tk), BF16)] * 3
                       + [pltpu.VMEM((tq, 1), F32)] * 4 + [pltpu.VMEM((tq, 2 * V_DIM), F32)],
        compiler_params=_params(dimension_semantics=("arbitrary", "arbitrary")),
    )(q, k, v)


def _tail(x, o, proj, p, tgt, g_oa, g_oc, g_pl, conv_w, w_o, w_pl, w_plg, tm):
    T = x.shape[0]
    nt = T // tm

    def body(x_ref, o_ref, za_ref, cb_ref, cc_ref, cx_ref, zc_ref, cch_ref, cxh_ref, p_ref, tgt_ref,
             g_oa_ref, g_oc_ref, g_pl_ref, cw_ref, w_o_ref, w_pl_ref, w_plg_ref,
             dx1_ref, do_ref, delta_ref, dtail_ref, du_ref,
             dw_o_ref, dw_pl_ref, dw_plg_ref, dg_oa_ref, dg_oc_ref, dg_pl_ref, dcw_ref, loss_ref):
        i = pl.program_id(0)

        @pl.when(i == 0)
        def _():
            for r in (dw_o_ref, dw_pl_ref, dw_plg_ref, dg_oa_ref, dg_oc_ref, dg_pl_ref, dcw_ref, loss_ref):
                r[...] = jnp.zeros_like(r)

        g_oa, g_oc, g_pl = g_oa_ref[...], g_oc_ref[...], g_pl_ref[...]
        w0, w1, w2 = cw_ref[0:1, :], cw_ref[1:2, :], cw_ref[2:3, :]

        xv, ov, za, cb, zc = x_ref[...], o_ref[...], za_ref[...], cb_ref[...], zc_ref[...]
        pb = p_ref[...].astype(BF16)
        pp = _dot(pb, w_pl_ref[...])

        sa = _sigmoid(za)
        silu_a = za * sa
        ga = ov * silu_a
        ra = _inv_rms(ga, ATTN_W)
        xa = ga * ra
        ya = (xa * g_oa).astype(BF16)
        x1_a = _dot(ya, w_o_ref[0:ATTN_W, :])
        v = cc_ref[...] * cx_ref[...]
        not_first = jnp.where(i > 0, 1.0, 0.0)
        hv6 = cch_ref[6:7, :] * cxh_ref[6:7, :] * not_first
        hv7 = cch_ref[7:8, :] * cxh_ref[7:8, :] * not_first
        row = lax.broadcasted_iota(jnp.int32, v.shape, 0)
        v1 = jnp.where(row == 0, hv7, pltpu.roll(v, 1, 0))
        v2 = jnp.where(row == 0, hv6, jnp.where(row == 1, hv7, pltpu.roll(v, 2, 0)))
        u = w0 * v2 + w1 * v1 + w2 * v
        sc = _sigmoid(zc)
        silu_c = zc * sc
        gc = cb * u * silu_c
        rc = _inv_rms(gc, CONV_W)
        xc = gc * rc
        yc = (xc * g_oc).astype(BF16)
        x1 = xv + (x1_a + _dot(yc, w_o_ref[ATTN_W:D_MODEL, :]))
        r1 = _inv_rms(x1, D_MODEL)
        xh1 = x1 * r1
        n1 = (xh1 * g_pl).astype(BF16)
        gate = _sigmoid(_dot(n1, w_plg_ref[...]))
        err = x1 + gate * pp - tgt_ref[...]
        loss_ref[...] += 0.5 * jnp.sum(err * err) / D_MODEL
        dy = err / D_MODEL

        dpp = (dy * gate).astype(BF16)
        da = (dy * pp * gate * (1.0 - gate)).astype(BF16)
        dn1 = _dot_nt(da, w_plg_ref[...])
        dw_pl_ref[...] += _dot_tn(pb, dpp)
        dw_plg_ref[...] += _dot_tn(n1, da)
        dg_pl_ref[...] += _colsum(dn1 * xh1)
        dxh = dn1 * g_pl
        dx1 = dy + r1 * (dxh - xh1 * (jnp.sum(dxh * xh1, axis=-1, keepdims=True) / D_MODEL))
        dx1_ref[...] = dx1
        dx1b = dx1.astype(BF16)
        dya = _dot_nt(dx1b, w_o_ref[0:ATTN_W, :])
        dyc = _dot_nt(dx1b, w_o_ref[ATTN_W:D_MODEL, :])

        dw_o_ref[0:ATTN_W, :] += _dot_tn(ya, dx1b)
        dg_oa_ref[...] += _colsum(dya * xa)
        dxa = dya * g_oa
        dga = ra * (dxa - xa * (jnp.sum(dxa * xa, axis=-1, keepdims=True) / ATTN_W))
        do = (dga * silu_a).astype(BF16)
        do_ref[...] = do
        dof = do.astype(F32) * ov
        for hd in range(N_HEADS):
            delta_ref[hd] = _lane_sum(dof[:, hd * V_DIM:(hd + 1) * V_DIM]).T[0:1, :]
        dtail_ref[:, 0:512] = (dga * ov * (sa * (1.0 + za * (1.0 - sa)))).astype(BF16)

        dw_o_ref[ATTN_W:D_MODEL, :] += _dot_tn(yc, dx1b)
        dg_oc_ref[...] += _colsum(dyc * xc)
        dxc = dyc * g_oc
        dgc = rc * (dxc - xc * (jnp.sum(dxc * xc, axis=-1, keepdims=True) / CONV_W))
        dtail_ref[:, 512:1024] = (dgc * u * silu_c).astype(BF16)
        du = dgc * cb * silu_c
        du_ref[...] = du
        dtail_ref[:, 1024:1536] = (dgc * cb * u * (sc * (1.0 + zc * (1.0 - sc)))).astype(BF16)
        dcw_ref[0:1, :] += _colsum(du * v2)
        dcw_ref[1:2, :] += _colsum(du * v1)
        dcw_ref[2:3, :] += _colsum(du * v)

    row = lambda i: (i, 0)
    col = lambda c: (lambda i: (i, c))
    halo = lambda c: (lambda i: (jnp.maximum(i * (tm // 8) - 1, 0), c))
    in_specs = [pl.BlockSpec((tm, D_MODEL), row), pl.BlockSpec((tm, ATTN_W), row)]
    in_specs += [pl.BlockSpec((tm, 512), col(c)) for c in (1, 2, 3, 4, 5)]
    in_specs += [pl.BlockSpec((8, 512), halo(3)), pl.BlockSpec((8, 512), halo(4))]
    in_specs += [pl.BlockSpec((tm, PLE), row), pl.BlockSpec((tm, D_MODEL), row),
                 _full((1, ATTN_W)), _full((1, CONV_W)), _full((1, D_MODEL)), _full((3, CONV_W)),
                 _full((D_MODEL, D_MODEL)), _full((PLE, D_MODEL)), _full((D_MODEL, D_MODEL))]
    out_specs = [pl.BlockSpec((tm, D_MODEL), row), pl.BlockSpec((tm, ATTN_W), row),
                 pl.BlockSpec((N_HEADS, 1, tm), lambda i: (0, 0, i)), pl.BlockSpec((tm, 1536), row),
                 pl.BlockSpec((tm, CONV_W), row),
                 _full((D_MODEL, D_MODEL)), _full((PLE, D_MODEL)), _full((D_MODEL, D_MODEL)),
                 _full((1, ATTN_W)), _full((1, CONV_W)), _full((1, D_MODEL)), _full((3, CONV_W)), _full((1, LANES))]
    out_shape = [jax.ShapeDtypeStruct((T, D_MODEL), F32), jax.ShapeDtypeStruct((T, ATTN_W), BF16),
                 jax.ShapeDtypeStruct((N_HEADS, 1, T), F32), jax.ShapeDtypeStruct((T, 1536), BF16),
                 jax.ShapeDtypeStruct((T, CONV_W), F32),
                 jax.ShapeDtypeStruct((D_MODEL, D_MODEL), F32), jax.ShapeDtypeStruct((PLE, D_MODEL), F32),
                 jax.ShapeDtypeStruct((D_MODEL, D_MODEL), F32),
                 jax.ShapeDtypeStruct((1, ATTN_W), F32), jax.ShapeDtypeStruct((1, CONV_W), F32),
                 jax.ShapeDtypeStruct((1, D_MODEL), F32), jax.ShapeDtypeStruct((3, CONV_W), F32),
                 jax.ShapeDtypeStruct((1, LANES), F32)]
    return pl.pallas_call(
        body, name="tail", grid=(nt,), in_specs=in_specs, out_specs=out_specs, out_shape=out_shape,
        compiler_params=_params(dimension_semantics=("arbitrary",)),
    )(x, o, proj, proj, proj, proj, proj, proj, proj, p, tgt, g_oa, g_oc, g_pl, conv_w, w_o, w_pl, w_plg)


def _attn_bwd(q, k, v, do, lse_row, delta_row, tk):
    T = q.shape[1]
    tq = tk
    nq = T // tq
    rc = min(SOFTMAX_ROWS, tk)

    def body(q_ref, k_ref, v_ref, do_ref, lse_ref, dl_ref, dq_ref, dk_ref, dv_ref,
             s0, s1, d0, d1, p0, p1, g0, g1, dk_acc, dv_acc):
        kj = pl.program_id(1)
        s_buf, dp_buf, p_buf, g_buf = (s0, s1), (d0, d1), (p0, p1), (g0, g1)

        @pl.when(kj == 0)
        def _():
            dq_ref[...] = jnp.zeros_like(dq_ref)

        def q_start(t):
            return pl.multiple_of((nq - 1 - t) * tq, tq)

        def matmuls(t, slot):
            qs = q_start(t)
            s_buf[slot][...] = _dot_nt(k_ref[0], q_ref[0, pl.ds(qs, tq), :])
            dp_buf[slot][...] = _dot_nt(v_ref[0], do_ref[pl.ds(qs, tq), :])

        def pointwise(t, slot, masked):
            qs = q_start(t)
            lse2 = lse_ref[0, :, pl.ds(qs, tq)] * LOG2E
            dl = dl_ref[0, :, pl.ds(qs, tq)]
            for r0 in range(0, tk, rc):
                st = s_buf[slot][r0:r0 + rc, :]
                if masked:
                    row = lax.broadcasted_iota(jnp.int32, (rc, tq), 0)
                    col = lax.broadcasted_iota(jnp.int32, (rc, tq), 1)
                    st = jnp.where(row + r0 <= col, st, NEG)
                pt = jnp.exp2(st * EXP2_SCALE - lse2)
                p_buf[slot][r0:r0 + rc, :] = pt.astype(BF16)
                g_buf[slot][r0:r0 + rc, :] = (pt * (dp_buf[slot][r0:r0 + rc, :] - dl) * SCALE).astype(BF16)

        def accumulate(t, slot):
            qs = q_start(t)
            dv_acc[...] += _dot(p_buf[slot][...], do_ref[pl.ds(qs, tq), :])
            dk_acc[...] += _dot(g_buf[slot][...], q_ref[0, pl.ds(qs, tq), :])
            dq_ref[0, pl.ds(qs, tq), :] += _dot_tn(g_buf[slot][...], k_ref[0])

        def last():
            dk_ref[0] = dk_acc[...]
            dv_ref[0] = dv_acc[...]

        dk_acc[...] = jnp.zeros_like(dk_acc)
        dv_acc[...] = jnp.zeros_like(dv_acc)
        _chunk_pipeline(nq - 1 - kj, 1, matmuls, pointwise, accumulate, last)

    return pl.pallas_call(
        body, name="attn_bwd", grid=(N_HEADS, T // tk),
        in_specs=[pl.BlockSpec((1, T, HEAD_PAD), lambda h, j: (h, 0, 0)),
                  pl.BlockSpec((1, tk, HEAD_PAD), lambda h, j: (h, j, 0)),
                  pl.BlockSpec((1, tk, V_DIM), lambda h, j: (h, j, 0)),
                  pl.BlockSpec((T, V_DIM), lambda h, j: (0, h)),
                  pl.BlockSpec((1, 1, T), lambda h, j: (h, 0, 0)),
                  pl.BlockSpec((1, 1, T), lambda h, j: (h, 0, 0))],
        out_specs=[pl.BlockSpec((1, T, HEAD_PAD), lambda h, j: (h, 0, 0)),
                   pl.BlockSpec((1, tk, HEAD_PAD), lambda h, j: (h, j, 0)),
                   pl.BlockSpec((1, tk, V_DIM), lambda h, j: (h, j, 0))],
        out_shape=[jax.ShapeDtypeStruct((N_HEADS, T, HEAD_PAD), F32), jax.ShapeDtypeStruct((N_HEADS, T, HEAD_PAD), F32),
                   jax.ShapeDtypeStruct((N_HEADS, T, V_DIM), F32)],
        scratch_shapes=[pltpu.VMEM((tk, tq), F32)] * 4 + [pltpu.VMEM((tk, tq), BF16)] * 4
                       + [pltpu.VMEM((tk, HEAD_PAD), F32), pltpu.VMEM((tk, V_DIM), F32)],
        compiler_params=_params(dimension_semantics=("arbitrary", "arbitrary")),
    )(q, k, v, do, lse_row, delta_row)


def _bwd_proj(x, dx1, pos, proj, dq, dk, dv, dtail, du, g_in, w_in, g_cq, w_uq, g_ckv, w_ukv, gq, gk, conv_w,
              invf, sgn, tm):
    T = x.shape[0]
    nt = T // tm

    ts = min(SUB_TILE, tm)

    def body(x_ref, dx1_ref, pos_ref, lat_ref, cc_ref, cx_ref, dq_ref, dk_ref, dv_ref, dtail_ref, du_ref, dun_ref, *rest):
        consts, (gx_ref, h_ref, dproj_ref), sums = rest[:11], rest[11:14], rest[14:]
        cw_ref = consts[8]
        i = pl.program_id(0)

        @pl.when(i == 0)
        def _():
            for r in sums:
                r[...] = jnp.zeros_like(r)

        du_v = du_ref[...]
        not_last = jnp.where(i < nt - 1, 1.0, 0.0)
        nx0 = dun_ref[0:1, :] * not_last
        nx1 = dun_ref[1:2, :] * not_last
        row = lax.broadcasted_iota(jnp.int32, du_v.shape, 0)
        du1 = jnp.where(row == tm - 1, nx0, pltpu.roll(du_v, tm - 1, 0))
        du2 = jnp.where(row == tm - 2, nx0, jnp.where(row == tm - 1, nx1, pltpu.roll(du_v, tm - 2, 0)))
        dvc = cw_ref[2:3, :] * du_v + cw_ref[1:2, :] * du1 + cw_ref[0:1, :] * du2
        dproj_ref[:, 1536:2048] = (dvc * cx_ref[...]).astype(BF16)
        dproj_ref[:, 2048:2560] = (dvc * cc_ref[...]).astype(BF16)

        for r0 in range(0, tm, ts):
            rows = slice(r0, r0 + ts)
            work(x_ref.at[rows, :], dx1_ref.at[rows, :], pos_ref.at[rows, :], lat_ref.at[rows, :],
                 dq_ref.at[:, rows, :], dk_ref.at[:, rows, :], dv_ref.at[:, rows, :], dtail_ref.at[rows, :], *consts,
                 gx_ref.at[rows, :], h_ref.at[:, rows], dproj_ref.at[rows, :], *sums)

    def work(x_ref, dx1_ref, pos_ref, lat_ref, dq_ref, dk_ref, dv_ref, dtail_ref,
             g_in_ref, w_in_ref, g_cq_ref, w_uq_ref, g_ckv_ref, w_ukv_ref, gq_ref, gk_ref, cw_ref, invf_ref, sgn_ref,
             gx_ref, h_ref, dproj_ref, dw_uq_ref, dw_ukv_ref, dg_in_ref, dg_cq_ref, dg_ckv_ref, dgq_ref, dgk_ref):
        xv = x_ref[...]
        r0 = _rep(_inv_rms_mxu(xv), D_MODEL)
        xh0 = xv * r0
        g_in = g_in_ref[...]
        h_ref[...] = (xh0 * g_in).astype(BF16).T

        c_q = lat_ref[:, 0:Q_LORA]
        rq = _rep(_inv_rms_mxu(c_q), Q_LORA)
        xq = c_q * rq
        g_cq = g_cq_ref[...]
        cqn = (xq * g_cq).astype(BF16)
        c_kv = lat_ref[:, Q_LORA:Q_LORA + KV_LORA]
        rkv = _inv_rms_mxu(c_kv)
        xkv = c_kv * rkv
        g_ckv = g_ckv_ref[...]
        ckvn = (xkv * g_ckv).astype(BF16)
        kpe = lat_ref[:, 384:512]
        kpe_sq = kpe * kpe
        cos_b, sin_b = _rope_tables(pos_ref, invf_ref, sgn_ref)
        gq_a, gq_b = gq_ref[:, 0:NOPE], gq_ref[:, NOPE:HEAD_PAD]
        gk_a, gk_b = gk_ref[:, 0:NOPE], gk_ref[:, NOPE:HEAD_PAD]

        dproj_ref[:, 512:1536] = dtail_ref[:, 0:1024]
        dproj_ref[:, 2560:3072] = dtail_ref[:, 1024:1536]

        def dh_part(c0):
            return _dot_nt(dproj_ref[:, c0:c0 + 512], w_in_ref[:, c0:c0 + 512])

        later_chunks = ((512,), (1024,), (1536, 2048), (2560,))
        dh = jnp.zeros((ts, D_MODEL), F32)
        acc = dict(dh=dh, dkpe=jnp.zeros((ts, LANES), F32), dcqn=jnp.zeros((ts, Q_LORA), F32),
                   dckvn=jnp.zeros((ts, KV_LORA), F32))

        def dh_chunks():
            for chunks in later_chunks:
                for chunk in chunks:
                    acc["dh"] = acc["dh"] + dh_part(chunk)
                    yield

        def queries(hd):
            qh = _dot(cqn, w_uq_ref[hd])
            yield
            a, b = qh[:, 0:NOPE], qh[:, NOPE:HEAD_PAD]
            r = lax.rsqrt(_lane_sum(a * a + b * b) / QK_DIM + EPS)
            yield
            xa, xb = a * r, b * r
            dan = dq_ref[hd, :, 0:NOPE]
            dbr = dq_ref[hd, :, NOPE:HEAD_PAD]
            dbn = dbr * cos_b + _swap_rope_halves(dbr * sin_b)
            yield
            dgq_ref[:, 0:NOPE] += _colsum(dan * xa)
            dgq_ref[:, NOPE:HEAD_PAD] += _colsum(dbn * xb)
            dxa, dxb = dan * gq_a, dbn * gq_b
            cq = _lane_sum(dxa * xa + dxb * xb) / QK_DIM
            yield
            dqh = jnp.concatenate([r * (dxa - xa * cq), r * (dxb - xb * cq)], axis=-1).astype(BF16)
            yield
            dw_uq_ref[hd] += _dot_tn(cqn, dqh)
            yield
            acc["dcqn"] = acc["dcqn"] + _dot_nt(dqh, w_uq_ref[hd])
            yield

        def keys(hd):
            kvh = _dot(ckvn, w_ukv_ref[hd])
            yield
            ka = kvh[:, 0:NOPE]
            rk = lax.rsqrt(_lane_sum(ka * ka + kpe_sq) / QK_DIM + EPS)
            yield
            xka, xkb = ka * rk, kpe * rk
            dkan = dk_ref[hd, :, 0:NOPE]
            dkbr = dk_ref[hd, :, NOPE:HEAD_PAD]
            dkbn = dkbr * cos_b + _swap_rope_halves(dkbr * sin_b)
            yield
            dgk_ref[:, 0:NOPE] += _colsum(dkan * xka)
            dgk_ref[:, NOPE:HEAD_PAD] += _colsum(dkbn * xkb)
            dxka, dxkb = dkan * gk_a, dkbn * gk_b
            ck = _lane_sum(dxka * xka + dxkb * xkb) / QK_DIM
            yield
            acc["dkpe"] = acc["dkpe"] + rk * (dxkb - xkb * ck)
            dkvh = jnp.concatenate([rk * (dxka - xka * ck), dv_ref[hd]], axis=-1).astype(BF16)
            yield
            dw_ukv_ref[hd] += _dot_tn(ckvn, dkvh)
            yield
            acc["dckvn"] = acc["dckvn"] + _dot_nt(dkvh, w_ukv_ref[hd])
            yield

        chains = [dh_chunks()]
        for hd in range(N_HEADS):
            chains += [queries(hd), keys(hd)]
        _round_robin(chains, 5)
        dh, dkpe, dcqn, dckvn = acc["dh"], acc["dkpe"], acc["dcqn"], acc["dckvn"]

        dg_cq_ref[...] += _colsum(dcqn * xq)
        dxq = dcqn * g_cq
        dproj_ref[:, 0:Q_LORA] = (rq * (dxq - xq * _rep(_lane_sum(dxq * xq) / Q_LORA, Q_LORA))).astype(BF16)
        dg_ckv_ref[...] += _colsum(dckvn * xkv)
        dxkv = dckvn * g_ckv
        dproj_ref[:, 256:384] = (rkv * (dxkv - xkv * (_lane_sum(dxkv * xkv) / KV_LORA))).astype(BF16)
        dproj_ref[:, 384:512] = dkpe.astype(BF16)
        dh = dh + dh_part(0)
        dg_in_ref[...] += _colsum(dh * xh0)
        dxh = dh * g_in
        gx_ref[...] = dx1_ref[...] + r0 * (dxh - xh0 * _rep(_lane_sum(dxh * xh0) / D_MODEL, D_MODEL))

    row = lambda i: (i, 0)
    col = lambda c: (lambda i: (i, c))
    head_rows = lambda i: (0, i, 0)
    nxt = lambda i: (jnp.minimum((i + 1) * (tm // 8), T // 8 - 1), 0)
    in_specs = [pl.BlockSpec((tm, D_MODEL), row), pl.BlockSpec((tm, D_MODEL), row), pl.BlockSpec((tm, 1), row),
                pl.BlockSpec((tm, 512), col(0)), pl.BlockSpec((tm, 512), col(3)), pl.BlockSpec((tm, 512), col(4)),
                pl.BlockSpec((N_HEADS, tm, HEAD_PAD), head_rows), pl.BlockSpec((N_HEADS, tm, HEAD_PAD), head_rows),
                pl.BlockSpec((N_HEADS, tm, V_DIM), head_rows), pl.BlockSpec((tm, 1536), row),
                pl.BlockSpec((tm, CONV_W), row), pl.BlockSpec((8, CONV_W), nxt),
                _full((1, D_MODEL)), _full((D_MODEL, PROJ_EXT)), _full((1, Q_LORA)), _full((N_HEADS, Q_LORA, HEAD_PAD)),
                _full((1, KV_LORA)), _full((N_HEADS, KV_LORA, HEAD_PAD)), _full((1, HEAD_PAD)), _full((1, HEAD_PAD)),
                _full((3, CONV_W)), _full((1, LANES)), _full((1, LANES))]
    out_specs = [pl.BlockSpec((tm, D_MODEL), row), pl.BlockSpec((D_MODEL, tm), lambda i: (0, i)),
                 pl.BlockSpec((tm, PROJ_EXT), row),
                 _full((N_HEADS, Q_LORA, HEAD_PAD)), _full((N_HEADS, KV_LORA, HEAD_PAD)),
                 _full((1, D_MODEL)), _full((1, Q_LORA)), _full((1, KV_LORA)), _full((1, HEAD_PAD)), _full((1, HEAD_PAD))]
    out_shape = [jax.ShapeDtypeStruct((T, D_MODEL), F32), jax.ShapeDtypeStruct((D_MODEL, T), BF16),
                 jax.ShapeDtypeStruct((T, PROJ_EXT), BF16),
                 jax.ShapeDtypeStruct((N_HEADS, Q_LORA, HEAD_PAD), F32), jax.ShapeDtypeStruct((N_HEADS, KV_LORA, HEAD_PAD), F32),
                 jax.ShapeDtypeStruct((1, D_MODEL), F32), jax.ShapeDtypeStruct((1, Q_LORA), F32),
                 jax.ShapeDtypeStruct((1, KV_LORA), F32), jax.ShapeDtypeStruct((1, HEAD_PAD), F32),
                 jax.ShapeDtypeStruct((1, HEAD_PAD), F32)]
    return pl.pallas_call(
        body, name="bwd_proj", grid=(nt,), in_specs=in_specs, out_specs=out_specs, out_shape=out_shape,
        compiler_params=_params(dimension_semantics=("arbitrary",)),
    )(x, dx1, pos, proj, proj, proj, dq, dk, dv, dtail, du, du, g_in, w_in, g_cq, w_uq, g_ckv, w_ukv, gq, gk, conv_w,
      invf, sgn)


def _matmul_acc(a, b, tt, tn, parts):
    M, T = a.shape
    N = b.shape[1]
    n = len(parts)
    grid = (N // tn, T // tt)

    def body(a_ref, b_ref, *rest):
        part_refs, o_ref, out_refs, sems = rest[:n], rest[n], rest[n + 1:2 * n + 1], rest[2 * n + 1:]
        j, t = pl.program_id(0), pl.program_id(1)
        if n:
            start, drain = _scatter_steps(part_refs, out_refs, *sems)
            pl.when(jnp.logical_and(j == 0, t == 0))(start)

        @pl.when(t == 0)
        def _():
            o_ref[...] = jnp.zeros_like(o_ref)

        o_ref[...] += _dot(a_ref[...], b_ref[...])
        if n:
            pl.when(jnp.logical_and(j == grid[0] - 1, t == grid[1] - 1))(drain)

    sems = [pltpu.SemaphoreType.DMA((3 * n,)), pltpu.SemaphoreType.DMA((3 * n,)), pltpu.SemaphoreType.DMA((n,))]
    outs = pl.pallas_call(
        body, name="dw_in", grid=grid,
        in_specs=[pl.BlockSpec((M, tt), lambda j, t: (0, t)), pl.BlockSpec((tt, tn), lambda j, t: (t, j))] + [_ANY] * n,
        out_specs=[pl.BlockSpec((M, tn), lambda j, t: (0, j))] + [_ANY] * n,
        out_shape=[jax.ShapeDtypeStruct((M, N), F32)] + _scattered_shapes(parts),
        scratch_shapes=sems if n else [],
        compiler_params=_params(dimension_semantics=("arbitrary", "arbitrary")),
    )(a, b, *parts)
    return outs[0], outs[1:]


def _add_chips(parts, small_parts):
    arrays = list(parts) + [small_parts]

    def body(*refs):
        ins, outs = refs[:len(arrays)], refs[len(arrays):]
        for a_ref, o_ref in zip(ins, outs):
            part = lambda k: a_ref[k].astype(F32)
            o_ref[...] = ((part(0) + part(1)) + part(2)) + part(3)

    in_specs, out_specs, out_shape = [], [], []
    for a in arrays:
        _, rows, cols = a.shape
        in_specs.append(pl.BlockSpec((N_CHIPS, rows // 2, cols), lambda i: (0, i, 0)))
        out_specs.append(pl.BlockSpec((rows // 2, cols), lambda i: (i, 0)))
        out_shape.append(jax.ShapeDtypeStruct((rows, cols), F32))
    outs = pl.pallas_call(body, name="add_chips", grid=(2,), in_specs=in_specs, out_specs=out_specs,
                          out_shape=out_shape, compiler_params=_params(dimension_semantics=("arbitrary",)))(*arrays)
    return outs[:-1], outs[-1]


def _adamw_small(ws, gs, ms, vs):
    n = len(ws)

    def body(*refs):
        for i in range(n):
            w_ref, g_ref, m_ref, v_ref = (refs[k * n + i] for k in range(4))
            d_ref, nm_ref, nv_ref = (refs[(4 + k) * n + i] for k in range(3))
            _adamw_math(g_ref[...], w_ref, m_ref, v_ref, d_ref, nm_ref, nv_ref)

    shapes = [jax.ShapeDtypeStruct(w.shape, F32) for w in ws]
    outs = pl.pallas_call(body, name="adamw_small", out_shape=shapes * 3)(*ws, *gs, *ms, *vs)
    return outs[:n], outs[n:2 * n], outs[2 * n:]


def _adamw_math(gv, w_ref, m_ref, v_ref, d_ref, nm_ref, nv_ref):
    nm = B1 * m_ref[...] + (1.0 - B1) * gv
    nv = B2 * v_ref[...] + (1.0 - B2) * (gv * gv)
    m_hat = nm / (1.0 - B1 ** STEP)
    v_hat = nv / (1.0 - B2 ** STEP)
    d_ref[...] = -LR * (m_hat / (jnp.sqrt(v_hat) + ADAM_EPS) + WD * w_ref[...])
    nm_ref[...] = nm
    nv_ref[...] = nv


def _adamw_halves(w, mine, other, m, v, c, name):
    hr, cols = mine.shape

    def body(c_ref, w_ref, mine_ref, other_ref, m_ref, v_ref, g_ref, d_ref, nm_ref, nv_ref):
        gv = jnp.where(pl.program_id(0) == c_ref[0], mine_ref[...], other_ref[...])
        g_ref[...] = gv
        _adamw_math(gv, w_ref, m_ref, v_ref, d_ref, nm_ref, nv_ref)

    half = pl.BlockSpec((hr, cols), lambda i, c_ref: (i, 0))
    whole = pl.BlockSpec((hr, cols), lambda i, c_ref: (0, 0))
    shp = jax.ShapeDtypeStruct(w.shape, F32)
    return pl.pallas_call(
        body, name=name, out_shape=[shp] * 4,
        grid_spec=pltpu.PrefetchScalarGridSpec(num_scalar_prefetch=1, grid=(2,), in_specs=[half, whole, whole, half, half],
                                               out_specs=[half] * 4),
        compiler_params=_params(dimension_semantics=("arbitrary",)),
    )(c.reshape(1), w, mine, other, m, v)


_ANY = pl.BlockSpec(memory_space=pl.ANY)


def _mesh_pos():
    return lax.axis_index("x"), lax.axis_index("y"), lax.axis_index("c")


def _other_chips(x, y):
    return [(1 - x, y), (x, 1 - y), (1 - x, 1 - y)]


def _remote(src, dst, send_sems, recv_sems, k, to):
    return pltpu.make_async_remote_copy(src_ref=src, dst_ref=dst, send_sem=send_sems.at[k], recv_sem=recv_sems.at[k],
                                        device_id=to, device_id_type=MESH)


def _gather_weights(shards):
    n = len(shards)

    def body(*refs):
        start, forward, drain = _gather_steps([s.shape for s in shards], refs[:n], refs[n:2 * n], refs[2 * n:3 * n],
                                              *refs[3 * n:])
        start()
        forward()
        drain()

    vmem = pl.BlockSpec(memory_space=pltpu.VMEM)
    return pl.pallas_call(
        body, name="gather_weights", in_specs=[vmem] * n, out_specs=[_ANY] * n,
        out_shape=_gathered_shapes(shards), scratch_shapes=_gather_scratch(shards), compiler_params=_params(),
    )(*shards)


def _travel_shape(shard):
    rows, cols = shard.shape
    return (rows, HEAD_PAD if cols == QK_DIM else cols)


def _gathered_shapes(shards):
    return [jax.ShapeDtypeStruct((N_CHIPS,) + _travel_shape(s), BF16) for s in shards]


def _gather_scratch(shards):
    n = len(shards)
    return ([pltpu.VMEM(_travel_shape(s), BF16) for s in shards]
            + [pltpu.SemaphoreType.DMA((6 * n,)), pltpu.SemaphoreType.DMA((6 * n,)), pltpu.SemaphoreType.DMA((n,))])


def _gather_steps(shapes, ins, outs, stage, send_sems, recv_sems, local_sems):
    n = len(shapes)
    halved = [s[0] % 32 == 0 for s in shapes]

    def part(i, ref, hc):
        if not halved[i]:
            return ref
        hr = shapes[i][0] // 2
        return ref.at[pl.ds(hc * hr, hr), :]

    def to_chip(i, j, x, y, c):
        cx, cy = _other_chips(x, y)[j]
        return _remote(part(i, stage[i], c), part(i, outs[i].at[2 * x + y], c), send_sems, recv_sems, 6 * i + j, (cx, cy, c))

    def to_sibling(i, j, x, y, c):
        cx, cy = _other_chips(x, y)[j]
        got = part(i, outs[i].at[2 * cx + cy], c)
        return _remote(got, got, send_sems, recv_sems, 6 * i + 3 + j, (x, y, 1 - c))

    def local(i, x, y):
        return pltpu.make_async_copy(stage[i], outs[i].at[2 * x + y], local_sems.at[i])

    def start():
        x, y, c = _mesh_pos()
        for i in range(n):
            cols = ins[i].shape[1]
            if stage[i].shape[1] != cols:
                stage[i][...] = jnp.zeros_like(stage[i])
            stage[i][:, 0:cols] = ins[i][...].astype(BF16)
            local(i, x, y).start()
            for j in range(3):
                to_chip(i, j, x, y, c).start()

    def forward():
        x, y, c = _mesh_pos()
        for i in range(n):
            for j, (cx, cy) in enumerate(_other_chips(x, y)):
                got = part(i, outs[i].at[2 * cx + cy], c)
                _remote(got, got, send_sems, recv_sems, 6 * i + j, (cx, cy, c)).wait_recv()
                if halved[i]:
                    to_sibling(i, j, x, y, c).start()

    def drain():
        x, y, c = _mesh_pos()
        for i in range(n):
            for j, (cx, cy) in enumerate(_other_chips(x, y)):
                if halved[i]:
                    got = part(i, outs[i].at[2 * cx + cy], 1 - c)
                    _remote(got, got, send_sems, recv_sems, 6 * i + 3 + j, (x, y, 1 - c)).wait_recv()
                    to_sibling(i, j, x, y, c).wait_send()
                to_chip(i, j, x, y, c).wait_send()
            local(i, x, y).wait()

    return start, forward, drain


def _swap_halves(grads, whole, name):
    n, m = len(grads), len(grads) + len(whole)

    def body(*refs):
        ins, outs, send_sems, recv_sems = refs[:m], refs[m:2 * m], refs[2 * m], refs[2 * m + 1]
        x, y, c = _mesh_pos()
        cps = []
        for i in range(m):
            src = ins[i]
            if i < n:
                hr = grads[i].shape[1] // 2
                src = src.at[:, pl.ds((1 - c) * hr, hr), :]
            cp = _remote(src, outs[i], send_sems, recv_sems, i, (x, y, 1 - c))
            cp.start()
            cps.append(cp)
        for cp in cps:
            cp.wait()

    out_shape = [jax.ShapeDtypeStruct((g.shape[0], g.shape[1] // 2, g.shape[2]), F32) for g in grads]
    out_shape += [jax.ShapeDtypeStruct(w.shape, F32) for w in whole]
    outs = pl.pallas_call(
        body, name=name, in_specs=[_ANY] * m, out_specs=[_ANY] * m, out_shape=out_shape,
        scratch_shapes=[pltpu.SemaphoreType.DMA((m,)), pltpu.SemaphoreType.DMA((m,))],
    )(*grads, *whole)
    return outs[:n], outs[n:]


def _scattered_shapes(parts):
    return [jax.ShapeDtypeStruct(p.shape if p.ndim == 3 else (N_CHIPS,) + p.shape, p.dtype) for p in parts]


def _scatter_steps(ins, outs, send_sems, recv_sems, local_sems):
    n = len(ins)

    def src(i, k):
        return ins[i].at[k] if len(ins[i].shape) == 3 else ins[i]

    def sends(x, y, c):
        return [_remote(src(i, 2 * cx + cy), outs[i].at[2 * x + y], send_sems, recv_sems, 3 * i + j, (cx, cy, c))
                for i in range(n) for j, (cx, cy) in enumerate(_other_chips(x, y))]

    def local(i, x, y):
        return pltpu.make_async_copy(src(i, 2 * x + y), outs[i].at[2 * x + y], local_sems.at[i])

    def start():
        x, y, c = _mesh_pos()
        for i in range(n):
            local(i, x, y).start()
        for cp in sends(x, y, c):
            cp.start()

    def drain():
        x, y, c = _mesh_pos()
        for i in range(n):
            for j, (cx, cy) in enumerate(_other_chips(x, y)):
                got = outs[i].at[2 * cx + cy]
                _remote(got, got, send_sems, recv_sems, 3 * i + j, (cx, cy, c)).wait_recv()
        for cp in sends(x, y, c):
            cp.wait_send()
        for i in range(n):
            local(i, x, y).wait()

    return start, drain


def _add_pair(grads, from_sibling, small, small_sibling, c):
    n = len(grads)

    def body(c_ref, *refs):
        ins, outs = refs[:2 * n + 2], refs[2 * n + 2:]
        for i in range(n + 1):
            outs[i][...] = (ins[2 * i][...] + ins[2 * i + 1][...]).astype(outs[i].dtype)

    in_specs, out_specs, out_shape, args = [], [], [], []
    for g, r in zip(grads, from_sibling):
        _, hr, cols = r.shape
        in_specs += [pl.BlockSpec((1, hr, cols), lambda k, c_ref: (k, c_ref[0], 0)),
                     pl.BlockSpec((1, hr, cols), lambda k, c_ref: (k, 0, 0))]
        out_specs.append(pl.BlockSpec((1, hr, cols), lambda k, c_ref: (k, 0, 0)))
        out_shape.append(jax.ShapeDtypeStruct(r.shape, BF16))
        args += [g, r]
    whole = pl.BlockSpec(small.shape, lambda k, c_ref: (0, 0))
    in_specs += [whole, whole]
    out_specs.append(whole)
    out_shape.append(jax.ShapeDtypeStruct(small.shape, F32))
    outs = pl.pallas_call(
        body, name="add_pair", out_shape=out_shape,
        grid_spec=pltpu.PrefetchScalarGridSpec(num_scalar_prefetch=1, grid=(N_CHIPS,), in_specs=in_specs,
                                               out_specs=out_specs),
        compiler_params=_params(dimension_semantics=("arbitrary",)),
    )(c.reshape(1), *args, small, small_sibling)
    return outs[:n], outs[n]


def _scatter_w_in(dw_in_e, from_sibling):
    hr = from_sibling.shape[1]
    shard = (N_CHIPS, hr, SHARD_COLS_IN)

    def body(g_in, r_in, out, g_buf, r_buf, p_buf, load_sems, send_sems, recv_sems, local_sems):
        c = lax.axis_index("c")
        loads = (pltpu.make_async_copy(g_in.at[0, pl.ds(c * hr, hr), :], g_buf, load_sems.at[0]),
                 pltpu.make_async_copy(r_in.at[0], r_buf, load_sems.at[1]))
        for cp in loads:
            cp.start()
        for cp in loads:
            cp.wait()
        g_buf[...] += r_buf[...]
        p_buf[0, :, 0:KPE_END] = g_buf[:, 0:KPE_END].astype(BF16)
        p_buf[0, :, KPE_END:SHARD_COLS_IN] = g_buf[:, KPE_END + KPE_PAD:SHARD_COLS_IN + KPE_PAD].astype(BF16)
        for k in range(1, N_CHIPS):
            p_buf[k] = g_buf[:, SHARD_COLS_IN * k + KPE_PAD:SHARD_COLS_IN * (k + 1) + KPE_PAD].astype(BF16)
        start, drain = _scatter_steps([p_buf], [out], send_sems, recv_sems, local_sems)
        start()
        drain()

    return pl.pallas_call(
        body, name="scatter_grads", in_specs=[_ANY] * 2, out_specs=_ANY, out_shape=jax.ShapeDtypeStruct(shard, BF16),
        scratch_shapes=[pltpu.VMEM((hr, PROJ_EXT), F32)] * 2 + [pltpu.VMEM(shard, BF16)]
                       + [pltpu.SemaphoreType.DMA((2,)), pltpu.SemaphoreType.DMA((3,)), pltpu.SemaphoreType.DMA((3,)),
                          pltpu.SemaphoreType.DMA((1,))],
        compiler_params=_params(),
    )(dw_in_e, from_sibling)


def _share_halves(halves):
    n = len(halves)

    def body(*refs):
        ins, outs, send_sems, recv_sems = refs[:n], refs[n:2 * n], refs[2 * n], refs[2 * n + 1]
        x, y, c = _mesh_pos()
        cps = [_remote(ins[i], outs[i], send_sems, recv_sems, i, (x, y, 1 - c)) for i in range(n)]
        for cp in cps:
            cp.start()
        for cp in cps:
            cp.wait()

    return pl.pallas_call(
        body, name="share_halves", in_specs=[_ANY] * n, out_specs=[_ANY] * n,
        out_shape=[jax.ShapeDtypeStruct(h.shape, h.dtype) for h in halves],
        scratch_shapes=[pltpu.SemaphoreType.DMA((n,)), pltpu.SemaphoreType.DMA((n,))],
    )(*halves)


SHARD_COLS_IN = IN_TOTAL // N_CHIPS
KPE_END = Q_LORA + KV_LORA + ROPE
KPE_PAD = PROJ_EXT - IN_TOTAL


def _by_cols(a):
    return a.transpose(1, 0, 2).reshape(a.shape[1], N_CHIPS * a.shape[2])


def _assemble_early(c_in, c_uq, c_ukv, c_conv):
    w_in_e = jnp.concatenate([c_in[0][:, :KPE_END], jnp.zeros((D_MODEL, KPE_PAD), BF16), c_in[0][:, KPE_END:],
                              c_in[1], c_in[2], c_in[3]], axis=1)
    return w_in_e, c_uq, c_ukv, _by_cols(c_conv).astype(F32)


def _assemble_late(c_o, c_pl, c_plg):
    return c_o.reshape(D_MODEL, D_MODEL), _by_cols(c_pl), c_plg.reshape(D_MODEL, D_MODEL)


def _split_others(dw_uq, dw_ukv, dw_o, dw_pl, dw_plg):
    chip_major = lambda a: a.reshape(a.shape[0], N_CHIPS, a.shape[1] // N_CHIPS).transpose(1, 0, 2)
    return [dw_uq[:, :, :QK_DIM], dw_ukv, dw_o.reshape(N_CHIPS, D_MODEL // N_CHIPS, D_MODEL),
            chip_major(dw_pl), dw_plg.reshape(N_CHIPS, D_MODEL // N_CHIPS, D_MODEL)]


def _local_step(x, p, pos, tgt, gains, early, late_shards, late_gathered, tm, tq):
    w_in_e, w_uq_e, w_ukv, conv_w = early
    g_in, g_cq, g_ckv, g_q, g_k, g_oa, g_oc, g_pl = gains
    T = x.shape[0]
    zpad = lambda a, n: jnp.concatenate([a, jnp.zeros(a.shape[:-1] + (n,), a.dtype)], axis=-1)
    gq, gk = zpad(g_q, HEAD_PAD - QK_DIM), zpad(g_k, HEAD_PAD - QK_DIM)
    inv_freq = 1.0 / (ROPE_THETA ** (jnp.arange(0, ROPE, 2, dtype=F32) / ROPE))
    invf = jnp.concatenate([inv_freq, inv_freq, jnp.zeros((64,), F32)]).reshape(1, LANES)
    sgn = jnp.concatenate([-jnp.ones((32,), F32), jnp.ones((32,), F32), jnp.zeros((64,), F32)]).reshape(1, LANES)

    (proj, q, k, v), gathered = _fwd_proj(x, pos, g_in, w_in_e, g_cq, w_uq_e, g_ckv, w_ukv, gq, gk, invf, sgn,
                                          late_shards, min(2 * tm, T))
    w_o, w_pl, w_plg = _assemble_late(*(gathered if late_shards else late_gathered))
    o, lse = _attn_fwd(q, k, v, tq)
    (dx1, do, delta, dtail, du, dw_o, dw_pl, dw_plg, dg_oa, dg_oc, dg_pl, dconv, loss) = _tail(
        x, o, proj, p, tgt, g_oa, g_oc, g_pl, conv_w, w_o, w_pl, w_plg, tm)
    dq, dk, dv = _attn_bwd(q, k, v, do, lse, delta, tq)
    (gx, h, dproj, dw_uq_e, dw_ukv, dg_in, dg_cq, dg_ckv, dgq, dgk) = _bwd_proj(
        x, dx1, pos, proj, dq, dk, dv, dtail, du, g_in, w_in_e, g_cq, w_uq_e, g_ckv, w_ukv, gq, gk, conv_w, invf, sgn, tm)
    wgrads = (dw_uq_e, dw_ukv, dw_o, dw_pl, dw_plg)
    ggrads = (dg_in, dg_cq, dg_ckv, dgq, dgk, dg_oa, dg_oc, dg_pl)
    return loss, gx, (h, dproj), wgrads, ggrads, dconv


def kernel(x, p, positions, g_in, w_in, g_cq, w_uq, g_ckv, w_ukv, g_q, g_k, conv_w, g_oa, g_oc, w_o, w_pl, w_plg, g_pl, loss_target, m_g_in, m_w_in, m_g_cq, m_w_uq, m_g_ckv, m_w_ukv, m_g_q, m_g_k, m_conv_w, m_g_oa, m_g_oc, m_w_o, m_w_pl, m_w_plg, m_g_pl, v_g_in, v_w_in, v_g_cq, v_w_uq, v_g_ckv, v_w_ukv, v_g_q, v_g_k, v_conv_w, v_g_oa, v_g_oc, v_w_o, v_w_pl, v_w_plg, v_g_pl):
    T = x.shape[1]
    c = lax.axis_index("c")
    chip = 2 * lax.axis_index("x") + lax.axis_index("y")
    gains = [g.reshape(1, -1) for g in (g_in, g_cq, g_ckv, g_q, g_k, g_oa, g_oc, g_pl)]

    early = _assemble_early(*_gather_weights([w_in[0], w_uq[0], w_ukv[0], conv_w[0]]))

    loss, gx, (h_t, dproj), wgrads, ggrads, dconv = _local_step(
        x[0], p[0, 0], positions.reshape(T, 1), loss_target[0], gains, early, [w_o[0], w_pl[0], w_plg[0]], None, 256, 512)

    others_cm = _split_others(*wgrads)
    small_parts = [a.reshape(-1, LANES) for a in (*ggrads, loss, dconv)]
    small_rows = [a.shape[0] for a in small_parts]
    tile_rows = [-(-r // 8) * 8 for r in small_rows]
    tile_rows[-1] += -sum(tile_rows) % 16
    small = jnp.concatenate([jnp.pad(a, ((0, t - r), (0, 0))) for a, r, t in zip(small_parts, small_rows, tile_rows)])
    from_sibling, (small_sibling,) = _swap_halves(others_cm, [small], "pair_grads")
    chip_parts, chip_small = _add_pair(others_cm, from_sibling, small, small_sibling, c)
    dw_in_e, exchanged = _matmul_acc(h_t, dproj, min(4096, T), 512, [*chip_parts, chip_small])
    dw_in_e = dw_in_e[None]
    (w_in_sibling,), _ = _swap_halves([dw_in_e], [], "pair_w_in")
    by_chip = [_scatter_w_in(dw_in_e, w_in_sibling), *exchanged[:-1]]
    halves, small_total = _add_chips(by_chip, exchanged[-1])
    other_halves = _share_halves(halves)

    gg, off = [], 0
    for rows, tiled in zip(small_rows, tile_rows):
        gg.append(small_total[off:off + rows].reshape(1, -1))
        off += tiled
    loss_out = gg[8][0, 0]
    conv_total = gg[9].reshape(3, CONV_W)
    conv_g = lax.dynamic_slice(conv_total, (0, chip * (CONV_W // N_CHIPS)), (3, CONV_W // N_CHIPS))
    g_by_name = dict(g_in=gg[0], g_cq=gg[1], g_ckv=gg[2], g_q=gg[3][:, :QK_DIM], g_k=gg[4][:, :QK_DIM], conv_w=conv_g,
                     g_oa=gg[5], g_oc=gg[6], g_pl=gg[7])
    half_by_name = dict(zip(("w_in", "w_uq", "w_ukv", "w_o", "w_pl", "w_plg"), zip(halves, other_halves)))
    weights = dict(g_in=g_in, w_in=w_in, g_cq=g_cq, w_uq=w_uq, g_ckv=g_ckv, w_ukv=w_ukv, g_q=g_q, g_k=g_k,
                   conv_w=conv_w, g_oa=g_oa, g_oc=g_oc, w_o=w_o, w_pl=w_pl, w_plg=w_plg, g_pl=g_pl)
    ms = dict(g_in=m_g_in, w_in=m_w_in, g_cq=m_g_cq, w_uq=m_w_uq, g_ckv=m_g_ckv, w_ukv=m_w_ukv, g_q=m_g_q, g_k=m_g_k,
              conv_w=m_conv_w, g_oa=m_g_oa, g_oc=m_g_oc, w_o=m_w_o, w_pl=m_w_pl, w_plg=m_w_plg, g_pl=m_g_pl)
    vs = dict(g_in=v_g_in, w_in=v_w_in, g_cq=v_g_cq, w_uq=v_w_uq, g_ckv=v_g_ckv, w_ukv=v_w_ukv, g_q=v_g_q, g_k=v_g_k,
              conv_w=v_conv_w, g_oa=v_g_oa, g_oc=v_g_oc, w_o=v_w_o, w_pl=v_w_pl, w_plg=v_w_plg, g_pl=v_g_pl)
    names = list(weights)
    flat = lambda a: a.reshape(-1, a.shape[-1])
    small_names = list(g_by_name)
    small_out = _adamw_small([flat(weights[n]) for n in small_names], [flat(g_by_name[n]) for n in small_names],
                             [flat(ms[n]) for n in small_names], [flat(vs[n]) for n in small_names])
    results = {n: (flat(g_by_name[n]), *(out[i] for out in small_out)) for i, n in enumerate(small_names)}
    for n in half_by_name:
        results[n] = _adamw_halves(flat(weights[n]), *half_by_name[n], flat(ms[n]), flat(vs[n]), c, "adamw_" + n)
    per_kind = [[results[n][kind].reshape(weights[n].shape) for n in names] for kind in range(4)]
    return (loss_out, gx.reshape(x.shape), *per_kind[0], *per_kind[1], *per_kind[2], *per_kind[3])
```

```python
import math

import jax
import jax.numpy as jnp
from jax import lax
from jax.experimental import pallas as pl
from jax.experimental.pallas import tpu as pltpu

F32 = jnp.float32
BF16 = jnp.bfloat16

D_MODEL = 1024
N_HEADS = 4
NOPE = 128
ROPE = 64
V_DIM = 128
QK_DIM = NOPE + ROPE
HEAD_PAD = 256
Q_LORA = 256
KV_LORA = 128
ATTN_W = 512
CONV_W = 512
PLE = 256
IN_TOTAL = 3008
PROJ_EXT = 3072
ROPE_THETA = 10000.0
EPS = 1e-6
SCALE = 1.0 / math.sqrt(QK_DIM)
LOG2E = math.log2(math.e)
EXP2_SCALE = SCALE * LOG2E
NEG = -1e30
SOFTMAX_ROWS = 32
SUB_TILE = 256

LR, B1, B2, ADAM_EPS, WD, STEP = 0.001, 0.9, 0.999, 1e-08, 0.01, 10

N_CHIPS = 4
LANES = 128
VMEM_LIMIT = 56 * 1024 * 1024
MESH = pl.DeviceIdType.MESH


def _params(**kw):
    return pltpu.CompilerParams(vmem_limit_bytes=VMEM_LIMIT, **kw)


def _inv_rms(x, n):
    return lax.rsqrt(jnp.sum(x * x, axis=-1, keepdims=True) / n + EPS)


def _lane_sum(a):
    folded = a[:, 0:LANES]
    for c0 in range(LANES, a.shape[1], LANES):
        folded = folded + a[:, c0:c0 + LANES]
    head = folded.astype(BF16)
    tail = (folded - head.astype(F32)).astype(BF16)
    return _dot(jnp.concatenate([head, tail], axis=1), jnp.ones((2 * LANES, LANES), BF16))


def _inv_rms_mxu(x):
    return lax.rsqrt(_lane_sum(x * x) / x.shape[1] + EPS)


def _rep(r, width):
    return r if width == LANES else jnp.tile(r, (1, width // LANES))


def _sigmoid(z):
    return jax.nn.sigmoid(z)


def _swap_rope_halves(b):
    lane = lax.broadcasted_iota(jnp.int32, b.shape, 1)
    swapped = jnp.where(lane < 32, pltpu.roll(b, 96, 1), pltpu.roll(b, 32, 1))
    return jnp.where(lane < ROPE, swapped, 0.0)


def _dot(a, b):
    return jnp.dot(a, b, preferred_element_type=F32)


def _dot_nt(a, b):
    return lax.dot_general(a, b, (((1,), (1,)), ((), ())), preferred_element_type=F32)


def _dot_tn(a, b):
    return lax.dot_general(a, b, (((0,), (0,)), ((), ())), preferred_element_type=F32)


def _colsum(a):
    return jnp.sum(a, axis=0, keepdims=True)


def _full(shape):
    return pl.BlockSpec(shape, lambda *_: (0,) * len(shape))


def _round_robin(chains, width):
    waiting, active = list(chains), []
    while waiting or active:
        while waiting and len(active) < width:
            active.append(waiting.pop(0))
        for chain in list(active):
            if next(chain, _DONE) is _DONE:
                active.remove(chain)


_DONE = object()


def _rope_tables(pos_ref, invf_ref, sgn_ref):
    ang = pos_ref[...].astype(F32) * invf_ref[...]
    return jnp.cos(ang), jnp.sin(ang) * sgn_ref[...]


def _fwd_proj(x, pos, g_in, c_in, g_cq, w_uq, g_ckv, w_ukv, gq, gk, invf, sgn, late_shards, tm):
    T = x.shape[0]
    nt = T // tm
    n_late = len(late_shards)
    ts = min(SUB_TILE, tm)

    def body(x_ref, pos_ref, g_in_ref, c_in_ref, g_cq_ref, w_uq_ref, g_ckv_ref, w_ukv_ref, gq_ref, gk_ref,
             invf_ref, sgn_ref, *rest):
        late_in, (proj_ref, q_ref, k_ref, v_ref, w_in_ref) = rest[:n_late], rest[n_late:n_late + 5]
        late_out, late_scratch = rest[n_late + 5:2 * n_late + 5], rest[2 * n_late + 5:]
        i = pl.program_id(0)

        @pl.when(i == 0)
        def _():
            w_in_ref[:, 0:KPE_END] = c_in_ref[0, :, 0:KPE_END]
            w_in_ref[:, KPE_END:KPE_END + KPE_PAD] = jnp.zeros((D_MODEL, KPE_PAD), BF16)
            w_in_ref[:, KPE_END + KPE_PAD:SHARD_COLS_IN + KPE_PAD] = c_in_ref[0, :, KPE_END:SHARD_COLS_IN]
            for chip in range(1, N_CHIPS):
                w_in_ref[:, SHARD_COLS_IN * chip + KPE_PAD:SHARD_COLS_IN * (chip + 1) + KPE_PAD] = c_in_ref[chip]

        if n_late:
            start, forward, drain = _gather_steps([s.shape for s in late_shards], late_in, late_out,
                                                  late_scratch[:n_late], *late_scratch[n_late:])
            pl.when(i == 0)(start)
            pl.when(i == nt // 2)(forward)

        for r0 in range(0, tm, ts):
            rows = slice(r0, r0 + ts)
            xv = x_ref[rows, :]
            h = (xv * _rep(_inv_rms_mxu(xv), D_MODEL) * g_in_ref[...]).astype(BF16)
            lat = _dot(h, w_in_ref[:, 0:512])
            proj_ref[rows, 0:512] = lat
            c_q = lat[:, 0:Q_LORA]
            cqn = (c_q * _rep(_inv_rms_mxu(c_q), Q_LORA) * g_cq_ref[...]).astype(BF16)
            c_kv = lat[:, Q_LORA:Q_LORA + KV_LORA]
            ckvn = (c_kv * _inv_rms_mxu(c_kv) * g_ckv_ref[...]).astype(BF16)
            kpe = lat[:, 384:512]
            kpe_sq = kpe * kpe
            cos_b, sin_b = _rope_tables(pos_ref.at[rows, :], invf_ref, sgn_ref)
            gq_a, gq_b = gq_ref[:, 0:NOPE], gq_ref[:, NOPE:HEAD_PAD]
            gk_a, gk_b = gk_ref[:, 0:NOPE], gk_ref[:, NOPE:HEAD_PAD]

            def projections(rows=rows, h=h):
                for c0 in range(512, PROJ_EXT, 512):
                    proj_ref[rows, c0:c0 + 512] = _dot(h, w_in_ref[:, c0:c0 + 512])
                    yield

            def queries(hd, rows=rows, cqn=cqn, cos_b=cos_b, sin_b=sin_b, gq_a=gq_a, gq_b=gq_b):
                qh = _dot(cqn, w_uq_ref[hd])
                yield
                a, b = qh[:, 0:NOPE], qh[:, NOPE:HEAD_PAD]
                r = lax.rsqrt(_lane_sum(a * a + b * b) / QK_DIM + EPS)
                yield
                bn = b * r * gq_b
                q_ref[hd, rows, 0:NOPE] = (a * r * gq_a).astype(BF16)
                q_ref[hd, rows, NOPE:HEAD_PAD] = (bn * cos_b + _swap_rope_halves(bn) * sin_b).astype(BF16)
                yield

            def keys(hd, rows=rows, ckvn=ckvn, kpe=kpe, kpe_sq=kpe_sq, cos_b=cos_b, sin_b=sin_b, gk_a=gk_a, gk_b=gk_b):
                kvh = _dot(ckvn, w_ukv_ref[hd])
                yield
                ka = kvh[:, 0:NOPE]
                rk = lax.rsqrt(_lane_sum(ka * ka + kpe_sq) / QK_DIM + EPS)
                yield
                kbn = kpe * rk * gk_b
                k_ref[hd, rows, 0:NOPE] = (ka * rk * gk_a).astype(BF16)
                k_ref[hd, rows, NOPE:HEAD_PAD] = (kbn * cos_b + _swap_rope_halves(kbn) * sin_b).astype(BF16)
                v_ref[hd, rows, 0:V_DIM] = kvh[:, NOPE:HEAD_PAD].astype(BF16)
                v_ref[hd, rows, V_DIM:2 * V_DIM] = jnp.ones((ts, V_DIM), BF16)
                yield

            chains = [projections()]
            for hd in range(N_HEADS):
                chains += [queries(hd), keys(hd)]
            _round_robin(chains, 4)

        if n_late:
            pl.when(i == nt - 1)(drain)

    row = lambda i: (i, 0)
    head_rows = lambda i: (0, i, 0)
    outs = pl.pallas_call(
        body, name="fwd_proj", grid=(nt,),
        in_specs=[pl.BlockSpec((tm, D_MODEL), row), pl.BlockSpec((tm, 1), row), _full((1, D_MODEL)),
                  _full((N_CHIPS, D_MODEL, SHARD_COLS_IN)), _full((1, Q_LORA)), _full((N_HEADS, Q_LORA, HEAD_PAD)),
                  _full((1, KV_LORA)), _full((N_HEADS, KV_LORA, HEAD_PAD)), _full((1, HEAD_PAD)), _full((1, HEAD_PAD)),
                  _full((1, LANES)), _full((1, LANES))] + [_full(s.shape) for s in late_shards],
        out_specs=[pl.BlockSpec((tm, PROJ_EXT), row), pl.BlockSpec((N_HEADS, tm, HEAD_PAD), head_rows),
                   pl.BlockSpec((N_HEADS, tm, HEAD_PAD), head_rows), pl.BlockSpec((N_HEADS, tm, 2 * V_DIM), head_rows),
                   _full((D_MODEL, PROJ_EXT))] + [_ANY] * n_late,
        out_shape=[jax.ShapeDtypeStruct((T, PROJ_EXT), F32), jax.ShapeDtypeStruct((N_HEADS, T, HEAD_PAD), BF16),
                   jax.ShapeDtypeStruct((N_HEADS, T, HEAD_PAD), BF16), jax.ShapeDtypeStruct((N_HEADS, T, 2 * V_DIM), BF16),
                   jax.ShapeDtypeStruct((D_MODEL, PROJ_EXT), BF16)] + _gathered_shapes(late_shards),
        scratch_shapes=_gather_scratch(late_shards) if n_late else [],
        compiler_params=_params(dimension_semantics=("arbitrary",)),
    )(x, pos, g_in, c_in, g_cq, w_uq, g_ckv, w_ukv, gq, gk, invf, sgn, *late_shards)
    return outs[:5], outs[5:]


def _chunk_pipeline(n_loop, lag, matmuls, pointwise, accumulate, last):
    slots = lag + 1

    def iteration(t, slot, pending=True):
        matmuls(jnp.minimum(t + lag, n_loop), (slot + lag) % slots)
        if pending:
            accumulate(t - lag, (slot + 1) % slots)
        pointwise(t, slot, False)

    def finish(slot, pending):
        for back in range(pending, 0, -1):
            accumulate(n_loop - back, (slot - back) % slots)
        pointwise(n_loop, slot, True)
        accumulate(n_loop, slot)
        last()

    for u in range(lag):
        matmuls(jnp.minimum(u, n_loop), u)
    for u in range(lag):
        pl.when(u < n_loop)(lambda u=u: iteration(u, u, pending=False))

    n_main = jnp.maximum(n_loop - lag, 0)

    def unrolled(tt, carry):
        for j in range(slots):
            iteration(lag + slots * tt + j, (lag + j) % slots)
        return carry

    lax.fori_loop(0, n_main // slots, unrolled, 0)
    rest = lax.rem(n_main, slots)
    t0 = n_loop - rest

    for r in range(slots):
        @pl.when(jnp.logical_and(n_loop >= lag, rest == r))
        def _():
            for j in range(r):
                iteration(t0 + j, (lag + j) % slots)
            finish((lag + r) % slots, lag)

    for short in range(lag):
        pl.when(n_loop == short)(lambda short=short: finish(short, short))


def _attn_fwd(q, k, v, tq):
    T = q.shape[1]
    tk = tq
    rc = min(SOFTMAX_ROWS, tq)

    def body(q_ref, k_ref, v_ref, o_ref, lse_ref, s0, s1, s2, p0, p1, p2, a0, a1, a2, m_ref, acc_ref):
        qi = pl.program_id(1)
        s_buf, p_buf, a_buf = (s0, s1, s2), (p0, p1, p2), (a0, a1, a2)

        def scores(t, slot):
            ks = pl.multiple_of(t * tk, tk)
            s_buf[slot][...] = _dot_nt(q_ref[0], k_ref[0, pl.ds(ks, tk), :])

        def values(t, slot):
            ks = pl.multiple_of(t * tk, tk)
            acc_ref[...] = acc_ref[...] * a_buf[slot][...] + _dot(p_buf[slot][...], v_ref[0, pl.ds(ks, tk), :])

        def softmax(t, slot, masked):
            s_all = s_buf[slot][...]
            if masked:
                row = lax.broadcasted_iota(jnp.int32, (tq, tk), 0)
                col = lax.broadcasted_iota(jnp.int32, (tq, tk), 1)
                s_all = jnp.where(col <= row, s_all, NEG)
                s_buf[slot][...] = s_all
            m_old = m_ref[...]
            m_new = jnp.maximum(m_old, jnp.max(s_all, axis=1, keepdims=True))
            a_buf[slot][...] = jnp.exp2((m_old - m_new) * EXP2_SCALE)
            m_ref[...] = m_new
            for r0 in range(0, tq, rc):
                s = s_buf[slot][r0:r0 + rc, :]
                p_buf[slot][r0:r0 + rc, :] = jnp.exp2((s - m_new[r0:r0 + rc, :]) * EXP2_SCALE).astype(BF16)

        def last():
            l = acc_ref[:, V_DIM:2 * V_DIM]
            o_ref[...] = acc_ref[:, 0:V_DIM] / l
            lse_ref[0] = (m_ref[...] * SCALE + jnp.log(l)).T[0:1, :]

        m_ref[...] = jnp.full_like(m_ref, NEG)
        acc_ref[...] = jnp.zeros_like(acc_ref)
        _chunk_pipeline(qi, 2, scores, softmax, values, last)

    return pl.pallas_call(
        body, name="attn_fwd", grid=(N_HEADS, T // tq),
        in_specs=[pl.BlockSpec((1, tq, HEAD_PAD), lambda h, i: (h, i, 0)),
                  pl.BlockSpec((1, T, HEAD_PAD), lambda h, i: (h, 0, 0)),
                  pl.BlockSpec((1, T, 2 * V_DIM), lambda h, i: (h, 0, 0))],
        out_specs=[pl.BlockSpec((tq, V_DIM), lambda h, i: (i, h)),
                   pl.BlockSpec((1, 1, tq), lambda h, i: (h, 0, i))],
        out_shape=[jax.ShapeDtypeStruct((T, ATTN_W), F32), jax.ShapeDtypeStruct((N_HEADS, 1, T), F32)],
        scratch_shapes=[pltpu.VMEM((tq, tk), F32)] * 3 + [pltpu.VMEM((tq, tk), BF16)] * 3
                       + [pltpu.VMEM((tq, 1), F32)] * 4 + [pltpu.VMEM((tq, 2 * V_DIM), F32)],
        compiler_params=_params(dimension_semantics=("arbitrary", "arbitrary")),
    )(q, k, v)


def _tail(x, o, proj, p, tgt, g_oa, g_oc, g_pl, conv_w, w_o, w_pl, w_plg, tm):
    T = x.shape[0]
    nt = T // tm

    def body(x_ref, o_ref, za_ref, cb_ref, cc_ref, cx_ref, zc_ref, cch_ref, cxh_ref, p_ref, tgt_ref,
             g_oa_ref, g_oc_ref, g_pl_ref, cw_ref, w_o_ref, w_pl_ref, w_plg_ref,
             dx1_ref, do_ref, delta_ref, dtail_ref, du_ref,
             dw_o_ref, dw_pl_ref, dw_plg_ref, dg_oa_ref, dg_oc_ref, dg_pl_ref, dcw_ref, loss_ref):
        i = pl.program_id(0)

        @pl.when(i == 0)
        def _():
            for r in (dw_o_ref, dw_pl_ref, dw_plg_ref, dg_oa_ref, dg_oc_ref, dg_pl_ref, dcw_ref, loss_ref):
                r[...] = jnp.zeros_like(r)

        g_oa, g_oc, g_pl = g_oa_ref[...], g_oc_ref[...], g_pl_ref[...]
        w0, w1, w2 = cw_ref[0:1, :], cw_ref[1:2, :], cw_ref[2:3, :]

        xv, ov, za, cb, zc = x_ref[...], o_ref[...], za_ref[...], cb_ref[...], zc_ref[...]
        pb = p_ref[...].astype(BF16)
        pp = _dot(pb, w_pl_ref[...])

        sa = _sigmoid(za)
        silu_a = za * sa
        ga = ov * silu_a
        ra = _inv_rms(ga, ATTN_W)
        xa = ga * ra
        ya = (xa * g_oa).astype(BF16)
        x1_a = _dot(ya, w_o_ref[0:ATTN_W, :])
        v = cc_ref[...] * cx_ref[...]
        not_first = jnp.where(i > 0, 1.0, 0.0)
        hv6 = cch_ref[6:7, :] * cxh_ref[6:7, :] * not_first
        hv7 = cch_ref[7:8, :] * cxh_ref[7:8, :] * not_first
        row = lax.broadcasted_iota(jnp.int32, v.shape, 0)
        v1 = jnp.where(row == 0, hv7, pltpu.roll(v, 1, 0))
        v2 = jnp.where(row == 0, hv6, jnp.where(row == 1, hv7, pltpu.roll(v, 2, 0)))
        u = w0 * v2 + w1 * v1 + w2 * v
        sc = _sigmoid(zc)
        silu_c = zc * sc
        gc = cb * u * silu_c
        rc = _inv_rms(gc, CONV_W)
        xc = gc * rc
        yc = (xc * g_oc).astype(BF16)
        x1 = xv + (x1_a + _dot(yc, w_o_ref[ATTN_W:D_MODEL, :]))
        r1 = _inv_rms(x1, D_MODEL)
        xh1 = x1 * r1
        n1 = (xh1 * g_pl).astype(BF16)
        gate = _sigmoid(_dot(n1, w_plg_ref[...]))
        err = x1 + gate * pp - tgt_ref[...]
        loss_ref[...] += 0.5 * jnp.sum(err * err) / D_MODEL
        dy = err / D_MODEL

        dpp = (dy * gate).astype(BF16)
        da = (dy * pp * gate * (1.0 - gate)).astype(BF16)
        dn1 = _dot_nt(da, w_plg_ref[...])
        dw_pl_ref[...] += _dot_tn(pb, dpp)
        dw_plg_ref[...] += _dot_tn(n1, da)
        dg_pl_ref[...] += _colsum(dn1 * xh1)
        dxh = dn1 * g_pl
        dx1 = dy + r1 * (dxh - xh1 * (jnp.sum(dxh * xh1, axis=-1, keepdims=True) / D_MODEL))
        dx1_ref[...] = dx1
        dx1b = dx1.astype(BF16)
        dya = _dot_nt(dx1b, w_o_ref[0:ATTN_W, :])
        dyc = _dot_nt(dx1b, w_o_ref[ATTN_W:D_MODEL, :])

        dw_o_ref[0:ATTN_W, :] += _dot_tn(ya, dx1b)
        dg_oa_ref[...] += _colsum(dya * xa)
        dxa = dya * g_oa
        dga = ra * (dxa - xa * (jnp.sum(dxa * xa, axis=-1, keepdims=True) / ATTN_W))
        do = (dga * silu_a).astype(BF16)
        do_ref[...] = do
        dof = do.astype(F32) * ov
        for hd in range(N_HEADS):
            delta_ref[hd] = _lane_sum(dof[:, hd * V_DIM:(hd + 1) * V_DIM]).T[0:1, :]
        dtail_ref[:, 0:512] = (dga * ov * (sa * (1.0 + za * (1.0 - sa)))).astype(BF16)

        dw_o_ref[ATTN_W:D_MODEL, :] += _dot_tn(yc, dx1b)
        dg_oc_ref[...] += _colsum(dyc * xc)
        dxc = dyc * g_oc
        dgc = rc * (dxc - xc * (jnp.sum(dxc * xc, axis=-1, keepdims=True) / CONV_W))
        dtail_ref[:, 512:1024] = (dgc * u * silu_c).astype(BF16)
        du = dgc * cb * silu_c
        du_ref[...] = du
        dtail_ref[:, 1024:1536] = (dgc * cb * u * (sc * (1.0 + zc * (1.0 - sc)))).astype(BF16)
        dcw_ref[0:1, :] += _colsum(du * v2)
        dcw_ref[1:2, :] += _colsum(du * v1)
        dcw_ref[2:3, :] += _colsum(du * v)

    row = lambda i: (i, 0)
    col = lambda c: (lambda i: (i, c))
    halo = lambda c: (lambda i: (jnp.maximum(i * (tm // 8) - 1, 0), c))
    in_specs = [pl.BlockSpec((tm, D_MODEL), row), pl.BlockSpec((tm, ATTN_W), row)]
    in_specs += [pl.BlockSpec((tm, 512), col(c)) for c in (1, 2, 3, 4, 5)]
    in_specs += [pl.BlockSpec((8, 512), halo(3)), pl.BlockSpec((8, 512), halo(4))]
    in_specs += [pl.BlockSpec((tm, PLE), row), pl.BlockSpec((tm, D_MODEL), row),
                 _full((1, ATTN_W)), _full((1, CONV_W)), _full((1, D_MODEL)), _full((3, CONV_W)),
                 _full((D_MODEL, D_MODEL)), _full((PLE, D_MODEL)), _full((D_MODEL, D_MODEL))]
    out_specs = [pl.BlockSpec((tm, D_MODEL), row), pl.BlockSpec((tm, ATTN_W), row),
                 pl.BlockSpec((N_HEADS, 1, tm), lambda i: (0, 0, i)), pl.BlockSpec((tm, 1536), row),
                 pl.BlockSpec((tm, CONV_W), row),
                 _full((D_MODEL, D_MODEL)), _full((PLE, D_MODEL)), _full((D_MODEL, D_MODEL)),
                 _full((1, ATTN_W)), _full((1, CONV_W)), _full((1, D_MODEL)), _full((3, CONV_W)), _full((1, LANES))]
    out_shape = [jax.ShapeDtypeStruct((T, D_MODEL), F32), jax.ShapeDtypeStruct((T, ATTN_W), BF16),
                 jax.ShapeDtypeStruct((N_HEADS, 1, T), F32), jax.ShapeDtypeStruct((T, 1536), BF16),
                 jax.ShapeDtypeStruct((T, CONV_W), F32),
                 jax.ShapeDtypeStruct((D_MODEL, D_MODEL), F32), jax.ShapeDtypeStruct((PLE, D_MODEL), F32),
                 jax.ShapeDtypeStruct((D_MODEL, D_MODEL), F32),
                 jax.ShapeDtypeStruct((1, ATTN_W), F32), jax.ShapeDtypeStruct((1, CONV_W), F32),
                 jax.ShapeDtypeStruct((1, D_MODEL), F32), jax.ShapeDtypeStruct((3, CONV_W), F32),
                 jax.ShapeDtypeStruct((1, LANES), F32)]
    return pl.pallas_call(
        body, name="tail", grid=(nt,), in_specs=in_specs, out_specs=out_specs, out_shape=out_shape,
        compiler_params=_params(dimension_semantics=("arbitrary",)),
    )(x, o, proj, proj, proj, proj, proj, proj, proj, p, tgt, g_oa, g_oc, g_pl, conv_w, w_o, w_pl, w_plg)


def _attn_bwd(q, k, v, do, lse_row, delta_row, tk):
    T = q.shape[1]
    tq = tk
    nq = T // tq
    rc = min(SOFTMAX_ROWS, tk)

    def body(q_ref, k_ref, v_ref, do_ref, lse_ref, dl_ref, dq_ref, dk_ref, dv_ref,
             s0, s1, d0, d1, p0, p1, g0, g1, dk_acc, dv_acc):
        kj = pl.program_id(1)
        s_buf, dp_buf, p_buf, g_buf = (s0, s1), (d0, d1), (p0, p1), (g0, g1)

        @pl.when(kj == 0)
        def _():
            dq_ref[...] = jnp.zeros_like(dq_ref)

        def q_start(t):
            return pl.multiple_of((nq - 1 - t) * tq, tq)

        def matmuls(t, slot):
            qs = q_start(t)
            s_buf[slot][...] = _dot_nt(k_ref[0], q_ref[0, pl.ds(qs, tq), :])
            dp_buf[slot][...] = _dot_nt(v_ref[0], do_ref[pl.ds(qs, tq), :])

        def pointwise(t, slot, masked):
            qs = q_start(t)
            lse2 = lse_ref[0, :, pl.ds(qs, tq)] * LOG2E
            dl = dl_ref[0, :, pl.ds(qs, tq)]
            for r0 in range(0, tk, rc):
                st = s_buf[slot][r0:r0 + rc, :]
                if masked:
                    row = lax.broadcasted_iota(jnp.int32, (rc, tq), 0)
                    col = lax.broadcasted_iota(jnp.int32, (rc, tq), 1)
                    st = jnp.where(row + r0 <= col, st, NEG)
                pt = jnp.exp2(st * EXP2_SCALE - lse2)
                p_buf[slot][r0:r0 + rc, :] = pt.astype(BF16)
                g_buf[slot][r0:r0 + rc, :] = (pt * (dp_buf[slot][r0:r0 + rc, :] - dl) * SCALE).astype(BF16)

        def accumulate(t, slot):
            qs = q_start(t)
            dv_acc[...] += _dot(p_buf[slot][...], do_ref[pl.ds(qs, tq), :])
            dk_acc[...] += _dot(g_buf[slot][...], q_ref[0, pl.ds(qs, tq), :])
            dq_ref[0, pl.ds(qs, tq), :] += _dot_tn(g_buf[slot][...], k_ref[0])

        def last():
            dk_ref[0] = dk_acc[...]
            dv_ref[0] = dv_acc[...]

        dk_acc[...] = jnp.zeros_like(dk_acc)
        dv_acc[...] = jnp.zeros_like(dv_acc)
        _chunk_pipeline(nq - 1 - kj, 1, matmuls, pointwise, accumulate, last)

    return pl.pallas_call(
        body, name="attn_bwd", grid=(N_HEADS, T // tk),
        in_specs=[pl.BlockSpec((1, T, HEAD_PAD), lambda h, j: (h, 0, 0)),
                  pl.BlockSpec((1, tk, HEAD_PAD), lambda h, j: (h, j, 0)),
                  pl.BlockSpec((1, tk, V_DIM), lambda h, j: (h, j, 0)),
                  pl.BlockSpec((T, V_DIM), lambda h, j: (0, h)),
                  pl.BlockSpec((1, 1, T), lambda h, j: (h, 0, 0)),
                  pl.BlockSpec((1, 1, T), lambda h, j: (h, 0, 0))],
        out_specs=[pl.BlockSpec((1, T, HEAD_PAD), lambda h, j: (h, 0, 0)),
                   pl.BlockSpec((1, tk, HEAD_PAD), lambda h, j: (h, j, 0)),
                   pl.BlockSpec((1, tk, V_DIM), lambda h, j: (h, j, 0))],
        out_shape=[jax.ShapeDtypeStruct((N_HEADS, T, HEAD_PAD), F32), jax.ShapeDtypeStruct((N_HEADS, T, HEAD_PAD), F32),
                   jax.ShapeDtypeStruct((N_HEADS, T, V_DIM), F32)],
        scratch_shapes=[pltpu.VMEM((tk, tq), F32)] * 4 + [pltpu.VMEM((tk, tq), BF16)] * 4
                       + [pltpu.VMEM((tk, HEAD_PAD), F32), pltpu.VMEM((tk, V_DIM), F32)],
        compiler_params=_params(dimension_semantics=("arbitrary", "arbitrary")),
    )(q, k, v, do, lse_row, delta_row)


def _bwd_proj(x, dx1, pos, proj, dq, dk, dv, dtail, du, g_in, w_in, g_cq, w_uq, g_ckv, w_ukv, gq, gk, conv_w,
              invf, sgn, tm):
    T = x.shape[0]
    nt = T // tm

    ts = min(SUB_TILE, tm)

    def body(x_ref, dx1_ref, pos_ref, lat_ref, cc_ref, cx_ref, dq_ref, dk_ref, dv_ref, dtail_ref, du_ref, dun_ref, *rest):
        consts, (gx_ref, h_ref, dproj_ref), sums = rest[:11], rest[11:14], rest[14:]
        cw_ref = consts[8]
        i = pl.program_id(0)

        @pl.when(i == 0)
        def _():
            for r in sums:
                r[...] = jnp.zeros_like(r)

        du_v = du_ref[...]
        not_last = jnp.where(i < nt - 1, 1.0, 0.0)
        nx0 = dun_ref[0:1, :] * not_last
        nx1 = dun_ref[1:2, :] * not_last
        row = lax.broadcasted_iota(jnp.int32, du_v.shape, 0)
        du1 = jnp.where(row == tm - 1, nx0, pltpu.roll(du_v, tm - 1, 0))
        du2 = jnp.where(row == tm - 2, nx0, jnp.where(row == tm - 1, nx1, pltpu.roll(du_v, tm - 2, 0)))
        dvc = cw_ref[2:3, :] * du_v + cw_ref[1:2, :] * du1 + cw_ref[0:1, :] * du2
        dproj_ref[:, 1536:2048] = (dvc * cx_ref[...]).astype(BF16)
        dproj_ref[:, 2048:2560] = (dvc * cc_ref[...]).astype(BF16)

        for r0 in range(0, tm, ts):
            rows = slice(r0, r0 + ts)
            work(x_ref.at[rows, :], dx1_ref.at[rows, :], pos_ref.at[rows, :], lat_ref.at[rows, :],
                 dq_ref.at[:, rows, :], dk_ref.at[:, rows, :], dv_ref.at[:, rows, :], dtail_ref.at[rows, :], *consts,
                 gx_ref.at[rows, :], h_ref.at[:, rows], dproj_ref.at[rows, :], *sums)

    def work(x_ref, dx1_ref, pos_ref, lat_ref, dq_ref, dk_ref, dv_ref, dtail_ref,
             g_in_ref, w_in_ref, g_cq_ref, w_uq_ref, g_ckv_ref, w_ukv_ref, gq_ref, gk_ref, cw_ref, invf_ref, sgn_ref,
             gx_ref, h_ref, dproj_ref, dw_uq_ref, dw_ukv_ref, dg_in_ref, dg_cq_ref, dg_ckv_ref, dgq_ref, dgk_ref):
        xv = x_ref[...]
        r0 = _rep(_inv_rms_mxu(xv), D_MODEL)
        xh0 = xv * r0
        g_in = g_in_ref[...]
        h_ref[...] = (xh0 * g_in).astype(BF16).T

        c_q = lat_ref[:, 0:Q_LORA]
        rq = _rep(_inv_rms_mxu(c_q), Q_LORA)
        xq = c_q * rq
        g_cq = g_cq_ref[...]
        cqn = (xq * g_cq).astype(BF16)
        c_kv = lat_ref[:, Q_LORA:Q_LORA + KV_LORA]
        rkv = _inv_rms_mxu(c_kv)
        xkv = c_kv * rkv
        g_ckv = g_ckv_ref[...]
        ckvn = (xkv * g_ckv).astype(BF16)
        kpe = lat_ref[:, 384:512]
        kpe_sq = kpe * kpe
        cos_b, sin_b = _rope_tables(pos_ref, invf_ref, sgn_ref)
        gq_a, gq_b = gq_ref[:, 0:NOPE], gq_ref[:, NOPE:HEAD_PAD]
        gk_a, gk_b = gk_ref[:, 0:NOPE], gk_ref[:, NOPE:HEAD_PAD]

        dproj_ref[:, 512:1536] = dtail_ref[:, 0:1024]
        dproj_ref[:, 2560:3072] = dtail_ref[:, 1024:1536]

        def dh_part(c0):
            return _dot_nt(dproj_ref[:, c0:c0 + 512], w_in_ref[:, c0:c0 + 512])

        later_chunks = ((512,), (1024,), (1536, 2048), (2560,))
        dh = jnp.zeros((ts, D_MODEL), F32)
        acc = dict(dh=dh, dkpe=jnp.zeros((ts, LANES), F32), dcqn=jnp.zeros((ts, Q_LORA), F32),
                   dckvn=jnp.zeros((ts, KV_LORA), F32))

        def dh_chunks():
            for chunks in later_chunks:
                for chunk in chunks:
                    acc["dh"] = acc["dh"] + dh_part(chunk)
                    yield

        def queries(hd):
            qh = _dot(cqn, w_uq_ref[hd])
            yield
            a, b = qh[:, 0:NOPE], qh[:, NOPE:HEAD_PAD]
            r = lax.rsqrt(_lane_sum(a * a + b * b) / QK_DIM + EPS)
            yield
            xa, xb = a * r, b * r
            dan = dq_ref[hd, :, 0:NOPE]
            dbr = dq_ref[hd, :, NOPE:HEAD_PAD]
            dbn = dbr * cos_b + _swap_rope_halves(dbr * sin_b)
            yield
            dgq_ref[:, 0:NOPE] += _colsum(dan * xa)
            dgq_ref[:, NOPE:HEAD_PAD] += _colsum(dbn * xb)
            dxa, dxb = dan * gq_a, dbn * gq_b
            cq = _lane_sum(dxa * xa + dxb * xb) / QK_DIM
            yield
            dqh = jnp.concatenate([r * (dxa - xa * cq), r * (dxb - xb * cq)], axis=-1).astype(BF16)
            yield
            dw_uq_ref[hd] += _dot_tn(cqn, dqh)
            yield
            acc["dcqn"] = acc["dcqn"] + _dot_nt(dqh, w_uq_ref[hd])
            yield

        def keys(hd):
            kvh = _dot(ckvn, w_ukv_ref[hd])
            yield
            ka = kvh[:, 0:NOPE]
            rk = lax.rsqrt(_lane_sum(ka * ka + kpe_sq) / QK_DIM + EPS)
            yield
            xka, xkb = ka * rk, kpe * rk
            dkan = dk_ref[hd, :, 0:NOPE]
            dkbr = dk_ref[hd, :, NOPE:HEAD_PAD]
            dkbn = dkbr * cos_b + _swap_rope_halves(dkbr * sin_b)
            yield
            dgk_ref[:, 0:NOPE] += _colsum(dkan * xka)
            dgk_ref[:, NOPE:HEAD_PAD] += _colsum(dkbn * xkb)
            dxka, dxkb = dkan * gk_a, dkbn * gk_b
            ck = _lane_sum(dxka * xka + dxkb * xkb) / QK_DIM
            yield
            acc["dkpe"] = acc["dkpe"] + rk * (dxkb - xkb * ck)
            dkvh = jnp.concatenate([rk * (dxka - xka * ck), dv_ref[hd]], axis=-1).astype(BF16)
            yield
            dw_ukv_ref[hd] += _dot_tn(ckvn, dkvh)
            yield
            acc["dckvn"] = acc["dckvn"] + _dot_nt(dkvh, w_ukv_ref[hd])
            yield

        chains = [dh_chunks()]
        for hd in range(N_HEADS):
            chains += [queries(hd), keys(hd)]
        _round_robin(chains, 5)
        dh, dkpe, dcqn, dckvn = acc["dh"], acc["dkpe"], acc["dcqn"], acc["dckvn"]

        dg_cq_ref[...] += _colsum(dcqn * xq)
        dxq = dcqn * g_cq
        dproj_ref[:, 0:Q_LORA] = (rq * (dxq - xq * _rep(_lane_sum(dxq * xq) / Q_LORA, Q_LORA))).astype(BF16)
        dg_ckv_ref[...] += _colsum(dckvn * xkv)
        dxkv = dckvn * g_ckv
        dproj_ref[:, 256:384] = (rkv * (dxkv - xkv * (_lane_sum(dxkv * xkv) / KV_LORA))).astype(BF16)
        dproj_ref[:, 384:512] = dkpe.astype(BF16)
        dh = dh + dh_part(0)
        dg_in_ref[...] += _colsum(dh * xh0)
        dxh = dh * g_in
        gx_ref[...] = dx1_ref[...] + r0 * (dxh - xh0 * _rep(_lane_sum(dxh * xh0) / D_MODEL, D_MODEL))

    row = lambda i: (i, 0)
    col = lambda c: (lambda i: (i, c))
    head_rows = lambda i: (0, i, 0)
    nxt = lambda i: (jnp.minimum((i + 1) * (tm // 8), T // 8 - 1), 0)
    in_specs = [pl.BlockSpec((tm, D_MODEL), row), pl.BlockSpec((tm, D_MODEL), row), pl.BlockSpec((tm, 1), row),
                pl.BlockSpec((tm, 512), col(0)), pl.BlockSpec((tm, 512), col(3)), pl.BlockSpec((tm, 512), col(4)),
                pl.BlockSpec((N_HEADS, tm, HEAD_PAD), head_rows), pl.BlockSpec((N_HEADS, tm, HEAD_PAD), head_rows),
                pl.BlockSpec((N_HEADS, tm, V_DIM), head_rows), pl.BlockSpec((tm, 1536), row),
                pl.BlockSpec((tm, CONV_W), row), pl.BlockSpec((8, CONV_W), nxt),
                _full((1, D_MODEL)), _full((D_MODEL, PROJ_EXT)), _full((1, Q_LORA)), _full((N_HEADS, Q_LORA, HEAD_PAD)),
                _full((1, KV_LORA)), _full((N_HEADS, KV_LORA, HEAD_PAD)), _full((1, HEAD_PAD)), _full((1, HEAD_PAD)),
                _full((3, CONV_W)), _full((1, LANES)), _full((1, LANES))]
    out_specs = [pl.BlockSpec((tm, D_MODEL), row), pl.BlockSpec((D_MODEL, tm), lambda i: (0, i)),
                 pl.BlockSpec((tm, PROJ_EXT), row),
                 _full((N_HEADS, Q_LORA, HEAD_PAD)), _full((N_HEADS, KV_LORA, HEAD_PAD)),
                 _full((1, D_MODEL)), _full((1, Q_LORA)), _full((1, KV_LORA)), _full((1, HEAD_PAD)), _full((1, HEAD_PAD))]
    out_shape = [jax.ShapeDtypeStruct((T, D_MODEL), F32), jax.ShapeDtypeStruct((D_MODEL, T), BF16),
                 jax.ShapeDtypeStruct((T, PROJ_EXT), BF16),
                 jax.ShapeDtypeStruct((N_HEADS, Q_LORA, HEAD_PAD), F32), jax.ShapeDtypeStruct((N_HEADS, KV_LORA, HEAD_PAD), F32),
                 jax.ShapeDtypeStruct((1, D_MODEL), F32), jax.ShapeDtypeStruct((1, Q_LORA), F32),
                 jax.ShapeDtypeStruct((1, KV_LORA), F32), jax.ShapeDtypeStruct((1, HEAD_PAD), F32),
                 jax.ShapeDtypeStruct((1, HEAD_PAD), F32)]
    return pl.pallas_call(
        body, name="bwd_proj", grid=(nt,), in_specs=in_specs, out_specs=out_specs, out_shape=out_shape,
        compiler_params=_params(dimension_semantics=("arbitrary",)),
    )(x, dx1, pos, proj, proj, proj, dq, dk, dv, dtail, du, du, g_in, w_in, g_cq, w_uq, g_ckv, w_ukv, gq, gk, conv_w,
      invf, sgn)


def _matmul_acc(a, b, tt, tn, parts):
    M, T = a.shape
    N = b.shape[1]
    n = len(parts)
    grid = (N // tn, T // tt)

    def body(a_ref, b_ref, *rest):
        part_refs, o_ref, out_refs, sems = rest[:n], rest[n], rest[n + 1:2 * n + 1], rest[2 * n + 1:]
        j, t = pl.program_id(0), pl.program_id(1)
        if n:
            start, drain = _scatter_steps(part_refs, out_refs, *sems)
            pl.when(jnp.logical_and(j == 0, t == 0))(start)

        @pl.when(t == 0)
        def _():
            o_ref[...] = jnp.zeros_like(o_ref)

        o_ref[...] += _dot(a_ref[...], b_ref[...])
        if n:
            pl.when(jnp.logical_and(j == grid[0] - 1, t == grid[1] - 1))(drain)

    sems = [pltpu.SemaphoreType.DMA((3 * n,)), pltpu.SemaphoreType.DMA((3 * n,)), pltpu.SemaphoreType.DMA((n,))]
    outs = pl.pallas_call(
        body, name="dw_in", grid=grid,
        in_specs=[pl.BlockSpec((M, tt), lambda j, t: (0, t)), pl.BlockSpec((tt, tn), lambda j, t: (t, j))] + [_ANY] * n,
        out_specs=[pl.BlockSpec((M, tn), lambda j, t: (0, j))] + [_ANY] * n,
        out_shape=[jax.ShapeDtypeStruct((M, N), F32)] + _scattered_shapes(parts),
        scratch_shapes=sems if n else [],
        compiler_params=_params(dimension_semantics=("arbitrary", "arbitrary")),
    )(a, b, *parts)
    return outs[0], outs[1:]


def _add_chips(parts, small_parts):
    arrays = list(parts) + [small_parts]

    def body(*refs):
        ins, outs = refs[:len(arrays)], refs[len(arrays):]
        for a_ref, o_ref in zip(ins, outs):
            part = lambda k: a_ref[k].astype(F32)
            o_ref[...] = ((part(0) + part(1)) + part(2)) + part(3)

    in_specs, out_specs, out_shape = [], [], []
    for a in arrays:
        _, rows, cols = a.shape
        in_specs.append(pl.BlockSpec((N_CHIPS, rows // 2, cols), lambda i: (0, i, 0)))
        out_specs.append(pl.BlockSpec((rows // 2, cols), lambda i: (i, 0)))
        out_shape.append(jax.ShapeDtypeStruct((rows, cols), F32))
    outs = pl.pallas_call(body, name="add_chips", grid=(2,), in_specs=in_specs, out_specs=out_specs,
                          out_shape=out_shape, compiler_params=_params(dimension_semantics=("arbitrary",)))(*arrays)
    return outs[:-1], outs[-1]


def _adamw_small(ws, gs, ms, vs):
    n = len(ws)

    def body(*refs):
        for i in range(n):
            w_ref, g_ref, m_ref, v_ref = (refs[k * n + i] for k in range(4))
            d_ref, nm_ref, nv_ref = (refs[(4 + k) * n + i] for k in range(3))
            _adamw_math(g_ref[...], w_ref, m_ref, v_ref, d_ref, nm_ref, nv_ref)

    shapes = [jax.ShapeDtypeStruct(w.shape, F32) for w in ws]
    outs = pl.pallas_call(body, name="adamw_small", out_shape=shapes * 3)(*ws, *gs, *ms, *vs)
    return outs[:n], outs[n:2 * n], outs[2 * n:]


def _adamw_math(gv, w_ref, m_ref, v_ref, d_ref, nm_ref, nv_ref):
    nm = B1 * m_ref[...] + (1.0 - B1) * gv
    nv = B2 * v_ref[...] + (1.0 - B2) * (gv * gv)
    m_hat = nm / (1.0 - B1 ** STEP)
    v_hat = nv / (1.0 - B2 ** STEP)
    d_ref[...] = -LR * (m_hat / (jnp.sqrt(v_hat) + ADAM_EPS) + WD * w_ref[...])
    nm_ref[...] = nm
    nv_ref[...] = nv


def _adamw_halves(w, mine, other, m, v, c, name):
    hr, cols = mine.shape

    def body(c_ref, w_ref, mine_ref, other_ref, m_ref, v_ref, g_ref, d_ref, nm_ref, nv_ref):
        gv = jnp.where(pl.program_id(0) == c_ref[0], mine_ref[...], other_ref[...])
        g_ref[...] = gv
        _adamw_math(gv, w_ref, m_ref, v_ref, d_ref, nm_ref, nv_ref)

    half = pl.BlockSpec((hr, cols), lambda i, c_ref: (i, 0))
    whole = pl.BlockSpec((hr, cols), lambda i, c_ref: (0, 0))
    shp = jax.ShapeDtypeStruct(w.shape, F32)
    return pl.pallas_call(
        body, name=name, out_shape=[shp] * 4,
        grid_spec=pltpu.PrefetchScalarGridSpec(num_scalar_prefetch=1, grid=(2,), in_specs=[half, whole, whole, half, half],
                                               out_specs=[half] * 4),
        compiler_params=_params(dimension_semantics=("arbitrary",)),
    )(c.reshape(1), w, mine, other, m, v)


_ANY = pl.BlockSpec(memory_space=pl.ANY)


def _mesh_pos():
    return lax.axis_index("x"), lax.axis_index("y"), lax.axis_index("c")


def _other_chips(x, y):
    return [(1 - x, y), (x, 1 - y), (1 - x, 1 - y)]


def _remote(src, dst, send_sems, recv_sems, k, to):
    return pltpu.make_async_remote_copy(src_ref=src, dst_ref=dst, send_sem=send_sems.at[k], recv_sem=recv_sems.at[k],
                                        device_id=to, device_id_type=MESH)


def _gather_weights(shards):
    n = len(shards)

    def body(*refs):
        start, forward, drain = _gather_steps([s.shape for s in shards], refs[:n], refs[n:2 * n], refs[2 * n:3 * n],
                                              *refs[3 * n:])
        start()
        forward()
        drain()

    vmem = pl.BlockSpec(memory_space=pltpu.VMEM)
    return pl.pallas_call(
        body, name="gather_weights", in_specs=[vmem] * n, out_specs=[_ANY] * n,
        out_shape=_gathered_shapes(shards), scratch_shapes=_gather_scratch(shards), compiler_params=_params(),
    )(*shards)


def _travel_shape(shard):
    rows, cols = shard.shape
    return (rows, HEAD_PAD if cols == QK_DIM else cols)


def _gathered_shapes(shards):
    return [jax.ShapeDtypeStruct((N_CHIPS,) + _travel_shape(s), BF16) for s in shards]


def _gather_scratch(shards):
    n = len(shards)
    return ([pltpu.VMEM(_travel_shape(s), BF16) for s in shards]
            + [pltpu.SemaphoreType.DMA((6 * n,)), pltpu.SemaphoreType.DMA((6 * n,)), pltpu.SemaphoreType.DMA((n,))])


def _gather_steps(shapes, ins, outs, stage, send_sems, recv_sems, local_sems):
    n = len(shapes)
    halved = [s[0] % 32 == 0 for s in shapes]

    def part(i, ref, hc):
        if not halved[i]:
            return ref
        hr = shapes[i][0] // 2
        return ref.at[pl.ds(hc * hr, hr), :]

    def to_chip(i, j, x, y, c):
        cx, cy = _other_chips(x, y)[j]
        return _remote(part(i, stage[i], c), part(i, outs[i].at[2 * x + y], c), send_sems, recv_sems, 6 * i + j, (cx, cy, c))

    def to_sibling(i, j, x, y, c):
        cx, cy = _other_chips(x, y)[j]
        got = part(i, outs[i].at[2 * cx + cy], c)
        return _remote(got, got, send_sems, recv_sems, 6 * i + 3 + j, (x, y, 1 - c))

    def local(i, x, y):
        return pltpu.make_async_copy(stage[i], outs[i].at[2 * x + y], local_sems.at[i])

    def start():
        x, y, c = _mesh_pos()
        for i in range(n):
            cols = ins[i].shape[1]
            if stage[i].shape[1] != cols:
                stage[i][...] = jnp.zeros_like(stage[i])
            stage[i][:, 0:cols] = ins[i][...].astype(BF16)
            local(i, x, y).start()
            for j in range(3):
                to_chip(i, j, x, y, c).start()

    def forward():
        x, y, c = _mesh_pos()
        for i in range(n):
            for j, (cx, cy) in enumerate(_other_chips(x, y)):
                got = part(i, outs[i].at[2 * cx + cy], c)
                _remote(got, got, send_sems, recv_sems, 6 * i + j, (cx, cy, c)).wait_recv()
                if halved[i]:
                    to_sibling(i, j, x, y, c).start()

    def drain():
        x, y, c = _mesh_pos()
        for i in range(n):
            for j, (cx, cy) in enumerate(_other_chips(x, y)):
                if halved[i]:
                    got = part(i, outs[i].at[2 * cx + cy], 1 - c)
                    _remote(got, got, send_sems, recv_sems, 6 * i + 3 + j, (x, y, 1 - c)).wait_recv()
                    to_sibling(i, j, x, y, c).wait_send()
                to_chip(i, j, x, y, c).wait_send()
            local(i, x, y).wait()

    return start, forward, drain


def _swap_halves(grads, whole, name):
    n, m = len(grads), len(grads) + len(whole)

    def body(*refs):
        ins, outs, send_sems, recv_sems = refs[:m], refs[m:2 * m], refs[2 * m], refs[2 * m + 1]
        x, y, c = _mesh_pos()
        cps = []
        for i in range(m):
            src = ins[i]
            if i < n:
                hr = grads[i].shape[1] // 2
                src = src.at[:, pl.ds((1 - c) * hr, hr), :]
            cp = _remote(src, outs[i], send_sems, recv_sems, i, (x, y, 1 - c))
            cp.start()
            cps.append(cp)
        for cp in cps:
            cp.wait()

    out_shape = [jax.ShapeDtypeStruct((g.shape[0], g.shape[1] // 2, g.shape[2]), F32) for g in grads]
    out_shape += [jax.ShapeDtypeStruct(w.shape, F32) for w in whole]
    outs = pl.pallas_call(
        body, name=name, in_specs=[_ANY] * m, out_specs=[_ANY] * m, out_shape=out_shape,
        scratch_shapes=[pltpu.SemaphoreType.DMA((m,)), pltpu.SemaphoreType.DMA((m,))],
    )(*grads, *whole)
    return outs[:n], outs[n:]


def _scattered_shapes(parts):
    return [jax.ShapeDtypeStruct(p.shape if p.ndim == 3 else (N_CHIPS,) + p.shape, p.dtype) for p in parts]


def _scatter_steps(ins, outs, send_sems, recv_sems, local_sems):
    n = len(ins)

    def src(i, k):
        return ins[i].at[k] if len(ins[i].shape) == 3 else ins[i]

    def sends(x, y, c):
        return [_remote(src(i, 2 * cx + cy), outs[i].at[2 * x + y], send_sems, recv_sems, 3 * i + j, (cx, cy, c))
                for i in range(n) for j, (cx, cy) in enumerate(_other_chips(x, y))]

    def local(i, x, y):
        return pltpu.make_async_copy(src(i, 2 * x + y), outs[i].at[2 * x + y], local_sems.at[i])

    def start():
        x, y, c = _mesh_pos()
        for i in range(n):
            local(i, x, y).start()
        for cp in sends(x, y, c):
            cp.start()

    def drain():
        x, y, c = _mesh_pos()
        for i in range(n):
            for j, (cx, cy) in enumerate(_other_chips(x, y)):
                got = outs[i].at[2 * cx + cy]
                _remote(got, got, send_sems, recv_sems, 3 * i + j, (cx, cy, c)).wait_recv()
        for cp in sends(x, y, c):
            cp.wait_send()
        for i in range(n):
            local(i, x, y).wait()

    return start, drain


def _add_pair(grads, from_sibling, small, small_sibling, c):
    n = len(grads)

    def body(c_ref, *refs):
        ins, outs = refs[:2 * n + 2], refs[2 * n + 2:]
        for i in range(n + 1):
            outs[i][...] = (ins[2 * i][...] + ins[2 * i + 1][...]).astype(outs[i].dtype)

    in_specs, out_specs, out_shape, args = [], [], [], []
    for g, r in zip(grads, from_sibling):
        _, hr, cols = r.shape
        in_specs += [pl.BlockSpec((1, hr, cols), lambda k, c_ref: (k, c_ref[0], 0)),
                     pl.BlockSpec((1, hr, cols), lambda k, c_ref: (k, 0, 0))]
        out_specs.append(pl.BlockSpec((1, hr, cols), lambda k, c_ref: (k, 0, 0)))
        out_shape.append(jax.ShapeDtypeStruct(r.shape, BF16))
        args += [g, r]
    whole = pl.BlockSpec(small.shape, lambda k, c_ref: (0, 0))
    in_specs += [whole, whole]
    out_specs.append(whole)
    out_shape.append(jax.ShapeDtypeStruct(small.shape, F32))
    outs = pl.pallas_call(
        body, name="add_pair", out_shape=out_shape,
        grid_spec=pltpu.PrefetchScalarGridSpec(num_scalar_prefetch=1, grid=(N_CHIPS,), in_specs=in_specs,
                                               out_specs=out_specs),
        compiler_params=_params(dimension_semantics=("arbitrary",)),
    )(c.reshape(1), *args, small, small_sibling)
    return outs[:n], outs[n]


def _scatter_w_in(dw_in_e, from_sibling):
    hr = from_sibling.shape[1]
    shard = (N_CHIPS, hr, SHARD_COLS_IN)

    def body(g_in, r_in, out, g_buf, r_buf, p_buf, load_sems, send_sems, recv_sems, local_sems):
        c = lax.axis_index("c")
        loads = (pltpu.make_async_copy(g_in.at[0, pl.ds(c * hr, hr), :], g_buf, load_sems.at[0]),
                 pltpu.make_async_copy(r_in.at[0], r_buf, load_sems.at[1]))
        for cp in loads:
            cp.start()
        for cp in loads:
            cp.wait()
        g_buf[...] += r_buf[...]
        p_buf[0, :, 0:KPE_END] = g_buf[:, 0:KPE_END].astype(BF16)
        p_buf[0, :, KPE_END:SHARD_COLS_IN] = g_buf[:, KPE_END + KPE_PAD:SHARD_COLS_IN + KPE_PAD].astype(BF16)
        for k in range(1, N_CHIPS):
            p_buf[k] = g_buf[:, SHARD_COLS_IN * k + KPE_PAD:SHARD_COLS_IN * (k + 1) + KPE_PAD].astype(BF16)
        start, drain = _scatter_steps([p_buf], [out], send_sems, recv_sems, local_sems)
        start()
        drain()

    return pl.pallas_call(
        body, name="scatter_grads", in_specs=[_ANY] * 2, out_specs=_ANY, out_shape=jax.ShapeDtypeStruct(shard, BF16),
        scratch_shapes=[pltpu.VMEM((hr, PROJ_EXT), F32)] * 2 + [pltpu.VMEM(shard, BF16)]
                       + [pltpu.SemaphoreType.DMA((2,)), pltpu.SemaphoreType.DMA((3,)), pltpu.SemaphoreType.DMA((3,)),
                          pltpu.SemaphoreType.DMA((1,))],
        compiler_params=_params(),
    )(dw_in_e, from_sibling)


def _share_halves(halves):
    n = len(halves)

    def body(*refs):
        ins, outs, send_sems, recv_sems = refs[:n], refs[n:2 * n], refs[2 * n], refs[2 * n + 1]
        x, y, c = _mesh_pos()
        cps = [_remote(ins[i], outs[i], send_sems, recv_sems, i, (x, y, 1 - c)) for i in range(n)]
        for cp in cps:
            cp.start()
        for cp in cps:
            cp.wait()

    return pl.pallas_call(
        body, name="share_halves", in_specs=[_ANY] * n, out_specs=[_ANY] * n,
        out_shape=[jax.ShapeDtypeStruct(h.shape, h.dtype) for h in halves],
        scratch_shapes=[pltpu.SemaphoreType.DMA((n,)), pltpu.SemaphoreType.DMA((n,))],
    )(*halves)


SHARD_COLS_IN = IN_TOTAL // N_CHIPS
KPE_END = Q_LORA + KV_LORA + ROPE
KPE_PAD = PROJ_EXT - IN_TOTAL


def _by_cols(a):
    return a.transpose(1, 0, 2).reshape(a.shape[1], N_CHIPS * a.shape[2])


def _assemble_early(c_in, c_uq, c_ukv, c_conv):
    return c_in, c_uq, c_ukv, _by_cols(c_conv).astype(F32)


def _assemble_late(c_o, c_pl, c_plg):
    return c_o.reshape(D_MODEL, D_MODEL), _by_cols(c_pl), c_plg.reshape(D_MODEL, D_MODEL)


def _split_others(dw_uq, dw_ukv, dw_o, dw_pl, dw_plg):
    chip_major = lambda a: a.reshape(a.shape[0], N_CHIPS, a.shape[1] // N_CHIPS).transpose(1, 0, 2)
    return [dw_uq[:, :, :QK_DIM], dw_ukv, dw_o.reshape(N_CHIPS, D_MODEL // N_CHIPS, D_MODEL),
            chip_major(dw_pl), dw_plg.reshape(N_CHIPS, D_MODEL // N_CHIPS, D_MODEL)]


def _local_step(x, p, pos, tgt, gains, early, late_shards, late_gathered, tm, tq):
    c_in, w_uq_e, w_ukv, conv_w = early
    g_in, g_cq, g_ckv, g_q, g_k, g_oa, g_oc, g_pl = gains
    T = x.shape[0]
    zpad = lambda a, n: jnp.concatenate([a, jnp.zeros(a.shape[:-1] + (n,), a.dtype)], axis=-1)
    gq, gk = zpad(g_q, HEAD_PAD - QK_DIM), zpad(g_k, HEAD_PAD - QK_DIM)
    inv_freq = 1.0 / (ROPE_THETA ** (jnp.arange(0, ROPE, 2, dtype=F32) / ROPE))
    invf = jnp.concatenate([inv_freq, inv_freq, jnp.zeros((64,), F32)]).reshape(1, LANES)
    sgn = jnp.concatenate([-jnp.ones((32,), F32), jnp.ones((32,), F32), jnp.zeros((64,), F32)]).reshape(1, LANES)

    (proj, q, k, v, w_in_e), gathered = _fwd_proj(x, pos, g_in, c_in, g_cq, w_uq_e, g_ckv, w_ukv, gq, gk, invf, sgn,
                                                  late_shards, min(2 * tm, T))
    w_o, w_pl, w_plg = _assemble_late(*(gathered if late_shards else late_gathered))
    o, lse = _attn_fwd(q, k, v, tq)
    (dx1, do, delta, dtail, du, dw_o, dw_pl, dw_plg, dg_oa, dg_oc, dg_pl, dconv, loss) = _tail(
        x, o, proj, p, tgt, g_oa, g_oc, g_pl, conv_w, w_o, w_pl, w_plg, tm)
    dq, dk, dv = _attn_bwd(q, k, v, do, lse, delta, tq)
    (gx, h, dproj, dw_uq_e, dw_ukv, dg_in, dg_cq, dg_ckv, dgq, dgk) = _bwd_proj(
        x, dx1, pos, proj, dq, dk, dv, dtail, du, g_in, w_in_e, g_cq, w_uq_e, g_ckv, w_ukv, gq, gk, conv_w, invf, sgn, tm)
    wgrads = (dw_uq_e, dw_ukv, dw_o, dw_pl, dw_plg)
    ggrads = (dg_in, dg_cq, dg_ckv, dgq, dgk, dg_oa, dg_oc, dg_pl)
    return loss, gx, (h, dproj), wgrads, ggrads, dconv


def kernel(x, p, positions, g_in, w_in, g_cq, w_uq, g_ckv, w_ukv, g_q, g_k, conv_w, g_oa, g_oc, w_o, w_pl, w_plg, g_pl, loss_target, m_g_in, m_w_in, m_g_cq, m_w_uq, m_g_ckv, m_w_ukv, m_g_q, m_g_k, m_conv_w, m_g_oa, m_g_oc, m_w_o, m_w_pl, m_w_plg, m_g_pl, v_g_in, v_w_in, v_g_cq, v_w_uq, v_g_ckv, v_w_ukv, v_g_q, v_g_k, v_conv_w, v_g_oa, v_g_oc, v_w_o, v_w_pl, v_w_plg, v_g_pl):
    T = x.shape[1]
    c = lax.axis_index("c")
    chip = 2 * lax.axis_index("x") + lax.axis_index("y")
    gains = [g.reshape(1, -1) for g in (g_in, g_cq, g_ckv, g_q, g_k, g_oa, g_oc, g_pl)]

    early = _assemble_early(*_gather_weights([w_in[0], w_uq[0], w_ukv[0], conv_w[0]]))

    loss, gx, (h_t, dproj), wgrads, ggrads, dconv = _local_step(
        x[0], p[0, 0], positions.reshape(T, 1), loss_target[0], gains, early, [w_o[0], w_pl[0], w_plg[0]], None, 256, 512)

    others_cm = _split_others(*wgrads)
    small_parts = [a.reshape(-1, LANES) for a in (*ggrads, loss, dconv)]
    small_rows = [a.shape[0] for a in small_parts]
    tile_rows = [-(-r // 8) * 8 for r in small_rows]
    tile_rows[-1] += -sum(tile_rows) % 16
    small = jnp.concatenate([jnp.pad(a, ((0, t - r), (0, 0))) for a, r, t in zip(small_parts, small_rows, tile_rows)])
    from_sibling, (small_sibling,) = _swap_halves(others_cm, [small], "pair_grads")
    chip_parts, chip_small = _add_pair(others_cm, from_sibling, small, small_sibling, c)
    dw_in_e, exchanged = _matmul_acc(h_t, dproj, min(4096, T), 512, [*chip_parts, chip_small])
    dw_in_e = dw_in_e[None]
    (w_in_sibling,), _ = _swap_halves([dw_in_e], [], "pair_w_in")
    by_chip = [_scatter_w_in(dw_in_e, w_in_sibling), *exchanged[:-1]]
    halves, small_total = _add_chips(by_chip, exchanged[-1])
    other_halves = _share_halves(halves)

    gg, off = [], 0
    for rows, tiled in zip(small_rows, tile_rows):
        gg.append(small_total[off:off + rows].reshape(1, -1))
        off += tiled
    loss_out = gg[8][0, 0]
    conv_total = gg[9].reshape(3, CONV_W)
    conv_g = lax.dynamic_slice(conv_total, (0, chip * (CONV_W // N_CHIPS)), (3, CONV_W // N_CHIPS))
    g_by_name = dict(g_in=gg[0], g_cq=gg[1], g_ckv=gg[2], g_q=gg[3][:, :QK_DIM], g_k=gg[4][:, :QK_DIM], conv_w=conv_g,
                     g_oa=gg[5], g_oc=gg[6], g_pl=gg[7])
    half_by_name = dict(zip(("w_in", "w_uq", "w_ukv", "w_o", "w_pl", "w_plg"), zip(halves, other_halves)))
    weights = dict(g_in=g_in, w_in=w_in, g_cq=g_cq, w_uq=w_uq, g_ckv=g_ckv, w_ukv=w_ukv, g_q=g_q, g_k=g_k,
                   conv_w=conv_w, g_oa=g_oa, g_oc=g_oc, w_o=w_o, w_pl=w_pl, w_plg=w_plg, g_pl=g_pl)
    ms = dict(g_in=m_g_in, w_in=m_w_in, g_cq=m_g_cq, w_uq=m_w_uq, g_ckv=m_g_ckv, w_ukv=m_w_ukv, g_q=m_g_q, g_k=m_g_k,
              conv_w=m_conv_w, g_oa=m_g_oa, g_oc=m_g_oc, w_o=m_w_o, w_pl=m_w_pl, w_plg=m_w_plg, g_pl=m_g_pl)
    vs = dict(g_in=v_g_in, w_in=v_w_in, g_cq=v_g_cq, w_uq=v_w_uq, g_ckv=v_g_ckv, w_ukv=v_w_ukv, g_q=v_g_q, g_k=v_g_k,
              conv_w=v_conv_w, g_oa=v_g_oa, g_oc=v_g_oc, w_o=v_w_o, w_pl=v_w_pl, w_plg=v_w_plg, g_pl=v_g_pl)
    names = list(weights)
    flat = lambda a: a.reshape(-1, a.shape[-1])
    small_names = list(g_by_name)
    small_out = _adamw_small([flat(weights[n]) for n in small_names], [flat(g_by_name[n]) for n in small_names],
                             [flat(ms[n]) for n in small_names], [flat(vs[n]) for n in small_names])
    results = {n: (flat(g_by_name[n]), *(out[i] for out in small_out)) for i, n in enumerate(small_names)}
    for n in half_by_name:
        results[n] = _adamw_halves(flat(weights[n]), *half_by_name[n], flat(ms[n]), flat(vs[n]), c, "adamw_" + n)
    per_kind = [[results[n][kind].reshape(weights[n].shape) for n in names] for kind in range(4)]
    return (loss_out, gx.reshape(x.shape), *per_kind[0], *per_kind[1], *per_kind[2], *per_kind[3])
```

```python
import math

import jax
import jax.numpy as jnp
from jax import lax
from jax.experimental import pallas as pl
from jax.experimental.pallas import tpu as pltpu

F32 = jnp.float32
BF16 = jnp.bfloat16

D_MODEL = 1024
N_HEADS = 4
NOPE = 128
ROPE = 64
V_DIM = 128
QK_DIM = NOPE + ROPE
HEAD_PAD = 256
Q_LORA = 256
KV_LORA = 128
ATTN_W = 512
CONV_W = 512
PLE = 256
IN_TOTAL = 3008
PROJ_EXT = 3072
ROPE_THETA = 10000.0
EPS = 1e-6
SCALE = 1.0 / math.sqrt(QK_DIM)
LOG2E = math.log2(math.e)
EXP2_SCALE = SCALE * LOG2E
NEG = -1e30
SOFTMAX_ROWS = 32
SUB_TILE = 256

LR, B1, B2, ADAM_EPS, WD, STEP = 0.001, 0.9, 0.999, 1e-08, 0.01, 10

N_CHIPS = 4
LANES = 128
VMEM_LIMIT = 56 * 1024 * 1024
MESH = pl.DeviceIdType.MESH


def _params(**kw):
    return pltpu.CompilerParams(vmem_limit_bytes=VMEM_LIMIT, **kw)


def _inv_rms(x, n):
    return lax.rsqrt(jnp.sum(x * x, axis=-1, keepdims=True) / n + EPS)


def _lane_sum(a):
    folded = a[:, 0:LANES]
    for c0 in range(LANES, a.shape[1], LANES):
        folded = folded + a[:, c0:c0 + LANES]
    head = folded.astype(BF16)
    tail = (folded - head.astype(F32)).astype(BF16)
    return _dot(jnp.concatenate([head, tail], axis=1), jnp.ones((2 * LANES, LANES), BF16))


def _inv_rms_mxu(x):
    return lax.rsqrt(_lane_sum(x * x) / x.shape[1] + EPS)


def _rep(r, width):
    return r if width == LANES else jnp.tile(r, (1, width // LANES))


def _sigmoid(z):
    return jax.nn.sigmoid(z)


def _swap_rope_halves(b):
    lane = lax.broadcasted_iota(jnp.int32, b.shape, 1)
    swapped = jnp.where(lane < 32, pltpu.roll(b, 96, 1), pltpu.roll(b, 32, 1))
    return jnp.where(lane < ROPE, swapped, 0.0)


def _dot(a, b):
    return jnp.dot(a, b, preferred_element_type=F32)


def _dot_nt(a, b):
    return lax.dot_general(a, b, (((1,), (1,)), ((), ())), preferred_element_type=F32)


def _dot_tn(a, b):
    return lax.dot_general(a, b, (((0,), (0,)), ((), ())), preferred_element_type=F32)


def _colsum(a):
    return jnp.sum(a, axis=0, keepdims=True)


def _store_transposed(src_ref, dst_ref):
    r, c = src_ref.shape
    for r0 in range(0, r, LANES):
        h = min(LANES, r - r0)
        for c0 in range(0, c, LANES):
            w = min(LANES, c - c0)
            piece = src_ref[r0:r0 + h, c0 + w - LANES:c0 + w]
            if h < LANES:
                piece = jnp.concatenate([piece, jnp.zeros((LANES - h, LANES), piece.dtype)], axis=0)
            dst_ref[c0:c0 + w, r0:r0 + h] = piece.T[LANES - w:, 0:h].astype(dst_ref.dtype)


def _full(shape):
    return pl.BlockSpec(shape, lambda *_: (0,) * len(shape))


def _round_robin(chains, width):
    waiting, active = list(chains), []
    while waiting or active:
        while waiting and len(active) < width:
            active.append(waiting.pop(0))
        for chain in list(active):
            if next(chain, _DONE) is _DONE:
                active.remove(chain)


_DONE = object()


def _rope_tables(pos_ref, invf_ref, sgn_ref):
    ang = pos_ref[...].astype(F32) * invf_ref[...]
    return jnp.cos(ang), jnp.sin(ang) * sgn_ref[...]


def _fwd_proj(x, pos, g_in, c_in, g_cq, w_uq, g_ckv, w_ukv, gq, gk, invf, sgn, late_shards, tm):
    T = x.shape[0]
    nt = T // tm
    n_late = len(late_shards)
    ts = min(SUB_TILE, tm)

    def body(x_ref, pos_ref, g_in_ref, c_in_ref, g_cq_ref, w_uq_ref, g_ckv_ref, w_ukv_ref, gq_ref, gk_ref,
             invf_ref, sgn_ref, *rest):
        late_in, (proj_ref, q_ref, k_ref, v_ref, w_in_ref) = rest[:n_late], rest[n_late:n_late + 5]
        late_out, late_scratch = rest[n_late + 5:2 * n_late + 5], rest[2 * n_late + 5:]
        i = pl.program_id(0)

        @pl.when(i == 0)
        def _():
            w_in_ref[:, 0:KPE_END] = c_in_ref[0, :, 0:KPE_END]
            w_in_ref[:, KPE_END:KPE_END + KPE_PAD] = jnp.zeros((D_MODEL, KPE_PAD), BF16)
            w_in_ref[:, KPE_END + KPE_PAD:SHARD_COLS_IN + KPE_PAD] = c_in_ref[0, :, KPE_END:SHARD_COLS_IN]
            for chip in range(1, N_CHIPS):
                w_in_ref[:, SHARD_COLS_IN * chip + KPE_PAD:SHARD_COLS_IN * (chip + 1) + KPE_PAD] = c_in_ref[chip]

        if n_late:
            start, forward, drain = _gather_steps([s.shape for s in late_shards], late_in, late_out,
                                                  late_scratch[:n_late], *late_scratch[n_late:])
            pl.when(i == 0)(start)
            pl.when(i == nt // 2)(forward)

        for r0 in range(0, tm, ts):
            rows = slice(r0, r0 + ts)
            xv = x_ref[rows, :]
            h = (xv * _rep(_inv_rms_mxu(xv), D_MODEL) * g_in_ref[...]).astype(BF16)
            lat = _dot(h, w_in_ref[:, 0:512])
            proj_ref[rows, 0:512] = lat
            c_q = lat[:, 0:Q_LORA]
            cqn = (c_q * _rep(_inv_rms_mxu(c_q), Q_LORA) * g_cq_ref[...]).astype(BF16)
            c_kv = lat[:, Q_LORA:Q_LORA + KV_LORA]
            ckvn = (c_kv * _inv_rms_mxu(c_kv) * g_ckv_ref[...]).astype(BF16)
            kpe = lat[:, 384:512]
            kpe_sq = kpe * kpe
            cos_b, sin_b = _rope_tables(pos_ref.at[rows, :], invf_ref, sgn_ref)
            gq_a, gq_b = gq_ref[:, 0:NOPE], gq_ref[:, NOPE:HEAD_PAD]
            gk_a, gk_b = gk_ref[:, 0:NOPE], gk_ref[:, NOPE:HEAD_PAD]

            def projections(rows=rows, h=h):
                for c0 in range(512, PROJ_EXT, 512):
                    proj_ref[rows, c0:c0 + 512] = _dot(h, w_in_ref[:, c0:c0 + 512])
                    yield

            def queries(hd, rows=rows, cqn=cqn, cos_b=cos_b, sin_b=sin_b, gq_a=gq_a, gq_b=gq_b):
                qh = _dot(cqn, w_uq_ref[hd])
                yield
                a, b = qh[:, 0:NOPE], qh[:, NOPE:HEAD_PAD]
                r = lax.rsqrt(_lane_sum(a * a + b * b) / QK_DIM + EPS)
                yield
                bn = b * r * gq_b
                q_ref[hd, rows, 0:NOPE] = (a * r * gq_a).astype(BF16)
                q_ref[hd, rows, NOPE:HEAD_PAD] = (bn * cos_b + _swap_rope_halves(bn) * sin_b).astype(BF16)
                yield

            def keys(hd, rows=rows, ckvn=ckvn, kpe=kpe, kpe_sq=kpe_sq, cos_b=cos_b, sin_b=sin_b, gk_a=gk_a, gk_b=gk_b):
                kvh = _dot(ckvn, w_ukv_ref[hd])
                yield
                ka = kvh[:, 0:NOPE]
                rk = lax.rsqrt(_lane_sum(ka * ka + kpe_sq) / QK_DIM + EPS)
                yield
                kbn = kpe * rk * gk_b
                k_ref[hd, rows, 0:NOPE] = (ka * rk * gk_a).astype(BF16)
                k_ref[hd, rows, NOPE:HEAD_PAD] = (kbn * cos_b + _swap_rope_halves(kbn) * sin_b).astype(BF16)
                v_ref[hd, rows, 0:V_DIM] = kvh[:, NOPE:HEAD_PAD].astype(BF16)
                v_ref[hd, rows, V_DIM:2 * V_DIM] = jnp.ones((ts, V_DIM), BF16)
                yield

            chains = [projections()]
            for hd in range(N_HEADS):
                chains += [queries(hd), keys(hd)]
            _round_robin(chains, 4)

        if n_late:
            pl.when(i == nt - 1)(drain)

    row = lambda i: (i, 0)
    head_rows = lambda i: (0, i, 0)
    outs = pl.pallas_call(
        body, name="fwd_proj", grid=(nt,),
        in_specs=[pl.BlockSpec((tm, D_MODEL), row), pl.BlockSpec((tm, 1), row), _full((1, D_MODEL)),
                  _full((N_CHIPS, D_MODEL, SHARD_COLS_IN)), _full((1, Q_LORA)), _full((N_HEADS, Q_LORA, HEAD_PAD)),
                  _full((1, KV_LORA)), _full((N_HEADS, KV_LORA, HEAD_PAD)), _full((1, HEAD_PAD)), _full((1, HEAD_PAD)),
                  _full((1, LANES)), _full((1, LANES))] + [_full(s.shape) for s in late_shards],
        out_specs=[pl.BlockSpec((tm, PROJ_EXT), row), pl.BlockSpec((N_HEADS, tm, HEAD_PAD), head_rows),
                   pl.BlockSpec((N_HEADS, tm, HEAD_PAD), head_rows), pl.BlockSpec((N_HEADS, tm, 2 * V_DIM), head_rows),
                   _full((D_MODEL, PROJ_EXT))] + [_ANY] * n_late,
        out_shape=[jax.ShapeDtypeStruct((T, PROJ_EXT), F32), jax.ShapeDtypeStruct((N_HEADS, T, HEAD_PAD), BF16),
                   jax.ShapeDtypeStruct((N_HEADS, T, HEAD_PAD), BF16), jax.ShapeDtypeStruct((N_HEADS, T, 2 * V_DIM), BF16),
                   jax.ShapeDtypeStruct((D_MODEL, PROJ_EXT), BF16)] + _gathered_shapes([s.shape for s in late_shards]),
        scratch_shapes=_gather_scratch([s.shape for s in late_shards]) if n_late else [],
        compiler_params=_params(dimension_semantics=("arbitrary",)),
    )(x, pos, g_in, c_in, g_cq, w_uq, g_ckv, w_ukv, gq, gk, invf, sgn, *late_shards)
    return outs[:5], outs[5:]


def _chunk_pipeline(n_loop, lag, matmuls, pointwise, accumulate, last):
    slots = lag + 1

    def iteration(t, slot, pending=True):
        matmuls(jnp.minimum(t + lag, n_loop), (slot + lag) % slots)
        if pending:
            accumulate(t - lag, (slot + 1) % slots)
        pointwise(t, slot, False)

    def finish(slot, pending):
        for back in range(pending, 0, -1):
            accumulate(n_loop - back, (slot - back) % slots)
        pointwise(n_loop, slot, True)
        accumulate(n_loop, slot)
        last()

    for u in range(lag):
        matmuls(jnp.minimum(u, n_loop), u)
    for u in range(lag):
        pl.when(u < n_loop)(lambda u=u: iteration(u, u, pending=False))

    n_main = jnp.maximum(n_loop - lag, 0)

    def unrolled(tt, carry):
        for j in range(slots):
            iteration(lag + slots * tt + j, (lag + j) % slots)
        return carry

    lax.fori_loop(0, n_main // slots, unrolled, 0)
    rest = lax.rem(n_main, slots)
    t0 = n_loop - rest

    for r in range(slots):
        @pl.when(jnp.logical_and(n_loop >= lag, rest == r))
        def _():
            for j in range(r):
                iteration(t0 + j, (lag + j) % slots)
            finish((lag + r) % slots, lag)

    for short in range(lag):
        pl.when(n_loop == short)(lambda short=short: finish(short, short))


def _attn_fwd(q, k, v, tq):
    T = q.shape[1]
    tk = tq
    rc = min(SOFTMAX_ROWS, tq)

    def body(q_ref, k_ref, v_ref, o_ref, lse_ref, s0, s1, s2, p0, p1, p2, a0, a1, a2, m_ref, acc_ref):
        qi = pl.program_id(1)
        s_buf, p_buf, a_buf = (s0, s1, s2), (p0, p1, p2), (a0, a1, a2)

        def scores(t, slot):
            ks = pl.multiple_of(t * tk, tk)
            s_buf[slot][...] = _dot_nt(q_ref[0], k_ref[0, pl.ds(ks, tk), :])

        def values(t, slot):
            ks = pl.multiple_of(t * tk, tk)
            acc_ref[...] = acc_ref[...] * a_buf[slot][...] + _dot(p_buf[slot][...], v_ref[0, pl.ds(ks, tk), :])

        def softmax(t, slot, masked):
            s_all = s_buf[slot][...]
            if masked:
                row = lax.broadcasted_iota(jnp.int32, (tq, tk), 0)
                col = lax.broadcasted_iota(jnp.int32, (tq, tk), 1)
                s_all = jnp.where(col <= row, s_all, NEG)
                s_buf[slot][...] = s_all
            m_old = m_ref[...]
            m_new = jnp.maximum(m_old, jnp.max(s_all, axis=1, keepdims=True))
            a_buf[slot][...] = jnp.exp2((m_old - m_new) * EXP2_SCALE)
            m_ref[...] = m_new
            for r0 in range(0, tq, rc):
                s = s_buf[slot][r0:r0 + rc, :]
                p_buf[slot][r0:r0 + rc, :] = jnp.exp2((s - m_new[r0:r0 + rc, :]) * EXP2_SCALE).astype(BF16)

        def last():
            l = acc_ref[:, V_DIM:2 * V_DIM]
            o_ref[...] = acc_ref[:, 0:V_DIM] / l
            lse_ref[0] = (m_ref[...] * SCALE + jnp.log(l)).T[0:1, :]

        m_ref[...] = jnp.full_like(m_ref, NEG)
        acc_ref[...] = jnp.zeros_like(acc_ref)
        _chunk_pipeline(qi, 2, scores, softmax, values, last)

    return pl.pallas_call(
        body, name="attn_fwd", grid=(N_HEADS, T // tq),
        in_specs=[pl.BlockSpec((1, tq, HEAD_PAD), lambda h, i: (h, i, 0)),
                  pl.BlockSpec((1, T, HEAD_PAD), lambda h, i: (h, 0, 0)),
                  pl.BlockSpec((1, T, 2 * V_DIM), lambda h, i: (h, 0, 0))],
        out_specs=[pl.BlockSpec((tq, V_DIM), lambda h, i: (i, h)),
                   pl.BlockSpec((1, 1, tq), lambda h, i: (h, 0, i))],
        out_shape=[jax.ShapeDtypeStruct((T, ATTN_W), F32), jax.ShapeDtypeStruct((N_HEADS, 1, T), F32)],
        scratch_shapes=[pltpu.VMEM((tq, tk), F32)] * 3 + [pltpu.VMEM((tq, tk), BF16)] * 3
                       + [pltpu.VMEM((tq, 1), F32)] * 4 + [pltpu.VMEM((tq, 2 * V_DIM), F32)],
        compiler_params=_params(dimension_semantics=("arbitrary", "arbitrary")),
    )(q, k, v)


def _tail(x, o, proj, p, tgt, g_oa, g_oc, g_pl, conv_w, w_o, w_pl, w_plg, tm):
    T = x.shape[0]
    nt = T // tm

    def body(x_ref, o_ref, za_ref, cb_ref, cc_ref, cx_ref, zc_ref, cch_ref, cxh_ref, p_ref, tgt_ref,
             g_oa_ref, g_oc_ref, g_pl_ref, cw_ref, w_o_ref, w_pl_ref, w_plg_ref,
             dx1_ref, do_ref, delta_ref, dtail_ref, du_ref,
             dw_o_ref, dw_pl_ref, dw_plg_ref, dg_oa_ref, dg_oc_ref, dg_pl_ref, dcw_ref, loss_ref):
        i = pl.program_id(0)

        @pl.when(i == 0)
        def _():
            for r in (dw_o_ref, dw_pl_ref, dw_plg_ref, dg_oa_ref, dg_oc_ref, dg_pl_ref, dcw_ref, loss_ref):
                r[...] = jnp.zeros_like(r)

        g_oa, g_oc, g_pl = g_oa_ref[...], g_oc_ref[...], g_pl_ref[...]
        w0, w1, w2 = cw_ref[0:1, :], cw_ref[1:2, :], cw_ref[2:3, :]

        xv, ov, za, cb, zc = x_ref[...], o_ref[...], za_ref[...], cb_ref[...], zc_ref[...]
        pb = p_ref[...].astype(BF16)
        pp = _dot(pb, w_pl_ref[...])

        sa = _sigmoid(za)
        silu_a = za * sa
        ga = ov * silu_a
        ra = _inv_rms(ga, ATTN_W)
        xa = ga * ra
        ya = (xa * g_oa).astype(BF16)
        x1_a = _dot(ya, w_o_ref[0:ATTN_W, :])
        v = cc_ref[...] * cx_ref[...]
        not_first = jnp.where(i > 0, 1.0, 0.0)
        hv6 = cch_ref[6:7, :] * cxh_ref[6:7, :] * not_first
        hv7 = cch_ref[7:8, :] * cxh_ref[7:8, :] * not_first
        row = lax.broadcasted_iota(jnp.int32, v.shape, 0)
        v1 = jnp.where(row == 0, hv7, pltpu.roll(v, 1, 0))
        v2 = jnp.where(row == 0, hv6, jnp.where(row == 1, hv7, pltpu.roll(v, 2, 0)))
        u = w0 * v2 + w1 * v1 + w2 * v
        sc = _sigmoid(zc)
        silu_c = zc * sc
        gc = cb * u * silu_c
        rc = _inv_rms(gc, CONV_W)
        xc = gc * rc
        yc = (xc * g_oc).astype(BF16)
        x1 = xv + (x1_a + _dot(yc, w_o_ref[ATTN_W:D_MODEL, :]))
        r1 = _inv_rms(x1, D_MODEL)
        xh1 = x1 * r1
        n1 = (xh1 * g_pl).astype(BF16)
        gate = _sigmoid(_dot(n1, w_plg_ref[...]))
        err = x1 + gate * pp - tgt_ref[...]
        loss_ref[...] += 0.5 * jnp.sum(err * err) / D_MODEL
        dy = err / D_MODEL

        dpp = (dy * gate).astype(BF16)
        da = (dy * pp * gate * (1.0 - gate)).astype(BF16)
        dn1 = _dot_nt(da, w_plg_ref[...])
        dw_pl_ref[...] += _dot_tn(pb, dpp)
        dw_plg_ref[...] += _dot_tn(n1, da)
        dg_pl_ref[...] += _colsum(dn1 * xh1)
        dxh = dn1 * g_pl
        dx1 = dy + r1 * (dxh - xh1 * (jnp.sum(dxh * xh1, axis=-1, keepdims=True) / D_MODEL))
        dx1_ref[...] = dx1
        dx1b = dx1.astype(BF16)
        dya = _dot_nt(dx1b, w_o_ref[0:ATTN_W, :])
        dyc = _dot_nt(dx1b, w_o_ref[ATTN_W:D_MODEL, :])

        dw_o_ref[0:ATTN_W, :] += _dot_tn(ya, dx1b)
        dg_oa_ref[...] += _colsum(dya * xa)
        dxa = dya * g_oa
        dga = ra * (dxa - xa * (jnp.sum(dxa * xa, axis=-1, keepdims=True) / ATTN_W))
        do = (dga * silu_a).astype(BF16)
        do_ref[...] = do
        dof = do.astype(F32) * ov
        for hd in range(N_HEADS):
            delta_ref[hd] = _lane_sum(dof[:, hd * V_DIM:(hd + 1) * V_DIM]).T[0:1, :]
        dtail_ref[:, 0:512] = (dga * ov * (sa * (1.0 + za * (1.0 - sa)))).astype(BF16)

        dw_o_ref[ATTN_W:D_MODEL, :] += _dot_tn(yc, dx1b)
        dg_oc_ref[...] += _colsum(dyc * xc)
        dxc = dyc * g_oc
        dgc = rc * (dxc - xc * (jnp.sum(dxc * xc, axis=-1, keepdims=True) / CONV_W))
        dtail_ref[:, 512:1024] = (dgc * u * silu_c).astype(BF16)
        du = dgc * cb * silu_c
        du_ref[...] = du
        dtail_ref[:, 1024:1536] = (dgc * cb * u * (sc * (1.0 + zc * (1.0 - sc)))).astype(BF16)
        dcw_ref[0:1, :] += _colsum(du * v2)
        dcw_ref[1:2, :] += _colsum(du * v1)
        dcw_ref[2:3, :] += _colsum(du * v)

    row = lambda i: (i, 0)
    col = lambda c: (lambda i: (i, c))
    halo = lambda c: (lambda i: (jnp.maximum(i * (tm // 8) - 1, 0), c))
    in_specs = [pl.BlockSpec((tm, D_MODEL), row), pl.BlockSpec((tm, ATTN_W), row)]
    in_specs += [pl.BlockSpec((tm, 512), col(c)) for c in (1, 2, 3, 4, 5)]
    in_specs += [pl.BlockSpec((8, 512), halo(3)), pl.BlockSpec((8, 512), halo(4))]
    in_specs += [pl.BlockSpec((tm, PLE), row), pl.BlockSpec((tm, D_MODEL), row),
                 _full((1, ATTN_W)), _full((1, CONV_W)), _full((1, D_MODEL)), _full((3, CONV_W)),
                 _full((D_MODEL, D_MODEL)), _full((PLE, D_MODEL)), _full((D_MODEL, D_MODEL))]
    out_specs = [pl.BlockSpec((tm, D_MODEL), row), pl.BlockSpec((tm, ATTN_W), row),
                 pl.BlockSpec((N_HEADS, 1, tm), lambda i: (0, 0, i)), pl.BlockSpec((tm, 1536), row),
                 pl.BlockSpec((tm, CONV_W), row),
                 _full((D_MODEL, D_MODEL)), _full((PLE, D_MODEL)), _full((D_MODEL, D_MODEL)),
                 _full((1, ATTN_W)), _full((1, CONV_W)), _full((1, D_MODEL)), _full((3, CONV_W)), _full((1, LANES))]
    out_shape = [jax.ShapeDtypeStruct((T, D_MODEL), F32), jax.ShapeDtypeStruct((T, ATTN_W), BF16),
                 jax.ShapeDtypeStruct((N_HEADS, 1, T), F32), jax.ShapeDtypeStruct((T, 1536), BF16),
                 jax.ShapeDtypeStruct((T, CONV_W), F32),
                 jax.ShapeDtypeStruct((D_MODEL, D_MODEL), F32), jax.ShapeDtypeStruct((PLE, D_MODEL), F32),
                 jax.ShapeDtypeStruct((D_MODEL, D_MODEL), F32),
                 jax.ShapeDtypeStruct((1, ATTN_W), F32), jax.ShapeDtypeStruct((1, CONV_W), F32),
                 jax.ShapeDtypeStruct((1, D_MODEL), F32), jax.ShapeDtypeStruct((3, CONV_W), F32),
                 jax.ShapeDtypeStruct((1, LANES), F32)]
    return pl.pallas_call(
        body, name="tail", grid=(nt,), in_specs=in_specs, out_specs=out_specs, out_shape=out_shape,
        compiler_params=_params(dimension_semantics=("arbitrary",)),
    )(x, o, proj, proj, proj, proj, proj, proj, proj, p, tgt, g_oa, g_oc, g_pl, conv_w, w_o, w_pl, w_plg)


def _attn_bwd(q, k, v, do, lse_row, delta_row, tk):
    T = q.shape[1]
    tq = tk
    nq = T // tq
    rc = min(SOFTMAX_ROWS, tk)

    def body(q_ref, k_ref, v_ref, do_ref, lse_ref, dl_ref, dq_ref, dk_ref, dv_ref,
             s0, s1, d0, d1, p0, p1, g0, g1, dk_acc, dv_acc):
        kj = pl.program_id(1)
        s_buf, dp_buf, p_buf, g_buf = (s0, s1), (d0, d1), (p0, p1), (g0, g1)

        @pl.when(kj == 0)
        def _():
            dq_ref[...] = jnp.zeros_like(dq_ref)

        def q_start(t):
            return pl.multiple_of((nq - 1 - t) * tq, tq)

        def matmuls(t, slot):
            qs = q_start(t)
            s_buf[slot][...] = _dot_nt(k_ref[0], q_ref[0, pl.ds(qs, tq), :])
            dp_buf[slot][...] = _dot_nt(v_ref[0], do_ref[pl.ds(qs, tq), :])

        def pointwise(t, slot, masked):
            qs = q_start(t)
            lse2 = lse_ref[0, :, pl.ds(qs, tq)] * LOG2E
            dl = dl_ref[0, :, pl.ds(qs, tq)]
            for r0 in range(0, tk, rc):
                st = s_buf[slot][r0:r0 + rc, :]
                if masked:
                    row = lax.broadcasted_iota(jnp.int32, (rc, tq), 0)
                    col = lax.broadcasted_iota(jnp.int32, (rc, tq), 1)
                    st = jnp.where(row + r0 <= col, st, NEG)
                pt = jnp.exp2(st * EXP2_SCALE - lse2)
                p_buf[slot][r0:r0 + rc, :] = pt.astype(BF16)
                g_buf[slot][r0:r0 + rc, :] = (pt * (dp_buf[slot][r0:r0 + rc, :] - dl) * SCALE).astype(BF16)

        def accumulate(t, slot):
            qs = q_start(t)
            dv_acc[...] += _dot(p_buf[slot][...], do_ref[pl.ds(qs, tq), :])
            dk_acc[...] += _dot(g_buf[slot][...], q_ref[0, pl.ds(qs, tq), :])
            dq_ref[0, pl.ds(qs, tq), :] += _dot_tn(g_buf[slot][...], k_ref[0])

        def last():
            dk_ref[0] = dk_acc[...]
            dv_ref[0] = dv_acc[...]

        dk_acc[...] = jnp.zeros_like(dk_acc)
        dv_acc[...] = jnp.zeros_like(dv_acc)
        _chunk_pipeline(nq - 1 - kj, 1, matmuls, pointwise, accumulate, last)

    return pl.pallas_call(
        body, name="attn_bwd", grid=(N_HEADS, T // tk),
        in_specs=[pl.BlockSpec((1, T, HEAD_PAD), lambda h, j: (h, 0, 0)),
                  pl.BlockSpec((1, tk, HEAD_PAD), lambda h, j: (h, j, 0)),
                  pl.BlockSpec((1, tk, V_DIM), lambda h, j: (h, j, 0)),
                  pl.BlockSpec((T, V_DIM), lambda h, j: (0, h)),
                  pl.BlockSpec((1, 1, T), lambda h, j: (h, 0, 0)),
                  pl.BlockSpec((1, 1, T), lambda h, j: (h, 0, 0))],
        out_specs=[pl.BlockSpec((1, T, HEAD_PAD), lambda h, j: (h, 0, 0)),
                   pl.BlockSpec((1, tk, HEAD_PAD), lambda h, j: (h, j, 0)),
                   pl.BlockSpec((1, tk, V_DIM), lambda h, j: (h, j, 0))],
        out_shape=[jax.ShapeDtypeStruct((N_HEADS, T, HEAD_PAD), F32), jax.ShapeDtypeStruct((N_HEADS, T, HEAD_PAD), F32),
                   jax.ShapeDtypeStruct((N_HEADS, T, V_DIM), F32)],
        scratch_shapes=[pltpu.VMEM((tk, tq), F32)] * 4 + [pltpu.VMEM((tk, tq), BF16)] * 4
                       + [pltpu.VMEM((tk, HEAD_PAD), F32), pltpu.VMEM((tk, V_DIM), F32)],
        compiler_params=_params(dimension_semantics=("arbitrary", "arbitrary")),
    )(q, k, v, do, lse_row, delta_row)


def _bwd_proj(x, dx1, pos, proj, dq, dk, dv, dtail, du, g_in, w_in, g_cq, w_uq, g_ckv, w_ukv, gq, gk, conv_w,
              invf, sgn, tm):
    T = x.shape[0]
    nt = T // tm

    ts = min(SUB_TILE, tm)

    def body(x_ref, dx1_ref, pos_ref, lat_ref, cc_ref, cx_ref, dq_ref, dk_ref, dv_ref, dtail_ref, du_ref, dun_ref, *rest):
        consts, (gx_ref, h_ref, dproj_ref), sums = rest[:11], rest[11:14], rest[14:]
        cw_ref = consts[8]
        i = pl.program_id(0)

        @pl.when(i == 0)
        def _():
            for r in sums:
                r[...] = jnp.zeros_like(r)

        du_v = du_ref[...]
        not_last = jnp.where(i < nt - 1, 1.0, 0.0)
        nx0 = dun_ref[0:1, :] * not_last
        nx1 = dun_ref[1:2, :] * not_last
        row = lax.broadcasted_iota(jnp.int32, du_v.shape, 0)
        du1 = jnp.where(row == tm - 1, nx0, pltpu.roll(du_v, tm - 1, 0))
        du2 = jnp.where(row == tm - 2, nx0, jnp.where(row == tm - 1, nx1, pltpu.roll(du_v, tm - 2, 0)))
        dvc = cw_ref[2:3, :] * du_v + cw_ref[1:2, :] * du1 + cw_ref[0:1, :] * du2
        dproj_ref[:, 1536:2048] = (dvc * cx_ref[...]).astype(BF16)
        dproj_ref[:, 2048:2560] = (dvc * cc_ref[...]).astype(BF16)

        for r0 in range(0, tm, ts):
            rows = slice(r0, r0 + ts)
            work(x_ref.at[rows, :], dx1_ref.at[rows, :], pos_ref.at[rows, :], lat_ref.at[rows, :],
                 dq_ref.at[:, rows, :], dk_ref.at[:, rows, :], dv_ref.at[:, rows, :], dtail_ref.at[rows, :], *consts,
                 gx_ref.at[rows, :], h_ref.at[:, rows], dproj_ref.at[rows, :], *sums)

    def work(x_ref, dx1_ref, pos_ref, lat_ref, dq_ref, dk_ref, dv_ref, dtail_ref,
             g_in_ref, w_in_ref, g_cq_ref, w_uq_ref, g_ckv_ref, w_ukv_ref, gq_ref, gk_ref, cw_ref, invf_ref, sgn_ref,
             gx_ref, h_ref, dproj_ref, dw_uq_ref, dw_ukv_ref, dg_in_ref, dg_cq_ref, dg_ckv_ref, dgq_ref, dgk_ref):
        xv = x_ref[...]
        r0 = _rep(_inv_rms_mxu(xv), D_MODEL)
        xh0 = xv * r0
        g_in = g_in_ref[...]
        h_ref[...] = (xh0 * g_in).astype(BF16).T

        c_q = lat_ref[:, 0:Q_LORA]
        rq = _rep(_inv_rms_mxu(c_q), Q_LORA)
        xq = c_q * rq
        g_cq = g_cq_ref[...]
        cqn = (xq * g_cq).astype(BF16)
        c_kv = lat_ref[:, Q_LORA:Q_LORA + KV_LORA]
        rkv = _inv_rms_mxu(c_kv)
        xkv = c_kv * rkv
        g_ckv = g_ckv_ref[...]
        ckvn = (xkv * g_ckv).astype(BF16)
        kpe = lat_ref[:, 384:512]
        kpe_sq = kpe * kpe
        cos_b, sin_b = _rope_tables(pos_ref, invf_ref, sgn_ref)
        gq_a, gq_b = gq_ref[:, 0:NOPE], gq_ref[:, NOPE:HEAD_PAD]
        gk_a, gk_b = gk_ref[:, 0:NOPE], gk_ref[:, NOPE:HEAD_PAD]

        dproj_ref[:, 512:1536] = dtail_ref[:, 0:1024]
        dproj_ref[:, 2560:3072] = dtail_ref[:, 1024:1536]

        def dh_part(c0):
            return _dot_nt(dproj_ref[:, c0:c0 + 512], w_in_ref[:, c0:c0 + 512])

        later_chunks = ((512,), (1024,), (1536, 2048), (2560,))
        dh = jnp.zeros((ts, D_MODEL), F32)
        acc = dict(dh=dh, dkpe=jnp.zeros((ts, LANES), F32), dcqn=jnp.zeros((ts, Q_LORA), F32),
                   dckvn=jnp.zeros((ts, KV_LORA), F32))

        def dh_chunks():
            for chunks in later_chunks:
                for chunk in chunks:
                    acc["dh"] = acc["dh"] + dh_part(chunk)
                    yield

        def queries(hd):
            qh = _dot(cqn, w_uq_ref[hd])
            yield
            a, b = qh[:, 0:NOPE], qh[:, NOPE:HEAD_PAD]
            r = lax.rsqrt(_lane_sum(a * a + b * b) / QK_DIM + EPS)
            yield
            xa, xb = a * r, b * r
            dan = dq_ref[hd, :, 0:NOPE]
            dbr = dq_ref[hd, :, NOPE:HEAD_PAD]
            dbn = dbr * cos_b + _swap_rope_halves(dbr * sin_b)
            yield
            dgq_ref[:, 0:NOPE] += _colsum(dan * xa)
            dgq_ref[:, NOPE:HEAD_PAD] += _colsum(dbn * xb)
            dxa, dxb = dan * gq_a, dbn * gq_b
            cq = _lane_sum(dxa * xa + dxb * xb) / QK_DIM
            yield
            dqh = jnp.concatenate([r * (dxa - xa * cq), r * (dxb - xb * cq)], axis=-1).astype(BF16)
            yield
            dw_uq_ref[hd] += _dot_tn(cqn, dqh)
            yield
            acc["dcqn"] = acc["dcqn"] + _dot_nt(dqh, w_uq_ref[hd])
            yield

        def keys(hd):
            kvh = _dot(ckvn, w_ukv_ref[hd])
            yield
            ka = kvh[:, 0:NOPE]
            rk = lax.rsqrt(_lane_sum(ka * ka + kpe_sq) / QK_DIM + EPS)
            yield
            xka, xkb = ka * rk, kpe * rk
            dkan = dk_ref[hd, :, 0:NOPE]
            dkbr = dk_ref[hd, :, NOPE:HEAD_PAD]
            dkbn = dkbr * cos_b + _swap_rope_halves(dkbr * sin_b)
            yield
            dgk_ref[:, 0:NOPE] += _colsum(dkan * xka)
            dgk_ref[:, NOPE:HEAD_PAD] += _colsum(dkbn * xkb)
            dxka, dxkb = dkan * gk_a, dkbn * gk_b
            ck = _lane_sum(dxka * xka + dxkb * xkb) / QK_DIM
            yield
            acc["dkpe"] = acc["dkpe"] + rk * (dxkb - xkb * ck)
            dkvh = jnp.concatenate([rk * (dxka - xka * ck), dv_ref[hd]], axis=-1).astype(BF16)
            yield
            dw_ukv_ref[hd] += _dot_tn(ckvn, dkvh)
            yield
            acc["dckvn"] = acc["dckvn"] + _dot_nt(dkvh, w_ukv_ref[hd])
            yield

        chains = [dh_chunks()]
        for hd in range(N_HEADS):
            chains += [queries(hd), keys(hd)]
        _round_robin(chains, 5)
        dh, dkpe, dcqn, dckvn = acc["dh"], acc["dkpe"], acc["dcqn"], acc["dckvn"]

        dg_cq_ref[...] += _colsum(dcqn * xq)
        dxq = dcqn * g_cq
        dproj_ref[:, 0:Q_LORA] = (rq * (dxq - xq * _rep(_lane_sum(dxq * xq) / Q_LORA, Q_LORA))).astype(BF16)
        dg_ckv_ref[...] += _colsum(dckvn * xkv)
        dxkv = dckvn * g_ckv
        dproj_ref[:, 256:384] = (rkv * (dxkv - xkv * (_lane_sum(dxkv * xkv) / KV_LORA))).astype(BF16)
        dproj_ref[:, 384:512] = dkpe.astype(BF16)
        dh = dh + dh_part(0)
        dg_in_ref[...] += _colsum(dh * xh0)
        dxh = dh * g_in
        gx_ref[...] = dx1_ref[...] + r0 * (dxh - xh0 * _rep(_lane_sum(dxh * xh0) / D_MODEL, D_MODEL))

    row = lambda i: (i, 0)
    col = lambda c: (lambda i: (i, c))
    head_rows = lambda i: (0, i, 0)
    nxt = lambda i: (jnp.minimum((i + 1) * (tm // 8), T // 8 - 1), 0)
    in_specs = [pl.BlockSpec((tm, D_MODEL), row), pl.BlockSpec((tm, D_MODEL), row), pl.BlockSpec((tm, 1), row),
                pl.BlockSpec((tm, 512), col(0)), pl.BlockSpec((tm, 512), col(3)), pl.BlockSpec((tm, 512), col(4)),
                pl.BlockSpec((N_HEADS, tm, HEAD_PAD), head_rows), pl.BlockSpec((N_HEADS, tm, HEAD_PAD), head_rows),
                pl.BlockSpec((N_HEADS, tm, V_DIM), head_rows), pl.BlockSpec((tm, 1536), row),
                pl.BlockSpec((tm, CONV_W), row), pl.BlockSpec((8, CONV_W), nxt),
                _full((1, D_MODEL)), _full((D_MODEL, PROJ_EXT)), _full((1, Q_LORA)), _full((N_HEADS, Q_LORA, HEAD_PAD)),
                _full((1, KV_LORA)), _full((N_HEADS, KV_LORA, HEAD_PAD)), _full((1, HEAD_PAD)), _full((1, HEAD_PAD)),
                _full((3, CONV_W)), _full((1, LANES)), _full((1, LANES))]
    out_specs = [pl.BlockSpec((tm, D_MODEL), row), pl.BlockSpec((D_MODEL, tm), lambda i: (0, i)),
                 pl.BlockSpec((tm, PROJ_EXT), row),
                 _full((N_HEADS, Q_LORA, HEAD_PAD)), _full((N_HEADS, KV_LORA, HEAD_PAD)),
                 _full((1, D_MODEL)), _full((1, Q_LORA)), _full((1, KV_LORA)), _full((1, HEAD_PAD)), _full((1, HEAD_PAD))]
    out_shape = [jax.ShapeDtypeStruct((T, D_MODEL), F32), jax.ShapeDtypeStruct((D_MODEL, T), BF16),
                 jax.ShapeDtypeStruct((T, PROJ_EXT), BF16),
                 jax.ShapeDtypeStruct((N_HEADS, Q_LORA, HEAD_PAD), F32), jax.ShapeDtypeStruct((N_HEADS, KV_LORA, HEAD_PAD), F32),
                 jax.ShapeDtypeStruct((1, D_MODEL), F32), jax.ShapeDtypeStruct((1, Q_LORA), F32),
                 jax.ShapeDtypeStruct((1, KV_LORA), F32), jax.ShapeDtypeStruct((1, HEAD_PAD), F32),
                 jax.ShapeDtypeStruct((1, HEAD_PAD), F32)]
    return pl.pallas_call(
        body, name="bwd_proj", grid=(nt,), in_specs=in_specs, out_specs=out_specs, out_shape=out_shape,
        compiler_params=_params(dimension_semantics=("arbitrary",)),
    )(x, dx1, pos, proj, proj, proj, dq, dk, dv, dtail, du, du, g_in, w_in, g_cq, w_uq, g_ckv, w_ukv, gq, gk, conv_w,
      invf, sgn)


def _matmul_acc(a, b, tt, tn, parts):
    M, T = a.shape
    N = b.shape[1]
    n = len(parts)
    grid = (N // tn, T // tt)

    def body(a_ref, b_ref, *rest):
        part_refs, o_ref, out_refs, sems = rest[:n], rest[n], rest[n + 1:2 * n + 1], rest[2 * n + 1:]
        j, t = pl.program_id(0), pl.program_id(1)
        if n:
            start, drain = _scatter_steps(part_refs, out_refs, *sems)
            pl.when(jnp.logical_and(j == 0, t == 0))(start)

        @pl.when(t == 0)
        def _():
            o_ref[...] = jnp.zeros_like(o_ref)

        o_ref[...] += _dot(a_ref[...], b_ref[...])
        if n:
            pl.when(jnp.logical_and(j == grid[0] - 1, t == grid[1] - 1))(drain)

    sems = [pltpu.SemaphoreType.DMA((3 * n,)), pltpu.SemaphoreType.DMA((3 * n,)), pltpu.SemaphoreType.DMA((n,))]
    outs = pl.pallas_call(
        body, name="dw_in", grid=grid,
        in_specs=[pl.BlockSpec((M, tt), lambda j, t: (0, t)), pl.BlockSpec((tt, tn), lambda j, t: (t, j))] + [_ANY] * n,
        out_specs=[pl.BlockSpec((M, tn), lambda j, t: (0, j))] + [_ANY] * n,
        out_shape=[jax.ShapeDtypeStruct((M, N), F32)] + _scattered_shapes(parts),
        scratch_shapes=sems if n else [],
        compiler_params=_params(dimension_semantics=("arbitrary", "arbitrary")),
    )(a, b, *parts)
    return outs[0], outs[1:]


def _add_chips(parts, small_parts):
    arrays = list(parts) + [small_parts]

    def body(*refs):
        ins, outs = refs[:len(arrays)], refs[len(arrays):]
        for a_ref, o_ref in zip(ins, outs):
            part = lambda k: a_ref[k].astype(F32)
            o_ref[...] = ((part(0) + part(1)) + part(2)) + part(3)

    in_specs, out_specs, out_shape = [], [], []
    for a in arrays:
        _, rows, cols = a.shape
        in_specs.append(pl.BlockSpec((N_CHIPS, rows // 2, cols), lambda i: (0, i, 0)))
        out_specs.append(pl.BlockSpec((rows // 2, cols), lambda i: (i, 0)))
        out_shape.append(jax.ShapeDtypeStruct((rows, cols), F32))
    outs = pl.pallas_call(body, name="add_chips", grid=(2,), in_specs=in_specs, out_specs=out_specs,
                          out_shape=out_shape, compiler_params=_params(dimension_semantics=("arbitrary",)))(*arrays)
    return outs[:-1], outs[-1]


def _adamw_small(ws, gs, ms, vs):
    n = len(ws)

    def body(*refs):
        for i in range(n):
            w_ref, g_ref, m_ref, v_ref = (refs[k * n + i] for k in range(4))
            d_ref, nm_ref, nv_ref = (refs[(4 + k) * n + i] for k in range(3))
            _adamw_math(g_ref[...], w_ref, m_ref, v_ref, d_ref, nm_ref, nv_ref)

    shapes = [jax.ShapeDtypeStruct(w.shape, F32) for w in ws]
    outs = pl.pallas_call(body, name="adamw_small", out_shape=shapes * 3)(*ws, *gs, *ms, *vs)
    return outs[:n], outs[n:2 * n], outs[2 * n:]


def _adamw_math(gv, w_ref, m_ref, v_ref, d_ref, nm_ref, nv_ref):
    nm = B1 * m_ref[...] + (1.0 - B1) * gv
    nv = B2 * v_ref[...] + (1.0 - B2) * (gv * gv)
    m_hat = nm / (1.0 - B1 ** STEP)
    v_hat = nv / (1.0 - B2 ** STEP)
    d_ref[...] = -LR * (m_hat / (jnp.sqrt(v_hat) + ADAM_EPS) + WD * w_ref[...])
    nm_ref[...] = nm
    nv_ref[...] = nv


def _adamw_halves(w, mine, other, m, v, c, name, transposed):
    hr, cols = mine.shape

    def body(c_ref, w_ref, mine_ref, other_ref, m_ref, v_ref, g_ref, d_ref, nm_ref, nv_ref, *picked):
        gv = jnp.where(pl.program_id(0) == c_ref[0], mine_ref[...], other_ref[...])
        if transposed:
            picked[0][...] = gv
            _store_transposed(picked[0], g_ref)
            gv = g_ref[...]
        else:
            g_ref[...] = gv
        _adamw_math(gv, w_ref, m_ref, v_ref, d_ref, nm_ref, nv_ref)

    if transposed:
        half = pl.BlockSpec((cols, hr), lambda i, c_ref: (0, i))
    else:
        half = pl.BlockSpec((hr, cols), lambda i, c_ref: (i, 0))
    whole = pl.BlockSpec((hr, cols), lambda i, c_ref: (0, 0))
    shp = jax.ShapeDtypeStruct(w.shape, F32)
    return pl.pallas_call(
        body, name=name, out_shape=[shp] * 4,
        grid_spec=pltpu.PrefetchScalarGridSpec(num_scalar_prefetch=1, grid=(2,), in_specs=[half, whole, whole, half, half],
                                               out_specs=[half] * 4,
                                               scratch_shapes=[pltpu.VMEM((hr, cols), F32)] if transposed else []),
        compiler_params=_params(dimension_semantics=("arbitrary",)),
    )(c.reshape(1), w, mine, other, m, v)


_ANY = pl.BlockSpec(memory_space=pl.ANY)


def _mesh_pos():
    return lax.axis_index("x"), lax.axis_index("y"), lax.axis_index("c")


def _other_chips(x, y):
    return [(1 - x, y), (x, 1 - y), (1 - x, 1 - y)]


def _remote(src, dst, send_sems, recv_sems, k, to):
    return pltpu.make_async_remote_copy(src_ref=src, dst_ref=dst, send_sem=send_sems.at[k], recv_sem=recv_sems.at[k],
                                        device_id=to, device_id_type=MESH)


def _gather_weights(shards, n_transposed):
    n = len(shards)
    shapes = [s.shape[::-1] if i < n_transposed else s.shape for i, s in enumerate(shards)]

    def body(*refs):
        start, forward, drain = _gather_steps(shapes, refs[:n], refs[n:2 * n], refs[2 * n:3 * n], *refs[3 * n:])
        start()
        forward()
        drain()

    vmem = pl.BlockSpec(memory_space=pltpu.VMEM)
    return pl.pallas_call(
        body, name="gather_weights", in_specs=[vmem] * n, out_specs=[_ANY] * n,
        out_shape=_gathered_shapes(shapes), scratch_shapes=_gather_scratch(shapes), compiler_params=_params(),
    )(*shards)


def _travel_shape(shape):
    rows, cols = shape
    return (rows, HEAD_PAD if cols == QK_DIM else cols)


def _gathered_shapes(shapes):
    return [jax.ShapeDtypeStruct((N_CHIPS,) + _travel_shape(s), BF16) for s in shapes]


def _gather_scratch(shapes):
    n = len(shapes)
    return ([pltpu.VMEM(_travel_shape(s), BF16) for s in shapes]
            + [pltpu.SemaphoreType.DMA((6 * n,)), pltpu.SemaphoreType.DMA((6 * n,)), pltpu.SemaphoreType.DMA((n,))])


def _gather_steps(shapes, ins, outs, stage, send_sems, recv_sems, local_sems):
    n = len(shapes)
    halved = [s[0] % 32 == 0 for s in shapes]

    def part(i, ref, hc):
        if not halved[i]:
            return ref
        hr = shapes[i][0] // 2
        return ref.at[pl.ds(hc * hr, hr), :]

    def to_chip(i, j, x, y, c):
        cx, cy = _other_chips(x, y)[j]
        return _remote(part(i, stage[i], c), part(i, outs[i].at[2 * x + y], c), send_sems, recv_sems, 6 * i + j, (cx, cy, c))

    def to_sibling(i, j, x, y, c):
        cx, cy = _other_chips(x, y)[j]
        got = part(i, outs[i].at[2 * cx + cy], c)
        return _remote(got, got, send_sems, recv_sems, 6 * i + 3 + j, (x, y, 1 - c))

    def local(i, x, y):
        return pltpu.make_async_copy(stage[i], outs[i].at[2 * x + y], local_sems.at[i])

    def start():
        x, y, c = _mesh_pos()
        for i in range(n):
            cols = shapes[i][1]
            if stage[i].shape[1] != cols:
                stage[i][...] = jnp.zeros_like(stage[i])
            if ins[i].shape == shapes[i]:
                stage[i][:, 0:cols] = ins[i][...].astype(BF16)
            else:
                _store_transposed(ins[i], stage[i])
            local(i, x, y).start()
            for j in range(3):
                to_chip(i, j, x, y, c).start()

    def forward():
        x, y, c = _mesh_pos()
        for i in range(n):
            for j, (cx, cy) in enumerate(_other_chips(x, y)):
                got = part(i, outs[i].at[2 * cx + cy], c)
                _remote(got, got, send_sems, recv_sems, 6 * i + j, (cx, cy, c)).wait_recv()
                if halved[i]:
                    to_sibling(i, j, x, y, c).start()

    def drain():
        x, y, c = _mesh_pos()
        for i in range(n):
            for j, (cx, cy) in enumerate(_other_chips(x, y)):
                if halved[i]:
                    got = part(i, outs[i].at[2 * cx + cy], 1 - c)
                    _remote(got, got, send_sems, recv_sems, 6 * i + 3 + j, (x, y, 1 - c)).wait_recv()
                    to_sibling(i, j, x, y, c).wait_send()
                to_chip(i, j, x, y, c).wait_send()
            local(i, x, y).wait()

    return start, forward, drain


def _swap_halves(grads, whole, name):
    n, m = len(grads), len(grads) + len(whole)

    def body(*refs):
        ins, outs, send_sems, recv_sems = refs[:m], refs[m:2 * m], refs[2 * m], refs[2 * m + 1]
        x, y, c = _mesh_pos()
        cps = []
        for i in range(m):
            src = ins[i]
            if i < n:
                hr = grads[i].shape[1] // 2
                src = src.at[:, pl.ds((1 - c) * hr, hr), :]
            cp = _remote(src, outs[i], send_sems, recv_sems, i, (x, y, 1 - c))
            cp.start()
            cps.append(cp)
        for cp in cps:
            cp.wait()

    out_shape = [jax.ShapeDtypeStruct((g.shape[0], g.shape[1] // 2, g.shape[2]), F32) for g in grads]
    out_shape += [jax.ShapeDtypeStruct(w.shape, F32) for w in whole]
    outs = pl.pallas_call(
        body, name=name, in_specs=[_ANY] * m, out_specs=[_ANY] * m, out_shape=out_shape,
        scratch_shapes=[pltpu.SemaphoreType.DMA((m,)), pltpu.SemaphoreType.DMA((m,))],
    )(*grads, *whole)
    return outs[:n], outs[n:]


def _scattered_shapes(parts):
    return [jax.ShapeDtypeStruct(p.shape if p.ndim == 3 else (N_CHIPS,) + p.shape, p.dtype) for p in parts]


def _scatter_steps(ins, outs, send_sems, recv_sems, local_sems):
    n = len(ins)

    def src(i, k):
        return ins[i].at[k] if len(ins[i].shape) == 3 else ins[i]

    def sends(x, y, c):
        return [_remote(src(i, 2 * cx + cy), outs[i].at[2 * x + y], send_sems, recv_sems, 3 * i + j, (cx, cy, c))
                for i in range(n) for j, (cx, cy) in enumerate(_other_chips(x, y))]

    def local(i, x, y):
        return pltpu.make_async_copy(src(i, 2 * x + y), outs[i].at[2 * x + y], local_sems.at[i])

    def start():
        x, y, c = _mesh_pos()
        for i in range(n):
            local(i, x, y).start()
        for cp in sends(x, y, c):
            cp.start()

    def drain():
        x, y, c = _mesh_pos()
        for i in range(n):
            for j, (cx, cy) in enumerate(_other_chips(x, y)):
                got = outs[i].at[2 * cx + cy]
                _remote(got, got, send_sems, recv_sems, 3 * i + j, (cx, cy, c)).wait_recv()
        for cp in sends(x, y, c):
            cp.wait_send()
        for i in range(n):
            local(i, x, y).wait()

    return start, drain


def _add_pair(grads, from_sibling, small, small_sibling, c):
    n = len(grads)

    def body(c_ref, *refs):
        ins, outs = refs[:2 * n + 2], refs[2 * n + 2:]
        for i in range(n + 1):
            outs[i][...] = (ins[2 * i][...] + ins[2 * i + 1][...]).astype(outs[i].dtype)

    in_specs, out_specs, out_shape, args = [], [], [], []
    for g, r in zip(grads, from_sibling):
        _, hr, cols = r.shape
        in_specs += [pl.BlockSpec((1, hr, cols), lambda k, c_ref: (k, c_ref[0], 0)),
                     pl.BlockSpec((1, hr, cols), lambda k, c_ref: (k, 0, 0))]
        out_specs.append(pl.BlockSpec((1, hr, cols), lambda k, c_ref: (k, 0, 0)))
        out_shape.append(jax.ShapeDtypeStruct(r.shape, BF16))
        args += [g, r]
    whole = pl.BlockSpec(small.shape, lambda k, c_ref: (0, 0))
    in_specs += [whole, whole]
    out_specs.append(whole)
    out_shape.append(jax.ShapeDtypeStruct(small.shape, F32))
    outs = pl.pallas_call(
        body, name="add_pair", out_shape=out_shape,
        grid_spec=pltpu.PrefetchScalarGridSpec(num_scalar_prefetch=1, grid=(N_CHIPS,), in_specs=in_specs,
                                               out_specs=out_specs),
        compiler_params=_params(dimension_semantics=("arbitrary",)),
    )(c.reshape(1), *args, small, small_sibling)
    return outs[:n], outs[n]


def _scatter_w_in(dw_in_e, from_sibling):
    hr = from_sibling.shape[1]
    shard = (N_CHIPS, hr, SHARD_COLS_IN)

    def body(g_in, r_in, out, g_buf, r_buf, p_buf, load_sems, send_sems, recv_sems, local_sems):
        c = lax.axis_index("c")
        loads = (pltpu.make_async_copy(g_in.at[0, pl.ds(c * hr, hr), :], g_buf, load_sems.at[0]),
                 pltpu.make_async_copy(r_in.at[0], r_buf, load_sems.at[1]))
        for cp in loads:
            cp.start()
        for cp in loads:
            cp.wait()
        g_buf[...] += r_buf[...]
        p_buf[0, :, 0:KPE_END] = g_buf[:, 0:KPE_END].astype(BF16)
        p_buf[0, :, KPE_END:SHARD_COLS_IN] = g_buf[:, KPE_END + KPE_PAD:SHARD_COLS_IN + KPE_PAD].astype(BF16)
        for k in range(1, N_CHIPS):
            p_buf[k] = g_buf[:, SHARD_COLS_IN * k + KPE_PAD:SHARD_COLS_IN * (k + 1) + KPE_PAD].astype(BF16)
        start, drain = _scatter_steps([p_buf], [out], send_sems, recv_sems, local_sems)
        start()
        drain()

    return pl.pallas_call(
        body, name="scatter_grads", in_specs=[_ANY] * 2, out_specs=_ANY, out_shape=jax.ShapeDtypeStruct(shard, BF16),
        scratch_shapes=[pltpu.VMEM((hr, PROJ_EXT), F32)] * 2 + [pltpu.VMEM(shard, BF16)]
                       + [pltpu.SemaphoreType.DMA((2,)), pltpu.SemaphoreType.DMA((3,)), pltpu.SemaphoreType.DMA((3,)),
                          pltpu.SemaphoreType.DMA((1,))],
        compiler_params=_params(),
    )(dw_in_e, from_sibling)


def _share_halves(halves):
    n = len(halves)

    def body(*refs):
        ins, outs, send_sems, recv_sems = refs[:n], refs[n:2 * n], refs[2 * n], refs[2 * n + 1]
        x, y, c = _mesh_pos()
        cps = [_remote(ins[i], outs[i], send_sems, recv_sems, i, (x, y, 1 - c)) for i in range(n)]
        for cp in cps:
            cp.start()
        for cp in cps:
            cp.wait()

    return pl.pallas_call(
        body, name="share_halves", in_specs=[_ANY] * n, out_specs=[_ANY] * n,
        out_shape=[jax.ShapeDtypeStruct(h.shape, h.dtype) for h in halves],
        scratch_shapes=[pltpu.SemaphoreType.DMA((n,)), pltpu.SemaphoreType.DMA((n,))],
    )(*halves)


SHARD_COLS_IN = IN_TOTAL // N_CHIPS
KPE_END = Q_LORA + KV_LORA + ROPE
KPE_PAD = PROJ_EXT - IN_TOTAL


def _by_cols(a):
    return a.transpose(1, 0, 2).reshape(a.shape[1], N_CHIPS * a.shape[2])


def _assemble_early(c_in, c_uq, c_ukv, c_conv):
    return c_in, c_uq, c_ukv, _by_cols(c_conv).astype(F32)


def _assemble_late(c_o, c_pl, c_plg):
    return c_o.reshape(D_MODEL, D_MODEL), _by_cols(c_pl), c_plg.reshape(D_MODEL, D_MODEL)


def _split_others(dw_uq, dw_ukv, dw_o, dw_pl, dw_plg):
    chip_major = lambda a: a.reshape(a.shape[0], N_CHIPS, a.shape[1] // N_CHIPS).transpose(1, 0, 2)
    return [dw_uq[:, :, :QK_DIM], dw_ukv, dw_o.reshape(N_CHIPS, D_MODEL // N_CHIPS, D_MODEL),
            chip_major(dw_pl), dw_plg.reshape(N_CHIPS, D_MODEL // N_CHIPS, D_MODEL)]


def _local_step(x, p, pos, tgt, gains, early, late_shards, late_gathered, tm, tq):
    c_in, w_uq_e, w_ukv, conv_w = early
    g_in, g_cq, g_ckv, g_q, g_k, g_oa, g_oc, g_pl = gains
    T = x.shape[0]
    zpad = lambda a, n: jnp.concatenate([a, jnp.zeros(a.shape[:-1] + (n,), a.dtype)], axis=-1)
    gq, gk = zpad(g_q, HEAD_PAD - QK_DIM), zpad(g_k, HEAD_PAD - QK_DIM)
    inv_freq = 1.0 / (ROPE_THETA ** (jnp.arange(0, ROPE, 2, dtype=F32) / ROPE))
    invf = jnp.concatenate([inv_freq, inv_freq, jnp.zeros((64,), F32)]).reshape(1, LANES)
    sgn = jnp.concatenate([-jnp.ones((32,), F32), jnp.ones((32,), F32), jnp.zeros((64,), F32)]).reshape(1, LANES)

    (proj, q, k, v, w_in_e), gathered = _fwd_proj(x, pos, g_in, c_in, g_cq, w_uq_e, g_ckv, w_ukv, gq, gk, invf, sgn,
                                                  late_shards, min(2 * tm, T))
    w_o, w_pl, w_plg = _assemble_late(*(gathered if late_shards else late_gathered))
    o, lse = _attn_fwd(q, k, v, tq)
    (dx1, do, delta, dtail, du, dw_o, dw_pl, dw_plg, dg_oa, dg_oc, dg_pl, dconv, loss) = _tail(
        x, o, proj, p, tgt, g_oa, g_oc, g_pl, conv_w, w_o, w_pl, w_plg, tm)
    dq, dk, dv = _attn_bwd(q, k, v, do, lse, delta, tq)
    (gx, h, dproj, dw_uq_e, dw_ukv, dg_in, dg_cq, dg_ckv, dgq, dgk) = _bwd_proj(
        x, dx1, pos, proj, dq, dk, dv, dtail, du, g_in, w_in_e, g_cq, w_uq_e, g_ckv, w_ukv, gq, gk, conv_w, invf, sgn, tm)
    wgrads = (dw_uq_e, dw_ukv, dw_o, dw_pl, dw_plg)
    ggrads = (dg_in, dg_cq, dg_ckv, dgq, dgk, dg_oa, dg_oc, dg_pl)
    return loss, gx, (h, dproj), wgrads, ggrads, dconv


def kernel(x, p, positions, g_in, w_in, g_cq, w_uq, g_ckv, w_ukv, g_q, g_k, conv_w, g_oa, g_oc, w_o, w_pl, w_plg, g_pl, loss_target, m_g_in, m_w_in, m_g_cq, m_w_uq, m_g_ckv, m_w_ukv, m_g_q, m_g_k, m_conv_w, m_g_oa, m_g_oc, m_w_o, m_w_pl, m_w_plg, m_g_pl, v_g_in, v_w_in, v_g_cq, v_w_uq, v_g_ckv, v_w_ukv, v_g_q, v_g_k, v_conv_w, v_g_oa, v_g_oc, v_w_o, v_w_pl, v_w_plg, v_g_pl):
    T = x.shape[1]
    c = lax.axis_index("c")
    chip = 2 * lax.axis_index("x") + lax.axis_index("y")
    gains = [g.reshape(1, -1) for g in (g_in, g_cq, g_ckv, g_q, g_k, g_oa, g_oc, g_pl)]

    transposed = ("w_in", "w_uq")
    early = _assemble_early(*_gather_weights([w_in[0].T, w_uq[0].T, w_ukv[0], conv_w[0]], len(transposed)))

    loss, gx, (h_t, dproj), wgrads, ggrads, dconv = _local_step(
        x[0], p[0, 0], positions.reshape(T, 1), loss_target[0], gains, early, [w_o[0], w_pl[0], w_plg[0]], None, 256, 512)

    others_cm = _split_others(*wgrads)
    small_parts = [a.reshape(-1, LANES) for a in (*ggrads, loss, dconv)]
    small_rows = [a.shape[0] for a in small_parts]
    tile_rows = [-(-r // 8) * 8 for r in small_rows]
    tile_rows[-1] += -sum(tile_rows) % 16
    small = jnp.concatenate([jnp.pad(a, ((0, t - r), (0, 0))) for a, r, t in zip(small_parts, small_rows, tile_rows)])
    from_sibling, (small_sibling,) = _swap_halves(others_cm, [small], "pair_grads")
    chip_parts, chip_small = _add_pair(others_cm, from_sibling, small, small_sibling, c)
    dw_in_e, exchanged = _matmul_acc(h_t, dproj, min(4096, T), 512, [*chip_parts, chip_small])
    dw_in_e = dw_in_e[None]
    (w_in_sibling,), _ = _swap_halves([dw_in_e], [], "pair_w_in")
    by_chip = [_scatter_w_in(dw_in_e, w_in_sibling), *exchanged[:-1]]
    halves, small_total = _add_chips(by_chip, exchanged[-1])
    other_halves = _share_halves(halves)

    gg, off = [], 0
    for rows, tiled in zip(small_rows, tile_rows):
        gg.append(small_total[off:off + rows].reshape(1, -1))
        off += tiled
    loss_out = gg[8][0, 0]
    conv_total = gg[9].reshape(3, CONV_W)
    conv_g = lax.dynamic_slice(conv_total, (0, chip * (CONV_W // N_CHIPS)), (3, CONV_W // N_CHIPS))
    g_by_name = dict(g_in=gg[0], g_cq=gg[1], g_ckv=gg[2], g_q=gg[3][:, :QK_DIM], g_k=gg[4][:, :QK_DIM], conv_w=conv_g,
                     g_oa=gg[5], g_oc=gg[6], g_pl=gg[7])
    half_by_name = dict(zip(("w_in", "w_uq", "w_ukv", "w_o", "w_pl", "w_plg"), zip(halves, other_halves)))
    weights = dict(g_in=g_in, w_in=w_in, g_cq=g_cq, w_uq=w_uq, g_ckv=g_ckv, w_ukv=w_ukv, g_q=g_q, g_k=g_k,
                   conv_w=conv_w, g_oa=g_oa, g_oc=g_oc, w_o=w_o, w_pl=w_pl, w_plg=w_plg, g_pl=g_pl)
    ms = dict(g_in=m_g_in, w_in=m_w_in, g_cq=m_g_cq, w_uq=m_w_uq, g_ckv=m_g_ckv, w_ukv=m_w_ukv, g_q=m_g_q, g_k=m_g_k,
              conv_w=m_conv_w, g_oa=m_g_oa, g_oc=m_g_oc, w_o=m_w_o, w_pl=m_w_pl, w_plg=m_w_plg, g_pl=m_g_pl)
    vs = dict(g_in=v_g_in, w_in=v_w_in, g_cq=v_g_cq, w_uq=v_w_uq, g_ckv=v_g_ckv, w_ukv=v_w_ukv, g_q=v_g_q, g_k=v_g_k,
              conv_w=v_conv_w, g_oa=v_g_oa, g_oc=v_g_oc, w_o=v_w_o, w_pl=v_w_pl, w_plg=v_w_plg, g_pl=v_g_pl)
    names = list(weights)
    flat = lambda a: a.reshape(-1, a.shape[-1])
    small_names = list(g_by_name)
    small_out = _adamw_small([flat(weights[n]) for n in small_names], [flat(g_by_name[n]) for n in small_names],
                             [flat(ms[n]) for n in small_names], [flat(vs[n]) for n in small_names])
    results = {n: (flat(g_by_name[n]), *(out[i] for out in small_out)) for i, n in enumerate(small_names)}
    for n in half_by_name:
        shard = (lambda a: a[0].T) if n in transposed else flat
        out = _adamw_halves(shard(weights[n]), *half_by_name[n], shard(ms[n]), shard(vs[n]), c, "adamw_" + n,
                            n in transposed)
        results[n] = [a.T for a in out] if n in transposed else out
    per_kind = [[results[n][kind].reshape(weights[n].shape) for n in names] for kind in range(4)]
    return (loss_out, gx.reshape(x.shape), *per_kind[0], *per_kind[1], *per_kind[2], *per_kind[3])
```

```python
import math

import jax
import jax.numpy as jnp
from jax import lax
from jax.experimental import pallas as pl
from jax.experimental.pallas import tpu as pltpu

F32 = jnp.float32
BF16 = jnp.bfloat16

D_MODEL = 1024
N_HEADS = 4
NOPE = 128
ROPE = 64
V_DIM = 128
QK_DIM = NOPE + ROPE
HEAD_PAD = 256
Q_LORA = 256
KV_LORA = 128
ATTN_W = 512
CONV_W = 512
PLE = 256
IN_TOTAL = 3008
PROJ_EXT = 3072
ROPE_THETA = 10000.0
EPS = 1e-6
SCALE = 1.0 / math.sqrt(QK_DIM)
LOG2E = math.log2(math.e)
EXP2_SCALE = SCALE * LOG2E
NEG = -1e30
SOFTMAX_ROWS = 32
SUB_TILE = 256

LR, B1, B2, ADAM_EPS, WD, STEP = 0.001, 0.9, 0.999, 1e-08, 0.01, 10

N_CHIPS = 4
LANES = 128
VMEM_LIMIT = 56 * 1024 * 1024
MESH = pl.DeviceIdType.MESH


def _params(**kw):
    return pltpu.CompilerParams(vmem_limit_bytes=VMEM_LIMIT, **kw)


def _inv_rms(x, n):
    return lax.rsqrt(jnp.sum(x * x, axis=-1, keepdims=True) / n + EPS)


def _lane_sum(a):
    folded = a[:, 0:LANES]
    for c0 in range(LANES, a.shape[1], LANES):
        folded = folded + a[:, c0:c0 + LANES]
    head = folded.astype(BF16)
    tail = (folded - head.astype(F32)).astype(BF16)
    return _dot(jnp.concatenate([head, tail], axis=1), jnp.ones((2 * LANES, LANES), BF16))


def _inv_rms_mxu(x):
    return lax.rsqrt(_lane_sum(x * x) / x.shape[1] + EPS)


def _rep(r, width):
    return r if width == LANES else jnp.tile(r, (1, width // LANES))


def _sigmoid(z):
    return jax.nn.sigmoid(z)


def _swap_rope_halves(b):
    lane = lax.broadcasted_iota(jnp.int32, b.shape, 1)
    swapped = jnp.where(lane < 32, pltpu.roll(b, 96, 1), pltpu.roll(b, 32, 1))
    return jnp.where(lane < ROPE, swapped, 0.0)


def _dot(a, b):
    return jnp.dot(a, b, preferred_element_type=F32)


def _dot_nt(a, b):
    return lax.dot_general(a, b, (((1,), (1,)), ((), ())), preferred_element_type=F32)


def _dot_tn(a, b):
    return lax.dot_general(a, b, (((0,), (0,)), ((), ())), preferred_element_type=F32)


def _colsum(a):
    return jnp.sum(a, axis=0, keepdims=True)


def _store_transposed(src_ref, dst_ref):
    r, c = src_ref.shape
    for r0 in range(0, r, LANES):
        h = min(LANES, r - r0)
        for c0 in range(0, c, LANES):
            w = min(LANES, c - c0)
            piece = src_ref[r0:r0 + h, c0 + w - LANES:c0 + w]
            if h < LANES:
                piece = jnp.concatenate([piece, jnp.zeros((LANES - h, LANES), piece.dtype)], axis=0)
            dst_ref[c0:c0 + w, r0:r0 + h] = piece.T[LANES - w:, 0:h].astype(dst_ref.dtype)


def _full(shape):
    return pl.BlockSpec(shape, lambda *_: (0,) * len(shape))


def _round_robin(chains, width):
    waiting, active = list(chains), []
    while waiting or active:
        while waiting and len(active) < width:
            active.append(waiting.pop(0))
        for chain in list(active):
            if next(chain, _DONE) is _DONE:
                active.remove(chain)


_DONE = object()


def _rope_tables(pos_ref, invf_ref, sgn_ref):
    ang = pos_ref[...].astype(F32) * invf_ref[...]
    return jnp.cos(ang), jnp.sin(ang) * sgn_ref[...]


def _fwd_proj(x, pos, g_in, c_in, g_cq, w_uq, g_ckv, w_ukv, gq, gk, invf, sgn, late_shards, tm):
    T = x.shape[0]
    nt = T // tm
    n_late = len(late_shards)
    ts = min(SUB_TILE, tm)

    def body(x_ref, pos_ref, g_in_ref, c_in_ref, g_cq_ref, w_uq_ref, g_ckv_ref, w_ukv_ref, gq_ref, gk_ref,
             invf_ref, sgn_ref, *rest):
        late_in, (proj_ref, q_ref, k_ref, v_ref, w_in_ref) = rest[:n_late], rest[n_late:n_late + 5]
        late_out, late_scratch = rest[n_late + 5:2 * n_late + 5], rest[2 * n_late + 5:]
        i = pl.program_id(0)

        @pl.when(i == 0)
        def _():
            w_in_ref[:, 0:KPE_END] = c_in_ref[0, :, 0:KPE_END]
            w_in_ref[:, KPE_END:KPE_END + KPE_PAD] = jnp.zeros((D_MODEL, KPE_PAD), BF16)
            w_in_ref[:, KPE_END + KPE_PAD:SHARD_COLS_IN + KPE_PAD] = c_in_ref[0, :, KPE_END:SHARD_COLS_IN]
            for chip in range(1, N_CHIPS):
                w_in_ref[:, SHARD_COLS_IN * chip + KPE_PAD:SHARD_COLS_IN * (chip + 1) + KPE_PAD] = c_in_ref[chip]

        if n_late:
            start, forward, drain = _gather_steps([s.shape for s in late_shards], late_in, late_out,
                                                  late_scratch[:n_late], *late_scratch[n_late:])
            pl.when(i == 0)(start)
            pl.when(i == nt // 2)(forward)

        for r0 in range(0, tm, ts):
            rows = slice(r0, r0 + ts)
            xv = x_ref[rows, :]
            h = (xv * _rep(_inv_rms_mxu(xv), D_MODEL) * g_in_ref[...]).astype(BF16)
            lat = _dot(h, w_in_ref[:, 0:512])
            proj_ref[rows, 0:512] = lat
            c_q = lat[:, 0:Q_LORA]
            cqn = (c_q * _rep(_inv_rms_mxu(c_q), Q_LORA) * g_cq_ref[...]).astype(BF16)
            c_kv = lat[:, Q_LORA:Q_LORA + KV_LORA]
            ckvn = (c_kv * _inv_rms_mxu(c_kv) * g_ckv_ref[...]).astype(BF16)
            kpe = lat[:, 384:512]
            kpe_sq = kpe * kpe
            cos_b, sin_b = _rope_tables(pos_ref.at[rows, :], invf_ref, sgn_ref)
            gq_a, gq_b = gq_ref[:, 0:NOPE], gq_ref[:, NOPE:HEAD_PAD]
            gk_a, gk_b = gk_ref[:, 0:NOPE], gk_ref[:, NOPE:HEAD_PAD]

            def projections(rows=rows, h=h):
                for c0 in range(512, PROJ_EXT, 512):
                    proj_ref[rows, c0:c0 + 512] = _dot(h, w_in_ref[:, c0:c0 + 512])
                    yield

            def queries(hd, rows=rows, cqn=cqn, cos_b=cos_b, sin_b=sin_b, gq_a=gq_a, gq_b=gq_b):
                qh = _dot(cqn, w_uq_ref[hd])
                yield
                a, b = qh[:, 0:NOPE], qh[:, NOPE:HEAD_PAD]
                r = lax.rsqrt(_lane_sum(a * a + b * b) / QK_DIM + EPS)
                yield
                bn = b * r * gq_b
                q_ref[hd, rows, 0:NOPE] = (a * r * gq_a).astype(BF16)
                q_ref[hd, rows, NOPE:HEAD_PAD] = (bn * cos_b + _swap_rope_halves(bn) * sin_b).astype(BF16)
                yield

            def keys(hd, rows=rows, ckvn=ckvn, kpe=kpe, kpe_sq=kpe_sq, cos_b=cos_b, sin_b=sin_b, gk_a=gk_a, gk_b=gk_b):
                kvh = _dot(ckvn, w_ukv_ref[hd])
                yield
                ka = kvh[:, 0:NOPE]
                rk = lax.rsqrt(_lane_sum(ka * ka + kpe_sq) / QK_DIM + EPS)
                yield
                kbn = kpe * rk * gk_b
                k_ref[hd, rows, 0:NOPE] = (ka * rk * gk_a).astype(BF16)
                k_ref[hd, rows, NOPE:HEAD_PAD] = (kbn * cos_b + _swap_rope_halves(kbn) * sin_b).astype(BF16)
                v_ref[hd, rows, 0:V_DIM] = kvh[:, NOPE:HEAD_PAD].astype(BF16)
                v_ref[hd, rows, V_DIM:2 * V_DIM] = jnp.ones((ts, V_DIM), BF16)
                yield

            chains = [projections()]
            for hd in range(N_HEADS):
                chains += [queries(hd), keys(hd)]
            _round_robin(chains, 4)

        if n_late:
            pl.when(i == nt - 1)(drain)

    row = lambda i: (i, 0)
    head_rows = lambda i: (0, i, 0)
    outs = pl.pallas_call(
        body, name="fwd_proj", grid=(nt,),
        in_specs=[pl.BlockSpec((tm, D_MODEL), row), pl.BlockSpec((tm, 1), row), _full((1, D_MODEL)),
                  _full((N_CHIPS, D_MODEL, SHARD_COLS_IN)), _full((1, Q_LORA)), _full((N_HEADS, Q_LORA, HEAD_PAD)),
                  _full((1, KV_LORA)), _full((N_HEADS, KV_LORA, HEAD_PAD)), _full((1, HEAD_PAD)), _full((1, HEAD_PAD)),
                  _full((1, LANES)), _full((1, LANES))] + [_full(s.shape) for s in late_shards],
        out_specs=[pl.BlockSpec((tm, PROJ_EXT), row), pl.BlockSpec((N_HEADS, tm, HEAD_PAD), head_rows),
                   pl.BlockSpec((N_HEADS, tm, HEAD_PAD), head_rows), pl.BlockSpec((N_HEADS, tm, 2 * V_DIM), head_rows),
                   _full((D_MODEL, PROJ_EXT))] + [_ANY] * n_late,
        out_shape=[jax.ShapeDtypeStruct((T, PROJ_EXT), F32), jax.ShapeDtypeStruct((N_HEADS, T, HEAD_PAD), BF16),
                   jax.ShapeDtypeStruct((N_HEADS, T, HEAD_PAD), BF16), jax.ShapeDtypeStruct((N_HEADS, T, 2 * V_DIM), BF16),
                   jax.ShapeDtypeStruct((D_MODEL, PROJ_EXT), BF16)] + _gathered_shapes([s.shape for s in late_shards]),
        scratch_shapes=_gather_scratch([s.shape for s in late_shards]) if n_late else [],
        compiler_params=_params(dimension_semantics=("arbitrary",)),
    )(x, pos, g_in, c_in, g_cq, w_uq, g_ckv, w_ukv, gq, gk, invf, sgn, *late_shards)
    return outs[:5], outs[5:]


def _chunk_pipeline(n_loop, lag, matmuls, pointwise, accumulate, last):
    slots = lag + 1

    def iteration(t, slot, pending=True):
        matmuls(jnp.minimum(t + lag, n_loop), (slot + lag) % slots)
        if pending:
            accumulate(t - lag, (slot + 1) % slots)
        pointwise(t, slot, False)

    def finish(slot, pending):
        for back in range(pending, 0, -1):
            accumulate(n_loop - back, (slot - back) % slots)
        pointwise(n_loop, slot, True)
        accumulate(n_loop, slot)
        last()

    for u in range(lag):
        matmuls(jnp.minimum(u, n_loop), u)
    for u in range(lag):
        pl.when(u < n_loop)(lambda u=u: iteration(u, u, pending=False))

    n_main = jnp.maximum(n_loop - lag, 0)

    def unrolled(tt, carry):
        for j in range(slots):
            iteration(lag + slots * tt + j, (lag + j) % slots)
        return carry

    lax.fori_loop(0, n_main // slots, unrolled, 0)
    rest = lax.rem(n_main, slots)
    t0 = n_loop - rest

    for r in range(slots):
        @pl.when(jnp.logical_and(n_loop >= lag, rest == r))
        def _():
            for j in range(r):
                iteration(t0 + j, (lag + j) % slots)
            finish((lag + r) % slots, lag)

    for short in range(lag):
        pl.when(n_loop == short)(lambda short=short: finish(short, short))


def _attn_fwd(q, k, v, tq):
    T = q.shape[1]
    tk = tq
    rc = min(SOFTMAX_ROWS, tq)

    def body(q_ref, k_ref, v_ref, o_ref, lse_ref, s0, s1, s2, p0, p1, p2, a0, a1, a2, m_ref, acc_ref):
        qi = pl.program_id(1)
        s_buf, p_buf, a_buf = (s0, s1, s2), (p0, p1, p2), (a0, a1, a2)

        def scores(t, slot):
            ks = pl.multiple_of(t * tk, tk)
            s_buf[slot][...] = _dot_nt(q_ref[0], k_ref[0, pl.ds(ks, tk), :])

        def values(t, slot):
            ks = pl.multiple_of(t * tk, tk)
            acc_ref[...] = acc_ref[...] * a_buf[slot][...] + _dot(p_buf[slot][...], v_ref[0, pl.ds(ks, tk), :])

        def softmax(t, slot, masked):
            s_all = s_buf[slot][...]
            if masked:
                row = lax.broadcasted_iota(jnp.int32, (tq, tk), 0)
                col = lax.broadcasted_iota(jnp.int32, (tq, tk), 1)
                s_all = jnp.where(col <= row, s_all, NEG)
                s_buf[slot][...] = s_all
            m_old = m_ref[...]
            m_new = jnp.maximum(m_old, jnp.max(s_all, axis=1, keepdims=True))
            a_buf[slot][...] = jnp.exp2((m_old - m_new) * EXP2_SCALE)
            m_ref[...] = m_new
            for r0 in range(0, tq, rc):
                s = s_buf[slot][r0:r0 + rc, :]
                p_buf[slot][r0:r0 + rc, :] = jnp.exp2((s - m_new[r0:r0 + rc, :]) * EXP2_SCALE).astype(BF16)

        def last():
            l = acc_ref[:, V_DIM:2 * V_DIM]
            o_ref[...] = acc_ref[:, 0:V_DIM] / l
            lse_ref[0] = (m_ref[...] * SCALE + jnp.log(l)).T[0:1, :]

        m_ref[...] = jnp.full_like(m_ref, NEG)
        acc_ref[...] = jnp.zeros_like(acc_ref)
        _chunk_pipeline(qi, 2, scores, softmax, values, last)

    return pl.pallas_call(
        body, name="attn_fwd", grid=(N_HEADS, T // tq),
        in_specs=[pl.BlockSpec((1, tq, HEAD_PAD), lambda h, i: (h, i, 0)),
                  pl.BlockSpec((1, T, HEAD_PAD), lambda h, i: (h, 0, 0)),
                  pl.BlockSpec((1, T, 2 * V_DIM), lambda h, i: (h, 0, 0))],
        out_specs=[pl.BlockSpec((tq, V_DIM), lambda h, i: (i, h)),
                   pl.BlockSpec((1, 1, tq), lambda h, i: (h, 0, i))],
        out_shape=[jax.ShapeDtypeStruct((T, ATTN_W), F32), jax.ShapeDtypeStruct((N_HEADS, 1, T), F32)],
        scratch_shapes=[pltpu.VMEM((tq, tk), F32)] * 3 + [pltpu.VMEM((tq, tk), BF16)] * 3
                       + [pltpu.VMEM((tq, 1), F32)] * 4 + [pltpu.VMEM((tq, 2 * V_DIM), F32)],
        compiler_params=_params(dimension_semantics=("arbitrary", "arbitrary")),
    )(q, k, v)


def _tail(x, o, proj, p, tgt, g_oa, g_oc, g_pl, conv_w, w_o, w_pl, w_plg, tm):
    T = x.shape[0]
    nt = T // tm

    def body(x_ref, o_ref, za_ref, cb_ref, cc_ref, cx_ref, zc_ref, cch_ref, cxh_ref, p_ref, tgt_ref,
             g_oa_ref, g_oc_ref, g_pl_ref, cw_ref, w_o_ref, w_pl_ref, w_plg_ref,
             dx1_ref, do_ref, delta_ref, dtail_ref, du_ref,
             dw_o_ref, dw_pl_ref, dw_plg_ref, dg_oa_ref, dg_oc_ref, dg_pl_ref, dcw_ref, loss_ref):
        i = pl.program_id(0)

        @pl.when(i == 0)
        def _():
            for r in (dw_o_ref, dw_pl_ref, dw_plg_ref, dg_oa_ref, dg_oc_ref, dg_pl_ref, dcw_ref, loss_ref):
                r[...] = jnp.zeros_like(r)

        g_oa, g_oc, g_pl = g_oa_ref[...], g_oc_ref[...], g_pl_ref[...]
        w0, w1, w2 = cw_ref[0:1, :], cw_ref[1:2, :], cw_ref[2:3, :]

        xv, ov, za, cb, zc = x_ref[...], o_ref[...], za_ref[...], cb_ref[...], zc_ref[...]
        pb = p_ref[...].astype(BF16)
        pp = _dot(pb, w_pl_ref[...])

        sa = _sigmoid(za)
        silu_a = za * sa
        ga = ov * silu_a
        ra = _inv_rms(ga, ATTN_W)
        xa = ga * ra
        ya = (xa * g_oa).astype(BF16)
        x1_a = _dot(ya, w_o_ref[0:ATTN_W, :])
        v = cc_ref[...] * cx_ref[...]
        not_first = jnp.where(i > 0, 1.0, 0.0)
        hv6 = cch_ref[6:7, :] * cxh_ref[6:7, :] * not_first
        hv7 = cch_ref[7:8, :] * cxh_ref[7:8, :] * not_first
        row = lax.broadcasted_iota(jnp.int32, v.shape, 0)
        v1 = jnp.where(row == 0, hv7, pltpu.roll(v, 1, 0))
        v2 = jnp.where(row == 0, hv6, jnp.where(row == 1, hv7, pltpu.roll(v, 2, 0)))
        u = w0 * v2 + w1 * v1 + w2 * v
        sc = _sigmoid(zc)
        silu_c = zc * sc
        gc = cb * u * silu_c
        rc = _inv_rms(gc, CONV_W)
        xc = gc * rc
        yc = (xc * g_oc).astype(BF16)
        x1 = xv + (x1_a + _dot(yc, w_o_ref[ATTN_W:D_MODEL, :]))
        r1 = _inv_rms(x1, D_MODEL)
        xh1 = x1 * r1
        n1 = (xh1 * g_pl).astype(BF16)
        gate = _sigmoid(_dot(n1, w_plg_ref[...]))
        err = x1 + gate * pp - tgt_ref[...]
        loss_ref[...] += 0.5 * jnp.sum(err * err) / D_MODEL
        dy = err / D_MODEL

        dpp = (dy * gate).astype(BF16)
        da = (dy * pp * gate * (1.0 - gate)).astype(BF16)
        dn1 = _dot_nt(da, w_plg_ref[...])
        dw_pl_ref[...] += _dot_tn(pb, dpp)
        dw_plg_ref[...] += _dot_tn(n1, da)
        dg_pl_ref[...] += _colsum(dn1 * xh1)
        dxh = dn1 * g_pl
        dx1 = dy + r1 * (dxh - xh1 * (jnp.sum(dxh * xh1, axis=-1, keepdims=True) / D_MODEL))
        dx1_ref[...] = dx1
        dx1b = dx1.astype(BF16)
        dya = _dot_nt(dx1b, w_o_ref[0:ATTN_W, :])
        dyc = _dot_nt(dx1b, w_o_ref[ATTN_W:D_MODEL, :])

        dw_o_ref[0:ATTN_W, :] += _dot_tn(ya, dx1b)
        dg_oa_ref[...] += _colsum(dya * xa)
        dxa = dya * g_oa
        dga = ra * (dxa - xa * (jnp.sum(dxa * xa, axis=-1, keepdims=True) / ATTN_W))
        do = (dga * silu_a).astype(BF16)
        do_ref[...] = do
        dof = do.astype(F32) * ov
        for hd in range(N_HEADS):
            delta_ref[hd] = _lane_sum(dof[:, hd * V_DIM:(hd + 1) * V_DIM]).T[0:1, :]
        dtail_ref[:, 0:512] = (dga * ov * (sa * (1.0 + za * (1.0 - sa)))).astype(BF16)

        dw_o_ref[ATTN_W:D_MODEL, :] += _dot_tn(yc, dx1b)
        dg_oc_ref[...] += _colsum(dyc * xc)
        dxc = dyc * g_oc
        dgc = rc * (dxc - xc * (jnp.sum(dxc * xc, axis=-1, keepdims=True) / CONV_W))
        dtail_ref[:, 512:1024] = (dgc * u * silu_c).astype(BF16)
        du = dgc * cb * silu_c
        du_ref[...] = du
        dtail_ref[:, 1024:1536] = (dgc * cb * u * (sc * (1.0 + zc * (1.0 - sc)))).astype(BF16)
        dcw_ref[0:1, :] += _colsum(du * v2)
        dcw_ref[1:2, :] += _colsum(du * v1)
        dcw_ref[2:3, :] += _colsum(du * v)

    row = lambda i: (i, 0)
    col = lambda c: (lambda i: (i, c))
    halo = lambda c: (lambda i: (jnp.maximum(i * (tm // 8) - 1, 0), c))
    in_specs = [pl.BlockSpec((tm, D_MODEL), row), pl.BlockSpec((tm, ATTN_W), row)]
    in_specs += [pl.BlockSpec((tm, 512), col(c)) for c in (1, 2, 3, 4, 5)]
    in_specs += [pl.BlockSpec((8, 512), halo(3)), pl.BlockSpec((8, 512), halo(4))]
    in_specs += [pl.BlockSpec((tm, PLE), row), pl.BlockSpec((tm, D_MODEL), row),
                 _full((1, ATTN_W)), _full((1, CONV_W)), _full((1, D_MODEL)), _full((3, CONV_W)),
                 _full((D_MODEL, D_MODEL)), _full((PLE, D_MODEL)), _full((D_MODEL, D_MODEL))]
    out_specs = [pl.BlockSpec((tm, D_MODEL), row), pl.BlockSpec((tm, ATTN_W), row),
                 pl.BlockSpec((N_HEADS, 1, tm), lambda i: (0, 0, i)), pl.BlockSpec((tm, 1536), row),
                 pl.BlockSpec((tm, CONV_W), row),
                 _full((D_MODEL, D_MODEL)), _full((PLE, D_MODEL)), _full((D_MODEL, D_MODEL)),
                 _full((1, ATTN_W)), _full((1, CONV_W)), _full((1, D_MODEL)), _full((3, CONV_W)), _full((1, LANES))]
    out_shape = [jax.ShapeDtypeStruct((T, D_MODEL), F32), jax.ShapeDtypeStruct((T, ATTN_W), BF16),
                 jax.ShapeDtypeStruct((N_HEADS, 1, T), F32), jax.ShapeDtypeStruct((T, 1536), BF16),
                 jax.ShapeDtypeStruct((T, CONV_W), F32),
                 jax.ShapeDtypeStruct((D_MODEL, D_MODEL), F32), jax.ShapeDtypeStruct((PLE, D_MODEL), F32),
                 jax.ShapeDtypeStruct((D_MODEL, D_MODEL), F32),
                 jax.ShapeDtypeStruct((1, ATTN_W), F32), jax.ShapeDtypeStruct((1, CONV_W), F32),
                 jax.ShapeDtypeStruct((1, D_MODEL), F32), jax.ShapeDtypeStruct((3, CONV_W), F32),
                 jax.ShapeDtypeStruct((1, LANES), F32)]
    return pl.pallas_call(
        body, name="tail", grid=(nt,), in_specs=in_specs, out_specs=out_specs, out_shape=out_shape,
        compiler_params=_params(dimension_semantics=("arbitrary",)),
    )(x, o, proj, proj, proj, proj, proj, proj, proj, p, tgt, g_oa, g_oc, g_pl, conv_w, w_o, w_pl, w_plg)


def _attn_bwd(q, k, v, do, lse_row, delta_row, tk, swap):
    T = q.shape[1]
    tq = tk
    nq = T // tq
    rc = min(SOFTMAX_ROWS, tk)
    n_swap = len(swap)

    def body(q_ref, k_ref, v_ref, do_ref, lse_ref, dl_ref, *rest):
        swap_in, (dq_ref, dk_ref, dv_ref), rest = rest[:n_swap], rest[n_swap:n_swap + 3], rest[n_swap + 3:]
        swap_out, (s0, s1, d0, d1, p0, p1, g0, g1, dk_acc, dv_acc), sems = rest[:n_swap], rest[n_swap:n_swap + 10], rest[n_swap + 10:]
        kj = pl.program_id(1)
        s_buf, dp_buf, p_buf, g_buf = (s0, s1), (d0, d1), (p0, p1), (g0, g1)

        if n_swap:
            start, drain = _swap_steps(n_swap, swap_in, swap_out, *sems)
            pl.when(jnp.logical_and(pl.program_id(0) == 0, kj == 0))(start)

        @pl.when(kj == 0)
        def _():
            dq_ref[...] = jnp.zeros_like(dq_ref)

        def q_start(t):
            return pl.multiple_of((nq - 1 - t) * tq, tq)

        def matmuls(t, slot):
            qs = q_start(t)
            s_buf[slot][...] = _dot_nt(k_ref[0], q_ref[0, pl.ds(qs, tq), :])
            dp_buf[slot][...] = _dot_nt(v_ref[0], do_ref[pl.ds(qs, tq), :])

        def pointwise(t, slot, masked):
            qs = q_start(t)
            lse2 = lse_ref[0, :, pl.ds(qs, tq)] * LOG2E
            dl = dl_ref[0, :, pl.ds(qs, tq)]
            for r0 in range(0, tk, rc):
                st = s_buf[slot][r0:r0 + rc, :]
                if masked:
                    row = lax.broadcasted_iota(jnp.int32, (rc, tq), 0)
                    col = lax.broadcasted_iota(jnp.int32, (rc, tq), 1)
                    st = jnp.where(row + r0 <= col, st, NEG)
                pt = jnp.exp2(st * EXP2_SCALE - lse2)
                p_buf[slot][r0:r0 + rc, :] = pt.astype(BF16)
                g_buf[slot][r0:r0 + rc, :] = (pt * (dp_buf[slot][r0:r0 + rc, :] - dl) * SCALE).astype(BF16)

        def accumulate(t, slot):
            qs = q_start(t)
            dv_acc[...] += _dot(p_buf[slot][...], do_ref[pl.ds(qs, tq), :])
            dk_acc[...] += _dot(g_buf[slot][...], q_ref[0, pl.ds(qs, tq), :])
            dq_ref[0, pl.ds(qs, tq), :] += _dot_tn(g_buf[slot][...], k_ref[0])

        def last():
            dk_ref[0] = dk_acc[...]
            dv_ref[0] = dv_acc[...]

        dk_acc[...] = jnp.zeros_like(dk_acc)
        dv_acc[...] = jnp.zeros_like(dv_acc)
        _chunk_pipeline(nq - 1 - kj, 1, matmuls, pointwise, accumulate, last)

        if n_swap:
            pl.when(jnp.logical_and(pl.program_id(0) == N_HEADS - 1, kj == T // tk - 1))(drain)

    outs = pl.pallas_call(
        body, name="attn_bwd", grid=(N_HEADS, T // tk),
        in_specs=[pl.BlockSpec((1, T, HEAD_PAD), lambda h, j: (h, 0, 0)),
                  pl.BlockSpec((1, tk, HEAD_PAD), lambda h, j: (h, j, 0)),
                  pl.BlockSpec((1, tk, V_DIM), lambda h, j: (h, j, 0)),
                  pl.BlockSpec((T, V_DIM), lambda h, j: (0, h)),
                  pl.BlockSpec((1, 1, T), lambda h, j: (h, 0, 0)),
                  pl.BlockSpec((1, 1, T), lambda h, j: (h, 0, 0))] + [_ANY] * n_swap,
        out_specs=[pl.BlockSpec((1, T, HEAD_PAD), lambda h, j: (h, 0, 0)),
                   pl.BlockSpec((1, tk, HEAD_PAD), lambda h, j: (h, j, 0)),
                   pl.BlockSpec((1, tk, V_DIM), lambda h, j: (h, j, 0))] + [_ANY] * n_swap,
        out_shape=[jax.ShapeDtypeStruct((N_HEADS, T, HEAD_PAD), F32), jax.ShapeDtypeStruct((N_HEADS, T, HEAD_PAD), F32),
                   jax.ShapeDtypeStruct((N_HEADS, T, V_DIM), F32)] + _swapped_shapes(swap, []),
        scratch_shapes=[pltpu.VMEM((tk, tq), F32)] * 4 + [pltpu.VMEM((tk, tq), BF16)] * 4
                       + [pltpu.VMEM((tk, HEAD_PAD), F32), pltpu.VMEM((tk, V_DIM), F32)]
                       + ([pltpu.SemaphoreType.DMA((n_swap,))] * 2 if n_swap else []),
        compiler_params=_params(dimension_semantics=("arbitrary", "arbitrary")),
    )(q, k, v, do, lse_row, delta_row, *swap)
    return outs[:3], outs[3:]


def _bwd_proj(x, dx1, pos, proj, dq, dk, dv, dtail, du, g_in, w_in, g_cq, w_uq, g_ckv, w_ukv, gq, gk, conv_w,
              invf, sgn, tm):
    T = x.shape[0]
    nt = T // tm

    ts = min(SUB_TILE, tm)

    def body(x_ref, dx1_ref, pos_ref, lat_ref, cc_ref, cx_ref, dq_ref, dk_ref, dv_ref, dtail_ref, du_ref, dun_ref, *rest):
        consts, (gx_ref, h_ref, dproj_ref), sums = rest[:11], rest[11:14], rest[14:]
        cw_ref = consts[8]
        i = pl.program_id(0)

        @pl.when(i == 0)
        def _():
            for r in sums:
                r[...] = jnp.zeros_like(r)

        du_v = du_ref[...]
        not_last = jnp.where(i < nt - 1, 1.0, 0.0)
        nx0 = dun_ref[0:1, :] * not_last
        nx1 = dun_ref[1:2, :] * not_last
        row = lax.broadcasted_iota(jnp.int32, du_v.shape, 0)
        du1 = jnp.where(row == tm - 1, nx0, pltpu.roll(du_v, tm - 1, 0))
        du2 = jnp.where(row == tm - 2, nx0, jnp.where(row == tm - 1, nx1, pltpu.roll(du_v, tm - 2, 0)))
        dvc = cw_ref[2:3, :] * du_v + cw_ref[1:2, :] * du1 + cw_ref[0:1, :] * du2
        dproj_ref[:, 1536:2048] = (dvc * cx_ref[...]).astype(BF16)
        dproj_ref[:, 2048:2560] = (dvc * cc_ref[...]).astype(BF16)

        for r0 in range(0, tm, ts):
            rows = slice(r0, r0 + ts)
            work(x_ref.at[rows, :], dx1_ref.at[rows, :], pos_ref.at[rows, :], lat_ref.at[rows, :],
                 dq_ref.at[:, rows, :], dk_ref.at[:, rows, :], dv_ref.at[:, rows, :], dtail_ref.at[rows, :], *consts,
                 gx_ref.at[rows, :], h_ref.at[:, rows], dproj_ref.at[rows, :], *sums)

    def work(x_ref, dx1_ref, pos_ref, lat_ref, dq_ref, dk_ref, dv_ref, dtail_ref,
             g_in_ref, w_in_ref, g_cq_ref, w_uq_ref, g_ckv_ref, w_ukv_ref, gq_ref, gk_ref, cw_ref, invf_ref, sgn_ref,
             gx_ref, h_ref, dproj_ref, dw_uq_ref, dw_ukv_ref, dg_in_ref, dg_cq_ref, dg_ckv_ref, dgq_ref, dgk_ref):
        xv = x_ref[...]
        r0 = _rep(_inv_rms_mxu(xv), D_MODEL)
        xh0 = xv * r0
        g_in = g_in_ref[...]
        h_ref[...] = (xh0 * g_in).astype(BF16).T

        c_q = lat_ref[:, 0:Q_LORA]
        rq = _rep(_inv_rms_mxu(c_q), Q_LORA)
        xq = c_q * rq
        g_cq = g_cq_ref[...]
        cqn = (xq * g_cq).astype(BF16)
        c_kv = lat_ref[:, Q_LORA:Q_LORA + KV_LORA]
        rkv = _inv_rms_mxu(c_kv)
        xkv = c_kv * rkv
        g_ckv = g_ckv_ref[...]
        ckvn = (xkv * g_ckv).astype(BF16)
        kpe = lat_ref[:, 384:512]
        kpe_sq = kpe * kpe
        cos_b, sin_b = _rope_tables(pos_ref, invf_ref, sgn_ref)
        gq_a, gq_b = gq_ref[:, 0:NOPE], gq_ref[:, NOPE:HEAD_PAD]
        gk_a, gk_b = gk_ref[:, 0:NOPE], gk_ref[:, NOPE:HEAD_PAD]

        dproj_ref[:, 512:1536] = dtail_ref[:, 0:1024]
        dproj_ref[:, 2560:3072] = dtail_ref[:, 1024:1536]

        def dh_part(c0):
            return _dot_nt(dproj_ref[:, c0:c0 + 512], w_in_ref[:, c0:c0 + 512])

        later_chunks = ((512,), (1024,), (1536, 2048), (2560,))
        dh = jnp.zeros((ts, D_MODEL), F32)
        acc = dict(dh=dh, dkpe=jnp.zeros((ts, LANES), F32), dcqn=jnp.zeros((ts, Q_LORA), F32),
                   dckvn=jnp.zeros((ts, KV_LORA), F32))

        def dh_chunks():
            for chunks in later_chunks:
                for chunk in chunks:
                    acc["dh"] = acc["dh"] + dh_part(chunk)
                    yield

        def queries(hd):
            qh = _dot(cqn, w_uq_ref[hd])
            yield
            a, b = qh[:, 0:NOPE], qh[:, NOPE:HEAD_PAD]
            r = lax.rsqrt(_lane_sum(a * a + b * b) / QK_DIM + EPS)
            yield
            xa, xb = a * r, b * r
            dan = dq_ref[hd, :, 0:NOPE]
            dbr = dq_ref[hd, :, NOPE:HEAD_PAD]
            dbn = dbr * cos_b + _swap_rope_halves(dbr * sin_b)
            yield
            dgq_ref[:, 0:NOPE] += _colsum(dan * xa)
            dgq_ref[:, NOPE:HEAD_PAD] += _colsum(dbn * xb)
            dxa, dxb = dan * gq_a, dbn * gq_b
            cq = _lane_sum(dxa * xa + dxb * xb) / QK_DIM
            yield
            dqh = jnp.concatenate([r * (dxa - xa * cq), r * (dxb - xb * cq)], axis=-1).astype(BF16)
            yield
            dw_uq_ref[hd] += _dot_tn(cqn, dqh)
            yield
            acc["dcqn"] = acc["dcqn"] + _dot_nt(dqh, w_uq_ref[hd])
            yield

        def keys(hd):
            kvh = _dot(ckvn, w_ukv_ref[hd])
            yield
            ka = kvh[:, 0:NOPE]
            rk = lax.rsqrt(_lane_sum(ka * ka + kpe_sq) / QK_DIM + EPS)
            yield
            xka, xkb = ka * rk, kpe * rk
            dkan = dk_ref[hd, :, 0:NOPE]
            dkbr = dk_ref[hd, :, NOPE:HEAD_PAD]
            dkbn = dkbr * cos_b + _swap_rope_halves(dkbr * sin_b)
            yield
            dgk_ref[:, 0:NOPE] += _colsum(dkan * xka)
            dgk_ref[:, NOPE:HEAD_PAD] += _colsum(dkbn * xkb)
            dxka, dxkb = dkan * gk_a, dkbn * gk_b
            ck = _lane_sum(dxka * xka + dxkb * xkb) / QK_DIM
            yield
            acc["dkpe"] = acc["dkpe"] + rk * (dxkb - xkb * ck)
            dkvh = jnp.concatenate([rk * (dxka - xka * ck), dv_ref[hd]], axis=-1).astype(BF16)
            yield
            dw_ukv_ref[hd] += _dot_tn(ckvn, dkvh)
            yield
            acc["dckvn"] = acc["dckvn"] + _dot_nt(dkvh, w_ukv_ref[hd])
            yield

        chains = [dh_chunks()]
        for hd in range(N_HEADS):
            chains += [queries(hd), keys(hd)]
        _round_robin(chains, 5)
        dh, dkpe, dcqn, dckvn = acc["dh"], acc["dkpe"], acc["dcqn"], acc["dckvn"]

        dg_cq_ref[...] += _colsum(dcqn * xq)
        dxq = dcqn * g_cq
        dproj_ref[:, 0:Q_LORA] = (rq * (dxq - xq * _rep(_lane_sum(dxq * xq) / Q_LORA, Q_LORA))).astype(BF16)
        dg_ckv_ref[...] += _colsum(dckvn * xkv)
        dxkv = dckvn * g_ckv
        dproj_ref[:, 256:384] = (rkv * (dxkv - xkv * (_lane_sum(dxkv * xkv) / KV_LORA))).astype(BF16)
        dproj_ref[:, 384:512] = dkpe.astype(BF16)
        dh = dh + dh_part(0)
        dg_in_ref[...] += _colsum(dh * xh0)
        dxh = dh * g_in
        gx_ref[...] = dx1_ref[...] + r0 * (dxh - xh0 * _rep(_lane_sum(dxh * xh0) / D_MODEL, D_MODEL))

    row = lambda i: (i, 0)
    col = lambda c: (lambda i: (i, c))
    head_rows = lambda i: (0, i, 0)
    nxt = lambda i: (jnp.minimum((i + 1) * (tm // 8), T // 8 - 1), 0)
    in_specs = [pl.BlockSpec((tm, D_MODEL), row), pl.BlockSpec((tm, D_MODEL), row), pl.BlockSpec((tm, 1), row),
                pl.BlockSpec((tm, 512), col(0)), pl.BlockSpec((tm, 512), col(3)), pl.BlockSpec((tm, 512), col(4)),
                pl.BlockSpec((N_HEADS, tm, HEAD_PAD), head_rows), pl.BlockSpec((N_HEADS, tm, HEAD_PAD), head_rows),
                pl.BlockSpec((N_HEADS, tm, V_DIM), head_rows), pl.BlockSpec((tm, 1536), row),
                pl.BlockSpec((tm, CONV_W), row), pl.BlockSpec((8, CONV_W), nxt),
                _full((1, D_MODEL)), _full((D_MODEL, PROJ_EXT)), _full((1, Q_LORA)), _full((N_HEADS, Q_LORA, HEAD_PAD)),
                _full((1, KV_LORA)), _full((N_HEADS, KV_LORA, HEAD_PAD)), _full((1, HEAD_PAD)), _full((1, HEAD_PAD)),
                _full((3, CONV_W)), _full((1, LANES)), _full((1, LANES))]
    out_specs = [pl.BlockSpec((tm, D_MODEL), row), pl.BlockSpec((D_MODEL, tm), lambda i: (0, i)),
                 pl.BlockSpec((tm, PROJ_EXT), row),
                 _full((N_HEADS, Q_LORA, HEAD_PAD)), _full((N_HEADS, KV_LORA, HEAD_PAD)),
                 _full((1, D_MODEL)), _full((1, Q_LORA)), _full((1, KV_LORA)), _full((1, HEAD_PAD)), _full((1, HEAD_PAD))]
    out_shape = [jax.ShapeDtypeStruct((T, D_MODEL), F32), jax.ShapeDtypeStruct((D_MODEL, T), BF16),
                 jax.ShapeDtypeStruct((T, PROJ_EXT), BF16),
                 jax.ShapeDtypeStruct((N_HEADS, Q_LORA, HEAD_PAD), F32), jax.ShapeDtypeStruct((N_HEADS, KV_LORA, HEAD_PAD), F32),
                 jax.ShapeDtypeStruct((1, D_MODEL), F32), jax.ShapeDtypeStruct((1, Q_LORA), F32),
                 jax.ShapeDtypeStruct((1, KV_LORA), F32), jax.ShapeDtypeStruct((1, HEAD_PAD), F32),
                 jax.ShapeDtypeStruct((1, HEAD_PAD), F32)]
    return pl.pallas_call(
        body, name="bwd_proj", grid=(nt,), in_specs=in_specs, out_specs=out_specs, out_shape=out_shape,
        compiler_params=_params(dimension_semantics=("arbitrary",)),
    )(x, dx1, pos, proj, proj, proj, dq, dk, dv, dtail, du, du, g_in, w_in, g_cq, w_uq, g_ckv, w_ukv, gq, gk, conv_w,
      invf, sgn)


def _matmul_acc(a, b, tt, tn, parts):
    M, T = a.shape
    N = b.shape[1]
    n = len(parts)
    grid = (N // tn, T // tt)

    def body(a_ref, b_ref, *rest):
        part_refs, o_ref, out_refs, sems = rest[:n], rest[n], rest[n + 1:2 * n + 1], rest[2 * n + 1:]
        j, t = pl.program_id(0), pl.program_id(1)
        if n:
            start, drain = _scatter_steps(part_refs, out_refs, *sems)
            pl.when(jnp.logical_and(j == 0, t == 0))(start)

        @pl.when(t == 0)
        def _():
            o_ref[...] = jnp.zeros_like(o_ref)

        o_ref[...] += _dot(a_ref[...], b_ref[...])
        if n:
            pl.when(jnp.logical_and(j == grid[0] - 1, t == grid[1] - 1))(drain)

    sems = [pltpu.SemaphoreType.DMA((3 * n,)), pltpu.SemaphoreType.DMA((3 * n,)), pltpu.SemaphoreType.DMA((n,))]
    outs = pl.pallas_call(
        body, name="dw_in", grid=grid,
        in_specs=[pl.BlockSpec((M, tt), lambda j, t: (0, t)), pl.BlockSpec((tt, tn), lambda j, t: (t, j))] + [_ANY] * n,
        out_specs=[pl.BlockSpec((M, tn), lambda j, t: (0, j))] + [_ANY] * n,
        out_shape=[jax.ShapeDtypeStruct((M, N), F32)] + _scattered_shapes(parts),
        scratch_shapes=sems if n else [],
        compiler_params=_params(dimension_semantics=("arbitrary", "arbitrary")),
    )(a, b, *parts)
    return outs[0], outs[1:]


def _add_chips(parts, small_parts):
    arrays = list(parts) + [small_parts]

    def body(*refs):
        ins, outs = refs[:len(arrays)], refs[len(arrays):]
        for a_ref, o_ref in zip(ins, outs):
            part = lambda k: a_ref[k].astype(F32)
            o_ref[...] = ((part(0) + part(1)) + part(2)) + part(3)

    in_specs, out_specs, out_shape = [], [], []
    for a in arrays:
        _, rows, cols = a.shape
        in_specs.append(pl.BlockSpec((N_CHIPS, rows // 2, cols), lambda i: (0, i, 0)))
        out_specs.append(pl.BlockSpec((rows // 2, cols), lambda i: (i, 0)))
        out_shape.append(jax.ShapeDtypeStruct((rows, cols), F32))
    outs = pl.pallas_call(body, name="add_chips", grid=(2,), in_specs=in_specs, out_specs=out_specs,
                          out_shape=out_shape, compiler_params=_params(dimension_semantics=("arbitrary",)))(*arrays)
    return outs[:-1], outs[-1]


def _adamw_small(ws, gs, ms, vs):
    n = len(ws)

    def body(*refs):
        for i in range(n):
            w_ref, g_ref, m_ref, v_ref = (refs[k * n + i] for k in range(4))
            d_ref, nm_ref, nv_ref = (refs[(4 + k) * n + i] for k in range(3))
            _adamw_math(g_ref[...], w_ref, m_ref, v_ref, d_ref, nm_ref, nv_ref)

    shapes = [jax.ShapeDtypeStruct(w.shape, F32) for w in ws]
    outs = pl.pallas_call(body, name="adamw_small", out_shape=shapes * 3)(*ws, *gs, *ms, *vs)
    return outs[:n], outs[n:2 * n], outs[2 * n:]


def _adamw_math(gv, w_ref, m_ref, v_ref, d_ref, nm_ref, nv_ref):
    nm = B1 * m_ref[...] + (1.0 - B1) * gv
    nv = B2 * v_ref[...] + (1.0 - B2) * (gv * gv)
    m_hat = nm / (1.0 - B1 ** STEP)
    v_hat = nv / (1.0 - B2 ** STEP)
    d_ref[...] = -LR * (m_hat / (jnp.sqrt(v_hat) + ADAM_EPS) + WD * w_ref[...])
    nm_ref[...] = nm
    nv_ref[...] = nv


def _adamw_halves(w, mine, other, m, v, c, name, transposed):
    hr, cols = mine.shape

    def body(c_ref, w_ref, mine_ref, other_ref, m_ref, v_ref, g_ref, d_ref, nm_ref, nv_ref, *picked):
        gv = jnp.where(pl.program_id(0) == c_ref[0], mine_ref[...], other_ref[...])
        if transposed:
            picked[0][...] = gv
            _store_transposed(picked[0], g_ref)
            gv = g_ref[...]
        else:
            g_ref[...] = gv
        _adamw_math(gv, w_ref, m_ref, v_ref, d_ref, nm_ref, nv_ref)

    if transposed:
        half = pl.BlockSpec((cols, hr), lambda i, c_ref: (0, i))
    else:
        half = pl.BlockSpec((hr, cols), lambda i, c_ref: (i, 0))
    whole = pl.BlockSpec((hr, cols), lambda i, c_ref: (0, 0))
    shp = jax.ShapeDtypeStruct(w.shape, F32)
    return pl.pallas_call(
        body, name=name, out_shape=[shp] * 4,
        grid_spec=pltpu.PrefetchScalarGridSpec(num_scalar_prefetch=1, grid=(2,), in_specs=[half, whole, whole, half, half],
                                               out_specs=[half] * 4,
                                               scratch_shapes=[pltpu.VMEM((hr, cols), F32)] if transposed else []),
        compiler_params=_params(dimension_semantics=("arbitrary",)),
    )(c.reshape(1), w, mine, other, m, v)


_ANY = pl.BlockSpec(memory_space=pl.ANY)


def _mesh_pos():
    return lax.axis_index("x"), lax.axis_index("y"), lax.axis_index("c")


def _other_chips(x, y):
    return [(1 - x, y), (x, 1 - y), (1 - x, 1 - y)]


def _remote(src, dst, send_sems, recv_sems, k, to):
    return pltpu.make_async_remote_copy(src_ref=src, dst_ref=dst, send_sem=send_sems.at[k], recv_sem=recv_sems.at[k],
                                        device_id=to, device_id_type=MESH)


def _gather_weights(shards, n_transposed):
    n = len(shards)
    shapes = [s.shape[::-1] if i < n_transposed else s.shape for i, s in enumerate(shards)]

    def body(*refs):
        start, forward, drain = _gather_steps(shapes, refs[:n], refs[n:2 * n], refs[2 * n:3 * n], *refs[3 * n:])
        start()
        forward()
        drain()

    vmem = pl.BlockSpec(memory_space=pltpu.VMEM)
    return pl.pallas_call(
        body, name="gather_weights", in_specs=[vmem] * n, out_specs=[_ANY] * n,
        out_shape=_gathered_shapes(shapes), scratch_shapes=_gather_scratch(shapes), compiler_params=_params(),
    )(*shards)


def _travel_shape(shape):
    rows, cols = shape
    return (rows, HEAD_PAD if cols == QK_DIM else cols)


def _gathered_shapes(shapes):
    return [jax.ShapeDtypeStruct((N_CHIPS,) + _travel_shape(s), BF16) for s in shapes]


def _gather_scratch(shapes):
    n = len(shapes)
    return ([pltpu.VMEM(_travel_shape(s), BF16) for s in shapes]
            + [pltpu.SemaphoreType.DMA((6 * n,)), pltpu.SemaphoreType.DMA((6 * n,)), pltpu.SemaphoreType.DMA((n,))])


def _gather_steps(shapes, ins, outs, stage, send_sems, recv_sems, local_sems):
    n = len(shapes)
    halved = [s[0] % 32 == 0 for s in shapes]

    def part(i, ref, hc):
        if not halved[i]:
            return ref
        hr = shapes[i][0] // 2
        return ref.at[pl.ds(hc * hr, hr), :]

    def to_chip(i, j, x, y, c):
        cx, cy = _other_chips(x, y)[j]
        return _remote(part(i, stage[i], c), part(i, outs[i].at[2 * x + y], c), send_sems, recv_sems, 6 * i + j, (cx, cy, c))

    def to_sibling(i, j, x, y, c):
        cx, cy = _other_chips(x, y)[j]
        got = part(i, outs[i].at[2 * cx + cy], c)
        return _remote(got, got, send_sems, recv_sems, 6 * i + 3 + j, (x, y, 1 - c))

    def local(i, x, y):
        return pltpu.make_async_copy(stage[i], outs[i].at[2 * x + y], local_sems.at[i])

    def start():
        x, y, c = _mesh_pos()
        for i in range(n):
            cols = shapes[i][1]
            if stage[i].shape[1] != cols:
                stage[i][...] = jnp.zeros_like(stage[i])
            if ins[i].shape == shapes[i]:
                stage[i][:, 0:cols] = ins[i][...].astype(BF16)
            else:
                _store_transposed(ins[i], stage[i])
            local(i, x, y).start()
            for j in range(3):
                to_chip(i, j, x, y, c).start()

    def forward():
        x, y, c = _mesh_pos()
        for i in range(n):
            for j, (cx, cy) in enumerate(_other_chips(x, y)):
                got = part(i, outs[i].at[2 * cx + cy], c)
                _remote(got, got, send_sems, recv_sems, 6 * i + j, (cx, cy, c)).wait_recv()
                if halved[i]:
                    to_sibling(i, j, x, y, c).start()

    def drain():
        x, y, c = _mesh_pos()
        for i in range(n):
            for j, (cx, cy) in enumerate(_other_chips(x, y)):
                if halved[i]:
                    got = part(i, outs[i].at[2 * cx + cy], 1 - c)
                    _remote(got, got, send_sems, recv_sems, 6 * i + 3 + j, (x, y, 1 - c)).wait_recv()
                    to_sibling(i, j, x, y, c).wait_send()
                to_chip(i, j, x, y, c).wait_send()
            local(i, x, y).wait()

    return start, forward, drain


def _swap_halves(grads, whole, name):
    n, m = len(grads), len(grads) + len(whole)

    def body(*refs):
        start, drain = _swap_steps(n, refs[:m], refs[m:2 * m], refs[2 * m], refs[2 * m + 1])
        start()
        drain()

    outs = pl.pallas_call(
        body, name=name, in_specs=[_ANY] * m, out_specs=[_ANY] * m, out_shape=_swapped_shapes(grads, whole),
        scratch_shapes=[pltpu.SemaphoreType.DMA((m,)), pltpu.SemaphoreType.DMA((m,))],
    )(*grads, *whole)
    return outs[:n], outs[n:]


def _swapped_shapes(grads, whole):
    return ([jax.ShapeDtypeStruct((g.shape[0], g.shape[1] // 2, g.shape[2]), F32) for g in grads]
            + [jax.ShapeDtypeStruct(w.shape, F32) for w in whole])


def _swap_steps(n, ins, outs, send_sems, recv_sems):
    def copies():
        x, y, c = _mesh_pos()
        cps = []
        for i, src in enumerate(ins):
            if i < n:
                hr = src.shape[1] // 2
                src = src.at[:, pl.ds((1 - c) * hr, hr), :]
            cps.append(_remote(src, outs[i], send_sems, recv_sems, i, (x, y, 1 - c)))
        return cps

    def start():
        for cp in copies():
            cp.start()

    def drain():
        for cp in copies():
            cp.wait()

    return start, drain


def _scattered_shapes(parts):
    return [jax.ShapeDtypeStruct(p.shape if p.ndim == 3 else (N_CHIPS,) + p.shape, p.dtype) for p in parts]


def _scatter_steps(ins, outs, send_sems, recv_sems, local_sems):
    n = len(ins)

    def src(i, k):
        return ins[i].at[k] if len(ins[i].shape) == 3 else ins[i]

    def sends(x, y, c):
        return [_remote(src(i, 2 * cx + cy), outs[i].at[2 * x + y], send_sems, recv_sems, 3 * i + j, (cx, cy, c))
                for i in range(n) for j, (cx, cy) in enumerate(_other_chips(x, y))]

    def local(i, x, y):
        return pltpu.make_async_copy(src(i, 2 * x + y), outs[i].at[2 * x + y], local_sems.at[i])

    def start():
        x, y, c = _mesh_pos()
        for i in range(n):
            local(i, x, y).start()
        for cp in sends(x, y, c):
            cp.start()

    def drain():
        x, y, c = _mesh_pos()
        for i in range(n):
            for j, (cx, cy) in enumerate(_other_chips(x, y)):
                got = outs[i].at[2 * cx + cy]
                _remote(got, got, send_sems, recv_sems, 3 * i + j, (cx, cy, c)).wait_recv()
        for cp in sends(x, y, c):
            cp.wait_send()
        for i in range(n):
            local(i, x, y).wait()

    return start, drain


def _add_pair(grads, from_sibling, small, small_sibling, c):
    n = len(grads)

    def body(c_ref, *refs):
        ins, outs = refs[:2 * n + 2], refs[2 * n + 2:]
        for i in range(n + 1):
            outs[i][...] = (ins[2 * i][...] + ins[2 * i + 1][...]).astype(outs[i].dtype)

    in_specs, out_specs, out_shape, args = [], [], [], []
    for g, r in zip(grads, from_sibling):
        _, hr, cols = r.shape
        in_specs += [pl.BlockSpec((1, hr, cols), lambda k, c_ref: (k, c_ref[0], 0)),
                     pl.BlockSpec((1, hr, cols), lambda k, c_ref: (k, 0, 0))]
        out_specs.append(pl.BlockSpec((1, hr, cols), lambda k, c_ref: (k, 0, 0)))
        out_shape.append(jax.ShapeDtypeStruct(r.shape, BF16))
        args += [g, r]
    whole = pl.BlockSpec(small.shape, lambda k, c_ref: (0, 0))
    in_specs += [whole, whole]
    out_specs.append(whole)
    out_shape.append(jax.ShapeDtypeStruct(small.shape, F32))
    outs = pl.pallas_call(
        body, name="add_pair", out_shape=out_shape,
        grid_spec=pltpu.PrefetchScalarGridSpec(num_scalar_prefetch=1, grid=(N_CHIPS,), in_specs=in_specs,
                                               out_specs=out_specs),
        compiler_params=_params(dimension_semantics=("arbitrary",)),
    )(c.reshape(1), *args, small, small_sibling)
    return outs[:n], outs[n]


def _scatter_w_in(dw_in_e, from_sibling):
    hr = from_sibling.shape[1]
    shard = (N_CHIPS, hr, SHARD_COLS_IN)

    def body(g_in, r_in, out, g_buf, r_buf, p_buf, load_sems, send_sems, recv_sems, local_sems):
        c = lax.axis_index("c")
        loads = (pltpu.make_async_copy(g_in.at[0, pl.ds(c * hr, hr), :], g_buf, load_sems.at[0]),
                 pltpu.make_async_copy(r_in.at[0], r_buf, load_sems.at[1]))
        for cp in loads:
            cp.start()
        for cp in loads:
            cp.wait()
        g_buf[...] += r_buf[...]
        p_buf[0, :, 0:KPE_END] = g_buf[:, 0:KPE_END].astype(BF16)
        p_buf[0, :, KPE_END:SHARD_COLS_IN] = g_buf[:, KPE_END + KPE_PAD:SHARD_COLS_IN + KPE_PAD].astype(BF16)
        for k in range(1, N_CHIPS):
            p_buf[k] = g_buf[:, SHARD_COLS_IN * k + KPE_PAD:SHARD_COLS_IN * (k + 1) + KPE_PAD].astype(BF16)
        start, drain = _scatter_steps([p_buf], [out], send_sems, recv_sems, local_sems)
        start()
        drain()

    return pl.pallas_call(
        body, name="scatter_grads", in_specs=[_ANY] * 2, out_specs=_ANY, out_shape=jax.ShapeDtypeStruct(shard, BF16),
        scratch_shapes=[pltpu.VMEM((hr, PROJ_EXT), F32)] * 2 + [pltpu.VMEM(shard, BF16)]
                       + [pltpu.SemaphoreType.DMA((2,)), pltpu.SemaphoreType.DMA((3,)), pltpu.SemaphoreType.DMA((3,)),
                          pltpu.SemaphoreType.DMA((1,))],
        compiler_params=_params(),
    )(dw_in_e, from_sibling)


def _share_halves(halves):
    n = len(halves)

    def body(*refs):
        ins, outs, send_sems, recv_sems = refs[:n], refs[n:2 * n], refs[2 * n], refs[2 * n + 1]
        x, y, c = _mesh_pos()
        cps = [_remote(ins[i], outs[i], send_sems, recv_sems, i, (x, y, 1 - c)) for i in range(n)]
        for cp in cps:
            cp.start()
        for cp in cps:
            cp.wait()

    return pl.pallas_call(
        body, name="share_halves", in_specs=[_ANY] * n, out_specs=[_ANY] * n,
        out_shape=[jax.ShapeDtypeStruct(h.shape, h.dtype) for h in halves],
        scratch_shapes=[pltpu.SemaphoreType.DMA((n,)), pltpu.SemaphoreType.DMA((n,))],
    )(*halves)


SHARD_COLS_IN = IN_TOTAL // N_CHIPS
KPE_END = Q_LORA + KV_LORA + ROPE
KPE_PAD = PROJ_EXT - IN_TOTAL


def _by_cols(a):
    return a.transpose(1, 0, 2).reshape(a.shape[1], N_CHIPS * a.shape[2])


def _assemble_early(c_in, c_uq, c_ukv, c_conv):
    return c_in, c_uq, c_ukv, _by_cols(c_conv).astype(F32)


def _assemble_late(c_o, c_pl, c_plg):
    return c_o.reshape(D_MODEL, D_MODEL), _by_cols(c_pl), c_plg.reshape(D_MODEL, D_MODEL)


def _split_late(dw_o, dw_pl, dw_plg):
    chip_major = lambda a: a.reshape(a.shape[0], N_CHIPS, a.shape[1] // N_CHIPS).transpose(1, 0, 2)
    return [dw_o.reshape(N_CHIPS, D_MODEL // N_CHIPS, D_MODEL), chip_major(dw_pl),
            dw_plg.reshape(N_CHIPS, D_MODEL // N_CHIPS, D_MODEL)]


def _local_step(x, p, pos, tgt, gains, early, late_shards, late_gathered, tm, tq):
    c_in, w_uq_e, w_ukv, conv_w = early
    g_in, g_cq, g_ckv, g_q, g_k, g_oa, g_oc, g_pl = gains
    T = x.shape[0]
    zpad = lambda a, n: jnp.concatenate([a, jnp.zeros(a.shape[:-1] + (n,), a.dtype)], axis=-1)
    gq, gk = zpad(g_q, HEAD_PAD - QK_DIM), zpad(g_k, HEAD_PAD - QK_DIM)
    inv_freq = 1.0 / (ROPE_THETA ** (jnp.arange(0, ROPE, 2, dtype=F32) / ROPE))
    invf = jnp.concatenate([inv_freq, inv_freq, jnp.zeros((64,), F32)]).reshape(1, LANES)
    sgn = jnp.concatenate([-jnp.ones((32,), F32), jnp.ones((32,), F32), jnp.zeros((64,), F32)]).reshape(1, LANES)

    (proj, q, k, v, w_in_e), gathered = _fwd_proj(x, pos, g_in, c_in, g_cq, w_uq_e, g_ckv, w_ukv, gq, gk, invf, sgn,
                                                  late_shards, min(2 * tm, T))
    w_o, w_pl, w_plg = _assemble_late(*(gathered if late_shards else late_gathered))
    o, lse = _attn_fwd(q, k, v, tq)
    (dx1, do, delta, dtail, du, dw_o, dw_pl, dw_plg, dg_oa, dg_oc, dg_pl, dconv, loss) = _tail(
        x, o, proj, p, tgt, g_oa, g_oc, g_pl, conv_w, w_o, w_pl, w_plg, tm)
    late_grads = _split_late(dw_o, dw_pl, dw_plg)
    (dq, dk, dv), late_sibling = _attn_bwd(q, k, v, do, lse, delta, tq, late_grads)
    (gx, h, dproj, dw_uq_e, dw_ukv, dg_in, dg_cq, dg_ckv, dgq, dgk) = _bwd_proj(
        x, dx1, pos, proj, dq, dk, dv, dtail, du, g_in, w_in_e, g_cq, w_uq_e, g_ckv, w_ukv, gq, gk, conv_w, invf, sgn, tm)
    wgrads = [dw_uq_e[:, :, :QK_DIM], dw_ukv, *late_grads]
    ggrads = (dg_in, dg_cq, dg_ckv, dgq, dgk, dg_oa, dg_oc, dg_pl)
    return loss, gx, (h, dproj), wgrads, late_sibling, ggrads, dconv


def kernel(x, p, positions, g_in, w_in, g_cq, w_uq, g_ckv, w_ukv, g_q, g_k, conv_w, g_oa, g_oc, w_o, w_pl, w_plg, g_pl, loss_target, m_g_in, m_w_in, m_g_cq, m_w_uq, m_g_ckv, m_w_ukv, m_g_q, m_g_k, m_conv_w, m_g_oa, m_g_oc, m_w_o, m_w_pl, m_w_plg, m_g_pl, v_g_in, v_w_in, v_g_cq, v_w_uq, v_g_ckv, v_w_ukv, v_g_q, v_g_k, v_conv_w, v_g_oa, v_g_oc, v_w_o, v_w_pl, v_w_plg, v_g_pl):
    T = x.shape[1]
    c = lax.axis_index("c")
    chip = 2 * lax.axis_index("x") + lax.axis_index("y")
    gains = [g.reshape(1, -1) for g in (g_in, g_cq, g_ckv, g_q, g_k, g_oa, g_oc, g_pl)]

    transposed = ("w_in", "w_uq")
    early = _assemble_early(*_gather_weights([w_in[0].T, w_uq[0].T, w_ukv[0], conv_w[0]], len(transposed)))

    loss, gx, (h_t, dproj), others_cm, late_sibling, ggrads, dconv = _local_step(
        x[0], p[0, 0], positions.reshape(T, 1), loss_target[0], gains, early, [w_o[0], w_pl[0], w_plg[0]], None, 256, 512)

    small_parts = [a.reshape(-1, LANES) for a in (*ggrads, loss, dconv)]
    small_rows = [a.shape[0] for a in small_parts]
    tile_rows = [-(-r // 8) * 8 for r in small_rows]
    tile_rows[-1] += -sum(tile_rows) % 16
    small = jnp.concatenate([jnp.pad(a, ((0, t - r), (0, 0))) for a, r, t in zip(small_parts, small_rows, tile_rows)])
    n_early = len(others_cm) - len(late_sibling)
    early_sibling, (small_sibling,) = _swap_halves(others_cm[:n_early], [small], "pair_grads")
    chip_parts, chip_small = _add_pair(others_cm, [*early_sibling, *late_sibling], small, small_sibling, c)
    dw_in_e, exchanged = _matmul_acc(h_t, dproj, min(4096, T), 512, [*chip_parts, chip_small])
    dw_in_e = dw_in_e[None]
    (w_in_sibling,), _ = _swap_halves([dw_in_e], [], "pair_w_in")
    by_chip = [_scatter_w_in(dw_in_e, w_in_sibling), *exchanged[:-1]]
    halves, small_total = _add_chips(by_chip, exchanged[-1])
    other_halves = _share_halves(halves)

    gg, off = [], 0
    for rows, tiled in zip(small_rows, tile_rows):
        gg.append(small_total[off:off + rows].reshape(1, -1))
        off += tiled
    loss_out = gg[8][0, 0]
    conv_total = gg[9].reshape(3, CONV_W)
    conv_g = lax.dynamic_slice(conv_total, (0, chip * (CONV_W // N_CHIPS)), (3, CONV_W // N_CHIPS))
    g_by_name = dict(g_in=gg[0], g_cq=gg[1], g_ckv=gg[2], g_q=gg[3][:, :QK_DIM], g_k=gg[4][:, :QK_DIM], conv_w=conv_g,
                     g_oa=gg[5], g_oc=gg[6], g_pl=gg[7])
    half_by_name = dict(zip(("w_in", "w_uq", "w_ukv", "w_o", "w_pl", "w_plg"), zip(halves, other_halves)))
    weights = dict(g_in=g_in, w_in=w_in, g_cq=g_cq, w_uq=w_uq, g_ckv=g_ckv, w_ukv=w_ukv, g_q=g_q, g_k=g_k,
                   conv_w=conv_w, g_oa=g_oa, g_oc=g_oc, w_o=w_o, w_pl=w_pl, w_plg=w_plg, g_pl=g_pl)
    ms = dict(g_in=m_g_in, w_in=m_w_in, g_cq=m_g_cq, w_uq=m_w_uq, g_ckv=m_g_ckv, w_ukv=m_w_ukv, g_q=m_g_q, g_k=m_g_k,
              conv_w=m_conv_w, g_oa=m_g_oa, g_oc=m_g_oc, w_o=m_w_o, w_pl=m_w_pl, w_plg=m_w_plg, g_pl=m_g_pl)
    vs = dict(g_in=v_g_in, w_in=v_w_in, g_cq=v_g_cq, w_uq=v_w_uq, g_ckv=v_g_ckv, w_ukv=v_w_ukv, g_q=v_g_q, g_k=v_g_k,
              conv_w=v_conv_w, g_oa=v_g_oa, g_oc=v_g_oc, w_o=v_w_o, w_pl=v_w_pl, w_plg=v_w_plg, g_pl=v_g_pl)
    names = list(weights)
    flat = lambda a: a.reshape(-1, a.shape[-1])
    small_names = list(g_by_name)
    small_out = _adamw_small([flat(weights[n]) for n in small_names], [flat(g_by_name[n]) for n in small_names],
                             [flat(ms[n]) for n in small_names], [flat(vs[n]) for n in small_names])
    results = {n: (flat(g_by_name[n]), *(out[i] for out in small_out)) for i, n in enumerate(small_names)}
    for n in half_by_name:
        shard = (lambda a: a[0].T) if n in transposed else flat
        out = _adamw_halves(shard(weights[n]), *half_by_name[n], shard(ms[n]), shard(vs[n]), c, "adamw_" + n,
                            n in transposed)
        results[n] = [a.T for a in out] if n in transposed else out
    per_kind = [[results[n][kind].reshape(weights[n].shape) for n in names] for kind in range(4)]
    return (loss_out, gx.reshape(x.shape), *per_kind[0], *per_kind[1], *per_kind[2], *per_kind[3])
```

```python
import math

import jax
import jax.numpy as jnp
from jax import lax
from jax.experimental import pallas as pl
from jax.experimental.pallas import tpu as pltpu

F32 = jnp.float32
BF16 = jnp.bfloat16

D_MODEL = 1024
N_HEADS = 4
NOPE = 128
ROPE = 64
V_DIM = 128
QK_DIM = NOPE + ROPE
HEAD_PAD = 256
Q_LORA = 256
KV_LORA = 128
ATTN_W = 512
CONV_W = 512
PLE = 256
IN_TOTAL = 3008
PROJ_EXT = 3072
ROPE_THETA = 10000.0
EPS = 1e-6
SCALE = 1.0 / math.sqrt(QK_DIM)
LOG2E = math.log2(math.e)
EXP2_SCALE = SCALE * LOG2E
NEG = -1e30
SOFTMAX_ROWS = 32
SUB_TILE = 256

LR, B1, B2, ADAM_EPS, WD, STEP = 0.001, 0.9, 0.999, 1e-08, 0.01, 10

N_CHIPS = 4
LANES = 128
VMEM_LIMIT = 56 * 1024 * 1024
MESH = pl.DeviceIdType.MESH


def _params(**kw):
    return pltpu.CompilerParams(vmem_limit_bytes=VMEM_LIMIT, **kw)


def _inv_rms(x, n):
    return lax.rsqrt(jnp.sum(x * x, axis=-1, keepdims=True) / n + EPS)


def _lane_sum(a):
    folded = a[:, 0:LANES]
    for c0 in range(LANES, a.shape[1], LANES):
        folded = folded + a[:, c0:c0 + LANES]
    head = folded.astype(BF16)
    tail = (folded - head.astype(F32)).astype(BF16)
    return _dot(jnp.concatenate([head, tail], axis=1), jnp.ones((2 * LANES, LANES), BF16))


def _inv_rms_mxu(x):
    return lax.rsqrt(_lane_sum(x * x) / x.shape[1] + EPS)


def _rep(r, width):
    return r if width == LANES else jnp.tile(r, (1, width // LANES))


def _sigmoid(z):
    return jax.nn.sigmoid(z)


def _swap_rope_halves(b):
    lane = lax.broadcasted_iota(jnp.int32, b.shape, 1)
    swapped = jnp.where(lane < 32, pltpu.roll(b, 96, 1), pltpu.roll(b, 32, 1))
    return jnp.where(lane < ROPE, swapped, 0.0)


def _dot(a, b):
    return jnp.dot(a, b, preferred_element_type=F32)


def _dot_nt(a, b):
    return lax.dot_general(a, b, (((1,), (1,)), ((), ())), preferred_element_type=F32)


def _dot_tn(a, b):
    return lax.dot_general(a, b, (((0,), (0,)), ((), ())), preferred_element_type=F32)


def _colsum(a):
    return jnp.sum(a, axis=0, keepdims=True)


def _store_transposed(src_ref, dst_ref):
    r, c = src_ref.shape
    for r0 in range(0, r, LANES):
        h = min(LANES, r - r0)
        for c0 in range(0, c, LANES):
            w = min(LANES, c - c0)
            piece = src_ref[r0:r0 + h, c0 + w - LANES:c0 + w]
            if h < LANES:
                piece = jnp.concatenate([piece, jnp.zeros((LANES - h, LANES), piece.dtype)], axis=0)
            dst_ref[c0:c0 + w, r0:r0 + h] = piece.T[LANES - w:, 0:h].astype(dst_ref.dtype)


def _full(shape):
    return pl.BlockSpec(shape, lambda *_: (0,) * len(shape))


def _round_robin(chains, width):
    waiting, active = list(chains), []
    while waiting or active:
        while waiting and len(active) < width:
            active.append(waiting.pop(0))
        for chain in list(active):
            if next(chain, _DONE) is _DONE:
                active.remove(chain)


_DONE = object()


def _rope_tables(pos_ref, invf_ref, sgn_ref):
    ang = pos_ref[...].astype(F32) * invf_ref[...]
    return jnp.cos(ang), jnp.sin(ang) * sgn_ref[...]


def _fwd_proj(x, pos, g_in, c_in, g_cq, w_uq, g_ckv, w_ukv, gq, gk, invf, sgn, late_shards, tm):
    T = x.shape[0]
    nt = T // tm
    n_late = len(late_shards)
    ts = min(SUB_TILE, tm)

    def body(x_ref, pos_ref, g_in_ref, c_in_ref, g_cq_ref, w_uq_ref, g_ckv_ref, w_ukv_ref, gq_ref, gk_ref,
             invf_ref, sgn_ref, *rest):
        late_in, (proj_ref, q_ref, k_ref, v_ref, w_in_ref) = rest[:n_late], rest[n_late:n_late + 5]
        late_out, late_scratch = rest[n_late + 5:2 * n_late + 5], rest[2 * n_late + 5:]
        i = pl.program_id(0)

        @pl.when(i == 0)
        def _():
            w_in_ref[:, 0:KPE_END] = c_in_ref[0, :, 0:KPE_END]
            w_in_ref[:, KPE_END:KPE_END + KPE_PAD] = jnp.zeros((D_MODEL, KPE_PAD), BF16)
            w_in_ref[:, KPE_END + KPE_PAD:SHARD_COLS_IN + KPE_PAD] = c_in_ref[0, :, KPE_END:SHARD_COLS_IN]
            for chip in range(1, N_CHIPS):
                w_in_ref[:, SHARD_COLS_IN * chip + KPE_PAD:SHARD_COLS_IN * (chip + 1) + KPE_PAD] = c_in_ref[chip]

        if n_late:
            start, forward, drain = _gather_steps([s.shape for s in late_shards], late_in, late_out,
                                                  late_scratch[:n_late], *late_scratch[n_late:])
            pl.when(i == 0)(start)
            pl.when(i == nt // 2)(forward)

        for r0 in range(0, tm, ts):
            rows = slice(r0, r0 + ts)
            xv = x_ref[rows, :]
            h = (xv * _rep(_inv_rms_mxu(xv), D_MODEL) * g_in_ref[...]).astype(BF16)
            lat = _dot(h, w_in_ref[:, 0:512])
            proj_ref[rows, 0:512] = lat
            c_q = lat[:, 0:Q_LORA]
            cqn = (c_q * _rep(_inv_rms_mxu(c_q), Q_LORA) * g_cq_ref[...]).astype(BF16)
            c_kv = lat[:, Q_LORA:Q_LORA + KV_LORA]
            ckvn = (c_kv * _inv_rms_mxu(c_kv) * g_ckv_ref[...]).astype(BF16)
            kpe = lat[:, 384:512]
            kpe_sq = kpe * kpe
            cos_b, sin_b = _rope_tables(pos_ref.at[rows, :], invf_ref, sgn_ref)
            gq_a, gq_b = gq_ref[:, 0:NOPE], gq_ref[:, NOPE:HEAD_PAD]
            gk_a, gk_b = gk_ref[:, 0:NOPE], gk_ref[:, NOPE:HEAD_PAD]

            def projections(rows=rows, h=h):
                for c0 in range(512, PROJ_EXT, 512):
                    proj_ref[rows, c0:c0 + 512] = _dot(h, w_in_ref[:, c0:c0 + 512])
                    yield

            def queries(hd, rows=rows, cqn=cqn, cos_b=cos_b, sin_b=sin_b, gq_a=gq_a, gq_b=gq_b):
                qh = _dot(cqn, w_uq_ref[hd])
                yield
                a, b = qh[:, 0:NOPE], qh[:, NOPE:HEAD_PAD]
                r = lax.rsqrt(_lane_sum(a * a + b * b) / QK_DIM + EPS)
                yield
                bn = b * r * gq_b
                q_ref[hd, rows, 0:NOPE] = (a * r * gq_a).astype(BF16)
                q_ref[hd, rows, NOPE:HEAD_PAD] = (bn * cos_b + _swap_rope_halves(bn) * sin_b).astype(BF16)
                yield

            def keys(hd, rows=rows, ckvn=ckvn, kpe=kpe, kpe_sq=kpe_sq, cos_b=cos_b, sin_b=sin_b, gk_a=gk_a, gk_b=gk_b):
                kvh = _dot(ckvn, w_ukv_ref[hd])
                yield
                ka = kvh[:, 0:NOPE]
                rk = lax.rsqrt(_lane_sum(ka * ka + kpe_sq) / QK_DIM + EPS)
                yield
                kbn = kpe * rk * gk_b
                k_ref[hd, rows, 0:NOPE] = (ka * rk * gk_a).astype(BF16)
                k_ref[hd, rows, NOPE:HEAD_PAD] = (kbn * cos_b + _swap_rope_halves(kbn) * sin_b).astype(BF16)
                v_ref[hd, rows, 0:V_DIM] = kvh[:, NOPE:HEAD_PAD].astype(BF16)
                v_ref[hd, rows, V_DIM:2 * V_DIM] = jnp.ones((ts, V_DIM), BF16)
                yield

            chains = [projections()]
            for hd in range(N_HEADS):
                chains += [queries(hd), keys(hd)]
            _round_robin(chains, 4)

        if n_late:
            pl.when(i == nt - 1)(drain)

    row = lambda i: (i, 0)
    head_rows = lambda i: (0, i, 0)
    outs = pl.pallas_call(
        body, name="fwd_proj", grid=(nt,),
        in_specs=[pl.BlockSpec((tm, D_MODEL), row), pl.BlockSpec((tm, 1), row), _full((1, D_MODEL)),
                  _full((N_CHIPS, D_MODEL, SHARD_COLS_IN)), _full((1, Q_LORA)), _full((N_HEADS, Q_LORA, HEAD_PAD)),
                  _full((1, KV_LORA)), _full((N_HEADS, KV_LORA, HEAD_PAD)), _full((1, HEAD_PAD)), _full((1, HEAD_PAD)),
                  _full((1, LANES)), _full((1, LANES))] + [_full(s.shape) for s in late_shards],
        out_specs=[pl.BlockSpec((tm, PROJ_EXT), row), pl.BlockSpec((N_HEADS, tm, HEAD_PAD), head_rows),
                   pl.BlockSpec((N_HEADS, tm, HEAD_PAD), head_rows), pl.BlockSpec((N_HEADS, tm, 2 * V_DIM), head_rows),
                   _full((D_MODEL, PROJ_EXT))] + [_ANY] * n_late,
        out_shape=[jax.ShapeDtypeStruct((T, PROJ_EXT), F32), jax.ShapeDtypeStruct((N_HEADS, T, HEAD_PAD), BF16),
                   jax.ShapeDtypeStruct((N_HEADS, T, HEAD_PAD), BF16), jax.ShapeDtypeStruct((N_HEADS, T, 2 * V_DIM), BF16),
                   jax.ShapeDtypeStruct((D_MODEL, PROJ_EXT), BF16)] + _gathered_shapes([s.shape for s in late_shards]),
        scratch_shapes=_gather_scratch([s.shape for s in late_shards]) if n_late else [],
        compiler_params=_params(dimension_semantics=("arbitrary",)),
    )(x, pos, g_in, c_in, g_cq, w_uq, g_ckv, w_ukv, gq, gk, invf, sgn, *late_shards)
    return outs[:5], outs[5:]


def _chunk_pipeline(n_loop, lag, matmuls, pointwise, accumulate, last):
    slots = lag + 1

    def iteration(t, slot, pending=True):
        matmuls(jnp.minimum(t + lag, n_loop), (slot + lag) % slots)
        if pending:
            accumulate(t - lag, (slot + 1) % slots)
        pointwise(t, slot, False)

    def finish(slot, pending):
        for back in range(pending, 0, -1):
            accumulate(n_loop - back, (slot - back) % slots)
        pointwise(n_loop, slot, True)
        accumulate(n_loop, slot)
        last()

    for u in range(lag):
        matmuls(jnp.minimum(u, n_loop), u)
    for u in range(lag):
        pl.when(u < n_loop)(lambda u=u: iteration(u, u, pending=False))

    n_main = jnp.maximum(n_loop - lag, 0)

    def unrolled(tt, carry):
        for j in range(slots):
            iteration(lag + slots * tt + j, (lag + j) % slots)
        return carry

    lax.fori_loop(0, n_main // slots, unrolled, 0)
    rest = lax.rem(n_main, slots)
    t0 = n_loop - rest

    for r in range(slots):
        @pl.when(jnp.logical_and(n_loop >= lag, rest == r))
        def _():
            for j in range(r):
                iteration(t0 + j, (lag + j) % slots)
            finish((lag + r) % slots, lag)

    for short in range(lag):
        pl.when(n_loop == short)(lambda short=short: finish(short, short))


def _attn_fwd(q, k, v, tq):
    T = q.shape[1]
    tk = tq
    rc = min(SOFTMAX_ROWS, tq)

    def body(q_ref, k_ref, v_ref, o_ref, lse_ref, s0, s1, s2, p0, p1, p2, a0, a1, a2, m_ref, acc_ref):
        qi = pl.program_id(1)
        s_buf, p_buf, a_buf = (s0, s1, s2), (p0, p1, p2), (a0, a1, a2)

        def scores(t, slot):
            ks = pl.multiple_of(t * tk, tk)
            s_buf[slot][...] = _dot_nt(q_ref[0], k_ref[0, pl.ds(ks, tk), :])

        def values(t, slot):
            ks = pl.multiple_of(t * tk, tk)
            acc_ref[...] = acc_ref[...] * a_buf[slot][...] + _dot(p_buf[slot][...], v_ref[0, pl.ds(ks, tk), :])

        def softmax(t, slot, masked):
            s_all = s_buf[slot][...]
            if masked:
                row = lax.broadcasted_iota(jnp.int32, (tq, tk), 0)
                col = lax.broadcasted_iota(jnp.int32, (tq, tk), 1)
                s_all = jnp.where(col <= row, s_all, NEG)
                s_buf[slot][...] = s_all
            m_old = m_ref[...]
            m_new = jnp.maximum(m_old, jnp.max(s_all, axis=1, keepdims=True))
            a_buf[slot][...] = jnp.exp2((m_old - m_new) * EXP2_SCALE)
            m_ref[...] = m_new
            for r0 in range(0, tq, rc):
                s = s_buf[slot][r0:r0 + rc, :]
                p_buf[slot][r0:r0 + rc, :] = jnp.exp2((s - m_new[r0:r0 + rc, :]) * EXP2_SCALE).astype(BF16)

        def last():
            l = acc_ref[:, V_DIM:2 * V_DIM]
            o_ref[...] = acc_ref[:, 0:V_DIM] / l
            lse_ref[0] = (m_ref[...] * SCALE + jnp.log(l)).T[0:1, :]

        m_ref[...] = jnp.full_like(m_ref, NEG)
        acc_ref[...] = jnp.zeros_like(acc_ref)
        _chunk_pipeline(qi, 2, scores, softmax, values, last)

    return pl.pallas_call(
        body, name="attn_fwd", grid=(N_HEADS, T // tq),
        in_specs=[pl.BlockSpec((1, tq, HEAD_PAD), lambda h, i: (h, i, 0)),
                  pl.BlockSpec((1, T, HEAD_PAD), lambda h, i: (h, 0, 0)),
                  pl.BlockSpec((1, T, 2 * V_DIM), lambda h, i: (h, 0, 0))],
        out_specs=[pl.BlockSpec((tq, V_DIM), lambda h, i: (i, h)),
                   pl.BlockSpec((1, 1, tq), lambda h, i: (h, 0, i))],
        out_shape=[jax.ShapeDtypeStruct((T, ATTN_W), F32), jax.ShapeDtypeStruct((N_HEADS, 1, T), F32)],
        scratch_shapes=[pltpu.VMEM((tq, tk), F32)] * 3 + [pltpu.VMEM((tq, tk), BF16)] * 3
                       + [pltpu.VMEM((tq, 1), F32)] * 4 + [pltpu.VMEM((tq, 2 * V_DIM), F32)],
        compiler_params=_params(dimension_semantics=("arbitrary", "arbitrary")),
    )(q, k, v)


def _tail(x, o, proj, p, tgt, g_oa, g_oc, g_pl, conv_w, w_o, w_pl, w_plg, tm):
    T = x.shape[0]
    nt = T // tm

    def body(x_ref, o_ref, za_ref, cb_ref, cc_ref, cx_ref, zc_ref, cch_ref, cxh_ref, p_ref, tgt_ref,
             g_oa_ref, g_oc_ref, g_pl_ref, cw_ref, w_o_ref, w_pl_ref, w_plg_ref,
             dx1_ref, do_ref, delta_ref, dtail_ref, du_ref,
             dw_o_ref, dw_pl_ref, dw_plg_ref, dg_oa_ref, dg_oc_ref, dg_pl_ref, dcw_ref, loss_ref):
        i = pl.program_id(0)

        @pl.when(i == 0)
        def _():
            for r in (dw_o_ref, dw_pl_ref, dw_plg_ref, dg_oa_ref, dg_oc_ref, dg_pl_ref, dcw_ref, loss_ref):
                r[...] = jnp.zeros_like(r)

        g_oa, g_oc, g_pl = g_oa_ref[...], g_oc_ref[...], g_pl_ref[...]
        w0, w1, w2 = cw_ref[0:1, :], cw_ref[1:2, :], cw_ref[2:3, :]

        xv, ov, za, cb, zc = x_ref[...], o_ref[...], za_ref[...], cb_ref[...], zc_ref[...]
        pb = p_ref[...].astype(BF16)
        pp = _dot(pb, w_pl_ref[...])

        sa = _sigmoid(za)
        silu_a = za * sa
        ga = ov * silu_a
        ra = _inv_rms(ga, ATTN_W)
        xa = ga * ra
        ya = (xa * g_oa).astype(BF16)
        x1_a = _dot(ya, w_o_ref[0:ATTN_W, :])
        v = cc_ref[...] * cx_ref[...]
        not_first = jnp.where(i > 0, 1.0, 0.0)
        hv6 = cch_ref[6:7, :] * cxh_ref[6:7, :] * not_first
        hv7 = cch_ref[7:8, :] * cxh_ref[7:8, :] * not_first
        row = lax.broadcasted_iota(jnp.int32, v.shape, 0)
        v1 = jnp.where(row == 0, hv7, pltpu.roll(v, 1, 0))
        v2 = jnp.where(row == 0, hv6, jnp.where(row == 1, hv7, pltpu.roll(v, 2, 0)))
        u = w0 * v2 + w1 * v1 + w2 * v
        sc = _sigmoid(zc)
        silu_c = zc * sc
        gc = cb * u * silu_c
        rc = _inv_rms(gc, CONV_W)
        xc = gc * rc
        yc = (xc * g_oc).astype(BF16)
        x1 = xv + (x1_a + _dot(yc, w_o_ref[ATTN_W:D_MODEL, :]))
        r1 = _inv_rms(x1, D_MODEL)
        xh1 = x1 * r1
        n1 = (xh1 * g_pl).astype(BF16)
        gate = _sigmoid(_dot(n1, w_plg_ref[...]))
        err = x1 + gate * pp - tgt_ref[...]
        loss_ref[...] += 0.5 * jnp.sum(err * err) / D_MODEL
        dy = err / D_MODEL

        dpp = (dy * gate).astype(BF16)
        da = (dy * pp * gate * (1.0 - gate)).astype(BF16)
        dn1 = _dot_nt(da, w_plg_ref[...])
        dw_pl_ref[...] += _dot_tn(pb, dpp)
        dw_plg_ref[...] += _dot_tn(n1, da)
        dg_pl_ref[...] += _colsum(dn1 * xh1)
        dxh = dn1 * g_pl
        dx1 = dy + r1 * (dxh - xh1 * (jnp.sum(dxh * xh1, axis=-1, keepdims=True) / D_MODEL))
        dx1_ref[...] = dx1
        dx1b = dx1.astype(BF16)
        dya = _dot_nt(dx1b, w_o_ref[0:ATTN_W, :])
        dyc = _dot_nt(dx1b, w_o_ref[ATTN_W:D_MODEL, :])

        dw_o_ref[0:ATTN_W, :] += _dot_tn(ya, dx1b)
        dg_oa_ref[...] += _colsum(dya * xa)
        dxa = dya * g_oa
        dga = ra * (dxa - xa * (jnp.sum(dxa * xa, axis=-1, keepdims=True) / ATTN_W))
        do = (dga * silu_a).astype(BF16)
        do_ref[...] = do
        dof = do.astype(F32) * ov
        for hd in range(N_HEADS):
            delta_ref[hd] = _lane_sum(dof[:, hd * V_DIM:(hd + 1) * V_DIM]).T[0:1, :]
        dtail_ref[:, 0:512] = (dga * ov * (sa * (1.0 + za * (1.0 - sa)))).astype(BF16)

        dw_o_ref[ATTN_W:D_MODEL, :] += _dot_tn(yc, dx1b)
        dg_oc_ref[...] += _colsum(dyc * xc)
        dxc = dyc * g_oc
        dgc = rc * (dxc - xc * (jnp.sum(dxc * xc, axis=-1, keepdims=True) / CONV_W))
        dtail_ref[:, 512:1024] = (dgc * u * silu_c).astype(BF16)
        du = dgc * cb * silu_c
        du_ref[...] = du
        dtail_ref[:, 1024:1536] = (dgc * cb * u * (sc * (1.0 + zc * (1.0 - sc)))).astype(BF16)
        dcw_ref[0:1, :] += _colsum(du * v2)
        dcw_ref[1:2, :] += _colsum(du * v1)
        dcw_ref[2:3, :] += _colsum(du * v)

    row = lambda i: (i, 0)
    col = lambda c: (lambda i: (i, c))
    halo = lambda c: (lambda i: (jnp.maximum(i * (tm // 8) - 1, 0), c))
    in_specs = [pl.BlockSpec((tm, D_MODEL), row), pl.BlockSpec((tm, ATTN_W), row)]
    in_specs += [pl.BlockSpec((tm, 512), col(c)) for c in (1, 2, 3, 4, 5)]
    in_specs += [pl.BlockSpec((8, 512), halo(3)), pl.BlockSpec((8, 512), halo(4))]
    in_specs += [pl.BlockSpec((tm, PLE), row), pl.BlockSpec((tm, D_MODEL), row),
                 _full((1, ATTN_W)), _full((1, CONV_W)), _full((1, D_MODEL)), _full((3, CONV_W)),
                 _full((D_MODEL, D_MODEL)), _full((PLE, D_MODEL)), _full((D_MODEL, D_MODEL))]
    out_specs = [pl.BlockSpec((tm, D_MODEL), row), pl.BlockSpec((tm, ATTN_W), row),
                 pl.BlockSpec((N_HEADS, 1, tm), lambda i: (0, 0, i)), pl.BlockSpec((tm, 1536), row),
                 pl.BlockSpec((tm, CONV_W), row),
                 _full((D_MODEL, D_MODEL)), _full((PLE, D_MODEL)), _full((D_MODEL, D_MODEL)),
                 _full((1, ATTN_W)), _full((1, CONV_W)), _full((1, D_MODEL)), _full((3, CONV_W)), _full((1, LANES))]
    out_shape = [jax.ShapeDtypeStruct((T, D_MODEL), F32), jax.ShapeDtypeStruct((T, ATTN_W), BF16),
                 jax.ShapeDtypeStruct((N_HEADS, 1, T), F32), jax.ShapeDtypeStruct((T, 1536), BF16),
                 jax.ShapeDtypeStruct((T, CONV_W), F32),
                 jax.ShapeDtypeStruct((D_MODEL, D_MODEL), F32), jax.ShapeDtypeStruct((PLE, D_MODEL), F32),
                 jax.ShapeDtypeStruct((D_MODEL, D_MODEL), F32),
                 jax.ShapeDtypeStruct((1, ATTN_W), F32), jax.ShapeDtypeStruct((1, CONV_W), F32),
                 jax.ShapeDtypeStruct((1, D_MODEL), F32), jax.ShapeDtypeStruct((3, CONV_W), F32),
                 jax.ShapeDtypeStruct((1, LANES), F32)]
    return pl.pallas_call(
        body, name="tail", grid=(nt,), in_specs=in_specs, out_specs=out_specs, out_shape=out_shape,
        compiler_params=_params(dimension_semantics=("arbitrary",)),
    )(x, o, proj, proj, proj, proj, proj, proj, proj, p, tgt, g_oa, g_oc, g_pl, conv_w, w_o, w_pl, w_plg)


def _attn_bwd(q, k, v, do, lse_row, delta_row, tk, swap):
    T = q.shape[1]
    tq = tk
    nq = T // tq
    rc = min(SOFTMAX_ROWS, tk)
    n_swap = len(swap)

    def body(q_ref, k_ref, v_ref, do_ref, lse_ref, dl_ref, *rest):
        swap_in, (dq_ref, dk_ref, dv_ref), rest = rest[:n_swap], rest[n_swap:n_swap + 3], rest[n_swap + 3:]
        swap_out, (s0, s1, d0, d1, p0, p1, g0, g1, dk_acc, dv_acc), sems = rest[:n_swap], rest[n_swap:n_swap + 10], rest[n_swap + 10:]
        kj = pl.program_id(1)
        s_buf, dp_buf, p_buf, g_buf = (s0, s1), (d0, d1), (p0, p1), (g0, g1)

        if n_swap:
            start, drain = _swap_steps(n_swap, swap_in, swap_out, *sems)
            pl.when(jnp.logical_and(pl.program_id(0) == 0, kj == 0))(start)

        @pl.when(kj == 0)
        def _():
            dq_ref[...] = jnp.zeros_like(dq_ref)

        def q_start(t):
            return pl.multiple_of((nq - 1 - t) * tq, tq)

        def matmuls(t, slot):
            qs = q_start(t)
            s_buf[slot][...] = _dot_nt(k_ref[0], q_ref[0, pl.ds(qs, tq), :])
            dp_buf[slot][...] = _dot_nt(v_ref[0], do_ref[pl.ds(qs, tq), :])

        def pointwise(t, slot, masked):
            qs = q_start(t)
            lse2 = lse_ref[0, :, pl.ds(qs, tq)] * LOG2E
            dl = dl_ref[0, :, pl.ds(qs, tq)]
            for r0 in range(0, tk, rc):
                st = s_buf[slot][r0:r0 + rc, :]
                if masked:
                    row = lax.broadcasted_iota(jnp.int32, (rc, tq), 0)
                    col = lax.broadcasted_iota(jnp.int32, (rc, tq), 1)
                    st = jnp.where(row + r0 <= col, st, NEG)
                pt = jnp.exp2(st * EXP2_SCALE - lse2)
                p_buf[slot][r0:r0 + rc, :] = pt.astype(BF16)
                g_buf[slot][r0:r0 + rc, :] = (pt * (dp_buf[slot][r0:r0 + rc, :] - dl) * SCALE).astype(BF16)

        def accumulate(t, slot):
            qs = q_start(t)
            dv_acc[...] += _dot(p_buf[slot][...], do_ref[pl.ds(qs, tq), :])
            dk_acc[...] += _dot(g_buf[slot][...], q_ref[0, pl.ds(qs, tq), :])
            dq_ref[0, pl.ds(qs, tq), :] += _dot_tn(g_buf[slot][...], k_ref[0])

        def last():
            dk_ref[0] = dk_acc[...]
            dv_ref[0] = dv_acc[...]

        dk_acc[...] = jnp.zeros_like(dk_acc)
        dv_acc[...] = jnp.zeros_like(dv_acc)
        _chunk_pipeline(nq - 1 - kj, 1, matmuls, pointwise, accumulate, last)

        if n_swap:
            pl.when(jnp.logical_and(pl.program_id(0) == N_HEADS - 1, kj == T // tk - 1))(drain)

    outs = pl.pallas_call(
        body, name="attn_bwd", grid=(N_HEADS, T // tk),
        in_specs=[pl.BlockSpec((1, T, HEAD_PAD), lambda h, j: (h, 0, 0)),
                  pl.BlockSpec((1, tk, HEAD_PAD), lambda h, j: (h, j, 0)),
                  pl.BlockSpec((1, tk, V_DIM), lambda h, j: (h, j, 0)),
                  pl.BlockSpec((T, V_DIM), lambda h, j: (0, h)),
                  pl.BlockSpec((1, 1, T), lambda h, j: (h, 0, 0)),
                  pl.BlockSpec((1, 1, T), lambda h, j: (h, 0, 0))] + [_ANY] * n_swap,
        out_specs=[pl.BlockSpec((1, T, HEAD_PAD), lambda h, j: (h, 0, 0)),
                   pl.BlockSpec((1, tk, HEAD_PAD), lambda h, j: (h, j, 0)),
                   pl.BlockSpec((1, tk, V_DIM), lambda h, j: (h, j, 0))] + [_ANY] * n_swap,
        out_shape=[jax.ShapeDtypeStruct((N_HEADS, T, HEAD_PAD), F32), jax.ShapeDtypeStruct((N_HEADS, T, HEAD_PAD), F32),
                   jax.ShapeDtypeStruct((N_HEADS, T, V_DIM), F32)] + _swapped_shapes(swap, []),
        scratch_shapes=[pltpu.VMEM((tk, tq), F32)] * 4 + [pltpu.VMEM((tk, tq), BF16)] * 4
                       + [pltpu.VMEM((tk, HEAD_PAD), F32), pltpu.VMEM((tk, V_DIM), F32)]
                       + ([pltpu.SemaphoreType.DMA((n_swap,))] * 2 if n_swap else []),
        compiler_params=_params(dimension_semantics=("arbitrary", "arbitrary")),
    )(q, k, v, do, lse_row, delta_row, *swap)
    return outs[:3], outs[3:]


def _bwd_proj(x, dx1, pos, proj, dq, dk, dv, dtail, du, g_in, w_in, g_cq, w_uq, g_ckv, w_ukv, gq, gk, conv_w,
              invf, sgn, tm):
    T = x.shape[0]
    nt = T // tm

    ts = min(SUB_TILE, tm)

    def body(x_ref, dx1_ref, pos_ref, lat_ref, cc_ref, cx_ref, dq_ref, dk_ref, dv_ref, dtail_ref, du_ref, dun_ref, *rest):
        consts, (gx_ref, h_ref, dproj_ref), sums = rest[:11], rest[11:14], rest[14:]
        cw_ref = consts[8]
        i = pl.program_id(0)

        @pl.when(i == 0)
        def _():
            for r in sums:
                r[...] = jnp.zeros_like(r)

        du_v = du_ref[...]
        not_last = jnp.where(i < nt - 1, 1.0, 0.0)
        nx0 = dun_ref[0:1, :] * not_last
        nx1 = dun_ref[1:2, :] * not_last
        row = lax.broadcasted_iota(jnp.int32, du_v.shape, 0)
        du1 = jnp.where(row == tm - 1, nx0, pltpu.roll(du_v, tm - 1, 0))
        du2 = jnp.where(row == tm - 2, nx0, jnp.where(row == tm - 1, nx1, pltpu.roll(du_v, tm - 2, 0)))
        dvc = cw_ref[2:3, :] * du_v + cw_ref[1:2, :] * du1 + cw_ref[0:1, :] * du2
        dproj_ref[:, 1536:2048] = (dvc * cx_ref[...]).astype(BF16)
        dproj_ref[:, 2048:2560] = (dvc * cc_ref[...]).astype(BF16)

        for r0 in range(0, tm, ts):
            rows = slice(r0, r0 + ts)
            work(x_ref.at[rows, :], dx1_ref.at[rows, :], pos_ref.at[rows, :], lat_ref.at[rows, :],
                 dq_ref.at[:, rows, :], dk_ref.at[:, rows, :], dv_ref.at[:, rows, :], dtail_ref.at[rows, :], *consts,
                 gx_ref.at[rows, :], h_ref.at[:, rows], dproj_ref.at[rows, :], *sums)

    def work(x_ref, dx1_ref, pos_ref, lat_ref, dq_ref, dk_ref, dv_ref, dtail_ref,
             g_in_ref, w_in_ref, g_cq_ref, w_uq_ref, g_ckv_ref, w_ukv_ref, gq_ref, gk_ref, cw_ref, invf_ref, sgn_ref,
             gx_ref, h_ref, dproj_ref, dw_uq_ref, dw_ukv_ref, dg_in_ref, dg_cq_ref, dg_ckv_ref, dgq_ref, dgk_ref):
        xv = x_ref[...]
        r0 = _rep(_inv_rms_mxu(xv), D_MODEL)
        xh0 = xv * r0
        g_in = g_in_ref[...]
        h_ref[...] = (xh0 * g_in).astype(BF16).T

        c_q = lat_ref[:, 0:Q_LORA]
        rq = _rep(_inv_rms_mxu(c_q), Q_LORA)
        xq = c_q * rq
        g_cq = g_cq_ref[...]
        cqn = (xq * g_cq).astype(BF16)
        c_kv = lat_ref[:, Q_LORA:Q_LORA + KV_LORA]
        rkv = _inv_rms_mxu(c_kv)
        xkv = c_kv * rkv
        g_ckv = g_ckv_ref[...]
        ckvn = (xkv * g_ckv).astype(BF16)
        kpe = lat_ref[:, 384:512]
        kpe_sq = kpe * kpe
        cos_b, sin_b = _rope_tables(pos_ref, invf_ref, sgn_ref)
        gq_a, gq_b = gq_ref[:, 0:NOPE], gq_ref[:, NOPE:HEAD_PAD]
        gk_a, gk_b = gk_ref[:, 0:NOPE], gk_ref[:, NOPE:HEAD_PAD]

        dproj_ref[:, 512:1536] = dtail_ref[:, 0:1024]
        dproj_ref[:, 2560:3072] = dtail_ref[:, 1024:1536]

        def dh_part(c0):
            return _dot_nt(dproj_ref[:, c0:c0 + 512], w_in_ref[:, c0:c0 + 512])

        later_chunks = ((512,), (1024,), (1536, 2048), (2560,))
        dh = jnp.zeros((ts, D_MODEL), F32)
        acc = dict(dh=dh, dkpe=jnp.zeros((ts, LANES), F32), dcqn=jnp.zeros((ts, Q_LORA), F32),
                   dckvn=jnp.zeros((ts, KV_LORA), F32))

        def dh_chunks():
            for chunks in later_chunks:
                for chunk in chunks:
                    acc["dh"] = acc["dh"] + dh_part(chunk)
                    yield

        def queries(hd):
            qh = _dot(cqn, w_uq_ref[hd])
            yield
            a, b = qh[:, 0:NOPE], qh[:, NOPE:HEAD_PAD]
            r = lax.rsqrt(_lane_sum(a * a + b * b) / QK_DIM + EPS)
            yield
            xa, xb = a * r, b * r
            dan = dq_ref[hd, :, 0:NOPE]
            dbr = dq_ref[hd, :, NOPE:HEAD_PAD]
            dbn = dbr * cos_b + _swap_rope_halves(dbr * sin_b)
            yield
            dgq_ref[:, 0:NOPE] += _colsum(dan * xa)
            dgq_ref[:, NOPE:HEAD_PAD] += _colsum(dbn * xb)
            dxa, dxb = dan * gq_a, dbn * gq_b
            cq = _lane_sum(dxa * xa + dxb * xb) / QK_DIM
            yield
            dqh = jnp.concatenate([r * (dxa - xa * cq), r * (dxb - xb * cq)], axis=-1).astype(BF16)
            yield
            dw_uq_ref[hd] += _dot_tn(cqn, dqh)
            yield
            acc["dcqn"] = acc["dcqn"] + _dot_nt(dqh, w_uq_ref[hd])
            yield

        def keys(hd):
            kvh = _dot(ckvn, w_ukv_ref[hd])
            yield
            ka = kvh[:, 0:NOPE]
            rk = lax.rsqrt(_lane_sum(ka * ka + kpe_sq) / QK_DIM + EPS)
            yield
            xka, xkb = ka * rk, kpe * rk
            dkan = dk_ref[hd, :, 0:NOPE]
            dkbr = dk_ref[hd, :, NOPE:HEAD_PAD]
            dkbn = dkbr * cos_b + _swap_rope_halves(dkbr * sin_b)
            yield
            dgk_ref[:, 0:NOPE] += _colsum(dkan * xka)
            dgk_ref[:, NOPE:HEAD_PAD] += _colsum(dkbn * xkb)
            dxka, dxkb = dkan * gk_a, dkbn * gk_b
            ck = _lane_sum(dxka * xka + dxkb * xkb) / QK_DIM
            yield
            acc["dkpe"] = acc["dkpe"] + rk * (dxkb - xkb * ck)
            dkvh = jnp.concatenate([rk * (dxka - xka * ck), dv_ref[hd]], axis=-1).astype(BF16)
            yield
            dw_ukv_ref[hd] += _dot_tn(ckvn, dkvh)
            yield
            acc["dckvn"] = acc["dckvn"] + _dot_nt(dkvh, w_ukv_ref[hd])
            yield

        chains = [dh_chunks()]
        for hd in range(N_HEADS):
            chains += [queries(hd), keys(hd)]
        _round_robin(chains, 5)
        dh, dkpe, dcqn, dckvn = acc["dh"], acc["dkpe"], acc["dcqn"], acc["dckvn"]

        dg_cq_ref[...] += _colsum(dcqn * xq)
        dxq = dcqn * g_cq
        dproj_ref[:, 0:Q_LORA] = (rq * (dxq - xq * _rep(_lane_sum(dxq * xq) / Q_LORA, Q_LORA))).astype(BF16)
        dg_ckv_ref[...] += _colsum(dckvn * xkv)
        dxkv = dckvn * g_ckv
        dproj_ref[:, 256:384] = (rkv * (dxkv - xkv * (_lane_sum(dxkv * xkv) / KV_LORA))).astype(BF16)
        dproj_ref[:, 384:512] = dkpe.astype(BF16)
        dh = dh + dh_part(0)
        dg_in_ref[...] += _colsum(dh * xh0)
        dxh = dh * g_in
        gx_ref[...] = dx1_ref[...] + r0 * (dxh - xh0 * _rep(_lane_sum(dxh * xh0) / D_MODEL, D_MODEL))

    row = lambda i: (i, 0)
    col = lambda c: (lambda i: (i, c))
    head_rows = lambda i: (0, i, 0)
    nxt = lambda i: (jnp.minimum((i + 1) * (tm // 8), T // 8 - 1), 0)
    in_specs = [pl.BlockSpec((tm, D_MODEL), row), pl.BlockSpec((tm, D_MODEL), row), pl.BlockSpec((tm, 1), row),
                pl.BlockSpec((tm, 512), col(0)), pl.BlockSpec((tm, 512), col(3)), pl.BlockSpec((tm, 512), col(4)),
                pl.BlockSpec((N_HEADS, tm, HEAD_PAD), head_rows), pl.BlockSpec((N_HEADS, tm, HEAD_PAD), head_rows),
                pl.BlockSpec((N_HEADS, tm, V_DIM), head_rows), pl.BlockSpec((tm, 1536), row),
                pl.BlockSpec((tm, CONV_W), row), pl.BlockSpec((8, CONV_W), nxt),
                _full((1, D_MODEL)), _full((D_MODEL, PROJ_EXT)), _full((1, Q_LORA)), _full((N_HEADS, Q_LORA, HEAD_PAD)),
                _full((1, KV_LORA)), _full((N_HEADS, KV_LORA, HEAD_PAD)), _full((1, HEAD_PAD)), _full((1, HEAD_PAD)),
                _full((3, CONV_W)), _full((1, LANES)), _full((1, LANES))]
    out_specs = [pl.BlockSpec((tm, D_MODEL), row), pl.BlockSpec((D_MODEL, tm), lambda i: (0, i)),
                 pl.BlockSpec((tm, PROJ_EXT), row),
                 _full((N_HEADS, Q_LORA, HEAD_PAD)), _full((N_HEADS, KV_LORA, HEAD_PAD)),
                 _full((1, D_MODEL)), _full((1, Q_LORA)), _full((1, KV_LORA)), _full((1, HEAD_PAD)), _full((1, HEAD_PAD))]
    out_shape = [jax.ShapeDtypeStruct((T, D_MODEL), F32), jax.ShapeDtypeStruct((D_MODEL, T), BF16),
                 jax.ShapeDtypeStruct((T, PROJ_EXT), BF16),
                 jax.ShapeDtypeStruct((N_HEADS, Q_LORA, HEAD_PAD), F32), jax.ShapeDtypeStruct((N_HEADS, KV_LORA, HEAD_PAD), F32),
                 jax.ShapeDtypeStruct((1, D_MODEL), F32), jax.ShapeDtypeStruct((1, Q_LORA), F32),
                 jax.ShapeDtypeStruct((1, KV_LORA), F32), jax.ShapeDtypeStruct((1, HEAD_PAD), F32),
                 jax.ShapeDtypeStruct((1, HEAD_PAD), F32)]
    return pl.pallas_call(
        body, name="bwd_proj", grid=(nt,), in_specs=in_specs, out_specs=out_specs, out_shape=out_shape,
        compiler_params=_params(dimension_semantics=("arbitrary",)),
    )(x, dx1, pos, proj, proj, proj, dq, dk, dv, dtail, du, du, g_in, w_in, g_cq, w_uq, g_ckv, w_ukv, gq, gk, conv_w,
      invf, sgn)


def _matmul_acc(a, b, tt, tn, parts):
    M, T = a.shape
    N = b.shape[1]
    n = len(parts)
    grid = (N // tn, T // tt)
    hm = M // 2

    def body(a_ref, b_ref, *rest):
        part_refs, (o_ref, sib_ref), rest = rest[:n], rest[n:n + 2], rest[n + 2:]
        out_refs, (stage_ref, tile_send, tile_recv), sems = rest[:n], rest[n:n + 3], rest[n + 3:]
        j, t = pl.program_id(0), pl.program_id(1)
        if n:
            start, drain = _scatter_steps(part_refs, out_refs, *sems)
            pl.when(jnp.logical_and(j == 0, t == 0))(start)

        def to_sibling(jj):
            x, y, c = _mesh_pos()
            return _remote(stage_ref, sib_ref.at[:, pl.ds(pl.multiple_of(jj * tn, tn), tn)],
                           tile_send, tile_recv, jj, (x, y, 1 - c))

        @pl.when(t == 0)
        def _():
            o_ref[...] = jnp.zeros_like(o_ref)

        o_ref[...] += _dot(a_ref[...], b_ref[...])

        tile_done = t == grid[1] - 1
        pl.when(jnp.logical_and(tile_done, j > 0))(lambda: to_sibling(j - 1).wait())

        @pl.when(tile_done)
        def _():
            c = lax.axis_index("c")
            stage_ref[...] = o_ref[pl.ds(pl.multiple_of((1 - c) * hm, hm), hm), :]
            to_sibling(j).start()

        pl.when(jnp.logical_and(tile_done, j == grid[0] - 1))(lambda: to_sibling(j).wait())
        if n:
            pl.when(jnp.logical_and(j == grid[0] - 1, t == grid[1] - 1))(drain)

    sems = [pltpu.SemaphoreType.DMA((3 * n,)), pltpu.SemaphoreType.DMA((3 * n,)), pltpu.SemaphoreType.DMA((n,))]
    outs = pl.pallas_call(
        body, name="dw_in", grid=grid,
        in_specs=[pl.BlockSpec((M, tt), lambda j, t: (0, t)), pl.BlockSpec((tt, tn), lambda j, t: (t, j))] + [_ANY] * n,
        out_specs=[pl.BlockSpec((M, tn), lambda j, t: (0, j)), _ANY] + [_ANY] * n,
        out_shape=[jax.ShapeDtypeStruct((M, N), F32), jax.ShapeDtypeStruct((hm, N), F32)] + _scattered_shapes(parts),
        scratch_shapes=[pltpu.VMEM((hm, tn), F32)] + [pltpu.SemaphoreType.DMA((grid[0],))] * 2 + (sems if n else []),
        compiler_params=_params(dimension_semantics=("arbitrary", "arbitrary")),
    )(a, b, *parts)
    return outs[0], outs[1], outs[2:]


def _add_chips(parts, small_parts):
    arrays = list(parts) + [small_parts]

    def body(*refs):
        ins, outs = refs[:len(arrays)], refs[len(arrays):]
        for a_ref, o_ref in zip(ins, outs):
            part = lambda k: a_ref[k].astype(F32)
            o_ref[...] = ((part(0) + part(1)) + part(2)) + part(3)

    in_specs, out_specs, out_shape = [], [], []
    for a in arrays:
        _, rows, cols = a.shape
        in_specs.append(pl.BlockSpec((N_CHIPS, rows // 2, cols), lambda i: (0, i, 0)))
        out_specs.append(pl.BlockSpec((rows // 2, cols), lambda i: (i, 0)))
        out_shape.append(jax.ShapeDtypeStruct((rows, cols), F32))
    outs = pl.pallas_call(body, name="add_chips", grid=(2,), in_specs=in_specs, out_specs=out_specs,
                          out_shape=out_shape, compiler_params=_params(dimension_semantics=("arbitrary",)))(*arrays)
    return outs[:-1], outs[-1]


def _adamw_small(ws, gs, ms, vs):
    n = len(ws)

    def body(*refs):
        for i in range(n):
            w_ref, g_ref, m_ref, v_ref = (refs[k * n + i] for k in range(4))
            d_ref, nm_ref, nv_ref = (refs[(4 + k) * n + i] for k in range(3))
            _adamw_math(g_ref[...], w_ref, m_ref, v_ref, d_ref, nm_ref, nv_ref)

    shapes = [jax.ShapeDtypeStruct(w.shape, F32) for w in ws]
    outs = pl.pallas_call(body, name="adamw_small", out_shape=shapes * 3)(*ws, *gs, *ms, *vs)
    return outs[:n], outs[n:2 * n], outs[2 * n:]


def _adamw_math(gv, w_ref, m_ref, v_ref, d_ref, nm_ref, nv_ref):
    nm = B1 * m_ref[...] + (1.0 - B1) * gv
    nv = B2 * v_ref[...] + (1.0 - B2) * (gv * gv)
    m_hat = nm / (1.0 - B1 ** STEP)
    v_hat = nv / (1.0 - B2 ** STEP)
    d_ref[...] = -LR * (m_hat / (jnp.sqrt(v_hat) + ADAM_EPS) + WD * w_ref[...])
    nm_ref[...] = nm
    nv_ref[...] = nv


def _adamw_halves(w, mine, other, m, v, c, name, transposed):
    hr, cols = mine.shape

    def body(c_ref, w_ref, mine_ref, other_ref, m_ref, v_ref, g_ref, d_ref, nm_ref, nv_ref, *picked):
        gv = jnp.where(pl.program_id(0) == c_ref[0], mine_ref[...], other_ref[...])
        if transposed:
            picked[0][...] = gv
            _store_transposed(picked[0], g_ref)
            gv = g_ref[...]
        else:
            g_ref[...] = gv
        _adamw_math(gv, w_ref, m_ref, v_ref, d_ref, nm_ref, nv_ref)

    if transposed:
        half = pl.BlockSpec((cols, hr), lambda i, c_ref: (0, i))
    else:
        half = pl.BlockSpec((hr, cols), lambda i, c_ref: (i, 0))
    whole = pl.BlockSpec((hr, cols), lambda i, c_ref: (0, 0))
    shp = jax.ShapeDtypeStruct(w.shape, F32)
    return pl.pallas_call(
        body, name=name, out_shape=[shp] * 4,
        grid_spec=pltpu.PrefetchScalarGridSpec(num_scalar_prefetch=1, grid=(2,), in_specs=[half, whole, whole, half, half],
                                               out_specs=[half] * 4,
                                               scratch_shapes=[pltpu.VMEM((hr, cols), F32)] if transposed else []),
        compiler_params=_params(dimension_semantics=("arbitrary",)),
    )(c.reshape(1), w, mine, other, m, v)


_ANY = pl.BlockSpec(memory_space=pl.ANY)


def _mesh_pos():
    return lax.axis_index("x"), lax.axis_index("y"), lax.axis_index("c")


def _other_chips(x, y):
    return [(1 - x, y), (x, 1 - y), (1 - x, 1 - y)]


def _remote(src, dst, send_sems, recv_sems, k, to):
    return pltpu.make_async_remote_copy(src_ref=src, dst_ref=dst, send_sem=send_sems.at[k], recv_sem=recv_sems.at[k],
                                        device_id=to, device_id_type=MESH)


def _gather_weights(shards, n_transposed):
    n = len(shards)
    shapes = [s.shape[::-1] if i < n_transposed else s.shape for i, s in enumerate(shards)]

    def body(*refs):
        start, forward, drain = _gather_steps(shapes, refs[:n], refs[n:2 * n], refs[2 * n:3 * n], *refs[3 * n:])
        start()
        forward()
        drain()

    vmem = pl.BlockSpec(memory_space=pltpu.VMEM)
    return pl.pallas_call(
        body, name="gather_weights", in_specs=[vmem] * n, out_specs=[_ANY] * n,
        out_shape=_gathered_shapes(shapes), scratch_shapes=_gather_scratch(shapes), compiler_params=_params(),
    )(*shards)


def _travel_shape(shape):
    rows, cols = shape
    return (rows, HEAD_PAD if cols == QK_DIM else cols)


def _gathered_shapes(shapes):
    return [jax.ShapeDtypeStruct((N_CHIPS,) + _travel_shape(s), BF16) for s in shapes]


def _gather_scratch(shapes):
    n = len(shapes)
    return ([pltpu.VMEM(_travel_shape(s), BF16) for s in shapes]
            + [pltpu.SemaphoreType.DMA((6 * n,)), pltpu.SemaphoreType.DMA((6 * n,)), pltpu.SemaphoreType.DMA((n,))])


def _gather_steps(shapes, ins, outs, stage, send_sems, recv_sems, local_sems):
    n = len(shapes)
    halved = [s[0] % 32 == 0 for s in shapes]

    def part(i, ref, hc):
        if not halved[i]:
            return ref
        hr = shapes[i][0] // 2
        return ref.at[pl.ds(hc * hr, hr), :]

    def to_chip(i, j, x, y, c):
        cx, cy = _other_chips(x, y)[j]
        return _remote(part(i, stage[i], c), part(i, outs[i].at[2 * x + y], c), send_sems, recv_sems, 6 * i + j, (cx, cy, c))

    def to_sibling(i, j, x, y, c):
        cx, cy = _other_chips(x, y)[j]
        got = part(i, outs[i].at[2 * cx + cy], c)
        return _remote(got, got, send_sems, recv_sems, 6 * i + 3 + j, (x, y, 1 - c))

    def local(i, x, y):
        return pltpu.make_async_copy(stage[i], outs[i].at[2 * x + y], local_sems.at[i])

    def start():
        x, y, c = _mesh_pos()
        for i in range(n):
            cols = shapes[i][1]
            if stage[i].shape[1] != cols:
                stage[i][...] = jnp.zeros_like(stage[i])
            if ins[i].shape == shapes[i]:
                stage[i][:, 0:cols] = ins[i][...].astype(BF16)
            else:
                _store_transposed(ins[i], stage[i])
            local(i, x, y).start()
            for j in range(3):
                to_chip(i, j, x, y, c).start()

    def forward():
        x, y, c = _mesh_pos()
        for i in range(n):
            for j, (cx, cy) in enumerate(_other_chips(x, y)):
                got = part(i, outs[i].at[2 * cx + cy], c)
                _remote(got, got, send_sems, recv_sems, 6 * i + j, (cx, cy, c)).wait_recv()
                if halved[i]:
                    to_sibling(i, j, x, y, c).start()

    def drain():
        x, y, c = _mesh_pos()
        for i in range(n):
            for j, (cx, cy) in enumerate(_other_chips(x, y)):
                if halved[i]:
                    got = part(i, outs[i].at[2 * cx + cy], 1 - c)
                    _remote(got, got, send_sems, recv_sems, 6 * i + 3 + j, (x, y, 1 - c)).wait_recv()
                    to_sibling(i, j, x, y, c).wait_send()
                to_chip(i, j, x, y, c).wait_send()
            local(i, x, y).wait()

    return start, forward, drain


def _swap_halves(grads, whole, name):
    n, m = len(grads), len(grads) + len(whole)

    def body(*refs):
        start, drain = _swap_steps(n, refs[:m], refs[m:2 * m], refs[2 * m], refs[2 * m + 1])
        start()
        drain()

    outs = pl.pallas_call(
        body, name=name, in_specs=[_ANY] * m, out_specs=[_ANY] * m, out_shape=_swapped_shapes(grads, whole),
        scratch_shapes=[pltpu.SemaphoreType.DMA((m,)), pltpu.SemaphoreType.DMA((m,))],
    )(*grads, *whole)
    return outs[:n], outs[n:]


def _swapped_shapes(grads, whole):
    return ([jax.ShapeDtypeStruct((g.shape[0], g.shape[1] // 2, g.shape[2]), F32) for g in grads]
            + [jax.ShapeDtypeStruct(w.shape, F32) for w in whole])


def _swap_steps(n, ins, outs, send_sems, recv_sems):
    def copies():
        x, y, c = _mesh_pos()
        cps = []
        for i, src in enumerate(ins):
            if i < n:
                hr = src.shape[1] // 2
                src = src.at[:, pl.ds((1 - c) * hr, hr), :]
            cps.append(_remote(src, outs[i], send_sems, recv_sems, i, (x, y, 1 - c)))
        return cps

    def start():
        for cp in copies():
            cp.start()

    def drain():
        for cp in copies():
            cp.wait()

    return start, drain


def _scattered_shapes(parts):
    return [jax.ShapeDtypeStruct(p.shape if p.ndim == 3 else (N_CHIPS,) + p.shape, p.dtype) for p in parts]


def _scatter_steps(ins, outs, send_sems, recv_sems, local_sems):
    n = len(ins)

    def src(i, k):
        return ins[i].at[k] if len(ins[i].shape) == 3 else ins[i]

    def sends(x, y, c):
        return [_remote(src(i, 2 * cx + cy), outs[i].at[2 * x + y], send_sems, recv_sems, 3 * i + j, (cx, cy, c))
                for i in range(n) for j, (cx, cy) in enumerate(_other_chips(x, y))]

    def local(i, x, y):
        return pltpu.make_async_copy(src(i, 2 * x + y), outs[i].at[2 * x + y], local_sems.at[i])

    def start():
        x, y, c = _mesh_pos()
        for i in range(n):
            local(i, x, y).start()
        for cp in sends(x, y, c):
            cp.start()

    def drain():
        x, y, c = _mesh_pos()
        for i in range(n):
            for j, (cx, cy) in enumerate(_other_chips(x, y)):
                got = outs[i].at[2 * cx + cy]
                _remote(got, got, send_sems, recv_sems, 3 * i + j, (cx, cy, c)).wait_recv()
        for cp in sends(x, y, c):
            cp.wait_send()
        for i in range(n):
            local(i, x, y).wait()

    return start, drain


def _add_pair(grads, from_sibling, small, small_sibling, c):
    n = len(grads)

    def body(c_ref, *refs):
        ins, outs = refs[:2 * n + 2], refs[2 * n + 2:]
        for i in range(n + 1):
            outs[i][...] = (ins[2 * i][...] + ins[2 * i + 1][...]).astype(outs[i].dtype)

    in_specs, out_specs, out_shape, args = [], [], [], []
    for g, r in zip(grads, from_sibling):
        _, hr, cols = r.shape
        in_specs += [pl.BlockSpec((1, hr, cols), lambda k, c_ref: (k, c_ref[0], 0)),
                     pl.BlockSpec((1, hr, cols), lambda k, c_ref: (k, 0, 0))]
        out_specs.append(pl.BlockSpec((1, hr, cols), lambda k, c_ref: (k, 0, 0)))
        out_shape.append(jax.ShapeDtypeStruct(r.shape, BF16))
        args += [g, r]
    whole = pl.BlockSpec(small.shape, lambda k, c_ref: (0, 0))
    in_specs += [whole, whole]
    out_specs.append(whole)
    out_shape.append(jax.ShapeDtypeStruct(small.shape, F32))
    outs = pl.pallas_call(
        body, name="add_pair", out_shape=out_shape,
        grid_spec=pltpu.PrefetchScalarGridSpec(num_scalar_prefetch=1, grid=(N_CHIPS,), in_specs=in_specs,
                                               out_specs=out_specs),
        compiler_params=_params(dimension_semantics=("arbitrary",)),
    )(c.reshape(1), *args, small, small_sibling)
    return outs[:n], outs[n]


def _scatter_w_in(dw_in_e, from_sibling):
    hr = from_sibling.shape[1]
    shard = (N_CHIPS, hr, SHARD_COLS_IN)

    def body(g_in, r_in, out, g_buf, r_buf, p_buf, load_sems, send_sems, recv_sems, local_sems):
        c = lax.axis_index("c")
        loads = (pltpu.make_async_copy(g_in.at[0, pl.ds(c * hr, hr), :], g_buf, load_sems.at[0]),
                 pltpu.make_async_copy(r_in.at[0], r_buf, load_sems.at[1]))
        for cp in loads:
            cp.start()
        for cp in loads:
            cp.wait()
        g_buf[...] += r_buf[...]
        p_buf[0, :, 0:KPE_END] = g_buf[:, 0:KPE_END].astype(BF16)
        p_buf[0, :, KPE_END:SHARD_COLS_IN] = g_buf[:, KPE_END + KPE_PAD:SHARD_COLS_IN + KPE_PAD].astype(BF16)
        for k in range(1, N_CHIPS):
            p_buf[k] = g_buf[:, SHARD_COLS_IN * k + KPE_PAD:SHARD_COLS_IN * (k + 1) + KPE_PAD].astype(BF16)
        start, drain = _scatter_steps([p_buf], [out], send_sems, recv_sems, local_sems)
        start()
        drain()

    return pl.pallas_call(
        body, name="scatter_grads", in_specs=[_ANY] * 2, out_specs=_ANY, out_shape=jax.ShapeDtypeStruct(shard, BF16),
        scratch_shapes=[pltpu.VMEM((hr, PROJ_EXT), F32)] * 2 + [pltpu.VMEM(shard, BF16)]
                       + [pltpu.SemaphoreType.DMA((2,)), pltpu.SemaphoreType.DMA((3,)), pltpu.SemaphoreType.DMA((3,)),
                          pltpu.SemaphoreType.DMA((1,))],
        compiler_params=_params(),
    )(dw_in_e, from_sibling)


def _share_halves(halves):
    n = len(halves)

    def body(*refs):
        ins, outs, send_sems, recv_sems = refs[:n], refs[n:2 * n], refs[2 * n], refs[2 * n + 1]
        x, y, c = _mesh_pos()
        cps = [_remote(ins[i], outs[i], send_sems, recv_sems, i, (x, y, 1 - c)) for i in range(n)]
        for cp in cps:
            cp.start()
        for cp in cps:
            cp.wait()

    return pl.pallas_call(
        body, name="share_halves", in_specs=[_ANY] * n, out_specs=[_ANY] * n,
        out_shape=[jax.ShapeDtypeStruct(h.shape, h.dtype) for h in halves],
        scratch_shapes=[pltpu.SemaphoreType.DMA((n,)), pltpu.SemaphoreType.DMA((n,))],
    )(*halves)


SHARD_COLS_IN = IN_TOTAL // N_CHIPS
KPE_END = Q_LORA + KV_LORA + ROPE
KPE_PAD = PROJ_EXT - IN_TOTAL


def _by_cols(a):
    return a.transpose(1, 0, 2).reshape(a.shape[1], N_CHIPS * a.shape[2])


def _assemble_early(c_in, c_uq, c_ukv, c_conv):
    return c_in, c_uq, c_ukv, _by_cols(c_conv).astype(F32)


def _assemble_late(c_o, c_pl, c_plg):
    return c_o.reshape(D_MODEL, D_MODEL), _by_cols(c_pl), c_plg.reshape(D_MODEL, D_MODEL)


def _split_late(dw_o, dw_pl, dw_plg):
    chip_major = lambda a: a.reshape(a.shape[0], N_CHIPS, a.shape[1] // N_CHIPS).transpose(1, 0, 2)
    return [dw_o.reshape(N_CHIPS, D_MODEL // N_CHIPS, D_MODEL), chip_major(dw_pl),
            dw_plg.reshape(N_CHIPS, D_MODEL // N_CHIPS, D_MODEL)]


def _local_step(x, p, pos, tgt, gains, early, late_shards, late_gathered, tm, tq):
    c_in, w_uq_e, w_ukv, conv_w = early
    g_in, g_cq, g_ckv, g_q, g_k, g_oa, g_oc, g_pl = gains
    T = x.shape[0]
    zpad = lambda a, n: jnp.concatenate([a, jnp.zeros(a.shape[:-1] + (n,), a.dtype)], axis=-1)
    gq, gk = zpad(g_q, HEAD_PAD - QK_DIM), zpad(g_k, HEAD_PAD - QK_DIM)
    inv_freq = 1.0 / (ROPE_THETA ** (jnp.arange(0, ROPE, 2, dtype=F32) / ROPE))
    invf = jnp.concatenate([inv_freq, inv_freq, jnp.zeros((64,), F32)]).reshape(1, LANES)
    sgn = jnp.concatenate([-jnp.ones((32,), F32), jnp.ones((32,), F32), jnp.zeros((64,), F32)]).reshape(1, LANES)

    (proj, q, k, v, w_in_e), gathered = _fwd_proj(x, pos, g_in, c_in, g_cq, w_uq_e, g_ckv, w_ukv, gq, gk, invf, sgn,
                                                  late_shards, min(2 * tm, T))
    w_o, w_pl, w_plg = _assemble_late(*(gathered if late_shards else late_gathered))
    o, lse = _attn_fwd(q, k, v, tq)
    (dx1, do, delta, dtail, du, dw_o, dw_pl, dw_plg, dg_oa, dg_oc, dg_pl, dconv, loss) = _tail(
        x, o, proj, p, tgt, g_oa, g_oc, g_pl, conv_w, w_o, w_pl, w_plg, tm)
    late_grads = _split_late(dw_o, dw_pl, dw_plg)
    (dq, dk, dv), late_sibling = _attn_bwd(q, k, v, do, lse, delta, tq, late_grads)
    (gx, h, dproj, dw_uq_e, dw_ukv, dg_in, dg_cq, dg_ckv, dgq, dgk) = _bwd_proj(
        x, dx1, pos, proj, dq, dk, dv, dtail, du, g_in, w_in_e, g_cq, w_uq_e, g_ckv, w_ukv, gq, gk, conv_w, invf, sgn, tm)
    wgrads = [dw_uq_e[:, :, :QK_DIM], dw_ukv, *late_grads]
    ggrads = (dg_in, dg_cq, dg_ckv, dgq, dgk, dg_oa, dg_oc, dg_pl)
    return loss, gx, (h, dproj), wgrads, late_sibling, ggrads, dconv


def kernel(x, p, positions, g_in, w_in, g_cq, w_uq, g_ckv, w_ukv, g_q, g_k, conv_w, g_oa, g_oc, w_o, w_pl, w_plg, g_pl, loss_target, m_g_in, m_w_in, m_g_cq, m_w_uq, m_g_ckv, m_w_ukv, m_g_q, m_g_k, m_conv_w, m_g_oa, m_g_oc, m_w_o, m_w_pl, m_w_plg, m_g_pl, v_g_in, v_w_in, v_g_cq, v_w_uq, v_g_ckv, v_w_ukv, v_g_q, v_g_k, v_conv_w, v_g_oa, v_g_oc, v_w_o, v_w_pl, v_w_plg, v_g_pl):
    T = x.shape[1]
    c = lax.axis_index("c")
    chip = 2 * lax.axis_index("x") + lax.axis_index("y")
    gains = [g.reshape(1, -1) for g in (g_in, g_cq, g_ckv, g_q, g_k, g_oa, g_oc, g_pl)]

    transposed = ("w_in", "w_uq")
    early = _assemble_early(*_gather_weights([w_in[0].T, w_uq[0].T, w_ukv[0], conv_w[0]], len(transposed)))

    loss, gx, (h_t, dproj), others_cm, late_sibling, ggrads, dconv = _local_step(
        x[0], p[0, 0], positions.reshape(T, 1), loss_target[0], gains, early, [w_o[0], w_pl[0], w_plg[0]], None, 256, 512)

    small_parts = [a.reshape(-1, LANES) for a in (*ggrads, loss, dconv)]
    small_rows = [a.shape[0] for a in small_parts]
    tile_rows = [-(-r // 8) * 8 for r in small_rows]
    tile_rows[-1] += -sum(tile_rows) % 16
    small = jnp.concatenate([jnp.pad(a, ((0, t - r), (0, 0))) for a, r, t in zip(small_parts, small_rows, tile_rows)])
    n_early = len(others_cm) - len(late_sibling)
    early_sibling, (small_sibling,) = _swap_halves(others_cm[:n_early], [small], "pair_grads")
    chip_parts, chip_small = _add_pair(others_cm, [*early_sibling, *late_sibling], small, small_sibling, c)
    dw_in_e, w_in_sibling, exchanged = _matmul_acc(h_t, dproj, min(4096, T), 512, [*chip_parts, chip_small])
    by_chip = [_scatter_w_in(dw_in_e[None], w_in_sibling[None]), *exchanged[:-1]]
    halves, small_total = _add_chips(by_chip, exchanged[-1])
    other_halves = _share_halves(halves)

    gg, off = [], 0
    for rows, tiled in zip(small_rows, tile_rows):
        gg.append(small_total[off:off + rows].reshape(1, -1))
        off += tiled
    loss_out = gg[8][0, 0]
    conv_total = gg[9].reshape(3, CONV_W)
    conv_g = lax.dynamic_slice(conv_total, (0, chip * (CONV_W // N_CHIPS)), (3, CONV_W // N_CHIPS))
    g_by_name = dict(g_in=gg[0], g_cq=gg[1], g_ckv=gg[2], g_q=gg[3][:, :QK_DIM], g_k=gg[4][:, :QK_DIM], conv_w=conv_g,
                     g_oa=gg[5], g_oc=gg[6], g_pl=gg[7])
    half_by_name = dict(zip(("w_in", "w_uq", "w_ukv", "w_o", "w_pl", "w_plg"), zip(halves, other_halves)))
    weights = dict(g_in=g_in, w_in=w_in, g_cq=g_cq, w_uq=w_uq, g_ckv=g_ckv, w_ukv=w_ukv, g_q=g_q, g_k=g_k,
                   conv_w=conv_w, g_oa=g_oa, g_oc=g_oc, w_o=w_o, w_pl=w_pl, w_plg=w_plg, g_pl=g_pl)
    ms = dict(g_in=m_g_in, w_in=m_w_in, g_cq=m_g_cq, w_uq=m_w_uq, g_ckv=m_g_ckv, w_ukv=m_w_ukv, g_q=m_g_q, g_k=m_g_k,
              conv_w=m_conv_w, g_oa=m_g_oa, g_oc=m_g_oc, w_o=m_w_o, w_pl=m_w_pl, w_plg=m_w_plg, g_pl=m_g_pl)
    vs = dict(g_in=v_g_in, w_in=v_w_in, g_cq=v_g_cq, w_uq=v_w_uq, g_ckv=v_g_ckv, w_ukv=v_w_ukv, g_q=v_g_q, g_k=v_g_k,
              conv_w=v_conv_w, g_oa=v_g_oa, g_oc=v_g_oc, w_o=v_w_o, w_pl=v_w_pl, w_plg=v_w_plg, g_pl=v_g_pl)
    names = list(weights)
    flat = lambda a: a.reshape(-1, a.shape[-1])
    small_names = list(g_by_name)
    small_out = _adamw_small([flat(weights[n]) for n in small_names], [flat(g_by_name[n]) for n in small_names],
                             [flat(ms[n]) for n in small_names], [flat(vs[n]) for n in small_names])
    results = {n: (flat(g_by_name[n]), *(out[i] for out in small_out)) for i, n in enumerate(small_names)}
    for n in half_by_name:
        shard = (lambda a: a[0].T) if n in transposed else flat
        out = _adamw_halves(shard(weights[n]), *half_by_name[n], shard(ms[n]), shard(vs[n]), c, "adamw_" + n,
                            n in transposed)
        results[n] = [a.T for a in out] if n in transposed else out
    per_kind = [[results[n][kind].reshape(weights[n].shape) for n in names] for kind in range(4)]
    return (loss_out, gx.reshape(x.shape), *per_kind[0], *per_kind[1], *per_kind[2], *per_kind[3])
```

```python
import math

import jax
import jax.numpy as jnp
from jax import lax
from jax.experimental import pallas as pl
from jax.experimental.pallas import tpu as pltpu

F32 = jnp.float32
BF16 = jnp.bfloat16

D_MODEL = 1024
N_HEADS = 4
NOPE = 128
ROPE = 64
V_DIM = 128
QK_DIM = NOPE + ROPE
HEAD_PAD = 256
Q_LORA = 256
KV_LORA = 128
ATTN_W = 512
CONV_W = 512
PLE = 256
IN_TOTAL = 3008
PROJ_EXT = 3072
ROPE_THETA = 10000.0
EPS = 1e-6
SCALE = 1.0 / math.sqrt(QK_DIM)
LOG2E = math.log2(math.e)
EXP2_SCALE = SCALE * LOG2E
NEG = -1e30
SOFTMAX_ROWS = 32
SUB_TILE = 256

LR, B1, B2, ADAM_EPS, WD, STEP = 0.001, 0.9, 0.999, 1e-08, 0.01, 10

N_CHIPS = 4
LANES = 128
VMEM_LIMIT = 56 * 1024 * 1024
MESH = pl.DeviceIdType.MESH


def _params(**kw):
    return pltpu.CompilerParams(vmem_limit_bytes=VMEM_LIMIT, **kw)


def _inv_rms(x, n):
    return lax.rsqrt(jnp.sum(x * x, axis=-1, keepdims=True) / n + EPS)


def _lane_sum(a):
    folded = a[:, 0:LANES]
    for c0 in range(LANES, a.shape[1], LANES):
        folded = folded + a[:, c0:c0 + LANES]
    head = folded.astype(BF16)
    tail = (folded - head.astype(F32)).astype(BF16)
    return _dot(jnp.concatenate([head, tail], axis=1), jnp.ones((2 * LANES, LANES), BF16))


def _inv_rms_mxu(x):
    return lax.rsqrt(_lane_sum(x * x) / x.shape[1] + EPS)


def _rep(r, width):
    return r if width == LANES else jnp.tile(r, (1, width // LANES))


def _sigmoid(z):
    return jax.nn.sigmoid(z)


def _swap_rope_halves(b):
    lane = lax.broadcasted_iota(jnp.int32, b.shape, 1)
    swapped = jnp.where(lane < 32, pltpu.roll(b, 96, 1), pltpu.roll(b, 32, 1))
    return jnp.where(lane < ROPE, swapped, 0.0)


def _dot(a, b):
    return jnp.dot(a, b, preferred_element_type=F32)


def _dot_nt(a, b):
    return lax.dot_general(a, b, (((1,), (1,)), ((), ())), preferred_element_type=F32)


def _dot_tn(a, b):
    return lax.dot_general(a, b, (((0,), (0,)), ((), ())), preferred_element_type=F32)


def _colsum(a):
    return jnp.sum(a, axis=0, keepdims=True)


def _store_transposed(src_ref, dst_ref):
    r, c = src_ref.shape
    for r0 in range(0, r, LANES):
        h = min(LANES, r - r0)
        for c0 in range(0, c, LANES):
            w = min(LANES, c - c0)
            piece = src_ref[r0:r0 + h, c0 + w - LANES:c0 + w]
            if h < LANES:
                piece = jnp.concatenate([piece, jnp.zeros((LANES - h, LANES), piece.dtype)], axis=0)
            dst_ref[c0:c0 + w, r0:r0 + h] = piece.T[LANES - w:, 0:h].astype(dst_ref.dtype)


def _full(shape):
    return pl.BlockSpec(shape, lambda *_: (0,) * len(shape))


def _round_robin(chains, width):
    waiting, active = list(chains), []
    while waiting or active:
        while waiting and len(active) < width:
            active.append(waiting.pop(0))
        for chain in list(active):
            if next(chain, _DONE) is _DONE:
                active.remove(chain)


_DONE = object()


def _rope_tables(pos_ref, invf_ref, sgn_ref):
    ang = pos_ref[...].astype(F32) * invf_ref[...]
    return jnp.cos(ang), jnp.sin(ang) * sgn_ref[...]


def _fwd_proj(x, pos, g_in, c_in, g_cq, w_uq, g_ckv, w_ukv, gq, gk, invf, sgn, late_shards, tm):
    T = x.shape[0]
    nt = T // tm
    n_late = len(late_shards)
    ts = min(SUB_TILE, tm)

    def body(x_ref, pos_ref, g_in_ref, c_in_ref, g_cq_ref, w_uq_ref, g_ckv_ref, w_ukv_ref, gq_ref, gk_ref,
             invf_ref, sgn_ref, *rest):
        late_in, (proj_ref, q_ref, k_ref, v_ref, w_in_ref) = rest[:n_late], rest[n_late:n_late + 5]
        late_out, late_scratch = rest[n_late + 5:2 * n_late + 5], rest[2 * n_late + 5:]
        i = pl.program_id(0)

        @pl.when(i == 0)
        def _():
            w_in_ref[:, 0:KPE_END] = c_in_ref[0, :, 0:KPE_END]
            w_in_ref[:, KPE_END:KPE_END + KPE_PAD] = jnp.zeros((D_MODEL, KPE_PAD), BF16)
            w_in_ref[:, KPE_END + KPE_PAD:SHARD_COLS_IN + KPE_PAD] = c_in_ref[0, :, KPE_END:SHARD_COLS_IN]
            for chip in range(1, N_CHIPS):
                w_in_ref[:, SHARD_COLS_IN * chip + KPE_PAD:SHARD_COLS_IN * (chip + 1) + KPE_PAD] = c_in_ref[chip]

        if n_late:
            start, forward, drain = _gather_steps([s.shape for s in late_shards], late_in, late_out,
                                                  late_scratch[:n_late], *late_scratch[n_late:])
            pl.when(i == 0)(start)
            pl.when(i == nt // 2)(forward)

        for r0 in range(0, tm, ts):
            rows = slice(r0, r0 + ts)
            xv = x_ref[rows, :]
            h = (xv * _rep(_inv_rms_mxu(xv), D_MODEL) * g_in_ref[...]).astype(BF16)
            lat = _dot(h, w_in_ref[:, 0:512])
            proj_ref[rows, 0:512] = lat
            c_q = lat[:, 0:Q_LORA]
            cqn = (c_q * _rep(_inv_rms_mxu(c_q), Q_LORA) * g_cq_ref[...]).astype(BF16)
            c_kv = lat[:, Q_LORA:Q_LORA + KV_LORA]
            ckvn = (c_kv * _inv_rms_mxu(c_kv) * g_ckv_ref[...]).astype(BF16)
            kpe = lat[:, 384:512]
            kpe_sq = kpe * kpe
            cos_b, sin_b = _rope_tables(pos_ref.at[rows, :], invf_ref, sgn_ref)
            gq_a, gq_b = gq_ref[:, 0:NOPE], gq_ref[:, NOPE:HEAD_PAD]
            gk_a, gk_b = gk_ref[:, 0:NOPE], gk_ref[:, NOPE:HEAD_PAD]

            def projections(rows=rows, h=h):
                for c0 in range(512, PROJ_EXT, 512):
                    proj_ref[rows, c0:c0 + 512] = _dot(h, w_in_ref[:, c0:c0 + 512])
                    yield

            def queries(hd, rows=rows, cqn=cqn, cos_b=cos_b, sin_b=sin_b, gq_a=gq_a, gq_b=gq_b):
                qh = _dot(cqn, w_uq_ref[hd])
                yield
                a, b = qh[:, 0:NOPE], qh[:, NOPE:HEAD_PAD]
                r = lax.rsqrt(_lane_sum(a * a + b * b) / QK_DIM + EPS)
                yield
                bn = b * r * gq_b
                q_ref[hd, rows, 0:NOPE] = (a * r * gq_a).astype(BF16)
                q_ref[hd, rows, NOPE:HEAD_PAD] = (bn * cos_b + _swap_rope_halves(bn) * sin_b).astype(BF16)
                yield

            def keys(hd, rows=rows, ckvn=ckvn, kpe=kpe, kpe_sq=kpe_sq, cos_b=cos_b, sin_b=sin_b, gk_a=gk_a, gk_b=gk_b):
                kvh = _dot(ckvn, w_ukv_ref[hd])
                yield
                ka = kvh[:, 0:NOPE]
                rk = lax.rsqrt(_lane_sum(ka * ka + kpe_sq) / QK_DIM + EPS)
                yield
                kbn = kpe * rk * gk_b
                k_ref[hd, rows, 0:NOPE] = (ka * rk * gk_a).astype(BF16)
                k_ref[hd, rows, NOPE:HEAD_PAD] = (kbn * cos_b + _swap_rope_halves(kbn) * sin_b).astype(BF16)
                v_ref[hd, rows, 0:V_DIM] = kvh[:, NOPE:HEAD_PAD].astype(BF16)
                v_ref[hd, rows, V_DIM:2 * V_DIM] = jnp.ones((ts, V_DIM), BF16)
                yield

            chains = [projections()]
            for hd in range(N_HEADS):
                chains += [queries(hd), keys(hd)]
            _round_robin(chains, 4)

        if n_late:
            pl.when(i == nt - 1)(drain)

    row = lambda i: (i, 0)
    head_rows = lambda i: (0, i, 0)
    outs = pl.pallas_call(
        body, name="fwd_proj", grid=(nt,),
        in_specs=[pl.BlockSpec((tm, D_MODEL), row), pl.BlockSpec((tm, 1), row), _full((1, D_MODEL)),
                  _full((N_CHIPS, D_MODEL, SHARD_COLS_IN)), _full((1, Q_LORA)), _full((N_HEADS, Q_LORA, HEAD_PAD)),
                  _full((1, KV_LORA)), _full((N_HEADS, KV_LORA, HEAD_PAD)), _full((1, HEAD_PAD)), _full((1, HEAD_PAD)),
                  _full((1, LANES)), _full((1, LANES))] + [_full(s.shape) for s in late_shards],
        out_specs=[pl.BlockSpec((tm, PROJ_EXT), row), pl.BlockSpec((N_HEADS, tm, HEAD_PAD), head_rows),
                   pl.BlockSpec((N_HEADS, tm, HEAD_PAD), head_rows), pl.BlockSpec((N_HEADS, tm, 2 * V_DIM), head_rows),
                   _full((D_MODEL, PROJ_EXT))] + [_ANY] * n_late,
        out_shape=[jax.ShapeDtypeStruct((T, PROJ_EXT), F32), jax.ShapeDtypeStruct((N_HEADS, T, HEAD_PAD), BF16),
                   jax.ShapeDtypeStruct((N_HEADS, T, HEAD_PAD), BF16), jax.ShapeDtypeStruct((N_HEADS, T, 2 * V_DIM), BF16),
                   jax.ShapeDtypeStruct((D_MODEL, PROJ_EXT), BF16)] + _gathered_shapes([s.shape for s in late_shards]),
        scratch_shapes=_gather_scratch([s.shape for s in late_shards]) if n_late else [],
        compiler_params=_params(dimension_semantics=("arbitrary",)),
    )(x, pos, g_in, c_in, g_cq, w_uq, g_ckv, w_ukv, gq, gk, invf, sgn, *late_shards)
    return outs[:5], outs[5:]


def _chunk_pipeline(n_loop, lag, matmuls, pointwise, accumulate, last):
    slots = lag + 1

    def iteration(t, slot, pending=True):
        matmuls(jnp.minimum(t + lag, n_loop), (slot + lag) % slots)
        if pending:
            accumulate(t - lag, (slot + 1) % slots, False)
        pointwise(t, slot, False)

    def finish(slot, pending):
        for back in range(pending, 0, -1):
            accumulate(n_loop - back, (slot - back) % slots, False)
        pointwise(n_loop, slot, True)
        accumulate(n_loop, slot, True)
        last()

    for u in range(lag):
        matmuls(jnp.minimum(u, n_loop), u)
    for u in range(lag):
        pl.when(u < n_loop)(lambda u=u: iteration(u, u, pending=False))

    n_main = jnp.maximum(n_loop - lag, 0)

    def unrolled(tt, carry):
        for j in range(slots):
            iteration(lag + slots * tt + j, (lag + j) % slots)
        return carry

    lax.fori_loop(0, n_main // slots, unrolled, 0)
    rest = lax.rem(n_main, slots)
    t0 = n_loop - rest

    for r in range(slots):
        @pl.when(jnp.logical_and(n_loop >= lag, rest == r))
        def _():
            for j in range(r):
                iteration(t0 + j, (lag + j) % slots)
            finish((lag + r) % slots, lag)

    for short in range(lag):
        pl.when(n_loop == short)(lambda short=short: finish(short, short))


def _attn_fwd(q, k, v, tq):
    T = q.shape[1]
    tk = tq
    rc = min(SOFTMAX_ROWS, tq)

    def body(q_ref, k_ref, v_ref, o_ref, lse_ref, s0, s1, s2, p0, p1, p2, a0, a1, a2, m_ref, acc_ref):
        qi = pl.program_id(1)
        s_buf, p_buf, a_buf = (s0, s1, s2), (p0, p1, p2), (a0, a1, a2)

        def scores(t, slot):
            ks = pl.multiple_of(t * tk, tk)
            s_buf[slot][...] = _dot_nt(q_ref[0], k_ref[0, pl.ds(ks, tk), :])

        def blocks(masked):
            return ((0, tq // 2, tk // 2), (tq // 2, tq // 2, tk)) if masked else ((0, tq, tk),)

        def values(t, slot, masked):
            ks = pl.multiple_of(t * tk, tk)
            for q0, nq, nk in blocks(masked):
                rows = slice(q0, q0 + nq)
                acc_ref[rows, :] = (acc_ref[rows, :] * a_buf[slot][rows, :]
                                    + _dot(p_buf[slot][rows, 0:nk], v_ref[0, pl.ds(ks, nk), :]))

        def softmax(t, slot, masked):
            for q0, nq, nk in blocks(masked):
                rows = slice(q0, q0 + nq)
                s_all = s_buf[slot][rows, 0:nk]
                if masked:
                    row = lax.broadcasted_iota(jnp.int32, (nq, nk), 0) + q0
                    col = lax.broadcasted_iota(jnp.int32, (nq, nk), 1)
                    s_all = jnp.where(col <= row, s_all, NEG)
                    s_buf[slot][rows, 0:nk] = s_all
                m_old = m_ref[rows, :]
                m_new = jnp.maximum(m_old, jnp.max(s_all, axis=1, keepdims=True))
                a_buf[slot][rows, :] = jnp.exp2((m_old - m_new) * EXP2_SCALE)
                m_ref[rows, :] = m_new
                for r0 in range(0, nq, rc):
                    s = s_buf[slot][q0 + r0:q0 + r0 + rc, 0:nk]
                    p_buf[slot][q0 + r0:q0 + r0 + rc, 0:nk] = jnp.exp2((s - m_new[r0:r0 + rc, :]) * EXP2_SCALE).astype(BF16)

        def last():
            l = acc_ref[:, V_DIM:2 * V_DIM]
            o_ref[...] = acc_ref[:, 0:V_DIM] / l
            lse_ref[0] = (m_ref[...] * SCALE + jnp.log(l)).T[0:1, :]

        m_ref[...] = jnp.full_like(m_ref, NEG)
        acc_ref[...] = jnp.zeros_like(acc_ref)
        _chunk_pipeline(qi, 2, scores, softmax, values, last)

    return pl.pallas_call(
        body, name="attn_fwd", grid=(N_HEADS, T // tq),
        in_specs=[pl.BlockSpec((1, tq, HEAD_PAD), lambda h, i: (h, i, 0)),
                  pl.BlockSpec((1, T, HEAD_PAD), lambda h, i: (h, 0, 0)),
                  pl.BlockSpec((1, T, 2 * V_DIM), lambda h, i: (h, 0, 0))],
        out_specs=[pl.BlockSpec((tq, V_DIM), lambda h, i: (i, h)),
                   pl.BlockSpec((1, 1, tq), lambda h, i: (h, 0, i))],
        out_shape=[jax.ShapeDtypeStruct((T, ATTN_W), F32), jax.ShapeDtypeStruct((N_HEADS, 1, T), F32)],
        scratch_shapes=[pltpu.VMEM((tq, tk), F32)] * 3 + [pltpu.VMEM((tq, tk), BF16)] * 3
                       + [pltpu.VMEM((tq, 1), F32)] * 4 + [pltpu.VMEM((tq, 2 * V_DIM), F32)],
        compiler_params=_params(dimension_semantics=("arbitrary", "arbitrary")),
    )(q, k, v)


def _tail(x, o, proj, p, tgt, g_oa, g_oc, g_pl, conv_w, w_o, w_pl, w_plg, tm):
    T = x.shape[0]
    nt = T // tm

    def body(x_ref, o_ref, za_ref, cb_ref, cc_ref, cx_ref, zc_ref, cch_ref, cxh_ref, p_ref, tgt_ref,
             g_oa_ref, g_oc_ref, g_pl_ref, cw_ref, w_o_ref, w_pl_ref, w_plg_ref,
             dx1_ref, do_ref, delta_ref, dtail_ref, du_ref,
             dw_o_ref, dw_pl_ref, dw_plg_ref, dg_oa_ref, dg_oc_ref, dg_pl_ref, dcw_ref, loss_ref):
        i = pl.program_id(0)

        @pl.when(i == 0)
        def _():
            for r in (dw_o_ref, dw_pl_ref, dw_plg_ref, dg_oa_ref, dg_oc_ref, dg_pl_ref, dcw_ref, loss_ref):
                r[...] = jnp.zeros_like(r)

        g_oa, g_oc, g_pl = g_oa_ref[...], g_oc_ref[...], g_pl_ref[...]
        w0, w1, w2 = cw_ref[0:1, :], cw_ref[1:2, :], cw_ref[2:3, :]

        xv, ov, za, cb, zc = x_ref[...], o_ref[...], za_ref[...], cb_ref[...], zc_ref[...]
        pb = p_ref[...].astype(BF16)
        pp = _dot(pb, w_pl_ref[...])

        sa = _sigmoid(za)
        silu_a = za * sa
        ga = ov * silu_a
        ra = _inv_rms(ga, ATTN_W)
        xa = ga * ra
        ya = (xa * g_oa).astype(BF16)
        x1_a = _dot(ya, w_o_ref[0:ATTN_W, :])
        v = cc_ref[...] * cx_ref[...]
        not_first = jnp.where(i > 0, 1.0, 0.0)
        hv6 = cch_ref[6:7, :] * cxh_ref[6:7, :] * not_first
        hv7 = cch_ref[7:8, :] * cxh_ref[7:8, :] * not_first
        row = lax.broadcasted_iota(jnp.int32, v.shape, 0)
        v1 = jnp.where(row == 0, hv7, pltpu.roll(v, 1, 0))
        v2 = jnp.where(row == 0, hv6, jnp.where(row == 1, hv7, pltpu.roll(v, 2, 0)))
        u = w0 * v2 + w1 * v1 + w2 * v
        sc = _sigmoid(zc)
        silu_c = zc * sc
        gc = cb * u * silu_c
        rc = _inv_rms(gc, CONV_W)
        xc = gc * rc
        yc = (xc * g_oc).astype(BF16)
        x1 = xv + (x1_a + _dot(yc, w_o_ref[ATTN_W:D_MODEL, :]))
        r1 = _inv_rms(x1, D_MODEL)
        xh1 = x1 * r1
        n1 = (xh1 * g_pl).astype(BF16)
        gate = _sigmoid(_dot(n1, w_plg_ref[...]))
        err = x1 + gate * pp - tgt_ref[...]
        loss_ref[...] += 0.5 * jnp.sum(err * err) / D_MODEL
        dy = err / D_MODEL

        dpp = (dy * gate).astype(BF16)
        da = (dy * pp * gate * (1.0 - gate)).astype(BF16)
        dn1 = _dot_nt(da, w_plg_ref[...])
        dw_pl_ref[...] += _dot_tn(pb, dpp)
        dw_plg_ref[...] += _dot_tn(n1, da)
        dg_pl_ref[...] += _colsum(dn1 * xh1)
        dxh = dn1 * g_pl
        dx1 = dy + r1 * (dxh - xh1 * (jnp.sum(dxh * xh1, axis=-1, keepdims=True) / D_MODEL))
        dx1_ref[...] = dx1
        dx1b = dx1.astype(BF16)
        dya = _dot_nt(dx1b, w_o_ref[0:ATTN_W, :])
        dyc = _dot_nt(dx1b, w_o_ref[ATTN_W:D_MODEL, :])

        dw_o_ref[0:ATTN_W, :] += _dot_tn(ya, dx1b)
        dg_oa_ref[...] += _colsum(dya * xa)
        dxa = dya * g_oa
        dga = ra * (dxa - xa * (jnp.sum(dxa * xa, axis=-1, keepdims=True) / ATTN_W))
        do = (dga * silu_a).astype(BF16)
        do_ref[...] = do
        dof = do.astype(F32) * ov
        for hd in range(N_HEADS):
            delta_ref[hd] = _lane_sum(dof[:, hd * V_DIM:(hd + 1) * V_DIM]).T[0:1, :]
        dtail_ref[:, 0:512] = (dga * ov * (sa * (1.0 + za * (1.0 - sa)))).astype(BF16)

        dw_o_ref[ATTN_W:D_MODEL, :] += _dot_tn(yc, dx1b)
        dg_oc_ref[...] += _colsum(dyc * xc)
        dxc = dyc * g_oc
        dgc = rc * (dxc - xc * (jnp.sum(dxc * xc, axis=-1, keepdims=True) / CONV_W))
        dtail_ref[:, 512:1024] = (dgc * u * silu_c).astype(BF16)
        du = dgc * cb * silu_c
        du_ref[...] = du
        dtail_ref[:, 1024:1536] = (dgc * cb * u * (sc * (1.0 + zc * (1.0 - sc)))).astype(BF16)
        dcw_ref[0:1, :] += _colsum(du * v2)
        dcw_ref[1:2, :] += _colsum(du * v1)
        dcw_ref[2:3, :] += _colsum(du * v)

    row = lambda i: (i, 0)
    col = lambda c: (lambda i: (i, c))
    halo = lambda c: (lambda i: (jnp.maximum(i * (tm // 8) - 1, 0), c))
    in_specs = [pl.BlockSpec((tm, D_MODEL), row), pl.BlockSpec((tm, ATTN_W), row)]
    in_specs += [pl.BlockSpec((tm, 512), col(c)) for c in (1, 2, 3, 4, 5)]
    in_specs += [pl.BlockSpec((8, 512), halo(3)), pl.BlockSpec((8, 512), halo(4))]
    in_specs += [pl.BlockSpec((tm, PLE), row), pl.BlockSpec((tm, D_MODEL), row),
                 _full((1, ATTN_W)), _full((1, CONV_W)), _full((1, D_MODEL)), _full((3, CONV_W)),
                 _full((D_MODEL, D_MODEL)), _full((PLE, D_MODEL)), _full((D_MODEL, D_MODEL))]
    out_specs = [pl.BlockSpec((tm, D_MODEL), row), pl.BlockSpec((tm, ATTN_W), row),
                 pl.BlockSpec((N_HEADS, 1, tm), lambda i: (0, 0, i)), pl.BlockSpec((tm, 1536), row),
                 pl.BlockSpec((tm, CONV_W), row),
                 _full((D_MODEL, D_MODEL)), _full((PLE, D_MODEL)), _full((D_MODEL, D_MODEL)),
                 _full((1, ATTN_W)), _full((1, CONV_W)), _full((1, D_MODEL)), _full((3, CONV_W)), _full((1, LANES))]
    out_shape = [jax.ShapeDtypeStruct((T, D_MODEL), F32), jax.ShapeDtypeStruct((T, ATTN_W), BF16),
                 jax.ShapeDtypeStruct((N_HEADS, 1, T), F32), jax.ShapeDtypeStruct((T, 1536), BF16),
                 jax.ShapeDtypeStruct((T, CONV_W), F32),
                 jax.ShapeDtypeStruct((D_MODEL, D_MODEL), F32), jax.ShapeDtypeStruct((PLE, D_MODEL), F32),
                 jax.ShapeDtypeStruct((D_MODEL, D_MODEL), F32),
                 jax.ShapeDtypeStruct((1, ATTN_W), F32), jax.ShapeDtypeStruct((1, CONV_W), F32),
                 jax.ShapeDtypeStruct((1, D_MODEL), F32), jax.ShapeDtypeStruct((3, CONV_W), F32),
                 jax.ShapeDtypeStruct((1, LANES), F32)]
    return pl.pallas_call(
        body, name="tail", grid=(nt,), in_specs=in_specs, out_specs=out_specs, out_shape=out_shape,
        compiler_params=_params(dimension_semantics=("arbitrary",)),
    )(x, o, proj, proj, proj, proj, proj, proj, proj, p, tgt, g_oa, g_oc, g_pl, conv_w, w_o, w_pl, w_plg)


def _attn_bwd(q, k, v, do, lse_row, delta_row, tk, swap):
    T = q.shape[1]
    tq = tk
    nq = T // tq
    rc = min(SOFTMAX_ROWS, tk)
    hk, hq = tk // 2, tq // 2
    n_swap = len(swap)

    def body(q_ref, k_ref, v_ref, do_ref, lse_ref, dl_ref, *rest):
        swap_in, (dq_ref, dk_ref, dv_ref), rest = rest[:n_swap], rest[n_swap:n_swap + 3], rest[n_swap + 3:]
        swap_out, (s0, s1, d0, d1, p0, p1, g0, g1, dk_acc, dv_acc), sems = rest[:n_swap], rest[n_swap:n_swap + 10], rest[n_swap + 10:]
        kj = pl.program_id(1)
        s_buf, dp_buf, p_buf, g_buf = (s0, s1), (d0, d1), (p0, p1), (g0, g1)

        if n_swap:
            start, drain = _swap_steps(n_swap, swap_in, swap_out, *sems)
            pl.when(jnp.logical_and(pl.program_id(0) == 0, kj == 0))(start)

        @pl.when(kj == 0)
        def _():
            dq_ref[...] = jnp.zeros_like(dq_ref)

        def q_start(t):
            return pl.multiple_of((nq - 1 - t) * tq, tq)

        def matmuls(t, slot):
            qs = q_start(t)
            s_buf[slot][...] = _dot_nt(k_ref[0], q_ref[0, pl.ds(qs, tq), :])
            dp_buf[slot][...] = _dot_nt(v_ref[0], do_ref[pl.ds(qs, tq), :])

        def pointwise(t, slot, masked):
            qs = q_start(t)
            lse2 = lse_ref[0, :, pl.ds(qs, tq)] * LOG2E
            dl = dl_ref[0, :, pl.ds(qs, tq)]
            for r0 in range(0, tk, rc):
                c0 = r0 // hk * hq if masked else 0
                rows, cols = slice(r0, r0 + rc), slice(c0, tq)
                st = s_buf[slot][rows, cols]
                if masked:
                    row = lax.broadcasted_iota(jnp.int32, (rc, tq - c0), 0) + r0
                    col = lax.broadcasted_iota(jnp.int32, (rc, tq - c0), 1) + c0
                    st = jnp.where(row <= col, st, NEG)
                pt = jnp.exp2(st * EXP2_SCALE - lse2[:, cols])
                p_buf[slot][rows, cols] = pt.astype(BF16)
                g_buf[slot][rows, cols] = (pt * (dp_buf[slot][rows, cols] - dl[:, cols]) * SCALE).astype(BF16)

        def accumulate(t, slot, masked):
            qs = q_start(t)
            p, g = p_buf[slot], g_buf[slot]
            if not masked:
                dv_acc[...] += _dot(p[...], do_ref[pl.ds(qs, tq), :])
                dk_acc[...] += _dot(g[...], q_ref[0, pl.ds(qs, tq), :])
                dq_ref[0, pl.ds(qs, tq), :] += _dot_tn(g[...], k_ref[0])
                return
            q2 = pl.multiple_of(qs + hq, hq)
            dv_acc[0:hk, :] += _dot(p[0:hk, :], do_ref[pl.ds(qs, tq), :])
            dv_acc[hk:tk, :] += _dot(p[hk:tk, hq:tq], do_ref[pl.ds(q2, hq), :])
            dk_acc[0:hk, :] += _dot(g[0:hk, :], q_ref[0, pl.ds(qs, tq), :])
            dk_acc[hk:tk, :] += _dot(g[hk:tk, hq:tq], q_ref[0, pl.ds(q2, hq), :])
            dq_ref[0, pl.ds(qs, hq), :] += _dot_tn(g[0:hk, 0:hq], k_ref[0, 0:hk, :])
            dq_ref[0, pl.ds(q2, hq), :] += _dot_tn(g[:, hq:tq], k_ref[0])

        def last():
            dk_ref[0] = dk_acc[...]
            dv_ref[0] = dv_acc[...]

        dk_acc[...] = jnp.zeros_like(dk_acc)
        dv_acc[...] = jnp.zeros_like(dv_acc)
        _chunk_pipeline(nq - 1 - kj, 1, matmuls, pointwise, accumulate, last)

        if n_swap:
            pl.when(jnp.logical_and(pl.program_id(0) == N_HEADS - 1, kj == T // tk - 1))(drain)

    outs = pl.pallas_call(
        body, name="attn_bwd", grid=(N_HEADS, T // tk),
        in_specs=[pl.BlockSpec((1, T, HEAD_PAD), lambda h, j: (h, 0, 0)),
                  pl.BlockSpec((1, tk, HEAD_PAD), lambda h, j: (h, j, 0)),
                  pl.BlockSpec((1, tk, V_DIM), lambda h, j: (h, j, 0)),
                  pl.BlockSpec((T, V_DIM), lambda h, j: (0, h)),
                  pl.BlockSpec((1, 1, T), lambda h, j: (h, 0, 0)),
                  pl.BlockSpec((1, 1, T), lambda h, j: (h, 0, 0))] + [_ANY] * n_swap,
        out_specs=[pl.BlockSpec((1, T, HEAD_PAD), lambda h, j: (h, 0, 0)),
                   pl.BlockSpec((1, tk, HEAD_PAD), lambda h, j: (h, j, 0)),
                   pl.BlockSpec((1, tk, V_DIM), lambda h, j: (h, j, 0))] + [_ANY] * n_swap,
        out_shape=[jax.ShapeDtypeStruct((N_HEADS, T, HEAD_PAD), F32), jax.ShapeDtypeStruct((N_HEADS, T, HEAD_PAD), F32),
                   jax.ShapeDtypeStruct((N_HEADS, T, V_DIM), F32)] + _swapped_shapes(swap, []),
        scratch_shapes=[pltpu.VMEM((tk, tq), F32)] * 4 + [pltpu.VMEM((tk, tq), BF16)] * 4
                       + [pltpu.VMEM((tk, HEAD_PAD), F32), pltpu.VMEM((tk, V_DIM), F32)]
                       + ([pltpu.SemaphoreType.DMA((n_swap,))] * 2 if n_swap else []),
        compiler_params=_params(dimension_semantics=("arbitrary", "arbitrary")),
    )(q, k, v, do, lse_row, delta_row, *swap)
    return outs[:3], outs[3:]


def _bwd_proj(x, dx1, pos, proj, dq, dk, dv, dtail, du, g_in, w_in, g_cq, w_uq, g_ckv, w_ukv, gq, gk, conv_w,
              invf, sgn, tm):
    T = x.shape[0]
    nt = T // tm

    ts = min(SUB_TILE, tm)

    def body(x_ref, dx1_ref, pos_ref, lat_ref, cc_ref, cx_ref, dq_ref, dk_ref, dv_ref, dtail_ref, du_ref, dun_ref, *rest):
        consts, (gx_ref, h_ref, dproj_ref), sums = rest[:11], rest[11:14], rest[14:]
        cw_ref = consts[8]
        i = pl.program_id(0)

        @pl.when(i == 0)
        def _():
            for r in sums:
                r[...] = jnp.zeros_like(r)

        du_v = du_ref[...]
        not_last = jnp.where(i < nt - 1, 1.0, 0.0)
        nx0 = dun_ref[0:1, :] * not_last
        nx1 = dun_ref[1:2, :] * not_last
        row = lax.broadcasted_iota(jnp.int32, du_v.shape, 0)
        du1 = jnp.where(row == tm - 1, nx0, pltpu.roll(du_v, tm - 1, 0))
        du2 = jnp.where(row == tm - 2, nx0, jnp.where(row == tm - 1, nx1, pltpu.roll(du_v, tm - 2, 0)))
        dvc = cw_ref[2:3, :] * du_v + cw_ref[1:2, :] * du1 + cw_ref[0:1, :] * du2
        dproj_ref[:, 1536:2048] = (dvc * cx_ref[...]).astype(BF16)
        dproj_ref[:, 2048:2560] = (dvc * cc_ref[...]).astype(BF16)

        for r0 in range(0, tm, ts):
            rows = slice(r0, r0 + ts)
            work(x_ref.at[rows, :], dx1_ref.at[rows, :], pos_ref.at[rows, :], lat_ref.at[rows, :],
                 dq_ref.at[:, rows, :], dk_ref.at[:, rows, :], dv_ref.at[:, rows, :], dtail_ref.at[rows, :], *consts,
                 gx_ref.at[rows, :], h_ref.at[:, rows], dproj_ref.at[rows, :], *sums)

    def work(x_ref, dx1_ref, pos_ref, lat_ref, dq_ref, dk_ref, dv_ref, dtail_ref,
             g_in_ref, w_in_ref, g_cq_ref, w_uq_ref, g_ckv_ref, w_ukv_ref, gq_ref, gk_ref, cw_ref, invf_ref, sgn_ref,
             gx_ref, h_ref, dproj_ref, dw_uq_ref, dw_ukv_ref, dg_in_ref, dg_cq_ref, dg_ckv_ref, dgq_ref, dgk_ref):
        xv = x_ref[...]
        r0 = _rep(_inv_rms_mxu(xv), D_MODEL)
        xh0 = xv * r0
        g_in = g_in_ref[...]
        h_ref[...] = (xh0 * g_in).astype(BF16).T

        c_q = lat_ref[:, 0:Q_LORA]
        rq = _rep(_inv_rms_mxu(c_q), Q_LORA)
        xq = c_q * rq
        g_cq = g_cq_ref[...]
        cqn = (xq * g_cq).astype(BF16)
        c_kv = lat_ref[:, Q_LORA:Q_LORA + KV_LORA]
        rkv = _inv_rms_mxu(c_kv)
        xkv = c_kv * rkv
        g_ckv = g_ckv_ref[...]
        ckvn = (xkv * g_ckv).astype(BF16)
        kpe = lat_ref[:, 384:512]
        kpe_sq = kpe * kpe
        cos_b, sin_b = _rope_tables(pos_ref, invf_ref, sgn_ref)
        gq_a, gq_b = gq_ref[:, 0:NOPE], gq_ref[:, NOPE:HEAD_PAD]
        gk_a, gk_b = gk_ref[:, 0:NOPE], gk_ref[:, NOPE:HEAD_PAD]

        dproj_ref[:, 512:1536] = dtail_ref[:, 0:1024]
        dproj_ref[:, 2560:3072] = dtail_ref[:, 1024:1536]

        def dh_part(c0):
            return _dot_nt(dproj_ref[:, c0:c0 + 512], w_in_ref[:, c0:c0 + 512])

        later_chunks = ((512,), (1024,), (1536, 2048), (2560,))
        dh = jnp.zeros((ts, D_MODEL), F32)
        acc = dict(dh=dh, dkpe=jnp.zeros((ts, LANES), F32), dcqn=jnp.zeros((ts, Q_LORA), F32),
                   dckvn=jnp.zeros((ts, KV_LORA), F32))

        def dh_chunks():
            for chunks in later_chunks:
                for chunk in chunks:
                    acc["dh"] = acc["dh"] + dh_part(chunk)
                    yield

        def queries(hd):
            qh = _dot(cqn, w_uq_ref[hd])
            yield
            a, b = qh[:, 0:NOPE], qh[:, NOPE:HEAD_PAD]
            r = lax.rsqrt(_lane_sum(a * a + b * b) / QK_DIM + EPS)
            yield
            xa, xb = a * r, b * r
            dan = dq_ref[hd, :, 0:NOPE]
            dbr = dq_ref[hd, :, NOPE:HEAD_PAD]
            dbn = dbr * cos_b + _swap_rope_halves(dbr * sin_b)
            yield
            dgq_ref[:, 0:NOPE] += _colsum(dan * xa)
            dgq_ref[:, NOPE:HEAD_PAD] += _colsum(dbn * xb)
            dxa, dxb = dan * gq_a, dbn * gq_b
            cq = _lane_sum(dxa * xa + dxb * xb) / QK_DIM
            yield
            dqh = jnp.concatenate([r * (dxa - xa * cq), r * (dxb - xb * cq)], axis=-1).astype(BF16)
            yield
            dw_uq_ref[hd] += _dot_tn(cqn, dqh)
            yield
            acc["dcqn"] = acc["dcqn"] + _dot_nt(dqh, w_uq_ref[hd])
            yield

        def keys(hd):
            kvh = _dot(ckvn, w_ukv_ref[hd])
            yield
            ka = kvh[:, 0:NOPE]
            rk = lax.rsqrt(_lane_sum(ka * ka + kpe_sq) / QK_DIM + EPS)
            yield
            xka, xkb = ka * rk, kpe * rk
            dkan = dk_ref[hd, :, 0:NOPE]
            dkbr = dk_ref[hd, :, NOPE:HEAD_PAD]
            dkbn = dkbr * cos_b + _swap_rope_halves(dkbr * sin_b)
            yield
            dgk_ref[:, 0:NOPE] += _colsum(dkan * xka)
            dgk_ref[:, NOPE:HEAD_PAD] += _colsum(dkbn * xkb)
            dxka, dxkb = dkan * gk_a, dkbn * gk_b
            ck = _lane_sum(dxka * xka + dxkb * xkb) / QK_DIM
            yield
            acc["dkpe"] = acc["dkpe"] + rk * (dxkb - xkb * ck)
            dkvh = jnp.concatenate([rk * (dxka - xka * ck), dv_ref[hd]], axis=-1).astype(BF16)
            yield
            dw_ukv_ref[hd] += _dot_tn(ckvn, dkvh)
            yield
            acc["dckvn"] = acc["dckvn"] + _dot_nt(dkvh, w_ukv_ref[hd])
            yield

        chains = [dh_chunks()]
        for hd in range(N_HEADS):
            chains += [queries(hd), keys(hd)]
        _round_robin(chains, 5)
        dh, dkpe, dcqn, dckvn = acc["dh"], acc["dkpe"], acc["dcqn"], acc["dckvn"]

        dg_cq_ref[...] += _colsum(dcqn * xq)
        dxq = dcqn * g_cq
        dproj_ref[:, 0:Q_LORA] = (rq * (dxq - xq * _rep(_lane_sum(dxq * xq) / Q_LORA, Q_LORA))).astype(BF16)
        dg_ckv_ref[...] += _colsum(dckvn * xkv)
        dxkv = dckvn * g_ckv
        dproj_ref[:, 256:384] = (rkv * (dxkv - xkv * (_lane_sum(dxkv * xkv) / KV_LORA))).astype(BF16)
        dproj_ref[:, 384:512] = dkpe.astype(BF16)
        dh = dh + dh_part(0)
        dg_in_ref[...] += _colsum(dh * xh0)
        dxh = dh * g_in
        gx_ref[...] = dx1_ref[...] + r0 * (dxh - xh0 * _rep(_lane_sum(dxh * xh0) / D_MODEL, D_MODEL))

    row = lambda i: (i, 0)
    col = lambda c: (lambda i: (i, c))
    head_rows = lambda i: (0, i, 0)
    nxt = lambda i: (jnp.minimum((i + 1) * (tm // 8), T // 8 - 1), 0)
    in_specs = [pl.BlockSpec((tm, D_MODEL), row), pl.BlockSpec((tm, D_MODEL), row), pl.BlockSpec((tm, 1), row),
                pl.BlockSpec((tm, 512), col(0)), pl.BlockSpec((tm, 512), col(3)), pl.BlockSpec((tm, 512), col(4)),
                pl.BlockSpec((N_HEADS, tm, HEAD_PAD), head_rows), pl.BlockSpec((N_HEADS, tm, HEAD_PAD), head_rows),
                pl.BlockSpec((N_HEADS, tm, V_DIM), head_rows), pl.BlockSpec((tm, 1536), row),
                pl.BlockSpec((tm, CONV_W), row), pl.BlockSpec((8, CONV_W), nxt),
                _full((1, D_MODEL)), _full((D_MODEL, PROJ_EXT)), _full((1, Q_LORA)), _full((N_HEADS, Q_LORA, HEAD_PAD)),
                _full((1, KV_LORA)), _full((N_HEADS, KV_LORA, HEAD_PAD)), _full((1, HEAD_PAD)), _full((1, HEAD_PAD)),
                _full((3, CONV_W)), _full((1, LANES)), _full((1, LANES))]
    out_specs = [pl.BlockSpec((tm, D_MODEL), row), pl.BlockSpec((D_MODEL, tm), lambda i: (0, i)),
                 pl.BlockSpec((tm, PROJ_EXT), row),
                 _full((N_HEADS, Q_LORA, HEAD_PAD)), _full((N_HEADS, KV_LORA, HEAD_PAD)),
                 _full((1, D_MODEL)), _full((1, Q_LORA)), _full((1, KV_LORA)), _full((1, HEAD_PAD)), _full((1, HEAD_PAD))]
    out_shape = [jax.ShapeDtypeStruct((T, D_MODEL), F32), jax.ShapeDtypeStruct((D_MODEL, T), BF16),
                 jax.ShapeDtypeStruct((T, PROJ_EXT), BF16),
                 jax.ShapeDtypeStruct((N_HEADS, Q_LORA, HEAD_PAD), F32), jax.ShapeDtypeStruct((N_HEADS, KV_LORA, HEAD_PAD), F32),
                 jax.ShapeDtypeStruct((1, D_MODEL), F32), jax.ShapeDtypeStruct((1, Q_LORA), F32),
                 jax.ShapeDtypeStruct((1, KV_LORA), F32), jax.ShapeDtypeStruct((1, HEAD_PAD), F32),
                 jax.ShapeDtypeStruct((1, HEAD_PAD), F32)]
    return pl.pallas_call(
        body, name="bwd_proj", grid=(nt,), in_specs=in_specs, out_specs=out_specs, out_shape=out_shape,
        compiler_params=_params(dimension_semantics=("arbitrary",)),
    )(x, dx1, pos, proj, proj, proj, dq, dk, dv, dtail, du, du, g_in, w_in, g_cq, w_uq, g_ckv, w_ukv, gq, gk, conv_w,
      invf, sgn)


def _matmul_acc(a, b, tt, tn, parts):
    M, T = a.shape
    N = b.shape[1]
    n = len(parts)
    grid = (N // tn, T // tt)
    hm = M // 2

    def body(a_ref, b_ref, *rest):
        part_refs, (o_ref, sib_ref), rest = rest[:n], rest[n:n + 2], rest[n + 2:]
        out_refs, (stage_ref, tile_send, tile_recv), sems = rest[:n], rest[n:n + 3], rest[n + 3:]
        j, t = pl.program_id(0), pl.program_id(1)
        if n:
            start, drain = _scatter_steps(part_refs, out_refs, *sems)
            pl.when(jnp.logical_and(j == 0, t == 0))(start)

        def to_sibling(jj):
            x, y, c = _mesh_pos()
            return _remote(stage_ref, sib_ref.at[:, pl.ds(pl.multiple_of(jj * tn, tn), tn)],
                           tile_send, tile_recv, jj, (x, y, 1 - c))

        @pl.when(t == 0)
        def _():
            o_ref[...] = jnp.zeros_like(o_ref)

        o_ref[...] += _dot(a_ref[...], b_ref[...])

        tile_done = t == grid[1] - 1
        pl.when(jnp.logical_and(tile_done, j > 0))(lambda: to_sibling(j - 1).wait())

        @pl.when(tile_done)
        def _():
            c = lax.axis_index("c")
            stage_ref[...] = o_ref[pl.ds(pl.multiple_of((1 - c) * hm, hm), hm), :]
            to_sibling(j).start()

        pl.when(jnp.logical_and(tile_done, j == grid[0] - 1))(lambda: to_sibling(j).wait())
        if n:
            pl.when(jnp.logical_and(j == grid[0] - 1, t == grid[1] - 1))(drain)

    sems = [pltpu.SemaphoreType.DMA((3 * n,)), pltpu.SemaphoreType.DMA((3 * n,)), pltpu.SemaphoreType.DMA((n,))]
    outs = pl.pallas_call(
        body, name="dw_in", grid=grid,
        in_specs=[pl.BlockSpec((M, tt), lambda j, t: (0, t)), pl.BlockSpec((tt, tn), lambda j, t: (t, j))] + [_ANY] * n,
        out_specs=[pl.BlockSpec((M, tn), lambda j, t: (0, j)), _ANY] + [_ANY] * n,
        out_shape=[jax.ShapeDtypeStruct((M, N), F32), jax.ShapeDtypeStruct((hm, N), F32)] + _scattered_shapes(parts),
        scratch_shapes=[pltpu.VMEM((hm, tn), F32)] + [pltpu.SemaphoreType.DMA((grid[0],))] * 2 + (sems if n else []),
        compiler_params=_params(dimension_semantics=("arbitrary", "arbitrary")),
    )(a, b, *parts)
    return outs[0], outs[1], outs[2:]


def _add_chips(parts, small_parts):
    arrays = list(parts) + [small_parts]

    def body(*refs):
        ins, outs = refs[:len(arrays)], refs[len(arrays):]
        for a_ref, o_ref in zip(ins, outs):
            part = lambda k: a_ref[k].astype(F32)
            o_ref[...] = ((part(0) + part(1)) + part(2)) + part(3)

    in_specs, out_specs, out_shape = [], [], []
    for a in arrays:
        _, rows, cols = a.shape
        in_specs.append(pl.BlockSpec((N_CHIPS, rows // 2, cols), lambda i: (0, i, 0)))
        out_specs.append(pl.BlockSpec((rows // 2, cols), lambda i: (i, 0)))
        out_shape.append(jax.ShapeDtypeStruct((rows, cols), F32))
    outs = pl.pallas_call(body, name="add_chips", grid=(2,), in_specs=in_specs, out_specs=out_specs,
                          out_shape=out_shape, compiler_params=_params(dimension_semantics=("arbitrary",)))(*arrays)
    return outs[:-1], outs[-1]


def _adamw_small(ws, gs, ms, vs):
    n = len(ws)

    def body(*refs):
        for i in range(n):
            w_ref, g_ref, m_ref, v_ref = (refs[k * n + i] for k in range(4))
            d_ref, nm_ref, nv_ref = (refs[(4 + k) * n + i] for k in range(3))
            _adamw_math(g_ref[...], w_ref, m_ref, v_ref, d_ref, nm_ref, nv_ref)

    shapes = [jax.ShapeDtypeStruct(w.shape, F32) for w in ws]
    outs = pl.pallas_call(body, name="adamw_small", out_shape=shapes * 3)(*ws, *gs, *ms, *vs)
    return outs[:n], outs[n:2 * n], outs[2 * n:]


def _adamw_math(gv, w_ref, m_ref, v_ref, d_ref, nm_ref, nv_ref):
    nm = B1 * m_ref[...] + (1.0 - B1) * gv
    nv = B2 * v_ref[...] + (1.0 - B2) * (gv * gv)
    m_hat = nm / (1.0 - B1 ** STEP)
    v_hat = nv / (1.0 - B2 ** STEP)
    d_ref[...] = -LR * (m_hat / (jnp.sqrt(v_hat) + ADAM_EPS) + WD * w_ref[...])
    nm_ref[...] = nm
    nv_ref[...] = nv


def _adamw_halves(w, mine, other, m, v, c, name, transposed):
    hr, cols = mine.shape

    def body(c_ref, w_ref, mine_ref, other_ref, m_ref, v_ref, g_ref, d_ref, nm_ref, nv_ref, *picked):
        gv = jnp.where(pl.program_id(0) == c_ref[0], mine_ref[...], other_ref[...])
        if transposed:
            picked[0][...] = gv
            _store_transposed(picked[0], g_ref)
            gv = g_ref[...]
        else:
            g_ref[...] = gv
        _adamw_math(gv, w_ref, m_ref, v_ref, d_ref, nm_ref, nv_ref)

    if transposed:
        half = pl.BlockSpec((cols, hr), lambda i, c_ref: (0, i))
    else:
        half = pl.BlockSpec((hr, cols), lambda i, c_ref: (i, 0))
    whole = pl.BlockSpec((hr, cols), lambda i, c_ref: (0, 0))
    shp = jax.ShapeDtypeStruct(w.shape, F32)
    return pl.pallas_call(
        body, name=name, out_shape=[shp] * 4,
        grid_spec=pltpu.PrefetchScalarGridSpec(num_scalar_prefetch=1, grid=(2,), in_specs=[half, whole, whole, half, half],
                                               out_specs=[half] * 4,
                                               scratch_shapes=[pltpu.VMEM((hr, cols), F32)] if transposed else []),
        compiler_params=_params(dimension_semantics=("arbitrary",)),
    )(c.reshape(1), w, mine, other, m, v)


_ANY = pl.BlockSpec(memory_space=pl.ANY)


def _mesh_pos():
    return lax.axis_index("x"), lax.axis_index("y"), lax.axis_index("c")


def _other_chips(x, y):
    return [(1 - x, y), (x, 1 - y), (1 - x, 1 - y)]


def _remote(src, dst, send_sems, recv_sems, k, to):
    return pltpu.make_async_remote_copy(src_ref=src, dst_ref=dst, send_sem=send_sems.at[k], recv_sem=recv_sems.at[k],
                                        device_id=to, device_id_type=MESH)


def _gather_weights(shards, n_transposed):
    n = len(shards)
    shapes = [s.shape[::-1] if i < n_transposed else s.shape for i, s in enumerate(shards)]

    def body(*refs):
        start, forward, drain = _gather_steps(shapes, refs[:n], refs[n:2 * n], refs[2 * n:3 * n], *refs[3 * n:])
        start()
        forward()
        drain()

    vmem = pl.BlockSpec(memory_space=pltpu.VMEM)
    return pl.pallas_call(
        body, name="gather_weights", in_specs=[vmem] * n, out_specs=[_ANY] * n,
        out_shape=_gathered_shapes(shapes), scratch_shapes=_gather_scratch(shapes), compiler_params=_params(),
    )(*shards)


def _travel_shape(shape):
    rows, cols = shape
    return (rows, HEAD_PAD if cols == QK_DIM else cols)


def _gathered_shapes(shapes):
    return [jax.ShapeDtypeStruct((N_CHIPS,) + _travel_shape(s), BF16) for s in shapes]


def _gather_scratch(shapes):
    n = len(shapes)
    return ([pltpu.VMEM(_travel_shape(s), BF16) for s in shapes]
            + [pltpu.SemaphoreType.DMA((6 * n,)), pltpu.SemaphoreType.DMA((6 * n,)), pltpu.SemaphoreType.DMA((n,))])


def _gather_steps(shapes, ins, outs, stage, send_sems, recv_sems, local_sems):
    n = len(shapes)
    halved = [s[0] % 32 == 0 for s in shapes]

    def part(i, ref, hc):
        if not halved[i]:
            return ref
        hr = shapes[i][0] // 2
        return ref.at[pl.ds(hc * hr, hr), :]

    def to_chip(i, j, x, y, c):
        cx, cy = _other_chips(x, y)[j]
        return _remote(part(i, stage[i], c), part(i, outs[i].at[2 * x + y], c), send_sems, recv_sems, 6 * i + j, (cx, cy, c))

    def to_sibling(i, j, x, y, c):
        cx, cy = _other_chips(x, y)[j]
        got = part(i, outs[i].at[2 * cx + cy], c)
        return _remote(got, got, send_sems, recv_sems, 6 * i + 3 + j, (x, y, 1 - c))

    def local(i, x, y):
        return pltpu.make_async_copy(stage[i], outs[i].at[2 * x + y], local_sems.at[i])

    def start():
        x, y, c = _mesh_pos()
        for i in range(n):
            cols = shapes[i][1]
            if stage[i].shape[1] != cols:
                stage[i][...] = jnp.zeros_like(stage[i])
            if ins[i].shape == shapes[i]:
                stage[i][:, 0:cols] = ins[i][...].astype(BF16)
            else:
                _store_transposed(ins[i], stage[i])
            local(i, x, y).start()
            for j in range(3):
                to_chip(i, j, x, y, c).start()

    def forward():
        x, y, c = _mesh_pos()
        for i in range(n):
            for j, (cx, cy) in enumerate(_other_chips(x, y)):
                got = part(i, outs[i].at[2 * cx + cy], c)
                _remote(got, got, send_sems, recv_sems, 6 * i + j, (cx, cy, c)).wait_recv()
                if halved[i]:
                    to_sibling(i, j, x, y, c).start()

    def drain():
        x, y, c = _mesh_pos()
        for i in range(n):
            for j, (cx, cy) in enumerate(_other_chips(x, y)):
                if halved[i]:
                    got = part(i, outs[i].at[2 * cx + cy], 1 - c)
                    _remote(got, got, send_sems, recv_sems, 6 * i + 3 + j, (x, y, 1 - c)).wait_recv()
                    to_sibling(i, j, x, y, c).wait_send()
                to_chip(i, j, x, y, c).wait_send()
            local(i, x, y).wait()

    return start, forward, drain


def _swap_halves(grads, whole, name):
    n, m = len(grads), len(grads) + len(whole)

    def body(*refs):
        start, drain = _swap_steps(n, refs[:m], refs[m:2 * m], refs[2 * m], refs[2 * m + 1])
        start()
        drain()

    outs = pl.pallas_call(
        body, name=name, in_specs=[_ANY] * m, out_specs=[_ANY] * m, out_shape=_swapped_shapes(grads, whole),
        scratch_shapes=[pltpu.SemaphoreType.DMA((m,)), pltpu.SemaphoreType.DMA((m,))],
    )(*grads, *whole)
    return outs[:n], outs[n:]


def _swapped_shapes(grads, whole):
    return ([jax.ShapeDtypeStruct((g.shape[0], g.shape[1] // 2, g.shape[2]), F32) for g in grads]
            + [jax.ShapeDtypeStruct(w.shape, F32) for w in whole])


def _swap_steps(n, ins, outs, send_sems, recv_sems):
    def copies():
        x, y, c = _mesh_pos()
        cps = []
        for i, src in enumerate(ins):
            if i < n:
                hr = src.shape[1] // 2
                src = src.at[:, pl.ds((1 - c) * hr, hr), :]
            cps.append(_remote(src, outs[i], send_sems, recv_sems, i, (x, y, 1 - c)))
        return cps

    def start():
        for cp in copies():
            cp.start()

    def drain():
        for cp in copies():
            cp.wait()

    return start, drain


def _scattered_shapes(parts):
    return [jax.ShapeDtypeStruct(p.shape if p.ndim == 3 else (N_CHIPS,) + p.shape, p.dtype) for p in parts]


def _scatter_steps(ins, outs, send_sems, recv_sems, local_sems):
    n = len(ins)

    def src(i, k):
        return ins[i].at[k] if len(ins[i].shape) == 3 else ins[i]

    def sends(x, y, c):
        return [_remote(src(i, 2 * cx + cy), outs[i].at[2 * x + y], send_sems, recv_sems, 3 * i + j, (cx, cy, c))
                for i in range(n) for j, (cx, cy) in enumerate(_other_chips(x, y))]

    def local(i, x, y):
        return pltpu.make_async_copy(src(i, 2 * x + y), outs[i].at[2 * x + y], local_sems.at[i])

    def start():
        x, y, c = _mesh_pos()
        for i in range(n):
            local(i, x, y).start()
        for cp in sends(x, y, c):
            cp.start()

    def drain():
        x, y, c = _mesh_pos()
        for i in range(n):
            for j, (cx, cy) in enumerate(_other_chips(x, y)):
                got = outs[i].at[2 * cx + cy]
                _remote(got, got, send_sems, recv_sems, 3 * i + j, (cx, cy, c)).wait_recv()
        for cp in sends(x, y, c):
            cp.wait_send()
        for i in range(n):
            local(i, x, y).wait()

    return start, drain


def _add_pair(grads, from_sibling, small, small_sibling, c):
    n = len(grads)

    def body(c_ref, *refs):
        ins, outs = refs[:2 * n + 2], refs[2 * n + 2:]
        for i in range(n + 1):
            outs[i][...] = (ins[2 * i][...] + ins[2 * i + 1][...]).astype(outs[i].dtype)

    in_specs, out_specs, out_shape, args = [], [], [], []
    for g, r in zip(grads, from_sibling):
        _, hr, cols = r.shape
        in_specs += [pl.BlockSpec((1, hr, cols), lambda k, c_ref: (k, c_ref[0], 0)),
                     pl.BlockSpec((1, hr, cols), lambda k, c_ref: (k, 0, 0))]
        out_specs.append(pl.BlockSpec((1, hr, cols), lambda k, c_ref: (k, 0, 0)))
        out_shape.append(jax.ShapeDtypeStruct(r.shape, BF16))
        args += [g, r]
    whole = pl.BlockSpec(small.shape, lambda k, c_ref: (0, 0))
    in_specs += [whole, whole]
    out_specs.append(whole)
    out_shape.append(jax.ShapeDtypeStruct(small.shape, F32))
    outs = pl.pallas_call(
        body, name="add_pair", out_shape=out_shape,
        grid_spec=pltpu.PrefetchScalarGridSpec(num_scalar_prefetch=1, grid=(N_CHIPS,), in_specs=in_specs,
                                               out_specs=out_specs),
        compiler_params=_params(dimension_semantics=("arbitrary",)),
    )(c.reshape(1), *args, small, small_sibling)
    return outs[:n], outs[n]


def _scatter_w_in(dw_in_e, from_sibling):
    hr = from_sibling.shape[1]
    shard = (N_CHIPS, hr, SHARD_COLS_IN)

    def body(g_in, r_in, out, g_buf, r_buf, p_buf, load_sems, send_sems, recv_sems, local_sems):
        c = lax.axis_index("c")
        loads = (pltpu.make_async_copy(g_in.at[0, pl.ds(c * hr, hr), :], g_buf, load_sems.at[0]),
                 pltpu.make_async_copy(r_in.at[0], r_buf, load_sems.at[1]))
        for cp in loads:
            cp.start()
        for cp in loads:
            cp.wait()
        g_buf[...] += r_buf[...]
        p_buf[0, :, 0:KPE_END] = g_buf[:, 0:KPE_END].astype(BF16)
        p_buf[0, :, KPE_END:SHARD_COLS_IN] = g_buf[:, KPE_END + KPE_PAD:SHARD_COLS_IN + KPE_PAD].astype(BF16)
        for k in range(1, N_CHIPS):
            p_buf[k] = g_buf[:, SHARD_COLS_IN * k + KPE_PAD:SHARD_COLS_IN * (k + 1) + KPE_PAD].astype(BF16)
        start, drain = _scatter_steps([p_buf], [out], send_sems, recv_sems, local_sems)
        start()
        drain()

    return pl.pallas_call(
        body, name="scatter_grads", in_specs=[_ANY] * 2, out_specs=_ANY, out_shape=jax.ShapeDtypeStruct(shard, BF16),
        scratch_shapes=[pltpu.VMEM((hr, PROJ_EXT), F32)] * 2 + [pltpu.VMEM(shard, BF16)]
                       + [pltpu.SemaphoreType.DMA((2,)), pltpu.SemaphoreType.DMA((3,)), pltpu.SemaphoreType.DMA((3,)),
                          pltpu.SemaphoreType.DMA((1,))],
        compiler_params=_params(),
    )(dw_in_e, from_sibling)


def _share_halves(halves):
    n = len(halves)

    def body(*refs):
        ins, outs, send_sems, recv_sems = refs[:n], refs[n:2 * n], refs[2 * n], refs[2 * n + 1]
        x, y, c = _mesh_pos()
        cps = [_remote(ins[i], outs[i], send_sems, recv_sems, i, (x, y, 1 - c)) for i in range(n)]
        for cp in cps:
            cp.start()
        for cp in cps:
            cp.wait()

    return pl.pallas_call(
        body, name="share_halves", in_specs=[_ANY] * n, out_specs=[_ANY] * n,
        out_shape=[jax.ShapeDtypeStruct(h.shape, h.dtype) for h in halves],
        scratch_shapes=[pltpu.SemaphoreType.DMA((n,)), pltpu.SemaphoreType.DMA((n,))],
    )(*halves)


SHARD_COLS_IN = IN_TOTAL // N_CHIPS
KPE_END = Q_LORA + KV_LORA + ROPE
KPE_PAD = PROJ_EXT - IN_TOTAL


def _by_cols(a):
    return a.transpose(1, 0, 2).reshape(a.shape[1], N_CHIPS * a.shape[2])


def _assemble_early(c_in, c_uq, c_ukv, c_conv):
    return c_in, c_uq, c_ukv, _by_cols(c_conv).astype(F32)


def _assemble_late(c_o, c_pl, c_plg):
    return c_o.reshape(D_MODEL, D_MODEL), _by_cols(c_pl), c_plg.reshape(D_MODEL, D_MODEL)


def _split_late(dw_o, dw_pl, dw_plg):
    chip_major = lambda a: a.reshape(a.shape[0], N_CHIPS, a.shape[1] // N_CHIPS).transpose(1, 0, 2)
    return [dw_o.reshape(N_CHIPS, D_MODEL // N_CHIPS, D_MODEL), chip_major(dw_pl),
            dw_plg.reshape(N_CHIPS, D_MODEL // N_CHIPS, D_MODEL)]


def _local_step(x, p, pos, tgt, gains, early, late_shards, late_gathered, tm, tq):
    c_in, w_uq_e, w_ukv, conv_w = early
    g_in, g_cq, g_ckv, g_q, g_k, g_oa, g_oc, g_pl = gains
    T = x.shape[0]
    zpad = lambda a, n: jnp.concatenate([a, jnp.zeros(a.shape[:-1] + (n,), a.dtype)], axis=-1)
    gq, gk = zpad(g_q, HEAD_PAD - QK_DIM), zpad(g_k, HEAD_PAD - QK_DIM)
    inv_freq = 1.0 / (ROPE_THETA ** (jnp.arange(0, ROPE, 2, dtype=F32) / ROPE))
    invf = jnp.concatenate([inv_freq, inv_freq, jnp.zeros((64,), F32)]).reshape(1, LANES)
    sgn = jnp.concatenate([-jnp.ones((32,), F32), jnp.ones((32,), F32), jnp.zeros((64,), F32)]).reshape(1, LANES)

    (proj, q, k, v, w_in_e), gathered = _fwd_proj(x, pos, g_in, c_in, g_cq, w_uq_e, g_ckv, w_ukv, gq, gk, invf, sgn,
                                                  late_shards, min(2 * tm, T))
    w_o, w_pl, w_plg = _assemble_late(*(gathered if late_shards else late_gathered))
    o, lse = _attn_fwd(q, k, v, tq)
    (dx1, do, delta, dtail, du, dw_o, dw_pl, dw_plg, dg_oa, dg_oc, dg_pl, dconv, loss) = _tail(
        x, o, proj, p, tgt, g_oa, g_oc, g_pl, conv_w, w_o, w_pl, w_plg, tm)
    late_grads = _split_late(dw_o, dw_pl, dw_plg)
    (dq, dk, dv), late_sibling = _attn_bwd(q, k, v, do, lse, delta, tq, late_grads)
    (gx, h, dproj, dw_uq_e, dw_ukv, dg_in, dg_cq, dg_ckv, dgq, dgk) = _bwd_proj(
        x, dx1, pos, proj, dq, dk, dv, dtail, du, g_in, w_in_e, g_cq, w_uq_e, g_ckv, w_ukv, gq, gk, conv_w, invf, sgn, tm)
    wgrads = [dw_uq_e[:, :, :QK_DIM], dw_ukv, *late_grads]
    ggrads = (dg_in, dg_cq, dg_ckv, dgq, dgk, dg_oa, dg_oc, dg_pl)
    return loss, gx, (h, dproj), wgrads, late_sibling, ggrads, dconv


def kernel(x, p, positions, g_in, w_in, g_cq, w_uq, g_ckv, w_ukv, g_q, g_k, conv_w, g_oa, g_oc, w_o, w_pl, w_plg, g_pl, loss_target, m_g_in, m_w_in, m_g_cq, m_w_uq, m_g_ckv, m_w_ukv, m_g_q, m_g_k, m_conv_w, m_g_oa, m_g_oc, m_w_o, m_w_pl, m_w_plg, m_g_pl, v_g_in, v_w_in, v_g_cq, v_w_uq, v_g_ckv, v_w_ukv, v_g_q, v_g_k, v_conv_w, v_g_oa, v_g_oc, v_w_o, v_w_pl, v_w_plg, v_g_pl):
    T = x.shape[1]
    c = lax.axis_index("c")
    chip = 2 * lax.axis_index("x") + lax.axis_index("y")
    gains = [g.reshape(1, -1) for g in (g_in, g_cq, g_ckv, g_q, g_k, g_oa, g_oc, g_pl)]

    transposed = ("w_in", "w_uq")
    early = _assemble_early(*_gather_weights([w_in[0].T, w_uq[0].T, w_ukv[0], conv_w[0]], len(transposed)))

    loss, gx, (h_t, dproj), others_cm, late_sibling, ggrads, dconv = _local_step(
        x[0], p[0, 0], positions.reshape(T, 1), loss_target[0], gains, early, [w_o[0], w_pl[0], w_plg[0]], None, 256, 512)

    small_parts = [a.reshape(-1, LANES) for a in (*ggrads, loss, dconv)]
    small_rows = [a.shape[0] for a in small_parts]
    tile_rows = [-(-r // 8) * 8 for r in small_rows]
    tile_rows[-1] += -sum(tile_rows) % 16
    small = jnp.concatenate([jnp.pad(a, ((0, t - r), (0, 0))) for a, r, t in zip(small_parts, small_rows, tile_rows)])
    n_early = len(others_cm) - len(late_sibling)
    early_sibling, (small_sibling,) = _swap_halves(others_cm[:n_early], [small], "pair_grads")
    chip_parts, chip_small = _add_pair(others_cm, [*early_sibling, *late_sibling], small, small_sibling, c)
    dw_in_e, w_in_sibling, exchanged = _matmul_acc(h_t, dproj, min(4096, T), 512, [*chip_parts, chip_small])
    by_chip = [_scatter_w_in(dw_in_e[None], w_in_sibling[None]), *exchanged[:-1]]
    halves, small_total = _add_chips(by_chip, exchanged[-1])
    other_halves = _share_halves(halves)

    gg, off = [], 0
    for rows, tiled in zip(small_rows, tile_rows):
        gg.append(small_total[off:off + rows].reshape(1, -1))
        off += tiled
    loss_out = gg[8][0, 0]
    conv_total = gg[9].reshape(3, CONV_W)
    conv_g = lax.dynamic_slice(conv_total, (0, chip * (CONV_W // N_CHIPS)), (3, CONV_W // N_CHIPS))
    g_by_name = dict(g_in=gg[0], g_cq=gg[1], g_ckv=gg[2], g_q=gg[3][:, :QK_DIM], g_k=gg[4][:, :QK_DIM], conv_w=conv_g,
                     g_oa=gg[5], g_oc=gg[6], g_pl=gg[7])
    half_by_name = dict(zip(("w_in", "w_uq", "w_ukv", "w_o", "w_pl", "w_plg"), zip(halves, other_halves)))
    weights = dict(g_in=g_in, w_in=w_in, g_cq=g_cq, w_uq=w_uq, g_ckv=g_ckv, w_ukv=w_ukv, g_q=g_q, g_k=g_k,
                   conv_w=conv_w, g_oa=g_oa, g_oc=g_oc, w_o=w_o, w_pl=w_pl, w_plg=w_plg, g_pl=g_pl)
    ms = dict(g_in=m_g_in, w_in=m_w_in, g_cq=m_g_cq, w_uq=m_w_uq, g_ckv=m_g_ckv, w_ukv=m_w_ukv, g_q=m_g_q, g_k=m_g_k,
              conv_w=m_conv_w, g_oa=m_g_oa, g_oc=m_g_oc, w_o=m_w_o, w_pl=m_w_pl, w_plg=m_w_plg, g_pl=m_g_pl)
    vs = dict(g_in=v_g_in, w_in=v_w_in, g_cq=v_g_cq, w_uq=v_w_uq, g_ckv=v_g_ckv, w_ukv=v_w_ukv, g_q=v_g_q, g_k=v_g_k,
              conv_w=v_conv_w, g_oa=v_g_oa, g_oc=v_g_oc, w_o=v_w_o, w_pl=v_w_pl, w_plg=v_w_plg, g_pl=v_g_pl)
    names = list(weights)
    flat = lambda a: a.reshape(-1, a.shape[-1])
    small_names = list(g_by_name)
    small_out = _adamw_small([flat(weights[n]) for n in small_names], [flat(g_by_name[n]) for n in small_names],
                             [flat(ms[n]) for n in small_names], [flat(vs[n]) for n in small_names])
    results = {n: (flat(g_by_name[n]), *(out[i] for out in small_out)) for i, n in enumerate(small_names)}
    for n in half_by_name:
        shard = (lambda a: a[0].T) if n in transposed else flat
        out = _adamw_halves(shard(weights[n]), *half_by_name[n], shard(ms[n]), shard(vs[n]), c, "adamw_" + n,
                            n in transposed)
        results[n] = [a.T for a in out] if n in transposed else out
    per_kind = [[results[n][kind].reshape(weights[n].shape) for n in names] for kind in range(4)]
    return (loss_out, gx.reshape(x.shape), *per_kind[0], *per_kind[1], *per_kind[2], *per_kind[3])
```

```python
import math

import jax
import jax.numpy as jnp
from jax import lax
from jax.experimental import pallas as pl
from jax.experimental.pallas import tpu as pltpu

F32 = jnp.float32
BF16 = jnp.bfloat16

D_MODEL = 1024
N_HEADS = 4
NOPE = 128
ROPE = 64
V_DIM = 128
QK_DIM = NOPE + ROPE
HEAD_PAD = 256
Q_LORA = 256
KV_LORA = 128
ATTN_W = 512
CONV_W = 512
PLE = 256
IN_TOTAL = 3008
PROJ_EXT = 3072
ROPE_THETA = 10000.0
EPS = 1e-6
SCALE = 1.0 / math.sqrt(QK_DIM)
LOG2E = math.log2(math.e)
EXP2_SCALE = SCALE * LOG2E
NEG = -1e30
SOFTMAX_ROWS = 32
SUB_TILE = 256

LR, B1, B2, ADAM_EPS, WD, STEP = 0.001, 0.9, 0.999, 1e-08, 0.01, 10

N_CHIPS = 4
LANES = 128
VMEM_LIMIT = 56 * 1024 * 1024
MESH = pl.DeviceIdType.MESH


def _params(**kw):
    return pltpu.CompilerParams(vmem_limit_bytes=VMEM_LIMIT, **kw)


def _inv_rms(x, n):
    return lax.rsqrt(jnp.sum(x * x, axis=-1, keepdims=True) / n + EPS)


def _lane_sum(a):
    folded = a[:, 0:LANES]
    for c0 in range(LANES, a.shape[1], LANES):
        folded = folded + a[:, c0:c0 + LANES]
    head = folded.astype(BF16)
    tail = (folded - head.astype(F32)).astype(BF16)
    return _dot(jnp.concatenate([head, tail], axis=1), jnp.ones((2 * LANES, LANES), BF16))


def _inv_rms_mxu(x):
    return lax.rsqrt(_lane_sum(x * x) / x.shape[1] + EPS)


def _rep(r, width):
    return r if width == LANES else jnp.tile(r, (1, width // LANES))


def _sigmoid(z):
    return jax.nn.sigmoid(z)


def _swap_rope_halves(b):
    lane = lax.broadcasted_iota(jnp.int32, b.shape, 1)
    swapped = jnp.where(lane < 32, pltpu.roll(b, 96, 1), pltpu.roll(b, 32, 1))
    return jnp.where(lane < ROPE, swapped, 0.0)


def _dot(a, b):
    return jnp.dot(a, b, preferred_element_type=F32)


def _dot_nt(a, b):
    return lax.dot_general(a, b, (((1,), (1,)), ((), ())), preferred_element_type=F32)


def _dot_tn(a, b):
    return lax.dot_general(a, b, (((0,), (0,)), ((), ())), preferred_element_type=F32)


def _colsum(a):
    return jnp.sum(a, axis=0, keepdims=True)


def _store_transposed(src_ref, dst_ref):
    r, c = src_ref.shape
    for r0 in range(0, r, LANES):
        h = min(LANES, r - r0)
        for c0 in range(0, c, LANES):
            w = min(LANES, c - c0)
            piece = src_ref[r0:r0 + h, c0 + w - LANES:c0 + w]
            if h < LANES:
                piece = jnp.concatenate([piece, jnp.zeros((LANES - h, LANES), piece.dtype)], axis=0)
            dst_ref[c0:c0 + w, r0:r0 + h] = piece.T[LANES - w:, 0:h].astype(dst_ref.dtype)


def _full(shape):
    return pl.BlockSpec(shape, lambda *_: (0,) * len(shape))


def _round_robin(chains, width):
    waiting, active = list(chains), []
    while waiting or active:
        while waiting and len(active) < width:
            active.append(waiting.pop(0))
        for chain in list(active):
            if next(chain, _DONE) is _DONE:
                active.remove(chain)


_DONE = object()


def _rope_tables(pos_ref, invf_ref, sgn_ref):
    pos = jnp.broadcast_to(pos_ref[...].astype(F32), (LANES, pos_ref.shape[1])).T
    ang = pos * invf_ref[...]
    return jnp.cos(ang), jnp.sin(ang) * sgn_ref[...]


def _fwd_proj(x, pos, g_in, c_in, g_cq, w_uq, g_ckv, w_ukv, gq, gk, invf, sgn, late_shards, tm):
    T = x.shape[0]
    nt = T // tm
    n_late = len(late_shards)
    ts = min(SUB_TILE, tm)

    def body(x_ref, pos_ref, g_in_ref, c_in_ref, g_cq_ref, w_uq_ref, g_ckv_ref, w_ukv_ref, gq_ref, gk_ref,
             invf_ref, sgn_ref, *rest):
        late_in, (proj_ref, q_ref, k_ref, v_ref, w_in_ref) = rest[:n_late], rest[n_late:n_late + 5]
        late_out, late_scratch = rest[n_late + 5:2 * n_late + 5], rest[2 * n_late + 5:]
        i = pl.program_id(0)

        @pl.when(i == 0)
        def _():
            w_in_ref[:, 0:KPE_END] = c_in_ref[0, :, 0:KPE_END]
            w_in_ref[:, KPE_END:KPE_END + KPE_PAD] = jnp.zeros((D_MODEL, KPE_PAD), BF16)
            w_in_ref[:, KPE_END + KPE_PAD:SHARD_COLS_IN + KPE_PAD] = c_in_ref[0, :, KPE_END:SHARD_COLS_IN]
            for chip in range(1, N_CHIPS):
                w_in_ref[:, SHARD_COLS_IN * chip + KPE_PAD:SHARD_COLS_IN * (chip + 1) + KPE_PAD] = c_in_ref[chip]

        if n_late:
            start, forward, drain = _gather_steps([s.shape for s in late_shards], late_in, late_out,
                                                  late_scratch[:n_late], *late_scratch[n_late:])
            pl.when(i == 0)(start)
            pl.when(i == nt // 2)(forward)

        for r0 in range(0, tm, ts):
            rows = slice(r0, r0 + ts)
            xv = x_ref[rows, :]
            h = (xv * _rep(_inv_rms_mxu(xv), D_MODEL) * g_in_ref[...]).astype(BF16)
            lat = _dot(h, w_in_ref[:, 0:512])
            proj_ref[rows, 0:512] = lat
            c_q = lat[:, 0:Q_LORA]
            cqn = (c_q * _rep(_inv_rms_mxu(c_q), Q_LORA) * g_cq_ref[...]).astype(BF16)
            c_kv = lat[:, Q_LORA:Q_LORA + KV_LORA]
            ckvn = (c_kv * _inv_rms_mxu(c_kv) * g_ckv_ref[...]).astype(BF16)
            kpe = lat[:, 384:512]
            kpe_sq = kpe * kpe
            cos_b, sin_b = _rope_tables(pos_ref.at[:, rows], invf_ref, sgn_ref)
            gq_a, gq_b = gq_ref[:, 0:NOPE], gq_ref[:, NOPE:HEAD_PAD]
            gk_a, gk_b = gk_ref[:, 0:NOPE], gk_ref[:, NOPE:HEAD_PAD]

            def projections(rows=rows, h=h):
                for c0 in range(512, PROJ_EXT, 512):
                    proj_ref[rows, c0:c0 + 512] = _dot(h, w_in_ref[:, c0:c0 + 512])
                    yield

            def queries(hd, rows=rows, cqn=cqn, cos_b=cos_b, sin_b=sin_b, gq_a=gq_a, gq_b=gq_b):
                qh = _dot(cqn, w_uq_ref[hd])
                yield
                a, b = qh[:, 0:NOPE], qh[:, NOPE:HEAD_PAD]
                r = lax.rsqrt(_lane_sum(a * a + b * b) / QK_DIM + EPS)
                yield
                bn = b * r * gq_b
                q_ref[hd, rows, 0:NOPE] = (a * r * gq_a).astype(BF16)
                q_ref[hd, rows, NOPE:HEAD_PAD] = (bn * cos_b + _swap_rope_halves(bn) * sin_b).astype(BF16)
                yield

            def keys(hd, rows=rows, ckvn=ckvn, kpe=kpe, kpe_sq=kpe_sq, cos_b=cos_b, sin_b=sin_b, gk_a=gk_a, gk_b=gk_b):
                kvh = _dot(ckvn, w_ukv_ref[hd])
                yield
                ka = kvh[:, 0:NOPE]
                rk = lax.rsqrt(_lane_sum(ka * ka + kpe_sq) / QK_DIM + EPS)
                yield
                kbn = kpe * rk * gk_b
                k_ref[hd, rows, 0:NOPE] = (ka * rk * gk_a).astype(BF16)
                k_ref[hd, rows, NOPE:HEAD_PAD] = (kbn * cos_b + _swap_rope_halves(kbn) * sin_b).astype(BF16)
                v_ref[hd, rows, 0:V_DIM] = kvh[:, NOPE:HEAD_PAD].astype(BF16)
                v_ref[hd, rows, V_DIM:2 * V_DIM] = jnp.ones((ts, V_DIM), BF16)
                yield

            chains = [projections()]
            for hd in range(N_HEADS):
                chains += [queries(hd), keys(hd)]
            _round_robin(chains, 4)

        if n_late:
            pl.when(i == nt - 1)(drain)

    row = lambda i: (i, 0)
    head_rows = lambda i: (0, i, 0)
    outs = pl.pallas_call(
        body, name="fwd_proj", grid=(nt,),
        in_specs=[pl.BlockSpec((tm, D_MODEL), row), pl.BlockSpec((1, tm), lambda i: (0, i)), _full((1, D_MODEL)),
                  _full((N_CHIPS, D_MODEL, SHARD_COLS_IN)), _full((1, Q_LORA)), _full((N_HEADS, Q_LORA, HEAD_PAD)),
                  _full((1, KV_LORA)), _full((N_HEADS, KV_LORA, HEAD_PAD)), _full((1, HEAD_PAD)), _full((1, HEAD_PAD)),
                  _full((1, LANES)), _full((1, LANES))] + [_full(s.shape) for s in late_shards],
        out_specs=[pl.BlockSpec((tm, PROJ_EXT), row), pl.BlockSpec((N_HEADS, tm, HEAD_PAD), head_rows),
                   pl.BlockSpec((N_HEADS, tm, HEAD_PAD), head_rows), pl.BlockSpec((N_HEADS, tm, 2 * V_DIM), head_rows),
                   _full((D_MODEL, PROJ_EXT))] + [_ANY] * n_late,
        out_shape=[jax.ShapeDtypeStruct((T, PROJ_EXT), F32), jax.ShapeDtypeStruct((N_HEADS, T, HEAD_PAD), BF16),
                   jax.ShapeDtypeStruct((N_HEADS, T, HEAD_PAD), BF16), jax.ShapeDtypeStruct((N_HEADS, T, 2 * V_DIM), BF16),
                   jax.ShapeDtypeStruct((D_MODEL, PROJ_EXT), BF16)] + _gathered_shapes([s.shape for s in late_shards]),
        scratch_shapes=_gather_scratch([s.shape for s in late_shards]) if n_late else [],
        compiler_params=_params(dimension_semantics=("arbitrary",)),
    )(x, pos, g_in, c_in, g_cq, w_uq, g_ckv, w_ukv, gq, gk, invf, sgn, *late_shards)
    return outs[:5], outs[5:]


def _chunk_pipeline(n_loop, lag, matmuls, pointwise, accumulate, last):
    slots = lag + 1

    def iteration(t, slot, pending=True):
        matmuls(jnp.minimum(t + lag, n_loop), (slot + lag) % slots)
        if pending:
            accumulate(t - lag, (slot + 1) % slots, False)
        pointwise(t, slot, False)

    def finish(slot, pending):
        for back in range(pending, 0, -1):
            accumulate(n_loop - back, (slot - back) % slots, False)
        pointwise(n_loop, slot, True)
        accumulate(n_loop, slot, True)
        last()

    for u in range(lag):
        matmuls(jnp.minimum(u, n_loop), u)
    for u in range(lag):
        pl.when(u < n_loop)(lambda u=u: iteration(u, u, pending=False))

    n_main = jnp.maximum(n_loop - lag, 0)

    def unrolled(tt, carry):
        for j in range(slots):
            iteration(lag + slots * tt + j, (lag + j) % slots)
        return carry

    lax.fori_loop(0, n_main // slots, unrolled, 0)
    rest = lax.rem(n_main, slots)
    t0 = n_loop - rest

    for r in range(slots):
        @pl.when(jnp.logical_and(n_loop >= lag, rest == r))
        def _():
            for j in range(r):
                iteration(t0 + j, (lag + j) % slots)
            finish((lag + r) % slots, lag)

    for short in range(lag):
        pl.when(n_loop == short)(lambda short=short: finish(short, short))


def _attn_fwd(q, k, v, tq):
    T = q.shape[1]
    tk = tq
    rc = min(SOFTMAX_ROWS, tq)

    def body(q_ref, k_ref, v_ref, o_ref, lse_ref, s0, s1, s2, p0, p1, p2, a0, a1, a2, m_ref, acc_ref):
        qi = pl.program_id(1)
        s_buf, p_buf, a_buf = (s0, s1, s2), (p0, p1, p2), (a0, a1, a2)

        def scores(t, slot):
            ks = pl.multiple_of(t * tk, tk)
            s_buf[slot][...] = _dot_nt(q_ref[0], k_ref[0, pl.ds(ks, tk), :])

        def blocks(masked):
            return ((0, tq // 2, tk // 2), (tq // 2, tq // 2, tk)) if masked else ((0, tq, tk),)

        def values(t, slot, masked):
            ks = pl.multiple_of(t * tk, tk)
            for q0, nq, nk in blocks(masked):
                rows = slice(q0, q0 + nq)
                acc_ref[rows, :] = (acc_ref[rows, :] * a_buf[slot][rows, :]
                                    + _dot(p_buf[slot][rows, 0:nk], v_ref[0, pl.ds(ks, nk), :]))

        def softmax(t, slot, masked):
            for q0, nq, nk in blocks(masked):
                rows = slice(q0, q0 + nq)
                s_all = s_buf[slot][rows, 0:nk]
                if masked:
                    row = lax.broadcasted_iota(jnp.int32, (nq, nk), 0) + q0
                    col = lax.broadcasted_iota(jnp.int32, (nq, nk), 1)
                    s_all = jnp.where(col <= row, s_all, NEG)
                    s_buf[slot][rows, 0:nk] = s_all
                m_old = m_ref[rows, :]
                m_new = jnp.maximum(m_old, jnp.max(s_all, axis=1, keepdims=True))
                a_buf[slot][rows, :] = jnp.exp2((m_old - m_new) * EXP2_SCALE)
                m_ref[rows, :] = m_new
                for r0 in range(0, nq, rc):
                    s = s_buf[slot][q0 + r0:q0 + r0 + rc, 0:nk]
                    p_buf[slot][q0 + r0:q0 + r0 + rc, 0:nk] = jnp.exp2((s - m_new[r0:r0 + rc, :]) * EXP2_SCALE).astype(BF16)

        def last():
            l = acc_ref[:, V_DIM:2 * V_DIM]
            o_ref[...] = acc_ref[:, 0:V_DIM] / l
            lse_ref[0] = (m_ref[...] * SCALE + jnp.log(l)).T[0:1, :]

        m_ref[...] = jnp.full_like(m_ref, NEG)
        acc_ref[...] = jnp.zeros_like(acc_ref)
        _chunk_pipeline(qi, 2, scores, softmax, values, last)

    return pl.pallas_call(
        body, name="attn_fwd", grid=(N_HEADS, T // tq),
        in_specs=[pl.BlockSpec((1, tq, HEAD_PAD), lambda h, i: (h, i, 0)),
                  pl.BlockSpec((1, T, HEAD_PAD), lambda h, i: (h, 0, 0)),
                  pl.BlockSpec((1, T, 2 * V_DIM), lambda h, i: (h, 0, 0))],
        out_specs=[pl.BlockSpec((tq, V_DIM), lambda h, i: (i, h)),
                   pl.BlockSpec((1, 1, tq), lambda h, i: (h, 0, i))],
        out_shape=[jax.ShapeDtypeStruct((T, ATTN_W), F32), jax.ShapeDtypeStruct((N_HEADS, 1, T), F32)],
        scratch_shapes=[pltpu.VMEM((tq, tk), F32)] * 3 + [pltpu.VMEM((tq, tk), BF16)] * 3
                       + [pltpu.VMEM((tq, 1), F32)] * 4 + [pltpu.VMEM((tq, 2 * V_DIM), F32)],
        compiler_params=_params(dimension_semantics=("arbitrary", "arbitrary")),
    )(q, k, v)


def _tail(x, o, proj, p, tgt, g_oa, g_oc, g_pl, conv_w, w_o, w_pl, w_plg, tm):
    T = x.shape[0]
    nt = T // tm

    def body(x_ref, o_ref, za_ref, cb_ref, cc_ref, cx_ref, zc_ref, cch_ref, cxh_ref, p_ref, tgt_ref,
             g_oa_ref, g_oc_ref, g_pl_ref, cw_ref, w_o_ref, w_pl_ref, w_plg_ref,
             dx1_ref, do_ref, delta_ref, dtail_ref, du_ref,
             dw_o_ref, dw_pl_ref, dw_plg_ref, dg_oa_ref, dg_oc_ref, dg_pl_ref, dcw_ref, loss_ref):
        i = pl.program_id(0)

        @pl.when(i == 0)
        def _():
            for r in (dw_o_ref, dw_pl_ref, dw_plg_ref, dg_oa_ref, dg_oc_ref, dg_pl_ref, dcw_ref, loss_ref):
                r[...] = jnp.zeros_like(r)

        g_oa, g_oc, g_pl = g_oa_ref[...], g_oc_ref[...], g_pl_ref[...]
        w0, w1, w2 = cw_ref[0:1, :], cw_ref[1:2, :], cw_ref[2:3, :]

        xv, ov, za, cb, zc = x_ref[...], o_ref[...], za_ref[...], cb_ref[...], zc_ref[...]
        pb = p_ref[...].astype(BF16)
        pp = _dot(pb, w_pl_ref[...])

        sa = _sigmoid(za)
        silu_a = za * sa
        ga = ov * silu_a
        ra = _inv_rms(ga, ATTN_W)
        xa = ga * ra
        ya = (xa * g_oa).astype(BF16)
        x1_a = _dot(ya, w_o_ref[0:ATTN_W, :])
        v = cc_ref[...] * cx_ref[...]
        not_first = jnp.where(i > 0, 1.0, 0.0)
        hv6 = cch_ref[6:7, :] * cxh_ref[6:7, :] * not_first
        hv7 = cch_ref[7:8, :] * cxh_ref[7:8, :] * not_first
        row = lax.broadcasted_iota(jnp.int32, v.shape, 0)
        v1 = jnp.where(row == 0, hv7, pltpu.roll(v, 1, 0))
        v2 = jnp.where(row == 0, hv6, jnp.where(row == 1, hv7, pltpu.roll(v, 2, 0)))
        u = w0 * v2 + w1 * v1 + w2 * v
        sc = _sigmoid(zc)
        silu_c = zc * sc
        gc = cb * u * silu_c
        rc = _inv_rms(gc, CONV_W)
        xc = gc * rc
        yc = (xc * g_oc).astype(BF16)
        x1 = xv + (x1_a + _dot(yc, w_o_ref[ATTN_W:D_MODEL, :]))
        r1 = _inv_rms(x1, D_MODEL)
        xh1 = x1 * r1
        n1 = (xh1 * g_pl).astype(BF16)
        gate = _sigmoid(_dot(n1, w_plg_ref[...]))
        err = x1 + gate * pp - tgt_ref[...]
        loss_ref[...] += 0.5 * jnp.sum(err * err) / D_MODEL
        dy = err / D_MODEL

        dpp = (dy * gate).astype(BF16)
        da = (dy * pp * gate * (1.0 - gate)).astype(BF16)
        dn1 = _dot_nt(da, w_plg_ref[...])
        dw_pl_ref[...] += _dot_tn(pb, dpp)
        dw_plg_ref[...] += _dot_tn(n1, da)
        dg_pl_ref[...] += _colsum(dn1 * xh1)
        dxh = dn1 * g_pl
        dx1 = dy + r1 * (dxh - xh1 * (jnp.sum(dxh * xh1, axis=-1, keepdims=True) / D_MODEL))
        dx1_ref[...] = dx1
        dx1b = dx1.astype(BF16)
        dya = _dot_nt(dx1b, w_o_ref[0:ATTN_W, :])
        dyc = _dot_nt(dx1b, w_o_ref[ATTN_W:D_MODEL, :])

        dw_o_ref[0:ATTN_W, :] += _dot_tn(ya, dx1b)
        dg_oa_ref[...] += _colsum(dya * xa)
        dxa = dya * g_oa
        dga = ra * (dxa - xa * (jnp.sum(dxa * xa, axis=-1, keepdims=True) / ATTN_W))
        do = (dga * silu_a).astype(BF16)
        do_ref[...] = do
        dof = do.astype(F32) * ov
        for hd in range(N_HEADS):
            delta_ref[hd] = _lane_sum(dof[:, hd * V_DIM:(hd + 1) * V_DIM]).T[0:1, :]
        dtail_ref[:, 0:512] = (dga * ov * (sa * (1.0 + za * (1.0 - sa)))).astype(BF16)

        dw_o_ref[ATTN_W:D_MODEL, :] += _dot_tn(yc, dx1b)
        dg_oc_ref[...] += _colsum(dyc * xc)
        dxc = dyc * g_oc
        dgc = rc * (dxc - xc * (jnp.sum(dxc * xc, axis=-1, keepdims=True) / CONV_W))
        dtail_ref[:, 512:1024] = (dgc * u * silu_c).astype(BF16)
        du = dgc * cb * silu_c
        du_ref[...] = du
        dtail_ref[:, 1024:1536] = (dgc * cb * u * (sc * (1.0 + zc * (1.0 - sc)))).astype(BF16)
        dcw_ref[0:1, :] += _colsum(du * v2)
        dcw_ref[1:2, :] += _colsum(du * v1)
        dcw_ref[2:3, :] += _colsum(du * v)

    row = lambda i: (i, 0)
    col = lambda c: (lambda i: (i, c))
    halo = lambda c: (lambda i: (jnp.maximum(i * (tm // 8) - 1, 0), c))
    in_specs = [pl.BlockSpec((tm, D_MODEL), row), pl.BlockSpec((tm, ATTN_W), row)]
    in_specs += [pl.BlockSpec((tm, 512), col(c)) for c in (1, 2, 3, 4, 5)]
    in_specs += [pl.BlockSpec((8, 512), halo(3)), pl.BlockSpec((8, 512), halo(4))]
    in_specs += [pl.BlockSpec((tm, PLE), row), pl.BlockSpec((tm, D_MODEL), row),
                 _full((1, ATTN_W)), _full((1, CONV_W)), _full((1, D_MODEL)), _full((3, CONV_W)),
                 _full((D_MODEL, D_MODEL)), _full((PLE, D_MODEL)), _full((D_MODEL, D_MODEL))]
    out_specs = [pl.BlockSpec((tm, D_MODEL), row), pl.BlockSpec((tm, ATTN_W), row),
                 pl.BlockSpec((N_HEADS, 1, tm), lambda i: (0, 0, i)), pl.BlockSpec((tm, 1536), row),
                 pl.BlockSpec((tm, CONV_W), row),
                 _full((D_MODEL, D_MODEL)), _full((PLE, D_MODEL)), _full((D_MODEL, D_MODEL)),
                 _full((1, ATTN_W)), _full((1, CONV_W)), _full((1, D_MODEL)), _full((3, CONV_W)), _full((1, LANES))]
    out_shape = [jax.ShapeDtypeStruct((T, D_MODEL), F32), jax.ShapeDtypeStruct((T, ATTN_W), BF16),
                 jax.ShapeDtypeStruct((N_HEADS, 1, T), F32), jax.ShapeDtypeStruct((T, 1536), BF16),
                 jax.ShapeDtypeStruct((T, CONV_W), F32),
                 jax.ShapeDtypeStruct((D_MODEL, D_MODEL), F32), jax.ShapeDtypeStruct((PLE, D_MODEL), F32),
                 jax.ShapeDtypeStruct((D_MODEL, D_MODEL), F32),
                 jax.ShapeDtypeStruct((1, ATTN_W), F32), jax.ShapeDtypeStruct((1, CONV_W), F32),
                 jax.ShapeDtypeStruct((1, D_MODEL), F32), jax.ShapeDtypeStruct((3, CONV_W), F32),
                 jax.ShapeDtypeStruct((1, LANES), F32)]
    return pl.pallas_call(
        body, name="tail", grid=(nt,), in_specs=in_specs, out_specs=out_specs, out_shape=out_shape,
        compiler_params=_params(dimension_semantics=("arbitrary",)),
    )(x, o, proj, proj, proj, proj, proj, proj, proj, p, tgt, g_oa, g_oc, g_pl, conv_w, w_o, w_pl, w_plg)


def _attn_bwd(q, k, v, do, lse_row, delta_row, tk, swap):
    T = q.shape[1]
    tq = tk
    nq = T // tq
    rc = min(SOFTMAX_ROWS, tk)
    hk, hq = tk // 2, tq // 2
    n_swap = len(swap)

    def body(q_ref, k_ref, v_ref, do_ref, lse_ref, dl_ref, *rest):
        swap_in, (dq_ref, dk_ref, dv_ref), rest = rest[:n_swap], rest[n_swap:n_swap + 3], rest[n_swap + 3:]
        swap_out, (s0, s1, d0, d1, p0, p1, g0, g1, dk_acc, dv_acc), sems = rest[:n_swap], rest[n_swap:n_swap + 10], rest[n_swap + 10:]
        kj = pl.program_id(1)
        s_buf, dp_buf, p_buf, g_buf = (s0, s1), (d0, d1), (p0, p1), (g0, g1)

        if n_swap:
            start, drain = _swap_steps(n_swap, swap_in, swap_out, *sems)
            pl.when(jnp.logical_and(pl.program_id(0) == 0, kj == 0))(start)

        @pl.when(kj == 0)
        def _():
            dq_ref[...] = jnp.zeros_like(dq_ref)

        def q_start(t):
            return pl.multiple_of((nq - 1 - t) * tq, tq)

        def matmuls(t, slot):
            qs = q_start(t)
            s_buf[slot][...] = _dot_nt(k_ref[0], q_ref[0, pl.ds(qs, tq), :])
            dp_buf[slot][...] = _dot_nt(v_ref[0], do_ref[pl.ds(qs, tq), :])

        def pointwise(t, slot, masked):
            qs = q_start(t)
            lse2 = lse_ref[0, :, pl.ds(qs, tq)] * LOG2E
            dl = dl_ref[0, :, pl.ds(qs, tq)]
            for r0 in range(0, tk, rc):
                c0 = r0 // hk * hq if masked else 0
                rows, cols = slice(r0, r0 + rc), slice(c0, tq)
                st = s_buf[slot][rows, cols]
                if masked:
                    row = lax.broadcasted_iota(jnp.int32, (rc, tq - c0), 0) + r0
                    col = lax.broadcasted_iota(jnp.int32, (rc, tq - c0), 1) + c0
                    st = jnp.where(row <= col, st, NEG)
                pt = jnp.exp2(st * EXP2_SCALE - lse2[:, cols])
                p_buf[slot][rows, cols] = pt.astype(BF16)
                g_buf[slot][rows, cols] = (pt * (dp_buf[slot][rows, cols] - dl[:, cols]) * SCALE).astype(BF16)

        def accumulate(t, slot, masked):
            qs = q_start(t)
            p, g = p_buf[slot], g_buf[slot]
            if not masked:
                dv_acc[...] += _dot(p[...], do_ref[pl.ds(qs, tq), :])
                dk_acc[...] += _dot(g[...], q_ref[0, pl.ds(qs, tq), :])
                dq_ref[0, pl.ds(qs, tq), :] += _dot_tn(g[...], k_ref[0])
                return
            q2 = pl.multiple_of(qs + hq, hq)
            dv_acc[0:hk, :] += _dot(p[0:hk, :], do_ref[pl.ds(qs, tq), :])
            dv_acc[hk:tk, :] += _dot(p[hk:tk, hq:tq], do_ref[pl.ds(q2, hq), :])
            dk_acc[0:hk, :] += _dot(g[0:hk, :], q_ref[0, pl.ds(qs, tq), :])
            dk_acc[hk:tk, :] += _dot(g[hk:tk, hq:tq], q_ref[0, pl.ds(q2, hq), :])
            dq_ref[0, pl.ds(qs, hq), :] += _dot_tn(g[0:hk, 0:hq], k_ref[0, 0:hk, :])
            dq_ref[0, pl.ds(q2, hq), :] += _dot_tn(g[:, hq:tq], k_ref[0])

        def last():
            dk_ref[0] = dk_acc[...]
            dv_ref[0] = dv_acc[...]

        dk_acc[...] = jnp.zeros_like(dk_acc)
        dv_acc[...] = jnp.zeros_like(dv_acc)
        _chunk_pipeline(nq - 1 - kj, 1, matmuls, pointwise, accumulate, last)

        if n_swap:
            pl.when(jnp.logical_and(pl.program_id(0) == N_HEADS - 1, kj == T // tk - 1))(drain)

    outs = pl.pallas_call(
        body, name="attn_bwd", grid=(N_HEADS, T // tk),
        in_specs=[pl.BlockSpec((1, T, HEAD_PAD), lambda h, j: (h, 0, 0)),
                  pl.BlockSpec((1, tk, HEAD_PAD), lambda h, j: (h, j, 0)),
                  pl.BlockSpec((1, tk, V_DIM), lambda h, j: (h, j, 0)),
                  pl.BlockSpec((T, V_DIM), lambda h, j: (0, h)),
                  pl.BlockSpec((1, 1, T), lambda h, j: (h, 0, 0)),
                  pl.BlockSpec((1, 1, T), lambda h, j: (h, 0, 0))] + [_ANY] * n_swap,
        out_specs=[pl.BlockSpec((1, T, HEAD_PAD), lambda h, j: (h, 0, 0)),
                   pl.BlockSpec((1, tk, HEAD_PAD), lambda h, j: (h, j, 0)),
                   pl.BlockSpec((1, tk, V_DIM), lambda h, j: (h, j, 0))] + [_ANY] * n_swap,
        out_shape=[jax.ShapeDtypeStruct((N_HEADS, T, HEAD_PAD), F32), jax.ShapeDtypeStruct((N_HEADS, T, HEAD_PAD), F32),
                   jax.ShapeDtypeStruct((N_HEADS, T, V_DIM), F32)] + _swapped_shapes(swap, []),
        scratch_shapes=[pltpu.VMEM((tk, tq), F32)] * 4 + [pltpu.VMEM((tk, tq), BF16)] * 4
                       + [pltpu.VMEM((tk, HEAD_PAD), F32), pltpu.VMEM((tk, V_DIM), F32)]
                       + ([pltpu.SemaphoreType.DMA((n_swap,))] * 2 if n_swap else []),
        compiler_params=_params(dimension_semantics=("arbitrary", "arbitrary")),
    )(q, k, v, do, lse_row, delta_row, *swap)
    return outs[:3], outs[3:]


def _bwd_proj(x, dx1, pos, proj, dq, dk, dv, dtail, du, g_in, w_in, g_cq, w_uq, g_ckv, w_ukv, gq, gk, conv_w,
              invf, sgn, tm):
    T = x.shape[0]
    nt = T // tm

    ts = min(SUB_TILE, tm)

    def body(x_ref, dx1_ref, pos_ref, lat_ref, cc_ref, cx_ref, dq_ref, dk_ref, dv_ref, dtail_ref, du_ref, dun_ref, *rest):
        consts, (gx_ref, h_ref, dproj_ref), sums = rest[:11], rest[11:14], rest[14:]
        cw_ref = consts[8]
        i = pl.program_id(0)

        @pl.when(i == 0)
        def _():
            for r in sums:
                r[...] = jnp.zeros_like(r)

        du_v = du_ref[...]
        not_last = jnp.where(i < nt - 1, 1.0, 0.0)
        nx0 = dun_ref[0:1, :] * not_last
        nx1 = dun_ref[1:2, :] * not_last
        row = lax.broadcasted_iota(jnp.int32, du_v.shape, 0)
        du1 = jnp.where(row == tm - 1, nx0, pltpu.roll(du_v, tm - 1, 0))
        du2 = jnp.where(row == tm - 2, nx0, jnp.where(row == tm - 1, nx1, pltpu.roll(du_v, tm - 2, 0)))
        dvc = cw_ref[2:3, :] * du_v + cw_ref[1:2, :] * du1 + cw_ref[0:1, :] * du2
        dproj_ref[:, 1536:2048] = (dvc * cx_ref[...]).astype(BF16)
        dproj_ref[:, 2048:2560] = (dvc * cc_ref[...]).astype(BF16)

        for r0 in range(0, tm, ts):
            rows = slice(r0, r0 + ts)
            work(x_ref.at[rows, :], dx1_ref.at[rows, :], pos_ref.at[:, rows], lat_ref.at[rows, :],
                 dq_ref.at[:, rows, :], dk_ref.at[:, rows, :], dv_ref.at[:, rows, :], dtail_ref.at[rows, :], *consts,
                 gx_ref.at[rows, :], h_ref.at[:, rows], dproj_ref.at[rows, :], *sums)

    def work(x_ref, dx1_ref, pos_ref, lat_ref, dq_ref, dk_ref, dv_ref, dtail_ref,
             g_in_ref, w_in_ref, g_cq_ref, w_uq_ref, g_ckv_ref, w_ukv_ref, gq_ref, gk_ref, cw_ref, invf_ref, sgn_ref,
             gx_ref, h_ref, dproj_ref, dw_uq_ref, dw_ukv_ref, dg_in_ref, dg_cq_ref, dg_ckv_ref, dgq_ref, dgk_ref):
        xv = x_ref[...]
        r0 = _rep(_inv_rms_mxu(xv), D_MODEL)
        xh0 = xv * r0
        g_in = g_in_ref[...]
        h_ref[...] = (xh0 * g_in).astype(BF16).T

        c_q = lat_ref[:, 0:Q_LORA]
        rq = _rep(_inv_rms_mxu(c_q), Q_LORA)
        xq = c_q * rq
        g_cq = g_cq_ref[...]
        cqn = (xq * g_cq).astype(BF16)
        c_kv = lat_ref[:, Q_LORA:Q_LORA + KV_LORA]
        rkv = _inv_rms_mxu(c_kv)
        xkv = c_kv * rkv
        g_ckv = g_ckv_ref[...]
        ckvn = (xkv * g_ckv).astype(BF16)
        kpe = lat_ref[:, 384:512]
        kpe_sq = kpe * kpe
        cos_b, sin_b = _rope_tables(pos_ref, invf_ref, sgn_ref)
        gq_a, gq_b = gq_ref[:, 0:NOPE], gq_ref[:, NOPE:HEAD_PAD]
        gk_a, gk_b = gk_ref[:, 0:NOPE], gk_ref[:, NOPE:HEAD_PAD]

        dproj_ref[:, 512:1536] = dtail_ref[:, 0:1024]
        dproj_ref[:, 2560:3072] = dtail_ref[:, 1024:1536]

        def dh_part(c0):
            return _dot_nt(dproj_ref[:, c0:c0 + 512], w_in_ref[:, c0:c0 + 512])

        later_chunks = ((512,), (1024,), (1536, 2048), (2560,))
        dh = jnp.zeros((ts, D_MODEL), F32)
        acc = dict(dh=dh, dkpe=jnp.zeros((ts, LANES), F32), dcqn=jnp.zeros((ts, Q_LORA), F32),
                   dckvn=jnp.zeros((ts, KV_LORA), F32))

        def dh_chunks():
            for chunks in later_chunks:
                for chunk in chunks:
                    acc["dh"] = acc["dh"] + dh_part(chunk)
                    yield

        def queries(hd):
            qh = _dot(cqn, w_uq_ref[hd])
            yield
            a, b = qh[:, 0:NOPE], qh[:, NOPE:HEAD_PAD]
            r = lax.rsqrt(_lane_sum(a * a + b * b) / QK_DIM + EPS)
            yield
            xa, xb = a * r, b * r
            dan = dq_ref[hd, :, 0:NOPE]
            dbr = dq_ref[hd, :, NOPE:HEAD_PAD]
            dbn = dbr * cos_b + _swap_rope_halves(dbr * sin_b)
            yield
            dgq_ref[:, 0:NOPE] += _colsum(dan * xa)
            dgq_ref[:, NOPE:HEAD_PAD] += _colsum(dbn * xb)
            dxa, dxb = dan * gq_a, dbn * gq_b
            cq = _lane_sum(dxa * xa + dxb * xb) / QK_DIM
            yield
            dqh = jnp.concatenate([r * (dxa - xa * cq), r * (dxb - xb * cq)], axis=-1).astype(BF16)
            yield
            dw_uq_ref[hd] += _dot_tn(cqn, dqh)
            yield
            acc["dcqn"] = acc["dcqn"] + _dot_nt(dqh, w_uq_ref[hd])
            yield

        def keys(hd):
            kvh = _dot(ckvn, w_ukv_ref[hd])
            yield
            ka = kvh[:, 0:NOPE]
            rk = lax.rsqrt(_lane_sum(ka * ka + kpe_sq) / QK_DIM + EPS)
            yield
            xka, xkb = ka * rk, kpe * rk
            dkan = dk_ref[hd, :, 0:NOPE]
            dkbr = dk_ref[hd, :, NOPE:HEAD_PAD]
            dkbn = dkbr * cos_b + _swap_rope_halves(dkbr * sin_b)
            yield
            dgk_ref[:, 0:NOPE] += _colsum(dkan * xka)
            dgk_ref[:, NOPE:HEAD_PAD] += _colsum(dkbn * xkb)
            dxka, dxkb = dkan * gk_a, dkbn * gk_b
            ck = _lane_sum(dxka * xka + dxkb * xkb) / QK_DIM
            yield
            acc["dkpe"] = acc["dkpe"] + rk * (dxkb - xkb * ck)
            dkvh = jnp.concatenate([rk * (dxka - xka * ck), dv_ref[hd]], axis=-1).astype(BF16)
            yield
            dw_ukv_ref[hd] += _dot_tn(ckvn, dkvh)
            yield
            acc["dckvn"] = acc["dckvn"] + _dot_nt(dkvh, w_ukv_ref[hd])
            yield

        chains = [dh_chunks()]
        for hd in range(N_HEADS):
            chains += [queries(hd), keys(hd)]
        _round_robin(chains, 5)
        dh, dkpe, dcqn, dckvn = acc["dh"], acc["dkpe"], acc["dcqn"], acc["dckvn"]

        dg_cq_ref[...] += _colsum(dcqn * xq)
        dxq = dcqn * g_cq
        dproj_ref[:, 0:Q_LORA] = (rq * (dxq - xq * _rep(_lane_sum(dxq * xq) / Q_LORA, Q_LORA))).astype(BF16)
        dg_ckv_ref[...] += _colsum(dckvn * xkv)
        dxkv = dckvn * g_ckv
        dproj_ref[:, 256:384] = (rkv * (dxkv - xkv * (_lane_sum(dxkv * xkv) / KV_LORA))).astype(BF16)
        dproj_ref[:, 384:512] = dkpe.astype(BF16)
        dh = dh + dh_part(0)
        dg_in_ref[...] += _colsum(dh * xh0)
        dxh = dh * g_in
        gx_ref[...] = dx1_ref[...] + r0 * (dxh - xh0 * _rep(_lane_sum(dxh * xh0) / D_MODEL, D_MODEL))

    row = lambda i: (i, 0)
    col = lambda c: (lambda i: (i, c))
    head_rows = lambda i: (0, i, 0)
    nxt = lambda i: (jnp.minimum((i + 1) * (tm // 8), T // 8 - 1), 0)
    in_specs = [pl.BlockSpec((tm, D_MODEL), row), pl.BlockSpec((tm, D_MODEL), row), pl.BlockSpec((1, tm), lambda i: (0, i)),
                pl.BlockSpec((tm, 512), col(0)), pl.BlockSpec((tm, 512), col(3)), pl.BlockSpec((tm, 512), col(4)),
                pl.BlockSpec((N_HEADS, tm, HEAD_PAD), head_rows), pl.BlockSpec((N_HEADS, tm, HEAD_PAD), head_rows),
                pl.BlockSpec((N_HEADS, tm, V_DIM), head_rows), pl.BlockSpec((tm, 1536), row),
                pl.BlockSpec((tm, CONV_W), row), pl.BlockSpec((8, CONV_W), nxt),
                _full((1, D_MODEL)), _full((D_MODEL, PROJ_EXT)), _full((1, Q_LORA)), _full((N_HEADS, Q_LORA, HEAD_PAD)),
                _full((1, KV_LORA)), _full((N_HEADS, KV_LORA, HEAD_PAD)), _full((1, HEAD_PAD)), _full((1, HEAD_PAD)),
                _full((3, CONV_W)), _full((1, LANES)), _full((1, LANES))]
    out_specs = [pl.BlockSpec((tm, D_MODEL), row), pl.BlockSpec((D_MODEL, tm), lambda i: (0, i)),
                 pl.BlockSpec((tm, PROJ_EXT), row),
                 _full((N_HEADS, Q_LORA, HEAD_PAD)), _full((N_HEADS, KV_LORA, HEAD_PAD)),
                 _full((1, D_MODEL)), _full((1, Q_LORA)), _full((1, KV_LORA)), _full((1, HEAD_PAD)), _full((1, HEAD_PAD))]
    out_shape = [jax.ShapeDtypeStruct((T, D_MODEL), F32), jax.ShapeDtypeStruct((D_MODEL, T), BF16),
                 jax.ShapeDtypeStruct((T, PROJ_EXT), BF16),
                 jax.ShapeDtypeStruct((N_HEADS, Q_LORA, HEAD_PAD), F32), jax.ShapeDtypeStruct((N_HEADS, KV_LORA, HEAD_PAD), F32),
                 jax.ShapeDtypeStruct((1, D_MODEL), F32), jax.ShapeDtypeStruct((1, Q_LORA), F32),
                 jax.ShapeDtypeStruct((1, KV_LORA), F32), jax.ShapeDtypeStruct((1, HEAD_PAD), F32),
                 jax.ShapeDtypeStruct((1, HEAD_PAD), F32)]
    return pl.pallas_call(
        body, name="bwd_proj", grid=(nt,), in_specs=in_specs, out_specs=out_specs, out_shape=out_shape,
        compiler_params=_params(dimension_semantics=("arbitrary",)),
    )(x, dx1, pos, proj, proj, proj, dq, dk, dv, dtail, du, du, g_in, w_in, g_cq, w_uq, g_ckv, w_ukv, gq, gk, conv_w,
      invf, sgn)


def _matmul_acc(a, b, tt, tn, parts):
    M, T = a.shape
    N = b.shape[1]
    n = len(parts)
    grid = (N // tn, T // tt)
    hm = M // 2

    def body(a_ref, b_ref, *rest):
        part_refs, (o_ref, sib_ref), rest = rest[:n], rest[n:n + 2], rest[n + 2:]
        out_refs, (stage_ref, tile_send, tile_recv), sems = rest[:n], rest[n:n + 3], rest[n + 3:]
        j, t = pl.program_id(0), pl.program_id(1)
        if n:
            start, drain = _scatter_steps(part_refs, out_refs, *sems)
            pl.when(jnp.logical_and(j == 0, t == 0))(start)

        def to_sibling(jj):
            x, y, c = _mesh_pos()
            return _remote(stage_ref, sib_ref.at[:, pl.ds(pl.multiple_of(jj * tn, tn), tn)],
                           tile_send, tile_recv, jj, (x, y, 1 - c))

        @pl.when(t == 0)
        def _():
            o_ref[...] = jnp.zeros_like(o_ref)

        o_ref[...] += _dot(a_ref[...], b_ref[...])

        tile_done = t == grid[1] - 1
        pl.when(jnp.logical_and(tile_done, j > 0))(lambda: to_sibling(j - 1).wait())

        @pl.when(tile_done)
        def _():
            c = lax.axis_index("c")
            stage_ref[...] = o_ref[pl.ds(pl.multiple_of((1 - c) * hm, hm), hm), :]
            to_sibling(j).start()

        pl.when(jnp.logical_and(tile_done, j == grid[0] - 1))(lambda: to_sibling(j).wait())
        if n:
            pl.when(jnp.logical_and(j == grid[0] - 1, t == grid[1] - 1))(drain)

    sems = [pltpu.SemaphoreType.DMA((3 * n,)), pltpu.SemaphoreType.DMA((3 * n,)), pltpu.SemaphoreType.DMA((n,))]
    outs = pl.pallas_call(
        body, name="dw_in", grid=grid,
        in_specs=[pl.BlockSpec((M, tt), lambda j, t: (0, t)), pl.BlockSpec((tt, tn), lambda j, t: (t, j))] + [_ANY] * n,
        out_specs=[pl.BlockSpec((M, tn), lambda j, t: (0, j)), _ANY] + [_ANY] * n,
        out_shape=[jax.ShapeDtypeStruct((M, N), F32), jax.ShapeDtypeStruct((hm, N), F32)] + _scattered_shapes(parts),
        scratch_shapes=[pltpu.VMEM((hm, tn), F32)] + [pltpu.SemaphoreType.DMA((grid[0],))] * 2 + (sems if n else []),
        compiler_params=_params(dimension_semantics=("arbitrary", "arbitrary")),
    )(a, b, *parts)
    return outs[0], outs[1], outs[2:]


def _add_chips(parts, small_parts):
    arrays = list(parts) + [small_parts]

    def body(*refs):
        ins, outs = refs[:len(arrays)], refs[len(arrays):]
        for a_ref, o_ref in zip(ins, outs):
            part = lambda k: a_ref[k].astype(F32)
            o_ref[...] = ((part(0) + part(1)) + part(2)) + part(3)

    in_specs, out_specs, out_shape = [], [], []
    for a in arrays:
        _, rows, cols = a.shape
        in_specs.append(pl.BlockSpec((N_CHIPS, rows // 2, cols), lambda i: (0, i, 0)))
        out_specs.append(pl.BlockSpec((rows // 2, cols), lambda i: (i, 0)))
        out_shape.append(jax.ShapeDtypeStruct((rows, cols), F32))
    outs = pl.pallas_call(body, name="add_chips", grid=(2,), in_specs=in_specs, out_specs=out_specs,
                          out_shape=out_shape, compiler_params=_params(dimension_semantics=("arbitrary",)))(*arrays)
    return outs[:-1], outs[-1]


def _adamw_small(ws, gs, ms, vs):
    n = len(ws)

    def body(*refs):
        for i in range(n):
            w_ref, g_ref, m_ref, v_ref = (refs[k * n + i] for k in range(4))
            d_ref, nm_ref, nv_ref = (refs[(4 + k) * n + i] for k in range(3))
            _adamw_math(g_ref[...], w_ref, m_ref, v_ref, d_ref, nm_ref, nv_ref)

    shapes = [jax.ShapeDtypeStruct(w.shape, F32) for w in ws]
    outs = pl.pallas_call(body, name="adamw_small", out_shape=shapes * 3)(*ws, *gs, *ms, *vs)
    return outs[:n], outs[n:2 * n], outs[2 * n:]


def _adamw_math(gv, w_ref, m_ref, v_ref, d_ref, nm_ref, nv_ref):
    nm = B1 * m_ref[...] + (1.0 - B1) * gv
    nv = B2 * v_ref[...] + (1.0 - B2) * (gv * gv)
    m_hat = nm / (1.0 - B1 ** STEP)
    v_hat = nv / (1.0 - B2 ** STEP)
    d_ref[...] = -LR * (m_hat / (jnp.sqrt(v_hat) + ADAM_EPS) + WD * w_ref[...])
    nm_ref[...] = nm
    nv_ref[...] = nv


def _adamw_halves(w, mine, other, m, v, c, name, transposed):
    hr, cols = mine.shape

    def body(c_ref, w_ref, mine_ref, other_ref, m_ref, v_ref, g_ref, d_ref, nm_ref, nv_ref, *picked):
        gv = jnp.where(pl.program_id(0) == c_ref[0], mine_ref[...], other_ref[...])
        if transposed:
            picked[0][...] = gv
            _store_transposed(picked[0], g_ref)
            gv = g_ref[...]
        else:
            g_ref[...] = gv
        _adamw_math(gv, w_ref, m_ref, v_ref, d_ref, nm_ref, nv_ref)

    if transposed:
        half = pl.BlockSpec((cols, hr), lambda i, c_ref: (0, i))
    else:
        half = pl.BlockSpec((hr, cols), lambda i, c_ref: (i, 0))
    whole = pl.BlockSpec((hr, cols), lambda i, c_ref: (0, 0))
    shp = jax.ShapeDtypeStruct(w.shape, F32)
    return pl.pallas_call(
        body, name=name, out_shape=[shp] * 4,
        grid_spec=pltpu.PrefetchScalarGridSpec(num_scalar_prefetch=1, grid=(2,), in_specs=[half, whole, whole, half, half],
                                               out_specs=[half] * 4,
                                               scratch_shapes=[pltpu.VMEM((hr, cols), F32)] if transposed else []),
        compiler_params=_params(dimension_semantics=("arbitrary",)),
    )(c.reshape(1), w, mine, other, m, v)


_ANY = pl.BlockSpec(memory_space=pl.ANY)


def _mesh_pos():
    return lax.axis_index("x"), lax.axis_index("y"), lax.axis_index("c")


def _other_chips(x, y):
    return [(1 - x, y), (x, 1 - y), (1 - x, 1 - y)]


def _remote(src, dst, send_sems, recv_sems, k, to):
    return pltpu.make_async_remote_copy(src_ref=src, dst_ref=dst, send_sem=send_sems.at[k], recv_sem=recv_sems.at[k],
                                        device_id=to, device_id_type=MESH)


def _gather_weights(shards, n_transposed):
    n = len(shards)
    shapes = [s.shape[::-1] if i < n_transposed else s.shape for i, s in enumerate(shards)]

    def body(*refs):
        start, forward, drain = _gather_steps(shapes, refs[:n], refs[n:2 * n], refs[2 * n:3 * n], *refs[3 * n:])
        start()
        forward()
        drain()

    vmem = pl.BlockSpec(memory_space=pltpu.VMEM)
    return pl.pallas_call(
        body, name="gather_weights", in_specs=[vmem] * n, out_specs=[_ANY] * n,
        out_shape=_gathered_shapes(shapes), scratch_shapes=_gather_scratch(shapes), compiler_params=_params(),
    )(*shards)


def _travel_shape(shape):
    rows, cols = shape
    return (rows, HEAD_PAD if cols == QK_DIM else cols)


def _gathered_shapes(shapes):
    return [jax.ShapeDtypeStruct((N_CHIPS,) + _travel_shape(s), BF16) for s in shapes]


def _gather_scratch(shapes):
    n = len(shapes)
    return ([pltpu.VMEM(_travel_shape(s), BF16) for s in shapes]
            + [pltpu.SemaphoreType.DMA((6 * n,)), pltpu.SemaphoreType.DMA((6 * n,)), pltpu.SemaphoreType.DMA((n,))])


def _gather_steps(shapes, ins, outs, stage, send_sems, recv_sems, local_sems):
    n = len(shapes)
    halved = [s[0] % 32 == 0 for s in shapes]

    def part(i, ref, hc):
        if not halved[i]:
            return ref
        hr = shapes[i][0] // 2
        return ref.at[pl.ds(hc * hr, hr), :]

    def to_chip(i, j, x, y, c):
        cx, cy = _other_chips(x, y)[j]
        return _remote(part(i, stage[i], c), part(i, outs[i].at[2 * x + y], c), send_sems, recv_sems, 6 * i + j, (cx, cy, c))

    def to_sibling(i, j, x, y, c):
        cx, cy = _other_chips(x, y)[j]
        got = part(i, outs[i].at[2 * cx + cy], c)
        return _remote(got, got, send_sems, recv_sems, 6 * i + 3 + j, (x, y, 1 - c))

    def local(i, x, y):
        return pltpu.make_async_copy(stage[i], outs[i].at[2 * x + y], local_sems.at[i])

    def start():
        x, y, c = _mesh_pos()
        for i in range(n):
            cols = shapes[i][1]
            if stage[i].shape[1] != cols:
                stage[i][...] = jnp.zeros_like(stage[i])
            if ins[i].shape == shapes[i]:
                stage[i][:, 0:cols] = ins[i][...].astype(BF16)
            else:
                _store_transposed(ins[i], stage[i])
            local(i, x, y).start()
            for j in range(3):
                to_chip(i, j, x, y, c).start()

    def forward():
        x, y, c = _mesh_pos()
        for i in range(n):
            for j, (cx, cy) in enumerate(_other_chips(x, y)):
                got = part(i, outs[i].at[2 * cx + cy], c)
                _remote(got, got, send_sems, recv_sems, 6 * i + j, (cx, cy, c)).wait_recv()
                if halved[i]:
                    to_sibling(i, j, x, y, c).start()

    def drain():
        x, y, c = _mesh_pos()
        for i in range(n):
            for j, (cx, cy) in enumerate(_other_chips(x, y)):
                if halved[i]:
                    got = part(i, outs[i].at[2 * cx + cy], 1 - c)
                    _remote(got, got, send_sems, recv_sems, 6 * i + 3 + j, (x, y, 1 - c)).wait_recv()
                    to_sibling(i, j, x, y, c).wait_send()
                to_chip(i, j, x, y, c).wait_send()
            local(i, x, y).wait()

    return start, forward, drain


def _swap_halves(grads, whole, name):
    n, m = len(grads), len(grads) + len(whole)

    def body(*refs):
        start, drain = _swap_steps(n, refs[:m], refs[m:2 * m], refs[2 * m], refs[2 * m + 1])
        start()
        drain()

    outs = pl.pallas_call(
        body, name=name, in_specs=[_ANY] * m, out_specs=[_ANY] * m, out_shape=_swapped_shapes(grads, whole),
        scratch_shapes=[pltpu.SemaphoreType.DMA((m,)), pltpu.SemaphoreType.DMA((m,))],
    )(*grads, *whole)
    return outs[:n], outs[n:]


def _swapped_shapes(grads, whole):
    return ([jax.ShapeDtypeStruct((g.shape[0], g.shape[1] // 2, g.shape[2]), F32) for g in grads]
            + [jax.ShapeDtypeStruct(w.shape, F32) for w in whole])


def _swap_steps(n, ins, outs, send_sems, recv_sems):
    def copies():
        x, y, c = _mesh_pos()
        cps = []
        for i, src in enumerate(ins):
            if i < n:
                hr = src.shape[1] // 2
                src = src.at[:, pl.ds((1 - c) * hr, hr), :]
            cps.append(_remote(src, outs[i], send_sems, recv_sems, i, (x, y, 1 - c)))
        return cps

    def start():
        for cp in copies():
            cp.start()

    def drain():
        for cp in copies():
            cp.wait()

    return start, drain


def _scattered_shapes(parts):
    return [jax.ShapeDtypeStruct(p.shape if p.ndim == 3 else (N_CHIPS,) + p.shape, p.dtype) for p in parts]


def _scatter_steps(ins, outs, send_sems, recv_sems, local_sems):
    n = len(ins)

    def src(i, k):
        return ins[i].at[k] if len(ins[i].shape) == 3 else ins[i]

    def sends(x, y, c):
        return [_remote(src(i, 2 * cx + cy), outs[i].at[2 * x + y], send_sems, recv_sems, 3 * i + j, (cx, cy, c))
                for i in range(n) for j, (cx, cy) in enumerate(_other_chips(x, y))]

    def local(i, x, y):
        return pltpu.make_async_copy(src(i, 2 * x + y), outs[i].at[2 * x + y], local_sems.at[i])

    def start():
        x, y, c = _mesh_pos()
        for i in range(n):
            local(i, x, y).start()
        for cp in sends(x, y, c):
            cp.start()

    def drain():
        x, y, c = _mesh_pos()
        for i in range(n):
            for j, (cx, cy) in enumerate(_other_chips(x, y)):
                got = outs[i].at[2 * cx + cy]
                _remote(got, got, send_sems, recv_sems, 3 * i + j, (cx, cy, c)).wait_recv()
        for cp in sends(x, y, c):
            cp.wait_send()
        for i in range(n):
            local(i, x, y).wait()

    return start, drain


def _add_pair(grads, from_sibling, small, small_sibling, c):
    n = len(grads)

    def body(c_ref, *refs):
        ins, outs = refs[:2 * n + 2], refs[2 * n + 2:]
        for i in range(n + 1):
            outs[i][...] = (ins[2 * i][...] + ins[2 * i + 1][...]).astype(outs[i].dtype)

    in_specs, out_specs, out_shape, args = [], [], [], []
    for g, r in zip(grads, from_sibling):
        _, hr, cols = r.shape
        in_specs += [pl.BlockSpec((1, hr, cols), lambda k, c_ref: (k, c_ref[0], 0)),
                     pl.BlockSpec((1, hr, cols), lambda k, c_ref: (k, 0, 0))]
        out_specs.append(pl.BlockSpec((1, hr, cols), lambda k, c_ref: (k, 0, 0)))
        out_shape.append(jax.ShapeDtypeStruct(r.shape, BF16))
        args += [g, r]
    whole = pl.BlockSpec(small.shape, lambda k, c_ref: (0, 0))
    in_specs += [whole, whole]
    out_specs.append(whole)
    out_shape.append(jax.ShapeDtypeStruct(small.shape, F32))
    outs = pl.pallas_call(
        body, name="add_pair", out_shape=out_shape,
        grid_spec=pltpu.PrefetchScalarGridSpec(num_scalar_prefetch=1, grid=(N_CHIPS,), in_specs=in_specs,
                                               out_specs=out_specs),
        compiler_params=_params(dimension_semantics=("arbitrary",)),
    )(c.reshape(1), *args, small, small_sibling)
    return outs[:n], outs[n]


def _scatter_w_in(dw_in_e, from_sibling):
    hr = from_sibling.shape[1]
    shard = (N_CHIPS, hr, SHARD_COLS_IN)

    def body(g_in, r_in, out, g_buf, r_buf, p_buf, load_sems, send_sems, recv_sems, local_sems):
        c = lax.axis_index("c")
        loads = (pltpu.make_async_copy(g_in.at[0, pl.ds(c * hr, hr), :], g_buf, load_sems.at[0]),
                 pltpu.make_async_copy(r_in.at[0], r_buf, load_sems.at[1]))
        for cp in loads:
            cp.start()
        for cp in loads:
            cp.wait()
        g_buf[...] += r_buf[...]
        p_buf[0, :, 0:KPE_END] = g_buf[:, 0:KPE_END].astype(BF16)
        p_buf[0, :, KPE_END:SHARD_COLS_IN] = g_buf[:, KPE_END + KPE_PAD:SHARD_COLS_IN + KPE_PAD].astype(BF16)
        for k in range(1, N_CHIPS):
            p_buf[k] = g_buf[:, SHARD_COLS_IN * k + KPE_PAD:SHARD_COLS_IN * (k + 1) + KPE_PAD].astype(BF16)
        start, drain = _scatter_steps([p_buf], [out], send_sems, recv_sems, local_sems)
        start()
        drain()

    return pl.pallas_call(
        body, name="scatter_grads", in_specs=[_ANY] * 2, out_specs=_ANY, out_shape=jax.ShapeDtypeStruct(shard, BF16),
        scratch_shapes=[pltpu.VMEM((hr, PROJ_EXT), F32)] * 2 + [pltpu.VMEM(shard, BF16)]
                       + [pltpu.SemaphoreType.DMA((2,)), pltpu.SemaphoreType.DMA((3,)), pltpu.SemaphoreType.DMA((3,)),
                          pltpu.SemaphoreType.DMA((1,))],
        compiler_params=_params(),
    )(dw_in_e, from_sibling)


def _share_halves(halves):
    n = len(halves)

    def body(*refs):
        ins, outs, send_sems, recv_sems = refs[:n], refs[n:2 * n], refs[2 * n], refs[2 * n + 1]
        x, y, c = _mesh_pos()
        cps = [_remote(ins[i], outs[i], send_sems, recv_sems, i, (x, y, 1 - c)) for i in range(n)]
        for cp in cps:
            cp.start()
        for cp in cps:
            cp.wait()

    return pl.pallas_call(
        body, name="share_halves", in_specs=[_ANY] * n, out_specs=[_ANY] * n,
        out_shape=[jax.ShapeDtypeStruct(h.shape, h.dtype) for h in halves],
        scratch_shapes=[pltpu.SemaphoreType.DMA((n,)), pltpu.SemaphoreType.DMA((n,))],
    )(*halves)


SHARD_COLS_IN = IN_TOTAL // N_CHIPS
KPE_END = Q_LORA + KV_LORA + ROPE
KPE_PAD = PROJ_EXT - IN_TOTAL


def _by_cols(a):
    return a.transpose(1, 0, 2).reshape(a.shape[1], N_CHIPS * a.shape[2])


def _assemble_early(c_in, c_uq, c_ukv, c_conv):
    return c_in, c_uq, c_ukv, _by_cols(c_conv).astype(F32)


def _assemble_late(c_o, c_pl, c_plg):
    return c_o.reshape(D_MODEL, D_MODEL), _by_cols(c_pl), c_plg.reshape(D_MODEL, D_MODEL)


def _split_late(dw_o, dw_pl, dw_plg):
    chip_major = lambda a: a.reshape(a.shape[0], N_CHIPS, a.shape[1] // N_CHIPS).transpose(1, 0, 2)
    return [dw_o.reshape(N_CHIPS, D_MODEL // N_CHIPS, D_MODEL), chip_major(dw_pl),
            dw_plg.reshape(N_CHIPS, D_MODEL // N_CHIPS, D_MODEL)]


def _local_step(x, p, pos, tgt, gains, early, late_shards, late_gathered, tm, tq):
    c_in, w_uq_e, w_ukv, conv_w = early
    g_in, g_cq, g_ckv, g_q, g_k, g_oa, g_oc, g_pl = gains
    T = x.shape[0]
    zpad = lambda a, n: jnp.concatenate([a, jnp.zeros(a.shape[:-1] + (n,), a.dtype)], axis=-1)
    gq, gk = zpad(g_q, HEAD_PAD - QK_DIM), zpad(g_k, HEAD_PAD - QK_DIM)
    inv_freq = 1.0 / (ROPE_THETA ** (jnp.arange(0, ROPE, 2, dtype=F32) / ROPE))
    invf = jnp.concatenate([inv_freq, inv_freq, jnp.zeros((64,), F32)]).reshape(1, LANES)
    sgn = jnp.concatenate([-jnp.ones((32,), F32), jnp.ones((32,), F32), jnp.zeros((64,), F32)]).reshape(1, LANES)

    (proj, q, k, v, w_in_e), gathered = _fwd_proj(x, pos, g_in, c_in, g_cq, w_uq_e, g_ckv, w_ukv, gq, gk, invf, sgn,
                                                  late_shards, min(2 * tm, T))
    w_o, w_pl, w_plg = _assemble_late(*(gathered if late_shards else late_gathered))
    o, lse = _attn_fwd(q, k, v, tq)
    (dx1, do, delta, dtail, du, dw_o, dw_pl, dw_plg, dg_oa, dg_oc, dg_pl, dconv, loss) = _tail(
        x, o, proj, p, tgt, g_oa, g_oc, g_pl, conv_w, w_o, w_pl, w_plg, tm)
    late_grads = _split_late(dw_o, dw_pl, dw_plg)
    (dq, dk, dv), late_sibling = _attn_bwd(q, k, v, do, lse, delta, tq, late_grads)
    (gx, h, dproj, dw_uq_e, dw_ukv, dg_in, dg_cq, dg_ckv, dgq, dgk) = _bwd_proj(
        x, dx1, pos, proj, dq, dk, dv, dtail, du, g_in, w_in_e, g_cq, w_uq_e, g_ckv, w_ukv, gq, gk, conv_w, invf, sgn, tm)
    wgrads = [dw_uq_e[:, :, :QK_DIM], dw_ukv, *late_grads]
    ggrads = (dg_in, dg_cq, dg_ckv, dgq, dgk, dg_oa, dg_oc, dg_pl)
    return loss, gx, (h, dproj), wgrads, late_sibling, ggrads, dconv


def kernel(x, p, positions, g_in, w_in, g_cq, w_uq, g_ckv, w_ukv, g_q, g_k, conv_w, g_oa, g_oc, w_o, w_pl, w_plg, g_pl, loss_target, m_g_in, m_w_in, m_g_cq, m_w_uq, m_g_ckv, m_w_ukv, m_g_q, m_g_k, m_conv_w, m_g_oa, m_g_oc, m_w_o, m_w_pl, m_w_plg, m_g_pl, v_g_in, v_w_in, v_g_cq, v_w_uq, v_g_ckv, v_w_ukv, v_g_q, v_g_k, v_conv_w, v_g_oa, v_g_oc, v_w_o, v_w_pl, v_w_plg, v_g_pl):
    T = x.shape[1]
    c = lax.axis_index("c")
    chip = 2 * lax.axis_index("x") + lax.axis_index("y")
    gains = [g.reshape(1, -1) for g in (g_in, g_cq, g_ckv, g_q, g_k, g_oa, g_oc, g_pl)]

    transposed = ("w_in", "w_uq")
    early = _assemble_early(*_gather_weights([w_in[0].T, w_uq[0].T, w_ukv[0], conv_w[0]], len(transposed)))

    loss, gx, (h_t, dproj), others_cm, late_sibling, ggrads, dconv = _local_step(
        x[0], p[0, 0], positions.reshape(1, T), loss_target[0], gains, early, [w_o[0], w_pl[0], w_plg[0]], None, 256, 512)

    small_parts = [a.reshape(-1, LANES) for a in (*ggrads, loss, dconv)]
    small_rows = [a.shape[0] for a in small_parts]
    tile_rows = [-(-r // 8) * 8 for r in small_rows]
    tile_rows[-1] += -sum(tile_rows) % 16
    small = jnp.concatenate([jnp.pad(a, ((0, t - r), (0, 0))) for a, r, t in zip(small_parts, small_rows, tile_rows)])
    n_early = len(others_cm) - len(late_sibling)
    early_sibling, (small_sibling,) = _swap_halves(others_cm[:n_early], [small], "pair_grads")
    chip_parts, chip_small = _add_pair(others_cm, [*early_sibling, *late_sibling], small, small_sibling, c)
    dw_in_e, w_in_sibling, exchanged = _matmul_acc(h_t, dproj, min(4096, T), 512, [*chip_parts, chip_small])
    by_chip = [_scatter_w_in(dw_in_e[None], w_in_sibling[None]), *exchanged[:-1]]
    halves, small_total = _add_chips(by_chip, exchanged[-1])
    other_halves = _share_halves(halves)

    gg, off = [], 0
    for rows, tiled in zip(small_rows, tile_rows):
        gg.append(small_total[off:off + rows].reshape(1, -1))
        off += tiled
    loss_out = gg[8][0, 0]
    conv_total = gg[9].reshape(3, CONV_W)
    conv_g = lax.dynamic_slice(conv_total, (0, chip * (CONV_W // N_CHIPS)), (3, CONV_W // N_CHIPS))
    g_by_name = dict(g_in=gg[0], g_cq=gg[1], g_ckv=gg[2], g_q=gg[3][:, :QK_DIM], g_k=gg[4][:, :QK_DIM], conv_w=conv_g,
                     g_oa=gg[5], g_oc=gg[6], g_pl=gg[7])
    half_by_name = dict(zip(("w_in", "w_uq", "w_ukv", "w_o", "w_pl", "w_plg"), zip(halves, other_halves)))
    weights = dict(g_in=g_in, w_in=w_in, g_cq=g_cq, w_uq=w_uq, g_ckv=g_ckv, w_ukv=w_ukv, g_q=g_q, g_k=g_k,
                   conv_w=conv_w, g_oa=g_oa, g_oc=g_oc, w_o=w_o, w_pl=w_pl, w_plg=w_plg, g_pl=g_pl)
    ms = dict(g_in=m_g_in, w_in=m_w_in, g_cq=m_g_cq, w_uq=m_w_uq, g_ckv=m_g_ckv, w_ukv=m_w_ukv, g_q=m_g_q, g_k=m_g_k,
              conv_w=m_conv_w, g_oa=m_g_oa, g_oc=m_g_oc, w_o=m_w_o, w_pl=m_w_pl, w_plg=m_w_plg, g_pl=m_g_pl)
    vs = dict(g_in=v_g_in, w_in=v_w_in, g_cq=v_g_cq, w_uq=v_w_uq, g_ckv=v_g_ckv, w_ukv=v_w_ukv, g_q=v_g_q, g_k=v_g_k,
              conv_w=v_conv_w, g_oa=v_g_oa, g_oc=v_g_oc, w_o=v_w_o, w_pl=v_w_pl, w_plg=v_w_plg, g_pl=v_g_pl)
    names = list(weights)
    flat = lambda a: a.reshape(-1, a.shape[-1])
    small_names = list(g_by_name)
    small_out = _adamw_small([flat(weights[n]) for n in small_names], [flat(g_by_name[n]) for n in small_names],
                             [flat(ms[n]) for n in small_names], [flat(vs[n]) for n in small_names])
    results = {n: (flat(g_by_name[n]), *(out[i] for out in small_out)) for i, n in enumerate(small_names)}
    for n in half_by_name:
        shard = (lambda a: a[0].T) if n in transposed else flat
        out = _adamw_halves(shard(weights[n]), *half_by_name[n], shard(ms[n]), shard(vs[n]), c, "adamw_" + n,
                            n in transposed)
        results[n] = [a.T for a in out] if n in transposed else out
    per_kind = [[results[n][kind].reshape(weights[n].shape) for n in names] for kind in range(4)]
    return (loss_out, gx.reshape(x.shape), *per_kind[0], *per_kind[1], *per_kind[2], *per_kind[3])
```

```python
import math

import jax
import jax.numpy as jnp
from jax import lax
from jax.experimental import pallas as pl
from jax.experimental.pallas import tpu as pltpu

F32 = jnp.float32
BF16 = jnp.bfloat16

D_MODEL = 1024
N_HEADS = 4
NOPE = 128
ROPE = 64
V_DIM = 128
QK_DIM = NOPE + ROPE
HEAD_PAD = 256
Q_LORA = 256
KV_LORA = 128
ATTN_W = 512
CONV_W = 512
PLE = 256
IN_TOTAL = 3008
PROJ_EXT = 3072
ROPE_THETA = 10000.0
EPS = 1e-6
SCALE = 1.0 / math.sqrt(QK_DIM)
LOG2E = math.log2(math.e)
EXP2_SCALE = SCALE * LOG2E
NEG = -1e30
SOFTMAX_ROWS = 32
SUB_TILE = 256

LR, B1, B2, ADAM_EPS, WD, STEP = 0.001, 0.9, 0.999, 1e-08, 0.01, 10

N_CHIPS = 4
LANES = 128
VMEM_LIMIT = 56 * 1024 * 1024
MESH = pl.DeviceIdType.MESH


def _params(**kw):
    return pltpu.CompilerParams(vmem_limit_bytes=VMEM_LIMIT, **kw)


def _inv_rms(x, n):
    return lax.rsqrt(jnp.sum(x * x, axis=-1, keepdims=True) / n + EPS)


def _lane_sum(a):
    folded = a[:, 0:LANES]
    for c0 in range(LANES, a.shape[1], LANES):
        folded = folded + a[:, c0:c0 + LANES]
    head = folded.astype(BF16)
    tail = (folded - head.astype(F32)).astype(BF16)
    return _dot(jnp.concatenate([head, tail], axis=1), jnp.ones((2 * LANES, LANES), BF16))


def _inv_rms_mxu(x):
    return lax.rsqrt(_lane_sum(x * x) / x.shape[1] + EPS)


def _rep(r, width):
    return r if width == LANES else jnp.tile(r, (1, width // LANES))


def _sigmoid(z):
    return jax.nn.sigmoid(z)


def _swap_rope_halves(b):
    lane = lax.broadcasted_iota(jnp.int32, b.shape, 1)
    swapped = jnp.where(lane < 32, pltpu.roll(b, 96, 1), pltpu.roll(b, 32, 1))
    return jnp.where(lane < ROPE, swapped, 0.0)


def _dot(a, b):
    return jnp.dot(a, b, preferred_element_type=F32)


def _dot_nt(a, b):
    return lax.dot_general(a, b, (((1,), (1,)), ((), ())), preferred_element_type=F32)


def _dot_tn(a, b):
    return lax.dot_general(a, b, (((0,), (0,)), ((), ())), preferred_element_type=F32)


def _colsum(a):
    return jnp.sum(a, axis=0, keepdims=True)


def _store_transposed(src_ref, dst_ref):
    r, c = src_ref.shape
    for r0 in range(0, r, LANES):
        h = min(LANES, r - r0)
        for c0 in range(0, c, LANES):
            w = min(LANES, c - c0)
            piece = src_ref[r0:r0 + h, c0 + w - LANES:c0 + w]
            if h < LANES:
                piece = jnp.concatenate([piece, jnp.zeros((LANES - h, LANES), piece.dtype)], axis=0)
            dst_ref[c0:c0 + w, r0:r0 + h] = piece.T[LANES - w:, 0:h].astype(dst_ref.dtype)


def _full(shape):
    return pl.BlockSpec(shape, lambda *_: (0,) * len(shape))


def _round_robin(chains, width):
    waiting, active = list(chains), []
    while waiting or active:
        while waiting and len(active) < width:
            active.append(waiting.pop(0))
        for chain in list(active):
            if next(chain, _DONE) is _DONE:
                active.remove(chain)


_DONE = object()


def _rope_tables(pos_ref, invf_ref, sgn_ref):
    pos = jnp.broadcast_to(pos_ref[...].astype(F32), (LANES, pos_ref.shape[1])).T
    ang = pos * invf_ref[...]
    return jnp.cos(ang), jnp.sin(ang) * sgn_ref[...]


def _fwd_proj(x, pos, g_in, c_in, g_cq, w_uq, g_ckv, w_ukv, gq, gk, invf, sgn, late_shards, tm):
    T = x.shape[0]
    nt = T // tm
    n_late = len(late_shards)
    ts = min(SUB_TILE, tm)

    def body(x_ref, pos_ref, g_in_ref, c_in_ref, g_cq_ref, w_uq_ref, g_ckv_ref, w_ukv_ref, gq_ref, gk_ref,
             invf_ref, sgn_ref, *rest):
        late_in, (proj_ref, q_ref, k_ref, v_ref, w_in_ref) = rest[:n_late], rest[n_late:n_late + 5]
        late_out, late_scratch = rest[n_late + 5:2 * n_late + 5], rest[2 * n_late + 5:]
        i = pl.program_id(0)

        @pl.when(i == 0)
        def _():
            w_in_ref[:, 0:KPE_END] = c_in_ref[0, :, 0:KPE_END]
            w_in_ref[:, KPE_END:KPE_END + KPE_PAD] = jnp.zeros((D_MODEL, KPE_PAD), BF16)
            w_in_ref[:, KPE_END + KPE_PAD:SHARD_COLS_IN + KPE_PAD] = c_in_ref[0, :, KPE_END:SHARD_COLS_IN]
            for chip in range(1, N_CHIPS):
                w_in_ref[:, SHARD_COLS_IN * chip + KPE_PAD:SHARD_COLS_IN * (chip + 1) + KPE_PAD] = c_in_ref[chip]

        if n_late:
            start, forward, drain = _gather_steps([s.shape for s in late_shards], late_in, late_out,
                                                  late_scratch[:n_late], *late_scratch[n_late:])
            pl.when(i == 0)(start)
            pl.when(i == nt // 2)(forward)

        for r0 in range(0, tm, ts):
            rows = slice(r0, r0 + ts)
            xv = x_ref[rows, :]
            h = (xv * _rep(_inv_rms_mxu(xv), D_MODEL) * g_in_ref[...]).astype(BF16)
            lat = _dot(h, w_in_ref[:, 0:512])
            proj_ref[rows, 0:512] = lat
            c_q = lat[:, 0:Q_LORA]
            cqn = (c_q * _rep(_inv_rms_mxu(c_q), Q_LORA) * g_cq_ref[...]).astype(BF16)
            c_kv = lat[:, Q_LORA:Q_LORA + KV_LORA]
            ckvn = (c_kv * _inv_rms_mxu(c_kv) * g_ckv_ref[...]).astype(BF16)
            kpe = lat[:, 384:512]
            kpe_sq = kpe * kpe
            cos_b, sin_b = _rope_tables(pos_ref.at[:, rows], invf_ref, sgn_ref)
            gq_a, gq_b = gq_ref[:, 0:NOPE], gq_ref[:, NOPE:HEAD_PAD]
            gk_a, gk_b = gk_ref[:, 0:NOPE], gk_ref[:, NOPE:HEAD_PAD]

            def projections(rows=rows, h=h):
                for c0 in range(512, PROJ_EXT, 512):
                    proj_ref[rows, c0:c0 + 512] = _dot(h, w_in_ref[:, c0:c0 + 512])
                    yield

            def queries(hd, rows=rows, cqn=cqn, cos_b=cos_b, sin_b=sin_b, gq_a=gq_a, gq_b=gq_b):
                qh = _dot(cqn, w_uq_ref[hd])
                yield
                a, b = qh[:, 0:NOPE], qh[:, NOPE:HEAD_PAD]
                r = lax.rsqrt(_lane_sum(a * a + b * b) / QK_DIM + EPS)
                yield
                bn = b * r * gq_b
                q_ref[hd, rows, 0:NOPE] = (a * r * gq_a).astype(BF16)
                q_ref[hd, rows, NOPE:HEAD_PAD] = (bn * cos_b + _swap_rope_halves(bn) * sin_b).astype(BF16)
                yield

            def keys(hd, rows=rows, ckvn=ckvn, kpe=kpe, kpe_sq=kpe_sq, cos_b=cos_b, sin_b=sin_b, gk_a=gk_a, gk_b=gk_b):
                kvh = _dot(ckvn, w_ukv_ref[hd])
                yield
                ka = kvh[:, 0:NOPE]
                rk = lax.rsqrt(_lane_sum(ka * ka + kpe_sq) / QK_DIM + EPS)
                yield
                kbn = kpe * rk * gk_b
                k_ref[hd, rows, 0:NOPE] = (ka * rk * gk_a).astype(BF16)
                k_ref[hd, rows, NOPE:HEAD_PAD] = (kbn * cos_b + _swap_rope_halves(kbn) * sin_b).astype(BF16)
                v_ref[hd, rows, 0:V_DIM] = kvh[:, NOPE:HEAD_PAD].astype(BF16)
                v_ref[hd, rows, V_DIM:2 * V_DIM] = jnp.ones((ts, V_DIM), BF16)
                yield

            chains = [projections()]
            for hd in range(N_HEADS):
                chains += [queries(hd), keys(hd)]
            _round_robin(chains, 4)

        if n_late:
            pl.when(i == nt - 1)(drain)

    row = lambda i: (i, 0)
    head_rows = lambda i: (0, i, 0)
    outs = pl.pallas_call(
        body, name="fwd_proj", grid=(nt,),
        in_specs=[pl.BlockSpec((tm, D_MODEL), row), pl.BlockSpec((1, tm), lambda i: (0, i)), _full((1, D_MODEL)),
                  _full((N_CHIPS, D_MODEL, SHARD_COLS_IN)), _full((1, Q_LORA)), _full((N_HEADS, Q_LORA, HEAD_PAD)),
                  _full((1, KV_LORA)), _full((N_HEADS, KV_LORA, HEAD_PAD)), _full((1, HEAD_PAD)), _full((1, HEAD_PAD)),
                  _full((1, LANES)), _full((1, LANES))] + [_full(s.shape) for s in late_shards],
        out_specs=[pl.BlockSpec((tm, PROJ_EXT), row), pl.BlockSpec((N_HEADS, tm, HEAD_PAD), head_rows),
                   pl.BlockSpec((N_HEADS, tm, HEAD_PAD), head_rows), pl.BlockSpec((N_HEADS, tm, 2 * V_DIM), head_rows),
                   _full((D_MODEL, PROJ_EXT))] + [_ANY] * n_late,
        out_shape=[jax.ShapeDtypeStruct((T, PROJ_EXT), F32), jax.ShapeDtypeStruct((N_HEADS, T, HEAD_PAD), BF16),
                   jax.ShapeDtypeStruct((N_HEADS, T, HEAD_PAD), BF16), jax.ShapeDtypeStruct((N_HEADS, T, 2 * V_DIM), BF16),
                   jax.ShapeDtypeStruct((D_MODEL, PROJ_EXT), BF16)] + _gathered_shapes([s.shape for s in late_shards]),
        scratch_shapes=_gather_scratch([s.shape for s in late_shards]) if n_late else [],
        compiler_params=_params(dimension_semantics=("arbitrary",)),
    )(x, pos, g_in, c_in, g_cq, w_uq, g_ckv, w_ukv, gq, gk, invf, sgn, *late_shards)
    return outs[:5], outs[5:]


def _chunk_pipeline(n_loop, lag, matmuls, pointwise, accumulate, last):
    slots = lag + 1

    def iteration(t, slot, pending=True):
        matmuls(jnp.minimum(t + lag, n_loop), (slot + lag) % slots)
        if pending:
            accumulate(t - lag, (slot + 1) % slots, False)
        pointwise(t, slot, False)

    def finish(slot, pending):
        for back in range(pending, 0, -1):
            accumulate(n_loop - back, (slot - back) % slots, False)
        pointwise(n_loop, slot, True)
        accumulate(n_loop, slot, True)
        last()

    for u in range(lag):
        matmuls(jnp.minimum(u, n_loop), u)
    for u in range(lag):
        pl.when(u < n_loop)(lambda u=u: iteration(u, u, pending=False))

    n_main = jnp.maximum(n_loop - lag, 0)

    def unrolled(tt, carry):
        for j in range(slots):
            iteration(lag + slots * tt + j, (lag + j) % slots)
        return carry

    lax.fori_loop(0, n_main // slots, unrolled, 0)
    rest = lax.rem(n_main, slots)
    t0 = n_loop - rest

    for r in range(slots):
        @pl.when(jnp.logical_and(n_loop >= lag, rest == r))
        def _():
            for j in range(r):
                iteration(t0 + j, (lag + j) % slots)
            finish((lag + r) % slots, lag)

    for short in range(lag):
        pl.when(n_loop == short)(lambda short=short: finish(short, short))


def _attn_fwd(q, k, v, tq):
    T = q.shape[1]
    tk = tq
    rc = min(SOFTMAX_ROWS, tq)

    def body(q_ref, k_ref, v_ref, o_ref, lse_ref, s0, s1, s2, p0, p1, p2, a0, a1, a2, m_ref, acc_ref):
        qi = pl.program_id(1)
        s_buf, p_buf, a_buf = (s0, s1, s2), (p0, p1, p2), (a0, a1, a2)

        def scores(t, slot):
            ks = pl.multiple_of(t * tk, tk)
            s_buf[slot][...] = _dot_nt(q_ref[0], k_ref[0, pl.ds(ks, tk), :])

        def blocks(masked):
            return ((0, tq // 2, tk // 2), (tq // 2, tq // 2, tk)) if masked else ((0, tq, tk),)

        def values(t, slot, masked):
            ks = pl.multiple_of(t * tk, tk)
            for q0, nq, nk in blocks(masked):
                rows = slice(q0, q0 + nq)
                acc_ref[rows, :] = (acc_ref[rows, :] * a_buf[slot][rows, :]
                                    + _dot(p_buf[slot][rows, 0:nk], v_ref[0, pl.ds(ks, nk), :]))

        def softmax(t, slot, masked):
            for q0, nq, nk in blocks(masked):
                rows = slice(q0, q0 + nq)
                s_all = s_buf[slot][rows, 0:nk]
                if masked:
                    row = lax.broadcasted_iota(jnp.int32, (nq, nk), 0) + q0
                    col = lax.broadcasted_iota(jnp.int32, (nq, nk), 1)
                    s_all = jnp.where(col <= row, s_all, NEG)
                    s_buf[slot][rows, 0:nk] = s_all
                m_old = m_ref[rows, :]
                m_new = jnp.maximum(m_old, jnp.max(s_all, axis=1, keepdims=True))
                a_buf[slot][rows, :] = jnp.exp2((m_old - m_new) * EXP2_SCALE)
                m_ref[rows, :] = m_new
                for r0 in range(0, nq, rc):
                    s = s_buf[slot][q0 + r0:q0 + r0 + rc, 0:nk]
                    p_buf[slot][q0 + r0:q0 + r0 + rc, 0:nk] = jnp.exp2((s - m_new[r0:r0 + rc, :]) * EXP2_SCALE).astype(BF16)

        def last():
            l = acc_ref[:, V_DIM:2 * V_DIM]
            o_ref[...] = acc_ref[:, 0:V_DIM] / l
            lse_ref[0] = (m_ref[...] * SCALE + jnp.log(l)).T[0:1, :]

        m_ref[...] = jnp.full_like(m_ref, NEG)
        acc_ref[...] = jnp.zeros_like(acc_ref)
        _chunk_pipeline(qi, 2, scores, softmax, values, last)

    return pl.pallas_call(
        body, name="attn_fwd", grid=(N_HEADS, T // tq),
        in_specs=[pl.BlockSpec((1, tq, HEAD_PAD), lambda h, i: (h, i, 0)),
                  pl.BlockSpec((1, T, HEAD_PAD), lambda h, i: (h, 0, 0)),
                  pl.BlockSpec((1, T, 2 * V_DIM), lambda h, i: (h, 0, 0))],
        out_specs=[pl.BlockSpec((tq, V_DIM), lambda h, i: (i, h)),
                   pl.BlockSpec((1, 1, tq), lambda h, i: (h, 0, i))],
        out_shape=[jax.ShapeDtypeStruct((T, ATTN_W), F32), jax.ShapeDtypeStruct((N_HEADS, 1, T), F32)],
        scratch_shapes=[pltpu.VMEM((tq, tk), F32)] * 3 + [pltpu.VMEM((tq, tk), BF16)] * 3
                       + [pltpu.VMEM((tq, 1), F32)] * 4 + [pltpu.VMEM((tq, 2 * V_DIM), F32)],
        compiler_params=_params(dimension_semantics=("arbitrary", "arbitrary")),
    )(q, k, v)


def _tail(x, o, proj, p, tgt, g_oa, g_oc, g_pl, conv_w, w_o, w_pl, w_plg, tm):
    T = x.shape[0]
    nt = T // tm

    def body(x_ref, o_ref, za_ref, cb_ref, cc_ref, cx_ref, zc_ref, cch_ref, cxh_ref, p_ref, tgt_ref,
             g_oa_ref, g_oc_ref, g_pl_ref, cw_ref, w_o_ref, w_pl_ref, w_plg_ref,
             dx1_ref, do_ref, delta_ref, dtail_ref, du_ref,
             dw_o_ref, dw_pl_ref, dw_plg_ref, dg_oa_ref, dg_oc_ref, dg_pl_ref, dcw_ref, loss_ref):
        i = pl.program_id(0)

        @pl.when(i == 0)
        def _():
            for r in (dw_o_ref, dw_pl_ref, dw_plg_ref, dg_oa_ref, dg_oc_ref, dg_pl_ref, dcw_ref, loss_ref):
                r[...] = jnp.zeros_like(r)

        g_oa, g_oc, g_pl = g_oa_ref[...], g_oc_ref[...], g_pl_ref[...]
        w0, w1, w2 = cw_ref[0:1, :], cw_ref[1:2, :], cw_ref[2:3, :]

        xv, ov, za, cb, zc = x_ref[...], o_ref[...], za_ref[...], cb_ref[...], zc_ref[...]
        pb = p_ref[...].astype(BF16)
        pp = _dot(pb, w_pl_ref[...])

        sa = _sigmoid(za)
        silu_a = za * sa
        ga = ov * silu_a
        ra = _inv_rms(ga, ATTN_W)
        xa = ga * ra
        ya = (xa * g_oa).astype(BF16)
        x1_a = _dot(ya, w_o_ref[0:ATTN_W, :])
        v = cc_ref[...] * cx_ref[...]
        not_first = jnp.where(i > 0, 1.0, 0.0)
        hv6 = cch_ref[6:7, :] * cxh_ref[6:7, :] * not_first
        hv7 = cch_ref[7:8, :] * cxh_ref[7:8, :] * not_first
        row = lax.broadcasted_iota(jnp.int32, v.shape, 0)
        v1 = jnp.where(row == 0, hv7, pltpu.roll(v, 1, 0))
        v2 = jnp.where(row == 0, hv6, jnp.where(row == 1, hv7, pltpu.roll(v, 2, 0)))
        u = w0 * v2 + w1 * v1 + w2 * v
        sc = _sigmoid(zc)
        silu_c = zc * sc
        gc = cb * u * silu_c
        rc = _inv_rms(gc, CONV_W)
        xc = gc * rc
        yc = (xc * g_oc).astype(BF16)
        x1 = xv + (x1_a + _dot(yc, w_o_ref[ATTN_W:D_MODEL, :]))
        r1 = _inv_rms(x1, D_MODEL)
        xh1 = x1 * r1
        n1 = (xh1 * g_pl).astype(BF16)
        gate = _sigmoid(_dot(n1, w_plg_ref[...]))
        err = x1 + gate * pp - tgt_ref[...]
        loss_ref[...] += 0.5 * jnp.sum(err * err) / D_MODEL
        dy = err / D_MODEL

        dpp = (dy * gate).astype(BF16)
        da = (dy * pp * gate * (1.0 - gate)).astype(BF16)
        dn1 = _dot_nt(da, w_plg_ref[...])
        dw_pl_ref[...] += _dot_tn(pb, dpp)
        dw_plg_ref[...] += _dot_tn(n1, da)
        dg_pl_ref[...] += _colsum(dn1 * xh1)
        dxh = dn1 * g_pl
        dx1 = dy + r1 * (dxh - xh1 * (jnp.sum(dxh * xh1, axis=-1, keepdims=True) / D_MODEL))
        dx1_ref[...] = dx1
        dx1b = dx1.astype(BF16)
        dya = _dot_nt(dx1b, w_o_ref[0:ATTN_W, :])
        dyc = _dot_nt(dx1b, w_o_ref[ATTN_W:D_MODEL, :])

        dw_o_ref[0:ATTN_W, :] += _dot_tn(ya, dx1b)
        dg_oa_ref[...] += _colsum(dya * xa)
        dxa = dya * g_oa
        dga = ra * (dxa - xa * (jnp.sum(dxa * xa, axis=-1, keepdims=True) / ATTN_W))
        do = (dga * silu_a).astype(BF16)
        do_ref[...] = do
        dof = do.astype(F32) * ov
        for hd in range(N_HEADS):
            delta_ref[hd] = _lane_sum(dof[:, hd * V_DIM:(hd + 1) * V_DIM]).T[0:1, :]
        dtail_ref[:, 0:512] = (dga * ov * (sa * (1.0 + za * (1.0 - sa)))).astype(BF16)

        dw_o_ref[ATTN_W:D_MODEL, :] += _dot_tn(yc, dx1b)
        dg_oc_ref[...] += _colsum(dyc * xc)
        dxc = dyc * g_oc
        dgc = rc * (dxc - xc * (jnp.sum(dxc * xc, axis=-1, keepdims=True) / CONV_W))
        dtail_ref[:, 512:1024] = (dgc * u * silu_c).astype(BF16)
        du = dgc * cb * silu_c
        du_ref[...] = du
        dtail_ref[:, 1024:1536] = (dgc * cb * u * (sc * (1.0 + zc * (1.0 - sc)))).astype(BF16)
        dcw_ref[0:1, :] += _colsum(du * v2)
        dcw_ref[1:2, :] += _colsum(du * v1)
        dcw_ref[2:3, :] += _colsum(du * v)

    row = lambda i: (i, 0)
    col = lambda c: (lambda i: (i, c))
    halo = lambda c: (lambda i: (jnp.maximum(i * (tm // 8) - 1, 0), c))
    in_specs = [pl.BlockSpec((tm, D_MODEL), row), pl.BlockSpec((tm, ATTN_W), row)]
    in_specs += [pl.BlockSpec((tm, 512), col(c)) for c in (1, 2, 3, 4, 5)]
    in_specs += [pl.BlockSpec((8, 512), halo(3)), pl.BlockSpec((8, 512), halo(4))]
    in_specs += [pl.BlockSpec((tm, PLE), row), pl.BlockSpec((tm, D_MODEL), row),
                 _full((1, ATTN_W)), _full((1, CONV_W)), _full((1, D_MODEL)), _full((3, CONV_W)),
                 _full((D_MODEL, D_MODEL)), _full((PLE, D_MODEL)), _full((D_MODEL, D_MODEL))]
    out_specs = [pl.BlockSpec((tm, D_MODEL), row), pl.BlockSpec((tm, ATTN_W), row),
                 pl.BlockSpec((N_HEADS, 1, tm), lambda i: (0, 0, i)), pl.BlockSpec((tm, 1536), row),
                 pl.BlockSpec((tm, CONV_W), row),
                 _full((D_MODEL, D_MODEL)), _full((PLE, D_MODEL)), _full((D_MODEL, D_MODEL)),
                 _full((1, ATTN_W)), _full((1, CONV_W)), _full((1, D_MODEL)), _full((3, CONV_W)), _full((1, LANES))]
    out_shape = [jax.ShapeDtypeStruct((T, D_MODEL), F32), jax.ShapeDtypeStruct((T, ATTN_W), BF16),
                 jax.ShapeDtypeStruct((N_HEADS, 1, T), F32), jax.ShapeDtypeStruct((T, 1536), BF16),
                 jax.ShapeDtypeStruct((T, CONV_W), F32),
                 jax.ShapeDtypeStruct((D_MODEL, D_MODEL), F32), jax.ShapeDtypeStruct((PLE, D_MODEL), F32),
                 jax.ShapeDtypeStruct((D_MODEL, D_MODEL), F32),
                 jax.ShapeDtypeStruct((1, ATTN_W), F32), jax.ShapeDtypeStruct((1, CONV_W), F32),
                 jax.ShapeDtypeStruct((1, D_MODEL), F32), jax.ShapeDtypeStruct((3, CONV_W), F32),
                 jax.ShapeDtypeStruct((1, LANES), F32)]
    return pl.pallas_call(
        body, name="tail", grid=(nt,), in_specs=in_specs, out_specs=out_specs, out_shape=out_shape,
        compiler_params=_params(dimension_semantics=("arbitrary",)),
    )(x, o, proj, proj, proj, proj, proj, proj, proj, p, tgt, g_oa, g_oc, g_pl, conv_w, w_o, w_pl, w_plg)


def _attn_bwd(q, k, v, do, lse_row, delta_row, tk, swap):
    T = q.shape[1]
    tq = tk
    nq = T // tq
    rc = min(SOFTMAX_ROWS, tk)
    hk, hq = tk // 2, tq // 2
    n_swap = len(swap)

    def body(q_ref, k_ref, v_ref, do_ref, lse_ref, dl_ref, *rest):
        swap_in, (dq_ref, dk_ref, dv_ref), rest = rest[:n_swap], rest[n_swap:n_swap + 3], rest[n_swap + 3:]
        swap_out, (s0, s1, d0, d1, p0, p1, g0, g1, dk_acc, dv_acc), sems = rest[:n_swap], rest[n_swap:n_swap + 10], rest[n_swap + 10:]
        kj = pl.program_id(1)
        s_buf, dp_buf, p_buf, g_buf = (s0, s1), (d0, d1), (p0, p1), (g0, g1)

        if n_swap:
            start, drain = _swap_steps(n_swap, swap_in, swap_out, *sems)
            pl.when(jnp.logical_and(pl.program_id(0) == 0, kj == 0))(start)

        @pl.when(kj == 0)
        def _():
            dq_ref[...] = jnp.zeros_like(dq_ref)

        def q_start(t):
            return pl.multiple_of((nq - 1 - t) * tq, tq)

        def matmuls(t, slot):
            qs = q_start(t)
            s_buf[slot][...] = _dot_nt(k_ref[0], q_ref[0, pl.ds(qs, tq), :])
            dp_buf[slot][...] = _dot_nt(v_ref[0], do_ref[pl.ds(qs, tq), :])

        def pointwise(t, slot, masked):
            qs = q_start(t)
            lse2 = lse_ref[0, :, pl.ds(qs, tq)] * LOG2E
            dl = dl_ref[0, :, pl.ds(qs, tq)]
            for r0 in range(0, tk, rc):
                c0 = r0 // hk * hq if masked else 0
                rows, cols = slice(r0, r0 + rc), slice(c0, tq)
                st = s_buf[slot][rows, cols]
                if masked:
                    row = lax.broadcasted_iota(jnp.int32, (rc, tq - c0), 0) + r0
                    col = lax.broadcasted_iota(jnp.int32, (rc, tq - c0), 1) + c0
                    st = jnp.where(row <= col, st, NEG)
                pt = jnp.exp2(st * EXP2_SCALE - lse2[:, cols])
                p_buf[slot][rows, cols] = pt.astype(BF16)
                g_buf[slot][rows, cols] = (pt * (dp_buf[slot][rows, cols] - dl[:, cols]) * SCALE).astype(BF16)

        def accumulate(t, slot, masked):
            qs = q_start(t)
            p, g = p_buf[slot], g_buf[slot]
            if not masked:
                dv_acc[...] += _dot(p[...], do_ref[pl.ds(qs, tq), :])
                dk_acc[...] += _dot(g[...], q_ref[0, pl.ds(qs, tq), :])
                dq_ref[0, pl.ds(qs, tq), :] += _dot_tn(g[...], k_ref[0])
                return
            q2 = pl.multiple_of(qs + hq, hq)
            dv_acc[0:hk, :] += _dot(p[0:hk, :], do_ref[pl.ds(qs, tq), :])
            dv_acc[hk:tk, :] += _dot(p[hk:tk, hq:tq], do_ref[pl.ds(q2, hq), :])
            dk_acc[0:hk, :] += _dot(g[0:hk, :], q_ref[0, pl.ds(qs, tq), :])
            dk_acc[hk:tk, :] += _dot(g[hk:tk, hq:tq], q_ref[0, pl.ds(q2, hq), :])
            dq_ref[0, pl.ds(qs, hq), :] += _dot_tn(g[0:hk, 0:hq], k_ref[0, 0:hk, :])
            dq_ref[0, pl.ds(q2, hq), :] += _dot_tn(g[:, hq:tq], k_ref[0])

        def last():
            dk_ref[0] = dk_acc[...]
            dv_ref[0] = dv_acc[...]

        dk_acc[...] = jnp.zeros_like(dk_acc)
        dv_acc[...] = jnp.zeros_like(dv_acc)
        _chunk_pipeline(nq - 1 - kj, 1, matmuls, pointwise, accumulate, last)

        if n_swap:
            pl.when(jnp.logical_and(pl.program_id(0) == N_HEADS - 1, kj == T // tk - 1))(drain)

    outs = pl.pallas_call(
        body, name="attn_bwd", grid=(N_HEADS, T // tk),
        in_specs=[pl.BlockSpec((1, T, HEAD_PAD), lambda h, j: (h, 0, 0)),
                  pl.BlockSpec((1, tk, HEAD_PAD), lambda h, j: (h, j, 0)),
                  pl.BlockSpec((1, tk, V_DIM), lambda h, j: (h, j, 0)),
                  pl.BlockSpec((T, V_DIM), lambda h, j: (0, h)),
                  pl.BlockSpec((1, 1, T), lambda h, j: (h, 0, 0)),
                  pl.BlockSpec((1, 1, T), lambda h, j: (h, 0, 0))] + [_ANY] * n_swap,
        out_specs=[pl.BlockSpec((1, T, HEAD_PAD), lambda h, j: (h, 0, 0)),
                   pl.BlockSpec((1, tk, HEAD_PAD), lambda h, j: (h, j, 0)),
                   pl.BlockSpec((1, tk, V_DIM), lambda h, j: (h, j, 0))] + [_ANY] * n_swap,
        out_shape=[jax.ShapeDtypeStruct((N_HEADS, T, HEAD_PAD), F32), jax.ShapeDtypeStruct((N_HEADS, T, HEAD_PAD), F32),
                   jax.ShapeDtypeStruct((N_HEADS, T, V_DIM), F32)] + _swapped_shapes(swap, []),
        scratch_shapes=[pltpu.VMEM((tk, tq), F32)] * 4 + [pltpu.VMEM((tk, tq), BF16)] * 4
                       + [pltpu.VMEM((tk, HEAD_PAD), F32), pltpu.VMEM((tk, V_DIM), F32)]
                       + ([pltpu.SemaphoreType.DMA((n_swap,))] * 2 if n_swap else []),
        compiler_params=_params(dimension_semantics=("arbitrary", "arbitrary")),
    )(q, k, v, do, lse_row, delta_row, *swap)
    return outs[:3], outs[3:]


def _bwd_proj(x, dx1, pos, proj, dq, dk, dv, dtail, du, g_in, w_in, g_cq, w_uq, g_ckv, w_ukv, gq, gk, conv_w,
              invf, sgn, tm):
    T = x.shape[0]
    nt = T // tm

    ts = min(SUB_TILE, tm)

    def body(x_ref, dx1_ref, pos_ref, lat_ref, cc_ref, cx_ref, dq_ref, dk_ref, dv_ref, dtail_ref, du_ref, dun_ref, *rest):
        consts, (gx_ref, h_ref, dproj_ref), sums = rest[:11], rest[11:14], rest[14:]
        cw_ref = consts[8]
        i = pl.program_id(0)

        @pl.when(i == 0)
        def _():
            for r in sums:
                r[...] = jnp.zeros_like(r)

        du_v = du_ref[...]
        not_last = jnp.where(i < nt - 1, 1.0, 0.0)
        nx0 = dun_ref[0:1, :] * not_last
        nx1 = dun_ref[1:2, :] * not_last
        row = lax.broadcasted_iota(jnp.int32, du_v.shape, 0)
        du1 = jnp.where(row == tm - 1, nx0, pltpu.roll(du_v, tm - 1, 0))
        du2 = jnp.where(row == tm - 2, nx0, jnp.where(row == tm - 1, nx1, pltpu.roll(du_v, tm - 2, 0)))
        dvc = cw_ref[2:3, :] * du_v + cw_ref[1:2, :] * du1 + cw_ref[0:1, :] * du2
        dproj_ref[:, 1536:2048] = (dvc * cx_ref[...]).astype(BF16)
        dproj_ref[:, 2048:2560] = (dvc * cc_ref[...]).astype(BF16)

        for r0 in range(0, tm, ts):
            rows = slice(r0, r0 + ts)
            work(x_ref.at[rows, :], dx1_ref.at[rows, :], pos_ref.at[:, rows], lat_ref.at[rows, :],
                 dq_ref.at[:, rows, :], dk_ref.at[:, rows, :], dv_ref.at[:, rows, :], dtail_ref.at[rows, :], *consts,
                 gx_ref.at[rows, :], h_ref.at[:, rows], dproj_ref.at[rows, :], *sums)

    def work(x_ref, dx1_ref, pos_ref, lat_ref, dq_ref, dk_ref, dv_ref, dtail_ref,
             g_in_ref, w_in_ref, g_cq_ref, w_uq_ref, g_ckv_ref, w_ukv_ref, gq_ref, gk_ref, cw_ref, invf_ref, sgn_ref,
             gx_ref, h_ref, dproj_ref, dw_uq_ref, dw_ukv_ref, dg_in_ref, dg_cq_ref, dg_ckv_ref, dgq_ref, dgk_ref):
        xv = x_ref[...]
        r0 = _rep(_inv_rms_mxu(xv), D_MODEL)
        xh0 = xv * r0
        g_in = g_in_ref[...]
        h_ref[...] = (xh0 * g_in).astype(BF16).T

        c_q = lat_ref[:, 0:Q_LORA]
        rq = _rep(_inv_rms_mxu(c_q), Q_LORA)
        xq = c_q * rq
        g_cq = g_cq_ref[...]
        cqn = (xq * g_cq).astype(BF16)
        c_kv = lat_ref[:, Q_LORA:Q_LORA + KV_LORA]
        rkv = _inv_rms_mxu(c_kv)
        xkv = c_kv * rkv
        g_ckv = g_ckv_ref[...]
        ckvn = (xkv * g_ckv).astype(BF16)
        kpe = lat_ref[:, 384:512]
        kpe_sq = kpe * kpe
        cos_b, sin_b = _rope_tables(pos_ref, invf_ref, sgn_ref)
        gq_a, gq_b = gq_ref[:, 0:NOPE], gq_ref[:, NOPE:HEAD_PAD]
        gk_a, gk_b = gk_ref[:, 0:NOPE], gk_ref[:, NOPE:HEAD_PAD]

        dproj_ref[:, 512:1536] = dtail_ref[:, 0:1024]
        dproj_ref[:, 2560:3072] = dtail_ref[:, 1024:1536]

        def dh_part(c0):
            return _dot_nt(dproj_ref[:, c0:c0 + 512], w_in_ref[:, c0:c0 + 512])

        later_chunks = ((512,), (1024,), (1536, 2048), (2560,))
        dh = jnp.zeros((ts, D_MODEL), F32)
        acc = dict(dh=dh, dkpe=jnp.zeros((ts, LANES), F32), dcqn=jnp.zeros((ts, Q_LORA), F32),
                   dckvn=jnp.zeros((ts, KV_LORA), F32))

        def dh_chunks():
            for chunks in later_chunks:
                for chunk in chunks:
                    acc["dh"] = acc["dh"] + dh_part(chunk)
                    yield

        def queries(hd):
            qh = _dot(cqn, w_uq_ref[hd])
            yield
            a, b = qh[:, 0:NOPE], qh[:, NOPE:HEAD_PAD]
            r = lax.rsqrt(_lane_sum(a * a + b * b) / QK_DIM + EPS)
            yield
            xa, xb = a * r, b * r
            dan = dq_ref[hd, :, 0:NOPE]
            dbr = dq_ref[hd, :, NOPE:HEAD_PAD]
            dbn = dbr * cos_b + _swap_rope_halves(dbr * sin_b)
            yield
            dgq_ref[:, 0:NOPE] += _colsum(dan * xa)
            dgq_ref[:, NOPE:HEAD_PAD] += _colsum(dbn * xb)
            dxa, dxb = dan * gq_a, dbn * gq_b
            cq = _lane_sum(dxa * xa + dxb * xb) / QK_DIM
            yield
            dqh = jnp.concatenate([r * (dxa - xa * cq), r * (dxb - xb * cq)], axis=-1).astype(BF16)
            yield
            dw_uq_ref[hd] += _dot_tn(cqn, dqh)
            yield
            acc["dcqn"] = acc["dcqn"] + _dot_nt(dqh, w_uq_ref[hd])
            yield

        def keys(hd):
            kvh = _dot(ckvn, w_ukv_ref[hd])
            yield
            ka = kvh[:, 0:NOPE]
            rk = lax.rsqrt(_lane_sum(ka * ka + kpe_sq) / QK_DIM + EPS)
            yield
            xka, xkb = ka * rk, kpe * rk
            dkan = dk_ref[hd, :, 0:NOPE]
            dkbr = dk_ref[hd, :, NOPE:HEAD_PAD]
            dkbn = dkbr * cos_b + _swap_rope_halves(dkbr * sin_b)
            yield
            dgk_ref[:, 0:NOPE] += _colsum(dkan * xka)
            dgk_ref[:, NOPE:HEAD_PAD] += _colsum(dkbn * xkb)
            dxka, dxkb = dkan * gk_a, dkbn * gk_b
            ck = _lane_sum(dxka * xka + dxkb * xkb) / QK_DIM
            yield
            acc["dkpe"] = acc["dkpe"] + rk * (dxkb - xkb * ck)
            dkvh = jnp.concatenate([rk * (dxka - xka * ck), dv_ref[hd]], axis=-1).astype(BF16)
            yield
            dw_ukv_ref[hd] += _dot_tn(ckvn, dkvh)
            yield
            acc["dckvn"] = acc["dckvn"] + _dot_nt(dkvh, w_ukv_ref[hd])
            yield

        chains = [dh_chunks()]
        for hd in range(N_HEADS):
            chains += [queries(hd), keys(hd)]
        _round_robin(chains, 5)
        dh, dkpe, dcqn, dckvn = acc["dh"], acc["dkpe"], acc["dcqn"], acc["dckvn"]

        dg_cq_ref[...] += _colsum(dcqn * xq)
        dxq = dcqn * g_cq
        dproj_ref[:, 0:Q_LORA] = (rq * (dxq - xq * _rep(_lane_sum(dxq * xq) / Q_LORA, Q_LORA))).astype(BF16)
        dg_ckv_ref[...] += _colsum(dckvn * xkv)
        dxkv = dckvn * g_ckv
        dproj_ref[:, 256:384] = (rkv * (dxkv - xkv * (_lane_sum(dxkv * xkv) / KV_LORA))).astype(BF16)
        dproj_ref[:, 384:512] = dkpe.astype(BF16)
        dh = dh + dh_part(0)
        dg_in_ref[...] += _colsum(dh * xh0)
        dxh = dh * g_in
        gx_ref[...] = dx1_ref[...] + r0 * (dxh - xh0 * _rep(_lane_sum(dxh * xh0) / D_MODEL, D_MODEL))

    row = lambda i: (i, 0)
    col = lambda c: (lambda i: (i, c))
    head_rows = lambda i: (0, i, 0)
    nxt = lambda i: (jnp.minimum((i + 1) * (tm // 8), T // 8 - 1), 0)
    in_specs = [pl.BlockSpec((tm, D_MODEL), row), pl.BlockSpec((tm, D_MODEL), row), pl.BlockSpec((1, tm), lambda i: (0, i)),
                pl.BlockSpec((tm, 512), col(0)), pl.BlockSpec((tm, 512), col(3)), pl.BlockSpec((tm, 512), col(4)),
                pl.BlockSpec((N_HEADS, tm, HEAD_PAD), head_rows), pl.BlockSpec((N_HEADS, tm, HEAD_PAD), head_rows),
                pl.BlockSpec((N_HEADS, tm, V_DIM), head_rows), pl.BlockSpec((tm, 1536), row),
                pl.BlockSpec((tm, CONV_W), row), pl.BlockSpec((8, CONV_W), nxt),
                _full((1, D_MODEL)), _full((D_MODEL, PROJ_EXT)), _full((1, Q_LORA)), _full((N_HEADS, Q_LORA, HEAD_PAD)),
                _full((1, KV_LORA)), _full((N_HEADS, KV_LORA, HEAD_PAD)), _full((1, HEAD_PAD)), _full((1, HEAD_PAD)),
                _full((3, CONV_W)), _full((1, LANES)), _full((1, LANES))]
    out_specs = [pl.BlockSpec((tm, D_MODEL), row), pl.BlockSpec((D_MODEL, tm), lambda i: (0, i)),
                 pl.BlockSpec((tm, PROJ_EXT), row),
                 _full((N_HEADS, Q_LORA, HEAD_PAD)), _full((N_HEADS, KV_LORA, HEAD_PAD)),
                 _full((1, D_MODEL)), _full((1, Q_LORA)), _full((1, KV_LORA)), _full((1, HEAD_PAD)), _full((1, HEAD_PAD))]
    out_shape = [jax.ShapeDtypeStruct((T, D_MODEL), F32), jax.ShapeDtypeStruct((D_MODEL, T), BF16),
                 jax.ShapeDtypeStruct((T, PROJ_EXT), BF16),
                 jax.ShapeDtypeStruct((N_HEADS, Q_LORA, HEAD_PAD), F32), jax.ShapeDtypeStruct((N_HEADS, KV_LORA, HEAD_PAD), F32),
                 jax.ShapeDtypeStruct((1, D_MODEL), F32), jax.ShapeDtypeStruct((1, Q_LORA), F32),
                 jax.ShapeDtypeStruct((1, KV_LORA), F32), jax.ShapeDtypeStruct((1, HEAD_PAD), F32),
                 jax.ShapeDtypeStruct((1, HEAD_PAD), F32)]
    return pl.pallas_call(
        body, name="bwd_proj", grid=(nt,), in_specs=in_specs, out_specs=out_specs, out_shape=out_shape,
        compiler_params=_params(dimension_semantics=("arbitrary",)),
    )(x, dx1, pos, proj, proj, proj, dq, dk, dv, dtail, du, du, g_in, w_in, g_cq, w_uq, g_ckv, w_ukv, gq, gk, conv_w,
      invf, sgn)


def _matmul_acc(a, b, tt, tn, parts):
    M, T = a.shape
    N = b.shape[1]
    n = len(parts)
    grid = (N // tn, T // tt)
    hm = M // 2

    def body(a_ref, b_ref, *rest):
        part_refs, (o_ref, sib_ref), rest = rest[:n], rest[n:n + 2], rest[n + 2:]
        out_refs, (stage_ref, tile_send, tile_recv), sems = rest[:n], rest[n:n + 3], rest[n + 3:]
        j, t = pl.program_id(0), pl.program_id(1)
        if n:
            start, drain = _scatter_steps(part_refs, out_refs, *sems)
            pl.when(jnp.logical_and(j == 0, t == 0))(start)

        def to_sibling(jj):
            x, y, c = _mesh_pos()
            return _remote(stage_ref, sib_ref.at[:, pl.ds(pl.multiple_of(jj * tn, tn), tn)],
                           tile_send, tile_recv, jj, (x, y, 1 - c))

        @pl.when(t == 0)
        def _():
            o_ref[...] = jnp.zeros_like(o_ref)

        o_ref[...] += _dot(a_ref[...], b_ref[...])

        tile_done = t == grid[1] - 1
        pl.when(jnp.logical_and(tile_done, j > 0))(lambda: to_sibling(j - 1).wait())

        @pl.when(tile_done)
        def _():
            c = lax.axis_index("c")
            stage_ref[...] = o_ref[pl.ds(pl.multiple_of((1 - c) * hm, hm), hm), :]
            to_sibling(j).start()

        pl.when(jnp.logical_and(tile_done, j == grid[0] - 1))(lambda: to_sibling(j).wait())
        if n:
            pl.when(jnp.logical_and(j == grid[0] - 1, t == grid[1] - 1))(drain)

    sems = [pltpu.SemaphoreType.DMA((3 * n,)), pltpu.SemaphoreType.DMA((3 * n,)), pltpu.SemaphoreType.DMA((n,))]
    outs = pl.pallas_call(
        body, name="dw_in", grid=grid,
        in_specs=[pl.BlockSpec((M, tt), lambda j, t: (0, t)), pl.BlockSpec((tt, tn), lambda j, t: (t, j))] + [_ANY] * n,
        out_specs=[pl.BlockSpec((M, tn), lambda j, t: (0, j)), _ANY] + [_ANY] * n,
        out_shape=[jax.ShapeDtypeStruct((M, N), F32), jax.ShapeDtypeStruct((hm, N), F32)] + _scattered_shapes(parts),
        scratch_shapes=[pltpu.VMEM((hm, tn), F32)] + [pltpu.SemaphoreType.DMA((grid[0],))] * 2 + (sems if n else []),
        compiler_params=_params(dimension_semantics=("arbitrary", "arbitrary")),
    )(a, b, *parts)
    return outs[0], outs[1], outs[2:]


def _add_chips(parts, small_parts):
    arrays = list(parts) + [small_parts]

    def body(*refs):
        ins, outs = refs[:len(arrays)], refs[len(arrays):]
        for a_ref, o_ref in zip(ins, outs):
            part = lambda k: a_ref[k].astype(F32)
            o_ref[...] = ((part(0) + part(1)) + part(2)) + part(3)

    in_specs, out_specs, out_shape = [], [], []
    for a in arrays:
        _, rows, cols = a.shape
        in_specs.append(pl.BlockSpec((N_CHIPS, rows // 2, cols), lambda i: (0, i, 0)))
        out_specs.append(pl.BlockSpec((rows // 2, cols), lambda i: (i, 0)))
        out_shape.append(jax.ShapeDtypeStruct((rows, cols), F32))
    outs = pl.pallas_call(body, name="add_chips", grid=(2,), in_specs=in_specs, out_specs=out_specs,
                          out_shape=out_shape, compiler_params=_params(dimension_semantics=("arbitrary",)))(*arrays)
    return outs[:-1], outs[-1]


def _adamw_small(ws, gs, ms, vs):
    n = len(ws)

    def body(*refs):
        for i in range(n):
            w_ref, g_ref, m_ref, v_ref = (refs[k * n + i] for k in range(4))
            d_ref, nm_ref, nv_ref = (refs[(4 + k) * n + i] for k in range(3))
            _adamw_math(g_ref[...], w_ref, m_ref, v_ref, d_ref, nm_ref, nv_ref)

    shapes = [jax.ShapeDtypeStruct(w.shape, F32) for w in ws]
    outs = pl.pallas_call(body, name="adamw_small", out_shape=shapes * 3)(*ws, *gs, *ms, *vs)
    return outs[:n], outs[n:2 * n], outs[2 * n:]


def _adamw_math(gv, w_ref, m_ref, v_ref, d_ref, nm_ref, nv_ref):
    nm = B1 * m_ref[...] + (1.0 - B1) * gv
    nv = B2 * v_ref[...] + (1.0 - B2) * (gv * gv)
    m_hat = nm / (1.0 - B1 ** STEP)
    v_hat = nv / (1.0 - B2 ** STEP)
    d_ref[...] = -LR * (m_hat / (jnp.sqrt(v_hat) + ADAM_EPS) + WD * w_ref[...])
    nm_ref[...] = nm
    nv_ref[...] = nv


def _adamw_halves(w, mine, other, m, v, c, name, transposed):
    hr, cols = mine.shape

    def body(c_ref, w_ref, mine_ref, other_ref, m_ref, v_ref, g_ref, d_ref, nm_ref, nv_ref, *picked):
        gv = jnp.where(pl.program_id(0) == c_ref[0], mine_ref[...], other_ref[...])
        if transposed:
            picked[0][...] = gv
            _store_transposed(picked[0], g_ref)
            gv = g_ref[...]
        else:
            g_ref[...] = gv
        _adamw_math(gv, w_ref, m_ref, v_ref, d_ref, nm_ref, nv_ref)

    if transposed:
        half = pl.BlockSpec((cols, hr), lambda i, c_ref: (0, i))
    else:
        half = pl.BlockSpec((hr, cols), lambda i, c_ref: (i, 0))
    whole = pl.BlockSpec((hr, cols), lambda i, c_ref: (0, 0))
    shp = jax.ShapeDtypeStruct(w.shape, F32)
    return pl.pallas_call(
        body, name=name, out_shape=[shp] * 4,
        grid_spec=pltpu.PrefetchScalarGridSpec(num_scalar_prefetch=1, grid=(2,), in_specs=[half, whole, whole, half, half],
                                               out_specs=[half] * 4,
                                               scratch_shapes=[pltpu.VMEM((hr, cols), F32)] if transposed else []),
        compiler_params=_params(dimension_semantics=("arbitrary",)),
    )(c.reshape(1), w, mine, other, m, v)


_ANY = pl.BlockSpec(memory_space=pl.ANY)


def _mesh_pos():
    return lax.axis_index("x"), lax.axis_index("y"), lax.axis_index("c")


def _other_chips(x, y):
    return [(1 - x, y), (x, 1 - y), (1 - x, 1 - y)]


def _remote(src, dst, send_sems, recv_sems, k, to):
    return pltpu.make_async_remote_copy(src_ref=src, dst_ref=dst, send_sem=send_sems.at[k], recv_sem=recv_sems.at[k],
                                        device_id=to, device_id_type=MESH)


def _gather_weights(shards, n_transposed):
    n = len(shards)
    shapes = [s.shape[::-1] if i < n_transposed else s.shape for i, s in enumerate(shards)]

    def body(*refs):
        start, forward, drain = _gather_steps(shapes, refs[:n], refs[n:2 * n], refs[2 * n:3 * n], *refs[3 * n:])
        start()
        forward()
        drain()

    vmem = pl.BlockSpec(memory_space=pltpu.VMEM)
    return pl.pallas_call(
        body, name="gather_weights", in_specs=[vmem] * n, out_specs=[_ANY] * n,
        out_shape=_gathered_shapes(shapes), scratch_shapes=_gather_scratch(shapes), compiler_params=_params(),
    )(*shards)


def _travel_shape(shape):
    rows, cols = shape
    return (rows, HEAD_PAD if cols == QK_DIM else cols)


def _gathered_shapes(shapes):
    return [jax.ShapeDtypeStruct((N_CHIPS,) + _travel_shape(s), BF16) for s in shapes]


def _gather_scratch(shapes):
    n = len(shapes)
    return ([pltpu.VMEM(_travel_shape(s), BF16) for s in shapes]
            + [pltpu.SemaphoreType.DMA((6 * n,)), pltpu.SemaphoreType.DMA((6 * n,)), pltpu.SemaphoreType.DMA((n,))])


def _gather_steps(shapes, ins, outs, stage, send_sems, recv_sems, local_sems):
    n = len(shapes)
    halved = [s[0] % 32 == 0 for s in shapes]

    def part(i, ref, hc):
        if not halved[i]:
            return ref
        hr = shapes[i][0] // 2
        return ref.at[pl.ds(hc * hr, hr), :]

    def to_chip(i, j, x, y, c):
        cx, cy = _other_chips(x, y)[j]
        return _remote(part(i, stage[i], c), part(i, outs[i].at[2 * x + y], c), send_sems, recv_sems, 6 * i + j, (cx, cy, c))

    def to_sibling(i, j, x, y, c):
        cx, cy = _other_chips(x, y)[j]
        got = part(i, outs[i].at[2 * cx + cy], c)
        return _remote(got, got, send_sems, recv_sems, 6 * i + 3 + j, (x, y, 1 - c))

    def local(i, x, y):
        return pltpu.make_async_copy(stage[i], outs[i].at[2 * x + y], local_sems.at[i])

    def start():
        x, y, c = _mesh_pos()
        for i in range(n):
            cols = shapes[i][1]
            if stage[i].shape[1] != cols:
                stage[i][...] = jnp.zeros_like(stage[i])
            if ins[i].shape == shapes[i]:
                stage[i][:, 0:cols] = ins[i][...].astype(BF16)
            else:
                _store_transposed(ins[i], stage[i])
            local(i, x, y).start()
            for j in range(3):
                to_chip(i, j, x, y, c).start()

    def forward():
        x, y, c = _mesh_pos()
        for i in range(n):
            for j, (cx, cy) in enumerate(_other_chips(x, y)):
                got = part(i, outs[i].at[2 * cx + cy], c)
                _remote(got, got, send_sems, recv_sems, 6 * i + j, (cx, cy, c)).wait_recv()
                if halved[i]:
                    to_sibling(i, j, x, y, c).start()

    def drain():
        x, y, c = _mesh_pos()
        for i in range(n):
            for j, (cx, cy) in enumerate(_other_chips(x, y)):
                if halved[i]:
                    got = part(i, outs[i].at[2 * cx + cy], 1 - c)
                    _remote(got, got, send_sems, recv_sems, 6 * i + 3 + j, (x, y, 1 - c)).wait_recv()
                    to_sibling(i, j, x, y, c).wait_send()
                to_chip(i, j, x, y, c).wait_send()
            local(i, x, y).wait()

    return start, forward, drain


def _swap_halves(grads, whole, name):
    n, m = len(grads), len(grads) + len(whole)

    def body(*refs):
        start, drain = _swap_steps(n, refs[:m], refs[m:2 * m], refs[2 * m], refs[2 * m + 1])
        start()
        drain()

    outs = pl.pallas_call(
        body, name=name, in_specs=[_ANY] * m, out_specs=[_ANY] * m, out_shape=_swapped_shapes(grads, whole),
        scratch_shapes=[pltpu.SemaphoreType.DMA((m,)), pltpu.SemaphoreType.DMA((m,))],
    )(*grads, *whole)
    return outs[:n], outs[n:]


def _swapped_shapes(grads, whole):
    return ([jax.ShapeDtypeStruct((g.shape[0], g.shape[1] // 2, g.shape[2]), F32) for g in grads]
            + [jax.ShapeDtypeStruct(w.shape, F32) for w in whole])


def _swap_steps(n, ins, outs, send_sems, recv_sems):
    def copies():
        x, y, c = _mesh_pos()
        cps = []
        for i, src in enumerate(ins):
            if i < n:
                hr = src.shape[1] // 2
                src = src.at[:, pl.ds((1 - c) * hr, hr), :]
            cps.append(_remote(src, outs[i], send_sems, recv_sems, i, (x, y, 1 - c)))
        return cps

    def start():
        for cp in copies():
            cp.start()

    def drain():
        for cp in copies():
            cp.wait()

    return start, drain


def _scattered_shapes(parts):
    return [jax.ShapeDtypeStruct(p.shape if p.ndim == 3 else (N_CHIPS,) + p.shape, p.dtype) for p in parts]


def _scatter_steps(ins, outs, send_sems, recv_sems, local_sems):
    n = len(ins)

    def src(i, k):
        return ins[i].at[k] if len(ins[i].shape) == 3 else ins[i]

    def sends(x, y, c):
        return [_remote(src(i, 2 * cx + cy), outs[i].at[2 * x + y], send_sems, recv_sems, 3 * i + j, (cx, cy, c))
                for i in range(n) for j, (cx, cy) in enumerate(_other_chips(x, y))]

    def local(i, x, y):
        return pltpu.make_async_copy(src(i, 2 * x + y), outs[i].at[2 * x + y], local_sems.at[i])

    def start():
        x, y, c = _mesh_pos()
        for i in range(n):
            local(i, x, y).start()
        for cp in sends(x, y, c):
            cp.start()

    def drain():
        x, y, c = _mesh_pos()
        for i in range(n):
            for j, (cx, cy) in enumerate(_other_chips(x, y)):
                got = outs[i].at[2 * cx + cy]
                _remote(got, got, send_sems, recv_sems, 3 * i + j, (cx, cy, c)).wait_recv()
        for cp in sends(x, y, c):
            cp.wait_send()
        for i in range(n):
            local(i, x, y).wait()

    return start, drain


def _add_pair(grads, from_sibling, small, small_sibling, c):
    n = len(grads)

    def body(c_ref, *refs):
        ins, outs = refs[:2 * n + 2], refs[2 * n + 2:]
        for i in range(n + 1):
            outs[i][...] = (ins[2 * i][...] + ins[2 * i + 1][...]).astype(outs[i].dtype)

    in_specs, out_specs, out_shape, args = [], [], [], []
    for g, r in zip(grads, from_sibling):
        _, hr, cols = r.shape
        in_specs += [pl.BlockSpec((1, hr, cols), lambda k, c_ref: (k, c_ref[0], 0)),
                     pl.BlockSpec((1, hr, cols), lambda k, c_ref: (k, 0, 0))]
        out_specs.append(pl.BlockSpec((1, hr, cols), lambda k, c_ref: (k, 0, 0)))
        out_shape.append(jax.ShapeDtypeStruct(r.shape, BF16))
        args += [g, r]
    whole = pl.BlockSpec(small.shape, lambda k, c_ref: (0, 0))
    in_specs += [whole, whole]
    out_specs.append(whole)
    out_shape.append(jax.ShapeDtypeStruct(small.shape, F32))
    outs = pl.pallas_call(
        body, name="add_pair", out_shape=out_shape,
        grid_spec=pltpu.PrefetchScalarGridSpec(num_scalar_prefetch=1, grid=(N_CHIPS,), in_specs=in_specs,
                                               out_specs=out_specs),
        compiler_params=_params(dimension_semantics=("arbitrary",)),
    )(c.reshape(1), *args, small, small_sibling)
    return outs[:n], outs[n]


def _scatter_w_in(dw_in_e, from_sibling):
    hr = from_sibling.shape[1]
    shard = (N_CHIPS, hr, SHARD_COLS_IN)

    def body(g_in, r_in, out, g_buf, r_buf, p_buf, load_sems, send_sems, recv_sems, local_sems):
        c = lax.axis_index("c")
        loads = (pltpu.make_async_copy(g_in.at[0, pl.ds(c * hr, hr), :], g_buf, load_sems.at[0]),
                 pltpu.make_async_copy(r_in.at[0], r_buf, load_sems.at[1]))
        for cp in loads:
            cp.start()
        for cp in loads:
            cp.wait()
        g_buf[...] += r_buf[...]
        p_buf[0, :, 0:KPE_END] = g_buf[:, 0:KPE_END].astype(BF16)
        p_buf[0, :, KPE_END:SHARD_COLS_IN] = g_buf[:, KPE_END + KPE_PAD:SHARD_COLS_IN + KPE_PAD].astype(BF16)
        for k in range(1, N_CHIPS):
            p_buf[k] = g_buf[:, SHARD_COLS_IN * k + KPE_PAD:SHARD_COLS_IN * (k + 1) + KPE_PAD].astype(BF16)
        start, drain = _scatter_steps([p_buf], [out], send_sems, recv_sems, local_sems)
        start()
        drain()

    return pl.pallas_call(
        body, name="scatter_grads", in_specs=[_ANY] * 2, out_specs=_ANY, out_shape=jax.ShapeDtypeStruct(shard, BF16),
        scratch_shapes=[pltpu.VMEM((hr, PROJ_EXT), F32)] * 2 + [pltpu.VMEM(shard, BF16)]
                       + [pltpu.SemaphoreType.DMA((2,)), pltpu.SemaphoreType.DMA((3,)), pltpu.SemaphoreType.DMA((3,)),
                          pltpu.SemaphoreType.DMA((1,))],
        compiler_params=_params(),
    )(dw_in_e, from_sibling)


def _share_halves(halves):
    n = len(halves)

    def body(*refs):
        ins, outs, send_sems, recv_sems = refs[:n], refs[n:2 * n], refs[2 * n], refs[2 * n + 1]
        x, y, c = _mesh_pos()
        cps = [_remote(ins[i], outs[i], send_sems, recv_sems, i, (x, y, 1 - c)) for i in range(n)]
        for cp in cps:
            cp.start()
        for cp in cps:
            cp.wait()

    return pl.pallas_call(
        body, name="share_halves", in_specs=[_ANY] * n, out_specs=[_ANY] * n,
        out_shape=[jax.ShapeDtypeStruct(h.shape, h.dtype) for h in halves],
        scratch_shapes=[pltpu.SemaphoreType.DMA((n,)), pltpu.SemaphoreType.DMA((n,))],
    )(*halves)


SHARD_COLS_IN = IN_TOTAL // N_CHIPS
KPE_END = Q_LORA + KV_LORA + ROPE
KPE_PAD = PROJ_EXT - IN_TOTAL


def _by_cols(a):
    return a.transpose(1, 0, 2).reshape(a.shape[1], N_CHIPS * a.shape[2])


def _assemble_early(c_in, c_uq, c_ukv, c_conv):
    return c_in, c_uq, c_ukv, _by_cols(c_conv).astype(F32)


def _assemble_late(c_o, c_pl, c_plg):
    return c_o.reshape(D_MODEL, D_MODEL), _by_cols(c_pl), c_plg.reshape(D_MODEL, D_MODEL)


def _split_late(dw_o, dw_pl, dw_plg):
    chip_major = lambda a: a.reshape(a.shape[0], N_CHIPS, a.shape[1] // N_CHIPS).transpose(1, 0, 2)
    return [dw_o.reshape(N_CHIPS, D_MODEL // N_CHIPS, D_MODEL), chip_major(dw_pl),
            dw_plg.reshape(N_CHIPS, D_MODEL // N_CHIPS, D_MODEL)]


def _local_step(x, p, pos, tgt, gains, early, late_shards, late_gathered, tm, tq):
    c_in, w_uq_e, w_ukv, conv_w = early
    g_in, g_cq, g_ckv, g_q, g_k, g_oa, g_oc, g_pl = gains
    T = x.shape[0]
    zpad = lambda a, n: jnp.concatenate([a, jnp.zeros(a.shape[:-1] + (n,), a.dtype)], axis=-1)
    gq, gk = zpad(g_q, HEAD_PAD - QK_DIM), zpad(g_k, HEAD_PAD - QK_DIM)
    inv_freq = 1.0 / (ROPE_THETA ** (jnp.arange(0, ROPE, 2, dtype=F32) / ROPE))
    invf = jnp.concatenate([inv_freq, inv_freq, jnp.zeros((64,), F32)]).reshape(1, LANES)
    sgn = jnp.concatenate([-jnp.ones((32,), F32), jnp.ones((32,), F32), jnp.zeros((64,), F32)]).reshape(1, LANES)

    (proj, q, k, v, w_in_e), gathered = _fwd_proj(x, pos, g_in, c_in, g_cq, w_uq_e, g_ckv, w_ukv, gq, gk, invf, sgn,
                                                  late_shards, min(2 * tm, T))
    w_o, w_pl, w_plg = _assemble_late(*(gathered if late_shards else late_gathered))
    o, lse = _attn_fwd(q, k, v, tq)
    (dx1, do, delta, dtail, du, dw_o, dw_pl, dw_plg, dg_oa, dg_oc, dg_pl, dconv, loss) = _tail(
        x, o, proj, p, tgt, g_oa, g_oc, g_pl, conv_w, w_o, w_pl, w_plg, tm)
    late_grads = _split_late(dw_o, dw_pl, dw_plg)
    (dq, dk, dv), late_sibling = _attn_bwd(q, k, v, do, lse, delta, tq, late_grads)
    (gx, h, dproj, dw_uq_e, dw_ukv, dg_in, dg_cq, dg_ckv, dgq, dgk) = _bwd_proj(
        x, dx1, pos, proj, dq, dk, dv, dtail, du, g_in, w_in_e, g_cq, w_uq_e, g_ckv, w_ukv, gq, gk, conv_w, invf, sgn, tm)
    wgrads = [dw_uq_e[:, :, :QK_DIM], dw_ukv, *late_grads]
    ggrads = (dg_in, dg_cq, dg_ckv, dgq, dgk, dg_oa, dg_oc, dg_pl)
    return loss, gx, (h, dproj), wgrads, late_sibling, ggrads, dconv


def kernel(x, p, positions, g_in, w_in, g_cq, w_uq, g_ckv, w_ukv, g_q, g_k, conv_w, g_oa, g_oc, w_o, w_pl, w_plg, g_pl, loss_target, m_g_in, m_w_in, m_g_cq, m_w_uq, m_g_ckv, m_w_ukv, m_g_q, m_g_k, m_conv_w, m_g_oa, m_g_oc, m_w_o, m_w_pl, m_w_plg, m_g_pl, v_g_in, v_w_in, v_g_cq, v_w_uq, v_g_ckv, v_w_ukv, v_g_q, v_g_k, v_conv_w, v_g_oa, v_g_oc, v_w_o, v_w_pl, v_w_plg, v_g_pl):
    T = x.shape[1]
    c = lax.axis_index("c")
    chip = 2 * lax.axis_index("x") + lax.axis_index("y")
    gains = [g.reshape(1, -1) for g in (g_in, g_cq, g_ckv, g_q, g_k, g_oa, g_oc, g_pl)]

    transposed = ("w_in", "w_uq")
    early = _assemble_early(*_gather_weights([w_in[0].T, w_uq[0].T, w_ukv[0], conv_w[0]], len(transposed)))

    loss, gx, (h_t, dproj), others_cm, late_sibling, ggrads, dconv = _local_step(
        x[0], p[0, 0], positions.reshape(1, T), loss_target[0], gains, early, [w_o[0], w_pl[0], w_plg[0]], None, 256, 512)

    small_parts = [a.reshape(-1, LANES) for a in (*ggrads, loss, dconv)]
    small_rows = [a.shape[0] for a in small_parts]
    tile_rows = [-(-r // 8) * 8 for r in small_rows]
    tile_rows[-1] += -sum(tile_rows) % 16
    small = jnp.concatenate([jnp.pad(a, ((0, t - r), (0, 0))) for a, r, t in zip(small_parts, small_rows, tile_rows)])
    n_early = len(others_cm) - len(late_sibling)
    early_sibling, (small_sibling,) = _swap_halves(others_cm[:n_early], [small], "pair_grads")
    chip_parts, chip_small = _add_pair(others_cm, [*early_sibling, *late_sibling], small, small_sibling, c)
    dw_in_e, w_in_sibling, exchanged = _matmul_acc(h_t, dproj, min(4096, T), 512, [*chip_parts, chip_small])
    by_chip = [_scatter_w_in(dw_in_e[None], w_in_sibling[None]), *exchanged[:-1]]
    halves, small_total = _add_chips(by_chip, exchanged[-1])
    other_halves = _share_halves(halves)

    gg, off = [], 0
    for rows, tiled in zip(small_rows, tile_rows):
        gg.append(small_total[off:off + rows].reshape(1, -1))
        off += tiled
    loss_out = gg[8][0, 0]
    conv_total = gg[9].reshape(3, CONV_W)
    conv_g = lax.dynamic_slice(conv_total, (0, chip * (CONV_W // N_CHIPS)), (3, CONV_W // N_CHIPS))
    g_by_name = dict(g_in=gg[0], g_cq=gg[1], g_ckv=gg[2], g_q=gg[3][:, :QK_DIM], g_k=gg[4][:, :QK_DIM], conv_w=conv_g,
                     g_oa=gg[5], g_oc=gg[6], g_pl=gg[7])
    half_by_name = dict(zip(("w_in", "w_uq", "w_ukv", "w_o", "w_pl", "w_plg"), zip(halves, other_halves)))
    weights = dict(g_in=g_in, w_in=w_in, g_cq=g_cq, w_uq=w_uq, g_ckv=g_ckv, w_ukv=w_ukv, g_q=g_q, g_k=g_k,
                   conv_w=conv_w, g_oa=g_oa, g_oc=g_oc, w_o=w_o, w_pl=w_pl, w_plg=w_plg, g_pl=g_pl)
    ms = dict(g_in=m_g_in, w_in=m_w_in, g_cq=m_g_cq, w_uq=m_w_uq, g_ckv=m_g_ckv, w_ukv=m_w_ukv, g_q=m_g_q, g_k=m_g_k,
              conv_w=m_conv_w, g_oa=m_g_oa, g_oc=m_g_oc, w_o=m_w_o, w_pl=m_w_pl, w_plg=m_w_plg, g_pl=m_g_pl)
    vs = dict(g_in=v_g_in, w_in=v_w_in, g_cq=v_g_cq, w_uq=v_w_uq, g_ckv=v_g_ckv, w_ukv=v_w_ukv, g_q=v_g_q, g_k=v_g_k,
              conv_w=v_conv_w, g_oa=v_g_oa, g_oc=v_g_oc, w_o=v_w_o, w_pl=v_w_pl, w_plg=v_w_plg, g_pl=v_g_pl)
    names = list(weights)
    flat = lambda a: a.reshape(-1, a.shape[-1])
    small_names = list(g_by_name)
    one_row = lambda a: a.reshape(1, -1)
    small_out = _adamw_small([one_row(weights[n]) for n in small_names], [one_row(g_by_name[n]) for n in small_names],
                             [one_row(ms[n]) for n in small_names], [one_row(vs[n]) for n in small_names])
    results = {n: (g_by_name[n], *(out[i] for out in small_out)) for i, n in enumerate(small_names)}
    for n in half_by_name:
        shard = (lambda a: a[0].T) if n in transposed else flat
        out = _adamw_halves(shard(weights[n]), *half_by_name[n], shard(ms[n]), shard(vs[n]), c, "adamw_" + n,
                            n in transposed)
        results[n] = [a.T for a in out] if n in transposed else out
    per_kind = [[results[n][kind].reshape(weights[n].shape) for n in names] for kind in range(4)]
    return (loss_out, gx.reshape(x.shape), *per_kind[0], *per_kind[1], *per_kind[2], *per_kind[3])
```

```python
import math

import jax
import jax.numpy as jnp
from jax import lax
from jax.experimental import pallas as pl
from jax.experimental.pallas import tpu as pltpu

F32 = jnp.float32
BF16 = jnp.bfloat16

D_MODEL = 1024
N_HEADS = 4
NOPE = 128
ROPE = 64
V_DIM = 128
QK_DIM = NOPE + ROPE
HEAD_PAD = 256
Q_LORA = 256
KV_LORA = 128
ATTN_W = 512
CONV_W = 512
PLE = 256
IN_TOTAL = 3008
PROJ_EXT = 3072
ROPE_THETA = 10000.0
EPS = 1e-6
SCALE = 1.0 / math.sqrt(QK_DIM)
LOG2E = math.log2(math.e)
EXP2_SCALE = SCALE * LOG2E
NEG = -1e30
SOFTMAX_ROWS = 32
SUB_TILE = 256

LR, B1, B2, ADAM_EPS, WD, STEP = 0.001, 0.9, 0.999, 1e-08, 0.01, 10

N_CHIPS = 4
LANES = 128
VMEM_LIMIT = 56 * 1024 * 1024
MESH = pl.DeviceIdType.MESH


def _params(**kw):
    return pltpu.CompilerParams(vmem_limit_bytes=VMEM_LIMIT, **kw)


def _inv_rms(x, n):
    return lax.rsqrt(jnp.sum(x * x, axis=-1, keepdims=True) / n + EPS)


def _lane_sum(a):
    folded = a[:, 0:LANES]
    for c0 in range(LANES, a.shape[1], LANES):
        folded = folded + a[:, c0:c0 + LANES]
    head = folded.astype(BF16)
    tail = (folded - head.astype(F32)).astype(BF16)
    return _dot(jnp.concatenate([head, tail], axis=1), jnp.ones((2 * LANES, LANES), BF16))


def _inv_rms_mxu(x):
    return lax.rsqrt(_lane_sum(x * x) / x.shape[1] + EPS)


def _rep(r, width):
    return r if width == LANES else jnp.tile(r, (1, width // LANES))


def _sigmoid(z):
    return jax.nn.sigmoid(z)


def _swap_rope_halves(b):
    lane = lax.broadcasted_iota(jnp.int32, b.shape, 1)
    swapped = jnp.where(lane < 32, pltpu.roll(b, 96, 1), pltpu.roll(b, 32, 1))
    return jnp.where(lane < ROPE, swapped, 0.0)


def _dot(a, b):
    return jnp.dot(a, b, preferred_element_type=F32)


def _dot_nt(a, b):
    return lax.dot_general(a, b, (((1,), (1,)), ((), ())), preferred_element_type=F32)


def _dot_tn(a, b):
    return lax.dot_general(a, b, (((0,), (0,)), ((), ())), preferred_element_type=F32)


def _colsum(a):
    return jnp.sum(a, axis=0, keepdims=True)


def _store_transposed(src_ref, dst_ref):
    r, c = src_ref.shape
    for r0 in range(0, r, LANES):
        h = min(LANES, r - r0)
        for c0 in range(0, c, LANES):
            w = min(LANES, c - c0)
            piece = src_ref[r0:r0 + h, c0 + w - LANES:c0 + w]
            if h < LANES:
                piece = jnp.concatenate([piece, jnp.zeros((LANES - h, LANES), piece.dtype)], axis=0)
            dst_ref[c0:c0 + w, r0:r0 + h] = piece.T[LANES - w:, 0:h].astype(dst_ref.dtype)


def _full(shape):
    return pl.BlockSpec(shape, lambda *_: (0,) * len(shape))


def _round_robin(chains, width):
    waiting, active = list(chains), []
    while waiting or active:
        while waiting and len(active) < width:
            active.append(waiting.pop(0))
        for chain in list(active):
            if next(chain, _DONE) is _DONE:
                active.remove(chain)


_DONE = object()


def _rope_tables(pos_ref, invf_ref, sgn_ref):
    pos = jnp.broadcast_to(pos_ref[...].astype(F32), (LANES, pos_ref.shape[1])).T
    ang = pos * invf_ref[...]
    return jnp.cos(ang), jnp.sin(ang) * sgn_ref[...]


def _fwd_proj(x, pos, g_in, c_in, g_cq, w_uq, g_ckv, w_ukv, gq, gk, invf, sgn, late_shards, tm):
    T = x.shape[0]
    nt = T // tm
    n_late = len(late_shards)
    ts = min(SUB_TILE, tm)

    def body(x_ref, pos_ref, g_in_ref, c_in_ref, g_cq_ref, w_uq_ref, g_ckv_ref, w_ukv_ref, gq_ref, gk_ref,
             invf_ref, sgn_ref, *rest):
        late_in, (proj_ref, q_ref, k_ref, v_ref, w_in_ref) = rest[:n_late], rest[n_late:n_late + 5]
        late_out, late_scratch = rest[n_late + 5:2 * n_late + 5], rest[2 * n_late + 5:]
        i = pl.program_id(0)

        @pl.when(i == 0)
        def _():
            w_in_ref[:, 0:KPE_END] = c_in_ref[0, :, 0:KPE_END]
            w_in_ref[:, KPE_END:KPE_END + KPE_PAD] = jnp.zeros((D_MODEL, KPE_PAD), BF16)
            w_in_ref[:, KPE_END + KPE_PAD:SHARD_COLS_IN + KPE_PAD] = c_in_ref[0, :, KPE_END:SHARD_COLS_IN]
            for chip in range(1, N_CHIPS):
                w_in_ref[:, SHARD_COLS_IN * chip + KPE_PAD:SHARD_COLS_IN * (chip + 1) + KPE_PAD] = c_in_ref[chip]

        if n_late:
            start, forward, drain = _gather_steps([s.shape for s in late_shards], late_in, late_out,
                                                  late_scratch[:n_late], *late_scratch[n_late:])
            pl.when(i == 0)(start)
            pl.when(i == nt // 2)(forward)

        for r0 in range(0, tm, ts):
            rows = slice(r0, r0 + ts)
            xv = x_ref[rows, :]
            h = (xv * _rep(_inv_rms_mxu(xv), D_MODEL) * g_in_ref[...]).astype(BF16)
            lat = _dot(h, w_in_ref[:, 0:512])
            proj_ref[rows, 0:512] = lat
            c_q = lat[:, 0:Q_LORA]
            cqn = (c_q * _rep(_inv_rms_mxu(c_q), Q_LORA) * g_cq_ref[...]).astype(BF16)
            c_kv = lat[:, Q_LORA:Q_LORA + KV_LORA]
            ckvn = (c_kv * _inv_rms_mxu(c_kv) * g_ckv_ref[...]).astype(BF16)
            kpe = lat[:, 384:512]
            kpe_sq = kpe * kpe
            cos_b, sin_b = _rope_tables(pos_ref.at[:, rows], invf_ref, sgn_ref)
            gq_a, gq_b = gq_ref[:, 0:NOPE], gq_ref[:, NOPE:HEAD_PAD]
            gk_a, gk_b = gk_ref[:, 0:NOPE], gk_ref[:, NOPE:HEAD_PAD]

            def projections(rows=rows, h=h):
                for c0 in range(512, PROJ_EXT, 512):
                    proj_ref[rows, c0:c0 + 512] = _dot(h, w_in_ref[:, c0:c0 + 512])
                    yield

            def queries(hd, rows=rows, cqn=cqn, cos_b=cos_b, sin_b=sin_b, gq_a=gq_a, gq_b=gq_b):
                qh = _dot(cqn, w_uq_ref[hd])
                yield
                a, b = qh[:, 0:NOPE], qh[:, NOPE:HEAD_PAD]
                r = lax.rsqrt(_lane_sum(a * a + b * b) / QK_DIM + EPS)
                yield
                bn = b * r * gq_b
                q_ref[hd, rows, 0:NOPE] = (a * r * gq_a).astype(BF16)
                q_ref[hd, rows, NOPE:HEAD_PAD] = (bn * cos_b + _swap_rope_halves(bn) * sin_b).astype(BF16)
                yield

            def keys(hd, rows=rows, ckvn=ckvn, kpe=kpe, kpe_sq=kpe_sq, cos_b=cos_b, sin_b=sin_b, gk_a=gk_a, gk_b=gk_b):
                kvh = _dot(ckvn, w_ukv_ref[hd])
                yield
                ka = kvh[:, 0:NOPE]
                rk = lax.rsqrt(_lane_sum(ka * ka + kpe_sq) / QK_DIM + EPS)
                yield
                kbn = kpe * rk * gk_b
                k_ref[hd, rows, 0:NOPE] = (ka * rk * gk_a).astype(BF16)
                k_ref[hd, rows, NOPE:HEAD_PAD] = (kbn * cos_b + _swap_rope_halves(kbn) * sin_b).astype(BF16)
                v_ref[hd, rows, 0:V_DIM] = kvh[:, NOPE:HEAD_PAD].astype(BF16)
                v_ref[hd, rows, V_DIM:2 * V_DIM] = jnp.ones((ts, V_DIM), BF16)
                yield

            chains = [projections()]
            for hd in range(N_HEADS):
                chains += [queries(hd), keys(hd)]
            _round_robin(chains, 4)

        if n_late:
            pl.when(i == nt - 1)(drain)

    row = lambda i: (i, 0)
    head_rows = lambda i: (0, i, 0)
    outs = pl.pallas_call(
        body, name="fwd_proj", grid=(nt,),
        in_specs=[pl.BlockSpec((tm, D_MODEL), row), pl.BlockSpec((1, tm), lambda i: (0, i)), _full((1, D_MODEL)),
                  _full((N_CHIPS, D_MODEL, SHARD_COLS_IN)), _full((1, Q_LORA)), _full((N_HEADS, Q_LORA, HEAD_PAD)),
                  _full((1, KV_LORA)), _full((N_HEADS, KV_LORA, HEAD_PAD)), _full((1, HEAD_PAD)), _full((1, HEAD_PAD)),
                  _full((1, LANES)), _full((1, LANES))] + [_full(s.shape) for s in late_shards],
        out_specs=[pl.BlockSpec((tm, PROJ_EXT), row), pl.BlockSpec((N_HEADS, tm, HEAD_PAD), head_rows),
                   pl.BlockSpec((N_HEADS, tm, HEAD_PAD), head_rows), pl.BlockSpec((N_HEADS, tm, 2 * V_DIM), head_rows),
                   _full((D_MODEL, PROJ_EXT))] + [_ANY] * n_late,
        out_shape=[jax.ShapeDtypeStruct((T, PROJ_EXT), F32), jax.ShapeDtypeStruct((N_HEADS, T, HEAD_PAD), BF16),
                   jax.ShapeDtypeStruct((N_HEADS, T, HEAD_PAD), BF16), jax.ShapeDtypeStruct((N_HEADS, T, 2 * V_DIM), BF16),
                   jax.ShapeDtypeStruct((D_MODEL, PROJ_EXT), BF16)] + _gathered_shapes([s.shape for s in late_shards]),
        scratch_shapes=_gather_scratch([s.shape for s in late_shards]) if n_late else [],
        compiler_params=_params(dimension_semantics=("arbitrary",)),
    )(x, pos, g_in, c_in, g_cq, w_uq, g_ckv, w_ukv, gq, gk, invf, sgn, *late_shards)
    return outs[:5], outs[5:]


def _chunk_pipeline(n_loop, lag, matmuls, pointwise, accumulate, last):
    slots = lag + 1

    def iteration(t, slot, pending=True):
        matmuls(jnp.minimum(t + lag, n_loop), (slot + lag) % slots)
        if pending:
            accumulate(t - lag, (slot + 1) % slots, False)
        pointwise(t, slot, False)

    def finish(slot, pending):
        for back in range(pending, 0, -1):
            accumulate(n_loop - back, (slot - back) % slots, False)
        pointwise(n_loop, slot, True)
        accumulate(n_loop, slot, True)
        last()

    for u in range(lag):
        matmuls(jnp.minimum(u, n_loop), u)
    for u in range(lag):
        pl.when(u < n_loop)(lambda u=u: iteration(u, u, pending=False))

    n_main = jnp.maximum(n_loop - lag, 0)

    def unrolled(tt, carry):
        for j in range(slots):
            iteration(lag + slots * tt + j, (lag + j) % slots)
        return carry

    lax.fori_loop(0, n_main // slots, unrolled, 0)
    rest = lax.rem(n_main, slots)
    t0 = n_loop - rest

    for r in range(slots):
        @pl.when(jnp.logical_and(n_loop >= lag, rest == r))
        def _():
            for j in range(r):
                iteration(t0 + j, (lag + j) % slots)
            finish((lag + r) % slots, lag)

    for short in range(lag):
        pl.when(n_loop == short)(lambda short=short: finish(short, short))


def _attn_fwd(q, k, v, tq):
    T = q.shape[1]
    tk = tq
    rc = min(SOFTMAX_ROWS, tq)

    def body(q_ref, k_ref, v_ref, o_ref, lse_ref, s0, s1, s2, p0, p1, p2, a0, a1, a2, m_ref, acc_ref):
        qi = pl.program_id(1)
        s_buf, p_buf, a_buf = (s0, s1, s2), (p0, p1, p2), (a0, a1, a2)

        def scores(t, slot):
            ks = pl.multiple_of(t * tk, tk)
            s_buf[slot][...] = _dot_nt(q_ref[0], k_ref[0, pl.ds(ks, tk), :])

        def blocks(masked):
            return ((0, tq // 2, tk // 2), (tq // 2, tq // 2, tk)) if masked else ((0, tq, tk),)

        def values(t, slot, masked):
            ks = pl.multiple_of(t * tk, tk)
            for q0, nq, nk in blocks(masked):
                rows = slice(q0, q0 + nq)
                acc_ref[rows, :] = (acc_ref[rows, :] * a_buf[slot][rows, :]
                                    + _dot(p_buf[slot][rows, 0:nk], v_ref[0, pl.ds(ks, nk), :]))

        def softmax(t, slot, masked):
            for q0, nq, nk in blocks(masked):
                rows = slice(q0, q0 + nq)
                s_all = s_buf[slot][rows, 0:nk]
                if masked:
                    row = lax.broadcasted_iota(jnp.int32, (nq, nk), 0) + q0
                    col = lax.broadcasted_iota(jnp.int32, (nq, nk), 1)
                    s_all = jnp.where(col <= row, s_all, NEG)
                    s_buf[slot][rows, 0:nk] = s_all
                m_old = m_ref[rows, :]
                m_new = jnp.maximum(m_old, jnp.max(s_all, axis=1, keepdims=True))
                a_buf[slot][rows, :] = jnp.exp2((m_old - m_new) * EXP2_SCALE)
                m_ref[rows, :] = m_new
                for r0 in range(0, nq, rc):
                    s = s_buf[slot][q0 + r0:q0 + r0 + rc, 0:nk]
                    p_buf[slot][q0 + r0:q0 + r0 + rc, 0:nk] = jnp.exp2((s - m_new[r0:r0 + rc, :]) * EXP2_SCALE).astype(BF16)

        def last():
            l = acc_ref[:, V_DIM:2 * V_DIM]
            o_ref[...] = acc_ref[:, 0:V_DIM] / l
            lse_ref[0] = (m_ref[...] * SCALE + jnp.log(l)).T[0:1, :]

        m_ref[...] = jnp.full_like(m_ref, NEG)
        acc_ref[...] = jnp.zeros_like(acc_ref)
        _chunk_pipeline(qi, 2, scores, softmax, values, last)

    return pl.pallas_call(
        body, name="attn_fwd", grid=(N_HEADS, T // tq),
        in_specs=[pl.BlockSpec((1, tq, HEAD_PAD), lambda h, i: (h, i, 0)),
                  pl.BlockSpec((1, T, HEAD_PAD), lambda h, i: (h, 0, 0)),
                  pl.BlockSpec((1, T, 2 * V_DIM), lambda h, i: (h, 0, 0))],
        out_specs=[pl.BlockSpec((tq, V_DIM), lambda h, i: (i, h)),
                   pl.BlockSpec((1, 1, tq), lambda h, i: (h, 0, i))],
        out_shape=[jax.ShapeDtypeStruct((T, ATTN_W), F32), jax.ShapeDtypeStruct((N_HEADS, 1, T), F32)],
        scratch_shapes=[pltpu.VMEM((tq, tk), F32)] * 3 + [pltpu.VMEM((tq, tk), BF16)] * 3
                       + [pltpu.VMEM((tq, 1), F32)] * 4 + [pltpu.VMEM((tq, 2 * V_DIM), F32)],
        compiler_params=_params(dimension_semantics=("arbitrary", "arbitrary")),
    )(q, k, v)


def _tail(x, o, proj, p, tgt, g_oa, g_oc, g_pl, conv_w, w_o, w_pl, w_plg, tm):
    T = x.shape[0]
    nt = T // tm

    def body(x_ref, o_ref, za_ref, cb_ref, cc_ref, cx_ref, zc_ref, cch_ref, cxh_ref, p_ref, tgt_ref,
             g_oa_ref, g_oc_ref, g_pl_ref, cw_ref, w_o_ref, w_pl_ref, w_plg_ref,
             dx1_ref, do_ref, delta_ref, dtail_ref, du_ref,
             dw_o_ref, dw_pl_ref, dw_plg_ref, dg_oa_ref, dg_oc_ref, dg_pl_ref, dcw_ref, loss_ref):
        i = pl.program_id(0)

        @pl.when(i == 0)
        def _():
            for r in (dw_o_ref, dw_pl_ref, dw_plg_ref, dg_oa_ref, dg_oc_ref, dg_pl_ref, dcw_ref, loss_ref):
                r[...] = jnp.zeros_like(r)

        g_oa, g_oc, g_pl = g_oa_ref[...], g_oc_ref[...], g_pl_ref[...]
        w0, w1, w2 = cw_ref[0:1, :], cw_ref[1:2, :], cw_ref[2:3, :]

        xv, ov, za, cb, zc = x_ref[...], o_ref[...], za_ref[...], cb_ref[...], zc_ref[...]
        pb = p_ref[...].astype(BF16)
        pp = _dot(pb, w_pl_ref[...])

        sa = _sigmoid(za)
        silu_a = za * sa
        ga = ov * silu_a
        ra = _inv_rms(ga, ATTN_W)
        xa = ga * ra
        ya = (xa * g_oa).astype(BF16)
        x1_a = _dot(ya, w_o_ref[0:ATTN_W, :])
        v = cc_ref[...] * cx_ref[...]
        not_first = jnp.where(i > 0, 1.0, 0.0)
        hv6 = cch_ref[6:7, :] * cxh_ref[6:7, :] * not_first
        hv7 = cch_ref[7:8, :] * cxh_ref[7:8, :] * not_first
        row = lax.broadcasted_iota(jnp.int32, v.shape, 0)
        v1 = jnp.where(row == 0, hv7, pltpu.roll(v, 1, 0))
        v2 = jnp.where(row == 0, hv6, jnp.where(row == 1, hv7, pltpu.roll(v, 2, 0)))
        u = w0 * v2 + w1 * v1 + w2 * v
        sc = _sigmoid(zc)
        silu_c = zc * sc
        gc = cb * u * silu_c
        rc = _inv_rms(gc, CONV_W)
        xc = gc * rc
        yc = (xc * g_oc).astype(BF16)
        x1 = xv + (x1_a + _dot(yc, w_o_ref[ATTN_W:D_MODEL, :]))
        r1 = _inv_rms(x1, D_MODEL)
        xh1 = x1 * r1
        n1 = (xh1 * g_pl).astype(BF16)
        gate = _sigmoid(_dot(n1, w_plg_ref[...]))
        err = x1 + gate * pp - tgt_ref[...]
        loss_ref[...] += 0.5 * jnp.sum(err * err) / D_MODEL
        dy = err / D_MODEL

        dpp = (dy * gate).astype(BF16)
        da = (dy * pp * gate * (1.0 - gate)).astype(BF16)
        dn1 = _dot_nt(da, w_plg_ref[...])
        dw_pl_ref[...] += _dot_tn(pb, dpp)
        dw_plg_ref[...] += _dot_tn(n1, da)
        dg_pl_ref[...] += _colsum(dn1 * xh1)
        dxh = dn1 * g_pl
        dx1 = dy + r1 * (dxh - xh1 * (jnp.sum(dxh * xh1, axis=-1, keepdims=True) / D_MODEL))
        dx1_ref[...] = dx1
        dx1b = dx1.astype(BF16)
        dya = _dot_nt(dx1b, w_o_ref[0:ATTN_W, :])
        dyc = _dot_nt(dx1b, w_o_ref[ATTN_W:D_MODEL, :])

        dw_o_ref[0:ATTN_W, :] += _dot_tn(ya, dx1b)
        dg_oa_ref[...] += _colsum(dya * xa)
        dxa = dya * g_oa
        dga = ra * (dxa - xa * (jnp.sum(dxa * xa, axis=-1, keepdims=True) / ATTN_W))
        do = (dga * silu_a).astype(BF16)
        do_ref[...] = do
        dof = do.astype(F32) * ov
        for hd in range(N_HEADS):
            delta_ref[hd] = _lane_sum(dof[:, hd * V_DIM:(hd + 1) * V_DIM]).T[0:1, :]
        dtail_ref[:, 0:512] = (dga * ov * (sa * (1.0 + za * (1.0 - sa)))).astype(BF16)

        dw_o_ref[ATTN_W:D_MODEL, :] += _dot_tn(yc, dx1b)
        dg_oc_ref[...] += _colsum(dyc * xc)
        dxc = dyc * g_oc
        dgc = rc * (dxc - xc * (jnp.sum(dxc * xc, axis=-1, keepdims=True) / CONV_W))
        dtail_ref[:, 512:1024] = (dgc * u * silu_c).astype(BF16)
        du = dgc * cb * silu_c
        du_ref[...] = du
        dtail_ref[:, 1024:1536] = (dgc * cb * u * (sc * (1.0 + zc * (1.0 - sc)))).astype(BF16)
        dcw_ref[0:1, :] += _colsum(du * v2)
        dcw_ref[1:2, :] += _colsum(du * v1)
        dcw_ref[2:3, :] += _colsum(du * v)

    row = lambda i: (i, 0)
    col = lambda c: (lambda i: (i, c))
    halo = lambda c: (lambda i: (jnp.maximum(i * (tm // 8) - 1, 0), c))
    in_specs = [pl.BlockSpec((tm, D_MODEL), row), pl.BlockSpec((tm, ATTN_W), row)]
    in_specs += [pl.BlockSpec((tm, 512), col(c)) for c in (1, 2, 3, 4, 5)]
    in_specs += [pl.BlockSpec((8, 512), halo(3)), pl.BlockSpec((8, 512), halo(4))]
    in_specs += [pl.BlockSpec((tm, PLE), row), pl.BlockSpec((tm, D_MODEL), row),
                 _full((1, ATTN_W)), _full((1, CONV_W)), _full((1, D_MODEL)), _full((3, CONV_W)),
                 _full((D_MODEL, D_MODEL)), _full((PLE, D_MODEL)), _full((D_MODEL, D_MODEL))]
    out_specs = [pl.BlockSpec((tm, D_MODEL), row), pl.BlockSpec((tm, ATTN_W), row),
                 pl.BlockSpec((N_HEADS, 1, tm), lambda i: (0, 0, i)), pl.BlockSpec((tm, 1536), row),
                 pl.BlockSpec((tm, CONV_W), row),
                 _full((D_MODEL, D_MODEL)), _full((PLE, D_MODEL)), _full((D_MODEL, D_MODEL)),
                 _full((1, ATTN_W)), _full((1, CONV_W)), _full((1, D_MODEL)), _full((3, CONV_W)), _full((1, LANES))]
    out_shape = [jax.ShapeDtypeStruct((T, D_MODEL), F32), jax.ShapeDtypeStruct((T, ATTN_W), BF16),
                 jax.ShapeDtypeStruct((N_HEADS, 1, T), F32), jax.ShapeDtypeStruct((T, 1536), BF16),
                 jax.ShapeDtypeStruct((T, CONV_W), F32),
                 jax.ShapeDtypeStruct((D_MODEL, D_MODEL), F32), jax.ShapeDtypeStruct((PLE, D_MODEL), F32),
                 jax.ShapeDtypeStruct((D_MODEL, D_MODEL), F32),
                 jax.ShapeDtypeStruct((1, ATTN_W), F32), jax.ShapeDtypeStruct((1, CONV_W), F32),
                 jax.ShapeDtypeStruct((1, D_MODEL), F32), jax.ShapeDtypeStruct((3, CONV_W), F32),
                 jax.ShapeDtypeStruct((1, LANES), F32)]
    return pl.pallas_call(
        body, name="tail", grid=(nt,), in_specs=in_specs, out_specs=out_specs, out_shape=out_shape,
        compiler_params=_params(dimension_semantics=("arbitrary",)),
    )(x, o, proj, proj, proj, proj, proj, proj, proj, p, tgt, g_oa, g_oc, g_pl, conv_w, w_o, w_pl, w_plg)


def _attn_bwd(q, k, v, do, lse_row, delta_row, tk, swap):
    T = q.shape[1]
    tq = tk
    nq = T // tq
    rc = min(SOFTMAX_ROWS, tk)
    hk, hq = tk // 2, tq // 2
    n_swap = len(swap)

    def body(q_ref, k_ref, v_ref, do_ref, lse_ref, dl_ref, *rest):
        swap_in, (dq_ref, dk_ref, dv_ref), rest = rest[:n_swap], rest[n_swap:n_swap + 3], rest[n_swap + 3:]
        swap_out, (s0, s1, d0, d1, p0, p1, g0, g1, dk_acc, dv_acc), sems = rest[:n_swap], rest[n_swap:n_swap + 10], rest[n_swap + 10:]
        kj = pl.program_id(1)
        s_buf, dp_buf, p_buf, g_buf = (s0, s1), (d0, d1), (p0, p1), (g0, g1)

        if n_swap:
            start, drain = _swap_steps(n_swap, swap_in, swap_out, *sems)
            pl.when(jnp.logical_and(pl.program_id(0) == 0, kj == 0))(start)

        @pl.when(kj == 0)
        def _():
            dq_ref[...] = jnp.zeros_like(dq_ref)

        def q_start(t):
            return pl.multiple_of((nq - 1 - t) * tq, tq)

        def matmuls(t, slot):
            qs = q_start(t)
            s_buf[slot][...] = _dot_nt(k_ref[0], q_ref[0, pl.ds(qs, tq), :])
            dp_buf[slot][...] = _dot_nt(v_ref[0], do_ref[pl.ds(qs, tq), :])

        def pointwise(t, slot, masked):
            qs = q_start(t)
            lse2 = lse_ref[0, :, pl.ds(qs, tq)] * LOG2E
            dl = dl_ref[0, :, pl.ds(qs, tq)]
            for r0 in range(0, tk, rc):
                c0 = r0 // hk * hq if masked else 0
                rows, cols = slice(r0, r0 + rc), slice(c0, tq)
                st = s_buf[slot][rows, cols]
                if masked:
                    row = lax.broadcasted_iota(jnp.int32, (rc, tq - c0), 0) + r0
                    col = lax.broadcasted_iota(jnp.int32, (rc, tq - c0), 1) + c0
                    st = jnp.where(row <= col, st, NEG)
                pt = jnp.exp2(st * EXP2_SCALE - lse2[:, cols])
                p_buf[slot][rows, cols] = pt.astype(BF16)
                g_buf[slot][rows, cols] = (pt * (dp_buf[slot][rows, cols] - dl[:, cols]) * SCALE).astype(BF16)

        def accumulate(t, slot, masked):
            qs = q_start(t)
            p, g = p_buf[slot], g_buf[slot]
            if not masked:
                dv_acc[...] += _dot(p[...], do_ref[pl.ds(qs, tq), :])
                dk_acc[...] += _dot(g[...], q_ref[0, pl.ds(qs, tq), :])
                dq_ref[0, pl.ds(qs, tq), :] += _dot_tn(g[...], k_ref[0])
                return
            q2 = pl.multiple_of(qs + hq, hq)
            dv_acc[0:hk, :] += _dot(p[0:hk, :], do_ref[pl.ds(qs, tq), :])
            dv_acc[hk:tk, :] += _dot(p[hk:tk, hq:tq], do_ref[pl.ds(q2, hq), :])
            dk_acc[0:hk, :] += _dot(g[0:hk, :], q_ref[0, pl.ds(qs, tq), :])
            dk_acc[hk:tk, :] += _dot(g[hk:tk, hq:tq], q_ref[0, pl.ds(q2, hq), :])
            dq_ref[0, pl.ds(qs, hq), :] += _dot_tn(g[0:hk, 0:hq], k_ref[0, 0:hk, :])
            dq_ref[0, pl.ds(q2, hq), :] += _dot_tn(g[:, hq:tq], k_ref[0])

        def last():
            dk_ref[0] = dk_acc[...]
            dv_ref[0] = dv_acc[...]

        dk_acc[...] = jnp.zeros_like(dk_acc)
        dv_acc[...] = jnp.zeros_like(dv_acc)
        _chunk_pipeline(nq - 1 - kj, 1, matmuls, pointwise, accumulate, last)

        if n_swap:
            pl.when(jnp.logical_and(pl.program_id(0) == N_HEADS - 1, kj == T // tk - 1))(drain)

    outs = pl.pallas_call(
        body, name="attn_bwd", grid=(N_HEADS, T // tk),
        in_specs=[pl.BlockSpec((1, T, HEAD_PAD), lambda h, j: (h, 0, 0)),
                  pl.BlockSpec((1, tk, HEAD_PAD), lambda h, j: (h, j, 0)),
                  pl.BlockSpec((1, tk, V_DIM), lambda h, j: (h, j, 0)),
                  pl.BlockSpec((T, V_DIM), lambda h, j: (0, h)),
                  pl.BlockSpec((1, 1, T), lambda h, j: (h, 0, 0)),
                  pl.BlockSpec((1, 1, T), lambda h, j: (h, 0, 0))] + [_ANY] * n_swap,
        out_specs=[pl.BlockSpec((1, T, HEAD_PAD), lambda h, j: (h, 0, 0)),
                   pl.BlockSpec((1, tk, HEAD_PAD), lambda h, j: (h, j, 0)),
                   pl.BlockSpec((1, tk, V_DIM), lambda h, j: (h, j, 0))] + [_ANY] * n_swap,
        out_shape=[jax.ShapeDtypeStruct((N_HEADS, T, HEAD_PAD), F32), jax.ShapeDtypeStruct((N_HEADS, T, HEAD_PAD), F32),
                   jax.ShapeDtypeStruct((N_HEADS, T, V_DIM), F32)] + _swapped_shapes(swap, []),
        scratch_shapes=[pltpu.VMEM((tk, tq), F32)] * 4 + [pltpu.VMEM((tk, tq), BF16)] * 4
                       + [pltpu.VMEM((tk, HEAD_PAD), F32), pltpu.VMEM((tk, V_DIM), F32)]
                       + ([pltpu.SemaphoreType.DMA((n_swap,))] * 2 if n_swap else []),
        compiler_params=_params(dimension_semantics=("arbitrary", "arbitrary")),
    )(q, k, v, do, lse_row, delta_row, *swap)
    return outs[:3], outs[3:]


def _bwd_proj(x, dx1, pos, proj, dq, dk, dv, dtail, du, g_in, w_in, g_cq, w_uq, g_ckv, w_ukv, gq, gk, conv_w,
              invf, sgn, tm):
    T = x.shape[0]
    nt = T // tm

    ts = min(SUB_TILE, tm)

    def body(x_ref, dx1_ref, pos_ref, lat_ref, cc_ref, cx_ref, dq_ref, dk_ref, dv_ref, dtail_ref, du_ref, dun_ref, *rest):
        consts, (gx_ref, h_ref, dproj_ref), sums = rest[:11], rest[11:14], rest[14:]
        cw_ref = consts[8]
        i = pl.program_id(0)

        @pl.when(i == 0)
        def _():
            for r in sums:
                r[...] = jnp.zeros_like(r)

        du_v = du_ref[...]
        not_last = jnp.where(i < nt - 1, 1.0, 0.0)
        nx0 = dun_ref[0:1, :] * not_last
        nx1 = dun_ref[1:2, :] * not_last
        row = lax.broadcasted_iota(jnp.int32, du_v.shape, 0)
        du1 = jnp.where(row == tm - 1, nx0, pltpu.roll(du_v, tm - 1, 0))
        du2 = jnp.where(row == tm - 2, nx0, jnp.where(row == tm - 1, nx1, pltpu.roll(du_v, tm - 2, 0)))
        dvc = cw_ref[2:3, :] * du_v + cw_ref[1:2, :] * du1 + cw_ref[0:1, :] * du2
        dproj_ref[:, 1536:2048] = (dvc * cx_ref[...]).astype(BF16)
        dproj_ref[:, 2048:2560] = (dvc * cc_ref[...]).astype(BF16)

        for r0 in range(0, tm, ts):
            rows = slice(r0, r0 + ts)
            work(x_ref.at[rows, :], dx1_ref.at[rows, :], pos_ref.at[:, rows], lat_ref.at[rows, :],
                 dq_ref.at[:, rows, :], dk_ref.at[:, rows, :], dv_ref.at[:, rows, :], dtail_ref.at[rows, :], *consts,
                 gx_ref.at[rows, :], h_ref.at[:, rows], dproj_ref.at[rows, :], *sums)

    def work(x_ref, dx1_ref, pos_ref, lat_ref, dq_ref, dk_ref, dv_ref, dtail_ref,
             g_in_ref, w_in_ref, g_cq_ref, w_uq_ref, g_ckv_ref, w_ukv_ref, gq_ref, gk_ref, cw_ref, invf_ref, sgn_ref,
             gx_ref, h_ref, dproj_ref, dw_uq_ref, dw_ukv_ref, dg_in_ref, dg_cq_ref, dg_ckv_ref, dgq_ref, dgk_ref):
        xv = x_ref[...]
        r0 = _rep(_inv_rms_mxu(xv), D_MODEL)
        xh0 = xv * r0
        g_in = g_in_ref[...]
        h_ref[...] = (xh0 * g_in).astype(BF16).T

        c_q = lat_ref[:, 0:Q_LORA]
        rq = _rep(_inv_rms_mxu(c_q), Q_LORA)
        xq = c_q * rq
        g_cq = g_cq_ref[...]
        cqn = (xq * g_cq).astype(BF16)
        c_kv = lat_ref[:, Q_LORA:Q_LORA + KV_LORA]
        rkv = _inv_rms_mxu(c_kv)
        xkv = c_kv * rkv
        g_ckv = g_ckv_ref[...]
        ckvn = (xkv * g_ckv).astype(BF16)
        kpe = lat_ref[:, 384:512]
        kpe_sq = kpe * kpe
        cos_b, sin_b = _rope_tables(pos_ref, invf_ref, sgn_ref)
        gq_a, gq_b = gq_ref[:, 0:NOPE], gq_ref[:, NOPE:HEAD_PAD]
        gk_a, gk_b = gk_ref[:, 0:NOPE], gk_ref[:, NOPE:HEAD_PAD]

        dproj_ref[:, 512:1536] = dtail_ref[:, 0:1024]
        dproj_ref[:, 2560:3072] = dtail_ref[:, 1024:1536]

        def dh_part(c0):
            return _dot_nt(dproj_ref[:, c0:c0 + 512], w_in_ref[:, c0:c0 + 512])

        later_chunks = ((512,), (1024,), (1536, 2048), (2560,))
        dh = jnp.zeros((ts, D_MODEL), F32)
        acc = dict(dh=dh, dkpe=jnp.zeros((ts, LANES), F32), dcqn=jnp.zeros((ts, Q_LORA), F32),
                   dckvn=jnp.zeros((ts, KV_LORA), F32))

        def dh_chunks():
            for chunks in later_chunks:
                for chunk in chunks:
                    acc["dh"] = acc["dh"] + dh_part(chunk)
                    yield

        def queries(hd):
            qh = _dot(cqn, w_uq_ref[hd])
            yield
            a, b = qh[:, 0:NOPE], qh[:, NOPE:HEAD_PAD]
            r = lax.rsqrt(_lane_sum(a * a + b * b) / QK_DIM + EPS)
            yield
            xa, xb = a * r, b * r
            dan = dq_ref[hd, :, 0:NOPE]
            dbr = dq_ref[hd, :, NOPE:HEAD_PAD]
            dbn = dbr * cos_b + _swap_rope_halves(dbr * sin_b)
            yield
            dgq_ref[:, 0:NOPE] += _colsum(dan * xa)
            dgq_ref[:, NOPE:HEAD_PAD] += _colsum(dbn * xb)
            dxa, dxb = dan * gq_a, dbn * gq_b
            cq = _lane_sum(dxa * xa + dxb * xb) / QK_DIM
            yield
            dqh = jnp.concatenate([r * (dxa - xa * cq), r * (dxb - xb * cq)], axis=-1).astype(BF16)
            yield
            dw_uq_ref[hd] += _dot_tn(cqn, dqh)
            yield
            acc["dcqn"] = acc["dcqn"] + _dot_nt(dqh, w_uq_ref[hd])
            yield

        def keys(hd):
            kvh = _dot(ckvn, w_ukv_ref[hd])
            yield
            ka = kvh[:, 0:NOPE]
            rk = lax.rsqrt(_lane_sum(ka * ka + kpe_sq) / QK_DIM + EPS)
            yield
            xka, xkb = ka * rk, kpe * rk
            dkan = dk_ref[hd, :, 0:NOPE]
            dkbr = dk_ref[hd, :, NOPE:HEAD_PAD]
            dkbn = dkbr * cos_b + _swap_rope_halves(dkbr * sin_b)
            yield
            dgk_ref[:, 0:NOPE] += _colsum(dkan * xka)
            dgk_ref[:, NOPE:HEAD_PAD] += _colsum(dkbn * xkb)
            dxka, dxkb = dkan * gk_a, dkbn * gk_b
            ck = _lane_sum(dxka * xka + dxkb * xkb) / QK_DIM
            yield
            acc["dkpe"] = acc["dkpe"] + rk * (dxkb - xkb * ck)
            dkvh = jnp.concatenate([rk * (dxka - xka * ck), dv_ref[hd]], axis=-1).astype(BF16)
            yield
            dw_ukv_ref[hd] += _dot_tn(ckvn, dkvh)
            yield
            acc["dckvn"] = acc["dckvn"] + _dot_nt(dkvh, w_ukv_ref[hd])
            yield

        chains = [dh_chunks()]
        for hd in range(N_HEADS):
            chains += [queries(hd), keys(hd)]
        _round_robin(chains, 5)
        dh, dkpe, dcqn, dckvn = acc["dh"], acc["dkpe"], acc["dcqn"], acc["dckvn"]

        dg_cq_ref[...] += _colsum(dcqn * xq)
        dxq = dcqn * g_cq
        dproj_ref[:, 0:Q_LORA] = (rq * (dxq - xq * _rep(_lane_sum(dxq * xq) / Q_LORA, Q_LORA))).astype(BF16)
        dg_ckv_ref[...] += _colsum(dckvn * xkv)
        dxkv = dckvn * g_ckv
        dproj_ref[:, 256:384] = (rkv * (dxkv - xkv * (_lane_sum(dxkv * xkv) / KV_LORA))).astype(BF16)
        dproj_ref[:, 384:512] = dkpe.astype(BF16)
        dh = dh + dh_part(0)
        dg_in_ref[...] += _colsum(dh * xh0)
        dxh = dh * g_in
        gx_ref[...] = dx1_ref[...] + r0 * (dxh - xh0 * _rep(_lane_sum(dxh * xh0) / D_MODEL, D_MODEL))

    row = lambda i: (i, 0)
    col = lambda c: (lambda i: (i, c))
    head_rows = lambda i: (0, i, 0)
    nxt = lambda i: (jnp.minimum((i + 1) * (tm // 8), T // 8 - 1), 0)
    in_specs = [pl.BlockSpec((tm, D_MODEL), row), pl.BlockSpec((tm, D_MODEL), row), pl.BlockSpec((1, tm), lambda i: (0, i)),
                pl.BlockSpec((tm, 512), col(0)), pl.BlockSpec((tm, 512), col(3)), pl.BlockSpec((tm, 512), col(4)),
                pl.BlockSpec((N_HEADS, tm, HEAD_PAD), head_rows), pl.BlockSpec((N_HEADS, tm, HEAD_PAD), head_rows),
                pl.BlockSpec((N_HEADS, tm, V_DIM), head_rows), pl.BlockSpec((tm, 1536), row),
                pl.BlockSpec((tm, CONV_W), row), pl.BlockSpec((8, CONV_W), nxt),
                _full((1, D_MODEL)), _full((D_MODEL, PROJ_EXT)), _full((1, Q_LORA)), _full((N_HEADS, Q_LORA, HEAD_PAD)),
                _full((1, KV_LORA)), _full((N_HEADS, KV_LORA, HEAD_PAD)), _full((1, HEAD_PAD)), _full((1, HEAD_PAD)),
                _full((3, CONV_W)), _full((1, LANES)), _full((1, LANES))]
    out_specs = [pl.BlockSpec((tm, D_MODEL), row), pl.BlockSpec((D_MODEL, tm), lambda i: (0, i)),
                 pl.BlockSpec((tm, PROJ_EXT), row),
                 _full((N_HEADS, Q_LORA, HEAD_PAD)), _full((N_HEADS, KV_LORA, HEAD_PAD)),
                 _full((1, D_MODEL)), _full((1, Q_LORA)), _full((1, KV_LORA)), _full((1, HEAD_PAD)), _full((1, HEAD_PAD))]
    out_shape = [jax.ShapeDtypeStruct((T, D_MODEL), F32), jax.ShapeDtypeStruct((D_MODEL, T), BF16),
                 jax.ShapeDtypeStruct((T, PROJ_EXT), BF16),
                 jax.ShapeDtypeStruct((N_HEADS, Q_LORA, HEAD_PAD), F32), jax.ShapeDtypeStruct((N_HEADS, KV_LORA, HEAD_PAD), F32),
                 jax.ShapeDtypeStruct((1, D_MODEL), F32), jax.ShapeDtypeStruct((1, Q_LORA), F32),
                 jax.ShapeDtypeStruct((1, KV_LORA), F32), jax.ShapeDtypeStruct((1, HEAD_PAD), F32),
                 jax.ShapeDtypeStruct((1, HEAD_PAD), F32)]
    return pl.pallas_call(
        body, name="bwd_proj", grid=(nt,), in_specs=in_specs, out_specs=out_specs, out_shape=out_shape,
        compiler_params=_params(dimension_semantics=("arbitrary",)),
    )(x, dx1, pos, proj, proj, proj, dq, dk, dv, dtail, du, du, g_in, w_in, g_cq, w_uq, g_ckv, w_ukv, gq, gk, conv_w,
      invf, sgn)


def _matmul_acc(a, b, tt, tn, parts):
    M, T = a.shape
    N = b.shape[1]
    n = len(parts)
    grid = (N // tn, T // tt)
    hm = M // 2

    def body(a_ref, b_ref, *rest):
        part_refs, (o_ref, sib_ref), rest = rest[:n], rest[n:n + 2], rest[n + 2:]
        out_refs, (stage_ref, tile_send, tile_recv), sems = rest[:n], rest[n:n + 3], rest[n + 3:]
        j, t = pl.program_id(0), pl.program_id(1)
        if n:
            start, drain = _scatter_steps(part_refs, out_refs, *sems)
            pl.when(jnp.logical_and(j == 0, t == 0))(start)

        def to_sibling(jj):
            x, y, c = _mesh_pos()
            return _remote(stage_ref, sib_ref.at[:, pl.ds(pl.multiple_of(jj * tn, tn), tn)],
                           tile_send, tile_recv, jj, (x, y, 1 - c))

        @pl.when(t == 0)
        def _():
            o_ref[...] = jnp.zeros_like(o_ref)

        o_ref[...] += _dot(a_ref[...], b_ref[...])

        tile_done = t == grid[1] - 1
        pl.when(jnp.logical_and(tile_done, j > 0))(lambda: to_sibling(j - 1).wait())

        @pl.when(tile_done)
        def _():
            c = lax.axis_index("c")
            stage_ref[...] = o_ref[pl.ds(pl.multiple_of((1 - c) * hm, hm), hm), :]
            to_sibling(j).start()

        pl.when(jnp.logical_and(tile_done, j == grid[0] - 1))(lambda: to_sibling(j).wait())
        if n:
            pl.when(jnp.logical_and(j == grid[0] - 1, t == grid[1] - 1))(drain)

    sems = [pltpu.SemaphoreType.DMA((3 * n,)), pltpu.SemaphoreType.DMA((3 * n,)), pltpu.SemaphoreType.DMA((n,))]
    outs = pl.pallas_call(
        body, name="dw_in", grid=grid,
        in_specs=[pl.BlockSpec((M, tt), lambda j, t: (0, t)), pl.BlockSpec((tt, tn), lambda j, t: (t, j))] + [_ANY] * n,
        out_specs=[pl.BlockSpec((M, tn), lambda j, t: (0, j)), _ANY] + [_ANY] * n,
        out_shape=[jax.ShapeDtypeStruct((M, N), F32), jax.ShapeDtypeStruct((hm, N), F32)] + _scattered_shapes(parts),
        scratch_shapes=[pltpu.VMEM((hm, tn), F32)] + [pltpu.SemaphoreType.DMA((grid[0],))] * 2 + (sems if n else []),
        compiler_params=_params(dimension_semantics=("arbitrary", "arbitrary")),
    )(a, b, *parts)
    return outs[0], outs[1], outs[2:]


def _add_chips(parts, small_parts):
    arrays = list(parts) + [small_parts]

    def body(*refs):
        ins, outs = refs[:len(arrays)], refs[len(arrays):]
        for a_ref, o_ref in zip(ins, outs):
            part = lambda k: a_ref[k].astype(F32)
            o_ref[...] = ((part(0) + part(1)) + part(2)) + part(3)

    in_specs, out_specs, out_shape = [], [], []
    for a in arrays:
        _, rows, cols = a.shape
        in_specs.append(pl.BlockSpec((N_CHIPS, rows // 2, cols), lambda i: (0, i, 0)))
        out_specs.append(pl.BlockSpec((rows // 2, cols), lambda i: (i, 0)))
        out_shape.append(jax.ShapeDtypeStruct((rows, cols), F32))
    outs = pl.pallas_call(body, name="add_chips", grid=(2,), in_specs=in_specs, out_specs=out_specs,
                          out_shape=out_shape, compiler_params=_params(dimension_semantics=("arbitrary",)))(*arrays)
    return outs[:-1], outs[-1]


def _adamw_small(ws, gs, ms, vs):
    n = len(ws)

    def body(*refs):
        for i in range(n):
            w_ref, g_ref, m_ref, v_ref = (refs[k * n + i] for k in range(4))
            d_ref, nm_ref, nv_ref = (refs[(4 + k) * n + i] for k in range(3))
            _adamw_math(g_ref[...], w_ref, m_ref, v_ref, d_ref, nm_ref, nv_ref)

    shapes = [jax.ShapeDtypeStruct(w.shape, F32) for w in ws]
    outs = pl.pallas_call(body, name="adamw_small", out_shape=shapes * 3)(*ws, *gs, *ms, *vs)
    return outs[:n], outs[n:2 * n], outs[2 * n:]


def _adamw_math(gv, w_ref, m_ref, v_ref, d_ref, nm_ref, nv_ref):
    nm = B1 * m_ref[...] + (1.0 - B1) * gv
    nv = B2 * v_ref[...] + (1.0 - B2) * (gv * gv)
    m_hat = nm / (1.0 - B1 ** STEP)
    v_hat = nv / (1.0 - B2 ** STEP)
    d_ref[...] = -LR * (m_hat / (jnp.sqrt(v_hat) + ADAM_EPS) + WD * w_ref[...])
    nm_ref[...] = nm
    nv_ref[...] = nv


def _adamw_halves(w, mine, other, m, v, c, name, transposed):
    hr, cols = mine.shape

    def body(c_ref, w_ref, mine_ref, other_ref, m_ref, v_ref, g_ref, d_ref, nm_ref, nv_ref, *picked):
        gv = jnp.where(pl.program_id(0) == c_ref[0], mine_ref[...], other_ref[...])
        if transposed:
            picked[0][...] = gv
            _store_transposed(picked[0], g_ref)
            gv = g_ref[...]
        else:
            g_ref[...] = gv
        _adamw_math(gv, w_ref, m_ref, v_ref, d_ref, nm_ref, nv_ref)

    if transposed:
        half = pl.BlockSpec((cols, hr), lambda i, c_ref: (0, i))
    else:
        half = pl.BlockSpec((hr, cols), lambda i, c_ref: (i, 0))
    whole = pl.BlockSpec((hr, cols), lambda i, c_ref: (0, 0))
    shp = jax.ShapeDtypeStruct(w.shape, F32)
    return pl.pallas_call(
        body, name=name, out_shape=[shp] * 4,
        grid_spec=pltpu.PrefetchScalarGridSpec(num_scalar_prefetch=1, grid=(2,), in_specs=[half, whole, whole, half, half],
                                               out_specs=[half] * 4,
                                               scratch_shapes=[pltpu.VMEM((hr, cols), F32)] if transposed else []),
        compiler_params=_params(dimension_semantics=("arbitrary",)),
    )(c.reshape(1), w, mine, other, m, v)


_ANY = pl.BlockSpec(memory_space=pl.ANY)


def _mesh_pos():
    return lax.axis_index("x"), lax.axis_index("y"), lax.axis_index("c")


def _other_chips(x, y):
    return [(1 - x, y), (x, 1 - y), (1 - x, 1 - y)]


def _remote(src, dst, send_sems, recv_sems, k, to):
    return pltpu.make_async_remote_copy(src_ref=src, dst_ref=dst, send_sem=send_sems.at[k], recv_sem=recv_sems.at[k],
                                        device_id=to, device_id_type=MESH)


def _gather_weights(shards, n_transposed):
    n = len(shards)
    shapes = [s.shape[::-1] if i < n_transposed else s.shape for i, s in enumerate(shards)]

    def body(*refs):
        start, forward, drain = _gather_steps(shapes, refs[:n], refs[n:2 * n], refs[2 * n:3 * n], *refs[3 * n:])
        start()
        forward()
        drain()

    vmem = pl.BlockSpec(memory_space=pltpu.VMEM)
    return pl.pallas_call(
        body, name="gather_weights", in_specs=[vmem] * n, out_specs=[_ANY] * n,
        out_shape=_gathered_shapes(shapes), scratch_shapes=_gather_scratch(shapes), compiler_params=_params(),
    )(*shards)


def _travel_shape(shape):
    rows, cols = shape
    return (rows, HEAD_PAD if cols == QK_DIM else cols)


def _gathered_shapes(shapes):
    return [jax.ShapeDtypeStruct((N_CHIPS,) + _travel_shape(s), BF16) for s in shapes]


def _gather_scratch(shapes):
    n = len(shapes)
    return ([pltpu.VMEM(_travel_shape(s), BF16) for s in shapes]
            + [pltpu.SemaphoreType.DMA((6 * n,)), pltpu.SemaphoreType.DMA((6 * n,)), pltpu.SemaphoreType.DMA((n,))])


def _gather_steps(shapes, ins, outs, stage, send_sems, recv_sems, local_sems):
    n = len(shapes)
    halved = [s[0] % 32 == 0 for s in shapes]

    def part(i, ref, hc):
        if not halved[i]:
            return ref
        hr = shapes[i][0] // 2
        return ref.at[pl.ds(hc * hr, hr), :]

    def to_chip(i, j, x, y, c):
        cx, cy = _other_chips(x, y)[j]
        return _remote(part(i, stage[i], c), part(i, outs[i].at[2 * x + y], c), send_sems, recv_sems, 6 * i + j, (cx, cy, c))

    def to_sibling(i, j, x, y, c):
        cx, cy = _other_chips(x, y)[j]
        got = part(i, outs[i].at[2 * cx + cy], c)
        return _remote(got, got, send_sems, recv_sems, 6 * i + 3 + j, (x, y, 1 - c))

    def local(i, x, y):
        return pltpu.make_async_copy(stage[i], outs[i].at[2 * x + y], local_sems.at[i])

    def start():
        x, y, c = _mesh_pos()
        for i in range(n):
            cols = shapes[i][1]
            if stage[i].shape[1] != cols:
                stage[i][...] = jnp.zeros_like(stage[i])
            if ins[i].shape == shapes[i]:
                stage[i][:, 0:cols] = ins[i][...].astype(BF16)
            else:
                _store_transposed(ins[i], stage[i])
            local(i, x, y).start()
            for j in range(3):
                to_chip(i, j, x, y, c).start()

    def forward():
        x, y, c = _mesh_pos()
        for i in range(n):
            for j, (cx, cy) in enumerate(_other_chips(x, y)):
                got = part(i, outs[i].at[2 * cx + cy], c)
                _remote(got, got, send_sems, recv_sems, 6 * i + j, (cx, cy, c)).wait_recv()
                if halved[i]:
                    to_sibling(i, j, x, y, c).start()

    def drain():
        x, y, c = _mesh_pos()
        for i in range(n):
            for j, (cx, cy) in enumerate(_other_chips(x, y)):
                if halved[i]:
                    got = part(i, outs[i].at[2 * cx + cy], 1 - c)
                    _remote(got, got, send_sems, recv_sems, 6 * i + 3 + j, (x, y, 1 - c)).wait_recv()
                    to_sibling(i, j, x, y, c).wait_send()
                to_chip(i, j, x, y, c).wait_send()
            local(i, x, y).wait()

    return start, forward, drain


def _swap_halves(grads, whole, name):
    n, m = len(grads), len(grads) + len(whole)

    def body(*refs):
        start, drain = _swap_steps(n, refs[:m], refs[m:2 * m], refs[2 * m], refs[2 * m + 1])
        start()
        drain()

    outs = pl.pallas_call(
        body, name=name, in_specs=[_ANY] * m, out_specs=[_ANY] * m, out_shape=_swapped_shapes(grads, whole),
        scratch_shapes=[pltpu.SemaphoreType.DMA((m,)), pltpu.SemaphoreType.DMA((m,))],
    )(*grads, *whole)
    return outs[:n], outs[n:]


def _swapped_shapes(grads, whole):
    return ([jax.ShapeDtypeStruct((g.shape[0], g.shape[1] // 2, g.shape[2]), F32) for g in grads]
            + [jax.ShapeDtypeStruct(w.shape, F32) for w in whole])


def _swap_steps(n, ins, outs, send_sems, recv_sems):
    def copies():
        x, y, c = _mesh_pos()
        cps = []
        for i, src in enumerate(ins):
            if i < n:
                hr = src.shape[1] // 2
                src = src.at[:, pl.ds((1 - c) * hr, hr), :]
            cps.append(_remote(src, outs[i], send_sems, recv_sems, i, (x, y, 1 - c)))
        return cps

    def start():
        for cp in copies():
            cp.start()

    def drain():
        for cp in copies():
            cp.wait()

    return start, drain


def _scattered_shapes(parts):
    return [jax.ShapeDtypeStruct(p.shape if p.ndim == 3 else (N_CHIPS,) + p.shape, p.dtype) for p in parts]


def _scatter_steps(ins, outs, send_sems, recv_sems, local_sems):
    n = len(ins)

    def src(i, k):
        return ins[i].at[k] if len(ins[i].shape) == 3 else ins[i]

    def sends(x, y, c):
        return [_remote(src(i, 2 * cx + cy), outs[i].at[2 * x + y], send_sems, recv_sems, 3 * i + j, (cx, cy, c))
                for i in range(n) for j, (cx, cy) in enumerate(_other_chips(x, y))]

    def local(i, x, y):
        return pltpu.make_async_copy(src(i, 2 * x + y), outs[i].at[2 * x + y], local_sems.at[i])

    def start():
        x, y, c = _mesh_pos()
        for i in range(n):
            local(i, x, y).start()
        for cp in sends(x, y, c):
            cp.start()

    def drain():
        x, y, c = _mesh_pos()
        for i in range(n):
            for j, (cx, cy) in enumerate(_other_chips(x, y)):
                got = outs[i].at[2 * cx + cy]
                _remote(got, got, send_sems, recv_sems, 3 * i + j, (cx, cy, c)).wait_recv()
        for cp in sends(x, y, c):
            cp.wait_send()
        for i in range(n):
            local(i, x, y).wait()

    return start, drain


def _add_pair(grads, from_sibling, small, small_sibling, c):
    n = len(grads)

    def body(c_ref, *refs):
        ins, outs = refs[:2 * n + 2], refs[2 * n + 2:]
        for i in range(n + 1):
            outs[i][...] = (ins[2 * i][...] + ins[2 * i + 1][...]).astype(outs[i].dtype)

    in_specs, out_specs, out_shape, args = [], [], [], []
    for g, r in zip(grads, from_sibling):
        _, hr, cols = r.shape
        in_specs += [pl.BlockSpec((2, hr, cols), lambda k, c_ref: (k, c_ref[0], 0)),
                     pl.BlockSpec((2, hr, cols), lambda k, c_ref: (k, 0, 0))]
        out_specs.append(pl.BlockSpec((2, hr, cols), lambda k, c_ref: (k, 0, 0)))
        out_shape.append(jax.ShapeDtypeStruct(r.shape, BF16))
        args += [g, r]
    whole = pl.BlockSpec(small.shape, lambda k, c_ref: (0, 0))
    in_specs += [whole, whole]
    out_specs.append(whole)
    out_shape.append(jax.ShapeDtypeStruct(small.shape, F32))
    outs = pl.pallas_call(
        body, name="add_pair", out_shape=out_shape,
        grid_spec=pltpu.PrefetchScalarGridSpec(num_scalar_prefetch=1, grid=(N_CHIPS // 2,), in_specs=in_specs,
                                               out_specs=out_specs),
        compiler_params=_params(dimension_semantics=("arbitrary",)),
    )(c.reshape(1), *args, small, small_sibling)
    return outs[:n], outs[n]


def _scatter_w_in(dw_in_e, from_sibling):
    hr = from_sibling.shape[1]
    shard = (N_CHIPS, hr, SHARD_COLS_IN)

    def body(g_in, r_in, out, g_buf, r_buf, p_buf, load_sems, send_sems, recv_sems, local_sems):
        c = lax.axis_index("c")
        loads = (pltpu.make_async_copy(g_in.at[0, pl.ds(c * hr, hr), :], g_buf, load_sems.at[0]),
                 pltpu.make_async_copy(r_in.at[0], r_buf, load_sems.at[1]))
        for cp in loads:
            cp.start()
        for cp in loads:
            cp.wait()
        g_buf[...] += r_buf[...]
        p_buf[0, :, 0:KPE_END] = g_buf[:, 0:KPE_END].astype(BF16)
        p_buf[0, :, KPE_END:SHARD_COLS_IN] = g_buf[:, KPE_END + KPE_PAD:SHARD_COLS_IN + KPE_PAD].astype(BF16)
        for k in range(1, N_CHIPS):
            p_buf[k] = g_buf[:, SHARD_COLS_IN * k + KPE_PAD:SHARD_COLS_IN * (k + 1) + KPE_PAD].astype(BF16)
        start, drain = _scatter_steps([p_buf], [out], send_sems, recv_sems, local_sems)
        start()
        drain()

    return pl.pallas_call(
        body, name="scatter_grads", in_specs=[_ANY] * 2, out_specs=_ANY, out_shape=jax.ShapeDtypeStruct(shard, BF16),
        scratch_shapes=[pltpu.VMEM((hr, PROJ_EXT), F32)] * 2 + [pltpu.VMEM(shard, BF16)]
                       + [pltpu.SemaphoreType.DMA((2,)), pltpu.SemaphoreType.DMA((3,)), pltpu.SemaphoreType.DMA((3,)),
                          pltpu.SemaphoreType.DMA((1,))],
        compiler_params=_params(),
    )(dw_in_e, from_sibling)


def _share_halves(halves):
    n = len(halves)

    def body(*refs):
        ins, outs, send_sems, recv_sems = refs[:n], refs[n:2 * n], refs[2 * n], refs[2 * n + 1]
        x, y, c = _mesh_pos()
        cps = [_remote(ins[i], outs[i], send_sems, recv_sems, i, (x, y, 1 - c)) for i in range(n)]
        for cp in cps:
            cp.start()
        for cp in cps:
            cp.wait()

    return pl.pallas_call(
        body, name="share_halves", in_specs=[_ANY] * n, out_specs=[_ANY] * n,
        out_shape=[jax.ShapeDtypeStruct(h.shape, h.dtype) for h in halves],
        scratch_shapes=[pltpu.SemaphoreType.DMA((n,)), pltpu.SemaphoreType.DMA((n,))],
    )(*halves)


SHARD_COLS_IN = IN_TOTAL // N_CHIPS
KPE_END = Q_LORA + KV_LORA + ROPE
KPE_PAD = PROJ_EXT - IN_TOTAL


def _by_cols(a):
    return a.transpose(1, 0, 2).reshape(a.shape[1], N_CHIPS * a.shape[2])


def _assemble_early(c_in, c_uq, c_ukv, c_conv):
    return c_in, c_uq, c_ukv, _by_cols(c_conv).astype(F32)


def _assemble_late(c_o, c_pl, c_plg):
    return c_o.reshape(D_MODEL, D_MODEL), _by_cols(c_pl), c_plg.reshape(D_MODEL, D_MODEL)


def _split_late(dw_o, dw_pl, dw_plg):
    chip_major = lambda a: a.reshape(a.shape[0], N_CHIPS, a.shape[1] // N_CHIPS).transpose(1, 0, 2)
    return [dw_o.reshape(N_CHIPS, D_MODEL // N_CHIPS, D_MODEL), chip_major(dw_pl),
            dw_plg.reshape(N_CHIPS, D_MODEL // N_CHIPS, D_MODEL)]


def _local_step(x, p, pos, tgt, gains, early, late_shards, late_gathered, tm, tq):
    c_in, w_uq_e, w_ukv, conv_w = early
    g_in, g_cq, g_ckv, g_q, g_k, g_oa, g_oc, g_pl = gains
    T = x.shape[0]
    zpad = lambda a, n: jnp.concatenate([a, jnp.zeros(a.shape[:-1] + (n,), a.dtype)], axis=-1)
    gq, gk = zpad(g_q, HEAD_PAD - QK_DIM), zpad(g_k, HEAD_PAD - QK_DIM)
    inv_freq = 1.0 / (ROPE_THETA ** (jnp.arange(0, ROPE, 2, dtype=F32) / ROPE))
    invf = jnp.concatenate([inv_freq, inv_freq, jnp.zeros((64,), F32)]).reshape(1, LANES)
    sgn = jnp.concatenate([-jnp.ones((32,), F32), jnp.ones((32,), F32), jnp.zeros((64,), F32)]).reshape(1, LANES)

    (proj, q, k, v, w_in_e), gathered = _fwd_proj(x, pos, g_in, c_in, g_cq, w_uq_e, g_ckv, w_ukv, gq, gk, invf, sgn,
                                                  late_shards, min(2 * tm, T))
    w_o, w_pl, w_plg = _assemble_late(*(gathered if late_shards else late_gathered))
    o, lse = _attn_fwd(q, k, v, tq)
    (dx1, do, delta, dtail, du, dw_o, dw_pl, dw_plg, dg_oa, dg_oc, dg_pl, dconv, loss) = _tail(
        x, o, proj, p, tgt, g_oa, g_oc, g_pl, conv_w, w_o, w_pl, w_plg, tm)
    late_grads = _split_late(dw_o, dw_pl, dw_plg)
    (dq, dk, dv), late_sibling = _attn_bwd(q, k, v, do, lse, delta, tq, late_grads)
    (gx, h, dproj, dw_uq_e, dw_ukv, dg_in, dg_cq, dg_ckv, dgq, dgk) = _bwd_proj(
        x, dx1, pos, proj, dq, dk, dv, dtail, du, g_in, w_in_e, g_cq, w_uq_e, g_ckv, w_ukv, gq, gk, conv_w, invf, sgn, tm)
    wgrads = [dw_uq_e[:, :, :QK_DIM], dw_ukv, *late_grads]
    ggrads = (dg_in, dg_cq, dg_ckv, dgq, dgk, dg_oa, dg_oc, dg_pl)
    return loss, gx, (h, dproj), wgrads, late_sibling, ggrads, dconv


def kernel(x, p, positions, g_in, w_in, g_cq, w_uq, g_ckv, w_ukv, g_q, g_k, conv_w, g_oa, g_oc, w_o, w_pl, w_plg, g_pl, loss_target, m_g_in, m_w_in, m_g_cq, m_w_uq, m_g_ckv, m_w_ukv, m_g_q, m_g_k, m_conv_w, m_g_oa, m_g_oc, m_w_o, m_w_pl, m_w_plg, m_g_pl, v_g_in, v_w_in, v_g_cq, v_w_uq, v_g_ckv, v_w_ukv, v_g_q, v_g_k, v_conv_w, v_g_oa, v_g_oc, v_w_o, v_w_pl, v_w_plg, v_g_pl):
    T = x.shape[1]
    c = lax.axis_index("c")
    chip = 2 * lax.axis_index("x") + lax.axis_index("y")
    gains = [g.reshape(1, -1) for g in (g_in, g_cq, g_ckv, g_q, g_k, g_oa, g_oc, g_pl)]

    transposed = ("w_in", "w_uq")
    early = _assemble_early(*_gather_weights([w_in[0].T, w_uq[0].T, w_ukv[0], conv_w[0]], len(transposed)))

    loss, gx, (h_t, dproj), others_cm, late_sibling, ggrads, dconv = _local_step(
        x[0], p[0, 0], positions.reshape(1, T), loss_target[0], gains, early, [w_o[0], w_pl[0], w_plg[0]], None, 256, 512)

    small_parts = [a.reshape(-1, LANES) for a in (*ggrads, loss, dconv)]
    small_rows = [a.shape[0] for a in small_parts]
    tile_rows = [-(-r // 8) * 8 for r in small_rows]
    tile_rows[-1] += -sum(tile_rows) % 16
    small = jnp.concatenate([jnp.pad(a, ((0, t - r), (0, 0))) for a, r, t in zip(small_parts, small_rows, tile_rows)])
    n_early = len(others_cm) - len(late_sibling)
    early_sibling, (small_sibling,) = _swap_halves(others_cm[:n_early], [small], "pair_grads")
    chip_parts, chip_small = _add_pair(others_cm, [*early_sibling, *late_sibling], small, small_sibling, c)
    dw_in_e, w_in_sibling, exchanged = _matmul_acc(h_t, dproj, min(4096, T), 512, [*chip_parts, chip_small])
    by_chip = [_scatter_w_in(dw_in_e[None], w_in_sibling[None]), *exchanged[:-1]]
    halves, small_total = _add_chips(by_chip, exchanged[-1])
    other_halves = _share_halves(halves)

    gg, off = [], 0
    for rows, tiled in zip(small_rows, tile_rows):
        gg.append(small_total[off:off + rows].reshape(1, -1))
        off += tiled
    loss_out = gg[8][0, 0]
    conv_total = gg[9].reshape(3, CONV_W)
    conv_g = lax.dynamic_slice(conv_total, (0, chip * (CONV_W // N_CHIPS)), (3, CONV_W // N_CHIPS))
    g_by_name = dict(g_in=gg[0], g_cq=gg[1], g_ckv=gg[2], g_q=gg[3][:, :QK_DIM], g_k=gg[4][:, :QK_DIM], conv_w=conv_g,
                     g_oa=gg[5], g_oc=gg[6], g_pl=gg[7])
    half_by_name = dict(zip(("w_in", "w_uq", "w_ukv", "w_o", "w_pl", "w_plg"), zip(halves, other_halves)))
    weights = dict(g_in=g_in, w_in=w_in, g_cq=g_cq, w_uq=w_uq, g_ckv=g_ckv, w_ukv=w_ukv, g_q=g_q, g_k=g_k,
                   conv_w=conv_w, g_oa=g_oa, g_oc=g_oc, w_o=w_o, w_pl=w_pl, w_plg=w_plg, g_pl=g_pl)
    ms = dict(g_in=m_g_in, w_in=m_w_in, g_cq=m_g_cq, w_uq=m_w_uq, g_ckv=m_g_ckv, w_ukv=m_w_ukv, g_q=m_g_q, g_k=m_g_k,
              conv_w=m_conv_w, g_oa=m_g_oa, g_oc=m_g_oc, w_o=m_w_o, w_pl=m_w_pl, w_plg=m_w_plg, g_pl=m_g_pl)
    vs = dict(g_in=v_g_in, w_in=v_w_in, g_cq=v_g_cq, w_uq=v_w_uq, g_ckv=v_g_ckv, w_ukv=v_w_ukv, g_q=v_g_q, g_k=v_g_k,
              conv_w=v_conv_w, g_oa=v_g_oa, g_oc=v_g_oc, w_o=v_w_o, w_pl=v_w_pl, w_plg=v_w_plg, g_pl=v_g_pl)
    names = list(weights)
    flat = lambda a: a.reshape(-1, a.shape[-1])
    small_names = list(g_by_name)
    one_row = lambda a: a.reshape(1, -1)
    small_out = _adamw_small([one_row(weights[n]) for n in small_names], [one_row(g_by_name[n]) for n in small_names],
                             [one_row(ms[n]) for n in small_names], [one_row(vs[n]) for n in small_names])
    results = {n: (g_by_name[n], *(out[i] for out in small_out)) for i, n in enumerate(small_names)}
    for n in half_by_name:
        shard = (lambda a: a[0].T) if n in transposed else flat
        out = _adamw_halves(shard(weights[n]), *half_by_name[n], shard(ms[n]), shard(vs[n]), c, "adamw_" + n,
                            n in transposed)
        results[n] = [a.T for a in out] if n in transposed else out
    per_kind = [[results[n][kind].reshape(weights[n].shape) for n in names] for kind in range(4)]
    return (loss_out, gx.reshape(x.shape), *per_kind[0], *per_kind[1], *per_kind[2], *per_kind[3])
```

```python
import math

import jax
import jax.numpy as jnp
from jax import lax
from jax.experimental import pallas as pl
from jax.experimental.pallas import tpu as pltpu

F32 = jnp.float32
BF16 = jnp.bfloat16

D_MODEL = 1024
N_HEADS = 4
NOPE = 128
ROPE = 64
V_DIM = 128
QK_DIM = NOPE + ROPE
HEAD_PAD = 256
Q_LORA = 256
KV_LORA = 128
ATTN_W = 512
CONV_W = 512
PLE = 256
IN_TOTAL = 3008
PROJ_EXT = 3072
ROPE_THETA = 10000.0
EPS = 1e-6
SCALE = 1.0 / math.sqrt(QK_DIM)
LOG2E = math.log2(math.e)
EXP2_SCALE = SCALE * LOG2E
NEG = -1e30
SOFTMAX_ROWS = 32
SUB_TILE = 256

LR, B1, B2, ADAM_EPS, WD, STEP = 0.001, 0.9, 0.999, 1e-08, 0.01, 10

N_CHIPS = 4
LANES = 128
VMEM_LIMIT = 56 * 1024 * 1024
MESH = pl.DeviceIdType.MESH


def _params(**kw):
    return pltpu.CompilerParams(vmem_limit_bytes=VMEM_LIMIT, **kw)


def _inv_rms(x, n):
    return lax.rsqrt(jnp.sum(x * x, axis=-1, keepdims=True) / n + EPS)


def _lane_sum(a):
    folded = a[:, 0:LANES]
    for c0 in range(LANES, a.shape[1], LANES):
        folded = folded + a[:, c0:c0 + LANES]
    head = folded.astype(BF16)
    tail = (folded - head.astype(F32)).astype(BF16)
    return _dot(jnp.concatenate([head, tail], axis=1), jnp.ones((2 * LANES, LANES), BF16))


def _inv_rms_mxu(x):
    return lax.rsqrt(_lane_sum(x * x) / x.shape[1] + EPS)


def _rep(r, width):
    return r if width == LANES else jnp.tile(r, (1, width // LANES))


def _sigmoid(z):
    return jax.nn.sigmoid(z)


def _swap_rope_halves(b):
    lane = lax.broadcasted_iota(jnp.int32, b.shape, 1)
    swapped = jnp.where(lane < 32, pltpu.roll(b, 96, 1), pltpu.roll(b, 32, 1))
    return jnp.where(lane < ROPE, swapped, 0.0)


def _dot(a, b):
    return jnp.dot(a, b, preferred_element_type=F32)


def _dot_nt(a, b):
    return lax.dot_general(a, b, (((1,), (1,)), ((), ())), preferred_element_type=F32)


def _dot_tn(a, b):
    return lax.dot_general(a, b, (((0,), (0,)), ((), ())), preferred_element_type=F32)


def _colsum(a):
    return jnp.sum(a, axis=0, keepdims=True)


def _store_transposed(src_ref, dst_ref):
    r, c = src_ref.shape
    for r0 in range(0, r, LANES):
        h = min(LANES, r - r0)
        for c0 in range(0, c, LANES):
            w = min(LANES, c - c0)
            piece = src_ref[r0:r0 + h, c0 + w - LANES:c0 + w]
            if h < LANES:
                piece = jnp.concatenate([piece, jnp.zeros((LANES - h, LANES), piece.dtype)], axis=0)
            dst_ref[c0:c0 + w, r0:r0 + h] = piece.T[LANES - w:, 0:h].astype(dst_ref.dtype)


def _full(shape):
    return pl.BlockSpec(shape, lambda *_: (0,) * len(shape))


def _round_robin(chains, width):
    waiting, active = list(chains), []
    while waiting or active:
        while waiting and len(active) < width:
            active.append(waiting.pop(0))
        for chain in list(active):
            if next(chain, _DONE) is _DONE:
                active.remove(chain)


_DONE = object()


def _rope_tables(pos_ref, invf_ref, sgn_ref):
    pos = jnp.broadcast_to(pos_ref[...].astype(F32), (LANES, pos_ref.shape[1])).T
    ang = pos * invf_ref[...]
    return jnp.cos(ang), jnp.sin(ang) * sgn_ref[...]


def _fwd_proj(x, pos, g_in, c_in, g_cq, w_uq, g_ckv, w_ukv, gq, gk, invf, sgn, late_shards, tm):
    T = x.shape[0]
    nt = T // tm
    n_late = len(late_shards)
    ts = min(SUB_TILE, tm)

    def body(x_ref, pos_ref, g_in_ref, c_in_ref, g_cq_ref, w_uq_ref, g_ckv_ref, w_ukv_ref, gq_ref, gk_ref,
             invf_ref, sgn_ref, *rest):
        late_in, (proj_ref, q_ref, k_ref, v_ref, w_in_ref) = rest[:n_late], rest[n_late:n_late + 5]
        late_out, late_scratch = rest[n_late + 5:2 * n_late + 5], rest[2 * n_late + 5:]
        i = pl.program_id(0)

        @pl.when(i == 0)
        def _():
            w_in_ref[:, 0:KPE_END] = c_in_ref[0, :, 0:KPE_END]
            w_in_ref[:, KPE_END:KPE_END + KPE_PAD] = jnp.zeros((D_MODEL, KPE_PAD), BF16)
            w_in_ref[:, KPE_END + KPE_PAD:SHARD_COLS_IN + KPE_PAD] = c_in_ref[0, :, KPE_END:SHARD_COLS_IN]
            for chip in range(1, N_CHIPS):
                w_in_ref[:, SHARD_COLS_IN * chip + KPE_PAD:SHARD_COLS_IN * (chip + 1) + KPE_PAD] = c_in_ref[chip]

        if n_late:
            start, forward, drain = _gather_steps([s.shape for s in late_shards], late_in, late_out,
                                                  late_scratch[:n_late], *late_scratch[n_late:])
            pl.when(i == 0)(start)
            pl.when(i == nt // 2)(forward)

        for r0 in range(0, tm, ts):
            rows = slice(r0, r0 + ts)
            xv = x_ref[rows, :]
            h = (xv * _rep(_inv_rms_mxu(xv), D_MODEL) * g_in_ref[...]).astype(BF16)
            lat = _dot(h, w_in_ref[:, 0:512])
            proj_ref[rows, 0:512] = lat
            c_q = lat[:, 0:Q_LORA]
            cqn = (c_q * _rep(_inv_rms_mxu(c_q), Q_LORA) * g_cq_ref[...]).astype(BF16)
            c_kv = lat[:, Q_LORA:Q_LORA + KV_LORA]
            ckvn = (c_kv * _inv_rms_mxu(c_kv) * g_ckv_ref[...]).astype(BF16)
            kpe = lat[:, 384:512]
            kpe_sq = kpe * kpe
            cos_b, sin_b = _rope_tables(pos_ref.at[:, rows], invf_ref, sgn_ref)
            gq_a, gq_b = gq_ref[:, 0:NOPE], gq_ref[:, NOPE:HEAD_PAD]
            gk_a, gk_b = gk_ref[:, 0:NOPE], gk_ref[:, NOPE:HEAD_PAD]

            def projections(rows=rows, h=h):
                for c0 in range(512, PROJ_EXT, 512):
                    proj_ref[rows, c0:c0 + 512] = _dot(h, w_in_ref[:, c0:c0 + 512])
                    yield

            def queries(hd, rows=rows, cqn=cqn, cos_b=cos_b, sin_b=sin_b, gq_a=gq_a, gq_b=gq_b):
                qh = _dot(cqn, w_uq_ref[hd])
                yield
                a, b = qh[:, 0:NOPE], qh[:, NOPE:HEAD_PAD]
                r = lax.rsqrt(_lane_sum(a * a + b * b) / QK_DIM + EPS)
                yield
                bn = b * r * gq_b
                q_ref[hd, rows, 0:NOPE] = (a * r * gq_a).astype(BF16)
                q_ref[hd, rows, NOPE:HEAD_PAD] = (bn * cos_b + _swap_rope_halves(bn) * sin_b).astype(BF16)
                yield

            def keys(hd, rows=rows, ckvn=ckvn, kpe=kpe, kpe_sq=kpe_sq, cos_b=cos_b, sin_b=sin_b, gk_a=gk_a, gk_b=gk_b):
                kvh = _dot(ckvn, w_ukv_ref[hd])
                yield
                ka = kvh[:, 0:NOPE]
                rk = lax.rsqrt(_lane_sum(ka * ka + kpe_sq) / QK_DIM + EPS)
                yield
                kbn = kpe * rk * gk_b
                k_ref[hd, rows, 0:NOPE] = (ka * rk * gk_a).astype(BF16)
                k_ref[hd, rows, NOPE:HEAD_PAD] = (kbn * cos_b + _swap_rope_halves(kbn) * sin_b).astype(BF16)
                v_ref[hd, rows, 0:V_DIM] = kvh[:, NOPE:HEAD_PAD].astype(BF16)
                v_ref[hd, rows, V_DIM:2 * V_DIM] = jnp.ones((ts, V_DIM), BF16)
                yield

            chains = [projections()]
            for hd in range(N_HEADS):
                chains += [queries(hd), keys(hd)]
            _round_robin(chains, 4)

        if n_late:
            pl.when(i == nt - 1)(drain)

    row = lambda i: (i, 0)
    head_rows = lambda i: (0, i, 0)
    outs = pl.pallas_call(
        body, name="fwd_proj", grid=(nt,),
        in_specs=[pl.BlockSpec((tm, D_MODEL), row), pl.BlockSpec((1, tm), lambda i: (0, i)), _full((1, D_MODEL)),
                  _full((N_CHIPS, D_MODEL, SHARD_COLS_IN)), _full((1, Q_LORA)), _full((N_HEADS, Q_LORA, HEAD_PAD)),
                  _full((1, KV_LORA)), _full((N_HEADS, KV_LORA, HEAD_PAD)), _full((1, HEAD_PAD)), _full((1, HEAD_PAD)),
                  _full((1, LANES)), _full((1, LANES))] + [_full(s.shape) for s in late_shards],
        out_specs=[pl.BlockSpec((tm, PROJ_EXT), row), pl.BlockSpec((N_HEADS, tm, HEAD_PAD), head_rows),
                   pl.BlockSpec((N_HEADS, tm, HEAD_PAD), head_rows), pl.BlockSpec((N_HEADS, tm, 2 * V_DIM), head_rows),
                   _full((D_MODEL, PROJ_EXT))] + [_ANY] * n_late,
        out_shape=[jax.ShapeDtypeStruct((T, PROJ_EXT), F32), jax.ShapeDtypeStruct((N_HEADS, T, HEAD_PAD), BF16),
                   jax.ShapeDtypeStruct((N_HEADS, T, HEAD_PAD), BF16), jax.ShapeDtypeStruct((N_HEADS, T, 2 * V_DIM), BF16),
                   jax.ShapeDtypeStruct((D_MODEL, PROJ_EXT), BF16)] + _gathered_shapes([s.shape for s in late_shards]),
        scratch_shapes=_gather_scratch([s.shape for s in late_shards]) if n_late else [],
        compiler_params=_params(dimension_semantics=("arbitrary",)),
    )(x, pos, g_in, c_in, g_cq, w_uq, g_ckv, w_ukv, gq, gk, invf, sgn, *late_shards)
    return outs[:5], outs[5:]


def _chunk_pipeline(n_loop, lag, matmuls, pointwise, accumulate, last):
    slots = lag + 1

    def iteration(t, slot, pending=True):
        matmuls(jnp.minimum(t + lag, n_loop), (slot + lag) % slots)
        if pending:
            accumulate(t - lag, (slot + 1) % slots, False)
        pointwise(t, slot, False)

    def finish(slot, pending):
        for back in range(pending, 0, -1):
            accumulate(n_loop - back, (slot - back) % slots, False)
        pointwise(n_loop, slot, True)
        accumulate(n_loop, slot, True)
        last()

    for u in range(lag):
        matmuls(jnp.minimum(u, n_loop), u)
    for u in range(lag):
        pl.when(u < n_loop)(lambda u=u: iteration(u, u, pending=False))

    n_main = jnp.maximum(n_loop - lag, 0)

    def unrolled(tt, carry):
        for j in range(slots):
            iteration(lag + slots * tt + j, (lag + j) % slots)
        return carry

    lax.fori_loop(0, n_main // slots, unrolled, 0)
    rest = lax.rem(n_main, slots)
    t0 = n_loop - rest

    for r in range(slots):
        @pl.when(jnp.logical_and(n_loop >= lag, rest == r))
        def _():
            for j in range(r):
                iteration(t0 + j, (lag + j) % slots)
            finish((lag + r) % slots, lag)

    for short in range(lag):
        pl.when(n_loop == short)(lambda short=short: finish(short, short))


def _attn_fwd(q, k, v, tq):
    T = q.shape[1]
    tk = tq
    rc = min(SOFTMAX_ROWS, tq)

    def body(q_ref, k_ref, v_ref, o_ref, lse_ref, s0, s1, s2, p0, p1, p2, a0, a1, a2, m_ref, acc_ref):
        qi = pl.program_id(1)
        s_buf, p_buf, a_buf = (s0, s1, s2), (p0, p1, p2), (a0, a1, a2)

        def scores(t, slot):
            ks = pl.multiple_of(t * tk, tk)
            s_buf[slot][...] = _dot_nt(q_ref[0], k_ref[0, pl.ds(ks, tk), :])

        def blocks(masked):
            return ((0, tq // 2, tk // 2), (tq // 2, tq // 2, tk)) if masked else ((0, tq, tk),)

        def values(t, slot, masked):
            ks = pl.multiple_of(t * tk, tk)
            for q0, nq, nk in blocks(masked):
                rows = slice(q0, q0 + nq)
                acc_ref[rows, :] = (acc_ref[rows, :] * a_buf[slot][rows, :]
                                    + _dot(p_buf[slot][rows, 0:nk], v_ref[0, pl.ds(ks, nk), :]))

        def softmax(t, slot, masked):
            for q0, nq, nk in blocks(masked):
                rows = slice(q0, q0 + nq)
                s_all = s_buf[slot][rows, 0:nk]
                if masked:
                    row = lax.broadcasted_iota(jnp.int32, (nq, nk), 0) + q0
                    col = lax.broadcasted_iota(jnp.int32, (nq, nk), 1)
                    s_all = jnp.where(col <= row, s_all, NEG)
                    s_buf[slot][rows, 0:nk] = s_all
                m_old = m_ref[rows, :]
                m_new = jnp.maximum(m_old, jnp.max(s_all, axis=1, keepdims=True))
                a_buf[slot][rows, :] = jnp.exp2((m_old - m_new) * EXP2_SCALE)
                m_ref[rows, :] = m_new
                for r0 in range(0, nq, rc):
                    s = s_buf[slot][q0 + r0:q0 + r0 + rc, 0:nk]
                    p_buf[slot][q0 + r0:q0 + r0 + rc, 0:nk] = jnp.exp2((s - m_new[r0:r0 + rc, :]) * EXP2_SCALE).astype(BF16)

        def last():
            l = acc_ref[:, V_DIM:2 * V_DIM]
            o_ref[...] = acc_ref[:, 0:V_DIM] / l
            lse_ref[0] = (m_ref[...] * SCALE + jnp.log(l)).T[0:1, :]

        m_ref[...] = jnp.full_like(m_ref, NEG)
        acc_ref[...] = jnp.zeros_like(acc_ref)
        _chunk_pipeline(qi, 2, scores, softmax, values, last)

    return pl.pallas_call(
        body, name="attn_fwd", grid=(N_HEADS, T // tq),
        in_specs=[pl.BlockSpec((1, tq, HEAD_PAD), lambda h, i: (h, i, 0)),
                  pl.BlockSpec((1, T, HEAD_PAD), lambda h, i: (h, 0, 0)),
                  pl.BlockSpec((1, T, 2 * V_DIM), lambda h, i: (h, 0, 0))],
        out_specs=[pl.BlockSpec((tq, V_DIM), lambda h, i: (i, h)),
                   pl.BlockSpec((1, 1, tq), lambda h, i: (h, 0, i))],
        out_shape=[jax.ShapeDtypeStruct((T, ATTN_W), F32), jax.ShapeDtypeStruct((N_HEADS, 1, T), F32)],
        scratch_shapes=[pltpu.VMEM((tq, tk), F32)] * 3 + [pltpu.VMEM((tq, tk), BF16)] * 3
                       + [pltpu.VMEM((tq, 1), F32)] * 4 + [pltpu.VMEM((tq, 2 * V_DIM), F32)],
        compiler_params=_params(dimension_semantics=("arbitrary", "arbitrary")),
    )(q, k, v)


def _tail(x, o, proj, p, tgt, g_oa, g_oc, g_pl, conv_w, w_o, w_pl, w_plg, tm):
    T = x.shape[0]
    nt = T // tm

    def body(x_ref, o_ref, za_ref, cb_ref, cc_ref, cx_ref, zc_ref, cch_ref, cxh_ref, p_ref, tgt_ref,
             g_oa_ref, g_oc_ref, g_pl_ref, cw_ref, w_o_ref, w_pl_ref, w_plg_ref,
             dx1_ref, do_ref, delta_ref, dtail_ref, du_ref,
             dw_o_ref, dw_pl_ref, dw_plg_ref, dg_oa_ref, dg_oc_ref, dg_pl_ref, dcw_ref, loss_ref):
        i = pl.program_id(0)

        @pl.when(i == 0)
        def _():
            for r in (dw_o_ref, dw_pl_ref, dw_plg_ref, dg_oa_ref, dg_oc_ref, dg_pl_ref, dcw_ref, loss_ref):
                r[...] = jnp.zeros_like(r)

        g_oa, g_oc, g_pl = g_oa_ref[...], g_oc_ref[...], g_pl_ref[...]
        w0, w1, w2 = cw_ref[0:1, :], cw_ref[1:2, :], cw_ref[2:3, :]

        xv, ov, za, cb, zc = x_ref[...], o_ref[...], za_ref[...], cb_ref[...], zc_ref[...]
        pb = p_ref[...].astype(BF16)
        pp = _dot(pb, w_pl_ref[...])

        sa = _sigmoid(za)
        silu_a = za * sa
        ga = ov * silu_a
        ra = _inv_rms(ga, ATTN_W)
        xa = ga * ra
        ya = (xa * g_oa).astype(BF16)
        x1_a = _dot(ya, w_o_ref[0:ATTN_W, :])
        v = cc_ref[...] * cx_ref[...]
        not_first = jnp.where(i > 0, 1.0, 0.0)
        hv6 = cch_ref[6:7, :] * cxh_ref[6:7, :] * not_first
        hv7 = cch_ref[7:8, :] * cxh_ref[7:8, :] * not_first
        row = lax.broadcasted_iota(jnp.int32, v.shape, 0)
        v1 = jnp.where(row == 0, hv7, pltpu.roll(v, 1, 0))
        v2 = jnp.where(row == 0, hv6, jnp.where(row == 1, hv7, pltpu.roll(v, 2, 0)))
        u = w0 * v2 + w1 * v1 + w2 * v
        sc = _sigmoid(zc)
        silu_c = zc * sc
        gc = cb * u * silu_c
        rc = _inv_rms(gc, CONV_W)
        xc = gc * rc
        yc = (xc * g_oc).astype(BF16)
        x1 = xv + (x1_a + _dot(yc, w_o_ref[ATTN_W:D_MODEL, :]))
        r1 = _inv_rms(x1, D_MODEL)
        xh1 = x1 * r1
        n1 = (xh1 * g_pl).astype(BF16)
        gate = _sigmoid(_dot(n1, w_plg_ref[...]))
        err = x1 + gate * pp - tgt_ref[...]
        loss_ref[...] += 0.5 * jnp.sum(err * err) / D_MODEL
        dy = err / D_MODEL

        dpp = (dy * gate).astype(BF16)
        da = (dy * pp * gate * (1.0 - gate)).astype(BF16)
        dn1 = _dot_nt(da, w_plg_ref[...])
        dw_pl_ref[...] += _dot_tn(pb, dpp)
        dw_plg_ref[...] += _dot_tn(n1, da)
        dg_pl_ref[...] += _colsum(dn1 * xh1)
        dxh = dn1 * g_pl
        dx1 = dy + r1 * (dxh - xh1 * (jnp.sum(dxh * xh1, axis=-1, keepdims=True) / D_MODEL))
        dx1_ref[...] = dx1
        dx1b = dx1.astype(BF16)
        dya = _dot_nt(dx1b, w_o_ref[0:ATTN_W, :])
        dyc = _dot_nt(dx1b, w_o_ref[ATTN_W:D_MODEL, :])

        dw_o_ref[0:ATTN_W, :] += _dot_tn(ya, dx1b)
        dg_oa_ref[...] += _colsum(dya * xa)
        dxa = dya * g_oa
        dga = ra * (dxa - xa * (jnp.sum(dxa * xa, axis=-1, keepdims=True) / ATTN_W))
        do = (dga * silu_a).astype(BF16)
        do_ref[...] = do
        dof = do.astype(F32) * ov
        for hd in range(N_HEADS):
            delta_ref[hd] = _lane_sum(dof[:, hd * V_DIM:(hd + 1) * V_DIM]).T[0:1, :]
        dtail_ref[:, 0:512] = (dga * ov * (sa * (1.0 + za * (1.0 - sa)))).astype(BF16)

        dw_o_ref[ATTN_W:D_MODEL, :] += _dot_tn(yc, dx1b)
        dg_oc_ref[...] += _colsum(dyc * xc)
        dxc = dyc * g_oc
        dgc = rc * (dxc - xc * (jnp.sum(dxc * xc, axis=-1, keepdims=True) / CONV_W))
        dtail_ref[:, 512:1024] = (dgc * u * silu_c).astype(BF16)
        du = dgc * cb * silu_c
        du_ref[...] = du
        dtail_ref[:, 1024:1536] = (dgc * cb * u * (sc * (1.0 + zc * (1.0 - sc)))).astype(BF16)
        dcw_ref[0:1, :] += _colsum(du * v2)
        dcw_ref[1:2, :] += _colsum(du * v1)
        dcw_ref[2:3, :] += _colsum(du * v)

    row = lambda i: (i, 0)
    col = lambda c: (lambda i: (i, c))
    halo = lambda c: (lambda i: (jnp.maximum(i * (tm // 8) - 1, 0), c))
    in_specs = [pl.BlockSpec((tm, D_MODEL), row), pl.BlockSpec((tm, ATTN_W), row)]
    in_specs += [pl.BlockSpec((tm, 512), col(c)) for c in (1, 2, 3, 4, 5)]
    in_specs += [pl.BlockSpec((8, 512), halo(3)), pl.BlockSpec((8, 512), halo(4))]
    in_specs += [pl.BlockSpec((tm, PLE), row), pl.BlockSpec((tm, D_MODEL), row),
                 _full((1, ATTN_W)), _full((1, CONV_W)), _full((1, D_MODEL)), _full((3, CONV_W)),
                 _full((D_MODEL, D_MODEL)), _full((PLE, D_MODEL)), _full((D_MODEL, D_MODEL))]
    out_specs = [pl.BlockSpec((tm, D_MODEL), row), pl.BlockSpec((tm, ATTN_W), row),
                 pl.BlockSpec((N_HEADS, 1, tm), lambda i: (0, 0, i)), pl.BlockSpec((tm, 1536), row),
                 pl.BlockSpec((tm, CONV_W), row),
                 _full((D_MODEL, D_MODEL)), _full((PLE, D_MODEL)), _full((D_MODEL, D_MODEL)),
                 _full((1, ATTN_W)), _full((1, CONV_W)), _full((1, D_MODEL)), _full((3, CONV_W)), _full((1, LANES))]
    out_shape = [jax.ShapeDtypeStruct((T, D_MODEL), F32), jax.ShapeDtypeStruct((T, ATTN_W), BF16),
                 jax.ShapeDtypeStruct((N_HEADS, 1, T), F32), jax.ShapeDtypeStruct((T, 1536), BF16),
                 jax.ShapeDtypeStruct((T, CONV_W), F32),
                 jax.ShapeDtypeStruct((D_MODEL, D_MODEL), F32), jax.ShapeDtypeStruct((PLE, D_MODEL), F32),
                 jax.ShapeDtypeStruct((D_MODEL, D_MODEL), F32),
                 jax.ShapeDtypeStruct((1, ATTN_W), F32), jax.ShapeDtypeStruct((1, CONV_W), F32),
                 jax.ShapeDtypeStruct((1, D_MODEL), F32), jax.ShapeDtypeStruct((3, CONV_W), F32),
                 jax.ShapeDtypeStruct((1, LANES), F32)]
    return pl.pallas_call(
        body, name="tail", grid=(nt,), in_specs=in_specs, out_specs=out_specs, out_shape=out_shape,
        compiler_params=_params(dimension_semantics=("arbitrary",)),
    )(x, o, proj, proj, proj, proj, proj, proj, proj, p, tgt, g_oa, g_oc, g_pl, conv_w, w_o, w_pl, w_plg)


def _attn_bwd(q, k, v, do, lse_row, delta_row, tk, swap):
    T = q.shape[1]
    tq = tk
    nq = T // tq
    rc = min(SOFTMAX_ROWS, tk)
    hk, hq = tk // 2, tq // 2
    n_swap = len(swap)

    def body(q_ref, k_ref, v_ref, do_ref, lse_ref, dl_ref, *rest):
        swap_in, (dq_ref, dk_ref, dv_ref), rest = rest[:n_swap], rest[n_swap:n_swap + 3], rest[n_swap + 3:]
        swap_out, (s0, s1, d0, d1, p0, p1, g0, g1, dk_acc, dv_acc), sems = rest[:n_swap], rest[n_swap:n_swap + 10], rest[n_swap + 10:]
        kj = pl.program_id(1)
        s_buf, dp_buf, p_buf, g_buf = (s0, s1), (d0, d1), (p0, p1), (g0, g1)

        if n_swap:
            start, drain = _swap_steps(n_swap, swap_in, swap_out, *sems)
            pl.when(jnp.logical_and(pl.program_id(0) == 0, kj == 0))(start)

        @pl.when(kj == 0)
        def _():
            dq_ref[...] = jnp.zeros_like(dq_ref)

        def q_start(t):
            return pl.multiple_of((nq - 1 - t) * tq, tq)

        def matmuls(t, slot):
            qs = q_start(t)
            s_buf[slot][...] = _dot_nt(k_ref[0], q_ref[0, pl.ds(qs, tq), :])
            dp_buf[slot][...] = _dot_nt(v_ref[0], do_ref[pl.ds(qs, tq), :])

        def pointwise(t, slot, masked):
            qs = q_start(t)
            lse2 = lse_ref[0, :, pl.ds(qs, tq)] * LOG2E
            dl = dl_ref[0, :, pl.ds(qs, tq)]
            for r0 in range(0, tk, rc):
                c0 = r0 // hk * hq if masked else 0
                rows, cols = slice(r0, r0 + rc), slice(c0, tq)
                st = s_buf[slot][rows, cols]
                if masked:
                    row = lax.broadcasted_iota(jnp.int32, (rc, tq - c0), 0) + r0
                    col = lax.broadcasted_iota(jnp.int32, (rc, tq - c0), 1) + c0
                    st = jnp.where(row <= col, st, NEG)
                pt = jnp.exp2(st * EXP2_SCALE - lse2[:, cols])
                p_buf[slot][rows, cols] = pt.astype(BF16)
                g_buf[slot][rows, cols] = (pt * (dp_buf[slot][rows, cols] - dl[:, cols]) * SCALE).astype(BF16)

        def accumulate(t, slot, masked):
            qs = q_start(t)
            p, g = p_buf[slot], g_buf[slot]
            if not masked:
                dv_acc[...] += _dot(p[...], do_ref[pl.ds(qs, tq), :])
                dk_acc[...] += _dot(g[...], q_ref[0, pl.ds(qs, tq), :])
                dq_ref[0, pl.ds(qs, tq), :] += _dot_tn(g[...], k_ref[0])
                return
            q2 = pl.multiple_of(qs + hq, hq)
            dv_acc[0:hk, :] += _dot(p[0:hk, :], do_ref[pl.ds(qs, tq), :])
            dv_acc[hk:tk, :] += _dot(p[hk:tk, hq:tq], do_ref[pl.ds(q2, hq), :])
            dk_acc[0:hk, :] += _dot(g[0:hk, :], q_ref[0, pl.ds(qs, tq), :])
            dk_acc[hk:tk, :] += _dot(g[hk:tk, hq:tq], q_ref[0, pl.ds(q2, hq), :])
            dq_ref[0, pl.ds(qs, hq), :] += _dot_tn(g[0:hk, 0:hq], k_ref[0, 0:hk, :])
            dq_ref[0, pl.ds(q2, hq), :] += _dot_tn(g[:, hq:tq], k_ref[0])

        def last():
            dk_ref[0] = dk_acc[...]
            dv_ref[0] = dv_acc[...]

        dk_acc[...] = jnp.zeros_like(dk_acc)
        dv_acc[...] = jnp.zeros_like(dv_acc)
        _chunk_pipeline(nq - 1 - kj, 1, matmuls, pointwise, accumulate, last)

        if n_swap:
            pl.when(jnp.logical_and(pl.program_id(0) == N_HEADS - 1, kj == T // tk - 1))(drain)

    outs = pl.pallas_call(
        body, name="attn_bwd", grid=(N_HEADS, T // tk),
        in_specs=[pl.BlockSpec((1, T, HEAD_PAD), lambda h, j: (h, 0, 0)),
                  pl.BlockSpec((1, tk, HEAD_PAD), lambda h, j: (h, j, 0)),
                  pl.BlockSpec((1, tk, V_DIM), lambda h, j: (h, j, 0)),
                  pl.BlockSpec((T, V_DIM), lambda h, j: (0, h)),
                  pl.BlockSpec((1, 1, T), lambda h, j: (h, 0, 0)),
                  pl.BlockSpec((1, 1, T), lambda h, j: (h, 0, 0))] + [_ANY] * n_swap,
        out_specs=[pl.BlockSpec((1, T, HEAD_PAD), lambda h, j: (h, 0, 0)),
                   pl.BlockSpec((1, tk, HEAD_PAD), lambda h, j: (h, j, 0)),
                   pl.BlockSpec((1, tk, V_DIM), lambda h, j: (h, j, 0))] + [_ANY] * n_swap,
        out_shape=[jax.ShapeDtypeStruct((N_HEADS, T, HEAD_PAD), F32), jax.ShapeDtypeStruct((N_HEADS, T, HEAD_PAD), F32),
                   jax.ShapeDtypeStruct((N_HEADS, T, V_DIM), F32)] + _swapped_shapes(swap, []),
        scratch_shapes=[pltpu.VMEM((tk, tq), F32)] * 4 + [pltpu.VMEM((tk, tq), BF16)] * 4
                       + [pltpu.VMEM((tk, HEAD_PAD), F32), pltpu.VMEM((tk, V_DIM), F32)]
                       + ([pltpu.SemaphoreType.DMA((n_swap,))] * 2 if n_swap else []),
        compiler_params=_params(dimension_semantics=("arbitrary", "arbitrary")),
    )(q, k, v, do, lse_row, delta_row, *swap)
    return outs[:3], outs[3:]


def _bwd_proj(x, dx1, pos, proj, dq, dk, dv, dtail, du, g_in, w_in, g_cq, w_uq, g_ckv, w_ukv, gq, gk, conv_w,
              invf, sgn, tm):
    T = x.shape[0]
    nt = T // tm

    ts = min(SUB_TILE, tm)

    def body(x_ref, dx1_ref, pos_ref, lat_ref, cc_ref, cx_ref, dq_ref, dk_ref, dv_ref, dtail_ref, du_ref, dun_ref, *rest):
        consts, (gx_ref, h_ref, dproj_ref), sums = rest[:11], rest[11:14], rest[14:]
        cw_ref = consts[8]
        i = pl.program_id(0)

        @pl.when(i == 0)
        def _():
            for r in sums:
                r[...] = jnp.zeros_like(r)

        du_v = du_ref[...]
        not_last = jnp.where(i < nt - 1, 1.0, 0.0)
        nx0 = dun_ref[0:1, :] * not_last
        nx1 = dun_ref[1:2, :] * not_last
        row = lax.broadcasted_iota(jnp.int32, du_v.shape, 0)
        du1 = jnp.where(row == tm - 1, nx0, pltpu.roll(du_v, tm - 1, 0))
        du2 = jnp.where(row == tm - 2, nx0, jnp.where(row == tm - 1, nx1, pltpu.roll(du_v, tm - 2, 0)))
        dvc = cw_ref[2:3, :] * du_v + cw_ref[1:2, :] * du1 + cw_ref[0:1, :] * du2
        dproj_ref[:, 1536:2048] = (dvc * cx_ref[...]).astype(BF16)
        dproj_ref[:, 2048:2560] = (dvc * cc_ref[...]).astype(BF16)

        for r0 in range(0, tm, ts):
            rows = slice(r0, r0 + ts)
            work(x_ref.at[rows, :], dx1_ref.at[rows, :], pos_ref.at[:, rows], lat_ref.at[rows, :],
                 dq_ref.at[:, rows, :], dk_ref.at[:, rows, :], dv_ref.at[:, rows, :], dtail_ref.at[rows, :], *consts,
                 gx_ref.at[rows, :], h_ref.at[:, rows], dproj_ref.at[rows, :], *sums)

    def work(x_ref, dx1_ref, pos_ref, lat_ref, dq_ref, dk_ref, dv_ref, dtail_ref,
             g_in_ref, w_in_ref, g_cq_ref, w_uq_ref, g_ckv_ref, w_ukv_ref, gq_ref, gk_ref, cw_ref, invf_ref, sgn_ref,
             gx_ref, h_ref, dproj_ref, dw_uq_ref, dw_ukv_ref, dg_in_ref, dg_cq_ref, dg_ckv_ref, dgq_ref, dgk_ref):
        xv = x_ref[...]
        r0 = _rep(_inv_rms_mxu(xv), D_MODEL)
        xh0 = xv * r0
        g_in = g_in_ref[...]
        h_ref[...] = (xh0 * g_in).astype(BF16).T

        c_q = lat_ref[:, 0:Q_LORA]
        rq = _rep(_inv_rms_mxu(c_q), Q_LORA)
        xq = c_q * rq
        g_cq = g_cq_ref[...]
        cqn = (xq * g_cq).astype(BF16)
        c_kv = lat_ref[:, Q_LORA:Q_LORA + KV_LORA]
        rkv = _inv_rms_mxu(c_kv)
        xkv = c_kv * rkv
        g_ckv = g_ckv_ref[...]
        ckvn = (xkv * g_ckv).astype(BF16)
        kpe = lat_ref[:, 384:512]
        kpe_sq = kpe * kpe
        cos_b, sin_b = _rope_tables(pos_ref, invf_ref, sgn_ref)
        gq_a, gq_b = gq_ref[:, 0:NOPE], gq_ref[:, NOPE:HEAD_PAD]
        gk_a, gk_b = gk_ref[:, 0:NOPE], gk_ref[:, NOPE:HEAD_PAD]

        dproj_ref[:, 512:1536] = dtail_ref[:, 0:1024]
        dproj_ref[:, 2560:3072] = dtail_ref[:, 1024:1536]

        def dh_part(c0):
            return _dot_nt(dproj_ref[:, c0:c0 + 512], w_in_ref[:, c0:c0 + 512])

        later_chunks = ((512,), (1024,), (1536, 2048), (2560,))
        dh = jnp.zeros((ts, D_MODEL), F32)
        acc = dict(dh=dh, dkpe=jnp.zeros((ts, LANES), F32), dcqn=jnp.zeros((ts, Q_LORA), F32),
                   dckvn=jnp.zeros((ts, KV_LORA), F32))

        def dh_chunks():
            for chunks in later_chunks:
                for chunk in chunks:
                    acc["dh"] = acc["dh"] + dh_part(chunk)
                    yield

        def queries(hd):
            qh = _dot(cqn, w_uq_ref[hd])
            yield
            a, b = qh[:, 0:NOPE], qh[:, NOPE:HEAD_PAD]
            r = lax.rsqrt(_lane_sum(a * a + b * b) / QK_DIM + EPS)
            yield
            xa, xb = a * r, b * r
            dan = dq_ref[hd, :, 0:NOPE]
            dbr = dq_ref[hd, :, NOPE:HEAD_PAD]
            dbn = dbr * cos_b + _swap_rope_halves(dbr * sin_b)
            yield
            dgq_ref[:, 0:NOPE] += _colsum(dan * xa)
            dgq_ref[:, NOPE:HEAD_PAD] += _colsum(dbn * xb)
            dxa, dxb = dan * gq_a, dbn * gq_b
            cq = _lane_sum(dxa * xa + dxb * xb) / QK_DIM
            yield
            dqh = jnp.concatenate([r * (dxa - xa * cq), r * (dxb - xb * cq)], axis=-1).astype(BF16)
            yield
            dw_uq_ref[hd] += _dot_tn(cqn, dqh)
            yield
            acc["dcqn"] = acc["dcqn"] + _dot_nt(dqh, w_uq_ref[hd])
            yield

        def keys(hd):
            kvh = _dot(ckvn, w_ukv_ref[hd])
            yield
            ka = kvh[:, 0:NOPE]
            rk = lax.rsqrt(_lane_sum(ka * ka + kpe_sq) / QK_DIM + EPS)
            yield
            xka, xkb = ka * rk, kpe * rk
            dkan = dk_ref[hd, :, 0:NOPE]
            dkbr = dk_ref[hd, :, NOPE:HEAD_PAD]
            dkbn = dkbr * cos_b + _swap_rope_halves(dkbr * sin_b)
            yield
            dgk_ref[:, 0:NOPE] += _colsum(dkan * xka)
            dgk_ref[:, NOPE:HEAD_PAD] += _colsum(dkbn * xkb)
            dxka, dxkb = dkan * gk_a, dkbn * gk_b
            ck = _lane_sum(dxka * xka + dxkb * xkb) / QK_DIM
            yield
            acc["dkpe"] = acc["dkpe"] + rk * (dxkb - xkb * ck)
            dkvh = jnp.concatenate([rk * (dxka - xka * ck), dv_ref[hd]], axis=-1).astype(BF16)
            yield
            dw_ukv_ref[hd] += _dot_tn(ckvn, dkvh)
            yield
            acc["dckvn"] = acc["dckvn"] + _dot_nt(dkvh, w_ukv_ref[hd])
            yield

        chains = [dh_chunks()]
        for hd in range(N_HEADS):
            chains += [queries(hd), keys(hd)]
        _round_robin(chains, 5)
        dh, dkpe, dcqn, dckvn = acc["dh"], acc["dkpe"], acc["dcqn"], acc["dckvn"]

        dg_cq_ref[...] += _colsum(dcqn * xq)
        dxq = dcqn * g_cq
        dproj_ref[:, 0:Q_LORA] = (rq * (dxq - xq * _rep(_lane_sum(dxq * xq) / Q_LORA, Q_LORA))).astype(BF16)
        dg_ckv_ref[...] += _colsum(dckvn * xkv)
        dxkv = dckvn * g_ckv
        dproj_ref[:, 256:384] = (rkv * (dxkv - xkv * (_lane_sum(dxkv * xkv) / KV_LORA))).astype(BF16)
        dproj_ref[:, 384:512] = dkpe.astype(BF16)
        dh = dh + dh_part(0)
        dg_in_ref[...] += _colsum(dh * xh0)
        dxh = dh * g_in
        gx_ref[...] = dx1_ref[...] + r0 * (dxh - xh0 * _rep(_lane_sum(dxh * xh0) / D_MODEL, D_MODEL))

    row = lambda i: (i, 0)
    col = lambda c: (lambda i: (i, c))
    head_rows = lambda i: (0, i, 0)
    nxt = lambda i: (jnp.minimum((i + 1) * (tm // 8), T // 8 - 1), 0)
    in_specs = [pl.BlockSpec((tm, D_MODEL), row), pl.BlockSpec((tm, D_MODEL), row), pl.BlockSpec((1, tm), lambda i: (0, i)),
                pl.BlockSpec((tm, 512), col(0)), pl.BlockSpec((tm, 512), col(3)), pl.BlockSpec((tm, 512), col(4)),
                pl.BlockSpec((N_HEADS, tm, HEAD_PAD), head_rows), pl.BlockSpec((N_HEADS, tm, HEAD_PAD), head_rows),
                pl.BlockSpec((N_HEADS, tm, V_DIM), head_rows), pl.BlockSpec((tm, 1536), row),
                pl.BlockSpec((tm, CONV_W), row), pl.BlockSpec((8, CONV_W), nxt),
                _full((1, D_MODEL)), _full((D_MODEL, PROJ_EXT)), _full((1, Q_LORA)), _full((N_HEADS, Q_LORA, HEAD_PAD)),
                _full((1, KV_LORA)), _full((N_HEADS, KV_LORA, HEAD_PAD)), _full((1, HEAD_PAD)), _full((1, HEAD_PAD)),
                _full((3, CONV_W)), _full((1, LANES)), _full((1, LANES))]
    out_specs = [pl.BlockSpec((tm, D_MODEL), row), pl.BlockSpec((D_MODEL, tm), lambda i: (0, i)),
                 pl.BlockSpec((tm, PROJ_EXT), row),
                 _full((N_HEADS, Q_LORA, HEAD_PAD)), _full((N_HEADS, KV_LORA, HEAD_PAD)),
                 _full((1, D_MODEL)), _full((1, Q_LORA)), _full((1, KV_LORA)), _full((1, HEAD_PAD)), _full((1, HEAD_PAD))]
    out_shape = [jax.ShapeDtypeStruct((T, D_MODEL), F32), jax.ShapeDtypeStruct((D_MODEL, T), BF16),
                 jax.ShapeDtypeStruct((T, PROJ_EXT), BF16),
                 jax.ShapeDtypeStruct((N_HEADS, Q_LORA, HEAD_PAD), F32), jax.ShapeDtypeStruct((N_HEADS, KV_LORA, HEAD_PAD), F32),
                 jax.ShapeDtypeStruct((1, D_MODEL), F32), jax.ShapeDtypeStruct((1, Q_LORA), F32),
                 jax.ShapeDtypeStruct((1, KV_LORA), F32), jax.ShapeDtypeStruct((1, HEAD_PAD), F32),
                 jax.ShapeDtypeStruct((1, HEAD_PAD), F32)]
    return pl.pallas_call(
        body, name="bwd_proj", grid=(nt,), in_specs=in_specs, out_specs=out_specs, out_shape=out_shape,
        compiler_params=_params(dimension_semantics=("arbitrary",)),
    )(x, dx1, pos, proj, proj, proj, dq, dk, dv, dtail, du, du, g_in, w_in, g_cq, w_uq, g_ckv, w_ukv, gq, gk, conv_w,
      invf, sgn)


def _matmul_acc(a, b, tt, tn, parts):
    M, T = a.shape
    N = b.shape[1]
    n = len(parts)
    grid = (N // tn, T // tt)
    hm = M // 2

    def body(a_ref, b_ref, *rest):
        part_refs, (o_ref, sib_ref), rest = rest[:n], rest[n:n + 2], rest[n + 2:]
        out_refs, (stage_ref, tile_send, tile_recv), sems = rest[:n], rest[n:n + 3], rest[n + 3:]
        j, t = pl.program_id(0), pl.program_id(1)
        if n:
            start, drain = _scatter_steps(part_refs, out_refs, *sems)
            pl.when(jnp.logical_and(j == 0, t == 0))(start)

        def to_sibling(jj):
            x, y, c = _mesh_pos()
            return _remote(stage_ref, sib_ref.at[:, pl.ds(pl.multiple_of(jj * tn, tn), tn)],
                           tile_send, tile_recv, jj, (x, y, 1 - c))

        @pl.when(t == 0)
        def _():
            o_ref[...] = jnp.zeros_like(o_ref)

        o_ref[...] += _dot(a_ref[...], b_ref[...])

        tile_done = t == grid[1] - 1
        pl.when(jnp.logical_and(tile_done, j > 0))(lambda: to_sibling(j - 1).wait())

        @pl.when(tile_done)
        def _():
            c = lax.axis_index("c")
            stage_ref[...] = o_ref[pl.ds(pl.multiple_of((1 - c) * hm, hm), hm), :]
            to_sibling(j).start()

        pl.when(jnp.logical_and(tile_done, j == grid[0] - 1))(lambda: to_sibling(j).wait())
        if n:
            pl.when(jnp.logical_and(j == grid[0] - 1, t == grid[1] - 1))(drain)

    sems = [pltpu.SemaphoreType.DMA((3 * n,)), pltpu.SemaphoreType.DMA((3 * n,)), pltpu.SemaphoreType.DMA((n,))]
    outs = pl.pallas_call(
        body, name="dw_in", grid=grid,
        in_specs=[pl.BlockSpec((M, tt), lambda j, t: (0, t)), pl.BlockSpec((tt, tn), lambda j, t: (t, j))] + [_ANY] * n,
        out_specs=[pl.BlockSpec((M, tn), lambda j, t: (0, j)), _ANY] + [_ANY] * n,
        out_shape=[jax.ShapeDtypeStruct((M, N), F32), jax.ShapeDtypeStruct((hm, N), F32)] + _scattered_shapes(parts),
        scratch_shapes=[pltpu.VMEM((hm, tn), F32)] + [pltpu.SemaphoreType.DMA((grid[0],))] * 2 + (sems if n else []),
        compiler_params=_params(dimension_semantics=("arbitrary", "arbitrary")),
    )(a, b, *parts)
    return outs[0], outs[1], outs[2:]


def _add_chips(parts, small_parts):
    arrays = list(parts) + [small_parts]

    def body(*refs):
        ins, outs = refs[:len(arrays)], refs[len(arrays):]
        for a_ref, o_ref in zip(ins, outs):
            part = lambda k: a_ref[k].astype(F32)
            o_ref[...] = ((part(0) + part(1)) + part(2)) + part(3)

    in_specs, out_specs, out_shape = [], [], []
    for a in arrays:
        _, rows, cols = a.shape
        in_specs.append(pl.BlockSpec((N_CHIPS, rows // 2, cols), lambda i: (0, i, 0)))
        out_specs.append(pl.BlockSpec((rows // 2, cols), lambda i: (i, 0)))
        out_shape.append(jax.ShapeDtypeStruct((rows, cols), F32))
    outs = pl.pallas_call(body, name="add_chips", grid=(2,), in_specs=in_specs, out_specs=out_specs,
                          out_shape=out_shape, compiler_params=_params(dimension_semantics=("arbitrary",)))(*arrays)
    return outs[:-1], outs[-1]


def _adamw_small(ws, gs, ms, vs):
    n = len(ws)

    def body(*refs):
        for i in range(n):
            w_ref, g_ref, m_ref, v_ref = (refs[k * n + i] for k in range(4))
            d_ref, nm_ref, nv_ref = (refs[(4 + k) * n + i] for k in range(3))
            _adamw_math(g_ref[...], w_ref, m_ref, v_ref, d_ref, nm_ref, nv_ref)

    shapes = [jax.ShapeDtypeStruct(w.shape, F32) for w in ws]
    outs = pl.pallas_call(body, name="adamw_small", out_shape=shapes * 3)(*ws, *gs, *ms, *vs)
    return outs[:n], outs[n:2 * n], outs[2 * n:]


def _adamw_math(gv, w_ref, m_ref, v_ref, d_ref, nm_ref, nv_ref):
    nm = B1 * m_ref[...] + (1.0 - B1) * gv
    nv = B2 * v_ref[...] + (1.0 - B2) * (gv * gv)
    m_hat = nm / (1.0 - B1 ** STEP)
    v_hat = nv / (1.0 - B2 ** STEP)
    d_ref[...] = -LR * (m_hat / (jnp.sqrt(v_hat) + ADAM_EPS) + WD * w_ref[...])
    nm_ref[...] = nm
    nv_ref[...] = nv


def _adamw_halves(w, mine, other, m, v, c, name, transposed):
    hr, cols = mine.shape

    def body(c_ref, w_ref, mine_ref, other_ref, m_ref, v_ref, g_ref, d_ref, nm_ref, nv_ref, *picked):
        gv = jnp.where(pl.program_id(0) == c_ref[0], mine_ref[...], other_ref[...])
        if transposed:
            picked[0][...] = gv
            _store_transposed(picked[0], g_ref)
            gv = g_ref[...]
        else:
            g_ref[...] = gv
        _adamw_math(gv, w_ref, m_ref, v_ref, d_ref, nm_ref, nv_ref)

    if transposed:
        half = pl.BlockSpec((cols, hr), lambda i, c_ref: (0, i))
    else:
        half = pl.BlockSpec((hr, cols), lambda i, c_ref: (i, 0))
    whole = pl.BlockSpec((hr, cols), lambda i, c_ref: (0, 0))
    shp = jax.ShapeDtypeStruct(w.shape, F32)
    return pl.pallas_call(
        body, name=name, out_shape=[shp] * 4,
        grid_spec=pltpu.PrefetchScalarGridSpec(num_scalar_prefetch=1, grid=(2,), in_specs=[half, whole, whole, half, half],
                                               out_specs=[half] * 4,
                                               scratch_shapes=[pltpu.VMEM((hr, cols), F32)] if transposed else []),
        compiler_params=_params(dimension_semantics=("arbitrary",)),
    )(c.reshape(1), w, mine, other, m, v)


_ANY = pl.BlockSpec(memory_space=pl.ANY)


def _mesh_pos():
    return lax.axis_index("x"), lax.axis_index("y"), lax.axis_index("c")


def _other_chips(x, y):
    return [(1 - x, y), (x, 1 - y), (1 - x, 1 - y)]


def _remote(src, dst, send_sems, recv_sems, k, to):
    return pltpu.make_async_remote_copy(src_ref=src, dst_ref=dst, send_sem=send_sems.at[k], recv_sem=recv_sems.at[k],
                                        device_id=to, device_id_type=MESH)


def _gather_weights(shards, n_transposed):
    n = len(shards)
    shapes = [s.shape[::-1] if i < n_transposed else s.shape for i, s in enumerate(shards)]

    def body(*refs):
        start, forward, drain = _gather_steps(shapes, refs[:n], refs[n:2 * n], refs[2 * n:3 * n], *refs[3 * n:])
        start()
        forward()
        drain()

    vmem = pl.BlockSpec(memory_space=pltpu.VMEM)
    return pl.pallas_call(
        body, name="gather_weights", in_specs=[vmem] * n, out_specs=[_ANY] * n,
        out_shape=_gathered_shapes(shapes), scratch_shapes=_gather_scratch(shapes), compiler_params=_params(),
    )(*shards)


def _travel_shape(shape):
    rows, cols = shape
    return (rows, HEAD_PAD if cols == QK_DIM else cols)


def _gathered_shapes(shapes):
    return [jax.ShapeDtypeStruct((N_CHIPS,) + _travel_shape(s), BF16) for s in shapes]


def _gather_scratch(shapes):
    n = len(shapes)
    return ([pltpu.VMEM(_travel_shape(s), BF16) for s in shapes]
            + [pltpu.SemaphoreType.DMA((6 * n,)), pltpu.SemaphoreType.DMA((6 * n,)), pltpu.SemaphoreType.DMA((n,))])


def _gather_steps(shapes, ins, outs, stage, send_sems, recv_sems, local_sems):
    n = len(shapes)
    halved = [s[0] % 32 == 0 for s in shapes]

    def part(i, ref, hc):
        if not halved[i]:
            return ref
        hr = shapes[i][0] // 2
        return ref.at[pl.ds(hc * hr, hr), :]

    def to_chip(i, j, x, y, c):
        cx, cy = _other_chips(x, y)[j]
        return _remote(part(i, stage[i], c), part(i, outs[i].at[2 * x + y], c), send_sems, recv_sems, 6 * i + j, (cx, cy, c))

    def to_sibling(i, j, x, y, c):
        cx, cy = _other_chips(x, y)[j]
        got = part(i, outs[i].at[2 * cx + cy], c)
        return _remote(got, got, send_sems, recv_sems, 6 * i + 3 + j, (x, y, 1 - c))

    def local(i, x, y):
        return pltpu.make_async_copy(stage[i], outs[i].at[2 * x + y], local_sems.at[i])

    def start():
        x, y, c = _mesh_pos()
        for i in range(n):
            cols = shapes[i][1]
            if stage[i].shape[1] != cols:
                stage[i][...] = jnp.zeros_like(stage[i])
            if ins[i].shape == shapes[i]:
                stage[i][:, 0:cols] = ins[i][...].astype(BF16)
            else:
                _store_transposed(ins[i], stage[i])
            local(i, x, y).start()
            for j in range(3):
                to_chip(i, j, x, y, c).start()

    def forward():
        x, y, c = _mesh_pos()
        for i in range(n):
            for j, (cx, cy) in enumerate(_other_chips(x, y)):
                got = part(i, outs[i].at[2 * cx + cy], c)
                _remote(got, got, send_sems, recv_sems, 6 * i + j, (cx, cy, c)).wait_recv()
                if halved[i]:
                    to_sibling(i, j, x, y, c).start()

    def drain():
        x, y, c = _mesh_pos()
        for i in range(n):
            for j, (cx, cy) in enumerate(_other_chips(x, y)):
                if halved[i]:
                    got = part(i, outs[i].at[2 * cx + cy], 1 - c)
                    _remote(got, got, send_sems, recv_sems, 6 * i + 3 + j, (x, y, 1 - c)).wait_recv()
                    to_sibling(i, j, x, y, c).wait_send()
                to_chip(i, j, x, y, c).wait_send()
            local(i, x, y).wait()

    return start, forward, drain


def _swap_halves(grads, whole, name):
    n, m = len(grads), len(grads) + len(whole)

    def body(*refs):
        start, drain = _swap_steps(n, refs[:m], refs[m:2 * m], refs[2 * m], refs[2 * m + 1])
        start()
        drain()

    outs = pl.pallas_call(
        body, name=name, in_specs=[_ANY] * m, out_specs=[_ANY] * m, out_shape=_swapped_shapes(grads, whole),
        scratch_shapes=[pltpu.SemaphoreType.DMA((m,)), pltpu.SemaphoreType.DMA((m,))],
    )(*grads, *whole)
    return outs[:n], outs[n:]


def _swapped_shapes(grads, whole):
    return ([jax.ShapeDtypeStruct((g.shape[0], g.shape[1] // 2, g.shape[2]), F32) for g in grads]
            + [jax.ShapeDtypeStruct(w.shape, F32) for w in whole])


def _swap_steps(n, ins, outs, send_sems, recv_sems):
    def copies():
        x, y, c = _mesh_pos()
        cps = []
        for i, src in enumerate(ins):
            if i < n:
                hr = src.shape[1] // 2
                src = src.at[:, pl.ds((1 - c) * hr, hr), :]
            cps.append(_remote(src, outs[i], send_sems, recv_sems, i, (x, y, 1 - c)))
        return cps

    def start():
        for cp in copies():
            cp.start()

    def drain():
        for cp in copies():
            cp.wait()

    return start, drain


def _scattered_shapes(parts):
    return [jax.ShapeDtypeStruct(p.shape if p.ndim == 3 else (N_CHIPS,) + p.shape, p.dtype) for p in parts]


def _scatter_steps(ins, outs, send_sems, recv_sems, local_sems):
    n = len(ins)

    def src(i, k):
        return ins[i].at[k] if len(ins[i].shape) == 3 else ins[i]

    def sends(x, y, c):
        return [_remote(src(i, 2 * cx + cy), outs[i].at[2 * x + y], send_sems, recv_sems, 3 * i + j, (cx, cy, c))
                for i in range(n) for j, (cx, cy) in enumerate(_other_chips(x, y))]

    def local(i, x, y):
        return pltpu.make_async_copy(src(i, 2 * x + y), outs[i].at[2 * x + y], local_sems.at[i])

    def start():
        x, y, c = _mesh_pos()
        for i in range(n):
            local(i, x, y).start()
        for cp in sends(x, y, c):
            cp.start()

    def drain():
        x, y, c = _mesh_pos()
        for i in range(n):
            for j, (cx, cy) in enumerate(_other_chips(x, y)):
                got = outs[i].at[2 * cx + cy]
                _remote(got, got, send_sems, recv_sems, 3 * i + j, (cx, cy, c)).wait_recv()
        for cp in sends(x, y, c):
            cp.wait_send()
        for i in range(n):
            local(i, x, y).wait()

    return start, drain


def _add_pair(grads, from_sibling, small, small_sibling, c):
    n = len(grads)

    def body(c_ref, *refs):
        ins, outs = refs[:2 * n + 2], refs[2 * n + 2:]
        for i in range(n + 1):
            outs[i][...] = (ins[2 * i][...] + ins[2 * i + 1][...]).astype(outs[i].dtype)

    in_specs, out_specs, out_shape, args = [], [], [], []
    for g, r in zip(grads, from_sibling):
        _, hr, cols = r.shape
        in_specs += [pl.BlockSpec((1, hr, cols), lambda k, c_ref: (k, c_ref[0], 0)),
                     pl.BlockSpec((1, hr, cols), lambda k, c_ref: (k, 0, 0))]
        out_specs.append(pl.BlockSpec((1, hr, cols), lambda k, c_ref: (k, 0, 0)))
        out_shape.append(jax.ShapeDtypeStruct(r.shape, BF16))
        args += [g, r]
    whole = pl.BlockSpec(small.shape, lambda k, c_ref: (0, 0))
    in_specs += [whole, whole]
    out_specs.append(whole)
    out_shape.append(jax.ShapeDtypeStruct(small.shape, F32))
    outs = pl.pallas_call(
        body, name="add_pair", out_shape=out_shape,
        grid_spec=pltpu.PrefetchScalarGridSpec(num_scalar_prefetch=1, grid=(N_CHIPS,), in_specs=in_specs,
                                               out_specs=out_specs),
        compiler_params=_params(dimension_semantics=("arbitrary",)),
    )(c.reshape(1), *args, small, small_sibling)
    return outs[:n], outs[n]


def _scatter_w_in(dw_in_e, from_sibling):
    hr = from_sibling.shape[1]
    shard = (N_CHIPS, hr, SHARD_COLS_IN)

    def body(g_in, r_in, out, g_buf, r_buf, p_buf, load_sems, send_sems, recv_sems, local_sems):
        c = lax.axis_index("c")
        loads = (pltpu.make_async_copy(g_in.at[0, pl.ds(c * hr, hr), :], g_buf, load_sems.at[0]),
                 pltpu.make_async_copy(r_in.at[0], r_buf, load_sems.at[1]))
        for cp in loads:
            cp.start()
        for cp in loads:
            cp.wait()
        g_buf[...] += r_buf[...]
        p_buf[0, :, 0:KPE_END] = g_buf[:, 0:KPE_END].astype(BF16)
        p_buf[0, :, KPE_END:SHARD_COLS_IN] = g_buf[:, KPE_END + KPE_PAD:SHARD_COLS_IN + KPE_PAD].astype(BF16)
        for k in range(1, N_CHIPS):
            p_buf[k] = g_buf[:, SHARD_COLS_IN * k + KPE_PAD:SHARD_COLS_IN * (k + 1) + KPE_PAD].astype(BF16)
        start, drain = _scatter_steps([p_buf], [out], send_sems, recv_sems, local_sems)
        start()
        drain()

    return pl.pallas_call(
        body, name="scatter_grads", in_specs=[_ANY] * 2, out_specs=_ANY, out_shape=jax.ShapeDtypeStruct(shard, BF16),
        scratch_shapes=[pltpu.VMEM((hr, PROJ_EXT), F32)] * 2 + [pltpu.VMEM(shard, BF16)]
                       + [pltpu.SemaphoreType.DMA((2,)), pltpu.SemaphoreType.DMA((3,)), pltpu.SemaphoreType.DMA((3,)),
                          pltpu.SemaphoreType.DMA((1,))],
        compiler_params=_params(),
    )(dw_in_e, from_sibling)


def _share_halves(halves):
    n = len(halves)

    def body(*refs):
        ins, outs, send_sems, recv_sems = refs[:n], refs[n:2 * n], refs[2 * n], refs[2 * n + 1]
        x, y, c = _mesh_pos()
        cps = [_remote(ins[i], outs[i], send_sems, recv_sems, i, (x, y, 1 - c)) for i in range(n)]
        for cp in cps:
            cp.start()
        for cp in cps:
            cp.wait()

    return pl.pallas_call(
        body, name="share_halves", in_specs=[_ANY] * n, out_specs=[_ANY] * n,
        out_shape=[jax.ShapeDtypeStruct(h.shape, h.dtype) for h in halves],
        scratch_shapes=[pltpu.SemaphoreType.DMA((n,)), pltpu.SemaphoreType.DMA((n,))],
    )(*halves)


SHARD_COLS_IN = IN_TOTAL // N_CHIPS
KPE_END = Q_LORA + KV_LORA + ROPE
KPE_PAD = PROJ_EXT - IN_TOTAL


def _by_cols(a):
    return a.transpose(1, 0, 2).reshape(a.shape[1], N_CHIPS * a.shape[2])


def _assemble_early(c_in, c_uq, c_ukv, c_conv):
    return c_in, c_uq, c_ukv, _by_cols(c_conv).astype(F32)


def _assemble_late(c_o, c_pl, c_plg):
    return c_o.reshape(D_MODEL, D_MODEL), _by_cols(c_pl), c_plg.reshape(D_MODEL, D_MODEL)


def _split_late(dw_o, dw_pl, dw_plg):
    chip_major = lambda a: a.reshape(a.shape[0], N_CHIPS, a.shape[1] // N_CHIPS).transpose(1, 0, 2)
    return [dw_o.reshape(N_CHIPS, D_MODEL // N_CHIPS, D_MODEL), chip_major(dw_pl),
            dw_plg.reshape(N_CHIPS, D_MODEL // N_CHIPS, D_MODEL)]


def _local_step(x, p, pos, tgt, gains, early, late_shards, late_gathered, tm, tq):
    c_in, w_uq_e, w_ukv, conv_w = early
    g_in, g_cq, g_ckv, g_q, g_k, g_oa, g_oc, g_pl = gains
    T = x.shape[0]
    zpad = lambda a, n: jnp.concatenate([a, jnp.zeros(a.shape[:-1] + (n,), a.dtype)], axis=-1)
    gq, gk = zpad(g_q, HEAD_PAD - QK_DIM), zpad(g_k, HEAD_PAD - QK_DIM)
    inv_freq = 1.0 / (ROPE_THETA ** (jnp.arange(0, ROPE, 2, dtype=F32) / ROPE))
    invf = jnp.concatenate([inv_freq, inv_freq, jnp.zeros((64,), F32)]).reshape(1, LANES)
    sgn = jnp.concatenate([-jnp.ones((32,), F32), jnp.ones((32,), F32), jnp.zeros((64,), F32)]).reshape(1, LANES)

    (proj, q, k, v, w_in_e), gathered = _fwd_proj(x, pos, g_in, c_in, g_cq, w_uq_e, g_ckv, w_ukv, gq, gk, invf, sgn,
                                                  late_shards, min(2 * tm, T))
    w_o, w_pl, w_plg = _assemble_late(*(gathered if late_shards else late_gathered))
    o, lse = _attn_fwd(q, k, v, tq)
    (dx1, do, delta, dtail, du, dw_o, dw_pl, dw_plg, dg_oa, dg_oc, dg_pl, dconv, loss) = _tail(
        x, o, proj, p, tgt, g_oa, g_oc, g_pl, conv_w, w_o, w_pl, w_plg, tm)
    late_grads = _split_late(dw_o, dw_pl, dw_plg)
    (dq, dk, dv), late_sibling = _attn_bwd(q, k, v, do, lse, delta, tq, late_grads)
    (gx, h, dproj, dw_uq_e, dw_ukv, dg_in, dg_cq, dg_ckv, dgq, dgk) = _bwd_proj(
        x, dx1, pos, proj, dq, dk, dv, dtail, du, g_in, w_in_e, g_cq, w_uq_e, g_ckv, w_ukv, gq, gk, conv_w, invf, sgn, tm)
    wgrads = [dw_uq_e[:, :, :QK_DIM], dw_ukv, *late_grads]
    ggrads = (dg_in, dg_cq, dg_ckv, dgq, dgk, dg_oa, dg_oc, dg_pl)
    return loss, gx, (h, dproj), wgrads, late_sibling, ggrads, dconv


def kernel(x, p, positions, g_in, w_in, g_cq, w_uq, g_ckv, w_ukv, g_q, g_k, conv_w, g_oa, g_oc, w_o, w_pl, w_plg, g_pl, loss_target, m_g_in, m_w_in, m_g_cq, m_w_uq, m_g_ckv, m_w_ukv, m_g_q, m_g_k, m_conv_w, m_g_oa, m_g_oc, m_w_o, m_w_pl, m_w_plg, m_g_pl, v_g_in, v_w_in, v_g_cq, v_w_uq, v_g_ckv, v_w_ukv, v_g_q, v_g_k, v_conv_w, v_g_oa, v_g_oc, v_w_o, v_w_pl, v_w_plg, v_g_pl):
    T = x.shape[1]
    c = lax.axis_index("c")
    chip = 2 * lax.axis_index("x") + lax.axis_index("y")
    gains = [g.reshape(1, -1) for g in (g_in, g_cq, g_ckv, g_q, g_k, g_oa, g_oc, g_pl)]

    transposed = ("w_in", "w_uq")
    early = _assemble_early(*_gather_weights([w_in[0].T, w_uq[0].T, w_ukv[0], conv_w[0]], len(transposed)))

    loss, gx, (h_t, dproj), others_cm, late_sibling, ggrads, dconv = _local_step(
        x[0], p[0, 0], positions.reshape(1, T), loss_target[0], gains, early, [w_o[0], w_pl[0], w_plg[0]], None, 256, 512)

    small_parts = [a.reshape(-1, LANES) for a in (*ggrads, loss, dconv)]
    small_rows = [a.shape[0] for a in small_parts]
    tile_rows = [-(-r // 8) * 8 for r in small_rows]
    tile_rows[-1] += -sum(tile_rows) % 16
    small = jnp.concatenate([jnp.pad(a, ((0, t - r), (0, 0))) for a, r, t in zip(small_parts, small_rows, tile_rows)])
    n_early = len(others_cm) - len(late_sibling)
    early_sibling, (small_sibling,) = _swap_halves(others_cm[:n_early], [small], "pair_grads")
    chip_parts, chip_small = _add_pair(others_cm, [*early_sibling, *late_sibling], small, small_sibling, c)
    dw_in_e, w_in_sibling, exchanged = _matmul_acc(h_t, dproj, min(4096, T), 1024, [*chip_parts, chip_small])
    by_chip = [_scatter_w_in(dw_in_e[None], w_in_sibling[None]), *exchanged[:-1]]
    halves, small_total = _add_chips(by_chip, exchanged[-1])
    other_halves = _share_halves(halves)

    gg, off = [], 0
    for rows, tiled in zip(small_rows, tile_rows):
        gg.append(small_total[off:off + rows].reshape(1, -1))
        off += tiled
    loss_out = gg[8][0, 0]
    conv_total = gg[9].reshape(3, CONV_W)
    conv_g = lax.dynamic_slice(conv_total, (0, chip * (CONV_W // N_CHIPS)), (3, CONV_W // N_CHIPS))
    g_by_name = dict(g_in=gg[0], g_cq=gg[1], g_ckv=gg[2], g_q=gg[3][:, :QK_DIM], g_k=gg[4][:, :QK_DIM], conv_w=conv_g,
                     g_oa=gg[5], g_oc=gg[6], g_pl=gg[7])
    half_by_name = dict(zip(("w_in", "w_uq", "w_ukv", "w_o", "w_pl", "w_plg"), zip(halves, other_halves)))
    weights = dict(g_in=g_in, w_in=w_in, g_cq=g_cq, w_uq=w_uq, g_ckv=g_ckv, w_ukv=w_ukv, g_q=g_q, g_k=g_k,
                   conv_w=conv_w, g_oa=g_oa, g_oc=g_oc, w_o=w_o, w_pl=w_pl, w_plg=w_plg, g_pl=g_pl)
    ms = dict(g_in=m_g_in, w_in=m_w_in, g_cq=m_g_cq, w_uq=m_w_uq, g_ckv=m_g_ckv, w_ukv=m_w_ukv, g_q=m_g_q, g_k=m_g_k,
              conv_w=m_conv_w, g_oa=m_g_oa, g_oc=m_g_oc, w_o=m_w_o, w_pl=m_w_pl, w_plg=m_w_plg, g_pl=m_g_pl)
    vs = dict(g_in=v_g_in, w_in=v_w_in, g_cq=v_g_cq, w_uq=v_w_uq, g_ckv=v_g_ckv, w_ukv=v_w_ukv, g_q=v_g_q, g_k=v_g_k,
              conv_w=v_conv_w, g_oa=v_g_oa, g_oc=v_g_oc, w_o=v_w_o, w_pl=v_w_pl, w_plg=v_w_plg, g_pl=v_g_pl)
    names = list(weights)
    flat = lambda a: a.reshape(-1, a.shape[-1])
    small_names = list(g_by_name)
    one_row = lambda a: a.reshape(1, -1)
    small_out = _adamw_small([one_row(weights[n]) for n in small_names], [one_row(g_by_name[n]) for n in small_names],
                             [one_row(ms[n]) for n in small_names], [one_row(vs[n]) for n in small_names])
    results = {n: (g_by_name[n], *(out[i] for out in small_out)) for i, n in enumerate(small_names)}
    for n in half_by_name:
        shard = (lambda a: a[0].T) if n in transposed else flat
        out = _adamw_halves(shard(weights[n]), *half_by_name[n], shard(ms[n]), shard(vs[n]), c, "adamw_" + n,
                            n in transposed)
        results[n] = [a.T for a in out] if n in transposed else out
    per_kind = [[results[n][kind].reshape(weights[n].shape) for n in names] for kind in range(4)]
    return (loss_out, gx.reshape(x.shape), *per_kind[0], *per_kind[1], *per_kind[2], *per_kind[3])
```

```python
import math

import jax
import jax.numpy as jnp
from jax import lax
from jax.experimental import pallas as pl
from jax.experimental.pallas import tpu as pltpu

F32 = jnp.float32
BF16 = jnp.bfloat16

D_MODEL = 1024
N_HEADS = 4
NOPE = 128
ROPE = 64
V_DIM = 128
QK_DIM = NOPE + ROPE
HEAD_PAD = 256
Q_LORA = 256
KV_LORA = 128
ATTN_W = 512
CONV_W = 512
PLE = 256
IN_TOTAL = 3008
PROJ_EXT = 3072
ROPE_THETA = 10000.0
EPS = 1e-6
SCALE = 1.0 / math.sqrt(QK_DIM)
LOG2E = math.log2(math.e)
EXP2_SCALE = SCALE * LOG2E
NEG = -1e30
SOFTMAX_ROWS = 32
SUB_TILE = 256

LR, B1, B2, ADAM_EPS, WD, STEP = 0.001, 0.9, 0.999, 1e-08, 0.01, 10

N_CHIPS = 4
LANES = 128
VMEM_LIMIT = 56 * 1024 * 1024
MESH = pl.DeviceIdType.MESH


def _params(**kw):
    return pltpu.CompilerParams(vmem_limit_bytes=VMEM_LIMIT, **kw)


def _inv_rms(x, n):
    return lax.rsqrt(jnp.sum(x * x, axis=-1, keepdims=True) / n + EPS)


def _lane_sum(a):
    folded = a[:, 0:LANES]
    for c0 in range(LANES, a.shape[1], LANES):
        folded = folded + a[:, c0:c0 + LANES]
    head = folded.astype(BF16)
    tail = (folded - head.astype(F32)).astype(BF16)
    return _dot(jnp.concatenate([head, tail], axis=1), jnp.ones((2 * LANES, LANES), BF16))


def _inv_rms_mxu(x):
    return lax.rsqrt(_lane_sum(x * x) / x.shape[1] + EPS)


def _rep(r, width):
    return r if width == LANES else jnp.tile(r, (1, width // LANES))


def _sigmoid(z):
    return jax.nn.sigmoid(z)


def _swap_rope_halves(b):
    lane = lax.broadcasted_iota(jnp.int32, b.shape, 1)
    swapped = jnp.where(lane < 32, pltpu.roll(b, 96, 1), pltpu.roll(b, 32, 1))
    return jnp.where(lane < ROPE, swapped, 0.0)


def _dot(a, b):
    return jnp.dot(a, b, preferred_element_type=F32)


def _dot_nt(a, b):
    return lax.dot_general(a, b, (((1,), (1,)), ((), ())), preferred_element_type=F32)


def _dot_tn(a, b):
    return lax.dot_general(a, b, (((0,), (0,)), ((), ())), preferred_element_type=F32)


def _colsum(a):
    return jnp.sum(a, axis=0, keepdims=True)


def _store_transposed(src_ref, dst_ref):
    r, c = src_ref.shape
    for r0 in range(0, r, LANES):
        h = min(LANES, r - r0)
        for c0 in range(0, c, LANES):
            w = min(LANES, c - c0)
            piece = src_ref[r0:r0 + h, c0 + w - LANES:c0 + w]
            if h < LANES:
                piece = jnp.concatenate([piece, jnp.zeros((LANES - h, LANES), piece.dtype)], axis=0)
            dst_ref[c0:c0 + w, r0:r0 + h] = piece.T[LANES - w:, 0:h].astype(dst_ref.dtype)


def _full(shape):
    return pl.BlockSpec(shape, lambda *_: (0,) * len(shape))


def _round_robin(chains, width):
    waiting, active = list(chains), []
    while waiting or active:
        while waiting and len(active) < width:
            active.append(waiting.pop(0))
        for chain in list(active):
            if next(chain, _DONE) is _DONE:
                active.remove(chain)


_DONE = object()


def _rope_tables(pos_ref, invf_ref, sgn_ref):
    pos = jnp.broadcast_to(pos_ref[...].astype(F32), (LANES, pos_ref.shape[1])).T
    ang = pos * invf_ref[...]
    return jnp.cos(ang), jnp.sin(ang) * sgn_ref[...]


def _fwd_proj(x, pos, g_in, c_in, g_cq, w_uq, g_ckv, w_ukv, gq, gk, invf, sgn, late_shards, tm):
    T = x.shape[0]
    nt = T // tm
    n_late = len(late_shards)
    ts = min(SUB_TILE, tm)

    def body(x_ref, pos_ref, g_in_ref, c_in_ref, g_cq_ref, w_uq_ref, g_ckv_ref, w_ukv_ref, gq_ref, gk_ref,
             invf_ref, sgn_ref, *rest):
        late_in, (proj_ref, q_ref, k_ref, v_ref, w_in_ref) = rest[:n_late], rest[n_late:n_late + 5]
        late_out, late_scratch = rest[n_late + 5:2 * n_late + 5], rest[2 * n_late + 5:]
        i = pl.program_id(0)

        @pl.when(i == 0)
        def _():
            w_in_ref[:, 0:KPE_END] = c_in_ref[0, :, 0:KPE_END]
            w_in_ref[:, KPE_END:KPE_END + KPE_PAD] = jnp.zeros((D_MODEL, KPE_PAD), BF16)
            w_in_ref[:, KPE_END + KPE_PAD:SHARD_COLS_IN + KPE_PAD] = c_in_ref[0, :, KPE_END:SHARD_COLS_IN]
            for chip in range(1, N_CHIPS):
                w_in_ref[:, SHARD_COLS_IN * chip + KPE_PAD:SHARD_COLS_IN * (chip + 1) + KPE_PAD] = c_in_ref[chip]

        if n_late:
            start, forward, drain = _gather_steps([s.shape for s in late_shards], late_in, late_out,
                                                  late_scratch[:n_late], *late_scratch[n_late:])
            pl.when(i == 0)(start)
            pl.when(i == nt // 2)(forward)

        for r0 in range(0, tm, ts):
            rows = slice(r0, r0 + ts)
            xv = x_ref[rows, :]
            h = (xv * _rep(_inv_rms_mxu(xv), D_MODEL) * g_in_ref[...]).astype(BF16)
            lat = _dot(h, w_in_ref[:, 0:512])
            proj_ref[rows, 0:512] = lat
            c_q = lat[:, 0:Q_LORA]
            cqn = (c_q * _rep(_inv_rms_mxu(c_q), Q_LORA) * g_cq_ref[...]).astype(BF16)
            c_kv = lat[:, Q_LORA:Q_LORA + KV_LORA]
            ckvn = (c_kv * _inv_rms_mxu(c_kv) * g_ckv_ref[...]).astype(BF16)
            kpe = lat[:, 384:512]
            kpe_sq = kpe * kpe
            cos_b, sin_b = _rope_tables(pos_ref.at[:, rows], invf_ref, sgn_ref)
            gq_a, gq_b = gq_ref[:, 0:NOPE], gq_ref[:, NOPE:HEAD_PAD]
            gk_a, gk_b = gk_ref[:, 0:NOPE], gk_ref[:, NOPE:HEAD_PAD]

            def projections(rows=rows, h=h):
                for c0 in range(512, PROJ_EXT, 512):
                    proj_ref[rows, c0:c0 + 512] = _dot(h, w_in_ref[:, c0:c0 + 512])
                    yield

            def queries(hd, rows=rows, cqn=cqn, cos_b=cos_b, sin_b=sin_b, gq_a=gq_a, gq_b=gq_b):
                qh = _dot(cqn, w_uq_ref[hd])
                yield
                a, b = qh[:, 0:NOPE], qh[:, NOPE:HEAD_PAD]
                r = lax.rsqrt(_lane_sum(a * a + b * b) / QK_DIM + EPS)
                yield
                bn = b * r * gq_b
                q_ref[hd, rows, 0:NOPE] = (a * r * gq_a).astype(BF16)
                q_ref[hd, rows, NOPE:HEAD_PAD] = (bn * cos_b + _swap_rope_halves(bn) * sin_b).astype(BF16)
                yield

            def keys(hd, rows=rows, ckvn=ckvn, kpe=kpe, kpe_sq=kpe_sq, cos_b=cos_b, sin_b=sin_b, gk_a=gk_a, gk_b=gk_b):
                kvh = _dot(ckvn, w_ukv_ref[hd])
                yield
                ka = kvh[:, 0:NOPE]
                rk = lax.rsqrt(_lane_sum(ka * ka + kpe_sq) / QK_DIM + EPS)
                yield
                kbn = kpe * rk * gk_b
                k_ref[hd, rows, 0:NOPE] = (ka * rk * gk_a).astype(BF16)
                k_ref[hd, rows, NOPE:HEAD_PAD] = (kbn * cos_b + _swap_rope_halves(kbn) * sin_b).astype(BF16)
                v_ref[hd, rows, 0:V_DIM] = kvh[:, NOPE:HEAD_PAD].astype(BF16)
                v_ref[hd, rows, V_DIM:2 * V_DIM] = jnp.ones((ts, V_DIM), BF16)
                yield

            chains = [projections()]
            for hd in range(N_HEADS):
                chains += [queries(hd), keys(hd)]
            _round_robin(chains, 4)

        if n_late:
            pl.when(i == nt - 1)(drain)

    row = lambda i: (i, 0)
    head_rows = lambda i: (0, i, 0)
    outs = pl.pallas_call(
        body, name="fwd_proj", grid=(nt,),
        in_specs=[pl.BlockSpec((tm, D_MODEL), row), pl.BlockSpec((1, tm), lambda i: (0, i)), _full((1, D_MODEL)),
                  _full((N_CHIPS, D_MODEL, SHARD_COLS_IN)), _full((1, Q_LORA)), _full((N_HEADS, Q_LORA, HEAD_PAD)),
                  _full((1, KV_LORA)), _full((N_HEADS, KV_LORA, HEAD_PAD)), _full((1, HEAD_PAD)), _full((1, HEAD_PAD)),
                  _full((1, LANES)), _full((1, LANES))] + [_full(s.shape) for s in late_shards],
        out_specs=[pl.BlockSpec((tm, PROJ_EXT), row), pl.BlockSpec((N_HEADS, tm, HEAD_PAD), head_rows),
                   pl.BlockSpec((N_HEADS, tm, HEAD_PAD), head_rows), pl.BlockSpec((N_HEADS, tm, 2 * V_DIM), head_rows),
                   _full((D_MODEL, PROJ_EXT))] + [_ANY] * n_late,
        out_shape=[jax.ShapeDtypeStruct((T, PROJ_EXT), F32), jax.ShapeDtypeStruct((N_HEADS, T, HEAD_PAD), BF16),
                   jax.ShapeDtypeStruct((N_HEADS, T, HEAD_PAD), BF16), jax.ShapeDtypeStruct((N_HEADS, T, 2 * V_DIM), BF16),
                   jax.ShapeDtypeStruct((D_MODEL, PROJ_EXT), BF16)] + _gathered_shapes([s.shape for s in late_shards]),
        scratch_shapes=_gather_scratch([s.shape for s in late_shards]) if n_late else [],
        compiler_params=_params(dimension_semantics=("arbitrary",)),
    )(x, pos, g_in, c_in, g_cq, w_uq, g_ckv, w_ukv, gq, gk, invf, sgn, *late_shards)
    return outs[:5], outs[5:]


def _chunk_pipeline(n_loop, lag, matmuls, pointwise, accumulate, last):
    slots = lag + 1

    def iteration(t, slot, pending=True, ahead=True):
        if ahead:
            matmuls(jnp.minimum(t + lag, n_loop), (slot + lag) % slots)
        if pending:
            accumulate(t - lag, (slot + 1) % slots, False)
        pointwise(t, slot, False)

    def finish(slot, pending):
        for back in range(pending, 0, -1):
            accumulate(n_loop - back, (slot - back) % slots, False)
        pointwise(n_loop, slot, True)
        accumulate(n_loop, slot, True)
        last()

    for u in range(lag):
        matmuls(jnp.minimum(u, n_loop), u)
    for u in range(lag):
        pl.when(u < n_loop)(lambda u=u: iteration(u, u, pending=False))

    n_main = jnp.maximum(n_loop - lag, 0)

    def unrolled(tt, carry):
        for j in range(slots):
            iteration(lag + slots * tt + j, (lag + j) % slots)
        return carry

    lax.fori_loop(0, n_main // slots, unrolled, 0)
    rest = lax.rem(n_main, slots)
    t0 = n_loop - rest

    for r in range(slots):
        @pl.when(jnp.logical_and(n_loop >= lag, rest == r))
        def _():
            for j in range(r):
                iteration(t0 + j, (lag + j) % slots, ahead=j + lag <= r)
            finish((lag + r) % slots, lag)

    for short in range(lag):
        pl.when(n_loop == short)(lambda short=short: finish(short, short))


def _attn_fwd(q, k, v, tq):
    T = q.shape[1]
    tk = tq
    rc = min(SOFTMAX_ROWS, tq)

    def body(q_ref, k_ref, v_ref, o_ref, lse_ref, s0, s1, s2, p0, p1, p2, a0, a1, a2, m_ref, acc_ref):
        qi = pl.program_id(1)
        s_buf, p_buf, a_buf = (s0, s1, s2), (p0, p1, p2), (a0, a1, a2)

        def scores(t, slot):
            ks = pl.multiple_of(t * tk, tk)
            s_buf[slot][...] = _dot_nt(q_ref[0], k_ref[0, pl.ds(ks, tk), :])

        def blocks(masked):
            return ((0, tq // 2, tk // 2), (tq // 2, tq // 2, tk)) if masked else ((0, tq, tk),)

        def values(t, slot, masked):
            ks = pl.multiple_of(t * tk, tk)
            for q0, nq, nk in blocks(masked):
                rows = slice(q0, q0 + nq)
                acc_ref[rows, :] = (acc_ref[rows, :] * a_buf[slot][rows, :]
                                    + _dot(p_buf[slot][rows, 0:nk], v_ref[0, pl.ds(ks, nk), :]))

        def softmax(t, slot, masked):
            for q0, nq, nk in blocks(masked):
                rows = slice(q0, q0 + nq)
                s_all = s_buf[slot][rows, 0:nk]
                if masked:
                    row = lax.broadcasted_iota(jnp.int32, (nq, nk), 0) + q0
                    col = lax.broadcasted_iota(jnp.int32, (nq, nk), 1)
                    s_all = jnp.where(col <= row, s_all, NEG)
                    s_buf[slot][rows, 0:nk] = s_all
                m_old = m_ref[rows, :]
                m_new = jnp.maximum(m_old, jnp.max(s_all, axis=1, keepdims=True))
                a_buf[slot][rows, :] = jnp.exp2((m_old - m_new) * EXP2_SCALE)
                m_ref[rows, :] = m_new
                for r0 in range(0, nq, rc):
                    s = s_buf[slot][q0 + r0:q0 + r0 + rc, 0:nk]
                    p_buf[slot][q0 + r0:q0 + r0 + rc, 0:nk] = jnp.exp2((s - m_new[r0:r0 + rc, :]) * EXP2_SCALE).astype(BF16)

        def last():
            l = acc_ref[:, V_DIM:2 * V_DIM]
            o_ref[...] = acc_ref[:, 0:V_DIM] / l
            lse_ref[0] = (m_ref[...] * SCALE + jnp.log(l)).T[0:1, :]

        m_ref[...] = jnp.full_like(m_ref, NEG)
        acc_ref[...] = jnp.zeros_like(acc_ref)
        _chunk_pipeline(qi, 2, scores, softmax, values, last)

    return pl.pallas_call(
        body, name="attn_fwd", grid=(N_HEADS, T // tq),
        in_specs=[pl.BlockSpec((1, tq, HEAD_PAD), lambda h, i: (h, i, 0)),
                  pl.BlockSpec((1, T, HEAD_PAD), lambda h, i: (h, 0, 0)),
                  pl.BlockSpec((1, T, 2 * V_DIM), lambda h, i: (h, 0, 0))],
        out_specs=[pl.BlockSpec((tq, V_DIM), lambda h, i: (i, h)),
                   pl.BlockSpec((1, 1, tq), lambda h, i: (h, 0, i))],
        out_shape=[jax.ShapeDtypeStruct((T, ATTN_W), F32), jax.ShapeDtypeStruct((N_HEADS, 1, T), F32)],
        scratch_shapes=[pltpu.VMEM((tq, tk), F32)] * 3 + [pltpu.VMEM((tq, tk), BF16)] * 3
                       + [pltpu.VMEM((tq, 1), F32)] * 4 + [pltpu.VMEM((tq, 2 * V_DIM), F32)],
        compiler_params=_params(dimension_semantics=("arbitrary", "arbitrary")),
    )(q, k, v)


def _tail(x, o, proj, p, tgt, g_oa, g_oc, g_pl, conv_w, w_o, w_pl, w_plg, tm):
    T = x.shape[0]
    nt = T // tm

    def body(x_ref, o_ref, za_ref, cb_ref, cc_ref, cx_ref, zc_ref, cch_ref, cxh_ref, p_ref, tgt_ref,
             g_oa_ref, g_oc_ref, g_pl_ref, cw_ref, w_o_ref, w_pl_ref, w_plg_ref,
             dx1_ref, do_ref, delta_ref, dtail_ref, du_ref,
             dw_o_ref, dw_pl_ref, dw_plg_ref, dg_oa_ref, dg_oc_ref, dg_pl_ref, dcw_ref, loss_ref):
        i = pl.program_id(0)

        @pl.when(i == 0)
        def _():
            for r in (dw_o_ref, dw_pl_ref, dw_plg_ref, dg_oa_ref, dg_oc_ref, dg_pl_ref, dcw_ref, loss_ref):
                r[...] = jnp.zeros_like(r)

        g_oa, g_oc, g_pl = g_oa_ref[...], g_oc_ref[...], g_pl_ref[...]
        w0, w1, w2 = cw_ref[0:1, :], cw_ref[1:2, :], cw_ref[2:3, :]

        xv, ov, za, cb, zc = x_ref[...], o_ref[...], za_ref[...], cb_ref[...], zc_ref[...]
        pb = p_ref[...].astype(BF16)
        pp = _dot(pb, w_pl_ref[...])

        sa = _sigmoid(za)
        silu_a = za * sa
        ga = ov * silu_a
        ra = _inv_rms(ga, ATTN_W)
        xa = ga * ra
        ya = (xa * g_oa).astype(BF16)
        x1_a = _dot(ya, w_o_ref[0:ATTN_W, :])
        v = cc_ref[...] * cx_ref[...]
        not_first = jnp.where(i > 0, 1.0, 0.0)
        hv6 = cch_ref[6:7, :] * cxh_ref[6:7, :] * not_first
        hv7 = cch_ref[7:8, :] * cxh_ref[7:8, :] * not_first
        row = lax.broadcasted_iota(jnp.int32, v.shape, 0)
        v1 = jnp.where(row == 0, hv7, pltpu.roll(v, 1, 0))
        v2 = jnp.where(row == 0, hv6, jnp.where(row == 1, hv7, pltpu.roll(v, 2, 0)))
        u = w0 * v2 + w1 * v1 + w2 * v
        sc = _sigmoid(zc)
        silu_c = zc * sc
        gc = cb * u * silu_c
        rc = _inv_rms(gc, CONV_W)
        xc = gc * rc
        yc = (xc * g_oc).astype(BF16)
        x1 = xv + (x1_a + _dot(yc, w_o_ref[ATTN_W:D_MODEL, :]))
        r1 = _inv_rms(x1, D_MODEL)
        xh1 = x1 * r1
        n1 = (xh1 * g_pl).astype(BF16)
        gate = _sigmoid(_dot(n1, w_plg_ref[...]))
        err = x1 + gate * pp - tgt_ref[...]
        loss_ref[...] += 0.5 * jnp.sum(err * err) / D_MODEL
        dy = err / D_MODEL

        dpp = (dy * gate).astype(BF16)
        da = (dy * pp * gate * (1.0 - gate)).astype(BF16)
        dn1 = _dot_nt(da, w_plg_ref[...])
        dw_pl_ref[...] += _dot_tn(pb, dpp)
        dw_plg_ref[...] += _dot_tn(n1, da)
        dg_pl_ref[...] += _colsum(dn1 * xh1)
        dxh = dn1 * g_pl
        dx1 = dy + r1 * (dxh - xh1 * (jnp.sum(dxh * xh1, axis=-1, keepdims=True) / D_MODEL))
        dx1_ref[...] = dx1
        dx1b = dx1.astype(BF16)
        dya = _dot_nt(dx1b, w_o_ref[0:ATTN_W, :])
        dyc = _dot_nt(dx1b, w_o_ref[ATTN_W:D_MODEL, :])

        dw_o_ref[0:ATTN_W, :] += _dot_tn(ya, dx1b)
        dg_oa_ref[...] += _colsum(dya * xa)
        dxa = dya * g_oa
        dga = ra * (dxa - xa * (jnp.sum(dxa * xa, axis=-1, keepdims=True) / ATTN_W))
        do = (dga * silu_a).astype(BF16)
        do_ref[...] = do
        dof = do.astype(F32) * ov
        for hd in range(N_HEADS):
            delta_ref[hd] = _lane_sum(dof[:, hd * V_DIM:(hd + 1) * V_DIM]).T[0:1, :]
        dtail_ref[:, 0:512] = (dga * ov * (sa * (1.0 + za * (1.0 - sa)))).astype(BF16)

        dw_o_ref[ATTN_W:D_MODEL, :] += _dot_tn(yc, dx1b)
        dg_oc_ref[...] += _colsum(dyc * xc)
        dxc = dyc * g_oc
        dgc = rc * (dxc - xc * (jnp.sum(dxc * xc, axis=-1, keepdims=True) / CONV_W))
        dtail_ref[:, 512:1024] = (dgc * u * silu_c).astype(BF16)
        du = dgc * cb * silu_c
        du_ref[...] = du
        dtail_ref[:, 1024:1536] = (dgc * cb * u * (sc * (1.0 + zc * (1.0 - sc)))).astype(BF16)
        dcw_ref[0:1, :] += _colsum(du * v2)
        dcw_ref[1:2, :] += _colsum(du * v1)
        dcw_ref[2:3, :] += _colsum(du * v)

    row = lambda i: (i, 0)
    col = lambda c: (lambda i: (i, c))
    halo = lambda c: (lambda i: (jnp.maximum(i * (tm // 8) - 1, 0), c))
    in_specs = [pl.BlockSpec((tm, D_MODEL), row), pl.BlockSpec((tm, ATTN_W), row)]
    in_specs += [pl.BlockSpec((tm, 512), col(c)) for c in (1, 2, 3, 4, 5)]
    in_specs += [pl.BlockSpec((8, 512), halo(3)), pl.BlockSpec((8, 512), halo(4))]
    in_specs += [pl.BlockSpec((tm, PLE), row), pl.BlockSpec((tm, D_MODEL), row),
                 _full((1, ATTN_W)), _full((1, CONV_W)), _full((1, D_MODEL)), _full((3, CONV_W)),
                 _full((D_MODEL, D_MODEL)), _full((PLE, D_MODEL)), _full((D_MODEL, D_MODEL))]
    out_specs = [pl.BlockSpec((tm, D_MODEL), row), pl.BlockSpec((tm, ATTN_W), row),
                 pl.BlockSpec((N_HEADS, 1, tm), lambda i: (0, 0, i)), pl.BlockSpec((tm, 1536), row),
                 pl.BlockSpec((tm, CONV_W), row),
                 _full((D_MODEL, D_MODEL)), _full((PLE, D_MODEL)), _full((D_MODEL, D_MODEL)),
                 _full((1, ATTN_W)), _full((1, CONV_W)), _full((1, D_MODEL)), _full((3, CONV_W)), _full((1, LANES))]
    out_shape = [jax.ShapeDtypeStruct((T, D_MODEL), F32), jax.ShapeDtypeStruct((T, ATTN_W), BF16),
                 jax.ShapeDtypeStruct((N_HEADS, 1, T), F32), jax.ShapeDtypeStruct((T, 1536), BF16),
                 jax.ShapeDtypeStruct((T, CONV_W), F32),
                 jax.ShapeDtypeStruct((D_MODEL, D_MODEL), F32), jax.ShapeDtypeStruct((PLE, D_MODEL), F32),
                 jax.ShapeDtypeStruct((D_MODEL, D_MODEL), F32),
                 jax.ShapeDtypeStruct((1, ATTN_W), F32), jax.ShapeDtypeStruct((1, CONV_W), F32),
                 jax.ShapeDtypeStruct((1, D_MODEL), F32), jax.ShapeDtypeStruct((3, CONV_W), F32),
                 jax.ShapeDtypeStruct((1, LANES), F32)]
    return pl.pallas_call(
        body, name="tail", grid=(nt,), in_specs=in_specs, out_specs=out_specs, out_shape=out_shape,
        compiler_params=_params(dimension_semantics=("arbitrary",)),
    )(x, o, proj, proj, proj, proj, proj, proj, proj, p, tgt, g_oa, g_oc, g_pl, conv_w, w_o, w_pl, w_plg)


def _attn_bwd(q, k, v, do, lse_row, delta_row, tk, swap):
    T = q.shape[1]
    tq = tk
    nq = T // tq
    rc = min(SOFTMAX_ROWS, tk)
    hk, hq = tk // 2, tq // 2
    n_swap = len(swap)

    def body(q_ref, k_ref, v_ref, do_ref, lse_ref, dl_ref, *rest):
        swap_in, (dq_ref, dk_ref, dv_ref), rest = rest[:n_swap], rest[n_swap:n_swap + 3], rest[n_swap + 3:]
        swap_out, (s0, s1, d0, d1, p0, p1, g0, g1, dk_acc, dv_acc), sems = rest[:n_swap], rest[n_swap:n_swap + 10], rest[n_swap + 10:]
        kj = pl.program_id(1)
        s_buf, dp_buf, p_buf, g_buf = (s0, s1), (d0, d1), (p0, p1), (g0, g1)

        if n_swap:
            start, drain = _swap_steps(n_swap, swap_in, swap_out, *sems)
            pl.when(jnp.logical_and(pl.program_id(0) == 0, kj == 0))(start)

        @pl.when(kj == 0)
        def _():
            dq_ref[...] = jnp.zeros_like(dq_ref)

        def q_start(t):
            return pl.multiple_of((nq - 1 - t) * tq, tq)

        def matmuls(t, slot):
            qs = q_start(t)
            s_buf[slot][...] = _dot_nt(k_ref[0], q_ref[0, pl.ds(qs, tq), :])
            dp_buf[slot][...] = _dot_nt(v_ref[0], do_ref[pl.ds(qs, tq), :])

        def pointwise(t, slot, masked):
            qs = q_start(t)
            lse2 = lse_ref[0, :, pl.ds(qs, tq)] * LOG2E
            dl = dl_ref[0, :, pl.ds(qs, tq)]
            for r0 in range(0, tk, rc):
                c0 = r0 // hk * hq if masked else 0
                rows, cols = slice(r0, r0 + rc), slice(c0, tq)
                st = s_buf[slot][rows, cols]
                if masked:
                    row = lax.broadcasted_iota(jnp.int32, (rc, tq - c0), 0) + r0
                    col = lax.broadcasted_iota(jnp.int32, (rc, tq - c0), 1) + c0
                    st = jnp.where(row <= col, st, NEG)
                pt = jnp.exp2(st * EXP2_SCALE - lse2[:, cols])
                p_buf[slot][rows, cols] = pt.astype(BF16)
                g_buf[slot][rows, cols] = (pt * (dp_buf[slot][rows, cols] - dl[:, cols]) * SCALE).astype(BF16)

        def accumulate(t, slot, masked):
            qs = q_start(t)
            p, g = p_buf[slot], g_buf[slot]
            if not masked:
                dv_acc[...] += _dot(p[...], do_ref[pl.ds(qs, tq), :])
                dk_acc[...] += _dot(g[...], q_ref[0, pl.ds(qs, tq), :])
                dq_ref[0, pl.ds(qs, tq), :] += _dot_tn(g[...], k_ref[0])
                return
            q2 = pl.multiple_of(qs + hq, hq)
            dv_acc[0:hk, :] += _dot(p[0:hk, :], do_ref[pl.ds(qs, tq), :])
            dv_acc[hk:tk, :] += _dot(p[hk:tk, hq:tq], do_ref[pl.ds(q2, hq), :])
            dk_acc[0:hk, :] += _dot(g[0:hk, :], q_ref[0, pl.ds(qs, tq), :])
            dk_acc[hk:tk, :] += _dot(g[hk:tk, hq:tq], q_ref[0, pl.ds(q2, hq), :])
            dq_ref[0, pl.ds(qs, hq), :] += _dot_tn(g[0:hk, 0:hq], k_ref[0, 0:hk, :])
            dq_ref[0, pl.ds(q2, hq), :] += _dot_tn(g[:, hq:tq], k_ref[0])

        def last():
            dk_ref[0] = dk_acc[...]
            dv_ref[0] = dv_acc[...]

        dk_acc[...] = jnp.zeros_like(dk_acc)
        dv_acc[...] = jnp.zeros_like(dv_acc)
        _chunk_pipeline(nq - 1 - kj, 1, matmuls, pointwise, accumulate, last)

        if n_swap:
            pl.when(jnp.logical_and(pl.program_id(0) == N_HEADS - 1, kj == T // tk - 1))(drain)

    outs = pl.pallas_call(
        body, name="attn_bwd", grid=(N_HEADS, T // tk),
        in_specs=[pl.BlockSpec((1, T, HEAD_PAD), lambda h, j: (h, 0, 0)),
                  pl.BlockSpec((1, tk, HEAD_PAD), lambda h, j: (h, j, 0)),
                  pl.BlockSpec((1, tk, V_DIM), lambda h, j: (h, j, 0)),
                  pl.BlockSpec((T, V_DIM), lambda h, j: (0, h)),
                  pl.BlockSpec((1, 1, T), lambda h, j: (h, 0, 0)),
                  pl.BlockSpec((1, 1, T), lambda h, j: (h, 0, 0))] + [_ANY] * n_swap,
        out_specs=[pl.BlockSpec((1, T, HEAD_PAD), lambda h, j: (h, 0, 0)),
                   pl.BlockSpec((1, tk, HEAD_PAD), lambda h, j: (h, j, 0)),
                   pl.BlockSpec((1, tk, V_DIM), lambda h, j: (h, j, 0))] + [_ANY] * n_swap,
        out_shape=[jax.ShapeDtypeStruct((N_HEADS, T, HEAD_PAD), F32), jax.ShapeDtypeStruct((N_HEADS, T, HEAD_PAD), F32),
                   jax.ShapeDtypeStruct((N_HEADS, T, V_DIM), F32)] + _swapped_shapes(swap, []),
        scratch_shapes=[pltpu.VMEM((tk, tq), F32)] * 4 + [pltpu.VMEM((tk, tq), BF16)] * 4
                       + [pltpu.VMEM((tk, HEAD_PAD), F32), pltpu.VMEM((tk, V_DIM), F32)]
                       + ([pltpu.SemaphoreType.DMA((n_swap,))] * 2 if n_swap else []),
        compiler_params=_params(dimension_semantics=("arbitrary", "arbitrary")),
    )(q, k, v, do, lse_row, delta_row, *swap)
    return outs[:3], outs[3:]


def _bwd_proj(x, dx1, pos, proj, dq, dk, dv, dtail, du, g_in, w_in, g_cq, w_uq, g_ckv, w_ukv, gq, gk, conv_w,
              invf, sgn, tm):
    T = x.shape[0]
    nt = T // tm

    ts = min(SUB_TILE, tm)

    def body(x_ref, dx1_ref, pos_ref, lat_ref, cc_ref, cx_ref, dq_ref, dk_ref, dv_ref, dtail_ref, du_ref, dun_ref, *rest):
        consts, (gx_ref, h_ref, dproj_ref), sums = rest[:11], rest[11:14], rest[14:]
        cw_ref = consts[8]
        i = pl.program_id(0)

        @pl.when(i == 0)
        def _():
            for r in sums:
                r[...] = jnp.zeros_like(r)

        du_v = du_ref[...]
        not_last = jnp.where(i < nt - 1, 1.0, 0.0)
        nx0 = dun_ref[0:1, :] * not_last
        nx1 = dun_ref[1:2, :] * not_last
        row = lax.broadcasted_iota(jnp.int32, du_v.shape, 0)
        du1 = jnp.where(row == tm - 1, nx0, pltpu.roll(du_v, tm - 1, 0))
        du2 = jnp.where(row == tm - 2, nx0, jnp.where(row == tm - 1, nx1, pltpu.roll(du_v, tm - 2, 0)))
        dvc = cw_ref[2:3, :] * du_v + cw_ref[1:2, :] * du1 + cw_ref[0:1, :] * du2
        dproj_ref[:, 1536:2048] = (dvc * cx_ref[...]).astype(BF16)
        dproj_ref[:, 2048:2560] = (dvc * cc_ref[...]).astype(BF16)

        for r0 in range(0, tm, ts):
            rows = slice(r0, r0 + ts)
            work(x_ref.at[rows, :], dx1_ref.at[rows, :], pos_ref.at[:, rows], lat_ref.at[rows, :],
                 dq_ref.at[:, rows, :], dk_ref.at[:, rows, :], dv_ref.at[:, rows, :], dtail_ref.at[rows, :], *consts,
                 gx_ref.at[rows, :], h_ref.at[:, rows], dproj_ref.at[rows, :], *sums)

    def work(x_ref, dx1_ref, pos_ref, lat_ref, dq_ref, dk_ref, dv_ref, dtail_ref,
             g_in_ref, w_in_ref, g_cq_ref, w_uq_ref, g_ckv_ref, w_ukv_ref, gq_ref, gk_ref, cw_ref, invf_ref, sgn_ref,
             gx_ref, h_ref, dproj_ref, dw_uq_ref, dw_ukv_ref, dg_in_ref, dg_cq_ref, dg_ckv_ref, dgq_ref, dgk_ref):
        xv = x_ref[...]
        r0 = _rep(_inv_rms_mxu(xv), D_MODEL)
        xh0 = xv * r0
        g_in = g_in_ref[...]
        h_ref[...] = (xh0 * g_in).astype(BF16).T

        c_q = lat_ref[:, 0:Q_LORA]
        rq = _rep(_inv_rms_mxu(c_q), Q_LORA)
        xq = c_q * rq
        g_cq = g_cq_ref[...]
        cqn = (xq * g_cq).astype(BF16)
        c_kv = lat_ref[:, Q_LORA:Q_LORA + KV_LORA]
        rkv = _inv_rms_mxu(c_kv)
        xkv = c_kv * rkv
        g_ckv = g_ckv_ref[...]
        ckvn = (xkv * g_ckv).astype(BF16)
        kpe = lat_ref[:, 384:512]
        kpe_sq = kpe * kpe
        cos_b, sin_b = _rope_tables(pos_ref, invf_ref, sgn_ref)
        gq_a, gq_b = gq_ref[:, 0:NOPE], gq_ref[:, NOPE:HEAD_PAD]
        gk_a, gk_b = gk_ref[:, 0:NOPE], gk_ref[:, NOPE:HEAD_PAD]

        dproj_ref[:, 512:1536] = dtail_ref[:, 0:1024]
        dproj_ref[:, 2560:3072] = dtail_ref[:, 1024:1536]

        def dh_part(c0):
            return _dot_nt(dproj_ref[:, c0:c0 + 512], w_in_ref[:, c0:c0 + 512])

        later_chunks = ((512,), (1024,), (1536, 2048), (2560,))
        dh = jnp.zeros((ts, D_MODEL), F32)
        acc = dict(dh=dh, dkpe=jnp.zeros((ts, LANES), F32), dcqn=jnp.zeros((ts, Q_LORA), F32),
                   dckvn=jnp.zeros((ts, KV_LORA), F32))

        def dh_chunks():
            for chunks in later_chunks:
                for chunk in chunks:
                    acc["dh"] = acc["dh"] + dh_part(chunk)
                    yield

        def queries(hd):
            qh = _dot(cqn, w_uq_ref[hd])
            yield
            a, b = qh[:, 0:NOPE], qh[:, NOPE:HEAD_PAD]
            r = lax.rsqrt(_lane_sum(a * a + b * b) / QK_DIM + EPS)
            yield
            xa, xb = a * r, b * r
            dan = dq_ref[hd, :, 0:NOPE]
            dbr = dq_ref[hd, :, NOPE:HEAD_PAD]
            dbn = dbr * cos_b + _swap_rope_halves(dbr * sin_b)
            yield
            dgq_ref[:, 0:NOPE] += _colsum(dan * xa)
            dgq_ref[:, NOPE:HEAD_PAD] += _colsum(dbn * xb)
            dxa, dxb = dan * gq_a, dbn * gq_b
            cq = _lane_sum(dxa * xa + dxb * xb) / QK_DIM
            yield
            dqh = jnp.concatenate([r * (dxa - xa * cq), r * (dxb - xb * cq)], axis=-1).astype(BF16)
            yield
            dw_uq_ref[hd] += _dot_tn(cqn, dqh)
            yield
            acc["dcqn"] = acc["dcqn"] + _dot_nt(dqh, w_uq_ref[hd])
            yield

        def keys(hd):
            kvh = _dot(ckvn, w_ukv_ref[hd])
            yield
            ka = kvh[:, 0:NOPE]
            rk = lax.rsqrt(_lane_sum(ka * ka + kpe_sq) / QK_DIM + EPS)
            yield
            xka, xkb = ka * rk, kpe * rk
            dkan = dk_ref[hd, :, 0:NOPE]
            dkbr = dk_ref[hd, :, NOPE:HEAD_PAD]
            dkbn = dkbr * cos_b + _swap_rope_halves(dkbr * sin_b)
            yield
            dgk_ref[:, 0:NOPE] += _colsum(dkan * xka)
            dgk_ref[:, NOPE:HEAD_PAD] += _colsum(dkbn * xkb)
            dxka, dxkb = dkan * gk_a, dkbn * gk_b
            ck = _lane_sum(dxka * xka + dxkb * xkb) / QK_DIM
            yield
            acc["dkpe"] = acc["dkpe"] + rk * (dxkb - xkb * ck)
            dkvh = jnp.concatenate([rk * (dxka - xka * ck), dv_ref[hd]], axis=-1).astype(BF16)
            yield
            dw_ukv_ref[hd] += _dot_tn(ckvn, dkvh)
            yield
            acc["dckvn"] = acc["dckvn"] + _dot_nt(dkvh, w_ukv_ref[hd])
            yield

        chains = [dh_chunks()]
        for hd in range(N_HEADS):
            chains += [queries(hd), keys(hd)]
        _round_robin(chains, 5)
        dh, dkpe, dcqn, dckvn = acc["dh"], acc["dkpe"], acc["dcqn"], acc["dckvn"]

        dg_cq_ref[...] += _colsum(dcqn * xq)
        dxq = dcqn * g_cq
        dproj_ref[:, 0:Q_LORA] = (rq * (dxq - xq * _rep(_lane_sum(dxq * xq) / Q_LORA, Q_LORA))).astype(BF16)
        dg_ckv_ref[...] += _colsum(dckvn * xkv)
        dxkv = dckvn * g_ckv
        dproj_ref[:, 256:384] = (rkv * (dxkv - xkv * (_lane_sum(dxkv * xkv) / KV_LORA))).astype(BF16)
        dproj_ref[:, 384:512] = dkpe.astype(BF16)
        dh = dh + dh_part(0)
        dg_in_ref[...] += _colsum(dh * xh0)
        dxh = dh * g_in
        gx_ref[...] = dx1_ref[...] + r0 * (dxh - xh0 * _rep(_lane_sum(dxh * xh0) / D_MODEL, D_MODEL))

    row = lambda i: (i, 0)
    col = lambda c: (lambda i: (i, c))
    head_rows = lambda i: (0, i, 0)
    nxt = lambda i: (jnp.minimum((i + 1) * (tm // 8), T // 8 - 1), 0)
    in_specs = [pl.BlockSpec((tm, D_MODEL), row), pl.BlockSpec((tm, D_MODEL), row), pl.BlockSpec((1, tm), lambda i: (0, i)),
                pl.BlockSpec((tm, 512), col(0)), pl.BlockSpec((tm, 512), col(3)), pl.BlockSpec((tm, 512), col(4)),
                pl.BlockSpec((N_HEADS, tm, HEAD_PAD), head_rows), pl.BlockSpec((N_HEADS, tm, HEAD_PAD), head_rows),
                pl.BlockSpec((N_HEADS, tm, V_DIM), head_rows), pl.BlockSpec((tm, 1536), row),
                pl.BlockSpec((tm, CONV_W), row), pl.BlockSpec((8, CONV_W), nxt),
                _full((1, D_MODEL)), _full((D_MODEL, PROJ_EXT)), _full((1, Q_LORA)), _full((N_HEADS, Q_LORA, HEAD_PAD)),
                _full((1, KV_LORA)), _full((N_HEADS, KV_LORA, HEAD_PAD)), _full((1, HEAD_PAD)), _full((1, HEAD_PAD)),
                _full((3, CONV_W)), _full((1, LANES)), _full((1, LANES))]
    out_specs = [pl.BlockSpec((tm, D_MODEL), row), pl.BlockSpec((D_MODEL, tm), lambda i: (0, i)),
                 pl.BlockSpec((tm, PROJ_EXT), row),
                 _full((N_HEADS, Q_LORA, HEAD_PAD)), _full((N_HEADS, KV_LORA, HEAD_PAD)),
                 _full((1, D_MODEL)), _full((1, Q_LORA)), _full((1, KV_LORA)), _full((1, HEAD_PAD)), _full((1, HEAD_PAD))]
    out_shape = [jax.ShapeDtypeStruct((T, D_MODEL), F32), jax.ShapeDtypeStruct((D_MODEL, T), BF16),
                 jax.ShapeDtypeStruct((T, PROJ_EXT), BF16),
                 jax.ShapeDtypeStruct((N_HEADS, Q_LORA, HEAD_PAD), F32), jax.ShapeDtypeStruct((N_HEADS, KV_LORA, HEAD_PAD), F32),
                 jax.ShapeDtypeStruct((1, D_MODEL), F32), jax.ShapeDtypeStruct((1, Q_LORA), F32),
                 jax.ShapeDtypeStruct((1, KV_LORA), F32), jax.ShapeDtypeStruct((1, HEAD_PAD), F32),
                 jax.ShapeDtypeStruct((1, HEAD_PAD), F32)]
    return pl.pallas_call(
        body, name="bwd_proj", grid=(nt,), in_specs=in_specs, out_specs=out_specs, out_shape=out_shape,
        compiler_params=_params(dimension_semantics=("arbitrary",)),
    )(x, dx1, pos, proj, proj, proj, dq, dk, dv, dtail, du, du, g_in, w_in, g_cq, w_uq, g_ckv, w_ukv, gq, gk, conv_w,
      invf, sgn)


def _matmul_acc(a, b, tt, tn, parts):
    M, T = a.shape
    N = b.shape[1]
    n = len(parts)
    grid = (N // tn, T // tt)
    hm = M // 2

    def body(a_ref, b_ref, *rest):
        part_refs, (o_ref, sib_ref), rest = rest[:n], rest[n:n + 2], rest[n + 2:]
        out_refs, (stage_ref, tile_send, tile_recv), sems = rest[:n], rest[n:n + 3], rest[n + 3:]
        j, t = pl.program_id(0), pl.program_id(1)
        if n:
            start, drain = _scatter_steps(part_refs, out_refs, *sems)
            pl.when(jnp.logical_and(j == 0, t == 0))(start)

        def to_sibling(jj):
            x, y, c = _mesh_pos()
            return _remote(stage_ref, sib_ref.at[:, pl.ds(pl.multiple_of(jj * tn, tn), tn)],
                           tile_send, tile_recv, jj, (x, y, 1 - c))

        @pl.when(t == 0)
        def _():
            o_ref[...] = jnp.zeros_like(o_ref)

        o_ref[...] += _dot(a_ref[...], b_ref[...])

        tile_done = t == grid[1] - 1
        pl.when(jnp.logical_and(tile_done, j > 0))(lambda: to_sibling(j - 1).wait())

        @pl.when(tile_done)
        def _():
            c = lax.axis_index("c")
            stage_ref[...] = o_ref[pl.ds(pl.multiple_of((1 - c) * hm, hm), hm), :]
            to_sibling(j).start()

        pl.when(jnp.logical_and(tile_done, j == grid[0] - 1))(lambda: to_sibling(j).wait())
        if n:
            pl.when(jnp.logical_and(j == grid[0] - 1, t == grid[1] - 1))(drain)

    sems = [pltpu.SemaphoreType.DMA((3 * n,)), pltpu.SemaphoreType.DMA((3 * n,)), pltpu.SemaphoreType.DMA((n,))]
    outs = pl.pallas_call(
        body, name="dw_in", grid=grid,
        in_specs=[pl.BlockSpec((M, tt), lambda j, t: (0, t)), pl.BlockSpec((tt, tn), lambda j, t: (t, j))] + [_ANY] * n,
        out_specs=[pl.BlockSpec((M, tn), lambda j, t: (0, j)), _ANY] + [_ANY] * n,
        out_shape=[jax.ShapeDtypeStruct((M, N), F32), jax.ShapeDtypeStruct((hm, N), F32)] + _scattered_shapes(parts),
        scratch_shapes=[pltpu.VMEM((hm, tn), F32)] + [pltpu.SemaphoreType.DMA((grid[0],))] * 2 + (sems if n else []),
        compiler_params=_params(dimension_semantics=("arbitrary", "arbitrary")),
    )(a, b, *parts)
    return outs[0], outs[1], outs[2:]


def _add_chips(parts, small_parts):
    arrays = list(parts) + [small_parts]

    def body(*refs):
        ins, outs = refs[:len(arrays)], refs[len(arrays):]
        for a_ref, o_ref in zip(ins, outs):
            part = lambda k: a_ref[k].astype(F32)
            o_ref[...] = ((part(0) + part(1)) + part(2)) + part(3)

    in_specs, out_specs, out_shape = [], [], []
    for a in arrays:
        _, rows, cols = a.shape
        in_specs.append(pl.BlockSpec((N_CHIPS, rows // 2, cols), lambda i: (0, i, 0)))
        out_specs.append(pl.BlockSpec((rows // 2, cols), lambda i: (i, 0)))
        out_shape.append(jax.ShapeDtypeStruct((rows, cols), F32))
    outs = pl.pallas_call(body, name="add_chips", grid=(2,), in_specs=in_specs, out_specs=out_specs,
                          out_shape=out_shape, compiler_params=_params(dimension_semantics=("arbitrary",)))(*arrays)
    return outs[:-1], outs[-1]


def _adamw_small(ws, gs, ms, vs):
    n = len(ws)

    def body(*refs):
        for i in range(n):
            w_ref, g_ref, m_ref, v_ref = (refs[k * n + i] for k in range(4))
            d_ref, nm_ref, nv_ref = (refs[(4 + k) * n + i] for k in range(3))
            _adamw_math(g_ref[...], w_ref, m_ref, v_ref, d_ref, nm_ref, nv_ref)

    shapes = [jax.ShapeDtypeStruct(w.shape, F32) for w in ws]
    outs = pl.pallas_call(body, name="adamw_small", out_shape=shapes * 3)(*ws, *gs, *ms, *vs)
    return outs[:n], outs[n:2 * n], outs[2 * n:]


def _adamw_math(gv, w_ref, m_ref, v_ref, d_ref, nm_ref, nv_ref):
    nm = B1 * m_ref[...] + (1.0 - B1) * gv
    nv = B2 * v_ref[...] + (1.0 - B2) * (gv * gv)
    m_hat = nm / (1.0 - B1 ** STEP)
    v_hat = nv / (1.0 - B2 ** STEP)
    d_ref[...] = -LR * (m_hat / (jnp.sqrt(v_hat) + ADAM_EPS) + WD * w_ref[...])
    nm_ref[...] = nm
    nv_ref[...] = nv


def _adamw_halves(w, mine, other, m, v, c, name, transposed):
    hr, cols = mine.shape

    def body(c_ref, w_ref, mine_ref, other_ref, m_ref, v_ref, g_ref, d_ref, nm_ref, nv_ref, *picked):
        gv = jnp.where(pl.program_id(0) == c_ref[0], mine_ref[...], other_ref[...])
        if transposed:
            picked[0][...] = gv
            _store_transposed(picked[0], g_ref)
            gv = g_ref[...]
        else:
            g_ref[...] = gv
        _adamw_math(gv, w_ref, m_ref, v_ref, d_ref, nm_ref, nv_ref)

    if transposed:
        half = pl.BlockSpec((cols, hr), lambda i, c_ref: (0, i))
    else:
        half = pl.BlockSpec((hr, cols), lambda i, c_ref: (i, 0))
    whole = pl.BlockSpec((hr, cols), lambda i, c_ref: (0, 0))
    shp = jax.ShapeDtypeStruct(w.shape, F32)
    return pl.pallas_call(
        body, name=name, out_shape=[shp] * 4,
        grid_spec=pltpu.PrefetchScalarGridSpec(num_scalar_prefetch=1, grid=(2,), in_specs=[half, whole, whole, half, half],
                                               out_specs=[half] * 4,
                                               scratch_shapes=[pltpu.VMEM((hr, cols), F32)] if transposed else []),
        compiler_params=_params(dimension_semantics=("arbitrary",)),
    )(c.reshape(1), w, mine, other, m, v)


_ANY = pl.BlockSpec(memory_space=pl.ANY)


def _mesh_pos():
    return lax.axis_index("x"), lax.axis_index("y"), lax.axis_index("c")


def _other_chips(x, y):
    return [(1 - x, y), (x, 1 - y), (1 - x, 1 - y)]


def _remote(src, dst, send_sems, recv_sems, k, to):
    return pltpu.make_async_remote_copy(src_ref=src, dst_ref=dst, send_sem=send_sems.at[k], recv_sem=recv_sems.at[k],
                                        device_id=to, device_id_type=MESH)


def _gather_weights(shards, n_transposed):
    n = len(shards)
    shapes = [s.shape[::-1] if i < n_transposed else s.shape for i, s in enumerate(shards)]

    def body(*refs):
        start, forward, drain = _gather_steps(shapes, refs[:n], refs[n:2 * n], refs[2 * n:3 * n], *refs[3 * n:])
        start()
        forward()
        drain()

    vmem = pl.BlockSpec(memory_space=pltpu.VMEM)
    return pl.pallas_call(
        body, name="gather_weights", in_specs=[vmem] * n, out_specs=[_ANY] * n,
        out_shape=_gathered_shapes(shapes), scratch_shapes=_gather_scratch(shapes), compiler_params=_params(),
    )(*shards)


def _travel_shape(shape):
    rows, cols = shape
    return (rows, HEAD_PAD if cols == QK_DIM else cols)


def _gathered_shapes(shapes):
    return [jax.ShapeDtypeStruct((N_CHIPS,) + _travel_shape(s), BF16) for s in shapes]


def _gather_scratch(shapes):
    n = len(shapes)
    return ([pltpu.VMEM(_travel_shape(s), BF16) for s in shapes]
            + [pltpu.SemaphoreType.DMA((6 * n,)), pltpu.SemaphoreType.DMA((6 * n,)), pltpu.SemaphoreType.DMA((n,))])


def _gather_steps(shapes, ins, outs, stage, send_sems, recv_sems, local_sems):
    n = len(shapes)
    halved = [s[0] % 32 == 0 for s in shapes]

    def part(i, ref, hc):
        if not halved[i]:
            return ref
        hr = shapes[i][0] // 2
        return ref.at[pl.ds(hc * hr, hr), :]

    def to_chip(i, j, x, y, c):
        cx, cy = _other_chips(x, y)[j]
        return _remote(part(i, stage[i], c), part(i, outs[i].at[2 * x + y], c), send_sems, recv_sems, 6 * i + j, (cx, cy, c))

    def to_sibling(i, j, x, y, c):
        cx, cy = _other_chips(x, y)[j]
        got = part(i, outs[i].at[2 * cx + cy], c)
        return _remote(got, got, send_sems, recv_sems, 6 * i + 3 + j, (x, y, 1 - c))

    def local(i, x, y):
        return pltpu.make_async_copy(stage[i], outs[i].at[2 * x + y], local_sems.at[i])

    def start():
        x, y, c = _mesh_pos()
        for i in range(n):
            cols = shapes[i][1]
            if stage[i].shape[1] != cols:
                stage[i][...] = jnp.zeros_like(stage[i])
            if ins[i].shape == shapes[i]:
                stage[i][:, 0:cols] = ins[i][...].astype(BF16)
            else:
                _store_transposed(ins[i], stage[i])
            local(i, x, y).start()
            for j in range(3):
                to_chip(i, j, x, y, c).start()

    def forward():
        x, y, c = _mesh_pos()
        for i in range(n):
            for j, (cx, cy) in enumerate(_other_chips(x, y)):
                got = part(i, outs[i].at[2 * cx + cy], c)
                _remote(got, got, send_sems, recv_sems, 6 * i + j, (cx, cy, c)).wait_recv()
                if halved[i]:
                    to_sibling(i, j, x, y, c).start()

    def drain():
        x, y, c = _mesh_pos()
        for i in range(n):
            for j, (cx, cy) in enumerate(_other_chips(x, y)):
                if halved[i]:
                    got = part(i, outs[i].at[2 * cx + cy], 1 - c)
                    _remote(got, got, send_sems, recv_sems, 6 * i + 3 + j, (x, y, 1 - c)).wait_recv()
                    to_sibling(i, j, x, y, c).wait_send()
                to_chip(i, j, x, y, c).wait_send()
            local(i, x, y).wait()

    return start, forward, drain


def _swap_halves(grads, whole, name):
    n, m = len(grads), len(grads) + len(whole)

    def body(*refs):
        start, drain = _swap_steps(n, refs[:m], refs[m:2 * m], refs[2 * m], refs[2 * m + 1])
        start()
        drain()

    outs = pl.pallas_call(
        body, name=name, in_specs=[_ANY] * m, out_specs=[_ANY] * m, out_shape=_swapped_shapes(grads, whole),
        scratch_shapes=[pltpu.SemaphoreType.DMA((m,)), pltpu.SemaphoreType.DMA((m,))],
    )(*grads, *whole)
    return outs[:n], outs[n:]


def _swapped_shapes(grads, whole):
    return ([jax.ShapeDtypeStruct((g.shape[0], g.shape[1] // 2, g.shape[2]), F32) for g in grads]
            + [jax.ShapeDtypeStruct(w.shape, F32) for w in whole])


def _swap_steps(n, ins, outs, send_sems, recv_sems):
    def copies():
        x, y, c = _mesh_pos()
        cps = []
        for i, src in enumerate(ins):
            if i < n:
                hr = src.shape[1] // 2
                src = src.at[:, pl.ds((1 - c) * hr, hr), :]
            cps.append(_remote(src, outs[i], send_sems, recv_sems, i, (x, y, 1 - c)))
        return cps

    def start():
        for cp in copies():
            cp.start()

    def drain():
        for cp in copies():
            cp.wait()

    return start, drain


def _scattered_shapes(parts):
    return [jax.ShapeDtypeStruct(p.shape if p.ndim == 3 else (N_CHIPS,) + p.shape, p.dtype) for p in parts]


def _scatter_steps(ins, outs, send_sems, recv_sems, local_sems):
    n = len(ins)

    def src(i, k):
        return ins[i].at[k] if len(ins[i].shape) == 3 else ins[i]

    def sends(x, y, c):
        return [_remote(src(i, 2 * cx + cy), outs[i].at[2 * x + y], send_sems, recv_sems, 3 * i + j, (cx, cy, c))
                for i in range(n) for j, (cx, cy) in enumerate(_other_chips(x, y))]

    def local(i, x, y):
        return pltpu.make_async_copy(src(i, 2 * x + y), outs[i].at[2 * x + y], local_sems.at[i])

    def start():
        x, y, c = _mesh_pos()
        for i in range(n):
            local(i, x, y).start()
        for cp in sends(x, y, c):
            cp.start()

    def drain():
        x, y, c = _mesh_pos()
        for i in range(n):
            for j, (cx, cy) in enumerate(_other_chips(x, y)):
                got = outs[i].at[2 * cx + cy]
                _remote(got, got, send_sems, recv_sems, 3 * i + j, (cx, cy, c)).wait_recv()
        for cp in sends(x, y, c):
            cp.wait_send()
        for i in range(n):
            local(i, x, y).wait()

    return start, drain


def _add_pair(grads, from_sibling, small, small_sibling, c):
    n = len(grads)

    def body(c_ref, *refs):
        ins, outs = refs[:2 * n + 2], refs[2 * n + 2:]
        for i in range(n + 1):
            outs[i][...] = (ins[2 * i][...] + ins[2 * i + 1][...]).astype(outs[i].dtype)

    in_specs, out_specs, out_shape, args = [], [], [], []
    for g, r in zip(grads, from_sibling):
        _, hr, cols = r.shape
        in_specs += [pl.BlockSpec((1, hr, cols), lambda k, c_ref: (k, c_ref[0], 0)),
                     pl.BlockSpec((1, hr, cols), lambda k, c_ref: (k, 0, 0))]
        out_specs.append(pl.BlockSpec((1, hr, cols), lambda k, c_ref: (k, 0, 0)))
        out_shape.append(jax.ShapeDtypeStruct(r.shape, BF16))
        args += [g, r]
    whole = pl.BlockSpec(small.shape, lambda k, c_ref: (0, 0))
    in_specs += [whole, whole]
    out_specs.append(whole)
    out_shape.append(jax.ShapeDtypeStruct(small.shape, F32))
    outs = pl.pallas_call(
        body, name="add_pair", out_shape=out_shape,
        grid_spec=pltpu.PrefetchScalarGridSpec(num_scalar_prefetch=1, grid=(N_CHIPS,), in_specs=in_specs,
                                               out_specs=out_specs),
        compiler_params=_params(dimension_semantics=("arbitrary",)),
    )(c.reshape(1), *args, small, small_sibling)
    return outs[:n], outs[n]


def _scatter_w_in(dw_in_e, from_sibling):
    hr = from_sibling.shape[1]
    shard = (N_CHIPS, hr, SHARD_COLS_IN)

    def body(g_in, r_in, out, g_buf, r_buf, p_buf, load_sems, send_sems, recv_sems, local_sems):
        c = lax.axis_index("c")
        loads = (pltpu.make_async_copy(g_in.at[0, pl.ds(c * hr, hr), :], g_buf, load_sems.at[0]),
                 pltpu.make_async_copy(r_in.at[0], r_buf, load_sems.at[1]))
        for cp in loads:
            cp.start()
        for cp in loads:
            cp.wait()
        g_buf[...] += r_buf[...]
        p_buf[0, :, 0:KPE_END] = g_buf[:, 0:KPE_END].astype(BF16)
        p_buf[0, :, KPE_END:SHARD_COLS_IN] = g_buf[:, KPE_END + KPE_PAD:SHARD_COLS_IN + KPE_PAD].astype(BF16)
        for k in range(1, N_CHIPS):
            p_buf[k] = g_buf[:, SHARD_COLS_IN * k + KPE_PAD:SHARD_COLS_IN * (k + 1) + KPE_PAD].astype(BF16)
        start, drain = _scatter_steps([p_buf], [out], send_sems, recv_sems, local_sems)
        start()
        drain()

    return pl.pallas_call(
        body, name="scatter_grads", in_specs=[_ANY] * 2, out_specs=_ANY, out_shape=jax.ShapeDtypeStruct(shard, BF16),
        scratch_shapes=[pltpu.VMEM((hr, PROJ_EXT), F32)] * 2 + [pltpu.VMEM(shard, BF16)]
                       + [pltpu.SemaphoreType.DMA((2,)), pltpu.SemaphoreType.DMA((3,)), pltpu.SemaphoreType.DMA((3,)),
                          pltpu.SemaphoreType.DMA((1,))],
        compiler_params=_params(),
    )(dw_in_e, from_sibling)


def _share_halves(halves):
    n = len(halves)

    def body(*refs):
        ins, outs, send_sems, recv_sems = refs[:n], refs[n:2 * n], refs[2 * n], refs[2 * n + 1]
        x, y, c = _mesh_pos()
        cps = [_remote(ins[i], outs[i], send_sems, recv_sems, i, (x, y, 1 - c)) for i in range(n)]
        for cp in cps:
            cp.start()
        for cp in cps:
            cp.wait()

    return pl.pallas_call(
        body, name="share_halves", in_specs=[_ANY] * n, out_specs=[_ANY] * n,
        out_shape=[jax.ShapeDtypeStruct(h.shape, h.dtype) for h in halves],
        scratch_shapes=[pltpu.SemaphoreType.DMA((n,)), pltpu.SemaphoreType.DMA((n,))],
    )(*halves)


SHARD_COLS_IN = IN_TOTAL // N_CHIPS
KPE_END = Q_LORA + KV_LORA + ROPE
KPE_PAD = PROJ_EXT - IN_TOTAL


def _by_cols(a):
    return a.transpose(1, 0, 2).reshape(a.shape[1], N_CHIPS * a.shape[2])


def _assemble_early(c_in, c_uq, c_ukv, c_conv):
    return c_in, c_uq, c_ukv, _by_cols(c_conv).astype(F32)


def _assemble_late(c_o, c_pl, c_plg):
    return c_o.reshape(D_MODEL, D_MODEL), _by_cols(c_pl), c_plg.reshape(D_MODEL, D_MODEL)


def _split_late(dw_o, dw_pl, dw_plg):
    chip_major = lambda a: a.reshape(a.shape[0], N_CHIPS, a.shape[1] // N_CHIPS).transpose(1, 0, 2)
    return [dw_o.reshape(N_CHIPS, D_MODEL // N_CHIPS, D_MODEL), chip_major(dw_pl),
            dw_plg.reshape(N_CHIPS, D_MODEL // N_CHIPS, D_MODEL)]


def _local_step(x, p, pos, tgt, gains, early, late_shards, late_gathered, tm, tq):
    c_in, w_uq_e, w_ukv, conv_w = early
    g_in, g_cq, g_ckv, g_q, g_k, g_oa, g_oc, g_pl = gains
    T = x.shape[0]
    zpad = lambda a, n: jnp.concatenate([a, jnp.zeros(a.shape[:-1] + (n,), a.dtype)], axis=-1)
    gq, gk = zpad(g_q, HEAD_PAD - QK_DIM), zpad(g_k, HEAD_PAD - QK_DIM)
    inv_freq = 1.0 / (ROPE_THETA ** (jnp.arange(0, ROPE, 2, dtype=F32) / ROPE))
    invf = jnp.concatenate([inv_freq, inv_freq, jnp.zeros((64,), F32)]).reshape(1, LANES)
    sgn = jnp.concatenate([-jnp.ones((32,), F32), jnp.ones((32,), F32), jnp.zeros((64,), F32)]).reshape(1, LANES)

    (proj, q, k, v, w_in_e), gathered = _fwd_proj(x, pos, g_in, c_in, g_cq, w_uq_e, g_ckv, w_ukv, gq, gk, invf, sgn,
                                                  late_shards, min(2 * tm, T))
    w_o, w_pl, w_plg = _assemble_late(*(gathered if late_shards else late_gathered))
    o, lse = _attn_fwd(q, k, v, tq)
    (dx1, do, delta, dtail, du, dw_o, dw_pl, dw_plg, dg_oa, dg_oc, dg_pl, dconv, loss) = _tail(
        x, o, proj, p, tgt, g_oa, g_oc, g_pl, conv_w, w_o, w_pl, w_plg, tm)
    late_grads = _split_late(dw_o, dw_pl, dw_plg)
    (dq, dk, dv), late_sibling = _attn_bwd(q, k, v, do, lse, delta, tq, late_grads)
    (gx, h, dproj, dw_uq_e, dw_ukv, dg_in, dg_cq, dg_ckv, dgq, dgk) = _bwd_proj(
        x, dx1, pos, proj, dq, dk, dv, dtail, du, g_in, w_in_e, g_cq, w_uq_e, g_ckv, w_ukv, gq, gk, conv_w, invf, sgn, tm)
    wgrads = [dw_uq_e[:, :, :QK_DIM], dw_ukv, *late_grads]
    ggrads = (dg_in, dg_cq, dg_ckv, dgq, dgk, dg_oa, dg_oc, dg_pl)
    return loss, gx, (h, dproj), wgrads, late_sibling, ggrads, dconv


def kernel(x, p, positions, g_in, w_in, g_cq, w_uq, g_ckv, w_ukv, g_q, g_k, conv_w, g_oa, g_oc, w_o, w_pl, w_plg, g_pl, loss_target, m_g_in, m_w_in, m_g_cq, m_w_uq, m_g_ckv, m_w_ukv, m_g_q, m_g_k, m_conv_w, m_g_oa, m_g_oc, m_w_o, m_w_pl, m_w_plg, m_g_pl, v_g_in, v_w_in, v_g_cq, v_w_uq, v_g_ckv, v_w_ukv, v_g_q, v_g_k, v_conv_w, v_g_oa, v_g_oc, v_w_o, v_w_pl, v_w_plg, v_g_pl):
    T = x.shape[1]
    c = lax.axis_index("c")
    chip = 2 * lax.axis_index("x") + lax.axis_index("y")
    gains = [g.reshape(1, -1) for g in (g_in, g_cq, g_ckv, g_q, g_k, g_oa, g_oc, g_pl)]

    transposed = ("w_in", "w_uq")
    early = _assemble_early(*_gather_weights([w_in[0].T, w_uq[0].T, w_ukv[0], conv_w[0]], len(transposed)))

    loss, gx, (h_t, dproj), others_cm, late_sibling, ggrads, dconv = _local_step(
        x[0], p[0, 0], positions.reshape(1, T), loss_target[0], gains, early, [w_o[0], w_pl[0], w_plg[0]], None, 256, 512)

    small_parts = [a.reshape(-1, LANES) for a in (*ggrads, loss, dconv)]
    small_rows = [a.shape[0] for a in small_parts]
    tile_rows = [-(-r // 8) * 8 for r in small_rows]
    tile_rows[-1] += -sum(tile_rows) % 16
    small = jnp.concatenate([jnp.pad(a, ((0, t - r), (0, 0))) for a, r, t in zip(small_parts, small_rows, tile_rows)])
    n_early = len(others_cm) - len(late_sibling)
    early_sibling, (small_sibling,) = _swap_halves(others_cm[:n_early], [small], "pair_grads")
    chip_parts, chip_small = _add_pair(others_cm, [*early_sibling, *late_sibling], small, small_sibling, c)
    dw_in_e, w_in_sibling, exchanged = _matmul_acc(h_t, dproj, min(4096, T), 1024, [*chip_parts, chip_small])
    by_chip = [_scatter_w_in(dw_in_e[None], w_in_sibling[None]), *exchanged[:-1]]
    halves, small_total = _add_chips(by_chip, exchanged[-1])
    other_halves = _share_halves(halves)

    gg, off = [], 0
    for rows, tiled in zip(small_rows, tile_rows):
        gg.append(small_total[off:off + rows].reshape(1, -1))
        off += tiled
    loss_out = gg[8][0, 0]
    conv_total = gg[9].reshape(3, CONV_W)
    conv_g = lax.dynamic_slice(conv_total, (0, chip * (CONV_W // N_CHIPS)), (3, CONV_W // N_CHIPS))
    g_by_name = dict(g_in=gg[0], g_cq=gg[1], g_ckv=gg[2], g_q=gg[3][:, :QK_DIM], g_k=gg[4][:, :QK_DIM], conv_w=conv_g,
                     g_oa=gg[5], g_oc=gg[6], g_pl=gg[7])
    half_by_name = dict(zip(("w_in", "w_uq", "w_ukv", "w_o", "w_pl", "w_plg"), zip(halves, other_halves)))
    weights = dict(g_in=g_in, w_in=w_in, g_cq=g_cq, w_uq=w_uq, g_ckv=g_ckv, w_ukv=w_ukv, g_q=g_q, g_k=g_k,
                   conv_w=conv_w, g_oa=g_oa, g_oc=g_oc, w_o=w_o, w_pl=w_pl, w_plg=w_plg, g_pl=g_pl)
    ms = dict(g_in=m_g_in, w_in=m_w_in, g_cq=m_g_cq, w_uq=m_w_uq, g_ckv=m_g_ckv, w_ukv=m_w_ukv, g_q=m_g_q, g_k=m_g_k,
              conv_w=m_conv_w, g_oa=m_g_oa, g_oc=m_g_oc, w_o=m_w_o, w_pl=m_w_pl, w_plg=m_w_plg, g_pl=m_g_pl)
    vs = dict(g_in=v_g_in, w_in=v_w_in, g_cq=v_g_cq, w_uq=v_w_uq, g_ckv=v_g_ckv, w_ukv=v_w_ukv, g_q=v_g_q, g_k=v_g_k,
              conv_w=v_conv_w, g_oa=v_g_oa, g_oc=v_g_oc, w_o=v_w_o, w_pl=v_w_pl, w_plg=v_w_plg, g_pl=v_g_pl)
    names = list(weights)
    flat = lambda a: a.reshape(-1, a.shape[-1])
    small_names = list(g_by_name)
    one_row = lambda a: a.reshape(1, -1)
    small_out = _adamw_small([one_row(weights[n]) for n in small_names], [one_row(g_by_name[n]) for n in small_names],
                             [one_row(ms[n]) for n in small_names], [one_row(vs[n]) for n in small_names])
    results = {n: (g_by_name[n], *(out[i] for out in small_out)) for i, n in enumerate(small_names)}
    for n in half_by_name:
        shard = (lambda a: a[0].T) if n in transposed else flat
        out = _adamw_halves(shard(weights[n]), *half_by_name[n], shard(ms[n]), shard(vs[n]), c, "adamw_" + n,
                            n in transposed)
        results[n] = [a.T for a in out] if n in transposed else out
    per_kind = [[results[n][kind].reshape(weights[n].shape) for n in names] for kind in range(4)]
    return (loss_out, gx.reshape(x.shape), *per_kind[0], *per_kind[1], *per_kind[2], *per_kind[3])
```

```python
import math

import jax
import jax.numpy as jnp
from jax import lax
from jax.experimental import pallas as pl
from jax.experimental.pallas import tpu as pltpu

F32 = jnp.float32
BF16 = jnp.bfloat16

D_MODEL = 1024
N_HEADS = 4
NOPE = 128
ROPE = 64
V_DIM = 128
QK_DIM = NOPE + ROPE
HEAD_PAD = 256
Q_LORA = 256
KV_LORA = 128
ATTN_W = 512
CONV_W = 512
PLE = 256
IN_TOTAL = 3008
PROJ_EXT = 3072
ROPE_THETA = 10000.0
EPS = 1e-6
SCALE = 1.0 / math.sqrt(QK_DIM)
LOG2E = math.log2(math.e)
EXP2_SCALE = SCALE * LOG2E
NEG = -1e30
SOFTMAX_ROWS = 32
SUB_TILE = 256

LR, B1, B2, ADAM_EPS, WD, STEP = 0.001, 0.9, 0.999, 1e-08, 0.01, 10

N_CHIPS = 4
LANES = 128
VMEM_LIMIT = 56 * 1024 * 1024
MESH = pl.DeviceIdType.MESH


def _params(**kw):
    return pltpu.CompilerParams(vmem_limit_bytes=VMEM_LIMIT, **kw)


def _inv_rms(x, n):
    return lax.rsqrt(jnp.sum(x * x, axis=-1, keepdims=True) / n + EPS)


def _lane_sum(a):
    folded = a[:, 0:LANES]
    for c0 in range(LANES, a.shape[1], LANES):
        folded = folded + a[:, c0:c0 + LANES]
    head = folded.astype(BF16)
    tail = (folded - head.astype(F32)).astype(BF16)
    return _dot(jnp.concatenate([head, tail], axis=1), jnp.ones((2 * LANES, LANES), BF16))


def _inv_rms_mxu(x):
    return lax.rsqrt(_lane_sum(x * x) / x.shape[1] + EPS)


def _rep(r, width):
    return r if width == LANES else jnp.tile(r, (1, width // LANES))


def _sigmoid(z):
    return jax.nn.sigmoid(z)


def _swap_rope_halves(b):
    lane = lax.broadcasted_iota(jnp.int32, b.shape, 1)
    swapped = jnp.where(lane < 32, pltpu.roll(b, 96, 1), pltpu.roll(b, 32, 1))
    return jnp.where(lane < ROPE, swapped, 0.0)


def _dot(a, b):
    return jnp.dot(a, b, preferred_element_type=F32)


def _dot_nt(a, b):
    return lax.dot_general(a, b, (((1,), (1,)), ((), ())), preferred_element_type=F32)


def _dot_tn(a, b):
    return lax.dot_general(a, b, (((0,), (0,)), ((), ())), preferred_element_type=F32)


def _colsum(a):
    return jnp.sum(a, axis=0, keepdims=True)


def _store_transposed(src_ref, dst_ref):
    r, c = src_ref.shape
    for r0 in range(0, r, LANES):
        h = min(LANES, r - r0)
        for c0 in range(0, c, LANES):
            w = min(LANES, c - c0)
            piece = src_ref[r0:r0 + h, c0 + w - LANES:c0 + w]
            if h < LANES:
                piece = jnp.concatenate([piece, jnp.zeros((LANES - h, LANES), piece.dtype)], axis=0)
            dst_ref[c0:c0 + w, r0:r0 + h] = piece.T[LANES - w:, 0:h].astype(dst_ref.dtype)


def _full(shape):
    return pl.BlockSpec(shape, lambda *_: (0,) * len(shape))


def _round_robin(chains, width):
    waiting, active = list(chains), []
    while waiting or active:
        while waiting and len(active) < width:
            active.append(waiting.pop(0))
        for chain in list(active):
            if next(chain, _DONE) is _DONE:
                active.remove(chain)


_DONE = object()


def _rope_tables(pos_ref, invf_ref, sgn_ref):
    pos = jnp.broadcast_to(pos_ref[...].astype(F32), (LANES, pos_ref.shape[1])).T
    ang = pos * invf_ref[...]
    return jnp.cos(ang), jnp.sin(ang) * sgn_ref[...]


def _fwd_proj(x, pos, g_in, c_in, g_cq, w_uq, g_ckv, w_ukv, gq, gk, invf, sgn, late_shards, tm):
    T = x.shape[0]
    nt = T // tm
    n_late = len(late_shards)
    ts = min(SUB_TILE, tm)

    def body(x_ref, pos_ref, g_in_ref, c_in_ref, g_cq_ref, w_uq_ref, g_ckv_ref, w_ukv_ref, gq_ref, gk_ref,
             invf_ref, sgn_ref, *rest):
        late_in, (proj_ref, q_ref, k_ref, v_ref, w_in_ref) = rest[:n_late], rest[n_late:n_late + 5]
        late_out, late_scratch = rest[n_late + 5:2 * n_late + 5], rest[2 * n_late + 5:]
        i = pl.program_id(0)

        @pl.when(i == 0)
        def _():
            w_in_ref[:, 0:KPE_END] = c_in_ref[0, :, 0:KPE_END]
            w_in_ref[:, KPE_END:KPE_END + KPE_PAD] = jnp.zeros((D_MODEL, KPE_PAD), BF16)
            w_in_ref[:, KPE_END + KPE_PAD:SHARD_COLS_IN + KPE_PAD] = c_in_ref[0, :, KPE_END:SHARD_COLS_IN]
            for chip in range(1, N_CHIPS):
                w_in_ref[:, SHARD_COLS_IN * chip + KPE_PAD:SHARD_COLS_IN * (chip + 1) + KPE_PAD] = c_in_ref[chip]

        if n_late:
            start, forward, drain = _gather_steps([s.shape for s in late_shards], late_in, late_out,
                                                  late_scratch[:n_late], *late_scratch[n_late:])
            pl.when(i == 0)(start)
            pl.when(i == nt // 2)(forward)

        for r0 in range(0, tm, ts):
            rows = slice(r0, r0 + ts)
            xv = x_ref[rows, :]
            h = (xv * _rep(_inv_rms_mxu(xv), D_MODEL) * g_in_ref[...]).astype(BF16)
            lat = _dot(h, w_in_ref[:, 0:512])
            proj_ref[rows, 0:512] = lat
            c_q = lat[:, 0:Q_LORA]
            cqn = (c_q * _rep(_inv_rms_mxu(c_q), Q_LORA) * g_cq_ref[...]).astype(BF16)
            c_kv = lat[:, Q_LORA:Q_LORA + KV_LORA]
            ckvn = (c_kv * _inv_rms_mxu(c_kv) * g_ckv_ref[...]).astype(BF16)
            kpe = lat[:, 384:512]
            kpe_sq = kpe * kpe
            cos_b, sin_b = _rope_tables(pos_ref.at[:, rows], invf_ref, sgn_ref)
            gq_a, gq_b = gq_ref[:, 0:NOPE], gq_ref[:, NOPE:HEAD_PAD]
            gk_a, gk_b = gk_ref[:, 0:NOPE], gk_ref[:, NOPE:HEAD_PAD]

            def projections(rows=rows, h=h):
                for c0 in range(512, PROJ_EXT, 512):
                    proj_ref[rows, c0:c0 + 512] = _dot(h, w_in_ref[:, c0:c0 + 512])
                    yield

            def queries(hd, rows=rows, cqn=cqn, cos_b=cos_b, sin_b=sin_b, gq_a=gq_a, gq_b=gq_b):
                qh = _dot(cqn, w_uq_ref[hd])
                yield
                a, b = qh[:, 0:NOPE], qh[:, NOPE:HEAD_PAD]
                r = lax.rsqrt(_lane_sum(a * a + b * b) / QK_DIM + EPS)
                yield
                bn = b * r * gq_b
                q_ref[hd, rows, 0:NOPE] = (a * r * gq_a).astype(BF16)
                q_ref[hd, rows, NOPE:HEAD_PAD] = (bn * cos_b + _swap_rope_halves(bn) * sin_b).astype(BF16)
                yield

            def keys(hd, rows=rows, ckvn=ckvn, kpe=kpe, kpe_sq=kpe_sq, cos_b=cos_b, sin_b=sin_b, gk_a=gk_a, gk_b=gk_b):
                kvh = _dot(ckvn, w_ukv_ref[hd])
                yield
                ka = kvh[:, 0:NOPE]
                rk = lax.rsqrt(_lane_sum(ka * ka + kpe_sq) / QK_DIM + EPS)
                yield
                kbn = kpe * rk * gk_b
                k_ref[hd, rows, 0:NOPE] = (ka * rk * gk_a).astype(BF16)
                k_ref[hd, rows, NOPE:HEAD_PAD] = (kbn * cos_b + _swap_rope_halves(kbn) * sin_b).astype(BF16)
                v_ref[hd, rows, 0:V_DIM] = kvh[:, NOPE:HEAD_PAD].astype(BF16)
                v_ref[hd, rows, V_DIM:2 * V_DIM] = jnp.ones((ts, V_DIM), BF16)
                yield

            chains = [projections()]
            for hd in range(N_HEADS):
                chains += [queries(hd), keys(hd)]
            _round_robin(chains, 4)

        if n_late:
            pl.when(i == nt - 1)(drain)

    row = lambda i: (i, 0)
    head_rows = lambda i: (0, i, 0)
    outs = pl.pallas_call(
        body, name="fwd_proj", grid=(nt,),
        in_specs=[pl.BlockSpec((tm, D_MODEL), row), pl.BlockSpec((1, tm), lambda i: (0, i)), _full((1, D_MODEL)),
                  _full((N_CHIPS, D_MODEL, SHARD_COLS_IN)), _full((1, Q_LORA)), _full((N_HEADS, Q_LORA, HEAD_PAD)),
                  _full((1, KV_LORA)), _full((N_HEADS, KV_LORA, HEAD_PAD)), _full((1, HEAD_PAD)), _full((1, HEAD_PAD)),
                  _full((1, LANES)), _full((1, LANES))] + [_full(s.shape) for s in late_shards],
        out_specs=[pl.BlockSpec((tm, PROJ_EXT), row), pl.BlockSpec((N_HEADS, tm, HEAD_PAD), head_rows),
                   pl.BlockSpec((N_HEADS, tm, HEAD_PAD), head_rows), pl.BlockSpec((N_HEADS, tm, 2 * V_DIM), head_rows),
                   _full((D_MODEL, PROJ_EXT))] + [_ANY] * n_late,
        out_shape=[jax.ShapeDtypeStruct((T, PROJ_EXT), F32), jax.ShapeDtypeStruct((N_HEADS, T, HEAD_PAD), BF16),
                   jax.ShapeDtypeStruct((N_HEADS, T, HEAD_PAD), BF16), jax.ShapeDtypeStruct((N_HEADS, T, 2 * V_DIM), BF16),
                   jax.ShapeDtypeStruct((D_MODEL, PROJ_EXT), BF16)] + _gathered_shapes([s.shape for s in late_shards]),
        scratch_shapes=_gather_scratch([s.shape for s in late_shards]) if n_late else [],
        compiler_params=_params(dimension_semantics=("arbitrary",)),
    )(x, pos, g_in, c_in, g_cq, w_uq, g_ckv, w_ukv, gq, gk, invf, sgn, *late_shards)
    return outs[:5], outs[5:]


def _chunk_pipeline(n_loop, lag, matmuls, pointwise, accumulate, last):
    slots = lag + 1

    def iteration(t, slot, pending=True, ahead=True):
        if ahead:
            matmuls(jnp.minimum(t + lag, n_loop), (slot + lag) % slots)
        if pending:
            accumulate(t - lag, (slot + 1) % slots, False)
        pointwise(t, slot, False)

    def finish(slot, pending):
        for back in range(pending, 0, -1):
            accumulate(n_loop - back, (slot - back) % slots, False)
        pointwise(n_loop, slot, True)
        accumulate(n_loop, slot, True)
        last()

    for u in range(lag):
        matmuls(jnp.minimum(u, n_loop), u)
    for u in range(lag):
        pl.when(u < n_loop)(lambda u=u: iteration(u, u, pending=False))

    n_main = jnp.maximum(n_loop - lag, 0)

    def unrolled(tt, carry):
        for j in range(slots):
            iteration(lag + slots * tt + j, (lag + j) % slots)
        return carry

    lax.fori_loop(0, n_main // slots, unrolled, 0)
    rest = lax.rem(n_main, slots)
    t0 = n_loop - rest

    for r in range(slots):
        @pl.when(jnp.logical_and(n_loop >= lag, rest == r))
        def _():
            for j in range(r):
                iteration(t0 + j, (lag + j) % slots, ahead=j + lag <= r)
            finish((lag + r) % slots, lag)

    for short in range(lag):
        pl.when(n_loop == short)(lambda short=short: finish(short, short))


def _attn_fwd(q, k, v, tq):
    T = q.shape[1]
    tk = tq
    rc = min(SOFTMAX_ROWS, tq)

    def body(q_ref, k_ref, v_ref, o_ref, lse_ref, s0, s1, s2, p0, p1, p2, a0, a1, a2, m_ref, acc_ref):
        qi = pl.program_id(1)
        s_buf, p_buf, a_buf = (s0, s1, s2), (p0, p1, p2), (a0, a1, a2)

        def scores(t, slot):
            ks = pl.multiple_of(t * tk, tk)
            s_buf[slot][...] = _dot_nt(q_ref[0], k_ref[0, pl.ds(ks, tk), :])

        def blocks(masked):
            return ((0, tq // 2, tk // 2), (tq // 2, tq // 2, tk)) if masked else ((0, tq, tk),)

        def values(t, slot, masked):
            ks = pl.multiple_of(t * tk, tk)
            for q0, nq, nk in blocks(masked):
                rows = slice(q0, q0 + nq)
                acc_ref[rows, :] = (acc_ref[rows, :] * a_buf[slot][rows, :]
                                    + _dot(p_buf[slot][rows, 0:nk], v_ref[0, pl.ds(ks, nk), :]))

        def softmax(t, slot, masked):
            for q0, nq, nk in blocks(masked):
                rows = slice(q0, q0 + nq)
                s_all = s_buf[slot][rows, 0:nk]
                if masked:
                    row = lax.broadcasted_iota(jnp.int32, (nq, nk), 0) + q0
                    col = lax.broadcasted_iota(jnp.int32, (nq, nk), 1)
                    s_all = jnp.where(col <= row, s_all, NEG)
                    s_buf[slot][rows, 0:nk] = s_all
                m_old = m_ref[rows, :]
                m_new = jnp.maximum(m_old, jnp.max(s_all, axis=1, keepdims=True))
                a_buf[slot][rows, :] = jnp.exp2((m_old - m_new) * EXP2_SCALE)
                m_ref[rows, :] = m_new
                for r0 in range(0, nq, rc):
                    s = s_buf[slot][q0 + r0:q0 + r0 + rc, 0:nk]
                    p_buf[slot][q0 + r0:q0 + r0 + rc, 0:nk] = jnp.exp2((s - m_new[r0:r0 + rc, :]) * EXP2_SCALE).astype(BF16)

        def last():
            l = acc_ref[:, V_DIM:2 * V_DIM]
            o_ref[...] = acc_ref[:, 0:V_DIM] / l
            lse_ref[0] = (m_ref[...] * SCALE + jnp.log(l)).T[0:1, :]

        m_ref[...] = jnp.full_like(m_ref, NEG)
        acc_ref[...] = jnp.zeros_like(acc_ref)
        _chunk_pipeline(qi, 2, scores, softmax, values, last)

    return pl.pallas_call(
        body, name="attn_fwd", grid=(N_HEADS, T // tq),
        in_specs=[pl.BlockSpec((1, tq, HEAD_PAD), lambda h, i: (h, i, 0)),
                  pl.BlockSpec((1, T, HEAD_PAD), lambda h, i: (h, 0, 0)),
                  pl.BlockSpec((1, T, 2 * V_DIM), lambda h, i: (h, 0, 0))],
        out_specs=[pl.BlockSpec((tq, V_DIM), lambda h, i: (i, h)),
                   pl.BlockSpec((1, 1, tq), lambda h, i: (h, 0, i))],
        out_shape=[jax.ShapeDtypeStruct((T, ATTN_W), F32), jax.ShapeDtypeStruct((N_HEADS, 1, T), F32)],
        scratch_shapes=[pltpu.VMEM((tq, tk), F32)] * 3 + [pltpu.VMEM((tq, tk), BF16)] * 3
                       + [pltpu.VMEM((tq, 1), F32)] * 4 + [pltpu.VMEM((tq, 2 * V_DIM), F32)],
        compiler_params=_params(dimension_semantics=("arbitrary", "arbitrary")),
    )(q, k, v)


def _tail(x, o, proj, p, tgt, g_oa, g_oc, g_pl, conv_w, w_o, w_pl, w_plg, tm):
    T = x.shape[0]
    nt = T // tm

    def body(x_ref, o_ref, za_ref, cb_ref, cc_ref, cx_ref, zc_ref, cch_ref, cxh_ref, p_ref, tgt_ref,
             g_oa_ref, g_oc_ref, g_pl_ref, cw_ref, w_o_ref, w_pl_ref, w_plg_ref,
             dx1_ref, do_ref, delta_ref, dtail_ref, du_ref,
             dw_o_ref, dw_pl_ref, dw_plg_ref, dg_oa_ref, dg_oc_ref, dg_pl_ref, dcw_ref, loss_ref):
        i = pl.program_id(0)

        @pl.when(i == 0)
        def _():
            for r in (dw_o_ref, dw_pl_ref, dw_plg_ref, dg_oa_ref, dg_oc_ref, dg_pl_ref, dcw_ref, loss_ref):
                r[...] = jnp.zeros_like(r)

        g_oa, g_oc, g_pl = g_oa_ref[...], g_oc_ref[...], g_pl_ref[...]
        w0, w1, w2 = cw_ref[0:1, :], cw_ref[1:2, :], cw_ref[2:3, :]

        xv, ov, za, cb, zc = x_ref[...], o_ref[...], za_ref[...], cb_ref[...], zc_ref[...]
        pb = p_ref[...].astype(BF16)
        pp = _dot(pb, w_pl_ref[...])

        sa = _sigmoid(za)
        silu_a = za * sa
        ga = ov * silu_a
        ra = _inv_rms(ga, ATTN_W)
        xa = ga * ra
        ya = (xa * g_oa).astype(BF16)
        x1_a = _dot(ya, w_o_ref[0:ATTN_W, :])
        v = cc_ref[...] * cx_ref[...]
        not_first = jnp.where(i > 0, 1.0, 0.0)
        hv6 = cch_ref[6:7, :] * cxh_ref[6:7, :] * not_first
        hv7 = cch_ref[7:8, :] * cxh_ref[7:8, :] * not_first
        row = lax.broadcasted_iota(jnp.int32, v.shape, 0)
        v1 = jnp.where(row == 0, hv7, pltpu.roll(v, 1, 0))
        v2 = jnp.where(row == 0, hv6, jnp.where(row == 1, hv7, pltpu.roll(v, 2, 0)))
        u = w0 * v2 + w1 * v1 + w2 * v
        sc = _sigmoid(zc)
        silu_c = zc * sc
        gc = cb * u * silu_c
        rc = _inv_rms(gc, CONV_W)
        xc = gc * rc
        yc = (xc * g_oc).astype(BF16)
        x1 = xv + (x1_a + _dot(yc, w_o_ref[ATTN_W:D_MODEL, :]))
        r1 = _inv_rms(x1, D_MODEL)
        xh1 = x1 * r1
        n1 = (xh1 * g_pl).astype(BF16)
        gate = _sigmoid(_dot(n1, w_plg_ref[...]))
        err = x1 + gate * pp - tgt_ref[...]
        loss_ref[...] += 0.5 * jnp.sum(err * err) / D_MODEL
        dy = err / D_MODEL

        dpp = (dy * gate).astype(BF16)
        da = (dy * pp * gate * (1.0 - gate)).astype(BF16)
        dn1 = _dot_nt(da, w_plg_ref[...])
        dw_pl_ref[...] += _dot_tn(pb, dpp)
        dw_plg_ref[...] += _dot_tn(n1, da)
        dg_pl_ref[...] += _colsum(dn1 * xh1)
        dxh = dn1 * g_pl
        dx1 = dy + r1 * (dxh - xh1 * (jnp.sum(dxh * xh1, axis=-1, keepdims=True) / D_MODEL))
        dx1_ref[...] = dx1
        dx1b = dx1.astype(BF16)
        dya = _dot_nt(dx1b, w_o_ref[0:ATTN_W, :])
        dyc = _dot_nt(dx1b, w_o_ref[ATTN_W:D_MODEL, :])

        dw_o_ref[0:ATTN_W, :] += _dot_tn(ya, dx1b)
        dg_oa_ref[...] += _colsum(dya * xa)
        dxa = dya * g_oa
        dga = ra * (dxa - xa * (jnp.sum(dxa * xa, axis=-1, keepdims=True) / ATTN_W))
        do = (dga * silu_a).astype(BF16)
        do_ref[...] = do
        dof = do.astype(F32) * ov
        for hd in range(N_HEADS):
            delta_ref[hd] = _lane_sum(dof[:, hd * V_DIM:(hd + 1) * V_DIM]).T[0:1, :]
        dtail_ref[:, 0:512] = (dga * ov * (sa * (1.0 + za * (1.0 - sa)))).astype(BF16)

        dw_o_ref[ATTN_W:D_MODEL, :] += _dot_tn(yc, dx1b)
        dg_oc_ref[...] += _colsum(dyc * xc)
        dxc = dyc * g_oc
        dgc = rc * (dxc - xc * (jnp.sum(dxc * xc, axis=-1, keepdims=True) / CONV_W))
        dtail_ref[:, 512:1024] = (dgc * u * silu_c).astype(BF16)
        du = dgc * cb * silu_c
        du_ref[...] = du
        dtail_ref[:, 1024:1536] = (dgc * cb * u * (sc * (1.0 + zc * (1.0 - sc)))).astype(BF16)
        dcw_ref[0:1, :] += _colsum(du * v2)
        dcw_ref[1:2, :] += _colsum(du * v1)
        dcw_ref[2:3, :] += _colsum(du * v)

    row = lambda i: (i, 0)
    col = lambda c: (lambda i: (i, c))
    halo = lambda c: (lambda i: (jnp.maximum(i * (tm // 8) - 1, 0), c))
    in_specs = [pl.BlockSpec((tm, D_MODEL), row), pl.BlockSpec((tm, ATTN_W), row)]
    in_specs += [pl.BlockSpec((tm, 512), col(c)) for c in (1, 2, 3, 4, 5)]
    in_specs += [pl.BlockSpec((8, 512), halo(3)), pl.BlockSpec((8, 512), halo(4))]
    in_specs += [pl.BlockSpec((tm, PLE), row), pl.BlockSpec((tm, D_MODEL), row),
                 _full((1, ATTN_W)), _full((1, CONV_W)), _full((1, D_MODEL)), _full((3, CONV_W)),
                 _full((D_MODEL, D_MODEL)), _full((PLE, D_MODEL)), _full((D_MODEL, D_MODEL))]
    out_specs = [pl.BlockSpec((tm, D_MODEL), row), pl.BlockSpec((tm, ATTN_W), row),
                 pl.BlockSpec((N_HEADS, 1, tm), lambda i: (0, 0, i)), pl.BlockSpec((tm, 1536), row),
                 pl.BlockSpec((tm, CONV_W), row),
                 _full((D_MODEL, D_MODEL)), _full((PLE, D_MODEL)), _full((D_MODEL, D_MODEL)),
                 _full((1, ATTN_W)), _full((1, CONV_W)), _full((1, D_MODEL)), _full((3, CONV_W)), _full((1, LANES))]
    out_shape = [jax.ShapeDtypeStruct((T, D_MODEL), F32), jax.ShapeDtypeStruct((T, ATTN_W), BF16),
                 jax.ShapeDtypeStruct((N_HEADS, 1, T), F32), jax.ShapeDtypeStruct((T, 1536), BF16),
                 jax.ShapeDtypeStruct((T, CONV_W), F32),
                 jax.ShapeDtypeStruct((D_MODEL, D_MODEL), F32), jax.ShapeDtypeStruct((PLE, D_MODEL), F32),
                 jax.ShapeDtypeStruct((D_MODEL, D_MODEL), F32),
                 jax.ShapeDtypeStruct((1, ATTN_W), F32), jax.ShapeDtypeStruct((1, CONV_W), F32),
                 jax.ShapeDtypeStruct((1, D_MODEL), F32), jax.ShapeDtypeStruct((3, CONV_W), F32),
                 jax.ShapeDtypeStruct((1, LANES), F32)]
    return pl.pallas_call(
        body, name="tail", grid=(nt,), in_specs=in_specs, out_specs=out_specs, out_shape=out_shape,
        compiler_params=_params(dimension_semantics=("arbitrary",)),
    )(x, o, proj, proj, proj, proj, proj, proj, proj, p, tgt, g_oa, g_oc, g_pl, conv_w, w_o, w_pl, w_plg)


def _attn_bwd(q, k, v, do, lse_row, delta_row, tk, swap):
    T = q.shape[1]
    tq = tk
    nq = T // tq
    rc = min(SOFTMAX_ROWS, tk)
    hk, hq = tk // 2, tq // 2
    n_swap = len(swap)

    def body(q_ref, k_ref, v_ref, do_ref, lse_ref, dl_ref, *rest):
        swap_in, (dq_ref, dk_ref, dv_ref), rest = rest[:n_swap], rest[n_swap:n_swap + 3], rest[n_swap + 3:]
        swap_out, (s0, s1, d0, d1, p0, p1, g0, g1, dk_acc, dv_acc), sems = rest[:n_swap], rest[n_swap:n_swap + 10], rest[n_swap + 10:]
        kj = pl.program_id(1)
        s_buf, dp_buf, p_buf, g_buf = (s0, s1), (d0, d1), (p0, p1), (g0, g1)

        if n_swap:
            start, drain = _swap_steps(n_swap, swap_in, swap_out, *sems)
            pl.when(jnp.logical_and(pl.program_id(0) == 0, kj == 0))(start)

        @pl.when(kj == 0)
        def _():
            dq_ref[...] = jnp.zeros_like(dq_ref)

        def q_start(t):
            return pl.multiple_of((nq - 1 - t) * tq, tq)

        def matmuls(t, slot):
            qs = q_start(t)
            s_buf[slot][...] = _dot_nt(k_ref[0], q_ref[0, pl.ds(qs, tq), :])
            dp_buf[slot][...] = _dot_nt(v_ref[0], do_ref[pl.ds(qs, tq), :])

        def pointwise(t, slot, masked):
            qs = q_start(t)
            lse2 = lse_ref[0, :, pl.ds(qs, tq)] * LOG2E
            dl = dl_ref[0, :, pl.ds(qs, tq)]
            for r0 in range(0, tk, rc):
                c0 = r0 // hk * hq if masked else 0
                rows, cols = slice(r0, r0 + rc), slice(c0, tq)
                st = s_buf[slot][rows, cols]
                if masked:
                    row = lax.broadcasted_iota(jnp.int32, (rc, tq - c0), 0) + r0
                    col = lax.broadcasted_iota(jnp.int32, (rc, tq - c0), 1) + c0
                    st = jnp.where(row <= col, st, NEG)
                pt = jnp.exp2(st * EXP2_SCALE - lse2[:, cols])
                p_buf[slot][rows, cols] = pt.astype(BF16)
                g_buf[slot][rows, cols] = (pt * (dp_buf[slot][rows, cols] - dl[:, cols]) * SCALE).astype(BF16)

        def accumulate(t, slot, masked):
            qs = q_start(t)
            p, g = p_buf[slot], g_buf[slot]
            if not masked:
                dv_acc[...] += _dot(p[...], do_ref[pl.ds(qs, tq), :])
                dk_acc[...] += _dot(g[...], q_ref[0, pl.ds(qs, tq), :])
                dq_ref[0, pl.ds(qs, tq), :] += _dot_tn(g[...], k_ref[0])
                return
            q2 = pl.multiple_of(qs + hq, hq)
            dv_acc[0:hk, :] += _dot(p[0:hk, :], do_ref[pl.ds(qs, tq), :])
            dv_acc[hk:tk, :] += _dot(p[hk:tk, hq:tq], do_ref[pl.ds(q2, hq), :])
            dk_acc[0:hk, :] += _dot(g[0:hk, :], q_ref[0, pl.ds(qs, tq), :])
            dk_acc[hk:tk, :] += _dot(g[hk:tk, hq:tq], q_ref[0, pl.ds(q2, hq), :])
            dq_ref[0, pl.ds(qs, hq), :] += _dot_tn(g[0:hk, 0:hq], k_ref[0, 0:hk, :])
            dq_ref[0, pl.ds(q2, hq), :] += _dot_tn(g[:, hq:tq], k_ref[0])

        def last():
            dk_ref[0] = dk_acc[...]
            dv_ref[0] = dv_acc[...]

        dk_acc[...] = jnp.zeros_like(dk_acc)
        dv_acc[...] = jnp.zeros_like(dv_acc)
        _chunk_pipeline(nq - 1 - kj, 1, matmuls, pointwise, accumulate, last)

        if n_swap:
            pl.when(jnp.logical_and(pl.program_id(0) == N_HEADS - 1, kj == T // tk - 1))(drain)

    outs = pl.pallas_call(
        body, name="attn_bwd", grid=(N_HEADS, T // tk),
        in_specs=[pl.BlockSpec((1, T, HEAD_PAD), lambda h, j: (h, 0, 0)),
                  pl.BlockSpec((1, tk, HEAD_PAD), lambda h, j: (h, j, 0)),
                  pl.BlockSpec((1, tk, V_DIM), lambda h, j: (h, j, 0)),
                  pl.BlockSpec((T, V_DIM), lambda h, j: (0, h)),
                  pl.BlockSpec((1, 1, T), lambda h, j: (h, 0, 0)),
                  pl.BlockSpec((1, 1, T), lambda h, j: (h, 0, 0))] + [_ANY] * n_swap,
        out_specs=[pl.BlockSpec((1, T, HEAD_PAD), lambda h, j: (h, 0, 0)),
                   pl.BlockSpec((1, tk, HEAD_PAD), lambda h, j: (h, j, 0)),
                   pl.BlockSpec((1, tk, V_DIM), lambda h, j: (h, j, 0))] + [_ANY] * n_swap,
        out_shape=[jax.ShapeDtypeStruct((N_HEADS, T, HEAD_PAD), F32), jax.ShapeDtypeStruct((N_HEADS, T, HEAD_PAD), F32),
                   jax.ShapeDtypeStruct((N_HEADS, T, V_DIM), F32)] + _swapped_shapes(swap, []),
        scratch_shapes=[pltpu.VMEM((tk, tq), F32)] * 4 + [pltpu.VMEM((tk, tq), BF16)] * 4
                       + [pltpu.VMEM((tk, HEAD_PAD), F32), pltpu.VMEM((tk, V_DIM), F32)]
                       + ([pltpu.SemaphoreType.DMA((n_swap,))] * 2 if n_swap else []),
        compiler_params=_params(dimension_semantics=("arbitrary", "arbitrary")),
    )(q, k, v, do, lse_row, delta_row, *swap)
    return outs[:3], outs[3:]


def _bwd_proj(x, dx1, pos, proj, dq, dk, dv, dtail, du, g_in, w_in, g_cq, w_uq, g_ckv, w_ukv, gq, gk, conv_w,
              invf, sgn, tm):
    T = x.shape[0]
    nt = T // tm

    ts = min(SUB_TILE, tm)

    def body(x_ref, dx1_ref, pos_ref, lat_ref, cc_ref, cx_ref, dq_ref, dk_ref, dv_ref, dtail_ref, du_ref, dun_ref, *rest):
        consts, (gx_ref, h_ref, dproj_ref), sums = rest[:11], rest[11:14], rest[14:]
        cw_ref = consts[8]
        i = pl.program_id(0)

        @pl.when(i == 0)
        def _():
            for r in sums:
                r[...] = jnp.zeros_like(r)

        du_v = du_ref[...]
        not_last = jnp.where(i < nt - 1, 1.0, 0.0)
        nx0 = dun_ref[0:1, :] * not_last
        nx1 = dun_ref[1:2, :] * not_last
        row = lax.broadcasted_iota(jnp.int32, du_v.shape, 0)
        du1 = jnp.where(row == tm - 1, nx0, pltpu.roll(du_v, tm - 1, 0))
        du2 = jnp.where(row == tm - 2, nx0, jnp.where(row == tm - 1, nx1, pltpu.roll(du_v, tm - 2, 0)))
        dvc = cw_ref[2:3, :] * du_v + cw_ref[1:2, :] * du1 + cw_ref[0:1, :] * du2
        dproj_ref[:, 1536:2048] = (dvc * cx_ref[...]).astype(BF16)
        dproj_ref[:, 2048:2560] = (dvc * cc_ref[...]).astype(BF16)

        for r0 in range(0, tm, ts):
            rows = slice(r0, r0 + ts)
            work(x_ref.at[rows, :], dx1_ref.at[rows, :], pos_ref.at[:, rows], lat_ref.at[rows, :],
                 dq_ref.at[:, rows, :], dk_ref.at[:, rows, :], dv_ref.at[:, rows, :], dtail_ref.at[rows, :], *consts,
                 gx_ref.at[rows, :], h_ref.at[:, rows], dproj_ref.at[rows, :], *sums)

    def work(x_ref, dx1_ref, pos_ref, lat_ref, dq_ref, dk_ref, dv_ref, dtail_ref,
             g_in_ref, w_in_ref, g_cq_ref, w_uq_ref, g_ckv_ref, w_ukv_ref, gq_ref, gk_ref, cw_ref, invf_ref, sgn_ref,
             gx_ref, h_ref, dproj_ref, dw_uq_ref, dw_ukv_ref, dg_in_ref, dg_cq_ref, dg_ckv_ref, dgq_ref, dgk_ref):
        xv = x_ref[...]
        r0 = _rep(_inv_rms_mxu(xv), D_MODEL)
        xh0 = xv * r0
        g_in = g_in_ref[...]
        h_ref[...] = (xh0 * g_in).astype(BF16).T

        c_q = lat_ref[:, 0:Q_LORA]
        rq = _rep(_inv_rms_mxu(c_q), Q_LORA)
        xq = c_q * rq
        g_cq = g_cq_ref[...]
        cqn = (xq * g_cq).astype(BF16)
        c_kv = lat_ref[:, Q_LORA:Q_LORA + KV_LORA]
        rkv = _inv_rms_mxu(c_kv)
        xkv = c_kv * rkv
        g_ckv = g_ckv_ref[...]
        ckvn = (xkv * g_ckv).astype(BF16)
        kpe = lat_ref[:, 384:512]
        kpe_sq = kpe * kpe
        cos_b, sin_b = _rope_tables(pos_ref, invf_ref, sgn_ref)
        gq_a, gq_b = gq_ref[:, 0:NOPE], gq_ref[:, NOPE:HEAD_PAD]
        gk_a, gk_b = gk_ref[:, 0:NOPE], gk_ref[:, NOPE:HEAD_PAD]

        dproj_ref[:, 512:1536] = dtail_ref[:, 0:1024]
        dproj_ref[:, 2560:3072] = dtail_ref[:, 1024:1536]

        def dh_part(c0):
            return _dot_nt(dproj_ref[:, c0:c0 + 512], w_in_ref[:, c0:c0 + 512])

        later_chunks = ((512,), (1024,), (1536, 2048), (2560,))
        dh = jnp.zeros((ts, D_MODEL), F32)
        acc = dict(dh=dh, dkpe=jnp.zeros((ts, LANES), F32), dcqn=jnp.zeros((ts, Q_LORA), F32),
                   dckvn=jnp.zeros((ts, KV_LORA), F32))

        def dh_chunks():
            for chunks in later_chunks:
                for chunk in chunks:
                    acc["dh"] = acc["dh"] + dh_part(chunk)
                    yield

        def queries(hd):
            qh = _dot(cqn, w_uq_ref[hd])
            yield
            a, b = qh[:, 0:NOPE], qh[:, NOPE:HEAD_PAD]
            r = lax.rsqrt(_lane_sum(a * a + b * b) / QK_DIM + EPS)
            yield
            xa, xb = a * r, b * r
            dan = dq_ref[hd, :, 0:NOPE]
            dbr = dq_ref[hd, :, NOPE:HEAD_PAD]
            dbn = dbr * cos_b + _swap_rope_halves(dbr * sin_b)
            yield
            dgq_ref[:, 0:NOPE] += _colsum(dan * xa)
            dgq_ref[:, NOPE:HEAD_PAD] += _colsum(dbn * xb)
            dxa, dxb = dan * gq_a, dbn * gq_b
            cq = _lane_sum(dxa * xa + dxb * xb) / QK_DIM
            yield
            dqh = jnp.concatenate([r * (dxa - xa * cq), r * (dxb - xb * cq)], axis=-1).astype(BF16)
            yield
            dw_uq_ref[hd] += _dot_tn(cqn, dqh)
            yield
            acc["dcqn"] = acc["dcqn"] + _dot_nt(dqh, w_uq_ref[hd])
            yield

        def keys(hd):
            kvh = _dot(ckvn, w_ukv_ref[hd])
            yield
            ka = kvh[:, 0:NOPE]
            rk = lax.rsqrt(_lane_sum(ka * ka + kpe_sq) / QK_DIM + EPS)
            yield
            xka, xkb = ka * rk, kpe * rk
            dkan = dk_ref[hd, :, 0:NOPE]
            dkbr = dk_ref[hd, :, NOPE:HEAD_PAD]
            dkbn = dkbr * cos_b + _swap_rope_halves(dkbr * sin_b)
            yield
            dgk_ref[:, 0:NOPE] += _colsum(dkan * xka)
            dgk_ref[:, NOPE:HEAD_PAD] += _colsum(dkbn * xkb)
            dxka, dxkb = dkan * gk_a, dkbn * gk_b
            ck = _lane_sum(dxka * xka + dxkb * xkb) / QK_DIM
            yield
            acc["dkpe"] = acc["dkpe"] + rk * (dxkb - xkb * ck)
            dkvh = jnp.concatenate([rk * (dxka - xka * ck), dv_ref[hd]], axis=-1).astype(BF16)
            yield
            dw_ukv_ref[hd] += _dot_tn(ckvn, dkvh)
            yield
            acc["dckvn"] = acc["dckvn"] + _dot_nt(dkvh, w_ukv_ref[hd])
            yield

        chains = [dh_chunks()]
        for hd in range(N_HEADS):
            chains += [queries(hd), keys(hd)]
        _round_robin(chains, 5)
        dh, dkpe, dcqn, dckvn = acc["dh"], acc["dkpe"], acc["dcqn"], acc["dckvn"]

        dg_cq_ref[...] += _colsum(dcqn * xq)
        dxq = dcqn * g_cq
        dproj_ref[:, 0:Q_LORA] = (rq * (dxq - xq * _rep(_lane_sum(dxq * xq) / Q_LORA, Q_LORA))).astype(BF16)
        dg_ckv_ref[...] += _colsum(dckvn * xkv)
        dxkv = dckvn * g_ckv
        dproj_ref[:, 256:384] = (rkv * (dxkv - xkv * (_lane_sum(dxkv * xkv) / KV_LORA))).astype(BF16)
        dproj_ref[:, 384:512] = dkpe.astype(BF16)
        dh = dh + dh_part(0)
        dg_in_ref[...] += _colsum(dh * xh0)
        dxh = dh * g_in
        gx_ref[...] = dx1_ref[...] + r0 * (dxh - xh0 * _rep(_lane_sum(dxh * xh0) / D_MODEL, D_MODEL))

    row = lambda i: (i, 0)
    col = lambda c: (lambda i: (i, c))
    head_rows = lambda i: (0, i, 0)
    nxt = lambda i: (jnp.minimum((i + 1) * (tm // 8), T // 8 - 1), 0)
    in_specs = [pl.BlockSpec((tm, D_MODEL), row), pl.BlockSpec((tm, D_MODEL), row), pl.BlockSpec((1, tm), lambda i: (0, i)),
                pl.BlockSpec((tm, 512), col(0)), pl.BlockSpec((tm, 512), col(3)), pl.BlockSpec((tm, 512), col(4)),
                pl.BlockSpec((N_HEADS, tm, HEAD_PAD), head_rows), pl.BlockSpec((N_HEADS, tm, HEAD_PAD), head_rows),
                pl.BlockSpec((N_HEADS, tm, V_DIM), head_rows), pl.BlockSpec((tm, 1536), row),
                pl.BlockSpec((tm, CONV_W), row), pl.BlockSpec((8, CONV_W), nxt),
                _full((1, D_MODEL)), _full((D_MODEL, PROJ_EXT)), _full((1, Q_LORA)), _full((N_HEADS, Q_LORA, HEAD_PAD)),
                _full((1, KV_LORA)), _full((N_HEADS, KV_LORA, HEAD_PAD)), _full((1, HEAD_PAD)), _full((1, HEAD_PAD)),
                _full((3, CONV_W)), _full((1, LANES)), _full((1, LANES))]
    out_specs = [pl.BlockSpec((tm, D_MODEL), row), pl.BlockSpec((D_MODEL, tm), lambda i: (0, i)),
                 pl.BlockSpec((tm, PROJ_EXT), row),
                 _full((N_HEADS, Q_LORA, HEAD_PAD)), _full((N_HEADS, KV_LORA, HEAD_PAD)),
                 _full((1, D_MODEL)), _full((1, Q_LORA)), _full((1, KV_LORA)), _full((1, HEAD_PAD)), _full((1, HEAD_PAD))]
    out_shape = [jax.ShapeDtypeStruct((T, D_MODEL), F32), jax.ShapeDtypeStruct((D_MODEL, T), BF16),
                 jax.ShapeDtypeStruct((T, PROJ_EXT), BF16),
                 jax.ShapeDtypeStruct((N_HEADS, Q_LORA, HEAD_PAD), F32), jax.ShapeDtypeStruct((N_HEADS, KV_LORA, HEAD_PAD), F32),
                 jax.ShapeDtypeStruct((1, D_MODEL), F32), jax.ShapeDtypeStruct((1, Q_LORA), F32),
                 jax.ShapeDtypeStruct((1, KV_LORA), F32), jax.ShapeDtypeStruct((1, HEAD_PAD), F32),
                 jax.ShapeDtypeStruct((1, HEAD_PAD), F32)]
    return pl.pallas_call(
        body, name="bwd_proj", grid=(nt,), in_specs=in_specs, out_specs=out_specs, out_shape=out_shape,
        compiler_params=_params(dimension_semantics=("arbitrary",)),
    )(x, dx1, pos, proj, proj, proj, dq, dk, dv, dtail, du, du, g_in, w_in, g_cq, w_uq, g_ckv, w_ukv, gq, gk, conv_w,
      invf, sgn)


def _matmul_acc(a, b, tt, tn, parts):
    M, T = a.shape
    N = b.shape[1]
    n = len(parts)
    grid = (N // tn, T // tt)
    hm = M // 2

    def body(a_ref, b_ref, *rest):
        part_refs, (o_ref, sib_ref), rest = rest[:n], rest[n:n + 2], rest[n + 2:]
        out_refs, (stage_ref, tile_send, tile_recv), sems = rest[:n], rest[n:n + 3], rest[n + 3:]
        j, t = pl.program_id(0), pl.program_id(1)
        if n:
            start, drain = _scatter_steps(part_refs, out_refs, *sems)
            pl.when(jnp.logical_and(j == 0, t == 0))(start)

        def to_sibling(jj):
            x, y, c = _mesh_pos()
            return _remote(stage_ref, sib_ref.at[:, pl.ds(pl.multiple_of(jj * tn, tn), tn)],
                           tile_send, tile_recv, jj, (x, y, 1 - c))

        @pl.when(t == 0)
        def _():
            o_ref[...] = jnp.zeros_like(o_ref)

        o_ref[...] += _dot(a_ref[...], b_ref[...])

        tile_done = t == grid[1] - 1
        pl.when(jnp.logical_and(tile_done, j > 0))(lambda: to_sibling(j - 1).wait())

        @pl.when(tile_done)
        def _():
            c = lax.axis_index("c")
            stage_ref[...] = o_ref[pl.ds(pl.multiple_of((1 - c) * hm, hm), hm), :]
            to_sibling(j).start()

        pl.when(jnp.logical_and(tile_done, j == grid[0] - 1))(lambda: to_sibling(j).wait())
        if n:
            pl.when(jnp.logical_and(j == grid[0] - 1, t == grid[1] - 1))(drain)

    sems = [pltpu.SemaphoreType.DMA((3 * n,)), pltpu.SemaphoreType.DMA((3 * n,)), pltpu.SemaphoreType.DMA((n,))]
    outs = pl.pallas_call(
        body, name="dw_in", grid=grid,
        in_specs=[pl.BlockSpec((M, tt), lambda j, t: (0, t)), pl.BlockSpec((tt, tn), lambda j, t: (t, j))] + [_ANY] * n,
        out_specs=[pl.BlockSpec((M, tn), lambda j, t: (0, j)), _ANY] + [_ANY] * n,
        out_shape=[jax.ShapeDtypeStruct((M, N), F32), jax.ShapeDtypeStruct((hm, N), F32)] + _scattered_shapes(parts),
        scratch_shapes=[pltpu.VMEM((hm, tn), F32)] + [pltpu.SemaphoreType.DMA((grid[0],))] * 2 + (sems if n else []),
        compiler_params=_params(dimension_semantics=("arbitrary", "arbitrary")),
    )(a, b, *parts)
    return outs[0], outs[1], outs[2:]


def _add_chips(parts, small_parts):
    arrays = list(parts) + [small_parts]

    def body(*refs):
        ins, outs = refs[:len(arrays)], refs[len(arrays):]
        for a_ref, o_ref in zip(ins, outs):
            part = lambda k: a_ref[k].astype(F32)
            o_ref[...] = ((part(0) + part(1)) + part(2)) + part(3)

    in_specs, out_specs, out_shape = [], [], []
    for a in arrays:
        _, rows, cols = a.shape
        in_specs.append(pl.BlockSpec((N_CHIPS, rows // 2, cols), lambda i: (0, i, 0)))
        out_specs.append(pl.BlockSpec((rows // 2, cols), lambda i: (i, 0)))
        out_shape.append(jax.ShapeDtypeStruct((rows, cols), F32))
    outs = pl.pallas_call(body, name="add_chips", grid=(2,), in_specs=in_specs, out_specs=out_specs,
                          out_shape=out_shape, compiler_params=_params(dimension_semantics=("arbitrary",)))(*arrays)
    return outs[:-1], outs[-1]


def _adamw_small(ws, gs, ms, vs):
    n = len(ws)

    def body(*refs):
        for i in range(n):
            w_ref, g_ref, m_ref, v_ref = (refs[k * n + i] for k in range(4))
            d_ref, nm_ref, nv_ref = (refs[(4 + k) * n + i] for k in range(3))
            _adamw_math(g_ref[...], w_ref, m_ref, v_ref, d_ref, nm_ref, nv_ref)

    shapes = [jax.ShapeDtypeStruct(w.shape, F32) for w in ws]
    outs = pl.pallas_call(body, name="adamw_small", out_shape=shapes * 3)(*ws, *gs, *ms, *vs)
    return outs[:n], outs[n:2 * n], outs[2 * n:]


def _adamw_math(gv, w_ref, m_ref, v_ref, d_ref, nm_ref, nv_ref):
    nm = B1 * m_ref[...] + (1.0 - B1) * gv
    nv = B2 * v_ref[...] + (1.0 - B2) * (gv * gv)
    m_hat = nm / (1.0 - B1 ** STEP)
    v_hat = nv / (1.0 - B2 ** STEP)
    d_ref[...] = -LR * (m_hat / (jnp.sqrt(v_hat) + ADAM_EPS) + WD * w_ref[...])
    nm_ref[...] = nm
    nv_ref[...] = nv


def _adamw_halves(w, mine, other, m, v, c, name, transposed):
    hr, cols = mine.shape

    def body(c_ref, w_ref, mine_ref, other_ref, m_ref, v_ref, g_ref, d_ref, nm_ref, nv_ref, *picked):
        gv = jnp.where(pl.program_id(0) == c_ref[0], mine_ref[...], other_ref[...])
        if transposed:
            picked[0][...] = gv
            _store_transposed(picked[0], g_ref)
            gv = g_ref[...]
        else:
            g_ref[...] = gv
        _adamw_math(gv, w_ref, m_ref, v_ref, d_ref, nm_ref, nv_ref)

    if transposed:
        half = pl.BlockSpec((cols, hr), lambda i, c_ref: (0, i))
    else:
        half = pl.BlockSpec((hr, cols), lambda i, c_ref: (i, 0))
    whole = pl.BlockSpec((hr, cols), lambda i, c_ref: (0, 0))
    shp = jax.ShapeDtypeStruct(w.shape, F32)
    return pl.pallas_call(
        body, name=name, out_shape=[shp] * 4,
        grid_spec=pltpu.PrefetchScalarGridSpec(num_scalar_prefetch=1, grid=(2,), in_specs=[half, whole, whole, half, half],
                                               out_specs=[half] * 4,
                                               scratch_shapes=[pltpu.VMEM((hr, cols), F32)] if transposed else []),
        compiler_params=_params(dimension_semantics=("arbitrary",)),
    )(c.reshape(1), w, mine, other, m, v)


_ANY = pl.BlockSpec(memory_space=pl.ANY)


def _mesh_pos():
    return lax.axis_index("x"), lax.axis_index("y"), lax.axis_index("c")


def _other_chips(x, y):
    return [(1 - x, y), (x, 1 - y), (1 - x, 1 - y)]


def _remote(src, dst, send_sems, recv_sems, k, to):
    return pltpu.make_async_remote_copy(src_ref=src, dst_ref=dst, send_sem=send_sems.at[k], recv_sem=recv_sems.at[k],
                                        device_id=to, device_id_type=MESH)


def _gather_weights(shards, n_transposed):
    n = len(shards)
    shapes = [s.shape[::-1] if i < n_transposed else s.shape for i, s in enumerate(shards)]

    def body(*refs):
        start, forward, drain = _gather_steps(shapes, refs[:n], refs[n:2 * n], refs[2 * n:3 * n], *refs[3 * n:])
        start()
        forward()
        drain()

    vmem = pl.BlockSpec(memory_space=pltpu.VMEM)
    return pl.pallas_call(
        body, name="gather_weights", in_specs=[vmem] * n, out_specs=[_ANY] * n,
        out_shape=_gathered_shapes(shapes), scratch_shapes=_gather_scratch(shapes), compiler_params=_params(),
    )(*shards)


def _travel_shape(shape):
    rows, cols = shape
    return (rows, HEAD_PAD if cols == QK_DIM else cols)


def _gathered_shapes(shapes):
    return [jax.ShapeDtypeStruct((N_CHIPS,) + _travel_shape(s), BF16) for s in shapes]


def _gather_scratch(shapes):
    n = len(shapes)
    return ([pltpu.VMEM(_travel_shape(s), BF16) for s in shapes]
            + [pltpu.SemaphoreType.DMA((6 * n,)), pltpu.SemaphoreType.DMA((6 * n,)), pltpu.SemaphoreType.DMA((n,))])


def _gather_steps(shapes, ins, outs, stage, send_sems, recv_sems, local_sems):
    n = len(shapes)
    halved = [s[0] % 32 == 0 for s in shapes]

    def part(i, ref, hc):
        if not halved[i]:
            return ref
        hr = shapes[i][0] // 2
        return ref.at[pl.ds(hc * hr, hr), :]

    def to_chip(i, j, x, y, c):
        cx, cy = _other_chips(x, y)[j]
        return _remote(part(i, stage[i], c), part(i, outs[i].at[2 * x + y], c), send_sems, recv_sems, 6 * i + j, (cx, cy, c))

    def to_sibling(i, j, x, y, c):
        cx, cy = _other_chips(x, y)[j]
        got = part(i, outs[i].at[2 * cx + cy], c)
        return _remote(got, got, send_sems, recv_sems, 6 * i + 3 + j, (x, y, 1 - c))

    def local(i, x, y):
        return pltpu.make_async_copy(stage[i], outs[i].at[2 * x + y], local_sems.at[i])

    def start():
        x, y, c = _mesh_pos()
        for i in range(n):
            cols = shapes[i][1]
            if stage[i].shape[1] != cols:
                stage[i][...] = jnp.zeros_like(stage[i])
            if ins[i].shape == shapes[i]:
                stage[i][:, 0:cols] = ins[i][...].astype(BF16)
            else:
                _store_transposed(ins[i], stage[i])
            local(i, x, y).start()
            for j in range(3):
                to_chip(i, j, x, y, c).start()

    def forward():
        x, y, c = _mesh_pos()
        for i in range(n):
            for j, (cx, cy) in enumerate(_other_chips(x, y)):
                got = part(i, outs[i].at[2 * cx + cy], c)
                _remote(got, got, send_sems, recv_sems, 6 * i + j, (cx, cy, c)).wait_recv()
                if halved[i]:
                    to_sibling(i, j, x, y, c).start()

    def drain():
        x, y, c = _mesh_pos()
        for i in range(n):
            for j, (cx, cy) in enumerate(_other_chips(x, y)):
                if halved[i]:
                    got = part(i, outs[i].at[2 * cx + cy], 1 - c)
                    _remote(got, got, send_sems, recv_sems, 6 * i + 3 + j, (x, y, 1 - c)).wait_recv()
                    to_sibling(i, j, x, y, c).wait_send()
                to_chip(i, j, x, y, c).wait_send()
            local(i, x, y).wait()

    return start, forward, drain


def _swap_halves(grads, whole, name):
    n, m = len(grads), len(grads) + len(whole)

    def body(*refs):
        start, drain = _swap_steps(n, refs[:m], refs[m:2 * m], refs[2 * m], refs[2 * m + 1])
        start()
        drain()

    outs = pl.pallas_call(
        body, name=name, in_specs=[_ANY] * m, out_specs=[_ANY] * m, out_shape=_swapped_shapes(grads, whole),
        scratch_shapes=[pltpu.SemaphoreType.DMA((m,)), pltpu.SemaphoreType.DMA((m,))],
    )(*grads, *whole)
    return outs[:n], outs[n:]


def _swapped_shapes(grads, whole):
    return ([jax.ShapeDtypeStruct((g.shape[0], g.shape[1] // 2, g.shape[2]), F32) for g in grads]
            + [jax.ShapeDtypeStruct(w.shape, F32) for w in whole])


def _swap_steps(n, ins, outs, send_sems, recv_sems):
    def copies():
        x, y, c = _mesh_pos()
        cps = []
        for i, src in enumerate(ins):
            if i < n:
                hr = src.shape[1] // 2
                src = src.at[:, pl.ds((1 - c) * hr, hr), :]
            cps.append(_remote(src, outs[i], send_sems, recv_sems, i, (x, y, 1 - c)))
        return cps

    def start():
        for cp in copies():
            cp.start()

    def drain():
        for cp in copies():
            cp.wait()

    return start, drain


def _scattered_shapes(parts):
    return [jax.ShapeDtypeStruct(p.shape if p.ndim == 3 else (N_CHIPS,) + p.shape, p.dtype) for p in parts]


def _scatter_steps(ins, outs, send_sems, recv_sems, local_sems):
    n = len(ins)

    def src(i, k):
        return ins[i].at[k] if len(ins[i].shape) == 3 else ins[i]

    def sends(x, y, c):
        return [_remote(src(i, 2 * cx + cy), outs[i].at[2 * x + y], send_sems, recv_sems, 3 * i + j, (cx, cy, c))
                for i in range(n) for j, (cx, cy) in enumerate(_other_chips(x, y))]

    def local(i, x, y):
        return pltpu.make_async_copy(src(i, 2 * x + y), outs[i].at[2 * x + y], local_sems.at[i])

    def start():
        x, y, c = _mesh_pos()
        for i in range(n):
            local(i, x, y).start()
        for cp in sends(x, y, c):
            cp.start()

    def drain():
        x, y, c = _mesh_pos()
        for i in range(n):
            for j, (cx, cy) in enumerate(_other_chips(x, y)):
                got = outs[i].at[2 * cx + cy]
                _remote(got, got, send_sems, recv_sems, 3 * i + j, (cx, cy, c)).wait_recv()
        for cp in sends(x, y, c):
            cp.wait_send()
        for i in range(n):
            local(i, x, y).wait()

    return start, drain


def _add_pair(grads, from_sibling, small, small_sibling, c):
    n = len(grads)

    def body(c_ref, *refs):
        ins, outs = refs[:2 * n + 2], refs[2 * n + 2:]
        for i in range(n + 1):
            outs[i][...] = (ins[2 * i][...] + ins[2 * i + 1][...]).astype(outs[i].dtype)

    in_specs, out_specs, out_shape, args = [], [], [], []
    for g, r in zip(grads, from_sibling):
        _, hr, cols = r.shape
        in_specs += [pl.BlockSpec((1, hr, cols), lambda k, c_ref: (k, c_ref[0], 0)),
                     pl.BlockSpec((1, hr, cols), lambda k, c_ref: (k, 0, 0))]
        out_specs.append(pl.BlockSpec((1, hr, cols), lambda k, c_ref: (k, 0, 0)))
        out_shape.append(jax.ShapeDtypeStruct(r.shape, BF16))
        args += [g, r]
    whole = pl.BlockSpec(small.shape, lambda k, c_ref: (0, 0))
    in_specs += [whole, whole]
    out_specs.append(whole)
    out_shape.append(jax.ShapeDtypeStruct(small.shape, F32))
    outs = pl.pallas_call(
        body, name="add_pair", out_shape=out_shape,
        grid_spec=pltpu.PrefetchScalarGridSpec(num_scalar_prefetch=1, grid=(N_CHIPS,), in_specs=in_specs,
                                               out_specs=out_specs),
        compiler_params=_params(dimension_semantics=("arbitrary",)),
    )(c.reshape(1), *args, small, small_sibling)
    return outs[:n], outs[n]


def _scatter_w_in(dw_in_e, from_sibling):
    hr = from_sibling.shape[1]
    shard = (N_CHIPS, hr, SHARD_COLS_IN)

    quarter = PROJ_EXT // N_CHIPS

    def body(g_in, r_in, out, g_buf, r_buf, p_buf, load_sems, send_sems, recv_sems, local_sems):
        x, y, c = _mesh_pos()
        mine = 2 * x + y

        def loads(q):
            cols = slice(q * quarter, (q + 1) * quarter)
            return (pltpu.make_async_copy(g_in.at[0, pl.ds(c * hr, hr), cols], g_buf.at[:, cols], load_sems.at[2 * q]),
                    pltpu.make_async_copy(r_in.at[0, :, cols], r_buf.at[:, cols], load_sems.at[2 * q + 1]))

        def between(src, dst, k, sender, peer):
            return pltpu.make_async_remote_copy(src_ref=src, dst_ref=dst, send_sem=send_sems.at[k], recv_sem=recv_sems.at[sender],
                                                device_id=(peer // 2, peer % 2, c), device_id_type=MESH)

        to_chip = lambda k: between(p_buf.at[k], out.at[mine], k, mine, jnp.int32(k))
        from_chip = lambda k: between(out.at[k], out.at[k], k, k, jnp.int32(k))
        keep = lambda k: pltpu.make_async_copy(p_buf.at[k], out.at[k], local_sems.at[0])

        for q in reversed(range(N_CHIPS)):
            for cp in loads(q):
                cp.start()
        for k in reversed(range(N_CHIPS)):
            for cp in loads(k):
                cp.wait()
            cols = slice(k * quarter, (k + 1) * quarter)
            g_buf[:, cols] += r_buf[:, cols]
            if k == 0:
                p_buf[0, :, 0:KPE_END] = g_buf[:, 0:KPE_END].astype(BF16)
                p_buf[0, :, KPE_END:SHARD_COLS_IN] = g_buf[:, KPE_END + KPE_PAD:SHARD_COLS_IN + KPE_PAD].astype(BF16)
            else:
                p_buf[k] = g_buf[:, SHARD_COLS_IN * k + KPE_PAD:SHARD_COLS_IN * (k + 1) + KPE_PAD].astype(BF16)
            pl.when(mine == k)(lambda k=k: keep(k).start())
            pl.when(mine != k)(lambda k=k: to_chip(k).start())

        for k in range(N_CHIPS):
            pl.when(mine == k)(lambda k=k: keep(k).wait())

            @pl.when(mine != k)
            def _(k=k):
                from_chip(k).wait_recv()
                to_chip(k).wait_send()

    return pl.pallas_call(
        body, name="scatter_grads", in_specs=[_ANY] * 2, out_specs=_ANY, out_shape=jax.ShapeDtypeStruct(shard, BF16),
        scratch_shapes=[pltpu.VMEM((hr, PROJ_EXT), F32)] * 2 + [pltpu.VMEM(shard, BF16)]
                       + [pltpu.SemaphoreType.DMA((2 * N_CHIPS,)), pltpu.SemaphoreType.DMA((N_CHIPS,)),
                          pltpu.SemaphoreType.DMA((N_CHIPS,)), pltpu.SemaphoreType.DMA((1,))],
        compiler_params=_params(),
    )(dw_in_e, from_sibling)


def _share_halves(halves):
    n = len(halves)

    def body(*refs):
        ins, outs, send_sems, recv_sems = refs[:n], refs[n:2 * n], refs[2 * n], refs[2 * n + 1]
        x, y, c = _mesh_pos()
        cps = [_remote(ins[i], outs[i], send_sems, recv_sems, i, (x, y, 1 - c)) for i in range(n)]
        for cp in cps:
            cp.start()
        for cp in cps:
            cp.wait()

    return pl.pallas_call(
        body, name="share_halves", in_specs=[_ANY] * n, out_specs=[_ANY] * n,
        out_shape=[jax.ShapeDtypeStruct(h.shape, h.dtype) for h in halves],
        scratch_shapes=[pltpu.SemaphoreType.DMA((n,)), pltpu.SemaphoreType.DMA((n,))],
    )(*halves)


SHARD_COLS_IN = IN_TOTAL // N_CHIPS
KPE_END = Q_LORA + KV_LORA + ROPE
KPE_PAD = PROJ_EXT - IN_TOTAL


def _by_cols(a):
    return a.transpose(1, 0, 2).reshape(a.shape[1], N_CHIPS * a.shape[2])


def _assemble_early(c_in, c_uq, c_ukv, c_conv):
    return c_in, c_uq, c_ukv, _by_cols(c_conv).astype(F32)


def _assemble_late(c_o, c_pl, c_plg):
    return c_o.reshape(D_MODEL, D_MODEL), _by_cols(c_pl), c_plg.reshape(D_MODEL, D_MODEL)


def _split_late(dw_o, dw_pl, dw_plg):
    chip_major = lambda a: a.reshape(a.shape[0], N_CHIPS, a.shape[1] // N_CHIPS).transpose(1, 0, 2)
    return [dw_o.reshape(N_CHIPS, D_MODEL // N_CHIPS, D_MODEL), chip_major(dw_pl),
            dw_plg.reshape(N_CHIPS, D_MODEL // N_CHIPS, D_MODEL)]


def _local_step(x, p, pos, tgt, gains, early, late_shards, late_gathered, tm, tq):
    c_in, w_uq_e, w_ukv, conv_w = early
    g_in, g_cq, g_ckv, g_q, g_k, g_oa, g_oc, g_pl = gains
    T = x.shape[0]
    zpad = lambda a, n: jnp.concatenate([a, jnp.zeros(a.shape[:-1] + (n,), a.dtype)], axis=-1)
    gq, gk = zpad(g_q, HEAD_PAD - QK_DIM), zpad(g_k, HEAD_PAD - QK_DIM)
    inv_freq = 1.0 / (ROPE_THETA ** (jnp.arange(0, ROPE, 2, dtype=F32) / ROPE))
    invf = jnp.concatenate([inv_freq, inv_freq, jnp.zeros((64,), F32)]).reshape(1, LANES)
    sgn = jnp.concatenate([-jnp.ones((32,), F32), jnp.ones((32,), F32), jnp.zeros((64,), F32)]).reshape(1, LANES)

    (proj, q, k, v, w_in_e), gathered = _fwd_proj(x, pos, g_in, c_in, g_cq, w_uq_e, g_ckv, w_ukv, gq, gk, invf, sgn,
                                                  late_shards, min(2 * tm, T))
    w_o, w_pl, w_plg = _assemble_late(*(gathered if late_shards else late_gathered))
    o, lse = _attn_fwd(q, k, v, tq)
    (dx1, do, delta, dtail, du, dw_o, dw_pl, dw_plg, dg_oa, dg_oc, dg_pl, dconv, loss) = _tail(
        x, o, proj, p, tgt, g_oa, g_oc, g_pl, conv_w, w_o, w_pl, w_plg, tm)
    late_grads = _split_late(dw_o, dw_pl, dw_plg)
    (dq, dk, dv), late_sibling = _attn_bwd(q, k, v, do, lse, delta, tq, late_grads)
    (gx, h, dproj, dw_uq_e, dw_ukv, dg_in, dg_cq, dg_ckv, dgq, dgk) = _bwd_proj(
        x, dx1, pos, proj, dq, dk, dv, dtail, du, g_in, w_in_e, g_cq, w_uq_e, g_ckv, w_ukv, gq, gk, conv_w, invf, sgn, tm)
    wgrads = [dw_uq_e[:, :, :QK_DIM], dw_ukv, *late_grads]
    ggrads = (dg_in, dg_cq, dg_ckv, dgq, dgk, dg_oa, dg_oc, dg_pl)
    return loss, gx, (h, dproj), wgrads, late_sibling, ggrads, dconv


def kernel(x, p, positions, g_in, w_in, g_cq, w_uq, g_ckv, w_ukv, g_q, g_k, conv_w, g_oa, g_oc, w_o, w_pl, w_plg, g_pl, loss_target, m_g_in, m_w_in, m_g_cq, m_w_uq, m_g_ckv, m_w_ukv, m_g_q, m_g_k, m_conv_w, m_g_oa, m_g_oc, m_w_o, m_w_pl, m_w_plg, m_g_pl, v_g_in, v_w_in, v_g_cq, v_w_uq, v_g_ckv, v_w_ukv, v_g_q, v_g_k, v_conv_w, v_g_oa, v_g_oc, v_w_o, v_w_pl, v_w_plg, v_g_pl):
    T = x.shape[1]
    c = lax.axis_index("c")
    chip = 2 * lax.axis_index("x") + lax.axis_index("y")
    gains = [g.reshape(1, -1) for g in (g_in, g_cq, g_ckv, g_q, g_k, g_oa, g_oc, g_pl)]

    transposed = ("w_in", "w_uq")
    early = _assemble_early(*_gather_weights([w_in[0].T, w_uq[0].T, w_ukv[0], conv_w[0]], len(transposed)))

    loss, gx, (h_t, dproj), others_cm, late_sibling, ggrads, dconv = _local_step(
        x[0], p[0, 0], positions.reshape(1, T), loss_target[0], gains, early, [w_o[0], w_pl[0], w_plg[0]], None, 256, 512)

    small_parts = [a.reshape(-1, LANES) for a in (*ggrads, loss, dconv)]
    small_rows = [a.shape[0] for a in small_parts]
    tile_rows = [-(-r // 8) * 8 for r in small_rows]
    tile_rows[-1] += -sum(tile_rows) % 16
    small = jnp.concatenate([jnp.pad(a, ((0, t - r), (0, 0))) for a, r, t in zip(small_parts, small_rows, tile_rows)])
    n_early = len(others_cm) - len(late_sibling)
    early_sibling, (small_sibling,) = _swap_halves(others_cm[:n_early], [small], "pair_grads")
    chip_parts, chip_small = _add_pair(others_cm, [*early_sibling, *late_sibling], small, small_sibling, c)
    dw_in_e, w_in_sibling, exchanged = _matmul_acc(h_t, dproj, min(4096, T), 1024, [*chip_parts, chip_small])
    by_chip = [_scatter_w_in(dw_in_e[None], w_in_sibling[None]), *exchanged[:-1]]
    halves, small_total = _add_chips(by_chip, exchanged[-1])
    other_halves = _share_halves(halves)

    gg, off = [], 0
    for rows, tiled in zip(small_rows, tile_rows):
        gg.append(small_total[off:off + rows].reshape(1, -1))
        off += tiled
    loss_out = gg[8][0, 0]
    conv_total = gg[9].reshape(3, CONV_W)
    conv_g = lax.dynamic_slice(conv_total, (0, chip * (CONV_W // N_CHIPS)), (3, CONV_W // N_CHIPS))
    g_by_name = dict(g_in=gg[0], g_cq=gg[1], g_ckv=gg[2], g_q=gg[3][:, :QK_DIM], g_k=gg[4][:, :QK_DIM], conv_w=conv_g,
                     g_oa=gg[5], g_oc=gg[6], g_pl=gg[7])
    half_by_name = dict(zip(("w_in", "w_uq", "w_ukv", "w_o", "w_pl", "w_plg"), zip(halves, other_halves)))
    weights = dict(g_in=g_in, w_in=w_in, g_cq=g_cq, w_uq=w_uq, g_ckv=g_ckv, w_ukv=w_ukv, g_q=g_q, g_k=g_k,
                   conv_w=conv_w, g_oa=g_oa, g_oc=g_oc, w_o=w_o, w_pl=w_pl, w_plg=w_plg, g_pl=g_pl)
    ms = dict(g_in=m_g_in, w_in=m_w_in, g_cq=m_g_cq, w_uq=m_w_uq, g_ckv=m_g_ckv, w_ukv=m_w_ukv, g_q=m_g_q, g_k=m_g_k,
              conv_w=m_conv_w, g_oa=m_g_oa, g_oc=m_g_oc, w_o=m_w_o, w_pl=m_w_pl, w_plg=m_w_plg, g_pl=m_g_pl)
    vs = dict(g_in=v_g_in, w_in=v_w_in, g_cq=v_g_cq, w_uq=v_w_uq, g_ckv=v_g_ckv, w_ukv=v_w_ukv, g_q=v_g_q, g_k=v_g_k,
              conv_w=v_conv_w, g_oa=v_g_oa, g_oc=v_g_oc, w_o=v_w_o, w_pl=v_w_pl, w_plg=v_w_plg, g_pl=v_g_pl)
    names = list(weights)
    flat = lambda a: a.reshape(-1, a.shape[-1])
    small_names = list(g_by_name)
    one_row = lambda a: a.reshape(1, -1)
    small_out = _adamw_small([one_row(weights[n]) for n in small_names], [one_row(g_by_name[n]) for n in small_names],
                             [one_row(ms[n]) for n in small_names], [one_row(vs[n]) for n in small_names])
    results = {n: (g_by_name[n], *(out[i] for out in small_out)) for i, n in enumerate(small_names)}
    for n in half_by_name:
        shard = (lambda a: a[0].T) if n in transposed else flat
        out = _adamw_halves(shard(weights[n]), *half_by_name[n], shard(ms[n]), shard(vs[n]), c, "adamw_" + n,
                            n in transposed)
        results[n] = [a.T for a in out] if n in transposed else out
    per_kind = [[results[n][kind].reshape(weights[n].shape) for n in names] for kind in range(4)]
    return (loss_out, gx.reshape(x.shape), *per_kind[0], *per_kind[1], *per_kind[2], *per_kind[3])
```

```python
import math

import jax
import jax.numpy as jnp
from jax import lax
from jax.experimental import pallas as pl
from jax.experimental.pallas import tpu as pltpu

F32 = jnp.float32
BF16 = jnp.bfloat16

D_MODEL = 1024
N_HEADS = 4
NOPE = 128
ROPE = 64
V_DIM = 128
QK_DIM = NOPE + ROPE
HEAD_PAD = 256
Q_LORA = 256
KV_LORA = 128
ATTN_W = 512
CONV_W = 512
PLE = 256
IN_TOTAL = 3008
PROJ_EXT = 3072
ROPE_THETA = 10000.0
EPS = 1e-6
SCALE = 1.0 / math.sqrt(QK_DIM)
LOG2E = math.log2(math.e)
EXP2_SCALE = SCALE * LOG2E
NEG = -1e30
SOFTMAX_ROWS = 32
SUB_TILE = 256

LR, B1, B2, ADAM_EPS, WD, STEP = 0.001, 0.9, 0.999, 1e-08, 0.01, 10

N_CHIPS = 4
LANES = 128
VMEM_LIMIT = 56 * 1024 * 1024
MESH = pl.DeviceIdType.MESH


def _params(**kw):
    return pltpu.CompilerParams(vmem_limit_bytes=VMEM_LIMIT, **kw)


def _inv_rms(x, n):
    return lax.rsqrt(jnp.sum(x * x, axis=-1, keepdims=True) / n + EPS)


def _lane_sum(a):
    folded = a[:, 0:LANES]
    for c0 in range(LANES, a.shape[1], LANES):
        folded = folded + a[:, c0:c0 + LANES]
    head = folded.astype(BF16)
    tail = (folded - head.astype(F32)).astype(BF16)
    return _dot(jnp.concatenate([head, tail], axis=1), jnp.ones((2 * LANES, LANES), BF16))


def _inv_rms_mxu(x):
    return lax.rsqrt(_lane_sum(x * x) / x.shape[1] + EPS)


def _rep(r, width):
    return r if width == LANES else jnp.tile(r, (1, width // LANES))


def _sigmoid(z):
    return jax.nn.sigmoid(z)


def _swap_rope_halves(b):
    lane = lax.broadcasted_iota(jnp.int32, b.shape, 1)
    swapped = jnp.where(lane < 32, pltpu.roll(b, 96, 1), pltpu.roll(b, 32, 1))
    return jnp.where(lane < ROPE, swapped, 0.0)


def _dot(a, b):
    return jnp.dot(a, b, preferred_element_type=F32)


def _dot_nt(a, b):
    return lax.dot_general(a, b, (((1,), (1,)), ((), ())), preferred_element_type=F32)


def _dot_tn(a, b):
    return lax.dot_general(a, b, (((0,), (0,)), ((), ())), preferred_element_type=F32)


def _colsum(a):
    return jnp.sum(a, axis=0, keepdims=True)


def _store_transposed(src_ref, dst_ref):
    r, c = src_ref.shape
    for r0 in range(0, r, LANES):
        h = min(LANES, r - r0)
        for c0 in range(0, c, LANES):
            w = min(LANES, c - c0)
            piece = src_ref[r0:r0 + h, c0 + w - LANES:c0 + w]
            if h < LANES:
                piece = jnp.concatenate([piece, jnp.zeros((LANES - h, LANES), piece.dtype)], axis=0)
            dst_ref[c0:c0 + w, r0:r0 + h] = piece.T[LANES - w:, 0:h].astype(dst_ref.dtype)


def _full(shape):
    return pl.BlockSpec(shape, lambda *_: (0,) * len(shape))


def _round_robin(chains, width):
    waiting, active = list(chains), []
    while waiting or active:
        while waiting and len(active) < width:
            active.append(waiting.pop(0))
        for chain in list(active):
            if next(chain, _DONE) is _DONE:
                active.remove(chain)


_DONE = object()


def _rope_tables(pos_ref, invf_ref, sgn_ref):
    pos = jnp.broadcast_to(pos_ref[...].astype(F32), (LANES, pos_ref.shape[1])).T
    ang = pos * invf_ref[...]
    return jnp.cos(ang), jnp.sin(ang) * sgn_ref[...]


def _fwd_proj(x, pos, g_in, c_in, g_cq, w_uq, g_ckv, w_ukv, gq, gk, invf, sgn, late_shards, tm):
    T = x.shape[0]
    nt = T // tm
    n_late = len(late_shards)
    ts = min(SUB_TILE, tm)

    def body(x_ref, pos_ref, g_in_ref, c_in_ref, g_cq_ref, w_uq_ref, g_ckv_ref, w_ukv_ref, gq_ref, gk_ref,
             invf_ref, sgn_ref, *rest):
        late_in, (proj_ref, q_ref, k_ref, v_ref, w_in_ref) = rest[:n_late], rest[n_late:n_late + 5]
        late_out, late_scratch = rest[n_late + 5:2 * n_late + 5], rest[2 * n_late + 5:]
        i = pl.program_id(0)

        @pl.when(i == 0)
        def _():
            w_in_ref[:, 0:KPE_END] = c_in_ref[0, :, 0:KPE_END]
            w_in_ref[:, KPE_END:KPE_END + KPE_PAD] = jnp.zeros((D_MODEL, KPE_PAD), BF16)
            w_in_ref[:, KPE_END + KPE_PAD:SHARD_COLS_IN + KPE_PAD] = c_in_ref[0, :, KPE_END:SHARD_COLS_IN]
            for chip in range(1, N_CHIPS):
                w_in_ref[:, SHARD_COLS_IN * chip + KPE_PAD:SHARD_COLS_IN * (chip + 1) + KPE_PAD] = c_in_ref[chip]

        if n_late:
            start, forward, drain = _gather_steps([s.shape for s in late_shards], late_in, late_out,
                                                  late_scratch[:n_late], *late_scratch[n_late:])
            pl.when(i == 0)(start)
            pl.when(i == nt // 2)(forward)

        for r0 in range(0, tm, ts):
            rows = slice(r0, r0 + ts)
            xv = x_ref[rows, :]
            h = (xv * _rep(_inv_rms_mxu(xv), D_MODEL) * g_in_ref[...]).astype(BF16)
            lat = _dot(h, w_in_ref[:, 0:512])
            proj_ref[rows, 0:512] = lat
            c_q = lat[:, 0:Q_LORA]
            cqn = (c_q * _rep(_inv_rms_mxu(c_q), Q_LORA) * g_cq_ref[...]).astype(BF16)
            c_kv = lat[:, Q_LORA:Q_LORA + KV_LORA]
            ckvn = (c_kv * _inv_rms_mxu(c_kv) * g_ckv_ref[...]).astype(BF16)
            kpe = lat[:, 384:512]
            kpe_sq = kpe * kpe
            cos_b, sin_b = _rope_tables(pos_ref.at[:, rows], invf_ref, sgn_ref)
            gq_a, gq_b = gq_ref[:, 0:NOPE], gq_ref[:, NOPE:HEAD_PAD]
            gk_a, gk_b = gk_ref[:, 0:NOPE], gk_ref[:, NOPE:HEAD_PAD]

            def projections(rows=rows, h=h):
                for c0 in range(512, PROJ_EXT, 512):
                    proj_ref[rows, c0:c0 + 512] = _dot(h, w_in_ref[:, c0:c0 + 512])
                    yield

            def queries(hd, rows=rows, cqn=cqn, cos_b=cos_b, sin_b=sin_b, gq_a=gq_a, gq_b=gq_b):
                qh = _dot(cqn, w_uq_ref[hd])
                yield
                a, b = qh[:, 0:NOPE], qh[:, NOPE:HEAD_PAD]
                r = lax.rsqrt(_lane_sum(a * a + b * b) / QK_DIM + EPS)
                yield
                bn = b * r * gq_b
                q_ref[hd, rows, 0:NOPE] = (a * r * gq_a).astype(BF16)
                q_ref[hd, rows, NOPE:HEAD_PAD] = (bn * cos_b + _swap_rope_halves(bn) * sin_b).astype(BF16)
                yield

            def keys(hd, rows=rows, ckvn=ckvn, kpe=kpe, kpe_sq=kpe_sq, cos_b=cos_b, sin_b=sin_b, gk_a=gk_a, gk_b=gk_b):
                kvh = _dot(ckvn, w_ukv_ref[hd])
                yield
                ka = kvh[:, 0:NOPE]
                rk = lax.rsqrt(_lane_sum(ka * ka + kpe_sq) / QK_DIM + EPS)
                yield
                kbn = kpe * rk * gk_b
                k_ref[hd, rows, 0:NOPE] = (ka * rk * gk_a).astype(BF16)
                k_ref[hd, rows, NOPE:HEAD_PAD] = (kbn * cos_b + _swap_rope_halves(kbn) * sin_b).astype(BF16)
                v_ref[hd, rows, 0:V_DIM] = kvh[:, NOPE:HEAD_PAD].astype(BF16)
                v_ref[hd, rows, V_DIM:2 * V_DIM] = jnp.ones((ts, V_DIM), BF16)
                yield

            chains = [projections()]
            for hd in range(N_HEADS):
                chains += [queries(hd), keys(hd)]
            _round_robin(chains, 4)

        if n_late:
            pl.when(i == nt - 1)(drain)

    row = lambda i: (i, 0)
    head_rows = lambda i: (0, i, 0)
    outs = pl.pallas_call(
        body, name="fwd_proj", grid=(nt,),
        in_specs=[pl.BlockSpec((tm, D_MODEL), row), pl.BlockSpec((1, tm), lambda i: (0, i)), _full((1, D_MODEL)),
                  _full((N_CHIPS, D_MODEL, SHARD_COLS_IN)), _full((1, Q_LORA)), _full((N_HEADS, Q_LORA, HEAD_PAD)),
                  _full((1, KV_LORA)), _full((N_HEADS, KV_LORA, HEAD_PAD)), _full((1, HEAD_PAD)), _full((1, HEAD_PAD)),
                  _full((1, LANES)), _full((1, LANES))] + [_full(s.shape) for s in late_shards],
        out_specs=[pl.BlockSpec((tm, PROJ_EXT), row), pl.BlockSpec((N_HEADS, tm, HEAD_PAD), head_rows),
                   pl.BlockSpec((N_HEADS, tm, HEAD_PAD), head_rows), pl.BlockSpec((N_HEADS, tm, 2 * V_DIM), head_rows),
                   _full((D_MODEL, PROJ_EXT))] + [_ANY] * n_late,
        out_shape=[jax.ShapeDtypeStruct((T, PROJ_EXT), F32), jax.ShapeDtypeStruct((N_HEADS, T, HEAD_PAD), BF16),
                   jax.ShapeDtypeStruct((N_HEADS, T, HEAD_PAD), BF16), jax.ShapeDtypeStruct((N_HEADS, T, 2 * V_DIM), BF16),
                   jax.ShapeDtypeStruct((D_MODEL, PROJ_EXT), BF16)] + _gathered_shapes([s.shape for s in late_shards]),
        scratch_shapes=_gather_scratch([s.shape for s in late_shards]) if n_late else [],
        compiler_params=_params(dimension_semantics=("arbitrary",)),
    )(x, pos, g_in, c_in, g_cq, w_uq, g_ckv, w_ukv, gq, gk, invf, sgn, *late_shards)
    return outs[:5], outs[5:]


def _chunk_pipeline(n_loop, lag, matmuls, pointwise, accumulate, last):
    slots = lag + 1

    def iteration(t, slot, pending=True, ahead=True):
        if ahead:
            matmuls(jnp.minimum(t + lag, n_loop), (slot + lag) % slots)
        if pending:
            accumulate(t - lag, (slot + 1) % slots, False)
        pointwise(t, slot, False)

    def finish(slot, pending):
        pointwise(n_loop, slot, True)
        for back in range(pending, 0, -1):
            accumulate(n_loop - back, (slot - back) % slots, False)
        accumulate(n_loop, slot, True)
        last()

    for u in range(lag):
        matmuls(jnp.minimum(u, n_loop), u)
    for u in range(lag):
        pl.when(u < n_loop)(lambda u=u: iteration(u, u, pending=False))

    n_main = jnp.maximum(n_loop - lag, 0)

    def unrolled(tt, carry):
        for j in range(slots):
            iteration(lag + slots * tt + j, (lag + j) % slots)
        return carry

    lax.fori_loop(0, n_main // slots, unrolled, 0)
    rest = lax.rem(n_main, slots)
    t0 = n_loop - rest

    for r in range(slots):
        @pl.when(jnp.logical_and(n_loop >= lag, rest == r))
        def _():
            for j in range(r):
                iteration(t0 + j, (lag + j) % slots, ahead=j + lag <= r)
            finish((lag + r) % slots, lag)

    for short in range(lag):
        pl.when(n_loop == short)(lambda short=short: finish(short, short))


def _attn_fwd(q, k, v, tq):
    T = q.shape[1]
    tk = tq
    rc = min(SOFTMAX_ROWS, tq)

    def body(q_ref, k_ref, v_ref, o_ref, lse_ref, s0, s1, s2, p0, p1, p2, a0, a1, a2, m_ref, acc_ref):
        qi = pl.program_id(1)
        s_buf, p_buf, a_buf = (s0, s1, s2), (p0, p1, p2), (a0, a1, a2)

        def scores(t, slot):
            ks = pl.multiple_of(t * tk, tk)
            s_buf[slot][...] = _dot_nt(q_ref[0], k_ref[0, pl.ds(ks, tk), :])

        def blocks(masked):
            return ((0, tq // 2, tk // 2), (tq // 2, tq // 2, tk)) if masked else ((0, tq, tk),)

        def values(t, slot, masked):
            ks = pl.multiple_of(t * tk, tk)
            for q0, nq, nk in blocks(masked):
                rows = slice(q0, q0 + nq)
                acc_ref[rows, :] = (acc_ref[rows, :] * a_buf[slot][rows, :]
                                    + _dot(p_buf[slot][rows, 0:nk], v_ref[0, pl.ds(ks, nk), :]))

        def softmax(t, slot, masked):
            for q0, nq, nk in blocks(masked):
                rows = slice(q0, q0 + nq)
                s_all = s_buf[slot][rows, 0:nk]
                if masked:
                    row = lax.broadcasted_iota(jnp.int32, (nq, nk), 0) + q0
                    col = lax.broadcasted_iota(jnp.int32, (nq, nk), 1)
                    s_all = jnp.where(col <= row, s_all, NEG)
                    s_buf[slot][rows, 0:nk] = s_all
                m_old = m_ref[rows, :]
                m_new = jnp.maximum(m_old, jnp.max(s_all, axis=1, keepdims=True))
                a_buf[slot][rows, :] = jnp.exp2((m_old - m_new) * EXP2_SCALE)
                m_ref[rows, :] = m_new
                for r0 in range(0, nq, rc):
                    s = s_buf[slot][q0 + r0:q0 + r0 + rc, 0:nk]
                    p_buf[slot][q0 + r0:q0 + r0 + rc, 0:nk] = jnp.exp2((s - m_new[r0:r0 + rc, :]) * EXP2_SCALE).astype(BF16)

        def last():
            l = acc_ref[:, V_DIM:2 * V_DIM]
            o_ref[...] = acc_ref[:, 0:V_DIM] / l
            lse_ref[0] = (m_ref[...] * SCALE + jnp.log(l)).T[0:1, :]

        m_ref[...] = jnp.full_like(m_ref, NEG)
        acc_ref[...] = jnp.zeros_like(acc_ref)
        _chunk_pipeline(qi, 2, scores, softmax, values, last)

    return pl.pallas_call(
        body, name="attn_fwd", grid=(N_HEADS, T // tq),
        in_specs=[pl.BlockSpec((1, tq, HEAD_PAD), lambda h, i: (h, i, 0)),
                  pl.BlockSpec((1, T, HEAD_PAD), lambda h, i: (h, 0, 0)),
                  pl.BlockSpec((1, T, 2 * V_DIM), lambda h, i: (h, 0, 0))],
        out_specs=[pl.BlockSpec((tq, V_DIM), lambda h, i: (i, h)),
                   pl.BlockSpec((1, 1, tq), lambda h, i: (h, 0, i))],
        out_shape=[jax.ShapeDtypeStruct((T, ATTN_W), F32), jax.ShapeDtypeStruct((N_HEADS, 1, T), F32)],
        scratch_shapes=[pltpu.VMEM((tq, tk), F32)] * 3 + [pltpu.VMEM((tq, tk), BF16)] * 3
                       + [pltpu.VMEM((tq, 1), F32)] * 4 + [pltpu.VMEM((tq, 2 * V_DIM), F32)],
        compiler_params=_params(dimension_semantics=("arbitrary", "arbitrary")),
    )(q, k, v)


def _tail(x, o, proj, p, tgt, g_oa, g_oc, g_pl, conv_w, w_o, w_pl, w_plg, tm):
    T = x.shape[0]
    nt = T // tm

    def body(x_ref, o_ref, za_ref, cb_ref, cc_ref, cx_ref, zc_ref, cch_ref, cxh_ref, p_ref, tgt_ref,
             g_oa_ref, g_oc_ref, g_pl_ref, cw_ref, w_o_ref, w_pl_ref, w_plg_ref,
             dx1_ref, do_ref, delta_ref, dtail_ref, du_ref,
             dw_o_ref, dw_pl_ref, dw_plg_ref, dg_oa_ref, dg_oc_ref, dg_pl_ref, dcw_ref, loss_ref):
        i = pl.program_id(0)

        @pl.when(i == 0)
        def _():
            for r in (dw_o_ref, dw_pl_ref, dw_plg_ref, dg_oa_ref, dg_oc_ref, dg_pl_ref, dcw_ref, loss_ref):
                r[...] = jnp.zeros_like(r)

        g_oa, g_oc, g_pl = g_oa_ref[...], g_oc_ref[...], g_pl_ref[...]
        w0, w1, w2 = cw_ref[0:1, :], cw_ref[1:2, :], cw_ref[2:3, :]

        xv, ov, za, cb, zc = x_ref[...], o_ref[...], za_ref[...], cb_ref[...], zc_ref[...]
        pb = p_ref[...].astype(BF16)
        pp = _dot(pb, w_pl_ref[...])

        sa = _sigmoid(za)
        silu_a = za * sa
        ga = ov * silu_a
        ra = _inv_rms(ga, ATTN_W)
        xa = ga * ra
        ya = (xa * g_oa).astype(BF16)
        x1_a = _dot(ya, w_o_ref[0:ATTN_W, :])
        v = cc_ref[...] * cx_ref[...]
        not_first = jnp.where(i > 0, 1.0, 0.0)
        hv6 = cch_ref[6:7, :] * cxh_ref[6:7, :] * not_first
        hv7 = cch_ref[7:8, :] * cxh_ref[7:8, :] * not_first
        row = lax.broadcasted_iota(jnp.int32, v.shape, 0)
        v1 = jnp.where(row == 0, hv7, pltpu.roll(v, 1, 0))
        v2 = jnp.where(row == 0, hv6, jnp.where(row == 1, hv7, pltpu.roll(v, 2, 0)))
        u = w0 * v2 + w1 * v1 + w2 * v
        sc = _sigmoid(zc)
        silu_c = zc * sc
        gc = cb * u * silu_c
        rc = _inv_rms(gc, CONV_W)
        xc = gc * rc
        yc = (xc * g_oc).astype(BF16)
        x1 = xv + (x1_a + _dot(yc, w_o_ref[ATTN_W:D_MODEL, :]))
        r1 = _inv_rms(x1, D_MODEL)
        xh1 = x1 * r1
        n1 = (xh1 * g_pl).astype(BF16)
        gate = _sigmoid(_dot(n1, w_plg_ref[...]))
        err = x1 + gate * pp - tgt_ref[...]
        loss_ref[...] += 0.5 * jnp.sum(err * err) / D_MODEL
        dy = err / D_MODEL

        dpp = (dy * gate).astype(BF16)
        da = (dy * pp * gate * (1.0 - gate)).astype(BF16)
        dn1 = _dot_nt(da, w_plg_ref[...])
        dw_pl_ref[...] += _dot_tn(pb, dpp)
        dw_plg_ref[...] += _dot_tn(n1, da)
        dg_pl_ref[...] += _colsum(dn1 * xh1)
        dxh = dn1 * g_pl
        dx1 = dy + r1 * (dxh - xh1 * (jnp.sum(dxh * xh1, axis=-1, keepdims=True) / D_MODEL))
        dx1_ref[...] = dx1
        dx1b = dx1.astype(BF16)
        dya = _dot_nt(dx1b, w_o_ref[0:ATTN_W, :])
        dyc = _dot_nt(dx1b, w_o_ref[ATTN_W:D_MODEL, :])

        dw_o_ref[0:ATTN_W, :] += _dot_tn(ya, dx1b)
        dg_oa_ref[...] += _colsum(dya * xa)
        dxa = dya * g_oa
        dga = ra * (dxa - xa * (jnp.sum(dxa * xa, axis=-1, keepdims=True) / ATTN_W))
        do = (dga * silu_a).astype(BF16)
        do_ref[...] = do
        dof = do.astype(F32) * ov
        for hd in range(N_HEADS):
            delta_ref[hd] = _lane_sum(dof[:, hd * V_DIM:(hd + 1) * V_DIM]).T[0:1, :]
        dtail_ref[:, 0:512] = (dga * ov * (sa * (1.0 + za * (1.0 - sa)))).astype(BF16)

        dw_o_ref[ATTN_W:D_MODEL, :] += _dot_tn(yc, dx1b)
        dg_oc_ref[...] += _colsum(dyc * xc)
        dxc = dyc * g_oc
        dgc = rc * (dxc - xc * (jnp.sum(dxc * xc, axis=-1, keepdims=True) / CONV_W))
        dtail_ref[:, 512:1024] = (dgc * u * silu_c).astype(BF16)
        du = dgc * cb * silu_c
        du_ref[...] = du
        dtail_ref[:, 1024:1536] = (dgc * cb * u * (sc * (1.0 + zc * (1.0 - sc)))).astype(BF16)
        dcw_ref[0:1, :] += _colsum(du * v2)
        dcw_ref[1:2, :] += _colsum(du * v1)
        dcw_ref[2:3, :] += _colsum(du * v)

    row = lambda i: (i, 0)
    col = lambda c: (lambda i: (i, c))
    halo = lambda c: (lambda i: (jnp.maximum(i * (tm // 8) - 1, 0), c))
    in_specs = [pl.BlockSpec((tm, D_MODEL), row), pl.BlockSpec((tm, ATTN_W), row)]
    in_specs += [pl.BlockSpec((tm, 512), col(c)) for c in (1, 2, 3, 4, 5)]
    in_specs += [pl.BlockSpec((8, 512), halo(3)), pl.BlockSpec((8, 512), halo(4))]
    in_specs += [pl.BlockSpec((tm, PLE), row), pl.BlockSpec((tm, D_MODEL), row),
                 _full((1, ATTN_W)), _full((1, CONV_W)), _full((1, D_MODEL)), _full((3, CONV_W)),
                 _full((D_MODEL, D_MODEL)), _full((PLE, D_MODEL)), _full((D_MODEL, D_MODEL))]
    out_specs = [pl.BlockSpec((tm, D_MODEL), row), pl.BlockSpec((tm, ATTN_W), row),
                 pl.BlockSpec((N_HEADS, 1, tm), lambda i: (0, 0, i)), pl.BlockSpec((tm, 1536), row),
                 pl.BlockSpec((tm, CONV_W), row),
                 _full((D_MODEL, D_MODEL)), _full((PLE, D_MODEL)), _full((D_MODEL, D_MODEL)),
                 _full((1, ATTN_W)), _full((1, CONV_W)), _full((1, D_MODEL)), _full((3, CONV_W)), _full((1, LANES))]
    out_shape = [jax.ShapeDtypeStruct((T, D_MODEL), F32), jax.ShapeDtypeStruct((T, ATTN_W), BF16),
                 jax.ShapeDtypeStruct((N_HEADS, 1, T), F32), jax.ShapeDtypeStruct((T, 1536), BF16),
                 jax.ShapeDtypeStruct((T, CONV_W), F32),
                 jax.ShapeDtypeStruct((D_MODEL, D_MODEL), F32), jax.ShapeDtypeStruct((PLE, D_MODEL), F32),
                 jax.ShapeDtypeStruct((D_MODEL, D_MODEL), F32),
                 jax.ShapeDtypeStruct((1, ATTN_W), F32), jax.ShapeDtypeStruct((1, CONV_W), F32),
                 jax.ShapeDtypeStruct((1, D_MODEL), F32), jax.ShapeDtypeStruct((3, CONV_W), F32),
                 jax.ShapeDtypeStruct((1, LANES), F32)]
    return pl.pallas_call(
        body, name="tail", grid=(nt,), in_specs=in_specs, out_specs=out_specs, out_shape=out_shape,
        compiler_params=_params(dimension_semantics=("arbitrary",)),
    )(x, o, proj, proj, proj, proj, proj, proj, proj, p, tgt, g_oa, g_oc, g_pl, conv_w, w_o, w_pl, w_plg)


def _attn_bwd(q, k, v, do, lse_row, delta_row, tk, swap):
    T = q.shape[1]
    tq = tk
    nq = T // tq
    rc = min(SOFTMAX_ROWS, tk)
    hk, hq = tk // 2, tq // 2
    n_swap = len(swap)

    def body(q_ref, k_ref, v_ref, do_ref, lse_ref, dl_ref, *rest):
        swap_in, (dq_ref, dk_ref, dv_ref), rest = rest[:n_swap], rest[n_swap:n_swap + 3], rest[n_swap + 3:]
        swap_out, (s0, s1, d0, d1, p0, p1, g0, g1, dk_acc, dv_acc), sems = rest[:n_swap], rest[n_swap:n_swap + 10], rest[n_swap + 10:]
        kj = pl.program_id(1)
        s_buf, dp_buf, p_buf, g_buf = (s0, s1), (d0, d1), (p0, p1), (g0, g1)

        if n_swap:
            start, drain = _swap_steps(n_swap, swap_in, swap_out, *sems)
            pl.when(jnp.logical_and(pl.program_id(0) == 0, kj == 0))(start)

        @pl.when(kj == 0)
        def _():
            dq_ref[...] = jnp.zeros_like(dq_ref)

        def q_start(t):
            return pl.multiple_of((nq - 1 - t) * tq, tq)

        def matmuls(t, slot):
            qs = q_start(t)
            s_buf[slot][...] = _dot_nt(k_ref[0], q_ref[0, pl.ds(qs, tq), :])
            dp_buf[slot][...] = _dot_nt(v_ref[0], do_ref[pl.ds(qs, tq), :])

        def pointwise(t, slot, masked):
            qs = q_start(t)
            lse2 = lse_ref[0, :, pl.ds(qs, tq)] * LOG2E
            dl = dl_ref[0, :, pl.ds(qs, tq)]
            for r0 in range(0, tk, rc):
                c0 = r0 // hk * hq if masked else 0
                rows, cols = slice(r0, r0 + rc), slice(c0, tq)
                st = s_buf[slot][rows, cols]
                if masked:
                    row = lax.broadcasted_iota(jnp.int32, (rc, tq - c0), 0) + r0
                    col = lax.broadcasted_iota(jnp.int32, (rc, tq - c0), 1) + c0
                    st = jnp.where(row <= col, st, NEG)
                pt = jnp.exp2(st * EXP2_SCALE - lse2[:, cols])
                p_buf[slot][rows, cols] = pt.astype(BF16)
                g_buf[slot][rows, cols] = (pt * (dp_buf[slot][rows, cols] - dl[:, cols]) * SCALE).astype(BF16)

        def accumulate(t, slot, masked):
            qs = q_start(t)
            p, g = p_buf[slot], g_buf[slot]
            if not masked:
                dv_acc[...] += _dot(p[...], do_ref[pl.ds(qs, tq), :])
                dk_acc[...] += _dot(g[...], q_ref[0, pl.ds(qs, tq), :])
                dq_ref[0, pl.ds(qs, tq), :] += _dot_tn(g[...], k_ref[0])
                return
            q2 = pl.multiple_of(qs + hq, hq)
            dv_acc[0:hk, :] += _dot(p[0:hk, :], do_ref[pl.ds(qs, tq), :])
            dv_acc[hk:tk, :] += _dot(p[hk:tk, hq:tq], do_ref[pl.ds(q2, hq), :])
            dk_acc[0:hk, :] += _dot(g[0:hk, :], q_ref[0, pl.ds(qs, tq), :])
            dk_acc[hk:tk, :] += _dot(g[hk:tk, hq:tq], q_ref[0, pl.ds(q2, hq), :])
            dq_ref[0, pl.ds(qs, hq), :] += _dot_tn(g[0:hk, 0:hq], k_ref[0, 0:hk, :])
            dq_ref[0, pl.ds(q2, hq), :] += _dot_tn(g[:, hq:tq], k_ref[0])

        def last():
            dk_ref[0] = dk_acc[...]
            dv_ref[0] = dv_acc[...]

        dk_acc[...] = jnp.zeros_like(dk_acc)
        dv_acc[...] = jnp.zeros_like(dv_acc)
        _chunk_pipeline(nq - 1 - kj, 1, matmuls, pointwise, accumulate, last)

        if n_swap:
            pl.when(jnp.logical_and(pl.program_id(0) == N_HEADS - 1, kj == T // tk - 1))(drain)

    outs = pl.pallas_call(
        body, name="attn_bwd", grid=(N_HEADS, T // tk),
        in_specs=[pl.BlockSpec((1, T, HEAD_PAD), lambda h, j: (h, 0, 0)),
                  pl.BlockSpec((1, tk, HEAD_PAD), lambda h, j: (h, j, 0)),
                  pl.BlockSpec((1, tk, V_DIM), lambda h, j: (h, j, 0)),
                  pl.BlockSpec((T, V_DIM), lambda h, j: (0, h)),
                  pl.BlockSpec((1, 1, T), lambda h, j: (h, 0, 0)),
                  pl.BlockSpec((1, 1, T), lambda h, j: (h, 0, 0))] + [_ANY] * n_swap,
        out_specs=[pl.BlockSpec((1, T, HEAD_PAD), lambda h, j: (h, 0, 0)),
                   pl.BlockSpec((1, tk, HEAD_PAD), lambda h, j: (h, j, 0)),
                   pl.BlockSpec((1, tk, V_DIM), lambda h, j: (h, j, 0))] + [_ANY] * n_swap,
        out_shape=[jax.ShapeDtypeStruct((N_HEADS, T, HEAD_PAD), F32), jax.ShapeDtypeStruct((N_HEADS, T, HEAD_PAD), F32),
                   jax.ShapeDtypeStruct((N_HEADS, T, V_DIM), F32)] + _swapped_shapes(swap, []),
        scratch_shapes=[pltpu.VMEM((tk, tq), F32)] * 4 + [pltpu.VMEM((tk, tq), BF16)] * 4
                       + [pltpu.VMEM((tk, HEAD_PAD), F32), pltpu.VMEM((tk, V_DIM), F32)]
                       + ([pltpu.SemaphoreType.DMA((n_swap,))] * 2 if n_swap else []),
        compiler_params=_params(dimension_semantics=("arbitrary", "arbitrary")),
    )(q, k, v, do, lse_row, delta_row, *swap)
    return outs[:3], outs[3:]


def _bwd_proj(x, dx1, pos, proj, dq, dk, dv, dtail, du, g_in, w_in, g_cq, w_uq, g_ckv, w_ukv, gq, gk, conv_w,
              invf, sgn, tm):
    T = x.shape[0]
    nt = T // tm

    ts = min(SUB_TILE, tm)

    def body(x_ref, dx1_ref, pos_ref, lat_ref, cc_ref, cx_ref, dq_ref, dk_ref, dv_ref, dtail_ref, du_ref, dun_ref, *rest):
        consts, (gx_ref, h_ref, dproj_ref), sums = rest[:11], rest[11:14], rest[14:]
        cw_ref = consts[8]
        i = pl.program_id(0)

        @pl.when(i == 0)
        def _():
            for r in sums:
                r[...] = jnp.zeros_like(r)

        du_v = du_ref[...]
        not_last = jnp.where(i < nt - 1, 1.0, 0.0)
        nx0 = dun_ref[0:1, :] * not_last
        nx1 = dun_ref[1:2, :] * not_last
        row = lax.broadcasted_iota(jnp.int32, du_v.shape, 0)
        du1 = jnp.where(row == tm - 1, nx0, pltpu.roll(du_v, tm - 1, 0))
        du2 = jnp.where(row == tm - 2, nx0, jnp.where(row == tm - 1, nx1, pltpu.roll(du_v, tm - 2, 0)))
        dvc = cw_ref[2:3, :] * du_v + cw_ref[1:2, :] * du1 + cw_ref[0:1, :] * du2
        dproj_ref[:, 1536:2048] = (dvc * cx_ref[...]).astype(BF16)
        dproj_ref[:, 2048:2560] = (dvc * cc_ref[...]).astype(BF16)

        for r0 in range(0, tm, ts):
            rows = slice(r0, r0 + ts)
            work(x_ref.at[rows, :], dx1_ref.at[rows, :], pos_ref.at[:, rows], lat_ref.at[rows, :],
                 dq_ref.at[:, rows, :], dk_ref.at[:, rows, :], dv_ref.at[:, rows, :], dtail_ref.at[rows, :], *consts,
                 gx_ref.at[rows, :], h_ref.at[:, rows], dproj_ref.at[rows, :], *sums)

    def work(x_ref, dx1_ref, pos_ref, lat_ref, dq_ref, dk_ref, dv_ref, dtail_ref,
             g_in_ref, w_in_ref, g_cq_ref, w_uq_ref, g_ckv_ref, w_ukv_ref, gq_ref, gk_ref, cw_ref, invf_ref, sgn_ref,
             gx_ref, h_ref, dproj_ref, dw_uq_ref, dw_ukv_ref, dg_in_ref, dg_cq_ref, dg_ckv_ref, dgq_ref, dgk_ref):
        xv = x_ref[...]
        r0 = _rep(_inv_rms_mxu(xv), D_MODEL)
        xh0 = xv * r0
        g_in = g_in_ref[...]
        h_ref[...] = (xh0 * g_in).astype(BF16).T

        c_q = lat_ref[:, 0:Q_LORA]
        rq = _rep(_inv_rms_mxu(c_q), Q_LORA)
        xq = c_q * rq
        g_cq = g_cq_ref[...]
        cqn = (xq * g_cq).astype(BF16)
        c_kv = lat_ref[:, Q_LORA:Q_LORA + KV_LORA]
        rkv = _inv_rms_mxu(c_kv)
        xkv = c_kv * rkv
        g_ckv = g_ckv_ref[...]
        ckvn = (xkv * g_ckv).astype(BF16)
        kpe = lat_ref[:, 384:512]
        kpe_sq = kpe * kpe
        cos_b, sin_b = _rope_tables(pos_ref, invf_ref, sgn_ref)
        gq_a, gq_b = gq_ref[:, 0:NOPE], gq_ref[:, NOPE:HEAD_PAD]
        gk_a, gk_b = gk_ref[:, 0:NOPE], gk_ref[:, NOPE:HEAD_PAD]

        dproj_ref[:, 512:1536] = dtail_ref[:, 0:1024]
        dproj_ref[:, 2560:3072] = dtail_ref[:, 1024:1536]

        def dh_part(c0):
            return _dot_nt(dproj_ref[:, c0:c0 + 512], w_in_ref[:, c0:c0 + 512])

        later_chunks = ((512,), (1024,), (1536, 2048), (2560,))
        dh = jnp.zeros((ts, D_MODEL), F32)
        acc = dict(dh=dh, dkpe=jnp.zeros((ts, LANES), F32), dcqn=jnp.zeros((ts, Q_LORA), F32),
                   dckvn=jnp.zeros((ts, KV_LORA), F32))

        def dh_chunks():
            for chunks in later_chunks:
                for chunk in chunks:
                    acc["dh"] = acc["dh"] + dh_part(chunk)
                    yield

        def queries(hd):
            qh = _dot(cqn, w_uq_ref[hd])
            yield
            a, b = qh[:, 0:NOPE], qh[:, NOPE:HEAD_PAD]
            r = lax.rsqrt(_lane_sum(a * a + b * b) / QK_DIM + EPS)
            yield
            xa, xb = a * r, b * r
            dan = dq_ref[hd, :, 0:NOPE]
            dbr = dq_ref[hd, :, NOPE:HEAD_PAD]
            dbn = dbr * cos_b + _swap_rope_halves(dbr * sin_b)
            yield
            dgq_ref[:, 0:NOPE] += _colsum(dan * xa)
            dgq_ref[:, NOPE:HEAD_PAD] += _colsum(dbn * xb)
            dxa, dxb = dan * gq_a, dbn * gq_b
            cq = _lane_sum(dxa * xa + dxb * xb) / QK_DIM
            yield
            dqh = jnp.concatenate([r * (dxa - xa * cq), r * (dxb - xb * cq)], axis=-1).astype(BF16)
            yield
            dw_uq_ref[hd] += _dot_tn(cqn, dqh)
            yield
            acc["dcqn"] = acc["dcqn"] + _dot_nt(dqh, w_uq_ref[hd])
            yield

        def keys(hd):
            kvh = _dot(ckvn, w_ukv_ref[hd])
            yield
            ka = kvh[:, 0:NOPE]
            rk = lax.rsqrt(_lane_sum(ka * ka + kpe_sq) / QK_DIM + EPS)
            yield
            xka, xkb = ka * rk, kpe * rk
            dkan = dk_ref[hd, :, 0:NOPE]
            dkbr = dk_ref[hd, :, NOPE:HEAD_PAD]
            dkbn = dkbr * cos_b + _swap_rope_halves(dkbr * sin_b)
            yield
            dgk_ref[:, 0:NOPE] += _colsum(dkan * xka)
            dgk_ref[:, NOPE:HEAD_PAD] += _colsum(dkbn * xkb)
            dxka, dxkb = dkan * gk_a, dkbn * gk_b
            ck = _lane_sum(dxka * xka + dxkb * xkb) / QK_DIM
            yield
            acc["dkpe"] = acc["dkpe"] + rk * (dxkb - xkb * ck)
            dkvh = jnp.concatenate([rk * (dxka - xka * ck), dv_ref[hd]], axis=-1).astype(BF16)
            yield
            dw_ukv_ref[hd] += _dot_tn(ckvn, dkvh)
            yield
            acc["dckvn"] = acc["dckvn"] + _dot_nt(dkvh, w_ukv_ref[hd])
            yield

        chains = [dh_chunks()]
        for hd in range(N_HEADS):
            chains += [queries(hd), keys(hd)]
        _round_robin(chains, 5)
        dh, dkpe, dcqn, dckvn = acc["dh"], acc["dkpe"], acc["dcqn"], acc["dckvn"]

        dg_cq_ref[...] += _colsum(dcqn * xq)
        dxq = dcqn * g_cq
        dproj_ref[:, 0:Q_LORA] = (rq * (dxq - xq * _rep(_lane_sum(dxq * xq) / Q_LORA, Q_LORA))).astype(BF16)
        dg_ckv_ref[...] += _colsum(dckvn * xkv)
        dxkv = dckvn * g_ckv
        dproj_ref[:, 256:384] = (rkv * (dxkv - xkv * (_lane_sum(dxkv * xkv) / KV_LORA))).astype(BF16)
        dproj_ref[:, 384:512] = dkpe.astype(BF16)
        dh = dh + dh_part(0)
        dg_in_ref[...] += _colsum(dh * xh0)
        dxh = dh * g_in
        gx_ref[...] = dx1_ref[...] + r0 * (dxh - xh0 * _rep(_lane_sum(dxh * xh0) / D_MODEL, D_MODEL))

    row = lambda i: (i, 0)
    col = lambda c: (lambda i: (i, c))
    head_rows = lambda i: (0, i, 0)
    nxt = lambda i: (jnp.minimum((i + 1) * (tm // 8), T // 8 - 1), 0)
    in_specs = [pl.BlockSpec((tm, D_MODEL), row), pl.BlockSpec((tm, D_MODEL), row), pl.BlockSpec((1, tm), lambda i: (0, i)),
                pl.BlockSpec((tm, 512), col(0)), pl.BlockSpec((tm, 512), col(3)), pl.BlockSpec((tm, 512), col(4)),
                pl.BlockSpec((N_HEADS, tm, HEAD_PAD), head_rows), pl.BlockSpec((N_HEADS, tm, HEAD_PAD), head_rows),
                pl.BlockSpec((N_HEADS, tm, V_DIM), head_rows), pl.BlockSpec((tm, 1536), row),
                pl.BlockSpec((tm, CONV_W), row), pl.BlockSpec((8, CONV_W), nxt),
                _full((1, D_MODEL)), _full((D_MODEL, PROJ_EXT)), _full((1, Q_LORA)), _full((N_HEADS, Q_LORA, HEAD_PAD)),
                _full((1, KV_LORA)), _full((N_HEADS, KV_LORA, HEAD_PAD)), _full((1, HEAD_PAD)), _full((1, HEAD_PAD)),
                _full((3, CONV_W)), _full((1, LANES)), _full((1, LANES))]
    out_specs = [pl.BlockSpec((tm, D_MODEL), row), pl.BlockSpec((D_MODEL, tm), lambda i: (0, i)),
                 pl.BlockSpec((tm, PROJ_EXT), row),
                 _full((N_HEADS, Q_LORA, HEAD_PAD)), _full((N_HEADS, KV_LORA, HEAD_PAD)),
                 _full((1, D_MODEL)), _full((1, Q_LORA)), _full((1, KV_LORA)), _full((1, HEAD_PAD)), _full((1, HEAD_PAD))]
    out_shape = [jax.ShapeDtypeStruct((T, D_MODEL), F32), jax.ShapeDtypeStruct((D_MODEL, T), BF16),
                 jax.ShapeDtypeStruct((T, PROJ_EXT), BF16),
                 jax.ShapeDtypeStruct((N_HEADS, Q_LORA, HEAD_PAD), F32), jax.ShapeDtypeStruct((N_HEADS, KV_LORA, HEAD_PAD), F32),
                 jax.ShapeDtypeStruct((1, D_MODEL), F32), jax.ShapeDtypeStruct((1, Q_LORA), F32),
                 jax.ShapeDtypeStruct((1, KV_LORA), F32), jax.ShapeDtypeStruct((1, HEAD_PAD), F32),
                 jax.ShapeDtypeStruct((1, HEAD_PAD), F32)]
    return pl.pallas_call(
        body, name="bwd_proj", grid=(nt,), in_specs=in_specs, out_specs=out_specs, out_shape=out_shape,
        compiler_params=_params(dimension_semantics=("arbitrary",)),
    )(x, dx1, pos, proj, proj, proj, dq, dk, dv, dtail, du, du, g_in, w_in, g_cq, w_uq, g_ckv, w_ukv, gq, gk, conv_w,
      invf, sgn)


def _matmul_acc(a, b, tt, tn, parts):
    M, T = a.shape
    N = b.shape[1]
    n = len(parts)
    grid = (N // tn, T // tt)
    hm = M // 2

    def body(a_ref, b_ref, *rest):
        part_refs, (o_ref, sib_ref), rest = rest[:n], rest[n:n + 2], rest[n + 2:]
        out_refs, (stage_ref, tile_send, tile_recv), sems = rest[:n], rest[n:n + 3], rest[n + 3:]
        j, t = pl.program_id(0), pl.program_id(1)
        if n:
            start, drain = _scatter_steps(part_refs, out_refs, *sems)
            pl.when(jnp.logical_and(j == 0, t == 0))(start)

        def to_sibling(jj):
            x, y, c = _mesh_pos()
            return _remote(stage_ref, sib_ref.at[:, pl.ds(pl.multiple_of(jj * tn, tn), tn)],
                           tile_send, tile_recv, jj, (x, y, 1 - c))

        @pl.when(t == 0)
        def _():
            o_ref[...] = jnp.zeros_like(o_ref)

        o_ref[...] += _dot(a_ref[...], b_ref[...])

        tile_done = t == grid[1] - 1
        pl.when(jnp.logical_and(tile_done, j > 0))(lambda: to_sibling(j - 1).wait())

        @pl.when(tile_done)
        def _():
            c = lax.axis_index("c")
            stage_ref[...] = o_ref[pl.ds(pl.multiple_of((1 - c) * hm, hm), hm), :]
            to_sibling(j).start()

        pl.when(jnp.logical_and(tile_done, j == grid[0] - 1))(lambda: to_sibling(j).wait())
        if n:
            pl.when(jnp.logical_and(j == grid[0] - 1, t == grid[1] - 1))(drain)

    sems = [pltpu.SemaphoreType.DMA((3 * n,)), pltpu.SemaphoreType.DMA((3 * n,)), pltpu.SemaphoreType.DMA((n,))]
    outs = pl.pallas_call(
        body, name="dw_in", grid=grid,
        in_specs=[pl.BlockSpec((M, tt), lambda j, t: (0, t)), pl.BlockSpec((tt, tn), lambda j, t: (t, j))] + [_ANY] * n,
        out_specs=[pl.BlockSpec((M, tn), lambda j, t: (0, j)), _ANY] + [_ANY] * n,
        out_shape=[jax.ShapeDtypeStruct((M, N), F32), jax.ShapeDtypeStruct((hm, N), F32)] + _scattered_shapes(parts),
        scratch_shapes=[pltpu.VMEM((hm, tn), F32)] + [pltpu.SemaphoreType.DMA((grid[0],))] * 2 + (sems if n else []),
        compiler_params=_params(dimension_semantics=("arbitrary", "arbitrary")),
    )(a, b, *parts)
    return outs[0], outs[1], outs[2:]


def _add_chips(parts, small_parts):
    arrays = list(parts) + [small_parts]

    def body(*refs):
        ins, outs = refs[:len(arrays)], refs[len(arrays):]
        for a_ref, o_ref in zip(ins, outs):
            part = lambda k: a_ref[k].astype(F32)
            o_ref[...] = ((part(0) + part(1)) + part(2)) + part(3)

    in_specs, out_specs, out_shape = [], [], []
    for a in arrays:
        _, rows, cols = a.shape
        in_specs.append(pl.BlockSpec((N_CHIPS, rows // 2, cols), lambda i: (0, i, 0)))
        out_specs.append(pl.BlockSpec((rows // 2, cols), lambda i: (i, 0)))
        out_shape.append(jax.ShapeDtypeStruct((rows, cols), F32))
    outs = pl.pallas_call(body, name="add_chips", grid=(2,), in_specs=in_specs, out_specs=out_specs,
                          out_shape=out_shape, compiler_params=_params(dimension_semantics=("arbitrary",)))(*arrays)
    return outs[:-1], outs[-1]


def _adamw_small(ws, gs, ms, vs):
    n = len(ws)

    def body(*refs):
        for i in range(n):
            w_ref, g_ref, m_ref, v_ref = (refs[k * n + i] for k in range(4))
            d_ref, nm_ref, nv_ref = (refs[(4 + k) * n + i] for k in range(3))
            _adamw_math(g_ref[...], w_ref, m_ref, v_ref, d_ref, nm_ref, nv_ref)

    shapes = [jax.ShapeDtypeStruct(w.shape, F32) for w in ws]
    outs = pl.pallas_call(body, name="adamw_small", out_shape=shapes * 3)(*ws, *gs, *ms, *vs)
    return outs[:n], outs[n:2 * n], outs[2 * n:]


def _adamw_math(gv, w_ref, m_ref, v_ref, d_ref, nm_ref, nv_ref):
    nm = B1 * m_ref[...] + (1.0 - B1) * gv
    nv = B2 * v_ref[...] + (1.0 - B2) * (gv * gv)
    m_hat = nm / (1.0 - B1 ** STEP)
    v_hat = nv / (1.0 - B2 ** STEP)
    d_ref[...] = -LR * (m_hat / (jnp.sqrt(v_hat) + ADAM_EPS) + WD * w_ref[...])
    nm_ref[...] = nm
    nv_ref[...] = nv


def _adamw_halves(w, mine, other, m, v, c, name, transposed):
    hr, cols = mine.shape

    def body(c_ref, w_ref, mine_ref, other_ref, m_ref, v_ref, g_ref, d_ref, nm_ref, nv_ref, *picked):
        gv = jnp.where(pl.program_id(0) == c_ref[0], mine_ref[...], other_ref[...])
        if transposed:
            picked[0][...] = gv
            _store_transposed(picked[0], g_ref)
            gv = g_ref[...]
        else:
            g_ref[...] = gv
        _adamw_math(gv, w_ref, m_ref, v_ref, d_ref, nm_ref, nv_ref)

    if transposed:
        half = pl.BlockSpec((cols, hr), lambda i, c_ref: (0, i))
    else:
        half = pl.BlockSpec((hr, cols), lambda i, c_ref: (i, 0))
    whole = pl.BlockSpec((hr, cols), lambda i, c_ref: (0, 0))
    shp = jax.ShapeDtypeStruct(w.shape, F32)
    return pl.pallas_call(
        body, name=name, out_shape=[shp] * 4,
        grid_spec=pltpu.PrefetchScalarGridSpec(num_scalar_prefetch=1, grid=(2,), in_specs=[half, whole, whole, half, half],
                                               out_specs=[half] * 4,
                                               scratch_shapes=[pltpu.VMEM((hr, cols), F32)] if transposed else []),
        compiler_params=_params(dimension_semantics=("arbitrary",)),
    )(c.reshape(1), w, mine, other, m, v)


_ANY = pl.BlockSpec(memory_space=pl.ANY)


def _mesh_pos():
    return lax.axis_index("x"), lax.axis_index("y"), lax.axis_index("c")


def _other_chips(x, y):
    return [(1 - x, y), (x, 1 - y), (1 - x, 1 - y)]


def _remote(src, dst, send_sems, recv_sems, k, to):
    return pltpu.make_async_remote_copy(src_ref=src, dst_ref=dst, send_sem=send_sems.at[k], recv_sem=recv_sems.at[k],
                                        device_id=to, device_id_type=MESH)


def _gather_weights(shards, n_transposed):
    n = len(shards)
    shapes = [s.shape[::-1] if i < n_transposed else s.shape for i, s in enumerate(shards)]

    def body(*refs):
        start, forward, drain = _gather_steps(shapes, refs[:n], refs[n:2 * n], refs[2 * n:3 * n], *refs[3 * n:])
        start()
        forward()
        drain()

    vmem = pl.BlockSpec(memory_space=pltpu.VMEM)
    return pl.pallas_call(
        body, name="gather_weights", in_specs=[vmem] * n, out_specs=[_ANY] * n,
        out_shape=_gathered_shapes(shapes), scratch_shapes=_gather_scratch(shapes), compiler_params=_params(),
    )(*shards)


def _travel_shape(shape):
    rows, cols = shape
    return (rows, HEAD_PAD if cols == QK_DIM else cols)


def _gathered_shapes(shapes):
    return [jax.ShapeDtypeStruct((N_CHIPS,) + _travel_shape(s), BF16) for s in shapes]


def _gather_scratch(shapes):
    n = len(shapes)
    return ([pltpu.VMEM(_travel_shape(s), BF16) for s in shapes]
            + [pltpu.SemaphoreType.DMA((6 * n,)), pltpu.SemaphoreType.DMA((6 * n,)), pltpu.SemaphoreType.DMA((n,))])


def _gather_steps(shapes, ins, outs, stage, send_sems, recv_sems, local_sems):
    n = len(shapes)
    halved = [s[0] % 32 == 0 for s in shapes]

    def part(i, ref, hc):
        if not halved[i]:
            return ref
        hr = shapes[i][0] // 2
        return ref.at[pl.ds(hc * hr, hr), :]

    def to_chip(i, j, x, y, c):
        cx, cy = _other_chips(x, y)[j]
        return _remote(part(i, stage[i], c), part(i, outs[i].at[2 * x + y], c), send_sems, recv_sems, 6 * i + j, (cx, cy, c))

    def to_sibling(i, j, x, y, c):
        cx, cy = _other_chips(x, y)[j]
        got = part(i, outs[i].at[2 * cx + cy], c)
        return _remote(got, got, send_sems, recv_sems, 6 * i + 3 + j, (x, y, 1 - c))

    def local(i, x, y):
        return pltpu.make_async_copy(stage[i], outs[i].at[2 * x + y], local_sems.at[i])

    def start():
        x, y, c = _mesh_pos()
        for i in range(n):
            cols = shapes[i][1]
            if stage[i].shape[1] != cols:
                stage[i][...] = jnp.zeros_like(stage[i])
            if ins[i].shape == shapes[i]:
                stage[i][:, 0:cols] = ins[i][...].astype(BF16)
            else:
                _store_transposed(ins[i], stage[i])
            local(i, x, y).start()
            for j in range(3):
                to_chip(i, j, x, y, c).start()

    def forward():
        x, y, c = _mesh_pos()
        for i in range(n):
            for j, (cx, cy) in enumerate(_other_chips(x, y)):
                got = part(i, outs[i].at[2 * cx + cy], c)
                _remote(got, got, send_sems, recv_sems, 6 * i + j, (cx, cy, c)).wait_recv()
                if halved[i]:
                    to_sibling(i, j, x, y, c).start()

    def drain():
        x, y, c = _mesh_pos()
        for i in range(n):
            for j, (cx, cy) in enumerate(_other_chips(x, y)):
                if halved[i]:
                    got = part(i, outs[i].at[2 * cx + cy], 1 - c)
                    _remote(got, got, send_sems, recv_sems, 6 * i + 3 + j, (x, y, 1 - c)).wait_recv()
                    to_sibling(i, j, x, y, c).wait_send()
                to_chip(i, j, x, y, c).wait_send()
            local(i, x, y).wait()

    return start, forward, drain


def _swap_halves(grads, whole, name):
    n, m = len(grads), len(grads) + len(whole)

    def body(*refs):
        start, drain = _swap_steps(n, refs[:m], refs[m:2 * m], refs[2 * m], refs[2 * m + 1])
        start()
        drain()

    outs = pl.pallas_call(
        body, name=name, in_specs=[_ANY] * m, out_specs=[_ANY] * m, out_shape=_swapped_shapes(grads, whole),
        scratch_shapes=[pltpu.SemaphoreType.DMA((m,)), pltpu.SemaphoreType.DMA((m,))],
    )(*grads, *whole)
    return outs[:n], outs[n:]


def _swapped_shapes(grads, whole):
    return ([jax.ShapeDtypeStruct((g.shape[0], g.shape[1] // 2, g.shape[2]), F32) for g in grads]
            + [jax.ShapeDtypeStruct(w.shape, F32) for w in whole])


def _swap_steps(n, ins, outs, send_sems, recv_sems):
    def copies():
        x, y, c = _mesh_pos()
        cps = []
        for i, src in enumerate(ins):
            if i < n:
                hr = src.shape[1] // 2
                src = src.at[:, pl.ds((1 - c) * hr, hr), :]
            cps.append(_remote(src, outs[i], send_sems, recv_sems, i, (x, y, 1 - c)))
        return cps

    def start():
        for cp in copies():
            cp.start()

    def drain():
        for cp in copies():
            cp.wait()

    return start, drain


def _scattered_shapes(parts):
    return [jax.ShapeDtypeStruct(p.shape if p.ndim == 3 else (N_CHIPS,) + p.shape, p.dtype) for p in parts]


def _scatter_steps(ins, outs, send_sems, recv_sems, local_sems):
    n = len(ins)

    def src(i, k):
        return ins[i].at[k] if len(ins[i].shape) == 3 else ins[i]

    def sends(x, y, c):
        return [_remote(src(i, 2 * cx + cy), outs[i].at[2 * x + y], send_sems, recv_sems, 3 * i + j, (cx, cy, c))
                for i in range(n) for j, (cx, cy) in enumerate(_other_chips(x, y))]

    def local(i, x, y):
        return pltpu.make_async_copy(src(i, 2 * x + y), outs[i].at[2 * x + y], local_sems.at[i])

    def start():
        x, y, c = _mesh_pos()
        for i in range(n):
            local(i, x, y).start()
        for cp in sends(x, y, c):
            cp.start()

    def drain():
        x, y, c = _mesh_pos()
        for i in range(n):
            for j, (cx, cy) in enumerate(_other_chips(x, y)):
                got = outs[i].at[2 * cx + cy]
                _remote(got, got, send_sems, recv_sems, 3 * i + j, (cx, cy, c)).wait_recv()
        for cp in sends(x, y, c):
            cp.wait_send()
        for i in range(n):
            local(i, x, y).wait()

    return start, drain


def _add_pair(grads, from_sibling, small, small_sibling, c):
    n = len(grads)

    def body(c_ref, *refs):
        ins, outs = refs[:2 * n + 2], refs[2 * n + 2:]
        for i in range(n + 1):
            outs[i][...] = (ins[2 * i][...] + ins[2 * i + 1][...]).astype(outs[i].dtype)

    in_specs, out_specs, out_shape, args = [], [], [], []
    for g, r in zip(grads, from_sibling):
        _, hr, cols = r.shape
        in_specs += [pl.BlockSpec((1, hr, cols), lambda k, c_ref: (k, c_ref[0], 0)),
                     pl.BlockSpec((1, hr, cols), lambda k, c_ref: (k, 0, 0))]
        out_specs.append(pl.BlockSpec((1, hr, cols), lambda k, c_ref: (k, 0, 0)))
        out_shape.append(jax.ShapeDtypeStruct(r.shape, BF16))
        args += [g, r]
    whole = pl.BlockSpec(small.shape, lambda k, c_ref: (0, 0))
    in_specs += [whole, whole]
    out_specs.append(whole)
    out_shape.append(jax.ShapeDtypeStruct(small.shape, F32))
    outs = pl.pallas_call(
        body, name="add_pair", out_shape=out_shape,
        grid_spec=pltpu.PrefetchScalarGridSpec(num_scalar_prefetch=1, grid=(N_CHIPS,), in_specs=in_specs,
                                               out_specs=out_specs),
        compiler_params=_params(dimension_semantics=("arbitrary",)),
    )(c.reshape(1), *args, small, small_sibling)
    return outs[:n], outs[n]


def _scatter_w_in(dw_in_e, from_sibling):
    hr = from_sibling.shape[1]
    shard = (N_CHIPS, hr, SHARD_COLS_IN)

    def body(g_in, r_in, out, g_buf, r_buf, p_buf, load_sems, send_sems, recv_sems, local_sems):
        c = lax.axis_index("c")
        loads = (pltpu.make_async_copy(g_in.at[0, pl.ds(c * hr, hr), :], g_buf, load_sems.at[0]),
                 pltpu.make_async_copy(r_in.at[0], r_buf, load_sems.at[1]))
        for cp in loads:
            cp.start()
        for cp in loads:
            cp.wait()
        g_buf[...] += r_buf[...]
        p_buf[0, :, 0:KPE_END] = g_buf[:, 0:KPE_END].astype(BF16)
        p_buf[0, :, KPE_END:SHARD_COLS_IN] = g_buf[:, KPE_END + KPE_PAD:SHARD_COLS_IN + KPE_PAD].astype(BF16)
        for k in range(1, N_CHIPS):
            p_buf[k] = g_buf[:, SHARD_COLS_IN * k + KPE_PAD:SHARD_COLS_IN * (k + 1) + KPE_PAD].astype(BF16)
        start, drain = _scatter_steps([p_buf], [out], send_sems, recv_sems, local_sems)
        start()
        drain()

    return pl.pallas_call(
        body, name="scatter_grads", in_specs=[_ANY] * 2, out_specs=_ANY, out_shape=jax.ShapeDtypeStruct(shard, BF16),
        scratch_shapes=[pltpu.VMEM((hr, PROJ_EXT), F32)] * 2 + [pltpu.VMEM(shard, BF16)]
                       + [pltpu.SemaphoreType.DMA((2,)), pltpu.SemaphoreType.DMA((3,)), pltpu.SemaphoreType.DMA((3,)),
                          pltpu.SemaphoreType.DMA((1,))],
        compiler_params=_params(),
    )(dw_in_e, from_sibling)


def _share_halves(halves):
    n = len(halves)

    def body(*refs):
        ins, outs, send_sems, recv_sems = refs[:n], refs[n:2 * n], refs[2 * n], refs[2 * n + 1]
        x, y, c = _mesh_pos()
        cps = [_remote(ins[i], outs[i], send_sems, recv_sems, i, (x, y, 1 - c)) for i in range(n)]
        for cp in cps:
            cp.start()
        for cp in cps:
            cp.wait()

    return pl.pallas_call(
        body, name="share_halves", in_specs=[_ANY] * n, out_specs=[_ANY] * n,
        out_shape=[jax.ShapeDtypeStruct(h.shape, h.dtype) for h in halves],
        scratch_shapes=[pltpu.SemaphoreType.DMA((n,)), pltpu.SemaphoreType.DMA((n,))],
    )(*halves)


SHARD_COLS_IN = IN_TOTAL // N_CHIPS
KPE_END = Q_LORA + KV_LORA + ROPE
KPE_PAD = PROJ_EXT - IN_TOTAL


def _by_cols(a):
    return a.transpose(1, 0, 2).reshape(a.shape[1], N_CHIPS * a.shape[2])


def _assemble_early(c_in, c_uq, c_ukv, c_conv):
    return c_in, c_uq, c_ukv, _by_cols(c_conv).astype(F32)


def _assemble_late(c_o, c_pl, c_plg):
    return c_o.reshape(D_MODEL, D_MODEL), _by_cols(c_pl), c_plg.reshape(D_MODEL, D_MODEL)


def _split_late(dw_o, dw_pl, dw_plg):
    chip_major = lambda a: a.reshape(a.shape[0], N_CHIPS, a.shape[1] // N_CHIPS).transpose(1, 0, 2)
    return [dw_o.reshape(N_CHIPS, D_MODEL // N_CHIPS, D_MODEL), chip_major(dw_pl),
            dw_plg.reshape(N_CHIPS, D_MODEL // N_CHIPS, D_MODEL)]


def _local_step(x, p, pos, tgt, gains, early, late_shards, late_gathered, tm, tq):
    c_in, w_uq_e, w_ukv, conv_w = early
    g_in, g_cq, g_ckv, g_q, g_k, g_oa, g_oc, g_pl = gains
    T = x.shape[0]
    zpad = lambda a, n: jnp.concatenate([a, jnp.zeros(a.shape[:-1] + (n,), a.dtype)], axis=-1)
    gq, gk = zpad(g_q, HEAD_PAD - QK_DIM), zpad(g_k, HEAD_PAD - QK_DIM)
    inv_freq = 1.0 / (ROPE_THETA ** (jnp.arange(0, ROPE, 2, dtype=F32) / ROPE))
    invf = jnp.concatenate([inv_freq, inv_freq, jnp.zeros((64,), F32)]).reshape(1, LANES)
    sgn = jnp.concatenate([-jnp.ones((32,), F32), jnp.ones((32,), F32), jnp.zeros((64,), F32)]).reshape(1, LANES)

    (proj, q, k, v, w_in_e), gathered = _fwd_proj(x, pos, g_in, c_in, g_cq, w_uq_e, g_ckv, w_ukv, gq, gk, invf, sgn,
                                                  late_shards, min(2 * tm, T))
    w_o, w_pl, w_plg = _assemble_late(*(gathered if late_shards else late_gathered))
    o, lse = _attn_fwd(q, k, v, tq)
    (dx1, do, delta, dtail, du, dw_o, dw_pl, dw_plg, dg_oa, dg_oc, dg_pl, dconv, loss) = _tail(
        x, o, proj, p, tgt, g_oa, g_oc, g_pl, conv_w, w_o, w_pl, w_plg, tm)
    late_grads = _split_late(dw_o, dw_pl, dw_plg)
    (dq, dk, dv), late_sibling = _attn_bwd(q, k, v, do, lse, delta, tq, late_grads)
    (gx, h, dproj, dw_uq_e, dw_ukv, dg_in, dg_cq, dg_ckv, dgq, dgk) = _bwd_proj(
        x, dx1, pos, proj, dq, dk, dv, dtail, du, g_in, w_in_e, g_cq, w_uq_e, g_ckv, w_ukv, gq, gk, conv_w, invf, sgn, tm)
    wgrads = [dw_uq_e[:, :, :QK_DIM], dw_ukv, *late_grads]
    ggrads = (dg_in, dg_cq, dg_ckv, dgq, dgk, dg_oa, dg_oc, dg_pl)
    return loss, gx, (h, dproj), wgrads, late_sibling, ggrads, dconv


def kernel(x, p, positions, g_in, w_in, g_cq, w_uq, g_ckv, w_ukv, g_q, g_k, conv_w, g_oa, g_oc, w_o, w_pl, w_plg, g_pl, loss_target, m_g_in, m_w_in, m_g_cq, m_w_uq, m_g_ckv, m_w_ukv, m_g_q, m_g_k, m_conv_w, m_g_oa, m_g_oc, m_w_o, m_w_pl, m_w_plg, m_g_pl, v_g_in, v_w_in, v_g_cq, v_w_uq, v_g_ckv, v_w_ukv, v_g_q, v_g_k, v_conv_w, v_g_oa, v_g_oc, v_w_o, v_w_pl, v_w_plg, v_g_pl):
    T = x.shape[1]
    c = lax.axis_index("c")
    chip = 2 * lax.axis_index("x") + lax.axis_index("y")
    gains = [g.reshape(1, -1) for g in (g_in, g_cq, g_ckv, g_q, g_k, g_oa, g_oc, g_pl)]

    transposed = ("w_in", "w_uq")
    early = _assemble_early(*_gather_weights([w_in[0].T, w_uq[0].T, w_ukv[0], conv_w[0]], len(transposed)))

    loss, gx, (h_t, dproj), others_cm, late_sibling, ggrads, dconv = _local_step(
        x[0], p[0, 0], positions.reshape(1, T), loss_target[0], gains, early, [w_o[0], w_pl[0], w_plg[0]], None, 256, 512)

    small_parts = [a.reshape(-1, LANES) for a in (*ggrads, loss, dconv)]
    small_rows = [a.shape[0] for a in small_parts]
    tile_rows = [-(-r // 8) * 8 for r in small_rows]
    tile_rows[-1] += -sum(tile_rows) % 16
    small = jnp.concatenate([jnp.pad(a, ((0, t - r), (0, 0))) for a, r, t in zip(small_parts, small_rows, tile_rows)])
    n_early = len(others_cm) - len(late_sibling)
    early_sibling, (small_sibling,) = _swap_halves(others_cm[:n_early], [small], "pair_grads")
    chip_parts, chip_small = _add_pair(others_cm, [*early_sibling, *late_sibling], small, small_sibling, c)
    dw_in_e, w_in_sibling, exchanged = _matmul_acc(h_t, dproj, min(4096, T), 1024, [*chip_parts, chip_small])
    by_chip = [_scatter_w_in(dw_in_e[None], w_in_sibling[None]), *exchanged[:-1]]
    halves, small_total = _add_chips(by_chip, exchanged[-1])
    other_halves = _share_halves(halves)

    gg, off = [], 0
    for rows, tiled in zip(small_rows, tile_rows):
        gg.append(small_total[off:off + rows].reshape(1, -1))
        off += tiled
    loss_out = gg[8][0, 0]
    conv_total = gg[9].reshape(3, CONV_W)
    conv_g = lax.dynamic_slice(conv_total, (0, chip * (CONV_W // N_CHIPS)), (3, CONV_W // N_CHIPS))
    g_by_name = dict(g_in=gg[0], g_cq=gg[1], g_ckv=gg[2], g_q=gg[3][:, :QK_DIM], g_k=gg[4][:, :QK_DIM], conv_w=conv_g,
                     g_oa=gg[5], g_oc=gg[6], g_pl=gg[7])
    half_by_name = dict(zip(("w_in", "w_uq", "w_ukv", "w_o", "w_pl", "w_plg"), zip(halves, other_halves)))
    weights = dict(g_in=g_in, w_in=w_in, g_cq=g_cq, w_uq=w_uq, g_ckv=g_ckv, w_ukv=w_ukv, g_q=g_q, g_k=g_k,
                   conv_w=conv_w, g_oa=g_oa, g_oc=g_oc, w_o=w_o, w_pl=w_pl, w_plg=w_plg, g_pl=g_pl)
    ms = dict(g_in=m_g_in, w_in=m_w_in, g_cq=m_g_cq, w_uq=m_w_uq, g_ckv=m_g_ckv, w_ukv=m_w_ukv, g_q=m_g_q, g_k=m_g_k,
              conv_w=m_conv_w, g_oa=m_g_oa, g_oc=m_g_oc, w_o=m_w_o, w_pl=m_w_pl, w_plg=m_w_plg, g_pl=m_g_pl)
    vs = dict(g_in=v_g_in, w_in=v_w_in, g_cq=v_g_cq, w_uq=v_w_uq, g_ckv=v_g_ckv, w_ukv=v_w_ukv, g_q=v_g_q, g_k=v_g_k,
              conv_w=v_conv_w, g_oa=v_g_oa, g_oc=v_g_oc, w_o=v_w_o, w_pl=v_w_pl, w_plg=v_w_plg, g_pl=v_g_pl)
    names = list(weights)
    flat = lambda a: a.reshape(-1, a.shape[-1])
    small_names = list(g_by_name)
    one_row = lambda a: a.reshape(1, -1)
    small_out = _adamw_small([one_row(weights[n]) for n in small_names], [one_row(g_by_name[n]) for n in small_names],
                             [one_row(ms[n]) for n in small_names], [one_row(vs[n]) for n in small_names])
    results = {n: (g_by_name[n], *(out[i] for out in small_out)) for i, n in enumerate(small_names)}
    for n in half_by_name:
        shard = (lambda a: a[0].T) if n in transposed else flat
        out = _adamw_halves(shard(weights[n]), *half_by_name[n], shard(ms[n]), shard(vs[n]), c, "adamw_" + n,
                            n in transposed)
        results[n] = [a.T for a in out] if n in transposed else out
    per_kind = [[results[n][kind].reshape(weights[n].shape) for n in names] for kind in range(4)]
    return (loss_out, gx.reshape(x.shape), *per_kind[0], *per_kind[1], *per_kind[2], *per_kind[3])
```
